```python
import math
import jax, jax.numpy as jnp
from jax import lax
import numpy as np

D_MODEL = 2048
BATCH = 8
SEQ = 2048
DEPTH = 1

HGRN_WIDTH = D_MODEL // 2
HGRN_HEAD_DIM = 128
HGRN_HEADS = HGRN_WIDTH // HGRN_HEAD_DIM
CHUNK = 64
ATTN_WIDTH = D_MODEL - HGRN_WIDTH
ATTN_HEAD_DIM = 128
ATTN_HEADS = ATTN_WIDTH // ATTN_HEAD_DIM
ATTN_KV_HEADS = 2
KV_WIDTH = ATTN_KV_HEADS * ATTN_HEAD_DIM
WINDOW = 128
ATTN_BLOCK = 128
KEY_SPAN = ATTN_BLOCK + 2 * WINDOW
REL_BUCKETS = 32
REL_MAX_DIST = 128
D_FF = 5632
EPS = 1e-6
NEG_INF = -1e30
IN_SPLITS = (HGRN_WIDTH, HGRN_WIDTH, HGRN_WIDTH, HGRN_WIDTH, HGRN_WIDTH, ATTN_WIDTH, KV_WIDTH, KV_WIDTH)
IN_COLS = sum(IN_SPLITS)

kernel_name = "hymba_hgrn2_swa_macaron_sandwich"


def rms_norm(x, gain):
    xf = x.astype(jnp.float32)
    y = xf * lax.rsqrt(jnp.mean(xf * xf, axis=-1, keepdims=True) + EPS)
    return (y * gain.astype(jnp.float32)).astype(x.dtype)


def swiglu(x, w_gate_up, w_down):
    gate, up = jnp.split(x @ w_gate_up, 2, axis=-1)
    return (jax.nn.silu(gate) * up) @ w_down


def hgrn_chunk_scan(q, k, v, log_f):
    b_, h_, l_, dk = q.shape
    dv = v.shape[-1]
    n = l_ // CHUNK
    q = q.reshape(b_, h_, n, CHUNK, dk)
    k = k.reshape(b_, h_, n, CHUNK, dk)
    log_f = log_f.reshape(b_, h_, n, CHUNK, dk)
    v = v.reshape(b_, h_, n, CHUNK, dv)
    cum = jnp.cumsum(log_f, axis=3)
    last = cum[:, :, :, -1:, :]
    q_dec = q * jnp.exp(cum)
    k_dec = k * jnp.exp(-cum)
    k_tail = k * jnp.exp(last - cum)
    lower = jnp.tril(jnp.ones((CHUNK, CHUNK), dtype=bool))
    scores = jnp.einsum('bhnck,bhnsk->bhncs', q_dec, k_dec)
    scores = jnp.where(lower, scores, 0.0)
    o_intra = jnp.einsum('bhncs,bhnsv->bhncv', scores, v)
    kv_chunk = jnp.einsum('bhnsk,bhnsv->bhnkv', k_tail, v)
    chunk_decay = jnp.exp(last[:, :, :, 0, :])

    def step(state, inp):
        kv_n, dec_n = inp
        return dec_n[..., None] * state + kv_n, state

    init = jnp.zeros((b_, h_, dk, dv), kv_chunk.dtype)
    _, prev = lax.scan(step, init, (jnp.moveaxis(kv_chunk, 2, 0), jnp.moveaxis(chunk_decay, 2, 0)))
    prev = jnp.moveaxis(prev, 0, 2)
    o_inter = jnp.einsum('bhnck,bhnkv->bhncv', q_dec, prev)
    return (o_intra + o_inter).reshape(b_, h_, l_, dv)


def hgrn2_mixer(q, i, f_fwd_logit, f_bwd_logit, g, lb_fwd, lb_bwd, out_gain):
    b_, l_, _ = q.shape

    def heads(t):
        return t.astype(jnp.float32).reshape(b_, l_, HGRN_HEADS, HGRN_HEAD_DIM).transpose(0, 2, 1, 3)

    qh, vh = heads(q), heads(i)

    def direction(f_logit, lb, flip):
        lb = lb.astype(jnp.float32).reshape(HGRN_HEADS, 1, HGRN_HEAD_DIM)
        f = lb + (1.0 - lb) * jax.nn.sigmoid(heads(f_logit))
        log_f, k = jnp.log(f), 1.0 - f
        qq, vv = qh, vh
        if flip:
            qq, vv, k, log_f = (jnp.flip(t, axis=2) for t in (qq, vv, k, log_f))
        o = hgrn_chunk_scan(qq, k, vv, log_f)
        return jnp.flip(o, axis=2) if flip else o

    o = direction(f_fwd_logit, lb_fwd, False) + direction(f_bwd_logit, lb_bwd, True)
    o = o.transpose(0, 2, 1, 3)
    o = o * lax.rsqrt(jnp.mean(o * o, axis=-1, keepdims=True) + EPS)
    o = o * out_gain.astype(jnp.float32).reshape(HGRN_HEADS, HGRN_HEAD_DIM)
    o = o.reshape(b_, l_, HGRN_WIDTH) * jax.nn.silu(g.astype(jnp.float32))
    return o.astype(q.dtype)


def t5_buckets(rel):
    nb = REL_BUCKETS // 2
    max_exact = nb // 2
    bucket = (rel > 0).astype(np.int32) * nb
    n = np.abs(rel)
    large = max_exact + (np.log(np.maximum(n, 1) / max_exact) / np.log(REL_MAX_DIST / max_exact)
                         * (nb - max_exact)).astype(np.int32)
    large = np.minimum(large, nb - 1)
    return bucket + np.where(n < max_exact, n, large).astype(np.int32)


def window_attention(q, k, v, sink, rel_table):
    b_, l_, _ = q.shape
    nb = l_ // ATTN_BLOCK
    grp = ATTN_HEADS // ATTN_KV_HEADS
    qb = q.reshape(b_, nb, ATTN_BLOCK, ATTN_KV_HEADS, grp, ATTN_HEAD_DIM)

    def band(t):
        tp = jnp.pad(t, ((0, 0), (WINDOW, WINDOW), (0, 0)))
        tp = tp.reshape(b_, nb + 2, ATTN_BLOCK, ATTN_KV_HEADS, ATTN_HEAD_DIM)
        return jnp.concatenate([tp[:, :-2], tp[:, 1:-1], tp[:, 2:]], axis=2)

    kb, vb = band(k), band(v)
    scores = jnp.einsum('bncxgd,bnsxd->bxgncs', qb, kb).astype(jnp.float32) / math.sqrt(ATTN_HEAD_DIM)
    c = np.arange(ATTN_BLOCK)[:, None]
    s = np.arange(KEY_SPAN)[None, :]
    rel = s - WINDOW - c
    bias = rel_table.astype(jnp.float32)[t5_buckets(rel)]
    bias = jnp.transpose(bias, (2, 0, 1)).reshape(ATTN_KV_HEADS, grp, 1, ATTN_BLOCK, KEY_SPAN)
    key_pos = np.arange(nb)[:, None, None] * ATTN_BLOCK - WINDOW + s[None]
    valid = (np.abs(rel)[None] <= WINDOW) & (key_pos >= 0) & (key_pos < l_)
    scores = jnp.where(valid, scores + bias, NEG_INF)
    sink_col = jnp.broadcast_to(sink.astype(jnp.float32).reshape(1, ATTN_KV_HEADS, grp, 1, 1, 1),
                                scores.shape[:-1] + (1,))
    probs = jax.nn.softmax(jnp.concatenate([scores, sink_col], axis=-1), axis=-1)[..., :KEY_SPAN]
    out = jnp.einsum('bxgncs,bnsxd->bncxgd', probs.astype(v.dtype), vb)
    return out.reshape(b_, l_, ATTN_WIDTH)


def _fwd_setup_inputs(seed: int = 0) -> dict:
    key = jax.random.key(seed)
    ks = jax.random.split(key, 20)
    f32 = jnp.float32

    def w(k, shape, fan_in):
        return jax.random.normal(k, shape, f32) * fan_in ** -0.5

    def gain(k, shape):
        return 1.0 + 0.02 * jax.random.normal(k, shape, f32)

    return {
        "x": jax.random.normal(ks[0], (BATCH, SEQ, D_MODEL), f32),
        "pre_norm_ffn1": gain(ks[1], (DEPTH, D_MODEL)),
        "post_norm_ffn1": gain(ks[2], (DEPTH, D_MODEL)),
        "w_ffn1_gate_up": w(ks[3], (DEPTH, D_MODEL, 2 * D_FF), D_MODEL),
        "w_ffn1_down": w(ks[4], (DEPTH, D_FF, D_MODEL), D_FF),
        "pre_norm_mix": gain(ks[5], (DEPTH, D_MODEL)),
        "post_norm_mix": gain(ks[6], (DEPTH, D_MODEL)),
        "w_mix_in": w(ks[7], (DEPTH, D_MODEL, IN_COLS), D_MODEL),
        "hgrn_lower_bounds_fwd": 0.1 * jax.random.normal(ks[8], (DEPTH + 1, HGRN_WIDTH), f32),
        "hgrn_lower_bounds_bwd": 0.1 * jax.random.normal(ks[9], (DEPTH + 1, HGRN_WIDTH), f32),
        "hgrn_out_norm": gain(ks[10], (DEPTH, HGRN_WIDTH)),
        "attn_sink": 0.5 * jax.random.normal(ks[11], (DEPTH, ATTN_HEADS), f32),
        "w_mix_out": w(ks[12], (DEPTH, HGRN_WIDTH + ATTN_WIDTH, D_MODEL), HGRN_WIDTH + ATTN_WIDTH),
        "pre_norm_ffn2": gain(ks[13], (DEPTH, D_MODEL)),
        "post_norm_ffn2": gain(ks[14], (DEPTH, D_MODEL)),
        "w_ffn2_gate_up": w(ks[15], (DEPTH, D_MODEL, 2 * D_FF), D_MODEL),
        "w_ffn2_down": w(ks[16], (DEPTH, D_FF, D_MODEL), D_FF),
        "rel_bias_table": 0.5 * jax.random.normal(ks[17], (REL_BUCKETS, ATTN_HEADS), f32),
    }


def _fwd_reference(x, pre_norm_ffn1, post_norm_ffn1, w_ffn1_gate_up, w_ffn1_down, pre_norm_mix,
              post_norm_mix, w_mix_in, hgrn_lower_bounds_fwd, hgrn_lower_bounds_bwd, hgrn_out_norm,
              attn_sink, w_mix_out, pre_norm_ffn2, post_norm_ffn2, w_ffn2_gate_up, w_ffn2_down,
              rel_bias_table):
    lb_fwd_all = jnp.cumsum(jax.nn.softmax(hgrn_lower_bounds_fwd.astype(jnp.float32), axis=0), axis=0)
    lb_bwd_all = jnp.cumsum(jax.nn.softmax(hgrn_lower_bounds_bwd.astype(jnp.float32), axis=0), axis=0)
    split_at = np.cumsum(IN_SPLITS)[:-1].tolist()
    for layer in range(DEPTH):
        ff = swiglu(rms_norm(x, pre_norm_ffn1[layer]), w_ffn1_gate_up[layer], w_ffn1_down[layer])
        x = x + 0.5 * rms_norm(ff, post_norm_ffn1[layer])
        h = rms_norm(x, pre_norm_mix[layer])
        q_h, i_h, f_fwd, f_bwd, g_h, q_a, k_a, v_a = jnp.split(h @ w_mix_in[layer], split_at, axis=-1)
        y_h = hgrn2_mixer(q_h, i_h, f_fwd, f_bwd, g_h, lb_fwd_all[layer], lb_bwd_all[layer],
                          hgrn_out_norm[layer])
        y_a = window_attention(q_a, k_a, v_a, attn_sink[layer], rel_bias_table)
        mixed = jnp.concatenate([y_h.astype(x.dtype), y_a.astype(x.dtype)], axis=-1) @ w_mix_out[layer]
        x = x + rms_norm(mixed, post_norm_mix[layer])
        ff = swiglu(rms_norm(x, pre_norm_ffn2[layer]), w_ffn2_gate_up[layer], w_ffn2_down[layer])
        x = x + 0.5 * rms_norm(ff, post_norm_ffn2[layer])
    return x


import jax as _jax
import jax.numpy as _jnp

TWIN_FORMAT = 'train_step'
FWD_PARAMS = ['x', 'pre_norm_ffn1', 'post_norm_ffn1', 'w_ffn1_gate_up', 'w_ffn1_down', 'pre_norm_mix', 'post_norm_mix', 'w_mix_in', 'hgrn_lower_bounds_fwd', 'hgrn_lower_bounds_bwd', 'hgrn_out_norm', 'attn_sink', 'w_mix_out', 'pre_norm_ffn2', 'post_norm_ffn2', 'w_ffn2_gate_up', 'w_ffn2_down', 'rel_bias_table']
TWIN_WEIGHTS = ['pre_norm_ffn1', 'post_norm_ffn1', 'w_ffn1_gate_up', 'w_ffn1_down', 'pre_norm_mix', 'post_norm_mix', 'w_mix_in', 'hgrn_lower_bounds_fwd', 'hgrn_lower_bounds_bwd', 'hgrn_out_norm', 'attn_sink', 'w_mix_out', 'pre_norm_ffn2', 'post_norm_ffn2', 'w_ffn2_gate_up', 'w_ffn2_down', 'rel_bias_table']
TWIN_DIFF_INPUT = 'x'
TWIN_INPUTS = ['x', 'pre_norm_ffn1', 'post_norm_ffn1', 'w_ffn1_gate_up', 'w_ffn1_down', 'pre_norm_mix', 'post_norm_mix', 'w_mix_in', 'hgrn_lower_bounds_fwd', 'hgrn_lower_bounds_bwd', 'hgrn_out_norm', 'attn_sink', 'w_mix_out', 'pre_norm_ffn2', 'post_norm_ffn2', 'w_ffn2_gate_up', 'w_ffn2_down', 'rel_bias_table', 'loss_target', 'm_pre_norm_ffn1', 'm_post_norm_ffn1', 'm_w_ffn1_gate_up', 'm_w_ffn1_down', 'm_pre_norm_mix', 'm_post_norm_mix', 'm_w_mix_in', 'm_hgrn_lower_bounds_fwd', 'm_hgrn_lower_bounds_bwd', 'm_hgrn_out_norm', 'm_attn_sink', 'm_w_mix_out', 'm_pre_norm_ffn2', 'm_post_norm_ffn2', 'm_w_ffn2_gate_up', 'm_w_ffn2_down', 'm_rel_bias_table', 'v_pre_norm_ffn1', 'v_post_norm_ffn1', 'v_w_ffn1_gate_up', 'v_w_ffn1_down', 'v_pre_norm_mix', 'v_post_norm_mix', 'v_w_mix_in', 'v_hgrn_lower_bounds_fwd', 'v_hgrn_lower_bounds_bwd', 'v_hgrn_out_norm', 'v_attn_sink', 'v_w_mix_out', 'v_pre_norm_ffn2', 'v_post_norm_ffn2', 'v_w_ffn2_gate_up', 'v_w_ffn2_down', 'v_rel_bias_table']
TWIN_OUTPUTS = ['loss', 'grad_x', 'grad_pre_norm_ffn1', 'grad_post_norm_ffn1', 'grad_w_ffn1_gate_up', 'grad_w_ffn1_down', 'grad_pre_norm_mix', 'grad_post_norm_mix', 'grad_w_mix_in', 'grad_hgrn_lower_bounds_fwd', 'grad_hgrn_lower_bounds_bwd', 'grad_hgrn_out_norm', 'grad_attn_sink', 'grad_w_mix_out', 'grad_pre_norm_ffn2', 'grad_post_norm_ffn2', 'grad_w_ffn2_gate_up', 'grad_w_ffn2_down', 'grad_rel_bias_table', 'delta_pre_norm_ffn1', 'delta_post_norm_ffn1', 'delta_w_ffn1_gate_up', 'delta_w_ffn1_down', 'delta_pre_norm_mix', 'delta_post_norm_mix', 'delta_w_mix_in', 'delta_hgrn_lower_bounds_fwd', 'delta_hgrn_lower_bounds_bwd', 'delta_hgrn_out_norm', 'delta_attn_sink', 'delta_w_mix_out', 'delta_pre_norm_ffn2', 'delta_post_norm_ffn2', 'delta_w_ffn2_gate_up', 'delta_w_ffn2_down', 'delta_rel_bias_table', 'new_m_pre_norm_ffn1', 'new_m_post_norm_ffn1', 'new_m_w_ffn1_gate_up', 'new_m_w_ffn1_down', 'new_m_pre_norm_mix', 'new_m_post_norm_mix', 'new_m_w_mix_in', 'new_m_hgrn_lower_bounds_fwd', 'new_m_hgrn_lower_bounds_bwd', 'new_m_hgrn_out_norm', 'new_m_attn_sink', 'new_m_w_mix_out', 'new_m_pre_norm_ffn2', 'new_m_post_norm_ffn2', 'new_m_w_ffn2_gate_up', 'new_m_w_ffn2_down', 'new_m_rel_bias_table', 'new_v_pre_norm_ffn1', 'new_v_post_norm_ffn1', 'new_v_w_ffn1_gate_up', 'new_v_w_ffn1_down', 'new_v_pre_norm_mix', 'new_v_post_norm_mix', 'new_v_w_mix_in', 'new_v_hgrn_lower_bounds_fwd', 'new_v_hgrn_lower_bounds_bwd', 'new_v_hgrn_out_norm', 'new_v_attn_sink', 'new_v_w_mix_out', 'new_v_pre_norm_ffn2', 'new_v_post_norm_ffn2', 'new_v_w_ffn2_gate_up', 'new_v_w_ffn2_down', 'new_v_rel_bias_table']
TWIN_LEAF_KINDS = {'loss': 'loss', 'grad_x': 'grad_x', 'grad_pre_norm_ffn1': 'grad_w', 'grad_post_norm_ffn1': 'grad_w', 'grad_w_ffn1_gate_up': 'grad_w', 'grad_w_ffn1_down': 'grad_w', 'grad_pre_norm_mix': 'grad_w', 'grad_post_norm_mix': 'grad_w', 'grad_w_mix_in': 'grad_w', 'grad_hgrn_lower_bounds_fwd': 'grad_w', 'grad_hgrn_lower_bounds_bwd': 'grad_w', 'grad_hgrn_out_norm': 'grad_w', 'grad_attn_sink': 'grad_w', 'grad_w_mix_out': 'grad_w', 'grad_pre_norm_ffn2': 'grad_w', 'grad_post_norm_ffn2': 'grad_w', 'grad_w_ffn2_gate_up': 'grad_w', 'grad_w_ffn2_down': 'grad_w', 'grad_rel_bias_table': 'grad_w', 'delta_pre_norm_ffn1': 'delta_w', 'delta_post_norm_ffn1': 'delta_w', 'delta_w_ffn1_gate_up': 'delta_w', 'delta_w_ffn1_down': 'delta_w', 'delta_pre_norm_mix': 'delta_w', 'delta_post_norm_mix': 'delta_w', 'delta_w_mix_in': 'delta_w', 'delta_hgrn_lower_bounds_fwd': 'delta_w', 'delta_hgrn_lower_bounds_bwd': 'delta_w', 'delta_hgrn_out_norm': 'delta_w', 'delta_attn_sink': 'delta_w', 'delta_w_mix_out': 'delta_w', 'delta_pre_norm_ffn2': 'delta_w', 'delta_post_norm_ffn2': 'delta_w', 'delta_w_ffn2_gate_up': 'delta_w', 'delta_w_ffn2_down': 'delta_w', 'delta_rel_bias_table': 'delta_w', 'new_m_pre_norm_ffn1': 'new_m', 'new_m_post_norm_ffn1': 'new_m', 'new_m_w_ffn1_gate_up': 'new_m', 'new_m_w_ffn1_down': 'new_m', 'new_m_pre_norm_mix': 'new_m', 'new_m_post_norm_mix': 'new_m', 'new_m_w_mix_in': 'new_m', 'new_m_hgrn_lower_bounds_fwd': 'new_m', 'new_m_hgrn_lower_bounds_bwd': 'new_m', 'new_m_hgrn_out_norm': 'new_m', 'new_m_attn_sink': 'new_m', 'new_m_w_mix_out': 'new_m', 'new_m_pre_norm_ffn2': 'new_m', 'new_m_post_norm_ffn2': 'new_m', 'new_m_w_ffn2_gate_up': 'new_m', 'new_m_w_ffn2_down': 'new_m', 'new_m_rel_bias_table': 'new_m', 'new_v_pre_norm_ffn1': 'new_v', 'new_v_post_norm_ffn1': 'new_v', 'new_v_w_ffn1_gate_up': 'new_v', 'new_v_w_ffn1_down': 'new_v', 'new_v_pre_norm_mix': 'new_v', 'new_v_post_norm_mix': 'new_v', 'new_v_w_mix_in': 'new_v', 'new_v_hgrn_lower_bounds_fwd': 'new_v', 'new_v_hgrn_lower_bounds_bwd': 'new_v', 'new_v_hgrn_out_norm': 'new_v', 'new_v_attn_sink': 'new_v', 'new_v_w_mix_out': 'new_v', 'new_v_pre_norm_ffn2': 'new_v', 'new_v_post_norm_ffn2': 'new_v', 'new_v_w_ffn2_gate_up': 'new_v', 'new_v_w_ffn2_down': 'new_v', 'new_v_rel_bias_table': 'new_v'}


def _forward(args):
    return _fwd_reference(*[args[k] for k in FWD_PARAMS])


def _output_shape():
    out = _jax.eval_shape(lambda: _forward(_fwd_setup_inputs(0)))
    return out.shape, out.dtype

N_MICROBATCH = 1
ADAM_LR = 0.001
ADAM_B1 = 0.9
ADAM_B2 = 0.999
ADAM_EPS = 1e-08
ADAM_WD = 0.01
ADAM_STEP = 10
PER_EXAMPLE_BATCH_AXIS = {'x': 0, 'loss_target': 0}
SHARED_INPUTS = []
_WEIGHT_DTYPES = {'pre_norm_ffn1': _jnp.float32, 'post_norm_ffn1': _jnp.float32, 'w_ffn1_gate_up': _jnp.float32, 'w_ffn1_down': _jnp.float32, 'pre_norm_mix': _jnp.float32, 'post_norm_mix': _jnp.float32, 'w_mix_in': _jnp.float32, 'hgrn_lower_bounds_fwd': _jnp.float32, 'hgrn_lower_bounds_bwd': _jnp.float32, 'hgrn_out_norm': _jnp.float32, 'attn_sink': _jnp.float32, 'w_mix_out': _jnp.float32, 'pre_norm_ffn2': _jnp.float32, 'post_norm_ffn2': _jnp.float32, 'w_ffn2_gate_up': _jnp.float32, 'w_ffn2_down': _jnp.float32, 'rel_bias_table': _jnp.float32}
MOMENT_SCALE = {'pre_norm_ffn1': 1.929873e-01, 'post_norm_ffn1': 1.966411e+00, 'w_ffn1_gate_up': 7.904995e-02, 'w_ffn1_down': 1.294537e-01, 'pre_norm_mix': 2.539523e-01, 'post_norm_mix': 7.993613e+00, 'w_mix_in': 1.374948e-01, 'hgrn_lower_bounds_fwd': 7.308429e-02, 'hgrn_lower_bounds_bwd': 6.977601e-02, 'hgrn_out_norm': 1.661645e-01, 'attn_sink': 2.226677e-03, 'w_mix_out': 1.162372e-01, 'pre_norm_ffn2': 8.833269e-02, 'post_norm_ffn2': 1.993508e+00, 'w_ffn2_gate_up': 3.793726e-02, 'w_ffn2_down': 6.744554e-02, 'rel_bias_table': 7.119197e-02}


def _to_microbatches(a, axis):
    t = _jnp.moveaxis(a, axis, 0)
    t = t.reshape((N_MICROBATCH, t.shape[0] // N_MICROBATCH) + t.shape[1:])
    return _jnp.moveaxis(t, 1, axis + 1)


def setup_inputs(seed: int = 0) -> dict:
    inp = _fwd_setup_inputs(seed)
    key = _jax.random.fold_in(_jax.random.key(seed), 7919)
    shape, _ = _output_shape()
    out = dict(inp)
    out["loss_target"] = _jax.random.normal(_jax.random.fold_in(key, 0), shape, _jnp.float32)
    for i, name in enumerate(TWIN_WEIGHTS):
        w = inp[name].astype(_jnp.float32)
        if MOMENT_SCALE is None:
            s = _jnp.sqrt(_jnp.mean(_jnp.square(w)) + 1e-30)
        else:
            s = MOMENT_SCALE[name]
        km, kv = _jax.random.split(_jax.random.fold_in(key, i + 1))
        out[name] = w
        out["m_" + name] = s * _jax.random.normal(km, w.shape, _jnp.float32)
        out["v_" + name] = (s * s) * _jax.random.uniform(kv, w.shape, _jnp.float32, 0.5, 1.5)
    if N_MICROBATCH > 1:
        for name, axis in PER_EXAMPLE_BATCH_AXIS.items():
            out[name] = _to_microbatches(out[name], axis)
    return {'x': out['x'], 'pre_norm_ffn1': out['pre_norm_ffn1'], 'post_norm_ffn1': out['post_norm_ffn1'], 'w_ffn1_gate_up': out['w_ffn1_gate_up'], 'w_ffn1_down': out['w_ffn1_down'], 'pre_norm_mix': out['pre_norm_mix'], 'post_norm_mix': out['post_norm_mix'], 'w_mix_in': out['w_mix_in'], 'hgrn_lower_bounds_fwd': out['hgrn_lower_bounds_fwd'], 'hgrn_lower_bounds_bwd': out['hgrn_lower_bounds_bwd'], 'hgrn_out_norm': out['hgrn_out_norm'], 'attn_sink': out['attn_sink'], 'w_mix_out': out['w_mix_out'], 'pre_norm_ffn2': out['pre_norm_ffn2'], 'post_norm_ffn2': out['post_norm_ffn2'], 'w_ffn2_gate_up': out['w_ffn2_gate_up'], 'w_ffn2_down': out['w_ffn2_down'], 'rel_bias_table': out['rel_bias_table'], 'loss_target': out['loss_target'], 'm_pre_norm_ffn1': out['m_pre_norm_ffn1'], 'm_post_norm_ffn1': out['m_post_norm_ffn1'], 'm_w_ffn1_gate_up': out['m_w_ffn1_gate_up'], 'm_w_ffn1_down': out['m_w_ffn1_down'], 'm_pre_norm_mix': out['m_pre_norm_mix'], 'm_post_norm_mix': out['m_post_norm_mix'], 'm_w_mix_in': out['m_w_mix_in'], 'm_hgrn_lower_bounds_fwd': out['m_hgrn_lower_bounds_fwd'], 'm_hgrn_lower_bounds_bwd': out['m_hgrn_lower_bounds_bwd'], 'm_hgrn_out_norm': out['m_hgrn_out_norm'], 'm_attn_sink': out['m_attn_sink'], 'm_w_mix_out': out['m_w_mix_out'], 'm_pre_norm_ffn2': out['m_pre_norm_ffn2'], 'm_post_norm_ffn2': out['m_post_norm_ffn2'], 'm_w_ffn2_gate_up': out['m_w_ffn2_gate_up'], 'm_w_ffn2_down': out['m_w_ffn2_down'], 'm_rel_bias_table': out['m_rel_bias_table'], 'v_pre_norm_ffn1': out['v_pre_norm_ffn1'], 'v_post_norm_ffn1': out['v_post_norm_ffn1'], 'v_w_ffn1_gate_up': out['v_w_ffn1_gate_up'], 'v_w_ffn1_down': out['v_w_ffn1_down'], 'v_pre_norm_mix': out['v_pre_norm_mix'], 'v_post_norm_mix': out['v_post_norm_mix'], 'v_w_mix_in': out['v_w_mix_in'], 'v_hgrn_lower_bounds_fwd': out['v_hgrn_lower_bounds_fwd'], 'v_hgrn_lower_bounds_bwd': out['v_hgrn_lower_bounds_bwd'], 'v_hgrn_out_norm': out['v_hgrn_out_norm'], 'v_attn_sink': out['v_attn_sink'], 'v_w_mix_out': out['v_w_mix_out'], 'v_pre_norm_ffn2': out['v_pre_norm_ffn2'], 'v_post_norm_ffn2': out['v_post_norm_ffn2'], 'v_w_ffn2_gate_up': out['v_w_ffn2_gate_up'], 'v_w_ffn2_down': out['v_w_ffn2_down'], 'v_rel_bias_table': out['v_rel_bias_table']}


def _loss(weights, diff, rest, loss_target):
    with _jax.named_scope("forward"):
        args = {**rest, TWIN_DIFF_INPUT: diff, **{k: w.astype(_WEIGHT_DTYPES[k]) for k, w in weights.items()}}
        y = _forward(args)
    with _jax.named_scope("loss_head"):
        err = _jnp.square(y.astype(_jnp.float32) - loss_target)
        return 0.5 * _jnp.sum(_jnp.mean(err, axis=-1)) if err.ndim else 0.5 * err


def _adamw(w, g, m, v):
    m = ADAM_B1 * m + (1.0 - ADAM_B1) * g
    v = ADAM_B2 * v + (1.0 - ADAM_B2) * _jnp.square(g)
    m_hat = m / (1.0 - ADAM_B1 ** ADAM_STEP)
    v_hat = v / (1.0 - ADAM_B2 ** ADAM_STEP)
    delta = -ADAM_LR * (m_hat / (_jnp.sqrt(v_hat) + ADAM_EPS) + ADAM_WD * w)
    return delta, m, v


def reference(x, pre_norm_ffn1, post_norm_ffn1, w_ffn1_gate_up, w_ffn1_down, pre_norm_mix, post_norm_mix, w_mix_in, hgrn_lower_bounds_fwd, hgrn_lower_bounds_bwd, hgrn_out_norm, attn_sink, w_mix_out, pre_norm_ffn2, post_norm_ffn2, w_ffn2_gate_up, w_ffn2_down, rel_bias_table, loss_target, m_pre_norm_ffn1, m_post_norm_ffn1, m_w_ffn1_gate_up, m_w_ffn1_down, m_pre_norm_mix, m_post_norm_mix, m_w_mix_in, m_hgrn_lower_bounds_fwd, m_hgrn_lower_bounds_bwd, m_hgrn_out_norm, m_attn_sink, m_w_mix_out, m_pre_norm_ffn2, m_post_norm_ffn2, m_w_ffn2_gate_up, m_w_ffn2_down, m_rel_bias_table, v_pre_norm_ffn1, v_post_norm_ffn1, v_w_ffn1_gate_up, v_w_ffn1_down, v_pre_norm_mix, v_post_norm_mix, v_w_mix_in, v_hgrn_lower_bounds_fwd, v_hgrn_lower_bounds_bwd, v_hgrn_out_norm, v_attn_sink, v_w_mix_out, v_pre_norm_ffn2, v_post_norm_ffn2, v_w_ffn2_gate_up, v_w_ffn2_down, v_rel_bias_table):
    given = dict(x=x, pre_norm_ffn1=pre_norm_ffn1, post_norm_ffn1=post_norm_ffn1, w_ffn1_gate_up=w_ffn1_gate_up, w_ffn1_down=w_ffn1_down, pre_norm_mix=pre_norm_mix, post_norm_mix=post_norm_mix, w_mix_in=w_mix_in, hgrn_lower_bounds_fwd=hgrn_lower_bounds_fwd, hgrn_lower_bounds_bwd=hgrn_lower_bounds_bwd, hgrn_out_norm=hgrn_out_norm, attn_sink=attn_sink, w_mix_out=w_mix_out, pre_norm_ffn2=pre_norm_ffn2, post_norm_ffn2=post_norm_ffn2, w_ffn2_gate_up=w_ffn2_gate_up, w_ffn2_down=w_ffn2_down, rel_bias_table=rel_bias_table, loss_target=loss_target, m_pre_norm_ffn1=m_pre_norm_ffn1, m_post_norm_ffn1=m_post_norm_ffn1, m_w_ffn1_gate_up=m_w_ffn1_gate_up, m_w_ffn1_down=m_w_ffn1_down, m_pre_norm_mix=m_pre_norm_mix, m_post_norm_mix=m_post_norm_mix, m_w_mix_in=m_w_mix_in, m_hgrn_lower_bounds_fwd=m_hgrn_lower_bounds_fwd, m_hgrn_lower_bounds_bwd=m_hgrn_lower_bounds_bwd, m_hgrn_out_norm=m_hgrn_out_norm, m_attn_sink=m_attn_sink, m_w_mix_out=m_w_mix_out, m_pre_norm_ffn2=m_pre_norm_ffn2, m_post_norm_ffn2=m_post_norm_ffn2, m_w_ffn2_gate_up=m_w_ffn2_gate_up, m_w_ffn2_down=m_w_ffn2_down, m_rel_bias_table=m_rel_bias_table, v_pre_norm_ffn1=v_pre_norm_ffn1, v_post_norm_ffn1=v_post_norm_ffn1, v_w_ffn1_gate_up=v_w_ffn1_gate_up, v_w_ffn1_down=v_w_ffn1_down, v_pre_norm_mix=v_pre_norm_mix, v_post_norm_mix=v_post_norm_mix, v_w_mix_in=v_w_mix_in, v_hgrn_lower_bounds_fwd=v_hgrn_lower_bounds_fwd, v_hgrn_lower_bounds_bwd=v_hgrn_lower_bounds_bwd, v_hgrn_out_norm=v_hgrn_out_norm, v_attn_sink=v_attn_sink, v_w_mix_out=v_w_mix_out, v_pre_norm_ffn2=v_pre_norm_ffn2, v_post_norm_ffn2=v_post_norm_ffn2, v_w_ffn2_gate_up=v_w_ffn2_gate_up, v_w_ffn2_down=v_w_ffn2_down, v_rel_bias_table=v_rel_bias_table)
    weights = {n: given[n] for n in TWIN_WEIGHTS}
    shared = {n: given[n] for n in SHARED_INPUTS}
    per_example = {n: given[n] for n in ['x']}
    grad_fn = _jax.value_and_grad(_loss, argnums=(0, 1))

    def one_microbatch(ex, loss_target):
        ex = dict(ex)
        diff = ex.pop(TWIN_DIFF_INPUT)
        return grad_fn(weights, diff, {**shared, **ex}, loss_target)

    if N_MICROBATCH == 1:
        loss, (grad_w, grad_x) = one_microbatch(per_example, given["loss_target"])
    else:
        def body(carry, xs):
            loss_sum, grad_sum = carry
            l_k, (gw_k, gx_k) = one_microbatch(xs[0], xs[1])
            with _jax.named_scope("update"):
                return (loss_sum + l_k, _jax.tree.map(_jnp.add, grad_sum, gw_k)), gx_k

        init = (_jnp.zeros((), _jnp.float32), _jax.tree.map(_jnp.zeros_like, weights))
        (loss, grad_w), grad_x = _jax.lax.scan(body, init, (per_example, given["loss_target"]))
    with _jax.named_scope("update"):
        delta_w, new_m, new_v = {}, {}, {}
        for n in TWIN_WEIGHTS:
            delta_w[n], new_m[n], new_v[n] = _adamw(weights[n], grad_w[n], given["m_" + n], given["v_" + n])
    return (loss, grad_x, *[grad_w[n] for n in TWIN_WEIGHTS], *[delta_w[n] for n in TWIN_WEIGHTS],
            *[new_m[n] for n in TWIN_WEIGHTS], *[new_v[n] for n in TWIN_WEIGHTS])
```

```python
import functools
import math

import jax
import jax.numpy as jnp
import numpy as np
from jax import lax
from jax.experimental import pallas as pl
from jax.experimental.pallas import tpu as pltpu

F32 = jnp.float32
BF16 = jnp.bfloat16

HEAD = 128
CHUNK = 64
WINDOW = 128
SPAN = 3 * WINDOW
KV_HEADS = 2
REL_BUCKETS = 32
REL_MAX_DIST = 128
EPS = 1e-6
NEG_INF = -1e30

ADAM_LR = 0.001
ADAM_B1 = 0.9
ADAM_B2 = 0.999
ADAM_EPS = 1e-08
ADAM_WD = 0.01
ADAM_STEP = 10

N_CHIPS = 4
N_DEV = 8
V7X_VMEM_BYTES = 64 * 1024 * 1024
MESH = pl.DeviceIdType.MESH
ANY = pl.BlockSpec(memory_space=pl.ANY)


def _tile(n, pref, mult):
    t = (min(pref, n) // mult) * mult
    while t >= mult:
        if n % t == 0:
            return t
        t -= mult
    return n


def _params(semantics, block_bytes):
    limit = min(V7X_VMEM_BYTES - (4 << 20), 2 * int(block_bytes) + (8 << 20))
    return pltpu.CompilerParams(dimension_semantics=semantics, vmem_limit_bytes=limit)


def _nbytes(shape, dtype):
    return int(np.prod(shape)) * jnp.dtype(dtype).itemsize


def _dot(a, b, ca=1, cb=0):
    return lax.dot_general(a, b, (((ca,), (cb,)), ((), ())), preferred_element_type=F32)


def _split3(x):
    hi = x.astype(BF16)
    r1 = x - hi.astype(F32)
    mid = r1.astype(BF16)
    lo = (r1 - mid.astype(F32)).astype(BF16)
    return hi, mid, lo


def _dot_exact(a, b, ca=1, cb=0, split="b"):
    if split == "b":
        return sum(_dot(a, p, ca, cb) for p in _split3(b))
    return sum(_dot(p, b, ca, cb) for p in _split3(a))


def _rms(x):
    return lax.rsqrt(jnp.mean(x * x, axis=-1, keepdims=True) + EPS)


def _norm_bwd(u, x, gain):
    r = _rms(x)
    xhat = x * r
    dgain = jnp.sum(u * xhat, axis=0, keepdims=True)
    v = u * gain
    dx = r * (v - xhat * jnp.mean(v * xhat, axis=-1, keepdims=True))
    return dx, dgain


def _sigmoid(x):
    return 1.0 / (1.0 + jnp.exp(-x))


def _accumulate(ref, val, first):
    @pl.when(first)
    def _():
        ref[...] = val

    @pl.when(jnp.logical_not(first))
    def _():
        ref[...] += val


def _matmul(name, a, b, *, form, out_dtype, tm, tn, tk, a_map=None, b_map=None,
            out_shape=None, out_block=None, out_map=None, sizes=None):
    if sizes is None:
        if form == "nn":
            (m, k), n = a.shape, b.shape[1]
        elif form == "nt":
            (m, k), n = a.shape, b.shape[0]
        else:
            (k, m), n = a.shape, b.shape[1]
    else:
        m, n, k = sizes
    gi, gj, gk = m // tm, n // tn, k // tk
    a_blk = (tm, tk) if form != "tn" else (tk, tm)
    b_blk = (tk, tn) if form != "nt" else (tn, tk)
    if a_map is None:
        a_map = (lambda i, j, kk: (i, kk)) if form != "tn" else (lambda i, j, kk: (kk, i))
    else:
        a_blk = (None,) + a_blk
    if b_map is None:
        b_map = (lambda i, j, kk: (kk, j)) if form != "nt" else (lambda i, j, kk: (j, kk))
    else:
        b_blk = (None,) + b_blk
    if out_shape is None:
        out_shape, out_block, out_map = (m, n), (tm, tn), (lambda i, j, kk: (i, j))
    ca, cb = {"nn": (1, 0), "nt": (1, 1), "tn": (0, 0)}[form]

    def body(a_ref, b_ref, o_ref, *acc):
        part = _dot(a_ref[...], b_ref[...], ca, cb)
        if gk == 1:
            o_ref[...] = part.astype(o_ref.dtype)
        else:
            kk = pl.program_id(2)
            _accumulate(acc[0], part, kk == 0)

            @pl.when(kk == gk - 1)
            def _():
                o_ref[...] = acc[0][...].astype(o_ref.dtype)

    scratch = [] if gk == 1 else [pltpu.VMEM((tm, tn), F32)]
    vmem = (_nbytes((tm, tk), a.dtype) + _nbytes((tk, tn), b.dtype) + _nbytes((tm, tn), out_dtype)
            + 2 * _nbytes((tm, tn), F32))
    return pl.pallas_call(
        body, name=name, grid=(gi, gj, gk),
        in_specs=[pl.BlockSpec(a_blk, a_map), pl.BlockSpec(b_blk, b_map)],
        out_specs=pl.BlockSpec(out_block, out_map),
        out_shape=jax.ShapeDtypeStruct(out_shape, out_dtype),
        scratch_shapes=scratch,
        compiler_params=_params(("parallel", "parallel", "arbitrary"), vmem),
    )(a, b)


def _mm_tiles(m, n, k):
    return _tile(m, 1024, 128), _tile(n, 512, 128), _tile(k, 2816, 128)


def _mm(name, a, b, form, out_dtype):
    if form == "nn":
        m, k, n = a.shape[0], a.shape[1], b.shape[1]
    elif form == "nt":
        m, k, n = a.shape[0], a.shape[1], b.shape[0]
    else:
        m, k, n = a.shape[1], a.shape[0], b.shape[1]
    tm, tn, tk = _mm_tiles(m, n, k)
    return _matmul(name, a, b, form=form, out_dtype=out_dtype, tm=tm, tn=tn, tk=tk)


def _row_tile(t):
    return _tile(t, 256, 8)


def _norm_fwd(name, x, gain):
    t, d = x.shape
    tm = _row_tile(t)

    def body(x_ref, g_ref, h_ref):
        xv = x_ref[...]
        h_ref[...] = (xv * _rms(xv) * g_ref[...]).astype(BF16)

    row = pl.BlockSpec((tm, d), lambda i: (i, 0))
    vec = pl.BlockSpec((1, d), lambda i: (0, 0))
    return pl.pallas_call(
        body, name=name, grid=(t // tm,), in_specs=[row, vec], out_specs=row,
        out_shape=jax.ShapeDtypeStruct((t, d), BF16),
        compiler_params=_params(("parallel",), 2 * _nbytes((tm, d), F32)),
    )(x, gain)


def _resid_norm_fwd(name, xres, ff, gpost, gpre, scale):
    t, d = xres.shape
    tm = _row_tile(t)

    def body(x_ref, f_ref, gp_ref, gn_ref, xn_ref, h_ref):
        f = f_ref[...]
        xn = x_ref[...] + scale * (f * _rms(f) * gp_ref[...])
        xn_ref[...] = xn
        h_ref[...] = (xn * _rms(xn) * gn_ref[...]).astype(BF16)

    row = pl.BlockSpec((tm, d), lambda i: (i, 0))
    vec = pl.BlockSpec((1, d), lambda i: (0, 0))
    return pl.pallas_call(
        body, name=name, grid=(t // tm,), in_specs=[row, row, vec, vec], out_specs=[row, row],
        out_shape=[jax.ShapeDtypeStruct((t, d), F32), jax.ShapeDtypeStruct((t, d), BF16)],
        compiler_params=_params(("parallel",), 4 * _nbytes((tm, d), F32)),
    )(xres, ff, gpost, gpre)


def _final_fwd_bwd(name, xres, ff, gpost, target, scale):
    t, d = xres.shape
    tm = _row_tile(t)

    def body(x_ref, f_ref, gp_ref, t_ref, loss_ref, dy_ref, dff_ref, dg_ref):
        i = pl.program_id(0)
        f = f_ref[...]
        gp = gp_ref[...]
        y = x_ref[...] + scale * (f * _rms(f) * gp)
        err = y - t_ref[...]
        part = 0.5 * jnp.sum(jnp.mean(err * err, axis=-1, keepdims=True), axis=0, keepdims=True)
        _accumulate(loss_ref, jnp.broadcast_to(part, loss_ref.shape), i == 0)
        dy = err / d
        dy_ref[...] = dy
        dff, dg = _norm_bwd(scale * dy, f, gp)
        dff_ref[...] = dff.astype(BF16)
        _accumulate(dg_ref, dg, i == 0)

    row = pl.BlockSpec((tm, d), lambda i: (i, 0))
    vec = pl.BlockSpec((1, d), lambda i: (0, 0))
    return pl.pallas_call(
        body, name=name, grid=(t // tm,), in_specs=[row, row, vec, row],
        out_specs=[pl.BlockSpec((8, 128), lambda i: (0, 0)), row, row, vec],
        out_shape=[jax.ShapeDtypeStruct((8, 128), F32), jax.ShapeDtypeStruct((t, d), F32),
                   jax.ShapeDtypeStruct((t, d), BF16), jax.ShapeDtypeStruct((1, d), F32)],
        compiler_params=_params(("arbitrary",), 5 * _nbytes((tm, d), F32)),
    )(xres, ff, gpost, target)


def _norms_bwd(name, dres, dh, xin, gpre, post=None):
    t, d = dres.shape
    tm = _row_tile(t)
    with_post = post is not None

    def body(*refs):
        if with_post:
            dr_ref, dh_ref, x_ref, g_ref, f_ref, gp_ref, dx_ref, dg_ref, dff_ref, dgp_ref = refs
        else:
            dr_ref, dh_ref, x_ref, g_ref, dx_ref, dg_ref = refs
        i = pl.program_id(0)
        dx, dg = _norm_bwd(dh_ref[...], x_ref[...], g_ref[...])
        dx = dr_ref[...] + dx
        dx_ref[...] = dx
        _accumulate(dg_ref, dg, i == 0)
        if with_post:
            dff, dgp = _norm_bwd(post[2] * dx, f_ref[...], gp_ref[...])
            dff_ref[...] = dff.astype(BF16)
            _accumulate(dgp_ref, dgp, i == 0)

    row = pl.BlockSpec((tm, d), lambda i: (i, 0))
    vec = pl.BlockSpec((1, d), lambda i: (0, 0))
    ins, in_specs = [dres, dh, xin, gpre], [row, row, row, vec]
    out_specs = [row, vec]
    out_shape = [jax.ShapeDtypeStruct((t, d), F32), jax.ShapeDtypeStruct((1, d), F32)]
    if with_post:
        ins += [post[0], post[1]]
        in_specs += [row, vec]
        out_specs += [row, vec]
        out_shape += [jax.ShapeDtypeStruct((t, d), BF16), jax.ShapeDtypeStruct((1, d), F32)]
    return pl.pallas_call(
        body, name=name, grid=(t // tm,), in_specs=in_specs, out_specs=out_specs, out_shape=out_shape,
        compiler_params=_params(("arbitrary",), 6 * _nbytes((tm, d), F32)),
    )(*ins)


def _swiglu_fwd(name, gu):
    t, f2 = gu.shape
    f = f2 // 2
    tm, tf = _tile(t, 512, 8), _tile(f, 512, 128)
    nf = f // tf

    def body(g_ref, u_ref, a_ref):
        g = g_ref[...].astype(F32)
        a_ref[...] = (g * _sigmoid(g) * u_ref[...].astype(F32)).astype(BF16)

    return pl.pallas_call(
        body, name=name, grid=(t // tm, nf),
        in_specs=[pl.BlockSpec((tm, tf), lambda i, j: (i, j)), pl.BlockSpec((tm, tf), lambda i, j: (i, j + nf))],
        out_specs=pl.BlockSpec((tm, tf), lambda i, j: (i, j)),
        out_shape=jax.ShapeDtypeStruct((t, f), BF16),
        compiler_params=_params(("parallel", "parallel"), 3 * _nbytes((tm, tf), F32)),
    )(gu, gu)


def _swiglu_bwd(name, da, gu):
    t, f = da.shape
    tm, tf = _tile(t, 512, 8), _tile(f, 512, 128)
    nf = f // tf

    def body(da_ref, g_ref, u_ref, o_ref):
        g = g_ref[...].astype(F32)
        u = u_ref[...].astype(F32)
        dav = da_ref[...]
        sig = _sigmoid(g)
        o_ref[0] = (dav * u * sig * (1.0 + g * (1.0 - sig))).astype(BF16)
        o_ref[1] = (dav * g * sig).astype(BF16)

    return pl.pallas_call(
        body, name=name, grid=(t // tm, nf),
        in_specs=[pl.BlockSpec((tm, tf), lambda i, j: (i, j)), pl.BlockSpec((tm, tf), lambda i, j: (i, j)),
                  pl.BlockSpec((tm, tf), lambda i, j: (i, j + nf))],
        out_specs=pl.BlockSpec((2, tm, tf), lambda i, j: (0, i, j)),
        out_shape=jax.ShapeDtypeStruct((2, t, f), BF16),
        compiler_params=_params(("parallel", "parallel"), 5 * _nbytes((tm, tf), F32)),
    )(da, gu, gu)


def _ffn_fwd(tag, h, w_gu, w_down):
    gu = _mm(f"{tag}_gate_up", h, w_gu, "nn", BF16)
    act = _swiglu_fwd(f"{tag}_act", gu)
    ff = _mm(f"{tag}_down", act, w_down, "nn", F32)
    return gu, act, ff


def _ffn_bwd(tag, dff, h, gu, act, w_gu, w_down):
    t, d = h.shape
    f = act.shape[1]
    da = _mm(f"{tag}_dact", dff, w_down, "nt", F32)
    dw_down = _mm(f"{tag}_dw_down", act, dff, "tn", BF16)
    dgu = _swiglu_bwd(f"{tag}_dact_bwd", da, gu)
    tm, tn, tk = _mm_tiles(t, d, f)
    nkf = f // tk
    dh = _matmul(f"{tag}_dh", dgu, w_gu, form="nt", out_dtype=F32, tm=tm, tn=tn, tk=tk, sizes=(t, d, 2 * f),
                 a_map=lambda i, j, kk: (kk // nkf, i, kk % nkf))
    tm, tn, tk = _mm_tiles(d, f, t)
    nf = f // tn
    dw_gu = _matmul(f"{tag}_dw_gate_up", h, dgu, form="tn", out_dtype=BF16, tm=tm, tn=tn, tk=tk,
                    sizes=(d, 2 * f, t), b_map=lambda i, j, kk: (j // nf, kk, j % nf))
    return dh, dw_gu, dw_down


def _lower_bound(lbp):
    m = jnp.max(lbp, axis=0, keepdims=True)
    e = jnp.exp(lbp - m)
    return e[0:1] / jnp.sum(e, axis=0, keepdims=True)


def _chunk_mask(reverse):
    row = lax.broadcasted_iota(jnp.int32, (CHUNK, CHUNK), 0)
    col = lax.broadcasted_iota(jnp.int32, (CHUNK, CHUNK), 1)
    return (col >= row) if reverse else (col <= row)


def _hgrn_gates(z, lb, mask_bf):
    sig = _sigmoid(z)
    f = lb + (1.0 - lb) * sig
    logf = jnp.log(f)
    k = 1.0 - f
    cum = _dot_exact(mask_bf, logf)
    last = jnp.sum(logf, axis=0, keepdims=True)
    return sig, f, k, cum, last


def _hgrn_fwd(name, p, lbp, f_group, reverse):
    t = p.shape[0]
    hw = lbp.shape[1]
    nh, nc = hw // HEAD, t // CHUNK

    def chunk(n):
        return (nc - 1 - n) if reverse else n

    def body(q_ref, v_ref, z_ref, lb_ref, o_ref, st_ref, state):
        n = pl.program_id(1)

        @pl.when(n == 0)
        def _():
            state[...] = jnp.zeros_like(state)

        mask = _chunk_mask(reverse)
        lb = _lower_bound(lb_ref[...])
        _, _, k, cum, last = _hgrn_gates(z_ref[...], lb, mask.astype(BF16))
        v = v_ref[...].astype(BF16)
        qd = (q_ref[...] * jnp.exp(cum)).astype(BF16)
        kd = (k * jnp.exp(-cum)).astype(BF16)
        kt = (k * jnp.exp(last - cum)).astype(BF16)
        s_in = state[...]
        st_ref[...] = s_in
        a = jnp.where(mask, _dot(qd, kd, 1, 1), 0.0).astype(BF16)
        o_ref[...] = _dot(a, v) + _dot(qd, s_in.astype(BF16), 1, 1)
        state[...] = s_in * jnp.exp(last) + _dot(v, kt, 0, 0)

    def col(group):
        return pl.BlockSpec((CHUNK, HEAD), lambda h, n: (chunk(n), group * nh + h))

    return pl.pallas_call(
        body, name=name, grid=(nh, nc),
        in_specs=[col(0), col(1), col(f_group), pl.BlockSpec((2, HEAD), lambda h, n: (0, h))],
        out_specs=[pl.BlockSpec((CHUNK, HEAD), lambda h, n: (chunk(n), h)),
                   pl.BlockSpec((None, None, HEAD, HEAD), lambda h, n: (h, chunk(n), 0, 0))],
        out_shape=[jax.ShapeDtypeStruct((t, hw), F32), jax.ShapeDtypeStruct((nh, nc, HEAD, HEAD), F32)],
        scratch_shapes=[pltpu.VMEM((HEAD, HEAD), F32)],
        compiler_params=_params(("parallel", "arbitrary"), 1 << 20),
    )(p, p, p, lbp)


def _hgrn_bwd(name, p, lbp, do, states, f_group, reverse):
    t = p.shape[0]
    hw = lbp.shape[1]
    nh, nc = hw // HEAD, t // CHUNK

    def chunk(n):
        return n if reverse else (nc - 1 - n)

    def body(q_ref, v_ref, z_ref, lb_ref, do_ref, st_ref, dq_ref, dv_ref, dz_ref, dlb_ref, dstate, dlb_acc):
        n = pl.program_id(1)

        @pl.when(n == 0)
        def _():
            dstate[...] = jnp.zeros_like(dstate)
            dlb_acc[...] = jnp.zeros_like(dlb_acc)

        mask = _chunk_mask(reverse)
        mask_bf = mask.astype(BF16)
        lbp_v = lb_ref[...]
        lb = _lower_bound(lbp_v)
        sig, f, k, cum, last = _hgrn_gates(z_ref[...], lb, mask_bf)
        e_pos, e_neg, e_tail = jnp.exp(cum), jnp.exp(-cum), jnp.exp(last - cum)
        dec = jnp.exp(last)
        v = v_ref[...].astype(BF16)
        qd, kd, kt = q_ref[...] * e_pos, k * e_neg, k * e_tail
        qd_bf, kd_bf, kt_bf = qd.astype(BF16), kd.astype(BF16), kt.astype(BF16)
        s_in = st_ref[...]
        ds_out = dstate[...]
        ds_bf = ds_out.astype(BF16)
        dov = do_ref[...].astype(BF16)
        a = jnp.where(mask, _dot(qd_bf, kd_bf, 1, 1), 0.0).astype(BF16)
        da = jnp.where(mask, _dot(dov, v, 1, 1), 0.0).astype(BF16)
        dv_ref[...] = _dot(a, dov, 0, 0) + _dot(kt_bf, ds_bf, 1, 1)
        dqd = _dot(da, kd_bf) + _dot(dov, s_in.astype(BF16))
        dkd = _dot(da, qd_bf, 0, 0)
        dkt = _dot(v, ds_bf)
        dstate[...] = _dot(dov, qd_bf, 0, 0) + ds_out * dec
        dlast = jnp.sum(dkt * kt, axis=0, keepdims=True) + dec * jnp.sum(ds_out * s_in, axis=0, keepdims=True)
        dq_ref[...] = dqd * e_pos
        dk = dkd * e_neg + dkt * e_tail
        dcum = dqd * qd - dkd * kd - dkt * kt
        dlogf = _dot_exact(mask_bf, dcum, 0, 0) + dlast
        df = dlogf / f - dk
        dz_ref[...] = df * (1.0 - lb) * sig * (1.0 - sig)
        dlb_acc[...] += jnp.sum(df * (1.0 - sig), axis=0, keepdims=True)

        @pl.when(n == nc - 1)
        def _():
            g = dlb_acc[...] * lb * (1.0 - lb)
            dlb_ref[0:1, :] = g
            dlb_ref[1:2, :] = -g

    def col(group):
        return pl.BlockSpec((CHUNK, HEAD), lambda h, n: (chunk(n), group * nh + h))

    blk = pl.BlockSpec((CHUNK, HEAD), lambda h, n: (chunk(n), h))
    out = jax.ShapeDtypeStruct((t, hw), F32)
    return pl.pallas_call(
        body, name=name, grid=(nh, nc),
        in_specs=[col(0), col(1), col(f_group), pl.BlockSpec((2, HEAD), lambda h, n: (0, h)), blk,
                  pl.BlockSpec((None, None, HEAD, HEAD), lambda h, n: (h, chunk(n), 0, 0))],
        out_specs=[blk, blk, blk, pl.BlockSpec((2, HEAD), lambda h, n: (0, h))],
        out_shape=[out, out, out, jax.ShapeDtypeStruct((2, hw), F32)],
        scratch_shapes=[pltpu.VMEM((HEAD, HEAD), F32), pltpu.VMEM((1, HEAD), F32)],
        compiler_params=_params(("parallel", "arbitrary"), 1 << 20),
    )(p, p, p, lbp, do, states)


def _hgrn_out_fwd(name, o_f, o_b, p, gain, g_group):
    t, hw = o_f.shape
    nh = hw // HEAD
    tm = _tile(t, 512, 8)

    def body(of_ref, ob_ref, g_ref, gain_ref, y_ref):
        o = of_ref[...] + ob_ref[...]
        g = g_ref[...]
        y_ref[...] = (o * _rms(o) * gain_ref[...] * (g * _sigmoid(g))).astype(BF16)

    blk = pl.BlockSpec((tm, HEAD), lambda i, h: (i, h))
    return pl.pallas_call(
        body, name=name, grid=(t // tm, nh),
        in_specs=[blk, blk, pl.BlockSpec((tm, HEAD), lambda i, h: (i, g_group * nh + h)),
                  pl.BlockSpec((1, HEAD), lambda i, h: (0, h))],
        out_specs=blk, out_shape=jax.ShapeDtypeStruct((t, hw), BF16),
        compiler_params=_params(("parallel", "parallel"), 1 << 20),
    )(o_f, o_b, p, gain)


def _hgrn_out_bwd(name, dy, o_f, o_b, p, gain, g_group):
    t, hw = o_f.shape
    nh = hw // HEAD
    tm = _tile(t, 512, 8)

    def body(dy_ref, of_ref, ob_ref, g_ref, gain_ref, do_ref, dg_ref, dgain_ref):
        i = pl.program_id(1)
        o = of_ref[...] + ob_ref[...]
        g = g_ref[...]
        gain_v = gain_ref[...]
        sig = _sigmoid(g)
        dyv = dy_ref[...]
        do, dgain = _norm_bwd(dyv * (g * sig), o, gain_v)
        do_ref[...] = do
        dg_ref[...] = dyv * (o * _rms(o) * gain_v) * sig * (1.0 + g * (1.0 - sig))
        _accumulate(dgain_ref, dgain, i == 0)

    blk = pl.BlockSpec((tm, HEAD), lambda h, i: (i, h))
    vec = pl.BlockSpec((1, HEAD), lambda h, i: (0, h))
    out = jax.ShapeDtypeStruct((t, hw), F32)
    return pl.pallas_call(
        body, name=name, grid=(nh, t // tm),
        in_specs=[blk, blk, blk, pl.BlockSpec((tm, HEAD), lambda h, i: (i, g_group * nh + h)), vec],
        out_specs=[blk, blk, vec], out_shape=[out, out, jax.ShapeDtypeStruct((1, hw), F32)],
        compiler_params=_params(("parallel", "arbitrary"), 1 << 20),
    )(dy, o_f, o_b, p, gain)


def _t5_bucket_ids():
    c = np.arange(WINDOW)[:, None]
    s = np.arange(SPAN)[None, :]
    rel = s - WINDOW - c
    nb = REL_BUCKETS // 2
    max_exact = nb // 2
    bucket = (rel > 0).astype(np.int32) * nb
    n = np.abs(rel)
    large = max_exact + (np.log(np.maximum(n, 1) / max_exact) / np.log(REL_MAX_DIST / max_exact)
                         * (nb - max_exact)).astype(np.int32)
    large = np.minimum(large, nb - 1)
    ids = bucket + np.where(n < max_exact, n, large).astype(np.int32)
    return jnp.asarray(ids.reshape(1, WINDOW * SPAN), jnp.int32)


def _bias_onehot(ids_ref):
    n = ids_ref.shape[1]
    return (lax.broadcasted_iota(jnp.int32, (REL_BUCKETS, n), 0) == ids_ref[...]).astype(BF16)


def _bias_gather(name, table_t, ids):
    nh = table_t.shape[0]

    def body(t_ref, ids_ref, o_ref):
        o_ref[...] = _dot_exact(t_ref[...], _bias_onehot(ids_ref), split="a")

    return pl.pallas_call(
        body, name=name, out_shape=jax.ShapeDtypeStruct((nh, ids.shape[1]), F32),
        compiler_params=pltpu.CompilerParams(vmem_limit_bytes=32 << 20),
    )(table_t, ids)


def _bias_scatter(name, dbias, ids):
    nh = dbias.shape[0]

    def body(d_ref, ids_ref, o_ref):
        o_ref[...] = _dot_exact(d_ref[...], _bias_onehot(ids_ref), 1, 1, split="a")

    return pl.pallas_call(
        body, name=name, out_shape=jax.ShapeDtypeStruct((nh, REL_BUCKETS), F32),
        compiler_params=pltpu.CompilerParams(vmem_limit_bytes=32 << 20),
    )(dbias, ids)


def _attn_valid(i, t):
    c = lax.broadcasted_iota(jnp.int32, (WINDOW, SPAN), 0)
    s = lax.broadcasted_iota(jnp.int32, (WINDOW, SPAN), 1)
    rel = s - WINDOW - c
    pos = i * WINDOW - WINDOW + s
    return (jnp.abs(rel) <= WINDOW) & (pos >= 0) & (pos < t)


def _attn_probs(qh, kh, bias_h, sink_h, valid):
    s = _dot(qh, kh, 1, 1) / math.sqrt(HEAD)
    s = jnp.where(valid, s + bias_h, NEG_INF)
    m = jnp.maximum(jnp.max(s, axis=-1, keepdims=True), sink_h)
    e = jnp.exp(s - m)
    es = jnp.exp(sink_h - m)
    inv = 1.0 / (jnp.sum(e, axis=-1, keepdims=True) + es)
    return e * inv, es * inv


def _attn_fwd(name, p, k_pad, v_pad, bias, sink, q_group_blk):
    t = p.shape[0]
    nh = bias.shape[0]
    aw = nh * HEAD
    grp = nh // KV_HEADS
    nb = t // WINDOW

    def body(q_ref, k_ref, v_ref, b_ref, s_ref, y_ref):
        i = pl.program_id(0)
        valid = _attn_valid(i, t)
        start = pl.multiple_of(i * WINDOW, WINDOW)
        ks = k_ref[pl.ds(start, SPAN), :]
        vs = v_ref[pl.ds(start, SPAN), :]
        for h in range(nh):
            kv = h // grp
            qh = q_ref[:, h * HEAD:(h + 1) * HEAD].astype(BF16)
            pr, _ = _attn_probs(qh, ks[:, kv * HEAD:(kv + 1) * HEAD], b_ref[h], s_ref[0:1, h:h + 1], valid)
            y_ref[:, h * HEAD:(h + 1) * HEAD] = _dot(pr.astype(BF16), vs[:, kv * HEAD:(kv + 1) * HEAD]).astype(BF16)

    full = lambda a: pl.BlockSpec(a.shape, lambda i: (0,) * a.ndim)
    return pl.pallas_call(
        body, name=name, grid=(nb,),
        in_specs=[pl.BlockSpec((WINDOW, aw), lambda i: (i, q_group_blk)), full(k_pad), full(v_pad), full(bias),
                  full(sink)],
        out_specs=pl.BlockSpec((WINDOW, aw), lambda i: (i, 0)),
        out_shape=jax.ShapeDtypeStruct((t, aw), BF16),
        compiler_params=_params(("parallel",), _nbytes(k_pad.shape, BF16) * 2 + _nbytes(bias.shape, F32)),
    )(p, k_pad, v_pad, bias, sink)


def _attn_bwd(name, p, k_pad, v_pad, bias, sink, dy, q_group_blk, dy_blk):
    t = p.shape[0]
    nh = bias.shape[0]
    aw = nh * HEAD
    grp = nh // KV_HEADS
    nb = t // WINDOW
    kvw = k_pad.shape[1]

    def body(q_ref, k_ref, v_ref, b_ref, s_ref, dy_ref, dq_ref, dk_ref, dv_ref, db_ref, ds_ref):
        i = pl.program_id(0)

        @pl.when(i == 0)
        def _():
            dk_ref[...] = jnp.zeros_like(dk_ref)
            dv_ref[...] = jnp.zeros_like(dv_ref)
            db_ref[...] = jnp.zeros_like(db_ref)
            ds_ref[...] = jnp.zeros_like(ds_ref)

        valid = _attn_valid(i, t)
        start = pl.multiple_of(i * WINDOW, WINDOW)
        ks = k_ref[pl.ds(start, SPAN), :]
        vs = v_ref[pl.ds(start, SPAN), :]
        inv_sqrt = 1.0 / math.sqrt(HEAD)
        for kv in range(KV_HEADS):
            kh = ks[:, kv * HEAD:(kv + 1) * HEAD]
            vh = vs[:, kv * HEAD:(kv + 1) * HEAD]
            dk_acc = jnp.zeros((SPAN, HEAD), F32)
            dv_acc = jnp.zeros((SPAN, HEAD), F32)
            for h in range(kv * grp, (kv + 1) * grp):
                qh = q_ref[:, h * HEAD:(h + 1) * HEAD].astype(BF16)
                pr, ps = _attn_probs(qh, kh, b_ref[h], s_ref[0:1, h:h + 1], valid)
                doh = dy_ref[:, h * HEAD:(h + 1) * HEAD].astype(BF16)
                dp = _dot(doh, vh, 1, 1)
                delta = jnp.sum(pr * dp, axis=-1, keepdims=True)
                dsc = pr * (dp - delta)
                db_ref[h] += dsc
                ds_ref[h:h + 1, :] += jnp.broadcast_to(jnp.sum(-ps * delta, axis=0, keepdims=True), (1, 128))
                dsr = (dsc * inv_sqrt).astype(BF16)
                dq_ref[:, h * HEAD:(h + 1) * HEAD] = _dot(dsr, kh)
                dk_acc += _dot(dsr, qh, 0, 0)
                dv_acc += _dot(pr.astype(BF16), doh, 0, 0)
            dk_ref[pl.ds(start, SPAN), kv * HEAD:(kv + 1) * HEAD] += dk_acc
            dv_ref[pl.ds(start, SPAN), kv * HEAD:(kv + 1) * HEAD] += dv_acc

    full = lambda a: pl.BlockSpec(a.shape, lambda i: (0,) * a.ndim)
    whole = lambda shape: pl.BlockSpec(shape, lambda i: (0,) * len(shape))
    pad_shape = (t + 2 * WINDOW, kvw)
    return pl.pallas_call(
        body, name=name, grid=(nb,),
        in_specs=[pl.BlockSpec((WINDOW, aw), lambda i: (i, q_group_blk)), full(k_pad), full(v_pad), full(bias),
                  full(sink), pl.BlockSpec((WINDOW, aw), lambda i: (i, dy_blk))],
        out_specs=[pl.BlockSpec((WINDOW, aw), lambda i: (i, 0)), whole(pad_shape), whole(pad_shape),
                   whole(bias.shape), whole((nh, 128))],
        out_shape=[jax.ShapeDtypeStruct((t, aw), F32), jax.ShapeDtypeStruct(pad_shape, F32),
                   jax.ShapeDtypeStruct(pad_shape, F32), jax.ShapeDtypeStruct(bias.shape, F32),
                   jax.ShapeDtypeStruct((nh, 128), F32)],
        compiler_params=_params(("arbitrary",), 3 * _nbytes(pad_shape, F32) + 2 * _nbytes(bias.shape, F32)),
    )(p, k_pad, v_pad, bias, sink, dy)


def _pad_kv(name, p, kv_blk, kvw):
    t = p.shape[0]
    nb = t // WINDOW

    def body(x_ref, o_ref):
        i = pl.program_id(0)
        inside = jnp.logical_and(i >= 1, i <= nb)
        o_ref[...] = jnp.where(inside, x_ref[...], 0.0).astype(BF16)

    return pl.pallas_call(
        body, name=name, grid=(nb + 2,),
        in_specs=[pl.BlockSpec((WINDOW, kvw), lambda i: (jnp.clip(i - 1, 0, nb - 1), kv_blk))],
        out_specs=pl.BlockSpec((WINDOW, kvw), lambda i: (i, 0)),
        out_shape=jax.ShapeDtypeStruct((t + 2 * WINDOW, kvw), BF16),
        compiler_params=_params(("parallel",), 1 << 20),
    )(p)


def _mix_dproj(name, pieces, kv_pads, t):
    hw = pieces[0][0].shape[1]
    kvw = kv_pads[0].shape[1]
    widths = [hw] * len(pieces) + [kvw] * len(kv_pads)
    total = sum(widths)
    tm = WINDOW
    flat = [a for pc in pieces for a in pc]

    def body(*refs):
        o_ref = refs[-1]
        pos, off = 0, 0
        for pc in pieces:
            val = refs[pos][...]
            for extra in range(1, len(pc)):
                val = val + refs[pos + extra][...]
            o_ref[:, off:off + hw] = val.astype(BF16)
            pos += len(pc)
            off += hw
        for _ in kv_pads:
            o_ref[:, off:off + kvw] = refs[pos][...].astype(BF16)
            pos += 1
            off += kvw

    in_specs = [pl.BlockSpec((tm, hw), lambda i: (i, 0)) for _ in flat]
    in_specs += [pl.BlockSpec((tm, kvw), lambda i: (i + 1, 0)) for _ in kv_pads]
    return pl.pallas_call(
        body, name=name, grid=(t // tm,), in_specs=in_specs,
        out_specs=pl.BlockSpec((tm, total), lambda i: (i, 0)),
        out_shape=jax.ShapeDtypeStruct((t, total), BF16),
        compiler_params=_params(("parallel",), 3 * _nbytes((tm, total), F32)),
    )(*flat, *kv_pads)


def _concat_cols(name, a, b):
    t, wa = a.shape
    wb = b.shape[1]
    tm = _tile(t, 512, 16)

    def body(a_ref, b_ref, o_ref):
        o_ref[:, :wa] = a_ref[...]
        o_ref[:, wa:] = b_ref[...]

    return pl.pallas_call(
        body, name=name, grid=(t // tm,),
        in_specs=[pl.BlockSpec((tm, wa), lambda i: (i, 0)), pl.BlockSpec((tm, wb), lambda i: (i, 0))],
        out_specs=pl.BlockSpec((tm, wa + wb), lambda i: (i, 0)),
        out_shape=jax.ShapeDtypeStruct((t, wa + wb), a.dtype),
        compiler_params=_params(("parallel",), 2 * _nbytes((tm, wa + wb), a.dtype)),
    )(a, b)


def _cast_bf16(name, w):
    r, c = w.shape
    tr = _tile(r, 256, 16)

    def body(w_ref, o_ref):
        o_ref[...] = w_ref[...].astype(BF16)

    blk = pl.BlockSpec((tr, c), lambda i: (i, 0))
    return pl.pallas_call(
        body, name=name, grid=(r // tr,), in_specs=[blk], out_specs=blk,
        out_shape=jax.ShapeDtypeStruct((r, c), BF16),
        compiler_params=_params(("parallel",), 2 * _nbytes((tr, c), F32)),
    )(w)


def _adamw(name, w, g, m, v):
    r, c = w.shape
    tr = _tile(r, 128, 8)
    bc1 = 1.0 - ADAM_B1 ** ADAM_STEP
    bc2 = 1.0 - ADAM_B2 ** ADAM_STEP

    def body(w_ref, g_ref, m_ref, v_ref, d_ref, nm_ref, nv_ref):
        gv = g_ref[...]
        nm = ADAM_B1 * m_ref[...] + (1.0 - ADAM_B1) * gv
        nv = ADAM_B2 * v_ref[...] + (1.0 - ADAM_B2) * (gv * gv)
        nm_ref[...] = nm
        nv_ref[...] = nv
        d_ref[...] = -ADAM_LR * ((nm / bc1) / (jnp.sqrt(nv / bc2) + ADAM_EPS) + ADAM_WD * w_ref[...])

    blk = pl.BlockSpec((tr, c), lambda i: (i, 0))
    out = jax.ShapeDtypeStruct((r, c), F32)
    return pl.pallas_call(
        body, name=name, grid=(r // tr,), in_specs=[blk] * 4, out_specs=[blk] * 3, out_shape=[out] * 3,
        compiler_params=_params(("parallel",), 7 * _nbytes((tr, c), F32)),
    )(w, g, m, v)


def _mesh_pos():
    return lax.axis_index("x"), lax.axis_index("y"), lax.axis_index("c")


def _other_chips(x, y):
    return [(1 - x, y), (x, 1 - y), (1 - x, 1 - y)]


class _Big:
    def __init__(self, shard_shape, col_sharded):
        self.col = col_sharded
        r, c = shard_shape
        self.shard_shape = (r, c)
        self.full_shape = (r, N_CHIPS * c) if col_sharded else (N_CHIPS * r, c)
        self.half_shape = (r // 2, N_CHIPS * c) if col_sharded else (N_CHIPS * r, c // 2)
        self.shard_half_shape = (r // 2, c) if col_sharded else (r, c // 2)

    def region(self, ref, s, half=None):
        r, c = self.shard_shape
        if self.col:
            rows = slice(None) if half is None else pl.ds(half * (r // 2), r // 2)
            return ref.at[rows, pl.ds(s * c, c)]
        cols = slice(None) if half is None else pl.ds(half * (c // 2), c // 2)
        return ref.at[pl.ds(s * r, r), cols]

    def half_of_full(self, ref, half):
        r, c = self.full_shape
        if self.col:
            return ref.at[pl.ds(half * (r // 2), r // 2), :]
        return ref.at[:, pl.ds(half * (c // 2), c // 2)]

    def half_of_shard(self, ref, half):
        r, c = self.shard_shape
        if self.col:
            return ref.at[pl.ds(half * (r // 2), r // 2), :]
        return ref.at[:, pl.ds(half * (c // 2), c // 2)]

    def shard_of_half(self, ref, s):
        r, c = self.shard_shape
        if self.col:
            return ref.at[:, pl.ds(s * c, c)]
        return ref.at[pl.ds(s * r, r), :]


def _gather_weights(shards, geoms):
    nw = len(shards)

    def body(*refs):
        src, dst = refs[:nw], refs[nw:2 * nw]
        local_sems, send_sems, recv_sems = refs[2 * nw:]
        x, y, c = _mesh_pos()
        chips = _other_chips(x, y)
        mine = 2 * x + y
        sibling = (x, y, 1 - c)

        def remote(w, k, src_ref, dst_ref, to):
            return pltpu.make_async_remote_copy(src_ref=src_ref, dst_ref=dst_ref, send_sem=send_sems.at[w, k],
                                                recv_sem=recv_sems.at[w, k], device_id=to, device_id_type=MESH)

        own = [pltpu.make_async_copy(src[w], geoms[w].region(dst[w], mine), local_sems.at[w]) for w in range(nw)]
        for cp in own:
            cp.start()
        sends = []
        for w in range(nw):
            g = geoms[w]
            for k, chip in enumerate(chips):
                sends.append(remote(w, k, g.half_of_shard(src[w], c), g.region(dst[w], mine, c), (*chip, c)))
        for cp in sends:
            cp.start()
        passed = []
        for w in range(nw):
            g = geoms[w]
            for k, chip in enumerate(chips):
                landed = g.region(dst[w], 2 * chip[0] + chip[1], c)
                remote(w, k, landed, landed, (*chip, c)).wait_recv()
                fwd = remote(w, 3 + k, landed, landed, sibling)
                fwd.start()
                passed.append(fwd)
        for w in range(nw):
            g = geoms[w]
            for k, chip in enumerate(chips):
                landed = g.region(dst[w], 2 * chip[0] + chip[1], 1 - c)
                remote(w, 3 + k, landed, landed, sibling).wait_recv()
        for cp in sends + passed:
            cp.wait_send()
        for cp in own:
            cp.wait()

    return pl.pallas_call(
        body, name="gather_weights", in_specs=[ANY] * nw, out_specs=[ANY] * nw,
        out_shape=[jax.ShapeDtypeStruct(g.full_shape, BF16) for g in geoms],
        scratch_shapes=[pltpu.SemaphoreType.DMA((nw,)), pltpu.SemaphoreType.DMA((nw, 6)),
                        pltpu.SemaphoreType.DMA((nw, 6))],
    )(*shards)


def _pair_exchange(grads, geoms):
    nw = len(grads)

    def body(*refs):
        src, dst = refs[:nw], refs[nw:2 * nw]
        send_sems, recv_sems = refs[2 * nw:]
        x, y, c = _mesh_pos()
        copies = [pltpu.make_async_remote_copy(
            src_ref=geoms[w].half_of_full(src[w], 1 - c), dst_ref=dst[w], send_sem=send_sems.at[w],
            recv_sem=recv_sems.at[w], device_id=(x, y, 1 - c), device_id_type=MESH) for w in range(nw)]
        for cp in copies:
            cp.start()
        for cp in copies:
            cp.wait()

    return pl.pallas_call(
        body, name="grads_pair_exchange", in_specs=[ANY] * nw, out_specs=[ANY] * nw,
        out_shape=[jax.ShapeDtypeStruct(g.half_shape, BF16) for g in geoms],
        scratch_shapes=[pltpu.SemaphoreType.DMA((nw,)), pltpu.SemaphoreType.DMA((nw,))],
    )(*grads)


def _chip_exchange(halves, geoms):
    nw = len(halves)

    def body(*refs):
        src, dst = refs[:nw], refs[nw:2 * nw]
        send_sems, recv_sems = refs[2 * nw:]
        x, y, c = _mesh_pos()
        copies = []
        for w in range(nw):
            for k, chip in enumerate(_other_chips(x, y)):
                copies.append(pltpu.make_async_remote_copy(
                    src_ref=geoms[w].shard_of_half(src[w], 2 * chip[0] + chip[1]), dst_ref=dst[w].at[k],
                    send_sem=send_sems.at[w, k], recv_sem=recv_sems.at[w, k], device_id=(*chip, c),
                    device_id_type=MESH))
        for cp in copies:
            cp.start()
        for cp in copies:
            cp.wait()

    return pl.pallas_call(
        body, name="grads_chip_exchange", in_specs=[ANY] * nw, out_specs=[ANY] * nw,
        out_shape=[jax.ShapeDtypeStruct((3,) + g.shard_half_shape, BF16) for g in geoms],
        scratch_shapes=[pltpu.SemaphoreType.DMA((nw, 3)), pltpu.SemaphoreType.DMA((nw, 3))],
    )(*halves)


def _pair_share(halves, geoms):
    nw = len(halves)

    def body(*refs):
        src, dst = refs[:nw], refs[nw:2 * nw]
        local_sems, send_sems, recv_sems = refs[2 * nw:]
        x, y, c = _mesh_pos()
        own = [pltpu.make_async_copy(src[w], geoms[w].half_of_shard(dst[w], c), local_sems.at[w]) for w in range(nw)]
        for cp in own:
            cp.start()
        sends = [pltpu.make_async_remote_copy(
            src_ref=src[w], dst_ref=geoms[w].half_of_shard(dst[w], c), send_sem=send_sems.at[w],
            recv_sem=recv_sems.at[w], device_id=(x, y, 1 - c), device_id_type=MESH) for w in range(nw)]
        for cp in sends:
            cp.start()
        for w in range(nw):
            theirs = geoms[w].half_of_shard(dst[w], 1 - c)
            pltpu.make_async_remote_copy(src_ref=theirs, dst_ref=theirs, send_sem=send_sems.at[w],
                                         recv_sem=recv_sems.at[w], device_id=(x, y, 1 - c),
                                         device_id_type=MESH).wait_recv()
        for cp in sends:
            cp.wait_send()
        for cp in own:
            cp.wait()

    return pl.pallas_call(
        body, name="grads_pair_share", in_specs=[ANY] * nw, out_specs=[ANY] * nw,
        out_shape=[jax.ShapeDtypeStruct(g.shard_shape, F32) for g in geoms],
        scratch_shapes=[pltpu.SemaphoreType.DMA((nw,)), pltpu.SemaphoreType.DMA((nw,)),
                        pltpu.SemaphoreType.DMA((nw,))],
    )(*halves)


def _pair_add(name, grad, recv, geom, c_idx):
    r, c = geom.half_shape
    tr, tc = _tile(r, 256, 16), _tile(c, 2048, 128)
    nr, ncol = r // tr, c // tc
    if geom.col:
        mine = lambda i, j, cref: (cref[0] * nr + i, j)
    else:
        mine = lambda i, j, cref: (i, cref[0] * ncol + j)

    def body(c_ref, g_ref, r_ref, o_ref):
        o_ref[...] = (g_ref[...].astype(F32) + r_ref[...].astype(F32)).astype(BF16)

    return pl.pallas_call(
        body, name=name,
        grid_spec=pltpu.PrefetchScalarGridSpec(
            num_scalar_prefetch=1, grid=(nr, ncol),
            in_specs=[pl.BlockSpec((tr, tc), mine), pl.BlockSpec((tr, tc), lambda i, j, cref: (i, j))],
            out_specs=pl.BlockSpec((tr, tc), lambda i, j, cref: (i, j))),
        out_shape=jax.ShapeDtypeStruct((r, c), BF16),
        compiler_params=_params(("parallel", "parallel"), 3 * _nbytes((tr, tc), F32)),
    )(c_idx, grad, recv)


def _chip_add(name, half, recv, geom, s_idx):
    r, c = geom.shard_half_shape
    tr, tc = _tile(r, 256, 16), _tile(c, 2048, 128)
    nr, ncol = r // tr, c // tc
    if geom.col:
        mine = lambda i, j, sref: (i, sref[0] * ncol + j)
    else:
        mine = lambda i, j, sref: (sref[0] * nr + i, j)

    def body(s_ref, h_ref, r_ref, o_ref):
        acc = h_ref[...].astype(F32)
        for k in range(3):
            acc = acc + r_ref[k].astype(F32)
        o_ref[...] = acc

    return pl.pallas_call(
        body, name=name,
        grid_spec=pltpu.PrefetchScalarGridSpec(
            num_scalar_prefetch=1, grid=(nr, ncol),
            in_specs=[pl.BlockSpec((tr, tc), mine), pl.BlockSpec((3, tr, tc), lambda i, j, sref: (0, i, j))],
            out_specs=pl.BlockSpec((tr, tc), lambda i, j, sref: (i, j))),
        out_shape=jax.ShapeDtypeStruct((r, c), F32),
        compiler_params=_params(("parallel", "parallel"), 4 * _nbytes((tr, tc), F32)),
    )(s_idx, half, recv)


def _all_reduce_small(pack):
    r, d = pack.shape

    def body(p_ref, o_ref, slots, send_sems, recv_sems):
        x, y, c = _mesh_pos()
        me = 4 * x + 2 * y + c
        slots[me] = p_ref[...]
        copies = []
        for k in range(1, N_DEV):
            px, py, pc = x ^ ((k >> 2) & 1), y ^ ((k >> 1) & 1), c ^ (k & 1)
            copies.append(pltpu.make_async_remote_copy(
                src_ref=p_ref, dst_ref=slots.at[me], send_sem=send_sems.at[k - 1], recv_sem=recv_sems.at[k - 1],
                device_id=(px, py, pc), device_id_type=MESH))
        for cp in copies:
            cp.start()
        for k in range(1, N_DEV):
            peer = 4 * (x ^ ((k >> 2) & 1)) + 2 * (y ^ ((k >> 1) & 1)) + (c ^ (k & 1))
            pltpu.make_async_remote_copy(
                src_ref=p_ref, dst_ref=slots.at[peer], send_sem=send_sems.at[k - 1], recv_sem=recv_sems.at[k - 1],
                device_id=(x, y, c), device_id_type=MESH).wait_recv()
        for cp in copies:
            cp.wait_send()
        acc = slots[0]
        for k in range(1, N_DEV):
            acc = acc + slots[k]
        o_ref[...] = acc

    vm = pl.BlockSpec(memory_space=pltpu.VMEM)
    return pl.pallas_call(
        body, name="all_reduce_small", in_specs=[vm], out_specs=vm,
        out_shape=jax.ShapeDtypeStruct((r, d), F32),
        scratch_shapes=[pltpu.VMEM((N_DEV, r, d), F32), pltpu.SemaphoreType.DMA((N_DEV - 1,)),
                        pltpu.SemaphoreType.DMA((N_DEV - 1,))],
    )(pack)


def _pack_rows(rows, d):
    out = []
    for a in rows:
        flat = a.reshape(-1)
        n = -(-flat.shape[0] // d) * d
        out.append(jnp.pad(flat, (0, n - flat.shape[0])).reshape(-1, d))
    packed = jnp.concatenate(out, axis=0)
    return jnp.pad(packed, ((0, 16 - packed.shape[0]), (0, 0)))


def _unpack_rows(packed, shapes, d):
    out, row = [], 0
    for shp in shapes:
        n = int(np.prod(shp))
        nrows = -(-n // d)
        out.append(packed[row:row + nrows].reshape(-1)[:n].reshape(shp))
        row += nrows
    return out


def kernel(x, pre_norm_ffn1, post_norm_ffn1, w_ffn1_gate_up, w_ffn1_down, pre_norm_mix, post_norm_mix, w_mix_in, hgrn_lower_bounds_fwd, hgrn_lower_bounds_bwd, hgrn_out_norm, attn_sink, w_mix_out, pre_norm_ffn2, post_norm_ffn2, w_ffn2_gate_up, w_ffn2_down, rel_bias_table, loss_target, m_pre_norm_ffn1, m_post_norm_ffn1, m_w_ffn1_gate_up, m_w_ffn1_down, m_pre_norm_mix, m_post_norm_mix, m_w_mix_in, m_hgrn_lower_bounds_fwd, m_hgrn_lower_bounds_bwd, m_hgrn_out_norm, m_attn_sink, m_w_mix_out, m_pre_norm_ffn2, m_post_norm_ffn2, m_w_ffn2_gate_up, m_w_ffn2_down, m_rel_bias_table, v_pre_norm_ffn1, v_post_norm_ffn1, v_w_ffn1_gate_up, v_w_ffn1_down, v_pre_norm_mix, v_post_norm_mix, v_w_mix_in, v_hgrn_lower_bounds_fwd, v_hgrn_lower_bounds_bwd, v_hgrn_out_norm, v_attn_sink, v_w_mix_out, v_pre_norm_ffn2, v_post_norm_ffn2, v_w_ffn2_gate_up, v_w_ffn2_down, v_rel_bias_table):
    t, d = x.shape[1], x.shape[2]
    hw = hgrn_out_norm.shape[1]
    aw = d - hw
    nah = aw // HEAD
    kvw = KV_HEADS * HEAD
    x0 = x[0]
    target = loss_target[0]

    big_names = ["w_ffn1_gate_up", "w_ffn1_down", "w_mix_in", "w_mix_out", "w_ffn2_gate_up", "w_ffn2_down"]
    big_w = [w_ffn1_gate_up[0], w_ffn1_down[0], w_mix_in[0], w_mix_out[0], w_ffn2_gate_up[0], w_ffn2_down[0]]
    big_m = [m_w_ffn1_gate_up[0], m_w_ffn1_down[0], m_w_mix_in[0], m_w_mix_out[0], m_w_ffn2_gate_up[0],
             m_w_ffn2_down[0]]
    big_v = [v_w_ffn1_gate_up[0], v_w_ffn1_down[0], v_w_mix_in[0], v_w_mix_out[0], v_w_ffn2_gate_up[0],
             v_w_ffn2_down[0]]
    col_sharded = [True, False, True, False, True, False]
    geoms = [_Big(w.shape, cs) for w, cs in zip(big_w, col_sharded)]

    shards_bf = [_cast_bf16(f"cast_{n}", w) for n, w in zip(big_names, big_w)]
    w_gu1, w_d1, w_in, w_out, w_gu2, w_d2 = _gather_weights(shards_bf, geoms)

    h1 = _norm_fwd("ffn1_pre_norm", x0, pre_norm_ffn1)
    gu1, act1, ff1 = _ffn_fwd("ffn1", h1, w_gu1, w_d1)
    x1, hm = _resid_norm_fwd("ffn1_residual", x0, ff1, post_norm_ffn1, pre_norm_mix, 0.5)
    p = _mm("mix_in", hm, w_in, "nn", F32)
    o_f, st_f = _hgrn_fwd("hgrn_fwd_scan", p, hgrn_lower_bounds_fwd, 2, False)
    o_b, st_b = _hgrn_fwd("hgrn_bwd_scan", p, hgrn_lower_bounds_bwd, 3, True)
    y_h = _hgrn_out_fwd("hgrn_out", o_f, o_b, p, hgrn_out_norm, 4)
    kv_blk0 = (5 * hw + aw) // kvw
    k_pad = _pad_kv("attn_pad_k", p, kv_blk0, kvw)
    v_pad = _pad_kv("attn_pad_v", p, kv_blk0 + 1, kvw)
    bucket_ids = _t5_bucket_ids()
    bias = _bias_gather("attn_bias", rel_bias_table.T, bucket_ids).reshape(nah, WINDOW, SPAN)
    y_a = _attn_fwd("attn_fwd", p, k_pad, v_pad, bias, attn_sink, 5 * hw // aw)
    y_mix = _concat_cols("mix_concat", y_h, y_a)
    mixed = _mm("mix_out", y_mix, w_out, "nn", F32)
    x2, h2 = _resid_norm_fwd("mix_residual", x1, mixed, post_norm_mix, pre_norm_ffn2, 1.0)
    gu2, act2, ff2 = _ffn_fwd("ffn2", h2, w_gu2, w_d2)
    loss_blk, dy, dff2, dg_post2 = _final_fwd_bwd("ffn2_residual_loss", x2, ff2, post_norm_ffn2, target, 0.5)

    dh2, dw_gu2, dw_d2 = _ffn_bwd("ffn2", dff2, h2, gu2, act2, w_gu2, w_d2)
    dx2, dg_pre2, dmixed, dg_postm = _norms_bwd("mix_residual_bwd", dy, dh2, x2, pre_norm_ffn2,
                                                post=(mixed, post_norm_mix, 1.0))
    dy_mix = _mm("mix_out_dx", dmixed, w_out, "nt", F32)
    dw_out = _mm("mix_out_dw", y_mix, dmixed, "tn", BF16)
    dq_a, dk_pad, dv_pad, dbias, dsink = _attn_bwd("attn_bwd", p, k_pad, v_pad, bias, attn_sink, dy_mix,
                                                   5 * hw // aw, hw // aw)
    drel_t = _bias_scatter("attn_dbias", dbias.reshape(nah, WINDOW * SPAN), bucket_ids)
    do, dg_h, dgain = _hgrn_out_bwd("hgrn_out_bwd", dy_mix, o_f, o_b, p, hgrn_out_norm, 4)
    dq_f, dv_f, dz_f, dlb_f = _hgrn_bwd("hgrn_fwd_scan_bwd", p, hgrn_lower_bounds_fwd, do, st_f, 2, False)
    dq_b, dv_b, dz_b, dlb_b = _hgrn_bwd("hgrn_bwd_scan_bwd", p, hgrn_lower_bounds_bwd, do, st_b, 3, True)
    dp = _mix_dproj("mix_dproj", [(dq_f, dq_b), (dv_f, dv_b), (dz_f,), (dz_b,), (dg_h,), (dq_a,)],
                    [dk_pad, dv_pad], t)
    dhm = _mm("mix_in_dx", dp, w_in, "nt", F32)
    dw_in = _mm("mix_in_dw", hm, dp, "tn", BF16)
    dx1, dg_prem, dff1, dg_post1 = _norms_bwd("ffn1_residual_bwd", dx2, dhm, x1, pre_norm_mix,
                                              post=(ff1, post_norm_ffn1, 0.5))
    dh1, dw_gu1, dw_d1 = _ffn_bwd("ffn1", dff1, h1, gu1, act1, w_gu1, w_d1)
    grad_x, dg_pre1 = _norms_bwd("ffn1_pre_norm_bwd", dx1, dh1, x0, pre_norm_ffn1)

    cx, cy, cc = _mesh_pos()
    c_idx = jnp.reshape(cc, (1,)).astype(jnp.int32)
    s_idx = jnp.reshape(2 * cx + cy, (1,)).astype(jnp.int32)
    dws = [dw_gu1, dw_d1, dw_in, dw_out, dw_gu2, dw_d2]
    from_sibling = _pair_exchange(dws, geoms)
    pair_sums = [_pair_add(f"pair_add_{n}", g, r, gm, c_idx) for n, g, r, gm in zip(big_names, dws, from_sibling, geoms)]
    from_chips = _chip_exchange(pair_sums, geoms)
    reduced = [_chip_add(f"chip_add_{n}", h, r, gm, s_idx) for n, h, r, gm in zip(big_names, pair_sums, from_chips, geoms)]
    big_grads = _pair_share(reduced, geoms)

    small_w = [pre_norm_ffn1, post_norm_ffn1, pre_norm_mix, post_norm_mix, hgrn_lower_bounds_fwd,
               hgrn_lower_bounds_bwd, hgrn_out_norm, attn_sink, pre_norm_ffn2, post_norm_ffn2, rel_bias_table]
    small_m = [m_pre_norm_ffn1, m_post_norm_ffn1, m_pre_norm_mix, m_post_norm_mix, m_hgrn_lower_bounds_fwd,
               m_hgrn_lower_bounds_bwd, m_hgrn_out_norm, m_attn_sink, m_pre_norm_ffn2, m_post_norm_ffn2,
               m_rel_bias_table]
    small_v = [v_pre_norm_ffn1, v_post_norm_ffn1, v_pre_norm_mix, v_post_norm_mix, v_hgrn_lower_bounds_fwd,
               v_hgrn_lower_bounds_bwd, v_hgrn_out_norm, v_attn_sink, v_pre_norm_ffn2, v_post_norm_ffn2,
               v_rel_bias_table]
    small_g = [dg_pre1, dg_post1, dg_prem, dg_postm, dlb_f, dlb_b, dgain, dsink[:, 0].reshape(1, nah), dg_pre2,
               dg_post2, drel_t.T]
    shapes = [a.shape for a in small_w]
    summed = _all_reduce_small(_pack_rows(small_g + [loss_blk[0:1, 0:1]], d))
    g_pack = summed
    loss =_unpack_rows(summed, shapes + [(1, 1)], d)[-1][0, 0]
    sd, sm, sv = _adamw("adamw_small", _pack_rows(small_w, d), g_pack, _pack_rows(small_m, d), _pack_rows(small_v, d))
    small_grads = _unpack_rows(g_pack, shapes, d)
    small_delta, small_new_m, small_new_v = (_unpack_rows(a, shapes, d) for a in (sd, sm, sv))

    big_delta, big_new_m, big_new_v = [], [], []
    for n, w, g, m, v in zip(big_names, big_w, big_grads, big_m, big_v):
        dl, nm, nv = _adamw(f"adamw_{n}", w, g, m, v)
        big_delta.append(dl[None])
        big_new_m.append(nm[None])
        big_new_v.append(nv[None])
    big_grads = [g[None] for g in big_grads]

    def ordered(small, big):
        s = dict(zip(["pre1", "post1", "prem", "postm", "lbf", "lbb", "gain", "sink", "pre2", "post2", "rel"], small))
        b = dict(zip(["gu1", "d1", "win", "wout", "gu2", "d2"], big))
        return [s["pre1"], s["post1"], b["gu1"], b["d1"], s["prem"], s["postm"], b["win"], s["lbf"], s["lbb"],
                s["gain"], s["sink"], b["wout"], s["pre2"], s["post2"], b["gu2"], b["d2"], s["rel"]]

    return (loss, grad_x[None], *ordered(small_grads, big_grads), *ordered(small_delta, big_delta),
            *ordered(small_new_m, big_new_m), *ordered(small_new_v, big_new_v))
```

```python
import functools
import math

import jax
import jax.numpy as jnp
import numpy as np
from jax import lax
from jax.experimental import pallas as pl
from jax.experimental.pallas import tpu as pltpu

F32 = jnp.float32
BF16 = jnp.bfloat16

HEAD = 128
CHUNK = 64
WINDOW = 128
SPAN = 3 * WINDOW
KV_HEADS = 2
REL_BUCKETS = 32
REL_MAX_DIST = 128
EPS = 1e-6
NEG_INF = -1e30

ADAM_LR = 0.001
ADAM_B1 = 0.9
ADAM_B2 = 0.999
ADAM_EPS = 1e-08
ADAM_WD = 0.01
ADAM_STEP = 10

N_CHIPS = 4
N_DEV = 8
V7X_VMEM_BYTES = 64 * 1024 * 1024
MESH = pl.DeviceIdType.MESH
ANY = pl.BlockSpec(memory_space=pl.ANY)


def _tile(n, pref, mult):
    t = (min(pref, n) // mult) * mult
    while t >= mult:
        if n % t == 0:
            return t
        t -= mult
    return n


def _params(semantics, block_bytes):
    limit = min(V7X_VMEM_BYTES - (4 << 20), 2 * int(block_bytes) + (8 << 20))
    return pltpu.CompilerParams(dimension_semantics=semantics, vmem_limit_bytes=limit)


def _nbytes(shape, dtype):
    return int(np.prod(shape)) * jnp.dtype(dtype).itemsize


def _dot(a, b, ca=1, cb=0):
    return lax.dot_general(a, b, (((ca,), (cb,)), ((), ())), preferred_element_type=F32)


def _split3(x):
    hi = x.astype(BF16)
    r1 = x - hi.astype(F32)
    mid = r1.astype(BF16)
    lo = (r1 - mid.astype(F32)).astype(BF16)
    return hi, mid, lo


def _dot_exact(a, b, ca=1, cb=0, split="b"):
    if split == "b":
        return sum(_dot(a, p, ca, cb) for p in _split3(b))
    return sum(_dot(p, b, ca, cb) for p in _split3(a))


def _rms(x):
    return lax.rsqrt(jnp.mean(x * x, axis=-1, keepdims=True) + EPS)


def _norm_bwd(u, x, gain):
    r = _rms(x)
    xhat = x * r
    dgain = jnp.sum(u * xhat, axis=0, keepdims=True)
    v = u * gain
    dx = r * (v - xhat * jnp.mean(v * xhat, axis=-1, keepdims=True))
    return dx, dgain


def _sigmoid(x):
    return 1.0 / (1.0 + jnp.exp(-x))


def _accumulate(ref, val, first):
    @pl.when(first)
    def _():
        ref[...] = val

    @pl.when(jnp.logical_not(first))
    def _():
        ref[...] += val


def _matmul(name, a, b, *, form, out_dtype, tm, tn, tk, a_map=None, b_map=None,
            out_shape=None, out_block=None, out_map=None, sizes=None):
    if sizes is None:
        if form == "nn":
            (m, k), n = a.shape, b.shape[1]
        elif form == "nt":
            (m, k), n = a.shape, b.shape[0]
        else:
            (k, m), n = a.shape, b.shape[1]
    else:
        m, n, k = sizes
    gi, gj, gk = m // tm, n // tn, k // tk
    a_blk = (tm, tk) if form != "tn" else (tk, tm)
    b_blk = (tk, tn) if form != "nt" else (tn, tk)
    if a_map is None:
        a_map = (lambda i, j, kk: (i, kk)) if form != "tn" else (lambda i, j, kk: (kk, i))
    else:
        a_blk = (None,) + a_blk
    if b_map is None:
        b_map = (lambda i, j, kk: (kk, j)) if form != "nt" else (lambda i, j, kk: (j, kk))
    else:
        b_blk = (None,) + b_blk
    if out_shape is None:
        out_shape, out_block, out_map = (m, n), (tm, tn), (lambda i, j, kk: (i, j))
    ca, cb = {"nn": (1, 0), "nt": (1, 1), "tn": (0, 0)}[form]

    def body(a_ref, b_ref, o_ref, *acc):
        part = _dot(a_ref[...], b_ref[...], ca, cb)
        if gk == 1:
            o_ref[...] = part.astype(o_ref.dtype)
        else:
            kk = pl.program_id(2)
            _accumulate(acc[0], part, kk == 0)

            @pl.when(kk == gk - 1)
            def _():
                o_ref[...] = acc[0][...].astype(o_ref.dtype)

    scratch = [] if gk == 1 else [pltpu.VMEM((tm, tn), F32)]
    vmem = (_nbytes((tm, tk), a.dtype) + _nbytes((tk, tn), b.dtype) + _nbytes((tm, tn), out_dtype)
            + 2 * _nbytes((tm, tn), F32))
    return pl.pallas_call(
        body, name=name, grid=(gi, gj, gk),
        in_specs=[pl.BlockSpec(a_blk, a_map), pl.BlockSpec(b_blk, b_map)],
        out_specs=pl.BlockSpec(out_block, out_map),
        out_shape=jax.ShapeDtypeStruct(out_shape, out_dtype),
        scratch_shapes=scratch,
        compiler_params=_params(("parallel", "parallel", "arbitrary"), vmem),
    )(a, b)


def _mm_tiles(m, n, k):
    return _tile(m, 1024, 128), _tile(n, 512, 128), _tile(k, 2816, 128)


def _mm(name, a, b, form, out_dtype):
    if form == "nn":
        m, k, n = a.shape[0], a.shape[1], b.shape[1]
    elif form == "nt":
        m, k, n = a.shape[0], a.shape[1], b.shape[0]
    else:
        m, k, n = a.shape[1], a.shape[0], b.shape[1]
    tm, tn, tk = _mm_tiles(m, n, k)
    return _matmul(name, a, b, form=form, out_dtype=out_dtype, tm=tm, tn=tn, tk=tk)


def _row_tile(t):
    return _tile(t, 256, 8)


def _norm_fwd(name, x, gain):
    t, d = x.shape
    tm = _row_tile(t)

    def body(x_ref, g_ref, h_ref):
        xv = x_ref[...]
        h_ref[...] = (xv * _rms(xv) * g_ref[...]).astype(BF16)

    row = pl.BlockSpec((tm, d), lambda i: (i, 0))
    vec = pl.BlockSpec((1, d), lambda i: (0, 0))
    return pl.pallas_call(
        body, name=name, grid=(t // tm,), in_specs=[row, vec], out_specs=row,
        out_shape=jax.ShapeDtypeStruct((t, d), BF16),
        compiler_params=_params(("parallel",), 2 * _nbytes((tm, d), F32)),
    )(x, gain)


def _resid_norm_fwd(name, xres, ff, gpost, gpre, scale):
    t, d = xres.shape
    tm = _row_tile(t)

    def body(x_ref, f_ref, gp_ref, gn_ref, xn_ref, h_ref):
        f = f_ref[...]
        xn = x_ref[...] + scale * (f * _rms(f) * gp_ref[...])
        xn_ref[...] = xn
        h_ref[...] = (xn * _rms(xn) * gn_ref[...]).astype(BF16)

    row = pl.BlockSpec((tm, d), lambda i: (i, 0))
    vec = pl.BlockSpec((1, d), lambda i: (0, 0))
    return pl.pallas_call(
        body, name=name, grid=(t // tm,), in_specs=[row, row, vec, vec], out_specs=[row, row],
        out_shape=[jax.ShapeDtypeStruct((t, d), F32), jax.ShapeDtypeStruct((t, d), BF16)],
        compiler_params=_params(("parallel",), 4 * _nbytes((tm, d), F32)),
    )(xres, ff, gpost, gpre)


def _final_fwd_bwd(name, xres, ff, gpost, target, scale):
    t, d = xres.shape
    tm = _row_tile(t)

    def body(x_ref, f_ref, gp_ref, t_ref, loss_ref, dy_ref, dff_ref, dg_ref):
        i = pl.program_id(0)
        f = f_ref[...]
        gp = gp_ref[...]
        y = x_ref[...] + scale * (f * _rms(f) * gp)
        err = y - t_ref[...]
        part = 0.5 * jnp.sum(jnp.mean(err * err, axis=-1, keepdims=True), axis=0, keepdims=True)
        _accumulate(loss_ref, jnp.broadcast_to(part, loss_ref.shape), i == 0)
        dy = err / d
        dy_ref[...] = dy
        dff, dg = _norm_bwd(scale * dy, f, gp)
        dff_ref[...] = dff.astype(BF16)
        _accumulate(dg_ref, dg, i == 0)

    row = pl.BlockSpec((tm, d), lambda i: (i, 0))
    vec = pl.BlockSpec((1, d), lambda i: (0, 0))
    return pl.pallas_call(
        body, name=name, grid=(t // tm,), in_specs=[row, row, vec, row],
        out_specs=[pl.BlockSpec((8, 128), lambda i: (0, 0)), row, row, vec],
        out_shape=[jax.ShapeDtypeStruct((8, 128), F32), jax.ShapeDtypeStruct((t, d), F32),
                   jax.ShapeDtypeStruct((t, d), BF16), jax.ShapeDtypeStruct((1, d), F32)],
        compiler_params=_params(("arbitrary",), 5 * _nbytes((tm, d), F32)),
    )(xres, ff, gpost, target)


def _norms_bwd(name, dres, dh, xin, gpre, post=None):
    t, d = dres.shape
    tm = _row_tile(t)
    with_post = post is not None

    def body(*refs):
        if with_post:
            dr_ref, dh_ref, x_ref, g_ref, f_ref, gp_ref, dx_ref, dg_ref, dff_ref, dgp_ref = refs
        else:
            dr_ref, dh_ref, x_ref, g_ref, dx_ref, dg_ref = refs
        i = pl.program_id(0)
        dx, dg = _norm_bwd(dh_ref[...], x_ref[...], g_ref[...])
        dx = dr_ref[...] + dx
        dx_ref[...] = dx
        _accumulate(dg_ref, dg, i == 0)
        if with_post:
            dff, dgp = _norm_bwd(post[2] * dx, f_ref[...], gp_ref[...])
            dff_ref[...] = dff.astype(BF16)
            _accumulate(dgp_ref, dgp, i == 0)

    row = pl.BlockSpec((tm, d), lambda i: (i, 0))
    vec = pl.BlockSpec((1, d), lambda i: (0, 0))
    ins, in_specs = [dres, dh, xin, gpre], [row, row, row, vec]
    out_specs = [row, vec]
    out_shape = [jax.ShapeDtypeStruct((t, d), F32), jax.ShapeDtypeStruct((1, d), F32)]
    if with_post:
        ins += [post[0], post[1]]
        in_specs += [row, vec]
        out_specs += [row, vec]
        out_shape += [jax.ShapeDtypeStruct((t, d), BF16), jax.ShapeDtypeStruct((1, d), F32)]
    return pl.pallas_call(
        body, name=name, grid=(t // tm,), in_specs=in_specs, out_specs=out_specs, out_shape=out_shape,
        compiler_params=_params(("arbitrary",), 6 * _nbytes((tm, d), F32)),
    )(*ins)


def _swiglu_fwd(name, gu):
    t, f2 = gu.shape
    f = f2 // 2
    tm, tf = _tile(t, 512, 8), _tile(f, 512, 128)
    nf = f // tf

    def body(g_ref, u_ref, a_ref):
        g = g_ref[...].astype(F32)
        a_ref[...] = (g * _sigmoid(g) * u_ref[...].astype(F32)).astype(BF16)

    return pl.pallas_call(
        body, name=name, grid=(t // tm, nf),
        in_specs=[pl.BlockSpec((tm, tf), lambda i, j: (i, j)), pl.BlockSpec((tm, tf), lambda i, j: (i, j + nf))],
        out_specs=pl.BlockSpec((tm, tf), lambda i, j: (i, j)),
        out_shape=jax.ShapeDtypeStruct((t, f), BF16),
        compiler_params=_params(("parallel", "parallel"), 3 * _nbytes((tm, tf), F32)),
    )(gu, gu)


def _swiglu_bwd(name, da, gu):
    t, f = da.shape
    tm, tf = _tile(t, 512, 8), _tile(f, 512, 128)
    nf = f // tf

    def body(da_ref, g_ref, u_ref, o_ref):
        g = g_ref[...].astype(F32)
        u = u_ref[...].astype(F32)
        dav = da_ref[...]
        sig = _sigmoid(g)
        o_ref[0] = (dav * u * sig * (1.0 + g * (1.0 - sig))).astype(BF16)
        o_ref[1] = (dav * g * sig).astype(BF16)

    return pl.pallas_call(
        body, name=name, grid=(t // tm, nf),
        in_specs=[pl.BlockSpec((tm, tf), lambda i, j: (i, j)), pl.BlockSpec((tm, tf), lambda i, j: (i, j)),
                  pl.BlockSpec((tm, tf), lambda i, j: (i, j + nf))],
        out_specs=pl.BlockSpec((2, tm, tf), lambda i, j: (0, i, j)),
        out_shape=jax.ShapeDtypeStruct((2, t, f), BF16),
        compiler_params=_params(("parallel", "parallel"), 5 * _nbytes((tm, tf), F32)),
    )(da, gu, gu)


def _ffn_fwd(tag, h, w_gu, w_down):
    gu = _mm(f"{tag}_gate_up", h, w_gu, "nn", BF16)
    act = _swiglu_fwd(f"{tag}_act", gu)
    ff = _mm(f"{tag}_down", act, w_down, "nn", F32)
    return gu, act, ff


def _ffn_bwd(tag, dff, h, gu, act, w_gu, w_down):
    t, d = h.shape
    f = act.shape[1]
    da = _mm(f"{tag}_dact", dff, w_down, "nt", F32)
    dw_down = _mm(f"{tag}_dw_down", act, dff, "tn", BF16)
    dgu = _swiglu_bwd(f"{tag}_dact_bwd", da, gu)
    tm, tn, tk = _mm_tiles(t, d, f)
    nkf = f // tk
    dh = _matmul(f"{tag}_dh", dgu, w_gu, form="nt", out_dtype=F32, tm=tm, tn=tn, tk=tk, sizes=(t, d, 2 * f),
                 a_map=lambda i, j, kk: (kk // nkf, i, kk % nkf))
    tm, tn, tk = _mm_tiles(d, f, t)
    nf = f // tn
    dw_gu = _matmul(f"{tag}_dw_gate_up", h, dgu, form="tn", out_dtype=BF16, tm=tm, tn=tn, tk=tk,
                    sizes=(d, 2 * f, t), b_map=lambda i, j, kk: (j // nf, kk, j % nf))
    return dh, dw_gu, dw_down


def _lower_bound(lbp):
    m = jnp.max(lbp, axis=0, keepdims=True)
    e = jnp.exp(lbp - m)
    return e[0:1] / jnp.sum(e, axis=0, keepdims=True)


def _chunk_mask(reverse):
    row = lax.broadcasted_iota(jnp.int32, (CHUNK, CHUNK), 0)
    col = lax.broadcasted_iota(jnp.int32, (CHUNK, CHUNK), 1)
    return (col >= row) if reverse else (col <= row)


def _hgrn_gates(z, lb, mask_bf):
    sig = _sigmoid(z)
    f = lb + (1.0 - lb) * sig
    logf = jnp.log(f)
    k = 1.0 - f
    cum = _dot_exact(mask_bf, logf)
    last = jnp.sum(logf, axis=0, keepdims=True)
    return sig, f, k, cum, last


def _hgrn_scan_fwd(name, p, lbp_f, lbp_b):
    t = p.shape[0]
    hw = lbp_f.shape[1]
    nh, nc = hw // HEAD, t // CHUNK

    def body(qf, vf, zf, qb, vb, zb, lbf, lbb, of_ref, ob_ref, stf_ref, stb_ref, state):
        n = pl.program_id(0)

        @pl.when(n == 0)
        def _():
            state[...] = jnp.zeros_like(state)

        directions = [(qf, vf, zf, lbf, of_ref, stf_ref), (qb, vb, zb, lbb, ob_ref, stb_ref)]
        for d, (q_ref, v_ref, z_ref, lb_ref, o_ref, st_ref) in enumerate(directions):
            mask = _chunk_mask(d == 1)
            lb = _lower_bound(lb_ref[...])
            _, _, k, cum, last = _hgrn_gates(z_ref[...], lb, mask.astype(BF16))
            v = v_ref[...].astype(BF16)
            qd = (q_ref[...] * jnp.exp(cum)).astype(BF16)
            kd = (k * jnp.exp(-cum)).astype(BF16)
            kt = (k * jnp.exp(last - cum)).astype(BF16)
            dec = jnp.exp(last)
            s_all = state[d]
            st_ref[...] = s_all
            for h in range(nh):
                sl = slice(h * HEAD, (h + 1) * HEAD)
                s_in = s_all[:, sl]
                a = jnp.where(mask, _dot(qd[:, sl], kd[:, sl], 1, 1), 0.0).astype(BF16)
                o_ref[:, sl] = _dot(a, v[:, sl]) + _dot(qd[:, sl], s_in.astype(BF16), 1, 1)
                state[d, :, sl] = s_in * dec[:, sl] + _dot(v[:, sl], kt[:, sl], 0, 0)

    def col(group, reverse):
        return pl.BlockSpec((CHUNK, hw), lambda n: ((nc - 1 - n) if reverse else n, group))

    def st(reverse):
        return pl.BlockSpec((None, HEAD, hw), lambda n: ((nc - 1 - n) if reverse else n, 0, 0))

    lb_spec = pl.BlockSpec((2, hw), lambda n: (0, 0))
    out = jax.ShapeDtypeStruct((t, hw), F32)
    states = jax.ShapeDtypeStruct((nc, HEAD, hw), F32)
    return pl.pallas_call(
        body, name=name, grid=(nc,),
        in_specs=[col(0, False), col(1, False), col(2, False), col(0, True), col(1, True), col(3, True),
                  lb_spec, lb_spec],
        out_specs=[col(0, False), col(0, True), st(False), st(True)],
        out_shape=[out, out, states, states],
        scratch_shapes=[pltpu.VMEM((2, HEAD, hw), F32)],
        compiler_params=_params(("arbitrary",), 12 * _nbytes((HEAD, hw), F32)),
    )(p, p, p, p, p, p, lbp_f, lbp_b)


def _hgrn_scan_bwd(name, p, lbp_f, lbp_b, do, st_f, st_b):
    t = p.shape[0]
    hw = lbp_f.shape[1]
    nh, nc = hw // HEAD, t // CHUNK

    def body(qf, vf, zf, dof, sf, qb, vb, zb, dob, sb, lbf, lbb, dqf, dvf, dzf, dlbf, dqb, dvb, dzb, dlbb,
             dstate, dlb_acc, dqd_s, dkd_s, dkt_s, ddec_s):
        n = pl.program_id(0)

        @pl.when(n == 0)
        def _():
            dstate[...] = jnp.zeros_like(dstate)
            dlb_acc[...] = jnp.zeros_like(dlb_acc)

        directions = [(qf, vf, zf, dof, sf, lbf, dqf, dvf, dzf, dlbf), (qb, vb, zb, dob, sb, lbb, dqb, dvb, dzb, dlbb)]
        for d, (q_ref, v_ref, z_ref, do_ref, st_ref, lb_ref, dq_ref, dv_ref, dz_ref, dlb_ref) in enumerate(directions):
            mask = _chunk_mask(d == 1)
            mask_bf = mask.astype(BF16)
            lb = _lower_bound(lb_ref[...])
            sig, f, k, cum, last = _hgrn_gates(z_ref[...], lb, mask_bf)
            e_pos, e_neg, e_tail = jnp.exp(cum), jnp.exp(-cum), jnp.exp(last - cum)
            dec = jnp.exp(last)
            v = v_ref[...].astype(BF16)
            qd, kd, kt = q_ref[...] * e_pos, k * e_neg, k * e_tail
            qd_bf, kd_bf, kt_bf = qd.astype(BF16), kd.astype(BF16), kt.astype(BF16)
            s_all = st_ref[...]
            ds_all = dstate[d]
            dov = do_ref[...].astype(BF16)
            for h in range(nh):
                sl = slice(h * HEAD, (h + 1) * HEAD)
                s_in, ds_out = s_all[:, sl], ds_all[:, sl]
                ds_bf = ds_out.astype(BF16)
                a = jnp.where(mask, _dot(qd_bf[:, sl], kd_bf[:, sl], 1, 1), 0.0).astype(BF16)
                da = jnp.where(mask, _dot(dov[:, sl], v[:, sl], 1, 1), 0.0).astype(BF16)
                dv_ref[:, sl] = _dot(a, dov[:, sl], 0, 0) + _dot(kt_bf[:, sl], ds_bf, 1, 1)
                dqd_s[:, sl] = _dot(da, kd_bf[:, sl]) + _dot(dov[:, sl], s_in.astype(BF16))
                dkd_s[:, sl] = _dot(da, qd_bf[:, sl], 0, 0)
                dkt_s[:, sl] = _dot(v[:, sl], ds_bf)
                dstate[d, :, sl] = _dot(dov[:, sl], qd_bf[:, sl], 0, 0) + ds_out * dec[:, sl]
                ddec_s[:, sl] = jnp.sum(ds_out * s_in, axis=0, keepdims=True)
            dqd, dkd, dkt = dqd_s[...], dkd_s[...], dkt_s[...]
            dlast = jnp.sum(dkt * kt, axis=0, keepdims=True) + dec * ddec_s[...]
            dq_ref[...] = dqd * e_pos
            dk = dkd * e_neg + dkt * e_tail
            dcum = dqd * qd - dkd * kd - dkt * kt
            dlogf = _dot_exact(mask_bf, dcum, 0, 0) + dlast
            df = dlogf / f - dk
            dz_ref[...] = df * (1.0 - lb) * sig * (1.0 - sig)
            dlb_acc[d] += jnp.sum(df * (1.0 - sig), axis=0, keepdims=True)

            @pl.when(n == nc - 1)
            def _():
                g = dlb_acc[d] * lb * (1.0 - lb)
                dlb_ref[0:1, :] = g
                dlb_ref[1:2, :] = -g

    def col(group, reverse):
        return pl.BlockSpec((CHUNK, hw), lambda n: (n if reverse else (nc - 1 - n), group))

    def st(reverse):
        return pl.BlockSpec((None, HEAD, hw), lambda n: (n if reverse else (nc - 1 - n), 0, 0))

    lb_spec = pl.BlockSpec((2, hw), lambda n: (0, 0))
    out = jax.ShapeDtypeStruct((t, hw), F32)
    dlb = jax.ShapeDtypeStruct((2, hw), F32)
    wide = pltpu.VMEM((CHUNK, hw), F32)
    return pl.pallas_call(
        body, name=name, grid=(nc,),
        in_specs=[col(0, False), col(1, False), col(2, False), col(0, False), st(False),
                  col(0, True), col(1, True), col(3, True), col(0, True), st(True), lb_spec, lb_spec],
        out_specs=[col(0, False), col(0, False), col(0, False), lb_spec,
                   col(0, True), col(0, True), col(0, True), lb_spec],
        out_shape=[out, out, out, dlb, out, out, out, dlb],
        scratch_shapes=[pltpu.VMEM((2, HEAD, hw), F32), pltpu.VMEM((2, 1, hw), F32), wide, wide, wide,
                        pltpu.VMEM((1, hw), F32)],
        compiler_params=_params(("arbitrary",), 16 * _nbytes((HEAD, hw), F32)),
    )(p, p, p, do, st_f, p, p, p, do, st_b, lbp_f, lbp_b)


def _hgrn_out_fwd(name, o_f, o_b, p, gain, g_group):
    t, hw = o_f.shape
    nh = hw // HEAD
    tm = _tile(t, 512, 8)

    def body(of_ref, ob_ref, g_ref, gain_ref, y_ref):
        o = of_ref[...] + ob_ref[...]
        g = g_ref[...]
        y_ref[...] = (o * _rms(o) * gain_ref[...] * (g * _sigmoid(g))).astype(BF16)

    blk = pl.BlockSpec((tm, HEAD), lambda i, h: (i, h))
    return pl.pallas_call(
        body, name=name, grid=(t // tm, nh),
        in_specs=[blk, blk, pl.BlockSpec((tm, HEAD), lambda i, h: (i, g_group * nh + h)),
                  pl.BlockSpec((1, HEAD), lambda i, h: (0, h))],
        out_specs=blk, out_shape=jax.ShapeDtypeStruct((t, hw), BF16),
        compiler_params=_params(("parallel", "parallel"), 1 << 20),
    )(o_f, o_b, p, gain)


def _hgrn_out_bwd(name, dy, o_f, o_b, p, gain, g_group):
    t, hw = o_f.shape
    nh = hw // HEAD
    tm = _tile(t, 512, 8)

    def body(dy_ref, of_ref, ob_ref, g_ref, gain_ref, do_ref, dg_ref, dgain_ref):
        i = pl.program_id(1)
        o = of_ref[...] + ob_ref[...]
        g = g_ref[...]
        gain_v = gain_ref[...]
        sig = _sigmoid(g)
        dyv = dy_ref[...]
        do, dgain = _norm_bwd(dyv * (g * sig), o, gain_v)
        do_ref[...] = do
        dg_ref[...] = dyv * (o * _rms(o) * gain_v) * sig * (1.0 + g * (1.0 - sig))
        _accumulate(dgain_ref, dgain, i == 0)

    blk = pl.BlockSpec((tm, HEAD), lambda h, i: (i, h))
    vec = pl.BlockSpec((1, HEAD), lambda h, i: (0, h))
    out = jax.ShapeDtypeStruct((t, hw), F32)
    return pl.pallas_call(
        body, name=name, grid=(nh, t // tm),
        in_specs=[blk, blk, blk, pl.BlockSpec((tm, HEAD), lambda h, i: (i, g_group * nh + h)), vec],
        out_specs=[blk, blk, vec], out_shape=[out, out, jax.ShapeDtypeStruct((1, hw), F32)],
        compiler_params=_params(("parallel", "arbitrary"), 1 << 20),
    )(dy, o_f, o_b, p, gain)


def _t5_bucket_ids():
    c = np.arange(WINDOW)[:, None]
    s = np.arange(SPAN)[None, :]
    rel = s - WINDOW - c
    nb = REL_BUCKETS // 2
    max_exact = nb // 2
    bucket = (rel > 0).astype(np.int32) * nb
    n = np.abs(rel)
    large = max_exact + (np.log(np.maximum(n, 1) / max_exact) / np.log(REL_MAX_DIST / max_exact)
                         * (nb - max_exact)).astype(np.int32)
    large = np.minimum(large, nb - 1)
    ids = bucket + np.where(n < max_exact, n, large).astype(np.int32)
    return jnp.asarray(ids.reshape(1, WINDOW * SPAN), jnp.int32)


def _bias_onehot(ids_ref):
    n = ids_ref.shape[1]
    return (lax.broadcasted_iota(jnp.int32, (REL_BUCKETS, n), 0) == ids_ref[...]).astype(BF16)


def _bias_gather(name, table_t, ids):
    nh = table_t.shape[0]

    def body(t_ref, ids_ref, o_ref):
        o_ref[...] = _dot_exact(t_ref[...], _bias_onehot(ids_ref), split="a")

    return pl.pallas_call(
        body, name=name, out_shape=jax.ShapeDtypeStruct((nh, ids.shape[1]), F32),
        compiler_params=pltpu.CompilerParams(vmem_limit_bytes=32 << 20),
    )(table_t, ids)


def _bias_scatter(name, dbias, ids):
    nh = dbias.shape[0]

    def body(d_ref, ids_ref, o_ref):
        o_ref[...] = _dot_exact(d_ref[...], _bias_onehot(ids_ref), 1, 1, split="a")

    return pl.pallas_call(
        body, name=name, out_shape=jax.ShapeDtypeStruct((nh, REL_BUCKETS), F32),
        compiler_params=pltpu.CompilerParams(vmem_limit_bytes=32 << 20),
    )(dbias, ids)


def _attn_valid(i, t):
    c = lax.broadcasted_iota(jnp.int32, (WINDOW, SPAN), 0)
    s = lax.broadcasted_iota(jnp.int32, (WINDOW, SPAN), 1)
    rel = s - WINDOW - c
    pos = i * WINDOW - WINDOW + s
    return (jnp.abs(rel) <= WINDOW) & (pos >= 0) & (pos < t)


def _attn_probs(qh, kh, bias_h, sink_h, valid):
    s = _dot(qh, kh, 1, 1) / math.sqrt(HEAD)
    s = jnp.where(valid, s + bias_h, NEG_INF)
    m = jnp.maximum(jnp.max(s, axis=-1, keepdims=True), sink_h)
    e = jnp.exp(s - m)
    es = jnp.exp(sink_h - m)
    inv = 1.0 / (jnp.sum(e, axis=-1, keepdims=True) + es)
    return e * inv, es * inv


def _attn_fwd(name, p, k_pad, v_pad, bias, sink, q_group_blk):
    t = p.shape[0]
    nh = bias.shape[0]
    aw = nh * HEAD
    grp = nh // KV_HEADS
    nb = t // WINDOW

    def body(q_ref, k_ref, v_ref, b_ref, s_ref, y_ref):
        i = pl.program_id(0)
        valid = _attn_valid(i, t)
        start = pl.multiple_of(i * WINDOW, WINDOW)
        ks = k_ref[pl.ds(start, SPAN), :]
        vs = v_ref[pl.ds(start, SPAN), :]
        for h in range(nh):
            kv = h // grp
            qh = q_ref[:, h * HEAD:(h + 1) * HEAD].astype(BF16)
            pr, _ = _attn_probs(qh, ks[:, kv * HEAD:(kv + 1) * HEAD], b_ref[h], s_ref[0:1, h:h + 1], valid)
            y_ref[:, h * HEAD:(h + 1) * HEAD] = _dot(pr.astype(BF16), vs[:, kv * HEAD:(kv + 1) * HEAD]).astype(BF16)

    full = lambda a: pl.BlockSpec(a.shape, lambda i: (0,) * a.ndim)
    return pl.pallas_call(
        body, name=name, grid=(nb,),
        in_specs=[pl.BlockSpec((WINDOW, aw), lambda i: (i, q_group_blk)), full(k_pad), full(v_pad), full(bias),
                  full(sink)],
        out_specs=pl.BlockSpec((WINDOW, aw), lambda i: (i, 0)),
        out_shape=jax.ShapeDtypeStruct((t, aw), BF16),
        compiler_params=_params(("parallel",), _nbytes(k_pad.shape, BF16) * 2 + _nbytes(bias.shape, F32)),
    )(p, k_pad, v_pad, bias, sink)


def _attn_bwd(name, p, k_pad, v_pad, bias, sink, dy, q_group_blk, dy_blk):
    t = p.shape[0]
    nh = bias.shape[0]
    aw = nh * HEAD
    grp = nh // KV_HEADS
    nb = t // WINDOW
    kvw = k_pad.shape[1]

    def body(q_ref, k_ref, v_ref, b_ref, s_ref, dy_ref, dq_ref, dk_ref, dv_ref, db_ref, ds_ref):
        i = pl.program_id(0)

        @pl.when(i == 0)
        def _():
            dk_ref[...] = jnp.zeros_like(dk_ref)
            dv_ref[...] = jnp.zeros_like(dv_ref)
            db_ref[...] = jnp.zeros_like(db_ref)
            ds_ref[...] = jnp.zeros_like(ds_ref)

        valid = _attn_valid(i, t)
        start = pl.multiple_of(i * WINDOW, WINDOW)
        ks = k_ref[pl.ds(start, SPAN), :]
        vs = v_ref[pl.ds(start, SPAN), :]
        inv_sqrt = 1.0 / math.sqrt(HEAD)
        for kv in range(KV_HEADS):
            kh = ks[:, kv * HEAD:(kv + 1) * HEAD]
            vh = vs[:, kv * HEAD:(kv + 1) * HEAD]
            dk_acc = jnp.zeros((SPAN, HEAD), F32)
            dv_acc = jnp.zeros((SPAN, HEAD), F32)
            for h in range(kv * grp, (kv + 1) * grp):
                qh = q_ref[:, h * HEAD:(h + 1) * HEAD].astype(BF16)
                pr, ps = _attn_probs(qh, kh, b_ref[h], s_ref[0:1, h:h + 1], valid)
                doh = dy_ref[:, h * HEAD:(h + 1) * HEAD].astype(BF16)
                dp = _dot(doh, vh, 1, 1)
                delta = jnp.sum(pr * dp, axis=-1, keepdims=True)
                dsc = pr * (dp - delta)
                db_ref[h] += dsc
                ds_ref[h:h + 1, :] += jnp.broadcast_to(jnp.sum(-ps * delta, axis=0, keepdims=True), (1, 128))
                dsr = (dsc * inv_sqrt).astype(BF16)
                dq_ref[:, h * HEAD:(h + 1) * HEAD] = _dot(dsr, kh)
                dk_acc += _dot(dsr, qh, 0, 0)
                dv_acc += _dot(pr.astype(BF16), doh, 0, 0)
            dk_ref[pl.ds(start, SPAN), kv * HEAD:(kv + 1) * HEAD] += dk_acc
            dv_ref[pl.ds(start, SPAN), kv * HEAD:(kv + 1) * HEAD] += dv_acc

    full = lambda a: pl.BlockSpec(a.shape, lambda i: (0,) * a.ndim)
    whole = lambda shape: pl.BlockSpec(shape, lambda i: (0,) * len(shape))
    pad_shape = (t + 2 * WINDOW, kvw)
    return pl.pallas_call(
        body, name=name, grid=(nb,),
        in_specs=[pl.BlockSpec((WINDOW, aw), lambda i: (i, q_group_blk)), full(k_pad), full(v_pad), full(bias),
                  full(sink), pl.BlockSpec((WINDOW, aw), lambda i: (i, dy_blk))],
        out_specs=[pl.BlockSpec((WINDOW, aw), lambda i: (i, 0)), whole(pad_shape), whole(pad_shape),
                   whole(bias.shape), whole((nh, 128))],
        out_shape=[jax.ShapeDtypeStruct((t, aw), F32), jax.ShapeDtypeStruct(pad_shape, F32),
                   jax.ShapeDtypeStruct(pad_shape, F32), jax.ShapeDtypeStruct(bias.shape, F32),
                   jax.ShapeDtypeStruct((nh, 128), F32)],
        compiler_params=_params(("arbitrary",), 3 * _nbytes(pad_shape, F32) + 2 * _nbytes(bias.shape, F32)),
    )(p, k_pad, v_pad, bias, sink, dy)


def _pad_kv(name, p, kv_blk, kvw):
    t = p.shape[0]
    nb = t // WINDOW

    def body(x_ref, o_ref):
        i = pl.program_id(0)
        inside = jnp.logical_and(i >= 1, i <= nb)
        o_ref[...] = jnp.where(inside, x_ref[...], 0.0).astype(BF16)

    return pl.pallas_call(
        body, name=name, grid=(nb + 2,),
        in_specs=[pl.BlockSpec((WINDOW, kvw), lambda i: (jnp.clip(i - 1, 0, nb - 1), kv_blk))],
        out_specs=pl.BlockSpec((WINDOW, kvw), lambda i: (i, 0)),
        out_shape=jax.ShapeDtypeStruct((t + 2 * WINDOW, kvw), BF16),
        compiler_params=_params(("parallel",), 1 << 20),
    )(p)


def _mix_dproj(name, pieces, kv_pads, t):
    hw = pieces[0][0].shape[1]
    kvw = kv_pads[0].shape[1]
    widths = [hw] * len(pieces) + [kvw] * len(kv_pads)
    total = sum(widths)
    tm = WINDOW
    flat = [a for pc in pieces for a in pc]

    def body(*refs):
        o_ref = refs[-1]
        pos, off = 0, 0
        for pc in pieces:
            val = refs[pos][...]
            for extra in range(1, len(pc)):
                val = val + refs[pos + extra][...]
            o_ref[:, off:off + hw] = val.astype(BF16)
            pos += len(pc)
            off += hw
        for _ in kv_pads:
            o_ref[:, off:off + kvw] = refs[pos][...].astype(BF16)
            pos += 1
            off += kvw

    in_specs = [pl.BlockSpec((tm, hw), lambda i: (i, 0)) for _ in flat]
    in_specs += [pl.BlockSpec((tm, kvw), lambda i: (i + 1, 0)) for _ in kv_pads]
    return pl.pallas_call(
        body, name=name, grid=(t // tm,), in_specs=in_specs,
        out_specs=pl.BlockSpec((tm, total), lambda i: (i, 0)),
        out_shape=jax.ShapeDtypeStruct((t, total), BF16),
        compiler_params=_params(("parallel",), 3 * _nbytes((tm, total), F32)),
    )(*flat, *kv_pads)


def _concat_cols(name, a, b):
    t, wa = a.shape
    wb = b.shape[1]
    tm = _tile(t, 512, 16)

    def body(a_ref, b_ref, o_ref):
        o_ref[:, :wa] = a_ref[...]
        o_ref[:, wa:] = b_ref[...]

    return pl.pallas_call(
        body, name=name, grid=(t // tm,),
        in_specs=[pl.BlockSpec((tm, wa), lambda i: (i, 0)), pl.BlockSpec((tm, wb), lambda i: (i, 0))],
        out_specs=pl.BlockSpec((tm, wa + wb), lambda i: (i, 0)),
        out_shape=jax.ShapeDtypeStruct((t, wa + wb), a.dtype),
        compiler_params=_params(("parallel",), 2 * _nbytes((tm, wa + wb), a.dtype)),
    )(a, b)


def _cast_into_full(name, w, geom, idx):
    r, c = w.shape
    tr = _tile(r, 256, 16)
    nr = r // tr
    if geom.col:
        place = lambda i, iref: (i, iref[0])
    else:
        place = lambda i, iref: (iref[0] * nr + i, 0)

    def body(i_ref, w_ref, o_ref):
        o_ref[...] = w_ref[...].astype(BF16)

    return pl.pallas_call(
        body, name=name,
        grid_spec=pltpu.PrefetchScalarGridSpec(
            num_scalar_prefetch=1, grid=(nr,),
            in_specs=[pl.BlockSpec((tr, c), lambda i, iref: (i, 0))],
            out_specs=pl.BlockSpec((tr, c), place)),
        out_shape=jax.ShapeDtypeStruct(geom.full_shape, BF16),
        compiler_params=_params(("parallel",), 2 * _nbytes((tr, c), F32)),
    )(idx, w)


def _adamw(name, w, g, m, v):
    r, c = w.shape
    tr = _tile(r, 128, 8)
    bc1 = 1.0 - ADAM_B1 ** ADAM_STEP
    bc2 = 1.0 - ADAM_B2 ** ADAM_STEP

    def body(w_ref, g_ref, m_ref, v_ref, d_ref, nm_ref, nv_ref):
        gv = g_ref[...]
        nm = ADAM_B1 * m_ref[...] + (1.0 - ADAM_B1) * gv
        nv = ADAM_B2 * v_ref[...] + (1.0 - ADAM_B2) * (gv * gv)
        nm_ref[...] = nm
        nv_ref[...] = nv
        d_ref[...] = -ADAM_LR * ((nm / bc1) / (jnp.sqrt(nv / bc2) + ADAM_EPS) + ADAM_WD * w_ref[...])

    blk = pl.BlockSpec((tr, c), lambda i: (i, 0))
    out = jax.ShapeDtypeStruct((r, c), F32)
    return pl.pallas_call(
        body, name=name, grid=(r // tr,), in_specs=[blk] * 4, out_specs=[blk] * 3, out_shape=[out] * 3,
        compiler_params=_params(("parallel",), 7 * _nbytes((tr, c), F32)),
    )(w, g, m, v)


def _mesh_pos():
    return lax.axis_index("x"), lax.axis_index("y"), lax.axis_index("c")


def _other_chips(x, y):
    return [(1 - x, y), (x, 1 - y), (1 - x, 1 - y)]


class _Big:
    def __init__(self, shard_shape, col_sharded):
        self.col = col_sharded
        r, c = shard_shape
        self.shard_shape = (r, c)
        self.full_shape = (r, N_CHIPS * c) if col_sharded else (N_CHIPS * r, c)
        self.half_shape = (r // 2, N_CHIPS * c) if col_sharded else (N_CHIPS * r, c // 2)
        self.shard_half_shape = (r // 2, c) if col_sharded else (r, c // 2)

    def region(self, ref, s, half=None):
        r, c = self.shard_shape
        if self.col:
            rows = slice(None) if half is None else pl.ds(half * (r // 2), r // 2)
            return ref.at[rows, pl.ds(s * c, c)]
        cols = slice(None) if half is None else pl.ds(half * (c // 2), c // 2)
        return ref.at[pl.ds(s * r, r), cols]

    def half_of_full(self, ref, half):
        r, c = self.full_shape
        if self.col:
            return ref.at[pl.ds(half * (r // 2), r // 2), :]
        return ref.at[:, pl.ds(half * (c // 2), c // 2)]

    def half_of_shard(self, ref, half):
        r, c = self.shard_shape
        if self.col:
            return ref.at[pl.ds(half * (r // 2), r // 2), :]
        return ref.at[:, pl.ds(half * (c // 2), c // 2)]

    def shard_of_half(self, ref, s):
        r, c = self.shard_shape
        if self.col:
            return ref.at[:, pl.ds(s * c, c)]
        return ref.at[pl.ds(s * r, r), :]


def _gather_weights(fulls, geoms):
    nw = len(fulls)

    def body(*refs):
        dst = refs[nw:2 * nw]
        send_sems, recv_sems = refs[2 * nw:]
        x, y, c = _mesh_pos()
        chips = _other_chips(x, y)
        mine = 2 * x + y
        sibling = (x, y, 1 - c)

        def remote(w, k, src_ref, dst_ref, to):
            return pltpu.make_async_remote_copy(src_ref=src_ref, dst_ref=dst_ref, send_sem=send_sems.at[w, k],
                                                recv_sem=recv_sems.at[w, k], device_id=to, device_id_type=MESH)

        sends = []
        for w in range(nw):
            own_half = geoms[w].region(dst[w], mine, c)
            for k, chip in enumerate(chips):
                sends.append(remote(w, k, own_half, own_half, (*chip, c)))
        for cp in sends:
            cp.start()
        passed = []
        for w in range(nw):
            g = geoms[w]
            for k, chip in enumerate(chips):
                landed = g.region(dst[w], 2 * chip[0] + chip[1], c)
                remote(w, k, landed, landed, (*chip, c)).wait_recv()
                fwd = remote(w, 3 + k, landed, landed, sibling)
                fwd.start()
                passed.append(fwd)
        for w in range(nw):
            g = geoms[w]
            for k, chip in enumerate(chips):
                landed = g.region(dst[w], 2 * chip[0] + chip[1], 1 - c)
                remote(w, 3 + k, landed, landed, sibling).wait_recv()
        for cp in sends + passed:
            cp.wait_send()

    return pl.pallas_call(
        body, name="gather_weights", in_specs=[ANY] * nw, out_specs=[ANY] * nw,
        out_shape=[jax.ShapeDtypeStruct(g.full_shape, BF16) for g in geoms],
        input_output_aliases={w: w for w in range(nw)},
        scratch_shapes=[pltpu.SemaphoreType.DMA((nw, 6)), pltpu.SemaphoreType.DMA((nw, 6))],
    )(*fulls)


def _pair_exchange(grads, geoms):
    nw = len(grads)

    def body(*refs):
        src, dst = refs[:nw], refs[nw:2 * nw]
        send_sems, recv_sems = refs[2 * nw:]
        x, y, c = _mesh_pos()
        copies = [pltpu.make_async_remote_copy(
            src_ref=geoms[w].half_of_full(src[w], 1 - c), dst_ref=dst[w], send_sem=send_sems.at[w],
            recv_sem=recv_sems.at[w], device_id=(x, y, 1 - c), device_id_type=MESH) for w in range(nw)]
        for cp in copies:
            cp.start()
        for cp in copies:
            cp.wait()

    return pl.pallas_call(
        body, name="grads_pair_exchange", in_specs=[ANY] * nw, out_specs=[ANY] * nw,
        out_shape=[jax.ShapeDtypeStruct(g.half_shape, BF16) for g in geoms],
        scratch_shapes=[pltpu.SemaphoreType.DMA((nw,)), pltpu.SemaphoreType.DMA((nw,))],
    )(*grads)


def _chip_exchange(halves, geoms):
    nw = len(halves)

    def body(*refs):
        src, dst = refs[:nw], refs[nw:2 * nw]
        send_sems, recv_sems = refs[2 * nw:]
        x, y, c = _mesh_pos()
        copies = []
        for w in range(nw):
            for k, chip in enumerate(_other_chips(x, y)):
                copies.append(pltpu.make_async_remote_copy(
                    src_ref=geoms[w].shard_of_half(src[w], 2 * chip[0] + chip[1]), dst_ref=dst[w].at[k],
                    send_sem=send_sems.at[w, k], recv_sem=recv_sems.at[w, k], device_id=(*chip, c),
                    device_id_type=MESH))
        for cp in copies:
            cp.start()
        for cp in copies:
            cp.wait()

    return pl.pallas_call(
        body, name="grads_chip_exchange", in_specs=[ANY] * nw, out_specs=[ANY] * nw,
        out_shape=[jax.ShapeDtypeStruct((3,) + g.shard_half_shape, BF16) for g in geoms],
        scratch_shapes=[pltpu.SemaphoreType.DMA((nw, 3)), pltpu.SemaphoreType.DMA((nw, 3))],
    )(*halves)


def _pair_share(quarters, geoms):
    nw = len(quarters)

    def body(*refs):
        dst = refs[nw:2 * nw]
        send_sems, recv_sems = refs[2 * nw:]
        x, y, c = _mesh_pos()
        sends = []
        for w in range(nw):
            own_half = geoms[w].half_of_shard(dst[w], c)
            sends.append(pltpu.make_async_remote_copy(
                src_ref=own_half, dst_ref=own_half, send_sem=send_sems.at[w], recv_sem=recv_sems.at[w],
                device_id=(x, y, 1 - c), device_id_type=MESH))
        for cp in sends:
            cp.start()
        for w in range(nw):
            theirs = geoms[w].half_of_shard(dst[w], 1 - c)
            pltpu.make_async_remote_copy(src_ref=theirs, dst_ref=theirs, send_sem=send_sems.at[w],
                                         recv_sem=recv_sems.at[w], device_id=(x, y, 1 - c),
                                         device_id_type=MESH).wait_recv()
        for cp in sends:
            cp.wait_send()

    return pl.pallas_call(
        body, name="grads_pair_share", in_specs=[ANY] * nw, out_specs=[ANY] * nw,
        out_shape=[jax.ShapeDtypeStruct(g.shard_shape, F32) for g in geoms],
        input_output_aliases={w: w for w in range(nw)},
        scratch_shapes=[pltpu.SemaphoreType.DMA((nw,)), pltpu.SemaphoreType.DMA((nw,))],
    )(*quarters)


def _pair_add(name, grad, recv, geom, c_idx):
    r, c = geom.half_shape
    tr, tc = _tile(r, 256, 16), _tile(c, 2048, 128)
    nr, ncol = r // tr, c // tc
    if geom.col:
        mine = lambda i, j, cref: (cref[0] * nr + i, j)
    else:
        mine = lambda i, j, cref: (i, cref[0] * ncol + j)

    def body(c_ref, g_ref, r_ref, o_ref):
        o_ref[...] = (g_ref[...].astype(F32) + r_ref[...].astype(F32)).astype(BF16)

    return pl.pallas_call(
        body, name=name,
        grid_spec=pltpu.PrefetchScalarGridSpec(
            num_scalar_prefetch=1, grid=(nr, ncol),
            in_specs=[pl.BlockSpec((tr, tc), mine), pl.BlockSpec((tr, tc), lambda i, j, cref: (i, j))],
            out_specs=pl.BlockSpec((tr, tc), lambda i, j, cref: (i, j))),
        out_shape=jax.ShapeDtypeStruct((r, c), BF16),
        compiler_params=_params(("parallel", "parallel"), 3 * _nbytes((tr, tc), F32)),
    )(c_idx, grad, recv)


def _chip_add(name, half, recv, geom, idx):
    r, c = geom.shard_half_shape
    tr, tc = _tile(r, 256, 16), _tile(c, 2048, 128)
    nr, ncol = r // tr, c // tc
    if geom.col:
        mine = lambda i, j, iref: (i, iref[0] * ncol + j)
        place = lambda i, j, iref: (iref[1] * nr + i, j)
    else:
        mine = lambda i, j, iref: (iref[0] * nr + i, j)
        place = lambda i, j, iref: (i, iref[1] * ncol + j)

    def body(i_ref, h_ref, r_ref, o_ref):
        acc = h_ref[...].astype(F32)
        for k in range(3):
            acc = acc + r_ref[k].astype(F32)
        o_ref[...] = acc

    return pl.pallas_call(
        body, name=name,
        grid_spec=pltpu.PrefetchScalarGridSpec(
            num_scalar_prefetch=1, grid=(nr, ncol),
            in_specs=[pl.BlockSpec((tr, tc), mine), pl.BlockSpec((3, tr, tc), lambda i, j, iref: (0, i, j))],
            out_specs=pl.BlockSpec((tr, tc), place)),
        out_shape=jax.ShapeDtypeStruct(geom.shard_shape, F32),
        compiler_params=_params(("parallel", "parallel"), 4 * _nbytes((tr, tc), F32)),
    )(idx, half, recv)


def _all_reduce_small(pack):
    r, d = pack.shape

    def body(p_ref, o_ref, slots, send_sems, recv_sems):
        x, y, c = _mesh_pos()
        me = 4 * x + 2 * y + c
        slots[me] = p_ref[...]
        copies = []
        for k in range(1, N_DEV):
            px, py, pc = x ^ ((k >> 2) & 1), y ^ ((k >> 1) & 1), c ^ (k & 1)
            copies.append(pltpu.make_async_remote_copy(
                src_ref=p_ref, dst_ref=slots.at[me], send_sem=send_sems.at[k - 1], recv_sem=recv_sems.at[k - 1],
                device_id=(px, py, pc), device_id_type=MESH))
        for cp in copies:
            cp.start()
        for k in range(1, N_DEV):
            peer = 4 * (x ^ ((k >> 2) & 1)) + 2 * (y ^ ((k >> 1) & 1)) + (c ^ (k & 1))
            pltpu.make_async_remote_copy(
                src_ref=p_ref, dst_ref=slots.at[peer], send_sem=send_sems.at[k - 1], recv_sem=recv_sems.at[k - 1],
                device_id=(x, y, c), device_id_type=MESH).wait_recv()
        for cp in copies:
            cp.wait_send()
        acc = slots[0]
        for k in range(1, N_DEV):
            acc = acc + slots[k]
        o_ref[...] = acc

    vm = pl.BlockSpec(memory_space=pltpu.VMEM)
    return pl.pallas_call(
        body, name="all_reduce_small", in_specs=[vm], out_specs=vm,
        out_shape=jax.ShapeDtypeStruct((r, d), F32),
        scratch_shapes=[pltpu.VMEM((N_DEV, r, d), F32), pltpu.SemaphoreType.DMA((N_DEV - 1,)),
                        pltpu.SemaphoreType.DMA((N_DEV - 1,))],
    )(pack)


def _pack_rows(rows, d):
    out = []
    for a in rows:
        flat = a.reshape(-1)
        n = -(-flat.shape[0] // d) * d
        out.append(jnp.pad(flat, (0, n - flat.shape[0])).reshape(-1, d))
    packed = jnp.concatenate(out, axis=0)
    return jnp.pad(packed, ((0, 16 - packed.shape[0]), (0, 0)))


def _unpack_rows(packed, shapes, d):
    out, row = [], 0
    for shp in shapes:
        n = int(np.prod(shp))
        nrows = -(-n // d)
        out.append(packed[row:row + nrows].reshape(-1)[:n].reshape(shp))
        row += nrows
    return out


def kernel(x, pre_norm_ffn1, post_norm_ffn1, w_ffn1_gate_up, w_ffn1_down, pre_norm_mix, post_norm_mix, w_mix_in, hgrn_lower_bounds_fwd, hgrn_lower_bounds_bwd, hgrn_out_norm, attn_sink, w_mix_out, pre_norm_ffn2, post_norm_ffn2, w_ffn2_gate_up, w_ffn2_down, rel_bias_table, loss_target, m_pre_norm_ffn1, m_post_norm_ffn1, m_w_ffn1_gate_up, m_w_ffn1_down, m_pre_norm_mix, m_post_norm_mix, m_w_mix_in, m_hgrn_lower_bounds_fwd, m_hgrn_lower_bounds_bwd, m_hgrn_out_norm, m_attn_sink, m_w_mix_out, m_pre_norm_ffn2, m_post_norm_ffn2, m_w_ffn2_gate_up, m_w_ffn2_down, m_rel_bias_table, v_pre_norm_ffn1, v_post_norm_ffn1, v_w_ffn1_gate_up, v_w_ffn1_down, v_pre_norm_mix, v_post_norm_mix, v_w_mix_in, v_hgrn_lower_bounds_fwd, v_hgrn_lower_bounds_bwd, v_hgrn_out_norm, v_attn_sink, v_w_mix_out, v_pre_norm_ffn2, v_post_norm_ffn2, v_w_ffn2_gate_up, v_w_ffn2_down, v_rel_bias_table):
    t, d = x.shape[1], x.shape[2]
    hw = hgrn_out_norm.shape[1]
    aw = d - hw
    nah = aw // HEAD
    kvw = KV_HEADS * HEAD
    x0 = x[0]
    target = loss_target[0]

    big_names = ["w_ffn1_gate_up", "w_ffn1_down", "w_mix_in", "w_mix_out", "w_ffn2_gate_up", "w_ffn2_down"]
    big_w = [w_ffn1_gate_up[0], w_ffn1_down[0], w_mix_in[0], w_mix_out[0], w_ffn2_gate_up[0], w_ffn2_down[0]]
    big_m = [m_w_ffn1_gate_up[0], m_w_ffn1_down[0], m_w_mix_in[0], m_w_mix_out[0], m_w_ffn2_gate_up[0],
             m_w_ffn2_down[0]]
    big_v = [v_w_ffn1_gate_up[0], v_w_ffn1_down[0], v_w_mix_in[0], v_w_mix_out[0], v_w_ffn2_gate_up[0],
             v_w_ffn2_down[0]]
    col_sharded = [True, False, True, False, True, False]
    geoms = [_Big(w.shape, cs) for w, cs in zip(big_w, col_sharded)]

    cx, cy, cc = _mesh_pos()
    idx = jnp.stack([2 * cx + cy, cc]).astype(jnp.int32)
    c_idx = jnp.reshape(cc, (1,)).astype(jnp.int32)
    own_quarters = [_cast_into_full(f"cast_{n}", w, gm, idx) for n, w, gm in zip(big_names, big_w, geoms)]
    w_gu1, w_d1, w_in, w_out, w_gu2, w_d2 = _gather_weights(own_quarters, geoms)

    h1 = _norm_fwd("ffn1_pre_norm", x0, pre_norm_ffn1)
    gu1, act1, ff1 = _ffn_fwd("ffn1", h1, w_gu1, w_d1)
    x1, hm = _resid_norm_fwd("ffn1_residual", x0, ff1, post_norm_ffn1, pre_norm_mix, 0.5)
    p = _mm("mix_in", hm, w_in, "nn", F32)
    o_f, o_b, st_f, st_b = _hgrn_scan_fwd("hgrn_scan", p, hgrn_lower_bounds_fwd, hgrn_lower_bounds_bwd)
    y_h = _hgrn_out_fwd("hgrn_out", o_f, o_b, p, hgrn_out_norm, 4)
    kv_blk0 = (5 * hw + aw) // kvw
    k_pad = _pad_kv("attn_pad_k", p, kv_blk0, kvw)
    v_pad = _pad_kv("attn_pad_v", p, kv_blk0 + 1, kvw)
    bucket_ids = _t5_bucket_ids()
    bias = _bias_gather("attn_bias", rel_bias_table.T, bucket_ids).reshape(nah, WINDOW, SPAN)
    y_a = _attn_fwd("attn_fwd", p, k_pad, v_pad, bias, attn_sink, 5 * hw // aw)
    y_mix = _concat_cols("mix_concat", y_h, y_a)
    mixed = _mm("mix_out", y_mix, w_out, "nn", F32)
    x2, h2 = _resid_norm_fwd("mix_residual", x1, mixed, post_norm_mix, pre_norm_ffn2, 1.0)
    gu2, act2, ff2 = _ffn_fwd("ffn2", h2, w_gu2, w_d2)
    loss_blk, dy, dff2, dg_post2 = _final_fwd_bwd("ffn2_residual_loss", x2, ff2, post_norm_ffn2, target, 0.5)

    dh2, dw_gu2, dw_d2 = _ffn_bwd("ffn2", dff2, h2, gu2, act2, w_gu2, w_d2)
    dx2, dg_pre2, dmixed, dg_postm = _norms_bwd("mix_residual_bwd", dy, dh2, x2, pre_norm_ffn2,
                                                post=(mixed, post_norm_mix, 1.0))
    dy_mix = _mm("mix_out_dx", dmixed, w_out, "nt", F32)
    dw_out = _mm("mix_out_dw", y_mix, dmixed, "tn", BF16)
    dq_a, dk_pad, dv_pad, dbias, dsink = _attn_bwd("attn_bwd", p, k_pad, v_pad, bias, attn_sink, dy_mix,
                                                   5 * hw // aw, hw // aw)
    drel_t = _bias_scatter("attn_dbias", dbias.reshape(nah, WINDOW * SPAN), bucket_ids)
    do, dg_h, dgain = _hgrn_out_bwd("hgrn_out_bwd", dy_mix, o_f, o_b, p, hgrn_out_norm, 4)
    dq_f, dv_f, dz_f, dlb_f, dq_b, dv_b, dz_b, dlb_b = _hgrn_scan_bwd(
        "hgrn_scan_bwd", p, hgrn_lower_bounds_fwd, hgrn_lower_bounds_bwd, do, st_f, st_b)
    dp = _mix_dproj("mix_dproj", [(dq_f, dq_b), (dv_f, dv_b), (dz_f,), (dz_b,), (dg_h,), (dq_a,)],
                    [dk_pad, dv_pad], t)
    dhm = _mm("mix_in_dx", dp, w_in, "nt", F32)
    dw_in = _mm("mix_in_dw", hm, dp, "tn", BF16)
    dx1, dg_prem, dff1, dg_post1 = _norms_bwd("ffn1_residual_bwd", dx2, dhm, x1, pre_norm_mix,
                                              post=(ff1, post_norm_ffn1, 0.5))
    dh1, dw_gu1, dw_d1 = _ffn_bwd("ffn1", dff1, h1, gu1, act1, w_gu1, w_d1)
    grad_x, dg_pre1 = _norms_bwd("ffn1_pre_norm_bwd", dx1, dh1, x0, pre_norm_ffn1)

    dws = [dw_gu1, dw_d1, dw_in, dw_out, dw_gu2, dw_d2]
    from_sibling = _pair_exchange(dws, geoms)
    pair_sums = [_pair_add(f"pair_add_{n}", g, r, gm, c_idx) for n, g, r, gm in zip(big_names, dws, from_sibling, geoms)]
    from_chips = _chip_exchange(pair_sums, geoms)
    reduced = [_chip_add(f"chip_add_{n}", h, r, gm, idx) for n, h, r, gm in zip(big_names, pair_sums, from_chips, geoms)]
    big_grads = _pair_share(reduced, geoms)

    small_w = [pre_norm_ffn1, post_norm_ffn1, pre_norm_mix, post_norm_mix, hgrn_lower_bounds_fwd,
               hgrn_lower_bounds_bwd, hgrn_out_norm, attn_sink, pre_norm_ffn2, post_norm_ffn2, rel_bias_table]
    small_m = [m_pre_norm_ffn1, m_post_norm_ffn1, m_pre_norm_mix, m_post_norm_mix, m_hgrn_lower_bounds_fwd,
               m_hgrn_lower_bounds_bwd, m_hgrn_out_norm, m_attn_sink, m_pre_norm_ffn2, m_post_norm_ffn2,
               m_rel_bias_table]
    small_v = [v_pre_norm_ffn1, v_post_norm_ffn1, v_pre_norm_mix, v_post_norm_mix, v_hgrn_lower_bounds_fwd,
               v_hgrn_lower_bounds_bwd, v_hgrn_out_norm, v_attn_sink, v_pre_norm_ffn2, v_post_norm_ffn2,
               v_rel_bias_table]
    small_g = [dg_pre1, dg_post1, dg_prem, dg_postm, dlb_f, dlb_b, dgain, dsink[:, 0].reshape(1, nah), dg_pre2,
               dg_post2, drel_t.T]
    shapes = [a.shape for a in small_w]
    summed = _all_reduce_small(_pack_rows(small_g + [loss_blk[0:1, 0:1]], d))
    g_pack = summed
    loss =_unpack_rows(summed, shapes + [(1, 1)], d)[-1][0, 0]
    sd, sm, sv = _adamw("adamw_small", _pack_rows(small_w, d), g_pack, _pack_rows(small_m, d), _pack_rows(small_v, d))
    small_grads = _unpack_rows(g_pack, shapes, d)
    small_delta, small_new_m, small_new_v = (_unpack_rows(a, shapes, d) for a in (sd, sm, sv))

    big_delta, big_new_m, big_new_v = [], [], []
    for n, w, g, m, v in zip(big_names, big_w, big_grads, big_m, big_v):
        dl, nm, nv = _adamw(f"adamw_{n}", w, g, m, v)
        big_delta.append(dl[None])
        big_new_m.append(nm[None])
        big_new_v.append(nv[None])
    big_grads = [g[None] for g in big_grads]

    def ordered(small, big):
        s = dict(zip(["pre1", "post1", "prem", "postm", "lbf", "lbb", "gain", "sink", "pre2", "post2", "rel"], small))
        b = dict(zip(["gu1", "d1", "win", "wout", "gu2", "d2"], big))
        return [s["pre1"], s["post1"], b["gu1"], b["d1"], s["prem"], s["postm"], b["win"], s["lbf"], s["lbb"],
                s["gain"], s["sink"], b["wout"], s["pre2"], s["post2"], b["gu2"], b["d2"], s["rel"]]

    return (loss, grad_x[None], *ordered(small_grads, big_grads), *ordered(small_delta, big_delta),
            *ordered(small_new_m, big_new_m), *ordered(small_new_v, big_new_v))
```

```python
import functools
import math

import jax
import jax.numpy as jnp
import numpy as np
from jax import lax
from jax.experimental import pallas as pl
from jax.experimental.pallas import tpu as pltpu

F32 = jnp.float32
BF16 = jnp.bfloat16

HEAD = 128
CHUNK = 64
WINDOW = 128
SPAN = 3 * WINDOW
KV_HEADS = 2
REL_BUCKETS = 32
REL_MAX_DIST = 128
EPS = 1e-6
NEG_INF = -1e30

ADAM_LR = 0.001
ADAM_B1 = 0.9
ADAM_B2 = 0.999
ADAM_EPS = 1e-08
ADAM_WD = 0.01
ADAM_STEP = 10

N_CHIPS = 4
N_DEV = 8
V7X_VMEM_BYTES = 64 * 1024 * 1024
MESH = pl.DeviceIdType.MESH
ANY = pl.BlockSpec(memory_space=pl.ANY)


def _tile(n, pref, mult):
    t = (min(pref, n) // mult) * mult
    while t >= mult:
        if n % t == 0:
            return t
        t -= mult
    return n


def _params(semantics, block_bytes):
    limit = min(V7X_VMEM_BYTES - (4 << 20), 2 * int(block_bytes) + (8 << 20))
    return pltpu.CompilerParams(dimension_semantics=semantics, vmem_limit_bytes=limit)


def _nbytes(shape, dtype):
    return int(np.prod(shape)) * jnp.dtype(dtype).itemsize


def _dot(a, b, ca=1, cb=0):
    return lax.dot_general(a, b, (((ca,), (cb,)), ((), ())), preferred_element_type=F32)


def _split3(x):
    hi = x.astype(BF16)
    r1 = x - hi.astype(F32)
    mid = r1.astype(BF16)
    lo = (r1 - mid.astype(F32)).astype(BF16)
    return hi, mid, lo


def _dot_exact(a, b, ca=1, cb=0, split="b"):
    if split == "b":
        return sum(_dot(a, p, ca, cb) for p in _split3(b))
    return sum(_dot(p, b, ca, cb) for p in _split3(a))


def _rms(x):
    return lax.rsqrt(jnp.mean(x * x, axis=-1, keepdims=True) + EPS)


def _norm_bwd(u, x, gain):
    r = _rms(x)
    xhat = x * r
    dgain = jnp.sum(u * xhat, axis=0, keepdims=True)
    v = u * gain
    dx = r * (v - xhat * jnp.mean(v * xhat, axis=-1, keepdims=True))
    return dx, dgain


def _sigmoid(x):
    return 1.0 / (1.0 + jnp.exp(-x))


def _accumulate(ref, val, first):
    @pl.when(first)
    def _():
        ref[...] = val

    @pl.when(jnp.logical_not(first))
    def _():
        ref[...] += val


def _matmul(name, a, b, *, form, out_dtype, tm, tn, tk, a_map=None, b_map=None,
            out_shape=None, out_block=None, out_map=None, sizes=None):
    if sizes is None:
        if form == "nn":
            (m, k), n = a.shape, b.shape[1]
        elif form == "nt":
            (m, k), n = a.shape, b.shape[0]
        else:
            (k, m), n = a.shape, b.shape[1]
    else:
        m, n, k = sizes
    gi, gj, gk = m // tm, n // tn, k // tk
    a_blk = (tm, tk) if form != "tn" else (tk, tm)
    b_blk = (tk, tn) if form != "nt" else (tn, tk)
    if a_map is None:
        a_map = (lambda i, j, kk: (i, kk)) if form != "tn" else (lambda i, j, kk: (kk, i))
    else:
        a_blk = (None,) + a_blk
    if b_map is None:
        b_map = (lambda i, j, kk: (kk, j)) if form != "nt" else (lambda i, j, kk: (j, kk))
    else:
        b_blk = (None,) + b_blk
    if out_shape is None:
        out_shape, out_block, out_map = (m, n), (tm, tn), (lambda i, j, kk: (i, j))
    ca, cb = {"nn": (1, 0), "nt": (1, 1), "tn": (0, 0)}[form]

    def body(a_ref, b_ref, o_ref, *acc):
        part = _dot(a_ref[...], b_ref[...], ca, cb)
        if gk == 1:
            o_ref[...] = part.astype(o_ref.dtype)
        else:
            kk = pl.program_id(2)
            _accumulate(acc[0], part, kk == 0)

            @pl.when(kk == gk - 1)
            def _():
                o_ref[...] = acc[0][...].astype(o_ref.dtype)

    scratch = [] if gk == 1 else [pltpu.VMEM((tm, tn), F32)]
    vmem = (_nbytes((tm, tk), a.dtype) + _nbytes((tk, tn), b.dtype) + _nbytes((tm, tn), out_dtype)
            + 2 * _nbytes((tm, tn), F32))
    return pl.pallas_call(
        body, name=name, grid=(gi, gj, gk),
        in_specs=[pl.BlockSpec(a_blk, a_map), pl.BlockSpec(b_blk, b_map)],
        out_specs=pl.BlockSpec(out_block, out_map),
        out_shape=jax.ShapeDtypeStruct(out_shape, out_dtype),
        scratch_shapes=scratch,
        compiler_params=_params(("parallel", "parallel", "arbitrary"), vmem),
    )(a, b)


def _mm_tiles(m, n, k):
    return _tile(m, 1024, 128), _tile(n, 512, 128), _tile(k, 2816, 128)


def _mm(name, a, b, form, out_dtype):
    if form == "nn":
        m, k, n = a.shape[0], a.shape[1], b.shape[1]
    elif form == "nt":
        m, k, n = a.shape[0], a.shape[1], b.shape[0]
    else:
        m, k, n = a.shape[1], a.shape[0], b.shape[1]
    tm, tn, tk = _mm_tiles(m, n, k)
    return _matmul(name, a, b, form=form, out_dtype=out_dtype, tm=tm, tn=tn, tk=tk)


def _row_tile(t):
    return _tile(t, 256, 8)


def _norm_fwd(name, x, gain):
    t, d = x.shape
    tm = _row_tile(t)

    def body(x_ref, g_ref, h_ref):
        xv = x_ref[...]
        h_ref[...] = (xv * _rms(xv) * g_ref[...]).astype(BF16)

    row = pl.BlockSpec((tm, d), lambda i: (i, 0))
    vec = pl.BlockSpec((1, d), lambda i: (0, 0))
    return pl.pallas_call(
        body, name=name, grid=(t // tm,), in_specs=[row, vec], out_specs=row,
        out_shape=jax.ShapeDtypeStruct((t, d), BF16),
        compiler_params=_params(("parallel",), 2 * _nbytes((tm, d), F32)),
    )(x, gain)


def _resid_norm_fwd(name, xres, ff, gpost, gpre, scale):
    t, d = xres.shape
    tm = _row_tile(t)

    def body(x_ref, f_ref, gp_ref, gn_ref, xn_ref, h_ref):
        f = f_ref[...]
        xn = x_ref[...] + scale * (f * _rms(f) * gp_ref[...])
        xn_ref[...] = xn
        h_ref[...] = (xn * _rms(xn) * gn_ref[...]).astype(BF16)

    row = pl.BlockSpec((tm, d), lambda i: (i, 0))
    vec = pl.BlockSpec((1, d), lambda i: (0, 0))
    return pl.pallas_call(
        body, name=name, grid=(t // tm,), in_specs=[row, row, vec, vec], out_specs=[row, row],
        out_shape=[jax.ShapeDtypeStruct((t, d), F32), jax.ShapeDtypeStruct((t, d), BF16)],
        compiler_params=_params(("parallel",), 4 * _nbytes((tm, d), F32)),
    )(xres, ff, gpost, gpre)


def _final_fwd_bwd(name, xres, ff, gpost, target, scale):
    t, d = xres.shape
    tm = _row_tile(t)

    def body(x_ref, f_ref, gp_ref, t_ref, loss_ref, dy_ref, dff_ref, dg_ref):
        i = pl.program_id(0)
        f = f_ref[...]
        gp = gp_ref[...]
        y = x_ref[...] + scale * (f * _rms(f) * gp)
        err = y - t_ref[...]
        part = 0.5 * jnp.sum(jnp.mean(err * err, axis=-1, keepdims=True), axis=0, keepdims=True)
        _accumulate(loss_ref, jnp.broadcast_to(part, loss_ref.shape), i == 0)
        dy = err / d
        dy_ref[...] = dy
        dff, dg = _norm_bwd(scale * dy, f, gp)
        dff_ref[...] = dff.astype(BF16)
        _accumulate(dg_ref, dg, i == 0)

    row = pl.BlockSpec((tm, d), lambda i: (i, 0))
    vec = pl.BlockSpec((1, d), lambda i: (0, 0))
    return pl.pallas_call(
        body, name=name, grid=(t // tm,), in_specs=[row, row, vec, row],
        out_specs=[pl.BlockSpec((8, 128), lambda i: (0, 0)), row, row, vec],
        out_shape=[jax.ShapeDtypeStruct((8, 128), F32), jax.ShapeDtypeStruct((t, d), F32),
                   jax.ShapeDtypeStruct((t, d), BF16), jax.ShapeDtypeStruct((1, d), F32)],
        compiler_params=_params(("arbitrary",), 5 * _nbytes((tm, d), F32)),
    )(xres, ff, gpost, target)


def _norms_bwd(name, dres, dh, xin, gpre, post=None):
    t, d = dres.shape
    tm = _row_tile(t)
    with_post = post is not None

    def body(*refs):
        if with_post:
            dr_ref, dh_ref, x_ref, g_ref, f_ref, gp_ref, dx_ref, dg_ref, dff_ref, dgp_ref = refs
        else:
            dr_ref, dh_ref, x_ref, g_ref, dx_ref, dg_ref = refs
        i = pl.program_id(0)
        dx, dg = _norm_bwd(dh_ref[...], x_ref[...], g_ref[...])
        dx = dr_ref[...] + dx
        dx_ref[...] = dx
        _accumulate(dg_ref, dg, i == 0)
        if with_post:
            dff, dgp = _norm_bwd(post[2] * dx, f_ref[...], gp_ref[...])
            dff_ref[...] = dff.astype(BF16)
            _accumulate(dgp_ref, dgp, i == 0)

    row = pl.BlockSpec((tm, d), lambda i: (i, 0))
    vec = pl.BlockSpec((1, d), lambda i: (0, 0))
    ins, in_specs = [dres, dh, xin, gpre], [row, row, row, vec]
    out_specs = [row, vec]
    out_shape = [jax.ShapeDtypeStruct((t, d), F32), jax.ShapeDtypeStruct((1, d), F32)]
    if with_post:
        ins += [post[0], post[1]]
        in_specs += [row, vec]
        out_specs += [row, vec]
        out_shape += [jax.ShapeDtypeStruct((t, d), BF16), jax.ShapeDtypeStruct((1, d), F32)]
    return pl.pallas_call(
        body, name=name, grid=(t // tm,), in_specs=in_specs, out_specs=out_specs, out_shape=out_shape,
        compiler_params=_params(("arbitrary",), 6 * _nbytes((tm, d), F32)),
    )(*ins)


def _swiglu_fwd(name, gu):
    t, f2 = gu.shape
    f = f2 // 2
    tm, tf = _tile(t, 512, 8), _tile(f, 512, 128)
    nf = f // tf

    def body(g_ref, u_ref, a_ref):
        g = g_ref[...].astype(F32)
        a_ref[...] = (g * _sigmoid(g) * u_ref[...].astype(F32)).astype(BF16)

    return pl.pallas_call(
        body, name=name, grid=(t // tm, nf),
        in_specs=[pl.BlockSpec((tm, tf), lambda i, j: (i, j)), pl.BlockSpec((tm, tf), lambda i, j: (i, j + nf))],
        out_specs=pl.BlockSpec((tm, tf), lambda i, j: (i, j)),
        out_shape=jax.ShapeDtypeStruct((t, f), BF16),
        compiler_params=_params(("parallel", "parallel"), 3 * _nbytes((tm, tf), F32)),
    )(gu, gu)


def _swiglu_bwd(name, da, gu):
    t, f = da.shape
    tm, tf = _tile(t, 512, 8), _tile(f, 512, 128)
    nf = f // tf

    def body(da_ref, g_ref, u_ref, o_ref):
        g = g_ref[...].astype(F32)
        u = u_ref[...].astype(F32)
        dav = da_ref[...]
        sig = _sigmoid(g)
        o_ref[0] = (dav * u * sig * (1.0 + g * (1.0 - sig))).astype(BF16)
        o_ref[1] = (dav * g * sig).astype(BF16)

    return pl.pallas_call(
        body, name=name, grid=(t // tm, nf),
        in_specs=[pl.BlockSpec((tm, tf), lambda i, j: (i, j)), pl.BlockSpec((tm, tf), lambda i, j: (i, j)),
                  pl.BlockSpec((tm, tf), lambda i, j: (i, j + nf))],
        out_specs=pl.BlockSpec((2, tm, tf), lambda i, j: (0, i, j)),
        out_shape=jax.ShapeDtypeStruct((2, t, f), BF16),
        compiler_params=_params(("parallel", "parallel"), 5 * _nbytes((tm, tf), F32)),
    )(da, gu, gu)


def _ffn_fwd(tag, h, w_gu, w_down):
    gu = _mm(f"{tag}_gate_up", h, w_gu, "nn", BF16)
    act = _swiglu_fwd(f"{tag}_act", gu)
    ff = _mm(f"{tag}_down", act, w_down, "nn", F32)
    return gu, act, ff


def _ffn_bwd(tag, dff, h, gu, act, w_gu, w_down):
    t, d = h.shape
    f = act.shape[1]
    da = _mm(f"{tag}_dact", dff, w_down, "nt", F32)
    dw_down = _mm(f"{tag}_dw_down", act, dff, "tn", BF16)
    dgu = _swiglu_bwd(f"{tag}_dact_bwd", da, gu)
    tm, tn, tk = _mm_tiles(t, d, f)
    nkf = f // tk
    dh = _matmul(f"{tag}_dh", dgu, w_gu, form="nt", out_dtype=F32, tm=tm, tn=tn, tk=tk, sizes=(t, d, 2 * f),
                 a_map=lambda i, j, kk: (kk // nkf, i, kk % nkf))
    tm, tn, tk = _mm_tiles(d, f, t)
    nf = f // tn
    dw_gu = _matmul(f"{tag}_dw_gate_up", h, dgu, form="tn", out_dtype=BF16, tm=tm, tn=tn, tk=tk,
                    sizes=(d, 2 * f, t), b_map=lambda i, j, kk: (j // nf, kk, j % nf))
    return dh, dw_gu, dw_down


def _lower_bound(lbp):
    m = jnp.max(lbp, axis=0, keepdims=True)
    e = jnp.exp(lbp - m)
    return e[0:1] / jnp.sum(e, axis=0, keepdims=True)


def _chunk_mask(reverse):
    row = lax.broadcasted_iota(jnp.int32, (CHUNK, CHUNK), 0)
    col = lax.broadcasted_iota(jnp.int32, (CHUNK, CHUNK), 1)
    return (col >= row) if reverse else (col <= row)


def _hgrn_gates(z, lb, mask_bf):
    sig = _sigmoid(z)
    f = lb + (1.0 - lb) * sig
    logf = jnp.log(f)
    k = 1.0 - f
    cum = _dot_exact(mask_bf, logf)
    last = jnp.sum(logf, axis=0, keepdims=True)
    return sig, f, k, cum, last


def _hgrn_scan_fwd(name, p, lbp_f, lbp_b):
    t = p.shape[0]
    hw = lbp_f.shape[1]
    nh, nc = hw // HEAD, t // CHUNK

    def body(qf, vf, zf, qb, vb, zb, lbf, lbb, of_ref, ob_ref, stf_ref, stb_ref, state):
        n = pl.program_id(0)

        @pl.when(n == 0)
        def _():
            state[...] = jnp.zeros_like(state)

        directions = [(qf, vf, zf, lbf, of_ref, stf_ref), (qb, vb, zb, lbb, ob_ref, stb_ref)]
        for d, (q_ref, v_ref, z_ref, lb_ref, o_ref, st_ref) in enumerate(directions):
            mask = _chunk_mask(d == 1)
            lb = _lower_bound(lb_ref[...])
            _, _, k, cum, last = _hgrn_gates(z_ref[...], lb, mask.astype(BF16))
            v = v_ref[...].astype(BF16)
            qd = (q_ref[...] * jnp.exp(cum)).astype(BF16)
            kd = (k * jnp.exp(-cum)).astype(BF16)
            kt = (k * jnp.exp(last - cum)).astype(BF16)
            dec = jnp.exp(last)
            s_all = state[d]
            st_ref[...] = s_all
            for h in range(nh):
                sl = slice(h * HEAD, (h + 1) * HEAD)
                s_in = s_all[:, sl]
                a = jnp.where(mask, _dot(qd[:, sl], kd[:, sl], 1, 1), 0.0).astype(BF16)
                o_ref[:, sl] = _dot(a, v[:, sl]) + _dot(qd[:, sl], s_in.astype(BF16), 1, 1)
                state[d, :, sl] = s_in * dec[:, sl] + _dot(v[:, sl], kt[:, sl], 0, 0)

    def col(group, reverse):
        return pl.BlockSpec((CHUNK, hw), lambda n: ((nc - 1 - n) if reverse else n, group))

    def st(reverse):
        return pl.BlockSpec((None, HEAD, hw), lambda n: ((nc - 1 - n) if reverse else n, 0, 0))

    lb_spec = pl.BlockSpec((2, hw), lambda n: (0, 0))
    out = jax.ShapeDtypeStruct((t, hw), F32)
    states = jax.ShapeDtypeStruct((nc, HEAD, hw), F32)
    return pl.pallas_call(
        body, name=name, grid=(nc,),
        in_specs=[col(0, False), col(1, False), col(2, False), col(0, True), col(1, True), col(3, True),
                  lb_spec, lb_spec],
        out_specs=[col(0, False), col(0, True), st(False), st(True)],
        out_shape=[out, out, states, states],
        scratch_shapes=[pltpu.VMEM((2, HEAD, hw), F32)],
        compiler_params=_params(("arbitrary",), 12 * _nbytes((HEAD, hw), F32)),
    )(p, p, p, p, p, p, lbp_f, lbp_b)


def _hgrn_scan_bwd(name, p, lbp_f, lbp_b, do, st_f, st_b):
    t = p.shape[0]
    hw = lbp_f.shape[1]
    nh, nc = hw // HEAD, t // CHUNK

    def body(qf, vf, zf, dof, sf, qb, vb, zb, dob, sb, lbf, lbb, dqf, dvf, dzf, dlbf, dqb, dvb, dzb, dlbb,
             dstate, dlb_acc, dqd_s, dkd_s, dkt_s, ddec_s):
        n = pl.program_id(0)

        @pl.when(n == 0)
        def _():
            dstate[...] = jnp.zeros_like(dstate)
            dlb_acc[...] = jnp.zeros_like(dlb_acc)

        directions = [(qf, vf, zf, dof, sf, lbf, dqf, dvf, dzf, dlbf), (qb, vb, zb, dob, sb, lbb, dqb, dvb, dzb, dlbb)]
        for d, (q_ref, v_ref, z_ref, do_ref, st_ref, lb_ref, dq_ref, dv_ref, dz_ref, dlb_ref) in enumerate(directions):
            mask = _chunk_mask(d == 1)
            mask_bf = mask.astype(BF16)
            lb = _lower_bound(lb_ref[...])
            sig, f, k, cum, last = _hgrn_gates(z_ref[...], lb, mask_bf)
            e_pos, e_neg, e_tail = jnp.exp(cum), jnp.exp(-cum), jnp.exp(last - cum)
            dec = jnp.exp(last)
            v = v_ref[...].astype(BF16)
            qd, kd, kt = q_ref[...] * e_pos, k * e_neg, k * e_tail
            qd_bf, kd_bf, kt_bf = qd.astype(BF16), kd.astype(BF16), kt.astype(BF16)
            s_all = st_ref[...]
            ds_all = dstate[d]
            dov = do_ref[...].astype(BF16)
            for h in range(nh):
                sl = slice(h * HEAD, (h + 1) * HEAD)
                s_in, ds_out = s_all[:, sl], ds_all[:, sl]
                ds_bf = ds_out.astype(BF16)
                a = jnp.where(mask, _dot(qd_bf[:, sl], kd_bf[:, sl], 1, 1), 0.0).astype(BF16)
                da = jnp.where(mask, _dot(dov[:, sl], v[:, sl], 1, 1), 0.0).astype(BF16)
                dv_ref[:, sl] = _dot(a, dov[:, sl], 0, 0) + _dot(kt_bf[:, sl], ds_bf, 1, 1)
                dqd_s[:, sl] = _dot(da, kd_bf[:, sl]) + _dot(dov[:, sl], s_in.astype(BF16))
                dkd_s[:, sl] = _dot(da, qd_bf[:, sl], 0, 0)
                dkt_s[:, sl] = _dot(v[:, sl], ds_bf)
                dstate[d, :, sl] = _dot(dov[:, sl], qd_bf[:, sl], 0, 0) + ds_out * dec[:, sl]
                ddec_s[:, sl] = jnp.sum(ds_out * s_in, axis=0, keepdims=True)
            dqd, dkd, dkt = dqd_s[...], dkd_s[...], dkt_s[...]
            dlast = jnp.sum(dkt * kt, axis=0, keepdims=True) + dec * ddec_s[...]
            dq_ref[...] = dqd * e_pos
            dk = dkd * e_neg + dkt * e_tail
            dcum = dqd * qd - dkd * kd - dkt * kt
            dlogf = _dot_exact(mask_bf, dcum, 0, 0) + dlast
            df = dlogf / f - dk
            dz_ref[...] = df * (1.0 - lb) * sig * (1.0 - sig)
            dlb_acc[d] += jnp.sum(df * (1.0 - sig), axis=0, keepdims=True)

            @pl.when(n == nc - 1)
            def _():
                g = dlb_acc[d] * lb * (1.0 - lb)
                dlb_ref[0:1, :] = g
                dlb_ref[1:2, :] = -g

    def col(group, reverse):
        return pl.BlockSpec((CHUNK, hw), lambda n: (n if reverse else (nc - 1 - n), group))

    def st(reverse):
        return pl.BlockSpec((None, HEAD, hw), lambda n: (n if reverse else (nc - 1 - n), 0, 0))

    lb_spec = pl.BlockSpec((2, hw), lambda n: (0, 0))
    out = jax.ShapeDtypeStruct((t, hw), F32)
    dlb = jax.ShapeDtypeStruct((2, hw), F32)
    wide = pltpu.VMEM((CHUNK, hw), F32)
    return pl.pallas_call(
        body, name=name, grid=(nc,),
        in_specs=[col(0, False), col(1, False), col(2, False), col(0, False), st(False),
                  col(0, True), col(1, True), col(3, True), col(0, True), st(True), lb_spec, lb_spec],
        out_specs=[col(0, False), col(0, False), col(0, False), lb_spec,
                   col(0, True), col(0, True), col(0, True), lb_spec],
        out_shape=[out, out, out, dlb, out, out, out, dlb],
        scratch_shapes=[pltpu.VMEM((2, HEAD, hw), F32), pltpu.VMEM((2, 1, hw), F32), wide, wide, wide,
                        pltpu.VMEM((1, hw), F32)],
        compiler_params=_params(("arbitrary",), 16 * _nbytes((HEAD, hw), F32)),
    )(p, p, p, do, st_f, p, p, p, do, st_b, lbp_f, lbp_b)


def _hgrn_out_fwd(name, o_f, o_b, p, gain, g_group):
    t, hw = o_f.shape
    nh = hw // HEAD
    tm = _tile(t, 512, 8)

    def body(of_ref, ob_ref, g_ref, gain_ref, y_ref):
        o = of_ref[...] + ob_ref[...]
        g = g_ref[...]
        y_ref[...] = (o * _rms(o) * gain_ref[...] * (g * _sigmoid(g))).astype(BF16)

    blk = pl.BlockSpec((tm, HEAD), lambda i, h: (i, h))
    return pl.pallas_call(
        body, name=name, grid=(t // tm, nh),
        in_specs=[blk, blk, pl.BlockSpec((tm, HEAD), lambda i, h: (i, g_group * nh + h)),
                  pl.BlockSpec((1, HEAD), lambda i, h: (0, h))],
        out_specs=blk, out_shape=jax.ShapeDtypeStruct((t, hw), BF16),
        compiler_params=_params(("parallel", "parallel"), 1 << 20),
    )(o_f, o_b, p, gain)


def _hgrn_out_bwd(name, dy, o_f, o_b, p, gain, g_group):
    t, hw = o_f.shape
    nh = hw // HEAD
    tm = _tile(t, 512, 8)

    def body(dy_ref, of_ref, ob_ref, g_ref, gain_ref, do_ref, dg_ref, dgain_ref):
        i = pl.program_id(1)
        o = of_ref[...] + ob_ref[...]
        g = g_ref[...]
        gain_v = gain_ref[...]
        sig = _sigmoid(g)
        dyv = dy_ref[...]
        do, dgain = _norm_bwd(dyv * (g * sig), o, gain_v)
        do_ref[...] = do
        dg_ref[...] = dyv * (o * _rms(o) * gain_v) * sig * (1.0 + g * (1.0 - sig))
        _accumulate(dgain_ref, dgain, i == 0)

    blk = pl.BlockSpec((tm, HEAD), lambda h, i: (i, h))
    vec = pl.BlockSpec((1, HEAD), lambda h, i: (0, h))
    out = jax.ShapeDtypeStruct((t, hw), F32)
    return pl.pallas_call(
        body, name=name, grid=(nh, t // tm),
        in_specs=[blk, blk, blk, pl.BlockSpec((tm, HEAD), lambda h, i: (i, g_group * nh + h)), vec],
        out_specs=[blk, blk, vec], out_shape=[out, out, jax.ShapeDtypeStruct((1, hw), F32)],
        compiler_params=_params(("parallel", "arbitrary"), 1 << 20),
    )(dy, o_f, o_b, p, gain)


def _t5_bucket_ids():
    c = np.arange(WINDOW)[:, None]
    s = np.arange(SPAN)[None, :]
    rel = s - WINDOW - c
    nb = REL_BUCKETS // 2
    max_exact = nb // 2
    bucket = (rel > 0).astype(np.int32) * nb
    n = np.abs(rel)
    large = max_exact + (np.log(np.maximum(n, 1) / max_exact) / np.log(REL_MAX_DIST / max_exact)
                         * (nb - max_exact)).astype(np.int32)
    large = np.minimum(large, nb - 1)
    ids = bucket + np.where(n < max_exact, n, large).astype(np.int32)
    return jnp.asarray(ids.reshape(1, WINDOW * SPAN), jnp.int32)


def _bias_onehot(ids_ref):
    n = ids_ref.shape[1]
    return (lax.broadcasted_iota(jnp.int32, (REL_BUCKETS, n), 0) == ids_ref[...]).astype(BF16)


def _bias_gather(name, table_t, ids):
    nh = table_t.shape[0]

    def body(t_ref, ids_ref, o_ref):
        o_ref[...] = _dot_exact(t_ref[...], _bias_onehot(ids_ref), split="a")

    return pl.pallas_call(
        body, name=name, out_shape=jax.ShapeDtypeStruct((nh, ids.shape[1]), F32),
        compiler_params=pltpu.CompilerParams(vmem_limit_bytes=32 << 20),
    )(table_t, ids)


def _bias_scatter(name, dbias, ids):
    nh = dbias.shape[0]

    def body(d_ref, ids_ref, o_ref):
        o_ref[...] = _dot_exact(d_ref[...], _bias_onehot(ids_ref), 1, 1, split="a")

    return pl.pallas_call(
        body, name=name, out_shape=jax.ShapeDtypeStruct((nh, REL_BUCKETS), F32),
        compiler_params=pltpu.CompilerParams(vmem_limit_bytes=32 << 20),
    )(dbias, ids)


def _attn_valid(i, t):
    c = lax.broadcasted_iota(jnp.int32, (WINDOW, SPAN), 0)
    s = lax.broadcasted_iota(jnp.int32, (WINDOW, SPAN), 1)
    rel = s - WINDOW - c
    pos = i * WINDOW - WINDOW + s
    return (jnp.abs(rel) <= WINDOW) & (pos >= 0) & (pos < t)


def _attn_probs(qh, kh, bias_h, sink_h, valid):
    s = _dot(qh, kh, 1, 1) / math.sqrt(HEAD)
    s = jnp.where(valid, s + bias_h, NEG_INF)
    m = jnp.maximum(jnp.max(s, axis=-1, keepdims=True), sink_h)
    e = jnp.exp(s - m)
    es = jnp.exp(sink_h - m)
    inv = 1.0 / (jnp.sum(e, axis=-1, keepdims=True) + es)
    return e * inv, es * inv


def _attn_fwd(name, p, k_pad, v_pad, bias, sink, q_group_blk):
    t = p.shape[0]
    nh = bias.shape[0]
    aw = nh * HEAD
    grp = nh // KV_HEADS
    nb = t // WINDOW

    def body(q_ref, k_ref, v_ref, b_ref, s_ref, y_ref):
        i = pl.program_id(0)
        valid = _attn_valid(i, t)
        start = pl.multiple_of(i * WINDOW, WINDOW)
        ks = k_ref[pl.ds(start, SPAN), :]
        vs = v_ref[pl.ds(start, SPAN), :]
        for h in range(nh):
            kv = h // grp
            qh = q_ref[:, h * HEAD:(h + 1) * HEAD].astype(BF16)
            pr, _ = _attn_probs(qh, ks[:, kv * HEAD:(kv + 1) * HEAD], b_ref[h], s_ref[0:1, h:h + 1], valid)
            y_ref[:, h * HEAD:(h + 1) * HEAD] = _dot(pr.astype(BF16), vs[:, kv * HEAD:(kv + 1) * HEAD]).astype(BF16)

    full = lambda a: pl.BlockSpec(a.shape, lambda i: (0,) * a.ndim)
    return pl.pallas_call(
        body, name=name, grid=(nb,),
        in_specs=[pl.BlockSpec((WINDOW, aw), lambda i: (i, q_group_blk)), full(k_pad), full(v_pad), full(bias),
                  full(sink)],
        out_specs=pl.BlockSpec((WINDOW, aw), lambda i: (i, 0)),
        out_shape=jax.ShapeDtypeStruct((t, aw), BF16),
        compiler_params=_params(("parallel",), _nbytes(k_pad.shape, BF16) * 2 + _nbytes(bias.shape, F32)),
    )(p, k_pad, v_pad, bias, sink)


def _attn_bwd(name, p, k_pad, v_pad, bias, sink, dy, q_group_blk, dy_blk):
    t = p.shape[0]
    nh = bias.shape[0]
    aw = nh * HEAD
    grp = nh // KV_HEADS
    nb = t // WINDOW
    kvw = k_pad.shape[1]

    def body(q_ref, k_ref, v_ref, b_ref, s_ref, dy_ref, dq_ref, dk_ref, dv_ref, db_ref, ds_ref):
        i = pl.program_id(0)

        @pl.when(i == 0)
        def _():
            dk_ref[...] = jnp.zeros_like(dk_ref)
            dv_ref[...] = jnp.zeros_like(dv_ref)
            db_ref[...] = jnp.zeros_like(db_ref)
            ds_ref[...] = jnp.zeros_like(ds_ref)

        valid = _attn_valid(i, t)
        start = pl.multiple_of(i * WINDOW, WINDOW)
        ks = k_ref[pl.ds(start, SPAN), :]
        vs = v_ref[pl.ds(start, SPAN), :]
        inv_sqrt = 1.0 / math.sqrt(HEAD)
        for kv in range(KV_HEADS):
            kh = ks[:, kv * HEAD:(kv + 1) * HEAD]
            vh = vs[:, kv * HEAD:(kv + 1) * HEAD]
            dk_acc = jnp.zeros((SPAN, HEAD), F32)
            dv_acc = jnp.zeros((SPAN, HEAD), F32)
            for h in range(kv * grp, (kv + 1) * grp):
                qh = q_ref[:, h * HEAD:(h + 1) * HEAD].astype(BF16)
                pr, ps = _attn_probs(qh, kh, b_ref[h], s_ref[0:1, h:h + 1], valid)
                doh = dy_ref[:, h * HEAD:(h + 1) * HEAD].astype(BF16)
                dp = _dot(doh, vh, 1, 1)
                delta = jnp.sum(pr * dp, axis=-1, keepdims=True)
                dsc = pr * (dp - delta)
                db_ref[h] += dsc
                ds_ref[h:h + 1, :] += jnp.broadcast_to(jnp.sum(-ps * delta, axis=0, keepdims=True), (1, 128))
                dsr = (dsc * inv_sqrt).astype(BF16)
                dq_ref[:, h * HEAD:(h + 1) * HEAD] = _dot(dsr, kh)
                dk_acc += _dot(dsr, qh, 0, 0)
                dv_acc += _dot(pr.astype(BF16), doh, 0, 0)
            dk_ref[pl.ds(start, SPAN), kv * HEAD:(kv + 1) * HEAD] += dk_acc
            dv_ref[pl.ds(start, SPAN), kv * HEAD:(kv + 1) * HEAD] += dv_acc

    full = lambda a: pl.BlockSpec(a.shape, lambda i: (0,) * a.ndim)
    whole = lambda shape: pl.BlockSpec(shape, lambda i: (0,) * len(shape))
    pad_shape = (t + 2 * WINDOW, kvw)
    return pl.pallas_call(
        body, name=name, grid=(nb,),
        in_specs=[pl.BlockSpec((WINDOW, aw), lambda i: (i, q_group_blk)), full(k_pad), full(v_pad), full(bias),
                  full(sink), pl.BlockSpec((WINDOW, aw), lambda i: (i, dy_blk))],
        out_specs=[pl.BlockSpec((WINDOW, aw), lambda i: (i, 0)), whole(pad_shape), whole(pad_shape),
                   whole(bias.shape), whole((nh, 128))],
        out_shape=[jax.ShapeDtypeStruct((t, aw), F32), jax.ShapeDtypeStruct(pad_shape, F32),
                   jax.ShapeDtypeStruct(pad_shape, F32), jax.ShapeDtypeStruct(bias.shape, F32),
                   jax.ShapeDtypeStruct((nh, 128), F32)],
        compiler_params=_params(("arbitrary",), 3 * _nbytes(pad_shape, F32) + 2 * _nbytes(bias.shape, F32)),
    )(p, k_pad, v_pad, bias, sink, dy)


def _pad_kv(name, p, kv_blk, kvw):
    t = p.shape[0]
    nb = t // WINDOW

    def body(x_ref, o_ref):
        i = pl.program_id(0)
        inside = jnp.logical_and(i >= 1, i <= nb)
        o_ref[...] = jnp.where(inside, x_ref[...], 0.0).astype(BF16)

    return pl.pallas_call(
        body, name=name, grid=(nb + 2,),
        in_specs=[pl.BlockSpec((WINDOW, kvw), lambda i: (jnp.clip(i - 1, 0, nb - 1), kv_blk))],
        out_specs=pl.BlockSpec((WINDOW, kvw), lambda i: (i, 0)),
        out_shape=jax.ShapeDtypeStruct((t + 2 * WINDOW, kvw), BF16),
        compiler_params=_params(("parallel",), 1 << 20),
    )(p)


def _mix_dproj(name, pieces, kv_pads, t):
    hw = pieces[0][0].shape[1]
    kvw = kv_pads[0].shape[1]
    widths = [hw] * len(pieces) + [kvw] * len(kv_pads)
    total = sum(widths)
    tm = WINDOW
    flat = [a for pc in pieces for a in pc]

    def body(*refs):
        o_ref = refs[-1]
        pos, off = 0, 0
        for pc in pieces:
            val = refs[pos][...]
            for extra in range(1, len(pc)):
                val = val + refs[pos + extra][...]
            o_ref[:, off:off + hw] = val.astype(BF16)
            pos += len(pc)
            off += hw
        for _ in kv_pads:
            o_ref[:, off:off + kvw] = refs[pos][...].astype(BF16)
            pos += 1
            off += kvw

    in_specs = [pl.BlockSpec((tm, hw), lambda i: (i, 0)) for _ in flat]
    in_specs += [pl.BlockSpec((tm, kvw), lambda i: (i + 1, 0)) for _ in kv_pads]
    return pl.pallas_call(
        body, name=name, grid=(t // tm,), in_specs=in_specs,
        out_specs=pl.BlockSpec((tm, total), lambda i: (i, 0)),
        out_shape=jax.ShapeDtypeStruct((t, total), BF16),
        compiler_params=_params(("parallel",), 3 * _nbytes((tm, total), F32)),
    )(*flat, *kv_pads)


def _concat_cols(name, a, b):
    t, wa = a.shape
    wb = b.shape[1]
    tm = _tile(t, 512, 16)

    def body(a_ref, b_ref, o_ref):
        o_ref[:, :wa] = a_ref[...]
        o_ref[:, wa:] = b_ref[...]

    return pl.pallas_call(
        body, name=name, grid=(t // tm,),
        in_specs=[pl.BlockSpec((tm, wa), lambda i: (i, 0)), pl.BlockSpec((tm, wb), lambda i: (i, 0))],
        out_specs=pl.BlockSpec((tm, wa + wb), lambda i: (i, 0)),
        out_shape=jax.ShapeDtypeStruct((t, wa + wb), a.dtype),
        compiler_params=_params(("parallel",), 2 * _nbytes((tm, wa + wb), a.dtype)),
    )(a, b)


def _cast_into_full(name, w, geom, idx):
    r, c = w.shape
    tr = _tile(r, 256, 16)
    nr = r // tr
    if geom.col:
        place = lambda i, iref: (i, iref[0])
    else:
        place = lambda i, iref: (iref[0] * nr + i, 0)

    def body(i_ref, w_ref, o_ref):
        o_ref[...] = w_ref[...].astype(BF16)

    return pl.pallas_call(
        body, name=name,
        grid_spec=pltpu.PrefetchScalarGridSpec(
            num_scalar_prefetch=1, grid=(nr,),
            in_specs=[pl.BlockSpec((tr, c), lambda i, iref: (i, 0))],
            out_specs=pl.BlockSpec((tr, c), place)),
        out_shape=jax.ShapeDtypeStruct(geom.full_shape, BF16),
        compiler_params=_params(("parallel",), 2 * _nbytes((tr, c), F32)),
    )(idx, w)


def _adamw(name, w, g, m, v):
    r, c = w.shape
    tr = _tile(r, 128, 8)
    bc1 = 1.0 - ADAM_B1 ** ADAM_STEP
    bc2 = 1.0 - ADAM_B2 ** ADAM_STEP

    def body(w_ref, g_ref, m_ref, v_ref, d_ref, nm_ref, nv_ref):
        gv = g_ref[...]
        nm = ADAM_B1 * m_ref[...] + (1.0 - ADAM_B1) * gv
        nv = ADAM_B2 * v_ref[...] + (1.0 - ADAM_B2) * (gv * gv)
        nm_ref[...] = nm
        nv_ref[...] = nv
        d_ref[...] = -ADAM_LR * ((nm / bc1) / (jnp.sqrt(nv / bc2) + ADAM_EPS) + ADAM_WD * w_ref[...])

    blk = pl.BlockSpec((tr, c), lambda i: (i, 0))
    out = jax.ShapeDtypeStruct((r, c), F32)
    return pl.pallas_call(
        body, name=name, grid=(r // tr,), in_specs=[blk] * 4, out_specs=[blk] * 3, out_shape=[out] * 3,
        compiler_params=_params(("parallel",), 7 * _nbytes((tr, c), F32)),
    )(w, g, m, v)


def _mesh_pos():
    return lax.axis_index("x"), lax.axis_index("y"), lax.axis_index("c")


def _other_chips(x, y):
    return [(1 - x, y), (x, 1 - y), (1 - x, 1 - y)]


class _Big:
    def __init__(self, shard_shape, col_sharded):
        self.col = col_sharded
        r, c = shard_shape
        self.shard_shape = (r, c)
        self.full_shape = (r, N_CHIPS * c) if col_sharded else (N_CHIPS * r, c)
        self.half_shape = (r // 2, N_CHIPS * c) if col_sharded else (N_CHIPS * r, c // 2)
        self.shard_half_shape = (r // 2, c) if col_sharded else (r, c // 2)

    def region(self, ref, s, half=None):
        r, c = self.shard_shape
        if self.col:
            rows = slice(None) if half is None else pl.ds(half * (r // 2), r // 2)
            return ref.at[rows, pl.ds(s * c, c)]
        cols = slice(None) if half is None else pl.ds(half * (c // 2), c // 2)
        return ref.at[pl.ds(s * r, r), cols]

    def three_halves(self, ref, half):
        r, c = self.shard_shape
        if self.col:
            return ref.at[pl.ds(half * (r // 2), r // 2), pl.ds(0, 3 * c)]
        return ref.at[pl.ds(0, 3 * r), pl.ds(half * (c // 2), c // 2)]

    def half_of_full(self, ref, half):
        r, c = self.full_shape
        if self.col:
            return ref.at[pl.ds(half * (r // 2), r // 2), :]
        return ref.at[:, pl.ds(half * (c // 2), c // 2)]

    def half_of_shard(self, ref, half):
        r, c = self.shard_shape
        if self.col:
            return ref.at[pl.ds(half * (r // 2), r // 2), :]
        return ref.at[:, pl.ds(half * (c // 2), c // 2)]

    def shard_of_half(self, ref, s):
        r, c = self.shard_shape
        if self.col:
            return ref.at[:, pl.ds(s * c, c)]
        return ref.at[pl.ds(s * r, r), :]


def _gather_weights(fulls, geoms):
    nw = len(fulls)

    def body(*refs):
        dst = refs[nw:2 * nw]
        send_sems, recv_sems = refs[2 * nw:]
        x, y, c = _mesh_pos()
        chips = _other_chips(x, y)
        mine = 2 * x + y
        sibling = (x, y, 1 - c)

        def remote(w, k, src_ref, dst_ref, to):
            return pltpu.make_async_remote_copy(src_ref=src_ref, dst_ref=dst_ref, send_sem=send_sems.at[w, k],
                                                recv_sem=recv_sems.at[w, k], device_id=to, device_id_type=MESH)

        sends = []
        for w in range(nw):
            own_half = geoms[w].region(dst[w], mine, c)
            for k, chip in enumerate(chips):
                sends.append(remote(w, k, own_half, own_half, (*chip, c)))
        for cp in sends:
            cp.start()
        passed = []
        for w in range(nw):
            g = geoms[w]
            for k, chip in enumerate(chips):
                landed = g.region(dst[w], 2 * chip[0] + chip[1], c)
                remote(w, k, landed, landed, (*chip, c)).wait_recv()
                fwd = remote(w, 3 + k, landed, landed, sibling)
                fwd.start()
                passed.append(fwd)
        for w in range(nw):
            g = geoms[w]
            for k, chip in enumerate(chips):
                landed = g.region(dst[w], 2 * chip[0] + chip[1], 1 - c)
                remote(w, 3 + k, landed, landed, sibling).wait_recv()
        for cp in sends + passed:
            cp.wait_send()

    return pl.pallas_call(
        body, name="gather_weights", in_specs=[ANY] * nw, out_specs=[ANY] * nw,
        out_shape=[jax.ShapeDtypeStruct(g.full_shape, BF16) for g in geoms],
        input_output_aliases={w: w for w in range(nw)},
        scratch_shapes=[pltpu.SemaphoreType.DMA((nw, 6)), pltpu.SemaphoreType.DMA((nw, 6))],
    )(*fulls)


HBM = pl.BlockSpec(memory_space=pltpu.HBM)
SEM = pl.BlockSpec(memory_space=pltpu.SEMAPHORE)
SPLIT_COPY = pltpu.CompilerParams(has_side_effects=pltpu.SideEffectType.DATAFLOW_SIDE_EFFECTING)


def _in_hbm(a):
    return pltpu.with_memory_space_constraint(a, pltpu.HBM)


def _gather_start(fulls, geoms):
    nw = len(fulls)

    def body(*refs):
        dst = refs[nw:2 * nw]
        sems = refs[2 * nw:]
        x, y, c = _mesh_pos()
        mine = 2 * x + y
        for w in range(nw):
            own_half = geoms[w].region(dst[w], mine, c)
            for chip in _other_chips(x, y):
                pltpu.make_async_remote_copy(src_ref=own_half, dst_ref=own_half, send_sem=sems[2 * w],
                                             recv_sem=sems[2 * w + 1], device_id=(*chip, c),
                                             device_id_type=MESH).start()

    out = pl.pallas_call(
        body, name="gather_start", in_specs=[HBM] * nw, out_specs=[HBM] * nw + [SEM] * (2 * nw),
        out_shape=[pltpu.HBM(g.full_shape, BF16) for g in geoms] + [pltpu.SemaphoreType.DMA(())] * (2 * nw),
        input_output_aliases={w: w for w in range(nw)}, compiler_params=SPLIT_COPY,
    )(*[_in_hbm(a) for a in fulls])
    return out[:nw], [(out[nw + 2 * w], out[nw + 2 * w + 1]) for w in range(nw)]


def _wait_three(geom, ref, half, send_sem, recv_sem, peer, recv):
    three = geom.three_halves(ref, half)
    copy = pltpu.make_async_remote_copy(src_ref=three, dst_ref=three, send_sem=send_sem, recv_sem=recv_sem,
                                        device_id=peer, device_id_type=MESH)
    if recv:
        copy.wait_recv()
    else:
        copy.wait_send()


def _gather_forward(name, full, geom, sems, after):
    def body(w_in, send_sem, recv_sem, after_ref, w_ref, fwd_send, fwd_recv):
        x, y, c = _mesh_pos()
        sibling = (x, y, 1 - c)
        _wait_three(geom, w_ref, c, send_sem, recv_sem, sibling, recv=True)
        for chip in _other_chips(x, y):
            landed = geom.region(w_ref, 2 * chip[0] + chip[1], c)
            pltpu.make_async_remote_copy(src_ref=landed, dst_ref=landed, send_sem=fwd_send, recv_sem=fwd_recv,
                                         device_id=sibling, device_id_type=MESH).start()
        _wait_three(geom, w_ref, c, send_sem, recv_sem, sibling, recv=False)

    sem = pltpu.SemaphoreType.DMA(())
    out = pl.pallas_call(
        body, name=name, in_specs=[HBM, SEM, SEM, pl.BlockSpec(memory_space=pl.ANY)], out_specs=[HBM, SEM, SEM],
        out_shape=[pltpu.HBM(geom.full_shape, BF16), sem, sem],
        input_output_aliases={0: 0}, compiler_params=SPLIT_COPY,
    )(full, sems[0], sems[1], after)
    return out[0], (out[1], out[2])


def _gather_end(name, full, geom, sems, after):
    def body(w_in, fwd_send, fwd_recv, after_ref, w_ref):
        x, y, c = _mesh_pos()
        sibling = (x, y, 1 - c)
        _wait_three(geom, w_ref, 1 - c, fwd_send, fwd_recv, sibling, recv=True)
        _wait_three(geom, w_ref, c, fwd_send, fwd_recv, sibling, recv=False)

    return pl.pallas_call(
        body, name=name, in_specs=[HBM, SEM, SEM, pl.BlockSpec(memory_space=pl.ANY)], out_specs=HBM,
        out_shape=pltpu.HBM(geom.full_shape, BF16),
        input_output_aliases={0: 0}, compiler_params=SPLIT_COPY,
    )(full, sems[0], sems[1], after)


def _pair_exchange(grads, geoms):
    nw = len(grads)

    def body(*refs):
        src, dst = refs[:nw], refs[nw:2 * nw]
        send_sems, recv_sems = refs[2 * nw:]
        x, y, c = _mesh_pos()
        copies = [pltpu.make_async_remote_copy(
            src_ref=geoms[w].half_of_full(src[w], 1 - c), dst_ref=dst[w], send_sem=send_sems.at[w],
            recv_sem=recv_sems.at[w], device_id=(x, y, 1 - c), device_id_type=MESH) for w in range(nw)]
        for cp in copies:
            cp.start()
        for cp in copies:
            cp.wait()

    return pl.pallas_call(
        body, name="grads_pair_exchange", in_specs=[ANY] * nw, out_specs=[ANY] * nw,
        out_shape=[jax.ShapeDtypeStruct(g.half_shape, BF16) for g in geoms],
        scratch_shapes=[pltpu.SemaphoreType.DMA((nw,)), pltpu.SemaphoreType.DMA((nw,))],
    )(*grads)


def _chip_exchange(halves, geoms):
    nw = len(halves)

    def body(*refs):
        src, dst = refs[:nw], refs[nw:2 * nw]
        send_sems, recv_sems = refs[2 * nw:]
        x, y, c = _mesh_pos()
        copies = []
        for w in range(nw):
            for k, chip in enumerate(_other_chips(x, y)):
                copies.append(pltpu.make_async_remote_copy(
                    src_ref=geoms[w].shard_of_half(src[w], 2 * chip[0] + chip[1]), dst_ref=dst[w].at[k],
                    send_sem=send_sems.at[w, k], recv_sem=recv_sems.at[w, k], device_id=(*chip, c),
                    device_id_type=MESH))
        for cp in copies:
            cp.start()
        for cp in copies:
            cp.wait()

    return pl.pallas_call(
        body, name="grads_chip_exchange", in_specs=[ANY] * nw, out_specs=[ANY] * nw,
        out_shape=[jax.ShapeDtypeStruct((3,) + g.shard_half_shape, BF16) for g in geoms],
        scratch_shapes=[pltpu.SemaphoreType.DMA((nw, 3)), pltpu.SemaphoreType.DMA((nw, 3))],
    )(*halves)


def _pair_share(quarters, geoms):
    nw = len(quarters)

    def body(*refs):
        dst = refs[nw:2 * nw]
        send_sems, recv_sems = refs[2 * nw:]
        x, y, c = _mesh_pos()
        sends = []
        for w in range(nw):
            own_half = geoms[w].half_of_shard(dst[w], c)
            sends.append(pltpu.make_async_remote_copy(
                src_ref=own_half, dst_ref=own_half, send_sem=send_sems.at[w], recv_sem=recv_sems.at[w],
                device_id=(x, y, 1 - c), device_id_type=MESH))
        for cp in sends:
            cp.start()
        for w in range(nw):
            theirs = geoms[w].half_of_shard(dst[w], 1 - c)
            pltpu.make_async_remote_copy(src_ref=theirs, dst_ref=theirs, send_sem=send_sems.at[w],
                                         recv_sem=recv_sems.at[w], device_id=(x, y, 1 - c),
                                         device_id_type=MESH).wait_recv()
        for cp in sends:
            cp.wait_send()

    return pl.pallas_call(
        body, name="grads_pair_share", in_specs=[ANY] * nw, out_specs=[ANY] * nw,
        out_shape=[jax.ShapeDtypeStruct(g.shard_shape, F32) for g in geoms],
        input_output_aliases={w: w for w in range(nw)},
        scratch_shapes=[pltpu.SemaphoreType.DMA((nw,)), pltpu.SemaphoreType.DMA((nw,))],
    )(*quarters)


def _pair_add(name, grad, recv, geom, c_idx):
    r, c = geom.half_shape
    tr, tc = _tile(r, 256, 16), _tile(c, 2048, 128)
    nr, ncol = r // tr, c // tc
    if geom.col:
        mine = lambda i, j, cref: (cref[0] * nr + i, j)
    else:
        mine = lambda i, j, cref: (i, cref[0] * ncol + j)

    def body(c_ref, g_ref, r_ref, o_ref):
        o_ref[...] = (g_ref[...].astype(F32) + r_ref[...].astype(F32)).astype(BF16)

    return pl.pallas_call(
        body, name=name,
        grid_spec=pltpu.PrefetchScalarGridSpec(
            num_scalar_prefetch=1, grid=(nr, ncol),
            in_specs=[pl.BlockSpec((tr, tc), mine), pl.BlockSpec((tr, tc), lambda i, j, cref: (i, j))],
            out_specs=pl.BlockSpec((tr, tc), lambda i, j, cref: (i, j))),
        out_shape=jax.ShapeDtypeStruct((r, c), BF16),
        compiler_params=_params(("parallel", "parallel"), 3 * _nbytes((tr, tc), F32)),
    )(c_idx, grad, recv)


def _chip_add(name, half, recv, geom, idx):
    r, c = geom.shard_half_shape
    tr, tc = _tile(r, 256, 16), _tile(c, 2048, 128)
    nr, ncol = r // tr, c // tc
    if geom.col:
        mine = lambda i, j, iref: (i, iref[0] * ncol + j)
        place = lambda i, j, iref: (iref[1] * nr + i, j)
    else:
        mine = lambda i, j, iref: (iref[0] * nr + i, j)
        place = lambda i, j, iref: (i, iref[1] * ncol + j)

    def body(i_ref, h_ref, r_ref, o_ref):
        acc = h_ref[...].astype(F32)
        for k in range(3):
            acc = acc + r_ref[k].astype(F32)
        o_ref[...] = acc

    return pl.pallas_call(
        body, name=name,
        grid_spec=pltpu.PrefetchScalarGridSpec(
            num_scalar_prefetch=1, grid=(nr, ncol),
            in_specs=[pl.BlockSpec((tr, tc), mine), pl.BlockSpec((3, tr, tc), lambda i, j, iref: (0, i, j))],
            out_specs=pl.BlockSpec((tr, tc), place)),
        out_shape=jax.ShapeDtypeStruct(geom.shard_shape, F32),
        compiler_params=_params(("parallel", "parallel"), 4 * _nbytes((tr, tc), F32)),
    )(idx, half, recv)


def _all_reduce_small(pack):
    r, d = pack.shape

    def body(p_ref, o_ref, slots, send_sems, recv_sems):
        x, y, c = _mesh_pos()
        me = 4 * x + 2 * y + c
        slots[me] = p_ref[...]
        copies = []
        for k in range(1, N_DEV):
            px, py, pc = x ^ ((k >> 2) & 1), y ^ ((k >> 1) & 1), c ^ (k & 1)
            copies.append(pltpu.make_async_remote_copy(
                src_ref=p_ref, dst_ref=slots.at[me], send_sem=send_sems.at[k - 1], recv_sem=recv_sems.at[k - 1],
                device_id=(px, py, pc), device_id_type=MESH))
        for cp in copies:
            cp.start()
        for k in range(1, N_DEV):
            peer = 4 * (x ^ ((k >> 2) & 1)) + 2 * (y ^ ((k >> 1) & 1)) + (c ^ (k & 1))
            pltpu.make_async_remote_copy(
                src_ref=p_ref, dst_ref=slots.at[peer], send_sem=send_sems.at[k - 1], recv_sem=recv_sems.at[k - 1],
                device_id=(x, y, c), device_id_type=MESH).wait_recv()
        for cp in copies:
            cp.wait_send()
        acc = slots[0]
        for k in range(1, N_DEV):
            acc = acc + slots[k]
        o_ref[...] = acc

    vm = pl.BlockSpec(memory_space=pltpu.VMEM)
    return pl.pallas_call(
        body, name="all_reduce_small", in_specs=[vm], out_specs=vm,
        out_shape=jax.ShapeDtypeStruct((r, d), F32),
        scratch_shapes=[pltpu.VMEM((N_DEV, r, d), F32), pltpu.SemaphoreType.DMA((N_DEV - 1,)),
                        pltpu.SemaphoreType.DMA((N_DEV - 1,))],
    )(pack)


def _pack_rows(rows, d):
    out = []
    for a in rows:
        flat = a.reshape(-1)
        n = -(-flat.shape[0] // d) * d
        out.append(jnp.pad(flat, (0, n - flat.shape[0])).reshape(-1, d))
    packed = jnp.concatenate(out, axis=0)
    return jnp.pad(packed, ((0, 16 - packed.shape[0]), (0, 0)))


def _unpack_rows(packed, shapes, d):
    out, row = [], 0
    for shp in shapes:
        n = int(np.prod(shp))
        nrows = -(-n // d)
        out.append(packed[row:row + nrows].reshape(-1)[:n].reshape(shp))
        row += nrows
    return out


def kernel(x, pre_norm_ffn1, post_norm_ffn1, w_ffn1_gate_up, w_ffn1_down, pre_norm_mix, post_norm_mix, w_mix_in, hgrn_lower_bounds_fwd, hgrn_lower_bounds_bwd, hgrn_out_norm, attn_sink, w_mix_out, pre_norm_ffn2, post_norm_ffn2, w_ffn2_gate_up, w_ffn2_down, rel_bias_table, loss_target, m_pre_norm_ffn1, m_post_norm_ffn1, m_w_ffn1_gate_up, m_w_ffn1_down, m_pre_norm_mix, m_post_norm_mix, m_w_mix_in, m_hgrn_lower_bounds_fwd, m_hgrn_lower_bounds_bwd, m_hgrn_out_norm, m_attn_sink, m_w_mix_out, m_pre_norm_ffn2, m_post_norm_ffn2, m_w_ffn2_gate_up, m_w_ffn2_down, m_rel_bias_table, v_pre_norm_ffn1, v_post_norm_ffn1, v_w_ffn1_gate_up, v_w_ffn1_down, v_pre_norm_mix, v_post_norm_mix, v_w_mix_in, v_hgrn_lower_bounds_fwd, v_hgrn_lower_bounds_bwd, v_hgrn_out_norm, v_attn_sink, v_w_mix_out, v_pre_norm_ffn2, v_post_norm_ffn2, v_w_ffn2_gate_up, v_w_ffn2_down, v_rel_bias_table):
    t, d = x.shape[1], x.shape[2]
    hw = hgrn_out_norm.shape[1]
    aw = d - hw
    nah = aw // HEAD
    kvw = KV_HEADS * HEAD
    x0 = x[0]
    target = loss_target[0]

    big_names = ["w_ffn1_gate_up", "w_ffn1_down", "w_mix_in", "w_mix_out", "w_ffn2_gate_up", "w_ffn2_down"]
    big_w = [w_ffn1_gate_up[0], w_ffn1_down[0], w_mix_in[0], w_mix_out[0], w_ffn2_gate_up[0], w_ffn2_down[0]]
    big_m = [m_w_ffn1_gate_up[0], m_w_ffn1_down[0], m_w_mix_in[0], m_w_mix_out[0], m_w_ffn2_gate_up[0],
             m_w_ffn2_down[0]]
    big_v = [v_w_ffn1_gate_up[0], v_w_ffn1_down[0], v_w_mix_in[0], v_w_mix_out[0], v_w_ffn2_gate_up[0],
             v_w_ffn2_down[0]]
    col_sharded = [True, False, True, False, True, False]
    geoms = [_Big(w.shape, cs) for w, cs in zip(big_w, col_sharded)]

    cx, cy, cc = _mesh_pos()
    idx = jnp.stack([2 * cx + cy, cc]).astype(jnp.int32)
    c_idx = jnp.reshape(cc, (1,)).astype(jnp.int32)
    own_quarters = [_cast_into_full(f"cast_{n}", w, gm, idx) for n, w, gm in zip(big_names, big_w, geoms)]
    started, gather_sems = _gather_start(own_quarters, geoms)

    def forward_weight(w, after):
        return _gather_forward(f"gather_forward_{big_names[w]}", started[w], geoms[w], gather_sems[w], after)

    def whole_weight(w, forwarded, after):
        return _gather_end(f"gather_end_{big_names[w]}", forwarded[0], geoms[w], forwarded[1], after)

    h1 = _norm_fwd("ffn1_pre_norm", x0, pre_norm_ffn1)
    w_gu1 = whole_weight(0, forward_weight(0, h1), h1)
    gu1 = _mm("ffn1_gate_up", h1, w_gu1, "nn", BF16)
    fw = forward_weight(1, gu1)
    act1 = _swiglu_fwd("ffn1_act", gu1)
    w_d1 = whole_weight(1, fw, act1)
    ff1 = _mm("ffn1_down", act1, w_d1, "nn", F32)
    fw = forward_weight(2, ff1)
    x1, hm = _resid_norm_fwd("ffn1_residual", x0, ff1, post_norm_ffn1, pre_norm_mix, 0.5)
    w_in = whole_weight(2, fw, hm)
    p = _mm("mix_in", hm, w_in, "nn", F32)
    fw = forward_weight(3, p)
    o_f, o_b, st_f, st_b = _hgrn_scan_fwd("hgrn_scan", p, hgrn_lower_bounds_fwd, hgrn_lower_bounds_bwd)
    y_h = _hgrn_out_fwd("hgrn_out", o_f, o_b, p, hgrn_out_norm, 4)
    kv_blk0 = (5 * hw + aw) // kvw
    k_pad = _pad_kv("attn_pad_k", p, kv_blk0, kvw)
    v_pad = _pad_kv("attn_pad_v", p, kv_blk0 + 1, kvw)
    bucket_ids = _t5_bucket_ids()
    bias = _bias_gather("attn_bias", rel_bias_table.T, bucket_ids).reshape(nah, WINDOW, SPAN)
    y_a = _attn_fwd("attn_fwd", p, k_pad, v_pad, bias, attn_sink, 5 * hw // aw)
    y_mix = _concat_cols("mix_concat", y_h, y_a)
    w_out = whole_weight(3, fw, y_mix)
    mixed = _mm("mix_out", y_mix, w_out, "nn", F32)
    fw = forward_weight(4, mixed)
    x2, h2 = _resid_norm_fwd("mix_residual", x1, mixed, post_norm_mix, pre_norm_ffn2, 1.0)
    w_gu2 = whole_weight(4, fw, h2)
    gu2 = _mm("ffn2_gate_up", h2, w_gu2, "nn", BF16)
    fw = forward_weight(5, gu2)
    act2 = _swiglu_fwd("ffn2_act", gu2)
    w_d2 = whole_weight(5, fw, act2)
    ff2 = _mm("ffn2_down", act2, w_d2, "nn", F32)
    loss_blk, dy, dff2, dg_post2 = _final_fwd_bwd("ffn2_residual_loss", x2, ff2, post_norm_ffn2, target, 0.5)

    dh2, dw_gu2, dw_d2 = _ffn_bwd("ffn2", dff2, h2, gu2, act2, w_gu2, w_d2)
    dx2, dg_pre2, dmixed, dg_postm = _norms_bwd("mix_residual_bwd", dy, dh2, x2, pre_norm_ffn2,
                                                post=(mixed, post_norm_mix, 1.0))
    dy_mix = _mm("mix_out_dx", dmixed, w_out, "nt", F32)
    dw_out = _mm("mix_out_dw", y_mix, dmixed, "tn", BF16)
    dq_a, dk_pad, dv_pad, dbias, dsink = _attn_bwd("attn_bwd", p, k_pad, v_pad, bias, attn_sink, dy_mix,
                                                   5 * hw // aw, hw // aw)
    drel_t = _bias_scatter("attn_dbias", dbias.reshape(nah, WINDOW * SPAN), bucket_ids)
    do, dg_h, dgain = _hgrn_out_bwd("hgrn_out_bwd", dy_mix, o_f, o_b, p, hgrn_out_norm, 4)
    dq_f, dv_f, dz_f, dlb_f, dq_b, dv_b, dz_b, dlb_b = _hgrn_scan_bwd(
        "hgrn_scan_bwd", p, hgrn_lower_bounds_fwd, hgrn_lower_bounds_bwd, do, st_f, st_b)
    dp = _mix_dproj("mix_dproj", [(dq_f, dq_b), (dv_f, dv_b), (dz_f,), (dz_b,), (dg_h,), (dq_a,)],
                    [dk_pad, dv_pad], t)
    dhm = _mm("mix_in_dx", dp, w_in, "nt", F32)
    dw_in = _mm("mix_in_dw", hm, dp, "tn", BF16)
    dx1, dg_prem, dff1, dg_post1 = _norms_bwd("ffn1_residual_bwd", dx2, dhm, x1, pre_norm_mix,
                                              post=(ff1, post_norm_ffn1, 0.5))
    dh1, dw_gu1, dw_d1 = _ffn_bwd("ffn1", dff1, h1, gu1, act1, w_gu1, w_d1)
    grad_x, dg_pre1 = _norms_bwd("ffn1_pre_norm_bwd", dx1, dh1, x0, pre_norm_ffn1)

    dws = [dw_gu1, dw_d1, dw_in, dw_out, dw_gu2, dw_d2]
    from_sibling = _pair_exchange(dws, geoms)
    pair_sums = [_pair_add(f"pair_add_{n}", g, r, gm, c_idx) for n, g, r, gm in zip(big_names, dws, from_sibling, geoms)]
    from_chips = _chip_exchange(pair_sums, geoms)
    reduced = [_chip_add(f"chip_add_{n}", h, r, gm, idx) for n, h, r, gm in zip(big_names, pair_sums, from_chips, geoms)]
    big_grads = _pair_share(reduced, geoms)

    small_w = [pre_norm_ffn1, post_norm_ffn1, pre_norm_mix, post_norm_mix, hgrn_lower_bounds_fwd,
               hgrn_lower_bounds_bwd, hgrn_out_norm, attn_sink, pre_norm_ffn2, post_norm_ffn2, rel_bias_table]
    small_m = [m_pre_norm_ffn1, m_post_norm_ffn1, m_pre_norm_mix, m_post_norm_mix, m_hgrn_lower_bounds_fwd,
               m_hgrn_lower_bounds_bwd, m_hgrn_out_norm, m_attn_sink, m_pre_norm_ffn2, m_post_norm_ffn2,
               m_rel_bias_table]
    small_v = [v_pre_norm_ffn1, v_post_norm_ffn1, v_pre_norm_mix, v_post_norm_mix, v_hgrn_lower_bounds_fwd,
               v_hgrn_lower_bounds_bwd, v_hgrn_out_norm, v_attn_sink, v_pre_norm_ffn2, v_post_norm_ffn2,
               v_rel_bias_table]
    small_g = [dg_pre1, dg_post1, dg_prem, dg_postm, dlb_f, dlb_b, dgain, dsink[:, 0].reshape(1, nah), dg_pre2,
               dg_post2, drel_t.T]
    shapes = [a.shape for a in small_w]
    summed = _all_reduce_small(_pack_rows(small_g + [loss_blk[0:1, 0:1]], d))
    g_pack = summed
    loss =_unpack_rows(summed, shapes + [(1, 1)], d)[-1][0, 0]
    sd, sm, sv = _adamw("adamw_small", _pack_rows(small_w, d), g_pack, _pack_rows(small_m, d), _pack_rows(small_v, d))
    small_grads = _unpack_rows(g_pack, shapes, d)
    small_delta, small_new_m, small_new_v = (_unpack_rows(a, shapes, d) for a in (sd, sm, sv))

    big_delta, big_new_m, big_new_v = [], [], []
    for n, w, g, m, v in zip(big_names, big_w, big_grads, big_m, big_v):
        dl, nm, nv = _adamw(f"adamw_{n}", w, g, m, v)
        big_delta.append(dl[None])
        big_new_m.append(nm[None])
        big_new_v.append(nv[None])
    big_grads = [g[None] for g in big_grads]

    def ordered(small, big):
        s = dict(zip(["pre1", "post1", "prem", "postm", "lbf", "lbb", "gain", "sink", "pre2", "post2", "rel"], small))
        b = dict(zip(["gu1", "d1", "win", "wout", "gu2", "d2"], big))
        return [s["pre1"], s["post1"], b["gu1"], b["d1"], s["prem"], s["postm"], b["win"], s["lbf"], s["lbb"],
                s["gain"], s["sink"], b["wout"], s["pre2"], s["post2"], b["gu2"], b["d2"], s["rel"]]

    return (loss, grad_x[None], *ordered(small_grads, big_grads), *ordered(small_delta, big_delta),
            *ordered(small_new_m, big_new_m), *ordered(small_new_v, big_new_v))
```

```python
import functools
import math

import jax
import jax.numpy as jnp
import numpy as np
from jax import lax
from jax.experimental import pallas as pl
from jax.experimental.pallas import tpu as pltpu

F32 = jnp.float32
BF16 = jnp.bfloat16

HEAD = 128
CHUNK = 64
WINDOW = 128
SPAN = 3 * WINDOW
KV_HEADS = 2
REL_BUCKETS = 32
REL_MAX_DIST = 128
EPS = 1e-6
NEG_INF = -1e30

ADAM_LR = 0.001
ADAM_B1 = 0.9
ADAM_B2 = 0.999
ADAM_EPS = 1e-08
ADAM_WD = 0.01
ADAM_STEP = 10

N_CHIPS = 4
N_DEV = 8
V7X_VMEM_BYTES = 64 * 1024 * 1024
MESH = pl.DeviceIdType.MESH
ANY = pl.BlockSpec(memory_space=pl.ANY)


def _tile(n, pref, mult):
    t = (min(pref, n) // mult) * mult
    while t >= mult:
        if n % t == 0:
            return t
        t -= mult
    return n


def _params(semantics, block_bytes):
    limit = min(V7X_VMEM_BYTES - (4 << 20), 2 * int(block_bytes) + (8 << 20))
    return pltpu.CompilerParams(dimension_semantics=semantics, vmem_limit_bytes=limit)


def _nbytes(shape, dtype):
    return int(np.prod(shape)) * jnp.dtype(dtype).itemsize


def _dot(a, b, ca=1, cb=0):
    return lax.dot_general(a, b, (((ca,), (cb,)), ((), ())), preferred_element_type=F32)


def _split3(x):
    hi = x.astype(BF16)
    r1 = x - hi.astype(F32)
    mid = r1.astype(BF16)
    lo = (r1 - mid.astype(F32)).astype(BF16)
    return hi, mid, lo


def _dot_exact(a, b, ca=1, cb=0, split="b"):
    if split == "b":
        return sum(_dot(a, p, ca, cb) for p in _split3(b))
    return sum(_dot(p, b, ca, cb) for p in _split3(a))


def _rms(x):
    return lax.rsqrt(jnp.mean(x * x, axis=-1, keepdims=True) + EPS)


def _norm_bwd(u, x, gain):
    r = _rms(x)
    xhat = x * r
    dgain = jnp.sum(u * xhat, axis=0, keepdims=True)
    v = u * gain
    dx = r * (v - xhat * jnp.mean(v * xhat, axis=-1, keepdims=True))
    return dx, dgain


def _sigmoid(x):
    return 1.0 / (1.0 + jnp.exp(-x))


def _accumulate(ref, val, first):
    @pl.when(first)
    def _():
        ref[...] = val

    @pl.when(jnp.logical_not(first))
    def _():
        ref[...] += val


def _matmul(name, a, b, *, form, out_dtype, tm, tn, tk, a_map=None, b_map=None,
            out_shape=None, out_block=None, out_map=None, sizes=None):
    if sizes is None:
        if form == "nn":
            (m, k), n = a.shape, b.shape[1]
        elif form == "nt":
            (m, k), n = a.shape, b.shape[0]
        else:
            (k, m), n = a.shape, b.shape[1]
    else:
        m, n, k = sizes
    gi, gj, gk = m // tm, n // tn, k // tk
    a_blk = (tm, tk) if form != "tn" else (tk, tm)
    b_blk = (tk, tn) if form != "nt" else (tn, tk)
    if a_map is None:
        a_map = (lambda i, j, kk: (i, kk)) if form != "tn" else (lambda i, j, kk: (kk, i))
    else:
        a_blk = (None,) + a_blk
    if b_map is None:
        b_map = (lambda i, j, kk: (kk, j)) if form != "nt" else (lambda i, j, kk: (j, kk))
    else:
        b_blk = (None,) + b_blk
    if out_shape is None:
        out_shape, out_block, out_map = (m, n), (tm, tn), (lambda i, j, kk: (i, j))
    ca, cb = {"nn": (1, 0), "nt": (1, 1), "tn": (0, 0)}[form]

    def body(a_ref, b_ref, o_ref, *acc):
        part = _dot(a_ref[...], b_ref[...], ca, cb)
        if gk == 1:
            o_ref[...] = part.astype(o_ref.dtype)
        else:
            kk = pl.program_id(2)
            _accumulate(acc[0], part, kk == 0)

            @pl.when(kk == gk - 1)
            def _():
                o_ref[...] = acc[0][...].astype(o_ref.dtype)

    scratch = [] if gk == 1 else [pltpu.VMEM((tm, tn), F32)]
    vmem = (_nbytes((tm, tk), a.dtype) + _nbytes((tk, tn), b.dtype) + _nbytes((tm, tn), out_dtype)
            + 2 * _nbytes((tm, tn), F32))
    return pl.pallas_call(
        body, name=name, grid=(gi, gj, gk),
        in_specs=[pl.BlockSpec(a_blk, a_map), pl.BlockSpec(b_blk, b_map)],
        out_specs=pl.BlockSpec(out_block, out_map),
        out_shape=jax.ShapeDtypeStruct(out_shape, out_dtype),
        scratch_shapes=scratch,
        compiler_params=_params(("parallel", "parallel", "arbitrary"), vmem),
    )(a, b)


def _mm_tiles(m, n, k):
    return _tile(m, 1024, 128), _tile(n, 512, 128), _tile(k, 2816, 128)


def _mm(name, a, b, form, out_dtype):
    if form == "nn":
        m, k, n = a.shape[0], a.shape[1], b.shape[1]
    elif form == "nt":
        m, k, n = a.shape[0], a.shape[1], b.shape[0]
    else:
        m, k, n = a.shape[1], a.shape[0], b.shape[1]
    tm, tn, tk = _mm_tiles(m, n, k)
    return _matmul(name, a, b, form=form, out_dtype=out_dtype, tm=tm, tn=tn, tk=tk)


def _row_tile(t):
    return _tile(t, 256, 8)


def _norm_fwd(name, x, gain):
    t, d = x.shape
    tm = _row_tile(t)

    def body(x_ref, g_ref, h_ref):
        xv = x_ref[...]
        h_ref[...] = (xv * _rms(xv) * g_ref[...]).astype(BF16)

    row = pl.BlockSpec((tm, d), lambda i: (i, 0))
    vec = pl.BlockSpec((1, d), lambda i: (0, 0))
    return pl.pallas_call(
        body, name=name, grid=(t // tm,), in_specs=[row, vec], out_specs=row,
        out_shape=jax.ShapeDtypeStruct((t, d), BF16),
        compiler_params=_params(("parallel",), 2 * _nbytes((tm, d), F32)),
    )(x, gain)


def _resid_norm_fwd(name, xres, ff, gpost, gpre, scale):
    t, d = xres.shape
    tm = _row_tile(t)

    def body(x_ref, f_ref, gp_ref, gn_ref, xn_ref, h_ref):
        f = f_ref[...]
        xn = x_ref[...] + scale * (f * _rms(f) * gp_ref[...])
        xn_ref[...] = xn
        h_ref[...] = (xn * _rms(xn) * gn_ref[...]).astype(BF16)

    row = pl.BlockSpec((tm, d), lambda i: (i, 0))
    vec = pl.BlockSpec((1, d), lambda i: (0, 0))
    return pl.pallas_call(
        body, name=name, grid=(t // tm,), in_specs=[row, row, vec, vec], out_specs=[row, row],
        out_shape=[jax.ShapeDtypeStruct((t, d), F32), jax.ShapeDtypeStruct((t, d), BF16)],
        compiler_params=_params(("parallel",), 4 * _nbytes((tm, d), F32)),
    )(xres, ff, gpost, gpre)


def _final_fwd_bwd(name, xres, ff, gpost, target, scale):
    t, d = xres.shape
    tm = _row_tile(t)

    def body(x_ref, f_ref, gp_ref, t_ref, loss_ref, dy_ref, dff_ref, dg_ref):
        i = pl.program_id(0)
        f = f_ref[...]
        gp = gp_ref[...]
        y = x_ref[...] + scale * (f * _rms(f) * gp)
        err = y - t_ref[...]
        part = 0.5 * jnp.sum(jnp.mean(err * err, axis=-1, keepdims=True), axis=0, keepdims=True)
        _accumulate(loss_ref, jnp.broadcast_to(part, loss_ref.shape), i == 0)
        dy = err / d
        dy_ref[...] = dy
        dff, dg = _norm_bwd(scale * dy, f, gp)
        dff_ref[...] = dff.astype(BF16)
        _accumulate(dg_ref, dg, i == 0)

    row = pl.BlockSpec((tm, d), lambda i: (i, 0))
    vec = pl.BlockSpec((1, d), lambda i: (0, 0))
    return pl.pallas_call(
        body, name=name, grid=(t // tm,), in_specs=[row, row, vec, row],
        out_specs=[pl.BlockSpec((8, 128), lambda i: (0, 0)), row, row, vec],
        out_shape=[jax.ShapeDtypeStruct((8, 128), F32), jax.ShapeDtypeStruct((t, d), F32),
                   jax.ShapeDtypeStruct((t, d), BF16), jax.ShapeDtypeStruct((1, d), F32)],
        compiler_params=_params(("arbitrary",), 5 * _nbytes((tm, d), F32)),
    )(xres, ff, gpost, target)


def _norms_bwd(name, dres, dh, xin, gpre, post=None):
    t, d = dres.shape
    tm = _row_tile(t)
    with_post = post is not None

    def body(*refs):
        if with_post:
            dr_ref, dh_ref, x_ref, g_ref, f_ref, gp_ref, dx_ref, dg_ref, dff_ref, dgp_ref = refs
        else:
            dr_ref, dh_ref, x_ref, g_ref, dx_ref, dg_ref = refs
        i = pl.program_id(0)
        dx, dg = _norm_bwd(dh_ref[...], x_ref[...], g_ref[...])
        dx = dr_ref[...] + dx
        dx_ref[...] = dx
        _accumulate(dg_ref, dg, i == 0)
        if with_post:
            dff, dgp = _norm_bwd(post[2] * dx, f_ref[...], gp_ref[...])
            dff_ref[...] = dff.astype(BF16)
            _accumulate(dgp_ref, dgp, i == 0)

    row = pl.BlockSpec((tm, d), lambda i: (i, 0))
    vec = pl.BlockSpec((1, d), lambda i: (0, 0))
    ins, in_specs = [dres, dh, xin, gpre], [row, row, row, vec]
    out_specs = [row, vec]
    out_shape = [jax.ShapeDtypeStruct((t, d), F32), jax.ShapeDtypeStruct((1, d), F32)]
    if with_post:
        ins += [post[0], post[1]]
        in_specs += [row, vec]
        out_specs += [row, vec]
        out_shape += [jax.ShapeDtypeStruct((t, d), BF16), jax.ShapeDtypeStruct((1, d), F32)]
    return pl.pallas_call(
        body, name=name, grid=(t // tm,), in_specs=in_specs, out_specs=out_specs, out_shape=out_shape,
        compiler_params=_params(("arbitrary",), 6 * _nbytes((tm, d), F32)),
    )(*ins)


def _swiglu_fwd(name, gu):
    t, f2 = gu.shape
    f = f2 // 2
    tm, tf = _tile(t, 512, 8), _tile(f, 512, 128)
    nf = f // tf

    def body(g_ref, u_ref, a_ref):
        g = g_ref[...].astype(F32)
        a_ref[...] = (g * _sigmoid(g) * u_ref[...].astype(F32)).astype(BF16)

    return pl.pallas_call(
        body, name=name, grid=(t // tm, nf),
        in_specs=[pl.BlockSpec((tm, tf), lambda i, j: (i, j)), pl.BlockSpec((tm, tf), lambda i, j: (i, j + nf))],
        out_specs=pl.BlockSpec((tm, tf), lambda i, j: (i, j)),
        out_shape=jax.ShapeDtypeStruct((t, f), BF16),
        compiler_params=_params(("parallel", "parallel"), 3 * _nbytes((tm, tf), F32)),
    )(gu, gu)


def _swiglu_bwd(name, da, gu):
    t, f = da.shape
    tm, tf = _tile(t, 512, 8), _tile(f, 512, 128)
    nf = f // tf

    def body(da_ref, g_ref, u_ref, o_ref):
        g = g_ref[...].astype(F32)
        u = u_ref[...].astype(F32)
        dav = da_ref[...]
        sig = _sigmoid(g)
        o_ref[0] = (dav * u * sig * (1.0 + g * (1.0 - sig))).astype(BF16)
        o_ref[1] = (dav * g * sig).astype(BF16)

    return pl.pallas_call(
        body, name=name, grid=(t // tm, nf),
        in_specs=[pl.BlockSpec((tm, tf), lambda i, j: (i, j)), pl.BlockSpec((tm, tf), lambda i, j: (i, j)),
                  pl.BlockSpec((tm, tf), lambda i, j: (i, j + nf))],
        out_specs=pl.BlockSpec((2, tm, tf), lambda i, j: (0, i, j)),
        out_shape=jax.ShapeDtypeStruct((2, t, f), BF16),
        compiler_params=_params(("parallel", "parallel"), 5 * _nbytes((tm, tf), F32)),
    )(da, gu, gu)


def _ffn_dh(name, dgu, w_gu):
    _, t, f = dgu.shape
    d = w_gu.shape[0]
    tm, tn, tk = _mm_tiles(t, d, f)
    nkf = f // tk
    return _matmul(name, dgu, w_gu, form="nt", out_dtype=F32, tm=tm, tn=tn, tk=tk, sizes=(t, d, 2 * f),
                   a_map=lambda i, j, kk: (kk // nkf, i, kk % nkf))


def _ffn_dw_gate_up(name, h, dgu):
    _, t, f = dgu.shape
    d = h.shape[1]
    tm, tn, tk = _mm_tiles(d, f, t)
    nf = f // tn
    return _matmul(name, h, dgu, form="tn", out_dtype=BF16, tm=tm, tn=tn, tk=tk, sizes=(d, 2 * f, t),
                   b_map=lambda i, j, kk: (j // nf, kk, j % nf))


def _lower_bound(lbp):
    m = jnp.max(lbp, axis=0, keepdims=True)
    e = jnp.exp(lbp - m)
    return e[0:1] / jnp.sum(e, axis=0, keepdims=True)


def _chunk_mask(reverse):
    row = lax.broadcasted_iota(jnp.int32, (CHUNK, CHUNK), 0)
    col = lax.broadcasted_iota(jnp.int32, (CHUNK, CHUNK), 1)
    return (col >= row) if reverse else (col <= row)


def _hgrn_gates(z, lb, mask_bf):
    sig = _sigmoid(z)
    f = lb + (1.0 - lb) * sig
    logf = jnp.log(f)
    k = 1.0 - f
    cum = _dot_exact(mask_bf, logf)
    last = jnp.sum(logf, axis=0, keepdims=True)
    return sig, f, k, cum, last


def _hgrn_scan_fwd(name, p, lbp_f, lbp_b):
    t = p.shape[0]
    hw = lbp_f.shape[1]
    nh, nc = hw // HEAD, t // CHUNK

    def body(qf, vf, zf, qb, vb, zb, lbf, lbb, of_ref, ob_ref, stf_ref, stb_ref, state):
        n = pl.program_id(0)

        @pl.when(n == 0)
        def _():
            state[...] = jnp.zeros_like(state)

        directions = [(qf, vf, zf, lbf, of_ref, stf_ref), (qb, vb, zb, lbb, ob_ref, stb_ref)]
        for d, (q_ref, v_ref, z_ref, lb_ref, o_ref, st_ref) in enumerate(directions):
            mask = _chunk_mask(d == 1)
            lb = _lower_bound(lb_ref[...])
            _, _, k, cum, last = _hgrn_gates(z_ref[...], lb, mask.astype(BF16))
            v = v_ref[...].astype(BF16)
            qd = (q_ref[...] * jnp.exp(cum)).astype(BF16)
            kd = (k * jnp.exp(-cum)).astype(BF16)
            kt = (k * jnp.exp(last - cum)).astype(BF16)
            dec = jnp.exp(last)
            s_all = state[d]
            st_ref[...] = s_all
            for h in range(nh):
                sl = slice(h * HEAD, (h + 1) * HEAD)
                s_in = s_all[:, sl]
                a = jnp.where(mask, _dot(qd[:, sl], kd[:, sl], 1, 1), 0.0).astype(BF16)
                o_ref[:, sl] = _dot(a, v[:, sl]) + _dot(qd[:, sl], s_in.astype(BF16), 1, 1)
                state[d, :, sl] = s_in * dec[:, sl] + _dot(v[:, sl], kt[:, sl], 0, 0)

    def col(group, reverse):
        return pl.BlockSpec((CHUNK, hw), lambda n: ((nc - 1 - n) if reverse else n, group))

    def st(reverse):
        return pl.BlockSpec((None, HEAD, hw), lambda n: ((nc - 1 - n) if reverse else n, 0, 0))

    lb_spec = pl.BlockSpec((2, hw), lambda n: (0, 0))
    out = jax.ShapeDtypeStruct((t, hw), F32)
    states = jax.ShapeDtypeStruct((nc, HEAD, hw), F32)
    return pl.pallas_call(
        body, name=name, grid=(nc,),
        in_specs=[col(0, False), col(1, False), col(2, False), col(0, True), col(1, True), col(3, True),
                  lb_spec, lb_spec],
        out_specs=[col(0, False), col(0, True), st(False), st(True)],
        out_shape=[out, out, states, states],
        scratch_shapes=[pltpu.VMEM((2, HEAD, hw), F32)],
        compiler_params=_params(("arbitrary",), 12 * _nbytes((HEAD, hw), F32)),
    )(p, p, p, p, p, p, lbp_f, lbp_b)


def _hgrn_scan_bwd(name, p, lbp_f, lbp_b, do, st_f, st_b):
    t = p.shape[0]
    hw = lbp_f.shape[1]
    nh, nc = hw // HEAD, t // CHUNK

    def body(qf, vf, zf, dof, sf, qb, vb, zb, dob, sb, lbf, lbb, dqf, dvf, dzf, dlbf, dqb, dvb, dzb, dlbb,
             dstate, dlb_acc, dqd_s, dkd_s, dkt_s, ddec_s):
        n = pl.program_id(0)

        @pl.when(n == 0)
        def _():
            dstate[...] = jnp.zeros_like(dstate)
            dlb_acc[...] = jnp.zeros_like(dlb_acc)

        directions = [(qf, vf, zf, dof, sf, lbf, dqf, dvf, dzf, dlbf), (qb, vb, zb, dob, sb, lbb, dqb, dvb, dzb, dlbb)]
        for d, (q_ref, v_ref, z_ref, do_ref, st_ref, lb_ref, dq_ref, dv_ref, dz_ref, dlb_ref) in enumerate(directions):
            mask = _chunk_mask(d == 1)
            mask_bf = mask.astype(BF16)
            lb = _lower_bound(lb_ref[...])
            sig, f, k, cum, last = _hgrn_gates(z_ref[...], lb, mask_bf)
            e_pos, e_neg, e_tail = jnp.exp(cum), jnp.exp(-cum), jnp.exp(last - cum)
            dec = jnp.exp(last)
            v = v_ref[...].astype(BF16)
            qd, kd, kt = q_ref[...] * e_pos, k * e_neg, k * e_tail
            qd_bf, kd_bf, kt_bf = qd.astype(BF16), kd.astype(BF16), kt.astype(BF16)
            s_all = st_ref[...]
            ds_all = dstate[d]
            dov = do_ref[...].astype(BF16)
            for h in range(nh):
                sl = slice(h * HEAD, (h + 1) * HEAD)
                s_in, ds_out = s_all[:, sl], ds_all[:, sl]
                ds_bf = ds_out.astype(BF16)
                a = jnp.where(mask, _dot(qd_bf[:, sl], kd_bf[:, sl], 1, 1), 0.0).astype(BF16)
                da = jnp.where(mask, _dot(dov[:, sl], v[:, sl], 1, 1), 0.0).astype(BF16)
                dv_ref[:, sl] = _dot(a, dov[:, sl], 0, 0) + _dot(kt_bf[:, sl], ds_bf, 1, 1)
                dqd_s[:, sl] = _dot(da, kd_bf[:, sl]) + _dot(dov[:, sl], s_in.astype(BF16))
                dkd_s[:, sl] = _dot(da, qd_bf[:, sl], 0, 0)
                dkt_s[:, sl] = _dot(v[:, sl], ds_bf)
                dstate[d, :, sl] = _dot(dov[:, sl], qd_bf[:, sl], 0, 0) + ds_out * dec[:, sl]
                ddec_s[:, sl] = jnp.sum(ds_out * s_in, axis=0, keepdims=True)
            dqd, dkd, dkt = dqd_s[...], dkd_s[...], dkt_s[...]
            dlast = jnp.sum(dkt * kt, axis=0, keepdims=True) + dec * ddec_s[...]
            dq_ref[...] = dqd * e_pos
            dk = dkd * e_neg + dkt * e_tail
            dcum = dqd * qd - dkd * kd - dkt * kt
            dlogf = _dot_exact(mask_bf, dcum, 0, 0) + dlast
            df = dlogf / f - dk
            dz_ref[...] = df * (1.0 - lb) * sig * (1.0 - sig)
            dlb_acc[d] += jnp.sum(df * (1.0 - sig), axis=0, keepdims=True)

            @pl.when(n == nc - 1)
            def _():
                g = dlb_acc[d] * lb * (1.0 - lb)
                dlb_ref[0:1, :] = g
                dlb_ref[1:2, :] = -g

    def col(group, reverse):
        return pl.BlockSpec((CHUNK, hw), lambda n: (n if reverse else (nc - 1 - n), group))

    def st(reverse):
        return pl.BlockSpec((None, HEAD, hw), lambda n: (n if reverse else (nc - 1 - n), 0, 0))

    lb_spec = pl.BlockSpec((2, hw), lambda n: (0, 0))
    out = jax.ShapeDtypeStruct((t, hw), F32)
    dlb = jax.ShapeDtypeStruct((2, hw), F32)
    wide = pltpu.VMEM((CHUNK, hw), F32)
    return pl.pallas_call(
        body, name=name, grid=(nc,),
        in_specs=[col(0, False), col(1, False), col(2, False), col(0, False), st(False),
                  col(0, True), col(1, True), col(3, True), col(0, True), st(True), lb_spec, lb_spec],
        out_specs=[col(0, False), col(0, False), col(0, False), lb_spec,
                   col(0, True), col(0, True), col(0, True), lb_spec],
        out_shape=[out, out, out, dlb, out, out, out, dlb],
        scratch_shapes=[pltpu.VMEM((2, HEAD, hw), F32), pltpu.VMEM((2, 1, hw), F32), wide, wide, wide,
                        pltpu.VMEM((1, hw), F32)],
        compiler_params=_params(("arbitrary",), 16 * _nbytes((HEAD, hw), F32)),
    )(p, p, p, do, st_f, p, p, p, do, st_b, lbp_f, lbp_b)


def _hgrn_out_fwd(name, o_f, o_b, p, gain, g_group):
    t, hw = o_f.shape
    nh = hw // HEAD
    tm = _tile(t, 512, 8)

    def body(of_ref, ob_ref, g_ref, gain_ref, y_ref):
        o = of_ref[...] + ob_ref[...]
        g = g_ref[...]
        y_ref[...] = (o * _rms(o) * gain_ref[...] * (g * _sigmoid(g))).astype(BF16)

    blk = pl.BlockSpec((tm, HEAD), lambda i, h: (i, h))
    return pl.pallas_call(
        body, name=name, grid=(t // tm, nh),
        in_specs=[blk, blk, pl.BlockSpec((tm, HEAD), lambda i, h: (i, g_group * nh + h)),
                  pl.BlockSpec((1, HEAD), lambda i, h: (0, h))],
        out_specs=blk, out_shape=jax.ShapeDtypeStruct((t, hw), BF16),
        compiler_params=_params(("parallel", "parallel"), 1 << 20),
    )(o_f, o_b, p, gain)


def _hgrn_out_bwd(name, dy, o_f, o_b, p, gain, g_group):
    t, hw = o_f.shape
    nh = hw // HEAD
    tm = _tile(t, 512, 8)

    def body(dy_ref, of_ref, ob_ref, g_ref, gain_ref, do_ref, dg_ref, dgain_ref):
        i = pl.program_id(1)
        o = of_ref[...] + ob_ref[...]
        g = g_ref[...]
        gain_v = gain_ref[...]
        sig = _sigmoid(g)
        dyv = dy_ref[...]
        do, dgain = _norm_bwd(dyv * (g * sig), o, gain_v)
        do_ref[...] = do
        dg_ref[...] = dyv * (o * _rms(o) * gain_v) * sig * (1.0 + g * (1.0 - sig))
        _accumulate(dgain_ref, dgain, i == 0)

    blk = pl.BlockSpec((tm, HEAD), lambda h, i: (i, h))
    vec = pl.BlockSpec((1, HEAD), lambda h, i: (0, h))
    out = jax.ShapeDtypeStruct((t, hw), F32)
    return pl.pallas_call(
        body, name=name, grid=(nh, t // tm),
        in_specs=[blk, blk, blk, pl.BlockSpec((tm, HEAD), lambda h, i: (i, g_group * nh + h)), vec],
        out_specs=[blk, blk, vec], out_shape=[out, out, jax.ShapeDtypeStruct((1, hw), F32)],
        compiler_params=_params(("parallel", "arbitrary"), 1 << 20),
    )(dy, o_f, o_b, p, gain)


def _t5_bucket_ids():
    c = np.arange(WINDOW)[:, None]
    s = np.arange(SPAN)[None, :]
    rel = s - WINDOW - c
    nb = REL_BUCKETS // 2
    max_exact = nb // 2
    bucket = (rel > 0).astype(np.int32) * nb
    n = np.abs(rel)
    large = max_exact + (np.log(np.maximum(n, 1) / max_exact) / np.log(REL_MAX_DIST / max_exact)
                         * (nb - max_exact)).astype(np.int32)
    large = np.minimum(large, nb - 1)
    ids = bucket + np.where(n < max_exact, n, large).astype(np.int32)
    return jnp.asarray(ids.reshape(1, WINDOW * SPAN), jnp.int32)


def _bias_onehot(ids_ref):
    n = ids_ref.shape[1]
    return (lax.broadcasted_iota(jnp.int32, (REL_BUCKETS, n), 0) == ids_ref[...]).astype(BF16)


def _bias_gather(name, table_t, ids):
    nh = table_t.shape[0]

    def body(t_ref, ids_ref, o_ref):
        o_ref[...] = _dot_exact(t_ref[...], _bias_onehot(ids_ref), split="a")

    return pl.pallas_call(
        body, name=name, out_shape=jax.ShapeDtypeStruct((nh, ids.shape[1]), F32),
        compiler_params=pltpu.CompilerParams(vmem_limit_bytes=32 << 20),
    )(table_t, ids)


def _bias_scatter(name, dbias, ids):
    nh = dbias.shape[0]

    def body(d_ref, ids_ref, o_ref):
        o_ref[...] = _dot_exact(d_ref[...], _bias_onehot(ids_ref), 1, 1, split="a")

    return pl.pallas_call(
        body, name=name, out_shape=jax.ShapeDtypeStruct((nh, REL_BUCKETS), F32),
        compiler_params=pltpu.CompilerParams(vmem_limit_bytes=32 << 20),
    )(dbias, ids)


def _attn_valid(i, t):
    c = lax.broadcasted_iota(jnp.int32, (WINDOW, SPAN), 0)
    s = lax.broadcasted_iota(jnp.int32, (WINDOW, SPAN), 1)
    rel = s - WINDOW - c
    pos = i * WINDOW - WINDOW + s
    return (jnp.abs(rel) <= WINDOW) & (pos >= 0) & (pos < t)


def _attn_probs(qh, kh, bias_h, sink_h, valid):
    s = _dot(qh, kh, 1, 1) / math.sqrt(HEAD)
    s = jnp.where(valid, s + bias_h, NEG_INF)
    m = jnp.maximum(jnp.max(s, axis=-1, keepdims=True), sink_h)
    e = jnp.exp(s - m)
    es = jnp.exp(sink_h - m)
    inv = 1.0 / (jnp.sum(e, axis=-1, keepdims=True) + es)
    return e * inv, es * inv


def _attn_fwd(name, p, k_pad, v_pad, bias, sink, q_group_blk):
    t = p.shape[0]
    nh = bias.shape[0]
    aw = nh * HEAD
    grp = nh // KV_HEADS
    nb = t // WINDOW

    def body(q_ref, k_ref, v_ref, b_ref, s_ref, y_ref):
        i = pl.program_id(0)
        valid = _attn_valid(i, t)
        start = pl.multiple_of(i * WINDOW, WINDOW)
        ks = k_ref[pl.ds(start, SPAN), :]
        vs = v_ref[pl.ds(start, SPAN), :]
        for h in range(nh):
            kv = h // grp
            qh = q_ref[:, h * HEAD:(h + 1) * HEAD].astype(BF16)
            pr, _ = _attn_probs(qh, ks[:, kv * HEAD:(kv + 1) * HEAD], b_ref[h], s_ref[0:1, h:h + 1], valid)
            y_ref[:, h * HEAD:(h + 1) * HEAD] = _dot(pr.astype(BF16), vs[:, kv * HEAD:(kv + 1) * HEAD]).astype(BF16)

    full = lambda a: pl.BlockSpec(a.shape, lambda i: (0,) * a.ndim)
    return pl.pallas_call(
        body, name=name, grid=(nb,),
        in_specs=[pl.BlockSpec((WINDOW, aw), lambda i: (i, q_group_blk)), full(k_pad), full(v_pad), full(bias),
                  full(sink)],
        out_specs=pl.BlockSpec((WINDOW, aw), lambda i: (i, 0)),
        out_shape=jax.ShapeDtypeStruct((t, aw), BF16),
        compiler_params=_params(("parallel",), _nbytes(k_pad.shape, BF16) * 2 + _nbytes(bias.shape, F32)),
    )(p, k_pad, v_pad, bias, sink)


def _attn_bwd(name, p, k_pad, v_pad, bias, sink, dy, q_group_blk, dy_blk):
    t = p.shape[0]
    nh = bias.shape[0]
    aw = nh * HEAD
    grp = nh // KV_HEADS
    nb = t // WINDOW
    kvw = k_pad.shape[1]

    def body(q_ref, k_ref, v_ref, b_ref, s_ref, dy_ref, dq_ref, dk_ref, dv_ref, db_ref, ds_ref):
        i = pl.program_id(0)

        @pl.when(i == 0)
        def _():
            dk_ref[...] = jnp.zeros_like(dk_ref)
            dv_ref[...] = jnp.zeros_like(dv_ref)
            db_ref[...] = jnp.zeros_like(db_ref)
            ds_ref[...] = jnp.zeros_like(ds_ref)

        valid = _attn_valid(i, t)
        start = pl.multiple_of(i * WINDOW, WINDOW)
        ks = k_ref[pl.ds(start, SPAN), :]
        vs = v_ref[pl.ds(start, SPAN), :]
        inv_sqrt = 1.0 / math.sqrt(HEAD)
        for kv in range(KV_HEADS):
            kh = ks[:, kv * HEAD:(kv + 1) * HEAD]
            vh = vs[:, kv * HEAD:(kv + 1) * HEAD]
            dk_acc = jnp.zeros((SPAN, HEAD), F32)
            dv_acc = jnp.zeros((SPAN, HEAD), F32)
            for h in range(kv * grp, (kv + 1) * grp):
                qh = q_ref[:, h * HEAD:(h + 1) * HEAD].astype(BF16)
                pr, ps = _attn_probs(qh, kh, b_ref[h], s_ref[0:1, h:h + 1], valid)
                doh = dy_ref[:, h * HEAD:(h + 1) * HEAD].astype(BF16)
                dp = _dot(doh, vh, 1, 1)
                delta = jnp.sum(pr * dp, axis=-1, keepdims=True)
                dsc = pr * (dp - delta)
                db_ref[h] += dsc
                ds_ref[h:h + 1, :] += jnp.broadcast_to(jnp.sum(-ps * delta, axis=0, keepdims=True), (1, 128))
                dsr = (dsc * inv_sqrt).astype(BF16)
                dq_ref[:, h * HEAD:(h + 1) * HEAD] = _dot(dsr, kh)
                dk_acc += _dot(dsr, qh, 0, 0)
                dv_acc += _dot(pr.astype(BF16), doh, 0, 0)
            dk_ref[pl.ds(start, SPAN), kv * HEAD:(kv + 1) * HEAD] += dk_acc
            dv_ref[pl.ds(start, SPAN), kv * HEAD:(kv + 1) * HEAD] += dv_acc

    full = lambda a: pl.BlockSpec(a.shape, lambda i: (0,) * a.ndim)
    whole = lambda shape: pl.BlockSpec(shape, lambda i: (0,) * len(shape))
    pad_shape = (t + 2 * WINDOW, kvw)
    return pl.pallas_call(
        body, name=name, grid=(nb,),
        in_specs=[pl.BlockSpec((WINDOW, aw), lambda i: (i, q_group_blk)), full(k_pad), full(v_pad), full(bias),
                  full(sink), pl.BlockSpec((WINDOW, aw), lambda i: (i, dy_blk))],
        out_specs=[pl.BlockSpec((WINDOW, aw), lambda i: (i, 0)), whole(pad_shape), whole(pad_shape),
                   whole(bias.shape), whole((nh, 128))],
        out_shape=[jax.ShapeDtypeStruct((t, aw), F32), jax.ShapeDtypeStruct(pad_shape, F32),
                   jax.ShapeDtypeStruct(pad_shape, F32), jax.ShapeDtypeStruct(bias.shape, F32),
                   jax.ShapeDtypeStruct((nh, 128), F32)],
        compiler_params=_params(("arbitrary",), 3 * _nbytes(pad_shape, F32) + 2 * _nbytes(bias.shape, F32)),
    )(p, k_pad, v_pad, bias, sink, dy)


def _pad_kv(name, p, kv_blk, kvw):
    t = p.shape[0]
    nb = t // WINDOW

    def body(x_ref, o_ref):
        i = pl.program_id(0)
        inside = jnp.logical_and(i >= 1, i <= nb)
        o_ref[...] = jnp.where(inside, x_ref[...], 0.0).astype(BF16)

    return pl.pallas_call(
        body, name=name, grid=(nb + 2,),
        in_specs=[pl.BlockSpec((WINDOW, kvw), lambda i: (jnp.clip(i - 1, 0, nb - 1), kv_blk))],
        out_specs=pl.BlockSpec((WINDOW, kvw), lambda i: (i, 0)),
        out_shape=jax.ShapeDtypeStruct((t + 2 * WINDOW, kvw), BF16),
        compiler_params=_params(("parallel",), 1 << 20),
    )(p)


def _mix_dproj(name, pieces, kv_pads, t):
    hw = pieces[0][0].shape[1]
    kvw = kv_pads[0].shape[1]
    widths = [hw] * len(pieces) + [kvw] * len(kv_pads)
    total = sum(widths)
    tm = WINDOW
    flat = [a for pc in pieces for a in pc]

    def body(*refs):
        o_ref = refs[-1]
        pos, off = 0, 0
        for pc in pieces:
            val = refs[pos][...]
            for extra in range(1, len(pc)):
                val = val + refs[pos + extra][...]
            o_ref[:, off:off + hw] = val.astype(BF16)
            pos += len(pc)
            off += hw
        for _ in kv_pads:
            o_ref[:, off:off + kvw] = refs[pos][...].astype(BF16)
            pos += 1
            off += kvw

    in_specs = [pl.BlockSpec((tm, hw), lambda i: (i, 0)) for _ in flat]
    in_specs += [pl.BlockSpec((tm, kvw), lambda i: (i + 1, 0)) for _ in kv_pads]
    return pl.pallas_call(
        body, name=name, grid=(t // tm,), in_specs=in_specs,
        out_specs=pl.BlockSpec((tm, total), lambda i: (i, 0)),
        out_shape=jax.ShapeDtypeStruct((t, total), BF16),
        compiler_params=_params(("parallel",), 3 * _nbytes((tm, total), F32)),
    )(*flat, *kv_pads)


def _concat_cols(name, a, b):
    t, wa = a.shape
    wb = b.shape[1]
    tm = _tile(t, 512, 16)

    def body(a_ref, b_ref, o_ref):
        o_ref[:, :wa] = a_ref[...]
        o_ref[:, wa:] = b_ref[...]

    return pl.pallas_call(
        body, name=name, grid=(t // tm,),
        in_specs=[pl.BlockSpec((tm, wa), lambda i: (i, 0)), pl.BlockSpec((tm, wb), lambda i: (i, 0))],
        out_specs=pl.BlockSpec((tm, wa + wb), lambda i: (i, 0)),
        out_shape=jax.ShapeDtypeStruct((t, wa + wb), a.dtype),
        compiler_params=_params(("parallel",), 2 * _nbytes((tm, wa + wb), a.dtype)),
    )(a, b)


def _cast_into_full(name, w, geom, idx):
    r, c = w.shape
    tr = _tile(r, 256, 16)
    nr = r // tr
    if geom.col:
        place = lambda i, iref: (i, iref[0])
    else:
        place = lambda i, iref: (iref[0] * nr + i, 0)

    def body(i_ref, w_ref, o_ref):
        o_ref[...] = w_ref[...].astype(BF16)

    return pl.pallas_call(
        body, name=name,
        grid_spec=pltpu.PrefetchScalarGridSpec(
            num_scalar_prefetch=1, grid=(nr,),
            in_specs=[pl.BlockSpec((tr, c), lambda i, iref: (i, 0))],
            out_specs=pl.BlockSpec((tr, c), place)),
        out_shape=jax.ShapeDtypeStruct(geom.full_shape, BF16),
        compiler_params=_params(("parallel",), 2 * _nbytes((tr, c), F32)),
    )(idx, w)


def _adamw(name, w, g, m, v):
    r, c = w.shape
    tr = _tile(r, 128, 8)
    bc1 = 1.0 - ADAM_B1 ** ADAM_STEP
    bc2 = 1.0 - ADAM_B2 ** ADAM_STEP

    def body(w_ref, g_ref, m_ref, v_ref, d_ref, nm_ref, nv_ref):
        gv = g_ref[...]
        nm = ADAM_B1 * m_ref[...] + (1.0 - ADAM_B1) * gv
        nv = ADAM_B2 * v_ref[...] + (1.0 - ADAM_B2) * (gv * gv)
        nm_ref[...] = nm
        nv_ref[...] = nv
        d_ref[...] = -ADAM_LR * ((nm / bc1) / (jnp.sqrt(nv / bc2) + ADAM_EPS) + ADAM_WD * w_ref[...])

    blk = pl.BlockSpec((tr, c), lambda i: (i, 0))
    out = jax.ShapeDtypeStruct((r, c), F32)
    return pl.pallas_call(
        body, name=name, grid=(r // tr,), in_specs=[blk] * 4, out_specs=[blk] * 3, out_shape=[out] * 3,
        compiler_params=_params(("parallel",), 7 * _nbytes((tr, c), F32)),
    )(w, g, m, v)


def _mesh_pos():
    return lax.axis_index("x"), lax.axis_index("y"), lax.axis_index("c")


def _other_chips(x, y):
    return [(1 - x, y), (x, 1 - y), (1 - x, 1 - y)]


class _Big:
    def __init__(self, shard_shape, col_sharded):
        self.col = col_sharded
        r, c = shard_shape
        self.shard_shape = (r, c)
        self.full_shape = (r, N_CHIPS * c) if col_sharded else (N_CHIPS * r, c)
        self.half_shape = (r // 2, N_CHIPS * c) if col_sharded else (N_CHIPS * r, c // 2)
        self.shard_half_shape = (r // 2, c) if col_sharded else (r, c // 2)

    def region(self, ref, s, half=None):
        r, c = self.shard_shape
        if self.col:
            rows = slice(None) if half is None else pl.ds(half * (r // 2), r // 2)
            return ref.at[rows, pl.ds(s * c, c)]
        cols = slice(None) if half is None else pl.ds(half * (c // 2), c // 2)
        return ref.at[pl.ds(s * r, r), cols]

    def three_halves(self, ref, half):
        r, c = self.shard_shape
        if self.col:
            return ref.at[pl.ds(half * (r // 2), r // 2), pl.ds(0, 3 * c)]
        return ref.at[pl.ds(0, 3 * r), pl.ds(half * (c // 2), c // 2)]

    def half_of_full(self, ref, half):
        r, c = self.full_shape
        if self.col:
            return ref.at[pl.ds(half * (r // 2), r // 2), :]
        return ref.at[:, pl.ds(half * (c // 2), c // 2)]

    def half_of_shard(self, ref, half):
        r, c = self.shard_shape
        if self.col:
            return ref.at[pl.ds(half * (r // 2), r // 2), :]
        return ref.at[:, pl.ds(half * (c // 2), c // 2)]

    def shard_of_half(self, ref, s):
        r, c = self.shard_shape
        if self.col:
            return ref.at[:, pl.ds(s * c, c)]
        return ref.at[pl.ds(s * r, r), :]


HBM =pl.BlockSpec(memory_space=pltpu.HBM)
SEM = pl.BlockSpec(memory_space=pltpu.SEMAPHORE)
SPLIT_COPY = pltpu.CompilerParams(has_side_effects=pltpu.SideEffectType.DATAFLOW_SIDE_EFFECTING)


def _in_hbm(a):
    return pltpu.with_memory_space_constraint(a, pltpu.HBM)


def _gather_start(fulls, geoms):
    nw = len(fulls)

    def body(*refs):
        dst = refs[nw:2 * nw]
        sems = refs[2 * nw:]
        x, y, c = _mesh_pos()
        mine = 2 * x + y
        for w in range(nw):
            own_half = geoms[w].region(dst[w], mine, c)
            for chip in _other_chips(x, y):
                pltpu.make_async_remote_copy(src_ref=own_half, dst_ref=own_half, send_sem=sems[2 * w],
                                             recv_sem=sems[2 * w + 1], device_id=(*chip, c),
                                             device_id_type=MESH).start()

    out = pl.pallas_call(
        body, name="gather_start", in_specs=[HBM] * nw, out_specs=[HBM] * nw + [SEM] * (2 * nw),
        out_shape=[pltpu.HBM(g.full_shape, BF16) for g in geoms] + [pltpu.SemaphoreType.DMA(())] * (2 * nw),
        input_output_aliases={w: w for w in range(nw)}, compiler_params=SPLIT_COPY,
    )(*[_in_hbm(a) for a in fulls])
    return out[:nw], [(out[nw + 2 * w], out[nw + 2 * w + 1]) for w in range(nw)]


def _wait_three(geom, ref, half, send_sem, recv_sem, peer, recv):
    three = geom.three_halves(ref, half)
    copy = pltpu.make_async_remote_copy(src_ref=three, dst_ref=three, send_sem=send_sem, recv_sem=recv_sem,
                                        device_id=peer, device_id_type=MESH)
    if recv:
        copy.wait_recv()
    else:
        copy.wait_send()


def _gather_forward(name, full, geom, sems, after):
    def body(w_in, send_sem, recv_sem, after_ref, w_ref, fwd_send, fwd_recv):
        x, y, c = _mesh_pos()
        sibling = (x, y, 1 - c)
        _wait_three(geom, w_ref, c, send_sem, recv_sem, sibling, recv=True)
        for chip in _other_chips(x, y):
            landed = geom.region(w_ref, 2 * chip[0] + chip[1], c)
            pltpu.make_async_remote_copy(src_ref=landed, dst_ref=landed, send_sem=fwd_send, recv_sem=fwd_recv,
                                         device_id=sibling, device_id_type=MESH).start()
        _wait_three(geom, w_ref, c, send_sem, recv_sem, sibling, recv=False)

    sem = pltpu.SemaphoreType.DMA(())
    out = pl.pallas_call(
        body, name=name, in_specs=[HBM, SEM, SEM, pl.BlockSpec(memory_space=pl.ANY)], out_specs=[HBM, SEM, SEM],
        out_shape=[pltpu.HBM(geom.full_shape, BF16), sem, sem],
        input_output_aliases={0: 0}, compiler_params=SPLIT_COPY,
    )(full, sems[0], sems[1], after)
    return out[0], (out[1], out[2])


def _gather_end(name, full, geom, sems, after):
    def body(w_in, fwd_send, fwd_recv, after_ref, w_ref):
        x, y, c = _mesh_pos()
        sibling = (x, y, 1 - c)
        _wait_three(geom, w_ref, 1 - c, fwd_send, fwd_recv, sibling, recv=True)
        _wait_three(geom, w_ref, c, fwd_send, fwd_recv, sibling, recv=False)

    return pl.pallas_call(
        body, name=name, in_specs=[HBM, SEM, SEM, pl.BlockSpec(memory_space=pl.ANY)], out_specs=HBM,
        out_shape=pltpu.HBM(geom.full_shape, BF16),
        input_output_aliases={0: 0}, compiler_params=SPLIT_COPY,
    )(full, sems[0], sems[1], after)


def _split_copy_call(name, arrays, fn, sems=(), after=None, new_sems=0):
    n, ns = len(arrays), len(sems)
    n_in = n + ns + (after is not None)

    def body(*refs):
        fn(refs[n_in:n_in + n], refs[n:n + ns], refs[n_in + n:])

    ins = list(arrays) if ns else [_in_hbm(a) for a in arrays]
    ins += list(sems) + ([after] if after is not None else [])
    in_specs = [HBM] * n + [SEM] * ns + ([pl.BlockSpec(memory_space=pl.ANY)] if after is not None else [])
    out = pl.pallas_call(
        body, name=name, in_specs=in_specs, out_specs=[HBM] * n + [SEM] * new_sems,
        out_shape=[pltpu.HBM(a.shape, a.dtype) for a in arrays] + [pltpu.SemaphoreType.DMA(())] * new_sems,
        input_output_aliases={i: i for i in range(n)}, compiler_params=SPLIT_COPY,
    )(*ins)
    return list(out[:n]), tuple(out[n:])


def _remote(src, dst, sems, to):
    return pltpu.make_async_remote_copy(src_ref=src, dst_ref=dst, send_sem=sems[0], recv_sem=sems[1],
                                        device_id=to, device_id_type=MESH)


class _GradReduce:
    def __init__(self, name, geom, idx, c_idx):
        self.name, self.geom, self.idx, self.c_idx = name, geom, idx, c_idx

    def pair_start(self, dw):
        g = self.geom

        def start(refs, _, new):
            x, y, c = _mesh_pos()
            _remote(g.half_of_full(refs[0], 1 - c), refs[1], new, (x, y, 1 - c)).start()

        self.arrays, self.sems = _split_copy_call(f"pair_start_{self.name}", [dw, lax.empty(g.half_shape, BF16)],
                                                  start, new_sems=2)

    def pair_finish(self, after):
        g = self.geom

        def wait(refs, sems, _):
            x, y, c = _mesh_pos()
            copy = _remote(g.half_of_full(refs[0], 1 - c), refs[1], sems, (x, y, 1 - c))
            copy.wait_send()
            copy.wait_recv()

        (dw, landed), _ = _split_copy_call(f"pair_wait_{self.name}", self.arrays, wait, self.sems, after)
        half = _pair_add(f"pair_add_{self.name}", dw, landed, g, self.c_idx)

        def start(refs, _, new):
            x, y, c = _mesh_pos()
            for k, chip in enumerate(_other_chips(x, y)):
                _remote(g.shard_of_half(refs[0], 2 * chip[0] + chip[1]), refs[1].at[k], new, (*chip, c)).start()

        self.arrays, self.sems = _split_copy_call(
            f"chip_start_{self.name}", [half, lax.empty((3,) + g.shard_half_shape, BF16)], start, new_sems=2)

    def chip_finish(self, after):
        g = self.geom

        def wait(refs, sems, _):
            x, y, c = _mesh_pos()
            three = _remote(refs[1], refs[1], sems, (x, y, 1 - c))
            three.wait_send()
            three.wait_recv()

        (half, landed), _ = _split_copy_call(f"chip_wait_{self.name}", self.arrays, wait, self.sems, after)
        quarter = _chip_add(f"chip_add_{self.name}", half, landed, g, self.idx)

        def start(refs, _, new):
            x, y, c = _mesh_pos()
            own = g.half_of_shard(refs[0], c)
            _remote(own, own, new, (x, y, 1 - c)).start()

        self.arrays, self.sems = _split_copy_call(f"share_start_{self.name}", [quarter], start, new_sems=2)

    def finish(self, after):
        g = self.geom

        def wait(refs, sems, _):
            x, y, c = _mesh_pos()
            own, theirs = g.half_of_shard(refs[0], c), g.half_of_shard(refs[0], 1 - c)
            _remote(own, own, sems, (x, y, 1 - c)).wait_send()
            _remote(theirs, theirs, sems, (x, y, 1 - c)).wait_recv()

        (quarter,), _ = _split_copy_call(f"share_wait_{self.name}", self.arrays, wait, self.sems, after)
        return quarter


def _pair_add(name, grad, recv, geom, c_idx):
    r, c = geom.half_shape
    tr, tc = _tile(r, 256, 16), _tile(c, 2048, 128)
    nr, ncol = r // tr, c // tc
    if geom.col:
        mine = lambda i, j, cref: (cref[0] * nr + i, j)
    else:
        mine = lambda i, j, cref: (i, cref[0] * ncol + j)

    def body(c_ref, g_ref, r_ref, o_ref):
        o_ref[...] = (g_ref[...].astype(F32) + r_ref[...].astype(F32)).astype(BF16)

    return pl.pallas_call(
        body, name=name,
        grid_spec=pltpu.PrefetchScalarGridSpec(
            num_scalar_prefetch=1, grid=(nr, ncol),
            in_specs=[pl.BlockSpec((tr, tc), mine), pl.BlockSpec((tr, tc), lambda i, j, cref: (i, j))],
            out_specs=pl.BlockSpec((tr, tc), lambda i, j, cref: (i, j))),
        out_shape=jax.ShapeDtypeStruct((r, c), BF16),
        compiler_params=_params(("parallel", "parallel"), 3 * _nbytes((tr, tc), F32)),
    )(c_idx, grad, recv)


def _chip_add(name, half, recv, geom, idx):
    r, c = geom.shard_half_shape
    tr, tc = _tile(r, 256, 16), _tile(c, 2048, 128)
    nr, ncol = r // tr, c // tc
    if geom.col:
        mine = lambda i, j, iref: (i, iref[0] * ncol + j)
        place = lambda i, j, iref: (iref[1] * nr + i, j)
    else:
        mine = lambda i, j, iref: (iref[0] * nr + i, j)
        place = lambda i, j, iref: (i, iref[1] * ncol + j)

    def body(i_ref, h_ref, r_ref, o_ref):
        acc = h_ref[...].astype(F32)
        for k in range(3):
            acc = acc + r_ref[k].astype(F32)
        o_ref[...] = acc

    return pl.pallas_call(
        body, name=name,
        grid_spec=pltpu.PrefetchScalarGridSpec(
            num_scalar_prefetch=1, grid=(nr, ncol),
            in_specs=[pl.BlockSpec((tr, tc), mine), pl.BlockSpec((3, tr, tc), lambda i, j, iref: (0, i, j))],
            out_specs=pl.BlockSpec((tr, tc), place)),
        out_shape=jax.ShapeDtypeStruct(geom.shard_shape, F32),
        compiler_params=_params(("parallel", "parallel"), 4 * _nbytes((tr, tc), F32)),
    )(idx, half, recv)


def _all_reduce_small(pack):
    r, d = pack.shape

    def body(p_ref, o_ref, slots, send_sems, recv_sems):
        x, y, c = _mesh_pos()
        me = 4 * x + 2 * y + c
        slots[me] = p_ref[...]
        copies = []
        for k in range(1, N_DEV):
            px, py, pc = x ^ ((k >> 2) & 1), y ^ ((k >> 1) & 1), c ^ (k & 1)
            copies.append(pltpu.make_async_remote_copy(
                src_ref=p_ref, dst_ref=slots.at[me], send_sem=send_sems.at[k - 1], recv_sem=recv_sems.at[k - 1],
                device_id=(px, py, pc), device_id_type=MESH))
        for cp in copies:
            cp.start()
        for k in range(1, N_DEV):
            peer = 4 * (x ^ ((k >> 2) & 1)) + 2 * (y ^ ((k >> 1) & 1)) + (c ^ (k & 1))
            pltpu.make_async_remote_copy(
                src_ref=p_ref, dst_ref=slots.at[peer], send_sem=send_sems.at[k - 1], recv_sem=recv_sems.at[k - 1],
                device_id=(x, y, c), device_id_type=MESH).wait_recv()
        for cp in copies:
            cp.wait_send()
        acc = slots[0]
        for k in range(1, N_DEV):
            acc = acc + slots[k]
        o_ref[...] = acc

    vm = pl.BlockSpec(memory_space=pltpu.VMEM)
    return pl.pallas_call(
        body, name="all_reduce_small", in_specs=[vm], out_specs=vm,
        out_shape=jax.ShapeDtypeStruct((r, d), F32),
        scratch_shapes=[pltpu.VMEM((N_DEV, r, d), F32), pltpu.SemaphoreType.DMA((N_DEV - 1,)),
                        pltpu.SemaphoreType.DMA((N_DEV - 1,))],
    )(pack)


def _pack_rows(rows, d):
    out = []
    for a in rows:
        flat = a.reshape(-1)
        n = -(-flat.shape[0] // d) * d
        out.append(jnp.pad(flat, (0, n - flat.shape[0])).reshape(-1, d))
    packed = jnp.concatenate(out, axis=0)
    return jnp.pad(packed, ((0, 16 - packed.shape[0]), (0, 0)))


def _unpack_rows(packed, shapes, d):
    out, row = [], 0
    for shp in shapes:
        n = int(np.prod(shp))
        nrows = -(-n // d)
        out.append(packed[row:row + nrows].reshape(-1)[:n].reshape(shp))
        row += nrows
    return out


def kernel(x, pre_norm_ffn1, post_norm_ffn1, w_ffn1_gate_up, w_ffn1_down, pre_norm_mix, post_norm_mix, w_mix_in, hgrn_lower_bounds_fwd, hgrn_lower_bounds_bwd, hgrn_out_norm, attn_sink, w_mix_out, pre_norm_ffn2, post_norm_ffn2, w_ffn2_gate_up, w_ffn2_down, rel_bias_table, loss_target, m_pre_norm_ffn1, m_post_norm_ffn1, m_w_ffn1_gate_up, m_w_ffn1_down, m_pre_norm_mix, m_post_norm_mix, m_w_mix_in, m_hgrn_lower_bounds_fwd, m_hgrn_lower_bounds_bwd, m_hgrn_out_norm, m_attn_sink, m_w_mix_out, m_pre_norm_ffn2, m_post_norm_ffn2, m_w_ffn2_gate_up, m_w_ffn2_down, m_rel_bias_table, v_pre_norm_ffn1, v_post_norm_ffn1, v_w_ffn1_gate_up, v_w_ffn1_down, v_pre_norm_mix, v_post_norm_mix, v_w_mix_in, v_hgrn_lower_bounds_fwd, v_hgrn_lower_bounds_bwd, v_hgrn_out_norm, v_attn_sink, v_w_mix_out, v_pre_norm_ffn2, v_post_norm_ffn2, v_w_ffn2_gate_up, v_w_ffn2_down, v_rel_bias_table):
    t, d = x.shape[1], x.shape[2]
    hw = hgrn_out_norm.shape[1]
    aw = d - hw
    nah = aw // HEAD
    kvw = KV_HEADS * HEAD
    x0 = x[0]
    target = loss_target[0]

    big_names = ["w_ffn1_gate_up", "w_ffn1_down", "w_mix_in", "w_mix_out", "w_ffn2_gate_up", "w_ffn2_down"]
    big_w = [w_ffn1_gate_up[0], w_ffn1_down[0], w_mix_in[0], w_mix_out[0], w_ffn2_gate_up[0], w_ffn2_down[0]]
    big_m = [m_w_ffn1_gate_up[0], m_w_ffn1_down[0], m_w_mix_in[0], m_w_mix_out[0], m_w_ffn2_gate_up[0],
             m_w_ffn2_down[0]]
    big_v = [v_w_ffn1_gate_up[0], v_w_ffn1_down[0], v_w_mix_in[0], v_w_mix_out[0], v_w_ffn2_gate_up[0],
             v_w_ffn2_down[0]]
    col_sharded = [True, False, True, False, True, False]
    geoms = [_Big(w.shape, cs) for w, cs in zip(big_w, col_sharded)]

    cx, cy, cc = _mesh_pos()
    idx = jnp.stack([2 * cx + cy, cc]).astype(jnp.int32)
    c_idx = jnp.reshape(cc, (1,)).astype(jnp.int32)
    own_quarters = [_cast_into_full(f"cast_{n}", w, gm, idx) for n, w, gm in zip(big_names, big_w, geoms)]
    started, gather_sems = _gather_start(own_quarters, geoms)

    def forward_weight(w, after):
        return _gather_forward(f"gather_forward_{big_names[w]}", started[w], geoms[w], gather_sems[w], after)

    def whole_weight(w, forwarded, after):
        return _gather_end(f"gather_end_{big_names[w]}", forwarded[0], geoms[w], forwarded[1], after)

    h1 = _norm_fwd("ffn1_pre_norm", x0, pre_norm_ffn1)
    w_gu1 = whole_weight(0, forward_weight(0, h1), h1)
    gu1 = _mm("ffn1_gate_up", h1, w_gu1, "nn", BF16)
    fw = forward_weight(1, gu1)
    act1 = _swiglu_fwd("ffn1_act", gu1)
    w_d1 = whole_weight(1, fw, act1)
    ff1 = _mm("ffn1_down", act1, w_d1, "nn", F32)
    fw = forward_weight(2, ff1)
    x1, hm = _resid_norm_fwd("ffn1_residual", x0, ff1, post_norm_ffn1, pre_norm_mix, 0.5)
    w_in = whole_weight(2, fw, hm)
    p = _mm("mix_in", hm, w_in, "nn", F32)
    fw = forward_weight(3, p)
    o_f, o_b, st_f, st_b = _hgrn_scan_fwd("hgrn_scan", p, hgrn_lower_bounds_fwd, hgrn_lower_bounds_bwd)
    y_h = _hgrn_out_fwd("hgrn_out", o_f, o_b, p, hgrn_out_norm, 4)
    kv_blk0 = (5 * hw + aw) // kvw
    k_pad = _pad_kv("attn_pad_k", p, kv_blk0, kvw)
    v_pad = _pad_kv("attn_pad_v", p, kv_blk0 + 1, kvw)
    bucket_ids = _t5_bucket_ids()
    bias = _bias_gather("attn_bias", rel_bias_table.T, bucket_ids).reshape(nah, WINDOW, SPAN)
    y_a = _attn_fwd("attn_fwd", p, k_pad, v_pad, bias, attn_sink, 5 * hw // aw)
    y_mix = _concat_cols("mix_concat", y_h, y_a)
    w_out = whole_weight(3, fw, y_mix)
    mixed = _mm("mix_out", y_mix, w_out, "nn", F32)
    fw = forward_weight(4, mixed)
    x2, h2 = _resid_norm_fwd("mix_residual", x1, mixed, post_norm_mix, pre_norm_ffn2, 1.0)
    w_gu2 = whole_weight(4, fw, h2)
    gu2 = _mm("ffn2_gate_up", h2, w_gu2, "nn", BF16)
    fw = forward_weight(5, gu2)
    act2 = _swiglu_fwd("ffn2_act", gu2)
    w_d2 = whole_weight(5, fw, act2)
    ff2 = _mm("ffn2_down", act2, w_d2, "nn", F32)
    loss_blk, dy, dff2, dg_post2 = _final_fwd_bwd("ffn2_residual_loss", x2, ff2, post_norm_ffn2, target, 0.5)

    reduce = [_GradReduce(n, gm, idx, c_idx) for n, gm in zip(big_names, geoms)]
    big_grads, big_delta, big_new_m, big_new_v = [None] * 6, [None] * 6, [None] * 6, [None] * 6

    def update(w, after):
        g = reduce[w].finish(after)
        dl, nm, nv = _adamw(f"adamw_{big_names[w]}", big_w[w], g, big_m[w], big_v[w])
        big_grads[w], big_delta[w], big_new_m[w], big_new_v[w] = g[None], dl[None], nm[None], nv[None]
        return dl

    da2 = _mm("ffn2_dact", dff2, w_d2, "nt", F32)
    dw_d2 = _mm("ffn2_dw_down", act2, dff2, "tn", BF16)
    reduce[5].pair_start(dw_d2)
    dgu2 = _swiglu_bwd("ffn2_dact_bwd", da2, gu2)
    reduce[5].pair_finish(dgu2)
    dw_gu2 = _ffn_dw_gate_up("ffn2_dw_gate_up", h2, dgu2)
    reduce[4].pair_start(dw_gu2)
    dh2 = _ffn_dh("ffn2_dh", dgu2, w_gu2)
    reduce[4].pair_finish(dh2)
    dx2, dg_pre2, dmixed, dg_postm = _norms_bwd("mix_residual_bwd", dy, dh2, x2, pre_norm_ffn2,
                                                post=(mixed, post_norm_mix, 1.0))
    dw_out = _mm("mix_out_dw", y_mix, dmixed, "tn", BF16)
    reduce[3].pair_start(dw_out)
    dy_mix = _mm("mix_out_dx", dmixed, w_out, "nt", F32)
    reduce[3].pair_finish(dy_mix)
    dq_a, dk_pad, dv_pad, dbias, dsink = _attn_bwd("attn_bwd", p, k_pad, v_pad, bias, attn_sink, dy_mix,
                                                   5 * hw // aw, hw // aw)
    reduce[5].chip_finish(dq_a)
    drel_t = _bias_scatter("attn_dbias", dbias.reshape(nah, WINDOW * SPAN), bucket_ids)
    do, dg_h, dgain = _hgrn_out_bwd("hgrn_out_bwd", dy_mix, o_f, o_b, p, hgrn_out_norm, 4)
    dq_f, dv_f, dz_f, dlb_f, dq_b, dv_b, dz_b, dlb_b = _hgrn_scan_bwd(
        "hgrn_scan_bwd", p, hgrn_lower_bounds_fwd, hgrn_lower_bounds_bwd, do, st_f, st_b)
    reduce[4].chip_finish(dq_f)
    done = update(5, dq_f)
    reduce[3].chip_finish(done)
    dp = _mix_dproj("mix_dproj", [(dq_f, dq_b), (dv_f, dv_b), (dz_f,), (dz_b,), (dg_h,), (dq_a,)],
                    [dk_pad, dv_pad], t)
    dw_in = _mm("mix_in_dw", hm, dp, "tn", BF16)
    reduce[2].pair_start(dw_in)
    dhm = _mm("mix_in_dx", dp, w_in, "nt", F32)
    reduce[2].pair_finish(dhm)
    done = update(4, dhm)
    done = update(3, done)
    dx1, dg_prem, dff1, dg_post1 = _norms_bwd("ffn1_residual_bwd", dx2, dhm, x1, pre_norm_mix,
                                              post=(ff1, post_norm_ffn1, 0.5))
    da1 = _mm("ffn1_dact", dff1, w_d1, "nt", F32)
    dw_d1 = _mm("ffn1_dw_down", act1, dff1, "tn", BF16)
    reduce[1].pair_start(dw_d1)
    dgu1 = _swiglu_bwd("ffn1_dact_bwd", da1, gu1)
    reduce[1].pair_finish(dgu1)
    reduce[2].chip_finish(dgu1)
    dw_gu1 = _ffn_dw_gate_up("ffn1_dw_gate_up", h1, dgu1)
    reduce[0].pair_start(dw_gu1)
    done = update(2, dw_gu1)
    reduce[0].pair_finish(done)
    dh1 = _ffn_dh("ffn1_dh", dgu1, w_gu1)
    grad_x, dg_pre1 = _norms_bwd("ffn1_pre_norm_bwd", dx1, dh1, x0, pre_norm_ffn1)

    small_w = [pre_norm_ffn1, post_norm_ffn1, pre_norm_mix, post_norm_mix, hgrn_lower_bounds_fwd,
               hgrn_lower_bounds_bwd, hgrn_out_norm, attn_sink, pre_norm_ffn2, post_norm_ffn2, rel_bias_table]
    small_m = [m_pre_norm_ffn1, m_post_norm_ffn1, m_pre_norm_mix, m_post_norm_mix, m_hgrn_lower_bounds_fwd,
               m_hgrn_lower_bounds_bwd, m_hgrn_out_norm, m_attn_sink, m_pre_norm_ffn2, m_post_norm_ffn2,
               m_rel_bias_table]
    small_v = [v_pre_norm_ffn1, v_post_norm_ffn1, v_pre_norm_mix, v_post_norm_mix, v_hgrn_lower_bounds_fwd,
               v_hgrn_lower_bounds_bwd, v_hgrn_out_norm, v_attn_sink, v_pre_norm_ffn2, v_post_norm_ffn2,
               v_rel_bias_table]
    small_g = [dg_pre1, dg_post1, dg_prem, dg_postm, dlb_f, dlb_b, dgain, dsink[:, 0].reshape(1, nah), dg_pre2,
               dg_post2, drel_t.T]
    shapes = [a.shape for a in small_w]
    summed = _all_reduce_small(_pack_rows(small_g + [loss_blk[0:1, 0:1]], d))
    g_pack = summed
    loss =_unpack_rows(summed, shapes + [(1, 1)], d)[-1][0, 0]
    sd, sm, sv = _adamw("adamw_small", _pack_rows(small_w, d), g_pack, _pack_rows(small_m, d), _pack_rows(small_v, d))
    small_grads = _unpack_rows(g_pack, shapes, d)
    small_delta, small_new_m, small_new_v = (_unpack_rows(a, shapes, d) for a in (sd, sm, sv))

    reduce[1].chip_finish(sd)
    done = update(1, sd)
    reduce[0].chip_finish(done)
    update(0, done)

    def ordered(small, big):
        s = dict(zip(["pre1", "post1", "prem", "postm", "lbf", "lbb", "gain", "sink", "pre2", "post2", "rel"], small))
        b = dict(zip(["gu1", "d1", "win", "wout", "gu2", "d2"], big))
        return [s["pre1"], s["post1"], b["gu1"], b["d1"], s["prem"], s["postm"], b["win"], s["lbf"], s["lbb"],
                s["gain"], s["sink"], b["wout"], s["pre2"], s["post2"], b["gu2"], b["d2"], s["rel"]]

    return (loss, grad_x[None], *ordered(small_grads, big_grads), *ordered(small_delta, big_delta),
            *ordered(small_new_m, big_new_m), *ordered(small_new_v, big_new_v))
```

```python
import functools
import math

import jax
import jax.numpy as jnp
import numpy as np
from jax import lax
from jax.experimental import pallas as pl
from jax.experimental.pallas import tpu as pltpu

F32 = jnp.float32
BF16 = jnp.bfloat16

HEAD = 128
CHUNK = 64
WINDOW = 128
SPAN = 3 * WINDOW
KV_HEADS = 2
REL_BUCKETS = 32
REL_MAX_DIST = 128
EPS = 1e-6
NEG_INF = -1e30

ADAM_LR = 0.001
ADAM_B1 = 0.9
ADAM_B2 = 0.999
ADAM_EPS = 1e-08
ADAM_WD = 0.01
ADAM_STEP = 10

N_CHIPS = 4
N_DEV = 8
V7X_VMEM_BYTES = 64 * 1024 * 1024
MESH = pl.DeviceIdType.MESH
ANY = pl.BlockSpec(memory_space=pl.ANY)


def _tile(n, pref, mult):
    t = (min(pref, n) // mult) * mult
    while t >= mult:
        if n % t == 0:
            return t
        t -= mult
    return n


def _params(semantics, block_bytes):
    limit = min(V7X_VMEM_BYTES - (4 << 20), 2 * int(block_bytes) + (8 << 20))
    return pltpu.CompilerParams(dimension_semantics=semantics, vmem_limit_bytes=limit)


def _nbytes(shape, dtype):
    return int(np.prod(shape)) * jnp.dtype(dtype).itemsize


def _dot(a, b, ca=1, cb=0):
    return lax.dot_general(a, b, (((ca,), (cb,)), ((), ())), preferred_element_type=F32)


def _split3(x):
    hi = x.astype(BF16)
    r1 = x - hi.astype(F32)
    mid = r1.astype(BF16)
    lo = (r1 - mid.astype(F32)).astype(BF16)
    return hi, mid, lo


def _dot_exact(a, b, ca=1, cb=0, split="b"):
    if split == "b":
        return sum(_dot(a, p, ca, cb) for p in _split3(b))
    return sum(_dot(p, b, ca, cb) for p in _split3(a))


def _rms(x):
    return lax.rsqrt(jnp.mean(x * x, axis=-1, keepdims=True) + EPS)


def _norm_bwd(u, x, gain):
    r = _rms(x)
    xhat = x * r
    dgain = jnp.sum(u * xhat, axis=0, keepdims=True)
    v = u * gain
    dx = r * (v - xhat * jnp.mean(v * xhat, axis=-1, keepdims=True))
    return dx, dgain


def _sigmoid(x):
    return 1.0 / (1.0 + jnp.exp(-x))


def _accumulate(ref, val, first):
    @pl.when(first)
    def _():
        ref[...] = val

    @pl.when(jnp.logical_not(first))
    def _():
        ref[...] += val


def _ordered(body, ins, in_specs, after):
    if after is None:
        return body, list(ins), list(in_specs)
    n_in = len(ins)

    def wrapped(*refs):
        body(*refs[:n_in], *refs[n_in + 1:])

    return wrapped, list(ins) + [after], list(in_specs) + [pl.BlockSpec(memory_space=pl.ANY)]


def _matmul(name, a, b, *, form, out_dtype, tm, tn, tk, a_map=None, b_map=None,
            out_shape=None, out_block=None, out_map=None, sizes=None, after=None):
    if sizes is None:
        if form == "nn":
            (m, k), n = a.shape, b.shape[1]
        elif form == "nt":
            (m, k), n = a.shape, b.shape[0]
        else:
            (k, m), n = a.shape, b.shape[1]
    else:
        m, n, k = sizes
    gi, gj, gk = m // tm, n // tn, k // tk
    a_blk = (tm, tk) if form != "tn" else (tk, tm)
    b_blk = (tk, tn) if form != "nt" else (tn, tk)
    if a_map is None:
        a_map = (lambda i, j, kk: (i, kk)) if form != "tn" else (lambda i, j, kk: (kk, i))
    else:
        a_blk = (None,) + a_blk
    if b_map is None:
        b_map = (lambda i, j, kk: (kk, j)) if form != "nt" else (lambda i, j, kk: (j, kk))
    else:
        b_blk = (None,) + b_blk
    if out_shape is None:
        out_shape, out_block, out_map = (m, n), (tm, tn), (lambda i, j, kk: (i, j))
    ca, cb = {"nn": (1, 0), "nt": (1, 1), "tn": (0, 0)}[form]

    def body(a_ref, b_ref, o_ref, *acc):
        part = _dot(a_ref[...], b_ref[...], ca, cb)
        if gk == 1:
            o_ref[...] = part.astype(o_ref.dtype)
        else:
            kk = pl.program_id(2)
            _accumulate(acc[0], part, kk == 0)

            @pl.when(kk == gk - 1)
            def _():
                o_ref[...] = acc[0][...].astype(o_ref.dtype)

    scratch = [] if gk == 1 else [pltpu.VMEM((tm, tn), F32)]
    vmem = (_nbytes((tm, tk), a.dtype) + _nbytes((tk, tn), b.dtype) + _nbytes((tm, tn), out_dtype)
            + 2 * _nbytes((tm, tn), F32))
    body, ins, in_specs = _ordered(body, [a, b], [pl.BlockSpec(a_blk, a_map), pl.BlockSpec(b_blk, b_map)], after)
    return pl.pallas_call(
        body, name=name, grid=(gi, gj, gk), in_specs=in_specs,
        out_specs=pl.BlockSpec(out_block, out_map),
        out_shape=jax.ShapeDtypeStruct(out_shape, out_dtype),
        scratch_shapes=scratch,
        compiler_params=_params(("parallel", "parallel", "arbitrary"), vmem),
    )(*ins)


def _mm_tiles(m, n, k):
    return _tile(m, 1024, 128), _tile(n, 512, 128), _tile(k, 2816, 128)


def _mm(name, a, b, form, out_dtype, after=None):
    if form == "nn":
        m, k, n = a.shape[0], a.shape[1], b.shape[1]
    elif form == "nt":
        m, k, n = a.shape[0], a.shape[1], b.shape[0]
    else:
        m, k, n = a.shape[1], a.shape[0], b.shape[1]
    tm, tn, tk = _mm_tiles(m, n, k)
    return _matmul(name, a, b, form=form, out_dtype=out_dtype, tm=tm, tn=tn, tk=tk, after=after)


def _row_tile(t):
    return _tile(t, 256, 8)


def _norm_fwd(name, x, gain):
    t, d = x.shape
    tm = _row_tile(t)

    def body(x_ref, g_ref, h_ref):
        xv = x_ref[...]
        h_ref[...] = (xv * _rms(xv) * g_ref[...]).astype(BF16)

    row = pl.BlockSpec((tm, d), lambda i: (i, 0))
    vec = pl.BlockSpec((1, d), lambda i: (0, 0))
    return pl.pallas_call(
        body, name=name, grid=(t // tm,), in_specs=[row, vec], out_specs=row,
        out_shape=jax.ShapeDtypeStruct((t, d), BF16),
        compiler_params=_params(("parallel",), 2 * _nbytes((tm, d), F32)),
    )(x, gain)


def _resid_norm_fwd(name, xres, ff, gpost, gpre, scale):
    t, d = xres.shape
    tm = _row_tile(t)

    def body(x_ref, f_ref, gp_ref, gn_ref, xn_ref, h_ref):
        f = f_ref[...]
        xn = x_ref[...] + scale * (f * _rms(f) * gp_ref[...])
        xn_ref[...] = xn
        h_ref[...] = (xn * _rms(xn) * gn_ref[...]).astype(BF16)

    row = pl.BlockSpec((tm, d), lambda i: (i, 0))
    vec = pl.BlockSpec((1, d), lambda i: (0, 0))
    return pl.pallas_call(
        body, name=name, grid=(t // tm,), in_specs=[row, row, vec, vec], out_specs=[row, row],
        out_shape=[jax.ShapeDtypeStruct((t, d), F32), jax.ShapeDtypeStruct((t, d), BF16)],
        compiler_params=_params(("parallel",), 4 * _nbytes((tm, d), F32)),
    )(xres, ff, gpost, gpre)


def _final_fwd_bwd(name, xres, ff, gpost, target, scale):
    t, d = xres.shape
    tm = _row_tile(t)

    def body(x_ref, f_ref, gp_ref, t_ref, loss_ref, dy_ref, dff_ref, dg_ref):
        i = pl.program_id(0)
        f = f_ref[...]
        gp = gp_ref[...]
        y = x_ref[...] + scale * (f * _rms(f) * gp)
        err = y - t_ref[...]
        part = 0.5 * jnp.sum(jnp.mean(err * err, axis=-1, keepdims=True), axis=0, keepdims=True)
        _accumulate(loss_ref, jnp.broadcast_to(part, loss_ref.shape), i == 0)
        dy = err / d
        dy_ref[...] = dy
        dff, dg = _norm_bwd(scale * dy, f, gp)
        dff_ref[...] = dff.astype(BF16)
        _accumulate(dg_ref, dg, i == 0)

    row = pl.BlockSpec((tm, d), lambda i: (i, 0))
    vec = pl.BlockSpec((1, d), lambda i: (0, 0))
    return pl.pallas_call(
        body, name=name, grid=(t // tm,), in_specs=[row, row, vec, row],
        out_specs=[pl.BlockSpec((8, 128), lambda i: (0, 0)), row, row, vec],
        out_shape=[jax.ShapeDtypeStruct((8, 128), F32), jax.ShapeDtypeStruct((t, d), F32),
                   jax.ShapeDtypeStruct((t, d), BF16), jax.ShapeDtypeStruct((1, d), F32)],
        compiler_params=_params(("arbitrary",), 5 * _nbytes((tm, d), F32)),
    )(xres, ff, gpost, target)


def _norms_bwd(name, dres, dh, xin, gpre, post=None, after=None):
    t, d = dres.shape
    tm = _row_tile(t)
    with_post = post is not None

    def body(*refs):
        if with_post:
            dr_ref, dh_ref, x_ref, g_ref, f_ref, gp_ref, dx_ref, dg_ref, dff_ref, dgp_ref = refs
        else:
            dr_ref, dh_ref, x_ref, g_ref, dx_ref, dg_ref = refs
        i = pl.program_id(0)
        dx, dg = _norm_bwd(dh_ref[...], x_ref[...], g_ref[...])
        dx = dr_ref[...] + dx
        dx_ref[...] = dx
        _accumulate(dg_ref, dg, i == 0)
        if with_post:
            dff, dgp = _norm_bwd(post[2] * dx, f_ref[...], gp_ref[...])
            dff_ref[...] = dff.astype(BF16)
            _accumulate(dgp_ref, dgp, i == 0)

    row = pl.BlockSpec((tm, d), lambda i: (i, 0))
    vec = pl.BlockSpec((1, d), lambda i: (0, 0))
    ins, in_specs = [dres, dh, xin, gpre], [row, row, row, vec]
    out_specs = [row, vec]
    out_shape = [jax.ShapeDtypeStruct((t, d), F32), jax.ShapeDtypeStruct((1, d), F32)]
    if with_post:
        ins += [post[0], post[1]]
        in_specs += [row, vec]
        out_specs += [row, vec]
        out_shape += [jax.ShapeDtypeStruct((t, d), BF16), jax.ShapeDtypeStruct((1, d), F32)]
    body, ins, in_specs = _ordered(body, ins, in_specs, after)
    return pl.pallas_call(
        body, name=name, grid=(t // tm,), in_specs=in_specs, out_specs=out_specs, out_shape=out_shape,
        compiler_params=_params(("arbitrary",), 6 * _nbytes((tm, d), F32)),
    )(*ins)


def _swiglu_fwd(name, gu):
    t, f2 = gu.shape
    f = f2 // 2
    tm, tf = _tile(t, 512, 8), _tile(f, 512, 128)
    nf = f // tf

    def body(g_ref, u_ref, a_ref):
        g = g_ref[...].astype(F32)
        a_ref[...] = (g * _sigmoid(g) * u_ref[...].astype(F32)).astype(BF16)

    return pl.pallas_call(
        body, name=name, grid=(t // tm, nf),
        in_specs=[pl.BlockSpec((tm, tf), lambda i, j: (i, j)), pl.BlockSpec((tm, tf), lambda i, j: (i, j + nf))],
        out_specs=pl.BlockSpec((tm, tf), lambda i, j: (i, j)),
        out_shape=jax.ShapeDtypeStruct((t, f), BF16),
        compiler_params=_params(("parallel", "parallel"), 3 * _nbytes((tm, tf), F32)),
    )(gu, gu)


def _swiglu_bwd(name, da, gu, after=None):
    t, f = da.shape
    tm, tf = _tile(t, 512, 8), _tile(f, 512, 128)
    nf = f // tf

    def body(da_ref, g_ref, u_ref, o_ref):
        g = g_ref[...].astype(F32)
        u = u_ref[...].astype(F32)
        dav = da_ref[...]
        sig = _sigmoid(g)
        o_ref[0] = (dav * u * sig * (1.0 + g * (1.0 - sig))).astype(BF16)
        o_ref[1] = (dav * g * sig).astype(BF16)

    body, ins, in_specs = _ordered(
        body, [da, gu, gu],
        [pl.BlockSpec((tm, tf), lambda i, j: (i, j)), pl.BlockSpec((tm, tf), lambda i, j: (i, j)),
         pl.BlockSpec((tm, tf), lambda i, j: (i, j + nf))], after)
    return pl.pallas_call(
        body, name=name, grid=(t // tm, nf), in_specs=in_specs,
        out_specs=pl.BlockSpec((2, tm, tf), lambda i, j: (0, i, j)),
        out_shape=jax.ShapeDtypeStruct((2, t, f), BF16),
        compiler_params=_params(("parallel", "parallel"), 5 * _nbytes((tm, tf), F32)),
    )(*ins)


def _ffn_dh(name, dgu, w_gu, after=None):
    _, t, f = dgu.shape
    d = w_gu.shape[0]
    tm, tn, tk = _mm_tiles(t, d, f)
    nkf = f // tk
    return _matmul(name, dgu, w_gu, form="nt", out_dtype=F32, tm=tm, tn=tn, tk=tk, sizes=(t, d, 2 * f),
                   a_map=lambda i, j, kk: (kk // nkf, i, kk % nkf), after=after)


def _ffn_dw_gate_up(name, h, dgu, after=None):
    _, t, f = dgu.shape
    d = h.shape[1]
    tm, tn, tk = _mm_tiles(d, f, t)
    nf = f // tn
    return _matmul(name, h, dgu, form="tn", out_dtype=BF16, tm=tm, tn=tn, tk=tk, sizes=(d, 2 * f, t),
                   b_map=lambda i, j, kk: (j // nf, kk, j % nf), after=after)


def _lower_bound(lbp):
    m = jnp.max(lbp, axis=0, keepdims=True)
    e = jnp.exp(lbp - m)
    return e[0:1] / jnp.sum(e, axis=0, keepdims=True)


def _chunk_mask(reverse):
    row = lax.broadcasted_iota(jnp.int32, (CHUNK, CHUNK), 0)
    col = lax.broadcasted_iota(jnp.int32, (CHUNK, CHUNK), 1)
    return (col >= row) if reverse else (col <= row)


def _hgrn_gates(z, lb, mask_bf):
    sig = _sigmoid(z)
    f = lb + (1.0 - lb) * sig
    logf = jnp.log(f)
    k = 1.0 - f
    cum = _dot_exact(mask_bf, logf)
    last = jnp.sum(logf, axis=0, keepdims=True)
    return sig, f, k, cum, last


def _hgrn_scan_fwd(name, p, lbp_f, lbp_b):
    t = p.shape[0]
    hw = lbp_f.shape[1]
    nh, nc = hw // HEAD, t // CHUNK

    def body(qf, vf, zf, qb, vb, zb, lbf, lbb, of_ref, ob_ref, stf_ref, stb_ref, state):
        n = pl.program_id(0)

        @pl.when(n == 0)
        def _():
            state[...] = jnp.zeros_like(state)

        directions = [(qf, vf, zf, lbf, of_ref, stf_ref), (qb, vb, zb, lbb, ob_ref, stb_ref)]
        for d, (q_ref, v_ref, z_ref, lb_ref, o_ref, st_ref) in enumerate(directions):
            mask = _chunk_mask(d == 1)
            lb = _lower_bound(lb_ref[...])
            _, _, k, cum, last = _hgrn_gates(z_ref[...], lb, mask.astype(BF16))
            v = v_ref[...].astype(BF16)
            qd = (q_ref[...] * jnp.exp(cum)).astype(BF16)
            kd = (k * jnp.exp(-cum)).astype(BF16)
            kt = (k * jnp.exp(last - cum)).astype(BF16)
            dec = jnp.exp(last)
            s_all = state[d]
            st_ref[...] = s_all
            for h in range(nh):
                sl = slice(h * HEAD, (h + 1) * HEAD)
                s_in = s_all[:, sl]
                a = jnp.where(mask, _dot(qd[:, sl], kd[:, sl], 1, 1), 0.0).astype(BF16)
                o_ref[:, sl] = _dot(a, v[:, sl]) + _dot(qd[:, sl], s_in.astype(BF16), 1, 1)
                state[d, :, sl] = s_in * dec[:, sl] + _dot(v[:, sl], kt[:, sl], 0, 0)

    def col(group, reverse):
        return pl.BlockSpec((CHUNK, hw), lambda n: ((nc - 1 - n) if reverse else n, group))

    def st(reverse):
        return pl.BlockSpec((None, HEAD, hw), lambda n: ((nc - 1 - n) if reverse else n, 0, 0))

    lb_spec = pl.BlockSpec((2, hw), lambda n: (0, 0))
    out = jax.ShapeDtypeStruct((t, hw), F32)
    states = jax.ShapeDtypeStruct((nc, HEAD, hw), F32)
    return pl.pallas_call(
        body, name=name, grid=(nc,),
        in_specs=[col(0, False), col(1, False), col(2, False), col(0, True), col(1, True), col(3, True),
                  lb_spec, lb_spec],
        out_specs=[col(0, False), col(0, True), st(False), st(True)],
        out_shape=[out, out, states, states],
        scratch_shapes=[pltpu.VMEM((2, HEAD, hw), F32)],
        compiler_params=_params(("arbitrary",), 12 * _nbytes((HEAD, hw), F32)),
    )(p, p, p, p, p, p, lbp_f, lbp_b)


def _hgrn_scan_bwd(name, p, lbp_f, lbp_b, do, st_f, st_b):
    t = p.shape[0]
    hw = lbp_f.shape[1]
    nh, nc = hw // HEAD, t // CHUNK

    def body(qf, vf, zf, dof, sf, qb, vb, zb, dob, sb, lbf, lbb, dqf, dvf, dzf, dlbf, dqb, dvb, dzb, dlbb,
             dstate, dlb_acc, dqd_s, dkd_s, dkt_s, ddec_s):
        n = pl.program_id(0)

        @pl.when(n == 0)
        def _():
            dstate[...] = jnp.zeros_like(dstate)
            dlb_acc[...] = jnp.zeros_like(dlb_acc)

        directions = [(qf, vf, zf, dof, sf, lbf, dqf, dvf, dzf, dlbf), (qb, vb, zb, dob, sb, lbb, dqb, dvb, dzb, dlbb)]
        for d, (q_ref, v_ref, z_ref, do_ref, st_ref, lb_ref, dq_ref, dv_ref, dz_ref, dlb_ref) in enumerate(directions):
            mask = _chunk_mask(d == 1)
            mask_bf = mask.astype(BF16)
            lb = _lower_bound(lb_ref[...])
            sig, f, k, cum, last = _hgrn_gates(z_ref[...], lb, mask_bf)
            e_pos, e_neg, e_tail = jnp.exp(cum), jnp.exp(-cum), jnp.exp(last - cum)
            dec = jnp.exp(last)
            v = v_ref[...].astype(BF16)
            qd, kd, kt = q_ref[...] * e_pos, k * e_neg, k * e_tail
            qd_bf, kd_bf, kt_bf = qd.astype(BF16), kd.astype(BF16), kt.astype(BF16)
            s_all = st_ref[...]
            ds_all = dstate[d]
            dov = do_ref[...].astype(BF16)
            for h in range(nh):
                sl = slice(h * HEAD, (h + 1) * HEAD)
                s_in, ds_out = s_all[:, sl], ds_all[:, sl]
                ds_bf = ds_out.astype(BF16)
                a = jnp.where(mask, _dot(qd_bf[:, sl], kd_bf[:, sl], 1, 1), 0.0).astype(BF16)
                da = jnp.where(mask, _dot(dov[:, sl], v[:, sl], 1, 1), 0.0).astype(BF16)
                dv_ref[:, sl] = _dot(a, dov[:, sl], 0, 0) + _dot(kt_bf[:, sl], ds_bf, 1, 1)
                dqd_s[:, sl] = _dot(da, kd_bf[:, sl]) + _dot(dov[:, sl], s_in.astype(BF16))
                dkd_s[:, sl] = _dot(da, qd_bf[:, sl], 0, 0)
                dkt_s[:, sl] = _dot(v[:, sl], ds_bf)
                dstate[d, :, sl] = _dot(dov[:, sl], qd_bf[:, sl], 0, 0) + ds_out * dec[:, sl]
                ddec_s[:, sl] = jnp.sum(ds_out * s_in, axis=0, keepdims=True)
            dqd, dkd, dkt = dqd_s[...], dkd_s[...], dkt_s[...]
            dlast = jnp.sum(dkt * kt, axis=0, keepdims=True) + dec * ddec_s[...]
            dq_ref[...] = dqd * e_pos
            dk = dkd * e_neg + dkt * e_tail
            dcum = dqd * qd - dkd * kd - dkt * kt
            dlogf = _dot_exact(mask_bf, dcum, 0, 0) + dlast
            df = dlogf / f - dk
            dz_ref[...] = df * (1.0 - lb) * sig * (1.0 - sig)
            dlb_acc[d] += jnp.sum(df * (1.0 - sig), axis=0, keepdims=True)

            @pl.when(n == nc - 1)
            def _():
                g = dlb_acc[d] * lb * (1.0 - lb)
                dlb_ref[0:1, :] = g
                dlb_ref[1:2, :] = -g

    def col(group, reverse):
        return pl.BlockSpec((CHUNK, hw), lambda n: (n if reverse else (nc - 1 - n), group))

    def st(reverse):
        return pl.BlockSpec((None, HEAD, hw), lambda n: (n if reverse else (nc - 1 - n), 0, 0))

    lb_spec = pl.BlockSpec((2, hw), lambda n: (0, 0))
    out = jax.ShapeDtypeStruct((t, hw), F32)
    dlb = jax.ShapeDtypeStruct((2, hw), F32)
    wide = pltpu.VMEM((CHUNK, hw), F32)
    return pl.pallas_call(
        body, name=name, grid=(nc,),
        in_specs=[col(0, False), col(1, False), col(2, False), col(0, False), st(False),
                  col(0, True), col(1, True), col(3, True), col(0, True), st(True), lb_spec, lb_spec],
        out_specs=[col(0, False), col(0, False), col(0, False), lb_spec,
                   col(0, True), col(0, True), col(0, True), lb_spec],
        out_shape=[out, out, out, dlb, out, out, out, dlb],
        scratch_shapes=[pltpu.VMEM((2, HEAD, hw), F32), pltpu.VMEM((2, 1, hw), F32), wide, wide, wide,
                        pltpu.VMEM((1, hw), F32)],
        compiler_params=_params(("arbitrary",), 16 * _nbytes((HEAD, hw), F32)),
    )(p, p, p, do, st_f, p, p, p, do, st_b, lbp_f, lbp_b)


def _hgrn_out_fwd(name, o_f, o_b, p, gain, g_group):
    t, hw = o_f.shape
    nh = hw // HEAD
    tm = _tile(t, 512, 8)

    def body(of_ref, ob_ref, g_ref, gain_ref, y_ref):
        o = of_ref[...] + ob_ref[...]
        g = g_ref[...]
        y_ref[...] = (o * _rms(o) * gain_ref[...] * (g * _sigmoid(g))).astype(BF16)

    blk = pl.BlockSpec((tm, HEAD), lambda i, h: (i, h))
    return pl.pallas_call(
        body, name=name, grid=(t // tm, nh),
        in_specs=[blk, blk, pl.BlockSpec((tm, HEAD), lambda i, h: (i, g_group * nh + h)),
                  pl.BlockSpec((1, HEAD), lambda i, h: (0, h))],
        out_specs=blk, out_shape=jax.ShapeDtypeStruct((t, hw), BF16),
        compiler_params=_params(("parallel", "parallel"), 1 << 20),
    )(o_f, o_b, p, gain)


def _hgrn_out_bwd(name, dy, o_f, o_b, p, gain, g_group, after=None):
    t, hw = o_f.shape
    nh = hw // HEAD
    tm = _tile(t, 512, 8)

    def body(dy_ref, of_ref, ob_ref, g_ref, gain_ref, do_ref, dg_ref, dgain_ref):
        i = pl.program_id(1)
        o = of_ref[...] + ob_ref[...]
        g = g_ref[...]
        gain_v = gain_ref[...]
        sig = _sigmoid(g)
        dyv = dy_ref[...]
        do, dgain = _norm_bwd(dyv * (g * sig), o, gain_v)
        do_ref[...] = do
        dg_ref[...] = dyv * (o * _rms(o) * gain_v) * sig * (1.0 + g * (1.0 - sig))
        _accumulate(dgain_ref, dgain, i == 0)

    blk = pl.BlockSpec((tm, HEAD), lambda h, i: (i, h))
    vec = pl.BlockSpec((1, HEAD), lambda h, i: (0, h))
    out = jax.ShapeDtypeStruct((t, hw), F32)
    body, ins, in_specs = _ordered(
        body, [dy, o_f, o_b, p, gain],
        [blk, blk, blk, pl.BlockSpec((tm, HEAD), lambda h, i: (i, g_group * nh + h)), vec], after)
    return pl.pallas_call(
        body, name=name, grid=(nh, t // tm), in_specs=in_specs,
        out_specs=[blk, blk, vec], out_shape=[out, out, jax.ShapeDtypeStruct((1, hw), F32)],
        compiler_params=_params(("parallel", "arbitrary"), 1 << 20),
    )(*ins)


def _t5_bucket_ids():
    c = np.arange(WINDOW)[:, None]
    s = np.arange(SPAN)[None, :]
    rel = s - WINDOW - c
    nb = REL_BUCKETS // 2
    max_exact = nb // 2
    bucket = (rel > 0).astype(np.int32) * nb
    n = np.abs(rel)
    large = max_exact + (np.log(np.maximum(n, 1) / max_exact) / np.log(REL_MAX_DIST / max_exact)
                         * (nb - max_exact)).astype(np.int32)
    large = np.minimum(large, nb - 1)
    ids = bucket + np.where(n < max_exact, n, large).astype(np.int32)
    return jnp.asarray(ids.reshape(1, WINDOW * SPAN), jnp.int32)


def _bias_onehot(ids_ref):
    n = ids_ref.shape[1]
    return (lax.broadcasted_iota(jnp.int32, (REL_BUCKETS, n), 0) == ids_ref[...]).astype(BF16)


def _bias_gather(name, table_t, ids):
    nh = table_t.shape[0]

    def body(t_ref, ids_ref, o_ref):
        o_ref[...] = _dot_exact(t_ref[...], _bias_onehot(ids_ref), split="a")

    return pl.pallas_call(
        body, name=name, out_shape=jax.ShapeDtypeStruct((nh, ids.shape[1]), F32),
        compiler_params=pltpu.CompilerParams(vmem_limit_bytes=32 << 20),
    )(table_t, ids)


def _bias_scatter(name, dbias, ids):
    nh = dbias.shape[0]

    def body(d_ref, ids_ref, o_ref):
        o_ref[...] = _dot_exact(d_ref[...], _bias_onehot(ids_ref), 1, 1, split="a")

    return pl.pallas_call(
        body, name=name, out_shape=jax.ShapeDtypeStruct((nh, REL_BUCKETS), F32),
        compiler_params=pltpu.CompilerParams(vmem_limit_bytes=32 << 20),
    )(dbias, ids)


def _attn_valid(i, t):
    c = lax.broadcasted_iota(jnp.int32, (WINDOW, SPAN), 0)
    s = lax.broadcasted_iota(jnp.int32, (WINDOW, SPAN), 1)
    rel = s - WINDOW - c
    pos = i * WINDOW - WINDOW + s
    return (jnp.abs(rel) <= WINDOW) & (pos >= 0) & (pos < t)


def _attn_probs(qh, kh, bias_h, sink_h, valid):
    s = _dot(qh, kh, 1, 1) / math.sqrt(HEAD)
    s = jnp.where(valid, s + bias_h, NEG_INF)
    m = jnp.maximum(jnp.max(s, axis=-1, keepdims=True), sink_h)
    e = jnp.exp(s - m)
    es = jnp.exp(sink_h - m)
    inv = 1.0 / (jnp.sum(e, axis=-1, keepdims=True) + es)
    return e * inv, es * inv


def _attn_fwd(name, p, k_pad, v_pad, bias, sink, q_group_blk):
    t = p.shape[0]
    nh = bias.shape[0]
    aw = nh * HEAD
    grp = nh // KV_HEADS
    nb = t // WINDOW

    def body(q_ref, k_ref, v_ref, b_ref, s_ref, y_ref):
        i = pl.program_id(0)
        valid = _attn_valid(i, t)
        start = pl.multiple_of(i * WINDOW, WINDOW)
        ks = k_ref[pl.ds(start, SPAN), :]
        vs = v_ref[pl.ds(start, SPAN), :]
        for h in range(nh):
            kv = h // grp
            qh = q_ref[:, h * HEAD:(h + 1) * HEAD].astype(BF16)
            pr, _ = _attn_probs(qh, ks[:, kv * HEAD:(kv + 1) * HEAD], b_ref[h], s_ref[0:1, h:h + 1], valid)
            y_ref[:, h * HEAD:(h + 1) * HEAD] = _dot(pr.astype(BF16), vs[:, kv * HEAD:(kv + 1) * HEAD]).astype(BF16)

    full = lambda a: pl.BlockSpec(a.shape, lambda i: (0,) * a.ndim)
    return pl.pallas_call(
        body, name=name, grid=(nb,),
        in_specs=[pl.BlockSpec((WINDOW, aw), lambda i: (i, q_group_blk)), full(k_pad), full(v_pad), full(bias),
                  full(sink)],
        out_specs=pl.BlockSpec((WINDOW, aw), lambda i: (i, 0)),
        out_shape=jax.ShapeDtypeStruct((t, aw), BF16),
        compiler_params=_params(("parallel",), _nbytes(k_pad.shape, BF16) * 2 + _nbytes(bias.shape, F32)),
    )(p, k_pad, v_pad, bias, sink)


def _attn_bwd(name, p, k_pad, v_pad, bias, sink, dy, q_group_blk, dy_blk, after=None):
    t = p.shape[0]
    nh = bias.shape[0]
    aw = nh * HEAD
    grp = nh // KV_HEADS
    nb = t // WINDOW
    kvw = k_pad.shape[1]

    def body(q_ref, k_ref, v_ref, b_ref, s_ref, dy_ref, dq_ref, dk_ref, dv_ref, db_ref, ds_ref):
        i = pl.program_id(0)

        @pl.when(i == 0)
        def _():
            dk_ref[...] = jnp.zeros_like(dk_ref)
            dv_ref[...] = jnp.zeros_like(dv_ref)
            db_ref[...] = jnp.zeros_like(db_ref)
            ds_ref[...] = jnp.zeros_like(ds_ref)

        valid = _attn_valid(i, t)
        start = pl.multiple_of(i * WINDOW, WINDOW)
        ks = k_ref[pl.ds(start, SPAN), :]
        vs = v_ref[pl.ds(start, SPAN), :]
        inv_sqrt = 1.0 / math.sqrt(HEAD)
        for kv in range(KV_HEADS):
            kh = ks[:, kv * HEAD:(kv + 1) * HEAD]
            vh = vs[:, kv * HEAD:(kv + 1) * HEAD]
            dk_acc = jnp.zeros((SPAN, HEAD), F32)
            dv_acc = jnp.zeros((SPAN, HEAD), F32)
            for h in range(kv * grp, (kv + 1) * grp):
                qh = q_ref[:, h * HEAD:(h + 1) * HEAD].astype(BF16)
                pr, ps = _attn_probs(qh, kh, b_ref[h], s_ref[0:1, h:h + 1], valid)
                doh = dy_ref[:, h * HEAD:(h + 1) * HEAD].astype(BF16)
                dp = _dot(doh, vh, 1, 1)
                delta = jnp.sum(pr * dp, axis=-1, keepdims=True)
                dsc = pr * (dp - delta)
                db_ref[h] += dsc
                ds_ref[h:h + 1, :] += jnp.broadcast_to(jnp.sum(-ps * delta, axis=0, keepdims=True), (1, 128))
                dsr = (dsc * inv_sqrt).astype(BF16)
                dq_ref[:, h * HEAD:(h + 1) * HEAD] = _dot(dsr, kh)
                dk_acc += _dot(dsr, qh, 0, 0)
                dv_acc += _dot(pr.astype(BF16), doh, 0, 0)
            dk_ref[pl.ds(start, SPAN), kv * HEAD:(kv + 1) * HEAD] += dk_acc
            dv_ref[pl.ds(start, SPAN), kv * HEAD:(kv + 1) * HEAD] += dv_acc

    full = lambda a: pl.BlockSpec(a.shape, lambda i: (0,) * a.ndim)
    whole = lambda shape: pl.BlockSpec(shape, lambda i: (0,) * len(shape))
    pad_shape = (t + 2 * WINDOW, kvw)
    body, ins, in_specs = _ordered(
        body, [p, k_pad, v_pad, bias, sink, dy],
        [pl.BlockSpec((WINDOW, aw), lambda i: (i, q_group_blk)), full(k_pad), full(v_pad), full(bias), full(sink),
         pl.BlockSpec((WINDOW, aw), lambda i: (i, dy_blk))], after)
    return pl.pallas_call(
        body, name=name, grid=(nb,), in_specs=in_specs,
        out_specs=[pl.BlockSpec((WINDOW, aw), lambda i: (i, 0)), whole(pad_shape), whole(pad_shape),
                   whole(bias.shape), whole((nh, 128))],
        out_shape=[jax.ShapeDtypeStruct((t, aw), F32), jax.ShapeDtypeStruct(pad_shape, F32),
                   jax.ShapeDtypeStruct(pad_shape, F32), jax.ShapeDtypeStruct(bias.shape, F32),
                   jax.ShapeDtypeStruct((nh, 128), F32)],
        compiler_params=_params(("arbitrary",), 3 * _nbytes(pad_shape, F32) + 2 * _nbytes(bias.shape, F32)),
    )(*ins)


def _pad_kv(name, p, kv_blk, kvw):
    t = p.shape[0]
    nb = t // WINDOW

    def body(x_ref, o_ref):
        i = pl.program_id(0)
        inside = jnp.logical_and(i >= 1, i <= nb)
        o_ref[...] = jnp.where(inside, x_ref[...], 0.0).astype(BF16)

    return pl.pallas_call(
        body, name=name, grid=(nb + 2,),
        in_specs=[pl.BlockSpec((WINDOW, kvw), lambda i: (jnp.clip(i - 1, 0, nb - 1), kv_blk))],
        out_specs=pl.BlockSpec((WINDOW, kvw), lambda i: (i, 0)),
        out_shape=jax.ShapeDtypeStruct((t + 2 * WINDOW, kvw), BF16),
        compiler_params=_params(("parallel",), 1 << 20),
    )(p)


def _mix_dproj(name, pieces, kv_pads, t, after=None):
    hw = pieces[0][0].shape[1]
    kvw = kv_pads[0].shape[1]
    widths = [hw] * len(pieces) + [kvw] * len(kv_pads)
    total = sum(widths)
    tm = WINDOW
    flat = [a for pc in pieces for a in pc]

    def body(*refs):
        o_ref = refs[-1]
        pos, off = 0, 0
        for pc in pieces:
            val = refs[pos][...]
            for extra in range(1, len(pc)):
                val = val + refs[pos + extra][...]
            o_ref[:, off:off + hw] = val.astype(BF16)
            pos += len(pc)
            off += hw
        for _ in kv_pads:
            o_ref[:, off:off + kvw] = refs[pos][...].astype(BF16)
            pos += 1
            off += kvw

    in_specs = [pl.BlockSpec((tm, hw), lambda i: (i, 0)) for _ in flat]
    in_specs += [pl.BlockSpec((tm, kvw), lambda i: (i + 1, 0)) for _ in kv_pads]
    body, ins, in_specs = _ordered(body, [*flat, *kv_pads], in_specs, after)
    return pl.pallas_call(
        body, name=name, grid=(t // tm,), in_specs=in_specs,
        out_specs=pl.BlockSpec((tm, total), lambda i: (i, 0)),
        out_shape=jax.ShapeDtypeStruct((t, total), BF16),
        compiler_params=_params(("parallel",), 3 * _nbytes((tm, total), F32)),
    )(*ins)


def _concat_cols(name, a, b):
    t, wa = a.shape
    wb = b.shape[1]
    tm = _tile(t, 512, 16)

    def body(a_ref, b_ref, o_ref):
        o_ref[:, :wa] = a_ref[...]
        o_ref[:, wa:] = b_ref[...]

    return pl.pallas_call(
        body, name=name, grid=(t // tm,),
        in_specs=[pl.BlockSpec((tm, wa), lambda i: (i, 0)), pl.BlockSpec((tm, wb), lambda i: (i, 0))],
        out_specs=pl.BlockSpec((tm, wa + wb), lambda i: (i, 0)),
        out_shape=jax.ShapeDtypeStruct((t, wa + wb), a.dtype),
        compiler_params=_params(("parallel",), 2 * _nbytes((tm, wa + wb), a.dtype)),
    )(a, b)


def _cast_into_full(name, w, geom, idx):
    r, c = w.shape
    tr = _tile(r, 256, 16)
    nr = r // tr
    if geom.col:
        place = lambda i, iref: (i, iref[0])
    else:
        place = lambda i, iref: (iref[0] * nr + i, 0)

    def body(i_ref, w_ref, o_ref):
        o_ref[...] = w_ref[...].astype(BF16)

    return pl.pallas_call(
        body, name=name,
        grid_spec=pltpu.PrefetchScalarGridSpec(
            num_scalar_prefetch=1, grid=(nr,),
            in_specs=[pl.BlockSpec((tr, c), lambda i, iref: (i, 0))],
            out_specs=pl.BlockSpec((tr, c), place)),
        out_shape=jax.ShapeDtypeStruct(geom.full_shape, BF16),
        compiler_params=_params(("parallel",), 2 * _nbytes((tr, c), F32)),
    )(idx, w)


def _adamw(name, w, g, m, v):
    r, c = w.shape
    tr = _tile(r, 128, 8)
    bc1 = 1.0 - ADAM_B1 ** ADAM_STEP
    bc2 = 1.0 - ADAM_B2 ** ADAM_STEP

    def body(w_ref, g_ref, m_ref, v_ref, d_ref, nm_ref, nv_ref):
        gv = g_ref[...]
        nm = ADAM_B1 * m_ref[...] + (1.0 - ADAM_B1) * gv
        nv = ADAM_B2 * v_ref[...] + (1.0 - ADAM_B2) * (gv * gv)
        nm_ref[...] = nm
        nv_ref[...] = nv
        d_ref[...] = -ADAM_LR * ((nm / bc1) / (jnp.sqrt(nv / bc2) + ADAM_EPS) + ADAM_WD * w_ref[...])

    blk = pl.BlockSpec((tr, c), lambda i: (i, 0))
    out = jax.ShapeDtypeStruct((r, c), F32)
    return pl.pallas_call(
        body, name=name, grid=(r // tr,), in_specs=[blk] * 4, out_specs=[blk] * 3, out_shape=[out] * 3,
        compiler_params=_params(("parallel",), 7 * _nbytes((tr, c), F32)),
    )(w, g, m, v)


def _mesh_pos():
    return lax.axis_index("x"), lax.axis_index("y"), lax.axis_index("c")


def _other_chips(x, y):
    return [(1 - x, y), (x, 1 - y), (1 - x, 1 - y)]


class _Big:
    def __init__(self, shard_shape, col_sharded):
        self.col = col_sharded
        r, c = shard_shape
        self.shard_shape = (r, c)
        self.full_shape = (r, N_CHIPS * c) if col_sharded else (N_CHIPS * r, c)
        self.half_shape = (r // 2, N_CHIPS * c) if col_sharded else (N_CHIPS * r, c // 2)
        self.shard_half_shape = (r // 2, c) if col_sharded else (r, c // 2)

    def region(self, ref, s, half=None):
        r, c = self.shard_shape
        if self.col:
            rows = slice(None) if half is None else pl.ds(half * (r // 2), r // 2)
            return ref.at[rows, pl.ds(s * c, c)]
        cols = slice(None) if half is None else pl.ds(half * (c // 2), c // 2)
        return ref.at[pl.ds(s * r, r), cols]

    def three_halves(self, ref, half):
        r, c = self.shard_shape
        if self.col:
            return ref.at[pl.ds(half * (r // 2), r // 2), pl.ds(0, 3 * c)]
        return ref.at[pl.ds(0, 3 * r), pl.ds(half * (c // 2), c // 2)]

    def half_of_full(self, ref, half):
        r, c = self.full_shape
        if self.col:
            return ref.at[pl.ds(half * (r // 2), r // 2), :]
        return ref.at[:, pl.ds(half * (c // 2), c // 2)]

    def half_of_shard(self, ref, half):
        r, c = self.shard_shape
        if self.col:
            return ref.at[pl.ds(half * (r // 2), r // 2), :]
        return ref.at[:, pl.ds(half * (c // 2), c // 2)]

    def shard_of_half(self, ref, s):
        r, c = self.shard_shape
        if self.col:
            return ref.at[:, pl.ds(s * c, c)]
        return ref.at[pl.ds(s * r, r), :]


HBM =pl.BlockSpec(memory_space=pltpu.HBM)
SEM = pl.BlockSpec(memory_space=pltpu.SEMAPHORE)
SPLIT_COPY = pltpu.CompilerParams(has_side_effects=pltpu.SideEffectType.DATAFLOW_SIDE_EFFECTING)


def _in_hbm(a):
    return pltpu.with_memory_space_constraint(a, pltpu.HBM)


def _gather_start(fulls, geoms):
    nw = len(fulls)

    def body(*refs):
        dst = refs[nw:2 * nw]
        sems = refs[2 * nw:]
        x, y, c = _mesh_pos()
        mine = 2 * x + y
        for w in range(nw):
            own_half = geoms[w].region(dst[w], mine, c)
            for chip in _other_chips(x, y):
                pltpu.make_async_remote_copy(src_ref=own_half, dst_ref=own_half, send_sem=sems[2 * w],
                                             recv_sem=sems[2 * w + 1], device_id=(*chip, c),
                                             device_id_type=MESH).start()

    out = pl.pallas_call(
        body, name="gather_start", in_specs=[HBM] * nw, out_specs=[HBM] * nw + [SEM] * (2 * nw),
        out_shape=[pltpu.HBM(g.full_shape, BF16) for g in geoms] + [pltpu.SemaphoreType.DMA(())] * (2 * nw),
        input_output_aliases={w: w for w in range(nw)}, compiler_params=SPLIT_COPY,
    )(*[_in_hbm(a) for a in fulls])
    return out[:nw], [(out[nw + 2 * w], out[nw + 2 * w + 1]) for w in range(nw)]


def _wait_three(geom, ref, half, send_sem, recv_sem, peer, recv):
    three = geom.three_halves(ref, half)
    copy = pltpu.make_async_remote_copy(src_ref=three, dst_ref=three, send_sem=send_sem, recv_sem=recv_sem,
                                        device_id=peer, device_id_type=MESH)
    if recv:
        copy.wait_recv()
    else:
        copy.wait_send()


def _gather_forward(name, full, geom, sems, after):
    def body(w_in, send_sem, recv_sem, after_ref, w_ref, fwd_send, fwd_recv):
        x, y, c = _mesh_pos()
        sibling = (x, y, 1 - c)
        _wait_three(geom, w_ref, c, send_sem, recv_sem, sibling, recv=True)
        for chip in _other_chips(x, y):
            landed = geom.region(w_ref, 2 * chip[0] + chip[1], c)
            pltpu.make_async_remote_copy(src_ref=landed, dst_ref=landed, send_sem=fwd_send, recv_sem=fwd_recv,
                                         device_id=sibling, device_id_type=MESH).start()
        _wait_three(geom, w_ref, c, send_sem, recv_sem, sibling, recv=False)

    sem = pltpu.SemaphoreType.DMA(())
    out = pl.pallas_call(
        body, name=name, in_specs=[HBM, SEM, SEM, pl.BlockSpec(memory_space=pl.ANY)], out_specs=[HBM, SEM, SEM],
        out_shape=[pltpu.HBM(geom.full_shape, BF16), sem, sem],
        input_output_aliases={0: 0}, compiler_params=SPLIT_COPY,
    )(full, sems[0], sems[1], after)
    return out[0], (out[1], out[2])


def _gather_end(name, full, geom, sems, after):
    def body(w_in, fwd_send, fwd_recv, after_ref, w_ref):
        x, y, c = _mesh_pos()
        sibling = (x, y, 1 - c)
        _wait_three(geom, w_ref, 1 - c, fwd_send, fwd_recv, sibling, recv=True)
        _wait_three(geom, w_ref, c, fwd_send, fwd_recv, sibling, recv=False)

    return pl.pallas_call(
        body, name=name, in_specs=[HBM, SEM, SEM, pl.BlockSpec(memory_space=pl.ANY)], out_specs=HBM,
        out_shape=pltpu.HBM(geom.full_shape, BF16),
        input_output_aliases={0: 0}, compiler_params=SPLIT_COPY,
    )(full, sems[0], sems[1], after)


def _split_copy_call(name, arrays, fn, sems=(), after=None, new_sems=0):
    n, ns = len(arrays), len(sems)
    n_in = n + ns + (after is not None)

    def body(*refs):
        fn(refs[n_in:n_in + n], refs[n:n + ns], refs[n_in + n:-1])
        refs[-1][...] = jnp.zeros_like(refs[-1])

    ins = list(arrays) if ns else [_in_hbm(a) for a in arrays]
    ins += list(sems) + ([after] if after is not None else [])
    in_specs = [HBM] * n + [SEM] * ns + ([pl.BlockSpec(memory_space=pl.ANY)] if after is not None else [])
    out = pl.pallas_call(
        body, name=name, in_specs=in_specs,
        out_specs=[HBM] * n + [SEM] * new_sems + [pl.BlockSpec(memory_space=pltpu.VMEM)],
        out_shape=[pltpu.HBM(a.shape, a.dtype) for a in arrays] + [pltpu.SemaphoreType.DMA(())] * new_sems
        + [jax.ShapeDtypeStruct((8, 128), F32)],
        input_output_aliases={i: i for i in range(n)}, compiler_params=SPLIT_COPY,
    )(*ins)
    return list(out[:n]), tuple(out[n:-1]), out[-1]


def _remote(src, dst, sems, to):
    return pltpu.make_async_remote_copy(src_ref=src, dst_ref=dst, send_sem=sems[0], recv_sem=sems[1],
                                        device_id=to, device_id_type=MESH)


class _GradReduce:
    def __init__(self, name, geom, idx, c_idx):
        self.name, self.geom, self.idx, self.c_idx = name, geom, idx, c_idx

    def pair_start(self, dw):
        g = self.geom

        def start(refs, _, new):
            x, y, c = _mesh_pos()
            _remote(g.half_of_full(refs[0], 1 - c), refs[1], new, (x, y, 1 - c)).start()

        self.arrays, self.sems, token = _split_copy_call(
            f"pair_start_{self.name}", [dw, lax.empty(g.half_shape, BF16)], start, new_sems=2)
        return token

    def pair_finish(self, after):
        g = self.geom

        def wait(refs, sems, _):
            x, y, c = _mesh_pos()
            copy = _remote(g.half_of_full(refs[0], 1 - c), refs[1], sems, (x, y, 1 - c))
            copy.wait_send()
            copy.wait_recv()

        (dw, landed), _, _ = _split_copy_call(f"pair_wait_{self.name}", self.arrays, wait, self.sems, after)
        half = _pair_add(f"pair_add_{self.name}", dw, landed, g, self.c_idx)

        def start(refs, _, new):
            x, y, c = _mesh_pos()
            for k, chip in enumerate(_other_chips(x, y)):
                _remote(g.shard_of_half(refs[0], 2 * chip[0] + chip[1]), refs[1].at[k], new, (*chip, c)).start()

        self.arrays, self.sems, token = _split_copy_call(
            f"chip_start_{self.name}", [half, lax.empty((3,) + g.shard_half_shape, BF16)], start, new_sems=2)
        return token

    def chip_finish(self, after):
        g = self.geom

        def wait(refs, sems, _):
            x, y, c = _mesh_pos()
            three = _remote(refs[1], refs[1], sems, (x, y, 1 - c))
            three.wait_send()
            three.wait_recv()

        (half, landed), _, _ = _split_copy_call(f"chip_wait_{self.name}", self.arrays, wait, self.sems, after)
        quarter = _chip_add(f"chip_add_{self.name}", half, landed, g, self.idx)

        def start(refs, _, new):
            x, y, c = _mesh_pos()
            own = g.half_of_shard(refs[0], c)
            _remote(own, own, new, (x, y, 1 - c)).start()

        self.arrays, self.sems, token = _split_copy_call(f"share_start_{self.name}", [quarter], start, new_sems=2)
        return token

    def finish(self, after):
        g = self.geom

        def wait(refs, sems, _):
            x, y, c = _mesh_pos()
            own, theirs = g.half_of_shard(refs[0], c), g.half_of_shard(refs[0], 1 - c)
            _remote(own, own, sems, (x, y, 1 - c)).wait_send()
            _remote(theirs, theirs, sems, (x, y, 1 - c)).wait_recv()

        (quarter,), _, _ = _split_copy_call(f"share_wait_{self.name}", self.arrays, wait, self.sems, after)
        return quarter


def _pair_add(name, grad, recv, geom, c_idx):
    r, c = geom.half_shape
    tr, tc = _tile(r, 256, 16), _tile(c, 2048, 128)
    nr, ncol = r // tr, c // tc
    if geom.col:
        mine = lambda i, j, cref: (cref[0] * nr + i, j)
    else:
        mine = lambda i, j, cref: (i, cref[0] * ncol + j)

    def body(c_ref, g_ref, r_ref, o_ref):
        o_ref[...] = (g_ref[...].astype(F32) + r_ref[...].astype(F32)).astype(BF16)

    return pl.pallas_call(
        body, name=name,
        grid_spec=pltpu.PrefetchScalarGridSpec(
            num_scalar_prefetch=1, grid=(nr, ncol),
            in_specs=[pl.BlockSpec((tr, tc), mine), pl.BlockSpec((tr, tc), lambda i, j, cref: (i, j))],
            out_specs=pl.BlockSpec((tr, tc), lambda i, j, cref: (i, j))),
        out_shape=jax.ShapeDtypeStruct((r, c), BF16),
        compiler_params=_params(("parallel", "parallel"), 3 * _nbytes((tr, tc), F32)),
    )(c_idx, grad, recv)


def _chip_add(name, half, recv, geom, idx):
    r, c = geom.shard_half_shape
    tr, tc = _tile(r, 256, 16), _tile(c, 2048, 128)
    nr, ncol = r // tr, c // tc
    if geom.col:
        mine = lambda i, j, iref: (i, iref[0] * ncol + j)
        place = lambda i, j, iref: (iref[1] * nr + i, j)
    else:
        mine = lambda i, j, iref: (iref[0] * nr + i, j)
        place = lambda i, j, iref: (i, iref[1] * ncol + j)

    def body(i_ref, h_ref, r_ref, o_ref):
        acc = h_ref[...].astype(F32)
        for k in range(3):
            acc = acc + r_ref[k].astype(F32)
        o_ref[...] = acc

    return pl.pallas_call(
        body, name=name,
        grid_spec=pltpu.PrefetchScalarGridSpec(
            num_scalar_prefetch=1, grid=(nr, ncol),
            in_specs=[pl.BlockSpec((tr, tc), mine), pl.BlockSpec((3, tr, tc), lambda i, j, iref: (0, i, j))],
            out_specs=pl.BlockSpec((tr, tc), place)),
        out_shape=jax.ShapeDtypeStruct(geom.shard_shape, F32),
        compiler_params=_params(("parallel", "parallel"), 4 * _nbytes((tr, tc), F32)),
    )(idx, half, recv)


def _all_reduce_small(pack):
    r, d = pack.shape

    def body(p_ref, o_ref, slots, send_sems, recv_sems):
        x, y, c = _mesh_pos()
        me = 4 * x + 2 * y + c
        slots[me] = p_ref[...]
        copies = []
        for k in range(1, N_DEV):
            px, py, pc = x ^ ((k >> 2) & 1), y ^ ((k >> 1) & 1), c ^ (k & 1)
            copies.append(pltpu.make_async_remote_copy(
                src_ref=p_ref, dst_ref=slots.at[me], send_sem=send_sems.at[k - 1], recv_sem=recv_sems.at[k - 1],
                device_id=(px, py, pc), device_id_type=MESH))
        for cp in copies:
            cp.start()
        for k in range(1, N_DEV):
            peer = 4 * (x ^ ((k >> 2) & 1)) + 2 * (y ^ ((k >> 1) & 1)) + (c ^ (k & 1))
            pltpu.make_async_remote_copy(
                src_ref=p_ref, dst_ref=slots.at[peer], send_sem=send_sems.at[k - 1], recv_sem=recv_sems.at[k - 1],
                device_id=(x, y, c), device_id_type=MESH).wait_recv()
        for cp in copies:
            cp.wait_send()
        acc = slots[0]
        for k in range(1, N_DEV):
            acc = acc + slots[k]
        o_ref[...] = acc

    vm = pl.BlockSpec(memory_space=pltpu.VMEM)
    return pl.pallas_call(
        body, name="all_reduce_small", in_specs=[vm], out_specs=vm,
        out_shape=jax.ShapeDtypeStruct((r, d), F32),
        scratch_shapes=[pltpu.VMEM((N_DEV, r, d), F32), pltpu.SemaphoreType.DMA((N_DEV - 1,)),
                        pltpu.SemaphoreType.DMA((N_DEV - 1,))],
    )(pack)


def _pack_rows(rows, d):
    out = []
    for a in rows:
        flat = a.reshape(-1)
        n = -(-flat.shape[0] // d) * d
        out.append(jnp.pad(flat, (0, n - flat.shape[0])).reshape(-1, d))
    packed = jnp.concatenate(out, axis=0)
    return jnp.pad(packed, ((0, 16 - packed.shape[0]), (0, 0)))


def _unpack_rows(packed, shapes, d):
    out, row = [], 0
    for shp in shapes:
        n = int(np.prod(shp))
        nrows = -(-n // d)
        out.append(packed[row:row + nrows].reshape(-1)[:n].reshape(shp))
        row += nrows
    return out


def kernel(x, pre_norm_ffn1, post_norm_ffn1, w_ffn1_gate_up, w_ffn1_down, pre_norm_mix, post_norm_mix, w_mix_in, hgrn_lower_bounds_fwd, hgrn_lower_bounds_bwd, hgrn_out_norm, attn_sink, w_mix_out, pre_norm_ffn2, post_norm_ffn2, w_ffn2_gate_up, w_ffn2_down, rel_bias_table, loss_target, m_pre_norm_ffn1, m_post_norm_ffn1, m_w_ffn1_gate_up, m_w_ffn1_down, m_pre_norm_mix, m_post_norm_mix, m_w_mix_in, m_hgrn_lower_bounds_fwd, m_hgrn_lower_bounds_bwd, m_hgrn_out_norm, m_attn_sink, m_w_mix_out, m_pre_norm_ffn2, m_post_norm_ffn2, m_w_ffn2_gate_up, m_w_ffn2_down, m_rel_bias_table, v_pre_norm_ffn1, v_post_norm_ffn1, v_w_ffn1_gate_up, v_w_ffn1_down, v_pre_norm_mix, v_post_norm_mix, v_w_mix_in, v_hgrn_lower_bounds_fwd, v_hgrn_lower_bounds_bwd, v_hgrn_out_norm, v_attn_sink, v_w_mix_out, v_pre_norm_ffn2, v_post_norm_ffn2, v_w_ffn2_gate_up, v_w_ffn2_down, v_rel_bias_table):
    t, d = x.shape[1], x.shape[2]
    hw = hgrn_out_norm.shape[1]
    aw = d - hw
    nah = aw // HEAD
    kvw = KV_HEADS * HEAD
    x0 = x[0]
    target = loss_target[0]

    big_names = ["w_ffn1_gate_up", "w_ffn1_down", "w_mix_in", "w_mix_out", "w_ffn2_gate_up", "w_ffn2_down"]
    big_w = [w_ffn1_gate_up[0], w_ffn1_down[0], w_mix_in[0], w_mix_out[0], w_ffn2_gate_up[0], w_ffn2_down[0]]
    big_m = [m_w_ffn1_gate_up[0], m_w_ffn1_down[0], m_w_mix_in[0], m_w_mix_out[0], m_w_ffn2_gate_up[0],
             m_w_ffn2_down[0]]
    big_v = [v_w_ffn1_gate_up[0], v_w_ffn1_down[0], v_w_mix_in[0], v_w_mix_out[0], v_w_ffn2_gate_up[0],
             v_w_ffn2_down[0]]
    col_sharded = [True, False, True, False, True, False]
    geoms = [_Big(w.shape, cs) for w, cs in zip(big_w, col_sharded)]

    cx, cy, cc = _mesh_pos()
    idx = jnp.stack([2 * cx + cy, cc]).astype(jnp.int32)
    c_idx = jnp.reshape(cc, (1,)).astype(jnp.int32)
    own_quarters = [_cast_into_full(f"cast_{n}", w, gm, idx) for n, w, gm in zip(big_names, big_w, geoms)]
    started, gather_sems = _gather_start(own_quarters, geoms)

    def forward_weight(w, after):
        return _gather_forward(f"gather_forward_{big_names[w]}", started[w], geoms[w], gather_sems[w], after)

    def whole_weight(w, forwarded, after):
        return _gather_end(f"gather_end_{big_names[w]}", forwarded[0], geoms[w], forwarded[1], after)

    h1 = _norm_fwd("ffn1_pre_norm", x0, pre_norm_ffn1)
    w_gu1 = whole_weight(0, forward_weight(0, h1), h1)
    gu1 = _mm("ffn1_gate_up", h1, w_gu1, "nn", BF16)
    fw = forward_weight(1, gu1)
    act1 = _swiglu_fwd("ffn1_act", gu1)
    w_d1 = whole_weight(1, fw, act1)
    ff1 = _mm("ffn1_down", act1, w_d1, "nn", F32)
    fw = forward_weight(2, ff1)
    x1, hm = _resid_norm_fwd("ffn1_residual", x0, ff1, post_norm_ffn1, pre_norm_mix, 0.5)
    w_in = whole_weight(2, fw, hm)
    p = _mm("mix_in", hm, w_in, "nn", F32)
    fw = forward_weight(3, p)
    o_f, o_b, st_f, st_b = _hgrn_scan_fwd("hgrn_scan", p, hgrn_lower_bounds_fwd, hgrn_lower_bounds_bwd)
    y_h = _hgrn_out_fwd("hgrn_out", o_f, o_b, p, hgrn_out_norm, 4)
    kv_blk0 = (5 * hw + aw) // kvw
    k_pad = _pad_kv("attn_pad_k", p, kv_blk0, kvw)
    v_pad = _pad_kv("attn_pad_v", p, kv_blk0 + 1, kvw)
    bucket_ids = _t5_bucket_ids()
    bias = _bias_gather("attn_bias", rel_bias_table.T, bucket_ids).reshape(nah, WINDOW, SPAN)
    y_a = _attn_fwd("attn_fwd", p, k_pad, v_pad, bias, attn_sink, 5 * hw // aw)
    y_mix = _concat_cols("mix_concat", y_h, y_a)
    w_out = whole_weight(3, fw, y_mix)
    mixed = _mm("mix_out", y_mix, w_out, "nn", F32)
    fw = forward_weight(4, mixed)
    x2, h2 = _resid_norm_fwd("mix_residual", x1, mixed, post_norm_mix, pre_norm_ffn2, 1.0)
    w_gu2 = whole_weight(4, fw, h2)
    gu2 = _mm("ffn2_gate_up", h2, w_gu2, "nn", BF16)
    fw = forward_weight(5, gu2)
    act2 = _swiglu_fwd("ffn2_act", gu2)
    w_d2 = whole_weight(5, fw, act2)
    ff2 = _mm("ffn2_down", act2, w_d2, "nn", F32)
    loss_blk, dy, dff2, dg_post2 = _final_fwd_bwd("ffn2_residual_loss", x2, ff2, post_norm_ffn2, target, 0.5)

    reduce = [_GradReduce(n, gm, idx, c_idx) for n, gm in zip(big_names, geoms)]
    big_grads, big_delta, big_new_m, big_new_v = [None] * 6, [None] * 6, [None] * 6, [None] * 6

    def update(w, after):
        g = reduce[w].finish(after)
        dl, nm, nv = _adamw(f"adamw_{big_names[w]}", big_w[w], g, big_m[w], big_v[w])
        big_grads[w], big_delta[w], big_new_m[w], big_new_v[w] = g[None], dl[None], nm[None], nv[None]
        return dl

    da2 = _mm("ffn2_dact", dff2, w_d2, "nt", F32)
    dw_d2 = _mm("ffn2_dw_down", act2, dff2, "tn", BF16)
    tok = reduce[5].pair_start(dw_d2)
    dgu2 = _swiglu_bwd("ffn2_dact_bwd", da2, gu2, after=tok)
    tok = reduce[5].pair_finish(dgu2)
    dw_gu2 = _ffn_dw_gate_up("ffn2_dw_gate_up", h2, dgu2, after=tok)
    tok = reduce[4].pair_start(dw_gu2)
    dh2 = _ffn_dh("ffn2_dh", dgu2, w_gu2, after=tok)
    tok = reduce[4].pair_finish(dh2)
    dx2, dg_pre2, dmixed, dg_postm = _norms_bwd("mix_residual_bwd", dy, dh2, x2, pre_norm_ffn2,
                                                post=(mixed, post_norm_mix, 1.0), after=tok)
    dw_out = _mm("mix_out_dw", y_mix, dmixed, "tn", BF16)
    tok = reduce[3].pair_start(dw_out)
    dy_mix = _mm("mix_out_dx", dmixed, w_out, "nt", F32, after=tok)
    tok = reduce[3].pair_finish(dy_mix)
    dq_a, dk_pad, dv_pad, dbias, dsink = _attn_bwd("attn_bwd", p, k_pad, v_pad, bias, attn_sink, dy_mix,
                                                   5 * hw // aw, hw // aw, after=tok)
    tok = reduce[5].chip_finish(dq_a)
    drel_t = _bias_scatter("attn_dbias", dbias.reshape(nah, WINDOW * SPAN), bucket_ids)
    do, dg_h, dgain = _hgrn_out_bwd("hgrn_out_bwd", dy_mix, o_f, o_b, p, hgrn_out_norm, 4, after=tok)
    dq_f, dv_f, dz_f, dlb_f, dq_b, dv_b, dz_b, dlb_b = _hgrn_scan_bwd(
        "hgrn_scan_bwd", p, hgrn_lower_bounds_fwd, hgrn_lower_bounds_bwd, do, st_f, st_b)
    tok = reduce[4].chip_finish(dq_f)
    tok = reduce[3].chip_finish(tok)
    dp = _mix_dproj("mix_dproj", [(dq_f, dq_b), (dv_f, dv_b), (dz_f,), (dz_b,), (dg_h,), (dq_a,)],
                    [dk_pad, dv_pad], t, after=tok)
    dw_in = _mm("mix_in_dw", hm, dp, "tn", BF16)
    tok = reduce[2].pair_start(dw_in)
    dhm = _mm("mix_in_dx", dp, w_in, "nt", F32, after=tok)
    tok = reduce[2].pair_finish(dhm)
    dx1, dg_prem, dff1, dg_post1 = _norms_bwd("ffn1_residual_bwd", dx2, dhm, x1, pre_norm_mix,
                                              post=(ff1, post_norm_ffn1, 0.5), after=tok)
    da1 = _mm("ffn1_dact", dff1, w_d1, "nt", F32)
    dw_d1 = _mm("ffn1_dw_down", act1, dff1, "tn", BF16)
    tok = reduce[1].pair_start(dw_d1)
    dgu1 = _swiglu_bwd("ffn1_dact_bwd", da1, gu1, after=tok)
    tok = reduce[1].pair_finish(dgu1)
    tok = reduce[2].chip_finish(tok)
    dw_gu1 = _ffn_dw_gate_up("ffn1_dw_gate_up", h1, dgu1, after=tok)
    tok = reduce[0].pair_start(dw_gu1)
    done = update(2, tok)
    tok = reduce[0].pair_finish(done)
    dh1 = _ffn_dh("ffn1_dh", dgu1, w_gu1, after=tok)
    grad_x, dg_pre1 = _norms_bwd("ffn1_pre_norm_bwd", dx1, dh1, x0, pre_norm_ffn1)

    small_w = [pre_norm_ffn1, post_norm_ffn1, pre_norm_mix, post_norm_mix, hgrn_lower_bounds_fwd,
               hgrn_lower_bounds_bwd, hgrn_out_norm, attn_sink, pre_norm_ffn2, post_norm_ffn2, rel_bias_table]
    small_m = [m_pre_norm_ffn1, m_post_norm_ffn1, m_pre_norm_mix, m_post_norm_mix, m_hgrn_lower_bounds_fwd,
               m_hgrn_lower_bounds_bwd, m_hgrn_out_norm, m_attn_sink, m_pre_norm_ffn2, m_post_norm_ffn2,
               m_rel_bias_table]
    small_v = [v_pre_norm_ffn1, v_post_norm_ffn1, v_pre_norm_mix, v_post_norm_mix, v_hgrn_lower_bounds_fwd,
               v_hgrn_lower_bounds_bwd, v_hgrn_out_norm, v_attn_sink, v_pre_norm_ffn2, v_post_norm_ffn2,
               v_rel_bias_table]
    small_g = [dg_pre1, dg_post1, dg_prem, dg_postm, dlb_f, dlb_b, dgain, dsink[:, 0].reshape(1, nah), dg_pre2,
               dg_post2, drel_t.T]
    shapes = [a.shape for a in small_w]
    summed = _all_reduce_small(_pack_rows(small_g + [loss_blk[0:1, 0:1]], d))
    g_pack = summed
    loss =_unpack_rows(summed, shapes + [(1, 1)], d)[-1][0, 0]
    sd, sm, sv = _adamw("adamw_small", _pack_rows(small_w, d), g_pack, _pack_rows(small_m, d), _pack_rows(small_v, d))
    small_grads = _unpack_rows(g_pack, shapes, d)
    small_delta, small_new_m, small_new_v = (_unpack_rows(a, shapes, d) for a in (sd, sm, sv))

    done = update(5, sd)
    done = update(4, done)
    done = update(3, done)
    tok = reduce[1].chip_finish(done)
    done = update(1, tok)
    tok = reduce[0].chip_finish(done)
    update(0, tok)

    def ordered(small, big):
        s = dict(zip(["pre1", "post1", "prem", "postm", "lbf", "lbb", "gain", "sink", "pre2", "post2", "rel"], small))
        b = dict(zip(["gu1", "d1", "win", "wout", "gu2", "d2"], big))
        return [s["pre1"], s["post1"], b["gu1"], b["d1"], s["prem"], s["postm"], b["win"], s["lbf"], s["lbb"],
                s["gain"], s["sink"], b["wout"], s["pre2"], s["post2"], b["gu2"], b["d2"], s["rel"]]

    return (loss, grad_x[None], *ordered(small_grads, big_grads), *ordered(small_delta, big_delta),
            *ordered(small_new_m, big_new_m), *ordered(small_new_v, big_new_v))
```

```python
import functools
import math

import jax
import jax.numpy as jnp
import numpy as np
from jax import lax
from jax.experimental import pallas as pl
from jax.experimental.pallas import tpu as pltpu

F32 = jnp.float32
BF16 = jnp.bfloat16

HEAD = 128
CHUNK = 64
WINDOW = 128
SPAN = 3 * WINDOW
KV_HEADS = 2
REL_BUCKETS = 32
REL_MAX_DIST = 128
EPS = 1e-6
NEG_INF = -1e30

ADAM_LR = 0.001
ADAM_B1 = 0.9
ADAM_B2 = 0.999
ADAM_EPS = 1e-08
ADAM_WD = 0.01
ADAM_STEP = 10

N_CHIPS = 4
N_DEV = 8
V7X_VMEM_BYTES = 64 * 1024 * 1024
MESH = pl.DeviceIdType.MESH
ANY = pl.BlockSpec(memory_space=pl.ANY)


def _tile(n, pref, mult):
    t = (min(pref, n) // mult) * mult
    while t >= mult:
        if n % t == 0:
            return t
        t -= mult
    return n


def _params(semantics, block_bytes):
    limit = min(V7X_VMEM_BYTES - (4 << 20), 2 * int(block_bytes) + (8 << 20))
    return pltpu.CompilerParams(dimension_semantics=semantics, vmem_limit_bytes=limit)


def _nbytes(shape, dtype):
    return int(np.prod(shape)) * jnp.dtype(dtype).itemsize


PIN_TO_HBM_BYTES = 4 << 20


def _pallas(body, **kw):
    def pin_shape(s):
        if isinstance(s, jax.ShapeDtypeStruct) and _nbytes(s.shape, s.dtype) >= PIN_TO_HBM_BYTES:
            return pltpu.HBM(s.shape, s.dtype)
        return s

    def pin(a):
        if getattr(a, "dtype", None) in (F32, BF16) and _nbytes(a.shape, a.dtype) >= PIN_TO_HBM_BYTES:
            return pltpu.with_memory_space_constraint(a, pltpu.HBM)
        return a

    out_shape = kw["out_shape"]
    kw["out_shape"] = [pin_shape(s) for s in out_shape] if isinstance(out_shape, (list, tuple)) else pin_shape(out_shape)
    call = pl.pallas_call(body, **kw)
    return lambda *args: call(*[pin(a) for a in args])


def _dot(a, b, ca=1, cb=0):
    return lax.dot_general(a, b, (((ca,), (cb,)), ((), ())), preferred_element_type=F32)


def _split3(x):
    hi = x.astype(BF16)
    r1 = x - hi.astype(F32)
    mid = r1.astype(BF16)
    lo = (r1 - mid.astype(F32)).astype(BF16)
    return hi, mid, lo


def _dot_exact(a, b, ca=1, cb=0, split="b"):
    if split == "b":
        return sum(_dot(a, p, ca, cb) for p in _split3(b))
    return sum(_dot(p, b, ca, cb) for p in _split3(a))


def _rms(x):
    return lax.rsqrt(jnp.mean(x * x, axis=-1, keepdims=True) + EPS)


def _norm_bwd(u, x, gain):
    r = _rms(x)
    xhat = x * r
    dgain = jnp.sum(u * xhat, axis=0, keepdims=True)
    v = u * gain
    dx = r * (v - xhat * jnp.mean(v * xhat, axis=-1, keepdims=True))
    return dx, dgain


def _sigmoid(x):
    return 1.0 / (1.0 + jnp.exp(-x))


def _accumulate(ref, val, first):
    @pl.when(first)
    def _():
        ref[...] = val

    @pl.when(jnp.logical_not(first))
    def _():
        ref[...] += val


def _ordered(body, ins, in_specs, after):
    if after is None:
        return body, list(ins), list(in_specs)
    n_in = len(ins)

    def wrapped(*refs):
        body(*refs[:n_in], *refs[n_in + 1:])

    return wrapped, list(ins) + [after], list(in_specs) + [pl.BlockSpec(memory_space=pl.ANY)]


def _matmul(name, a, b, *, form, out_dtype, tm, tn, tk, a_map=None, b_map=None,
            out_shape=None, out_block=None, out_map=None, sizes=None, after=None):
    if sizes is None:
        if form == "nn":
            (m, k), n = a.shape, b.shape[1]
        elif form == "nt":
            (m, k), n = a.shape, b.shape[0]
        else:
            (k, m), n = a.shape, b.shape[1]
    else:
        m, n, k = sizes
    gi, gj, gk = m // tm, n // tn, k // tk
    a_blk = (tm, tk) if form != "tn" else (tk, tm)
    b_blk = (tk, tn) if form != "nt" else (tn, tk)
    if a_map is None:
        a_map = (lambda i, j, kk: (i, kk)) if form != "tn" else (lambda i, j, kk: (kk, i))
    else:
        a_blk = (None,) + a_blk
    if b_map is None:
        b_map = (lambda i, j, kk: (kk, j)) if form != "nt" else (lambda i, j, kk: (j, kk))
    else:
        b_blk = (None,) + b_blk
    if out_shape is None:
        out_shape, out_block, out_map = (m, n), (tm, tn), (lambda i, j, kk: (i, j))
    ca, cb = {"nn": (1, 0), "nt": (1, 1), "tn": (0, 0)}[form]

    def body(a_ref, b_ref, o_ref, *acc):
        part = _dot(a_ref[...], b_ref[...], ca, cb)
        if gk == 1:
            o_ref[...] = part.astype(o_ref.dtype)
        else:
            kk = pl.program_id(2)
            _accumulate(acc[0], part, kk == 0)

            @pl.when(kk == gk - 1)
            def _():
                o_ref[...] = acc[0][...].astype(o_ref.dtype)

    scratch = [] if gk == 1 else [pltpu.VMEM((tm, tn), F32)]
    vmem = (_nbytes((tm, tk), a.dtype) + _nbytes((tk, tn), b.dtype) + _nbytes((tm, tn), out_dtype)
            + 2 * _nbytes((tm, tn), F32))
    body, ins, in_specs = _ordered(body, [a, b], [pl.BlockSpec(a_blk, a_map), pl.BlockSpec(b_blk, b_map)], after)
    return _pallas(
        body, name=name, grid=(gi, gj, gk), in_specs=in_specs,
        out_specs=pl.BlockSpec(out_block, out_map),
        out_shape=jax.ShapeDtypeStruct(out_shape, out_dtype),
        scratch_shapes=scratch,
        compiler_params=_params(("parallel", "parallel", "arbitrary"), vmem),
    )(*ins)


def _mm_tiles(m, n, k):
    return _tile(m, 1024, 128), _tile(n, 512, 128), _tile(k, 2816, 128)


def _mm(name, a, b, form, out_dtype, after=None):
    if form == "nn":
        m, k, n = a.shape[0], a.shape[1], b.shape[1]
    elif form == "nt":
        m, k, n = a.shape[0], a.shape[1], b.shape[0]
    else:
        m, k, n = a.shape[1], a.shape[0], b.shape[1]
    tm, tn, tk = _mm_tiles(m, n, k)
    return _matmul(name, a, b, form=form, out_dtype=out_dtype, tm=tm, tn=tn, tk=tk, after=after)


def _row_tile(t):
    return _tile(t, 256, 8)


def _norm_fwd(name, x, gain):
    t, d = x.shape
    tm = _row_tile(t)

    def body(x_ref, g_ref, h_ref):
        xv = x_ref[...]
        h_ref[...] = (xv * _rms(xv) * g_ref[...]).astype(BF16)

    row = pl.BlockSpec((tm, d), lambda i: (i, 0))
    vec = pl.BlockSpec((1, d), lambda i: (0, 0))
    return _pallas(
        body, name=name, grid=(t // tm,), in_specs=[row, vec], out_specs=row,
        out_shape=jax.ShapeDtypeStruct((t, d), BF16),
        compiler_params=_params(("parallel",), 2 * _nbytes((tm, d), F32)),
    )(x, gain)


def _resid_norm_fwd(name, xres, ff, gpost, gpre, scale):
    t, d = xres.shape
    tm = _row_tile(t)

    def body(x_ref, f_ref, gp_ref, gn_ref, xn_ref, h_ref):
        f = f_ref[...]
        xn = x_ref[...] + scale * (f * _rms(f) * gp_ref[...])
        xn_ref[...] = xn
        h_ref[...] = (xn * _rms(xn) * gn_ref[...]).astype(BF16)

    row = pl.BlockSpec((tm, d), lambda i: (i, 0))
    vec = pl.BlockSpec((1, d), lambda i: (0, 0))
    return _pallas(
        body, name=name, grid=(t // tm,), in_specs=[row, row, vec, vec], out_specs=[row, row],
        out_shape=[jax.ShapeDtypeStruct((t, d), F32), jax.ShapeDtypeStruct((t, d), BF16)],
        compiler_params=_params(("parallel",), 4 * _nbytes((tm, d), F32)),
    )(xres, ff, gpost, gpre)


def _final_fwd_bwd(name, xres, ff, gpost, target, scale):
    t, d = xres.shape
    tm = _row_tile(t)

    def body(x_ref, f_ref, gp_ref, t_ref, loss_ref, dy_ref, dff_ref, dg_ref):
        i = pl.program_id(0)
        f = f_ref[...]
        gp = gp_ref[...]
        y = x_ref[...] + scale * (f * _rms(f) * gp)
        err = y - t_ref[...]
        part = 0.5 * jnp.sum(jnp.mean(err * err, axis=-1, keepdims=True), axis=0, keepdims=True)
        _accumulate(loss_ref, jnp.broadcast_to(part, loss_ref.shape), i == 0)
        dy = err / d
        dy_ref[...] = dy
        dff, dg = _norm_bwd(scale * dy, f, gp)
        dff_ref[...] = dff.astype(BF16)
        _accumulate(dg_ref, dg, i == 0)

    row = pl.BlockSpec((tm, d), lambda i: (i, 0))
    vec = pl.BlockSpec((1, d), lambda i: (0, 0))
    return _pallas(
        body, name=name, grid=(t // tm,), in_specs=[row, row, vec, row],
        out_specs=[pl.BlockSpec((8, 128), lambda i: (0, 0)), row, row, vec],
        out_shape=[jax.ShapeDtypeStruct((8, 128), F32), jax.ShapeDtypeStruct((t, d), F32),
                   jax.ShapeDtypeStruct((t, d), BF16), jax.ShapeDtypeStruct((1, d), F32)],
        compiler_params=_params(("arbitrary",), 5 * _nbytes((tm, d), F32)),
    )(xres, ff, gpost, target)


def _norms_bwd(name, dres, dh, xin, gpre, post=None, after=None):
    t, d = dres.shape
    tm = _row_tile(t)
    with_post = post is not None

    def body(*refs):
        if with_post:
            dr_ref, dh_ref, x_ref, g_ref, f_ref, gp_ref, dx_ref, dg_ref, dff_ref, dgp_ref = refs
        else:
            dr_ref, dh_ref, x_ref, g_ref, dx_ref, dg_ref = refs
        i = pl.program_id(0)
        dx, dg = _norm_bwd(dh_ref[...], x_ref[...], g_ref[...])
        dx = dr_ref[...] + dx
        dx_ref[...] = dx
        _accumulate(dg_ref, dg, i == 0)
        if with_post:
            dff, dgp = _norm_bwd(post[2] * dx, f_ref[...], gp_ref[...])
            dff_ref[...] = dff.astype(BF16)
            _accumulate(dgp_ref, dgp, i == 0)

    row = pl.BlockSpec((tm, d), lambda i: (i, 0))
    vec = pl.BlockSpec((1, d), lambda i: (0, 0))
    ins, in_specs = [dres, dh, xin, gpre], [row, row, row, vec]
    out_specs = [row, vec]
    out_shape = [jax.ShapeDtypeStruct((t, d), F32), jax.ShapeDtypeStruct((1, d), F32)]
    if with_post:
        ins += [post[0], post[1]]
        in_specs += [row, vec]
        out_specs += [row, vec]
        out_shape += [jax.ShapeDtypeStruct((t, d), BF16), jax.ShapeDtypeStruct((1, d), F32)]
    body, ins, in_specs = _ordered(body, ins, in_specs, after)
    return _pallas(
        body, name=name, grid=(t // tm,), in_specs=in_specs, out_specs=out_specs, out_shape=out_shape,
        compiler_params=_params(("arbitrary",), 6 * _nbytes((tm, d), F32)),
    )(*ins)


def _swiglu_fwd(name, gu):
    t, f2 = gu.shape
    f = f2 // 2
    tm, tf = _tile(t, 512, 8), _tile(f, 512, 128)
    nf = f // tf

    def body(g_ref, u_ref, a_ref):
        g = g_ref[...].astype(F32)
        a_ref[...] = (g * _sigmoid(g) * u_ref[...].astype(F32)).astype(BF16)

    return _pallas(
        body, name=name, grid=(t // tm, nf),
        in_specs=[pl.BlockSpec((tm, tf), lambda i, j: (i, j)), pl.BlockSpec((tm, tf), lambda i, j: (i, j + nf))],
        out_specs=pl.BlockSpec((tm, tf), lambda i, j: (i, j)),
        out_shape=jax.ShapeDtypeStruct((t, f), BF16),
        compiler_params=_params(("parallel", "parallel"), 3 * _nbytes((tm, tf), F32)),
    )(gu, gu)


def _swiglu_bwd(name, da, gu, after=None):
    t, f = da.shape
    tm, tf = _tile(t, 512, 8), _tile(f, 512, 128)
    nf = f // tf

    def body(da_ref, g_ref, u_ref, o_ref):
        g = g_ref[...].astype(F32)
        u = u_ref[...].astype(F32)
        dav = da_ref[...]
        sig = _sigmoid(g)
        o_ref[0] = (dav * u * sig * (1.0 + g * (1.0 - sig))).astype(BF16)
        o_ref[1] = (dav * g * sig).astype(BF16)

    body, ins, in_specs = _ordered(
        body, [da, gu, gu],
        [pl.BlockSpec((tm, tf), lambda i, j: (i, j)), pl.BlockSpec((tm, tf), lambda i, j: (i, j)),
         pl.BlockSpec((tm, tf), lambda i, j: (i, j + nf))], after)
    return _pallas(
        body, name=name, grid=(t // tm, nf), in_specs=in_specs,
        out_specs=pl.BlockSpec((2, tm, tf), lambda i, j: (0, i, j)),
        out_shape=jax.ShapeDtypeStruct((2, t, f), BF16),
        compiler_params=_params(("parallel", "parallel"), 5 * _nbytes((tm, tf), F32)),
    )(*ins)


def _ffn_dh(name, dgu, w_gu, after=None):
    _, t, f = dgu.shape
    d = w_gu.shape[0]
    tm, tn, tk = _mm_tiles(t, d, f)
    nkf = f // tk
    return _matmul(name, dgu, w_gu, form="nt", out_dtype=F32, tm=tm, tn=tn, tk=tk, sizes=(t, d, 2 * f),
                   a_map=lambda i, j, kk: (kk // nkf, i, kk % nkf), after=after)


def _ffn_dw_gate_up(name, h, dgu, after=None):
    _, t, f = dgu.shape
    d = h.shape[1]
    tm, tn, tk = _mm_tiles(d, f, t)
    nf = f // tn
    return _matmul(name, h, dgu, form="tn", out_dtype=BF16, tm=tm, tn=tn, tk=tk, sizes=(d, 2 * f, t),
                   b_map=lambda i, j, kk: (j // nf, kk, j % nf), after=after)


def _lower_bound(lbp):
    m = jnp.max(lbp, axis=0, keepdims=True)
    e = jnp.exp(lbp - m)
    return e[0:1] / jnp.sum(e, axis=0, keepdims=True)


def _chunk_mask(reverse):
    row = lax.broadcasted_iota(jnp.int32, (CHUNK, CHUNK), 0)
    col = lax.broadcasted_iota(jnp.int32, (CHUNK, CHUNK), 1)
    return (col >= row) if reverse else (col <= row)


def _hgrn_gates(z, lb, mask_bf):
    sig = _sigmoid(z)
    f = lb + (1.0 - lb) * sig
    logf = jnp.log(f)
    k = 1.0 - f
    cum = _dot_exact(mask_bf, logf)
    last = jnp.sum(logf, axis=0, keepdims=True)
    return sig, f, k, cum, last


def _hgrn_scan_fwd(name, p, lbp_f, lbp_b):
    t = p.shape[0]
    hw = lbp_f.shape[1]
    nh, nc = hw // HEAD, t // CHUNK

    def body(qf, vf, zf, qb, vb, zb, lbf, lbb, of_ref, ob_ref, stf_ref, stb_ref, state):
        n = pl.program_id(0)

        @pl.when(n == 0)
        def _():
            state[...] = jnp.zeros_like(state)

        directions = [(qf, vf, zf, lbf, of_ref, stf_ref), (qb, vb, zb, lbb, ob_ref, stb_ref)]
        for d, (q_ref, v_ref, z_ref, lb_ref, o_ref, st_ref) in enumerate(directions):
            mask = _chunk_mask(d == 1)
            lb = _lower_bound(lb_ref[...])
            _, _, k, cum, last = _hgrn_gates(z_ref[...], lb, mask.astype(BF16))
            v = v_ref[...].astype(BF16)
            qd = (q_ref[...] * jnp.exp(cum)).astype(BF16)
            kd = (k * jnp.exp(-cum)).astype(BF16)
            kt = (k * jnp.exp(last - cum)).astype(BF16)
            dec = jnp.exp(last)
            s_all = state[d]
            st_ref[...] = s_all
            for h in range(nh):
                sl = slice(h * HEAD, (h + 1) * HEAD)
                s_in = s_all[:, sl]
                a = jnp.where(mask, _dot(qd[:, sl], kd[:, sl], 1, 1), 0.0).astype(BF16)
                o_ref[:, sl] = _dot(a, v[:, sl]) + _dot(qd[:, sl], s_in.astype(BF16), 1, 1)
                state[d, :, sl] = s_in * dec[:, sl] + _dot(v[:, sl], kt[:, sl], 0, 0)

    def col(group, reverse):
        return pl.BlockSpec((CHUNK, hw), lambda n: ((nc - 1 - n) if reverse else n, group))

    def st(reverse):
        return pl.BlockSpec((None, HEAD, hw), lambda n: ((nc - 1 - n) if reverse else n, 0, 0))

    lb_spec = pl.BlockSpec((2, hw), lambda n: (0, 0))
    out = jax.ShapeDtypeStruct((t, hw), F32)
    states = jax.ShapeDtypeStruct((nc, HEAD, hw), F32)
    return _pallas(
        body, name=name, grid=(nc,),
        in_specs=[col(0, False), col(1, False), col(2, False), col(0, True), col(1, True), col(3, True),
                  lb_spec, lb_spec],
        out_specs=[col(0, False), col(0, True), st(False), st(True)],
        out_shape=[out, out, states, states],
        scratch_shapes=[pltpu.VMEM((2, HEAD, hw), F32)],
        compiler_params=_params(("arbitrary",), 12 * _nbytes((HEAD, hw), F32)),
    )(p, p, p, p, p, p, lbp_f, lbp_b)


def _hgrn_scan_bwd(name, p, lbp_f, lbp_b, do, st_f, st_b):
    t = p.shape[0]
    hw = lbp_f.shape[1]
    nh, nc = hw // HEAD, t // CHUNK

    def body(qf, vf, zf, dof, sf, qb, vb, zb, dob, sb, lbf, lbb, dqf, dvf, dzf, dlbf, dqb, dvb, dzb, dlbb,
             dstate, dlb_acc, dqd_s, dkd_s, dkt_s, ddec_s):
        n = pl.program_id(0)

        @pl.when(n == 0)
        def _():
            dstate[...] = jnp.zeros_like(dstate)
            dlb_acc[...] = jnp.zeros_like(dlb_acc)

        directions = [(qf, vf, zf, dof, sf, lbf, dqf, dvf, dzf, dlbf), (qb, vb, zb, dob, sb, lbb, dqb, dvb, dzb, dlbb)]
        for d, (q_ref, v_ref, z_ref, do_ref, st_ref, lb_ref, dq_ref, dv_ref, dz_ref, dlb_ref) in enumerate(directions):
            mask = _chunk_mask(d == 1)
            mask_bf = mask.astype(BF16)
            lb = _lower_bound(lb_ref[...])
            sig, f, k, cum, last = _hgrn_gates(z_ref[...], lb, mask_bf)
            e_pos, e_neg, e_tail = jnp.exp(cum), jnp.exp(-cum), jnp.exp(last - cum)
            dec = jnp.exp(last)
            v = v_ref[...].astype(BF16)
            qd, kd, kt = q_ref[...] * e_pos, k * e_neg, k * e_tail
            qd_bf, kd_bf, kt_bf = qd.astype(BF16), kd.astype(BF16), kt.astype(BF16)
            s_all = st_ref[...]
            ds_all = dstate[d]
            dov = do_ref[...].astype(BF16)
            for h in range(nh):
                sl = slice(h * HEAD, (h + 1) * HEAD)
                s_in, ds_out = s_all[:, sl], ds_all[:, sl]
                ds_bf = ds_out.astype(BF16)
                a = jnp.where(mask, _dot(qd_bf[:, sl], kd_bf[:, sl], 1, 1), 0.0).astype(BF16)
                da = jnp.where(mask, _dot(dov[:, sl], v[:, sl], 1, 1), 0.0).astype(BF16)
                dv_ref[:, sl] = _dot(a, dov[:, sl], 0, 0) + _dot(kt_bf[:, sl], ds_bf, 1, 1)
                dqd_s[:, sl] = _dot(da, kd_bf[:, sl]) + _dot(dov[:, sl], s_in.astype(BF16))
                dkd_s[:, sl] = _dot(da, qd_bf[:, sl], 0, 0)
                dkt_s[:, sl] = _dot(v[:, sl], ds_bf)
                dstate[d, :, sl] = _dot(dov[:, sl], qd_bf[:, sl], 0, 0) + ds_out * dec[:, sl]
                ddec_s[:, sl] = jnp.sum(ds_out * s_in, axis=0, keepdims=True)
            dqd, dkd, dkt = dqd_s[...], dkd_s[...], dkt_s[...]
            dlast = jnp.sum(dkt * kt, axis=0, keepdims=True) + dec * ddec_s[...]
            dq_ref[...] = dqd * e_pos
            dk = dkd * e_neg + dkt * e_tail
            dcum = dqd * qd - dkd * kd - dkt * kt
            dlogf = _dot_exact(mask_bf, dcum, 0, 0) + dlast
            df = dlogf / f - dk
            dz_ref[...] = df * (1.0 - lb) * sig * (1.0 - sig)
            dlb_acc[d] += jnp.sum(df * (1.0 - sig), axis=0, keepdims=True)

            @pl.when(n == nc - 1)
            def _():
                g = dlb_acc[d] * lb * (1.0 - lb)
                dlb_ref[0:1, :] = g
                dlb_ref[1:2, :] = -g

    def col(group, reverse):
        return pl.BlockSpec((CHUNK, hw), lambda n: (n if reverse else (nc - 1 - n), group))

    def st(reverse):
        return pl.BlockSpec((None, HEAD, hw), lambda n: (n if reverse else (nc - 1 - n), 0, 0))

    lb_spec = pl.BlockSpec((2, hw), lambda n: (0, 0))
    out = jax.ShapeDtypeStruct((t, hw), F32)
    dlb = jax.ShapeDtypeStruct((2, hw), F32)
    wide = pltpu.VMEM((CHUNK, hw), F32)
    return _pallas(
        body, name=name, grid=(nc,),
        in_specs=[col(0, False), col(1, False), col(2, False), col(0, False), st(False),
                  col(0, True), col(1, True), col(3, True), col(0, True), st(True), lb_spec, lb_spec],
        out_specs=[col(0, False), col(0, False), col(0, False), lb_spec,
                   col(0, True), col(0, True), col(0, True), lb_spec],
        out_shape=[out, out, out, dlb, out, out, out, dlb],
        scratch_shapes=[pltpu.VMEM((2, HEAD, hw), F32), pltpu.VMEM((2, 1, hw), F32), wide, wide, wide,
                        pltpu.VMEM((1, hw), F32)],
        compiler_params=_params(("arbitrary",), 16 * _nbytes((HEAD, hw), F32)),
    )(p, p, p, do, st_f, p, p, p, do, st_b, lbp_f, lbp_b)


def _hgrn_out_fwd(name, o_f, o_b, p, gain, g_group):
    t, hw = o_f.shape
    nh = hw // HEAD
    tm = _tile(t, 512, 8)

    def body(of_ref, ob_ref, g_ref, gain_ref, y_ref):
        o = of_ref[...] + ob_ref[...]
        g = g_ref[...]
        y_ref[...] = (o * _rms(o) * gain_ref[...] * (g * _sigmoid(g))).astype(BF16)

    blk = pl.BlockSpec((tm, HEAD), lambda i, h: (i, h))
    return _pallas(
        body, name=name, grid=(t // tm, nh),
        in_specs=[blk, blk, pl.BlockSpec((tm, HEAD), lambda i, h: (i, g_group * nh + h)),
                  pl.BlockSpec((1, HEAD), lambda i, h: (0, h))],
        out_specs=blk, out_shape=jax.ShapeDtypeStruct((t, hw), BF16),
        compiler_params=_params(("parallel", "parallel"), 1 << 20),
    )(o_f, o_b, p, gain)


def _hgrn_out_bwd(name, dy, o_f, o_b, p, gain, g_group, after=None):
    t, hw = o_f.shape
    nh = hw // HEAD
    tm = _tile(t, 512, 8)

    def body(dy_ref, of_ref, ob_ref, g_ref, gain_ref, do_ref, dg_ref, dgain_ref):
        i = pl.program_id(1)
        o = of_ref[...] + ob_ref[...]
        g = g_ref[...]
        gain_v = gain_ref[...]
        sig = _sigmoid(g)
        dyv = dy_ref[...]
        do, dgain = _norm_bwd(dyv * (g * sig), o, gain_v)
        do_ref[...] = do
        dg_ref[...] = dyv * (o * _rms(o) * gain_v) * sig * (1.0 + g * (1.0 - sig))
        _accumulate(dgain_ref, dgain, i == 0)

    blk = pl.BlockSpec((tm, HEAD), lambda h, i: (i, h))
    vec = pl.BlockSpec((1, HEAD), lambda h, i: (0, h))
    out = jax.ShapeDtypeStruct((t, hw), F32)
    body, ins, in_specs = _ordered(
        body, [dy, o_f, o_b, p, gain],
        [blk, blk, blk, pl.BlockSpec((tm, HEAD), lambda h, i: (i, g_group * nh + h)), vec], after)
    return _pallas(
        body, name=name, grid=(nh, t // tm), in_specs=in_specs,
        out_specs=[blk, blk, vec], out_shape=[out, out, jax.ShapeDtypeStruct((1, hw), F32)],
        compiler_params=_params(("parallel", "arbitrary"), 1 << 20),
    )(*ins)


def _t5_bucket_ids():
    c = np.arange(WINDOW)[:, None]
    s = np.arange(SPAN)[None, :]
    rel = s - WINDOW - c
    nb = REL_BUCKETS // 2
    max_exact = nb // 2
    bucket = (rel > 0).astype(np.int32) * nb
    n = np.abs(rel)
    large = max_exact + (np.log(np.maximum(n, 1) / max_exact) / np.log(REL_MAX_DIST / max_exact)
                         * (nb - max_exact)).astype(np.int32)
    large = np.minimum(large, nb - 1)
    ids = bucket + np.where(n < max_exact, n, large).astype(np.int32)
    return jnp.asarray(ids.reshape(1, WINDOW * SPAN), jnp.int32)


def _bias_onehot(ids_ref):
    n = ids_ref.shape[1]
    return (lax.broadcasted_iota(jnp.int32, (REL_BUCKETS, n), 0) == ids_ref[...]).astype(BF16)


def _bias_gather(name, table_t, ids):
    nh = table_t.shape[0]

    def body(t_ref, ids_ref, o_ref):
        o_ref[...] = _dot_exact(t_ref[...], _bias_onehot(ids_ref), split="a")

    return _pallas(
        body, name=name, out_shape=jax.ShapeDtypeStruct((nh, ids.shape[1]), F32),
        compiler_params=pltpu.CompilerParams(vmem_limit_bytes=32 << 20),
    )(table_t, ids)


def _bias_scatter(name, dbias, ids):
    nh = dbias.shape[0]

    def body(d_ref, ids_ref, o_ref):
        o_ref[...] = _dot_exact(d_ref[...], _bias_onehot(ids_ref), 1, 1, split="a")

    return _pallas(
        body, name=name, out_shape=jax.ShapeDtypeStruct((nh, REL_BUCKETS), F32),
        compiler_params=pltpu.CompilerParams(vmem_limit_bytes=32 << 20),
    )(dbias, ids)


def _attn_valid(i, t):
    c = lax.broadcasted_iota(jnp.int32, (WINDOW, SPAN), 0)
    s = lax.broadcasted_iota(jnp.int32, (WINDOW, SPAN), 1)
    rel = s - WINDOW - c
    pos = i * WINDOW - WINDOW + s
    return (jnp.abs(rel) <= WINDOW) & (pos >= 0) & (pos < t)


def _attn_probs(qh, kh, bias_h, sink_h, valid):
    s = _dot(qh, kh, 1, 1) / math.sqrt(HEAD)
    s = jnp.where(valid, s + bias_h, NEG_INF)
    m = jnp.maximum(jnp.max(s, axis=-1, keepdims=True), sink_h)
    e = jnp.exp(s - m)
    es = jnp.exp(sink_h - m)
    inv = 1.0 / (jnp.sum(e, axis=-1, keepdims=True) + es)
    return e * inv, es * inv


def _attn_fwd(name, p, k_pad, v_pad, bias, sink, q_group_blk):
    t = p.shape[0]
    nh = bias.shape[0]
    aw = nh * HEAD
    grp = nh // KV_HEADS
    nb = t // WINDOW

    def body(q_ref, k_ref, v_ref, b_ref, s_ref, y_ref):
        i = pl.program_id(0)
        valid = _attn_valid(i, t)
        start = pl.multiple_of(i * WINDOW, WINDOW)
        ks = k_ref[pl.ds(start, SPAN), :]
        vs = v_ref[pl.ds(start, SPAN), :]
        for h in range(nh):
            kv = h // grp
            qh = q_ref[:, h * HEAD:(h + 1) * HEAD].astype(BF16)
            pr, _ = _attn_probs(qh, ks[:, kv * HEAD:(kv + 1) * HEAD], b_ref[h], s_ref[0:1, h:h + 1], valid)
            y_ref[:, h * HEAD:(h + 1) * HEAD] = _dot(pr.astype(BF16), vs[:, kv * HEAD:(kv + 1) * HEAD]).astype(BF16)

    full = lambda a: pl.BlockSpec(a.shape, lambda i: (0,) * a.ndim)
    return _pallas(
        body, name=name, grid=(nb,),
        in_specs=[pl.BlockSpec((WINDOW, aw), lambda i: (i, q_group_blk)), full(k_pad), full(v_pad), full(bias),
                  full(sink)],
        out_specs=pl.BlockSpec((WINDOW, aw), lambda i: (i, 0)),
        out_shape=jax.ShapeDtypeStruct((t, aw), BF16),
        compiler_params=_params(("parallel",), _nbytes(k_pad.shape, BF16) * 2 + _nbytes(bias.shape, F32)),
    )(p, k_pad, v_pad, bias, sink)


def _attn_bwd(name, p, k_pad, v_pad, bias, sink, dy, q_group_blk, dy_blk, after=None):
    t = p.shape[0]
    nh = bias.shape[0]
    aw = nh * HEAD
    grp = nh // KV_HEADS
    nb = t // WINDOW
    kvw = k_pad.shape[1]

    def body(q_ref, k_ref, v_ref, b_ref, s_ref, dy_ref, dq_ref, dk_ref, dv_ref, db_ref, ds_ref):
        i = pl.program_id(0)

        @pl.when(i == 0)
        def _():
            dk_ref[...] = jnp.zeros_like(dk_ref)
            dv_ref[...] = jnp.zeros_like(dv_ref)
            db_ref[...] = jnp.zeros_like(db_ref)
            ds_ref[...] = jnp.zeros_like(ds_ref)

        valid = _attn_valid(i, t)
        start = pl.multiple_of(i * WINDOW, WINDOW)
        ks = k_ref[pl.ds(start, SPAN), :]
        vs = v_ref[pl.ds(start, SPAN), :]
        inv_sqrt = 1.0 / math.sqrt(HEAD)
        for kv in range(KV_HEADS):
            kh = ks[:, kv * HEAD:(kv + 1) * HEAD]
            vh = vs[:, kv * HEAD:(kv + 1) * HEAD]
            dk_acc = jnp.zeros((SPAN, HEAD), F32)
            dv_acc = jnp.zeros((SPAN, HEAD), F32)
            for h in range(kv * grp, (kv + 1) * grp):
                qh = q_ref[:, h * HEAD:(h + 1) * HEAD].astype(BF16)
                pr, ps = _attn_probs(qh, kh, b_ref[h], s_ref[0:1, h:h + 1], valid)
                doh = dy_ref[:, h * HEAD:(h + 1) * HEAD].astype(BF16)
                dp = _dot(doh, vh, 1, 1)
                delta = jnp.sum(pr * dp, axis=-1, keepdims=True)
                dsc = pr * (dp - delta)
                db_ref[h] += dsc
                ds_ref[h:h + 1, :] += jnp.broadcast_to(jnp.sum(-ps * delta, axis=0, keepdims=True), (1, 128))
                dsr = (dsc * inv_sqrt).astype(BF16)
                dq_ref[:, h * HEAD:(h + 1) * HEAD] = _dot(dsr, kh)
                dk_acc += _dot(dsr, qh, 0, 0)
                dv_acc += _dot(pr.astype(BF16), doh, 0, 0)
            dk_ref[pl.ds(start, SPAN), kv * HEAD:(kv + 1) * HEAD] += dk_acc
            dv_ref[pl.ds(start, SPAN), kv * HEAD:(kv + 1) * HEAD] += dv_acc

    full = lambda a: pl.BlockSpec(a.shape, lambda i: (0,) * a.ndim)
    whole = lambda shape: pl.BlockSpec(shape, lambda i: (0,) * len(shape))
    pad_shape = (t + 2 * WINDOW, kvw)
    body, ins, in_specs = _ordered(
        body, [p, k_pad, v_pad, bias, sink, dy],
        [pl.BlockSpec((WINDOW, aw), lambda i: (i, q_group_blk)), full(k_pad), full(v_pad), full(bias), full(sink),
         pl.BlockSpec((WINDOW, aw), lambda i: (i, dy_blk))], after)
    return _pallas(
        body, name=name, grid=(nb,), in_specs=in_specs,
        out_specs=[pl.BlockSpec((WINDOW, aw), lambda i: (i, 0)), whole(pad_shape), whole(pad_shape),
                   whole(bias.shape), whole((nh, 128))],
        out_shape=[jax.ShapeDtypeStruct((t, aw), F32), jax.ShapeDtypeStruct(pad_shape, F32),
                   jax.ShapeDtypeStruct(pad_shape, F32), jax.ShapeDtypeStruct(bias.shape, F32),
                   jax.ShapeDtypeStruct((nh, 128), F32)],
        compiler_params=_params(("arbitrary",), 3 * _nbytes(pad_shape, F32) + 2 * _nbytes(bias.shape, F32)),
    )(*ins)


def _pad_kv(name, p, kv_blk, kvw):
    t = p.shape[0]
    nb = t // WINDOW

    def body(x_ref, o_ref):
        i = pl.program_id(0)
        inside = jnp.logical_and(i >= 1, i <= nb)
        o_ref[...] = jnp.where(inside, x_ref[...], 0.0).astype(BF16)

    return _pallas(
        body, name=name, grid=(nb + 2,),
        in_specs=[pl.BlockSpec((WINDOW, kvw), lambda i: (jnp.clip(i - 1, 0, nb - 1), kv_blk))],
        out_specs=pl.BlockSpec((WINDOW, kvw), lambda i: (i, 0)),
        out_shape=jax.ShapeDtypeStruct((t + 2 * WINDOW, kvw), BF16),
        compiler_params=_params(("parallel",), 1 << 20),
    )(p)


def _mix_dproj(name, pieces, kv_pads, t, after=None):
    hw = pieces[0][0].shape[1]
    kvw = kv_pads[0].shape[1]
    widths = [hw] * len(pieces) + [kvw] * len(kv_pads)
    total = sum(widths)
    tm = WINDOW
    flat = [a for pc in pieces for a in pc]

    def body(*refs):
        o_ref = refs[-1]
        pos, off = 0, 0
        for pc in pieces:
            val = refs[pos][...]
            for extra in range(1, len(pc)):
                val = val + refs[pos + extra][...]
            o_ref[:, off:off + hw] = val.astype(BF16)
            pos += len(pc)
            off += hw
        for _ in kv_pads:
            o_ref[:, off:off + kvw] = refs[pos][...].astype(BF16)
            pos += 1
            off += kvw

    in_specs = [pl.BlockSpec((tm, hw), lambda i: (i, 0)) for _ in flat]
    in_specs += [pl.BlockSpec((tm, kvw), lambda i: (i + 1, 0)) for _ in kv_pads]
    body, ins, in_specs = _ordered(body, [*flat, *kv_pads], in_specs, after)
    return _pallas(
        body, name=name, grid=(t // tm,), in_specs=in_specs,
        out_specs=pl.BlockSpec((tm, total), lambda i: (i, 0)),
        out_shape=jax.ShapeDtypeStruct((t, total), BF16),
        compiler_params=_params(("parallel",), 3 * _nbytes((tm, total), F32)),
    )(*ins)


def _concat_cols(name, a, b):
    t, wa = a.shape
    wb = b.shape[1]
    tm = _tile(t, 512, 16)

    def body(a_ref, b_ref, o_ref):
        o_ref[:, :wa] = a_ref[...]
        o_ref[:, wa:] = b_ref[...]

    return _pallas(
        body, name=name, grid=(t // tm,),
        in_specs=[pl.BlockSpec((tm, wa), lambda i: (i, 0)), pl.BlockSpec((tm, wb), lambda i: (i, 0))],
        out_specs=pl.BlockSpec((tm, wa + wb), lambda i: (i, 0)),
        out_shape=jax.ShapeDtypeStruct((t, wa + wb), a.dtype),
        compiler_params=_params(("parallel",), 2 * _nbytes((tm, wa + wb), a.dtype)),
    )(a, b)


def _cast_into_full(name, w, geom, idx, after=None):
    r, c = w.shape
    tr = _tile(r, 256, 16)
    nr = r // tr
    if geom.col:
        place = lambda i, iref: (i, iref[0])
    else:
        place = lambda i, iref: (iref[0] * nr + i, 0)

    def body(i_ref, w_ref, *rest):
        rest[-1][...] = w_ref[...].astype(BF16)

    in_specs = [pl.BlockSpec((tr, c), lambda i, iref: (i, 0))]
    ins = [w]
    if after is not None:
        in_specs.append(pl.BlockSpec(memory_space=pl.ANY))
        ins.append(after)
    return _pallas(
        body, name=name,
        grid_spec=pltpu.PrefetchScalarGridSpec(
            num_scalar_prefetch=1, grid=(nr,), in_specs=in_specs, out_specs=pl.BlockSpec((tr, c), place)),
        out_shape=pltpu.HBM(geom.full_shape, BF16),
        compiler_params=_params(("parallel",), 2 * _nbytes((tr, c), F32)),
    )(idx, *ins)


def _adamw(name, w, g, m, v):
    r, c = w.shape
    tr = _tile(r, 128, 8)
    bc1 = 1.0 - ADAM_B1 ** ADAM_STEP
    bc2 = 1.0 - ADAM_B2 ** ADAM_STEP

    def body(w_ref, g_ref, m_ref, v_ref, d_ref, nm_ref, nv_ref):
        gv = g_ref[...]
        nm = ADAM_B1 * m_ref[...] + (1.0 - ADAM_B1) * gv
        nv = ADAM_B2 * v_ref[...] + (1.0 - ADAM_B2) * (gv * gv)
        nm_ref[...] = nm
        nv_ref[...] = nv
        d_ref[...] = -ADAM_LR * ((nm / bc1) / (jnp.sqrt(nv / bc2) + ADAM_EPS) + ADAM_WD * w_ref[...])

    blk = pl.BlockSpec((tr, c), lambda i: (i, 0))
    out = jax.ShapeDtypeStruct((r, c), F32)
    return _pallas(
        body, name=name, grid=(r // tr,), in_specs=[blk] * 4, out_specs=[blk] * 3, out_shape=[out] * 3,
        compiler_params=_params(("parallel",), 7 * _nbytes((tr, c), F32)),
    )(w, g, m, v)


def _mesh_pos():
    return lax.axis_index("x"), lax.axis_index("y"), lax.axis_index("c")


def _other_chips(x, y):
    return [(1 - x, y), (x, 1 - y), (1 - x, 1 - y)]


class _Big:
    def __init__(self, shard_shape, col_sharded):
        self.col = col_sharded
        r, c = shard_shape
        self.shard_shape = (r, c)
        self.full_shape = (r, N_CHIPS * c) if col_sharded else (N_CHIPS * r, c)
        self.half_shape = (r // 2, N_CHIPS * c) if col_sharded else (N_CHIPS * r, c // 2)
        self.shard_half_shape = (r // 2, c) if col_sharded else (r, c // 2)

    def region(self, ref, s, half=None):
        r, c = self.shard_shape
        if self.col:
            rows = slice(None) if half is None else pl.ds(half * (r // 2), r // 2)
            return ref.at[rows, pl.ds(s * c, c)]
        cols = slice(None) if half is None else pl.ds(half * (c // 2), c // 2)
        return ref.at[pl.ds(s * r, r), cols]

    def three_halves(self, ref, half):
        r, c = self.shard_shape
        if self.col:
            return ref.at[pl.ds(half * (r // 2), r // 2), pl.ds(0, 3 * c)]
        return ref.at[pl.ds(0, 3 * r), pl.ds(half * (c // 2), c // 2)]

    def half_of_full(self, ref, half):
        r, c = self.full_shape
        if self.col:
            return ref.at[pl.ds(half * (r // 2), r // 2), :]
        return ref.at[:, pl.ds(half * (c // 2), c // 2)]

    def half_of_shard(self, ref, half):
        r, c = self.shard_shape
        if self.col:
            return ref.at[pl.ds(half * (r // 2), r // 2), :]
        return ref.at[:, pl.ds(half * (c // 2), c // 2)]

    def shard_of_half(self, ref, s):
        r, c = self.shard_shape
        if self.col:
            return ref.at[:, pl.ds(s * c, c)]
        return ref.at[pl.ds(s * r, r), :]


HBM =pl.BlockSpec(memory_space=pltpu.HBM)
SEM = pl.BlockSpec(memory_space=pltpu.SEMAPHORE)
SPLIT_COPY = pltpu.CompilerParams(has_side_effects=pltpu.SideEffectType.DATAFLOW_SIDE_EFFECTING)


def _in_hbm(a):
    return pltpu.with_memory_space_constraint(a, pltpu.HBM)


def _gather_start(name, fulls, geoms):
    nw = len(fulls)

    def body(*refs):
        dst = refs[nw:2 * nw]
        sems = refs[2 * nw:-1]
        x, y, c = _mesh_pos()
        mine = 2 * x + y
        for w in range(nw):
            own_half = geoms[w].region(dst[w], mine, c)
            for chip in _other_chips(x, y):
                pltpu.make_async_remote_copy(src_ref=own_half, dst_ref=own_half, send_sem=sems[2 * w],
                                             recv_sem=sems[2 * w + 1], device_id=(*chip, c),
                                             device_id_type=MESH).start()
        refs[-1][...] = jnp.zeros_like(refs[-1])

    out = _pallas(
        body, name=name, in_specs=[HBM] * nw,
        out_specs=[HBM] * nw + [SEM] * (2 * nw) + [pl.BlockSpec(memory_space=pltpu.VMEM)],
        out_shape=[pltpu.HBM(g.full_shape, BF16) for g in geoms] + [pltpu.SemaphoreType.DMA(())] * (2 * nw)
        + [jax.ShapeDtypeStruct((8, 128), F32)],
        input_output_aliases={w: w for w in range(nw)}, compiler_params=SPLIT_COPY,
    )(*[_in_hbm(a) for a in fulls])
    return out[:nw], [(out[nw + 2 * w], out[nw + 2 * w + 1]) for w in range(nw)], out[-1]


def _wait_three(geom, ref, half, send_sem, recv_sem, peer, recv):
    three = geom.three_halves(ref, half)
    copy = pltpu.make_async_remote_copy(src_ref=three, dst_ref=three, send_sem=send_sem, recv_sem=recv_sem,
                                        device_id=peer, device_id_type=MESH)
    if recv:
        copy.wait_recv()
    else:
        copy.wait_send()


def _gather_forward(name, full, geom, sems, after):
    def body(w_in, send_sem, recv_sem, after_ref, w_ref, fwd_send, fwd_recv):
        x, y, c = _mesh_pos()
        sibling = (x, y, 1 - c)
        _wait_three(geom, w_ref, c, send_sem, recv_sem, sibling, recv=True)
        for chip in _other_chips(x, y):
            landed = geom.region(w_ref, 2 * chip[0] + chip[1], c)
            pltpu.make_async_remote_copy(src_ref=landed, dst_ref=landed, send_sem=fwd_send, recv_sem=fwd_recv,
                                         device_id=sibling, device_id_type=MESH).start()
        _wait_three(geom, w_ref, c, send_sem, recv_sem, sibling, recv=False)

    sem = pltpu.SemaphoreType.DMA(())
    out = _pallas(
        body, name=name, in_specs=[HBM, SEM, SEM, pl.BlockSpec(memory_space=pl.ANY)], out_specs=[HBM, SEM, SEM],
        out_shape=[pltpu.HBM(geom.full_shape, BF16), sem, sem],
        input_output_aliases={0: 0}, compiler_params=SPLIT_COPY,
    )(full, sems[0], sems[1], after)
    return out[0], (out[1], out[2])


def _gather_end(name, full, geom, sems, after):
    def body(w_in, fwd_send, fwd_recv, after_ref, w_ref):
        x, y, c = _mesh_pos()
        sibling = (x, y, 1 - c)
        _wait_three(geom, w_ref, 1 - c, fwd_send, fwd_recv, sibling, recv=True)
        _wait_three(geom, w_ref, c, fwd_send, fwd_recv, sibling, recv=False)

    return _pallas(
        body, name=name, in_specs=[HBM, SEM, SEM, pl.BlockSpec(memory_space=pl.ANY)], out_specs=HBM,
        out_shape=pltpu.HBM(geom.full_shape, BF16),
        input_output_aliases={0: 0}, compiler_params=SPLIT_COPY,
    )(full, sems[0], sems[1], after)


def _split_copy_call(name, arrays, fn, sems=(), after=None, new_sems=0):
    n, ns = len(arrays), len(sems)
    n_in = n + ns + (after is not None)

    def body(*refs):
        fn(refs[n_in:n_in + n], refs[n:n + ns], refs[n_in + n:-1])
        refs[-1][...] = jnp.zeros_like(refs[-1])

    ins = list(arrays) if ns else [_in_hbm(a) for a in arrays]
    ins += list(sems) + ([after] if after is not None else [])
    in_specs = [HBM] * n + [SEM] * ns + ([pl.BlockSpec(memory_space=pl.ANY)] if after is not None else [])
    out = _pallas(
        body, name=name, in_specs=in_specs,
        out_specs=[HBM] * n + [SEM] * new_sems + [pl.BlockSpec(memory_space=pltpu.VMEM)],
        out_shape=[pltpu.HBM(a.shape, a.dtype) for a in arrays] + [pltpu.SemaphoreType.DMA(())] * new_sems
        + [jax.ShapeDtypeStruct((8, 128), F32)],
        input_output_aliases={i: i for i in range(n)}, compiler_params=SPLIT_COPY,
    )(*ins)
    return list(out[:n]), tuple(out[n:-1]), out[-1]


def _remote(src, dst, sems, to):
    return pltpu.make_async_remote_copy(src_ref=src, dst_ref=dst, send_sem=sems[0], recv_sem=sems[1],
                                        device_id=to, device_id_type=MESH)


class _GradReduce:
    def __init__(self, name, geom, idx, c_idx):
        self.name, self.geom, self.idx, self.c_idx = name, geom, idx, c_idx

    def pair_start(self, dw):
        g = self.geom

        def start(refs, _, new):
            x, y, c = _mesh_pos()
            _remote(g.half_of_full(refs[0], 1 - c), refs[1], new, (x, y, 1 - c)).start()

        self.arrays, self.sems, token = _split_copy_call(
            f"pair_start_{self.name}", [dw, lax.empty(g.half_shape, BF16)], start, new_sems=2)
        return token

    def pair_finish(self, after):
        g = self.geom

        def wait(refs, sems, _):
            x, y, c = _mesh_pos()
            copy = _remote(g.half_of_full(refs[0], 1 - c), refs[1], sems, (x, y, 1 - c))
            copy.wait_send()
            copy.wait_recv()

        (dw, landed), _, _ = _split_copy_call(f"pair_wait_{self.name}", self.arrays, wait, self.sems, after)
        half = _pair_add(f"pair_add_{self.name}", dw, landed, g, self.c_idx)

        def start(refs, _, new):
            x, y, c = _mesh_pos()
            for k, chip in enumerate(_other_chips(x, y)):
                _remote(g.shard_of_half(refs[0], 2 * chip[0] + chip[1]), refs[1].at[k], new, (*chip, c)).start()

        self.arrays, self.sems, token = _split_copy_call(
            f"chip_start_{self.name}", [half, lax.empty((3,) + g.shard_half_shape, BF16)], start, new_sems=2)
        return token

    def chip_finish(self, after):
        g = self.geom

        def wait(refs, sems, _):
            x, y, c = _mesh_pos()
            three = _remote(refs[1], refs[1], sems, (x, y, 1 - c))
            three.wait_send()
            three.wait_recv()

        (half, landed), _, _ = _split_copy_call(f"chip_wait_{self.name}", self.arrays, wait, self.sems, after)
        quarter = _chip_add(f"chip_add_{self.name}", half, landed, g, self.idx)

        def start(refs, _, new):
            x, y, c = _mesh_pos()
            own = g.half_of_shard(refs[0], c)
            _remote(own, own, new, (x, y, 1 - c)).start()

        self.arrays, self.sems, token = _split_copy_call(f"share_start_{self.name}", [quarter], start, new_sems=2)
        return token

    def finish(self, after):
        g = self.geom

        def wait(refs, sems, _):
            x, y, c = _mesh_pos()
            own, theirs = g.half_of_shard(refs[0], c), g.half_of_shard(refs[0], 1 - c)
            _remote(own, own, sems, (x, y, 1 - c)).wait_send()
            _remote(theirs, theirs, sems, (x, y, 1 - c)).wait_recv()

        (quarter,), _, _ = _split_copy_call(f"share_wait_{self.name}", self.arrays, wait, self.sems, after)
        return quarter


def _pair_add(name, grad, recv, geom, c_idx):
    r, c = geom.half_shape
    tr, tc = _tile(r, 256, 16), _tile(c, 2048, 128)
    nr, ncol = r // tr, c // tc
    if geom.col:
        mine = lambda i, j, cref: (cref[0] * nr + i, j)
    else:
        mine = lambda i, j, cref: (i, cref[0] * ncol + j)

    def body(c_ref, g_ref, r_ref, o_ref):
        o_ref[...] = (g_ref[...].astype(F32) + r_ref[...].astype(F32)).astype(BF16)

    return _pallas(
        body, name=name,
        grid_spec=pltpu.PrefetchScalarGridSpec(
            num_scalar_prefetch=1, grid=(nr, ncol),
            in_specs=[pl.BlockSpec((tr, tc), mine), pl.BlockSpec((tr, tc), lambda i, j, cref: (i, j))],
            out_specs=pl.BlockSpec((tr, tc), lambda i, j, cref: (i, j))),
        out_shape=jax.ShapeDtypeStruct((r, c), BF16),
        compiler_params=_params(("parallel", "parallel"), 3 * _nbytes((tr, tc), F32)),
    )(c_idx, grad, recv)


def _chip_add(name, half, recv, geom, idx):
    r, c = geom.shard_half_shape
    tr, tc = _tile(r, 256, 16), _tile(c, 2048, 128)
    nr, ncol = r // tr, c // tc
    if geom.col:
        mine = lambda i, j, iref: (i, iref[0] * ncol + j)
        place = lambda i, j, iref: (iref[1] * nr + i, j)
    else:
        mine = lambda i, j, iref: (iref[0] * nr + i, j)
        place = lambda i, j, iref: (i, iref[1] * ncol + j)

    def body(i_ref, h_ref, r_ref, o_ref):
        acc = h_ref[...].astype(F32)
        for k in range(3):
            acc = acc + r_ref[k].astype(F32)
        o_ref[...] = acc

    return _pallas(
        body, name=name,
        grid_spec=pltpu.PrefetchScalarGridSpec(
            num_scalar_prefetch=1, grid=(nr, ncol),
            in_specs=[pl.BlockSpec((tr, tc), mine), pl.BlockSpec((3, tr, tc), lambda i, j, iref: (0, i, j))],
            out_specs=pl.BlockSpec((tr, tc), place)),
        out_shape=jax.ShapeDtypeStruct(geom.shard_shape, F32),
        compiler_params=_params(("parallel", "parallel"), 4 * _nbytes((tr, tc), F32)),
    )(idx, half, recv)


def _all_reduce_small(pack):
    r, d = pack.shape

    def body(p_ref, o_ref, slots, send_sems, recv_sems):
        x, y, c = _mesh_pos()
        me = 4 * x + 2 * y + c
        slots[me] = p_ref[...]
        copies = []
        for k in range(1, N_DEV):
            px, py, pc = x ^ ((k >> 2) & 1), y ^ ((k >> 1) & 1), c ^ (k & 1)
            copies.append(pltpu.make_async_remote_copy(
                src_ref=p_ref, dst_ref=slots.at[me], send_sem=send_sems.at[k - 1], recv_sem=recv_sems.at[k - 1],
                device_id=(px, py, pc), device_id_type=MESH))
        for cp in copies:
            cp.start()
        for k in range(1, N_DEV):
            peer = 4 * (x ^ ((k >> 2) & 1)) + 2 * (y ^ ((k >> 1) & 1)) + (c ^ (k & 1))
            pltpu.make_async_remote_copy(
                src_ref=p_ref, dst_ref=slots.at[peer], send_sem=send_sems.at[k - 1], recv_sem=recv_sems.at[k - 1],
                device_id=(x, y, c), device_id_type=MESH).wait_recv()
        for cp in copies:
            cp.wait_send()
        acc = slots[0]
        for k in range(1, N_DEV):
            acc = acc + slots[k]
        o_ref[...] = acc

    vm = pl.BlockSpec(memory_space=pltpu.VMEM)
    return _pallas(
        body, name="all_reduce_small", in_specs=[vm], out_specs=vm,
        out_shape=jax.ShapeDtypeStruct((r, d), F32),
        scratch_shapes=[pltpu.VMEM((N_DEV, r, d), F32), pltpu.SemaphoreType.DMA((N_DEV - 1,)),
                        pltpu.SemaphoreType.DMA((N_DEV - 1,))],
    )(pack)


def _pack_rows(rows, d):
    out = []
    for a in rows:
        flat = a.reshape(-1)
        n = -(-flat.shape[0] // d) * d
        out.append(jnp.pad(flat, (0, n - flat.shape[0])).reshape(-1, d))
    packed = jnp.concatenate(out, axis=0)
    return jnp.pad(packed, ((0, 16 - packed.shape[0]), (0, 0)))


def _unpack_rows(packed, shapes, d):
    out, row = [], 0
    for shp in shapes:
        n = int(np.prod(shp))
        nrows = -(-n // d)
        out.append(packed[row:row + nrows].reshape(-1)[:n].reshape(shp))
        row += nrows
    return out


def kernel(x, pre_norm_ffn1, post_norm_ffn1, w_ffn1_gate_up, w_ffn1_down, pre_norm_mix, post_norm_mix, w_mix_in, hgrn_lower_bounds_fwd, hgrn_lower_bounds_bwd, hgrn_out_norm, attn_sink, w_mix_out, pre_norm_ffn2, post_norm_ffn2, w_ffn2_gate_up, w_ffn2_down, rel_bias_table, loss_target, m_pre_norm_ffn1, m_post_norm_ffn1, m_w_ffn1_gate_up, m_w_ffn1_down, m_pre_norm_mix, m_post_norm_mix, m_w_mix_in, m_hgrn_lower_bounds_fwd, m_hgrn_lower_bounds_bwd, m_hgrn_out_norm, m_attn_sink, m_w_mix_out, m_pre_norm_ffn2, m_post_norm_ffn2, m_w_ffn2_gate_up, m_w_ffn2_down, m_rel_bias_table, v_pre_norm_ffn1, v_post_norm_ffn1, v_w_ffn1_gate_up, v_w_ffn1_down, v_pre_norm_mix, v_post_norm_mix, v_w_mix_in, v_hgrn_lower_bounds_fwd, v_hgrn_lower_bounds_bwd, v_hgrn_out_norm, v_attn_sink, v_w_mix_out, v_pre_norm_ffn2, v_post_norm_ffn2, v_w_ffn2_gate_up, v_w_ffn2_down, v_rel_bias_table):
    t, d = x.shape[1], x.shape[2]
    hw = hgrn_out_norm.shape[1]
    aw = d - hw
    nah = aw // HEAD
    kvw = KV_HEADS * HEAD
    x0 = x[0]
    target = loss_target[0]

    big_names = ["w_ffn1_gate_up", "w_ffn1_down", "w_mix_in", "w_mix_out", "w_ffn2_gate_up", "w_ffn2_down"]
    big_w = [w_ffn1_gate_up[0], w_ffn1_down[0], w_mix_in[0], w_mix_out[0], w_ffn2_gate_up[0], w_ffn2_down[0]]
    big_m = [m_w_ffn1_gate_up[0], m_w_ffn1_down[0], m_w_mix_in[0], m_w_mix_out[0], m_w_ffn2_gate_up[0],
             m_w_ffn2_down[0]]
    big_v = [v_w_ffn1_gate_up[0], v_w_ffn1_down[0], v_w_mix_in[0], v_w_mix_out[0], v_w_ffn2_gate_up[0],
             v_w_ffn2_down[0]]
    col_sharded = [True, False, True, False, True, False]
    geoms = [_Big(w.shape, cs) for w, cs in zip(big_w, col_sharded)]

    cx, cy, cc = _mesh_pos()
    idx = jnp.stack([2 * cx + cy, cc]).astype(jnp.int32)
    c_idx = jnp.reshape(cc, (1,)).astype(jnp.int32)
    first = _cast_into_full(f"cast_{big_names[0]}", big_w[0], geoms[0], idx)
    started, gather_sems, tok = _gather_start("gather_start_first", [first], geoms[:1])
    rest = [_cast_into_full(f"cast_{n}", w, gm, idx, after=tok)
            for n, w, gm in zip(big_names[1:], big_w[1:], geoms[1:])]
    started_rest, sems_rest, _ = _gather_start("gather_start_rest", rest, geoms[1:])
    started, gather_sems = list(started) + list(started_rest), gather_sems + sems_rest

    def forward_weight(w, after):
        return _gather_forward(f"gather_forward_{big_names[w]}", started[w], geoms[w], gather_sems[w], after)

    def whole_weight(w, forwarded, after):
        return _gather_end(f"gather_end_{big_names[w]}", forwarded[0], geoms[w], forwarded[1], after)

    h1 = _norm_fwd("ffn1_pre_norm", x0, pre_norm_ffn1)
    w_gu1 = whole_weight(0, forward_weight(0, h1), h1)
    gu1 = _mm("ffn1_gate_up", h1, w_gu1, "nn", BF16)
    fw = forward_weight(1, gu1)
    act1 = _swiglu_fwd("ffn1_act", gu1)
    w_d1 = whole_weight(1, fw, act1)
    ff1 = _mm("ffn1_down", act1, w_d1, "nn", F32)
    fw = forward_weight(2, ff1)
    x1, hm = _resid_norm_fwd("ffn1_residual", x0, ff1, post_norm_ffn1, pre_norm_mix, 0.5)
    w_in = whole_weight(2, fw, hm)
    p = _mm("mix_in", hm, w_in, "nn", F32)
    fw = forward_weight(3, p)
    o_f, o_b, st_f, st_b = _hgrn_scan_fwd("hgrn_scan", p, hgrn_lower_bounds_fwd, hgrn_lower_bounds_bwd)
    y_h = _hgrn_out_fwd("hgrn_out", o_f, o_b, p, hgrn_out_norm, 4)
    kv_blk0 = (5 * hw + aw) // kvw
    k_pad = _pad_kv("attn_pad_k", p, kv_blk0, kvw)
    v_pad = _pad_kv("attn_pad_v", p, kv_blk0 + 1, kvw)
    bucket_ids = _t5_bucket_ids()
    bias = _bias_gather("attn_bias", rel_bias_table.T, bucket_ids).reshape(nah, WINDOW, SPAN)
    y_a = _attn_fwd("attn_fwd", p, k_pad, v_pad, bias, attn_sink, 5 * hw // aw)
    y_mix = _concat_cols("mix_concat", y_h, y_a)
    w_out = whole_weight(3, fw, y_mix)
    mixed = _mm("mix_out", y_mix, w_out, "nn", F32)
    fw = forward_weight(4, mixed)
    x2, h2 = _resid_norm_fwd("mix_residual", x1, mixed, post_norm_mix, pre_norm_ffn2, 1.0)
    w_gu2 = whole_weight(4, fw, h2)
    gu2 = _mm("ffn2_gate_up", h2, w_gu2, "nn", BF16)
    fw = forward_weight(5, gu2)
    act2 = _swiglu_fwd("ffn2_act", gu2)
    w_d2 = whole_weight(5, fw, act2)
    ff2 = _mm("ffn2_down", act2, w_d2, "nn", F32)
    loss_blk, dy, dff2, dg_post2 = _final_fwd_bwd("ffn2_residual_loss", x2, ff2, post_norm_ffn2, target, 0.5)

    reduce = [_GradReduce(n, gm, idx, c_idx) for n, gm in zip(big_names, geoms)]
    big_grads, big_delta, big_new_m, big_new_v = [None] * 6, [None] * 6, [None] * 6, [None] * 6

    def update(w, after):
        g = reduce[w].finish(after)
        dl, nm, nv = _adamw(f"adamw_{big_names[w]}", big_w[w], g, big_m[w], big_v[w])
        big_grads[w], big_delta[w], big_new_m[w], big_new_v[w] = g[None], dl[None], nm[None], nv[None]
        return dl

    da2 = _mm("ffn2_dact", dff2, w_d2, "nt", F32)
    dw_d2 = _mm("ffn2_dw_down", act2, dff2, "tn", BF16)
    tok = reduce[5].pair_start(dw_d2)
    dgu2 = _swiglu_bwd("ffn2_dact_bwd", da2, gu2, after=tok)
    tok = reduce[5].pair_finish(dgu2)
    dw_gu2 = _ffn_dw_gate_up("ffn2_dw_gate_up", h2, dgu2, after=tok)
    tok = reduce[4].pair_start(dw_gu2)
    dh2 = _ffn_dh("ffn2_dh", dgu2, w_gu2, after=tok)
    tok = reduce[4].pair_finish(dh2)
    dx2, dg_pre2, dmixed, dg_postm = _norms_bwd("mix_residual_bwd", dy, dh2, x2, pre_norm_ffn2,
                                                post=(mixed, post_norm_mix, 1.0), after=tok)
    dw_out = _mm("mix_out_dw", y_mix, dmixed, "tn", BF16)
    tok = reduce[3].pair_start(dw_out)
    dy_mix = _mm("mix_out_dx", dmixed, w_out, "nt", F32, after=tok)
    tok = reduce[3].pair_finish(dy_mix)
    dq_a, dk_pad, dv_pad, dbias, dsink = _attn_bwd("attn_bwd", p, k_pad, v_pad, bias, attn_sink, dy_mix,
                                                   5 * hw // aw, hw // aw, after=tok)
    tok = reduce[5].chip_finish(dq_a)
    drel_t = _bias_scatter("attn_dbias", dbias.reshape(nah, WINDOW * SPAN), bucket_ids)
    do, dg_h, dgain = _hgrn_out_bwd("hgrn_out_bwd", dy_mix, o_f, o_b, p, hgrn_out_norm, 4, after=tok)
    dq_f, dv_f, dz_f, dlb_f, dq_b, dv_b, dz_b, dlb_b = _hgrn_scan_bwd(
        "hgrn_scan_bwd", p, hgrn_lower_bounds_fwd, hgrn_lower_bounds_bwd, do, st_f, st_b)
    tok = reduce[4].chip_finish(dq_f)
    tok = reduce[3].chip_finish(tok)
    dp = _mix_dproj("mix_dproj", [(dq_f, dq_b), (dv_f, dv_b), (dz_f,), (dz_b,), (dg_h,), (dq_a,)],
                    [dk_pad, dv_pad], t, after=tok)
    dw_in = _mm("mix_in_dw", hm, dp, "tn", BF16)
    tok = reduce[2].pair_start(dw_in)
    dhm = _mm("mix_in_dx", dp, w_in, "nt", F32, after=tok)
    tok = reduce[2].pair_finish(dhm)
    dx1, dg_prem, dff1, dg_post1 = _norms_bwd("ffn1_residual_bwd", dx2, dhm, x1, pre_norm_mix,
                                              post=(ff1, post_norm_ffn1, 0.5), after=tok)
    da1 = _mm("ffn1_dact", dff1, w_d1, "nt", F32)
    dw_d1 = _mm("ffn1_dw_down", act1, dff1, "tn", BF16)
    tok = reduce[1].pair_start(dw_d1)
    dgu1 = _swiglu_bwd("ffn1_dact_bwd", da1, gu1, after=tok)
    tok = reduce[1].pair_finish(dgu1)
    tok = reduce[2].chip_finish(tok)
    dw_gu1 = _ffn_dw_gate_up("ffn1_dw_gate_up", h1, dgu1, after=tok)
    tok = reduce[0].pair_start(dw_gu1)
    done = update(2, tok)
    tok = reduce[0].pair_finish(done)
    dh1 = _ffn_dh("ffn1_dh", dgu1, w_gu1, after=tok)
    grad_x, dg_pre1 = _norms_bwd("ffn1_pre_norm_bwd", dx1, dh1, x0, pre_norm_ffn1)

    small_w = [pre_norm_ffn1, post_norm_ffn1, pre_norm_mix, post_norm_mix, hgrn_lower_bounds_fwd,
               hgrn_lower_bounds_bwd, hgrn_out_norm, attn_sink, pre_norm_ffn2, post_norm_ffn2, rel_bias_table]
    small_m = [m_pre_norm_ffn1, m_post_norm_ffn1, m_pre_norm_mix, m_post_norm_mix, m_hgrn_lower_bounds_fwd,
               m_hgrn_lower_bounds_bwd, m_hgrn_out_norm, m_attn_sink, m_pre_norm_ffn2, m_post_norm_ffn2,
               m_rel_bias_table]
    small_v = [v_pre_norm_ffn1, v_post_norm_ffn1, v_pre_norm_mix, v_post_norm_mix, v_hgrn_lower_bounds_fwd,
               v_hgrn_lower_bounds_bwd, v_hgrn_out_norm, v_attn_sink, v_pre_norm_ffn2, v_post_norm_ffn2,
               v_rel_bias_table]
    small_g = [dg_pre1, dg_post1, dg_prem, dg_postm, dlb_f, dlb_b, dgain, dsink[:, 0].reshape(1, nah), dg_pre2,
               dg_post2, drel_t.T]
    shapes = [a.shape for a in small_w]
    summed = _all_reduce_small(_pack_rows(small_g + [loss_blk[0:1, 0:1]], d))
    g_pack = summed
    loss =_unpack_rows(summed, shapes + [(1, 1)], d)[-1][0, 0]
    sd, sm, sv = _adamw("adamw_small", _pack_rows(small_w, d), g_pack, _pack_rows(small_m, d), _pack_rows(small_v, d))
    small_grads = _unpack_rows(g_pack, shapes, d)
    small_delta, small_new_m, small_new_v = (_unpack_rows(a, shapes, d) for a in (sd, sm, sv))

    done = update(5, sd)
    done = update(4, done)
    done = update(3, done)
    tok = reduce[1].chip_finish(done)
    done = update(1, tok)
    tok = reduce[0].chip_finish(done)
    update(0, tok)

    def ordered(small, big):
        s = dict(zip(["pre1", "post1", "prem", "postm", "lbf", "lbb", "gain", "sink", "pre2", "post2", "rel"], small))
        b = dict(zip(["gu1", "d1", "win", "wout", "gu2", "d2"], big))
        return [s["pre1"], s["post1"], b["gu1"], b["d1"], s["prem"], s["postm"], b["win"], s["lbf"], s["lbb"],
                s["gain"], s["sink"], b["wout"], s["pre2"], s["post2"], b["gu2"], b["d2"], s["rel"]]

    return (loss, grad_x[None], *ordered(small_grads, big_grads), *ordered(small_delta, big_delta),
            *ordered(small_new_m, big_new_m), *ordered(small_new_v, big_new_v))
```

```python
import functools
import math

import jax
import jax.numpy as jnp
import numpy as np
from jax import lax
from jax.experimental import pallas as pl
from jax.experimental.pallas import tpu as pltpu

F32 = jnp.float32
BF16 = jnp.bfloat16

HEAD = 128
CHUNK = 64
WINDOW = 128
SPAN = 3 * WINDOW
KV_HEADS = 2
REL_BUCKETS = 32
REL_MAX_DIST = 128
EPS = 1e-6
NEG_INF = -1e30

ADAM_LR = 0.001
ADAM_B1 = 0.9
ADAM_B2 = 0.999
ADAM_EPS = 1e-08
ADAM_WD = 0.01
ADAM_STEP = 10

N_CHIPS = 4
N_DEV = 8
V7X_VMEM_BYTES = 64 * 1024 * 1024
MESH = pl.DeviceIdType.MESH
ANY = pl.BlockSpec(memory_space=pl.ANY)


def _tile(n, pref, mult):
    t = (min(pref, n) // mult) * mult
    while t >= mult:
        if n % t == 0:
            return t
        t -= mult
    return n


def _params(semantics, block_bytes):
    limit = min(V7X_VMEM_BYTES - (4 << 20), 2 * int(block_bytes) + (8 << 20))
    return pltpu.CompilerParams(dimension_semantics=semantics, vmem_limit_bytes=limit)


def _nbytes(shape, dtype):
    return int(np.prod(shape)) * jnp.dtype(dtype).itemsize


PIN_TO_HBM_BYTES = 4 << 20


def _pallas(body, **kw):
    def pin_shape(s):
        if isinstance(s, jax.ShapeDtypeStruct) and _nbytes(s.shape, s.dtype) >= PIN_TO_HBM_BYTES:
            return pltpu.HBM(s.shape, s.dtype)
        return s

    def pin(a):
        if getattr(a, "dtype", None) in (F32, BF16) and _nbytes(a.shape, a.dtype) >= PIN_TO_HBM_BYTES:
            return pltpu.with_memory_space_constraint(a, pltpu.HBM)
        return a

    out_shape = kw["out_shape"]
    kw["out_shape"] = [pin_shape(s) for s in out_shape] if isinstance(out_shape, (list, tuple)) else pin_shape(out_shape)
    call = pl.pallas_call(body, **kw)
    return lambda *args: call(*[pin(a) for a in args])


def _dot(a, b, ca=1, cb=0):
    return lax.dot_general(a, b, (((ca,), (cb,)), ((), ())), preferred_element_type=F32)


def _split3(x):
    hi = x.astype(BF16)
    r1 = x - hi.astype(F32)
    mid = r1.astype(BF16)
    lo = (r1 - mid.astype(F32)).astype(BF16)
    return hi, mid, lo


def _dot_exact(a, b, ca=1, cb=0, split="b"):
    if split == "b":
        return sum(_dot(a, p, ca, cb) for p in _split3(b))
    return sum(_dot(p, b, ca, cb) for p in _split3(a))


def _rms(x):
    return lax.rsqrt(jnp.mean(x * x, axis=-1, keepdims=True) + EPS)


def _norm_bwd(u, x, gain):
    r = _rms(x)
    xhat = x * r
    dgain = jnp.sum(u * xhat, axis=0, keepdims=True)
    v = u * gain
    dx = r * (v - xhat * jnp.mean(v * xhat, axis=-1, keepdims=True))
    return dx, dgain


def _sigmoid(x):
    return 1.0 / (1.0 + jnp.exp(-x))


def _accumulate(ref, val, first):
    @pl.when(first)
    def _():
        ref[...] = val

    @pl.when(jnp.logical_not(first))
    def _():
        ref[...] += val


def _ordered(body, ins, in_specs, after):
    if after is None:
        return body, list(ins), list(in_specs)
    n_in = len(ins)

    def wrapped(*refs):
        body(*refs[:n_in], *refs[n_in + 1:])

    return wrapped, list(ins) + [after], list(in_specs) + [pl.BlockSpec(memory_space=pl.ANY)]


def _matmul(name, a, b, *, form, out_dtype, tm, tn, tk, a_map=None, b_map=None,
            out_shape=None, out_block=None, out_map=None, sizes=None, after=None):
    if sizes is None:
        if form == "nn":
            (m, k), n = a.shape, b.shape[1]
        elif form == "nt":
            (m, k), n = a.shape, b.shape[0]
        else:
            (k, m), n = a.shape, b.shape[1]
    else:
        m, n, k = sizes
    gi, gj, gk = m // tm, n // tn, k // tk
    a_blk = (tm, tk) if form != "tn" else (tk, tm)
    b_blk = (tk, tn) if form != "nt" else (tn, tk)
    if a_map is None:
        a_map = (lambda i, j, kk: (i, kk)) if form != "tn" else (lambda i, j, kk: (kk, i))
    else:
        a_blk = (None,) + a_blk
    if b_map is None:
        b_map = (lambda i, j, kk: (kk, j)) if form != "nt" else (lambda i, j, kk: (j, kk))
    else:
        b_blk = (None,) + b_blk
    if out_shape is None:
        out_shape, out_block, out_map = (m, n), (tm, tn), (lambda i, j, kk: (i, j))
    ca, cb = {"nn": (1, 0), "nt": (1, 1), "tn": (0, 0)}[form]

    def body(a_ref, b_ref, o_ref, *acc):
        part = _dot(a_ref[...], b_ref[...], ca, cb)
        if gk == 1:
            o_ref[...] = part.astype(o_ref.dtype)
        else:
            kk = pl.program_id(2)
            _accumulate(acc[0], part, kk == 0)

            @pl.when(kk == gk - 1)
            def _():
                o_ref[...] = acc[0][...].astype(o_ref.dtype)

    scratch = [] if gk == 1 else [pltpu.VMEM((tm, tn), F32)]
    vmem = (_nbytes((tm, tk), a.dtype) + _nbytes((tk, tn), b.dtype) + _nbytes((tm, tn), out_dtype)
            + 2 * _nbytes((tm, tn), F32))
    body, ins, in_specs = _ordered(body, [a, b], [pl.BlockSpec(a_blk, a_map), pl.BlockSpec(b_blk, b_map)], after)
    return _pallas(
        body, name=name, grid=(gi, gj, gk), in_specs=in_specs,
        out_specs=pl.BlockSpec(out_block, out_map),
        out_shape=jax.ShapeDtypeStruct(out_shape, out_dtype),
        scratch_shapes=scratch,
        compiler_params=_params(("parallel", "parallel", "arbitrary"), vmem),
    )(*ins)


V7X_HBM_BYTES_PER_US = 3.0e6
V7X_MXU_FLOPS_PER_US = 0.9e9
V7X_VMEM_RMW_BYTES_PER_US = 10e6
GRID_STEP_US = 0.35
MATMUL_VMEM_BUDGET = 40 << 20
MATMUL_MAX_TILE_FLOPS = 1 << 33


def _divisors(n, mult, lo):
    return [t for t in range(mult, n + 1, mult) if n % t == 0 and t >= min(lo, n)]


def _mm_tiles(m, n, k, out_dtype=F32, n_unit=None, k_unit=None):
    out_bytes = jnp.dtype(out_dtype).itemsize
    best = None
    for tm in _divisors(m, 128, 256):
        for tn in _divisors(n_unit or n, 128, 256):
            for tk in _divisors(k_unit or k, 128, 512):
                gi, gj, gk = m // tm, n // tn, k // tk
                vmem = 4 * tm * tk + 4 * tk * tn + 2 * tm * tn * out_bytes + 4 * tm * tn * (2 if gk > 1 else 1)
                if vmem > MATMUL_VMEM_BUDGET or 2 * tm * tn * tk > MATMUL_MAX_TILE_FLOPS:
                    continue
                a_bytes = 2 * m * k * (gj if gk > 1 else 1)
                b_bytes = 2 * k * n * (1 if gj == 1 and gk == 1 else gi)
                hbm_us = (a_bytes + b_bytes + m * n * out_bytes) / V7X_HBM_BYTES_PER_US
                acc_us = (8 * m * n * gk / V7X_VMEM_RMW_BYTES_PER_US) if gk > 1 else 0.0
                cost = max(2 * m * n * k / V7X_MXU_FLOPS_PER_US, 1.3 * hbm_us) + GRID_STEP_US * gi * gj * gk + acc_us
                key = (round(cost, 1), vmem)
                if best is None or key < best[0]:
                    best = (key, (tm, tn, tk))
    return best[1]


def _mm(name, a, b, form, out_dtype, after=None):
    if form == "nn":
        m, k, n = a.shape[0], a.shape[1], b.shape[1]
    elif form == "nt":
        m, k, n = a.shape[0], a.shape[1], b.shape[0]
    else:
        m, k, n = a.shape[1], a.shape[0], b.shape[1]
    tm, tn, tk = _mm_tiles(m, n, k, out_dtype)
    return _matmul(name, a, b, form=form, out_dtype=out_dtype, tm=tm, tn=tn, tk=tk, after=after)


def _row_tile(t):
    return _tile(t, 256, 8)


def _norm_fwd(name, x, gain):
    t, d = x.shape
    tm = _row_tile(t)

    def body(x_ref, g_ref, h_ref):
        xv = x_ref[...]
        h_ref[...] = (xv * _rms(xv) * g_ref[...]).astype(BF16)

    row = pl.BlockSpec((tm, d), lambda i: (i, 0))
    vec = pl.BlockSpec((1, d), lambda i: (0, 0))
    return _pallas(
        body, name=name, grid=(t // tm,), in_specs=[row, vec], out_specs=row,
        out_shape=jax.ShapeDtypeStruct((t, d), BF16),
        compiler_params=_params(("parallel",), 2 * _nbytes((tm, d), F32)),
    )(x, gain)


def _resid_norm_fwd(name, xres, ff, gpost, gpre, scale):
    t, d = xres.shape
    tm = _row_tile(t)

    def body(x_ref, f_ref, gp_ref, gn_ref, xn_ref, h_ref):
        f = f_ref[...]
        xn = x_ref[...] + scale * (f * _rms(f) * gp_ref[...])
        xn_ref[...] = xn
        h_ref[...] = (xn * _rms(xn) * gn_ref[...]).astype(BF16)

    row = pl.BlockSpec((tm, d), lambda i: (i, 0))
    vec = pl.BlockSpec((1, d), lambda i: (0, 0))
    return _pallas(
        body, name=name, grid=(t // tm,), in_specs=[row, row, vec, vec], out_specs=[row, row],
        out_shape=[jax.ShapeDtypeStruct((t, d), F32), jax.ShapeDtypeStruct((t, d), BF16)],
        compiler_params=_params(("parallel",), 4 * _nbytes((tm, d), F32)),
    )(xres, ff, gpost, gpre)


def _final_fwd_bwd(name, xres, ff, gpost, target, scale):
    t, d = xres.shape
    tm = _row_tile(t)

    def body(x_ref, f_ref, gp_ref, t_ref, loss_ref, dy_ref, dff_ref, dg_ref):
        i = pl.program_id(0)
        f = f_ref[...]
        gp = gp_ref[...]
        y = x_ref[...] + scale * (f * _rms(f) * gp)
        err = y - t_ref[...]
        part = 0.5 * jnp.sum(jnp.mean(err * err, axis=-1, keepdims=True), axis=0, keepdims=True)
        _accumulate(loss_ref, jnp.broadcast_to(part, loss_ref.shape), i == 0)
        dy = err / d
        dy_ref[...] = dy
        dff, dg = _norm_bwd(scale * dy, f, gp)
        dff_ref[...] = dff.astype(BF16)
        _accumulate(dg_ref, dg, i == 0)

    row = pl.BlockSpec((tm, d), lambda i: (i, 0))
    vec = pl.BlockSpec((1, d), lambda i: (0, 0))
    return _pallas(
        body, name=name, grid=(t // tm,), in_specs=[row, row, vec, row],
        out_specs=[pl.BlockSpec((8, 128), lambda i: (0, 0)), row, row, vec],
        out_shape=[jax.ShapeDtypeStruct((8, 128), F32), jax.ShapeDtypeStruct((t, d), F32),
                   jax.ShapeDtypeStruct((t, d), BF16), jax.ShapeDtypeStruct((1, d), F32)],
        compiler_params=_params(("arbitrary",), 5 * _nbytes((tm, d), F32)),
    )(xres, ff, gpost, target)


def _norms_bwd(name, dres, dh, xin, gpre, post=None, after=None):
    t, d = dres.shape
    tm = _row_tile(t)
    with_post = post is not None

    def body(*refs):
        if with_post:
            dr_ref, dh_ref, x_ref, g_ref, f_ref, gp_ref, dx_ref, dg_ref, dff_ref, dgp_ref = refs
        else:
            dr_ref, dh_ref, x_ref, g_ref, dx_ref, dg_ref = refs
        i = pl.program_id(0)
        dx, dg = _norm_bwd(dh_ref[...], x_ref[...], g_ref[...])
        dx = dr_ref[...] + dx
        dx_ref[...] = dx
        _accumulate(dg_ref, dg, i == 0)
        if with_post:
            dff, dgp = _norm_bwd(post[2] * dx, f_ref[...], gp_ref[...])
            dff_ref[...] = dff.astype(BF16)
            _accumulate(dgp_ref, dgp, i == 0)

    row = pl.BlockSpec((tm, d), lambda i: (i, 0))
    vec = pl.BlockSpec((1, d), lambda i: (0, 0))
    ins, in_specs = [dres, dh, xin, gpre], [row, row, row, vec]
    out_specs = [row, vec]
    out_shape = [jax.ShapeDtypeStruct((t, d), F32), jax.ShapeDtypeStruct((1, d), F32)]
    if with_post:
        ins += [post[0], post[1]]
        in_specs += [row, vec]
        out_specs += [row, vec]
        out_shape += [jax.ShapeDtypeStruct((t, d), BF16), jax.ShapeDtypeStruct((1, d), F32)]
    body, ins, in_specs = _ordered(body, ins, in_specs, after)
    return _pallas(
        body, name=name, grid=(t // tm,), in_specs=in_specs, out_specs=out_specs, out_shape=out_shape,
        compiler_params=_params(("arbitrary",), 6 * _nbytes((tm, d), F32)),
    )(*ins)


SWIGLU_TILE = (1024, 512)


def _ffn_gate_up_act(name, h, w_gu):
    t, d = h.shape
    f = w_gu.shape[1] // 2
    tm, tn = _tile(t, SWIGLU_TILE[0], 128), _tile(f, SWIGLU_TILE[1], 128)
    nf = f // tn

    def body(h_ref, wg_ref, wu_ref, g_ref, u_ref, a_ref):
        hv = h_ref[...]
        g = _dot(hv, wg_ref[...])
        u = _dot(hv, wu_ref[...])
        g_ref[...] = g.astype(BF16)
        u_ref[...] = u.astype(BF16)
        a_ref[...] = (g * _sigmoid(g) * u).astype(BF16)

    out = jax.ShapeDtypeStruct((t, f), BF16)
    blk = pl.BlockSpec((tm, tn), lambda i, j: (i, j))
    return _pallas(
        body, name=name, grid=(t // tm, nf),
        in_specs=[pl.BlockSpec((tm, d), lambda i, j: (i, 0)), pl.BlockSpec((d, tn), lambda i, j: (0, j)),
                  pl.BlockSpec((d, tn), lambda i, j: (0, j + nf))],
        out_specs=[blk, blk, blk], out_shape=[out, out, out],
        compiler_params=_params(("parallel", "parallel"),
                                _nbytes((tm, d), BF16) + 2 * _nbytes((d, tn), BF16) + 5 * _nbytes((tm, tn), F32)),
    )(h, w_gu, w_gu)


def _ffn_dact(name, dff, w_down, gate, up, after=None):
    t, d = dff.shape
    f = w_down.shape[0]
    tm, tn = _tile(t, SWIGLU_TILE[0], 128), _tile(f, SWIGLU_TILE[1], 128)

    def body(d_ref, w_ref, g_ref, u_ref, o_ref):
        da = _dot(d_ref[...], w_ref[...], 1, 1)
        g = g_ref[...].astype(F32)
        u = u_ref[...].astype(F32)
        sig = _sigmoid(g)
        o_ref[0] = (da * u * sig * (1.0 + g * (1.0 - sig))).astype(BF16)
        o_ref[1] = (da * g * sig).astype(BF16)

    blk = pl.BlockSpec((tm, tn), lambda i, j: (i, j))
    body, ins, in_specs = _ordered(
        body, [dff, w_down, gate, up],
        [pl.BlockSpec((tm, d), lambda i, j: (i, 0)), pl.BlockSpec((tn, d), lambda i, j: (j, 0)), blk, blk], after)
    return _pallas(
        body, name=name, grid=(t // tm, f // tn), in_specs=in_specs,
        out_specs=pl.BlockSpec((2, tm, tn), lambda i, j: (0, i, j)),
        out_shape=jax.ShapeDtypeStruct((2, t, f), BF16),
        compiler_params=_params(("parallel", "parallel"),
                                _nbytes((tm, d), BF16) + _nbytes((tn, d), BF16) + 5 * _nbytes((tm, tn), F32)),
    )(*ins)


def _ffn_dh(name, dgu, w_gu, after=None):
    _, t, f = dgu.shape
    d = w_gu.shape[0]
    tm, tn, tk = _mm_tiles(t, d, 2 * f, F32, k_unit=f)
    nkf = f // tk
    return _matmul(name, dgu, w_gu, form="nt", out_dtype=F32, tm=tm, tn=tn, tk=tk, sizes=(t, d, 2 * f),
                   a_map=lambda i, j, kk: (kk // nkf, i, kk % nkf), after=after)


def _ffn_dw_gate_up(name, h, dgu, after=None):
    _, t, f = dgu.shape
    d = h.shape[1]
    tm, tn, tk = _mm_tiles(d, 2 * f, t, BF16, n_unit=f)
    nf = f // tn
    return _matmul(name, h, dgu, form="tn", out_dtype=BF16, tm=tm, tn=tn, tk=tk, sizes=(d, 2 * f, t),
                   b_map=lambda i, j, kk: (j // nf, kk, j % nf), after=after)


def _lower_bound(lbp):
    m = jnp.max(lbp, axis=0, keepdims=True)
    e = jnp.exp(lbp - m)
    return e[0:1] / jnp.sum(e, axis=0, keepdims=True)


def _chunk_mask(reverse):
    row = lax.broadcasted_iota(jnp.int32, (CHUNK, CHUNK), 0)
    col = lax.broadcasted_iota(jnp.int32, (CHUNK, CHUNK), 1)
    return (col >= row) if reverse else (col <= row)


def _hgrn_gates(z, lb, mask_bf):
    sig = _sigmoid(z)
    f = lb + (1.0 - lb) * sig
    logf = jnp.log(f)
    k = 1.0 - f
    cum = _dot_exact(mask_bf, logf)
    last = jnp.sum(logf, axis=0, keepdims=True)
    return sig, f, k, cum, last


def _hgrn_scan_fwd(name, p, lbp_f, lbp_b):
    t = p.shape[0]
    hw = lbp_f.shape[1]
    nh, nc = hw // HEAD, t // CHUNK

    def body(qf, vf, zf, qb, vb, zb, lbf, lbb, of_ref, ob_ref, stf_ref, stb_ref, state):
        n = pl.program_id(0)

        @pl.when(n == 0)
        def _():
            state[...] = jnp.zeros_like(state)

        directions = [(qf, vf, zf, lbf, of_ref, stf_ref), (qb, vb, zb, lbb, ob_ref, stb_ref)]
        for d, (q_ref, v_ref, z_ref, lb_ref, o_ref, st_ref) in enumerate(directions):
            mask = _chunk_mask(d == 1)
            lb = _lower_bound(lb_ref[...])
            _, _, k, cum, last = _hgrn_gates(z_ref[...], lb, mask.astype(BF16))
            v = v_ref[...].astype(BF16)
            qd = (q_ref[...] * jnp.exp(cum)).astype(BF16)
            kd = (k * jnp.exp(-cum)).astype(BF16)
            kt = (k * jnp.exp(last - cum)).astype(BF16)
            dec = jnp.exp(last)
            s_all = state[d]
            st_ref[...] = s_all
            for h in range(nh):
                sl = slice(h * HEAD, (h + 1) * HEAD)
                s_in = s_all[:, sl]
                a = jnp.where(mask, _dot(qd[:, sl], kd[:, sl], 1, 1), 0.0).astype(BF16)
                o_ref[:, sl] = _dot(a, v[:, sl]) + _dot(qd[:, sl], s_in.astype(BF16), 1, 1)
                state[d, :, sl] = s_in * dec[:, sl] + _dot(v[:, sl], kt[:, sl], 0, 0)

    def col(group, reverse):
        return pl.BlockSpec((CHUNK, hw), lambda n: ((nc - 1 - n) if reverse else n, group))

    def st(reverse):
        return pl.BlockSpec((None, HEAD, hw), lambda n: ((nc - 1 - n) if reverse else n, 0, 0))

    lb_spec = pl.BlockSpec((2, hw), lambda n: (0, 0))
    out = jax.ShapeDtypeStruct((t, hw), F32)
    states = jax.ShapeDtypeStruct((nc, HEAD, hw), F32)
    return _pallas(
        body, name=name, grid=(nc,),
        in_specs=[col(0, False), col(1, False), col(2, False), col(0, True), col(1, True), col(3, True),
                  lb_spec, lb_spec],
        out_specs=[col(0, False), col(0, True), st(False), st(True)],
        out_shape=[out, out, states, states],
        scratch_shapes=[pltpu.VMEM((2, HEAD, hw), F32)],
        compiler_params=_params(("arbitrary",), 12 * _nbytes((HEAD, hw), F32)),
    )(p, p, p, p, p, p, lbp_f, lbp_b)


def _hgrn_scan_bwd(name, p, lbp_f, lbp_b, do, st_f, st_b):
    t = p.shape[0]
    hw = lbp_f.shape[1]
    nh, nc = hw // HEAD, t // CHUNK

    def body(qf, vf, zf, dof, sf, qb, vb, zb, dob, sb, lbf, lbb, dqf, dvf, dzf, dlbf, dqb, dvb, dzb, dlbb,
             dstate, dlb_acc, dqd_s, dkd_s, dkt_s, ddec_s):
        n = pl.program_id(0)

        @pl.when(n == 0)
        def _():
            dstate[...] = jnp.zeros_like(dstate)
            dlb_acc[...] = jnp.zeros_like(dlb_acc)

        directions = [(qf, vf, zf, dof, sf, lbf, dqf, dvf, dzf, dlbf), (qb, vb, zb, dob, sb, lbb, dqb, dvb, dzb, dlbb)]
        for d, (q_ref, v_ref, z_ref, do_ref, st_ref, lb_ref, dq_ref, dv_ref, dz_ref, dlb_ref) in enumerate(directions):
            mask = _chunk_mask(d == 1)
            mask_bf = mask.astype(BF16)
            lb = _lower_bound(lb_ref[...])
            sig, f, k, cum, last = _hgrn_gates(z_ref[...], lb, mask_bf)
            e_pos, e_neg, e_tail = jnp.exp(cum), jnp.exp(-cum), jnp.exp(last - cum)
            dec = jnp.exp(last)
            v = v_ref[...].astype(BF16)
            qd, kd, kt = q_ref[...] * e_pos, k * e_neg, k * e_tail
            qd_bf, kd_bf, kt_bf = qd.astype(BF16), kd.astype(BF16), kt.astype(BF16)
            s_all = st_ref[...]
            ds_all = dstate[d]
            dov = do_ref[...].astype(BF16)
            for h in range(nh):
                sl = slice(h * HEAD, (h + 1) * HEAD)
                s_in, ds_out = s_all[:, sl], ds_all[:, sl]
                ds_bf = ds_out.astype(BF16)
                a = jnp.where(mask, _dot(qd_bf[:, sl], kd_bf[:, sl], 1, 1), 0.0).astype(BF16)
                da = jnp.where(mask, _dot(dov[:, sl], v[:, sl], 1, 1), 0.0).astype(BF16)
                dv_ref[:, sl] = _dot(a, dov[:, sl], 0, 0) + _dot(kt_bf[:, sl], ds_bf, 1, 1)
                dqd_s[:, sl] = _dot(da, kd_bf[:, sl]) + _dot(dov[:, sl], s_in.astype(BF16))
                dkd_s[:, sl] = _dot(da, qd_bf[:, sl], 0, 0)
                dkt_s[:, sl] = _dot(v[:, sl], ds_bf)
                dstate[d, :, sl] = _dot(dov[:, sl], qd_bf[:, sl], 0, 0) + ds_out * dec[:, sl]
                ddec_s[:, sl] = jnp.sum(ds_out * s_in, axis=0, keepdims=True)
            dqd, dkd, dkt = dqd_s[...], dkd_s[...], dkt_s[...]
            dlast = jnp.sum(dkt * kt, axis=0, keepdims=True) + dec * ddec_s[...]
            dq_ref[...] = dqd * e_pos
            dk = dkd * e_neg + dkt * e_tail
            dcum = dqd * qd - dkd * kd - dkt * kt
            dlogf = _dot_exact(mask_bf, dcum, 0, 0) + dlast
            df = dlogf / f - dk
            dz_ref[...] = df * (1.0 - lb) * sig * (1.0 - sig)
            dlb_acc[d] += jnp.sum(df * (1.0 - sig), axis=0, keepdims=True)

            @pl.when(n == nc - 1)
            def _():
                g = dlb_acc[d] * lb * (1.0 - lb)
                dlb_ref[0:1, :] = g
                dlb_ref[1:2, :] = -g

    def col(group, reverse):
        return pl.BlockSpec((CHUNK, hw), lambda n: (n if reverse else (nc - 1 - n), group))

    def st(reverse):
        return pl.BlockSpec((None, HEAD, hw), lambda n: (n if reverse else (nc - 1 - n), 0, 0))

    lb_spec = pl.BlockSpec((2, hw), lambda n: (0, 0))
    out = jax.ShapeDtypeStruct((t, hw), F32)
    dlb = jax.ShapeDtypeStruct((2, hw), F32)
    wide = pltpu.VMEM((CHUNK, hw), F32)
    return _pallas(
        body, name=name, grid=(nc,),
        in_specs=[col(0, False), col(1, False), col(2, False), col(0, False), st(False),
                  col(0, True), col(1, True), col(3, True), col(0, True), st(True), lb_spec, lb_spec],
        out_specs=[col(0, False), col(0, False), col(0, False), lb_spec,
                   col(0, True), col(0, True), col(0, True), lb_spec],
        out_shape=[out, out, out, dlb, out, out, out, dlb],
        scratch_shapes=[pltpu.VMEM((2, HEAD, hw), F32), pltpu.VMEM((2, 1, hw), F32), wide, wide, wide,
                        pltpu.VMEM((1, hw), F32)],
        compiler_params=_params(("arbitrary",), 16 * _nbytes((HEAD, hw), F32)),
    )(p, p, p, do, st_f, p, p, p, do, st_b, lbp_f, lbp_b)


def _hgrn_out_fwd(name, o_f, o_b, p, gain, g_group):
    t, hw = o_f.shape
    nh = hw // HEAD
    tm = _tile(t, 512, 8)

    def body(of_ref, ob_ref, g_ref, gain_ref, y_ref):
        o = of_ref[...] + ob_ref[...]
        g = g_ref[...]
        y_ref[...] = (o * _rms(o) * gain_ref[...] * (g * _sigmoid(g))).astype(BF16)

    blk = pl.BlockSpec((tm, HEAD), lambda i, h: (i, h))
    return _pallas(
        body, name=name, grid=(t // tm, nh),
        in_specs=[blk, blk, pl.BlockSpec((tm, HEAD), lambda i, h: (i, g_group * nh + h)),
                  pl.BlockSpec((1, HEAD), lambda i, h: (0, h))],
        out_specs=blk, out_shape=jax.ShapeDtypeStruct((t, hw), BF16),
        compiler_params=_params(("parallel", "parallel"), 1 << 20),
    )(o_f, o_b, p, gain)


def _hgrn_out_bwd(name, dy, o_f, o_b, p, gain, g_group, after=None):
    t, hw = o_f.shape
    nh = hw // HEAD
    tm = _tile(t, 512, 8)

    def body(dy_ref, of_ref, ob_ref, g_ref, gain_ref, do_ref, dg_ref, dgain_ref):
        i = pl.program_id(1)
        o = of_ref[...] + ob_ref[...]
        g = g_ref[...]
        gain_v = gain_ref[...]
        sig = _sigmoid(g)
        dyv = dy_ref[...]
        do, dgain = _norm_bwd(dyv * (g * sig), o, gain_v)
        do_ref[...] = do
        dg_ref[...] = dyv * (o * _rms(o) * gain_v) * sig * (1.0 + g * (1.0 - sig))
        _accumulate(dgain_ref, dgain, i == 0)

    blk = pl.BlockSpec((tm, HEAD), lambda h, i: (i, h))
    vec = pl.BlockSpec((1, HEAD), lambda h, i: (0, h))
    out = jax.ShapeDtypeStruct((t, hw), F32)
    body, ins, in_specs = _ordered(
        body, [dy, o_f, o_b, p, gain],
        [blk, blk, blk, pl.BlockSpec((tm, HEAD), lambda h, i: (i, g_group * nh + h)), vec], after)
    return _pallas(
        body, name=name, grid=(nh, t // tm), in_specs=in_specs,
        out_specs=[blk, blk, vec], out_shape=[out, out, jax.ShapeDtypeStruct((1, hw), F32)],
        compiler_params=_params(("parallel", "arbitrary"), 1 << 20),
    )(*ins)


def _t5_bucket_ids():
    c = np.arange(WINDOW)[:, None]
    s = np.arange(SPAN)[None, :]
    rel = s - WINDOW - c
    nb = REL_BUCKETS // 2
    max_exact = nb // 2
    bucket = (rel > 0).astype(np.int32) * nb
    n = np.abs(rel)
    large = max_exact + (np.log(np.maximum(n, 1) / max_exact) / np.log(REL_MAX_DIST / max_exact)
                         * (nb - max_exact)).astype(np.int32)
    large = np.minimum(large, nb - 1)
    ids = bucket + np.where(n < max_exact, n, large).astype(np.int32)
    return jnp.asarray(ids.reshape(1, WINDOW * SPAN), jnp.int32)


def _bias_onehot(ids_ref):
    n = ids_ref.shape[1]
    return (lax.broadcasted_iota(jnp.int32, (REL_BUCKETS, n), 0) == ids_ref[...]).astype(BF16)


def _bias_gather(name, table_t, ids):
    nh = table_t.shape[0]

    def body(t_ref, ids_ref, o_ref):
        o_ref[...] = _dot_exact(t_ref[...], _bias_onehot(ids_ref), split="a")

    return _pallas(
        body, name=name, out_shape=jax.ShapeDtypeStruct((nh, ids.shape[1]), F32),
        compiler_params=pltpu.CompilerParams(vmem_limit_bytes=32 << 20),
    )(table_t, ids)


def _bias_scatter(name, dbias, ids):
    nh = dbias.shape[0]

    def body(d_ref, ids_ref, o_ref):
        o_ref[...] = _dot_exact(d_ref[...], _bias_onehot(ids_ref), 1, 1, split="a")

    return _pallas(
        body, name=name, out_shape=jax.ShapeDtypeStruct((nh, REL_BUCKETS), F32),
        compiler_params=pltpu.CompilerParams(vmem_limit_bytes=32 << 20),
    )(dbias, ids)


def _attn_valid(i, t):
    c = lax.broadcasted_iota(jnp.int32, (WINDOW, SPAN), 0)
    s = lax.broadcasted_iota(jnp.int32, (WINDOW, SPAN), 1)
    rel = s - WINDOW - c
    pos = i * WINDOW - WINDOW + s
    return (jnp.abs(rel) <= WINDOW) & (pos >= 0) & (pos < t)


def _attn_probs(qh, kh, bias_h, sink_h, valid):
    s = _dot(qh, kh, 1, 1) / math.sqrt(HEAD)
    s = jnp.where(valid, s + bias_h, NEG_INF)
    m = jnp.maximum(jnp.max(s, axis=-1, keepdims=True), sink_h)
    e = jnp.exp(s - m)
    es = jnp.exp(sink_h - m)
    inv = 1.0 / (jnp.sum(e, axis=-1, keepdims=True) + es)
    return e * inv, es * inv


def _attn_fwd(name, p, k_pad, v_pad, bias, sink, q_group_blk):
    t = p.shape[0]
    nh = bias.shape[0]
    aw = nh * HEAD
    grp = nh // KV_HEADS
    nb = t // WINDOW

    def body(q_ref, k_ref, v_ref, b_ref, s_ref, y_ref):
        i = pl.program_id(0)
        valid = _attn_valid(i, t)
        start = pl.multiple_of(i * WINDOW, WINDOW)
        ks = k_ref[pl.ds(start, SPAN), :]
        vs = v_ref[pl.ds(start, SPAN), :]
        for h in range(nh):
            kv = h // grp
            qh = q_ref[:, h * HEAD:(h + 1) * HEAD].astype(BF16)
            pr, _ = _attn_probs(qh, ks[:, kv * HEAD:(kv + 1) * HEAD], b_ref[h], s_ref[0:1, h:h + 1], valid)
            y_ref[:, h * HEAD:(h + 1) * HEAD] = _dot(pr.astype(BF16), vs[:, kv * HEAD:(kv + 1) * HEAD]).astype(BF16)

    full = lambda a: pl.BlockSpec(a.shape, lambda i: (0,) * a.ndim)
    return _pallas(
        body, name=name, grid=(nb,),
        in_specs=[pl.BlockSpec((WINDOW, aw), lambda i: (i, q_group_blk)), full(k_pad), full(v_pad), full(bias),
                  full(sink)],
        out_specs=pl.BlockSpec((WINDOW, aw), lambda i: (i, 0)),
        out_shape=jax.ShapeDtypeStruct((t, aw), BF16),
        compiler_params=_params(("parallel",), _nbytes(k_pad.shape, BF16) * 2 + _nbytes(bias.shape, F32)),
    )(p, k_pad, v_pad, bias, sink)


def _attn_bwd(name, p, k_pad, v_pad, bias, sink, dy, q_group_blk, dy_blk, after=None):
    t = p.shape[0]
    nh = bias.shape[0]
    aw = nh * HEAD
    grp = nh // KV_HEADS
    nb = t // WINDOW
    kvw = k_pad.shape[1]

    def body(q_ref, k_ref, v_ref, b_ref, s_ref, dy_ref, dq_ref, dk_ref, dv_ref, db_ref, ds_ref):
        i = pl.program_id(0)

        @pl.when(i == 0)
        def _():
            dk_ref[...] = jnp.zeros_like(dk_ref)
            dv_ref[...] = jnp.zeros_like(dv_ref)
            db_ref[...] = jnp.zeros_like(db_ref)
            ds_ref[...] = jnp.zeros_like(ds_ref)

        valid = _attn_valid(i, t)
        start = pl.multiple_of(i * WINDOW, WINDOW)
        ks = k_ref[pl.ds(start, SPAN), :]
        vs = v_ref[pl.ds(start, SPAN), :]
        inv_sqrt = 1.0 / math.sqrt(HEAD)
        for kv in range(KV_HEADS):
            kh = ks[:, kv * HEAD:(kv + 1) * HEAD]
            vh = vs[:, kv * HEAD:(kv + 1) * HEAD]
            dk_acc = jnp.zeros((SPAN, HEAD), F32)
            dv_acc = jnp.zeros((SPAN, HEAD), F32)
            for h in range(kv * grp, (kv + 1) * grp):
                qh = q_ref[:, h * HEAD:(h + 1) * HEAD].astype(BF16)
                pr, ps = _attn_probs(qh, kh, b_ref[h], s_ref[0:1, h:h + 1], valid)
                doh = dy_ref[:, h * HEAD:(h + 1) * HEAD].astype(BF16)
                dp = _dot(doh, vh, 1, 1)
                delta = jnp.sum(pr * dp, axis=-1, keepdims=True)
                dsc = pr * (dp - delta)
                db_ref[h] += dsc
                ds_ref[h:h + 1, :] += jnp.broadcast_to(jnp.sum(-ps * delta, axis=0, keepdims=True), (1, 128))
                dsr = (dsc * inv_sqrt).astype(BF16)
                dq_ref[:, h * HEAD:(h + 1) * HEAD] = _dot(dsr, kh)
                dk_acc += _dot(dsr, qh, 0, 0)
                dv_acc += _dot(pr.astype(BF16), doh, 0, 0)
            dk_ref[pl.ds(start, SPAN), kv * HEAD:(kv + 1) * HEAD] += dk_acc
            dv_ref[pl.ds(start, SPAN), kv * HEAD:(kv + 1) * HEAD] += dv_acc

    full = lambda a: pl.BlockSpec(a.shape, lambda i: (0,) * a.ndim)
    whole = lambda shape: pl.BlockSpec(shape, lambda i: (0,) * len(shape))
    pad_shape = (t + 2 * WINDOW, kvw)
    body, ins, in_specs = _ordered(
        body, [p, k_pad, v_pad, bias, sink, dy],
        [pl.BlockSpec((WINDOW, aw), lambda i: (i, q_group_blk)), full(k_pad), full(v_pad), full(bias), full(sink),
         pl.BlockSpec((WINDOW, aw), lambda i: (i, dy_blk))], after)
    return _pallas(
        body, name=name, grid=(nb,), in_specs=in_specs,
        out_specs=[pl.BlockSpec((WINDOW, aw), lambda i: (i, 0)), whole(pad_shape), whole(pad_shape),
                   whole(bias.shape), whole((nh, 128))],
        out_shape=[jax.ShapeDtypeStruct((t, aw), F32), jax.ShapeDtypeStruct(pad_shape, F32),
                   jax.ShapeDtypeStruct(pad_shape, F32), jax.ShapeDtypeStruct(bias.shape, F32),
                   jax.ShapeDtypeStruct((nh, 128), F32)],
        compiler_params=_params(("arbitrary",), 3 * _nbytes(pad_shape, F32) + 2 * _nbytes(bias.shape, F32)),
    )(*ins)


def _pad_kv(name, p, kv_blk, kvw):
    t = p.shape[0]
    nb = t // WINDOW

    def body(x_ref, o_ref):
        i = pl.program_id(0)
        inside = jnp.logical_and(i >= 1, i <= nb)
        o_ref[...] = jnp.where(inside, x_ref[...], 0.0).astype(BF16)

    return _pallas(
        body, name=name, grid=(nb + 2,),
        in_specs=[pl.BlockSpec((WINDOW, kvw), lambda i: (jnp.clip(i - 1, 0, nb - 1), kv_blk))],
        out_specs=pl.BlockSpec((WINDOW, kvw), lambda i: (i, 0)),
        out_shape=jax.ShapeDtypeStruct((t + 2 * WINDOW, kvw), BF16),
        compiler_params=_params(("parallel",), 1 << 20),
    )(p)


def _mix_dproj(name, pieces, kv_pads, t, after=None):
    hw = pieces[0][0].shape[1]
    kvw = kv_pads[0].shape[1]
    widths = [hw] * len(pieces) + [kvw] * len(kv_pads)
    total = sum(widths)
    tm = WINDOW
    flat = [a for pc in pieces for a in pc]

    def body(*refs):
        o_ref = refs[-1]
        pos, off = 0, 0
        for pc in pieces:
            val = refs[pos][...]
            for extra in range(1, len(pc)):
                val = val + refs[pos + extra][...]
            o_ref[:, off:off + hw] = val.astype(BF16)
            pos += len(pc)
            off += hw
        for _ in kv_pads:
            o_ref[:, off:off + kvw] = refs[pos][...].astype(BF16)
            pos += 1
            off += kvw

    in_specs = [pl.BlockSpec((tm, hw), lambda i: (i, 0)) for _ in flat]
    in_specs += [pl.BlockSpec((tm, kvw), lambda i: (i + 1, 0)) for _ in kv_pads]
    body, ins, in_specs = _ordered(body, [*flat, *kv_pads], in_specs, after)
    return _pallas(
        body, name=name, grid=(t // tm,), in_specs=in_specs,
        out_specs=pl.BlockSpec((tm, total), lambda i: (i, 0)),
        out_shape=jax.ShapeDtypeStruct((t, total), BF16),
        compiler_params=_params(("parallel",), 3 * _nbytes((tm, total), F32)),
    )(*ins)


def _concat_cols(name, a, b):
    t, wa = a.shape
    wb = b.shape[1]
    tm = _tile(t, 512, 16)

    def body(a_ref, b_ref, o_ref):
        o_ref[:, :wa] = a_ref[...]
        o_ref[:, wa:] = b_ref[...]

    return _pallas(
        body, name=name, grid=(t // tm,),
        in_specs=[pl.BlockSpec((tm, wa), lambda i: (i, 0)), pl.BlockSpec((tm, wb), lambda i: (i, 0))],
        out_specs=pl.BlockSpec((tm, wa + wb), lambda i: (i, 0)),
        out_shape=jax.ShapeDtypeStruct((t, wa + wb), a.dtype),
        compiler_params=_params(("parallel",), 2 * _nbytes((tm, wa + wb), a.dtype)),
    )(a, b)


def _cast_into_full(name, w, geom, idx, after=None):
    r, c = w.shape
    tr = _tile(r, 256, 16)
    nr = r // tr
    if geom.col:
        place = lambda i, iref: (i, iref[0])
    else:
        place = lambda i, iref: (iref[0] * nr + i, 0)

    def body(i_ref, w_ref, *rest):
        rest[-1][...] = w_ref[...].astype(BF16)

    in_specs = [pl.BlockSpec((tr, c), lambda i, iref: (i, 0))]
    ins = [w]
    if after is not None:
        in_specs.append(pl.BlockSpec(memory_space=pl.ANY))
        ins.append(after)
    return _pallas(
        body, name=name,
        grid_spec=pltpu.PrefetchScalarGridSpec(
            num_scalar_prefetch=1, grid=(nr,), in_specs=in_specs, out_specs=pl.BlockSpec((tr, c), place)),
        out_shape=pltpu.HBM(geom.full_shape, BF16),
        compiler_params=_params(("parallel",), 2 * _nbytes((tr, c), F32)),
    )(idx, *ins)


def _adamw(name, w, g, m, v):
    r, c = w.shape
    tr = _tile(r, 128, 8)
    bc1 = 1.0 - ADAM_B1 ** ADAM_STEP
    bc2 = 1.0 - ADAM_B2 ** ADAM_STEP

    def body(w_ref, g_ref, m_ref, v_ref, d_ref, nm_ref, nv_ref):
        gv = g_ref[...]
        nm = ADAM_B1 * m_ref[...] + (1.0 - ADAM_B1) * gv
        nv = ADAM_B2 * v_ref[...] + (1.0 - ADAM_B2) * (gv * gv)
        nm_ref[...] = nm
        nv_ref[...] = nv
        d_ref[...] = -ADAM_LR * ((nm / bc1) / (jnp.sqrt(nv / bc2) + ADAM_EPS) + ADAM_WD * w_ref[...])

    blk = pl.BlockSpec((tr, c), lambda i: (i, 0))
    out = jax.ShapeDtypeStruct((r, c), F32)
    return _pallas(
        body, name=name, grid=(r // tr,), in_specs=[blk] * 4, out_specs=[blk] * 3, out_shape=[out] * 3,
        compiler_params=_params(("parallel",), 7 * _nbytes((tr, c), F32)),
    )(w, g, m, v)


def _mesh_pos():
    return lax.axis_index("x"), lax.axis_index("y"), lax.axis_index("c")


def _other_chips(x, y):
    return [(1 - x, y), (x, 1 - y), (1 - x, 1 - y)]


class _Big:
    def __init__(self, shard_shape, col_sharded):
        self.col = col_sharded
        r, c = shard_shape
        self.shard_shape = (r, c)
        self.full_shape = (r, N_CHIPS * c) if col_sharded else (N_CHIPS * r, c)
        self.half_shape = (r // 2, N_CHIPS * c) if col_sharded else (N_CHIPS * r, c // 2)
        self.shard_half_shape = (r // 2, c) if col_sharded else (r, c // 2)

    def region(self, ref, s, half=None):
        r, c = self.shard_shape
        if self.col:
            rows = slice(None) if half is None else pl.ds(half * (r // 2), r // 2)
            return ref.at[rows, pl.ds(s * c, c)]
        cols = slice(None) if half is None else pl.ds(half * (c // 2), c // 2)
        return ref.at[pl.ds(s * r, r), cols]

    def three_halves(self, ref, half):
        r, c = self.shard_shape
        if self.col:
            return ref.at[pl.ds(half * (r // 2), r // 2), pl.ds(0, 3 * c)]
        return ref.at[pl.ds(0, 3 * r), pl.ds(half * (c // 2), c // 2)]

    def half_of_full(self, ref, half):
        r, c = self.full_shape
        if self.col:
            return ref.at[pl.ds(half * (r // 2), r // 2), :]
        return ref.at[:, pl.ds(half * (c // 2), c // 2)]

    def half_of_shard(self, ref, half):
        r, c = self.shard_shape
        if self.col:
            return ref.at[pl.ds(half * (r // 2), r // 2), :]
        return ref.at[:, pl.ds(half * (c // 2), c // 2)]

    def shard_of_half(self, ref, s):
        r, c = self.shard_shape
        if self.col:
            return ref.at[:, pl.ds(s * c, c)]
        return ref.at[pl.ds(s * r, r), :]


HBM =pl.BlockSpec(memory_space=pltpu.HBM)
SEM = pl.BlockSpec(memory_space=pltpu.SEMAPHORE)
SPLIT_COPY = pltpu.CompilerParams(has_side_effects=pltpu.SideEffectType.DATAFLOW_SIDE_EFFECTING)


def _in_hbm(a):
    return pltpu.with_memory_space_constraint(a, pltpu.HBM)


def _gather_start(name, fulls, geoms):
    nw = len(fulls)

    def body(*refs):
        dst = refs[nw:2 * nw]
        sems = refs[2 * nw:-1]
        x, y, c = _mesh_pos()
        mine = 2 * x + y
        for w in range(nw):
            own_half = geoms[w].region(dst[w], mine, c)
            for chip in _other_chips(x, y):
                pltpu.make_async_remote_copy(src_ref=own_half, dst_ref=own_half, send_sem=sems[2 * w],
                                             recv_sem=sems[2 * w + 1], device_id=(*chip, c),
                                             device_id_type=MESH).start()
        refs[-1][...] = jnp.zeros_like(refs[-1])

    out = _pallas(
        body, name=name, in_specs=[HBM] * nw,
        out_specs=[HBM] * nw + [SEM] * (2 * nw) + [pl.BlockSpec(memory_space=pltpu.VMEM)],
        out_shape=[pltpu.HBM(g.full_shape, BF16) for g in geoms] + [pltpu.SemaphoreType.DMA(())] * (2 * nw)
        + [jax.ShapeDtypeStruct((8, 128), F32)],
        input_output_aliases={w: w for w in range(nw)}, compiler_params=SPLIT_COPY,
    )(*[_in_hbm(a) for a in fulls])
    return out[:nw], [(out[nw + 2 * w], out[nw + 2 * w + 1]) for w in range(nw)], out[-1]


def _wait_three(geom, ref, half, send_sem, recv_sem, peer, recv):
    three = geom.three_halves(ref, half)
    copy = pltpu.make_async_remote_copy(src_ref=three, dst_ref=three, send_sem=send_sem, recv_sem=recv_sem,
                                        device_id=peer, device_id_type=MESH)
    if recv:
        copy.wait_recv()
    else:
        copy.wait_send()


def _gather_forward(name, full, geom, sems, after):
    def body(w_in, send_sem, recv_sem, after_ref, w_ref, fwd_send, fwd_recv):
        x, y, c = _mesh_pos()
        sibling = (x, y, 1 - c)
        _wait_three(geom, w_ref, c, send_sem, recv_sem, sibling, recv=True)
        for chip in _other_chips(x, y):
            landed = geom.region(w_ref, 2 * chip[0] + chip[1], c)
            pltpu.make_async_remote_copy(src_ref=landed, dst_ref=landed, send_sem=fwd_send, recv_sem=fwd_recv,
                                         device_id=sibling, device_id_type=MESH).start()
        _wait_three(geom, w_ref, c, send_sem, recv_sem, sibling, recv=False)

    sem = pltpu.SemaphoreType.DMA(())
    out = _pallas(
        body, name=name, in_specs=[HBM, SEM, SEM, pl.BlockSpec(memory_space=pl.ANY)], out_specs=[HBM, SEM, SEM],
        out_shape=[pltpu.HBM(geom.full_shape, BF16), sem, sem],
        input_output_aliases={0: 0}, compiler_params=SPLIT_COPY,
    )(full, sems[0], sems[1], after)
    return out[0], (out[1], out[2])


def _gather_end(name, full, geom, sems, after):
    def body(w_in, fwd_send, fwd_recv, after_ref, w_ref):
        x, y, c = _mesh_pos()
        sibling = (x, y, 1 - c)
        _wait_three(geom, w_ref, 1 - c, fwd_send, fwd_recv, sibling, recv=True)
        _wait_three(geom, w_ref, c, fwd_send, fwd_recv, sibling, recv=False)

    return _pallas(
        body, name=name, in_specs=[HBM, SEM, SEM, pl.BlockSpec(memory_space=pl.ANY)], out_specs=HBM,
        out_shape=pltpu.HBM(geom.full_shape, BF16),
        input_output_aliases={0: 0}, compiler_params=SPLIT_COPY,
    )(full, sems[0], sems[1], after)


def _split_copy_call(name, arrays, fn, sems=(), after=None, new_sems=0):
    n, ns = len(arrays), len(sems)
    n_in = n + ns + (after is not None)

    def body(*refs):
        fn(refs[n_in:n_in + n], refs[n:n + ns], refs[n_in + n:-1])
        refs[-1][...] = jnp.zeros_like(refs[-1])

    ins = list(arrays) if ns else [_in_hbm(a) for a in arrays]
    ins += list(sems) + ([after] if after is not None else [])
    in_specs = [HBM] * n + [SEM] * ns + ([pl.BlockSpec(memory_space=pl.ANY)] if after is not None else [])
    out = _pallas(
        body, name=name, in_specs=in_specs,
        out_specs=[HBM] * n + [SEM] * new_sems + [pl.BlockSpec(memory_space=pltpu.VMEM)],
        out_shape=[pltpu.HBM(a.shape, a.dtype) for a in arrays] + [pltpu.SemaphoreType.DMA(())] * new_sems
        + [jax.ShapeDtypeStruct((8, 128), F32)],
        input_output_aliases={i: i for i in range(n)}, compiler_params=SPLIT_COPY,
    )(*ins)
    return list(out[:n]), tuple(out[n:-1]), out[-1]


def _remote(src, dst, sems, to):
    return pltpu.make_async_remote_copy(src_ref=src, dst_ref=dst, send_sem=sems[0], recv_sem=sems[1],
                                        device_id=to, device_id_type=MESH)


class _GradReduce:
    def __init__(self, name, geom, idx, c_idx):
        self.name, self.geom, self.idx, self.c_idx = name, geom, idx, c_idx

    def pair_start(self, dw):
        g = self.geom

        def start(refs, _, new):
            x, y, c = _mesh_pos()
            _remote(g.half_of_full(refs[0], 1 - c), refs[1], new, (x, y, 1 - c)).start()

        self.arrays, self.sems, token = _split_copy_call(
            f"pair_start_{self.name}", [dw, lax.empty(g.half_shape, BF16)], start, new_sems=2)
        return token

    def pair_finish(self, after):
        g = self.geom

        def wait(refs, sems, _):
            x, y, c = _mesh_pos()
            copy = _remote(g.half_of_full(refs[0], 1 - c), refs[1], sems, (x, y, 1 - c))
            copy.wait_send()
            copy.wait_recv()

        (dw, landed), _, _ = _split_copy_call(f"pair_wait_{self.name}", self.arrays, wait, self.sems, after)
        half = _pair_add(f"pair_add_{self.name}", dw, landed, g, self.c_idx)

        def start(refs, _, new):
            x, y, c = _mesh_pos()
            for k, chip in enumerate(_other_chips(x, y)):
                _remote(g.shard_of_half(refs[0], 2 * chip[0] + chip[1]), refs[1].at[k], new, (*chip, c)).start()

        self.arrays, self.sems, token = _split_copy_call(
            f"chip_start_{self.name}", [half, lax.empty((3,) + g.shard_half_shape, BF16)], start, new_sems=2)
        return token

    def chip_finish(self, after):
        g = self.geom

        def wait(refs, sems, _):
            x, y, c = _mesh_pos()
            three = _remote(refs[1], refs[1], sems, (x, y, 1 - c))
            three.wait_send()
            three.wait_recv()

        (half, landed), _, _ = _split_copy_call(f"chip_wait_{self.name}", self.arrays, wait, self.sems, after)
        quarter = _chip_add(f"chip_add_{self.name}", half, landed, g, self.idx)

        def start(refs, _, new):
            x, y, c = _mesh_pos()
            own = g.half_of_shard(refs[0], c)
            _remote(own, own, new, (x, y, 1 - c)).start()

        self.arrays, self.sems, token = _split_copy_call(f"share_start_{self.name}", [quarter], start, new_sems=2)
        return token

    def finish(self, after):
        g = self.geom

        def wait(refs, sems, _):
            x, y, c = _mesh_pos()
            own, theirs = g.half_of_shard(refs[0], c), g.half_of_shard(refs[0], 1 - c)
            _remote(own, own, sems, (x, y, 1 - c)).wait_send()
            _remote(theirs, theirs, sems, (x, y, 1 - c)).wait_recv()

        (quarter,), _, _ = _split_copy_call(f"share_wait_{self.name}", self.arrays, wait, self.sems, after)
        return quarter


def _pair_add(name, grad, recv, geom, c_idx):
    r, c = geom.half_shape
    tr, tc = _tile(r, 256, 16), _tile(c, 2048, 128)
    nr, ncol = r // tr, c // tc
    if geom.col:
        mine = lambda i, j, cref: (cref[0] * nr + i, j)
    else:
        mine = lambda i, j, cref: (i, cref[0] * ncol + j)

    def body(c_ref, g_ref, r_ref, o_ref):
        o_ref[...] = (g_ref[...].astype(F32) + r_ref[...].astype(F32)).astype(BF16)

    return _pallas(
        body, name=name,
        grid_spec=pltpu.PrefetchScalarGridSpec(
            num_scalar_prefetch=1, grid=(nr, ncol),
            in_specs=[pl.BlockSpec((tr, tc), mine), pl.BlockSpec((tr, tc), lambda i, j, cref: (i, j))],
            out_specs=pl.BlockSpec((tr, tc), lambda i, j, cref: (i, j))),
        out_shape=jax.ShapeDtypeStruct((r, c), BF16),
        compiler_params=_params(("parallel", "parallel"), 3 * _nbytes((tr, tc), F32)),
    )(c_idx, grad, recv)


def _chip_add(name, half, recv, geom, idx):
    r, c = geom.shard_half_shape
    tr, tc = _tile(r, 256, 16), _tile(c, 2048, 128)
    nr, ncol = r // tr, c // tc
    if geom.col:
        mine = lambda i, j, iref: (i, iref[0] * ncol + j)
        place = lambda i, j, iref: (iref[1] * nr + i, j)
    else:
        mine = lambda i, j, iref: (iref[0] * nr + i, j)
        place = lambda i, j, iref: (i, iref[1] * ncol + j)

    def body(i_ref, h_ref, r_ref, o_ref):
        acc = h_ref[...].astype(F32)
        for k in range(3):
            acc = acc + r_ref[k].astype(F32)
        o_ref[...] = acc

    return _pallas(
        body, name=name,
        grid_spec=pltpu.PrefetchScalarGridSpec(
            num_scalar_prefetch=1, grid=(nr, ncol),
            in_specs=[pl.BlockSpec((tr, tc), mine), pl.BlockSpec((3, tr, tc), lambda i, j, iref: (0, i, j))],
            out_specs=pl.BlockSpec((tr, tc), place)),
        out_shape=jax.ShapeDtypeStruct(geom.shard_shape, F32),
        compiler_params=_params(("parallel", "parallel"), 4 * _nbytes((tr, tc), F32)),
    )(idx, half, recv)


def _all_reduce_small(pack):
    r, d = pack.shape

    def body(p_ref, o_ref, slots, send_sems, recv_sems):
        x, y, c = _mesh_pos()
        me = 4 * x + 2 * y + c
        slots[me] = p_ref[...]
        copies = []
        for k in range(1, N_DEV):
            px, py, pc = x ^ ((k >> 2) & 1), y ^ ((k >> 1) & 1), c ^ (k & 1)
            copies.append(pltpu.make_async_remote_copy(
                src_ref=p_ref, dst_ref=slots.at[me], send_sem=send_sems.at[k - 1], recv_sem=recv_sems.at[k - 1],
                device_id=(px, py, pc), device_id_type=MESH))
        for cp in copies:
            cp.start()
        for k in range(1, N_DEV):
            peer = 4 * (x ^ ((k >> 2) & 1)) + 2 * (y ^ ((k >> 1) & 1)) + (c ^ (k & 1))
            pltpu.make_async_remote_copy(
                src_ref=p_ref, dst_ref=slots.at[peer], send_sem=send_sems.at[k - 1], recv_sem=recv_sems.at[k - 1],
                device_id=(x, y, c), device_id_type=MESH).wait_recv()
        for cp in copies:
            cp.wait_send()
        acc = slots[0]
        for k in range(1, N_DEV):
            acc = acc + slots[k]
        o_ref[...] = acc

    vm = pl.BlockSpec(memory_space=pltpu.VMEM)
    return _pallas(
        body, name="all_reduce_small", in_specs=[vm], out_specs=vm,
        out_shape=jax.ShapeDtypeStruct((r, d), F32),
        scratch_shapes=[pltpu.VMEM((N_DEV, r, d), F32), pltpu.SemaphoreType.DMA((N_DEV - 1,)),
                        pltpu.SemaphoreType.DMA((N_DEV - 1,))],
    )(pack)


def _pack_rows(rows, d):
    out = []
    for a in rows:
        flat = a.reshape(-1)
        n = -(-flat.shape[0] // d) * d
        out.append(jnp.pad(flat, (0, n - flat.shape[0])).reshape(-1, d))
    packed = jnp.concatenate(out, axis=0)
    return jnp.pad(packed, ((0, 16 - packed.shape[0]), (0, 0)))


def _unpack_rows(packed, shapes, d):
    out, row = [], 0
    for shp in shapes:
        n = int(np.prod(shp))
        nrows = -(-n // d)
        out.append(packed[row:row + nrows].reshape(-1)[:n].reshape(shp))
        row += nrows
    return out


def kernel(x, pre_norm_ffn1, post_norm_ffn1, w_ffn1_gate_up, w_ffn1_down, pre_norm_mix, post_norm_mix, w_mix_in, hgrn_lower_bounds_fwd, hgrn_lower_bounds_bwd, hgrn_out_norm, attn_sink, w_mix_out, pre_norm_ffn2, post_norm_ffn2, w_ffn2_gate_up, w_ffn2_down, rel_bias_table, loss_target, m_pre_norm_ffn1, m_post_norm_ffn1, m_w_ffn1_gate_up, m_w_ffn1_down, m_pre_norm_mix, m_post_norm_mix, m_w_mix_in, m_hgrn_lower_bounds_fwd, m_hgrn_lower_bounds_bwd, m_hgrn_out_norm, m_attn_sink, m_w_mix_out, m_pre_norm_ffn2, m_post_norm_ffn2, m_w_ffn2_gate_up, m_w_ffn2_down, m_rel_bias_table, v_pre_norm_ffn1, v_post_norm_ffn1, v_w_ffn1_gate_up, v_w_ffn1_down, v_pre_norm_mix, v_post_norm_mix, v_w_mix_in, v_hgrn_lower_bounds_fwd, v_hgrn_lower_bounds_bwd, v_hgrn_out_norm, v_attn_sink, v_w_mix_out, v_pre_norm_ffn2, v_post_norm_ffn2, v_w_ffn2_gate_up, v_w_ffn2_down, v_rel_bias_table):
    t, d = x.shape[1], x.shape[2]
    hw = hgrn_out_norm.shape[1]
    aw = d - hw
    nah = aw // HEAD
    kvw = KV_HEADS * HEAD
    x0 = x[0]
    target = loss_target[0]

    big_names = ["w_ffn1_gate_up", "w_ffn1_down", "w_mix_in", "w_mix_out", "w_ffn2_gate_up", "w_ffn2_down"]
    big_w = [w_ffn1_gate_up[0], w_ffn1_down[0], w_mix_in[0], w_mix_out[0], w_ffn2_gate_up[0], w_ffn2_down[0]]
    big_m = [m_w_ffn1_gate_up[0], m_w_ffn1_down[0], m_w_mix_in[0], m_w_mix_out[0], m_w_ffn2_gate_up[0],
             m_w_ffn2_down[0]]
    big_v = [v_w_ffn1_gate_up[0], v_w_ffn1_down[0], v_w_mix_in[0], v_w_mix_out[0], v_w_ffn2_gate_up[0],
             v_w_ffn2_down[0]]
    col_sharded = [True, False, True, False, True, False]
    geoms = [_Big(w.shape, cs) for w, cs in zip(big_w, col_sharded)]

    cx, cy, cc = _mesh_pos()
    idx = jnp.stack([2 * cx + cy, cc]).astype(jnp.int32)
    c_idx = jnp.reshape(cc, (1,)).astype(jnp.int32)
    first = _cast_into_full(f"cast_{big_names[0]}", big_w[0], geoms[0], idx)
    started, gather_sems, tok = _gather_start("gather_start_first", [first], geoms[:1])
    rest = [_cast_into_full(f"cast_{n}", w, gm, idx, after=tok)
            for n, w, gm in zip(big_names[1:], big_w[1:], geoms[1:])]
    started_rest, sems_rest, _ = _gather_start("gather_start_rest", rest, geoms[1:])
    started, gather_sems = list(started) + list(started_rest), gather_sems + sems_rest

    def forward_weight(w, after):
        return _gather_forward(f"gather_forward_{big_names[w]}", started[w], geoms[w], gather_sems[w], after)

    def whole_weight(w, forwarded, after):
        return _gather_end(f"gather_end_{big_names[w]}", forwarded[0], geoms[w], forwarded[1], after)

    h1 = _norm_fwd("ffn1_pre_norm", x0, pre_norm_ffn1)
    w_gu1 = whole_weight(0, forward_weight(0, h1), h1)
    gate1, up1, act1 = _ffn_gate_up_act("ffn1_gate_up", h1, w_gu1)
    w_d1 = whole_weight(1, forward_weight(1, act1), act1)
    ff1 = _mm("ffn1_down", act1, w_d1, "nn", F32)
    fw = forward_weight(2, ff1)
    x1, hm = _resid_norm_fwd("ffn1_residual", x0, ff1, post_norm_ffn1, pre_norm_mix, 0.5)
    w_in = whole_weight(2, fw, hm)
    p = _mm("mix_in", hm, w_in, "nn", F32)
    fw = forward_weight(3, p)
    o_f, o_b, st_f, st_b = _hgrn_scan_fwd("hgrn_scan", p, hgrn_lower_bounds_fwd, hgrn_lower_bounds_bwd)
    y_h = _hgrn_out_fwd("hgrn_out", o_f, o_b, p, hgrn_out_norm, 4)
    kv_blk0 = (5 * hw + aw) // kvw
    k_pad = _pad_kv("attn_pad_k", p, kv_blk0, kvw)
    v_pad = _pad_kv("attn_pad_v", p, kv_blk0 + 1, kvw)
    bucket_ids = _t5_bucket_ids()
    bias = _bias_gather("attn_bias", rel_bias_table.T, bucket_ids).reshape(nah, WINDOW, SPAN)
    y_a = _attn_fwd("attn_fwd", p, k_pad, v_pad, bias, attn_sink, 5 * hw // aw)
    y_mix = _concat_cols("mix_concat", y_h, y_a)
    w_out = whole_weight(3, fw, y_mix)
    mixed = _mm("mix_out", y_mix, w_out, "nn", F32)
    fw = forward_weight(4, mixed)
    x2, h2 = _resid_norm_fwd("mix_residual", x1, mixed, post_norm_mix, pre_norm_ffn2, 1.0)
    w_gu2 = whole_weight(4, fw, h2)
    gate2, up2, act2 = _ffn_gate_up_act("ffn2_gate_up", h2, w_gu2)
    w_d2 = whole_weight(5, forward_weight(5, act2), act2)
    ff2 = _mm("ffn2_down", act2, w_d2, "nn", F32)
    loss_blk, dy, dff2, dg_post2 = _final_fwd_bwd("ffn2_residual_loss", x2, ff2, post_norm_ffn2, target, 0.5)

    reduce = [_GradReduce(n, gm, idx, c_idx) for n, gm in zip(big_names, geoms)]
    big_grads, big_delta, big_new_m, big_new_v = [None] * 6, [None] * 6, [None] * 6, [None] * 6

    def update(w, after):
        g = reduce[w].finish(after)
        dl, nm, nv = _adamw(f"adamw_{big_names[w]}", big_w[w], g, big_m[w], big_v[w])
        big_grads[w], big_delta[w], big_new_m[w], big_new_v[w] = g[None], dl[None], nm[None], nv[None]
        return dl

    dw_d2 = _mm("ffn2_dw_down", act2, dff2, "tn", BF16)
    tok = reduce[5].pair_start(dw_d2)
    dgu2 = _ffn_dact("ffn2_dact", dff2, w_d2, gate2, up2, after=tok)
    tok = reduce[5].pair_finish(dgu2)
    dw_gu2 = _ffn_dw_gate_up("ffn2_dw_gate_up", h2, dgu2, after=tok)
    tok = reduce[4].pair_start(dw_gu2)
    dh2 = _ffn_dh("ffn2_dh", dgu2, w_gu2, after=tok)
    tok = reduce[4].pair_finish(dh2)
    dx2, dg_pre2, dmixed, dg_postm = _norms_bwd("mix_residual_bwd", dy, dh2, x2, pre_norm_ffn2,
                                                post=(mixed, post_norm_mix, 1.0), after=tok)
    dw_out = _mm("mix_out_dw", y_mix, dmixed, "tn", BF16)
    tok = reduce[3].pair_start(dw_out)
    dy_mix = _mm("mix_out_dx", dmixed, w_out, "nt", F32, after=tok)
    tok = reduce[3].pair_finish(dy_mix)
    dq_a, dk_pad, dv_pad, dbias, dsink = _attn_bwd("attn_bwd", p, k_pad, v_pad, bias, attn_sink, dy_mix,
                                                   5 * hw // aw, hw // aw, after=tok)
    tok = reduce[5].chip_finish(dq_a)
    drel_t = _bias_scatter("attn_dbias", dbias.reshape(nah, WINDOW * SPAN), bucket_ids)
    do, dg_h, dgain = _hgrn_out_bwd("hgrn_out_bwd", dy_mix, o_f, o_b, p, hgrn_out_norm, 4, after=tok)
    dq_f, dv_f, dz_f, dlb_f, dq_b, dv_b, dz_b, dlb_b = _hgrn_scan_bwd(
        "hgrn_scan_bwd", p, hgrn_lower_bounds_fwd, hgrn_lower_bounds_bwd, do, st_f, st_b)
    tok = reduce[4].chip_finish(dq_f)
    tok = reduce[3].chip_finish(tok)
    dp = _mix_dproj("mix_dproj", [(dq_f, dq_b), (dv_f, dv_b), (dz_f,), (dz_b,), (dg_h,), (dq_a,)],
                    [dk_pad, dv_pad], t, after=tok)
    dw_in = _mm("mix_in_dw", hm, dp, "tn", BF16)
    tok = reduce[2].pair_start(dw_in)
    dhm = _mm("mix_in_dx", dp, w_in, "nt", F32, after=tok)
    tok = reduce[2].pair_finish(dhm)
    dx1, dg_prem, dff1, dg_post1 = _norms_bwd("ffn1_residual_bwd", dx2, dhm, x1, pre_norm_mix,
                                              post=(ff1, post_norm_ffn1, 0.5), after=tok)
    dw_d1 = _mm("ffn1_dw_down", act1, dff1, "tn", BF16)
    tok = reduce[1].pair_start(dw_d1)
    dgu1 = _ffn_dact("ffn1_dact", dff1, w_d1, gate1, up1, after=tok)
    tok = reduce[1].pair_finish(dgu1)
    tok = reduce[2].chip_finish(tok)
    dw_gu1 = _ffn_dw_gate_up("ffn1_dw_gate_up", h1, dgu1, after=tok)
    tok = reduce[0].pair_start(dw_gu1)
    done = update(2, tok)
    tok = reduce[0].pair_finish(done)
    dh1 = _ffn_dh("ffn1_dh", dgu1, w_gu1, after=tok)
    grad_x, dg_pre1 = _norms_bwd("ffn1_pre_norm_bwd", dx1, dh1, x0, pre_norm_ffn1)

    small_w = [pre_norm_ffn1, post_norm_ffn1, pre_norm_mix, post_norm_mix, hgrn_lower_bounds_fwd,
               hgrn_lower_bounds_bwd, hgrn_out_norm, attn_sink, pre_norm_ffn2, post_norm_ffn2, rel_bias_table]
    small_m = [m_pre_norm_ffn1, m_post_norm_ffn1, m_pre_norm_mix, m_post_norm_mix, m_hgrn_lower_bounds_fwd,
               m_hgrn_lower_bounds_bwd, m_hgrn_out_norm, m_attn_sink, m_pre_norm_ffn2, m_post_norm_ffn2,
               m_rel_bias_table]
    small_v = [v_pre_norm_ffn1, v_post_norm_ffn1, v_pre_norm_mix, v_post_norm_mix, v_hgrn_lower_bounds_fwd,
               v_hgrn_lower_bounds_bwd, v_hgrn_out_norm, v_attn_sink, v_pre_norm_ffn2, v_post_norm_ffn2,
               v_rel_bias_table]
    small_g = [dg_pre1, dg_post1, dg_prem, dg_postm, dlb_f, dlb_b, dgain, dsink[:, 0].reshape(1, nah), dg_pre2,
               dg_post2, drel_t.T]
    shapes = [a.shape for a in small_w]
    summed = _all_reduce_small(_pack_rows(small_g + [loss_blk[0:1, 0:1]], d))
    g_pack = summed
    loss =_unpack_rows(summed, shapes + [(1, 1)], d)[-1][0, 0]
    sd, sm, sv = _adamw("adamw_small", _pack_rows(small_w, d), g_pack, _pack_rows(small_m, d), _pack_rows(small_v, d))
    small_grads = _unpack_rows(g_pack, shapes, d)
    small_delta, small_new_m, small_new_v = (_unpack_rows(a, shapes, d) for a in (sd, sm, sv))

    done = update(5, sd)
    done = update(4, done)
    done = update(3, done)
    tok = reduce[1].chip_finish(done)
    done = update(1, tok)
    tok = reduce[0].chip_finish(done)
    update(0, tok)

    def ordered(small, big):
        s = dict(zip(["pre1", "post1", "prem", "postm", "lbf", "lbb", "gain", "sink", "pre2", "post2", "rel"], small))
        b = dict(zip(["gu1", "d1", "win", "wout", "gu2", "d2"], big))
        return [s["pre1"], s["post1"], b["gu1"], b["d1"], s["prem"], s["postm"], b["win"], s["lbf"], s["lbb"],
                s["gain"], s["sink"], b["wout"], s["pre2"], s["post2"], b["gu2"], b["d2"], s["rel"]]

    return (loss, grad_x[None], *ordered(small_grads, big_grads), *ordered(small_delta, big_delta),
            *ordered(small_new_m, big_new_m), *ordered(small_new_v, big_new_v))
```

```python
import functools
import math

import jax
import jax.numpy as jnp
import numpy as np
from jax import lax
from jax.experimental import pallas as pl
from jax.experimental.pallas import tpu as pltpu

F32 = jnp.float32
BF16 = jnp.bfloat16

HEAD = 128
CHUNK = 64
WINDOW = 128
SPAN = 3 * WINDOW
KV_HEADS = 2
REL_BUCKETS = 32
REL_MAX_DIST = 128
EPS = 1e-6
NEG_INF = -1e30

ADAM_LR = 0.001
ADAM_B1 = 0.9
ADAM_B2 = 0.999
ADAM_EPS = 1e-08
ADAM_WD = 0.01
ADAM_STEP = 10

N_CHIPS = 4
N_DEV = 8
V7X_VMEM_BYTES = 64 * 1024 * 1024
MESH = pl.DeviceIdType.MESH
ANY = pl.BlockSpec(memory_space=pl.ANY)


def _tile(n, pref, mult):
    t = (min(pref, n) // mult) * mult
    while t >= mult:
        if n % t == 0:
            return t
        t -= mult
    return n


def _params(semantics, block_bytes):
    limit = min(V7X_VMEM_BYTES - (4 << 20), 2 * int(block_bytes) + (8 << 20))
    return pltpu.CompilerParams(dimension_semantics=semantics, vmem_limit_bytes=limit)


def _nbytes(shape, dtype):
    return int(np.prod(shape)) * jnp.dtype(dtype).itemsize


PIN_TO_HBM_BYTES = 4 << 20


def _pallas(body, **kw):
    def pin_shape(s):
        if isinstance(s, jax.ShapeDtypeStruct) and _nbytes(s.shape, s.dtype) >= PIN_TO_HBM_BYTES:
            return pltpu.HBM(s.shape, s.dtype)
        return s

    def pin(a):
        if getattr(a, "dtype", None) in (F32, BF16) and _nbytes(a.shape, a.dtype) >= PIN_TO_HBM_BYTES:
            return pltpu.with_memory_space_constraint(a, pltpu.HBM)
        return a

    out_shape = kw["out_shape"]
    kw["out_shape"] = [pin_shape(s) for s in out_shape] if isinstance(out_shape, (list, tuple)) else pin_shape(out_shape)
    call = pl.pallas_call(body, **kw)
    return lambda *args: call(*[pin(a) for a in args])


def _dot(a, b, ca=1, cb=0):
    return lax.dot_general(a, b, (((ca,), (cb,)), ((), ())), preferred_element_type=F32)


def _split3(x):
    hi = x.astype(BF16)
    r1 = x - hi.astype(F32)
    mid = r1.astype(BF16)
    lo = (r1 - mid.astype(F32)).astype(BF16)
    return hi, mid, lo


def _dot_exact(a, b, ca=1, cb=0, split="b"):
    if split == "b":
        return sum(_dot(a, p, ca, cb) for p in _split3(b))
    return sum(_dot(p, b, ca, cb) for p in _split3(a))


def _rms(x):
    return lax.rsqrt(jnp.mean(x * x, axis=-1, keepdims=True) + EPS)


def _norm_bwd(u, x, gain):
    r = _rms(x)
    xhat = x * r
    dgain = jnp.sum(u * xhat, axis=0, keepdims=True)
    v = u * gain
    dx = r * (v - xhat * jnp.mean(v * xhat, axis=-1, keepdims=True))
    return dx, dgain


def _sigmoid(x):
    return 1.0 / (1.0 + jnp.exp(-x))


def _accumulate(ref, val, first):
    @pl.when(first)
    def _():
        ref[...] = val

    @pl.when(jnp.logical_not(first))
    def _():
        ref[...] += val


def _ordered(body, ins, in_specs, after):
    if after is None:
        return body, list(ins), list(in_specs)
    n_in = len(ins)

    def wrapped(*refs):
        body(*refs[:n_in], *refs[n_in + 1:])

    return wrapped, list(ins) + [after], list(in_specs) + [pl.BlockSpec(memory_space=pl.ANY)]


def _matmul(name, a, b, *, form, out_dtype, tm, tn, tk, a_map=None, b_map=None,
            out_shape=None, out_block=None, out_map=None, sizes=None, after=None):
    if sizes is None:
        if form == "nn":
            (m, k), n = a.shape, b.shape[1]
        elif form == "nt":
            (m, k), n = a.shape, b.shape[0]
        else:
            (k, m), n = a.shape, b.shape[1]
    else:
        m, n, k = sizes
    gi, gj, gk = m // tm, n // tn, k // tk
    a_blk = (tm, tk) if form != "tn" else (tk, tm)
    b_blk = (tk, tn) if form != "nt" else (tn, tk)
    if a_map is None:
        a_map = (lambda i, j, kk: (i, kk)) if form != "tn" else (lambda i, j, kk: (kk, i))
    else:
        a_blk = (None,) + a_blk
    if b_map is None:
        b_map = (lambda i, j, kk: (kk, j)) if form != "nt" else (lambda i, j, kk: (j, kk))
    else:
        b_blk = (None,) + b_blk
    if out_shape is None:
        out_shape, out_block, out_map = (m, n), (tm, tn), (lambda i, j, kk: (i, j))
    ca, cb = {"nn": (1, 0), "nt": (1, 1), "tn": (0, 0)}[form]

    def body(a_ref, b_ref, o_ref, *acc):
        part = _dot(a_ref[...], b_ref[...], ca, cb)
        if gk == 1:
            o_ref[...] = part.astype(o_ref.dtype)
        else:
            kk = pl.program_id(2)
            _accumulate(acc[0], part, kk == 0)

            @pl.when(kk == gk - 1)
            def _():
                o_ref[...] = acc[0][...].astype(o_ref.dtype)

    scratch = [] if gk == 1 else [pltpu.VMEM((tm, tn), F32)]
    vmem = (_nbytes((tm, tk), a.dtype) + _nbytes((tk, tn), b.dtype) + _nbytes((tm, tn), out_dtype)
            + 2 * _nbytes((tm, tn), F32))
    body, ins, in_specs = _ordered(body, [a, b], [pl.BlockSpec(a_blk, a_map), pl.BlockSpec(b_blk, b_map)], after)
    return _pallas(
        body, name=name, grid=(gi, gj, gk), in_specs=in_specs,
        out_specs=pl.BlockSpec(out_block, out_map),
        out_shape=jax.ShapeDtypeStruct(out_shape, out_dtype),
        scratch_shapes=scratch,
        compiler_params=_params(("parallel", "parallel", "arbitrary"), vmem),
    )(*ins)


V7X_HBM_BYTES_PER_US = 3.0e6
V7X_MXU_FLOPS_PER_US = 0.9e9
V7X_VMEM_RMW_BYTES_PER_US = 10e6
GRID_STEP_US = 0.35
MATMUL_VMEM_BUDGET = 40 << 20
MATMUL_MAX_TILE_FLOPS = 1 << 33


def _divisors(n, mult, lo):
    return [t for t in range(mult, n + 1, mult) if n % t == 0 and t >= min(lo, n)]


def _mm_tiles(m, n, k, out_dtype=F32, n_unit=None, k_unit=None):
    out_bytes = jnp.dtype(out_dtype).itemsize
    best = None
    for tm in _divisors(m, 128, 256):
        for tn in _divisors(n_unit or n, 128, 256):
            for tk in _divisors(k_unit or k, 128, 512):
                gi, gj, gk = m // tm, n // tn, k // tk
                vmem = 4 * tm * tk + 4 * tk * tn + 2 * tm * tn * out_bytes + 4 * tm * tn * (2 if gk > 1 else 1)
                if vmem > MATMUL_VMEM_BUDGET or 2 * tm * tn * tk > MATMUL_MAX_TILE_FLOPS:
                    continue
                a_bytes = 2 * m * k * (gj if gk > 1 else 1)
                b_bytes = 2 * k * n * (1 if gj == 1 and gk == 1 else gi)
                hbm_us = (a_bytes + b_bytes + m * n * out_bytes) / V7X_HBM_BYTES_PER_US
                acc_us = (8 * m * n * gk / V7X_VMEM_RMW_BYTES_PER_US) if gk > 1 else 0.0
                cost = max(2 * m * n * k / V7X_MXU_FLOPS_PER_US, 1.3 * hbm_us) + GRID_STEP_US * gi * gj * gk + acc_us
                key = (round(cost, 1), vmem)
                if best is None or key < best[0]:
                    best = (key, (tm, tn, tk))
    return best[1]


def _mm(name, a, b, form, out_dtype, after=None):
    if form == "nn":
        m, k, n = a.shape[0], a.shape[1], b.shape[1]
    elif form == "nt":
        m, k, n = a.shape[0], a.shape[1], b.shape[0]
    else:
        m, k, n = a.shape[1], a.shape[0], b.shape[1]
    tm, tn, tk = _mm_tiles(m, n, k, out_dtype)
    return _matmul(name, a, b, form=form, out_dtype=out_dtype, tm=tm, tn=tn, tk=tk, after=after)


def _row_tile(t):
    return _tile(t, 256, 8)


def _norm_fwd(name, x, gain):
    t, d = x.shape
    tm = _row_tile(t)

    def body(x_ref, g_ref, h_ref):
        xv = x_ref[...]
        h_ref[...] = (xv * _rms(xv) * g_ref[...]).astype(BF16)

    row = pl.BlockSpec((tm, d), lambda i: (i, 0))
    vec = pl.BlockSpec((1, d), lambda i: (0, 0))
    return _pallas(
        body, name=name, grid=(t // tm,), in_specs=[row, vec], out_specs=row,
        out_shape=jax.ShapeDtypeStruct((t, d), BF16),
        compiler_params=_params(("parallel",), 2 * _nbytes((tm, d), F32)),
    )(x, gain)


def _resid_norm_fwd(name, xres, ff, gpost, gpre, scale):
    t, d = xres.shape
    tm = _row_tile(t)

    def body(x_ref, f_ref, gp_ref, gn_ref, xn_ref, h_ref):
        f = f_ref[...]
        xn = x_ref[...] + scale * (f * _rms(f) * gp_ref[...])
        xn_ref[...] = xn
        h_ref[...] = (xn * _rms(xn) * gn_ref[...]).astype(BF16)

    row = pl.BlockSpec((tm, d), lambda i: (i, 0))
    vec = pl.BlockSpec((1, d), lambda i: (0, 0))
    return _pallas(
        body, name=name, grid=(t // tm,), in_specs=[row, row, vec, vec], out_specs=[row, row],
        out_shape=[jax.ShapeDtypeStruct((t, d), F32), jax.ShapeDtypeStruct((t, d), BF16)],
        compiler_params=_params(("parallel",), 4 * _nbytes((tm, d), F32)),
    )(xres, ff, gpost, gpre)


def _final_fwd_bwd(name, xres, ff, gpost, target, scale):
    t, d = xres.shape
    tm = _row_tile(t)

    def body(x_ref, f_ref, gp_ref, t_ref, loss_ref, dy_ref, dff_ref, dg_ref):
        i = pl.program_id(0)
        f = f_ref[...]
        gp = gp_ref[...]
        y = x_ref[...] + scale * (f * _rms(f) * gp)
        err = y - t_ref[...]
        part = 0.5 * jnp.sum(jnp.mean(err * err, axis=-1, keepdims=True), axis=0, keepdims=True)
        _accumulate(loss_ref, jnp.broadcast_to(part, loss_ref.shape), i == 0)
        dy = err / d
        dy_ref[...] = dy
        dff, dg = _norm_bwd(scale * dy, f, gp)
        dff_ref[...] = dff.astype(BF16)
        _accumulate(dg_ref, dg, i == 0)

    row = pl.BlockSpec((tm, d), lambda i: (i, 0))
    vec = pl.BlockSpec((1, d), lambda i: (0, 0))
    return _pallas(
        body, name=name, grid=(t // tm,), in_specs=[row, row, vec, row],
        out_specs=[pl.BlockSpec((8, 128), lambda i: (0, 0)), row, row, vec],
        out_shape=[jax.ShapeDtypeStruct((8, 128), F32), jax.ShapeDtypeStruct((t, d), F32),
                   jax.ShapeDtypeStruct((t, d), BF16), jax.ShapeDtypeStruct((1, d), F32)],
        compiler_params=_params(("arbitrary",), 5 * _nbytes((tm, d), F32)),
    )(xres, ff, gpost, target)


def _norms_bwd(name, dres, dh, xin, gpre, post=None, after=None):
    t, d = dres.shape
    tm = _row_tile(t)
    with_post = post is not None

    def body(*refs):
        if with_post:
            dr_ref, dh_ref, x_ref, g_ref, f_ref, gp_ref, dx_ref, dg_ref, dff_ref, dgp_ref = refs
        else:
            dr_ref, dh_ref, x_ref, g_ref, dx_ref, dg_ref = refs
        i = pl.program_id(0)
        dx, dg = _norm_bwd(dh_ref[...], x_ref[...], g_ref[...])
        dx = dr_ref[...] + dx
        dx_ref[...] = dx
        _accumulate(dg_ref, dg, i == 0)
        if with_post:
            dff, dgp = _norm_bwd(post[2] * dx, f_ref[...], gp_ref[...])
            dff_ref[...] = dff.astype(BF16)
            _accumulate(dgp_ref, dgp, i == 0)

    row = pl.BlockSpec((tm, d), lambda i: (i, 0))
    vec = pl.BlockSpec((1, d), lambda i: (0, 0))
    ins, in_specs = [dres, dh, xin, gpre], [row, row, row, vec]
    out_specs = [row, vec]
    out_shape = [jax.ShapeDtypeStruct((t, d), F32), jax.ShapeDtypeStruct((1, d), F32)]
    if with_post:
        ins += [post[0], post[1]]
        in_specs += [row, vec]
        out_specs += [row, vec]
        out_shape += [jax.ShapeDtypeStruct((t, d), BF16), jax.ShapeDtypeStruct((1, d), F32)]
    body, ins, in_specs = _ordered(body, ins, in_specs, after)
    return _pallas(
        body, name=name, grid=(t // tm,), in_specs=in_specs, out_specs=out_specs, out_shape=out_shape,
        compiler_params=_params(("arbitrary",), 6 * _nbytes((tm, d), F32)),
    )(*ins)


SWIGLU_TILE = (1024, 512)


def _ffn_gate_up_act(name, h, w_gu):
    t, d = h.shape
    f = w_gu.shape[1] // 2
    tm, tn = _tile(t, SWIGLU_TILE[0], 128), _tile(f, SWIGLU_TILE[1], 128)
    nf = f // tn

    def body(h_ref, wg_ref, wu_ref, g_ref, u_ref, a_ref):
        hv = h_ref[...]
        g = _dot(hv, wg_ref[...])
        u = _dot(hv, wu_ref[...])
        g_ref[...] = g.astype(BF16)
        u_ref[...] = u.astype(BF16)
        a_ref[...] = (g * _sigmoid(g) * u).astype(BF16)

    out = jax.ShapeDtypeStruct((t, f), BF16)
    blk = pl.BlockSpec((tm, tn), lambda i, j: (i, j))
    return _pallas(
        body, name=name, grid=(t // tm, nf),
        in_specs=[pl.BlockSpec((tm, d), lambda i, j: (i, 0)), pl.BlockSpec((d, tn), lambda i, j: (0, j)),
                  pl.BlockSpec((d, tn), lambda i, j: (0, j + nf))],
        out_specs=[blk, blk, blk], out_shape=[out, out, out],
        compiler_params=_params(("parallel", "parallel"),
                                _nbytes((tm, d), BF16) + 2 * _nbytes((d, tn), BF16) + 5 * _nbytes((tm, tn), F32)),
    )(h, w_gu, w_gu)


def _ffn_dact(name, dff, w_down, gate, up, after=None):
    t, d = dff.shape
    f = w_down.shape[0]
    tm, tn = _tile(t, SWIGLU_TILE[0], 128), _tile(f, SWIGLU_TILE[1], 128)

    def body(d_ref, w_ref, g_ref, u_ref, o_ref):
        da = _dot(d_ref[...], w_ref[...], 1, 1)
        g = g_ref[...].astype(F32)
        u = u_ref[...].astype(F32)
        sig = _sigmoid(g)
        o_ref[0] = (da * u * sig * (1.0 + g * (1.0 - sig))).astype(BF16)
        o_ref[1] = (da * g * sig).astype(BF16)

    blk = pl.BlockSpec((tm, tn), lambda i, j: (i, j))
    body, ins, in_specs = _ordered(
        body, [dff, w_down, gate, up],
        [pl.BlockSpec((tm, d), lambda i, j: (i, 0)), pl.BlockSpec((tn, d), lambda i, j: (j, 0)), blk, blk], after)
    return _pallas(
        body, name=name, grid=(t // tm, f // tn), in_specs=in_specs,
        out_specs=pl.BlockSpec((2, tm, tn), lambda i, j: (0, i, j)),
        out_shape=jax.ShapeDtypeStruct((2, t, f), BF16),
        compiler_params=_params(("parallel", "parallel"),
                                _nbytes((tm, d), BF16) + _nbytes((tn, d), BF16) + 5 * _nbytes((tm, tn), F32)),
    )(*ins)


def _ffn_dh(name, dgu, w_gu, after=None):
    _, t, f = dgu.shape
    d = w_gu.shape[0]
    tm, tn, tk = _mm_tiles(t, d, 2 * f, F32, k_unit=f)
    nkf = f // tk
    return _matmul(name, dgu, w_gu, form="nt", out_dtype=F32, tm=tm, tn=tn, tk=tk, sizes=(t, d, 2 * f),
                   a_map=lambda i, j, kk: (kk // nkf, i, kk % nkf), after=after)


def _ffn_dw_gate_up(name, h, dgu, after=None):
    _, t, f = dgu.shape
    d = h.shape[1]
    tm, tn, tk = _mm_tiles(d, 2 * f, t, BF16, n_unit=f)
    nf = f // tn
    return _matmul(name, h, dgu, form="tn", out_dtype=BF16, tm=tm, tn=tn, tk=tk, sizes=(d, 2 * f, t),
                   b_map=lambda i, j, kk: (j // nf, kk, j % nf), after=after)


def _lower_bound(lbp):
    m = jnp.max(lbp, axis=0, keepdims=True)
    e = jnp.exp(lbp - m)
    return e[0:1] / jnp.sum(e, axis=0, keepdims=True)


def _chunk_mask(reverse):
    row = lax.broadcasted_iota(jnp.int32, (CHUNK, CHUNK), 0)
    col = lax.broadcasted_iota(jnp.int32, (CHUNK, CHUNK), 1)
    return (col >= row) if reverse else (col <= row)


def _hgrn_gates(z, lb, mask_bf):
    sig = _sigmoid(z)
    f = lb + (1.0 - lb) * sig
    logf = jnp.log(f)
    k = 1.0 - f
    cum = _dot_exact(mask_bf, logf)
    last = jnp.sum(logf, axis=0, keepdims=True)
    return sig, f, k, cum, last


def _hgrn_scan_fwd(name, p, lbp_f, lbp_b):
    t = p.shape[0]
    hw = lbp_f.shape[1]
    nh, nc = hw // HEAD, t // CHUNK

    def body(qf, vf, zf, qb, vb, zb, lbf, lbb, of_ref, ob_ref, stf_ref, stb_ref, state):
        n = pl.program_id(0)

        @pl.when(n == 0)
        def _():
            state[...] = jnp.zeros_like(state)

        directions = [(qf, vf, zf, lbf, of_ref, stf_ref), (qb, vb, zb, lbb, ob_ref, stb_ref)]
        for d, (q_ref, v_ref, z_ref, lb_ref, o_ref, st_ref) in enumerate(directions):
            mask = _chunk_mask(d == 1)
            lb = _lower_bound(lb_ref[...])
            _, _, k, cum, last = _hgrn_gates(z_ref[...], lb, mask.astype(BF16))
            v = v_ref[...].astype(BF16)
            qd = (q_ref[...] * jnp.exp(cum)).astype(BF16)
            kd = (k * jnp.exp(-cum)).astype(BF16)
            kt = (k * jnp.exp(last - cum)).astype(BF16)
            dec = jnp.exp(last)
            s_all = state[d]
            st_ref[...] = s_all
            for h in range(nh):
                sl = slice(h * HEAD, (h + 1) * HEAD)
                s_in = s_all[:, sl]
                a = jnp.where(mask, _dot(qd[:, sl], kd[:, sl], 1, 1), 0.0).astype(BF16)
                o_ref[:, sl] = _dot(a, v[:, sl]) + _dot(qd[:, sl], s_in.astype(BF16), 1, 1)
                state[d, :, sl] = s_in * dec[:, sl] + _dot(v[:, sl], kt[:, sl], 0, 0)

    def col(group, reverse):
        return pl.BlockSpec((CHUNK, hw), lambda n: ((nc - 1 - n) if reverse else n, group))

    def st(reverse):
        return pl.BlockSpec((None, HEAD, hw), lambda n: ((nc - 1 - n) if reverse else n, 0, 0))

    lb_spec = pl.BlockSpec((2, hw), lambda n: (0, 0))
    out = jax.ShapeDtypeStruct((t, hw), F32)
    states = jax.ShapeDtypeStruct((nc, HEAD, hw), F32)
    return _pallas(
        body, name=name, grid=(nc,),
        in_specs=[col(0, False), col(1, False), col(2, False), col(0, True), col(1, True), col(3, True),
                  lb_spec, lb_spec],
        out_specs=[col(0, False), col(0, True), st(False), st(True)],
        out_shape=[out, out, states, states],
        scratch_shapes=[pltpu.VMEM((2, HEAD, hw), F32)],
        compiler_params=_params(("arbitrary",), 12 * _nbytes((HEAD, hw), F32)),
    )(p, p, p, p, p, p, lbp_f, lbp_b)


def _hgrn_scan_bwd(name, p, lbp_f, lbp_b, do, st_f, st_b):
    t = p.shape[0]
    hw = lbp_f.shape[1]
    nh, nc = hw // HEAD, t // CHUNK

    def body(qf, vf, zf, dof, sf, qb, vb, zb, dob, sb, lbf, lbb, dqf, dvf, dzf, dlbf, dqb, dvb, dzb, dlbb,
             dstate, dlb_acc, dqd_s, dkd_s, dkt_s, ddec_s):
        n = pl.program_id(0)

        @pl.when(n == 0)
        def _():
            dstate[...] = jnp.zeros_like(dstate)
            dlb_acc[...] = jnp.zeros_like(dlb_acc)

        directions = [(qf, vf, zf, dof, sf, lbf, dqf, dvf, dzf, dlbf), (qb, vb, zb, dob, sb, lbb, dqb, dvb, dzb, dlbb)]
        for d, (q_ref, v_ref, z_ref, do_ref, st_ref, lb_ref, dq_ref, dv_ref, dz_ref, dlb_ref) in enumerate(directions):
            mask = _chunk_mask(d == 1)
            mask_bf = mask.astype(BF16)
            lb = _lower_bound(lb_ref[...])
            sig, f, k, cum, last = _hgrn_gates(z_ref[...], lb, mask_bf)
            e_pos, e_neg, e_tail = jnp.exp(cum), jnp.exp(-cum), jnp.exp(last - cum)
            dec = jnp.exp(last)
            v = v_ref[...].astype(BF16)
            qd, kd, kt = q_ref[...] * e_pos, k * e_neg, k * e_tail
            qd_bf, kd_bf, kt_bf = qd.astype(BF16), kd.astype(BF16), kt.astype(BF16)
            s_all = st_ref[...]
            ds_all = dstate[d]
            dov = do_ref[...].astype(BF16)
            for h in range(nh):
                sl = slice(h * HEAD, (h + 1) * HEAD)
                s_in, ds_out = s_all[:, sl], ds_all[:, sl]
                ds_bf = ds_out.astype(BF16)
                a = jnp.where(mask, _dot(qd_bf[:, sl], kd_bf[:, sl], 1, 1), 0.0).astype(BF16)
                da = jnp.where(mask, _dot(dov[:, sl], v[:, sl], 1, 1), 0.0).astype(BF16)
                dv_ref[:, sl] = _dot(a, dov[:, sl], 0, 0) + _dot(kt_bf[:, sl], ds_bf, 1, 1)
                dqd_s[:, sl] = _dot(da, kd_bf[:, sl]) + _dot(dov[:, sl], s_in.astype(BF16))
                dkd_s[:, sl] = _dot(da, qd_bf[:, sl], 0, 0)
                dkt_s[:, sl] = _dot(v[:, sl], ds_bf)
                dstate[d, :, sl] = _dot(dov[:, sl], qd_bf[:, sl], 0, 0) + ds_out * dec[:, sl]
                ddec_s[:, sl] = jnp.sum(ds_out * s_in, axis=0, keepdims=True)
            dqd, dkd, dkt = dqd_s[...], dkd_s[...], dkt_s[...]
            dlast = jnp.sum(dkt * kt, axis=0, keepdims=True) + dec * ddec_s[...]
            dq_ref[...] = dqd * e_pos
            dk = dkd * e_neg + dkt * e_tail
            dcum = dqd * qd - dkd * kd - dkt * kt
            dlogf = _dot_exact(mask_bf, dcum, 0, 0) + dlast
            df = dlogf / f - dk
            dz_ref[...] = df * (1.0 - lb) * sig * (1.0 - sig)
            dlb_acc[d] += jnp.sum(df * (1.0 - sig), axis=0, keepdims=True)

            @pl.when(n == nc - 1)
            def _():
                g = dlb_acc[d] * lb * (1.0 - lb)
                dlb_ref[0:1, :] = g
                dlb_ref[1:2, :] = -g

    def col(group, reverse):
        return pl.BlockSpec((CHUNK, hw), lambda n: (n if reverse else (nc - 1 - n), group))

    def st(reverse):
        return pl.BlockSpec((None, HEAD, hw), lambda n: (n if reverse else (nc - 1 - n), 0, 0))

    lb_spec = pl.BlockSpec((2, hw), lambda n: (0, 0))
    out = jax.ShapeDtypeStruct((t, hw), F32)
    dlb = jax.ShapeDtypeStruct((2, hw), F32)
    wide = pltpu.VMEM((CHUNK, hw), F32)
    return _pallas(
        body, name=name, grid=(nc,),
        in_specs=[col(0, False), col(1, False), col(2, False), col(0, False), st(False),
                  col(0, True), col(1, True), col(3, True), col(0, True), st(True), lb_spec, lb_spec],
        out_specs=[col(0, False), col(0, False), col(0, False), lb_spec,
                   col(0, True), col(0, True), col(0, True), lb_spec],
        out_shape=[out, out, out, dlb, out, out, out, dlb],
        scratch_shapes=[pltpu.VMEM((2, HEAD, hw), F32), pltpu.VMEM((2, 1, hw), F32), wide, wide, wide,
                        pltpu.VMEM((1, hw), F32)],
        compiler_params=_params(("arbitrary",), 16 * _nbytes((HEAD, hw), F32)),
    )(p, p, p, do, st_f, p, p, p, do, st_b, lbp_f, lbp_b)


def _hgrn_out_fwd(name, o_f, o_b, p, gain, g_group):
    t, hw = o_f.shape
    nh = hw // HEAD
    tm = _tile(t, 512, 8)

    def body(of_ref, ob_ref, g_ref, gain_ref, y_ref):
        o = of_ref[...] + ob_ref[...]
        g = g_ref[...]
        y_ref[...] = (o * _rms(o) * gain_ref[...] * (g * _sigmoid(g))).astype(BF16)

    blk = pl.BlockSpec((tm, HEAD), lambda i, h: (i, h))
    return _pallas(
        body, name=name, grid=(t // tm, nh),
        in_specs=[blk, blk, pl.BlockSpec((tm, HEAD), lambda i, h: (i, g_group * nh + h)),
                  pl.BlockSpec((1, HEAD), lambda i, h: (0, h))],
        out_specs=blk, out_shape=jax.ShapeDtypeStruct((t, hw), BF16),
        compiler_params=_params(("parallel", "parallel"), 1 << 20),
    )(o_f, o_b, p, gain)


def _hgrn_out_bwd(name, dy, o_f, o_b, p, gain, g_group, after=None):
    t, hw = o_f.shape
    nh = hw // HEAD
    tm = _tile(t, 512, 8)

    def body(dy_ref, of_ref, ob_ref, g_ref, gain_ref, do_ref, dg_ref, dgain_ref):
        i = pl.program_id(1)
        o = of_ref[...] + ob_ref[...]
        g = g_ref[...]
        gain_v = gain_ref[...]
        sig = _sigmoid(g)
        dyv = dy_ref[...]
        do, dgain = _norm_bwd(dyv * (g * sig), o, gain_v)
        do_ref[...] = do
        dg_ref[...] = dyv * (o * _rms(o) * gain_v) * sig * (1.0 + g * (1.0 - sig))
        _accumulate(dgain_ref, dgain, i == 0)

    blk = pl.BlockSpec((tm, HEAD), lambda h, i: (i, h))
    vec = pl.BlockSpec((1, HEAD), lambda h, i: (0, h))
    out = jax.ShapeDtypeStruct((t, hw), F32)
    body, ins, in_specs = _ordered(
        body, [dy, o_f, o_b, p, gain],
        [blk, blk, blk, pl.BlockSpec((tm, HEAD), lambda h, i: (i, g_group * nh + h)), vec], after)
    return _pallas(
        body, name=name, grid=(nh, t // tm), in_specs=in_specs,
        out_specs=[blk, blk, vec], out_shape=[out, out, jax.ShapeDtypeStruct((1, hw), F32)],
        compiler_params=_params(("parallel", "arbitrary"), 1 << 20),
    )(*ins)


def _t5_bucket_ids():
    c = np.arange(WINDOW)[:, None]
    s = np.arange(SPAN)[None, :]
    rel = s - WINDOW - c
    nb = REL_BUCKETS // 2
    max_exact = nb // 2
    bucket = (rel > 0).astype(np.int32) * nb
    n = np.abs(rel)
    large = max_exact + (np.log(np.maximum(n, 1) / max_exact) / np.log(REL_MAX_DIST / max_exact)
                         * (nb - max_exact)).astype(np.int32)
    large = np.minimum(large, nb - 1)
    ids = bucket + np.where(n < max_exact, n, large).astype(np.int32)
    return jnp.asarray(ids.reshape(1, WINDOW * SPAN), jnp.int32)


def _bias_onehot(ids_ref):
    n = ids_ref.shape[1]
    return (lax.broadcasted_iota(jnp.int32, (REL_BUCKETS, n), 0) == ids_ref[...]).astype(BF16)


def _bias_gather(name, table_t, ids):
    nh = table_t.shape[0]

    def body(t_ref, ids_ref, o_ref):
        o_ref[...] = _dot_exact(t_ref[...], _bias_onehot(ids_ref), split="a")

    return _pallas(
        body, name=name, out_shape=jax.ShapeDtypeStruct((nh, ids.shape[1]), F32),
        compiler_params=pltpu.CompilerParams(vmem_limit_bytes=32 << 20),
    )(table_t, ids)


def _bias_scatter(name, dbias, ids):
    nh = dbias.shape[0]

    def body(d_ref, ids_ref, o_ref):
        o_ref[...] = _dot_exact(d_ref[...], _bias_onehot(ids_ref), 1, 1, split="a")

    return _pallas(
        body, name=name, out_shape=jax.ShapeDtypeStruct((nh, REL_BUCKETS), F32),
        compiler_params=pltpu.CompilerParams(vmem_limit_bytes=32 << 20),
    )(dbias, ids)


def _attn_valid(i, t):
    c = lax.broadcasted_iota(jnp.int32, (WINDOW, SPAN), 0)
    s = lax.broadcasted_iota(jnp.int32, (WINDOW, SPAN), 1)
    rel = s - WINDOW - c
    pos = i * WINDOW - WINDOW + s
    return (jnp.abs(rel) <= WINDOW) & (pos >= 0) & (pos < t)


def _attn_probs(qh, kh, bias_h, sink_h, valid):
    s = _dot(qh, kh, 1, 1) / math.sqrt(HEAD)
    s = jnp.where(valid, s + bias_h, NEG_INF)
    m = jnp.maximum(jnp.max(s, axis=-1, keepdims=True), sink_h)
    e = jnp.exp(s - m)
    es = jnp.exp(sink_h - m)
    inv = 1.0 / (jnp.sum(e, axis=-1, keepdims=True) + es)
    return e * inv, es * inv


def _attn_fwd(name, p, k_pad, v_pad, bias, sink, q_group_blk):
    t = p.shape[0]
    nh = bias.shape[0]
    aw = nh * HEAD
    grp = nh // KV_HEADS
    nb = t // WINDOW

    def body(q_ref, k_ref, v_ref, b_ref, s_ref, y_ref):
        i = pl.program_id(0)
        valid = _attn_valid(i, t)
        start = pl.multiple_of(i * WINDOW, WINDOW)
        ks = k_ref[pl.ds(start, SPAN), :]
        vs = v_ref[pl.ds(start, SPAN), :]
        for h in range(nh):
            kv = h // grp
            qh = q_ref[:, h * HEAD:(h + 1) * HEAD].astype(BF16)
            pr, _ = _attn_probs(qh, ks[:, kv * HEAD:(kv + 1) * HEAD], b_ref[h], s_ref[0:1, h:h + 1], valid)
            y_ref[:, h * HEAD:(h + 1) * HEAD] = _dot(pr.astype(BF16), vs[:, kv * HEAD:(kv + 1) * HEAD]).astype(BF16)

    full = lambda a: pl.BlockSpec(a.shape, lambda i: (0,) * a.ndim)
    return _pallas(
        body, name=name, grid=(nb,),
        in_specs=[pl.BlockSpec((WINDOW, aw), lambda i: (i, q_group_blk)), full(k_pad), full(v_pad), full(bias),
                  full(sink)],
        out_specs=pl.BlockSpec((WINDOW, aw), lambda i: (i, 0)),
        out_shape=jax.ShapeDtypeStruct((t, aw), BF16),
        compiler_params=_params(("parallel",), _nbytes(k_pad.shape, BF16) * 2 + _nbytes(bias.shape, F32)),
    )(p, k_pad, v_pad, bias, sink)


def _attn_bwd(name, p, k_pad, v_pad, bias, sink, dy, q_group_blk, dy_blk, after=None):
    t = p.shape[0]
    nh = bias.shape[0]
    aw = nh * HEAD
    grp = nh // KV_HEADS
    nb = t // WINDOW
    kvw = k_pad.shape[1]

    def body(q_ref, k_ref, v_ref, b_ref, s_ref, dy_ref, dq_ref, dk_ref, dv_ref, db_ref, ds_ref):
        i = pl.program_id(0)

        @pl.when(i == 0)
        def _():
            dk_ref[...] = jnp.zeros_like(dk_ref)
            dv_ref[...] = jnp.zeros_like(dv_ref)
            db_ref[...] = jnp.zeros_like(db_ref)
            ds_ref[...] = jnp.zeros_like(ds_ref)

        valid = _attn_valid(i, t)
        start = pl.multiple_of(i * WINDOW, WINDOW)
        ks = k_ref[pl.ds(start, SPAN), :]
        vs = v_ref[pl.ds(start, SPAN), :]
        inv_sqrt = 1.0 / math.sqrt(HEAD)
        for kv in range(KV_HEADS):
            kh = ks[:, kv * HEAD:(kv + 1) * HEAD]
            vh = vs[:, kv * HEAD:(kv + 1) * HEAD]
            dk_acc = jnp.zeros((SPAN, HEAD), F32)
            dv_acc = jnp.zeros((SPAN, HEAD), F32)
            for h in range(kv * grp, (kv + 1) * grp):
                qh = q_ref[:, h * HEAD:(h + 1) * HEAD].astype(BF16)
                pr, ps = _attn_probs(qh, kh, b_ref[h], s_ref[0:1, h:h + 1], valid)
                doh = dy_ref[:, h * HEAD:(h + 1) * HEAD].astype(BF16)
                dp = _dot(doh, vh, 1, 1)
                delta = jnp.sum(pr * dp, axis=-1, keepdims=True)
                dsc = pr * (dp - delta)
                db_ref[h] += dsc
                ds_ref[h:h + 1, :] += jnp.broadcast_to(jnp.sum(-ps * delta, axis=0, keepdims=True), (1, 128))
                dsr = (dsc * inv_sqrt).astype(BF16)
                dq_ref[:, h * HEAD:(h + 1) * HEAD] = _dot(dsr, kh)
                dk_acc += _dot(dsr, qh, 0, 0)
                dv_acc += _dot(pr.astype(BF16), doh, 0, 0)
            dk_ref[pl.ds(start, SPAN), kv * HEAD:(kv + 1) * HEAD] += dk_acc
            dv_ref[pl.ds(start, SPAN), kv * HEAD:(kv + 1) * HEAD] += dv_acc

    full = lambda a: pl.BlockSpec(a.shape, lambda i: (0,) * a.ndim)
    whole = lambda shape: pl.BlockSpec(shape, lambda i: (0,) * len(shape))
    pad_shape = (t + 2 * WINDOW, kvw)
    body, ins, in_specs = _ordered(
        body, [p, k_pad, v_pad, bias, sink, dy],
        [pl.BlockSpec((WINDOW, aw), lambda i: (i, q_group_blk)), full(k_pad), full(v_pad), full(bias), full(sink),
         pl.BlockSpec((WINDOW, aw), lambda i: (i, dy_blk))], after)
    return _pallas(
        body, name=name, grid=(nb,), in_specs=in_specs,
        out_specs=[pl.BlockSpec((WINDOW, aw), lambda i: (i, 0)), whole(pad_shape), whole(pad_shape),
                   whole(bias.shape), whole((nh, 128))],
        out_shape=[jax.ShapeDtypeStruct((t, aw), F32), jax.ShapeDtypeStruct(pad_shape, F32),
                   jax.ShapeDtypeStruct(pad_shape, F32), jax.ShapeDtypeStruct(bias.shape, F32),
                   jax.ShapeDtypeStruct((nh, 128), F32)],
        compiler_params=_params(("arbitrary",), 3 * _nbytes(pad_shape, F32) + 2 * _nbytes(bias.shape, F32)),
    )(*ins)


def _pad_kv(name, p, kv_blk, kvw):
    t = p.shape[0]
    nb = t // WINDOW

    def body(x_ref, o_ref):
        i = pl.program_id(0)
        inside = jnp.logical_and(i >= 1, i <= nb)
        o_ref[...] = jnp.where(inside, x_ref[...], 0.0).astype(BF16)

    return _pallas(
        body, name=name, grid=(nb + 2,),
        in_specs=[pl.BlockSpec((WINDOW, kvw), lambda i: (jnp.clip(i - 1, 0, nb - 1), kv_blk))],
        out_specs=pl.BlockSpec((WINDOW, kvw), lambda i: (i, 0)),
        out_shape=jax.ShapeDtypeStruct((t + 2 * WINDOW, kvw), BF16),
        compiler_params=_params(("parallel",), 1 << 20),
    )(p)


def _mix_dproj(name, pieces, kv_pads, t, after=None):
    hw = pieces[0][0].shape[1]
    kvw = kv_pads[0].shape[1]
    widths = [hw] * len(pieces) + [kvw] * len(kv_pads)
    total = sum(widths)
    tm = WINDOW
    flat = [a for pc in pieces for a in pc]

    def body(*refs):
        o_ref = refs[-1]
        pos, off = 0, 0
        for pc in pieces:
            val = refs[pos][...]
            for extra in range(1, len(pc)):
                val = val + refs[pos + extra][...]
            o_ref[:, off:off + hw] = val.astype(BF16)
            pos += len(pc)
            off += hw
        for _ in kv_pads:
            o_ref[:, off:off + kvw] = refs[pos][...].astype(BF16)
            pos += 1
            off += kvw

    in_specs = [pl.BlockSpec((tm, hw), lambda i: (i, 0)) for _ in flat]
    in_specs += [pl.BlockSpec((tm, kvw), lambda i: (i + 1, 0)) for _ in kv_pads]
    body, ins, in_specs = _ordered(body, [*flat, *kv_pads], in_specs, after)
    return _pallas(
        body, name=name, grid=(t // tm,), in_specs=in_specs,
        out_specs=pl.BlockSpec((tm, total), lambda i: (i, 0)),
        out_shape=jax.ShapeDtypeStruct((t, total), BF16),
        compiler_params=_params(("parallel",), 3 * _nbytes((tm, total), F32)),
    )(*ins)


def _concat_cols(name, a, b):
    t, wa = a.shape
    wb = b.shape[1]
    tm = _tile(t, 512, 16)

    def body(a_ref, b_ref, o_ref):
        o_ref[:, :wa] = a_ref[...]
        o_ref[:, wa:] = b_ref[...]

    return _pallas(
        body, name=name, grid=(t // tm,),
        in_specs=[pl.BlockSpec((tm, wa), lambda i: (i, 0)), pl.BlockSpec((tm, wb), lambda i: (i, 0))],
        out_specs=pl.BlockSpec((tm, wa + wb), lambda i: (i, 0)),
        out_shape=jax.ShapeDtypeStruct((t, wa + wb), a.dtype),
        compiler_params=_params(("parallel",), 2 * _nbytes((tm, wa + wb), a.dtype)),
    )(a, b)


def _cast_into_full(name, w, geom, idx, after=None):
    r, c = w.shape
    tr = _tile(r, 256, 16)
    nr = r // tr
    if geom.col:
        place = lambda i, iref: (i, iref[0])
    else:
        place = lambda i, iref: (iref[0] * nr + i, 0)

    def body(i_ref, w_ref, *rest):
        rest[-1][...] = w_ref[...].astype(BF16)

    in_specs = [pl.BlockSpec((tr, c), lambda i, iref: (i, 0))]
    ins = [w]
    if after is not None:
        in_specs.append(pl.BlockSpec(memory_space=pl.ANY))
        ins.append(after)
    return _pallas(
        body, name=name,
        grid_spec=pltpu.PrefetchScalarGridSpec(
            num_scalar_prefetch=1, grid=(nr,), in_specs=in_specs, out_specs=pl.BlockSpec((tr, c), place)),
        out_shape=pltpu.HBM(geom.full_shape, BF16),
        compiler_params=_params(("parallel",), 2 * _nbytes((tr, c), F32)),
    )(idx, *ins)


def _adamw(name, w, g, m, v):
    r, c = w.shape
    tr = _tile(r, 128, 8)
    bc1 = 1.0 - ADAM_B1 ** ADAM_STEP
    bc2 = 1.0 - ADAM_B2 ** ADAM_STEP

    def body(w_ref, g_ref, m_ref, v_ref, go_ref, d_ref, nm_ref, nv_ref):
        gv = g_ref[...]
        go_ref[...] = gv
        nm = ADAM_B1 * m_ref[...] + (1.0 - ADAM_B1) * gv
        nv = ADAM_B2 * v_ref[...] + (1.0 - ADAM_B2) * (gv * gv)
        nm_ref[...] = nm
        nv_ref[...] = nv
        d_ref[...] = -ADAM_LR * ((nm / bc1) / (jnp.sqrt(nv / bc2) + ADAM_EPS) + ADAM_WD * w_ref[...])

    blk = pl.BlockSpec((tr, c), lambda i: (i, 0))
    out = jax.ShapeDtypeStruct((r, c), F32)
    return _pallas(
        body, name=name, grid=(r // tr,), in_specs=[blk] * 4, out_specs=[blk] * 4, out_shape=[out] * 4,
        compiler_params=_params(("parallel",), 8 * _nbytes((tr, c), F32)),
    )(w, g, m, v)


def _mesh_pos():
    return lax.axis_index("x"), lax.axis_index("y"), lax.axis_index("c")


def _other_chips(x, y):
    return [(1 - x, y), (x, 1 - y), (1 - x, 1 - y)]


class _Big:
    def __init__(self, shard_shape, col_sharded):
        self.col = col_sharded
        r, c = shard_shape
        self.shard_shape = (r, c)
        self.full_shape = (r, N_CHIPS * c) if col_sharded else (N_CHIPS * r, c)
        self.half_shape = (r // 2, N_CHIPS * c) if col_sharded else (N_CHIPS * r, c // 2)
        self.shard_half_shape = (r // 2, c) if col_sharded else (r, c // 2)

    def region(self, ref, s, half=None):
        r, c = self.shard_shape
        if self.col:
            rows = slice(None) if half is None else pl.ds(half * (r // 2), r // 2)
            return ref.at[rows, pl.ds(s * c, c)]
        cols = slice(None) if half is None else pl.ds(half * (c // 2), c // 2)
        return ref.at[pl.ds(s * r, r), cols]

    def three_halves(self, ref, half):
        r, c = self.shard_shape
        if self.col:
            return ref.at[pl.ds(half * (r // 2), r // 2), pl.ds(0, 3 * c)]
        return ref.at[pl.ds(0, 3 * r), pl.ds(half * (c // 2), c // 2)]

    def half_of_full(self, ref, half):
        r, c = self.full_shape
        if self.col:
            return ref.at[pl.ds(half * (r // 2), r // 2), :]
        return ref.at[:, pl.ds(half * (c // 2), c // 2)]

    def half_of_shard(self, ref, half):
        r, c = self.shard_shape
        if self.col:
            return ref.at[pl.ds(half * (r // 2), r // 2), :]
        return ref.at[:, pl.ds(half * (c // 2), c // 2)]

    def shard_of_half(self, ref, s):
        r, c = self.shard_shape
        if self.col:
            return ref.at[:, pl.ds(s * c, c)]
        return ref.at[pl.ds(s * r, r), :]


HBM =pl.BlockSpec(memory_space=pltpu.HBM)
SEM = pl.BlockSpec(memory_space=pltpu.SEMAPHORE)
SPLIT_COPY = pltpu.CompilerParams(has_side_effects=pltpu.SideEffectType.DATAFLOW_SIDE_EFFECTING)


def _in_hbm(a):
    return pltpu.with_memory_space_constraint(a, pltpu.HBM)


def _gather_start(name, fulls, geoms):
    nw = len(fulls)

    def body(*refs):
        dst = refs[nw:2 * nw]
        sems = refs[2 * nw:-1]
        x, y, c = _mesh_pos()
        mine = 2 * x + y
        for w in range(nw):
            own_half = geoms[w].region(dst[w], mine, c)
            for chip in _other_chips(x, y):
                pltpu.make_async_remote_copy(src_ref=own_half, dst_ref=own_half, send_sem=sems[2 * w],
                                             recv_sem=sems[2 * w + 1], device_id=(*chip, c),
                                             device_id_type=MESH).start()
        refs[-1][...] = jnp.zeros_like(refs[-1])

    out = _pallas(
        body, name=name, in_specs=[HBM] * nw,
        out_specs=[HBM] * nw + [SEM] * (2 * nw) + [pl.BlockSpec(memory_space=pltpu.VMEM)],
        out_shape=[pltpu.HBM(g.full_shape, BF16) for g in geoms] + [pltpu.SemaphoreType.DMA(())] * (2 * nw)
        + [jax.ShapeDtypeStruct((8, 128), F32)],
        input_output_aliases={w: w for w in range(nw)}, compiler_params=SPLIT_COPY,
    )(*[_in_hbm(a) for a in fulls])
    return out[:nw], [(out[nw + 2 * w], out[nw + 2 * w + 1]) for w in range(nw)], out[-1]


def _wait_three(geom, ref, half, send_sem, recv_sem, peer, recv):
    three = geom.three_halves(ref, half)
    copy = pltpu.make_async_remote_copy(src_ref=three, dst_ref=three, send_sem=send_sem, recv_sem=recv_sem,
                                        device_id=peer, device_id_type=MESH)
    if recv:
        copy.wait_recv()
    else:
        copy.wait_send()


def _gather_forward(name, full, geom, sems, after):
    def body(w_in, send_sem, recv_sem, after_ref, w_ref, fwd_send, fwd_recv):
        x, y, c = _mesh_pos()
        sibling = (x, y, 1 - c)
        _wait_three(geom, w_ref, c, send_sem, recv_sem, sibling, recv=True)
        for chip in _other_chips(x, y):
            landed = geom.region(w_ref, 2 * chip[0] + chip[1], c)
            pltpu.make_async_remote_copy(src_ref=landed, dst_ref=landed, send_sem=fwd_send, recv_sem=fwd_recv,
                                         device_id=sibling, device_id_type=MESH).start()
        _wait_three(geom, w_ref, c, send_sem, recv_sem, sibling, recv=False)

    sem = pltpu.SemaphoreType.DMA(())
    out = _pallas(
        body, name=name, in_specs=[HBM, SEM, SEM, pl.BlockSpec(memory_space=pl.ANY)], out_specs=[HBM, SEM, SEM],
        out_shape=[pltpu.HBM(geom.full_shape, BF16), sem, sem],
        input_output_aliases={0: 0}, compiler_params=SPLIT_COPY,
    )(full, sems[0], sems[1], after)
    return out[0], (out[1], out[2])


def _gather_end(name, full, geom, sems, after):
    def body(w_in, fwd_send, fwd_recv, after_ref, w_ref):
        x, y, c = _mesh_pos()
        sibling = (x, y, 1 - c)
        _wait_three(geom, w_ref, 1 - c, fwd_send, fwd_recv, sibling, recv=True)
        _wait_three(geom, w_ref, c, fwd_send, fwd_recv, sibling, recv=False)

    return _pallas(
        body, name=name, in_specs=[HBM, SEM, SEM, pl.BlockSpec(memory_space=pl.ANY)], out_specs=HBM,
        out_shape=pltpu.HBM(geom.full_shape, BF16),
        input_output_aliases={0: 0}, compiler_params=SPLIT_COPY,
    )(full, sems[0], sems[1], after)


def _split_copy_call(name, arrays, fn, sems=(), after=None, new_sems=0):
    n, ns = len(arrays), len(sems)
    n_in = n + ns + (after is not None)

    def body(*refs):
        fn(refs[n_in:n_in + n], refs[n:n + ns], refs[n_in + n:-1])
        refs[-1][...] = jnp.zeros_like(refs[-1])

    ins = list(arrays) if ns else [_in_hbm(a) for a in arrays]
    ins += list(sems) + ([after] if after is not None else [])
    in_specs = [HBM] * n + [SEM] * ns + ([pl.BlockSpec(memory_space=pl.ANY)] if after is not None else [])
    out = _pallas(
        body, name=name, in_specs=in_specs,
        out_specs=[HBM] * n + [SEM] * new_sems + [pl.BlockSpec(memory_space=pltpu.VMEM)],
        out_shape=[pltpu.HBM(a.shape, a.dtype) for a in arrays] + [pltpu.SemaphoreType.DMA(())] * new_sems
        + [jax.ShapeDtypeStruct((8, 128), F32)],
        input_output_aliases={i: i for i in range(n)}, compiler_params=SPLIT_COPY,
    )(*ins)
    return list(out[:n]), tuple(out[n:-1]), out[-1]


def _remote(src, dst, sems, to):
    return pltpu.make_async_remote_copy(src_ref=src, dst_ref=dst, send_sem=sems[0], recv_sem=sems[1],
                                        device_id=to, device_id_type=MESH)


class _GradReduce:
    def __init__(self, name, geom, idx, c_idx):
        self.name, self.geom, self.idx, self.c_idx = name, geom, idx, c_idx

    def pair_start(self, dw):
        g = self.geom

        def start(refs, _, new):
            x, y, c = _mesh_pos()
            _remote(g.half_of_full(refs[0], 1 - c), refs[1], new, (x, y, 1 - c)).start()

        self.arrays, self.sems, token = _split_copy_call(
            f"pair_start_{self.name}", [dw, lax.empty(g.half_shape, BF16)], start, new_sems=2)
        return token

    def pair_finish(self, after):
        g = self.geom

        def wait(refs, sems, _):
            x, y, c = _mesh_pos()
            copy = _remote(g.half_of_full(refs[0], 1 - c), refs[1], sems, (x, y, 1 - c))
            copy.wait_send()
            copy.wait_recv()

        (dw, landed), _, _ = _split_copy_call(f"pair_wait_{self.name}", self.arrays, wait, self.sems, after)
        half = _pair_add(f"pair_add_{self.name}", dw, landed, g, self.c_idx)

        def start(refs, _, new):
            x, y, c = _mesh_pos()
            for k, chip in enumerate(_other_chips(x, y)):
                _remote(g.shard_of_half(refs[0], 2 * chip[0] + chip[1]), refs[1].at[k], new, (*chip, c)).start()

        self.arrays, self.sems, token = _split_copy_call(
            f"chip_start_{self.name}", [half, lax.empty((3,) + g.shard_half_shape, BF16)], start, new_sems=2)
        return token

    def chip_finish(self, after):
        g = self.geom

        def wait(refs, sems, _):
            x, y, c = _mesh_pos()
            three = _remote(refs[1], refs[1], sems, (x, y, 1 - c))
            three.wait_send()
            three.wait_recv()

        (half, landed), _, _ = _split_copy_call(f"chip_wait_{self.name}", self.arrays, wait, self.sems, after)
        quarter = _chip_add(f"chip_add_{self.name}", half, landed, g, self.idx)

        def start(refs, _, new):
            x, y, c = _mesh_pos()
            own = g.half_of_shard(refs[0], c)
            _remote(own, own, new, (x, y, 1 - c)).start()

        self.arrays, self.sems, token = _split_copy_call(f"share_start_{self.name}", [quarter], start, new_sems=2)
        return token

    def finish(self, after):
        g = self.geom

        def wait(refs, sems, _):
            x, y, c = _mesh_pos()
            own, theirs = g.half_of_shard(refs[0], c), g.half_of_shard(refs[0], 1 - c)
            _remote(own, own, sems, (x, y, 1 - c)).wait_send()
            _remote(theirs, theirs, sems, (x, y, 1 - c)).wait_recv()

        (quarter,), _, _ = _split_copy_call(f"share_wait_{self.name}", self.arrays, wait, self.sems, after)
        return quarter


def _pair_add(name, grad, recv, geom, c_idx):
    r, c = geom.half_shape
    tr, tc = _tile(r, 256, 16), _tile(c, 2048, 128)
    nr, ncol = r // tr, c // tc
    if geom.col:
        mine = lambda i, j, cref: (cref[0] * nr + i, j)
    else:
        mine = lambda i, j, cref: (i, cref[0] * ncol + j)

    def body(c_ref, g_ref, r_ref, o_ref):
        o_ref[...] = (g_ref[...].astype(F32) + r_ref[...].astype(F32)).astype(BF16)

    return _pallas(
        body, name=name,
        grid_spec=pltpu.PrefetchScalarGridSpec(
            num_scalar_prefetch=1, grid=(nr, ncol),
            in_specs=[pl.BlockSpec((tr, tc), mine), pl.BlockSpec((tr, tc), lambda i, j, cref: (i, j))],
            out_specs=pl.BlockSpec((tr, tc), lambda i, j, cref: (i, j))),
        out_shape=jax.ShapeDtypeStruct((r, c), BF16),
        compiler_params=_params(("parallel", "parallel"), 3 * _nbytes((tr, tc), F32)),
    )(c_idx, grad, recv)


def _chip_add(name, half, recv, geom, idx):
    r, c = geom.shard_half_shape
    tr, tc = _tile(r, 256, 16), _tile(c, 2048, 128)
    nr, ncol = r // tr, c // tc
    if geom.col:
        mine = lambda i, j, iref: (i, iref[0] * ncol + j)
        place = lambda i, j, iref: (iref[1] * nr + i, j)
    else:
        mine = lambda i, j, iref: (iref[0] * nr + i, j)
        place = lambda i, j, iref: (i, iref[1] * ncol + j)

    def body(i_ref, h_ref, r_ref, o_ref):
        acc = h_ref[...].astype(F32)
        for k in range(3):
            acc = acc + r_ref[k].astype(F32)
        o_ref[...] = acc

    return _pallas(
        body, name=name,
        grid_spec=pltpu.PrefetchScalarGridSpec(
            num_scalar_prefetch=1, grid=(nr, ncol),
            in_specs=[pl.BlockSpec((tr, tc), mine), pl.BlockSpec((3, tr, tc), lambda i, j, iref: (0, i, j))],
            out_specs=pl.BlockSpec((tr, tc), place)),
        out_shape=jax.ShapeDtypeStruct(geom.shard_shape, F32),
        compiler_params=_params(("parallel", "parallel"), 4 * _nbytes((tr, tc), F32)),
    )(idx, half, recv)


def _all_reduce_small(pack, after=None):
    r, d = pack.shape

    def body(p_ref, o_ref, slots, send_sems, recv_sems):
        x, y, c = _mesh_pos()
        me = 4 * x + 2 * y + c
        slots[me] = p_ref[...]
        copies = []
        for k in range(1, N_DEV):
            px, py, pc = x ^ ((k >> 2) & 1), y ^ ((k >> 1) & 1), c ^ (k & 1)
            copies.append(pltpu.make_async_remote_copy(
                src_ref=p_ref, dst_ref=slots.at[me], send_sem=send_sems.at[k - 1], recv_sem=recv_sems.at[k - 1],
                device_id=(px, py, pc), device_id_type=MESH))
        for cp in copies:
            cp.start()
        for k in range(1, N_DEV):
            peer = 4 * (x ^ ((k >> 2) & 1)) + 2 * (y ^ ((k >> 1) & 1)) + (c ^ (k & 1))
            pltpu.make_async_remote_copy(
                src_ref=p_ref, dst_ref=slots.at[peer], send_sem=send_sems.at[k - 1], recv_sem=recv_sems.at[k - 1],
                device_id=(x, y, c), device_id_type=MESH).wait_recv()
        for cp in copies:
            cp.wait_send()
        acc = slots[0]
        for k in range(1, N_DEV):
            acc = acc + slots[k]
        o_ref[...] = acc

    vm = pl.BlockSpec(memory_space=pltpu.VMEM)
    body, ins, in_specs = _ordered(body, [pack], [vm], after)
    return _pallas(
        body, name="all_reduce_small", in_specs=in_specs, out_specs=vm,
        out_shape=jax.ShapeDtypeStruct((r, d), F32),
        scratch_shapes=[pltpu.VMEM((N_DEV, r, d), F32), pltpu.SemaphoreType.DMA((N_DEV - 1,)),
                        pltpu.SemaphoreType.DMA((N_DEV - 1,))],
    )(*ins)


def _pack_rows(rows, d):
    out = []
    for a in rows:
        flat = a.reshape(-1)
        n = -(-flat.shape[0] // d) * d
        out.append(jnp.pad(flat, (0, n - flat.shape[0])).reshape(-1, d))
    packed = jnp.concatenate(out, axis=0)
    return jnp.pad(packed, ((0, 16 - packed.shape[0]), (0, 0)))


def _unpack_rows(packed, shapes, d):
    out, row = [], 0
    for shp in shapes:
        n = int(np.prod(shp))
        nrows = -(-n // d)
        out.append(packed[row:row + nrows].reshape(-1)[:n].reshape(shp))
        row += nrows
    return out


def kernel(x, pre_norm_ffn1, post_norm_ffn1, w_ffn1_gate_up, w_ffn1_down, pre_norm_mix, post_norm_mix, w_mix_in, hgrn_lower_bounds_fwd, hgrn_lower_bounds_bwd, hgrn_out_norm, attn_sink, w_mix_out, pre_norm_ffn2, post_norm_ffn2, w_ffn2_gate_up, w_ffn2_down, rel_bias_table, loss_target, m_pre_norm_ffn1, m_post_norm_ffn1, m_w_ffn1_gate_up, m_w_ffn1_down, m_pre_norm_mix, m_post_norm_mix, m_w_mix_in, m_hgrn_lower_bounds_fwd, m_hgrn_lower_bounds_bwd, m_hgrn_out_norm, m_attn_sink, m_w_mix_out, m_pre_norm_ffn2, m_post_norm_ffn2, m_w_ffn2_gate_up, m_w_ffn2_down, m_rel_bias_table, v_pre_norm_ffn1, v_post_norm_ffn1, v_w_ffn1_gate_up, v_w_ffn1_down, v_pre_norm_mix, v_post_norm_mix, v_w_mix_in, v_hgrn_lower_bounds_fwd, v_hgrn_lower_bounds_bwd, v_hgrn_out_norm, v_attn_sink, v_w_mix_out, v_pre_norm_ffn2, v_post_norm_ffn2, v_w_ffn2_gate_up, v_w_ffn2_down, v_rel_bias_table):
    t, d = x.shape[1], x.shape[2]
    hw = hgrn_out_norm.shape[1]
    aw = d - hw
    nah = aw // HEAD
    kvw = KV_HEADS * HEAD
    x0 = x[0]
    target = loss_target[0]

    big_names = ["w_ffn1_gate_up", "w_ffn1_down", "w_mix_in", "w_mix_out", "w_ffn2_gate_up", "w_ffn2_down"]
    big_w = [w_ffn1_gate_up[0], w_ffn1_down[0], w_mix_in[0], w_mix_out[0], w_ffn2_gate_up[0], w_ffn2_down[0]]
    big_m = [m_w_ffn1_gate_up[0], m_w_ffn1_down[0], m_w_mix_in[0], m_w_mix_out[0], m_w_ffn2_gate_up[0],
             m_w_ffn2_down[0]]
    big_v = [v_w_ffn1_gate_up[0], v_w_ffn1_down[0], v_w_mix_in[0], v_w_mix_out[0], v_w_ffn2_gate_up[0],
             v_w_ffn2_down[0]]
    col_sharded = [True, False, True, False, True, False]
    geoms = [_Big(w.shape, cs) for w, cs in zip(big_w, col_sharded)]

    cx, cy, cc = _mesh_pos()
    idx = jnp.stack([2 * cx + cy, cc]).astype(jnp.int32)
    c_idx = jnp.reshape(cc, (1,)).astype(jnp.int32)
    first = _cast_into_full(f"cast_{big_names[0]}", big_w[0], geoms[0], idx)
    started, gather_sems, tok = _gather_start("gather_start_first", [first], geoms[:1])
    rest = [_cast_into_full(f"cast_{n}", w, gm, idx, after=tok)
            for n, w, gm in zip(big_names[1:], big_w[1:], geoms[1:])]
    started_rest, sems_rest, _ = _gather_start("gather_start_rest", rest, geoms[1:])
    started, gather_sems = list(started) + list(started_rest), gather_sems + sems_rest

    def forward_weight(w, after):
        return _gather_forward(f"gather_forward_{big_names[w]}", started[w], geoms[w], gather_sems[w], after)

    def whole_weight(w, forwarded, after):
        return _gather_end(f"gather_end_{big_names[w]}", forwarded[0], geoms[w], forwarded[1], after)

    h1 = _norm_fwd("ffn1_pre_norm", x0, pre_norm_ffn1)
    w_gu1 = whole_weight(0, forward_weight(0, h1), h1)
    gate1, up1, act1 = _ffn_gate_up_act("ffn1_gate_up", h1, w_gu1)
    w_d1 = whole_weight(1, forward_weight(1, act1), act1)
    ff1 = _mm("ffn1_down", act1, w_d1, "nn", F32)
    fw = forward_weight(2, ff1)
    x1, hm = _resid_norm_fwd("ffn1_residual", x0, ff1, post_norm_ffn1, pre_norm_mix, 0.5)
    w_in = whole_weight(2, fw, hm)
    p = _mm("mix_in", hm, w_in, "nn", F32)
    fw = forward_weight(3, p)
    o_f, o_b, st_f, st_b = _hgrn_scan_fwd("hgrn_scan", p, hgrn_lower_bounds_fwd, hgrn_lower_bounds_bwd)
    y_h = _hgrn_out_fwd("hgrn_out", o_f, o_b, p, hgrn_out_norm, 4)
    kv_blk0 = (5 * hw + aw) // kvw
    k_pad = _pad_kv("attn_pad_k", p, kv_blk0, kvw)
    v_pad = _pad_kv("attn_pad_v", p, kv_blk0 + 1, kvw)
    bucket_ids = _t5_bucket_ids()
    bias = _bias_gather("attn_bias", rel_bias_table.T, bucket_ids).reshape(nah, WINDOW, SPAN)
    y_a = _attn_fwd("attn_fwd", p, k_pad, v_pad, bias, attn_sink, 5 * hw // aw)
    y_mix = _concat_cols("mix_concat", y_h, y_a)
    w_out = whole_weight(3, fw, y_mix)
    mixed = _mm("mix_out", y_mix, w_out, "nn", F32)
    fw = forward_weight(4, mixed)
    x2, h2 = _resid_norm_fwd("mix_residual", x1, mixed, post_norm_mix, pre_norm_ffn2, 1.0)
    w_gu2 = whole_weight(4, fw, h2)
    gate2, up2, act2 = _ffn_gate_up_act("ffn2_gate_up", h2, w_gu2)
    w_d2 = whole_weight(5, forward_weight(5, act2), act2)
    ff2 = _mm("ffn2_down", act2, w_d2, "nn", F32)
    loss_blk, dy, dff2, dg_post2 = _final_fwd_bwd("ffn2_residual_loss", x2, ff2, post_norm_ffn2, target, 0.5)

    reduce = [_GradReduce(n, gm, idx, c_idx) for n, gm in zip(big_names, geoms)]
    big_grads, big_delta, big_new_m, big_new_v = [None] * 6, [None] * 6, [None] * 6, [None] * 6

    def update(w, after):
        g, dl, nm, nv = _adamw(f"adamw_{big_names[w]}", big_w[w], reduce[w].finish(after), big_m[w], big_v[w])
        big_grads[w], big_delta[w], big_new_m[w], big_new_v[w] = g[None], dl[None], nm[None], nv[None]
        return dl

    dw_d2 = _mm("ffn2_dw_down", act2, dff2, "tn", BF16)
    tok = reduce[5].pair_start(dw_d2)
    dgu2 = _ffn_dact("ffn2_dact", dff2, w_d2, gate2, up2, after=tok)
    tok = reduce[5].pair_finish(dgu2)
    dw_gu2 = _ffn_dw_gate_up("ffn2_dw_gate_up", h2, dgu2, after=tok)
    tok = reduce[4].pair_start(dw_gu2)
    dh2 = _ffn_dh("ffn2_dh", dgu2, w_gu2, after=tok)
    tok = reduce[4].pair_finish(dh2)
    dx2, dg_pre2, dmixed, dg_postm = _norms_bwd("mix_residual_bwd", dy, dh2, x2, pre_norm_ffn2,
                                                post=(mixed, post_norm_mix, 1.0), after=tok)
    dw_out = _mm("mix_out_dw", y_mix, dmixed, "tn", BF16)
    tok = reduce[3].pair_start(dw_out)
    dy_mix = _mm("mix_out_dx", dmixed, w_out, "nt", F32, after=tok)
    tok = reduce[3].pair_finish(dy_mix)
    dq_a, dk_pad, dv_pad, dbias, dsink = _attn_bwd("attn_bwd", p, k_pad, v_pad, bias, attn_sink, dy_mix,
                                                   5 * hw // aw, hw // aw, after=tok)
    tok = reduce[5].chip_finish(dq_a)
    drel_t = _bias_scatter("attn_dbias", dbias.reshape(nah, WINDOW * SPAN), bucket_ids)
    do, dg_h, dgain = _hgrn_out_bwd("hgrn_out_bwd", dy_mix, o_f, o_b, p, hgrn_out_norm, 4, after=tok)
    dq_f, dv_f, dz_f, dlb_f, dq_b, dv_b, dz_b, dlb_b = _hgrn_scan_bwd(
        "hgrn_scan_bwd", p, hgrn_lower_bounds_fwd, hgrn_lower_bounds_bwd, do, st_f, st_b)
    tok = reduce[4].chip_finish(dq_f)
    tok = reduce[3].chip_finish(tok)
    dp = _mix_dproj("mix_dproj", [(dq_f, dq_b), (dv_f, dv_b), (dz_f,), (dz_b,), (dg_h,), (dq_a,)],
                    [dk_pad, dv_pad], t, after=tok)
    dw_in = _mm("mix_in_dw", hm, dp, "tn", BF16)
    tok = reduce[2].pair_start(dw_in)
    dhm = _mm("mix_in_dx", dp, w_in, "nt", F32, after=tok)
    tok = reduce[2].pair_finish(dhm)
    dx1, dg_prem, dff1, dg_post1 = _norms_bwd("ffn1_residual_bwd", dx2, dhm, x1, pre_norm_mix,
                                              post=(ff1, post_norm_ffn1, 0.5), after=tok)
    dw_d1 = _mm("ffn1_dw_down", act1, dff1, "tn", BF16)
    tok = reduce[1].pair_start(dw_d1)
    dgu1 = _ffn_dact("ffn1_dact", dff1, w_d1, gate1, up1, after=tok)
    tok = reduce[1].pair_finish(dgu1)
    tok = reduce[2].chip_finish(tok)
    dw_gu1 = _ffn_dw_gate_up("ffn1_dw_gate_up", h1, dgu1, after=tok)
    tok = reduce[0].pair_start(dw_gu1)
    done = update(2, tok)
    tok = reduce[0].pair_finish(done)
    dh1 = _ffn_dh("ffn1_dh", dgu1, w_gu1, after=tok)
    grad_x, dg_pre1 = _norms_bwd("ffn1_pre_norm_bwd", dx1, dh1, x0, pre_norm_ffn1)

    small_w = [pre_norm_ffn1, post_norm_ffn1, pre_norm_mix, post_norm_mix, hgrn_lower_bounds_fwd,
               hgrn_lower_bounds_bwd, hgrn_out_norm, attn_sink, pre_norm_ffn2, post_norm_ffn2, rel_bias_table]
    small_m = [m_pre_norm_ffn1, m_post_norm_ffn1, m_pre_norm_mix, m_post_norm_mix, m_hgrn_lower_bounds_fwd,
               m_hgrn_lower_bounds_bwd, m_hgrn_out_norm, m_attn_sink, m_pre_norm_ffn2, m_post_norm_ffn2,
               m_rel_bias_table]
    small_v = [v_pre_norm_ffn1, v_post_norm_ffn1, v_pre_norm_mix, v_post_norm_mix, v_hgrn_lower_bounds_fwd,
               v_hgrn_lower_bounds_bwd, v_hgrn_out_norm, v_attn_sink, v_pre_norm_ffn2, v_post_norm_ffn2,
               v_rel_bias_table]
    small_g = [dg_pre1, dg_post1, dg_prem, dg_postm, dlb_f, dlb_b, dgain, dsink[:, 0].reshape(1, nah), dg_pre2,
               dg_post2, drel_t.T]
    shapes = [a.shape for a in small_w]
    done = update(5, grad_x)
    done = update(4, done)
    done = update(3, done)
    summed = _all_reduce_small(_pack_rows(small_g + [loss_blk[0:1, 0:1]], d), after=done)
    loss = _unpack_rows(summed, shapes + [(1, 1)], d)[-1][0, 0]
    _, sd, sm, sv = _adamw("adamw_small", _pack_rows(small_w, d), summed, _pack_rows(small_m, d),
                           _pack_rows(small_v, d))
    small_grads = _unpack_rows(summed, shapes, d)
    small_delta, small_new_m, small_new_v = (_unpack_rows(a, shapes, d) for a in (sd, sm, sv))

    tok = reduce[1].chip_finish(sd)
    done = update(1, tok)
    tok = reduce[0].chip_finish(done)
    update(0, tok)

    def ordered(small, big):
        s = dict(zip(["pre1", "post1", "prem", "postm", "lbf", "lbb", "gain", "sink", "pre2", "post2", "rel"], small))
        b = dict(zip(["gu1", "d1", "win", "wout", "gu2", "d2"], big))
        return [s["pre1"], s["post1"], b["gu1"], b["d1"], s["prem"], s["postm"], b["win"], s["lbf"], s["lbb"],
                s["gain"], s["sink"], b["wout"], s["pre2"], s["post2"], b["gu2"], b["d2"], s["rel"]]

    return (loss, grad_x[None], *ordered(small_grads, big_grads), *ordered(small_delta, big_delta),
            *ordered(small_new_m, big_new_m), *ordered(small_new_v, big_new_v))
```

```python
import functools
import math

import jax
import jax.numpy as jnp
import numpy as np
from jax import lax
from jax.experimental import pallas as pl
from jax.experimental.pallas import tpu as pltpu

F32 = jnp.float32
BF16 = jnp.bfloat16

HEAD = 128
CHUNK = 64
WINDOW = 128
SPAN = 3 * WINDOW
KV_HEADS = 2
REL_BUCKETS = 32
REL_MAX_DIST = 128
EPS = 1e-6
NEG_INF = -1e30

ADAM_LR = 0.001
ADAM_B1 = 0.9
ADAM_B2 = 0.999
ADAM_EPS = 1e-08
ADAM_WD = 0.01
ADAM_STEP = 10

N_CHIPS = 4
N_DEV = 8
V7X_VMEM_BYTES = 64 * 1024 * 1024
MESH = pl.DeviceIdType.MESH
ANY = pl.BlockSpec(memory_space=pl.ANY)


def _tile(n, pref, mult):
    t = (min(pref, n) // mult) * mult
    while t >= mult:
        if n % t == 0:
            return t
        t -= mult
    return n


def _params(semantics, block_bytes):
    limit = min(V7X_VMEM_BYTES - (4 << 20), 2 * int(block_bytes) + (8 << 20))
    return pltpu.CompilerParams(dimension_semantics=semantics, vmem_limit_bytes=limit)


def _nbytes(shape, dtype):
    return int(np.prod(shape)) * jnp.dtype(dtype).itemsize


PIN_TO_HBM_BYTES = 4 << 20


def _pallas(body, **kw):
    def pin_shape(s):
        if isinstance(s, jax.ShapeDtypeStruct) and _nbytes(s.shape, s.dtype) >= PIN_TO_HBM_BYTES:
            return pltpu.HBM(s.shape, s.dtype)
        return s

    def pin(a):
        if getattr(a, "dtype", None) in (F32, BF16) and _nbytes(a.shape, a.dtype) >= PIN_TO_HBM_BYTES:
            return pltpu.with_memory_space_constraint(a, pltpu.HBM)
        return a

    out_shape = kw["out_shape"]
    kw["out_shape"] = [pin_shape(s) for s in out_shape] if isinstance(out_shape, (list, tuple)) else pin_shape(out_shape)
    call = pl.pallas_call(body, **kw)
    return lambda *args: call(*[pin(a) for a in args])


def _dot(a, b, ca=1, cb=0):
    return lax.dot_general(a, b, (((ca,), (cb,)), ((), ())), preferred_element_type=F32)


def _split3(x):
    hi = x.astype(BF16)
    r1 = x - hi.astype(F32)
    mid = r1.astype(BF16)
    lo = (r1 - mid.astype(F32)).astype(BF16)
    return hi, mid, lo


def _dot_exact(a, b, ca=1, cb=0, split="b"):
    if split == "b":
        return sum(_dot(a, p, ca, cb) for p in _split3(b))
    return sum(_dot(p, b, ca, cb) for p in _split3(a))


def _rms(x):
    return lax.rsqrt(jnp.mean(x * x, axis=-1, keepdims=True) + EPS)


def _norm_bwd(u, x, gain):
    r = _rms(x)
    xhat = x * r
    dgain = jnp.sum(u * xhat, axis=0, keepdims=True)
    v = u * gain
    dx = r * (v - xhat * jnp.mean(v * xhat, axis=-1, keepdims=True))
    return dx, dgain


def _sigmoid(x):
    return 1.0 / (1.0 + jnp.exp(-x))


def _accumulate(ref, val, first):
    @pl.when(first)
    def _():
        ref[...] = val

    @pl.when(jnp.logical_not(first))
    def _():
        ref[...] += val


def _ordered(body, ins, in_specs, after):
    if after is None:
        return body, list(ins), list(in_specs)
    n_in = len(ins)

    def wrapped(*refs):
        body(*refs[:n_in], *refs[n_in + 1:])

    return wrapped, list(ins) + [after], list(in_specs) + [pl.BlockSpec(memory_space=pl.ANY)]


def _matmul(name, a, b, *, form, out_dtype, tm, tn, tk, a_map=None, b_map=None,
            out_shape=None, out_block=None, out_map=None, sizes=None, after=None):
    if sizes is None:
        if form == "nn":
            (m, k), n = a.shape, b.shape[1]
        elif form == "nt":
            (m, k), n = a.shape, b.shape[0]
        else:
            (k, m), n = a.shape, b.shape[1]
    else:
        m, n, k = sizes
    gi, gj, gk = m // tm, n // tn, k // tk
    a_blk = (tm, tk) if form != "tn" else (tk, tm)
    b_blk = (tk, tn) if form != "nt" else (tn, tk)
    if a_map is None:
        a_map = (lambda i, j, kk: (i, kk)) if form != "tn" else (lambda i, j, kk: (kk, i))
    else:
        a_blk = (None,) + a_blk
    if b_map is None:
        b_map = (lambda i, j, kk: (kk, j)) if form != "nt" else (lambda i, j, kk: (j, kk))
    else:
        b_blk = (None,) + b_blk
    if out_shape is None:
        out_shape, out_block, out_map = (m, n), (tm, tn), (lambda i, j, kk: (i, j))
    ca, cb = {"nn": (1, 0), "nt": (1, 1), "tn": (0, 0)}[form]

    def body(a_ref, b_ref, o_ref, *acc):
        part = _dot(a_ref[...], b_ref[...], ca, cb)
        if gk == 1:
            o_ref[...] = part.astype(o_ref.dtype)
        else:
            kk = pl.program_id(2)
            _accumulate(acc[0], part, kk == 0)

            @pl.when(kk == gk - 1)
            def _():
                o_ref[...] = acc[0][...].astype(o_ref.dtype)

    scratch = [] if gk == 1 else [pltpu.VMEM((tm, tn), F32)]
    vmem = (_nbytes((tm, tk), a.dtype) + _nbytes((tk, tn), b.dtype) + _nbytes((tm, tn), out_dtype)
            + 2 * _nbytes((tm, tn), F32))
    body, ins, in_specs = _ordered(body, [a, b], [pl.BlockSpec(a_blk, a_map), pl.BlockSpec(b_blk, b_map)], after)
    return _pallas(
        body, name=name, grid=(gi, gj, gk), in_specs=in_specs,
        out_specs=pl.BlockSpec(out_block, out_map),
        out_shape=jax.ShapeDtypeStruct(out_shape, out_dtype),
        scratch_shapes=scratch,
        compiler_params=_params(("parallel", "parallel", "arbitrary"), vmem),
    )(*ins)


V7X_HBM_BYTES_PER_US = 3.0e6
V7X_MXU_FLOPS_PER_US = 0.9e9
V7X_VMEM_RMW_BYTES_PER_US = 10e6
GRID_STEP_US = 0.35
MATMUL_VMEM_BUDGET = 40 << 20
MATMUL_MAX_TILE_FLOPS = 1 << 33


def _divisors(n, mult, lo):
    return [t for t in range(mult, n + 1, mult) if n % t == 0 and t >= min(lo, n)]


def _mm_tiles(m, n, k, out_dtype=F32, n_unit=None, k_unit=None):
    out_bytes = jnp.dtype(out_dtype).itemsize
    best = None
    for tm in _divisors(m, 128, 256):
        for tn in _divisors(n_unit or n, 128, 256):
            for tk in _divisors(k_unit or k, 128, 512):
                gi, gj, gk = m // tm, n // tn, k // tk
                vmem = 4 * tm * tk + 4 * tk * tn + 2 * tm * tn * out_bytes + 4 * tm * tn * (2 if gk > 1 else 1)
                if vmem > MATMUL_VMEM_BUDGET or 2 * tm * tn * tk > MATMUL_MAX_TILE_FLOPS:
                    continue
                a_bytes = 2 * m * k * (gj if gk > 1 else 1)
                b_bytes = 2 * k * n * (1 if gj == 1 and gk == 1 else gi)
                hbm_us = (a_bytes + b_bytes + m * n * out_bytes) / V7X_HBM_BYTES_PER_US
                acc_us = (8 * m * n * gk / V7X_VMEM_RMW_BYTES_PER_US) if gk > 1 else 0.0
                cost = max(2 * m * n * k / V7X_MXU_FLOPS_PER_US, 1.3 * hbm_us) + GRID_STEP_US * gi * gj * gk + acc_us
                key = (round(cost, 1), vmem)
                if best is None or key < best[0]:
                    best = (key, (tm, tn, tk))
    return best[1]


def _mm(name, a, b, form, out_dtype, after=None):
    if form == "nn":
        m, k, n = a.shape[0], a.shape[1], b.shape[1]
    elif form == "nt":
        m, k, n = a.shape[0], a.shape[1], b.shape[0]
    else:
        m, k, n = a.shape[1], a.shape[0], b.shape[1]
    tm, tn, tk = _mm_tiles(m, n, k, out_dtype)
    return _matmul(name, a, b, form=form, out_dtype=out_dtype, tm=tm, tn=tn, tk=tk, after=after)


def _row_tile(t):
    return _tile(t, 256, 8)


def _norm_fwd(name, x, gain):
    t, d = x.shape
    tm = _row_tile(t)

    def body(x_ref, g_ref, h_ref):
        xv = x_ref[...]
        h_ref[...] = (xv * _rms(xv) * g_ref[...]).astype(BF16)

    row = pl.BlockSpec((tm, d), lambda i: (i, 0))
    vec = pl.BlockSpec((1, d), lambda i: (0, 0))
    return _pallas(
        body, name=name, grid=(t // tm,), in_specs=[row, vec], out_specs=row,
        out_shape=jax.ShapeDtypeStruct((t, d), BF16),
        compiler_params=_params(("parallel",), 2 * _nbytes((tm, d), F32)),
    )(x, gain)


def _resid_norm_fwd(name, xres, ff, gpost, gpre, scale):
    t, d = xres.shape
    tm = _row_tile(t)

    def body(x_ref, f_ref, gp_ref, gn_ref, xn_ref, h_ref):
        f = f_ref[...]
        xn = x_ref[...] + scale * (f * _rms(f) * gp_ref[...])
        xn_ref[...] = xn
        h_ref[...] = (xn * _rms(xn) * gn_ref[...]).astype(BF16)

    row = pl.BlockSpec((tm, d), lambda i: (i, 0))
    vec = pl.BlockSpec((1, d), lambda i: (0, 0))
    return _pallas(
        body, name=name, grid=(t // tm,), in_specs=[row, row, vec, vec], out_specs=[row, row],
        out_shape=[jax.ShapeDtypeStruct((t, d), F32), jax.ShapeDtypeStruct((t, d), BF16)],
        compiler_params=_params(("parallel",), 4 * _nbytes((tm, d), F32)),
    )(xres, ff, gpost, gpre)


def _final_fwd_bwd(name, xres, ff, gpost, target, scale):
    t, d = xres.shape
    tm = _row_tile(t)

    def body(x_ref, f_ref, gp_ref, t_ref, loss_ref, dy_ref, dff_ref, dg_ref):
        i = pl.program_id(0)
        f = f_ref[...]
        gp = gp_ref[...]
        y = x_ref[...] + scale * (f * _rms(f) * gp)
        err = y - t_ref[...]
        part = 0.5 * jnp.sum(jnp.mean(err * err, axis=-1, keepdims=True), axis=0, keepdims=True)
        _accumulate(loss_ref, jnp.broadcast_to(part, loss_ref.shape), i == 0)
        dy = err / d
        dy_ref[...] = dy
        dff, dg = _norm_bwd(scale * dy, f, gp)
        dff_ref[...] = dff.astype(BF16)
        _accumulate(dg_ref, dg, i == 0)

    row = pl.BlockSpec((tm, d), lambda i: (i, 0))
    vec = pl.BlockSpec((1, d), lambda i: (0, 0))
    return _pallas(
        body, name=name, grid=(t // tm,), in_specs=[row, row, vec, row],
        out_specs=[pl.BlockSpec((8, 128), lambda i: (0, 0)), row, row, vec],
        out_shape=[jax.ShapeDtypeStruct((8, 128), F32), jax.ShapeDtypeStruct((t, d), F32),
                   jax.ShapeDtypeStruct((t, d), BF16), jax.ShapeDtypeStruct((1, d), F32)],
        compiler_params=_params(("arbitrary",), 5 * _nbytes((tm, d), F32)),
    )(xres, ff, gpost, target)


def _norms_bwd(name, dres, dh, xin, gpre, post=None, after=None):
    t, d = dres.shape
    tm = _row_tile(t)
    with_post = post is not None

    def body(*refs):
        if with_post:
            dr_ref, dh_ref, x_ref, g_ref, f_ref, gp_ref, dx_ref, dg_ref, dff_ref, dgp_ref = refs
        else:
            dr_ref, dh_ref, x_ref, g_ref, dx_ref, dg_ref = refs
        i = pl.program_id(0)
        dx, dg = _norm_bwd(dh_ref[...], x_ref[...], g_ref[...])
        dx = dr_ref[...] + dx
        dx_ref[...] = dx
        _accumulate(dg_ref, dg, i == 0)
        if with_post:
            dff, dgp = _norm_bwd(post[2] * dx, f_ref[...], gp_ref[...])
            dff_ref[...] = dff.astype(BF16)
            _accumulate(dgp_ref, dgp, i == 0)

    row = pl.BlockSpec((tm, d), lambda i: (i, 0))
    vec = pl.BlockSpec((1, d), lambda i: (0, 0))
    ins, in_specs = [dres, dh, xin, gpre], [row, row, row, vec]
    out_specs = [row, vec]
    out_shape = [jax.ShapeDtypeStruct((t, d), F32), jax.ShapeDtypeStruct((1, d), F32)]
    if with_post:
        ins += [post[0], post[1]]
        in_specs += [row, vec]
        out_specs += [row, vec]
        out_shape += [jax.ShapeDtypeStruct((t, d), BF16), jax.ShapeDtypeStruct((1, d), F32)]
    body, ins, in_specs = _ordered(body, ins, in_specs, after)
    return _pallas(
        body, name=name, grid=(t // tm,), in_specs=in_specs, out_specs=out_specs, out_shape=out_shape,
        compiler_params=_params(("arbitrary",), 6 * _nbytes((tm, d), F32)),
    )(*ins)


SWIGLU_TILE = (1024, 512)


def _ffn_gate_up_act(name, h, w_gu):
    t, d = h.shape
    f = w_gu.shape[1] // 2
    tm, tn = _tile(t, SWIGLU_TILE[0], 128), _tile(f, SWIGLU_TILE[1], 128)
    nf = f // tn

    def body(h_ref, wg_ref, wu_ref, g_ref, u_ref, a_ref):
        hv = h_ref[...]
        g = _dot(hv, wg_ref[...])
        u = _dot(hv, wu_ref[...])
        g_ref[...] = g.astype(BF16)
        u_ref[...] = u.astype(BF16)
        a_ref[...] = (g * _sigmoid(g) * u).astype(BF16)

    out = jax.ShapeDtypeStruct((t, f), BF16)
    blk = pl.BlockSpec((tm, tn), lambda i, j: (i, j))
    return _pallas(
        body, name=name, grid=(t // tm, nf),
        in_specs=[pl.BlockSpec((tm, d), lambda i, j: (i, 0)), pl.BlockSpec((d, tn), lambda i, j: (0, j)),
                  pl.BlockSpec((d, tn), lambda i, j: (0, j + nf))],
        out_specs=[blk, blk, blk], out_shape=[out, out, out],
        compiler_params=_params(("parallel", "parallel"),
                                _nbytes((tm, d), BF16) + 2 * _nbytes((d, tn), BF16) + 5 * _nbytes((tm, tn), F32)),
    )(h, w_gu, w_gu)


def _ffn_dact(name, dff, w_down, gate, up, after=None):
    t, d = dff.shape
    f = w_down.shape[0]
    tm, tn = _tile(t, SWIGLU_TILE[0], 128), _tile(f, SWIGLU_TILE[1], 128)

    def body(d_ref, w_ref, g_ref, u_ref, o_ref):
        da = _dot(d_ref[...], w_ref[...], 1, 1)
        g = g_ref[...].astype(F32)
        u = u_ref[...].astype(F32)
        sig = _sigmoid(g)
        o_ref[0] = (da * u * sig * (1.0 + g * (1.0 - sig))).astype(BF16)
        o_ref[1] = (da * g * sig).astype(BF16)

    blk = pl.BlockSpec((tm, tn), lambda i, j: (i, j))
    body, ins, in_specs = _ordered(
        body, [dff, w_down, gate, up],
        [pl.BlockSpec((tm, d), lambda i, j: (i, 0)), pl.BlockSpec((tn, d), lambda i, j: (j, 0)), blk, blk], after)
    return _pallas(
        body, name=name, grid=(t // tm, f // tn), in_specs=in_specs,
        out_specs=pl.BlockSpec((2, tm, tn), lambda i, j: (0, i, j)),
        out_shape=jax.ShapeDtypeStruct((2, t, f), BF16),
        compiler_params=_params(("parallel", "parallel"),
                                _nbytes((tm, d), BF16) + _nbytes((tn, d), BF16) + 5 * _nbytes((tm, tn), F32)),
    )(*ins)


def _ffn_dh(name, dgu, w_gu, after=None):
    _, t, f = dgu.shape
    d = w_gu.shape[0]
    tm, tn, tk = _mm_tiles(t, d, 2 * f, F32, k_unit=f)
    nkf = f // tk
    return _matmul(name, dgu, w_gu, form="nt", out_dtype=F32, tm=tm, tn=tn, tk=tk, sizes=(t, d, 2 * f),
                   a_map=lambda i, j, kk: (kk // nkf, i, kk % nkf), after=after)


def _ffn_dw_gate_up(name, h, dgu, after=None):
    _, t, f = dgu.shape
    d = h.shape[1]
    tm, tn, tk = _mm_tiles(d, 2 * f, t, BF16, n_unit=f)
    nf = f // tn
    return _matmul(name, h, dgu, form="tn", out_dtype=BF16, tm=tm, tn=tn, tk=tk, sizes=(d, 2 * f, t),
                   b_map=lambda i, j, kk: (j // nf, kk, j % nf), after=after)


def _lower_bound(lbp):
    m = jnp.max(lbp, axis=0, keepdims=True)
    e = jnp.exp(lbp - m)
    return e[0:1] / jnp.sum(e, axis=0, keepdims=True)


def _chunk_mask(reverse):
    row = lax.broadcasted_iota(jnp.int32, (CHUNK, CHUNK), 0)
    col = lax.broadcasted_iota(jnp.int32, (CHUNK, CHUNK), 1)
    return (col >= row) if reverse else (col <= row)


def _hgrn_gates(z, lb, mask_bf):
    sig = _sigmoid(z)
    f = lb + (1.0 - lb) * sig
    logf = jnp.log(f)
    k = 1.0 - f
    cum = _dot_exact(mask_bf, logf)
    last = jnp.sum(logf, axis=0, keepdims=True)
    return sig, f, k, cum, last


def _hgrn_scan_fwd(name, p, lbp_f, lbp_b):
    t = p.shape[0]
    hw = lbp_f.shape[1]
    nh, nc = hw // HEAD, t // CHUNK

    def body(qf, vf, zf, qb, vb, zb, lbf, lbb, of_ref, ob_ref, stf_ref, stb_ref, state):
        n = pl.program_id(0)

        @pl.when(n == 0)
        def _():
            state[...] = jnp.zeros_like(state)

        directions = [(qf, vf, zf, lbf, of_ref, stf_ref), (qb, vb, zb, lbb, ob_ref, stb_ref)]
        wide = []
        for d, (q_ref, v_ref, z_ref, lb_ref, o_ref, st_ref) in enumerate(directions):
            mask = _chunk_mask(d == 1)
            lb = _lower_bound(lb_ref[...])
            _, _, k, cum, last = _hgrn_gates(z_ref[...], lb, mask.astype(BF16))
            v = v_ref[...].astype(BF16)
            qd = (q_ref[...] * jnp.exp(cum)).astype(BF16)
            kd = (k * jnp.exp(-cum)).astype(BF16)
            kt = (k * jnp.exp(last - cum)).astype(BF16)
            s_all = state[d]
            st_ref[...] = s_all
            wide.append((mask, v, qd, kd, kt, jnp.exp(last), s_all, o_ref))
        pairs = [(d, slice(h * HEAD, (h + 1) * HEAD)) for d in range(2) for h in range(nh)]
        a = [jnp.where(wide[d][0], _dot(wide[d][2][:, sl], wide[d][3][:, sl], 1, 1), 0.0).astype(BF16)
             for d, sl in pairs]
        inter = [_dot(wide[d][2][:, sl], wide[d][6][:, sl].astype(BF16), 1, 1) for d, sl in pairs]
        intra = [_dot(a[i], wide[d][1][:, sl]) for i, (d, sl) in enumerate(pairs)]
        grow = [_dot(wide[d][1][:, sl], wide[d][4][:, sl], 0, 0) for d, sl in pairs]
        for i, (d, sl) in enumerate(pairs):
            wide[d][7][:, sl] = intra[i] + inter[i]
            state[d, :, sl] = wide[d][6][:, sl] * wide[d][5][:, sl] + grow[i]

    def col(group, reverse):
        return pl.BlockSpec((CHUNK, hw), lambda n: ((nc - 1 - n) if reverse else n, group))

    def st(reverse):
        return pl.BlockSpec((None, HEAD, hw), lambda n: ((nc - 1 - n) if reverse else n, 0, 0))

    lb_spec = pl.BlockSpec((2, hw), lambda n: (0, 0))
    out = jax.ShapeDtypeStruct((t, hw), F32)
    states = jax.ShapeDtypeStruct((nc, HEAD, hw), F32)
    return _pallas(
        body, name=name, grid=(nc,),
        in_specs=[col(0, False), col(1, False), col(2, False), col(0, True), col(1, True), col(3, True),
                  lb_spec, lb_spec],
        out_specs=[col(0, False), col(0, True), st(False), st(True)],
        out_shape=[out, out, states, states],
        scratch_shapes=[pltpu.VMEM((2, HEAD, hw), F32)],
        compiler_params=_params(("arbitrary",), 12 * _nbytes((HEAD, hw), F32)),
    )(p, p, p, p, p, p, lbp_f, lbp_b)


def _hgrn_scan_bwd(name, p, lbp_f, lbp_b, do, st_f, st_b):
    t = p.shape[0]
    hw = lbp_f.shape[1]
    nh, nc = hw // HEAD, t // CHUNK

    def body(qf, vf, zf, dof, sf, qb, vb, zb, dob, sb, lbf, lbb, dqf, dvf, dzf, dlbf, dqb, dvb, dzb, dlbb,
             dstate, dlb_acc, dqd_s, dkd_s, dkt_s, ddec_s):
        n = pl.program_id(0)

        @pl.when(n == 0)
        def _():
            dstate[...] = jnp.zeros_like(dstate)
            dlb_acc[...] = jnp.zeros_like(dlb_acc)

        directions = [(qf, vf, zf, dof, sf, lbf, dqf, dvf, dzf, dlbf), (qb, vb, zb, dob, sb, lbb, dqb, dvb, dzb, dlbb)]
        for d, (q_ref, v_ref, z_ref, do_ref, st_ref, lb_ref, dq_ref, dv_ref, dz_ref, dlb_ref) in enumerate(directions):
            mask = _chunk_mask(d == 1)
            mask_bf = mask.astype(BF16)
            lb = _lower_bound(lb_ref[...])
            sig, f, k, cum, last = _hgrn_gates(z_ref[...], lb, mask_bf)
            e_pos, e_neg, e_tail = jnp.exp(cum), jnp.exp(-cum), jnp.exp(last - cum)
            dec = jnp.exp(last)
            v = v_ref[...].astype(BF16)
            qd, kd, kt = q_ref[...] * e_pos, k * e_neg, k * e_tail
            qd_bf, kd_bf, kt_bf = qd.astype(BF16), kd.astype(BF16), kt.astype(BF16)
            s_all = st_ref[...]
            ds_all = dstate[d]
            dov = do_ref[...].astype(BF16)
            cols = [slice(h * HEAD, (h + 1) * HEAD) for h in range(nh)]
            s_bf = [s_all[:, sl].astype(BF16) for sl in cols]
            ds_bf = [ds_all[:, sl].astype(BF16) for sl in cols]
            a = [jnp.where(mask, _dot(qd_bf[:, sl], kd_bf[:, sl], 1, 1), 0.0).astype(BF16) for sl in cols]
            da = [jnp.where(mask, _dot(dov[:, sl], v[:, sl], 1, 1), 0.0).astype(BF16) for sl in cols]
            dv_h = [_dot(a[h], dov[:, sl], 0, 0) + _dot(kt_bf[:, sl], ds_bf[h], 1, 1) for h, sl in enumerate(cols)]
            dqd_h = [_dot(da[h], kd_bf[:, sl]) + _dot(dov[:, sl], s_bf[h]) for h, sl in enumerate(cols)]
            dkd_h = [_dot(da[h], qd_bf[:, sl], 0, 0) for h, sl in enumerate(cols)]
            dkt_h = [_dot(v[:, sl], ds_bf[h]) for h, sl in enumerate(cols)]
            dst_h = [_dot(dov[:, sl], qd_bf[:, sl], 0, 0) + ds_all[:, sl] * dec[:, sl] for sl in cols]
            for h, sl in enumerate(cols):
                dv_ref[:, sl] = dv_h[h]
                dqd_s[:, sl] = dqd_h[h]
                dkd_s[:, sl] = dkd_h[h]
                dkt_s[:, sl] = dkt_h[h]
                dstate[d, :, sl] = dst_h[h]
                ddec_s[:, sl] = jnp.sum(ds_all[:, sl] * s_all[:, sl], axis=0, keepdims=True)
            dqd, dkd, dkt = dqd_s[...], dkd_s[...], dkt_s[...]
            dlast = jnp.sum(dkt * kt, axis=0, keepdims=True) + dec * ddec_s[...]
            dq_ref[...] = dqd * e_pos
            dk = dkd * e_neg + dkt * e_tail
            dcum = dqd * qd - dkd * kd - dkt * kt
            dlogf = _dot_exact(mask_bf, dcum, 0, 0) + dlast
            df = dlogf / f - dk
            dz_ref[...] = df * (1.0 - lb) * sig * (1.0 - sig)
            dlb_acc[d] += jnp.sum(df * (1.0 - sig), axis=0, keepdims=True)

            @pl.when(n == nc - 1)
            def _():
                g = dlb_acc[d] * lb * (1.0 - lb)
                dlb_ref[0:1, :] = g
                dlb_ref[1:2, :] = -g

    def col(group, reverse):
        return pl.BlockSpec((CHUNK, hw), lambda n: (n if reverse else (nc - 1 - n), group))

    def st(reverse):
        return pl.BlockSpec((None, HEAD, hw), lambda n: (n if reverse else (nc - 1 - n), 0, 0))

    lb_spec = pl.BlockSpec((2, hw), lambda n: (0, 0))
    out = jax.ShapeDtypeStruct((t, hw), F32)
    dlb = jax.ShapeDtypeStruct((2, hw), F32)
    wide = pltpu.VMEM((CHUNK, hw), F32)
    return _pallas(
        body, name=name, grid=(nc,),
        in_specs=[col(0, False), col(1, False), col(2, False), col(0, False), st(False),
                  col(0, True), col(1, True), col(3, True), col(0, True), st(True), lb_spec, lb_spec],
        out_specs=[col(0, False), col(0, False), col(0, False), lb_spec,
                   col(0, True), col(0, True), col(0, True), lb_spec],
        out_shape=[out, out, out, dlb, out, out, out, dlb],
        scratch_shapes=[pltpu.VMEM((2, HEAD, hw), F32), pltpu.VMEM((2, 1, hw), F32), wide, wide, wide,
                        pltpu.VMEM((1, hw), F32)],
        compiler_params=_params(("arbitrary",), 16 * _nbytes((HEAD, hw), F32)),
    )(p, p, p, do, st_f, p, p, p, do, st_b, lbp_f, lbp_b)


def _hgrn_out_fwd(name, o_f, o_b, p, gain, g_group):
    t, hw = o_f.shape
    nh = hw // HEAD
    tm = _tile(t, 512, 8)

    def body(of_ref, ob_ref, g_ref, gain_ref, y_ref):
        o = of_ref[...] + ob_ref[...]
        g = g_ref[...]
        y_ref[...] = (o * _rms(o) * gain_ref[...] * (g * _sigmoid(g))).astype(BF16)

    blk = pl.BlockSpec((tm, HEAD), lambda i, h: (i, h))
    return _pallas(
        body, name=name, grid=(t // tm, nh),
        in_specs=[blk, blk, pl.BlockSpec((tm, HEAD), lambda i, h: (i, g_group * nh + h)),
                  pl.BlockSpec((1, HEAD), lambda i, h: (0, h))],
        out_specs=blk, out_shape=jax.ShapeDtypeStruct((t, hw), BF16),
        compiler_params=_params(("parallel", "parallel"), 1 << 20),
    )(o_f, o_b, p, gain)


def _hgrn_out_bwd(name, dy, o_f, o_b, p, gain, g_group, after=None):
    t, hw = o_f.shape
    nh = hw // HEAD
    tm = _tile(t, 512, 8)

    def body(dy_ref, of_ref, ob_ref, g_ref, gain_ref, do_ref, dg_ref, dgain_ref):
        i = pl.program_id(1)
        o = of_ref[...] + ob_ref[...]
        g = g_ref[...]
        gain_v = gain_ref[...]
        sig = _sigmoid(g)
        dyv = dy_ref[...]
        do, dgain = _norm_bwd(dyv * (g * sig), o, gain_v)
        do_ref[...] = do
        dg_ref[...] = dyv * (o * _rms(o) * gain_v) * sig * (1.0 + g * (1.0 - sig))
        _accumulate(dgain_ref, dgain, i == 0)

    blk = pl.BlockSpec((tm, HEAD), lambda h, i: (i, h))
    vec = pl.BlockSpec((1, HEAD), lambda h, i: (0, h))
    out = jax.ShapeDtypeStruct((t, hw), F32)
    body, ins, in_specs = _ordered(
        body, [dy, o_f, o_b, p, gain],
        [blk, blk, blk, pl.BlockSpec((tm, HEAD), lambda h, i: (i, g_group * nh + h)), vec], after)
    return _pallas(
        body, name=name, grid=(nh, t // tm), in_specs=in_specs,
        out_specs=[blk, blk, vec], out_shape=[out, out, jax.ShapeDtypeStruct((1, hw), F32)],
        compiler_params=_params(("parallel", "arbitrary"), 1 << 20),
    )(*ins)


def _t5_bucket_ids():
    c = np.arange(WINDOW)[:, None]
    s = np.arange(SPAN)[None, :]
    rel = s - WINDOW - c
    nb = REL_BUCKETS // 2
    max_exact = nb // 2
    bucket = (rel > 0).astype(np.int32) * nb
    n = np.abs(rel)
    large = max_exact + (np.log(np.maximum(n, 1) / max_exact) / np.log(REL_MAX_DIST / max_exact)
                         * (nb - max_exact)).astype(np.int32)
    large = np.minimum(large, nb - 1)
    ids = bucket + np.where(n < max_exact, n, large).astype(np.int32)
    return jnp.asarray(ids.reshape(1, WINDOW * SPAN), jnp.int32)


def _bias_onehot(ids_ref):
    n = ids_ref.shape[1]
    return (lax.broadcasted_iota(jnp.int32, (REL_BUCKETS, n), 0) == ids_ref[...]).astype(BF16)


def _bias_gather(name, table_t, ids):
    nh = table_t.shape[0]

    def body(t_ref, ids_ref, o_ref):
        o_ref[...] = _dot_exact(t_ref[...], _bias_onehot(ids_ref), split="a")

    return _pallas(
        body, name=name, out_shape=jax.ShapeDtypeStruct((nh, ids.shape[1]), F32),
        compiler_params=pltpu.CompilerParams(vmem_limit_bytes=32 << 20),
    )(table_t, ids)


def _bias_scatter(name, dbias, ids):
    nh = dbias.shape[0]

    def body(d_ref, ids_ref, o_ref):
        o_ref[...] = _dot_exact(d_ref[...], _bias_onehot(ids_ref), 1, 1, split="a")

    return _pallas(
        body, name=name, out_shape=jax.ShapeDtypeStruct((nh, REL_BUCKETS), F32),
        compiler_params=pltpu.CompilerParams(vmem_limit_bytes=32 << 20),
    )(dbias, ids)


def _attn_valid(i, t):
    c = lax.broadcasted_iota(jnp.int32, (WINDOW, SPAN), 0)
    s = lax.broadcasted_iota(jnp.int32, (WINDOW, SPAN), 1)
    rel = s - WINDOW - c
    pos = i * WINDOW - WINDOW + s
    return (jnp.abs(rel) <= WINDOW) & (pos >= 0) & (pos < t)


def _attn_probs(qs, khs, b_ref, s_ref, valid):
    heads = range(len(qs))
    sinks = [s_ref[0:1, h:h + 1] for h in heads]
    s = [_dot(qs[h], khs[h], 1, 1) / math.sqrt(HEAD) for h in heads]
    s = [jnp.where(valid, s[h] + b_ref[h], NEG_INF) for h in heads]
    m = [jnp.maximum(jnp.max(s[h], axis=-1, keepdims=True), sinks[h]) for h in heads]
    e = [jnp.exp(s[h] - m[h]) for h in heads]
    es = [jnp.exp(sinks[h] - m[h]) for h in heads]
    inv = [1.0 / (jnp.sum(e[h], axis=-1, keepdims=True) + es[h]) for h in heads]
    return [e[h] * inv[h] for h in heads], [es[h] * inv[h] for h in heads]


def _attn_fwd(name, p, k_pad, v_pad, bias, sink, q_group_blk):
    t = p.shape[0]
    nh = bias.shape[0]
    aw = nh * HEAD
    grp = nh // KV_HEADS
    nb = t // WINDOW

    def body(q_ref, k_ref, v_ref, b_ref, s_ref, y_ref):
        i = pl.program_id(0)
        valid = _attn_valid(i, t)
        start = pl.multiple_of(i * WINDOW, WINDOW)
        ks = k_ref[pl.ds(start, SPAN), :]
        vs = v_ref[pl.ds(start, SPAN), :]
        heads = range(nh)
        col = lambda h: slice(h * HEAD, (h + 1) * HEAD)
        qs = [q_ref[:, col(h)].astype(BF16) for h in heads]
        pr, _ = _attn_probs(qs, [ks[:, col(h // grp)] for h in heads], b_ref, s_ref, valid)
        out = [_dot(pr[h].astype(BF16), vs[:, col(h // grp)]) for h in heads]
        for h in heads:
            y_ref[:, col(h)] = out[h].astype(BF16)

    full = lambda a: pl.BlockSpec(a.shape, lambda i: (0,) * a.ndim)
    return _pallas(
        body, name=name, grid=(nb,),
        in_specs=[pl.BlockSpec((WINDOW, aw), lambda i: (i, q_group_blk)), full(k_pad), full(v_pad), full(bias),
                  full(sink)],
        out_specs=pl.BlockSpec((WINDOW, aw), lambda i: (i, 0)),
        out_shape=jax.ShapeDtypeStruct((t, aw), BF16),
        compiler_params=_params(("parallel",), _nbytes(k_pad.shape, BF16) * 2 + _nbytes(bias.shape, F32)),
    )(p, k_pad, v_pad, bias, sink)


def _attn_bwd(name, p, k_pad, v_pad, bias, sink, dy, q_group_blk, dy_blk, after=None):
    t = p.shape[0]
    nh = bias.shape[0]
    aw = nh * HEAD
    grp = nh // KV_HEADS
    nb = t // WINDOW
    kvw = k_pad.shape[1]

    def body(q_ref, k_ref, v_ref, b_ref, s_ref, dy_ref, dq_ref, dk_ref, dv_ref, db_ref, ds_ref):
        i = pl.program_id(0)

        @pl.when(i == 0)
        def _():
            dk_ref[...] = jnp.zeros_like(dk_ref)
            dv_ref[...] = jnp.zeros_like(dv_ref)
            db_ref[...] = jnp.zeros_like(db_ref)
            ds_ref[...] = jnp.zeros_like(ds_ref)

        valid = _attn_valid(i, t)
        start = pl.multiple_of(i * WINDOW, WINDOW)
        ks = k_ref[pl.ds(start, SPAN), :]
        vs = v_ref[pl.ds(start, SPAN), :]
        inv_sqrt = 1.0 / math.sqrt(HEAD)
        heads = range(nh)
        col = lambda h: slice(h * HEAD, (h + 1) * HEAD)
        qs = [q_ref[:, col(h)].astype(BF16) for h in heads]
        khs = [ks[:, col(h // grp)] for h in heads]
        pr, ps = _attn_probs(qs, khs, b_ref, s_ref, valid)
        dos = [dy_ref[:, col(h)].astype(BF16) for h in heads]
        dp = [_dot(dos[h], vs[:, col(h // grp)], 1, 1) for h in heads]
        delta = [jnp.sum(pr[h] * dp[h], axis=-1, keepdims=True) for h in heads]
        dsc = [pr[h] * (dp[h] - delta[h]) for h in heads]
        dsr = [(dsc[h] * inv_sqrt).astype(BF16) for h in heads]
        dq = [_dot(dsr[h], khs[h]) for h in heads]
        dk = [_dot(dsr[h], qs[h], 0, 0) for h in heads]
        dv = [_dot(pr[h].astype(BF16), dos[h], 0, 0) for h in heads]
        for h in heads:
            db_ref[h] += dsc[h]
            ds_ref[h:h + 1, :] += jnp.broadcast_to(jnp.sum(-ps[h] * delta[h], axis=0, keepdims=True), (1, 128))
            dq_ref[:, col(h)] = dq[h]
        for kv in range(KV_HEADS):
            group = range(kv * grp, (kv + 1) * grp)
            dk_ref[pl.ds(start, SPAN), col(kv)] += sum(dk[h] for h in group)
            dv_ref[pl.ds(start, SPAN), col(kv)] += sum(dv[h] for h in group)

    full = lambda a: pl.BlockSpec(a.shape, lambda i: (0,) * a.ndim)
    whole = lambda shape: pl.BlockSpec(shape, lambda i: (0,) * len(shape))
    pad_shape = (t + 2 * WINDOW, kvw)
    body, ins, in_specs = _ordered(
        body, [p, k_pad, v_pad, bias, sink, dy],
        [pl.BlockSpec((WINDOW, aw), lambda i: (i, q_group_blk)), full(k_pad), full(v_pad), full(bias), full(sink),
         pl.BlockSpec((WINDOW, aw), lambda i: (i, dy_blk))], after)
    return _pallas(
        body, name=name, grid=(nb,), in_specs=in_specs,
        out_specs=[pl.BlockSpec((WINDOW, aw), lambda i: (i, 0)), whole(pad_shape), whole(pad_shape),
                   whole(bias.shape), whole((nh, 128))],
        out_shape=[jax.ShapeDtypeStruct((t, aw), F32), jax.ShapeDtypeStruct(pad_shape, F32),
                   jax.ShapeDtypeStruct(pad_shape, F32), jax.ShapeDtypeStruct(bias.shape, F32),
                   jax.ShapeDtypeStruct((nh, 128), F32)],
        compiler_params=_params(("arbitrary",), 3 * _nbytes(pad_shape, F32) + 2 * _nbytes(bias.shape, F32)),
    )(*ins)


def _pad_kv(name, p, kv_blk, kvw):
    t = p.shape[0]
    nb = t // WINDOW

    def body(x_ref, o_ref):
        i = pl.program_id(0)
        inside = jnp.logical_and(i >= 1, i <= nb)
        o_ref[...] = jnp.where(inside, x_ref[...], 0.0).astype(BF16)

    return _pallas(
        body, name=name, grid=(nb + 2,),
        in_specs=[pl.BlockSpec((WINDOW, kvw), lambda i: (jnp.clip(i - 1, 0, nb - 1), kv_blk))],
        out_specs=pl.BlockSpec((WINDOW, kvw), lambda i: (i, 0)),
        out_shape=jax.ShapeDtypeStruct((t + 2 * WINDOW, kvw), BF16),
        compiler_params=_params(("parallel",), 1 << 20),
    )(p)


def _mix_dproj(name, pieces, kv_pads, t, after=None):
    hw = pieces[0][0].shape[1]
    kvw = kv_pads[0].shape[1]
    widths = [hw] * len(pieces) + [kvw] * len(kv_pads)
    total = sum(widths)
    tm = WINDOW
    flat = [a for pc in pieces for a in pc]

    def body(*refs):
        o_ref = refs[-1]
        pos, off = 0, 0
        for pc in pieces:
            val = refs[pos][...]
            for extra in range(1, len(pc)):
                val = val + refs[pos + extra][...]
            o_ref[:, off:off + hw] = val.astype(BF16)
            pos += len(pc)
            off += hw
        for _ in kv_pads:
            o_ref[:, off:off + kvw] = refs[pos][...].astype(BF16)
            pos += 1
            off += kvw

    in_specs = [pl.BlockSpec((tm, hw), lambda i: (i, 0)) for _ in flat]
    in_specs += [pl.BlockSpec((tm, kvw), lambda i: (i + 1, 0)) for _ in kv_pads]
    body, ins, in_specs = _ordered(body, [*flat, *kv_pads], in_specs, after)
    return _pallas(
        body, name=name, grid=(t // tm,), in_specs=in_specs,
        out_specs=pl.BlockSpec((tm, total), lambda i: (i, 0)),
        out_shape=jax.ShapeDtypeStruct((t, total), BF16),
        compiler_params=_params(("parallel",), 3 * _nbytes((tm, total), F32)),
    )(*ins)


def _concat_cols(name, a, b):
    t, wa = a.shape
    wb = b.shape[1]
    tm = _tile(t, 512, 16)

    def body(a_ref, b_ref, o_ref):
        o_ref[:, :wa] = a_ref[...]
        o_ref[:, wa:] = b_ref[...]

    return _pallas(
        body, name=name, grid=(t // tm,),
        in_specs=[pl.BlockSpec((tm, wa), lambda i: (i, 0)), pl.BlockSpec((tm, wb), lambda i: (i, 0))],
        out_specs=pl.BlockSpec((tm, wa + wb), lambda i: (i, 0)),
        out_shape=jax.ShapeDtypeStruct((t, wa + wb), a.dtype),
        compiler_params=_params(("parallel",), 2 * _nbytes((tm, wa + wb), a.dtype)),
    )(a, b)


def _cast_into_full(name, w, geom, idx, after=None):
    r, c = w.shape
    tr = _tile(r, 256, 16)
    nr = r // tr
    if geom.col:
        place = lambda i, iref: (i, iref[0])
    else:
        place = lambda i, iref: (iref[0] * nr + i, 0)

    def body(i_ref, w_ref, *rest):
        rest[-1][...] = w_ref[...].astype(BF16)

    in_specs = [pl.BlockSpec((tr, c), lambda i, iref: (i, 0))]
    ins = [w]
    if after is not None:
        in_specs.append(pl.BlockSpec(memory_space=pl.ANY))
        ins.append(after)
    return _pallas(
        body, name=name,
        grid_spec=pltpu.PrefetchScalarGridSpec(
            num_scalar_prefetch=1, grid=(nr,), in_specs=in_specs, out_specs=pl.BlockSpec((tr, c), place)),
        out_shape=pltpu.HBM(geom.full_shape, BF16),
        compiler_params=_params(("parallel",), 2 * _nbytes((tr, c), F32)),
    )(idx, *ins)


def _adamw(name, w, g, m, v):
    r, c = w.shape
    tr = _tile(r, 128, 8)
    bc1 = 1.0 - ADAM_B1 ** ADAM_STEP
    bc2 = 1.0 - ADAM_B2 ** ADAM_STEP

    def body(w_ref, g_ref, m_ref, v_ref, go_ref, d_ref, nm_ref, nv_ref):
        gv = g_ref[...]
        go_ref[...] = gv
        nm = ADAM_B1 * m_ref[...] + (1.0 - ADAM_B1) * gv
        nv = ADAM_B2 * v_ref[...] + (1.0 - ADAM_B2) * (gv * gv)
        nm_ref[...] = nm
        nv_ref[...] = nv
        d_ref[...] = -ADAM_LR * ((nm / bc1) / (jnp.sqrt(nv / bc2) + ADAM_EPS) + ADAM_WD * w_ref[...])

    blk = pl.BlockSpec((tr, c), lambda i: (i, 0))
    out = jax.ShapeDtypeStruct((r, c), F32)
    return _pallas(
        body, name=name, grid=(r // tr,), in_specs=[blk] * 4, out_specs=[blk] * 4, out_shape=[out] * 4,
        compiler_params=_params(("parallel",), 8 * _nbytes((tr, c), F32)),
    )(w, g, m, v)


def _mesh_pos():
    return lax.axis_index("x"), lax.axis_index("y"), lax.axis_index("c")


def _other_chips(x, y):
    return [(1 - x, y), (x, 1 - y), (1 - x, 1 - y)]


class _Big:
    def __init__(self, shard_shape, col_sharded):
        self.col = col_sharded
        r, c = shard_shape
        self.shard_shape = (r, c)
        self.full_shape = (r, N_CHIPS * c) if col_sharded else (N_CHIPS * r, c)
        self.half_shape = (r // 2, N_CHIPS * c) if col_sharded else (N_CHIPS * r, c // 2)
        self.shard_half_shape = (r // 2, c) if col_sharded else (r, c // 2)

    def region(self, ref, s, half=None):
        r, c = self.shard_shape
        if self.col:
            rows = slice(None) if half is None else pl.ds(half * (r // 2), r // 2)
            return ref.at[rows, pl.ds(s * c, c)]
        cols = slice(None) if half is None else pl.ds(half * (c // 2), c // 2)
        return ref.at[pl.ds(s * r, r), cols]

    def three_halves(self, ref, half):
        r, c = self.shard_shape
        if self.col:
            return ref.at[pl.ds(half * (r // 2), r // 2), pl.ds(0, 3 * c)]
        return ref.at[pl.ds(0, 3 * r), pl.ds(half * (c // 2), c // 2)]

    def half_of_full(self, ref, half):
        r, c = self.full_shape
        if self.col:
            return ref.at[pl.ds(half * (r // 2), r // 2), :]
        return ref.at[:, pl.ds(half * (c // 2), c // 2)]

    def half_of_shard(self, ref, half):
        r, c = self.shard_shape
        if self.col:
            return ref.at[pl.ds(half * (r // 2), r // 2), :]
        return ref.at[:, pl.ds(half * (c // 2), c // 2)]

    def shard_of_half(self, ref, s):
        r, c = self.shard_shape
        if self.col:
            return ref.at[:, pl.ds(s * c, c)]
        return ref.at[pl.ds(s * r, r), :]


HBM =pl.BlockSpec(memory_space=pltpu.HBM)
SEM = pl.BlockSpec(memory_space=pltpu.SEMAPHORE)
SPLIT_COPY = pltpu.CompilerParams(has_side_effects=pltpu.SideEffectType.DATAFLOW_SIDE_EFFECTING)


def _in_hbm(a):
    return pltpu.with_memory_space_constraint(a, pltpu.HBM)


def _gather_start(name, fulls, geoms):
    nw = len(fulls)

    def body(*refs):
        dst = refs[nw:2 * nw]
        sems = refs[2 * nw:-1]
        x, y, c = _mesh_pos()
        mine = 2 * x + y
        for w in range(nw):
            own_half = geoms[w].region(dst[w], mine, c)
            for chip in _other_chips(x, y):
                pltpu.make_async_remote_copy(src_ref=own_half, dst_ref=own_half, send_sem=sems[2 * w],
                                             recv_sem=sems[2 * w + 1], device_id=(*chip, c),
                                             device_id_type=MESH).start()
        refs[-1][...] = jnp.zeros_like(refs[-1])

    out = _pallas(
        body, name=name, in_specs=[HBM] * nw,
        out_specs=[HBM] * nw + [SEM] * (2 * nw) + [pl.BlockSpec(memory_space=pltpu.VMEM)],
        out_shape=[pltpu.HBM(g.full_shape, BF16) for g in geoms] + [pltpu.SemaphoreType.DMA(())] * (2 * nw)
        + [jax.ShapeDtypeStruct((8, 128), F32)],
        input_output_aliases={w: w for w in range(nw)}, compiler_params=SPLIT_COPY,
    )(*[_in_hbm(a) for a in fulls])
    return out[:nw], [(out[nw + 2 * w], out[nw + 2 * w + 1]) for w in range(nw)], out[-1]


def _wait_three(geom, ref, half, send_sem, recv_sem, peer, recv):
    three = geom.three_halves(ref, half)
    copy = pltpu.make_async_remote_copy(src_ref=three, dst_ref=three, send_sem=send_sem, recv_sem=recv_sem,
                                        device_id=peer, device_id_type=MESH)
    if recv:
        copy.wait_recv()
    else:
        copy.wait_send()


def _gather_forward(name, full, geom, sems, after):
    def body(w_in, send_sem, recv_sem, after_ref, w_ref, fwd_send, fwd_recv):
        x, y, c = _mesh_pos()
        sibling = (x, y, 1 - c)
        _wait_three(geom, w_ref, c, send_sem, recv_sem, sibling, recv=True)
        for chip in _other_chips(x, y):
            landed = geom.region(w_ref, 2 * chip[0] + chip[1], c)
            pltpu.make_async_remote_copy(src_ref=landed, dst_ref=landed, send_sem=fwd_send, recv_sem=fwd_recv,
                                         device_id=sibling, device_id_type=MESH).start()
        _wait_three(geom, w_ref, c, send_sem, recv_sem, sibling, recv=False)

    sem = pltpu.SemaphoreType.DMA(())
    out = _pallas(
        body, name=name, in_specs=[HBM, SEM, SEM, pl.BlockSpec(memory_space=pl.ANY)], out_specs=[HBM, SEM, SEM],
        out_shape=[pltpu.HBM(geom.full_shape, BF16), sem, sem],
        input_output_aliases={0: 0}, compiler_params=SPLIT_COPY,
    )(full, sems[0], sems[1], after)
    return out[0], (out[1], out[2])


def _gather_end(name, full, geom, sems, after):
    def body(w_in, fwd_send, fwd_recv, after_ref, w_ref):
        x, y, c = _mesh_pos()
        sibling = (x, y, 1 - c)
        _wait_three(geom, w_ref, 1 - c, fwd_send, fwd_recv, sibling, recv=True)
        _wait_three(geom, w_ref, c, fwd_send, fwd_recv, sibling, recv=False)

    return _pallas(
        body, name=name, in_specs=[HBM, SEM, SEM, pl.BlockSpec(memory_space=pl.ANY)], out_specs=HBM,
        out_shape=pltpu.HBM(geom.full_shape, BF16),
        input_output_aliases={0: 0}, compiler_params=SPLIT_COPY,
    )(full, sems[0], sems[1], after)


def _split_copy_call(name, arrays, fn, sems=(), after=None, new_sems=0):
    n, ns = len(arrays), len(sems)
    n_in = n + ns + (after is not None)

    def body(*refs):
        fn(refs[n_in:n_in + n], refs[n:n + ns], refs[n_in + n:-1])
        refs[-1][...] = jnp.zeros_like(refs[-1])

    ins = list(arrays) if ns else [_in_hbm(a) for a in arrays]
    ins += list(sems) + ([after] if after is not None else [])
    in_specs = [HBM] * n + [SEM] * ns + ([pl.BlockSpec(memory_space=pl.ANY)] if after is not None else [])
    out = _pallas(
        body, name=name, in_specs=in_specs,
        out_specs=[HBM] * n + [SEM] * new_sems + [pl.BlockSpec(memory_space=pltpu.VMEM)],
        out_shape=[pltpu.HBM(a.shape, a.dtype) for a in arrays] + [pltpu.SemaphoreType.DMA(())] * new_sems
        + [jax.ShapeDtypeStruct((8, 128), F32)],
        input_output_aliases={i: i for i in range(n)}, compiler_params=SPLIT_COPY,
    )(*ins)
    return list(out[:n]), tuple(out[n:-1]), out[-1]


def _remote(src, dst, sems, to):
    return pltpu.make_async_remote_copy(src_ref=src, dst_ref=dst, send_sem=sems[0], recv_sem=sems[1],
                                        device_id=to, device_id_type=MESH)


class _GradReduce:
    def __init__(self, name, geom, idx, c_idx):
        self.name, self.geom, self.idx, self.c_idx = name, geom, idx, c_idx

    def pair_start(self, dw):
        g = self.geom

        def start(refs, _, new):
            x, y, c = _mesh_pos()
            _remote(g.half_of_full(refs[0], 1 - c), refs[1], new, (x, y, 1 - c)).start()

        self.arrays, self.sems, token = _split_copy_call(
            f"pair_start_{self.name}", [dw, lax.empty(g.half_shape, BF16)], start, new_sems=2)
        return token

    def pair_finish(self, after):
        g = self.geom

        def wait(refs, sems, _):
            x, y, c = _mesh_pos()
            copy = _remote(g.half_of_full(refs[0], 1 - c), refs[1], sems, (x, y, 1 - c))
            copy.wait_send()
            copy.wait_recv()

        (dw, landed), _, _ = _split_copy_call(f"pair_wait_{self.name}", self.arrays, wait, self.sems, after)
        half = _pair_add(f"pair_add_{self.name}", dw, landed, g, self.c_idx)

        def start(refs, _, new):
            x, y, c = _mesh_pos()
            for k, chip in enumerate(_other_chips(x, y)):
                _remote(g.shard_of_half(refs[0], 2 * chip[0] + chip[1]), refs[1].at[k], new, (*chip, c)).start()

        self.arrays, self.sems, token = _split_copy_call(
            f"chip_start_{self.name}", [half, lax.empty((3,) + g.shard_half_shape, BF16)], start, new_sems=2)
        return token

    def chip_finish(self, after):
        g = self.geom

        def wait(refs, sems, _):
            x, y, c = _mesh_pos()
            three = _remote(refs[1], refs[1], sems, (x, y, 1 - c))
            three.wait_send()
            three.wait_recv()

        (half, landed), _, _ = _split_copy_call(f"chip_wait_{self.name}", self.arrays, wait, self.sems, after)
        quarter = _chip_add(f"chip_add_{self.name}", half, landed, g, self.idx)

        def start(refs, _, new):
            x, y, c = _mesh_pos()
            own = g.half_of_shard(refs[0], c)
            _remote(own, own, new, (x, y, 1 - c)).start()

        self.arrays, self.sems, token = _split_copy_call(f"share_start_{self.name}", [quarter], start, new_sems=2)
        return token

    def finish(self, after):
        g = self.geom

        def wait(refs, sems, _):
            x, y, c = _mesh_pos()
            own, theirs = g.half_of_shard(refs[0], c), g.half_of_shard(refs[0], 1 - c)
            _remote(own, own, sems, (x, y, 1 - c)).wait_send()
            _remote(theirs, theirs, sems, (x, y, 1 - c)).wait_recv()

        (quarter,), _, _ = _split_copy_call(f"share_wait_{self.name}", self.arrays, wait, self.sems, after)
        return quarter


def _pair_add(name, grad, recv, geom, c_idx):
    r, c = geom.half_shape
    tr, tc = _tile(r, 512, 16), _tile(c, 2048, 128)
    nr, ncol = r // tr, c // tc
    if geom.col:
        mine = lambda i, j, cref: (cref[0] * nr + i, j)
    else:
        mine = lambda i, j, cref: (i, cref[0] * ncol + j)

    def body(c_ref, g_ref, r_ref, o_ref):
        o_ref[...] = (g_ref[...].astype(F32) + r_ref[...].astype(F32)).astype(BF16)

    return _pallas(
        body, name=name,
        grid_spec=pltpu.PrefetchScalarGridSpec(
            num_scalar_prefetch=1, grid=(nr, ncol),
            in_specs=[pl.BlockSpec((tr, tc), mine), pl.BlockSpec((tr, tc), lambda i, j, cref: (i, j))],
            out_specs=pl.BlockSpec((tr, tc), lambda i, j, cref: (i, j))),
        out_shape=jax.ShapeDtypeStruct((r, c), BF16),
        compiler_params=_params(("parallel", "parallel"), 3 * _nbytes((tr, tc), F32)),
    )(c_idx, grad, recv)


def _chip_add(name, half, recv, geom, idx):
    r, c = geom.shard_half_shape
    tr, tc = _tile(r, 512, 16), _tile(c, 2048, 128)
    nr, ncol = r // tr, c // tc
    if geom.col:
        mine = lambda i, j, iref: (i, iref[0] * ncol + j)
        place = lambda i, j, iref: (iref[1] * nr + i, j)
    else:
        mine = lambda i, j, iref: (iref[0] * nr + i, j)
        place = lambda i, j, iref: (i, iref[1] * ncol + j)

    def body(i_ref, h_ref, r_ref, o_ref):
        acc = h_ref[...].astype(F32)
        for k in range(3):
            acc = acc + r_ref[k].astype(F32)
        o_ref[...] = acc

    return _pallas(
        body, name=name,
        grid_spec=pltpu.PrefetchScalarGridSpec(
            num_scalar_prefetch=1, grid=(nr, ncol),
            in_specs=[pl.BlockSpec((tr, tc), mine), pl.BlockSpec((3, tr, tc), lambda i, j, iref: (0, i, j))],
            out_specs=pl.BlockSpec((tr, tc), place)),
        out_shape=jax.ShapeDtypeStruct(geom.shard_shape, F32),
        compiler_params=_params(("parallel", "parallel"), 4 * _nbytes((tr, tc), F32)),
    )(idx, half, recv)


def _all_reduce_small(pack, after=None):
    r, d = pack.shape

    def body(p_ref, o_ref, slots, send_sems, recv_sems):
        x, y, c = _mesh_pos()
        me = 4 * x + 2 * y + c
        slots[me] = p_ref[...]
        copies = []
        for k in range(1, N_DEV):
            px, py, pc = x ^ ((k >> 2) & 1), y ^ ((k >> 1) & 1), c ^ (k & 1)
            copies.append(pltpu.make_async_remote_copy(
                src_ref=p_ref, dst_ref=slots.at[me], send_sem=send_sems.at[k - 1], recv_sem=recv_sems.at[k - 1],
                device_id=(px, py, pc), device_id_type=MESH))
        for cp in copies:
            cp.start()
        for k in range(1, N_DEV):
            peer = 4 * (x ^ ((k >> 2) & 1)) + 2 * (y ^ ((k >> 1) & 1)) + (c ^ (k & 1))
            pltpu.make_async_remote_copy(
                src_ref=p_ref, dst_ref=slots.at[peer], send_sem=send_sems.at[k - 1], recv_sem=recv_sems.at[k - 1],
                device_id=(x, y, c), device_id_type=MESH).wait_recv()
        for cp in copies:
            cp.wait_send()
        acc = slots[0]
        for k in range(1, N_DEV):
            acc = acc + slots[k]
        o_ref[...] = acc

    vm = pl.BlockSpec(memory_space=pltpu.VMEM)
    body, ins, in_specs = _ordered(body, [pack], [vm], after)
    return _pallas(
        body, name="all_reduce_small", in_specs=in_specs, out_specs=vm,
        out_shape=jax.ShapeDtypeStruct((r, d), F32),
        scratch_shapes=[pltpu.VMEM((N_DEV, r, d), F32), pltpu.SemaphoreType.DMA((N_DEV - 1,)),
                        pltpu.SemaphoreType.DMA((N_DEV - 1,))],
    )(*ins)


def _pack_rows(rows, d):
    out = []
    for a in rows:
        flat = a.reshape(-1)
        n = -(-flat.shape[0] // d) * d
        out.append(jnp.pad(flat, (0, n - flat.shape[0])).reshape(-1, d))
    packed = jnp.concatenate(out, axis=0)
    return jnp.pad(packed, ((0, 16 - packed.shape[0]), (0, 0)))


def _unpack_rows(packed, shapes, d):
    out, row = [], 0
    for shp in shapes:
        n = int(np.prod(shp))
        nrows = -(-n // d)
        out.append(packed[row:row + nrows].reshape(-1)[:n].reshape(shp))
        row += nrows
    return out


def kernel(x, pre_norm_ffn1, post_norm_ffn1, w_ffn1_gate_up, w_ffn1_down, pre_norm_mix, post_norm_mix, w_mix_in, hgrn_lower_bounds_fwd, hgrn_lower_bounds_bwd, hgrn_out_norm, attn_sink, w_mix_out, pre_norm_ffn2, post_norm_ffn2, w_ffn2_gate_up, w_ffn2_down, rel_bias_table, loss_target, m_pre_norm_ffn1, m_post_norm_ffn1, m_w_ffn1_gate_up, m_w_ffn1_down, m_pre_norm_mix, m_post_norm_mix, m_w_mix_in, m_hgrn_lower_bounds_fwd, m_hgrn_lower_bounds_bwd, m_hgrn_out_norm, m_attn_sink, m_w_mix_out, m_pre_norm_ffn2, m_post_norm_ffn2, m_w_ffn2_gate_up, m_w_ffn2_down, m_rel_bias_table, v_pre_norm_ffn1, v_post_norm_ffn1, v_w_ffn1_gate_up, v_w_ffn1_down, v_pre_norm_mix, v_post_norm_mix, v_w_mix_in, v_hgrn_lower_bounds_fwd, v_hgrn_lower_bounds_bwd, v_hgrn_out_norm, v_attn_sink, v_w_mix_out, v_pre_norm_ffn2, v_post_norm_ffn2, v_w_ffn2_gate_up, v_w_ffn2_down, v_rel_bias_table):
    t, d = x.shape[1], x.shape[2]
    hw = hgrn_out_norm.shape[1]
    aw = d - hw
    nah = aw // HEAD
    kvw = KV_HEADS * HEAD
    x0 = x[0]
    target = loss_target[0]

    big_names = ["w_ffn1_gate_up", "w_ffn1_down", "w_mix_in", "w_mix_out", "w_ffn2_gate_up", "w_ffn2_down"]
    big_w = [w_ffn1_gate_up[0], w_ffn1_down[0], w_mix_in[0], w_mix_out[0], w_ffn2_gate_up[0], w_ffn2_down[0]]
    big_m = [m_w_ffn1_gate_up[0], m_w_ffn1_down[0], m_w_mix_in[0], m_w_mix_out[0], m_w_ffn2_gate_up[0],
             m_w_ffn2_down[0]]
    big_v = [v_w_ffn1_gate_up[0], v_w_ffn1_down[0], v_w_mix_in[0], v_w_mix_out[0], v_w_ffn2_gate_up[0],
             v_w_ffn2_down[0]]
    col_sharded = [True, False, True, False, True, False]
    geoms = [_Big(w.shape, cs) for w, cs in zip(big_w, col_sharded)]

    cx, cy, cc = _mesh_pos()
    idx = jnp.stack([2 * cx + cy, cc]).astype(jnp.int32)
    c_idx = jnp.reshape(cc, (1,)).astype(jnp.int32)
    first = _cast_into_full(f"cast_{big_names[0]}", big_w[0], geoms[0], idx)
    started, gather_sems, tok = _gather_start("gather_start_first", [first], geoms[:1])
    rest = [_cast_into_full(f"cast_{n}", w, gm, idx, after=tok)
            for n, w, gm in zip(big_names[1:], big_w[1:], geoms[1:])]
    started_rest, sems_rest, _ = _gather_start("gather_start_rest", rest, geoms[1:])
    started, gather_sems = list(started) + list(started_rest), gather_sems + sems_rest

    def forward_weight(w, after):
        return _gather_forward(f"gather_forward_{big_names[w]}", started[w], geoms[w], gather_sems[w], after)

    def whole_weight(w, forwarded, after):
        return _gather_end(f"gather_end_{big_names[w]}", forwarded[0], geoms[w], forwarded[1], after)

    h1 = _norm_fwd("ffn1_pre_norm", x0, pre_norm_ffn1)
    w_gu1 = whole_weight(0, forward_weight(0, h1), h1)
    gate1, up1, act1 = _ffn_gate_up_act("ffn1_gate_up", h1, w_gu1)
    w_d1 = whole_weight(1, forward_weight(1, act1), act1)
    ff1 = _mm("ffn1_down", act1, w_d1, "nn", F32)
    fw = forward_weight(2, ff1)
    x1, hm = _resid_norm_fwd("ffn1_residual", x0, ff1, post_norm_ffn1, pre_norm_mix, 0.5)
    w_in = whole_weight(2, fw, hm)
    p = _mm("mix_in", hm, w_in, "nn", F32)
    fw = forward_weight(3, p)
    o_f, o_b, st_f, st_b = _hgrn_scan_fwd("hgrn_scan", p, hgrn_lower_bounds_fwd, hgrn_lower_bounds_bwd)
    y_h = _hgrn_out_fwd("hgrn_out", o_f, o_b, p, hgrn_out_norm, 4)
    kv_blk0 = (5 * hw + aw) // kvw
    k_pad = _pad_kv("attn_pad_k", p, kv_blk0, kvw)
    v_pad = _pad_kv("attn_pad_v", p, kv_blk0 + 1, kvw)
    bucket_ids = _t5_bucket_ids()
    bias = _bias_gather("attn_bias", rel_bias_table.T, bucket_ids).reshape(nah, WINDOW, SPAN)
    y_a = _attn_fwd("attn_fwd", p, k_pad, v_pad, bias, attn_sink, 5 * hw // aw)
    y_mix = _concat_cols("mix_concat", y_h, y_a)
    w_out = whole_weight(3, fw, y_mix)
    mixed = _mm("mix_out", y_mix, w_out, "nn", F32)
    fw = forward_weight(4, mixed)
    x2, h2 = _resid_norm_fwd("mix_residual", x1, mixed, post_norm_mix, pre_norm_ffn2, 1.0)
    w_gu2 = whole_weight(4, fw, h2)
    gate2, up2, act2 = _ffn_gate_up_act("ffn2_gate_up", h2, w_gu2)
    w_d2 = whole_weight(5, forward_weight(5, act2), act2)
    ff2 = _mm("ffn2_down", act2, w_d2, "nn", F32)
    loss_blk, dy, dff2, dg_post2 = _final_fwd_bwd("ffn2_residual_loss", x2, ff2, post_norm_ffn2, target, 0.5)

    reduce = [_GradReduce(n, gm, idx, c_idx) for n, gm in zip(big_names, geoms)]
    big_grads, big_delta, big_new_m, big_new_v = [None] * 6, [None] * 6, [None] * 6, [None] * 6

    def update(w, after):
        g, dl, nm, nv = _adamw(f"adamw_{big_names[w]}", big_w[w], reduce[w].finish(after), big_m[w], big_v[w])
        big_grads[w], big_delta[w], big_new_m[w], big_new_v[w] = g[None], dl[None], nm[None], nv[None]
        return dl

    dw_d2 = _mm("ffn2_dw_down", act2, dff2, "tn", BF16)
    tok = reduce[5].pair_start(dw_d2)
    dgu2 = _ffn_dact("ffn2_dact", dff2, w_d2, gate2, up2, after=tok)
    tok = reduce[5].pair_finish(dgu2)
    dw_gu2 = _ffn_dw_gate_up("ffn2_dw_gate_up", h2, dgu2, after=tok)
    tok = reduce[4].pair_start(dw_gu2)
    dh2 = _ffn_dh("ffn2_dh", dgu2, w_gu2, after=tok)
    tok = reduce[4].pair_finish(dh2)
    dx2, dg_pre2, dmixed, dg_postm = _norms_bwd("mix_residual_bwd", dy, dh2, x2, pre_norm_ffn2,
                                                post=(mixed, post_norm_mix, 1.0), after=tok)
    dw_out = _mm("mix_out_dw", y_mix, dmixed, "tn", BF16)
    tok = reduce[3].pair_start(dw_out)
    dy_mix = _mm("mix_out_dx", dmixed, w_out, "nt", F32, after=tok)
    tok = reduce[3].pair_finish(dy_mix)
    dq_a, dk_pad, dv_pad, dbias, dsink = _attn_bwd("attn_bwd", p, k_pad, v_pad, bias, attn_sink, dy_mix,
                                                   5 * hw // aw, hw // aw, after=tok)
    tok = reduce[5].chip_finish(dq_a)
    drel_t = _bias_scatter("attn_dbias", dbias.reshape(nah, WINDOW * SPAN), bucket_ids)
    do, dg_h, dgain = _hgrn_out_bwd("hgrn_out_bwd", dy_mix, o_f, o_b, p, hgrn_out_norm, 4, after=tok)
    dq_f, dv_f, dz_f, dlb_f, dq_b, dv_b, dz_b, dlb_b = _hgrn_scan_bwd(
        "hgrn_scan_bwd", p, hgrn_lower_bounds_fwd, hgrn_lower_bounds_bwd, do, st_f, st_b)
    tok = reduce[4].chip_finish(dq_f)
    tok = reduce[3].chip_finish(tok)
    dp = _mix_dproj("mix_dproj", [(dq_f, dq_b), (dv_f, dv_b), (dz_f,), (dz_b,), (dg_h,), (dq_a,)],
                    [dk_pad, dv_pad], t, after=tok)
    dw_in = _mm("mix_in_dw", hm, dp, "tn", BF16)
    tok = reduce[2].pair_start(dw_in)
    dhm = _mm("mix_in_dx", dp, w_in, "nt", F32, after=tok)
    tok = reduce[2].pair_finish(dhm)
    dx1, dg_prem, dff1, dg_post1 = _norms_bwd("ffn1_residual_bwd", dx2, dhm, x1, pre_norm_mix,
                                              post=(ff1, post_norm_ffn1, 0.5), after=tok)
    dw_d1 = _mm("ffn1_dw_down", act1, dff1, "tn", BF16)
    tok = reduce[1].pair_start(dw_d1)
    dgu1 = _ffn_dact("ffn1_dact", dff1, w_d1, gate1, up1, after=tok)
    tok = reduce[1].pair_finish(dgu1)
    tok = reduce[2].chip_finish(tok)
    dw_gu1 = _ffn_dw_gate_up("ffn1_dw_gate_up", h1, dgu1, after=tok)
    tok = reduce[0].pair_start(dw_gu1)
    done = update(2, tok)
    tok = reduce[0].pair_finish(done)
    dh1 = _ffn_dh("ffn1_dh", dgu1, w_gu1, after=tok)
    grad_x, dg_pre1 = _norms_bwd("ffn1_pre_norm_bwd", dx1, dh1, x0, pre_norm_ffn1)

    small_w = [pre_norm_ffn1, post_norm_ffn1, pre_norm_mix, post_norm_mix, hgrn_lower_bounds_fwd,
               hgrn_lower_bounds_bwd, hgrn_out_norm, attn_sink, pre_norm_ffn2, post_norm_ffn2, rel_bias_table]
    small_m = [m_pre_norm_ffn1, m_post_norm_ffn1, m_pre_norm_mix, m_post_norm_mix, m_hgrn_lower_bounds_fwd,
               m_hgrn_lower_bounds_bwd, m_hgrn_out_norm, m_attn_sink, m_pre_norm_ffn2, m_post_norm_ffn2,
               m_rel_bias_table]
    small_v = [v_pre_norm_ffn1, v_post_norm_ffn1, v_pre_norm_mix, v_post_norm_mix, v_hgrn_lower_bounds_fwd,
               v_hgrn_lower_bounds_bwd, v_hgrn_out_norm, v_attn_sink, v_pre_norm_ffn2, v_post_norm_ffn2,
               v_rel_bias_table]
    small_g = [dg_pre1, dg_post1, dg_prem, dg_postm, dlb_f, dlb_b, dgain, dsink[:, 0].reshape(1, nah), dg_pre2,
               dg_post2, drel_t.T]
    shapes = [a.shape for a in small_w]
    done = update(5, grad_x)
    done = update(4, done)
    done = update(3, done)
    summed = _all_reduce_small(_pack_rows(small_g + [loss_blk[0:1, 0:1]], d), after=done)
    loss = _unpack_rows(summed, shapes + [(1, 1)], d)[-1][0, 0]
    _, sd, sm, sv = _adamw("adamw_small", _pack_rows(small_w, d), summed, _pack_rows(small_m, d),
                           _pack_rows(small_v, d))
    small_grads = _unpack_rows(summed, shapes, d)
    small_delta, small_new_m, small_new_v = (_unpack_rows(a, shapes, d) for a in (sd, sm, sv))

    tok = reduce[1].chip_finish(sd)
    done = update(1, tok)
    tok = reduce[0].chip_finish(done)
    update(0, tok)

    def ordered(small, big):
        s = dict(zip(["pre1", "post1", "prem", "postm", "lbf", "lbb", "gain", "sink", "pre2", "post2", "rel"], small))
        b = dict(zip(["gu1", "d1", "win", "wout", "gu2", "d2"], big))
        return [s["pre1"], s["post1"], b["gu1"], b["d1"], s["prem"], s["postm"], b["win"], s["lbf"], s["lbb"],
                s["gain"], s["sink"], b["wout"], s["pre2"], s["post2"], b["gu2"], b["d2"], s["rel"]]

    return (loss, grad_x[None], *ordered(small_grads, big_grads), *ordered(small_delta, big_delta),
            *ordered(small_new_m, big_new_m), *ordered(small_new_v, big_new_v))
```

```python
import functools
import math

import jax
import jax.numpy as jnp
import numpy as np
from jax import lax
from jax.experimental import pallas as pl
from jax.experimental.pallas import tpu as pltpu

F32 = jnp.float32
BF16 = jnp.bfloat16

HEAD = 128
CHUNK = 64
WINDOW = 128
SPAN = 3 * WINDOW
KV_HEADS = 2
REL_BUCKETS = 32
REL_MAX_DIST = 128
EPS = 1e-6
NEG_INF = -1e30

ADAM_LR = 0.001
ADAM_B1 = 0.9
ADAM_B2 = 0.999
ADAM_EPS = 1e-08
ADAM_WD = 0.01
ADAM_STEP = 10

N_CHIPS = 4
N_DEV = 8
V7X_VMEM_BYTES = 64 * 1024 * 1024
MESH = pl.DeviceIdType.MESH
ANY = pl.BlockSpec(memory_space=pl.ANY)


def _tile(n, pref, mult):
    t = (min(pref, n) // mult) * mult
    while t >= mult:
        if n % t == 0:
            return t
        t -= mult
    return n


def _params(semantics, block_bytes):
    limit = min(V7X_VMEM_BYTES - (4 << 20), 2 * int(block_bytes) + (8 << 20))
    return pltpu.CompilerParams(dimension_semantics=semantics, vmem_limit_bytes=limit)


def _nbytes(shape, dtype):
    return int(np.prod(shape)) * jnp.dtype(dtype).itemsize


PIN_TO_HBM_BYTES = 4 << 20


def _pallas(body, **kw):
    def pin_shape(s):
        if isinstance(s, jax.ShapeDtypeStruct) and _nbytes(s.shape, s.dtype) >= PIN_TO_HBM_BYTES:
            return pltpu.HBM(s.shape, s.dtype)
        return s

    def pin(a):
        if getattr(a, "dtype", None) in (F32, BF16) and _nbytes(a.shape, a.dtype) >= PIN_TO_HBM_BYTES:
            return pltpu.with_memory_space_constraint(a, pltpu.HBM)
        return a

    out_shape = kw["out_shape"]
    kw["out_shape"] = [pin_shape(s) for s in out_shape] if isinstance(out_shape, (list, tuple)) else pin_shape(out_shape)
    call = pl.pallas_call(body, **kw)
    return lambda *args: call(*[pin(a) for a in args])


def _dot(a, b, ca=1, cb=0):
    return lax.dot_general(a, b, (((ca,), (cb,)), ((), ())), preferred_element_type=F32)


def _split3(x):
    hi = x.astype(BF16)
    r1 = x - hi.astype(F32)
    mid = r1.astype(BF16)
    lo = (r1 - mid.astype(F32)).astype(BF16)
    return hi, mid, lo


def _dot_exact(a, b, ca=1, cb=0, split="b"):
    if split == "b":
        return sum(_dot(a, p, ca, cb) for p in _split3(b))
    return sum(_dot(p, b, ca, cb) for p in _split3(a))


def _rms(x):
    return lax.rsqrt(jnp.mean(x * x, axis=-1, keepdims=True) + EPS)


def _norm_bwd(u, x, gain):
    r = _rms(x)
    xhat = x * r
    dgain = jnp.sum(u * xhat, axis=0, keepdims=True)
    v = u * gain
    dx = r * (v - xhat * jnp.mean(v * xhat, axis=-1, keepdims=True))
    return dx, dgain


def _sigmoid(x):
    return 1.0 / (1.0 + jnp.exp(-x))


def _accumulate(ref, val, first):
    @pl.when(first)
    def _():
        ref[...] = val

    @pl.when(jnp.logical_not(first))
    def _():
        ref[...] += val


def _ordered(body, ins, in_specs, after):
    if after is None:
        return body, list(ins), list(in_specs)
    n_in = len(ins)

    def wrapped(*refs):
        body(*refs[:n_in], *refs[n_in + 1:])

    return wrapped, list(ins) + [after], list(in_specs) + [pl.BlockSpec(memory_space=pl.ANY)]


def _matmul(name, a, b, *, form, out_dtype, tm, tn, tk, a_map=None, b_map=None,
            out_shape=None, out_block=None, out_map=None, sizes=None, after=None):
    if sizes is None:
        if form == "nn":
            (m, k), n = a.shape, b.shape[1]
        elif form == "nt":
            (m, k), n = a.shape, b.shape[0]
        else:
            (k, m), n = a.shape, b.shape[1]
    else:
        m, n, k = sizes
    gi, gj, gk = m // tm, n // tn, k // tk
    a_blk = (tm, tk) if form != "tn" else (tk, tm)
    b_blk = (tk, tn) if form != "nt" else (tn, tk)
    if a_map is None:
        a_map = (lambda i, j, kk: (i, kk)) if form != "tn" else (lambda i, j, kk: (kk, i))
    else:
        a_blk = (None,) + a_blk
    if b_map is None:
        b_map = (lambda i, j, kk: (kk, j)) if form != "nt" else (lambda i, j, kk: (j, kk))
    else:
        b_blk = (None,) + b_blk
    if out_shape is None:
        out_shape, out_block, out_map = (m, n), (tm, tn), (lambda i, j, kk: (i, j))
    ca, cb = {"nn": (1, 0), "nt": (1, 1), "tn": (0, 0)}[form]

    def body(a_ref, b_ref, o_ref, *acc):
        part = _dot(a_ref[...], b_ref[...], ca, cb)
        if gk == 1:
            o_ref[...] = part.astype(o_ref.dtype)
        else:
            kk = pl.program_id(2)
            _accumulate(acc[0], part, kk == 0)

            @pl.when(kk == gk - 1)
            def _():
                o_ref[...] = acc[0][...].astype(o_ref.dtype)

    scratch = [] if gk == 1 else [pltpu.VMEM((tm, tn), F32)]
    vmem = (_nbytes((tm, tk), a.dtype) + _nbytes((tk, tn), b.dtype) + _nbytes((tm, tn), out_dtype)
            + 2 * _nbytes((tm, tn), F32))
    body, ins, in_specs = _ordered(body, [a, b], [pl.BlockSpec(a_blk, a_map), pl.BlockSpec(b_blk, b_map)], after)
    return _pallas(
        body, name=name, grid=(gi, gj, gk), in_specs=in_specs,
        out_specs=pl.BlockSpec(out_block, out_map),
        out_shape=jax.ShapeDtypeStruct(out_shape, out_dtype),
        scratch_shapes=scratch,
        compiler_params=_params(("parallel", "parallel", "arbitrary"), vmem),
    )(*ins)


V7X_HBM_BYTES_PER_US = 3.0e6
V7X_MXU_FLOPS_PER_US = 0.9e9
V7X_VMEM_RMW_BYTES_PER_US = 10e6
GRID_STEP_US = 0.35
MATMUL_VMEM_BUDGET = 40 << 20
MATMUL_MAX_TILE_FLOPS = 1 << 33


def _divisors(n, mult, lo):
    return [t for t in range(mult, n + 1, mult) if n % t == 0 and t >= min(lo, n)]


def _mm_tiles(m, n, k, out_dtype=F32, n_unit=None, k_unit=None):
    out_bytes = jnp.dtype(out_dtype).itemsize
    best = None
    for tm in _divisors(m, 128, 256):
        for tn in _divisors(n_unit or n, 128, 256):
            for tk in _divisors(k_unit or k, 128, 512):
                gi, gj, gk = m // tm, n // tn, k // tk
                vmem = 4 * tm * tk + 4 * tk * tn + 2 * tm * tn * out_bytes + 4 * tm * tn * (2 if gk > 1 else 1)
                if vmem > MATMUL_VMEM_BUDGET or 2 * tm * tn * tk > MATMUL_MAX_TILE_FLOPS:
                    continue
                a_bytes = 2 * m * k * (gj if gk > 1 else 1)
                b_bytes = 2 * k * n * (1 if gj == 1 and gk == 1 else gi)
                hbm_us = (a_bytes + b_bytes + m * n * out_bytes) / V7X_HBM_BYTES_PER_US
                acc_us = (8 * m * n * gk / V7X_VMEM_RMW_BYTES_PER_US) if gk > 1 else 0.0
                cost = max(2 * m * n * k / V7X_MXU_FLOPS_PER_US, 1.3 * hbm_us) + GRID_STEP_US * gi * gj * gk + acc_us
                key = (round(cost, 1), vmem)
                if best is None or key < best[0]:
                    best = (key, (tm, tn, tk))
    return best[1]


def _mm(name, a, b, form, out_dtype, after=None):
    if form == "nn":
        m, k, n = a.shape[0], a.shape[1], b.shape[1]
    elif form == "nt":
        m, k, n = a.shape[0], a.shape[1], b.shape[0]
    else:
        m, k, n = a.shape[1], a.shape[0], b.shape[1]
    tm, tn, tk = _mm_tiles(m, n, k, out_dtype)
    return _matmul(name, a, b, form=form, out_dtype=out_dtype, tm=tm, tn=tn, tk=tk, after=after)


def _row_tile(t):
    return _tile(t, 256, 8)


def _norm_fwd(name, x, gain):
    t, d = x.shape
    tm = _row_tile(t)

    def body(x_ref, g_ref, h_ref):
        xv = x_ref[...]
        h_ref[...] = (xv * _rms(xv) * g_ref[...]).astype(BF16)

    row = pl.BlockSpec((tm, d), lambda i: (i, 0))
    vec = pl.BlockSpec((1, d), lambda i: (0, 0))
    return _pallas(
        body, name=name, grid=(t // tm,), in_specs=[row, vec], out_specs=row,
        out_shape=jax.ShapeDtypeStruct((t, d), BF16),
        compiler_params=_params(("parallel",), 2 * _nbytes((tm, d), F32)),
    )(x, gain)


def _resid_norm_fwd(name, xres, ff, gpost, gpre, scale):
    t, d = xres.shape
    tm = _row_tile(t)

    def body(x_ref, f_ref, gp_ref, gn_ref, xn_ref, h_ref):
        f = f_ref[...]
        xn = x_ref[...] + scale * (f * _rms(f) * gp_ref[...])
        xn_ref[...] = xn
        h_ref[...] = (xn * _rms(xn) * gn_ref[...]).astype(BF16)

    row = pl.BlockSpec((tm, d), lambda i: (i, 0))
    vec = pl.BlockSpec((1, d), lambda i: (0, 0))
    return _pallas(
        body, name=name, grid=(t // tm,), in_specs=[row, row, vec, vec], out_specs=[row, row],
        out_shape=[jax.ShapeDtypeStruct((t, d), F32), jax.ShapeDtypeStruct((t, d), BF16)],
        compiler_params=_params(("parallel",), 4 * _nbytes((tm, d), F32)),
    )(xres, ff, gpost, gpre)


def _final_fwd_bwd(name, xres, ff, gpost, target, scale):
    t, d = xres.shape
    tm = _row_tile(t)

    def body(x_ref, f_ref, gp_ref, t_ref, loss_ref, dy_ref, dff_ref, dg_ref):
        i = pl.program_id(0)
        f = f_ref[...]
        gp = gp_ref[...]
        y = x_ref[...] + scale * (f * _rms(f) * gp)
        err = y - t_ref[...]
        part = 0.5 * jnp.sum(jnp.mean(err * err, axis=-1, keepdims=True), axis=0, keepdims=True)
        _accumulate(loss_ref, jnp.broadcast_to(part, loss_ref.shape), i == 0)
        dy = err / d
        dy_ref[...] = dy
        dff, dg = _norm_bwd(scale * dy, f, gp)
        dff_ref[...] = dff.astype(BF16)
        _accumulate(dg_ref, dg, i == 0)

    row = pl.BlockSpec((tm, d), lambda i: (i, 0))
    vec = pl.BlockSpec((1, d), lambda i: (0, 0))
    return _pallas(
        body, name=name, grid=(t // tm,), in_specs=[row, row, vec, row],
        out_specs=[pl.BlockSpec((8, 128), lambda i: (0, 0)), row, row, vec],
        out_shape=[jax.ShapeDtypeStruct((8, 128), F32), jax.ShapeDtypeStruct((t, d), F32),
                   jax.ShapeDtypeStruct((t, d), BF16), jax.ShapeDtypeStruct((1, d), F32)],
        compiler_params=_params(("arbitrary",), 5 * _nbytes((tm, d), F32)),
    )(xres, ff, gpost, target)


def _norms_bwd(name, dres, dh, xin, gpre, post=None, after=None):
    t, d = dres.shape
    tm = _row_tile(t)
    with_post = post is not None

    def body(*refs):
        if with_post:
            dr_ref, dh_ref, x_ref, g_ref, f_ref, gp_ref, dx_ref, dg_ref, dff_ref, dgp_ref = refs
        else:
            dr_ref, dh_ref, x_ref, g_ref, dx_ref, dg_ref = refs
        i = pl.program_id(0)
        dx, dg = _norm_bwd(dh_ref[...], x_ref[...], g_ref[...])
        dx = dr_ref[...] + dx
        dx_ref[...] = dx
        _accumulate(dg_ref, dg, i == 0)
        if with_post:
            dff, dgp = _norm_bwd(post[2] * dx, f_ref[...], gp_ref[...])
            dff_ref[...] = dff.astype(BF16)
            _accumulate(dgp_ref, dgp, i == 0)

    row = pl.BlockSpec((tm, d), lambda i: (i, 0))
    vec = pl.BlockSpec((1, d), lambda i: (0, 0))
    ins, in_specs = [dres, dh, xin, gpre], [row, row, row, vec]
    out_specs = [row, vec]
    out_shape = [jax.ShapeDtypeStruct((t, d), F32), jax.ShapeDtypeStruct((1, d), F32)]
    if with_post:
        ins += [post[0], post[1]]
        in_specs += [row, vec]
        out_specs += [row, vec]
        out_shape += [jax.ShapeDtypeStruct((t, d), BF16), jax.ShapeDtypeStruct((1, d), F32)]
    body, ins, in_specs = _ordered(body, ins, in_specs, after)
    return _pallas(
        body, name=name, grid=(t // tm,), in_specs=in_specs, out_specs=out_specs, out_shape=out_shape,
        compiler_params=_params(("arbitrary",), 6 * _nbytes((tm, d), F32)),
    )(*ins)


SWIGLU_TILE = (1024, 512)
V7X_MXU_COLS = 256


def _ffn_gate_up_act(name, h, w_gu):
    t, d = h.shape
    f = w_gu.shape[1] // 2
    tm, tn = _tile(t, SWIGLU_TILE[0], 128), _tile(f, SWIGLU_TILE[1], 128)
    nf = f // tn

    def body(h_ref, wg_ref, wu_ref, g_ref, u_ref, a_ref):
        hv = h_ref[...]
        for c0 in range(0, tn, min(tn, V7X_MXU_COLS)):
            cols = slice(c0, c0 + min(tn, V7X_MXU_COLS))
            g = _dot(hv, wg_ref[:, cols])
            u = _dot(hv, wu_ref[:, cols])
            g_ref[:, cols] = g.astype(BF16)
            u_ref[:, cols] = u.astype(BF16)
            a_ref[:, cols] = (g * _sigmoid(g) * u).astype(BF16)

    out = jax.ShapeDtypeStruct((t, f), BF16)
    blk = pl.BlockSpec((tm, tn), lambda i, j: (i, j))
    return _pallas(
        body, name=name, grid=(t // tm, nf),
        in_specs=[pl.BlockSpec((tm, d), lambda i, j: (i, 0)), pl.BlockSpec((d, tn), lambda i, j: (0, j)),
                  pl.BlockSpec((d, tn), lambda i, j: (0, j + nf))],
        out_specs=[blk, blk, blk], out_shape=[out, out, out],
        compiler_params=_params(("parallel", "parallel"),
                                _nbytes((tm, d), BF16) + 2 * _nbytes((d, tn), BF16) + 5 * _nbytes((tm, tn), F32)),
    )(h, w_gu, w_gu)


def _ffn_dact(name, dff, w_down, gate, up, after=None):
    t, d = dff.shape
    f = w_down.shape[0]
    tm, tn = _tile(t, SWIGLU_TILE[0], 128), _tile(f, SWIGLU_TILE[1], 128)

    def body(d_ref, w_ref, g_ref, u_ref, o_ref):
        dv = d_ref[...]
        for c0 in range(0, tn, min(tn, V7X_MXU_COLS)):
            cols = slice(c0, c0 + min(tn, V7X_MXU_COLS))
            da = _dot(dv, w_ref[cols, :], 1, 1)
            g = g_ref[:, cols].astype(F32)
            u = u_ref[:, cols].astype(F32)
            sig = _sigmoid(g)
            o_ref[0, :, cols] = (da * u * sig * (1.0 + g * (1.0 - sig))).astype(BF16)
            o_ref[1, :, cols] = (da * g * sig).astype(BF16)

    blk = pl.BlockSpec((tm, tn), lambda i, j: (i, j))
    body, ins, in_specs = _ordered(
        body, [dff, w_down, gate, up],
        [pl.BlockSpec((tm, d), lambda i, j: (i, 0)), pl.BlockSpec((tn, d), lambda i, j: (j, 0)), blk, blk], after)
    return _pallas(
        body, name=name, grid=(t // tm, f // tn), in_specs=in_specs,
        out_specs=pl.BlockSpec((2, tm, tn), lambda i, j: (0, i, j)),
        out_shape=jax.ShapeDtypeStruct((2, t, f), BF16),
        compiler_params=_params(("parallel", "parallel"),
                                _nbytes((tm, d), BF16) + _nbytes((tn, d), BF16) + 5 * _nbytes((tm, tn), F32)),
    )(*ins)


def _ffn_dh(name, dgu, w_gu, after=None):
    _, t, f = dgu.shape
    d = w_gu.shape[0]
    tm, tn, tk = _mm_tiles(t, d, 2 * f, F32, k_unit=f)
    nkf = f // tk
    return _matmul(name, dgu, w_gu, form="nt", out_dtype=F32, tm=tm, tn=tn, tk=tk, sizes=(t, d, 2 * f),
                   a_map=lambda i, j, kk: (kk // nkf, i, kk % nkf), after=after)


def _ffn_dw_gate_up(name, h, dgu, after=None):
    _, t, f = dgu.shape
    d = h.shape[1]
    tm, tn, tk = _mm_tiles(d, 2 * f, t, BF16, n_unit=f)
    nf = f // tn
    return _matmul(name, h, dgu, form="tn", out_dtype=BF16, tm=tm, tn=tn, tk=tk, sizes=(d, 2 * f, t),
                   b_map=lambda i, j, kk: (j // nf, kk, j % nf), after=after)


def _lower_bound(lbp):
    m = jnp.max(lbp, axis=0, keepdims=True)
    e = jnp.exp(lbp - m)
    return e[0:1] / jnp.sum(e, axis=0, keepdims=True)


def _chunk_mask(reverse):
    row = lax.broadcasted_iota(jnp.int32, (CHUNK, CHUNK), 0)
    col = lax.broadcasted_iota(jnp.int32, (CHUNK, CHUNK), 1)
    return (col >= row) if reverse else (col <= row)


def _hgrn_gates(z, lb, mask_bf):
    sig = _sigmoid(z)
    f = lb + (1.0 - lb) * sig
    logf = jnp.log(f)
    k = 1.0 - f
    cum = _dot_exact(mask_bf, logf)
    last = jnp.sum(logf, axis=0, keepdims=True)
    return sig, f, k, cum, last


def _hgrn_scan_fwd(name, p, lbp_f, lbp_b):
    t = p.shape[0]
    hw = lbp_f.shape[1]
    nh, nc = hw // HEAD, t // CHUNK

    def body(qf, vf, zf, qb, vb, zb, lbf, lbb, of_ref, ob_ref, stf_ref, stb_ref, state):
        n = pl.program_id(0)

        @pl.when(n == 0)
        def _():
            state[...] = jnp.zeros_like(state)

        directions = [(qf, vf, zf, lbf, of_ref, stf_ref), (qb, vb, zb, lbb, ob_ref, stb_ref)]
        wide = []
        for d, (q_ref, v_ref, z_ref, lb_ref, o_ref, st_ref) in enumerate(directions):
            mask = _chunk_mask(d == 1)
            lb = _lower_bound(lb_ref[...])
            _, _, k, cum, last = _hgrn_gates(z_ref[...], lb, mask.astype(BF16))
            v = v_ref[...].astype(BF16)
            qd = (q_ref[...] * jnp.exp(cum)).astype(BF16)
            kd = (k * jnp.exp(-cum)).astype(BF16)
            kt = (k * jnp.exp(last - cum)).astype(BF16)
            s_all = state[d]
            st_ref[...] = s_all
            wide.append((mask, v, qd, kd, kt, jnp.exp(last), s_all, o_ref))
        pairs = [(d, slice(h * HEAD, (h + 1) * HEAD)) for d in range(2) for h in range(nh)]
        a = [jnp.where(wide[d][0], _dot(wide[d][2][:, sl], wide[d][3][:, sl], 1, 1), 0.0).astype(BF16)
             for d, sl in pairs]
        inter = [_dot(wide[d][2][:, sl], wide[d][6][:, sl].astype(BF16), 1, 1) for d, sl in pairs]
        intra = [_dot(a[i], wide[d][1][:, sl]) for i, (d, sl) in enumerate(pairs)]
        grow = [_dot(wide[d][1][:, sl], wide[d][4][:, sl], 0, 0) for d, sl in pairs]
        for i, (d, sl) in enumerate(pairs):
            wide[d][7][:, sl] = intra[i] + inter[i]
            state[d, :, sl] = wide[d][6][:, sl] * wide[d][5][:, sl] + grow[i]

    def col(group, reverse):
        return pl.BlockSpec((CHUNK, hw), lambda n: ((nc - 1 - n) if reverse else n, group))

    def st(reverse):
        return pl.BlockSpec((None, HEAD, hw), lambda n: ((nc - 1 - n) if reverse else n, 0, 0))

    lb_spec = pl.BlockSpec((2, hw), lambda n: (0, 0))
    out = jax.ShapeDtypeStruct((t, hw), F32)
    states = jax.ShapeDtypeStruct((nc, HEAD, hw), F32)
    return _pallas(
        body, name=name, grid=(nc,),
        in_specs=[col(0, False), col(1, False), col(2, False), col(0, True), col(1, True), col(3, True),
                  lb_spec, lb_spec],
        out_specs=[col(0, False), col(0, True), st(False), st(True)],
        out_shape=[out, out, states, states],
        scratch_shapes=[pltpu.VMEM((2, HEAD, hw), F32)],
        compiler_params=_params(("arbitrary",), 12 * _nbytes((HEAD, hw), F32)),
    )(p, p, p, p, p, p, lbp_f, lbp_b)


def _hgrn_scan_bwd(name, p, lbp_f, lbp_b, do, st_f, st_b):
    t = p.shape[0]
    hw = lbp_f.shape[1]
    nh, nc = hw // HEAD, t // CHUNK

    def body(qf, vf, zf, dof, sf, qb, vb, zb, dob, sb, lbf, lbb, dqf, dvf, dzf, dlbf, dqb, dvb, dzb, dlbb,
             dstate, dlb_acc, dqd_s, dkd_s, dkt_s, ddec_s):
        n = pl.program_id(0)

        @pl.when(n == 0)
        def _():
            dstate[...] = jnp.zeros_like(dstate)
            dlb_acc[...] = jnp.zeros_like(dlb_acc)

        directions = [(qf, vf, zf, dof, sf, lbf, dqf, dvf, dzf, dlbf), (qb, vb, zb, dob, sb, lbb, dqb, dvb, dzb, dlbb)]
        for d, (q_ref, v_ref, z_ref, do_ref, st_ref, lb_ref, dq_ref, dv_ref, dz_ref, dlb_ref) in enumerate(directions):
            mask = _chunk_mask(d == 1)
            mask_bf = mask.astype(BF16)
            lb = _lower_bound(lb_ref[...])
            sig, f, k, cum, last = _hgrn_gates(z_ref[...], lb, mask_bf)
            e_pos, e_neg, e_tail = jnp.exp(cum), jnp.exp(-cum), jnp.exp(last - cum)
            dec = jnp.exp(last)
            v = v_ref[...].astype(BF16)
            qd, kd, kt = q_ref[...] * e_pos, k * e_neg, k * e_tail
            qd_bf, kd_bf, kt_bf = qd.astype(BF16), kd.astype(BF16), kt.astype(BF16)
            s_all = st_ref[...]
            ds_all = dstate[d]
            dov = do_ref[...].astype(BF16)
            cols = [slice(h * HEAD, (h + 1) * HEAD) for h in range(nh)]
            s_bf = [s_all[:, sl].astype(BF16) for sl in cols]
            ds_bf = [ds_all[:, sl].astype(BF16) for sl in cols]
            a = [jnp.where(mask, _dot(qd_bf[:, sl], kd_bf[:, sl], 1, 1), 0.0).astype(BF16) for sl in cols]
            da = [jnp.where(mask, _dot(dov[:, sl], v[:, sl], 1, 1), 0.0).astype(BF16) for sl in cols]
            dv_h = [_dot(a[h], dov[:, sl], 0, 0) + _dot(kt_bf[:, sl], ds_bf[h], 1, 1) for h, sl in enumerate(cols)]
            dqd_h = [_dot(da[h], kd_bf[:, sl]) + _dot(dov[:, sl], s_bf[h]) for h, sl in enumerate(cols)]
            dkd_h = [_dot(da[h], qd_bf[:, sl], 0, 0) for h, sl in enumerate(cols)]
            dkt_h = [_dot(v[:, sl], ds_bf[h]) for h, sl in enumerate(cols)]
            dst_h = [_dot(dov[:, sl], qd_bf[:, sl], 0, 0) + ds_all[:, sl] * dec[:, sl] for sl in cols]
            for h, sl in enumerate(cols):
                dv_ref[:, sl] = dv_h[h]
                dqd_s[:, sl] = dqd_h[h]
                dkd_s[:, sl] = dkd_h[h]
                dkt_s[:, sl] = dkt_h[h]
                dstate[d, :, sl] = dst_h[h]
                ddec_s[:, sl] = jnp.sum(ds_all[:, sl] * s_all[:, sl], axis=0, keepdims=True)
            dqd, dkd, dkt = dqd_s[...], dkd_s[...], dkt_s[...]
            dlast = jnp.sum(dkt * kt, axis=0, keepdims=True) + dec * ddec_s[...]
            dq_ref[...] = dqd * e_pos
            dk = dkd * e_neg + dkt * e_tail
            dcum = dqd * qd - dkd * kd - dkt * kt
            dlogf = _dot_exact(mask_bf, dcum, 0, 0) + dlast
            df = dlogf / f - dk
            dz_ref[...] = df * (1.0 - lb) * sig * (1.0 - sig)
            dlb_acc[d] += jnp.sum(df * (1.0 - sig), axis=0, keepdims=True)

            @pl.when(n == nc - 1)
            def _():
                g = dlb_acc[d] * lb * (1.0 - lb)
                dlb_ref[0:1, :] = g
                dlb_ref[1:2, :] = -g

    def col(group, reverse):
        return pl.BlockSpec((CHUNK, hw), lambda n: (n if reverse else (nc - 1 - n), group))

    def st(reverse):
        return pl.BlockSpec((None, HEAD, hw), lambda n: (n if reverse else (nc - 1 - n), 0, 0))

    lb_spec = pl.BlockSpec((2, hw), lambda n: (0, 0))
    out = jax.ShapeDtypeStruct((t, hw), F32)
    dlb = jax.ShapeDtypeStruct((2, hw), F32)
    wide = pltpu.VMEM((CHUNK, hw), F32)
    return _pallas(
        body, name=name, grid=(nc,),
        in_specs=[col(0, False), col(1, False), col(2, False), col(0, False), st(False),
                  col(0, True), col(1, True), col(3, True), col(0, True), st(True), lb_spec, lb_spec],
        out_specs=[col(0, False), col(0, False), col(0, False), lb_spec,
                   col(0, True), col(0, True), col(0, True), lb_spec],
        out_shape=[out, out, out, dlb, out, out, out, dlb],
        scratch_shapes=[pltpu.VMEM((2, HEAD, hw), F32), pltpu.VMEM((2, 1, hw), F32), wide, wide, wide,
                        pltpu.VMEM((1, hw), F32)],
        compiler_params=_params(("arbitrary",), 16 * _nbytes((HEAD, hw), F32)),
    )(p, p, p, do, st_f, p, p, p, do, st_b, lbp_f, lbp_b)


def _hgrn_out_fwd(name, o_f, o_b, p, gain, g_group):
    t, hw = o_f.shape
    nh = hw // HEAD
    tm = _tile(t, 512, 8)

    def body(of_ref, ob_ref, g_ref, gain_ref, y_ref):
        o = of_ref[...] + ob_ref[...]
        g = g_ref[...]
        y_ref[...] = (o * _rms(o) * gain_ref[...] * (g * _sigmoid(g))).astype(BF16)

    blk = pl.BlockSpec((tm, HEAD), lambda i, h: (i, h))
    return _pallas(
        body, name=name, grid=(t // tm, nh),
        in_specs=[blk, blk, pl.BlockSpec((tm, HEAD), lambda i, h: (i, g_group * nh + h)),
                  pl.BlockSpec((1, HEAD), lambda i, h: (0, h))],
        out_specs=blk, out_shape=jax.ShapeDtypeStruct((t, hw), BF16),
        compiler_params=_params(("parallel", "parallel"), 1 << 20),
    )(o_f, o_b, p, gain)


def _hgrn_out_bwd(name, dy, o_f, o_b, p, gain, g_group, after=None):
    t, hw = o_f.shape
    nh = hw // HEAD
    tm = _tile(t, 512, 8)

    def body(dy_ref, of_ref, ob_ref, g_ref, gain_ref, do_ref, dg_ref, dgain_ref):
        i = pl.program_id(1)
        o = of_ref[...] + ob_ref[...]
        g = g_ref[...]
        gain_v = gain_ref[...]
        sig = _sigmoid(g)
        dyv = dy_ref[...]
        do, dgain = _norm_bwd(dyv * (g * sig), o, gain_v)
        do_ref[...] = do
        dg_ref[...] = dyv * (o * _rms(o) * gain_v) * sig * (1.0 + g * (1.0 - sig))
        _accumulate(dgain_ref, dgain, i == 0)

    blk = pl.BlockSpec((tm, HEAD), lambda h, i: (i, h))
    vec = pl.BlockSpec((1, HEAD), lambda h, i: (0, h))
    out = jax.ShapeDtypeStruct((t, hw), F32)
    body, ins, in_specs = _ordered(
        body, [dy, o_f, o_b, p, gain],
        [blk, blk, blk, pl.BlockSpec((tm, HEAD), lambda h, i: (i, g_group * nh + h)), vec], after)
    return _pallas(
        body, name=name, grid=(nh, t // tm), in_specs=in_specs,
        out_specs=[blk, blk, vec], out_shape=[out, out, jax.ShapeDtypeStruct((1, hw), F32)],
        compiler_params=_params(("parallel", "arbitrary"), 1 << 20),
    )(*ins)


def _t5_bucket_ids():
    c = np.arange(WINDOW)[:, None]
    s = np.arange(SPAN)[None, :]
    rel = s - WINDOW - c
    nb = REL_BUCKETS // 2
    max_exact = nb // 2
    bucket = (rel > 0).astype(np.int32) * nb
    n = np.abs(rel)
    large = max_exact + (np.log(np.maximum(n, 1) / max_exact) / np.log(REL_MAX_DIST / max_exact)
                         * (nb - max_exact)).astype(np.int32)
    large = np.minimum(large, nb - 1)
    ids = bucket + np.where(n < max_exact, n, large).astype(np.int32)
    return jnp.asarray(ids.reshape(1, WINDOW * SPAN), jnp.int32)


def _bias_onehot(ids_ref):
    n = ids_ref.shape[1]
    return (lax.broadcasted_iota(jnp.int32, (REL_BUCKETS, n), 0) == ids_ref[...]).astype(BF16)


def _bias_gather(name, table_t, ids):
    nh = table_t.shape[0]

    def body(t_ref, ids_ref, o_ref):
        o_ref[...] = _dot_exact(t_ref[...], _bias_onehot(ids_ref), split="a")

    return _pallas(
        body, name=name, out_shape=jax.ShapeDtypeStruct((nh, ids.shape[1]), F32),
        compiler_params=pltpu.CompilerParams(vmem_limit_bytes=32 << 20),
    )(table_t, ids)


def _bias_scatter(name, dbias, ids):
    nh = dbias.shape[0]

    def body(d_ref, ids_ref, o_ref):
        o_ref[...] = _dot_exact(d_ref[...], _bias_onehot(ids_ref), 1, 1, split="a")

    return _pallas(
        body, name=name, out_shape=jax.ShapeDtypeStruct((nh, REL_BUCKETS), F32),
        compiler_params=pltpu.CompilerParams(vmem_limit_bytes=32 << 20),
    )(dbias, ids)


def _attn_valid(i, t):
    c = lax.broadcasted_iota(jnp.int32, (WINDOW, SPAN), 0)
    s = lax.broadcasted_iota(jnp.int32, (WINDOW, SPAN), 1)
    rel = s - WINDOW - c
    pos = i * WINDOW - WINDOW + s
    return (jnp.abs(rel) <= WINDOW) & (pos >= 0) & (pos < t)


def _attn_probs(qs, khs, b_ref, s_ref, valid):
    heads = range(len(qs))
    sinks = [s_ref[0:1, h:h + 1] for h in heads]
    s = [_dot(qs[h], khs[h], 1, 1) / math.sqrt(HEAD) for h in heads]
    s = [jnp.where(valid, s[h] + b_ref[h], NEG_INF) for h in heads]
    m = [jnp.maximum(jnp.max(s[h], axis=-1, keepdims=True), sinks[h]) for h in heads]
    e = [jnp.exp(s[h] - m[h]) for h in heads]
    es = [jnp.exp(sinks[h] - m[h]) for h in heads]
    inv = [1.0 / (jnp.sum(e[h], axis=-1, keepdims=True) + es[h]) for h in heads]
    return [e[h] * inv[h] for h in heads], [es[h] * inv[h] for h in heads]


def _attn_fwd(name, p, k_pad, v_pad, bias, sink, q_group_blk):
    t = p.shape[0]
    nh = bias.shape[0]
    aw = nh * HEAD
    grp = nh // KV_HEADS
    nb = t // WINDOW

    def body(q_ref, k_ref, v_ref, b_ref, s_ref, y_ref):
        i = pl.program_id(0)
        valid = _attn_valid(i, t)
        start = pl.multiple_of(i * WINDOW, WINDOW)
        ks = k_ref[pl.ds(start, SPAN), :]
        vs = v_ref[pl.ds(start, SPAN), :]
        heads = range(nh)
        col = lambda h: slice(h * HEAD, (h + 1) * HEAD)
        qs = [q_ref[:, col(h)].astype(BF16) for h in heads]
        pr, _ = _attn_probs(qs, [ks[:, col(h // grp)] for h in heads], b_ref, s_ref, valid)
        out = [_dot(pr[h].astype(BF16), vs[:, col(h // grp)]) for h in heads]
        for h in heads:
            y_ref[:, col(h)] = out[h].astype(BF16)

    full = lambda a: pl.BlockSpec(a.shape, lambda i: (0,) * a.ndim)
    return _pallas(
        body, name=name, grid=(nb,),
        in_specs=[pl.BlockSpec((WINDOW, aw), lambda i: (i, q_group_blk)), full(k_pad), full(v_pad), full(bias),
                  full(sink)],
        out_specs=pl.BlockSpec((WINDOW, aw), lambda i: (i, 0)),
        out_shape=jax.ShapeDtypeStruct((t, aw), BF16),
        compiler_params=_params(("parallel",), _nbytes(k_pad.shape, BF16) * 2 + _nbytes(bias.shape, F32)),
    )(p, k_pad, v_pad, bias, sink)


def _attn_bwd(name, p, k_pad, v_pad, bias, sink, dy, q_group_blk, dy_blk, after=None):
    t = p.shape[0]
    nh = bias.shape[0]
    aw = nh * HEAD
    grp = nh // KV_HEADS
    nb = t // WINDOW
    kvw = k_pad.shape[1]

    def body(q_ref, k_ref, v_ref, b_ref, s_ref, dy_ref, dq_ref, dk_ref, dv_ref, db_ref, ds_ref):
        i = pl.program_id(0)

        @pl.when(i == 0)
        def _():
            dk_ref[...] = jnp.zeros_like(dk_ref)
            dv_ref[...] = jnp.zeros_like(dv_ref)
            db_ref[...] = jnp.zeros_like(db_ref)
            ds_ref[...] = jnp.zeros_like(ds_ref)

        valid = _attn_valid(i, t)
        start = pl.multiple_of(i * WINDOW, WINDOW)
        ks = k_ref[pl.ds(start, SPAN), :]
        vs = v_ref[pl.ds(start, SPAN), :]
        inv_sqrt = 1.0 / math.sqrt(HEAD)
        heads = range(nh)
        col = lambda h: slice(h * HEAD, (h + 1) * HEAD)
        qs = [q_ref[:, col(h)].astype(BF16) for h in heads]
        khs = [ks[:, col(h // grp)] for h in heads]
        pr, ps = _attn_probs(qs, khs, b_ref, s_ref, valid)
        dos = [dy_ref[:, col(h)].astype(BF16) for h in heads]
        dp = [_dot(dos[h], vs[:, col(h // grp)], 1, 1) for h in heads]
        delta = [jnp.sum(pr[h] * dp[h], axis=-1, keepdims=True) for h in heads]
        dsc = [pr[h] * (dp[h] - delta[h]) for h in heads]
        dsr = [(dsc[h] * inv_sqrt).astype(BF16) for h in heads]
        dq = [_dot(dsr[h], khs[h]) for h in heads]
        dk = [_dot(dsr[h], qs[h], 0, 0) for h in heads]
        dv = [_dot(pr[h].astype(BF16), dos[h], 0, 0) for h in heads]
        for h in heads:
            db_ref[h] += dsc[h]
            ds_ref[h:h + 1, :] += jnp.broadcast_to(jnp.sum(-ps[h] * delta[h], axis=0, keepdims=True), (1, 128))
            dq_ref[:, col(h)] = dq[h]
        for kv in range(KV_HEADS):
            group = range(kv * grp, (kv + 1) * grp)
            dk_ref[pl.ds(start, SPAN), col(kv)] += sum(dk[h] for h in group)
            dv_ref[pl.ds(start, SPAN), col(kv)] += sum(dv[h] for h in group)

    full = lambda a: pl.BlockSpec(a.shape, lambda i: (0,) * a.ndim)
    whole = lambda shape: pl.BlockSpec(shape, lambda i: (0,) * len(shape))
    pad_shape = (t + 2 * WINDOW, kvw)
    body, ins, in_specs = _ordered(
        body, [p, k_pad, v_pad, bias, sink, dy],
        [pl.BlockSpec((WINDOW, aw), lambda i: (i, q_group_blk)), full(k_pad), full(v_pad), full(bias), full(sink),
         pl.BlockSpec((WINDOW, aw), lambda i: (i, dy_blk))], after)
    return _pallas(
        body, name=name, grid=(nb,), in_specs=in_specs,
        out_specs=[pl.BlockSpec((WINDOW, aw), lambda i: (i, 0)), whole(pad_shape), whole(pad_shape),
                   whole(bias.shape), whole((nh, 128))],
        out_shape=[jax.ShapeDtypeStruct((t, aw), F32), jax.ShapeDtypeStruct(pad_shape, F32),
                   jax.ShapeDtypeStruct(pad_shape, F32), jax.ShapeDtypeStruct(bias.shape, F32),
                   jax.ShapeDtypeStruct((nh, 128), F32)],
        compiler_params=_params(("arbitrary",), 3 * _nbytes(pad_shape, F32) + 2 * _nbytes(bias.shape, F32)),
    )(*ins)


def _pad_kv(name, p, kv_blk, kvw):
    t = p.shape[0]
    nb = t // WINDOW

    def body(x_ref, o_ref):
        i = pl.program_id(0)
        inside = jnp.logical_and(i >= 1, i <= nb)
        o_ref[...] = jnp.where(inside, x_ref[...], 0.0).astype(BF16)

    return _pallas(
        body, name=name, grid=(nb + 2,),
        in_specs=[pl.BlockSpec((WINDOW, kvw), lambda i: (jnp.clip(i - 1, 0, nb - 1), kv_blk))],
        out_specs=pl.BlockSpec((WINDOW, kvw), lambda i: (i, 0)),
        out_shape=jax.ShapeDtypeStruct((t + 2 * WINDOW, kvw), BF16),
        compiler_params=_params(("parallel",), 1 << 20),
    )(p)


def _mix_dproj(name, pieces, kv_pads, t, after=None):
    hw = pieces[0][0].shape[1]
    kvw = kv_pads[0].shape[1]
    widths = [hw] * len(pieces) + [kvw] * len(kv_pads)
    total = sum(widths)
    tm = WINDOW
    flat = [a for pc in pieces for a in pc]

    def body(*refs):
        o_ref = refs[-1]
        pos, off = 0, 0
        for pc in pieces:
            val = refs[pos][...]
            for extra in range(1, len(pc)):
                val = val + refs[pos + extra][...]
            o_ref[:, off:off + hw] = val.astype(BF16)
            pos += len(pc)
            off += hw
        for _ in kv_pads:
            o_ref[:, off:off + kvw] = refs[pos][...].astype(BF16)
            pos += 1
            off += kvw

    in_specs = [pl.BlockSpec((tm, hw), lambda i: (i, 0)) for _ in flat]
    in_specs += [pl.BlockSpec((tm, kvw), lambda i: (i + 1, 0)) for _ in kv_pads]
    body, ins, in_specs = _ordered(body, [*flat, *kv_pads], in_specs, after)
    return _pallas(
        body, name=name, grid=(t // tm,), in_specs=in_specs,
        out_specs=pl.BlockSpec((tm, total), lambda i: (i, 0)),
        out_shape=jax.ShapeDtypeStruct((t, total), BF16),
        compiler_params=_params(("parallel",), 3 * _nbytes((tm, total), F32)),
    )(*ins)


def _concat_cols(name, a, b):
    t, wa = a.shape
    wb = b.shape[1]
    tm = _tile(t, 512, 16)

    def body(a_ref, b_ref, o_ref):
        o_ref[:, :wa] = a_ref[...]
        o_ref[:, wa:] = b_ref[...]

    return _pallas(
        body, name=name, grid=(t // tm,),
        in_specs=[pl.BlockSpec((tm, wa), lambda i: (i, 0)), pl.BlockSpec((tm, wb), lambda i: (i, 0))],
        out_specs=pl.BlockSpec((tm, wa + wb), lambda i: (i, 0)),
        out_shape=jax.ShapeDtypeStruct((t, wa + wb), a.dtype),
        compiler_params=_params(("parallel",), 2 * _nbytes((tm, wa + wb), a.dtype)),
    )(a, b)


def _cast_into_full(name, w, geom, idx, after=None):
    r, c = w.shape
    tr = _tile(r, 256, 16)
    nr = r // tr
    if geom.col:
        place = lambda i, iref: (i, iref[0])
    else:
        place = lambda i, iref: (iref[0] * nr + i, 0)

    def body(i_ref, w_ref, *rest):
        rest[-1][...] = w_ref[...].astype(BF16)

    in_specs = [pl.BlockSpec((tr, c), lambda i, iref: (i, 0))]
    ins = [w]
    if after is not None:
        in_specs.append(pl.BlockSpec(memory_space=pl.ANY))
        ins.append(after)
    return _pallas(
        body, name=name,
        grid_spec=pltpu.PrefetchScalarGridSpec(
            num_scalar_prefetch=1, grid=(nr,), in_specs=in_specs, out_specs=pl.BlockSpec((tr, c), place)),
        out_shape=pltpu.HBM(geom.full_shape, BF16),
        compiler_params=_params(("parallel",), 2 * _nbytes((tr, c), F32)),
    )(idx, *ins)


def _adamw(name, w, g, m, v):
    r, c = w.shape
    tr = _tile(r, 128, 8)
    bc1 = 1.0 - ADAM_B1 ** ADAM_STEP
    bc2 = 1.0 - ADAM_B2 ** ADAM_STEP

    def body(w_ref, g_ref, m_ref, v_ref, go_ref, d_ref, nm_ref, nv_ref):
        gv = g_ref[...]
        go_ref[...] = gv
        nm = ADAM_B1 * m_ref[...] + (1.0 - ADAM_B1) * gv
        nv = ADAM_B2 * v_ref[...] + (1.0 - ADAM_B2) * (gv * gv)
        nm_ref[...] = nm
        nv_ref[...] = nv
        d_ref[...] = -ADAM_LR * ((nm / bc1) / (jnp.sqrt(nv / bc2) + ADAM_EPS) + ADAM_WD * w_ref[...])

    blk = pl.BlockSpec((tr, c), lambda i: (i, 0))
    out = jax.ShapeDtypeStruct((r, c), F32)
    return _pallas(
        body, name=name, grid=(r // tr,), in_specs=[blk] * 4, out_specs=[blk] * 4, out_shape=[out] * 4,
        compiler_params=_params(("parallel",), 8 * _nbytes((tr, c), F32)),
    )(w, g, m, v)


def _mesh_pos():
    return lax.axis_index("x"), lax.axis_index("y"), lax.axis_index("c")


def _other_chips(x, y):
    return [(1 - x, y), (x, 1 - y), (1 - x, 1 - y)]


class _Big:
    def __init__(self, shard_shape, col_sharded):
        self.col = col_sharded
        r, c = shard_shape
        self.shard_shape = (r, c)
        self.full_shape = (r, N_CHIPS * c) if col_sharded else (N_CHIPS * r, c)
        self.half_shape = (r // 2, N_CHIPS * c) if col_sharded else (N_CHIPS * r, c // 2)
        self.shard_half_shape = (r // 2, c) if col_sharded else (r, c // 2)

    def region(self, ref, s, half=None):
        r, c = self.shard_shape
        if self.col:
            rows = slice(None) if half is None else pl.ds(half * (r // 2), r // 2)
            return ref.at[rows, pl.ds(s * c, c)]
        cols = slice(None) if half is None else pl.ds(half * (c // 2), c // 2)
        return ref.at[pl.ds(s * r, r), cols]

    def n_halves(self, ref, half, n):
        r, c = self.shard_shape
        if self.col:
            return ref.at[pl.ds(half * (r // 2), r // 2), pl.ds(0, n * c)]
        return ref.at[pl.ds(0, n * r), pl.ds(half * (c // 2), c // 2)]

    def three_halves(self, ref, half):
        return self.n_halves(ref, half, 3)

    def sub_half(self, ref, s, half, j):
        r, c = self.shard_shape
        if self.col:
            return ref.at[pl.ds(half * (r // 2) + j * (r // 4), r // 4), pl.ds(s * c, c)]
        return ref.at[pl.ds(s * r + j * (r // 2), r // 2), pl.ds(half * (c // 2), c // 2)]

    def half_of_full(self, ref, half):
        r, c = self.full_shape
        if self.col:
            return ref.at[pl.ds(half * (r // 2), r // 2), :]
        return ref.at[:, pl.ds(half * (c // 2), c // 2)]

    def half_of_shard(self, ref, half):
        r, c = self.shard_shape
        if self.col:
            return ref.at[pl.ds(half * (r // 2), r // 2), :]
        return ref.at[:, pl.ds(half * (c // 2), c // 2)]

    def shard_of_half(self, ref, s):
        r, c = self.shard_shape
        if self.col:
            return ref.at[:, pl.ds(s * c, c)]
        return ref.at[pl.ds(s * r, r), :]


HBM =pl.BlockSpec(memory_space=pltpu.HBM)
SEM = pl.BlockSpec(memory_space=pltpu.SEMAPHORE)
SPLIT_COPY = pltpu.CompilerParams(has_side_effects=pltpu.SideEffectType.DATAFLOW_SIDE_EFFECTING)


def _in_hbm(a):
    return pltpu.with_memory_space_constraint(a, pltpu.HBM)


def _gather_start(name, fulls, geoms, after):
    nw = len(fulls)

    def body(*refs):
        dst = refs[nw + 1:2 * nw + 1]
        sems = refs[2 * nw + 1:-1]
        x, y, c = _mesh_pos()
        mine = 2 * x + y
        for w in range(nw):
            own_half = geoms[w].region(dst[w], mine, c)
            for chip in _other_chips(x, y):
                pltpu.make_async_remote_copy(src_ref=own_half, dst_ref=own_half, send_sem=sems[2 * w],
                                             recv_sem=sems[2 * w + 1], device_id=(*chip, c),
                                             device_id_type=MESH).start()
        refs[-1][...] = jnp.zeros_like(refs[-1])

    out = _pallas(
        body, name=name, in_specs=[HBM] * nw + [pl.BlockSpec(memory_space=pl.ANY)],
        out_specs=[HBM] * nw + [SEM] * (2 * nw) + [pl.BlockSpec(memory_space=pltpu.VMEM)],
        out_shape=[pltpu.HBM(g.full_shape, BF16) for g in geoms] + [pltpu.SemaphoreType.DMA(())] * (2 * nw)
        + [jax.ShapeDtypeStruct((8, 128), F32)],
        input_output_aliases={w: w for w in range(nw)}, compiler_params=SPLIT_COPY,
    )(*[_in_hbm(a) for a in fulls], after)
    return list(out[:nw]), [(out[nw + 2 * w], out[nw + 2 * w + 1]) for w in range(nw)], out[-1]


def _wait_three(geom, ref, half, send_sem, recv_sem, peer, recv):
    three = geom.three_halves(ref, half)
    copy = pltpu.make_async_remote_copy(src_ref=three, dst_ref=three, send_sem=send_sem, recv_sem=recv_sem,
                                        device_id=peer, device_id_type=MESH)
    if recv:
        copy.wait_recv()
    else:
        copy.wait_send()


def _gather_first_direct(full, geom):
    def start(refs, _, new):
        x, y, c = _mesh_pos()
        own = geom.region(refs[0], 2 * x + y, c)
        for chip in ((1 - x, y), (x, 1 - y)):
            _remote(own, own, new, (*chip, c)).start()

    return _split_copy_call("gather_first_direct", [full], start, new_sems=2)


def _gather_first_relay(full, geom, sems, after):
    def relay(refs, got, new):
        x, y, c = _mesh_pos()
        w = refs[0]
        two = geom.n_halves(w, c, 2)
        _remote(two, two, got, (x, y, 1 - c)).wait_recv()
        from_x = geom.sub_half(w, 2 * (1 - x) + y, c, 0)
        from_y = geom.sub_half(w, 2 * x + (1 - y), c, 1)
        _remote(from_x, from_x, new, (x, 1 - y, c)).start()
        _remote(from_y, from_y, new, (1 - x, y, c)).start()
        _remote(two, two, got, (x, y, 1 - c)).wait_send()

    return _split_copy_call("gather_first_relay", [full], relay, sems=sems, after=after, new_sems=2)


def _gather_forward(name, full, geom, sems, after, arrivals=3):
    def body(w_in, send_sem, recv_sem, after_ref, w_ref, fwd_send, fwd_recv):
        x, y, c = _mesh_pos()
        sibling = (x, y, 1 - c)
        landed_all = geom.n_halves(w_ref, c, arrivals)
        _remote(landed_all, landed_all, (send_sem, recv_sem), sibling).wait_recv()
        for chip in _other_chips(x, y):
            landed = geom.region(w_ref, 2 * chip[0] + chip[1], c)
            pltpu.make_async_remote_copy(src_ref=landed, dst_ref=landed, send_sem=fwd_send, recv_sem=fwd_recv,
                                         device_id=sibling, device_id_type=MESH).start()
        _remote(landed_all, landed_all, (send_sem, recv_sem), sibling).wait_send()

    sem = pltpu.SemaphoreType.DMA(())
    out = _pallas(
        body, name=name, in_specs=[HBM, SEM, SEM, pl.BlockSpec(memory_space=pl.ANY)], out_specs=[HBM, SEM, SEM],
        out_shape=[pltpu.HBM(geom.full_shape, BF16), sem, sem],
        input_output_aliases={0: 0}, compiler_params=SPLIT_COPY,
    )(full, sems[0], sems[1], after)
    return out[0], (out[1], out[2])


def _gather_end(name, full, geom, sems, after):
    def body(w_in, fwd_send, fwd_recv, after_ref, w_ref):
        x, y, c = _mesh_pos()
        sibling = (x, y, 1 - c)
        _wait_three(geom, w_ref, 1 - c, fwd_send, fwd_recv, sibling, recv=True)
        _wait_three(geom, w_ref, c, fwd_send, fwd_recv, sibling, recv=False)

    return _pallas(
        body, name=name, in_specs=[HBM, SEM, SEM, pl.BlockSpec(memory_space=pl.ANY)], out_specs=HBM,
        out_shape=pltpu.HBM(geom.full_shape, BF16),
        input_output_aliases={0: 0}, compiler_params=SPLIT_COPY,
    )(full, sems[0], sems[1], after)


def _split_copy_call(name, arrays, fn, sems=(), after=None, new_sems=0):
    n, ns = len(arrays), len(sems)
    n_in = n + ns + (after is not None)

    def body(*refs):
        fn(refs[n_in:n_in + n], refs[n:n + ns], refs[n_in + n:-1])
        refs[-1][...] = jnp.zeros_like(refs[-1])

    ins = list(arrays) if ns else [_in_hbm(a) for a in arrays]
    ins += list(sems) + ([after] if after is not None else [])
    in_specs = [HBM] * n + [SEM] * ns + ([pl.BlockSpec(memory_space=pl.ANY)] if after is not None else [])
    out = _pallas(
        body, name=name, in_specs=in_specs,
        out_specs=[HBM] * n + [SEM] * new_sems + [pl.BlockSpec(memory_space=pltpu.VMEM)],
        out_shape=[pltpu.HBM(a.shape, a.dtype) for a in arrays] + [pltpu.SemaphoreType.DMA(())] * new_sems
        + [jax.ShapeDtypeStruct((8, 128), F32)],
        input_output_aliases={i: i for i in range(n)}, compiler_params=SPLIT_COPY,
    )(*ins)
    return list(out[:n]), tuple(out[n:-1]), out[-1]


def _remote(src, dst, sems, to):
    return pltpu.make_async_remote_copy(src_ref=src, dst_ref=dst, send_sem=sems[0], recv_sem=sems[1],
                                        device_id=to, device_id_type=MESH)


class _GradReduce:
    def __init__(self, name, geom, idx, c_idx):
        self.name, self.geom, self.idx, self.c_idx = name, geom, idx, c_idx

    def pair_start(self, dw):
        g = self.geom

        def start(refs, _, new):
            x, y, c = _mesh_pos()
            _remote(g.half_of_full(refs[0], 1 - c), refs[1], new, (x, y, 1 - c)).start()

        self.arrays, self.sems, token = _split_copy_call(
            f"pair_start_{self.name}", [dw, lax.empty(g.half_shape, BF16)], start, new_sems=2)
        return token

    def pair_finish(self, after):
        g = self.geom

        def wait(refs, sems, _):
            x, y, c = _mesh_pos()
            copy = _remote(g.half_of_full(refs[0], 1 - c), refs[1], sems, (x, y, 1 - c))
            copy.wait_send()
            copy.wait_recv()

        (dw, landed), _, _ = _split_copy_call(f"pair_wait_{self.name}", self.arrays, wait, self.sems, after)
        half = _pair_add(f"pair_add_{self.name}", dw, landed, g, self.c_idx)

        def start(refs, _, new):
            x, y, c = _mesh_pos()
            for k, chip in enumerate(_other_chips(x, y)):
                _remote(g.shard_of_half(refs[0], 2 * chip[0] + chip[1]), refs[1].at[k], new, (*chip, c)).start()

        self.arrays, self.sems, token = _split_copy_call(
            f"chip_start_{self.name}", [half, lax.empty((3,) + g.shard_half_shape, BF16)], start, new_sems=2)
        return token

    def chip_finish(self, after):
        g = self.geom

        def wait(refs, sems, _):
            x, y, c = _mesh_pos()
            three = _remote(refs[1], refs[1], sems, (x, y, 1 - c))
            three.wait_send()
            three.wait_recv()

        (half, landed), _, _ = _split_copy_call(f"chip_wait_{self.name}", self.arrays, wait, self.sems, after)
        quarter = _chip_add(f"chip_add_{self.name}", half, landed, g, self.idx)

        def start(refs, _, new):
            x, y, c = _mesh_pos()
            own = g.half_of_shard(refs[0], c)
            _remote(own, own, new, (x, y, 1 - c)).start()

        self.arrays, self.sems, token = _split_copy_call(f"share_start_{self.name}", [quarter], start, new_sems=2)
        return token

    def finish(self, after):
        g = self.geom

        def wait(refs, sems, _):
            x, y, c = _mesh_pos()
            own, theirs = g.half_of_shard(refs[0], c), g.half_of_shard(refs[0], 1 - c)
            _remote(own, own, sems, (x, y, 1 - c)).wait_send()
            _remote(theirs, theirs, sems, (x, y, 1 - c)).wait_recv()

        (quarter,), _, _ = _split_copy_call(f"share_wait_{self.name}", self.arrays, wait, self.sems, after)
        return quarter


def _pair_add(name, grad, recv, geom, c_idx):
    r, c = geom.half_shape
    tr, tc = _tile(r, 512, 16), _tile(c, 2048, 128)
    nr, ncol = r // tr, c // tc
    if geom.col:
        mine = lambda i, j, cref: (cref[0] * nr + i, j)
    else:
        mine = lambda i, j, cref: (i, cref[0] * ncol + j)

    def body(c_ref, g_ref, r_ref, o_ref):
        o_ref[...] = (g_ref[...].astype(F32) + r_ref[...].astype(F32)).astype(BF16)

    return _pallas(
        body, name=name,
        grid_spec=pltpu.PrefetchScalarGridSpec(
            num_scalar_prefetch=1, grid=(nr, ncol),
            in_specs=[pl.BlockSpec((tr, tc), mine), pl.BlockSpec((tr, tc), lambda i, j, cref: (i, j))],
            out_specs=pl.BlockSpec((tr, tc), lambda i, j, cref: (i, j))),
        out_shape=jax.ShapeDtypeStruct((r, c), BF16),
        compiler_params=_params(("parallel", "parallel"), 3 * _nbytes((tr, tc), F32)),
    )(c_idx, grad, recv)


def _chip_add(name, half, recv, geom, idx):
    r, c = geom.shard_half_shape
    tr, tc = _tile(r, 512, 16), _tile(c, 2048, 128)
    nr, ncol = r // tr, c // tc
    if geom.col:
        mine = lambda i, j, iref: (i, iref[0] * ncol + j)
        place = lambda i, j, iref: (iref[1] * nr + i, j)
    else:
        mine = lambda i, j, iref: (iref[0] * nr + i, j)
        place = lambda i, j, iref: (i, iref[1] * ncol + j)

    def body(i_ref, h_ref, r_ref, o_ref):
        acc = h_ref[...].astype(F32)
        for k in range(3):
            acc = acc + r_ref[k].astype(F32)
        o_ref[...] = acc

    return _pallas(
        body, name=name,
        grid_spec=pltpu.PrefetchScalarGridSpec(
            num_scalar_prefetch=1, grid=(nr, ncol),
            in_specs=[pl.BlockSpec((tr, tc), mine), pl.BlockSpec((3, tr, tc), lambda i, j, iref: (0, i, j))],
            out_specs=pl.BlockSpec((tr, tc), place)),
        out_shape=jax.ShapeDtypeStruct(geom.shard_shape, F32),
        compiler_params=_params(("parallel", "parallel"), 4 * _nbytes((tr, tc), F32)),
    )(idx, half, recv)


def _all_reduce_small(pack, after=None):
    r, d = pack.shape

    def body(p_ref, o_ref, slots, send_sems, recv_sems):
        x, y, c = _mesh_pos()
        me = 4 * x + 2 * y + c
        slots[me] = p_ref[...]
        copies = []
        for k in range(1, N_DEV):
            px, py, pc = x ^ ((k >> 2) & 1), y ^ ((k >> 1) & 1), c ^ (k & 1)
            copies.append(pltpu.make_async_remote_copy(
                src_ref=p_ref, dst_ref=slots.at[me], send_sem=send_sems.at[k - 1], recv_sem=recv_sems.at[k - 1],
                device_id=(px, py, pc), device_id_type=MESH))
        for cp in copies:
            cp.start()
        for k in range(1, N_DEV):
            peer = 4 * (x ^ ((k >> 2) & 1)) + 2 * (y ^ ((k >> 1) & 1)) + (c ^ (k & 1))
            pltpu.make_async_remote_copy(
                src_ref=p_ref, dst_ref=slots.at[peer], send_sem=send_sems.at[k - 1], recv_sem=recv_sems.at[k - 1],
                device_id=(x, y, c), device_id_type=MESH).wait_recv()
        for cp in copies:
            cp.wait_send()
        acc = slots[0]
        for k in range(1, N_DEV):
            acc = acc + slots[k]
        o_ref[...] = acc

    vm = pl.BlockSpec(memory_space=pltpu.VMEM)
    body, ins, in_specs = _ordered(body, [pack], [vm], after)
    return _pallas(
        body, name="all_reduce_small", in_specs=in_specs, out_specs=vm,
        out_shape=jax.ShapeDtypeStruct((r, d), F32),
        scratch_shapes=[pltpu.VMEM((N_DEV, r, d), F32), pltpu.SemaphoreType.DMA((N_DEV - 1,)),
                        pltpu.SemaphoreType.DMA((N_DEV - 1,))],
    )(*ins)


def _pack_rows(rows, d):
    out = []
    for a in rows:
        flat = a.reshape(-1)
        n = -(-flat.shape[0] // d) * d
        out.append(jnp.pad(flat, (0, n - flat.shape[0])).reshape(-1, d))
    packed = jnp.concatenate(out, axis=0)
    return jnp.pad(packed, ((0, 16 - packed.shape[0]), (0, 0)))


def _unpack_rows(packed, shapes, d):
    out, row = [], 0
    for shp in shapes:
        n = int(np.prod(shp))
        nrows = -(-n // d)
        out.append(packed[row:row + nrows].reshape(-1)[:n].reshape(shp))
        row += nrows
    return out


def kernel(x, pre_norm_ffn1, post_norm_ffn1, w_ffn1_gate_up, w_ffn1_down, pre_norm_mix, post_norm_mix, w_mix_in, hgrn_lower_bounds_fwd, hgrn_lower_bounds_bwd, hgrn_out_norm, attn_sink, w_mix_out, pre_norm_ffn2, post_norm_ffn2, w_ffn2_gate_up, w_ffn2_down, rel_bias_table, loss_target, m_pre_norm_ffn1, m_post_norm_ffn1, m_w_ffn1_gate_up, m_w_ffn1_down, m_pre_norm_mix, m_post_norm_mix, m_w_mix_in, m_hgrn_lower_bounds_fwd, m_hgrn_lower_bounds_bwd, m_hgrn_out_norm, m_attn_sink, m_w_mix_out, m_pre_norm_ffn2, m_post_norm_ffn2, m_w_ffn2_gate_up, m_w_ffn2_down, m_rel_bias_table, v_pre_norm_ffn1, v_post_norm_ffn1, v_w_ffn1_gate_up, v_w_ffn1_down, v_pre_norm_mix, v_post_norm_mix, v_w_mix_in, v_hgrn_lower_bounds_fwd, v_hgrn_lower_bounds_bwd, v_hgrn_out_norm, v_attn_sink, v_w_mix_out, v_pre_norm_ffn2, v_post_norm_ffn2, v_w_ffn2_gate_up, v_w_ffn2_down, v_rel_bias_table):
    t, d = x.shape[1], x.shape[2]
    hw = hgrn_out_norm.shape[1]
    aw = d - hw
    nah = aw // HEAD
    kvw = KV_HEADS * HEAD
    x0 = x[0]
    target = loss_target[0]

    big_names = ["w_ffn1_gate_up", "w_ffn1_down", "w_mix_in", "w_mix_out", "w_ffn2_gate_up", "w_ffn2_down"]
    big_w = [w_ffn1_gate_up[0], w_ffn1_down[0], w_mix_in[0], w_mix_out[0], w_ffn2_gate_up[0], w_ffn2_down[0]]
    big_m = [m_w_ffn1_gate_up[0], m_w_ffn1_down[0], m_w_mix_in[0], m_w_mix_out[0], m_w_ffn2_gate_up[0],
             m_w_ffn2_down[0]]
    big_v = [v_w_ffn1_gate_up[0], v_w_ffn1_down[0], v_w_mix_in[0], v_w_mix_out[0], v_w_ffn2_gate_up[0],
             v_w_ffn2_down[0]]
    col_sharded = [True, False, True, False, True, False]
    geoms = [_Big(w.shape, cs) for w, cs in zip(big_w, col_sharded)]

    cx, cy, cc = _mesh_pos()
    idx = jnp.stack([2 * cx + cy, cc]).astype(jnp.int32)
    c_idx = jnp.reshape(cc, (1,)).astype(jnp.int32)
    first = _cast_into_full(f"cast_{big_names[0]}", big_w[0], geoms[0], idx)
    (first,), direct_sems, tok = _gather_first_direct(first, geoms[0])
    rest = [_cast_into_full(f"cast_{n}", w, gm, idx, after=tok)
            for n, w, gm in zip(big_names[1:], big_w[1:], geoms[1:])]
    (first,), relay_sems, tok = _gather_first_relay(first, geoms[0], direct_sems, after=rest[-1])
    started_rest, sems_rest, _ = _gather_start("gather_start_rest", rest, geoms[1:], after=tok)
    started, gather_sems = [first] + started_rest, [relay_sems] + sems_rest

    def forward_weight(w, after):
        return _gather_forward(f"gather_forward_{big_names[w]}", started[w], geoms[w], gather_sems[w], after,
                               arrivals=1 if w == 0 else 3)

    def whole_weight(w, forwarded, after):
        return _gather_end(f"gather_end_{big_names[w]}", forwarded[0], geoms[w], forwarded[1], after)

    h1 = _norm_fwd("ffn1_pre_norm", x0, pre_norm_ffn1)
    w_gu1 = whole_weight(0, forward_weight(0, h1), h1)
    gate1, up1, act1 = _ffn_gate_up_act("ffn1_gate_up", h1, w_gu1)
    w_d1 = whole_weight(1, forward_weight(1, act1), act1)
    ff1 = _mm("ffn1_down", act1, w_d1, "nn", F32)
    fw = forward_weight(2, ff1)
    x1, hm = _resid_norm_fwd("ffn1_residual", x0, ff1, post_norm_ffn1, pre_norm_mix, 0.5)
    w_in = whole_weight(2, fw, hm)
    p = _mm("mix_in", hm, w_in, "nn", F32)
    fw = forward_weight(3, p)
    o_f, o_b, st_f, st_b = _hgrn_scan_fwd("hgrn_scan", p, hgrn_lower_bounds_fwd, hgrn_lower_bounds_bwd)
    y_h = _hgrn_out_fwd("hgrn_out", o_f, o_b, p, hgrn_out_norm, 4)
    kv_blk0 = (5 * hw + aw) // kvw
    k_pad = _pad_kv("attn_pad_k", p, kv_blk0, kvw)
    v_pad = _pad_kv("attn_pad_v", p, kv_blk0 + 1, kvw)
    bucket_ids = _t5_bucket_ids()
    bias = _bias_gather("attn_bias", rel_bias_table.T, bucket_ids).reshape(nah, WINDOW, SPAN)
    y_a = _attn_fwd("attn_fwd", p, k_pad, v_pad, bias, attn_sink, 5 * hw // aw)
    y_mix = _concat_cols("mix_concat", y_h, y_a)
    w_out = whole_weight(3, fw, y_mix)
    mixed = _mm("mix_out", y_mix, w_out, "nn", F32)
    fw = forward_weight(4, mixed)
    x2, h2 = _resid_norm_fwd("mix_residual", x1, mixed, post_norm_mix, pre_norm_ffn2, 1.0)
    w_gu2 = whole_weight(4, fw, h2)
    gate2, up2, act2 = _ffn_gate_up_act("ffn2_gate_up", h2, w_gu2)
    w_d2 = whole_weight(5, forward_weight(5, act2), act2)
    ff2 = _mm("ffn2_down", act2, w_d2, "nn", F32)
    loss_blk, dy, dff2, dg_post2 = _final_fwd_bwd("ffn2_residual_loss", x2, ff2, post_norm_ffn2, target, 0.5)

    reduce = [_GradReduce(n, gm, idx, c_idx) for n, gm in zip(big_names, geoms)]
    big_grads, big_delta, big_new_m, big_new_v = [None] * 6, [None] * 6, [None] * 6, [None] * 6

    def update(w, after):
        g, dl, nm, nv = _adamw(f"adamw_{big_names[w]}", big_w[w], reduce[w].finish(after), big_m[w], big_v[w])
        big_grads[w], big_delta[w], big_new_m[w], big_new_v[w] = g[None], dl[None], nm[None], nv[None]
        return dl

    dw_d2 = _mm("ffn2_dw_down", act2, dff2, "tn", BF16)
    tok = reduce[5].pair_start(dw_d2)
    dgu2 = _ffn_dact("ffn2_dact", dff2, w_d2, gate2, up2, after=tok)
    tok = reduce[5].pair_finish(dgu2)
    dw_gu2 = _ffn_dw_gate_up("ffn2_dw_gate_up", h2, dgu2, after=tok)
    tok = reduce[4].pair_start(dw_gu2)
    dh2 = _ffn_dh("ffn2_dh", dgu2, w_gu2, after=tok)
    tok = reduce[4].pair_finish(dh2)
    dx2, dg_pre2, dmixed, dg_postm = _norms_bwd("mix_residual_bwd", dy, dh2, x2, pre_norm_ffn2,
                                                post=(mixed, post_norm_mix, 1.0), after=tok)
    dw_out = _mm("mix_out_dw", y_mix, dmixed, "tn", BF16)
    tok = reduce[3].pair_start(dw_out)
    dy_mix = _mm("mix_out_dx", dmixed, w_out, "nt", F32, after=tok)
    tok = reduce[3].pair_finish(dy_mix)
    dq_a, dk_pad, dv_pad, dbias, dsink = _attn_bwd("attn_bwd", p, k_pad, v_pad, bias, attn_sink, dy_mix,
                                                   5 * hw // aw, hw // aw, after=tok)
    tok = reduce[5].chip_finish(dq_a)
    drel_t = _bias_scatter("attn_dbias", dbias.reshape(nah, WINDOW * SPAN), bucket_ids)
    do, dg_h, dgain = _hgrn_out_bwd("hgrn_out_bwd", dy_mix, o_f, o_b, p, hgrn_out_norm, 4, after=tok)
    dq_f, dv_f, dz_f, dlb_f, dq_b, dv_b, dz_b, dlb_b = _hgrn_scan_bwd(
        "hgrn_scan_bwd", p, hgrn_lower_bounds_fwd, hgrn_lower_bounds_bwd, do, st_f, st_b)
    tok = reduce[4].chip_finish(dq_f)
    tok = reduce[3].chip_finish(tok)
    dp = _mix_dproj("mix_dproj", [(dq_f, dq_b), (dv_f, dv_b), (dz_f,), (dz_b,), (dg_h,), (dq_a,)],
                    [dk_pad, dv_pad], t, after=tok)
    dw_in = _mm("mix_in_dw", hm, dp, "tn", BF16)
    tok = reduce[2].pair_start(dw_in)
    dhm = _mm("mix_in_dx", dp, w_in, "nt", F32, after=tok)
    tok = reduce[2].pair_finish(dhm)
    dx1, dg_prem, dff1, dg_post1 = _norms_bwd("ffn1_residual_bwd", dx2, dhm, x1, pre_norm_mix,
                                              post=(ff1, post_norm_ffn1, 0.5), after=tok)
    dw_d1 = _mm("ffn1_dw_down", act1, dff1, "tn", BF16)
    tok = reduce[1].pair_start(dw_d1)
    dgu1 = _ffn_dact("ffn1_dact", dff1, w_d1, gate1, up1, after=tok)
    tok = reduce[1].pair_finish(dgu1)
    tok = reduce[2].chip_finish(tok)
    dw_gu1 = _ffn_dw_gate_up("ffn1_dw_gate_up", h1, dgu1, after=tok)
    tok = reduce[0].pair_start(dw_gu1)
    done = update(2, tok)
    tok = reduce[0].pair_finish(done)
    dh1 = _ffn_dh("ffn1_dh", dgu1, w_gu1, after=tok)
    grad_x, dg_pre1 = _norms_bwd("ffn1_pre_norm_bwd", dx1, dh1, x0, pre_norm_ffn1)

    small_w = [pre_norm_ffn1, post_norm_ffn1, pre_norm_mix, post_norm_mix, hgrn_lower_bounds_fwd,
               hgrn_lower_bounds_bwd, hgrn_out_norm, attn_sink, pre_norm_ffn2, post_norm_ffn2, rel_bias_table]
    small_m = [m_pre_norm_ffn1, m_post_norm_ffn1, m_pre_norm_mix, m_post_norm_mix, m_hgrn_lower_bounds_fwd,
               m_hgrn_lower_bounds_bwd, m_hgrn_out_norm, m_attn_sink, m_pre_norm_ffn2, m_post_norm_ffn2,
               m_rel_bias_table]
    small_v = [v_pre_norm_ffn1, v_post_norm_ffn1, v_pre_norm_mix, v_post_norm_mix, v_hgrn_lower_bounds_fwd,
               v_hgrn_lower_bounds_bwd, v_hgrn_out_norm, v_attn_sink, v_pre_norm_ffn2, v_post_norm_ffn2,
               v_rel_bias_table]
    small_g = [dg_pre1, dg_post1, dg_prem, dg_postm, dlb_f, dlb_b, dgain, dsink[:, 0].reshape(1, nah), dg_pre2,
               dg_post2, drel_t.T]
    shapes = [a.shape for a in small_w]
    done = update(5, grad_x)
    done = update(4, done)
    done = update(3, done)
    summed = _all_reduce_small(_pack_rows(small_g + [loss_blk[0:1, 0:1]], d), after=done)
    loss = _unpack_rows(summed, shapes + [(1, 1)], d)[-1][0, 0]
    _, sd, sm, sv = _adamw("adamw_small", _pack_rows(small_w, d), summed, _pack_rows(small_m, d),
                           _pack_rows(small_v, d))
    small_grads = _unpack_rows(summed, shapes, d)
    small_delta, small_new_m, small_new_v = (_unpack_rows(a, shapes, d) for a in (sd, sm, sv))

    tok = reduce[1].chip_finish(sd)
    done = update(1, tok)
    tok = reduce[0].chip_finish(done)
    update(0, tok)

    def ordered(small, big):
        s = dict(zip(["pre1", "post1", "prem", "postm", "lbf", "lbb", "gain", "sink", "pre2", "post2", "rel"], small))
        b = dict(zip(["gu1", "d1", "win", "wout", "gu2", "d2"], big))
        return [s["pre1"], s["post1"], b["gu1"], b["d1"], s["prem"], s["postm"], b["win"], s["lbf"], s["lbb"],
                s["gain"], s["sink"], b["wout"], s["pre2"], s["post2"], b["gu2"], b["d2"], s["rel"]]

    return (loss, grad_x[None], *ordered(small_grads, big_grads), *ordered(small_delta, big_delta),
            *ordered(small_new_m, big_new_m), *ordered(small_new_v, big_new_v))
```

```python
import functools
import math

import jax
import jax.numpy as jnp
import numpy as np
from jax import lax
from jax.experimental import pallas as pl
from jax.experimental.pallas import tpu as pltpu

F32 = jnp.float32
BF16 = jnp.bfloat16

HEAD = 128
CHUNK = 64
WINDOW = 128
SPAN = 3 * WINDOW
KV_HEADS = 2
REL_BUCKETS = 32
REL_MAX_DIST = 128
EPS = 1e-6
NEG_INF = -1e30

ADAM_LR = 0.001
ADAM_B1 = 0.9
ADAM_B2 = 0.999
ADAM_EPS = 1e-08
ADAM_WD = 0.01
ADAM_STEP = 10

N_CHIPS = 4
N_DEV = 8
V7X_VMEM_BYTES = 64 * 1024 * 1024
MESH = pl.DeviceIdType.MESH
ANY = pl.BlockSpec(memory_space=pl.ANY)


def _tile(n, pref, mult):
    t = (min(pref, n) // mult) * mult
    while t >= mult:
        if n % t == 0:
            return t
        t -= mult
    return n


def _params(semantics, block_bytes):
    limit = min(V7X_VMEM_BYTES - (4 << 20), 2 * int(block_bytes) + (8 << 20))
    return pltpu.CompilerParams(dimension_semantics=semantics, vmem_limit_bytes=limit)


def _nbytes(shape, dtype):
    return int(np.prod(shape)) * jnp.dtype(dtype).itemsize


PIN_TO_HBM_BYTES = 4 << 20


def _pallas(body, **kw):
    def pin_shape(s):
        if isinstance(s, jax.ShapeDtypeStruct) and _nbytes(s.shape, s.dtype) >= PIN_TO_HBM_BYTES:
            return pltpu.HBM(s.shape, s.dtype)
        return s

    def pin(a):
        if getattr(a, "dtype", None) in (F32, BF16) and _nbytes(a.shape, a.dtype) >= PIN_TO_HBM_BYTES:
            return pltpu.with_memory_space_constraint(a, pltpu.HBM)
        return a

    out_shape = kw["out_shape"]
    kw["out_shape"] = [pin_shape(s) for s in out_shape] if isinstance(out_shape, (list, tuple)) else pin_shape(out_shape)
    call = pl.pallas_call(body, **kw)
    return lambda *args: call(*[pin(a) for a in args])


def _dot(a, b, ca=1, cb=0):
    return lax.dot_general(a, b, (((ca,), (cb,)), ((), ())), preferred_element_type=F32)


def _split3(x):
    hi = x.astype(BF16)
    r1 = x - hi.astype(F32)
    mid = r1.astype(BF16)
    lo = (r1 - mid.astype(F32)).astype(BF16)
    return hi, mid, lo


def _dot_exact(a, b, ca=1, cb=0, split="b"):
    if split == "b":
        return sum(_dot(a, p, ca, cb) for p in _split3(b))
    return sum(_dot(p, b, ca, cb) for p in _split3(a))


def _rms(x):
    return lax.rsqrt(jnp.mean(x * x, axis=-1, keepdims=True) + EPS)


def _norm_bwd(u, x, gain):
    r = _rms(x)
    xhat = x * r
    dgain = jnp.sum(u * xhat, axis=0, keepdims=True)
    v = u * gain
    dx = r * (v - xhat * jnp.mean(v * xhat, axis=-1, keepdims=True))
    return dx, dgain


def _sigmoid(x):
    return 1.0 / (1.0 + jnp.exp(-x))


def _accumulate(ref, val, first):
    @pl.when(first)
    def _():
        ref[...] = val

    @pl.when(jnp.logical_not(first))
    def _():
        ref[...] += val


def _ordered(body, ins, in_specs, after):
    if after is None:
        return body, list(ins), list(in_specs)
    n_in = len(ins)

    def wrapped(*refs):
        body(*refs[:n_in], *refs[n_in + 1:])

    return wrapped, list(ins) + [after], list(in_specs) + [pl.BlockSpec(memory_space=pl.ANY)]


def _matmul(name, a, b, *, form, out_dtype, tm, tn, tk, a_map=None, b_map=None,
            out_shape=None, out_block=None, out_map=None, sizes=None, after=None):
    if sizes is None:
        if form == "nn":
            (m, k), n = a.shape, b.shape[1]
        elif form == "nt":
            (m, k), n = a.shape, b.shape[0]
        else:
            (k, m), n = a.shape, b.shape[1]
    else:
        m, n, k = sizes
    gi, gj, gk = m // tm, n // tn, k // tk
    a_blk = (tm, tk) if form != "tn" else (tk, tm)
    b_blk = (tk, tn) if form != "nt" else (tn, tk)
    if a_map is None:
        a_map = (lambda i, j, kk: (i, kk)) if form != "tn" else (lambda i, j, kk: (kk, i))
    else:
        a_blk = (None,) + a_blk
    if b_map is None:
        b_map = (lambda i, j, kk: (kk, j)) if form != "nt" else (lambda i, j, kk: (j, kk))
    else:
        b_blk = (None,) + b_blk
    if out_shape is None:
        out_shape, out_block, out_map = (m, n), (tm, tn), (lambda i, j, kk: (i, j))
    ca, cb = {"nn": (1, 0), "nt": (1, 1), "tn": (0, 0)}[form]

    def body(a_ref, b_ref, o_ref, *acc):
        part = _dot(a_ref[...], b_ref[...], ca, cb)
        if gk == 1:
            o_ref[...] = part.astype(o_ref.dtype)
        else:
            kk = pl.program_id(2)
            _accumulate(acc[0], part, kk == 0)

            @pl.when(kk == gk - 1)
            def _():
                o_ref[...] = acc[0][...].astype(o_ref.dtype)

    scratch = [] if gk == 1 else [pltpu.VMEM((tm, tn), F32)]
    vmem = (_nbytes((tm, tk), a.dtype) + _nbytes((tk, tn), b.dtype) + _nbytes((tm, tn), out_dtype)
            + 2 * _nbytes((tm, tn), F32))
    body, ins, in_specs = _ordered(body, [a, b], [pl.BlockSpec(a_blk, a_map), pl.BlockSpec(b_blk, b_map)], after)
    return _pallas(
        body, name=name, grid=(gi, gj, gk), in_specs=in_specs,
        out_specs=pl.BlockSpec(out_block, out_map),
        out_shape=jax.ShapeDtypeStruct(out_shape, out_dtype),
        scratch_shapes=scratch,
        compiler_params=_params(("parallel", "parallel", "arbitrary"), vmem),
    )(*ins)


V7X_HBM_BYTES_PER_US = 3.0e6
V7X_MXU_FLOPS_PER_US = 0.9e9
V7X_VMEM_RMW_BYTES_PER_US = 10e6
GRID_STEP_US = 0.35
MATMUL_VMEM_BUDGET = 40 << 20
MATMUL_MAX_TILE_FLOPS = 1 << 33


def _divisors(n, mult, lo):
    return [t for t in range(mult, n + 1, mult) if n % t == 0 and t >= min(lo, n)]


def _mm_tiles(m, n, k, out_dtype=F32, n_unit=None, k_unit=None):
    out_bytes = jnp.dtype(out_dtype).itemsize
    best = None
    for tm in _divisors(m, 128, 256):
        for tn in _divisors(n_unit or n, 128, 256):
            for tk in _divisors(k_unit or k, 128, 512):
                gi, gj, gk = m // tm, n // tn, k // tk
                vmem = 4 * tm * tk + 4 * tk * tn + 2 * tm * tn * out_bytes + 4 * tm * tn * (2 if gk > 1 else 1)
                if vmem > MATMUL_VMEM_BUDGET or 2 * tm * tn * tk > MATMUL_MAX_TILE_FLOPS:
                    continue
                a_bytes = 2 * m * k * (gj if gk > 1 else 1)
                b_bytes = 2 * k * n * (1 if gj == 1 and gk == 1 else gi)
                hbm_us = (a_bytes + b_bytes + m * n * out_bytes) / V7X_HBM_BYTES_PER_US
                acc_us = (8 * m * n * gk / V7X_VMEM_RMW_BYTES_PER_US) if gk > 1 else 0.0
                cost = max(2 * m * n * k / V7X_MXU_FLOPS_PER_US, 1.3 * hbm_us) + GRID_STEP_US * gi * gj * gk + acc_us
                key = (round(cost, 1), vmem)
                if best is None or key < best[0]:
                    best = (key, (tm, tn, tk))
    return best[1]


def _mm(name, a, b, form, out_dtype, after=None):
    if form == "nn":
        m, k, n = a.shape[0], a.shape[1], b.shape[1]
    elif form == "nt":
        m, k, n = a.shape[0], a.shape[1], b.shape[0]
    else:
        m, k, n = a.shape[1], a.shape[0], b.shape[1]
    tm, tn, tk = _mm_tiles(m, n, k, out_dtype)
    return _matmul(name, a, b, form=form, out_dtype=out_dtype, tm=tm, tn=tn, tk=tk, after=after)


def _row_tile(t):
    return _tile(t, 256, 8)


def _norm_fwd(name, x, gain):
    t, d = x.shape
    tm = _row_tile(t)

    def body(x_ref, g_ref, h_ref):
        xv = x_ref[...]
        h_ref[...] = (xv * _rms(xv) * g_ref[...]).astype(BF16)

    row = pl.BlockSpec((tm, d), lambda i: (i, 0))
    vec = pl.BlockSpec((1, d), lambda i: (0, 0))
    return _pallas(
        body, name=name, grid=(t // tm,), in_specs=[row, vec], out_specs=row,
        out_shape=jax.ShapeDtypeStruct((t, d), BF16),
        compiler_params=_params(("parallel",), 2 * _nbytes((tm, d), F32)),
    )(x, gain)


def _resid_norm_fwd(name, xres, ff, gpost, gpre, scale):
    t, d = xres.shape
    tm = _row_tile(t)

    def body(x_ref, f_ref, gp_ref, gn_ref, xn_ref, h_ref):
        f = f_ref[...]
        xn = x_ref[...] + scale * (f * _rms(f) * gp_ref[...])
        xn_ref[...] = xn
        h_ref[...] = (xn * _rms(xn) * gn_ref[...]).astype(BF16)

    row = pl.BlockSpec((tm, d), lambda i: (i, 0))
    vec = pl.BlockSpec((1, d), lambda i: (0, 0))
    return _pallas(
        body, name=name, grid=(t // tm,), in_specs=[row, row, vec, vec], out_specs=[row, row],
        out_shape=[jax.ShapeDtypeStruct((t, d), F32), jax.ShapeDtypeStruct((t, d), BF16)],
        compiler_params=_params(("parallel",), 4 * _nbytes((tm, d), F32)),
    )(xres, ff, gpost, gpre)


def _final_fwd_bwd(name, xres, ff, gpost, target, scale):
    t, d = xres.shape
    tm = _row_tile(t)

    def body(x_ref, f_ref, gp_ref, t_ref, loss_ref, dy_ref, dff_ref, dg_ref):
        i = pl.program_id(0)
        f = f_ref[...]
        gp = gp_ref[...]
        y = x_ref[...] + scale * (f * _rms(f) * gp)
        err = y - t_ref[...]
        part = 0.5 * jnp.sum(jnp.mean(err * err, axis=-1, keepdims=True), axis=0, keepdims=True)
        _accumulate(loss_ref, jnp.broadcast_to(part, loss_ref.shape), i == 0)
        dy = err / d
        dy_ref[...] = dy
        dff, dg = _norm_bwd(scale * dy, f, gp)
        dff_ref[...] = dff.astype(BF16)
        _accumulate(dg_ref, dg, i == 0)

    row = pl.BlockSpec((tm, d), lambda i: (i, 0))
    vec = pl.BlockSpec((1, d), lambda i: (0, 0))
    return _pallas(
        body, name=name, grid=(t // tm,), in_specs=[row, row, vec, row],
        out_specs=[pl.BlockSpec((8, 128), lambda i: (0, 0)), row, row, vec],
        out_shape=[jax.ShapeDtypeStruct((8, 128), F32), jax.ShapeDtypeStruct((t, d), F32),
                   jax.ShapeDtypeStruct((t, d), BF16), jax.ShapeDtypeStruct((1, d), F32)],
        compiler_params=_params(("arbitrary",), 5 * _nbytes((tm, d), F32)),
    )(xres, ff, gpost, target)


def _norms_bwd(name, dres, dh, xin, gpre, post=None, after=None):
    t, d = dres.shape
    tm = _row_tile(t)
    with_post = post is not None

    def body(*refs):
        if with_post:
            dr_ref, dh_ref, x_ref, g_ref, f_ref, gp_ref, dx_ref, dg_ref, dff_ref, dgp_ref = refs
        else:
            dr_ref, dh_ref, x_ref, g_ref, dx_ref, dg_ref = refs
        i = pl.program_id(0)
        dx, dg = _norm_bwd(dh_ref[...], x_ref[...], g_ref[...])
        dx = dr_ref[...] + dx
        dx_ref[...] = dx
        _accumulate(dg_ref, dg, i == 0)
        if with_post:
            dff, dgp = _norm_bwd(post[2] * dx, f_ref[...], gp_ref[...])
            dff_ref[...] = dff.astype(BF16)
            _accumulate(dgp_ref, dgp, i == 0)

    row = pl.BlockSpec((tm, d), lambda i: (i, 0))
    vec = pl.BlockSpec((1, d), lambda i: (0, 0))
    ins, in_specs = [dres, dh, xin, gpre], [row, row, row, vec]
    out_specs = [row, vec]
    out_shape = [jax.ShapeDtypeStruct((t, d), F32), jax.ShapeDtypeStruct((1, d), F32)]
    if with_post:
        ins += [post[0], post[1]]
        in_specs += [row, vec]
        out_specs += [row, vec]
        out_shape += [jax.ShapeDtypeStruct((t, d), BF16), jax.ShapeDtypeStruct((1, d), F32)]
    body, ins, in_specs = _ordered(body, ins, in_specs, after)
    return _pallas(
        body, name=name, grid=(t // tm,), in_specs=in_specs, out_specs=out_specs, out_shape=out_shape,
        compiler_params=_params(("arbitrary",), 6 * _nbytes((tm, d), F32)),
    )(*ins)


SWIGLU_TILE = (1024, 512)
V7X_MXU_COLS = 256


def _ffn_gate_up_act(name, h, w_gu):
    t, d = h.shape
    f = w_gu.shape[1] // 2
    tm, tn = _tile(t, SWIGLU_TILE[0], 128), _tile(f, SWIGLU_TILE[1], 128)
    nf = f // tn

    def body(h_ref, wg_ref, wu_ref, g_ref, u_ref, a_ref):
        hv = h_ref[...]
        for c0 in range(0, tn, min(tn, V7X_MXU_COLS)):
            cols = slice(c0, c0 + min(tn, V7X_MXU_COLS))
            g = _dot(hv, wg_ref[:, cols])
            u = _dot(hv, wu_ref[:, cols])
            g_ref[:, cols] = g.astype(BF16)
            u_ref[:, cols] = u.astype(BF16)
            a_ref[:, cols] = (g * _sigmoid(g) * u).astype(BF16)

    out = jax.ShapeDtypeStruct((t, f), BF16)
    blk = pl.BlockSpec((tm, tn), lambda i, j: (i, j))
    return _pallas(
        body, name=name, grid=(t // tm, nf),
        in_specs=[pl.BlockSpec((tm, d), lambda i, j: (i, 0)), pl.BlockSpec((d, tn), lambda i, j: (0, j)),
                  pl.BlockSpec((d, tn), lambda i, j: (0, j + nf))],
        out_specs=[blk, blk, blk], out_shape=[out, out, out],
        compiler_params=_params(("parallel", "parallel"),
                                _nbytes((tm, d), BF16) + 2 * _nbytes((d, tn), BF16) + 5 * _nbytes((tm, tn), F32)),
    )(h, w_gu, w_gu)


def _ffn_dact(name, dff, w_down, gate, up, after=None):
    t, d = dff.shape
    f = w_down.shape[0]
    tm, tn = _tile(t, SWIGLU_TILE[0], 128), _tile(f, SWIGLU_TILE[1], 128)

    def body(d_ref, w_ref, g_ref, u_ref, o_ref):
        dv = d_ref[...]
        for c0 in range(0, tn, min(tn, V7X_MXU_COLS)):
            cols = slice(c0, c0 + min(tn, V7X_MXU_COLS))
            da = _dot(dv, w_ref[cols, :], 1, 1)
            g = g_ref[:, cols].astype(F32)
            u = u_ref[:, cols].astype(F32)
            sig = _sigmoid(g)
            o_ref[0, :, cols] = (da * u * sig * (1.0 + g * (1.0 - sig))).astype(BF16)
            o_ref[1, :, cols] = (da * g * sig).astype(BF16)

    blk = pl.BlockSpec((tm, tn), lambda i, j: (i, j))
    body, ins, in_specs = _ordered(
        body, [dff, w_down, gate, up],
        [pl.BlockSpec((tm, d), lambda i, j: (i, 0)), pl.BlockSpec((tn, d), lambda i, j: (j, 0)), blk, blk], after)
    return _pallas(
        body, name=name, grid=(t // tm, f // tn), in_specs=in_specs,
        out_specs=pl.BlockSpec((2, tm, tn), lambda i, j: (0, i, j)),
        out_shape=jax.ShapeDtypeStruct((2, t, f), BF16),
        compiler_params=_params(("parallel", "parallel"),
                                _nbytes((tm, d), BF16) + _nbytes((tn, d), BF16) + 5 * _nbytes((tm, tn), F32)),
    )(*ins)


def _ffn_dh(name, dgu, w_gu, after=None):
    _, t, f = dgu.shape
    d = w_gu.shape[0]
    tm, tn, tk = _mm_tiles(t, d, 2 * f, F32, k_unit=f)
    nkf = f // tk
    return _matmul(name, dgu, w_gu, form="nt", out_dtype=F32, tm=tm, tn=tn, tk=tk, sizes=(t, d, 2 * f),
                   a_map=lambda i, j, kk: (kk // nkf, i, kk % nkf), after=after)


def _ffn_dw_gate_up(name, h, dgu, after=None):
    _, t, f = dgu.shape
    d = h.shape[1]
    tm, tn, tk = _mm_tiles(d, 2 * f, t, BF16, n_unit=f)
    nf = f // tn
    return _matmul(name, h, dgu, form="tn", out_dtype=BF16, tm=tm, tn=tn, tk=tk, sizes=(d, 2 * f, t),
                   b_map=lambda i, j, kk: (j // nf, kk, j % nf), after=after)


def _lower_bound(lbp):
    m = jnp.max(lbp, axis=0, keepdims=True)
    e = jnp.exp(lbp - m)
    return e[0:1] / jnp.sum(e, axis=0, keepdims=True)


def _chunk_mask(reverse):
    row = lax.broadcasted_iota(jnp.int32, (CHUNK, CHUNK), 0)
    col = lax.broadcasted_iota(jnp.int32, (CHUNK, CHUNK), 1)
    return (col >= row) if reverse else (col <= row)


def _hgrn_gates(z, lb, mask_bf):
    sig = _sigmoid(z)
    f = lb + (1.0 - lb) * sig
    logf = jnp.log(f)
    k = 1.0 - f
    cum = _dot_exact(mask_bf, logf)
    last = jnp.sum(logf, axis=0, keepdims=True)
    return sig, f, k, cum, last


def _hgrn_scan_fwd(name, p, lbp_f, lbp_b):
    t = p.shape[0]
    hw = lbp_f.shape[1]
    nh, nc = hw // HEAD, t // CHUNK

    def body(qf, vf, zf, qb, vb, zb, lbf, lbb, of_ref, ob_ref, stf_ref, stb_ref, state):
        n = pl.program_id(0)

        @pl.when(n == 0)
        def _():
            state[...] = jnp.zeros_like(state)

        directions = [(qf, vf, zf, lbf, of_ref, stf_ref), (qb, vb, zb, lbb, ob_ref, stb_ref)]
        wide = []
        for d, (q_ref, v_ref, z_ref, lb_ref, o_ref, st_ref) in enumerate(directions):
            mask = _chunk_mask(d == 1)
            lb = _lower_bound(lb_ref[...])
            _, _, k, cum, last = _hgrn_gates(z_ref[...], lb, mask.astype(BF16))
            v = v_ref[...].astype(BF16)
            qd = (q_ref[...] * jnp.exp(cum)).astype(BF16)
            kd = (k * jnp.exp(-cum)).astype(BF16)
            kt = (k * jnp.exp(last - cum)).astype(BF16)
            s_all = state[d]
            st_ref[...] = s_all
            wide.append((mask, v, qd, kd, kt, jnp.exp(last), s_all, o_ref))
        pairs = [(d, slice(h * HEAD, (h + 1) * HEAD)) for d in range(2) for h in range(nh)]
        a = [jnp.where(wide[d][0], _dot(wide[d][2][:, sl], wide[d][3][:, sl], 1, 1), 0.0).astype(BF16)
             for d, sl in pairs]
        inter = [_dot(wide[d][2][:, sl], wide[d][6][:, sl].astype(BF16), 1, 1) for d, sl in pairs]
        intra = [_dot(a[i], wide[d][1][:, sl]) for i, (d, sl) in enumerate(pairs)]
        grow = [_dot(wide[d][1][:, sl], wide[d][4][:, sl], 0, 0) for d, sl in pairs]
        for i, (d, sl) in enumerate(pairs):
            wide[d][7][:, sl] = intra[i] + inter[i]
            state[d, :, sl] = wide[d][6][:, sl] * wide[d][5][:, sl] + grow[i]

    def col(group, reverse):
        return pl.BlockSpec((CHUNK, hw), lambda n: ((nc - 1 - n) if reverse else n, group))

    def st(reverse):
        return pl.BlockSpec((None, HEAD, hw), lambda n: ((nc - 1 - n) if reverse else n, 0, 0))

    lb_spec = pl.BlockSpec((2, hw), lambda n: (0, 0))
    out = jax.ShapeDtypeStruct((t, hw), F32)
    states = jax.ShapeDtypeStruct((nc, HEAD, hw), F32)
    return _pallas(
        body, name=name, grid=(nc,),
        in_specs=[col(0, False), col(1, False), col(2, False), col(0, True), col(1, True), col(3, True),
                  lb_spec, lb_spec],
        out_specs=[col(0, False), col(0, True), st(False), st(True)],
        out_shape=[out, out, states, states],
        scratch_shapes=[pltpu.VMEM((2, HEAD, hw), F32)],
        compiler_params=_params(("arbitrary",), 12 * _nbytes((HEAD, hw), F32)),
    )(p, p, p, p, p, p, lbp_f, lbp_b)


def _hgrn_scan_bwd(name, p, lbp_f, lbp_b, do, st_f, st_b):
    t = p.shape[0]
    hw = lbp_f.shape[1]
    nh, nc = hw // HEAD, t // CHUNK

    def body(qf, vf, zf, dof, sf, qb, vb, zb, dob, sb, lbf, lbb, dqf, dvf, dzf, dlbf, dqb, dvb, dzb, dlbb,
             dstate, dlb_acc, dqd_s, dkd_s, dkt_s, ddec_s):
        n = pl.program_id(0)

        @pl.when(n == 0)
        def _():
            dstate[...] = jnp.zeros_like(dstate)
            dlb_acc[...] = jnp.zeros_like(dlb_acc)

        directions = [(qf, vf, zf, dof, sf, lbf, dqf, dvf, dzf, dlbf), (qb, vb, zb, dob, sb, lbb, dqb, dvb, dzb, dlbb)]
        for d, (q_ref, v_ref, z_ref, do_ref, st_ref, lb_ref, dq_ref, dv_ref, dz_ref, dlb_ref) in enumerate(directions):
            mask = _chunk_mask(d == 1)
            mask_bf = mask.astype(BF16)
            lb = _lower_bound(lb_ref[...])
            sig, f, k, cum, last = _hgrn_gates(z_ref[...], lb, mask_bf)
            e_pos, e_neg, e_tail = jnp.exp(cum), jnp.exp(-cum), jnp.exp(last - cum)
            dec = jnp.exp(last)
            v = v_ref[...].astype(BF16)
            qd, kd, kt = q_ref[...] * e_pos, k * e_neg, k * e_tail
            qd_bf, kd_bf, kt_bf = qd.astype(BF16), kd.astype(BF16), kt.astype(BF16)
            s_all = st_ref[...]
            ds_all = dstate[d]
            dov = do_ref[...].astype(BF16)
            cols = [slice(h * HEAD, (h + 1) * HEAD) for h in range(nh)]
            s_bf = [s_all[:, sl].astype(BF16) for sl in cols]
            ds_bf = [ds_all[:, sl].astype(BF16) for sl in cols]
            a = [jnp.where(mask, _dot(qd_bf[:, sl], kd_bf[:, sl], 1, 1), 0.0).astype(BF16) for sl in cols]
            da = [jnp.where(mask, _dot(dov[:, sl], v[:, sl], 1, 1), 0.0).astype(BF16) for sl in cols]
            dv_h = [_dot(a[h], dov[:, sl], 0, 0) + _dot(kt_bf[:, sl], ds_bf[h], 1, 1) for h, sl in enumerate(cols)]
            dqd_h = [_dot(da[h], kd_bf[:, sl]) + _dot(dov[:, sl], s_bf[h]) for h, sl in enumerate(cols)]
            dkd_h = [_dot(da[h], qd_bf[:, sl], 0, 0) for h, sl in enumerate(cols)]
            dkt_h = [_dot(v[:, sl], ds_bf[h]) for h, sl in enumerate(cols)]
            dst_h = [_dot(dov[:, sl], qd_bf[:, sl], 0, 0) + ds_all[:, sl] * dec[:, sl] for sl in cols]
            for h, sl in enumerate(cols):
                dv_ref[:, sl] = dv_h[h]
                dqd_s[:, sl] = dqd_h[h]
                dkd_s[:, sl] = dkd_h[h]
                dkt_s[:, sl] = dkt_h[h]
                dstate[d, :, sl] = dst_h[h]
                ddec_s[:, sl] = jnp.sum(ds_all[:, sl] * s_all[:, sl], axis=0, keepdims=True)
            dqd, dkd, dkt = dqd_s[...], dkd_s[...], dkt_s[...]
            dlast = jnp.sum(dkt * kt, axis=0, keepdims=True) + dec * ddec_s[...]
            dq_ref[...] = dqd * e_pos
            dk = dkd * e_neg + dkt * e_tail
            dcum = dqd * qd - dkd * kd - dkt * kt
            dlogf = _dot_exact(mask_bf, dcum, 0, 0) + dlast
            df = dlogf / f - dk
            dz_ref[...] = df * (1.0 - lb) * sig * (1.0 - sig)
            dlb_acc[d] += jnp.sum(df * (1.0 - sig), axis=0, keepdims=True)

            @pl.when(n == nc - 1)
            def _():
                g = dlb_acc[d] * lb * (1.0 - lb)
                dlb_ref[0:1, :] = g
                dlb_ref[1:2, :] = -g

    def col(group, reverse):
        return pl.BlockSpec((CHUNK, hw), lambda n: (n if reverse else (nc - 1 - n), group))

    def st(reverse):
        return pl.BlockSpec((None, HEAD, hw), lambda n: (n if reverse else (nc - 1 - n), 0, 0))

    lb_spec = pl.BlockSpec((2, hw), lambda n: (0, 0))
    out = jax.ShapeDtypeStruct((t, hw), F32)
    dlb = jax.ShapeDtypeStruct((2, hw), F32)
    wide = pltpu.VMEM((CHUNK, hw), F32)
    return _pallas(
        body, name=name, grid=(nc,),
        in_specs=[col(0, False), col(1, False), col(2, False), col(0, False), st(False),
                  col(0, True), col(1, True), col(3, True), col(0, True), st(True), lb_spec, lb_spec],
        out_specs=[col(0, False), col(0, False), col(0, False), lb_spec,
                   col(0, True), col(0, True), col(0, True), lb_spec],
        out_shape=[out, out, out, dlb, out, out, out, dlb],
        scratch_shapes=[pltpu.VMEM((2, HEAD, hw), F32), pltpu.VMEM((2, 1, hw), F32), wide, wide, wide,
                        pltpu.VMEM((1, hw), F32)],
        compiler_params=_params(("arbitrary",), 16 * _nbytes((HEAD, hw), F32)),
    )(p, p, p, do, st_f, p, p, p, do, st_b, lbp_f, lbp_b)


def _hgrn_out_fwd(name, o_f, o_b, p, gain, g_group):
    t, hw = o_f.shape
    nh = hw // HEAD
    tm = _tile(t, 512, 8)

    def body(of_ref, ob_ref, g_ref, gain_ref, y_ref):
        o = of_ref[...] + ob_ref[...]
        g = g_ref[...]
        y_ref[...] = (o * _rms(o) * gain_ref[...] * (g * _sigmoid(g))).astype(BF16)

    blk = pl.BlockSpec((tm, HEAD), lambda i, h: (i, h))
    return _pallas(
        body, name=name, grid=(t // tm, nh),
        in_specs=[blk, blk, pl.BlockSpec((tm, HEAD), lambda i, h: (i, g_group * nh + h)),
                  pl.BlockSpec((1, HEAD), lambda i, h: (0, h))],
        out_specs=blk, out_shape=jax.ShapeDtypeStruct((t, hw), BF16),
        compiler_params=_params(("parallel", "parallel"), 1 << 20),
    )(o_f, o_b, p, gain)


def _hgrn_out_bwd(name, dy, o_f, o_b, p, gain, g_group, after=None):
    t, hw = o_f.shape
    nh = hw // HEAD
    tm = _tile(t, 512, 8)

    def body(dy_ref, of_ref, ob_ref, g_ref, gain_ref, do_ref, dg_ref, dgain_ref):
        i = pl.program_id(1)
        o = of_ref[...] + ob_ref[...]
        g = g_ref[...]
        gain_v = gain_ref[...]
        sig = _sigmoid(g)
        dyv = dy_ref[...]
        do, dgain = _norm_bwd(dyv * (g * sig), o, gain_v)
        do_ref[...] = do
        dg_ref[...] = dyv * (o * _rms(o) * gain_v) * sig * (1.0 + g * (1.0 - sig))
        _accumulate(dgain_ref, dgain, i == 0)

    blk = pl.BlockSpec((tm, HEAD), lambda h, i: (i, h))
    vec = pl.BlockSpec((1, HEAD), lambda h, i: (0, h))
    out = jax.ShapeDtypeStruct((t, hw), F32)
    body, ins, in_specs = _ordered(
        body, [dy, o_f, o_b, p, gain],
        [blk, blk, blk, pl.BlockSpec((tm, HEAD), lambda h, i: (i, g_group * nh + h)), vec], after)
    return _pallas(
        body, name=name, grid=(nh, t // tm), in_specs=in_specs,
        out_specs=[blk, blk, vec], out_shape=[out, out, jax.ShapeDtypeStruct((1, hw), F32)],
        compiler_params=_params(("parallel", "arbitrary"), 1 << 20),
    )(*ins)


def _t5_bucket_ids():
    c = np.arange(WINDOW)[:, None]
    s = np.arange(SPAN)[None, :]
    rel = s - WINDOW - c
    nb = REL_BUCKETS // 2
    max_exact = nb // 2
    bucket = (rel > 0).astype(np.int32) * nb
    n = np.abs(rel)
    large = max_exact + (np.log(np.maximum(n, 1) / max_exact) / np.log(REL_MAX_DIST / max_exact)
                         * (nb - max_exact)).astype(np.int32)
    large = np.minimum(large, nb - 1)
    ids = bucket + np.where(n < max_exact, n, large).astype(np.int32)
    return jnp.asarray(ids.reshape(1, WINDOW * SPAN), jnp.int32)


def _bias_onehot(ids_ref):
    n = ids_ref.shape[1]
    return (lax.broadcasted_iota(jnp.int32, (REL_BUCKETS, n), 0) == ids_ref[...]).astype(BF16)


def _bias_gather(name, table_t, ids):
    nh = table_t.shape[0]

    def body(t_ref, ids_ref, o_ref):
        o_ref[...] = _dot_exact(t_ref[...], _bias_onehot(ids_ref), split="a")

    return _pallas(
        body, name=name, out_shape=jax.ShapeDtypeStruct((nh, ids.shape[1]), F32),
        compiler_params=pltpu.CompilerParams(vmem_limit_bytes=32 << 20),
    )(table_t, ids)


def _bias_scatter(name, dbias, ids):
    nh = dbias.shape[0]

    def body(d_ref, ids_ref, o_ref):
        o_ref[...] = _dot_exact(d_ref[...], _bias_onehot(ids_ref), 1, 1, split="a")

    return _pallas(
        body, name=name, out_shape=jax.ShapeDtypeStruct((nh, REL_BUCKETS), F32),
        compiler_params=pltpu.CompilerParams(vmem_limit_bytes=32 << 20),
    )(dbias, ids)


def _attn_valid(i, t):
    c = lax.broadcasted_iota(jnp.int32, (WINDOW, SPAN), 0)
    s = lax.broadcasted_iota(jnp.int32, (WINDOW, SPAN), 1)
    rel = s - WINDOW - c
    pos = i * WINDOW - WINDOW + s
    return (jnp.abs(rel) <= WINDOW) & (pos >= 0) & (pos < t)


def _attn_probs(qs, khs, b_ref, s_ref, valid):
    heads = range(len(qs))
    sinks = [s_ref[0:1, h:h + 1] for h in heads]
    s = [_dot(qs[h], khs[h], 1, 1) / math.sqrt(HEAD) for h in heads]
    s = [jnp.where(valid, s[h] + b_ref[h], NEG_INF) for h in heads]
    m = [jnp.maximum(jnp.max(s[h], axis=-1, keepdims=True), sinks[h]) for h in heads]
    e = [jnp.exp(s[h] - m[h]) for h in heads]
    es = [jnp.exp(sinks[h] - m[h]) for h in heads]
    inv = [1.0 / (jnp.sum(e[h], axis=-1, keepdims=True) + es[h]) for h in heads]
    return [e[h] * inv[h] for h in heads], [es[h] * inv[h] for h in heads]


def _attn_fwd(name, p, k_pad, v_pad, bias, sink, q_group_blk):
    t = p.shape[0]
    nh = bias.shape[0]
    aw = nh * HEAD
    grp = nh // KV_HEADS
    nb = t // WINDOW

    def body(q_ref, k_ref, v_ref, b_ref, s_ref, y_ref):
        i = pl.program_id(0)
        valid = _attn_valid(i, t)
        start = pl.multiple_of(i * WINDOW, WINDOW)
        ks = k_ref[pl.ds(start, SPAN), :]
        vs = v_ref[pl.ds(start, SPAN), :]
        heads = range(nh)
        col = lambda h: slice(h * HEAD, (h + 1) * HEAD)
        qs = [q_ref[:, col(h)].astype(BF16) for h in heads]
        pr, _ = _attn_probs(qs, [ks[:, col(h // grp)] for h in heads], b_ref, s_ref, valid)
        out = [_dot(pr[h].astype(BF16), vs[:, col(h // grp)]) for h in heads]
        for h in heads:
            y_ref[:, col(h)] = out[h].astype(BF16)

    full = lambda a: pl.BlockSpec(a.shape, lambda i: (0,) * a.ndim)
    return _pallas(
        body, name=name, grid=(nb,),
        in_specs=[pl.BlockSpec((WINDOW, aw), lambda i: (i, q_group_blk)), full(k_pad), full(v_pad), full(bias),
                  full(sink)],
        out_specs=pl.BlockSpec((WINDOW, aw), lambda i: (i, 0)),
        out_shape=jax.ShapeDtypeStruct((t, aw), BF16),
        compiler_params=_params(("parallel",), _nbytes(k_pad.shape, BF16) * 2 + _nbytes(bias.shape, F32)),
    )(p, k_pad, v_pad, bias, sink)


def _attn_bwd(name, p, k_pad, v_pad, bias, sink, dy, q_group_blk, dy_blk, after=None):
    t = p.shape[0]
    nh = bias.shape[0]
    aw = nh * HEAD
    grp = nh // KV_HEADS
    nb = t // WINDOW
    kvw = k_pad.shape[1]

    def body(q_ref, k_ref, v_ref, b_ref, s_ref, dy_ref, dq_ref, dk_ref, dv_ref, db_ref, ds_ref):
        i = pl.program_id(0)

        @pl.when(i == 0)
        def _():
            dk_ref[...] = jnp.zeros_like(dk_ref)
            dv_ref[...] = jnp.zeros_like(dv_ref)
            db_ref[...] = jnp.zeros_like(db_ref)
            ds_ref[...] = jnp.zeros_like(ds_ref)

        valid = _attn_valid(i, t)
        start = pl.multiple_of(i * WINDOW, WINDOW)
        ks = k_ref[pl.ds(start, SPAN), :]
        vs = v_ref[pl.ds(start, SPAN), :]
        inv_sqrt = 1.0 / math.sqrt(HEAD)
        heads = range(nh)
        col = lambda h: slice(h * HEAD, (h + 1) * HEAD)
        qs = [q_ref[:, col(h)].astype(BF16) for h in heads]
        khs = [ks[:, col(h // grp)] for h in heads]
        pr, ps = _attn_probs(qs, khs, b_ref, s_ref, valid)
        dos = [dy_ref[:, col(h)].astype(BF16) for h in heads]
        dp = [_dot(dos[h], vs[:, col(h // grp)], 1, 1) for h in heads]
        delta = [jnp.sum(pr[h] * dp[h], axis=-1, keepdims=True) for h in heads]
        dsc = [pr[h] * (dp[h] - delta[h]) for h in heads]
        dsr = [(dsc[h] * inv_sqrt).astype(BF16) for h in heads]
        dq = [_dot(dsr[h], khs[h]) for h in heads]
        dk = [_dot(dsr[h], qs[h], 0, 0) for h in heads]
        dv = [_dot(pr[h].astype(BF16), dos[h], 0, 0) for h in heads]
        for h in heads:
            db_ref[h] += dsc[h]
            ds_ref[h:h + 1, :] += jnp.broadcast_to(jnp.sum(-ps[h] * delta[h], axis=0, keepdims=True), (1, 128))
            dq_ref[:, col(h)] = dq[h]
        for kv in range(KV_HEADS):
            group = range(kv * grp, (kv + 1) * grp)
            dk_ref[pl.ds(start, SPAN), col(kv)] += sum(dk[h] for h in group)
            dv_ref[pl.ds(start, SPAN), col(kv)] += sum(dv[h] for h in group)

    full = lambda a: pl.BlockSpec(a.shape, lambda i: (0,) * a.ndim)
    whole = lambda shape: pl.BlockSpec(shape, lambda i: (0,) * len(shape))
    pad_shape = (t + 2 * WINDOW, kvw)
    body, ins, in_specs = _ordered(
        body, [p, k_pad, v_pad, bias, sink, dy],
        [pl.BlockSpec((WINDOW, aw), lambda i: (i, q_group_blk)), full(k_pad), full(v_pad), full(bias), full(sink),
         pl.BlockSpec((WINDOW, aw), lambda i: (i, dy_blk))], after)
    return _pallas(
        body, name=name, grid=(nb,), in_specs=in_specs,
        out_specs=[pl.BlockSpec((WINDOW, aw), lambda i: (i, 0)), whole(pad_shape), whole(pad_shape),
                   whole(bias.shape), whole((nh, 128))],
        out_shape=[jax.ShapeDtypeStruct((t, aw), F32), jax.ShapeDtypeStruct(pad_shape, F32),
                   jax.ShapeDtypeStruct(pad_shape, F32), jax.ShapeDtypeStruct(bias.shape, F32),
                   jax.ShapeDtypeStruct((nh, 128), F32)],
        compiler_params=_params(("arbitrary",), 3 * _nbytes(pad_shape, F32) + 2 * _nbytes(bias.shape, F32)),
    )(*ins)


def _pad_kv(name, p, kv_blk, kvw):
    t = p.shape[0]
    nb = t // WINDOW

    def body(x_ref, o_ref):
        i = pl.program_id(0)
        inside = jnp.logical_and(i >= 1, i <= nb)
        o_ref[...] = jnp.where(inside, x_ref[...], 0.0).astype(BF16)

    return _pallas(
        body, name=name, grid=(nb + 2,),
        in_specs=[pl.BlockSpec((WINDOW, kvw), lambda i: (jnp.clip(i - 1, 0, nb - 1), kv_blk))],
        out_specs=pl.BlockSpec((WINDOW, kvw), lambda i: (i, 0)),
        out_shape=jax.ShapeDtypeStruct((t + 2 * WINDOW, kvw), BF16),
        compiler_params=_params(("parallel",), 1 << 20),
    )(p)


def _mix_dproj(name, pieces, kv_pads, t, after=None):
    hw = pieces[0][0].shape[1]
    kvw = kv_pads[0].shape[1]
    widths = [hw] * len(pieces) + [kvw] * len(kv_pads)
    total = sum(widths)
    tm = WINDOW
    flat = [a for pc in pieces for a in pc]

    def body(*refs):
        o_ref = refs[-1]
        pos, off = 0, 0
        for pc in pieces:
            val = refs[pos][...]
            for extra in range(1, len(pc)):
                val = val + refs[pos + extra][...]
            o_ref[:, off:off + hw] = val.astype(BF16)
            pos += len(pc)
            off += hw
        for _ in kv_pads:
            o_ref[:, off:off + kvw] = refs[pos][...].astype(BF16)
            pos += 1
            off += kvw

    in_specs = [pl.BlockSpec((tm, hw), lambda i: (i, 0)) for _ in flat]
    in_specs += [pl.BlockSpec((tm, kvw), lambda i: (i + 1, 0)) for _ in kv_pads]
    body, ins, in_specs = _ordered(body, [*flat, *kv_pads], in_specs, after)
    return _pallas(
        body, name=name, grid=(t // tm,), in_specs=in_specs,
        out_specs=pl.BlockSpec((tm, total), lambda i: (i, 0)),
        out_shape=jax.ShapeDtypeStruct((t, total), BF16),
        compiler_params=_params(("parallel",), 3 * _nbytes((tm, total), F32)),
    )(*ins)


def _concat_cols(name, a, b):
    t, wa = a.shape
    wb = b.shape[1]
    tm = _tile(t, 512, 16)

    def body(a_ref, b_ref, o_ref):
        o_ref[:, :wa] = a_ref[...]
        o_ref[:, wa:] = b_ref[...]

    return _pallas(
        body, name=name, grid=(t // tm,),
        in_specs=[pl.BlockSpec((tm, wa), lambda i: (i, 0)), pl.BlockSpec((tm, wb), lambda i: (i, 0))],
        out_specs=pl.BlockSpec((tm, wa + wb), lambda i: (i, 0)),
        out_shape=jax.ShapeDtypeStruct((t, wa + wb), a.dtype),
        compiler_params=_params(("parallel",), 2 * _nbytes((tm, wa + wb), a.dtype)),
    )(a, b)


def _cast_into_full(name, w, geom, idx, after=None):
    r, c = w.shape
    tr = _tile(r, 256, 16)
    nr = r // tr
    if geom.col:
        place = lambda i, iref: (i, iref[0])
    else:
        place = lambda i, iref: (iref[0] * nr + i, 0)

    def body(i_ref, w_ref, *rest):
        rest[-1][...] = w_ref[...].astype(BF16)

    in_specs = [pl.BlockSpec((tr, c), lambda i, iref: (i, 0))]
    ins = [w]
    if after is not None:
        in_specs.append(pl.BlockSpec(memory_space=pl.ANY))
        ins.append(after)
    return _pallas(
        body, name=name,
        grid_spec=pltpu.PrefetchScalarGridSpec(
            num_scalar_prefetch=1, grid=(nr,), in_specs=in_specs, out_specs=pl.BlockSpec((tr, c), place)),
        out_shape=pltpu.HBM(geom.full_shape, BF16),
        compiler_params=_params(("parallel",), 2 * _nbytes((tr, c), F32)),
    )(idx, *ins)


def _adamw(name, w, g, m, v):
    r, c = w.shape
    tr = _tile(r, 128, 8)
    bc1 = 1.0 - ADAM_B1 ** ADAM_STEP
    bc2 = 1.0 - ADAM_B2 ** ADAM_STEP

    def body(w_ref, g_ref, m_ref, v_ref, go_ref, d_ref, nm_ref, nv_ref):
        gv = g_ref[...]
        go_ref[...] = gv
        nm = ADAM_B1 * m_ref[...] + (1.0 - ADAM_B1) * gv
        nv = ADAM_B2 * v_ref[...] + (1.0 - ADAM_B2) * (gv * gv)
        nm_ref[...] = nm
        nv_ref[...] = nv
        d_ref[...] = -ADAM_LR * ((nm / bc1) / (jnp.sqrt(nv / bc2) + ADAM_EPS) + ADAM_WD * w_ref[...])

    blk = pl.BlockSpec((tr, c), lambda i: (i, 0))
    out = jax.ShapeDtypeStruct((r, c), F32)
    return _pallas(
        body, name=name, grid=(r // tr,), in_specs=[blk] * 4, out_specs=[blk] * 4, out_shape=[out] * 4,
        compiler_params=_params(("parallel",), 8 * _nbytes((tr, c), F32)),
    )(w, g, m, v)


def _mesh_pos():
    return lax.axis_index("x"), lax.axis_index("y"), lax.axis_index("c")


def _other_chips(x, y):
    return [(1 - x, y), (x, 1 - y), (1 - x, 1 - y)]


class _Big:
    def __init__(self, shard_shape, col_sharded):
        self.col = col_sharded
        r, c = shard_shape
        self.shard_shape = (r, c)
        self.full_shape = (r, N_CHIPS * c) if col_sharded else (N_CHIPS * r, c)
        self.half_shape = (r // 2, N_CHIPS * c) if col_sharded else (N_CHIPS * r, c // 2)
        self.shard_half_shape = (r // 2, c) if col_sharded else (r, c // 2)

    def region(self, ref, s, half=None):
        r, c = self.shard_shape
        if self.col:
            rows = slice(None) if half is None else pl.ds(half * (r // 2), r // 2)
            return ref.at[rows, pl.ds(s * c, c)]
        cols = slice(None) if half is None else pl.ds(half * (c // 2), c // 2)
        return ref.at[pl.ds(s * r, r), cols]

    def n_halves(self, ref, half, n):
        r, c = self.shard_shape
        if self.col:
            return ref.at[pl.ds(half * (r // 2), r // 2), pl.ds(0, n * c)]
        return ref.at[pl.ds(0, n * r), pl.ds(half * (c // 2), c // 2)]

    def three_halves(self, ref, half):
        return self.n_halves(ref, half, 3)

    def sub_half(self, ref, s, half, j):
        r, c = self.shard_shape
        if self.col:
            return ref.at[pl.ds(half * (r // 2) + j * (r // 4), r // 4), pl.ds(s * c, c)]
        return ref.at[pl.ds(s * r + j * (r // 2), r // 2), pl.ds(half * (c // 2), c // 2)]

    def half_of_full(self, ref, half):
        r, c = self.full_shape
        if self.col:
            return ref.at[pl.ds(half * (r // 2), r // 2), :]
        return ref.at[:, pl.ds(half * (c // 2), c // 2)]

    def half_of_shard(self, ref, half):
        r, c = self.shard_shape
        if self.col:
            return ref.at[pl.ds(half * (r // 2), r // 2), :]
        return ref.at[:, pl.ds(half * (c // 2), c // 2)]

    def shard_of_half(self, ref, s):
        r, c = self.shard_shape
        if self.col:
            return ref.at[:, pl.ds(s * c, c)]
        return ref.at[pl.ds(s * r, r), :]


HBM =pl.BlockSpec(memory_space=pltpu.HBM)
SEM = pl.BlockSpec(memory_space=pltpu.SEMAPHORE)
SPLIT_COPY = pltpu.CompilerParams(has_side_effects=pltpu.SideEffectType.DATAFLOW_SIDE_EFFECTING)


def _in_hbm(a):
    return pltpu.with_memory_space_constraint(a, pltpu.HBM)


def _gather_start(name, fulls, geoms, after):
    nw = len(fulls)

    def body(*refs):
        dst = refs[nw + 1:2 * nw + 1]
        sems = refs[2 * nw + 1:-1]
        x, y, c = _mesh_pos()
        mine = 2 * x + y
        for w in range(nw):
            own_half = geoms[w].region(dst[w], mine, c)
            for chip in _other_chips(x, y):
                pltpu.make_async_remote_copy(src_ref=own_half, dst_ref=own_half, send_sem=sems[2 * w],
                                             recv_sem=sems[2 * w + 1], device_id=(*chip, c),
                                             device_id_type=MESH).start()
        refs[-1][...] = jnp.zeros_like(refs[-1])

    out = _pallas(
        body, name=name, in_specs=[HBM] * nw + [pl.BlockSpec(memory_space=pl.ANY)],
        out_specs=[HBM] * nw + [SEM] * (2 * nw) + [pl.BlockSpec(memory_space=pltpu.VMEM)],
        out_shape=[pltpu.HBM(g.full_shape, BF16) for g in geoms] + [pltpu.SemaphoreType.DMA(())] * (2 * nw)
        + [jax.ShapeDtypeStruct((8, 128), F32)],
        input_output_aliases={w: w for w in range(nw)}, compiler_params=SPLIT_COPY,
    )(*[_in_hbm(a) for a in fulls], after)
    return list(out[:nw]), [(out[nw + 2 * w], out[nw + 2 * w + 1]) for w in range(nw)], out[-1]


def _wait_three(geom, ref, half, send_sem, recv_sem, peer, recv):
    three = geom.three_halves(ref, half)
    copy = pltpu.make_async_remote_copy(src_ref=three, dst_ref=three, send_sem=send_sem, recv_sem=recv_sem,
                                        device_id=peer, device_id_type=MESH)
    if recv:
        copy.wait_recv()
    else:
        copy.wait_send()


def _gather_first_direct(full, geom):
    def start(refs, _, new):
        x, y, c = _mesh_pos()
        own = geom.region(refs[0], 2 * x + y, c)
        for chip in ((1 - x, y), (x, 1 - y)):
            _remote(own, own, new, (*chip, c)).start()

    return _split_copy_call("gather_first_direct", [full], start, new_sems=2)


def _gather_first_relay(full, geom, sems, after):
    def relay(refs, got, new):
        x, y, c = _mesh_pos()
        w = refs[0]
        two = geom.n_halves(w, c, 2)
        _remote(two, two, got, (x, y, 1 - c)).wait_recv()
        from_x = geom.sub_half(w, 2 * (1 - x) + y, c, 0)
        from_y = geom.sub_half(w, 2 * x + (1 - y), c, 1)
        _remote(from_x, from_x, new, (x, 1 - y, c)).start()
        _remote(from_y, from_y, new, (1 - x, y, c)).start()
        _remote(two, two, got, (x, y, 1 - c)).wait_send()

    return _split_copy_call("gather_first_relay", [full], relay, sems=sems, after=after, new_sems=2)


def _gather_forward(name, full, geom, sems, after, arrivals=3):
    def body(w_in, send_sem, recv_sem, after_ref, w_ref, fwd_send, fwd_recv):
        x, y, c = _mesh_pos()
        sibling = (x, y, 1 - c)
        landed_all = geom.n_halves(w_ref, c, arrivals)
        _remote(landed_all, landed_all, (send_sem, recv_sem), sibling).wait_recv()
        for chip in _other_chips(x, y):
            landed = geom.region(w_ref, 2 * chip[0] + chip[1], c)
            pltpu.make_async_remote_copy(src_ref=landed, dst_ref=landed, send_sem=fwd_send, recv_sem=fwd_recv,
                                         device_id=sibling, device_id_type=MESH).start()
        _remote(landed_all, landed_all, (send_sem, recv_sem), sibling).wait_send()

    sem = pltpu.SemaphoreType.DMA(())
    out = _pallas(
        body, name=name, in_specs=[HBM, SEM, SEM, pl.BlockSpec(memory_space=pl.ANY)], out_specs=[HBM, SEM, SEM],
        out_shape=[pltpu.HBM(geom.full_shape, BF16), sem, sem],
        input_output_aliases={0: 0}, compiler_params=SPLIT_COPY,
    )(full, sems[0], sems[1], after)
    return out[0], (out[1], out[2])


def _gather_end(name, full, geom, sems, after):
    def body(w_in, fwd_send, fwd_recv, after_ref, w_ref):
        x, y, c = _mesh_pos()
        sibling = (x, y, 1 - c)
        _wait_three(geom, w_ref, 1 - c, fwd_send, fwd_recv, sibling, recv=True)
        _wait_three(geom, w_ref, c, fwd_send, fwd_recv, sibling, recv=False)

    return _pallas(
        body, name=name, in_specs=[HBM, SEM, SEM, pl.BlockSpec(memory_space=pl.ANY)], out_specs=HBM,
        out_shape=pltpu.HBM(geom.full_shape, BF16),
        input_output_aliases={0: 0}, compiler_params=SPLIT_COPY,
    )(full, sems[0], sems[1], after)


def _split_copy_call(name, arrays, fn, sems=(), after=None, new_sems=0):
    n, ns = len(arrays), len(sems)
    n_in = n + ns + (after is not None)

    def body(*refs):
        fn(refs[n_in:n_in + n], refs[n:n + ns], refs[n_in + n:-1])
        refs[-1][...] = jnp.zeros_like(refs[-1])

    ins = list(arrays) if ns else [_in_hbm(a) for a in arrays]
    ins += list(sems) + ([after] if after is not None else [])
    in_specs = [HBM] * n + [SEM] * ns + ([pl.BlockSpec(memory_space=pl.ANY)] if after is not None else [])
    out = _pallas(
        body, name=name, in_specs=in_specs,
        out_specs=[HBM] * n + [SEM] * new_sems + [pl.BlockSpec(memory_space=pltpu.VMEM)],
        out_shape=[pltpu.HBM(a.shape, a.dtype) for a in arrays] + [pltpu.SemaphoreType.DMA(())] * new_sems
        + [jax.ShapeDtypeStruct((8, 128), F32)],
        input_output_aliases={i: i for i in range(n)}, compiler_params=SPLIT_COPY,
    )(*ins)
    return list(out[:n]), tuple(out[n:-1]), out[-1]


def _remote(src, dst, sems, to):
    return pltpu.make_async_remote_copy(src_ref=src, dst_ref=dst, send_sem=sems[0], recv_sem=sems[1],
                                        device_id=to, device_id_type=MESH)


class _GradReduce:
    def __init__(self, name, geom, idx, c_idx):
        self.name, self.geom, self.idx, self.c_idx = name, geom, idx, c_idx

    def pair_start(self, dw):
        g = self.geom

        def start(refs, _, new):
            x, y, c = _mesh_pos()
            _remote(g.half_of_full(refs[0], 1 - c), refs[1], new, (x, y, 1 - c)).start()

        self.arrays, self.sems, token = _split_copy_call(
            f"pair_start_{self.name}", [dw, lax.empty(g.half_shape, BF16)], start, new_sems=2)
        return token

    def pair_finish(self, after):
        g = self.geom

        def wait(refs, sems, _):
            x, y, c = _mesh_pos()
            copy = _remote(g.half_of_full(refs[0], 1 - c), refs[1], sems, (x, y, 1 - c))
            copy.wait_send()
            copy.wait_recv()

        (dw, landed), _, _ = _split_copy_call(f"pair_wait_{self.name}", self.arrays, wait, self.sems, after)
        half = _pair_add(f"pair_add_{self.name}", dw, landed, g, self.c_idx)

        def start(refs, _, new):
            x, y, c = _mesh_pos()
            for k, chip in enumerate(_other_chips(x, y)):
                _remote(g.shard_of_half(refs[0], 2 * chip[0] + chip[1]), refs[1].at[k], new, (*chip, c)).start()

        self.arrays, self.sems, token = _split_copy_call(
            f"chip_start_{self.name}", [half, lax.empty((3,) + g.shard_half_shape, BF16)], start, new_sems=2)
        return token

    def chip_finish(self, after):
        g = self.geom

        def wait(refs, sems, _):
            x, y, c = _mesh_pos()
            three = _remote(refs[1], refs[1], sems, (x, y, 1 - c))
            three.wait_send()
            three.wait_recv()

        (half, landed), _, _ = _split_copy_call(f"chip_wait_{self.name}", self.arrays, wait, self.sems, after)
        quarter = _chip_add(f"chip_add_{self.name}", half, landed, g, self.idx)

        def start(refs, _, new):
            x, y, c = _mesh_pos()
            own = g.half_of_shard(refs[0], c)
            _remote(own, own, new, (x, y, 1 - c)).start()

        self.arrays, self.sems, token = _split_copy_call(f"share_start_{self.name}", [quarter], start, new_sems=2)
        return token

    def finish(self, after):
        g = self.geom

        def wait(refs, sems, _):
            x, y, c = _mesh_pos()
            own, theirs = g.half_of_shard(refs[0], c), g.half_of_shard(refs[0], 1 - c)
            _remote(own, own, sems, (x, y, 1 - c)).wait_send()
            _remote(theirs, theirs, sems, (x, y, 1 - c)).wait_recv()

        (quarter,), _, _ = _split_copy_call(f"share_wait_{self.name}", self.arrays, wait, self.sems, after)
        return quarter


def _pair_add(name, grad, recv, geom, c_idx):
    r, c = geom.half_shape
    tr, tc = _tile(r, 512, 16), _tile(c, 2048, 128)
    nr, ncol = r // tr, c // tc
    if geom.col:
        mine = lambda i, j, cref: (cref[0] * nr + i, j)
    else:
        mine = lambda i, j, cref: (i, cref[0] * ncol + j)

    def body(c_ref, g_ref, r_ref, o_ref):
        o_ref[...] = (g_ref[...].astype(F32) + r_ref[...].astype(F32)).astype(BF16)

    return _pallas(
        body, name=name,
        grid_spec=pltpu.PrefetchScalarGridSpec(
            num_scalar_prefetch=1, grid=(nr, ncol),
            in_specs=[pl.BlockSpec((tr, tc), mine), pl.BlockSpec((tr, tc), lambda i, j, cref: (i, j))],
            out_specs=pl.BlockSpec((tr, tc), lambda i, j, cref: (i, j))),
        out_shape=jax.ShapeDtypeStruct((r, c), BF16),
        compiler_params=_params(("parallel", "parallel"), 3 * _nbytes((tr, tc), F32)),
    )(c_idx, grad, recv)


def _chip_add(name, half, recv, geom, idx):
    r, c = geom.shard_half_shape
    tr, tc = _tile(r, 512, 16), _tile(c, 2048, 128)
    nr, ncol = r // tr, c // tc
    if geom.col:
        mine = lambda i, j, iref: (i, iref[0] * ncol + j)
        place = lambda i, j, iref: (iref[1] * nr + i, j)
    else:
        mine = lambda i, j, iref: (iref[0] * nr + i, j)
        place = lambda i, j, iref: (i, iref[1] * ncol + j)

    def body(i_ref, h_ref, r_ref, o_ref):
        acc = h_ref[...].astype(F32)
        for k in range(3):
            acc = acc + r_ref[k].astype(F32)
        o_ref[...] = acc

    return _pallas(
        body, name=name,
        grid_spec=pltpu.PrefetchScalarGridSpec(
            num_scalar_prefetch=1, grid=(nr, ncol),
            in_specs=[pl.BlockSpec((tr, tc), mine), pl.BlockSpec((3, tr, tc), lambda i, j, iref: (0, i, j))],
            out_specs=pl.BlockSpec((tr, tc), place)),
        out_shape=jax.ShapeDtypeStruct(geom.shard_shape, F32),
        compiler_params=_params(("parallel", "parallel"), 4 * _nbytes((tr, tc), F32)),
    )(idx, half, recv)


def _all_reduce_small(pack, after=None):
    r, d = pack.shape

    def body(p_ref, o_ref, slots, send_sems, recv_sems):
        x, y, c = _mesh_pos()
        me = 4 * x + 2 * y + c
        slots[me] = p_ref[...]
        copies = []
        for k in range(1, N_DEV):
            px, py, pc = x ^ ((k >> 2) & 1), y ^ ((k >> 1) & 1), c ^ (k & 1)
            copies.append(pltpu.make_async_remote_copy(
                src_ref=p_ref, dst_ref=slots.at[me], send_sem=send_sems.at[k - 1], recv_sem=recv_sems.at[k - 1],
                device_id=(px, py, pc), device_id_type=MESH))
        for cp in copies:
            cp.start()
        for k in range(1, N_DEV):
            peer = 4 * (x ^ ((k >> 2) & 1)) + 2 * (y ^ ((k >> 1) & 1)) + (c ^ (k & 1))
            pltpu.make_async_remote_copy(
                src_ref=p_ref, dst_ref=slots.at[peer], send_sem=send_sems.at[k - 1], recv_sem=recv_sems.at[k - 1],
                device_id=(x, y, c), device_id_type=MESH).wait_recv()
        for cp in copies:
            cp.wait_send()
        acc = slots[0]
        for k in range(1, N_DEV):
            acc = acc + slots[k]
        o_ref[...] = acc

    vm = pl.BlockSpec(memory_space=pltpu.VMEM)
    body, ins, in_specs = _ordered(body, [pack], [vm], after)
    return _pallas(
        body, name="all_reduce_small", in_specs=in_specs, out_specs=vm,
        out_shape=jax.ShapeDtypeStruct((r, d), F32),
        scratch_shapes=[pltpu.VMEM((N_DEV, r, d), F32), pltpu.SemaphoreType.DMA((N_DEV - 1,)),
                        pltpu.SemaphoreType.DMA((N_DEV - 1,))],
    )(*ins)


def _pack_rows(rows, d):
    out = []
    for a in rows:
        flat = a.reshape(-1)
        n = -(-flat.shape[0] // d) * d
        out.append(jnp.pad(flat, (0, n - flat.shape[0])).reshape(-1, d))
    packed = jnp.concatenate(out, axis=0)
    return jnp.pad(packed, ((0, 16 - packed.shape[0]), (0, 0)))


def _unpack_rows(packed, shapes, d):
    out, row = [], 0
    for shp in shapes:
        n = int(np.prod(shp))
        nrows = -(-n // d)
        out.append(packed[row:row + nrows].reshape(-1)[:n].reshape(shp))
        row += nrows
    return out


def kernel(x, pre_norm_ffn1, post_norm_ffn1, w_ffn1_gate_up, w_ffn1_down, pre_norm_mix, post_norm_mix, w_mix_in, hgrn_lower_bounds_fwd, hgrn_lower_bounds_bwd, hgrn_out_norm, attn_sink, w_mix_out, pre_norm_ffn2, post_norm_ffn2, w_ffn2_gate_up, w_ffn2_down, rel_bias_table, loss_target, m_pre_norm_ffn1, m_post_norm_ffn1, m_w_ffn1_gate_up, m_w_ffn1_down, m_pre_norm_mix, m_post_norm_mix, m_w_mix_in, m_hgrn_lower_bounds_fwd, m_hgrn_lower_bounds_bwd, m_hgrn_out_norm, m_attn_sink, m_w_mix_out, m_pre_norm_ffn2, m_post_norm_ffn2, m_w_ffn2_gate_up, m_w_ffn2_down, m_rel_bias_table, v_pre_norm_ffn1, v_post_norm_ffn1, v_w_ffn1_gate_up, v_w_ffn1_down, v_pre_norm_mix, v_post_norm_mix, v_w_mix_in, v_hgrn_lower_bounds_fwd, v_hgrn_lower_bounds_bwd, v_hgrn_out_norm, v_attn_sink, v_w_mix_out, v_pre_norm_ffn2, v_post_norm_ffn2, v_w_ffn2_gate_up, v_w_ffn2_down, v_rel_bias_table):
    t, d = x.shape[1], x.shape[2]
    hw = hgrn_out_norm.shape[1]
    aw = d - hw
    nah = aw // HEAD
    kvw = KV_HEADS * HEAD
    x0 = x[0]
    target = loss_target[0]

    big_names = ["w_ffn1_gate_up", "w_ffn1_down", "w_mix_in", "w_mix_out", "w_ffn2_gate_up", "w_ffn2_down"]
    big_w = [w_ffn1_gate_up[0], w_ffn1_down[0], w_mix_in[0], w_mix_out[0], w_ffn2_gate_up[0], w_ffn2_down[0]]
    big_m = [m_w_ffn1_gate_up[0], m_w_ffn1_down[0], m_w_mix_in[0], m_w_mix_out[0], m_w_ffn2_gate_up[0],
             m_w_ffn2_down[0]]
    big_v = [v_w_ffn1_gate_up[0], v_w_ffn1_down[0], v_w_mix_in[0], v_w_mix_out[0], v_w_ffn2_gate_up[0],
             v_w_ffn2_down[0]]
    col_sharded = [True, False, True, False, True, False]
    geoms = [_Big(w.shape, cs) for w, cs in zip(big_w, col_sharded)]

    cx, cy, cc = _mesh_pos()
    idx = jnp.stack([2 * cx + cy, cc]).astype(jnp.int32)
    c_idx = jnp.reshape(cc, (1,)).astype(jnp.int32)
    first = _cast_into_full(f"cast_{big_names[0]}", big_w[0], geoms[0], idx)
    (first,), direct_sems, tok = _gather_first_direct(first, geoms[0])
    rest = []
    for n, w, gm in zip(big_names[1:], big_w[1:], geoms[1:]):
        tok = _cast_into_full(f"cast_{n}", w, gm, idx, after=tok)
        rest.append(tok)
    (first,), relay_sems, tok = _gather_first_relay(first, geoms[0], direct_sems, after=tok)
    started_rest, sems_rest, rest_started = _gather_start("gather_start_rest", rest, geoms[1:], after=tok)
    started, gather_sems = [first] + started_rest, [relay_sems] + sems_rest

    def forward_weight(w, after):
        return _gather_forward(f"gather_forward_{big_names[w]}", started[w], geoms[w], gather_sems[w], after,
                               arrivals=1 if w == 0 else 3)

    def whole_weight(w, forwarded, after):
        return _gather_end(f"gather_end_{big_names[w]}", forwarded[0], geoms[w], forwarded[1], after)

    h1 = _norm_fwd("ffn1_pre_norm", x0, pre_norm_ffn1)
    w_gu1 = whole_weight(0, forward_weight(0, rest_started), h1)
    gate1, up1, act1 = _ffn_gate_up_act("ffn1_gate_up", h1, w_gu1)
    w_d1 = whole_weight(1, forward_weight(1, act1), act1)
    ff1 = _mm("ffn1_down", act1, w_d1, "nn", F32)
    fw = forward_weight(2, ff1)
    x1, hm = _resid_norm_fwd("ffn1_residual", x0, ff1, post_norm_ffn1, pre_norm_mix, 0.5)
    w_in = whole_weight(2, fw, hm)
    p = _mm("mix_in", hm, w_in, "nn", F32)
    fw = forward_weight(3, p)
    o_f, o_b, st_f, st_b = _hgrn_scan_fwd("hgrn_scan", p, hgrn_lower_bounds_fwd, hgrn_lower_bounds_bwd)
    y_h = _hgrn_out_fwd("hgrn_out", o_f, o_b, p, hgrn_out_norm, 4)
    kv_blk0 = (5 * hw + aw) // kvw
    k_pad = _pad_kv("attn_pad_k", p, kv_blk0, kvw)
    v_pad = _pad_kv("attn_pad_v", p, kv_blk0 + 1, kvw)
    bucket_ids = _t5_bucket_ids()
    bias = _bias_gather("attn_bias", rel_bias_table.T, bucket_ids).reshape(nah, WINDOW, SPAN)
    y_a = _attn_fwd("attn_fwd", p, k_pad, v_pad, bias, attn_sink, 5 * hw // aw)
    y_mix = _concat_cols("mix_concat", y_h, y_a)
    w_out = whole_weight(3, fw, y_mix)
    mixed = _mm("mix_out", y_mix, w_out, "nn", F32)
    fw = forward_weight(4, mixed)
    x2, h2 = _resid_norm_fwd("mix_residual", x1, mixed, post_norm_mix, pre_norm_ffn2, 1.0)
    w_gu2 = whole_weight(4, fw, h2)
    gate2, up2, act2 = _ffn_gate_up_act("ffn2_gate_up", h2, w_gu2)
    w_d2 = whole_weight(5, forward_weight(5, act2), act2)
    ff2 = _mm("ffn2_down", act2, w_d2, "nn", F32)
    loss_blk, dy, dff2, dg_post2 = _final_fwd_bwd("ffn2_residual_loss", x2, ff2, post_norm_ffn2, target, 0.5)

    reduce = [_GradReduce(n, gm, idx, c_idx) for n, gm in zip(big_names, geoms)]
    big_grads, big_delta, big_new_m, big_new_v = [None] * 6, [None] * 6, [None] * 6, [None] * 6

    def update(w, after):
        g, dl, nm, nv = _adamw(f"adamw_{big_names[w]}", big_w[w], reduce[w].finish(after), big_m[w], big_v[w])
        big_grads[w], big_delta[w], big_new_m[w], big_new_v[w] = g[None], dl[None], nm[None], nv[None]
        return dl

    dw_d2 = _mm("ffn2_dw_down", act2, dff2, "tn", BF16)
    tok = reduce[5].pair_start(dw_d2)
    dgu2 = _ffn_dact("ffn2_dact", dff2, w_d2, gate2, up2, after=tok)
    tok = reduce[5].pair_finish(dgu2)
    dw_gu2 = _ffn_dw_gate_up("ffn2_dw_gate_up", h2, dgu2, after=tok)
    tok = reduce[4].pair_start(dw_gu2)
    dh2 = _ffn_dh("ffn2_dh", dgu2, w_gu2, after=tok)
    tok = reduce[4].pair_finish(dh2)
    dx2, dg_pre2, dmixed, dg_postm = _norms_bwd("mix_residual_bwd", dy, dh2, x2, pre_norm_ffn2,
                                                post=(mixed, post_norm_mix, 1.0), after=tok)
    dw_out = _mm("mix_out_dw", y_mix, dmixed, "tn", BF16)
    tok = reduce[3].pair_start(dw_out)
    dy_mix = _mm("mix_out_dx", dmixed, w_out, "nt", F32, after=tok)
    tok = reduce[3].pair_finish(dy_mix)
    dq_a, dk_pad, dv_pad, dbias, dsink = _attn_bwd("attn_bwd", p, k_pad, v_pad, bias, attn_sink, dy_mix,
                                                   5 * hw // aw, hw // aw, after=tok)
    tok = reduce[5].chip_finish(dq_a)
    drel_t = _bias_scatter("attn_dbias", dbias.reshape(nah, WINDOW * SPAN), bucket_ids)
    do, dg_h, dgain = _hgrn_out_bwd("hgrn_out_bwd", dy_mix, o_f, o_b, p, hgrn_out_norm, 4, after=tok)
    dq_f, dv_f, dz_f, dlb_f, dq_b, dv_b, dz_b, dlb_b = _hgrn_scan_bwd(
        "hgrn_scan_bwd", p, hgrn_lower_bounds_fwd, hgrn_lower_bounds_bwd, do, st_f, st_b)
    tok = reduce[4].chip_finish(dq_f)
    tok = reduce[3].chip_finish(tok)
    dp = _mix_dproj("mix_dproj", [(dq_f, dq_b), (dv_f, dv_b), (dz_f,), (dz_b,), (dg_h,), (dq_a,)],
                    [dk_pad, dv_pad], t, after=tok)
    dw_in = _mm("mix_in_dw", hm, dp, "tn", BF16)
    tok = reduce[2].pair_start(dw_in)
    dhm = _mm("mix_in_dx", dp, w_in, "nt", F32, after=tok)
    tok = reduce[2].pair_finish(dhm)
    dx1, dg_prem, dff1, dg_post1 = _norms_bwd("ffn1_residual_bwd", dx2, dhm, x1, pre_norm_mix,
                                              post=(ff1, post_norm_ffn1, 0.5), after=tok)
    dw_d1 = _mm("ffn1_dw_down", act1, dff1, "tn", BF16)
    tok = reduce[1].pair_start(dw_d1)
    dgu1 = _ffn_dact("ffn1_dact", dff1, w_d1, gate1, up1, after=tok)
    tok = reduce[1].pair_finish(dgu1)
    tok = reduce[2].chip_finish(tok)
    dw_gu1 = _ffn_dw_gate_up("ffn1_dw_gate_up", h1, dgu1, after=tok)
    tok = reduce[0].pair_start(dw_gu1)
    done = update(2, tok)
    tok = reduce[0].pair_finish(done)
    dh1 = _ffn_dh("ffn1_dh", dgu1, w_gu1, after=tok)
    grad_x, dg_pre1 = _norms_bwd("ffn1_pre_norm_bwd", dx1, dh1, x0, pre_norm_ffn1)

    small_w = [pre_norm_ffn1, post_norm_ffn1, pre_norm_mix, post_norm_mix, hgrn_lower_bounds_fwd,
               hgrn_lower_bounds_bwd, hgrn_out_norm, attn_sink, pre_norm_ffn2, post_norm_ffn2, rel_bias_table]
    small_m = [m_pre_norm_ffn1, m_post_norm_ffn1, m_pre_norm_mix, m_post_norm_mix, m_hgrn_lower_bounds_fwd,
               m_hgrn_lower_bounds_bwd, m_hgrn_out_norm, m_attn_sink, m_pre_norm_ffn2, m_post_norm_ffn2,
               m_rel_bias_table]
    small_v = [v_pre_norm_ffn1, v_post_norm_ffn1, v_pre_norm_mix, v_post_norm_mix, v_hgrn_lower_bounds_fwd,
               v_hgrn_lower_bounds_bwd, v_hgrn_out_norm, v_attn_sink, v_pre_norm_ffn2, v_post_norm_ffn2,
               v_rel_bias_table]
    small_g = [dg_pre1, dg_post1, dg_prem, dg_postm, dlb_f, dlb_b, dgain, dsink[:, 0].reshape(1, nah), dg_pre2,
               dg_post2, drel_t.T]
    shapes = [a.shape for a in small_w]
    done = update(5, grad_x)
    done = update(4, done)
    done = update(3, done)
    summed = _all_reduce_small(_pack_rows(small_g + [loss_blk[0:1, 0:1]], d), after=done)
    loss = _unpack_rows(summed, shapes + [(1, 1)], d)[-1][0, 0]
    _, sd, sm, sv = _adamw("adamw_small", _pack_rows(small_w, d), summed, _pack_rows(small_m, d),
                           _pack_rows(small_v, d))
    small_grads = _unpack_rows(summed, shapes, d)
    small_delta, small_new_m, small_new_v = (_unpack_rows(a, shapes, d) for a in (sd, sm, sv))

    tok = reduce[1].chip_finish(sd)
    done = update(1, tok)
    tok = reduce[0].chip_finish(done)
    update(0, tok)

    def ordered(small, big):
        s = dict(zip(["pre1", "post1", "prem", "postm", "lbf", "lbb", "gain", "sink", "pre2", "post2", "rel"], small))
        b = dict(zip(["gu1", "d1", "win", "wout", "gu2", "d2"], big))
        return [s["pre1"], s["post1"], b["gu1"], b["d1"], s["prem"], s["postm"], b["win"], s["lbf"], s["lbb"],
                s["gain"], s["sink"], b["wout"], s["pre2"], s["post2"], b["gu2"], b["d2"], s["rel"]]

    return (loss, grad_x[None], *ordered(small_grads, big_grads), *ordered(small_delta, big_delta),
            *ordered(small_new_m, big_new_m), *ordered(small_new_v, big_new_v))
```

```python
import functools
import math

import jax
import jax.numpy as jnp
import numpy as np
from jax import lax
from jax.experimental import pallas as pl
from jax.experimental.pallas import tpu as pltpu

F32 = jnp.float32
BF16 = jnp.bfloat16

HEAD = 128
CHUNK = 64
WINDOW = 128
SPAN = 3 * WINDOW
KV_HEADS = 2
REL_BUCKETS = 32
REL_MAX_DIST = 128
EPS = 1e-6
NEG_INF = -1e30

ADAM_LR = 0.001
ADAM_B1 = 0.9
ADAM_B2 = 0.999
ADAM_EPS = 1e-08
ADAM_WD = 0.01
ADAM_STEP = 10

N_CHIPS = 4
N_DEV = 8
V7X_VMEM_BYTES = 64 * 1024 * 1024
MESH = pl.DeviceIdType.MESH
ANY = pl.BlockSpec(memory_space=pl.ANY)


def _tile(n, pref, mult):
    t = (min(pref, n) // mult) * mult
    while t >= mult:
        if n % t == 0:
            return t
        t -= mult
    return n


def _params(semantics, block_bytes):
    limit = min(V7X_VMEM_BYTES - (4 << 20), 2 * int(block_bytes) + (8 << 20))
    return pltpu.CompilerParams(dimension_semantics=semantics, vmem_limit_bytes=limit)


def _nbytes(shape, dtype):
    return int(np.prod(shape)) * jnp.dtype(dtype).itemsize


PIN_TO_HBM_BYTES = 4 << 20


def _pallas(body, **kw):
    def pin_shape(s):
        if isinstance(s, jax.ShapeDtypeStruct) and _nbytes(s.shape, s.dtype) >= PIN_TO_HBM_BYTES:
            return pltpu.HBM(s.shape, s.dtype)
        return s

    def pin(a):
        if getattr(a, "dtype", None) in (F32, BF16) and _nbytes(a.shape, a.dtype) >= PIN_TO_HBM_BYTES:
            return pltpu.with_memory_space_constraint(a, pltpu.HBM)
        return a

    out_shape = kw["out_shape"]
    kw["out_shape"] = [pin_shape(s) for s in out_shape] if isinstance(out_shape, (list, tuple)) else pin_shape(out_shape)
    call = pl.pallas_call(body, **kw)
    return lambda *args: call(*[pin(a) for a in args])


def _dot(a, b, ca=1, cb=0):
    return lax.dot_general(a, b, (((ca,), (cb,)), ((), ())), preferred_element_type=F32)


def _split3(x):
    hi = x.astype(BF16)
    r1 = x - hi.astype(F32)
    mid = r1.astype(BF16)
    lo = (r1 - mid.astype(F32)).astype(BF16)
    return hi, mid, lo


def _dot_exact(a, b, ca=1, cb=0, split="b"):
    if split == "b":
        return sum(_dot(a, p, ca, cb) for p in _split3(b))
    return sum(_dot(p, b, ca, cb) for p in _split3(a))


def _rms(x):
    return lax.rsqrt(jnp.mean(x * x, axis=-1, keepdims=True) + EPS)


def _norm_bwd(u, x, gain):
    r = _rms(x)
    xhat = x * r
    dgain = jnp.sum(u * xhat, axis=0, keepdims=True)
    v = u * gain
    dx = r * (v - xhat * jnp.mean(v * xhat, axis=-1, keepdims=True))
    return dx, dgain


def _sigmoid(x):
    return 1.0 / (1.0 + jnp.exp(-x))


def _accumulate(ref, val, first):
    @pl.when(first)
    def _():
        ref[...] = val

    @pl.when(jnp.logical_not(first))
    def _():
        ref[...] += val


def _ordered(body, ins, in_specs, after):
    if after is None:
        return body, list(ins), list(in_specs)
    n_in = len(ins)

    def wrapped(*refs):
        body(*refs[:n_in], *refs[n_in + 1:])

    return wrapped, list(ins) + [after], list(in_specs) + [pl.BlockSpec(memory_space=pl.ANY)]


def _matmul(name, a, b, *, form, out_dtype, tm, tn, tk, a_map=None, b_map=None,
            out_shape=None, out_block=None, out_map=None, sizes=None, after=None):
    if sizes is None:
        if form == "nn":
            (m, k), n = a.shape, b.shape[1]
        elif form == "nt":
            (m, k), n = a.shape, b.shape[0]
        else:
            (k, m), n = a.shape, b.shape[1]
    else:
        m, n, k = sizes
    gi, gj, gk = m // tm, n // tn, k // tk
    a_blk = (tm, tk) if form != "tn" else (tk, tm)
    b_blk = (tk, tn) if form != "nt" else (tn, tk)
    if a_map is None:
        a_map = (lambda i, j, kk: (i, kk)) if form != "tn" else (lambda i, j, kk: (kk, i))
    else:
        a_blk = (None,) + a_blk
    if b_map is None:
        b_map = (lambda i, j, kk: (kk, j)) if form != "nt" else (lambda i, j, kk: (j, kk))
    else:
        b_blk = (None,) + b_blk
    if out_shape is None:
        out_shape, out_block, out_map = (m, n), (tm, tn), (lambda i, j, kk: (i, j))
    ca, cb = {"nn": (1, 0), "nt": (1, 1), "tn": (0, 0)}[form]

    def body(a_ref, b_ref, o_ref, *acc):
        part = _dot(a_ref[...], b_ref[...], ca, cb)
        if gk == 1:
            o_ref[...] = part.astype(o_ref.dtype)
        else:
            kk = pl.program_id(2)
            _accumulate(acc[0], part, kk == 0)

            @pl.when(kk == gk - 1)
            def _():
                o_ref[...] = acc[0][...].astype(o_ref.dtype)

    scratch = [] if gk == 1 else [pltpu.VMEM((tm, tn), F32)]
    vmem = (_nbytes((tm, tk), a.dtype) + _nbytes((tk, tn), b.dtype) + _nbytes((tm, tn), out_dtype)
            + 2 * _nbytes((tm, tn), F32))
    body, ins, in_specs = _ordered(body, [a, b], [pl.BlockSpec(a_blk, a_map), pl.BlockSpec(b_blk, b_map)], after)
    return _pallas(
        body, name=name, grid=(gi, gj, gk), in_specs=in_specs,
        out_specs=pl.BlockSpec(out_block, out_map),
        out_shape=jax.ShapeDtypeStruct(out_shape, out_dtype),
        scratch_shapes=scratch,
        compiler_params=_params(("parallel", "parallel", "arbitrary"), vmem),
    )(*ins)


V7X_HBM_BYTES_PER_US = 3.0e6
V7X_MXU_FLOPS_PER_US = 0.9e9
V7X_VMEM_RMW_BYTES_PER_US = 10e6
GRID_STEP_US = 0.35
MATMUL_VMEM_BUDGET = 40 << 20
MATMUL_MAX_TILE_FLOPS = 1 << 33


def _divisors(n, mult, lo):
    return [t for t in range(mult, n + 1, mult) if n % t == 0 and t >= min(lo, n)]


def _mm_tiles(m, n, k, out_dtype=F32, n_unit=None, k_unit=None):
    out_bytes = jnp.dtype(out_dtype).itemsize
    best = None
    for tm in _divisors(m, 128, 256):
        for tn in _divisors(n_unit or n, 128, 256):
            for tk in _divisors(k_unit or k, 128, 512):
                gi, gj, gk = m // tm, n // tn, k // tk
                vmem = 4 * tm * tk + 4 * tk * tn + 2 * tm * tn * out_bytes + 4 * tm * tn * (2 if gk > 1 else 1)
                if vmem > MATMUL_VMEM_BUDGET or 2 * tm * tn * tk > MATMUL_MAX_TILE_FLOPS:
                    continue
                a_bytes = 2 * m * k * (gj if gk > 1 else 1)
                b_bytes = 2 * k * n * (1 if gj == 1 and gk == 1 else gi)
                hbm_us = (a_bytes + b_bytes + m * n * out_bytes) / V7X_HBM_BYTES_PER_US
                acc_us = (8 * m * n * gk / V7X_VMEM_RMW_BYTES_PER_US) if gk > 1 else 0.0
                cost = max(2 * m * n * k / V7X_MXU_FLOPS_PER_US, 1.3 * hbm_us) + GRID_STEP_US * gi * gj * gk + acc_us
                key = (round(cost, 1), vmem)
                if best is None or key < best[0]:
                    best = (key, (tm, tn, tk))
    return best[1]


def _mm(name, a, b, form, out_dtype, after=None):
    if form == "nn":
        m, k, n = a.shape[0], a.shape[1], b.shape[1]
    elif form == "nt":
        m, k, n = a.shape[0], a.shape[1], b.shape[0]
    else:
        m, k, n = a.shape[1], a.shape[0], b.shape[1]
    tm, tn, tk = _mm_tiles(m, n, k, out_dtype)
    return _matmul(name, a, b, form=form, out_dtype=out_dtype, tm=tm, tn=tn, tk=tk, after=after)


def _row_tile(t):
    return _tile(t, 256, 8)


def _norm_fwd(name, x, gain):
    t, d = x.shape
    tm = _row_tile(t)

    def body(x_ref, g_ref, h_ref):
        xv = x_ref[...]
        h_ref[...] = (xv * _rms(xv) * g_ref[...]).astype(BF16)

    row = pl.BlockSpec((tm, d), lambda i: (i, 0))
    vec = pl.BlockSpec((1, d), lambda i: (0, 0))
    return _pallas(
        body, name=name, grid=(t // tm,), in_specs=[row, vec], out_specs=row,
        out_shape=jax.ShapeDtypeStruct((t, d), BF16),
        compiler_params=_params(("parallel",), 2 * _nbytes((tm, d), F32)),
    )(x, gain)


def _resid_norm_fwd(name, xres, ff, gpost, gpre, scale):
    t, d = xres.shape
    tm = _row_tile(t)

    def body(x_ref, f_ref, gp_ref, gn_ref, xn_ref, h_ref):
        f = f_ref[...]
        xn = x_ref[...] + scale * (f * _rms(f) * gp_ref[...])
        xn_ref[...] = xn
        h_ref[...] = (xn * _rms(xn) * gn_ref[...]).astype(BF16)

    row = pl.BlockSpec((tm, d), lambda i: (i, 0))
    vec = pl.BlockSpec((1, d), lambda i: (0, 0))
    return _pallas(
        body, name=name, grid=(t // tm,), in_specs=[row, row, vec, vec], out_specs=[row, row],
        out_shape=[jax.ShapeDtypeStruct((t, d), F32), jax.ShapeDtypeStruct((t, d), BF16)],
        compiler_params=_params(("parallel",), 4 * _nbytes((tm, d), F32)),
    )(xres, ff, gpost, gpre)


def _final_fwd_bwd(name, xres, ff, gpost, target, scale):
    t, d = xres.shape
    tm = _row_tile(t)

    def body(x_ref, f_ref, gp_ref, t_ref, loss_ref, dy_ref, dff_ref, dg_ref):
        i = pl.program_id(0)
        f = f_ref[...]
        gp = gp_ref[...]
        y = x_ref[...] + scale * (f * _rms(f) * gp)
        err = y - t_ref[...]
        part = 0.5 * jnp.sum(jnp.mean(err * err, axis=-1, keepdims=True), axis=0, keepdims=True)
        _accumulate(loss_ref, jnp.broadcast_to(part, loss_ref.shape), i == 0)
        dy = err / d
        dy_ref[...] = dy
        dff, dg = _norm_bwd(scale * dy, f, gp)
        dff_ref[...] = dff.astype(BF16)
        _accumulate(dg_ref, dg, i == 0)

    row = pl.BlockSpec((tm, d), lambda i: (i, 0))
    vec = pl.BlockSpec((1, d), lambda i: (0, 0))
    return _pallas(
        body, name=name, grid=(t // tm,), in_specs=[row, row, vec, row],
        out_specs=[pl.BlockSpec((8, 128), lambda i: (0, 0)), row, row, vec],
        out_shape=[jax.ShapeDtypeStruct((8, 128), F32), jax.ShapeDtypeStruct((t, d), F32),
                   jax.ShapeDtypeStruct((t, d), BF16), jax.ShapeDtypeStruct((1, d), F32)],
        compiler_params=_params(("arbitrary",), 5 * _nbytes((tm, d), F32)),
    )(xres, ff, gpost, target)


def _norms_bwd(name, dres, dh, xin, gpre, post=None, after=None):
    t, d = dres.shape
    tm = _row_tile(t)
    with_post = post is not None

    def body(*refs):
        if with_post:
            dr_ref, dh_ref, x_ref, g_ref, f_ref, gp_ref, dx_ref, dg_ref, dff_ref, dgp_ref = refs
        else:
            dr_ref, dh_ref, x_ref, g_ref, dx_ref, dg_ref = refs
        i = pl.program_id(0)
        dx, dg = _norm_bwd(dh_ref[...], x_ref[...], g_ref[...])
        dx = dr_ref[...] + dx
        dx_ref[...] = dx
        _accumulate(dg_ref, dg, i == 0)
        if with_post:
            dff, dgp = _norm_bwd(post[2] * dx, f_ref[...], gp_ref[...])
            dff_ref[...] = dff.astype(BF16)
            _accumulate(dgp_ref, dgp, i == 0)

    row = pl.BlockSpec((tm, d), lambda i: (i, 0))
    vec = pl.BlockSpec((1, d), lambda i: (0, 0))
    ins, in_specs = [dres, dh, xin, gpre], [row, row, row, vec]
    out_specs = [row, vec]
    out_shape = [jax.ShapeDtypeStruct((t, d), F32), jax.ShapeDtypeStruct((1, d), F32)]
    if with_post:
        ins += [post[0], post[1]]
        in_specs += [row, vec]
        out_specs += [row, vec]
        out_shape += [jax.ShapeDtypeStruct((t, d), BF16), jax.ShapeDtypeStruct((1, d), F32)]
    body, ins, in_specs = _ordered(body, ins, in_specs, after)
    return _pallas(
        body, name=name, grid=(t // tm,), in_specs=in_specs, out_specs=out_specs, out_shape=out_shape,
        compiler_params=_params(("arbitrary",), 6 * _nbytes((tm, d), F32)),
    )(*ins)


SWIGLU_TILE = (1024, 512)
V7X_MXU_COLS = 256


def _ffn_gate_up_act(name, h, w_gu):
    t, d = h.shape
    f = w_gu.shape[1] // 2
    tm, tn = _tile(t, SWIGLU_TILE[0], 128), _tile(f, SWIGLU_TILE[1], 128)
    nf = f // tn

    def body(h_ref, wg_ref, wu_ref, g_ref, u_ref, a_ref):
        hv = h_ref[...]
        for c0 in range(0, tn, min(tn, V7X_MXU_COLS)):
            cols = slice(c0, c0 + min(tn, V7X_MXU_COLS))
            g = _dot(hv, wg_ref[:, cols])
            u = _dot(hv, wu_ref[:, cols])
            g_ref[:, cols] = g.astype(BF16)
            u_ref[:, cols] = u.astype(BF16)
            a_ref[:, cols] = (g * _sigmoid(g) * u).astype(BF16)

    out = jax.ShapeDtypeStruct((t, f), BF16)
    blk = pl.BlockSpec((tm, tn), lambda i, j: (i, j))
    return _pallas(
        body, name=name, grid=(t // tm, nf),
        in_specs=[pl.BlockSpec((tm, d), lambda i, j: (i, 0)), pl.BlockSpec((d, tn), lambda i, j: (0, j)),
                  pl.BlockSpec((d, tn), lambda i, j: (0, j + nf))],
        out_specs=[blk, blk, blk], out_shape=[out, out, out],
        compiler_params=_params(("parallel", "parallel"),
                                _nbytes((tm, d), BF16) + 2 * _nbytes((d, tn), BF16) + 5 * _nbytes((tm, tn), F32)),
    )(h, w_gu, w_gu)


def _ffn_dact(name, dff, w_down, gate, up, after=None):
    t, d = dff.shape
    f = w_down.shape[0]
    tm, tn = _tile(t, SWIGLU_TILE[0], 128), _tile(f, SWIGLU_TILE[1], 128)

    def body(d_ref, w_ref, g_ref, u_ref, o_ref):
        dv = d_ref[...]
        for c0 in range(0, tn, min(tn, V7X_MXU_COLS)):
            cols = slice(c0, c0 + min(tn, V7X_MXU_COLS))
            da = _dot(dv, w_ref[cols, :], 1, 1)
            g = g_ref[:, cols].astype(F32)
            u = u_ref[:, cols].astype(F32)
            sig = _sigmoid(g)
            o_ref[0, :, cols] = (da * u * sig * (1.0 + g * (1.0 - sig))).astype(BF16)
            o_ref[1, :, cols] = (da * g * sig).astype(BF16)

    blk = pl.BlockSpec((tm, tn), lambda i, j: (i, j))
    body, ins, in_specs = _ordered(
        body, [dff, w_down, gate, up],
        [pl.BlockSpec((tm, d), lambda i, j: (i, 0)), pl.BlockSpec((tn, d), lambda i, j: (j, 0)), blk, blk], after)
    return _pallas(
        body, name=name, grid=(t // tm, f // tn), in_specs=in_specs,
        out_specs=pl.BlockSpec((2, tm, tn), lambda i, j: (0, i, j)),
        out_shape=jax.ShapeDtypeStruct((2, t, f), BF16),
        compiler_params=_params(("parallel", "parallel"),
                                _nbytes((tm, d), BF16) + _nbytes((tn, d), BF16) + 5 * _nbytes((tm, tn), F32)),
    )(*ins)


def _ffn_dh(name, dgu, w_gu, after=None):
    _, t, f = dgu.shape
    d = w_gu.shape[0]
    tm, tn, tk = _mm_tiles(t, d, 2 * f, F32, k_unit=f)
    nkf = f // tk
    return _matmul(name, dgu, w_gu, form="nt", out_dtype=F32, tm=tm, tn=tn, tk=tk, sizes=(t, d, 2 * f),
                   a_map=lambda i, j, kk: (kk // nkf, i, kk % nkf), after=after)


def _lower_bound(lbp):
    m = jnp.max(lbp, axis=0, keepdims=True)
    e = jnp.exp(lbp - m)
    return e[0:1] / jnp.sum(e, axis=0, keepdims=True)


def _chunk_mask(reverse):
    row = lax.broadcasted_iota(jnp.int32, (CHUNK, CHUNK), 0)
    col = lax.broadcasted_iota(jnp.int32, (CHUNK, CHUNK), 1)
    return (col >= row) if reverse else (col <= row)


def _hgrn_gates(z, lb, mask_bf):
    sig = _sigmoid(z)
    f = lb + (1.0 - lb) * sig
    logf = jnp.log(f)
    k = 1.0 - f
    cum = _dot_exact(mask_bf, logf)
    last = jnp.sum(logf, axis=0, keepdims=True)
    return sig, f, k, cum, last


def _hgrn_scan_fwd(name, p, lbp_f, lbp_b):
    t = p.shape[0]
    hw = lbp_f.shape[1]
    nh, nc = hw // HEAD, t // CHUNK

    def body(qf, vf, zf, qb, vb, zb, lbf, lbb, of_ref, ob_ref, stf_ref, stb_ref, state):
        n = pl.program_id(0)

        @pl.when(n == 0)
        def _():
            state[...] = jnp.zeros_like(state)

        directions = [(qf, vf, zf, lbf, of_ref, stf_ref), (qb, vb, zb, lbb, ob_ref, stb_ref)]
        wide = []
        for d, (q_ref, v_ref, z_ref, lb_ref, o_ref, st_ref) in enumerate(directions):
            mask = _chunk_mask(d == 1)
            lb = _lower_bound(lb_ref[...])
            _, _, k, cum, last = _hgrn_gates(z_ref[...], lb, mask.astype(BF16))
            v = v_ref[...].astype(BF16)
            qd = (q_ref[...] * jnp.exp(cum)).astype(BF16)
            kd = (k * jnp.exp(-cum)).astype(BF16)
            kt = (k * jnp.exp(last - cum)).astype(BF16)
            s_all = state[d]
            st_ref[...] = s_all
            wide.append((mask, v, qd, kd, kt, jnp.exp(last), s_all, o_ref))
        pairs = [(d, slice(h * HEAD, (h + 1) * HEAD)) for d in range(2) for h in range(nh)]
        a = [jnp.where(wide[d][0], _dot(wide[d][2][:, sl], wide[d][3][:, sl], 1, 1), 0.0).astype(BF16)
             for d, sl in pairs]
        inter = [_dot(wide[d][2][:, sl], wide[d][6][:, sl].astype(BF16), 1, 1) for d, sl in pairs]
        intra = [_dot(a[i], wide[d][1][:, sl]) for i, (d, sl) in enumerate(pairs)]
        grow = [_dot(wide[d][1][:, sl], wide[d][4][:, sl], 0, 0) for d, sl in pairs]
        for i, (d, sl) in enumerate(pairs):
            wide[d][7][:, sl] = intra[i] + inter[i]
            state[d, :, sl] = wide[d][6][:, sl] * wide[d][5][:, sl] + grow[i]

    def col(group, reverse):
        return pl.BlockSpec((CHUNK, hw), lambda n: ((nc - 1 - n) if reverse else n, group))

    def st(reverse):
        return pl.BlockSpec((None, HEAD, hw), lambda n: ((nc - 1 - n) if reverse else n, 0, 0))

    lb_spec = pl.BlockSpec((2, hw), lambda n: (0, 0))
    out = jax.ShapeDtypeStruct((t, hw), F32)
    states = jax.ShapeDtypeStruct((nc, HEAD, hw), F32)
    return _pallas(
        body, name=name, grid=(nc,),
        in_specs=[col(0, False), col(1, False), col(2, False), col(0, True), col(1, True), col(3, True),
                  lb_spec, lb_spec],
        out_specs=[col(0, False), col(0, True), st(False), st(True)],
        out_shape=[out, out, states, states],
        scratch_shapes=[pltpu.VMEM((2, HEAD, hw), F32)],
        compiler_params=_params(("arbitrary",), 12 * _nbytes((HEAD, hw), F32)),
    )(p, p, p, p, p, p, lbp_f, lbp_b)


def _hgrn_scan_bwd(name, p, lbp_f, lbp_b, do, st_f, st_b):
    t = p.shape[0]
    hw = lbp_f.shape[1]
    nh, nc = hw // HEAD, t // CHUNK

    def body(qf, vf, zf, dof, sf, qb, vb, zb, dob, sb, lbf, lbb, dqf, dvf, dzf, dlbf, dqb, dvb, dzb, dlbb,
             dstate, dlb_acc, dqd_s, dkd_s, dkt_s, ddec_s):
        n = pl.program_id(0)

        @pl.when(n == 0)
        def _():
            dstate[...] = jnp.zeros_like(dstate)
            dlb_acc[...] = jnp.zeros_like(dlb_acc)

        directions = [(qf, vf, zf, dof, sf, lbf, dqf, dvf, dzf, dlbf), (qb, vb, zb, dob, sb, lbb, dqb, dvb, dzb, dlbb)]
        for d, (q_ref, v_ref, z_ref, do_ref, st_ref, lb_ref, dq_ref, dv_ref, dz_ref, dlb_ref) in enumerate(directions):
            mask = _chunk_mask(d == 1)
            mask_bf = mask.astype(BF16)
            lb = _lower_bound(lb_ref[...])
            sig, f, k, cum, last = _hgrn_gates(z_ref[...], lb, mask_bf)
            e_pos, e_neg, e_tail = jnp.exp(cum), jnp.exp(-cum), jnp.exp(last - cum)
            dec = jnp.exp(last)
            v = v_ref[...].astype(BF16)
            qd, kd, kt = q_ref[...] * e_pos, k * e_neg, k * e_tail
            qd_bf, kd_bf, kt_bf = qd.astype(BF16), kd.astype(BF16), kt.astype(BF16)
            s_all = st_ref[...]
            ds_all = dstate[d]
            dov = do_ref[...].astype(BF16)
            cols = [slice(h * HEAD, (h + 1) * HEAD) for h in range(nh)]
            s_bf = [s_all[:, sl].astype(BF16) for sl in cols]
            ds_bf = [ds_all[:, sl].astype(BF16) for sl in cols]
            a = [jnp.where(mask, _dot(qd_bf[:, sl], kd_bf[:, sl], 1, 1), 0.0).astype(BF16) for sl in cols]
            da = [jnp.where(mask, _dot(dov[:, sl], v[:, sl], 1, 1), 0.0).astype(BF16) for sl in cols]
            dv_h = [_dot(a[h], dov[:, sl], 0, 0) + _dot(kt_bf[:, sl], ds_bf[h], 1, 1) for h, sl in enumerate(cols)]
            dqd_h = [_dot(da[h], kd_bf[:, sl]) + _dot(dov[:, sl], s_bf[h]) for h, sl in enumerate(cols)]
            dkd_h = [_dot(da[h], qd_bf[:, sl], 0, 0) for h, sl in enumerate(cols)]
            dkt_h = [_dot(v[:, sl], ds_bf[h]) for h, sl in enumerate(cols)]
            dst_h = [_dot(dov[:, sl], qd_bf[:, sl], 0, 0) + ds_all[:, sl] * dec[:, sl] for sl in cols]
            for h, sl in enumerate(cols):
                dv_ref[:, sl] = dv_h[h]
                dqd_s[:, sl] = dqd_h[h]
                dkd_s[:, sl] = dkd_h[h]
                dkt_s[:, sl] = dkt_h[h]
                dstate[d, :, sl] = dst_h[h]
                ddec_s[:, sl] = jnp.sum(ds_all[:, sl] * s_all[:, sl], axis=0, keepdims=True)
            dqd, dkd, dkt = dqd_s[...], dkd_s[...], dkt_s[...]
            dlast = jnp.sum(dkt * kt, axis=0, keepdims=True) + dec * ddec_s[...]
            dq_ref[...] = dqd * e_pos
            dk = dkd * e_neg + dkt * e_tail
            dcum = dqd * qd - dkd * kd - dkt * kt
            dlogf = _dot_exact(mask_bf, dcum, 0, 0) + dlast
            df = dlogf / f - dk
            dz_ref[...] = df * (1.0 - lb) * sig * (1.0 - sig)
            dlb_acc[d] += jnp.sum(df * (1.0 - sig), axis=0, keepdims=True)

            @pl.when(n == nc - 1)
            def _():
                g = dlb_acc[d] * lb * (1.0 - lb)
                dlb_ref[0:1, :] = g
                dlb_ref[1:2, :] = -g

    def col(group, reverse):
        return pl.BlockSpec((CHUNK, hw), lambda n: (n if reverse else (nc - 1 - n), group))

    def st(reverse):
        return pl.BlockSpec((None, HEAD, hw), lambda n: (n if reverse else (nc - 1 - n), 0, 0))

    lb_spec = pl.BlockSpec((2, hw), lambda n: (0, 0))
    out = jax.ShapeDtypeStruct((t, hw), F32)
    dlb = jax.ShapeDtypeStruct((2, hw), F32)
    wide = pltpu.VMEM((CHUNK, hw), F32)
    return _pallas(
        body, name=name, grid=(nc,),
        in_specs=[col(0, False), col(1, False), col(2, False), col(0, False), st(False),
                  col(0, True), col(1, True), col(3, True), col(0, True), st(True), lb_spec, lb_spec],
        out_specs=[col(0, False), col(0, False), col(0, False), lb_spec,
                   col(0, True), col(0, True), col(0, True), lb_spec],
        out_shape=[out, out, out, dlb, out, out, out, dlb],
        scratch_shapes=[pltpu.VMEM((2, HEAD, hw), F32), pltpu.VMEM((2, 1, hw), F32), wide, wide, wide,
                        pltpu.VMEM((1, hw), F32)],
        compiler_params=_params(("arbitrary",), 16 * _nbytes((HEAD, hw), F32)),
    )(p, p, p, do, st_f, p, p, p, do, st_b, lbp_f, lbp_b)


def _hgrn_out_fwd(name, o_f, o_b, p, gain, g_group):
    t, hw = o_f.shape
    nh = hw // HEAD
    tm = _tile(t, 512, 8)

    def body(of_ref, ob_ref, g_ref, gain_ref, y_ref):
        o = of_ref[...] + ob_ref[...]
        g = g_ref[...]
        y_ref[...] = (o * _rms(o) * gain_ref[...] * (g * _sigmoid(g))).astype(BF16)

    blk = pl.BlockSpec((tm, HEAD), lambda i, h: (i, h))
    return _pallas(
        body, name=name, grid=(t // tm, nh),
        in_specs=[blk, blk, pl.BlockSpec((tm, HEAD), lambda i, h: (i, g_group * nh + h)),
                  pl.BlockSpec((1, HEAD), lambda i, h: (0, h))],
        out_specs=blk, out_shape=jax.ShapeDtypeStruct((t, hw), BF16),
        compiler_params=_params(("parallel", "parallel"), 1 << 20),
    )(o_f, o_b, p, gain)


def _hgrn_out_bwd(name, dy, o_f, o_b, p, gain, g_group, after=None):
    t, hw = o_f.shape
    nh = hw // HEAD
    tm = _tile(t, 512, 8)

    def body(dy_ref, of_ref, ob_ref, g_ref, gain_ref, do_ref, dg_ref, dgain_ref):
        i = pl.program_id(1)
        o = of_ref[...] + ob_ref[...]
        g = g_ref[...]
        gain_v = gain_ref[...]
        sig = _sigmoid(g)
        dyv = dy_ref[...]
        do, dgain = _norm_bwd(dyv * (g * sig), o, gain_v)
        do_ref[...] = do
        dg_ref[...] = dyv * (o * _rms(o) * gain_v) * sig * (1.0 + g * (1.0 - sig))
        _accumulate(dgain_ref, dgain, i == 0)

    blk = pl.BlockSpec((tm, HEAD), lambda h, i: (i, h))
    vec = pl.BlockSpec((1, HEAD), lambda h, i: (0, h))
    out = jax.ShapeDtypeStruct((t, hw), F32)
    body, ins, in_specs = _ordered(
        body, [dy, o_f, o_b, p, gain],
        [blk, blk, blk, pl.BlockSpec((tm, HEAD), lambda h, i: (i, g_group * nh + h)), vec], after)
    return _pallas(
        body, name=name, grid=(nh, t // tm), in_specs=in_specs,
        out_specs=[blk, blk, vec], out_shape=[out, out, jax.ShapeDtypeStruct((1, hw), F32)],
        compiler_params=_params(("parallel", "arbitrary"), 1 << 20),
    )(*ins)


def _t5_bucket_ids():
    c = np.arange(WINDOW)[:, None]
    s = np.arange(SPAN)[None, :]
    rel = s - WINDOW - c
    nb = REL_BUCKETS // 2
    max_exact = nb // 2
    bucket = (rel > 0).astype(np.int32) * nb
    n = np.abs(rel)
    large = max_exact + (np.log(np.maximum(n, 1) / max_exact) / np.log(REL_MAX_DIST / max_exact)
                         * (nb - max_exact)).astype(np.int32)
    large = np.minimum(large, nb - 1)
    ids = bucket + np.where(n < max_exact, n, large).astype(np.int32)
    return jnp.asarray(ids.reshape(1, WINDOW * SPAN), jnp.int32)


def _bias_onehot(ids_ref):
    n = ids_ref.shape[1]
    return (lax.broadcasted_iota(jnp.int32, (REL_BUCKETS, n), 0) == ids_ref[...]).astype(BF16)


def _bias_gather(name, table_t, ids):
    nh = table_t.shape[0]

    def body(t_ref, ids_ref, o_ref):
        o_ref[...] = _dot_exact(t_ref[...], _bias_onehot(ids_ref), split="a")

    return _pallas(
        body, name=name, out_shape=jax.ShapeDtypeStruct((nh, ids.shape[1]), F32),
        compiler_params=pltpu.CompilerParams(vmem_limit_bytes=32 << 20),
    )(table_t, ids)


def _bias_scatter(name, dbias, ids):
    nh = dbias.shape[0]

    def body(d_ref, ids_ref, o_ref):
        o_ref[...] = _dot_exact(d_ref[...], _bias_onehot(ids_ref), 1, 1, split="a")

    return _pallas(
        body, name=name, out_shape=jax.ShapeDtypeStruct((nh, REL_BUCKETS), F32),
        compiler_params=pltpu.CompilerParams(vmem_limit_bytes=32 << 20),
    )(dbias, ids)


def _attn_valid(i, t):
    c = lax.broadcasted_iota(jnp.int32, (WINDOW, SPAN), 0)
    s = lax.broadcasted_iota(jnp.int32, (WINDOW, SPAN), 1)
    rel = s - WINDOW - c
    pos = i * WINDOW - WINDOW + s
    return (jnp.abs(rel) <= WINDOW) & (pos >= 0) & (pos < t)


def _attn_probs(qs, khs, b_ref, s_ref, valid):
    heads = range(len(qs))
    sinks = [s_ref[0:1, h:h + 1] for h in heads]
    s = [_dot(qs[h], khs[h], 1, 1) / math.sqrt(HEAD) for h in heads]
    s = [jnp.where(valid, s[h] + b_ref[h], NEG_INF) for h in heads]
    m = [jnp.maximum(jnp.max(s[h], axis=-1, keepdims=True), sinks[h]) for h in heads]
    e = [jnp.exp(s[h] - m[h]) for h in heads]
    es = [jnp.exp(sinks[h] - m[h]) for h in heads]
    inv = [1.0 / (jnp.sum(e[h], axis=-1, keepdims=True) + es[h]) for h in heads]
    return [e[h] * inv[h] for h in heads], [es[h] * inv[h] for h in heads]


def _attn_fwd(name, p, k_pad, v_pad, bias, sink, q_group_blk):
    t = p.shape[0]
    nh = bias.shape[0]
    aw = nh * HEAD
    grp = nh // KV_HEADS
    nb = t // WINDOW

    def body(q_ref, k_ref, v_ref, b_ref, s_ref, y_ref):
        i = pl.program_id(0)
        valid = _attn_valid(i, t)
        start = pl.multiple_of(i * WINDOW, WINDOW)
        ks = k_ref[pl.ds(start, SPAN), :]
        vs = v_ref[pl.ds(start, SPAN), :]
        heads = range(nh)
        col = lambda h: slice(h * HEAD, (h + 1) * HEAD)
        qs = [q_ref[:, col(h)].astype(BF16) for h in heads]
        pr, _ = _attn_probs(qs, [ks[:, col(h // grp)] for h in heads], b_ref, s_ref, valid)
        out = [_dot(pr[h].astype(BF16), vs[:, col(h // grp)]) for h in heads]
        for h in heads:
            y_ref[:, col(h)] = out[h].astype(BF16)

    full = lambda a: pl.BlockSpec(a.shape, lambda i: (0,) * a.ndim)
    return _pallas(
        body, name=name, grid=(nb,),
        in_specs=[pl.BlockSpec((WINDOW, aw), lambda i: (i, q_group_blk)), full(k_pad), full(v_pad), full(bias),
                  full(sink)],
        out_specs=pl.BlockSpec((WINDOW, aw), lambda i: (i, 0)),
        out_shape=jax.ShapeDtypeStruct((t, aw), BF16),
        compiler_params=_params(("parallel",), _nbytes(k_pad.shape, BF16) * 2 + _nbytes(bias.shape, F32)),
    )(p, k_pad, v_pad, bias, sink)


def _attn_bwd(name, p, k_pad, v_pad, bias, sink, dy, q_group_blk, dy_blk, after=None):
    t = p.shape[0]
    nh = bias.shape[0]
    aw = nh * HEAD
    grp = nh // KV_HEADS
    nb = t // WINDOW
    kvw = k_pad.shape[1]

    def body(q_ref, k_ref, v_ref, b_ref, s_ref, dy_ref, dq_ref, dk_ref, dv_ref, db_ref, ds_ref):
        i = pl.program_id(0)

        @pl.when(i == 0)
        def _():
            dk_ref[...] = jnp.zeros_like(dk_ref)
            dv_ref[...] = jnp.zeros_like(dv_ref)
            db_ref[...] = jnp.zeros_like(db_ref)
            ds_ref[...] = jnp.zeros_like(ds_ref)

        valid = _attn_valid(i, t)
        start = pl.multiple_of(i * WINDOW, WINDOW)
        ks = k_ref[pl.ds(start, SPAN), :]
        vs = v_ref[pl.ds(start, SPAN), :]
        inv_sqrt = 1.0 / math.sqrt(HEAD)
        heads = range(nh)
        col = lambda h: slice(h * HEAD, (h + 1) * HEAD)
        qs = [q_ref[:, col(h)].astype(BF16) for h in heads]
        khs = [ks[:, col(h // grp)] for h in heads]
        pr, ps = _attn_probs(qs, khs, b_ref, s_ref, valid)
        dos = [dy_ref[:, col(h)].astype(BF16) for h in heads]
        dp = [_dot(dos[h], vs[:, col(h // grp)], 1, 1) for h in heads]
        delta = [jnp.sum(pr[h] * dp[h], axis=-1, keepdims=True) for h in heads]
        dsc = [pr[h] * (dp[h] - delta[h]) for h in heads]
        dsr = [(dsc[h] * inv_sqrt).astype(BF16) for h in heads]
        dq = [_dot(dsr[h], khs[h]) for h in heads]
        dk = [_dot(dsr[h], qs[h], 0, 0) for h in heads]
        dv = [_dot(pr[h].astype(BF16), dos[h], 0, 0) for h in heads]
        for h in heads:
            db_ref[h] += dsc[h]
            ds_ref[h:h + 1, :] += jnp.broadcast_to(jnp.sum(-ps[h] * delta[h], axis=0, keepdims=True), (1, 128))
            dq_ref[:, col(h)] = dq[h]
        for kv in range(KV_HEADS):
            group = range(kv * grp, (kv + 1) * grp)
            dk_ref[pl.ds(start, SPAN), col(kv)] += sum(dk[h] for h in group)
            dv_ref[pl.ds(start, SPAN), col(kv)] += sum(dv[h] for h in group)

    full = lambda a: pl.BlockSpec(a.shape, lambda i: (0,) * a.ndim)
    whole = lambda shape: pl.BlockSpec(shape, lambda i: (0,) * len(shape))
    pad_shape = (t + 2 * WINDOW, kvw)
    body, ins, in_specs = _ordered(
        body, [p, k_pad, v_pad, bias, sink, dy],
        [pl.BlockSpec((WINDOW, aw), lambda i: (i, q_group_blk)), full(k_pad), full(v_pad), full(bias), full(sink),
         pl.BlockSpec((WINDOW, aw), lambda i: (i, dy_blk))], after)
    return _pallas(
        body, name=name, grid=(nb,), in_specs=in_specs,
        out_specs=[pl.BlockSpec((WINDOW, aw), lambda i: (i, 0)), whole(pad_shape), whole(pad_shape),
                   whole(bias.shape), whole((nh, 128))],
        out_shape=[jax.ShapeDtypeStruct((t, aw), F32), jax.ShapeDtypeStruct(pad_shape, F32),
                   jax.ShapeDtypeStruct(pad_shape, F32), jax.ShapeDtypeStruct(bias.shape, F32),
                   jax.ShapeDtypeStruct((nh, 128), F32)],
        compiler_params=_params(("arbitrary",), 3 * _nbytes(pad_shape, F32) + 2 * _nbytes(bias.shape, F32)),
    )(*ins)


def _pad_kv(name, p, kv_blk, kvw):
    t = p.shape[0]
    nb = t // WINDOW

    def body(x_ref, o_ref):
        i = pl.program_id(0)
        inside = jnp.logical_and(i >= 1, i <= nb)
        o_ref[...] = jnp.where(inside, x_ref[...], 0.0).astype(BF16)

    return _pallas(
        body, name=name, grid=(nb + 2,),
        in_specs=[pl.BlockSpec((WINDOW, kvw), lambda i: (jnp.clip(i - 1, 0, nb - 1), kv_blk))],
        out_specs=pl.BlockSpec((WINDOW, kvw), lambda i: (i, 0)),
        out_shape=jax.ShapeDtypeStruct((t + 2 * WINDOW, kvw), BF16),
        compiler_params=_params(("parallel",), 1 << 20),
    )(p)


def _mix_dproj(name, pieces, kv_pads, t, after=None):
    hw = pieces[0][0].shape[1]
    kvw = kv_pads[0].shape[1]
    widths = [hw] * len(pieces) + [kvw] * len(kv_pads)
    total = sum(widths)
    tm = WINDOW
    flat = [a for pc in pieces for a in pc]

    def body(*refs):
        o_ref = refs[-1]
        pos, off = 0, 0
        for pc in pieces:
            val = refs[pos][...]
            for extra in range(1, len(pc)):
                val = val + refs[pos + extra][...]
            o_ref[:, off:off + hw] = val.astype(BF16)
            pos += len(pc)
            off += hw
        for _ in kv_pads:
            o_ref[:, off:off + kvw] = refs[pos][...].astype(BF16)
            pos += 1
            off += kvw

    in_specs = [pl.BlockSpec((tm, hw), lambda i: (i, 0)) for _ in flat]
    in_specs += [pl.BlockSpec((tm, kvw), lambda i: (i + 1, 0)) for _ in kv_pads]
    body, ins, in_specs = _ordered(body, [*flat, *kv_pads], in_specs, after)
    return _pallas(
        body, name=name, grid=(t // tm,), in_specs=in_specs,
        out_specs=pl.BlockSpec((tm, total), lambda i: (i, 0)),
        out_shape=jax.ShapeDtypeStruct((t, total), BF16),
        compiler_params=_params(("parallel",), 3 * _nbytes((tm, total), F32)),
    )(*ins)


def _concat_cols(name, a, b):
    t, wa = a.shape
    wb = b.shape[1]
    tm = _tile(t, 512, 16)

    def body(a_ref, b_ref, o_ref):
        o_ref[:, :wa] = a_ref[...]
        o_ref[:, wa:] = b_ref[...]

    return _pallas(
        body, name=name, grid=(t // tm,),
        in_specs=[pl.BlockSpec((tm, wa), lambda i: (i, 0)), pl.BlockSpec((tm, wb), lambda i: (i, 0))],
        out_specs=pl.BlockSpec((tm, wa + wb), lambda i: (i, 0)),
        out_shape=jax.ShapeDtypeStruct((t, wa + wb), a.dtype),
        compiler_params=_params(("parallel",), 2 * _nbytes((tm, wa + wb), a.dtype)),
    )(a, b)


def _cast_into_full(name, w, geom, idx, after=None):
    r, c = w.shape
    tr = _tile(r, 256, 16)
    nr = r // tr
    if geom.col:
        place = lambda i, iref: (i, iref[0])
    else:
        place = lambda i, iref: (iref[0] * nr + i, 0)

    def body(i_ref, w_ref, *rest):
        rest[-1][...] = w_ref[...].astype(BF16)

    in_specs = [pl.BlockSpec((tr, c), lambda i, iref: (i, 0))]
    ins = [w]
    if after is not None:
        in_specs.append(pl.BlockSpec(memory_space=pl.ANY))
        ins.append(after)
    return _pallas(
        body, name=name,
        grid_spec=pltpu.PrefetchScalarGridSpec(
            num_scalar_prefetch=1, grid=(nr,), in_specs=in_specs, out_specs=pl.BlockSpec((tr, c), place)),
        out_shape=pltpu.HBM(geom.full_shape, BF16),
        compiler_params=_params(("parallel",), 2 * _nbytes((tr, c), F32)),
    )(idx, *ins)


def _adamw(name, w, g, m, v):
    r, c = w.shape
    tr = _tile(r, 128, 8)
    bc1 = 1.0 - ADAM_B1 ** ADAM_STEP
    bc2 = 1.0 - ADAM_B2 ** ADAM_STEP

    def body(w_ref, g_ref, m_ref, v_ref, go_ref, d_ref, nm_ref, nv_ref):
        gv = g_ref[...]
        go_ref[...] = gv
        nm = ADAM_B1 * m_ref[...] + (1.0 - ADAM_B1) * gv
        nv = ADAM_B2 * v_ref[...] + (1.0 - ADAM_B2) * (gv * gv)
        nm_ref[...] = nm
        nv_ref[...] = nv
        d_ref[...] = -ADAM_LR * ((nm / bc1) / (jnp.sqrt(nv / bc2) + ADAM_EPS) + ADAM_WD * w_ref[...])

    blk = pl.BlockSpec((tr, c), lambda i: (i, 0))
    out = jax.ShapeDtypeStruct((r, c), F32)
    return _pallas(
        body, name=name, grid=(r // tr,), in_specs=[blk] * 4, out_specs=[blk] * 4, out_shape=[out] * 4,
        compiler_params=_params(("parallel",), 8 * _nbytes((tr, c), F32)),
    )(w, g, m, v)


def _mesh_pos():
    return lax.axis_index("x"), lax.axis_index("y"), lax.axis_index("c")


def _other_chips(x, y):
    return [(1 - x, y), (x, 1 - y), (1 - x, 1 - y)]


class _Big:
    def __init__(self, shard_shape, col_sharded):
        self.col = col_sharded
        r, c = shard_shape
        self.shard_shape = (r, c)
        self.full_shape = (r, N_CHIPS * c) if col_sharded else (N_CHIPS * r, c)
        self.half_shape = (r // 2, N_CHIPS * c) if col_sharded else (N_CHIPS * r, c // 2)
        self.shard_half_shape = (r // 2, c) if col_sharded else (r, c // 2)

    def region(self, ref, s, half=None):
        r, c = self.shard_shape
        if self.col:
            rows = slice(None) if half is None else pl.ds(half * (r // 2), r // 2)
            return ref.at[rows, pl.ds(s * c, c)]
        cols = slice(None) if half is None else pl.ds(half * (c // 2), c // 2)
        return ref.at[pl.ds(s * r, r), cols]

    def n_halves(self, ref, half, n):
        r, c = self.shard_shape
        if self.col:
            return ref.at[pl.ds(half * (r // 2), r // 2), pl.ds(0, n * c)]
        return ref.at[pl.ds(0, n * r), pl.ds(half * (c // 2), c // 2)]

    def three_halves(self, ref, half):
        return self.n_halves(ref, half, 3)

    def sub_half(self, ref, s, half, j):
        r, c = self.shard_shape
        if self.col:
            return ref.at[pl.ds(half * (r // 2) + j * (r // 4), r // 4), pl.ds(s * c, c)]
        return ref.at[pl.ds(s * r + j * (r // 2), r // 2), pl.ds(half * (c // 2), c // 2)]

    def half_of_full(self, ref, half):
        r, c = self.full_shape
        if self.col:
            return ref.at[pl.ds(half * (r // 2), r // 2), :]
        return ref.at[:, pl.ds(half * (c // 2), c // 2)]

    def half_of_shard(self, ref, half):
        r, c = self.shard_shape
        if self.col:
            return ref.at[pl.ds(half * (r // 2), r // 2), :]
        return ref.at[:, pl.ds(half * (c // 2), c // 2)]

    def shard_of_half(self, ref, s):
        r, c = self.shard_shape
        if self.col:
            return ref.at[:, pl.ds(s * c, c)]
        return ref.at[pl.ds(s * r, r), :]


HBM =pl.BlockSpec(memory_space=pltpu.HBM)
SEM = pl.BlockSpec(memory_space=pltpu.SEMAPHORE)
SPLIT_COPY = pltpu.CompilerParams(has_side_effects=pltpu.SideEffectType.DATAFLOW_SIDE_EFFECTING)


def _in_hbm(a):
    return pltpu.with_memory_space_constraint(a, pltpu.HBM)


def _gather_start(name, fulls, geoms, after):
    nw = len(fulls)

    def body(*refs):
        dst = refs[nw + 1:2 * nw + 1]
        sems = refs[2 * nw + 1:-1]
        x, y, c = _mesh_pos()
        mine = 2 * x + y
        for w in range(nw):
            own_half = geoms[w].region(dst[w], mine, c)
            for chip in _other_chips(x, y):
                pltpu.make_async_remote_copy(src_ref=own_half, dst_ref=own_half, send_sem=sems[2 * w],
                                             recv_sem=sems[2 * w + 1], device_id=(*chip, c),
                                             device_id_type=MESH).start()
        refs[-1][...] = jnp.zeros_like(refs[-1])

    out = _pallas(
        body, name=name, in_specs=[HBM] * nw + [pl.BlockSpec(memory_space=pl.ANY)],
        out_specs=[HBM] * nw + [SEM] * (2 * nw) + [pl.BlockSpec(memory_space=pltpu.VMEM)],
        out_shape=[pltpu.HBM(g.full_shape, BF16) for g in geoms] + [pltpu.SemaphoreType.DMA(())] * (2 * nw)
        + [jax.ShapeDtypeStruct((8, 128), F32)],
        input_output_aliases={w: w for w in range(nw)}, compiler_params=SPLIT_COPY,
    )(*[_in_hbm(a) for a in fulls], after)
    return list(out[:nw]), [(out[nw + 2 * w], out[nw + 2 * w + 1]) for w in range(nw)], out[-1]


def _wait_three(geom, ref, half, send_sem, recv_sem, peer, recv):
    three = geom.three_halves(ref, half)
    copy = pltpu.make_async_remote_copy(src_ref=three, dst_ref=three, send_sem=send_sem, recv_sem=recv_sem,
                                        device_id=peer, device_id_type=MESH)
    if recv:
        copy.wait_recv()
    else:
        copy.wait_send()


def _gather_first_direct(full, geom):
    def start(refs, _, new):
        x, y, c = _mesh_pos()
        own = geom.region(refs[0], 2 * x + y, c)
        for chip in ((1 - x, y), (x, 1 - y)):
            _remote(own, own, new, (*chip, c)).start()

    return _split_copy_call("gather_first_direct", [full], start, new_sems=2)


def _gather_first_relay(full, geom, sems, after):
    def relay(refs, got, new):
        x, y, c = _mesh_pos()
        w = refs[0]
        two = geom.n_halves(w, c, 2)
        _remote(two, two, got, (x, y, 1 - c)).wait_recv()
        from_x = geom.sub_half(w, 2 * (1 - x) + y, c, 0)
        from_y = geom.sub_half(w, 2 * x + (1 - y), c, 1)
        _remote(from_x, from_x, new, (x, 1 - y, c)).start()
        _remote(from_y, from_y, new, (1 - x, y, c)).start()
        _remote(two, two, got, (x, y, 1 - c)).wait_send()

    return _split_copy_call("gather_first_relay", [full], relay, sems=sems, after=after, new_sems=2)


def _gather_forward(name, full, geom, sems, after, arrivals=3):
    def body(w_in, send_sem, recv_sem, after_ref, w_ref, fwd_send, fwd_recv):
        x, y, c = _mesh_pos()
        sibling = (x, y, 1 - c)
        landed_all = geom.n_halves(w_ref, c, arrivals)
        _remote(landed_all, landed_all, (send_sem, recv_sem), sibling).wait_recv()
        for chip in _other_chips(x, y):
            landed = geom.region(w_ref, 2 * chip[0] + chip[1], c)
            pltpu.make_async_remote_copy(src_ref=landed, dst_ref=landed, send_sem=fwd_send, recv_sem=fwd_recv,
                                         device_id=sibling, device_id_type=MESH).start()
        _remote(landed_all, landed_all, (send_sem, recv_sem), sibling).wait_send()

    sem = pltpu.SemaphoreType.DMA(())
    out = _pallas(
        body, name=name, in_specs=[HBM, SEM, SEM, pl.BlockSpec(memory_space=pl.ANY)], out_specs=[HBM, SEM, SEM],
        out_shape=[pltpu.HBM(geom.full_shape, BF16), sem, sem],
        input_output_aliases={0: 0}, compiler_params=SPLIT_COPY,
    )(full, sems[0], sems[1], after)
    return out[0], (out[1], out[2])


def _gather_end(name, full, geom, sems, after):
    def body(w_in, fwd_send, fwd_recv, after_ref, w_ref):
        x, y, c = _mesh_pos()
        sibling = (x, y, 1 - c)
        _wait_three(geom, w_ref, 1 - c, fwd_send, fwd_recv, sibling, recv=True)
        _wait_three(geom, w_ref, c, fwd_send, fwd_recv, sibling, recv=False)

    return _pallas(
        body, name=name, in_specs=[HBM, SEM, SEM, pl.BlockSpec(memory_space=pl.ANY)], out_specs=HBM,
        out_shape=pltpu.HBM(geom.full_shape, BF16),
        input_output_aliases={0: 0}, compiler_params=SPLIT_COPY,
    )(full, sems[0], sems[1], after)


def _split_copy_call(name, arrays, fn, sems=(), after=None, new_sems=0):
    n, ns = len(arrays), len(sems)
    n_in = n + ns + (after is not None)

    def body(*refs):
        fn(refs[n_in:n_in + n], refs[n:n + ns], refs[n_in + n:-1])
        refs[-1][...] = jnp.zeros_like(refs[-1])

    ins = list(arrays) if ns else [_in_hbm(a) for a in arrays]
    ins += list(sems) + ([after] if after is not None else [])
    in_specs = [HBM] * n + [SEM] * ns + ([pl.BlockSpec(memory_space=pl.ANY)] if after is not None else [])
    out = _pallas(
        body, name=name, in_specs=in_specs,
        out_specs=[HBM] * n + [SEM] * new_sems + [pl.BlockSpec(memory_space=pltpu.VMEM)],
        out_shape=[pltpu.HBM(a.shape, a.dtype) for a in arrays] + [pltpu.SemaphoreType.DMA(())] * new_sems
        + [jax.ShapeDtypeStruct((8, 128), F32)],
        input_output_aliases={i: i for i in range(n)}, compiler_params=SPLIT_COPY,
    )(*ins)
    return list(out[:n]), tuple(out[n:-1]), out[-1]


def _remote(src, dst, sems, to):
    return pltpu.make_async_remote_copy(src_ref=src, dst_ref=dst, send_sem=sems[0], recv_sem=sems[1],
                                        device_id=to, device_id_type=MESH)


class _GradReduce:
    def __init__(self, name, geom, idx, c_idx):
        self.name, self.geom, self.idx, self.c_idx = name, geom, idx, c_idx

    def pair_start(self, theirs):
        g = self.geom

        def start(refs, _, new):
            x, y, c = _mesh_pos()
            _remote(refs[0], refs[1], new, (x, y, 1 - c)).start()

        self.arrays, self.sems, token = _split_copy_call(
            f"pair_start_{self.name}", [theirs, lax.empty(g.half_shape, BF16)], start, new_sems=2)
        return token

    def pair_wait(self, after):
        def wait(refs, sems, _):
            x, y, c = _mesh_pos()
            copy = _remote(refs[0], refs[1], sems, (x, y, 1 - c))
            copy.wait_send()
            copy.wait_recv()

        (_, landed), _, _ = _split_copy_call(f"pair_wait_{self.name}", self.arrays, wait, self.sems, after)
        return landed

    def chip_start(self, half):
        g = self.geom

        def start(refs, _, new):
            x, y, c = _mesh_pos()
            for k, chip in enumerate(_other_chips(x, y)):
                _remote(g.shard_of_half(refs[0], 2 * chip[0] + chip[1]), refs[1].at[k], new, (*chip, c)).start()

        self.arrays, self.sems, token = _split_copy_call(
            f"chip_start_{self.name}", [half, lax.empty((3,) + g.shard_half_shape, BF16)], start, new_sems=2)
        return token

    def chip_finish(self, after):
        g = self.geom

        def wait(refs, sems, _):
            x, y, c = _mesh_pos()
            three = _remote(refs[1], refs[1], sems, (x, y, 1 - c))
            three.wait_send()
            three.wait_recv()

        (half, landed), _, _ = _split_copy_call(f"chip_wait_{self.name}", self.arrays, wait, self.sems, after)
        quarter = _chip_add(f"chip_add_{self.name}", half, landed, g, self.idx)

        def start(refs, _, new):
            x, y, c = _mesh_pos()
            own = g.half_of_shard(refs[0], c)
            _remote(own, own, new, (x, y, 1 - c)).start()

        self.arrays, self.sems, token = _split_copy_call(f"share_start_{self.name}", [quarter], start, new_sems=2)
        return token

    def finish(self, after):
        g = self.geom

        def wait(refs, sems, _):
            x, y, c = _mesh_pos()
            own, theirs = g.half_of_shard(refs[0], c), g.half_of_shard(refs[0], 1 - c)
            _remote(own, own, sems, (x, y, 1 - c)).wait_send()
            _remote(theirs, theirs, sems, (x, y, 1 - c)).wait_recv()

        (quarter,), _, _ = _split_copy_call(f"share_wait_{self.name}", self.arrays, wait, self.sems, after)
        return quarter


def _dw_half(name, x, dy, geom, c_idx, own, addend=None, after=None):
    stacked = dy.ndim == 3
    t, m = x.shape
    n = 2 * dy.shape[2] if stacked else dy.shape[1]
    hm, hn = (m // 2, n) if geom.col else (m, n // 2)
    tm, tn, tk = _mm_tiles(hm, hn, t, BF16, n_unit=(n // 2 if stacked else None))
    if tk != t:
        tm, tn = _tile(hm, 512, 128), _tile(hn // (2 if stacked else 1), 512, 128)
    gi, gj = hm // tm, hn // tn
    nf = (n // 2) // tn

    def sel(cref):
        return cref[0] if own else 1 - cref[0]

    a_map = (lambda i, j, cref: (0, sel(cref) * gi + i)) if geom.col else (lambda i, j, cref: (0, i))
    if stacked:
        b_blk, b_map = (None, t, tn), (lambda i, j, cref: (j // nf, 0, j % nf))
    elif geom.col:
        b_blk, b_map = (t, tn), (lambda i, j, cref: (0, j))
    else:
        b_blk, b_map = (t, tn), (lambda i, j, cref: (0, sel(cref) * gj + j))
    out_blk = pl.BlockSpec((tm, tn), lambda i, j, cref: (i, j))
    ins, in_specs = [x, dy], [pl.BlockSpec((t, tm), a_map), pl.BlockSpec(b_blk, b_map)]
    if addend is not None:
        ins.append(addend)
        in_specs.append(out_blk)
    if after is not None:
        ins.append(after)
        in_specs.append(pl.BlockSpec(memory_space=pl.ANY))

    def body(c_ref, *refs):
        acc = _dot(refs[0][...], refs[1][...], 0, 0)
        if addend is not None:
            acc = acc + refs[2][...].astype(F32)
        refs[len(ins)][...] = acc.astype(BF16)

    return _pallas(
        body, name=name,
        grid_spec=pltpu.PrefetchScalarGridSpec(num_scalar_prefetch=1, grid=(gi, gj), in_specs=in_specs,
                                               out_specs=out_blk),
        out_shape=jax.ShapeDtypeStruct((hm, hn), BF16),
        compiler_params=_params(("parallel", "parallel"),
                                _nbytes((t, tm), BF16) + _nbytes((t, tn), BF16) + 3 * _nbytes((tm, tn), F32)),
    )(c_idx, *ins)


def _chip_add(name, half, recv, geom, idx):
    r, c = geom.shard_half_shape
    tr, tc = _tile(r, 512, 16), _tile(c, 2048, 128)
    nr, ncol = r // tr, c // tc
    if geom.col:
        mine = lambda i, j, iref: (i, iref[0] * ncol + j)
        place = lambda i, j, iref: (iref[1] * nr + i, j)
    else:
        mine = lambda i, j, iref: (iref[0] * nr + i, j)
        place = lambda i, j, iref: (i, iref[1] * ncol + j)

    def body(i_ref, h_ref, r_ref, o_ref):
        acc = h_ref[...].astype(F32)
        for k in range(3):
            acc = acc + r_ref[k].astype(F32)
        o_ref[...] = acc

    return _pallas(
        body, name=name,
        grid_spec=pltpu.PrefetchScalarGridSpec(
            num_scalar_prefetch=1, grid=(nr, ncol),
            in_specs=[pl.BlockSpec((tr, tc), mine), pl.BlockSpec((3, tr, tc), lambda i, j, iref: (0, i, j))],
            out_specs=pl.BlockSpec((tr, tc), place)),
        out_shape=jax.ShapeDtypeStruct(geom.shard_shape, F32),
        compiler_params=_params(("parallel", "parallel"), 4 * _nbytes((tr, tc), F32)),
    )(idx, half, recv)


def _all_reduce_small(pack, after=None):
    r, d = pack.shape

    def body(p_ref, o_ref, slots, send_sems, recv_sems):
        x, y, c = _mesh_pos()
        me = 4 * x + 2 * y + c
        slots[me] = p_ref[...]
        copies = []
        for k in range(1, N_DEV):
            px, py, pc = x ^ ((k >> 2) & 1), y ^ ((k >> 1) & 1), c ^ (k & 1)
            copies.append(pltpu.make_async_remote_copy(
                src_ref=p_ref, dst_ref=slots.at[me], send_sem=send_sems.at[k - 1], recv_sem=recv_sems.at[k - 1],
                device_id=(px, py, pc), device_id_type=MESH))
        for cp in copies:
            cp.start()
        for k in range(1, N_DEV):
            peer = 4 * (x ^ ((k >> 2) & 1)) + 2 * (y ^ ((k >> 1) & 1)) + (c ^ (k & 1))
            pltpu.make_async_remote_copy(
                src_ref=p_ref, dst_ref=slots.at[peer], send_sem=send_sems.at[k - 1], recv_sem=recv_sems.at[k - 1],
                device_id=(x, y, c), device_id_type=MESH).wait_recv()
        for cp in copies:
            cp.wait_send()
        acc = slots[0]
        for k in range(1, N_DEV):
            acc = acc + slots[k]
        o_ref[...] = acc

    vm = pl.BlockSpec(memory_space=pltpu.VMEM)
    body, ins, in_specs = _ordered(body, [pack], [vm], after)
    return _pallas(
        body, name="all_reduce_small", in_specs=in_specs, out_specs=vm,
        out_shape=jax.ShapeDtypeStruct((r, d), F32),
        scratch_shapes=[pltpu.VMEM((N_DEV, r, d), F32), pltpu.SemaphoreType.DMA((N_DEV - 1,)),
                        pltpu.SemaphoreType.DMA((N_DEV - 1,))],
    )(*ins)


def _pack_rows(rows, d):
    out = []
    for a in rows:
        flat = a.reshape(-1)
        n = -(-flat.shape[0] // d) * d
        out.append(jnp.pad(flat, (0, n - flat.shape[0])).reshape(-1, d))
    packed = jnp.concatenate(out, axis=0)
    return jnp.pad(packed, ((0, 16 - packed.shape[0]), (0, 0)))


def _unpack_rows(packed, shapes, d):
    out, row = [], 0
    for shp in shapes:
        n = int(np.prod(shp))
        nrows = -(-n // d)
        out.append(packed[row:row + nrows].reshape(-1)[:n].reshape(shp))
        row += nrows
    return out


def kernel(x, pre_norm_ffn1, post_norm_ffn1, w_ffn1_gate_up, w_ffn1_down, pre_norm_mix, post_norm_mix, w_mix_in, hgrn_lower_bounds_fwd, hgrn_lower_bounds_bwd, hgrn_out_norm, attn_sink, w_mix_out, pre_norm_ffn2, post_norm_ffn2, w_ffn2_gate_up, w_ffn2_down, rel_bias_table, loss_target, m_pre_norm_ffn1, m_post_norm_ffn1, m_w_ffn1_gate_up, m_w_ffn1_down, m_pre_norm_mix, m_post_norm_mix, m_w_mix_in, m_hgrn_lower_bounds_fwd, m_hgrn_lower_bounds_bwd, m_hgrn_out_norm, m_attn_sink, m_w_mix_out, m_pre_norm_ffn2, m_post_norm_ffn2, m_w_ffn2_gate_up, m_w_ffn2_down, m_rel_bias_table, v_pre_norm_ffn1, v_post_norm_ffn1, v_w_ffn1_gate_up, v_w_ffn1_down, v_pre_norm_mix, v_post_norm_mix, v_w_mix_in, v_hgrn_lower_bounds_fwd, v_hgrn_lower_bounds_bwd, v_hgrn_out_norm, v_attn_sink, v_w_mix_out, v_pre_norm_ffn2, v_post_norm_ffn2, v_w_ffn2_gate_up, v_w_ffn2_down, v_rel_bias_table):
    t, d = x.shape[1], x.shape[2]
    hw = hgrn_out_norm.shape[1]
    aw = d - hw
    nah = aw // HEAD
    kvw = KV_HEADS * HEAD
    x0 = x[0]
    target = loss_target[0]

    big_names = ["w_ffn1_gate_up", "w_ffn1_down", "w_mix_in", "w_mix_out", "w_ffn2_gate_up", "w_ffn2_down"]
    big_w = [w_ffn1_gate_up[0], w_ffn1_down[0], w_mix_in[0], w_mix_out[0], w_ffn2_gate_up[0], w_ffn2_down[0]]
    big_m = [m_w_ffn1_gate_up[0], m_w_ffn1_down[0], m_w_mix_in[0], m_w_mix_out[0], m_w_ffn2_gate_up[0],
             m_w_ffn2_down[0]]
    big_v = [v_w_ffn1_gate_up[0], v_w_ffn1_down[0], v_w_mix_in[0], v_w_mix_out[0], v_w_ffn2_gate_up[0],
             v_w_ffn2_down[0]]
    col_sharded = [True, False, True, False, True, False]
    geoms = [_Big(w.shape, cs) for w, cs in zip(big_w, col_sharded)]

    cx, cy, cc = _mesh_pos()
    idx = jnp.stack([2 * cx + cy, cc]).astype(jnp.int32)
    c_idx = jnp.reshape(cc, (1,)).astype(jnp.int32)
    first = _cast_into_full(f"cast_{big_names[0]}", big_w[0], geoms[0], idx)
    (first,), direct_sems, tok = _gather_first_direct(first, geoms[0])
    rest = []
    for n, w, gm in zip(big_names[1:], big_w[1:], geoms[1:]):
        tok = _cast_into_full(f"cast_{n}", w, gm, idx, after=tok)
        rest.append(tok)
    (first,), relay_sems, tok = _gather_first_relay(first, geoms[0], direct_sems, after=tok)
    started_rest, sems_rest, rest_started = _gather_start("gather_start_rest", rest, geoms[1:], after=tok)
    started, gather_sems = [first] + started_rest, [relay_sems] + sems_rest

    def forward_weight(w, after):
        return _gather_forward(f"gather_forward_{big_names[w]}", started[w], geoms[w], gather_sems[w], after,
                               arrivals=1 if w == 0 else 3)

    def whole_weight(w, forwarded, after):
        return _gather_end(f"gather_end_{big_names[w]}", forwarded[0], geoms[w], forwarded[1], after)

    h1 = _norm_fwd("ffn1_pre_norm", x0, pre_norm_ffn1)
    w_gu1 = whole_weight(0, forward_weight(0, rest_started), h1)
    gate1, up1, act1 = _ffn_gate_up_act("ffn1_gate_up", h1, w_gu1)
    w_d1 = whole_weight(1, forward_weight(1, act1), act1)
    ff1 = _mm("ffn1_down", act1, w_d1, "nn", F32)
    fw = forward_weight(2, ff1)
    x1, hm = _resid_norm_fwd("ffn1_residual", x0, ff1, post_norm_ffn1, pre_norm_mix, 0.5)
    w_in = whole_weight(2, fw, hm)
    p = _mm("mix_in", hm, w_in, "nn", F32)
    fw = forward_weight(3, p)
    o_f, o_b, st_f, st_b = _hgrn_scan_fwd("hgrn_scan", p, hgrn_lower_bounds_fwd, hgrn_lower_bounds_bwd)
    y_h = _hgrn_out_fwd("hgrn_out", o_f, o_b, p, hgrn_out_norm, 4)
    kv_blk0 = (5 * hw + aw) // kvw
    k_pad = _pad_kv("attn_pad_k", p, kv_blk0, kvw)
    v_pad = _pad_kv("attn_pad_v", p, kv_blk0 + 1, kvw)
    bucket_ids = _t5_bucket_ids()
    bias = _bias_gather("attn_bias", rel_bias_table.T, bucket_ids).reshape(nah, WINDOW, SPAN)
    y_a = _attn_fwd("attn_fwd", p, k_pad, v_pad, bias, attn_sink, 5 * hw // aw)
    y_mix = _concat_cols("mix_concat", y_h, y_a)
    w_out = whole_weight(3, fw, y_mix)
    mixed = _mm("mix_out", y_mix, w_out, "nn", F32)
    fw = forward_weight(4, mixed)
    x2, h2 = _resid_norm_fwd("mix_residual", x1, mixed, post_norm_mix, pre_norm_ffn2, 1.0)
    w_gu2 = whole_weight(4, fw, h2)
    gate2, up2, act2 = _ffn_gate_up_act("ffn2_gate_up", h2, w_gu2)
    w_d2 = whole_weight(5, forward_weight(5, act2), act2)
    ff2 = _mm("ffn2_down", act2, w_d2, "nn", F32)
    loss_blk, dy, dff2, dg_post2 = _final_fwd_bwd("ffn2_residual_loss", x2, ff2, post_norm_ffn2, target, 0.5)

    reduce = [_GradReduce(n, gm, idx, c_idx) for n, gm in zip(big_names, geoms)]
    big_grads, big_delta, big_new_m, big_new_v = [None] * 6, [None] * 6, [None] * 6, [None] * 6

    def update(w, after):
        g, dl, nm, nv = _adamw(f"adamw_{big_names[w]}", big_w[w], reduce[w].finish(after), big_m[w], big_v[w])
        big_grads[w], big_delta[w], big_new_m[w], big_new_v[w] = g[None], dl[None], nm[None], nv[None]
        return dl

    def dw_start(w, x_act, dy_act, after=None):
        theirs = _dw_half(f"dw_theirs_{big_names[w]}", x_act, dy_act, geoms[w], c_idx, own=False, after=after)
        return reduce[w].pair_start(theirs)

    def dw_finish(w, x_act, dy_act, after):
        landed = reduce[w].pair_wait(after)
        half = _dw_half(f"dw_own_{big_names[w]}", x_act, dy_act, geoms[w], c_idx, own=True, addend=landed)
        return reduce[w].chip_start(half)

    tok = dw_start(5, act2, dff2)
    dgu2 = _ffn_dact("ffn2_dact", dff2, w_d2, gate2, up2, after=tok)
    tok = dw_finish(5, act2, dff2, after=dgu2)
    tok = dw_start(4, h2, dgu2, after=tok)
    dh2 = _ffn_dh("ffn2_dh", dgu2, w_gu2, after=tok)
    tok = dw_finish(4, h2, dgu2, after=dh2)
    dx2, dg_pre2, dmixed, dg_postm = _norms_bwd("mix_residual_bwd", dy, dh2, x2, pre_norm_ffn2,
                                                post=(mixed, post_norm_mix, 1.0), after=tok)
    tok = dw_start(3, y_mix, dmixed)
    dy_mix = _mm("mix_out_dx", dmixed, w_out, "nt", F32, after=tok)
    tok = dw_finish(3, y_mix, dmixed, after=dy_mix)
    dq_a, dk_pad, dv_pad, dbias, dsink = _attn_bwd("attn_bwd", p, k_pad, v_pad, bias, attn_sink, dy_mix,
                                                   5 * hw // aw, hw // aw, after=tok)
    tok = reduce[5].chip_finish(dq_a)
    drel_t = _bias_scatter("attn_dbias", dbias.reshape(nah, WINDOW * SPAN), bucket_ids)
    do, dg_h, dgain = _hgrn_out_bwd("hgrn_out_bwd", dy_mix, o_f, o_b, p, hgrn_out_norm, 4, after=tok)
    dq_f, dv_f, dz_f, dlb_f, dq_b, dv_b, dz_b, dlb_b = _hgrn_scan_bwd(
        "hgrn_scan_bwd", p, hgrn_lower_bounds_fwd, hgrn_lower_bounds_bwd, do, st_f, st_b)
    tok = reduce[4].chip_finish(dq_f)
    tok = reduce[3].chip_finish(tok)
    dp = _mix_dproj("mix_dproj", [(dq_f, dq_b), (dv_f, dv_b), (dz_f,), (dz_b,), (dg_h,), (dq_a,)],
                    [dk_pad, dv_pad], t, after=tok)
    tok = dw_start(2, hm, dp)
    dhm = _mm("mix_in_dx", dp, w_in, "nt", F32, after=tok)
    tok = dw_finish(2, hm, dp, after=dhm)
    dx1, dg_prem, dff1, dg_post1 = _norms_bwd("ffn1_residual_bwd", dx2, dhm, x1, pre_norm_mix,
                                              post=(ff1, post_norm_ffn1, 0.5), after=tok)
    tok = dw_start(1, act1, dff1)
    dgu1 = _ffn_dact("ffn1_dact", dff1, w_d1, gate1, up1, after=tok)
    tok = dw_finish(1, act1, dff1, after=dgu1)
    tok = reduce[2].chip_finish(tok)
    tok = dw_start(0, h1, dgu1, after=tok)
    done = update(2, tok)
    tok = dw_finish(0, h1, dgu1, after=done)
    dh1 = _ffn_dh("ffn1_dh", dgu1, w_gu1, after=tok)
    grad_x, dg_pre1 = _norms_bwd("ffn1_pre_norm_bwd", dx1, dh1, x0, pre_norm_ffn1)

    small_w = [pre_norm_ffn1, post_norm_ffn1, pre_norm_mix, post_norm_mix, hgrn_lower_bounds_fwd,
               hgrn_lower_bounds_bwd, hgrn_out_norm, attn_sink, pre_norm_ffn2, post_norm_ffn2, rel_bias_table]
    small_m = [m_pre_norm_ffn1, m_post_norm_ffn1, m_pre_norm_mix, m_post_norm_mix, m_hgrn_lower_bounds_fwd,
               m_hgrn_lower_bounds_bwd, m_hgrn_out_norm, m_attn_sink, m_pre_norm_ffn2, m_post_norm_ffn2,
               m_rel_bias_table]
    small_v = [v_pre_norm_ffn1, v_post_norm_ffn1, v_pre_norm_mix, v_post_norm_mix, v_hgrn_lower_bounds_fwd,
               v_hgrn_lower_bounds_bwd, v_hgrn_out_norm, v_attn_sink, v_pre_norm_ffn2, v_post_norm_ffn2,
               v_rel_bias_table]
    small_g = [dg_pre1, dg_post1, dg_prem, dg_postm, dlb_f, dlb_b, dgain, dsink[:, 0].reshape(1, nah), dg_pre2,
               dg_post2, drel_t.T]
    shapes = [a.shape for a in small_w]
    done = update(5, grad_x)
    done = update(4, done)
    done = update(3, done)
    summed = _all_reduce_small(_pack_rows(small_g + [loss_blk[0:1, 0:1]], d), after=done)
    loss = _unpack_rows(summed, shapes + [(1, 1)], d)[-1][0, 0]
    _, sd, sm, sv = _adamw("adamw_small", _pack_rows(small_w, d), summed, _pack_rows(small_m, d),
                           _pack_rows(small_v, d))
    small_grads = _unpack_rows(summed, shapes, d)
    small_delta, small_new_m, small_new_v = (_unpack_rows(a, shapes, d) for a in (sd, sm, sv))

    tok = reduce[1].chip_finish(sd)
    done = update(1, tok)
    tok = reduce[0].chip_finish(done)
    update(0, tok)

    def ordered(small, big):
        s = dict(zip(["pre1", "post1", "prem", "postm", "lbf", "lbb", "gain", "sink", "pre2", "post2", "rel"], small))
        b = dict(zip(["gu1", "d1", "win", "wout", "gu2", "d2"], big))
        return [s["pre1"], s["post1"], b["gu1"], b["d1"], s["prem"], s["postm"], b["win"], s["lbf"], s["lbb"],
                s["gain"], s["sink"], b["wout"], s["pre2"], s["post2"], b["gu2"], b["d2"], s["rel"]]

    return (loss, grad_x[None], *ordered(small_grads, big_grads), *ordered(small_delta, big_delta),
            *ordered(small_new_m, big_new_m), *ordered(small_new_v, big_new_v))
```

```python
import functools
import math

import jax
import jax.numpy as jnp
import numpy as np
from jax import lax
from jax.experimental import pallas as pl
from jax.experimental.pallas import tpu as pltpu

F32 = jnp.float32
BF16 = jnp.bfloat16

HEAD = 128
CHUNK = 64
WINDOW = 128
SPAN = 3 * WINDOW
KV_HEADS = 2
REL_BUCKETS = 32
REL_MAX_DIST = 128
EPS = 1e-6
NEG_INF = -1e30

ADAM_LR = 0.001
ADAM_B1 = 0.9
ADAM_B2 = 0.999
ADAM_EPS = 1e-08
ADAM_WD = 0.01
ADAM_STEP = 10

N_CHIPS = 4
N_DEV = 8
V7X_VMEM_BYTES = 64 * 1024 * 1024
MESH = pl.DeviceIdType.MESH
ANY = pl.BlockSpec(memory_space=pl.ANY)


def _tile(n, pref, mult):
    t = (min(pref, n) // mult) * mult
    while t >= mult:
        if n % t == 0:
            return t
        t -= mult
    return n


def _params(semantics, block_bytes):
    limit = min(V7X_VMEM_BYTES - (4 << 20), 2 * int(block_bytes) + (8 << 20))
    return pltpu.CompilerParams(dimension_semantics=semantics, vmem_limit_bytes=limit)


def _nbytes(shape, dtype):
    return int(np.prod(shape)) * jnp.dtype(dtype).itemsize


PIN_TO_HBM_BYTES = 4 << 20


def _pallas(body, **kw):
    def pin_shape(s):
        if isinstance(s, jax.ShapeDtypeStruct) and _nbytes(s.shape, s.dtype) >= PIN_TO_HBM_BYTES:
            return pltpu.HBM(s.shape, s.dtype)
        return s

    def pin(a):
        if getattr(a, "dtype", None) in (F32, BF16) and _nbytes(a.shape, a.dtype) >= PIN_TO_HBM_BYTES:
            return pltpu.with_memory_space_constraint(a, pltpu.HBM)
        return a

    out_shape = kw["out_shape"]
    kw["out_shape"] = [pin_shape(s) for s in out_shape] if isinstance(out_shape, (list, tuple)) else pin_shape(out_shape)
    call = pl.pallas_call(body, **kw)
    return lambda *args: call(*[pin(a) for a in args])


def _dot(a, b, ca=1, cb=0):
    return lax.dot_general(a, b, (((ca,), (cb,)), ((), ())), preferred_element_type=F32)


def _split3(x):
    hi = x.astype(BF16)
    r1 = x - hi.astype(F32)
    mid = r1.astype(BF16)
    lo = (r1 - mid.astype(F32)).astype(BF16)
    return hi, mid, lo


def _dot_exact(a, b, ca=1, cb=0, split="b"):
    if split == "b":
        return sum(_dot(a, p, ca, cb) for p in _split3(b))
    return sum(_dot(p, b, ca, cb) for p in _split3(a))


def _rms(x):
    return lax.rsqrt(jnp.mean(x * x, axis=-1, keepdims=True) + EPS)


def _norm_bwd(u, x, gain):
    r = _rms(x)
    xhat = x * r
    dgain = jnp.sum(u * xhat, axis=0, keepdims=True)
    v = u * gain
    dx = r * (v - xhat * jnp.mean(v * xhat, axis=-1, keepdims=True))
    return dx, dgain


def _sigmoid(x):
    return 1.0 / (1.0 + jnp.exp(-x))


def _accumulate(ref, val, first):
    @pl.when(first)
    def _():
        ref[...] = val

    @pl.when(jnp.logical_not(first))
    def _():
        ref[...] += val


def _ordered(body, ins, in_specs, after):
    if after is None:
        return body, list(ins), list(in_specs)
    n_in = len(ins)

    def wrapped(*refs):
        body(*refs[:n_in], *refs[n_in + 1:])

    return wrapped, list(ins) + [after], list(in_specs) + [pl.BlockSpec(memory_space=pl.ANY)]


def _matmul(name, a, b, *, form, out_dtype, tm, tn, tk, a_map=None, b_map=None,
            out_shape=None, out_block=None, out_map=None, sizes=None, after=None):
    if sizes is None:
        if form == "nn":
            (m, k), n = a.shape, b.shape[1]
        elif form == "nt":
            (m, k), n = a.shape, b.shape[0]
        else:
            (k, m), n = a.shape, b.shape[1]
    else:
        m, n, k = sizes
    gi, gj, gk = m // tm, n // tn, k // tk
    a_blk = (tm, tk) if form != "tn" else (tk, tm)
    b_blk = (tk, tn) if form != "nt" else (tn, tk)
    if a_map is None:
        a_map = (lambda i, j, kk: (i, kk)) if form != "tn" else (lambda i, j, kk: (kk, i))
    else:
        a_blk = (None,) + a_blk
    if b_map is None:
        b_map = (lambda i, j, kk: (kk, j)) if form != "nt" else (lambda i, j, kk: (j, kk))
    else:
        b_blk = (None,) + b_blk
    if out_shape is None:
        out_shape, out_block, out_map = (m, n), (tm, tn), (lambda i, j, kk: (i, j))
    ca, cb = {"nn": (1, 0), "nt": (1, 1), "tn": (0, 0)}[form]

    def body(a_ref, b_ref, o_ref, *acc):
        part = _dot(a_ref[...], b_ref[...], ca, cb)
        if gk == 1:
            o_ref[...] = part.astype(o_ref.dtype)
        else:
            kk = pl.program_id(2)
            _accumulate(acc[0], part, kk == 0)

            @pl.when(kk == gk - 1)
            def _():
                o_ref[...] = acc[0][...].astype(o_ref.dtype)

    scratch = [] if gk == 1 else [pltpu.VMEM((tm, tn), F32)]
    vmem = (_nbytes((tm, tk), a.dtype) + _nbytes((tk, tn), b.dtype) + _nbytes((tm, tn), out_dtype)
            + 2 * _nbytes((tm, tn), F32))
    body, ins, in_specs = _ordered(body, [a, b], [pl.BlockSpec(a_blk, a_map), pl.BlockSpec(b_blk, b_map)], after)
    return _pallas(
        body, name=name, grid=(gi, gj, gk), in_specs=in_specs,
        out_specs=pl.BlockSpec(out_block, out_map),
        out_shape=jax.ShapeDtypeStruct(out_shape, out_dtype),
        scratch_shapes=scratch,
        compiler_params=_params(("parallel", "parallel", "arbitrary"), vmem),
    )(*ins)


V7X_HBM_BYTES_PER_US = 3.0e6
V7X_MXU_FLOPS_PER_US = 0.9e9
V7X_VMEM_RMW_BYTES_PER_US = 10e6
GRID_STEP_US = 0.35
MATMUL_VMEM_BUDGET = 40 << 20
MATMUL_MAX_TILE_FLOPS = 1 << 33


def _divisors(n, mult, lo):
    return [t for t in range(mult, n + 1, mult) if n % t == 0 and t >= min(lo, n)]


def _mm_tiles(m, n, k, out_dtype=F32, n_unit=None, k_unit=None):
    out_bytes = jnp.dtype(out_dtype).itemsize
    best = None
    for tm in _divisors(m, 128, 256):
        for tn in _divisors(n_unit or n, 128, 256):
            for tk in _divisors(k_unit or k, 128, 512):
                gi, gj, gk = m // tm, n // tn, k // tk
                vmem = 4 * tm * tk + 4 * tk * tn + 2 * tm * tn * out_bytes + 4 * tm * tn * (2 if gk > 1 else 1)
                if vmem > MATMUL_VMEM_BUDGET or 2 * tm * tn * tk > MATMUL_MAX_TILE_FLOPS:
                    continue
                a_bytes = 2 * m * k * (gj if gk > 1 else 1)
                b_bytes = 2 * k * n * (1 if gj == 1 and gk == 1 else gi)
                hbm_us = (a_bytes + b_bytes + m * n * out_bytes) / V7X_HBM_BYTES_PER_US
                acc_us = (8 * m * n * gk / V7X_VMEM_RMW_BYTES_PER_US) if gk > 1 else 0.0
                cost = max(2 * m * n * k / V7X_MXU_FLOPS_PER_US, 1.3 * hbm_us) + GRID_STEP_US * gi * gj * gk + acc_us
                key = (round(cost, 1), vmem)
                if best is None or key < best[0]:
                    best = (key, (tm, tn, tk))
    return best[1]


def _mm(name, a, b, form, out_dtype, after=None):
    if form == "nn":
        m, k, n = a.shape[0], a.shape[1], b.shape[1]
    elif form == "nt":
        m, k, n = a.shape[0], a.shape[1], b.shape[0]
    else:
        m, k, n = a.shape[1], a.shape[0], b.shape[1]
    tm, tn, tk = _mm_tiles(m, n, k, out_dtype)
    return _matmul(name, a, b, form=form, out_dtype=out_dtype, tm=tm, tn=tn, tk=tk, after=after)


def _row_tile(t):
    return _tile(t, 256, 8)


def _norm_fwd(name, x, gain):
    t, d = x.shape
    tm = _row_tile(t)

    def body(x_ref, g_ref, h_ref):
        xv = x_ref[...]
        h_ref[...] = (xv * _rms(xv) * g_ref[...]).astype(BF16)

    row = pl.BlockSpec((tm, d), lambda i: (i, 0))
    vec = pl.BlockSpec((1, d), lambda i: (0, 0))
    return _pallas(
        body, name=name, grid=(t // tm,), in_specs=[row, vec], out_specs=row,
        out_shape=jax.ShapeDtypeStruct((t, d), BF16),
        compiler_params=_params(("parallel",), 2 * _nbytes((tm, d), F32)),
    )(x, gain)


def _resid_norm_fwd(name, xres, ff, gpost, gpre, scale):
    t, d = xres.shape
    tm = _row_tile(t)

    def body(x_ref, f_ref, gp_ref, gn_ref, xn_ref, h_ref):
        f = f_ref[...]
        xn = x_ref[...] + scale * (f * _rms(f) * gp_ref[...])
        xn_ref[...] = xn
        h_ref[...] = (xn * _rms(xn) * gn_ref[...]).astype(BF16)

    row = pl.BlockSpec((tm, d), lambda i: (i, 0))
    vec = pl.BlockSpec((1, d), lambda i: (0, 0))
    return _pallas(
        body, name=name, grid=(t // tm,), in_specs=[row, row, vec, vec], out_specs=[row, row],
        out_shape=[jax.ShapeDtypeStruct((t, d), F32), jax.ShapeDtypeStruct((t, d), BF16)],
        compiler_params=_params(("parallel",), 4 * _nbytes((tm, d), F32)),
    )(xres, ff, gpost, gpre)


def _final_fwd_bwd(name, xres, ff, gpost, target, scale):
    t, d = xres.shape
    tm = _row_tile(t)

    def body(x_ref, f_ref, gp_ref, t_ref, loss_ref, dy_ref, dff_ref, dg_ref):
        i = pl.program_id(0)
        f = f_ref[...]
        gp = gp_ref[...]
        y = x_ref[...] + scale * (f * _rms(f) * gp)
        err = y - t_ref[...]
        part = 0.5 * jnp.sum(jnp.mean(err * err, axis=-1, keepdims=True), axis=0, keepdims=True)
        _accumulate(loss_ref, jnp.broadcast_to(part, loss_ref.shape), i == 0)
        dy = err / d
        dy_ref[...] = dy
        dff, dg = _norm_bwd(scale * dy, f, gp)
        dff_ref[...] = dff.astype(BF16)
        _accumulate(dg_ref, dg, i == 0)

    row = pl.BlockSpec((tm, d), lambda i: (i, 0))
    vec = pl.BlockSpec((1, d), lambda i: (0, 0))
    return _pallas(
        body, name=name, grid=(t // tm,), in_specs=[row, row, vec, row],
        out_specs=[pl.BlockSpec((8, 128), lambda i: (0, 0)), row, row, vec],
        out_shape=[jax.ShapeDtypeStruct((8, 128), F32), jax.ShapeDtypeStruct((t, d), F32),
                   jax.ShapeDtypeStruct((t, d), BF16), jax.ShapeDtypeStruct((1, d), F32)],
        compiler_params=_params(("arbitrary",), 5 * _nbytes((tm, d), F32)),
    )(xres, ff, gpost, target)


def _norms_bwd(name, dres, dh, xin, gpre, post=None, after=None):
    t, d = dres.shape
    tm = _row_tile(t)
    with_post = post is not None

    def body(*refs):
        if with_post:
            dr_ref, dh_ref, x_ref, g_ref, f_ref, gp_ref, dx_ref, dg_ref, dff_ref, dgp_ref = refs
        else:
            dr_ref, dh_ref, x_ref, g_ref, dx_ref, dg_ref = refs
        i = pl.program_id(0)
        dx, dg = _norm_bwd(dh_ref[...], x_ref[...], g_ref[...])
        dx = dr_ref[...] + dx
        dx_ref[...] = dx
        _accumulate(dg_ref, dg, i == 0)
        if with_post:
            dff, dgp = _norm_bwd(post[2] * dx, f_ref[...], gp_ref[...])
            dff_ref[...] = dff.astype(BF16)
            _accumulate(dgp_ref, dgp, i == 0)

    row = pl.BlockSpec((tm, d), lambda i: (i, 0))
    vec = pl.BlockSpec((1, d), lambda i: (0, 0))
    ins, in_specs = [dres, dh, xin, gpre], [row, row, row, vec]
    out_specs = [row, vec]
    out_shape = [jax.ShapeDtypeStruct((t, d), F32), jax.ShapeDtypeStruct((1, d), F32)]
    if with_post:
        ins += [post[0], post[1]]
        in_specs += [row, vec]
        out_specs += [row, vec]
        out_shape += [jax.ShapeDtypeStruct((t, d), BF16), jax.ShapeDtypeStruct((1, d), F32)]
    body, ins, in_specs = _ordered(body, ins, in_specs, after)
    return _pallas(
        body, name=name, grid=(t // tm,), in_specs=in_specs, out_specs=out_specs, out_shape=out_shape,
        compiler_params=_params(("arbitrary",), 6 * _nbytes((tm, d), F32)),
    )(*ins)


SWIGLU_TILE = (1024, 512)
V7X_MXU_COLS = 256


def _ffn_gate_up_act(name, h, w_gu):
    t, d = h.shape
    f = w_gu.shape[1] // 2
    tm, tn = _tile(t, SWIGLU_TILE[0], 128), _tile(f, SWIGLU_TILE[1], 128)
    nf = f // tn

    def body(h_ref, wg_ref, wu_ref, g_ref, u_ref, a_ref):
        hv = h_ref[...]
        for c0 in range(0, tn, min(tn, V7X_MXU_COLS)):
            cols = slice(c0, c0 + min(tn, V7X_MXU_COLS))
            g = _dot(hv, wg_ref[:, cols])
            u = _dot(hv, wu_ref[:, cols])
            g_ref[:, cols] = g.astype(BF16)
            u_ref[:, cols] = u.astype(BF16)
            a_ref[:, cols] = (g * _sigmoid(g) * u).astype(BF16)

    out = jax.ShapeDtypeStruct((t, f), BF16)
    blk = pl.BlockSpec((tm, tn), lambda i, j: (i, j))
    return _pallas(
        body, name=name, grid=(t // tm, nf),
        in_specs=[pl.BlockSpec((tm, d), lambda i, j: (i, 0)), pl.BlockSpec((d, tn), lambda i, j: (0, j)),
                  pl.BlockSpec((d, tn), lambda i, j: (0, j + nf))],
        out_specs=[blk, blk, blk], out_shape=[out, out, out],
        compiler_params=_params(("parallel", "parallel"),
                                _nbytes((tm, d), BF16) + 2 * _nbytes((d, tn), BF16) + 5 * _nbytes((tm, tn), F32)),
    )(h, w_gu, w_gu)


def _ffn_gate_up_act_half(name, h, w_gu, blk, into=None):
    t, d = h.shape
    f = w_gu.shape[1] // 2
    tm, tn = _tile(t, SWIGLU_TILE[0], 128), _tile(f // 2, SWIGLU_TILE[1], 128)
    nf, nh = f // tn, (f // 2) // tn

    def body(blk_ref, h_ref, wg_ref, wu_ref, *rest):
        g_ref, u_ref, a_ref = rest[-3:]
        hv = h_ref[...]
        g = _dot(hv, wg_ref[...])
        u = _dot(hv, wu_ref[...])
        g_ref[...] = g.astype(BF16)
        u_ref[...] = u.astype(BF16)
        a_ref[...] = (g * _sigmoid(g) * u).astype(BF16)

    out = jax.ShapeDtypeStruct((t, f), BF16)
    blk_spec = pl.BlockSpec((tm, tn), lambda i, j, b: (i, b[0] * nh + j))
    in_specs = [pl.BlockSpec((tm, d), lambda i, j, b: (i, 0)),
                pl.BlockSpec((d, tn), lambda i, j, b: (0, b[0] * nh + j)),
                pl.BlockSpec((d, tn), lambda i, j, b: (0, nf + b[0] * nh + j))]
    ins = [h, w_gu, w_gu]
    aliases = {}
    if into is not None:
        in_specs += [pl.BlockSpec(memory_space=pl.ANY)] * 3
        ins += list(into)
        aliases = {4 + k: k for k in range(3)}
    return _pallas(
        body, name=name,
        grid_spec=pltpu.PrefetchScalarGridSpec(num_scalar_prefetch=1, grid=(t // tm, nh), in_specs=in_specs,
                                               out_specs=[blk_spec, blk_spec, blk_spec]),
        out_shape=[out, out, out], input_output_aliases=aliases,
        compiler_params=_params(("parallel", "parallel"),
                                _nbytes((tm, d), BF16) + 2 * _nbytes((d, tn), BF16) + 5 * _nbytes((tm, tn), F32)),
    )(blk, *ins)


def _ffn_dact(name, dff, w_down, gate, up, after=None):
    t, d = dff.shape
    f = w_down.shape[0]
    tm, tn = _tile(t, SWIGLU_TILE[0], 128), _tile(f, SWIGLU_TILE[1], 128)

    def body(d_ref, w_ref, g_ref, u_ref, o_ref):
        dv = d_ref[...]
        for c0 in range(0, tn, min(tn, V7X_MXU_COLS)):
            cols = slice(c0, c0 + min(tn, V7X_MXU_COLS))
            da = _dot(dv, w_ref[cols, :], 1, 1)
            g = g_ref[:, cols].astype(F32)
            u = u_ref[:, cols].astype(F32)
            sig = _sigmoid(g)
            o_ref[0, :, cols] = (da * u * sig * (1.0 + g * (1.0 - sig))).astype(BF16)
            o_ref[1, :, cols] = (da * g * sig).astype(BF16)

    blk = pl.BlockSpec((tm, tn), lambda i, j: (i, j))
    body, ins, in_specs = _ordered(
        body, [dff, w_down, gate, up],
        [pl.BlockSpec((tm, d), lambda i, j: (i, 0)), pl.BlockSpec((tn, d), lambda i, j: (j, 0)), blk, blk], after)
    return _pallas(
        body, name=name, grid=(t // tm, f // tn), in_specs=in_specs,
        out_specs=pl.BlockSpec((2, tm, tn), lambda i, j: (0, i, j)),
        out_shape=jax.ShapeDtypeStruct((2, t, f), BF16),
        compiler_params=_params(("parallel", "parallel"),
                                _nbytes((tm, d), BF16) + _nbytes((tn, d), BF16) + 5 * _nbytes((tm, tn), F32)),
    )(*ins)


def _ffn_dh(name, dgu, w_gu, after=None):
    _, t, f = dgu.shape
    d = w_gu.shape[0]
    tm, tn, tk = _mm_tiles(t, d, 2 * f, F32, k_unit=f)
    nkf = f // tk
    return _matmul(name, dgu, w_gu, form="nt", out_dtype=F32, tm=tm, tn=tn, tk=tk, sizes=(t, d, 2 * f),
                   a_map=lambda i, j, kk: (kk // nkf, i, kk % nkf), after=after)


def _lower_bound(lbp):
    m = jnp.max(lbp, axis=0, keepdims=True)
    e = jnp.exp(lbp - m)
    return e[0:1] / jnp.sum(e, axis=0, keepdims=True)


def _chunk_mask(reverse):
    row = lax.broadcasted_iota(jnp.int32, (CHUNK, CHUNK), 0)
    col = lax.broadcasted_iota(jnp.int32, (CHUNK, CHUNK), 1)
    return (col >= row) if reverse else (col <= row)


def _hgrn_gates(z, lb, mask_bf):
    sig = _sigmoid(z)
    f = lb + (1.0 - lb) * sig
    logf = jnp.log(f)
    k = 1.0 - f
    cum = _dot_exact(mask_bf, logf)
    last = jnp.sum(logf, axis=0, keepdims=True)
    return sig, f, k, cum, last


def _hgrn_scan_fwd(name, p, lbp_f, lbp_b):
    t = p.shape[0]
    hw = lbp_f.shape[1]
    nh, nc = hw // HEAD, t // CHUNK

    def body(qf, vf, zf, qb, vb, zb, lbf, lbb, of_ref, ob_ref, stf_ref, stb_ref, state):
        n = pl.program_id(0)

        @pl.when(n == 0)
        def _():
            state[...] = jnp.zeros_like(state)

        directions = [(qf, vf, zf, lbf, of_ref, stf_ref), (qb, vb, zb, lbb, ob_ref, stb_ref)]
        wide = []
        for d, (q_ref, v_ref, z_ref, lb_ref, o_ref, st_ref) in enumerate(directions):
            mask = _chunk_mask(d == 1)
            lb = _lower_bound(lb_ref[...])
            _, _, k, cum, last = _hgrn_gates(z_ref[...], lb, mask.astype(BF16))
            v = v_ref[...].astype(BF16)
            qd = (q_ref[...] * jnp.exp(cum)).astype(BF16)
            kd = (k * jnp.exp(-cum)).astype(BF16)
            kt = (k * jnp.exp(last - cum)).astype(BF16)
            s_all = state[d]
            st_ref[...] = s_all
            wide.append((mask, v, qd, kd, kt, jnp.exp(last), s_all, o_ref))
        pairs = [(d, slice(h * HEAD, (h + 1) * HEAD)) for d in range(2) for h in range(nh)]
        a = [jnp.where(wide[d][0], _dot(wide[d][2][:, sl], wide[d][3][:, sl], 1, 1), 0.0).astype(BF16)
             for d, sl in pairs]
        inter = [_dot(wide[d][2][:, sl], wide[d][6][:, sl].astype(BF16), 1, 1) for d, sl in pairs]
        intra = [_dot(a[i], wide[d][1][:, sl]) for i, (d, sl) in enumerate(pairs)]
        grow = [_dot(wide[d][1][:, sl], wide[d][4][:, sl], 0, 0) for d, sl in pairs]
        for i, (d, sl) in enumerate(pairs):
            wide[d][7][:, sl] = intra[i] + inter[i]
            state[d, :, sl] = wide[d][6][:, sl] * wide[d][5][:, sl] + grow[i]

    def col(group, reverse):
        return pl.BlockSpec((CHUNK, hw), lambda n: ((nc - 1 - n) if reverse else n, group))

    def st(reverse):
        return pl.BlockSpec((None, HEAD, hw), lambda n: ((nc - 1 - n) if reverse else n, 0, 0))

    lb_spec = pl.BlockSpec((2, hw), lambda n: (0, 0))
    out = jax.ShapeDtypeStruct((t, hw), F32)
    states = jax.ShapeDtypeStruct((nc, HEAD, hw), F32)
    return _pallas(
        body, name=name, grid=(nc,),
        in_specs=[col(0, False), col(1, False), col(2, False), col(0, True), col(1, True), col(3, True),
                  lb_spec, lb_spec],
        out_specs=[col(0, False), col(0, True), st(False), st(True)],
        out_shape=[out, out, states, states],
        scratch_shapes=[pltpu.VMEM((2, HEAD, hw), F32)],
        compiler_params=_params(("arbitrary",), 12 * _nbytes((HEAD, hw), F32)),
    )(p, p, p, p, p, p, lbp_f, lbp_b)


def _hgrn_scan_bwd(name, p, lbp_f, lbp_b, do, st_f, st_b):
    t = p.shape[0]
    hw = lbp_f.shape[1]
    nh, nc = hw // HEAD, t // CHUNK

    def body(qf, vf, zf, dof, sf, qb, vb, zb, dob, sb, lbf, lbb, dqf, dvf, dzf, dlbf, dqb, dvb, dzb, dlbb,
             dstate, dlb_acc, dqd_s, dkd_s, dkt_s, ddec_s):
        n = pl.program_id(0)

        @pl.when(n == 0)
        def _():
            dstate[...] = jnp.zeros_like(dstate)
            dlb_acc[...] = jnp.zeros_like(dlb_acc)

        directions = [(qf, vf, zf, dof, sf, lbf, dqf, dvf, dzf, dlbf), (qb, vb, zb, dob, sb, lbb, dqb, dvb, dzb, dlbb)]
        for d, (q_ref, v_ref, z_ref, do_ref, st_ref, lb_ref, dq_ref, dv_ref, dz_ref, dlb_ref) in enumerate(directions):
            mask = _chunk_mask(d == 1)
            mask_bf = mask.astype(BF16)
            lb = _lower_bound(lb_ref[...])
            sig, f, k, cum, last = _hgrn_gates(z_ref[...], lb, mask_bf)
            e_pos, e_neg, e_tail = jnp.exp(cum), jnp.exp(-cum), jnp.exp(last - cum)
            dec = jnp.exp(last)
            v = v_ref[...].astype(BF16)
            qd, kd, kt = q_ref[...] * e_pos, k * e_neg, k * e_tail
            qd_bf, kd_bf, kt_bf = qd.astype(BF16), kd.astype(BF16), kt.astype(BF16)
            s_all = st_ref[...]
            ds_all = dstate[d]
            dov = do_ref[...].astype(BF16)
            cols = [slice(h * HEAD, (h + 1) * HEAD) for h in range(nh)]
            s_bf = [s_all[:, sl].astype(BF16) for sl in cols]
            ds_bf = [ds_all[:, sl].astype(BF16) for sl in cols]
            a = [jnp.where(mask, _dot(qd_bf[:, sl], kd_bf[:, sl], 1, 1), 0.0).astype(BF16) for sl in cols]
            da = [jnp.where(mask, _dot(dov[:, sl], v[:, sl], 1, 1), 0.0).astype(BF16) for sl in cols]
            dv_h = [_dot(a[h], dov[:, sl], 0, 0) + _dot(kt_bf[:, sl], ds_bf[h], 1, 1) for h, sl in enumerate(cols)]
            dqd_h = [_dot(da[h], kd_bf[:, sl]) + _dot(dov[:, sl], s_bf[h]) for h, sl in enumerate(cols)]
            dkd_h = [_dot(da[h], qd_bf[:, sl], 0, 0) for h, sl in enumerate(cols)]
            dkt_h = [_dot(v[:, sl], ds_bf[h]) for h, sl in enumerate(cols)]
            dst_h = [_dot(dov[:, sl], qd_bf[:, sl], 0, 0) + ds_all[:, sl] * dec[:, sl] for sl in cols]
            for h, sl in enumerate(cols):
                dv_ref[:, sl] = dv_h[h]
                dqd_s[:, sl] = dqd_h[h]
                dkd_s[:, sl] = dkd_h[h]
                dkt_s[:, sl] = dkt_h[h]
                dstate[d, :, sl] = dst_h[h]
                ddec_s[:, sl] = jnp.sum(ds_all[:, sl] * s_all[:, sl], axis=0, keepdims=True)
            dqd, dkd, dkt = dqd_s[...], dkd_s[...], dkt_s[...]
            dlast = jnp.sum(dkt * kt, axis=0, keepdims=True) + dec * ddec_s[...]
            dq_ref[...] = dqd * e_pos
            dk = dkd * e_neg + dkt * e_tail
            dcum = dqd * qd - dkd * kd - dkt * kt
            dlogf = _dot_exact(mask_bf, dcum, 0, 0) + dlast
            df = dlogf / f - dk
            dz_ref[...] = df * (1.0 - lb) * sig * (1.0 - sig)
            dlb_acc[d] += jnp.sum(df * (1.0 - sig), axis=0, keepdims=True)

            @pl.when(n == nc - 1)
            def _():
                g = dlb_acc[d] * lb * (1.0 - lb)
                dlb_ref[0:1, :] = g
                dlb_ref[1:2, :] = -g

    def col(group, reverse):
        return pl.BlockSpec((CHUNK, hw), lambda n: (n if reverse else (nc - 1 - n), group))

    def st(reverse):
        return pl.BlockSpec((None, HEAD, hw), lambda n: (n if reverse else (nc - 1 - n), 0, 0))

    lb_spec = pl.BlockSpec((2, hw), lambda n: (0, 0))
    out = jax.ShapeDtypeStruct((t, hw), F32)
    dlb = jax.ShapeDtypeStruct((2, hw), F32)
    wide = pltpu.VMEM((CHUNK, hw), F32)
    return _pallas(
        body, name=name, grid=(nc,),
        in_specs=[col(0, False), col(1, False), col(2, False), col(0, False), st(False),
                  col(0, True), col(1, True), col(3, True), col(0, True), st(True), lb_spec, lb_spec],
        out_specs=[col(0, False), col(0, False), col(0, False), lb_spec,
                   col(0, True), col(0, True), col(0, True), lb_spec],
        out_shape=[out, out, out, dlb, out, out, out, dlb],
        scratch_shapes=[pltpu.VMEM((2, HEAD, hw), F32), pltpu.VMEM((2, 1, hw), F32), wide, wide, wide,
                        pltpu.VMEM((1, hw), F32)],
        compiler_params=_params(("arbitrary",), 16 * _nbytes((HEAD, hw), F32)),
    )(p, p, p, do, st_f, p, p, p, do, st_b, lbp_f, lbp_b)


def _hgrn_out_fwd(name, o_f, o_b, p, gain, g_group):
    t, hw = o_f.shape
    nh = hw // HEAD
    tm = _tile(t, 512, 8)

    def body(of_ref, ob_ref, g_ref, gain_ref, y_ref):
        o = of_ref[...] + ob_ref[...]
        g = g_ref[...]
        y_ref[...] = (o * _rms(o) * gain_ref[...] * (g * _sigmoid(g))).astype(BF16)

    blk = pl.BlockSpec((tm, HEAD), lambda i, h: (i, h))
    return _pallas(
        body, name=name, grid=(t // tm, nh),
        in_specs=[blk, blk, pl.BlockSpec((tm, HEAD), lambda i, h: (i, g_group * nh + h)),
                  pl.BlockSpec((1, HEAD), lambda i, h: (0, h))],
        out_specs=blk, out_shape=jax.ShapeDtypeStruct((t, hw), BF16),
        compiler_params=_params(("parallel", "parallel"), 1 << 20),
    )(o_f, o_b, p, gain)


def _hgrn_out_bwd(name, dy, o_f, o_b, p, gain, g_group, after=None):
    t, hw = o_f.shape
    nh = hw // HEAD
    tm = _tile(t, 512, 8)

    def body(dy_ref, of_ref, ob_ref, g_ref, gain_ref, do_ref, dg_ref, dgain_ref):
        i = pl.program_id(1)
        o = of_ref[...] + ob_ref[...]
        g = g_ref[...]
        gain_v = gain_ref[...]
        sig = _sigmoid(g)
        dyv = dy_ref[...]
        do, dgain = _norm_bwd(dyv * (g * sig), o, gain_v)
        do_ref[...] = do
        dg_ref[...] = dyv * (o * _rms(o) * gain_v) * sig * (1.0 + g * (1.0 - sig))
        _accumulate(dgain_ref, dgain, i == 0)

    blk = pl.BlockSpec((tm, HEAD), lambda h, i: (i, h))
    vec = pl.BlockSpec((1, HEAD), lambda h, i: (0, h))
    out = jax.ShapeDtypeStruct((t, hw), F32)
    body, ins, in_specs = _ordered(
        body, [dy, o_f, o_b, p, gain],
        [blk, blk, blk, pl.BlockSpec((tm, HEAD), lambda h, i: (i, g_group * nh + h)), vec], after)
    return _pallas(
        body, name=name, grid=(nh, t // tm), in_specs=in_specs,
        out_specs=[blk, blk, vec], out_shape=[out, out, jax.ShapeDtypeStruct((1, hw), F32)],
        compiler_params=_params(("parallel", "arbitrary"), 1 << 20),
    )(*ins)


def _t5_bucket_ids():
    c = np.arange(WINDOW)[:, None]
    s = np.arange(SPAN)[None, :]
    rel = s - WINDOW - c
    nb = REL_BUCKETS // 2
    max_exact = nb // 2
    bucket = (rel > 0).astype(np.int32) * nb
    n = np.abs(rel)
    large = max_exact + (np.log(np.maximum(n, 1) / max_exact) / np.log(REL_MAX_DIST / max_exact)
                         * (nb - max_exact)).astype(np.int32)
    large = np.minimum(large, nb - 1)
    ids = bucket + np.where(n < max_exact, n, large).astype(np.int32)
    return jnp.asarray(ids.reshape(1, WINDOW * SPAN), jnp.int32)


def _bias_onehot(ids_ref):
    n = ids_ref.shape[1]
    return (lax.broadcasted_iota(jnp.int32, (REL_BUCKETS, n), 0) == ids_ref[...]).astype(BF16)


def _bias_gather(name, table_t, ids):
    nh = table_t.shape[0]

    def body(t_ref, ids_ref, o_ref):
        o_ref[...] = _dot_exact(t_ref[...], _bias_onehot(ids_ref), split="a")

    return _pallas(
        body, name=name, out_shape=jax.ShapeDtypeStruct((nh, ids.shape[1]), F32),
        compiler_params=pltpu.CompilerParams(vmem_limit_bytes=32 << 20),
    )(table_t, ids)


def _bias_scatter(name, dbias, ids):
    nh = dbias.shape[0]

    def body(d_ref, ids_ref, o_ref):
        o_ref[...] = _dot_exact(d_ref[...], _bias_onehot(ids_ref), 1, 1, split="a")

    return _pallas(
        body, name=name, out_shape=jax.ShapeDtypeStruct((nh, REL_BUCKETS), F32),
        compiler_params=pltpu.CompilerParams(vmem_limit_bytes=32 << 20),
    )(dbias, ids)


def _attn_valid(i, t):
    c = lax.broadcasted_iota(jnp.int32, (WINDOW, SPAN), 0)
    s = lax.broadcasted_iota(jnp.int32, (WINDOW, SPAN), 1)
    rel = s - WINDOW - c
    pos = i * WINDOW - WINDOW + s
    return (jnp.abs(rel) <= WINDOW) & (pos >= 0) & (pos < t)


def _attn_probs(qs, khs, b_ref, s_ref, valid):
    heads = range(len(qs))
    sinks = [s_ref[0:1, h:h + 1] for h in heads]
    s = [_dot(qs[h], khs[h], 1, 1) / math.sqrt(HEAD) for h in heads]
    s = [jnp.where(valid, s[h] + b_ref[h], NEG_INF) for h in heads]
    m = [jnp.maximum(jnp.max(s[h], axis=-1, keepdims=True), sinks[h]) for h in heads]
    e = [jnp.exp(s[h] - m[h]) for h in heads]
    es = [jnp.exp(sinks[h] - m[h]) for h in heads]
    inv = [1.0 / (jnp.sum(e[h], axis=-1, keepdims=True) + es[h]) for h in heads]
    return [e[h] * inv[h] for h in heads], [es[h] * inv[h] for h in heads]


def _attn_fwd(name, p, k_pad, v_pad, bias, sink, q_group_blk):
    t = p.shape[0]
    nh = bias.shape[0]
    aw = nh * HEAD
    grp = nh // KV_HEADS
    nb = t // WINDOW

    def body(q_ref, k_ref, v_ref, b_ref, s_ref, y_ref):
        i = pl.program_id(0)
        valid = _attn_valid(i, t)
        start = pl.multiple_of(i * WINDOW, WINDOW)
        ks = k_ref[pl.ds(start, SPAN), :]
        vs = v_ref[pl.ds(start, SPAN), :]
        heads = range(nh)
        col = lambda h: slice(h * HEAD, (h + 1) * HEAD)
        qs = [q_ref[:, col(h)].astype(BF16) for h in heads]
        pr, _ = _attn_probs(qs, [ks[:, col(h // grp)] for h in heads], b_ref, s_ref, valid)
        out = [_dot(pr[h].astype(BF16), vs[:, col(h // grp)]) for h in heads]
        for h in heads:
            y_ref[:, col(h)] = out[h].astype(BF16)

    full = lambda a: pl.BlockSpec(a.shape, lambda i: (0,) * a.ndim)
    return _pallas(
        body, name=name, grid=(nb,),
        in_specs=[pl.BlockSpec((WINDOW, aw), lambda i: (i, q_group_blk)), full(k_pad), full(v_pad), full(bias),
                  full(sink)],
        out_specs=pl.BlockSpec((WINDOW, aw), lambda i: (i, 0)),
        out_shape=jax.ShapeDtypeStruct((t, aw), BF16),
        compiler_params=_params(("parallel",), _nbytes(k_pad.shape, BF16) * 2 + _nbytes(bias.shape, F32)),
    )(p, k_pad, v_pad, bias, sink)


def _attn_bwd(name, p, k_pad, v_pad, bias, sink, dy, q_group_blk, dy_blk, after=None):
    t = p.shape[0]
    nh = bias.shape[0]
    aw = nh * HEAD
    grp = nh // KV_HEADS
    nb = t // WINDOW
    kvw = k_pad.shape[1]

    def body(q_ref, k_ref, v_ref, b_ref, s_ref, dy_ref, dq_ref, dk_ref, dv_ref, db_ref, ds_ref):
        i = pl.program_id(0)

        @pl.when(i == 0)
        def _():
            dk_ref[...] = jnp.zeros_like(dk_ref)
            dv_ref[...] = jnp.zeros_like(dv_ref)
            db_ref[...] = jnp.zeros_like(db_ref)
            ds_ref[...] = jnp.zeros_like(ds_ref)

        valid = _attn_valid(i, t)
        start = pl.multiple_of(i * WINDOW, WINDOW)
        ks = k_ref[pl.ds(start, SPAN), :]
        vs = v_ref[pl.ds(start, SPAN), :]
        inv_sqrt = 1.0 / math.sqrt(HEAD)
        heads = range(nh)
        col = lambda h: slice(h * HEAD, (h + 1) * HEAD)
        qs = [q_ref[:, col(h)].astype(BF16) for h in heads]
        khs = [ks[:, col(h // grp)] for h in heads]
        pr, ps = _attn_probs(qs, khs, b_ref, s_ref, valid)
        dos = [dy_ref[:, col(h)].astype(BF16) for h in heads]
        dp = [_dot(dos[h], vs[:, col(h // grp)], 1, 1) for h in heads]
        delta = [jnp.sum(pr[h] * dp[h], axis=-1, keepdims=True) for h in heads]
        dsc = [pr[h] * (dp[h] - delta[h]) for h in heads]
        dsr = [(dsc[h] * inv_sqrt).astype(BF16) for h in heads]
        dq = [_dot(dsr[h], khs[h]) for h in heads]
        dk = [_dot(dsr[h], qs[h], 0, 0) for h in heads]
        dv = [_dot(pr[h].astype(BF16), dos[h], 0, 0) for h in heads]
        for h in heads:
            db_ref[h] += dsc[h]
            ds_ref[h:h + 1, :] += jnp.broadcast_to(jnp.sum(-ps[h] * delta[h], axis=0, keepdims=True), (1, 128))
            dq_ref[:, col(h)] = dq[h]
        for kv in range(KV_HEADS):
            group = range(kv * grp, (kv + 1) * grp)
            dk_ref[pl.ds(start, SPAN), col(kv)] += sum(dk[h] for h in group)
            dv_ref[pl.ds(start, SPAN), col(kv)] += sum(dv[h] for h in group)

    full = lambda a: pl.BlockSpec(a.shape, lambda i: (0,) * a.ndim)
    whole = lambda shape: pl.BlockSpec(shape, lambda i: (0,) * len(shape))
    pad_shape = (t + 2 * WINDOW, kvw)
    body, ins, in_specs = _ordered(
        body, [p, k_pad, v_pad, bias, sink, dy],
        [pl.BlockSpec((WINDOW, aw), lambda i: (i, q_group_blk)), full(k_pad), full(v_pad), full(bias), full(sink),
         pl.BlockSpec((WINDOW, aw), lambda i: (i, dy_blk))], after)
    return _pallas(
        body, name=name, grid=(nb,), in_specs=in_specs,
        out_specs=[pl.BlockSpec((WINDOW, aw), lambda i: (i, 0)), whole(pad_shape), whole(pad_shape),
                   whole(bias.shape), whole((nh, 128))],
        out_shape=[jax.ShapeDtypeStruct((t, aw), F32), jax.ShapeDtypeStruct(pad_shape, F32),
                   jax.ShapeDtypeStruct(pad_shape, F32), jax.ShapeDtypeStruct(bias.shape, F32),
                   jax.ShapeDtypeStruct((nh, 128), F32)],
        compiler_params=_params(("arbitrary",), 3 * _nbytes(pad_shape, F32) + 2 * _nbytes(bias.shape, F32)),
    )(*ins)


def _pad_kv(name, p, kv_blk, kvw):
    t = p.shape[0]
    nb = t // WINDOW

    def body(x_ref, o_ref):
        i = pl.program_id(0)
        inside = jnp.logical_and(i >= 1, i <= nb)
        o_ref[...] = jnp.where(inside, x_ref[...], 0.0).astype(BF16)

    return _pallas(
        body, name=name, grid=(nb + 2,),
        in_specs=[pl.BlockSpec((WINDOW, kvw), lambda i: (jnp.clip(i - 1, 0, nb - 1), kv_blk))],
        out_specs=pl.BlockSpec((WINDOW, kvw), lambda i: (i, 0)),
        out_shape=jax.ShapeDtypeStruct((t + 2 * WINDOW, kvw), BF16),
        compiler_params=_params(("parallel",), 1 << 20),
    )(p)


def _mix_dproj(name, pieces, kv_pads, t, after=None):
    hw = pieces[0][0].shape[1]
    kvw = kv_pads[0].shape[1]
    widths = [hw] * len(pieces) + [kvw] * len(kv_pads)
    total = sum(widths)
    tm = WINDOW
    flat = [a for pc in pieces for a in pc]

    def body(*refs):
        o_ref = refs[-1]
        pos, off = 0, 0
        for pc in pieces:
            val = refs[pos][...]
            for extra in range(1, len(pc)):
                val = val + refs[pos + extra][...]
            o_ref[:, off:off + hw] = val.astype(BF16)
            pos += len(pc)
            off += hw
        for _ in kv_pads:
            o_ref[:, off:off + kvw] = refs[pos][...].astype(BF16)
            pos += 1
            off += kvw

    in_specs = [pl.BlockSpec((tm, hw), lambda i: (i, 0)) for _ in flat]
    in_specs += [pl.BlockSpec((tm, kvw), lambda i: (i + 1, 0)) for _ in kv_pads]
    body, ins, in_specs = _ordered(body, [*flat, *kv_pads], in_specs, after)
    return _pallas(
        body, name=name, grid=(t // tm,), in_specs=in_specs,
        out_specs=pl.BlockSpec((tm, total), lambda i: (i, 0)),
        out_shape=jax.ShapeDtypeStruct((t, total), BF16),
        compiler_params=_params(("parallel",), 3 * _nbytes((tm, total), F32)),
    )(*ins)


def _concat_cols(name, a, b):
    t, wa = a.shape
    wb = b.shape[1]
    tm = _tile(t, 512, 16)

    def body(a_ref, b_ref, o_ref):
        o_ref[:, :wa] = a_ref[...]
        o_ref[:, wa:] = b_ref[...]

    return _pallas(
        body, name=name, grid=(t // tm,),
        in_specs=[pl.BlockSpec((tm, wa), lambda i: (i, 0)), pl.BlockSpec((tm, wb), lambda i: (i, 0))],
        out_specs=pl.BlockSpec((tm, wa + wb), lambda i: (i, 0)),
        out_shape=jax.ShapeDtypeStruct((t, wa + wb), a.dtype),
        compiler_params=_params(("parallel",), 2 * _nbytes((tm, wa + wb), a.dtype)),
    )(a, b)


def _cast_into_full(name, w, geom, idx, after=None):
    r, c = w.shape
    tr = _tile(r, 256, 16)
    nr = r // tr
    if geom.col:
        place = lambda i, iref: (i, iref[0])
    else:
        place = lambda i, iref: (iref[0] * nr + i, 0)

    def body(i_ref, w_ref, *rest):
        rest[-1][...] = w_ref[...].astype(BF16)

    in_specs = [pl.BlockSpec((tr, c), lambda i, iref: (i, 0))]
    ins = [w]
    if after is not None:
        in_specs.append(pl.BlockSpec(memory_space=pl.ANY))
        ins.append(after)
    return _pallas(
        body, name=name,
        grid_spec=pltpu.PrefetchScalarGridSpec(
            num_scalar_prefetch=1, grid=(nr,), in_specs=in_specs, out_specs=pl.BlockSpec((tr, c), place)),
        out_shape=pltpu.HBM(geom.full_shape, BF16),
        compiler_params=_params(("parallel",), 2 * _nbytes((tr, c), F32)),
    )(idx, *ins)


def _adamw(name, w, g, m, v):
    r, c = w.shape
    tr = _tile(r, 128, 8)
    bc1 = 1.0 - ADAM_B1 ** ADAM_STEP
    bc2 = 1.0 - ADAM_B2 ** ADAM_STEP

    def body(w_ref, g_ref, m_ref, v_ref, go_ref, d_ref, nm_ref, nv_ref):
        gv = g_ref[...]
        go_ref[...] = gv
        nm = ADAM_B1 * m_ref[...] + (1.0 - ADAM_B1) * gv
        nv = ADAM_B2 * v_ref[...] + (1.0 - ADAM_B2) * (gv * gv)
        nm_ref[...] = nm
        nv_ref[...] = nv
        d_ref[...] = -ADAM_LR * ((nm / bc1) / (jnp.sqrt(nv / bc2) + ADAM_EPS) + ADAM_WD * w_ref[...])

    blk = pl.BlockSpec((tr, c), lambda i: (i, 0))
    out = jax.ShapeDtypeStruct((r, c), F32)
    return _pallas(
        body, name=name, grid=(r // tr,), in_specs=[blk] * 4, out_specs=[blk] * 4, out_shape=[out] * 4,
        compiler_params=_params(("parallel",), 8 * _nbytes((tr, c), F32)),
    )(w, g, m, v)


def _mesh_pos():
    return lax.axis_index("x"), lax.axis_index("y"), lax.axis_index("c")


def _other_chips(x, y):
    return [(1 - x, y), (x, 1 - y), (1 - x, 1 - y)]


class _Big:
    def __init__(self, shard_shape, col_sharded):
        self.col = col_sharded
        r, c = shard_shape
        self.shard_shape = (r, c)
        self.full_shape = (r, N_CHIPS * c) if col_sharded else (N_CHIPS * r, c)
        self.half_shape = (r // 2, N_CHIPS * c) if col_sharded else (N_CHIPS * r, c // 2)
        self.shard_half_shape = (r // 2, c) if col_sharded else (r, c // 2)

    def region(self, ref, s, half=None):
        r, c = self.shard_shape
        if self.col:
            rows = slice(None) if half is None else pl.ds(half * (r // 2), r // 2)
            return ref.at[rows, pl.ds(s * c, c)]
        cols = slice(None) if half is None else pl.ds(half * (c // 2), c // 2)
        return ref.at[pl.ds(s * r, r), cols]

    def n_halves(self, ref, half, n):
        r, c = self.shard_shape
        if self.col:
            return ref.at[pl.ds(half * (r // 2), r // 2), pl.ds(0, n * c)]
        return ref.at[pl.ds(0, n * r), pl.ds(half * (c // 2), c // 2)]

    def three_halves(self, ref, half):
        return self.n_halves(ref, half, 3)

    def sub_half(self, ref, s, half, j):
        r, c = self.shard_shape
        if self.col:
            return ref.at[pl.ds(half * (r // 2) + j * (r // 4), r // 4), pl.ds(s * c, c)]
        return ref.at[pl.ds(s * r + j * (r // 2), r // 2), pl.ds(half * (c // 2), c // 2)]

    def half_of_full(self, ref, half):
        r, c = self.full_shape
        if self.col:
            return ref.at[pl.ds(half * (r // 2), r // 2), :]
        return ref.at[:, pl.ds(half * (c // 2), c // 2)]

    def half_of_shard(self, ref, half):
        r, c = self.shard_shape
        if self.col:
            return ref.at[pl.ds(half * (r // 2), r // 2), :]
        return ref.at[:, pl.ds(half * (c // 2), c // 2)]

    def shard_of_half(self, ref, s):
        r, c = self.shard_shape
        if self.col:
            return ref.at[:, pl.ds(s * c, c)]
        return ref.at[pl.ds(s * r, r), :]


HBM =pl.BlockSpec(memory_space=pltpu.HBM)
SEM = pl.BlockSpec(memory_space=pltpu.SEMAPHORE)
SPLIT_COPY = pltpu.CompilerParams(has_side_effects=pltpu.SideEffectType.DATAFLOW_SIDE_EFFECTING)


def _in_hbm(a):
    return pltpu.with_memory_space_constraint(a, pltpu.HBM)


def _gather_start(name, fulls, geoms, after):
    nw = len(fulls)

    def body(*refs):
        dst = refs[nw + 1:2 * nw + 1]
        sems = refs[2 * nw + 1:-1]
        x, y, c = _mesh_pos()
        mine = 2 * x + y
        for w in range(nw):
            own_half = geoms[w].region(dst[w], mine, c)
            for chip in _other_chips(x, y):
                pltpu.make_async_remote_copy(src_ref=own_half, dst_ref=own_half, send_sem=sems[2 * w],
                                             recv_sem=sems[2 * w + 1], device_id=(*chip, c),
                                             device_id_type=MESH).start()
        refs[-1][...] = jnp.zeros_like(refs[-1])

    out = _pallas(
        body, name=name, in_specs=[HBM] * nw + [pl.BlockSpec(memory_space=pl.ANY)],
        out_specs=[HBM] * nw + [SEM] * (2 * nw) + [pl.BlockSpec(memory_space=pltpu.VMEM)],
        out_shape=[pltpu.HBM(g.full_shape, BF16) for g in geoms] + [pltpu.SemaphoreType.DMA(())] * (2 * nw)
        + [jax.ShapeDtypeStruct((8, 128), F32)],
        input_output_aliases={w: w for w in range(nw)}, compiler_params=SPLIT_COPY,
    )(*[_in_hbm(a) for a in fulls], after)
    return list(out[:nw]), [(out[nw + 2 * w], out[nw + 2 * w + 1]) for w in range(nw)], out[-1]


def _gather_first_direct(full, geom):
    def start(refs, _, new):
        x, y, c = _mesh_pos()
        own = geom.region(refs[0], 2 * x + y, c)
        for chip in ((1 - x, y), (x, 1 - y)):
            _remote(own, own, new, (*chip, c)).start()

    return _split_copy_call("gather_first_direct", [full], start, new_sems=2)


def _gather_first_relay(full, geom, sems, after):
    def relay(refs, got, new):
        x, y, c = _mesh_pos()
        w = refs[0]
        qx, qy = 2 * (1 - x) + y, 2 * x + (1 - y)
        two = geom.n_halves(w, c, 2)
        _remote(two, two, got, (x, y, 1 - c)).wait_recv()
        from_x = geom.sub_half(w, qx, c, 0)
        from_y = geom.sub_half(w, qy, c, 1)
        _remote(from_x, from_x, new[:2], (x, 1 - y, c)).start()
        _remote(from_y, from_y, new[:2], (1 - x, y, c)).start()
        for q in (qx, qy):
            landed = geom.region(w, q, c)
            _remote(landed, landed, new[2:], (x, y, 1 - c)).start()
        _remote(two, two, got, (x, y, 1 - c)).wait_send()

    arrays, new, token = _split_copy_call("gather_first_relay", [full], relay, sems=sems, after=after, new_sems=4)
    return arrays, new[:2], new[2:], token


def _gather_forward(name, full, geom, sems, after, arrivals=3, only_diagonal=False):
    def body(w_in, send_sem, recv_sem, after_ref, w_ref, fwd_send, fwd_recv):
        x, y, c = _mesh_pos()
        sibling = (x, y, 1 - c)
        landed_all = geom.n_halves(w_ref, c, arrivals)
        _remote(landed_all, landed_all, (send_sem, recv_sem), sibling).wait_recv()
        for chip in _other_chips(x, y)[2 if only_diagonal else 0:]:
            landed = geom.region(w_ref, 2 * chip[0] + chip[1], c)
            pltpu.make_async_remote_copy(src_ref=landed, dst_ref=landed, send_sem=fwd_send, recv_sem=fwd_recv,
                                         device_id=sibling, device_id_type=MESH).start()
        _remote(landed_all, landed_all, (send_sem, recv_sem), sibling).wait_send()

    sem = pltpu.SemaphoreType.DMA(())
    out = _pallas(
        body, name=name, in_specs=[HBM, SEM, SEM, pl.BlockSpec(memory_space=pl.ANY)], out_specs=[HBM, SEM, SEM],
        out_shape=[pltpu.HBM(geom.full_shape, BF16), sem, sem],
        input_output_aliases={0: 0}, compiler_params=SPLIT_COPY,
    )(full, sems[0], sems[1], after)
    return out[0], (out[1], out[2])


def _gather_end(name, full, geom, sems, after, halves=3):
    def body(w_in, fwd_send, fwd_recv, after_ref, w_ref):
        x, y, c = _mesh_pos()
        sibling = (x, y, 1 - c)
        theirs, ours = geom.n_halves(w_ref, 1 - c, halves), geom.n_halves(w_ref, c, halves)
        _remote(theirs, theirs, (fwd_send, fwd_recv), sibling).wait_recv()
        _remote(ours, ours, (fwd_send, fwd_recv), sibling).wait_send()

    return _pallas(
        body, name=name, in_specs=[HBM, SEM, SEM, pl.BlockSpec(memory_space=pl.ANY)], out_specs=HBM,
        out_shape=pltpu.HBM(geom.full_shape, BF16),
        input_output_aliases={0: 0}, compiler_params=SPLIT_COPY,
    )(full, sems[0], sems[1], after)


def _split_copy_call(name, arrays, fn, sems=(), after=None, new_sems=0):
    n, ns = len(arrays), len(sems)
    n_in = n + ns + (after is not None)

    def body(*refs):
        fn(refs[n_in:n_in + n], refs[n:n + ns], refs[n_in + n:-1])
        refs[-1][...] = jnp.zeros_like(refs[-1])

    ins = list(arrays) if ns else [_in_hbm(a) for a in arrays]
    ins += list(sems) + ([after] if after is not None else [])
    in_specs = [HBM] * n + [SEM] * ns + ([pl.BlockSpec(memory_space=pl.ANY)] if after is not None else [])
    out = _pallas(
        body, name=name, in_specs=in_specs,
        out_specs=[HBM] * n + [SEM] * new_sems + [pl.BlockSpec(memory_space=pltpu.VMEM)],
        out_shape=[pltpu.HBM(a.shape, a.dtype) for a in arrays] + [pltpu.SemaphoreType.DMA(())] * new_sems
        + [jax.ShapeDtypeStruct((8, 128), F32)],
        input_output_aliases={i: i for i in range(n)}, compiler_params=SPLIT_COPY,
    )(*ins)
    return list(out[:n]), tuple(out[n:-1]), out[-1]


def _remote(src, dst, sems, to):
    return pltpu.make_async_remote_copy(src_ref=src, dst_ref=dst, send_sem=sems[0], recv_sem=sems[1],
                                        device_id=to, device_id_type=MESH)


class _GradReduce:
    def __init__(self, name, geom, idx, c_idx):
        self.name, self.geom, self.idx, self.c_idx = name, geom, idx, c_idx

    def pair_start(self, theirs):
        g = self.geom

        def start(refs, _, new):
            x, y, c = _mesh_pos()
            _remote(refs[0], refs[1], new, (x, y, 1 - c)).start()

        self.arrays, self.sems, token = _split_copy_call(
            f"pair_start_{self.name}", [theirs, lax.empty(g.half_shape, BF16)], start, new_sems=2)
        return token

    def pair_wait(self, after):
        def wait(refs, sems, _):
            x, y, c = _mesh_pos()
            copy = _remote(refs[0], refs[1], sems, (x, y, 1 - c))
            copy.wait_send()
            copy.wait_recv()

        (_, landed), _, _ = _split_copy_call(f"pair_wait_{self.name}", self.arrays, wait, self.sems, after)
        return landed

    def chip_start(self, half):
        g = self.geom

        def start(refs, _, new):
            x, y, c = _mesh_pos()
            for k, chip in enumerate(_other_chips(x, y)):
                _remote(g.shard_of_half(refs[0], 2 * chip[0] + chip[1]), refs[1].at[k], new, (*chip, c)).start()

        self.arrays, self.sems, token = _split_copy_call(
            f"chip_start_{self.name}", [half, lax.empty((3,) + g.shard_half_shape, BF16)], start, new_sems=2)
        return token

    def chip_finish(self, after):
        g = self.geom

        def wait(refs, sems, _):
            x, y, c = _mesh_pos()
            three = _remote(refs[1], refs[1], sems, (x, y, 1 - c))
            three.wait_send()
            three.wait_recv()

        (half, landed), _, _ = _split_copy_call(f"chip_wait_{self.name}", self.arrays, wait, self.sems, after)
        quarter = _chip_add(f"chip_add_{self.name}", half, landed, g, self.idx)

        def start(refs, _, new):
            x, y, c = _mesh_pos()
            own = g.half_of_shard(refs[0], c)
            _remote(own, own, new, (x, y, 1 - c)).start()

        self.arrays, self.sems, token = _split_copy_call(f"share_start_{self.name}", [quarter], start, new_sems=2)
        return token

    def finish(self, after):
        g = self.geom

        def wait(refs, sems, _):
            x, y, c = _mesh_pos()
            own, theirs = g.half_of_shard(refs[0], c), g.half_of_shard(refs[0], 1 - c)
            _remote(own, own, sems, (x, y, 1 - c)).wait_send()
            _remote(theirs, theirs, sems, (x, y, 1 - c)).wait_recv()

        (quarter,), _, _ = _split_copy_call(f"share_wait_{self.name}", self.arrays, wait, self.sems, after)
        return quarter


def _dw_half(name, x, dy, geom, c_idx, own, addend=None, after=None):
    stacked = dy.ndim == 3
    t, m = x.shape
    n = 2 * dy.shape[2] if stacked else dy.shape[1]
    hm, hn = (m // 2, n) if geom.col else (m, n // 2)
    tm, tn, tk = _mm_tiles(hm, hn, t, BF16, n_unit=(n // 2 if stacked else None))
    if tk != t:
        tm, tn = _tile(hm, 512, 128), _tile(hn // (2 if stacked else 1), 512, 128)
    gi, gj = hm // tm, hn // tn
    nf = (n // 2) // tn

    def sel(cref):
        return cref[0] if own else 1 - cref[0]

    a_map = (lambda i, j, cref: (0, sel(cref) * gi + i)) if geom.col else (lambda i, j, cref: (0, i))
    if stacked:
        b_blk, b_map = (None, t, tn), (lambda i, j, cref: (j // nf, 0, j % nf))
    elif geom.col:
        b_blk, b_map = (t, tn), (lambda i, j, cref: (0, j))
    else:
        b_blk, b_map = (t, tn), (lambda i, j, cref: (0, sel(cref) * gj + j))
    out_blk = pl.BlockSpec((tm, tn), lambda i, j, cref: (i, j))
    ins, in_specs = [x, dy], [pl.BlockSpec((t, tm), a_map), pl.BlockSpec(b_blk, b_map)]
    if addend is not None:
        ins.append(addend)
        in_specs.append(out_blk)
    if after is not None:
        ins.append(after)
        in_specs.append(pl.BlockSpec(memory_space=pl.ANY))

    def body(c_ref, *refs):
        acc = _dot(refs[0][...], refs[1][...], 0, 0)
        if addend is not None:
            acc = acc + refs[2][...].astype(F32)
        refs[len(ins)][...] = acc.astype(BF16)

    return _pallas(
        body, name=name,
        grid_spec=pltpu.PrefetchScalarGridSpec(num_scalar_prefetch=1, grid=(gi, gj), in_specs=in_specs,
                                               out_specs=out_blk),
        out_shape=jax.ShapeDtypeStruct((hm, hn), BF16),
        compiler_params=_params(("parallel", "parallel"),
                                _nbytes((t, tm), BF16) + _nbytes((t, tn), BF16) + 3 * _nbytes((tm, tn), F32)),
    )(c_idx, *ins)


def _chip_add(name, half, recv, geom, idx):
    r, c = geom.shard_half_shape
    tr, tc = _tile(r, 512, 16), _tile(c, 2048, 128)
    nr, ncol = r // tr, c // tc
    if geom.col:
        mine = lambda i, j, iref: (i, iref[0] * ncol + j)
        place = lambda i, j, iref: (iref[1] * nr + i, j)
    else:
        mine = lambda i, j, iref: (iref[0] * nr + i, j)
        place = lambda i, j, iref: (i, iref[1] * ncol + j)

    def body(i_ref, h_ref, r_ref, o_ref):
        acc = h_ref[...].astype(F32)
        for k in range(3):
            acc = acc + r_ref[k].astype(F32)
        o_ref[...] = acc

    return _pallas(
        body, name=name,
        grid_spec=pltpu.PrefetchScalarGridSpec(
            num_scalar_prefetch=1, grid=(nr, ncol),
            in_specs=[pl.BlockSpec((tr, tc), mine), pl.BlockSpec((3, tr, tc), lambda i, j, iref: (0, i, j))],
            out_specs=pl.BlockSpec((tr, tc), place)),
        out_shape=jax.ShapeDtypeStruct(geom.shard_shape, F32),
        compiler_params=_params(("parallel", "parallel"), 4 * _nbytes((tr, tc), F32)),
    )(idx, half, recv)


def _all_reduce_small(pack, after=None):
    r, d = pack.shape

    def body(p_ref, o_ref, slots, send_sems, recv_sems):
        x, y, c = _mesh_pos()
        me = 4 * x + 2 * y + c
        slots[me] = p_ref[...]
        copies = []
        for k in range(1, N_DEV):
            px, py, pc = x ^ ((k >> 2) & 1), y ^ ((k >> 1) & 1), c ^ (k & 1)
            copies.append(pltpu.make_async_remote_copy(
                src_ref=p_ref, dst_ref=slots.at[me], send_sem=send_sems.at[k - 1], recv_sem=recv_sems.at[k - 1],
                device_id=(px, py, pc), device_id_type=MESH))
        for cp in copies:
            cp.start()
        for k in range(1, N_DEV):
            peer = 4 * (x ^ ((k >> 2) & 1)) + 2 * (y ^ ((k >> 1) & 1)) + (c ^ (k & 1))
            pltpu.make_async_remote_copy(
                src_ref=p_ref, dst_ref=slots.at[peer], send_sem=send_sems.at[k - 1], recv_sem=recv_sems.at[k - 1],
                device_id=(x, y, c), device_id_type=MESH).wait_recv()
        for cp in copies:
            cp.wait_send()
        acc = slots[0]
        for k in range(1, N_DEV):
            acc = acc + slots[k]
        o_ref[...] = acc

    vm = pl.BlockSpec(memory_space=pltpu.VMEM)
    body, ins, in_specs = _ordered(body, [pack], [vm], after)
    return _pallas(
        body, name="all_reduce_small", in_specs=in_specs, out_specs=vm,
        out_shape=jax.ShapeDtypeStruct((r, d), F32),
        scratch_shapes=[pltpu.VMEM((N_DEV, r, d), F32), pltpu.SemaphoreType.DMA((N_DEV - 1,)),
                        pltpu.SemaphoreType.DMA((N_DEV - 1,))],
    )(*ins)


def _pack_rows(rows, d):
    out = []
    for a in rows:
        flat = a.reshape(-1)
        n = -(-flat.shape[0] // d) * d
        out.append(jnp.pad(flat, (0, n - flat.shape[0])).reshape(-1, d))
    packed = jnp.concatenate(out, axis=0)
    return jnp.pad(packed, ((0, 16 - packed.shape[0]), (0, 0)))


def _unpack_rows(packed, shapes, d):
    out, row = [], 0
    for shp in shapes:
        n = int(np.prod(shp))
        nrows = -(-n // d)
        out.append(packed[row:row + nrows].reshape(-1)[:n].reshape(shp))
        row += nrows
    return out


def kernel(x, pre_norm_ffn1, post_norm_ffn1, w_ffn1_gate_up, w_ffn1_down, pre_norm_mix, post_norm_mix, w_mix_in, hgrn_lower_bounds_fwd, hgrn_lower_bounds_bwd, hgrn_out_norm, attn_sink, w_mix_out, pre_norm_ffn2, post_norm_ffn2, w_ffn2_gate_up, w_ffn2_down, rel_bias_table, loss_target, m_pre_norm_ffn1, m_post_norm_ffn1, m_w_ffn1_gate_up, m_w_ffn1_down, m_pre_norm_mix, m_post_norm_mix, m_w_mix_in, m_hgrn_lower_bounds_fwd, m_hgrn_lower_bounds_bwd, m_hgrn_out_norm, m_attn_sink, m_w_mix_out, m_pre_norm_ffn2, m_post_norm_ffn2, m_w_ffn2_gate_up, m_w_ffn2_down, m_rel_bias_table, v_pre_norm_ffn1, v_post_norm_ffn1, v_w_ffn1_gate_up, v_w_ffn1_down, v_pre_norm_mix, v_post_norm_mix, v_w_mix_in, v_hgrn_lower_bounds_fwd, v_hgrn_lower_bounds_bwd, v_hgrn_out_norm, v_attn_sink, v_w_mix_out, v_pre_norm_ffn2, v_post_norm_ffn2, v_w_ffn2_gate_up, v_w_ffn2_down, v_rel_bias_table):
    t, d = x.shape[1], x.shape[2]
    hw = hgrn_out_norm.shape[1]
    aw = d - hw
    nah = aw // HEAD
    kvw = KV_HEADS * HEAD
    x0 = x[0]
    target = loss_target[0]

    big_names = ["w_ffn1_gate_up", "w_ffn1_down", "w_mix_in", "w_mix_out", "w_ffn2_gate_up", "w_ffn2_down"]
    big_w = [w_ffn1_gate_up[0], w_ffn1_down[0], w_mix_in[0], w_mix_out[0], w_ffn2_gate_up[0], w_ffn2_down[0]]
    big_m = [m_w_ffn1_gate_up[0], m_w_ffn1_down[0], m_w_mix_in[0], m_w_mix_out[0], m_w_ffn2_gate_up[0],
             m_w_ffn2_down[0]]
    big_v = [v_w_ffn1_gate_up[0], v_w_ffn1_down[0], v_w_mix_in[0], v_w_mix_out[0], v_w_ffn2_gate_up[0],
             v_w_ffn2_down[0]]
    col_sharded = [True, False, True, False, True, False]
    geoms = [_Big(w.shape, cs) for w, cs in zip(big_w, col_sharded)]

    cx, cy, cc = _mesh_pos()
    idx = jnp.stack([2 * cx + cy, cc]).astype(jnp.int32)
    c_idx = jnp.reshape(cc, (1,)).astype(jnp.int32)
    first = _cast_into_full(f"cast_{big_names[0]}", big_w[0], geoms[0], idx)
    (first,), direct_sems, tok = _gather_first_direct(first, geoms[0])
    rest = []
    for n, w, gm in zip(big_names[1:], big_w[1:], geoms[1:]):
        tok = _cast_into_full(f"cast_{n}", w, gm, idx, after=tok)
        rest.append(tok)
    (first,), relay_sems, pair_sems, tok = _gather_first_relay(first, geoms[0], direct_sems, after=tok)
    started_rest, sems_rest, rest_started = _gather_start("gather_start_rest", rest, geoms[1:], after=tok)
    started, gather_sems = [first] + started_rest, [relay_sems] + sems_rest

    def forward_weight(w, after):
        return _gather_forward(f"gather_forward_{big_names[w]}", started[w], geoms[w], gather_sems[w], after)

    def whole_weight(w, forwarded, after):
        return _gather_end(f"gather_end_{big_names[w]}", forwarded[0], geoms[w], forwarded[1], after)

    h1 = _norm_fwd("ffn1_pre_norm", x0, pre_norm_ffn1)
    paired = _gather_end("gather_end_neighbours", first, geoms[0], pair_sems, rest_started, halves=2)
    col_half = jnp.reshape(cy, (1,)).astype(jnp.int32)
    part = _ffn_gate_up_act_half("ffn1_gate_up_paired", h1, paired, col_half)
    fw = _gather_forward("gather_forward_diagonal", paired, geoms[0], relay_sems, part[2], arrivals=1,
                         only_diagonal=True)
    w_gu1 = _gather_end(f"gather_end_{big_names[0]}", fw[0], geoms[0], fw[1], part[2], halves=1)
    gate1, up1, act1 = _ffn_gate_up_act_half("ffn1_gate_up_others", h1, w_gu1, 1 - col_half, into=part)
    w_d1 = whole_weight(1, forward_weight(1, act1), act1)
    ff1 = _mm("ffn1_down", act1, w_d1, "nn", F32)
    fw = forward_weight(2, ff1)
    x1, hm = _resid_norm_fwd("ffn1_residual", x0, ff1, post_norm_ffn1, pre_norm_mix, 0.5)
    w_in = whole_weight(2, fw, hm)
    p = _mm("mix_in", hm, w_in, "nn", F32)
    fw = forward_weight(3, p)
    o_f, o_b, st_f, st_b = _hgrn_scan_fwd("hgrn_scan", p, hgrn_lower_bounds_fwd, hgrn_lower_bounds_bwd)
    y_h = _hgrn_out_fwd("hgrn_out", o_f, o_b, p, hgrn_out_norm, 4)
    kv_blk0 = (5 * hw + aw) // kvw
    k_pad = _pad_kv("attn_pad_k", p, kv_blk0, kvw)
    v_pad = _pad_kv("attn_pad_v", p, kv_blk0 + 1, kvw)
    bucket_ids = _t5_bucket_ids()
    bias = _bias_gather("attn_bias", rel_bias_table.T, bucket_ids).reshape(nah, WINDOW, SPAN)
    y_a = _attn_fwd("attn_fwd", p, k_pad, v_pad, bias, attn_sink, 5 * hw // aw)
    y_mix = _concat_cols("mix_concat", y_h, y_a)
    w_out = whole_weight(3, fw, y_mix)
    mixed = _mm("mix_out", y_mix, w_out, "nn", F32)
    fw = forward_weight(4, mixed)
    x2, h2 = _resid_norm_fwd("mix_residual", x1, mixed, post_norm_mix, pre_norm_ffn2, 1.0)
    w_gu2 = whole_weight(4, fw, h2)
    gate2, up2, act2 = _ffn_gate_up_act("ffn2_gate_up", h2, w_gu2)
    w_d2 = whole_weight(5, forward_weight(5, act2), act2)
    ff2 = _mm("ffn2_down", act2, w_d2, "nn", F32)
    loss_blk, dy, dff2, dg_post2 = _final_fwd_bwd("ffn2_residual_loss", x2, ff2, post_norm_ffn2, target, 0.5)

    reduce = [_GradReduce(n, gm, idx, c_idx) for n, gm in zip(big_names, geoms)]
    big_grads, big_delta, big_new_m, big_new_v = [None] * 6, [None] * 6, [None] * 6, [None] * 6

    def update(w, after):
        g, dl, nm, nv = _adamw(f"adamw_{big_names[w]}", big_w[w], reduce[w].finish(after), big_m[w], big_v[w])
        big_grads[w], big_delta[w], big_new_m[w], big_new_v[w] = g[None], dl[None], nm[None], nv[None]
        return dl

    def dw_start(w, x_act, dy_act, after=None):
        theirs = _dw_half(f"dw_theirs_{big_names[w]}", x_act, dy_act, geoms[w], c_idx, own=False, after=after)
        return reduce[w].pair_start(theirs)

    def dw_finish(w, x_act, dy_act, after):
        landed = reduce[w].pair_wait(after)
        half = _dw_half(f"dw_own_{big_names[w]}", x_act, dy_act, geoms[w], c_idx, own=True, addend=landed)
        return reduce[w].chip_start(half)

    tok = dw_start(5, act2, dff2)
    dgu2 = _ffn_dact("ffn2_dact", dff2, w_d2, gate2, up2, after=tok)
    tok = dw_finish(5, act2, dff2, after=dgu2)
    tok = dw_start(4, h2, dgu2, after=tok)
    dh2 = _ffn_dh("ffn2_dh", dgu2, w_gu2, after=tok)
    tok = dw_finish(4, h2, dgu2, after=dh2)
    dx2, dg_pre2, dmixed, dg_postm = _norms_bwd("mix_residual_bwd", dy, dh2, x2, pre_norm_ffn2,
                                                post=(mixed, post_norm_mix, 1.0), after=tok)
    tok = dw_start(3, y_mix, dmixed)
    dy_mix = _mm("mix_out_dx", dmixed, w_out, "nt", F32, after=tok)
    tok = dw_finish(3, y_mix, dmixed, after=dy_mix)
    dq_a, dk_pad, dv_pad, dbias, dsink = _attn_bwd("attn_bwd", p, k_pad, v_pad, bias, attn_sink, dy_mix,
                                                   5 * hw // aw, hw // aw, after=tok)
    tok = reduce[5].chip_finish(dq_a)
    drel_t = _bias_scatter("attn_dbias", dbias.reshape(nah, WINDOW * SPAN), bucket_ids)
    do, dg_h, dgain = _hgrn_out_bwd("hgrn_out_bwd", dy_mix, o_f, o_b, p, hgrn_out_norm, 4, after=tok)
    dq_f, dv_f, dz_f, dlb_f, dq_b, dv_b, dz_b, dlb_b = _hgrn_scan_bwd(
        "hgrn_scan_bwd", p, hgrn_lower_bounds_fwd, hgrn_lower_bounds_bwd, do, st_f, st_b)
    tok = reduce[4].chip_finish(dq_f)
    tok = reduce[3].chip_finish(tok)
    dp = _mix_dproj("mix_dproj", [(dq_f, dq_b), (dv_f, dv_b), (dz_f,), (dz_b,), (dg_h,), (dq_a,)],
                    [dk_pad, dv_pad], t, after=tok)
    tok = dw_start(2, hm, dp)
    dhm = _mm("mix_in_dx", dp, w_in, "nt", F32, after=tok)
    tok = dw_finish(2, hm, dp, after=dhm)
    dx1, dg_prem, dff1, dg_post1 = _norms_bwd("ffn1_residual_bwd", dx2, dhm, x1, pre_norm_mix,
                                              post=(ff1, post_norm_ffn1, 0.5), after=tok)
    tok = dw_start(1, act1, dff1)
    dgu1 = _ffn_dact("ffn1_dact", dff1, w_d1, gate1, up1, after=tok)
    tok = dw_finish(1, act1, dff1, after=dgu1)
    tok = reduce[2].chip_finish(tok)
    tok = dw_start(0, h1, dgu1, after=tok)
    done = update(2, tok)
    tok = dw_finish(0, h1, dgu1, after=done)
    dh1 = _ffn_dh("ffn1_dh", dgu1, w_gu1, after=tok)
    grad_x, dg_pre1 = _norms_bwd("ffn1_pre_norm_bwd", dx1, dh1, x0, pre_norm_ffn1)

    small_w = [pre_norm_ffn1, post_norm_ffn1, pre_norm_mix, post_norm_mix, hgrn_lower_bounds_fwd,
               hgrn_lower_bounds_bwd, hgrn_out_norm, attn_sink, pre_norm_ffn2, post_norm_ffn2, rel_bias_table]
    small_m = [m_pre_norm_ffn1, m_post_norm_ffn1, m_pre_norm_mix, m_post_norm_mix, m_hgrn_lower_bounds_fwd,
               m_hgrn_lower_bounds_bwd, m_hgrn_out_norm, m_attn_sink, m_pre_norm_ffn2, m_post_norm_ffn2,
               m_rel_bias_table]
    small_v = [v_pre_norm_ffn1, v_post_norm_ffn1, v_pre_norm_mix, v_post_norm_mix, v_hgrn_lower_bounds_fwd,
               v_hgrn_lower_bounds_bwd, v_hgrn_out_norm, v_attn_sink, v_pre_norm_ffn2, v_post_norm_ffn2,
               v_rel_bias_table]
    small_g = [dg_pre1, dg_post1, dg_prem, dg_postm, dlb_f, dlb_b, dgain, dsink[:, 0].reshape(1, nah), dg_pre2,
               dg_post2, drel_t.T]
    shapes = [a.shape for a in small_w]
    done = update(5, grad_x)
    done = update(4, done)
    done = update(3, done)
    summed = _all_reduce_small(_pack_rows(small_g + [loss_blk[0:1, 0:1]], d), after=done)
    loss = _unpack_rows(summed, shapes + [(1, 1)], d)[-1][0, 0]
    _, sd, sm, sv = _adamw("adamw_small", _pack_rows(small_w, d), summed, _pack_rows(small_m, d),
                           _pack_rows(small_v, d))
    small_grads = _unpack_rows(summed, shapes, d)
    small_delta, small_new_m, small_new_v = (_unpack_rows(a, shapes, d) for a in (sd, sm, sv))

    tok = reduce[1].chip_finish(sd)
    done = update(1, tok)
    tok = reduce[0].chip_finish(done)
    update(0, tok)

    def ordered(small, big):
        s = dict(zip(["pre1", "post1", "prem", "postm", "lbf", "lbb", "gain", "sink", "pre2", "post2", "rel"], small))
        b = dict(zip(["gu1", "d1", "win", "wout", "gu2", "d2"], big))
        return [s["pre1"], s["post1"], b["gu1"], b["d1"], s["prem"], s["postm"], b["win"], s["lbf"], s["lbb"],
                s["gain"], s["sink"], b["wout"], s["pre2"], s["post2"], b["gu2"], b["d2"], s["rel"]]

    return (loss, grad_x[None], *ordered(small_grads, big_grads), *ordered(small_delta, big_delta),
            *ordered(small_new_m, big_new_m), *ordered(small_new_v, big_new_v))
```

```python
import functools
import math

import jax
import jax.numpy as jnp
import numpy as np
from jax import lax
from jax.experimental import pallas as pl
from jax.experimental.pallas import tpu as pltpu

F32 = jnp.float32
BF16 = jnp.bfloat16

HEAD = 128
CHUNK = 64
WINDOW = 128
SPAN = 3 * WINDOW
KV_HEADS = 2
REL_BUCKETS = 32
REL_MAX_DIST = 128
EPS = 1e-6
NEG_INF = -1e30

ADAM_LR = 0.001
ADAM_B1 = 0.9
ADAM_B2 = 0.999
ADAM_EPS = 1e-08
ADAM_WD = 0.01
ADAM_STEP = 10

N_CHIPS = 4
N_DEV = 8
V7X_VMEM_BYTES = 64 * 1024 * 1024
MESH = pl.DeviceIdType.MESH
ANY = pl.BlockSpec(memory_space=pl.ANY)


def _tile(n, pref, mult):
    t = (min(pref, n) // mult) * mult
    while t >= mult:
        if n % t == 0:
            return t
        t -= mult
    return n


def _params(semantics, block_bytes):
    limit = min(V7X_VMEM_BYTES - (4 << 20), 2 * int(block_bytes) + (8 << 20))
    return pltpu.CompilerParams(dimension_semantics=semantics, vmem_limit_bytes=limit)


def _nbytes(shape, dtype):
    return int(np.prod(shape)) * jnp.dtype(dtype).itemsize


PIN_TO_HBM_BYTES = 4 << 20


def _pallas(body, **kw):
    def pin_shape(s):
        if isinstance(s, jax.ShapeDtypeStruct) and _nbytes(s.shape, s.dtype) >= PIN_TO_HBM_BYTES:
            return pltpu.HBM(s.shape, s.dtype)
        return s

    def pin(a):
        if getattr(a, "dtype", None) in (F32, BF16) and _nbytes(a.shape, a.dtype) >= PIN_TO_HBM_BYTES:
            return pltpu.with_memory_space_constraint(a, pltpu.HBM)
        return a

    out_shape = kw["out_shape"]
    kw["out_shape"] = [pin_shape(s) for s in out_shape] if isinstance(out_shape, (list, tuple)) else pin_shape(out_shape)
    call = pl.pallas_call(body, **kw)
    return lambda *args: call(*[pin(a) for a in args])


def _dot(a, b, ca=1, cb=0):
    return lax.dot_general(a, b, (((ca,), (cb,)), ((), ())), preferred_element_type=F32)


def _split3(x):
    hi = x.astype(BF16)
    r1 = x - hi.astype(F32)
    mid = r1.astype(BF16)
    lo = (r1 - mid.astype(F32)).astype(BF16)
    return hi, mid, lo


def _dot_exact(a, b, ca=1, cb=0, split="b"):
    if split == "b":
        return sum(_dot(a, p, ca, cb) for p in _split3(b))
    return sum(_dot(p, b, ca, cb) for p in _split3(a))


def _rms(x):
    return lax.rsqrt(jnp.mean(x * x, axis=-1, keepdims=True) + EPS)


def _norm_bwd(u, x, gain):
    r = _rms(x)
    xhat = x * r
    dgain = jnp.sum(u * xhat, axis=0, keepdims=True)
    v = u * gain
    dx = r * (v - xhat * jnp.mean(v * xhat, axis=-1, keepdims=True))
    return dx, dgain


def _sigmoid(x):
    return 1.0 / (1.0 + jnp.exp(-x))


def _accumulate(ref, val, first):
    @pl.when(first)
    def _():
        ref[...] = val

    @pl.when(jnp.logical_not(first))
    def _():
        ref[...] += val


def _ordered(body, ins, in_specs, after):
    if after is None:
        return body, list(ins), list(in_specs)
    n_in = len(ins)

    def wrapped(*refs):
        body(*refs[:n_in], *refs[n_in + 1:])

    return wrapped, list(ins) + [after], list(in_specs) + [pl.BlockSpec(memory_space=pl.ANY)]


def _matmul(name, a, b, *, form, out_dtype, tm, tn, tk, a_map=None, b_map=None,
            out_shape=None, out_block=None, out_map=None, sizes=None, after=None):
    if sizes is None:
        if form == "nn":
            (m, k), n = a.shape, b.shape[1]
        elif form == "nt":
            (m, k), n = a.shape, b.shape[0]
        else:
            (k, m), n = a.shape, b.shape[1]
    else:
        m, n, k = sizes
    gi, gj, gk = m // tm, n // tn, k // tk
    a_blk = (tm, tk) if form != "tn" else (tk, tm)
    b_blk = (tk, tn) if form != "nt" else (tn, tk)
    if a_map is None:
        a_map = (lambda i, j, kk: (i, kk)) if form != "tn" else (lambda i, j, kk: (kk, i))
    else:
        a_blk = (None,) + a_blk
    if b_map is None:
        b_map = (lambda i, j, kk: (kk, j)) if form != "nt" else (lambda i, j, kk: (j, kk))
    else:
        b_blk = (None,) + b_blk
    if out_shape is None:
        out_shape, out_block, out_map = (m, n), (tm, tn), (lambda i, j, kk: (i, j))
    ca, cb = {"nn": (1, 0), "nt": (1, 1), "tn": (0, 0)}[form]

    def body(a_ref, b_ref, o_ref, *acc):
        part = _dot(a_ref[...], b_ref[...], ca, cb)
        if gk == 1:
            o_ref[...] = part.astype(o_ref.dtype)
        else:
            kk = pl.program_id(2)
            _accumulate(acc[0], part, kk == 0)

            @pl.when(kk == gk - 1)
            def _():
                o_ref[...] = acc[0][...].astype(o_ref.dtype)

    scratch = [] if gk == 1 else [pltpu.VMEM((tm, tn), F32)]
    vmem = (_nbytes((tm, tk), a.dtype) + _nbytes((tk, tn), b.dtype) + _nbytes((tm, tn), out_dtype)
            + 2 * _nbytes((tm, tn), F32))
    body, ins, in_specs = _ordered(body, [a, b], [pl.BlockSpec(a_blk, a_map), pl.BlockSpec(b_blk, b_map)], after)
    return _pallas(
        body, name=name, grid=(gi, gj, gk), in_specs=in_specs,
        out_specs=pl.BlockSpec(out_block, out_map),
        out_shape=jax.ShapeDtypeStruct(out_shape, out_dtype),
        scratch_shapes=scratch,
        compiler_params=_params(("parallel", "parallel", "arbitrary"), vmem),
    )(*ins)


V7X_HBM_BYTES_PER_US = 3.0e6
V7X_MXU_FLOPS_PER_US = 0.9e9
V7X_VMEM_RMW_BYTES_PER_US = 10e6
GRID_STEP_US = 0.35
MATMUL_VMEM_BUDGET = 40 << 20
MATMUL_MAX_TILE_FLOPS = 1 << 33


def _divisors(n, mult, lo):
    return [t for t in range(mult, n + 1, mult) if n % t == 0 and t >= min(lo, n)]


def _mm_tiles(m, n, k, out_dtype=F32, n_unit=None, k_unit=None):
    out_bytes = jnp.dtype(out_dtype).itemsize
    best = None
    for tm in _divisors(m, 128, 256):
        for tn in _divisors(n_unit or n, 128, 256):
            for tk in _divisors(k_unit or k, 128, 512):
                gi, gj, gk = m // tm, n // tn, k // tk
                vmem = 4 * tm * tk + 4 * tk * tn + 2 * tm * tn * out_bytes + 4 * tm * tn * (2 if gk > 1 else 1)
                if vmem > MATMUL_VMEM_BUDGET or 2 * tm * tn * tk > MATMUL_MAX_TILE_FLOPS:
                    continue
                a_bytes = 2 * m * k * (gj if gk > 1 else 1)
                b_bytes = 2 * k * n * (1 if gj == 1 and gk == 1 else gi)
                hbm_us = (a_bytes + b_bytes + m * n * out_bytes) / V7X_HBM_BYTES_PER_US
                acc_us = (8 * m * n * gk / V7X_VMEM_RMW_BYTES_PER_US) if gk > 1 else 0.0
                cost = max(2 * m * n * k / V7X_MXU_FLOPS_PER_US, 1.3 * hbm_us) + GRID_STEP_US * gi * gj * gk + acc_us
                key = (round(cost, 1), vmem)
                if best is None or key < best[0]:
                    best = (key, (tm, tn, tk))
    return best[1]


def _mm(name, a, b, form, out_dtype, after=None):
    if form == "nn":
        m, k, n = a.shape[0], a.shape[1], b.shape[1]
    elif form == "nt":
        m, k, n = a.shape[0], a.shape[1], b.shape[0]
    else:
        m, k, n = a.shape[1], a.shape[0], b.shape[1]
    tm, tn, tk = _mm_tiles(m, n, k, out_dtype)
    return _matmul(name, a, b, form=form, out_dtype=out_dtype, tm=tm, tn=tn, tk=tk, after=after)


def _row_tile(t):
    return _tile(t, 256, 8)


def _norm_fwd(name, x, gain):
    t, d = x.shape
    tm = _row_tile(t)

    def body(x_ref, g_ref, h_ref):
        xv = x_ref[...]
        h_ref[...] = (xv * _rms(xv) * g_ref[...]).astype(BF16)

    row = pl.BlockSpec((tm, d), lambda i: (i, 0))
    vec = pl.BlockSpec((1, d), lambda i: (0, 0))
    return _pallas(
        body, name=name, grid=(t // tm,), in_specs=[row, vec], out_specs=row,
        out_shape=jax.ShapeDtypeStruct((t, d), BF16),
        compiler_params=_params(("parallel",), 2 * _nbytes((tm, d), F32)),
    )(x, gain)


def _resid_norm_fwd(name, xres, ff, gpost, gpre, scale):
    t, d = xres.shape
    tm = _row_tile(t)

    def body(x_ref, f_ref, gp_ref, gn_ref, xn_ref, h_ref):
        f = f_ref[...]
        xn = x_ref[...] + scale * (f * _rms(f) * gp_ref[...])
        xn_ref[...] = xn
        h_ref[...] = (xn * _rms(xn) * gn_ref[...]).astype(BF16)

    row = pl.BlockSpec((tm, d), lambda i: (i, 0))
    vec = pl.BlockSpec((1, d), lambda i: (0, 0))
    return _pallas(
        body, name=name, grid=(t // tm,), in_specs=[row, row, vec, vec], out_specs=[row, row],
        out_shape=[jax.ShapeDtypeStruct((t, d), F32), jax.ShapeDtypeStruct((t, d), BF16)],
        compiler_params=_params(("parallel",), 4 * _nbytes((tm, d), F32)),
    )(xres, ff, gpost, gpre)


def _final_fwd_bwd(name, xres, ff, gpost, target, scale):
    t, d = xres.shape
    tm = _row_tile(t)

    def body(x_ref, f_ref, gp_ref, t_ref, loss_ref, dy_ref, dff_ref, dg_ref):
        i = pl.program_id(0)
        f = f_ref[...]
        gp = gp_ref[...]
        y = x_ref[...] + scale * (f * _rms(f) * gp)
        err = y - t_ref[...]
        part = 0.5 * jnp.sum(jnp.mean(err * err, axis=-1, keepdims=True), axis=0, keepdims=True)
        _accumulate(loss_ref, jnp.broadcast_to(part, loss_ref.shape), i == 0)
        dy = err / d
        dy_ref[...] = dy
        dff, dg = _norm_bwd(scale * dy, f, gp)
        dff_ref[...] = dff.astype(BF16)
        _accumulate(dg_ref, dg, i == 0)

    row = pl.BlockSpec((tm, d), lambda i: (i, 0))
    vec = pl.BlockSpec((1, d), lambda i: (0, 0))
    return _pallas(
        body, name=name, grid=(t // tm,), in_specs=[row, row, vec, row],
        out_specs=[pl.BlockSpec((8, 128), lambda i: (0, 0)), row, row, vec],
        out_shape=[jax.ShapeDtypeStruct((8, 128), F32), jax.ShapeDtypeStruct((t, d), F32),
                   jax.ShapeDtypeStruct((t, d), BF16), jax.ShapeDtypeStruct((1, d), F32)],
        compiler_params=_params(("arbitrary",), 5 * _nbytes((tm, d), F32)),
    )(xres, ff, gpost, target)


def _norms_bwd(name, dres, dh, xin, gpre, post=None, after=None):
    t, d = dres.shape
    tm = _row_tile(t)
    with_post = post is not None

    def body(*refs):
        if with_post:
            dr_ref, dh_ref, x_ref, g_ref, f_ref, gp_ref, dx_ref, dg_ref, dff_ref, dgp_ref = refs
        else:
            dr_ref, dh_ref, x_ref, g_ref, dx_ref, dg_ref = refs
        i = pl.program_id(0)
        dx, dg = _norm_bwd(dh_ref[...], x_ref[...], g_ref[...])
        dx = dr_ref[...] + dx
        dx_ref[...] = dx
        _accumulate(dg_ref, dg, i == 0)
        if with_post:
            dff, dgp = _norm_bwd(post[2] * dx, f_ref[...], gp_ref[...])
            dff_ref[...] = dff.astype(BF16)
            _accumulate(dgp_ref, dgp, i == 0)

    row = pl.BlockSpec((tm, d), lambda i: (i, 0))
    vec = pl.BlockSpec((1, d), lambda i: (0, 0))
    ins, in_specs = [dres, dh, xin, gpre], [row, row, row, vec]
    out_specs = [row, vec]
    out_shape = [jax.ShapeDtypeStruct((t, d), F32), jax.ShapeDtypeStruct((1, d), F32)]
    if with_post:
        ins += [post[0], post[1]]
        in_specs += [row, vec]
        out_specs += [row, vec]
        out_shape += [jax.ShapeDtypeStruct((t, d), BF16), jax.ShapeDtypeStruct((1, d), F32)]
    body, ins, in_specs = _ordered(body, ins, in_specs, after)
    return _pallas(
        body, name=name, grid=(t // tm,), in_specs=in_specs, out_specs=out_specs, out_shape=out_shape,
        compiler_params=_params(("arbitrary",), 6 * _nbytes((tm, d), F32)),
    )(*ins)


SWIGLU_TILE = (1024, 512)


def _ffn_gate_up_act(name, h, w_gu):
    t, d = h.shape
    f = w_gu.shape[1] // 2
    tm, tn = _tile(t, SWIGLU_TILE[0], 128), _tile(f, SWIGLU_TILE[1], 128)
    nf = f // tn

    def body(h_ref, wg_ref, wu_ref, a_ref, dg_ref, du_ref):
        hv = h_ref[...]
        g = _dot(hv, wg_ref[...])
        u = _dot(hv, wu_ref[...])
        sig = _sigmoid(g)
        silu = g * sig
        a_ref[...] = (silu * u).astype(BF16)
        dg_ref[...] = (u * sig * (1.0 + g * (1.0 - sig))).astype(BF16)
        du_ref[...] = silu.astype(BF16)

    out = jax.ShapeDtypeStruct((t, f), BF16)
    blk = pl.BlockSpec((tm, tn), lambda i, j: (i, j))
    return _pallas(
        body, name=name, grid=(t // tm, nf),
        in_specs=[pl.BlockSpec((tm, d), lambda i, j: (i, 0)), pl.BlockSpec((d, tn), lambda i, j: (0, j)),
                  pl.BlockSpec((d, tn), lambda i, j: (0, j + nf))],
        out_specs=[blk, blk, blk], out_shape=[out, out, out],
        compiler_params=_params(("parallel", "parallel"),
                                _nbytes((tm, d), BF16) + 2 * _nbytes((d, tn), BF16) + 5 * _nbytes((tm, tn), F32)),
    )(h, w_gu, w_gu)


def _ffn_dact(name, dff, w_down, dact_dgate, dact_dup, after=None):
    t, d = dff.shape
    f = w_down.shape[0]
    tm, tn = _tile(t, SWIGLU_TILE[0], 128), _tile(f, SWIGLU_TILE[1], 128)

    def body(d_ref, w_ref, dg_ref, du_ref, o_ref):
        da = _dot(d_ref[...], w_ref[...], 1, 1)
        o_ref[0] = (da * dg_ref[...].astype(F32)).astype(BF16)
        o_ref[1] = (da * du_ref[...].astype(F32)).astype(BF16)

    blk = pl.BlockSpec((tm, tn), lambda i, j: (i, j))
    body, ins, in_specs = _ordered(
        body, [dff, w_down, dact_dgate, dact_dup],
        [pl.BlockSpec((tm, d), lambda i, j: (i, 0)), pl.BlockSpec((tn, d), lambda i, j: (j, 0)), blk, blk], after)
    return _pallas(
        body, name=name, grid=(t // tm, f // tn), in_specs=in_specs,
        out_specs=pl.BlockSpec((2, tm, tn), lambda i, j: (0, i, j)),
        out_shape=jax.ShapeDtypeStruct((2, t, f), BF16),
        compiler_params=_params(("parallel", "parallel"),
                                _nbytes((tm, d), BF16) + _nbytes((tn, d), BF16) + 5 * _nbytes((tm, tn), F32)),
    )(*ins)


def _ffn_dh(name, dgu, w_gu, after=None):
    _, t, f = dgu.shape
    d = w_gu.shape[0]
    tm, tn, tk = _mm_tiles(t, d, 2 * f, F32, k_unit=f)
    nkf = f // tk
    return _matmul(name, dgu, w_gu, form="nt", out_dtype=F32, tm=tm, tn=tn, tk=tk, sizes=(t, d, 2 * f),
                   a_map=lambda i, j, kk: (kk // nkf, i, kk % nkf), after=after)


def _lower_bound(lbp):
    m = jnp.max(lbp, axis=0, keepdims=True)
    e = jnp.exp(lbp - m)
    return e[0:1] / jnp.sum(e, axis=0, keepdims=True)


def _chunk_mask(reverse):
    row = lax.broadcasted_iota(jnp.int32, (CHUNK, CHUNK), 0)
    col = lax.broadcasted_iota(jnp.int32, (CHUNK, CHUNK), 1)
    return (col >= row) if reverse else (col <= row)


def _hgrn_gates(z, lb, mask_bf):
    sig = _sigmoid(z)
    f = lb + (1.0 - lb) * sig
    logf = jnp.log(f)
    k = 1.0 - f
    cum = _dot_exact(mask_bf, logf)
    last = jnp.sum(logf, axis=0, keepdims=True)
    return sig, f, k, cum, last


def _hgrn_scan_fwd(name, p, lbp_f, lbp_b):
    t = p.shape[0]
    hw = lbp_f.shape[1]
    nh, nc = hw // HEAD, t // CHUNK

    def body(qf, vf, zf, qb, vb, zb, lbf, lbb, of_ref, ob_ref, stf_ref, stb_ref, state):
        n = pl.program_id(0)

        @pl.when(n == 0)
        def _():
            state[...] = jnp.zeros_like(state)

        directions = [(qf, vf, zf, lbf, of_ref, stf_ref), (qb, vb, zb, lbb, ob_ref, stb_ref)]
        wide = []
        for d, (q_ref, v_ref, z_ref, lb_ref, o_ref, st_ref) in enumerate(directions):
            mask = _chunk_mask(d == 1)
            lb = _lower_bound(lb_ref[...])
            _, _, k, cum, last = _hgrn_gates(z_ref[...], lb, mask.astype(BF16))
            v = v_ref[...].astype(BF16)
            qd = (q_ref[...] * jnp.exp(cum)).astype(BF16)
            kd = (k * jnp.exp(-cum)).astype(BF16)
            kt = (k * jnp.exp(last - cum)).astype(BF16)
            s_all = state[d]
            st_ref[...] = s_all
            wide.append((mask, v, qd, kd, kt, jnp.exp(last), s_all, o_ref))
        pairs = [(d, slice(h * HEAD, (h + 1) * HEAD)) for d in range(2) for h in range(nh)]
        a = [jnp.where(wide[d][0], _dot(wide[d][2][:, sl], wide[d][3][:, sl], 1, 1), 0.0).astype(BF16)
             for d, sl in pairs]
        inter = [_dot(wide[d][2][:, sl], wide[d][6][:, sl].astype(BF16), 1, 1) for d, sl in pairs]
        intra = [_dot(a[i], wide[d][1][:, sl]) for i, (d, sl) in enumerate(pairs)]
        grow = [_dot(wide[d][1][:, sl], wide[d][4][:, sl], 0, 0) for d, sl in pairs]
        for i, (d, sl) in enumerate(pairs):
            wide[d][7][:, sl] = intra[i] + inter[i]
            state[d, :, sl] = wide[d][6][:, sl] * wide[d][5][:, sl] + grow[i]

    def col(group, reverse):
        return pl.BlockSpec((CHUNK, hw), lambda n: ((nc - 1 - n) if reverse else n, group))

    def st(reverse):
        return pl.BlockSpec((None, HEAD, hw), lambda n: ((nc - 1 - n) if reverse else n, 0, 0))

    lb_spec = pl.BlockSpec((2, hw), lambda n: (0, 0))
    out = jax.ShapeDtypeStruct((t, hw), F32)
    states = jax.ShapeDtypeStruct((nc, HEAD, hw), F32)
    return _pallas(
        body, name=name, grid=(nc,),
        in_specs=[col(0, False), col(1, False), col(2, False), col(0, True), col(1, True), col(3, True),
                  lb_spec, lb_spec],
        out_specs=[col(0, False), col(0, True), st(False), st(True)],
        out_shape=[out, out, states, states],
        scratch_shapes=[pltpu.VMEM((2, HEAD, hw), F32)],
        compiler_params=_params(("arbitrary",), 12 * _nbytes((HEAD, hw), F32)),
    )(p, p, p, p, p, p, lbp_f, lbp_b)


def _hgrn_scan_bwd(name, p, lbp_f, lbp_b, do, st_f, st_b):
    t = p.shape[0]
    hw = lbp_f.shape[1]
    nh, nc = hw // HEAD, t // CHUNK

    def body(qf, vf, zf, dof, sf, qb, vb, zb, dob, sb, lbf, lbb, dqf, dvf, dzf, dlbf, dqb, dvb, dzb, dlbb,
             dstate, dlb_acc, dqd_s, dkd_s, dkt_s, ddec_s):
        n = pl.program_id(0)

        @pl.when(n == 0)
        def _():
            dstate[...] = jnp.zeros_like(dstate)
            dlb_acc[...] = jnp.zeros_like(dlb_acc)

        directions = [(qf, vf, zf, dof, sf, lbf, dqf, dvf, dzf, dlbf), (qb, vb, zb, dob, sb, lbb, dqb, dvb, dzb, dlbb)]
        for d, (q_ref, v_ref, z_ref, do_ref, st_ref, lb_ref, dq_ref, dv_ref, dz_ref, dlb_ref) in enumerate(directions):
            mask = _chunk_mask(d == 1)
            mask_bf = mask.astype(BF16)
            lb = _lower_bound(lb_ref[...])
            sig, f, k, cum, last = _hgrn_gates(z_ref[...], lb, mask_bf)
            e_pos, e_neg, e_tail = jnp.exp(cum), jnp.exp(-cum), jnp.exp(last - cum)
            dec = jnp.exp(last)
            v = v_ref[...].astype(BF16)
            qd, kd, kt = q_ref[...] * e_pos, k * e_neg, k * e_tail
            qd_bf, kd_bf, kt_bf = qd.astype(BF16), kd.astype(BF16), kt.astype(BF16)
            s_all = st_ref[...]
            ds_all = dstate[d]
            dov = do_ref[...].astype(BF16)
            cols = [slice(h * HEAD, (h + 1) * HEAD) for h in range(nh)]
            s_bf = [s_all[:, sl].astype(BF16) for sl in cols]
            ds_bf = [ds_all[:, sl].astype(BF16) for sl in cols]
            a = [jnp.where(mask, _dot(qd_bf[:, sl], kd_bf[:, sl], 1, 1), 0.0).astype(BF16) for sl in cols]
            da = [jnp.where(mask, _dot(dov[:, sl], v[:, sl], 1, 1), 0.0).astype(BF16) for sl in cols]
            dv_h = [_dot(a[h], dov[:, sl], 0, 0) + _dot(kt_bf[:, sl], ds_bf[h], 1, 1) for h, sl in enumerate(cols)]
            dqd_h = [_dot(da[h], kd_bf[:, sl]) + _dot(dov[:, sl], s_bf[h]) for h, sl in enumerate(cols)]
            dkd_h = [_dot(da[h], qd_bf[:, sl], 0, 0) for h, sl in enumerate(cols)]
            dkt_h = [_dot(v[:, sl], ds_bf[h]) for h, sl in enumerate(cols)]
            dst_h = [_dot(dov[:, sl], qd_bf[:, sl], 0, 0) + ds_all[:, sl] * dec[:, sl] for sl in cols]
            for h, sl in enumerate(cols):
                dv_ref[:, sl] = dv_h[h]
                dqd_s[:, sl] = dqd_h[h]
                dkd_s[:, sl] = dkd_h[h]
                dkt_s[:, sl] = dkt_h[h]
                dstate[d, :, sl] = dst_h[h]
                ddec_s[:, sl] = jnp.sum(ds_all[:, sl] * s_all[:, sl], axis=0, keepdims=True)
            dqd, dkd, dkt = dqd_s[...], dkd_s[...], dkt_s[...]
            dlast = jnp.sum(dkt * kt, axis=0, keepdims=True) + dec * ddec_s[...]
            dq_ref[...] = dqd * e_pos
            dk = dkd * e_neg + dkt * e_tail
            dcum = dqd * qd - dkd * kd - dkt * kt
            dlogf = _dot_exact(mask_bf, dcum, 0, 0) + dlast
            df = dlogf / f - dk
            dz_ref[...] = df * (1.0 - lb) * sig * (1.0 - sig)
            dlb_acc[d] += jnp.sum(df * (1.0 - sig), axis=0, keepdims=True)

            @pl.when(n == nc - 1)
            def _():
                g = dlb_acc[d] * lb * (1.0 - lb)
                dlb_ref[0:1, :] = g
                dlb_ref[1:2, :] = -g

    def col(group, reverse):
        return pl.BlockSpec((CHUNK, hw), lambda n: (n if reverse else (nc - 1 - n), group))

    def st(reverse):
        return pl.BlockSpec((None, HEAD, hw), lambda n: (n if reverse else (nc - 1 - n), 0, 0))

    lb_spec = pl.BlockSpec((2, hw), lambda n: (0, 0))
    out = jax.ShapeDtypeStruct((t, hw), F32)
    dlb = jax.ShapeDtypeStruct((2, hw), F32)
    wide = pltpu.VMEM((CHUNK, hw), F32)
    return _pallas(
        body, name=name, grid=(nc,),
        in_specs=[col(0, False), col(1, False), col(2, False), col(0, False), st(False),
                  col(0, True), col(1, True), col(3, True), col(0, True), st(True), lb_spec, lb_spec],
        out_specs=[col(0, False), col(0, False), col(0, False), lb_spec,
                   col(0, True), col(0, True), col(0, True), lb_spec],
        out_shape=[out, out, out, dlb, out, out, out, dlb],
        scratch_shapes=[pltpu.VMEM((2, HEAD, hw), F32), pltpu.VMEM((2, 1, hw), F32), wide, wide, wide,
                        pltpu.VMEM((1, hw), F32)],
        compiler_params=_params(("arbitrary",), 16 * _nbytes((HEAD, hw), F32)),
    )(p, p, p, do, st_f, p, p, p, do, st_b, lbp_f, lbp_b)


def _hgrn_out_fwd(name, o_f, o_b, p, gain, g_group):
    t, hw = o_f.shape
    nh = hw // HEAD
    tm = _tile(t, 512, 8)

    def body(of_ref, ob_ref, g_ref, gain_ref, y_ref):
        o = of_ref[...] + ob_ref[...]
        g = g_ref[...]
        y_ref[...] = (o * _rms(o) * gain_ref[...] * (g * _sigmoid(g))).astype(BF16)

    blk = pl.BlockSpec((tm, HEAD), lambda i, h: (i, h))
    return _pallas(
        body, name=name, grid=(t // tm, nh),
        in_specs=[blk, blk, pl.BlockSpec((tm, HEAD), lambda i, h: (i, g_group * nh + h)),
                  pl.BlockSpec((1, HEAD), lambda i, h: (0, h))],
        out_specs=blk, out_shape=jax.ShapeDtypeStruct((t, hw), BF16),
        compiler_params=_params(("parallel", "parallel"), 1 << 20),
    )(o_f, o_b, p, gain)


def _hgrn_out_bwd(name, dy, o_f, o_b, p, gain, g_group, after=None):
    t, hw = o_f.shape
    nh = hw // HEAD
    tm = _tile(t, 512, 8)

    def body(dy_ref, of_ref, ob_ref, g_ref, gain_ref, do_ref, dg_ref, dgain_ref):
        i = pl.program_id(1)
        o = of_ref[...] + ob_ref[...]
        g = g_ref[...]
        gain_v = gain_ref[...]
        sig = _sigmoid(g)
        dyv = dy_ref[...]
        do, dgain = _norm_bwd(dyv * (g * sig), o, gain_v)
        do_ref[...] = do
        dg_ref[...] = dyv * (o * _rms(o) * gain_v) * sig * (1.0 + g * (1.0 - sig))
        _accumulate(dgain_ref, dgain, i == 0)

    blk = pl.BlockSpec((tm, HEAD), lambda h, i: (i, h))
    vec = pl.BlockSpec((1, HEAD), lambda h, i: (0, h))
    out = jax.ShapeDtypeStruct((t, hw), F32)
    body, ins, in_specs = _ordered(
        body, [dy, o_f, o_b, p, gain],
        [blk, blk, blk, pl.BlockSpec((tm, HEAD), lambda h, i: (i, g_group * nh + h)), vec], after)
    return _pallas(
        body, name=name, grid=(nh, t // tm), in_specs=in_specs,
        out_specs=[blk, blk, vec], out_shape=[out, out, jax.ShapeDtypeStruct((1, hw), F32)],
        compiler_params=_params(("parallel", "arbitrary"), 1 << 20),
    )(*ins)


def _t5_bucket_ids():
    c = np.arange(WINDOW)[:, None]
    s = np.arange(SPAN)[None, :]
    rel = s - WINDOW - c
    nb = REL_BUCKETS // 2
    max_exact = nb // 2
    bucket = (rel > 0).astype(np.int32) * nb
    n = np.abs(rel)
    large = max_exact + (np.log(np.maximum(n, 1) / max_exact) / np.log(REL_MAX_DIST / max_exact)
                         * (nb - max_exact)).astype(np.int32)
    large = np.minimum(large, nb - 1)
    ids = bucket + np.where(n < max_exact, n, large).astype(np.int32)
    return jnp.asarray(ids.reshape(1, WINDOW * SPAN), jnp.int32)


def _bias_onehot(ids_ref):
    n = ids_ref.shape[1]
    return (lax.broadcasted_iota(jnp.int32, (REL_BUCKETS, n), 0) == ids_ref[...]).astype(BF16)


def _bias_gather(name, table_t, ids):
    nh = table_t.shape[0]

    def body(t_ref, ids_ref, o_ref):
        o_ref[...] = _dot_exact(t_ref[...], _bias_onehot(ids_ref), split="a")

    return _pallas(
        body, name=name, out_shape=jax.ShapeDtypeStruct((nh, ids.shape[1]), F32),
        compiler_params=pltpu.CompilerParams(vmem_limit_bytes=32 << 20),
    )(table_t, ids)


def _bias_scatter(name, dbias, ids):
    nh = dbias.shape[0]

    def body(d_ref, ids_ref, o_ref):
        o_ref[...] = _dot_exact(d_ref[...], _bias_onehot(ids_ref), 1, 1, split="a")

    return _pallas(
        body, name=name, out_shape=jax.ShapeDtypeStruct((nh, REL_BUCKETS), F32),
        compiler_params=pltpu.CompilerParams(vmem_limit_bytes=32 << 20),
    )(dbias, ids)


def _attn_valid(i, t):
    c = lax.broadcasted_iota(jnp.int32, (WINDOW, SPAN), 0)
    s = lax.broadcasted_iota(jnp.int32, (WINDOW, SPAN), 1)
    rel = s - WINDOW - c
    pos = i * WINDOW - WINDOW + s
    return (jnp.abs(rel) <= WINDOW) & (pos >= 0) & (pos < t)


def _attn_probs(qs, khs, b_ref, s_ref, valid):
    heads = range(len(qs))
    sinks = [s_ref[0:1, h:h + 1] for h in heads]
    s = [_dot(qs[h], khs[h], 1, 1) / math.sqrt(HEAD) for h in heads]
    s = [jnp.where(valid, s[h] + b_ref[h], NEG_INF) for h in heads]
    m = [jnp.maximum(jnp.max(s[h], axis=-1, keepdims=True), sinks[h]) for h in heads]
    e = [jnp.exp(s[h] - m[h]) for h in heads]
    es = [jnp.exp(sinks[h] - m[h]) for h in heads]
    inv = [1.0 / (jnp.sum(e[h], axis=-1, keepdims=True) + es[h]) for h in heads]
    return [e[h] * inv[h] for h in heads], [es[h] * inv[h] for h in heads]


def _attn_fwd(name, p, k_pad, v_pad, bias, sink, q_group_blk):
    t = p.shape[0]
    nh = bias.shape[0]
    aw = nh * HEAD
    grp = nh // KV_HEADS
    nb = t // WINDOW

    def body(q_ref, k_ref, v_ref, b_ref, s_ref, y_ref):
        i = pl.program_id(0)
        valid = _attn_valid(i, t)
        start = pl.multiple_of(i * WINDOW, WINDOW)
        ks = k_ref[pl.ds(start, SPAN), :]
        vs = v_ref[pl.ds(start, SPAN), :]
        heads = range(nh)
        col = lambda h: slice(h * HEAD, (h + 1) * HEAD)
        qs = [q_ref[:, col(h)].astype(BF16) for h in heads]
        pr, _ = _attn_probs(qs, [ks[:, col(h // grp)] for h in heads], b_ref, s_ref, valid)
        out = [_dot(pr[h].astype(BF16), vs[:, col(h // grp)]) for h in heads]
        for h in heads:
            y_ref[:, col(h)] = out[h].astype(BF16)

    full = lambda a: pl.BlockSpec(a.shape, lambda i: (0,) * a.ndim)
    return _pallas(
        body, name=name, grid=(nb,),
        in_specs=[pl.BlockSpec((WINDOW, aw), lambda i: (i, q_group_blk)), full(k_pad), full(v_pad), full(bias),
                  full(sink)],
        out_specs=pl.BlockSpec((WINDOW, aw), lambda i: (i, 0)),
        out_shape=jax.ShapeDtypeStruct((t, aw), BF16),
        compiler_params=_params(("parallel",), _nbytes(k_pad.shape, BF16) * 2 + _nbytes(bias.shape, F32)),
    )(p, k_pad, v_pad, bias, sink)


def _attn_bwd(name, p, k_pad, v_pad, bias, sink, dy, q_group_blk, dy_blk, after=None):
    t = p.shape[0]
    nh = bias.shape[0]
    aw = nh * HEAD
    grp = nh // KV_HEADS
    nb = t // WINDOW
    kvw = k_pad.shape[1]

    def body(q_ref, k_ref, v_ref, b_ref, s_ref, dy_ref, dq_ref, dk_ref, dv_ref, db_ref, ds_ref):
        i = pl.program_id(0)

        @pl.when(i == 0)
        def _():
            dk_ref[...] = jnp.zeros_like(dk_ref)
            dv_ref[...] = jnp.zeros_like(dv_ref)
            db_ref[...] = jnp.zeros_like(db_ref)
            ds_ref[...] = jnp.zeros_like(ds_ref)

        valid = _attn_valid(i, t)
        start = pl.multiple_of(i * WINDOW, WINDOW)
        ks = k_ref[pl.ds(start, SPAN), :]
        vs = v_ref[pl.ds(start, SPAN), :]
        inv_sqrt = 1.0 / math.sqrt(HEAD)
        heads = range(nh)
        col = lambda h: slice(h * HEAD, (h + 1) * HEAD)
        qs = [q_ref[:, col(h)].astype(BF16) for h in heads]
        khs = [ks[:, col(h // grp)] for h in heads]
        pr, ps = _attn_probs(qs, khs, b_ref, s_ref, valid)
        dos = [dy_ref[:, col(h)].astype(BF16) for h in heads]
        dp = [_dot(dos[h], vs[:, col(h // grp)], 1, 1) for h in heads]
        delta = [jnp.sum(pr[h] * dp[h], axis=-1, keepdims=True) for h in heads]
        dsc = [pr[h] * (dp[h] - delta[h]) for h in heads]
        dsr = [(dsc[h] * inv_sqrt).astype(BF16) for h in heads]
        dq = [_dot(dsr[h], khs[h]) for h in heads]
        dk = [_dot(dsr[h], qs[h], 0, 0) for h in heads]
        dv = [_dot(pr[h].astype(BF16), dos[h], 0, 0) for h in heads]
        for h in heads:
            db_ref[h] += dsc[h]
            ds_ref[h:h + 1, :] += jnp.broadcast_to(jnp.sum(-ps[h] * delta[h], axis=0, keepdims=True), (1, 128))
            dq_ref[:, col(h)] = dq[h]
        for kv in range(KV_HEADS):
            group = range(kv * grp, (kv + 1) * grp)
            dk_ref[pl.ds(start, SPAN), col(kv)] += sum(dk[h] for h in group)
            dv_ref[pl.ds(start, SPAN), col(kv)] += sum(dv[h] for h in group)

    full = lambda a: pl.BlockSpec(a.shape, lambda i: (0,) * a.ndim)
    whole = lambda shape: pl.BlockSpec(shape, lambda i: (0,) * len(shape))
    pad_shape = (t + 2 * WINDOW, kvw)
    body, ins, in_specs = _ordered(
        body, [p, k_pad, v_pad, bias, sink, dy],
        [pl.BlockSpec((WINDOW, aw), lambda i: (i, q_group_blk)), full(k_pad), full(v_pad), full(bias), full(sink),
         pl.BlockSpec((WINDOW, aw), lambda i: (i, dy_blk))], after)
    return _pallas(
        body, name=name, grid=(nb,), in_specs=in_specs,
        out_specs=[pl.BlockSpec((WINDOW, aw), lambda i: (i, 0)), whole(pad_shape), whole(pad_shape),
                   whole(bias.shape), whole((nh, 128))],
        out_shape=[jax.ShapeDtypeStruct((t, aw), F32), jax.ShapeDtypeStruct(pad_shape, F32),
                   jax.ShapeDtypeStruct(pad_shape, F32), jax.ShapeDtypeStruct(bias.shape, F32),
                   jax.ShapeDtypeStruct((nh, 128), F32)],
        compiler_params=_params(("arbitrary",), 3 * _nbytes(pad_shape, F32) + 2 * _nbytes(bias.shape, F32)),
    )(*ins)


def _pad_kv(name, p, kv_blk, kvw):
    t = p.shape[0]
    nb = t // WINDOW

    def body(x_ref, o_ref):
        i = pl.program_id(0)
        inside = jnp.logical_and(i >= 1, i <= nb)
        o_ref[...] = jnp.where(inside, x_ref[...], 0.0).astype(BF16)

    return _pallas(
        body, name=name, grid=(nb + 2,),
        in_specs=[pl.BlockSpec((WINDOW, kvw), lambda i: (jnp.clip(i - 1, 0, nb - 1), kv_blk))],
        out_specs=pl.BlockSpec((WINDOW, kvw), lambda i: (i, 0)),
        out_shape=jax.ShapeDtypeStruct((t + 2 * WINDOW, kvw), BF16),
        compiler_params=_params(("parallel",), 1 << 20),
    )(p)


def _mix_dproj(name, pieces, kv_pads, t, after=None):
    hw = pieces[0][0].shape[1]
    kvw = kv_pads[0].shape[1]
    widths = [hw] * len(pieces) + [kvw] * len(kv_pads)
    total = sum(widths)
    tm = WINDOW
    flat = [a for pc in pieces for a in pc]

    def body(*refs):
        o_ref = refs[-1]
        pos, off = 0, 0
        for pc in pieces:
            val = refs[pos][...]
            for extra in range(1, len(pc)):
                val = val + refs[pos + extra][...]
            o_ref[:, off:off + hw] = val.astype(BF16)
            pos += len(pc)
            off += hw
        for _ in kv_pads:
            o_ref[:, off:off + kvw] = refs[pos][...].astype(BF16)
            pos += 1
            off += kvw

    in_specs = [pl.BlockSpec((tm, hw), lambda i: (i, 0)) for _ in flat]
    in_specs += [pl.BlockSpec((tm, kvw), lambda i: (i + 1, 0)) for _ in kv_pads]
    body, ins, in_specs = _ordered(body, [*flat, *kv_pads], in_specs, after)
    return _pallas(
        body, name=name, grid=(t // tm,), in_specs=in_specs,
        out_specs=pl.BlockSpec((tm, total), lambda i: (i, 0)),
        out_shape=jax.ShapeDtypeStruct((t, total), BF16),
        compiler_params=_params(("parallel",), 3 * _nbytes((tm, total), F32)),
    )(*ins)


def _concat_cols(name, a, b):
    t, wa = a.shape
    wb = b.shape[1]
    tm = _tile(t, 512, 16)

    def body(a_ref, b_ref, o_ref):
        o_ref[:, :wa] = a_ref[...]
        o_ref[:, wa:] = b_ref[...]

    return _pallas(
        body, name=name, grid=(t // tm,),
        in_specs=[pl.BlockSpec((tm, wa), lambda i: (i, 0)), pl.BlockSpec((tm, wb), lambda i: (i, 0))],
        out_specs=pl.BlockSpec((tm, wa + wb), lambda i: (i, 0)),
        out_shape=jax.ShapeDtypeStruct((t, wa + wb), a.dtype),
        compiler_params=_params(("parallel",), 2 * _nbytes((tm, wa + wb), a.dtype)),
    )(a, b)


def _cast_into_full(name, w, geom, idx, after=None):
    r, c = w.shape
    tr = _tile(r, 256, 16)
    nr = r // tr
    if geom.col:
        place = lambda i, iref: (i, iref[0])
    else:
        place = lambda i, iref: (iref[0] * nr + i, 0)

    def body(i_ref, w_ref, *rest):
        rest[-1][...] = w_ref[...].astype(BF16)

    in_specs = [pl.BlockSpec((tr, c), lambda i, iref: (i, 0))]
    ins = [w]
    if after is not None:
        in_specs.append(pl.BlockSpec(memory_space=pl.ANY))
        ins.append(after)
    return _pallas(
        body, name=name,
        grid_spec=pltpu.PrefetchScalarGridSpec(
            num_scalar_prefetch=1, grid=(nr,), in_specs=in_specs, out_specs=pl.BlockSpec((tr, c), place)),
        out_shape=pltpu.HBM(geom.full_shape, BF16),
        compiler_params=_params(("parallel",), 2 * _nbytes((tr, c), F32)),
    )(idx, *ins)


def _adamw(name, w, g, m, v):
    r, c = w.shape
    tr = _tile(r, 128, 8)
    bc1 = 1.0 - ADAM_B1 ** ADAM_STEP
    bc2 = 1.0 - ADAM_B2 ** ADAM_STEP

    def body(w_ref, g_ref, m_ref, v_ref, go_ref, d_ref, nm_ref, nv_ref):
        gv = g_ref[...]
        go_ref[...] = gv
        nm = ADAM_B1 * m_ref[...] + (1.0 - ADAM_B1) * gv
        nv = ADAM_B2 * v_ref[...] + (1.0 - ADAM_B2) * (gv * gv)
        nm_ref[...] = nm
        nv_ref[...] = nv
        d_ref[...] = -ADAM_LR * ((nm / bc1) / (jnp.sqrt(nv / bc2) + ADAM_EPS) + ADAM_WD * w_ref[...])

    blk = pl.BlockSpec((tr, c), lambda i: (i, 0))
    out = jax.ShapeDtypeStruct((r, c), F32)
    return _pallas(
        body, name=name, grid=(r // tr,), in_specs=[blk] * 4, out_specs=[blk] * 4, out_shape=[out] * 4,
        compiler_params=_params(("parallel",), 8 * _nbytes((tr, c), F32)),
    )(w, g, m, v)


def _mesh_pos():
    return lax.axis_index("x"), lax.axis_index("y"), lax.axis_index("c")


def _other_chips(x, y):
    return [(1 - x, y), (x, 1 - y), (1 - x, 1 - y)]


class _Big:
    def __init__(self, shard_shape, col_sharded):
        self.col = col_sharded
        r, c = shard_shape
        self.shard_shape = (r, c)
        self.full_shape = (r, N_CHIPS * c) if col_sharded else (N_CHIPS * r, c)
        self.half_shape = (r // 2, N_CHIPS * c) if col_sharded else (N_CHIPS * r, c // 2)
        self.shard_half_shape = (r // 2, c) if col_sharded else (r, c // 2)

    def region(self, ref, s, half=None):
        r, c = self.shard_shape
        if self.col:
            rows = slice(None) if half is None else pl.ds(half * (r // 2), r // 2)
            return ref.at[rows, pl.ds(s * c, c)]
        cols = slice(None) if half is None else pl.ds(half * (c // 2), c // 2)
        return ref.at[pl.ds(s * r, r), cols]

    def n_halves(self, ref, half, n):
        r, c = self.shard_shape
        if self.col:
            return ref.at[pl.ds(half * (r // 2), r // 2), pl.ds(0, n * c)]
        return ref.at[pl.ds(0, n * r), pl.ds(half * (c // 2), c // 2)]

    def three_halves(self, ref, half):
        return self.n_halves(ref, half, 3)

    def sub_half(self, ref, s, half, j):
        r, c = self.shard_shape
        if self.col:
            return ref.at[pl.ds(half * (r // 2) + j * (r // 4), r // 4), pl.ds(s * c, c)]
        return ref.at[pl.ds(s * r + j * (r // 2), r // 2), pl.ds(half * (c // 2), c // 2)]

    def half_of_full(self, ref, half):
        r, c = self.full_shape
        if self.col:
            return ref.at[pl.ds(half * (r // 2), r // 2), :]
        return ref.at[:, pl.ds(half * (c // 2), c // 2)]

    def half_of_shard(self, ref, half):
        r, c = self.shard_shape
        if self.col:
            return ref.at[pl.ds(half * (r // 2), r // 2), :]
        return ref.at[:, pl.ds(half * (c // 2), c // 2)]

    def shard_of_half(self, ref, s):
        r, c = self.shard_shape
        if self.col:
            return ref.at[:, pl.ds(s * c, c)]
        return ref.at[pl.ds(s * r, r), :]


HBM =pl.BlockSpec(memory_space=pltpu.HBM)
SEM = pl.BlockSpec(memory_space=pltpu.SEMAPHORE)
SPLIT_COPY = pltpu.CompilerParams(has_side_effects=pltpu.SideEffectType.DATAFLOW_SIDE_EFFECTING)


def _in_hbm(a):
    return pltpu.with_memory_space_constraint(a, pltpu.HBM)


def _gather_start(name, fulls, geoms, after):
    nw = len(fulls)

    def body(*refs):
        dst = refs[nw + 1:2 * nw + 1]
        sems = refs[2 * nw + 1:-1]
        x, y, c = _mesh_pos()
        mine = 2 * x + y
        for w in range(nw):
            own_half = geoms[w].region(dst[w], mine, c)
            for chip in _other_chips(x, y):
                pltpu.make_async_remote_copy(src_ref=own_half, dst_ref=own_half, send_sem=sems[2 * w],
                                             recv_sem=sems[2 * w + 1], device_id=(*chip, c),
                                             device_id_type=MESH).start()
        refs[-1][...] = jnp.zeros_like(refs[-1])

    out = _pallas(
        body, name=name, in_specs=[HBM] * nw + [pl.BlockSpec(memory_space=pl.ANY)],
        out_specs=[HBM] * nw + [SEM] * (2 * nw) + [pl.BlockSpec(memory_space=pltpu.VMEM)],
        out_shape=[pltpu.HBM(g.full_shape, BF16) for g in geoms] + [pltpu.SemaphoreType.DMA(())] * (2 * nw)
        + [jax.ShapeDtypeStruct((8, 128), F32)],
        input_output_aliases={w: w for w in range(nw)}, compiler_params=SPLIT_COPY,
    )(*[_in_hbm(a) for a in fulls], after)
    return list(out[:nw]), [(out[nw + 2 * w], out[nw + 2 * w + 1]) for w in range(nw)], out[-1]


def _gather_first_direct(full, geom):
    def start(refs, _, new):
        x, y, c = _mesh_pos()
        own = geom.region(refs[0], 2 * x + y, c)
        for chip in ((1 - x, y), (x, 1 - y)):
            _remote(own, own, new, (*chip, c)).start()

    return _split_copy_call("gather_first_direct", [full], start, new_sems=2)


def _gather_first_relay(full, geom, sems, after):
    def relay(refs, got, new):
        x, y, c = _mesh_pos()
        w = refs[0]
        two = geom.n_halves(w, c, 2)
        _remote(two, two, got, (x, y, 1 - c)).wait_recv()
        from_x = geom.sub_half(w, 2 * (1 - x) + y, c, 0)
        from_y = geom.sub_half(w, 2 * x + (1 - y), c, 1)
        _remote(from_x, from_x, new, (x, 1 - y, c)).start()
        _remote(from_y, from_y, new, (1 - x, y, c)).start()
        _remote(two, two, got, (x, y, 1 - c)).wait_send()

    return _split_copy_call("gather_first_relay", [full], relay, sems=sems, after=after, new_sems=2)


def _gather_forward(name, full, geom, sems, after, arrivals=3, only_diagonal=False):
    def body(w_in, send_sem, recv_sem, after_ref, w_ref, fwd_send, fwd_recv):
        x, y, c = _mesh_pos()
        sibling = (x, y, 1 - c)
        landed_all = geom.n_halves(w_ref, c, arrivals)
        _remote(landed_all, landed_all, (send_sem, recv_sem), sibling).wait_recv()
        for chip in _other_chips(x, y)[2 if only_diagonal else 0:]:
            landed = geom.region(w_ref, 2 * chip[0] + chip[1], c)
            pltpu.make_async_remote_copy(src_ref=landed, dst_ref=landed, send_sem=fwd_send, recv_sem=fwd_recv,
                                         device_id=sibling, device_id_type=MESH).start()
        _remote(landed_all, landed_all, (send_sem, recv_sem), sibling).wait_send()

    sem = pltpu.SemaphoreType.DMA(())
    out = _pallas(
        body, name=name, in_specs=[HBM, SEM, SEM, pl.BlockSpec(memory_space=pl.ANY)], out_specs=[HBM, SEM, SEM],
        out_shape=[pltpu.HBM(geom.full_shape, BF16), sem, sem],
        input_output_aliases={0: 0}, compiler_params=SPLIT_COPY,
    )(full, sems[0], sems[1], after)
    return out[0], (out[1], out[2])


def _gather_end(name, full, geom, sems, after, halves=3):
    def body(w_in, fwd_send, fwd_recv, after_ref, w_ref):
        x, y, c = _mesh_pos()
        sibling = (x, y, 1 - c)
        theirs, ours = geom.n_halves(w_ref, 1 - c, halves), geom.n_halves(w_ref, c, halves)
        _remote(theirs, theirs, (fwd_send, fwd_recv), sibling).wait_recv()
        _remote(ours, ours, (fwd_send, fwd_recv), sibling).wait_send()

    return _pallas(
        body, name=name, in_specs=[HBM, SEM, SEM, pl.BlockSpec(memory_space=pl.ANY)], out_specs=HBM,
        out_shape=pltpu.HBM(geom.full_shape, BF16),
        input_output_aliases={0: 0}, compiler_params=SPLIT_COPY,
    )(full, sems[0], sems[1], after)


def _split_copy_call(name, arrays, fn, sems=(), after=None, new_sems=0):
    n, ns = len(arrays), len(sems)
    n_in = n + ns + (after is not None)

    def body(*refs):
        fn(refs[n_in:n_in + n], refs[n:n + ns], refs[n_in + n:-1])
        refs[-1][...] = jnp.zeros_like(refs[-1])

    ins = list(arrays) if ns else [_in_hbm(a) for a in arrays]
    ins += list(sems) + ([after] if after is not None else [])
    in_specs = [HBM] * n + [SEM] * ns + ([pl.BlockSpec(memory_space=pl.ANY)] if after is not None else [])
    out = _pallas(
        body, name=name, in_specs=in_specs,
        out_specs=[HBM] * n + [SEM] * new_sems + [pl.BlockSpec(memory_space=pltpu.VMEM)],
        out_shape=[pltpu.HBM(a.shape, a.dtype) for a in arrays] + [pltpu.SemaphoreType.DMA(())] * new_sems
        + [jax.ShapeDtypeStruct((8, 128), F32)],
        input_output_aliases={i: i for i in range(n)}, compiler_params=SPLIT_COPY,
    )(*ins)
    return list(out[:n]), tuple(out[n:-1]), out[-1]


def _remote(src, dst, sems, to):
    return pltpu.make_async_remote_copy(src_ref=src, dst_ref=dst, send_sem=sems[0], recv_sem=sems[1],
                                        device_id=to, device_id_type=MESH)


class _GradReduce:
    def __init__(self, name, geom, idx, c_idx):
        self.name, self.geom, self.idx, self.c_idx = name, geom, idx, c_idx

    def pair_start(self, theirs):
        g = self.geom

        def start(refs, _, new):
            x, y, c = _mesh_pos()
            _remote(refs[0], refs[1], new, (x, y, 1 - c)).start()

        self.arrays, self.sems, token = _split_copy_call(
            f"pair_start_{self.name}", [theirs, lax.empty(g.half_shape, BF16)], start, new_sems=2)
        return token

    def pair_wait(self, after):
        def wait(refs, sems, _):
            x, y, c = _mesh_pos()
            copy = _remote(refs[0], refs[1], sems, (x, y, 1 - c))
            copy.wait_send()
            copy.wait_recv()

        (_, landed), _, _ = _split_copy_call(f"pair_wait_{self.name}", self.arrays, wait, self.sems, after)
        return landed

    def chip_start(self, half):
        g = self.geom

        def start(refs, _, new):
            x, y, c = _mesh_pos()
            for k, chip in enumerate(_other_chips(x, y)):
                _remote(g.shard_of_half(refs[0], 2 * chip[0] + chip[1]), refs[1].at[k], new, (*chip, c)).start()

        self.arrays, self.sems, token = _split_copy_call(
            f"chip_start_{self.name}", [half, lax.empty((3,) + g.shard_half_shape, BF16)], start, new_sems=2)
        return token

    def chip_finish(self, after):
        g = self.geom

        def wait(refs, sems, _):
            x, y, c = _mesh_pos()
            three = _remote(refs[1], refs[1], sems, (x, y, 1 - c))
            three.wait_send()
            three.wait_recv()

        (half, landed), _, _ = _split_copy_call(f"chip_wait_{self.name}", self.arrays, wait, self.sems, after)
        quarter = _chip_add(f"chip_add_{self.name}", half, landed, g, self.idx)

        def start(refs, _, new):
            x, y, c = _mesh_pos()
            own = g.half_of_shard(refs[0], c)
            _remote(own, own, new, (x, y, 1 - c)).start()

        self.arrays, self.sems, token = _split_copy_call(f"share_start_{self.name}", [quarter], start, new_sems=2)
        return token

    def finish(self, after):
        g = self.geom

        def wait(refs, sems, _):
            x, y, c = _mesh_pos()
            own, theirs = g.half_of_shard(refs[0], c), g.half_of_shard(refs[0], 1 - c)
            _remote(own, own, sems, (x, y, 1 - c)).wait_send()
            _remote(theirs, theirs, sems, (x, y, 1 - c)).wait_recv()

        (quarter,), _, _ = _split_copy_call(f"share_wait_{self.name}", self.arrays, wait, self.sems, after)
        return quarter


def _dw_half(name, x, dy, geom, c_idx, own, addend=None, after=None):
    stacked = dy.ndim == 3
    t, m = x.shape
    n = 2 * dy.shape[2] if stacked else dy.shape[1]
    hm, hn = (m // 2, n) if geom.col else (m, n // 2)
    tm, tn, tk = _mm_tiles(hm, hn, t, BF16, n_unit=(n // 2 if stacked else None))
    if tk != t:
        tm, tn = _tile(hm, 512, 128), _tile(hn // (2 if stacked else 1), 512, 128)
    gi, gj = hm // tm, hn // tn
    nf = (n // 2) // tn

    def sel(cref):
        return cref[0] if own else 1 - cref[0]

    a_map = (lambda i, j, cref: (0, sel(cref) * gi + i)) if geom.col else (lambda i, j, cref: (0, i))
    if stacked:
        b_blk, b_map = (None, t, tn), (lambda i, j, cref: (j // nf, 0, j % nf))
    elif geom.col:
        b_blk, b_map = (t, tn), (lambda i, j, cref: (0, j))
    else:
        b_blk, b_map = (t, tn), (lambda i, j, cref: (0, sel(cref) * gj + j))
    out_blk = pl.BlockSpec((tm, tn), lambda i, j, cref: (i, j))
    ins, in_specs = [x, dy], [pl.BlockSpec((t, tm), a_map), pl.BlockSpec(b_blk, b_map)]
    if addend is not None:
        ins.append(addend)
        in_specs.append(out_blk)
    if after is not None:
        ins.append(after)
        in_specs.append(pl.BlockSpec(memory_space=pl.ANY))

    def body(c_ref, *refs):
        acc = _dot(refs[0][...], refs[1][...], 0, 0)
        if addend is not None:
            acc = acc + refs[2][...].astype(F32)
        refs[len(ins)][...] = acc.astype(BF16)

    return _pallas(
        body, name=name,
        grid_spec=pltpu.PrefetchScalarGridSpec(num_scalar_prefetch=1, grid=(gi, gj), in_specs=in_specs,
                                               out_specs=out_blk),
        out_shape=jax.ShapeDtypeStruct((hm, hn), BF16),
        compiler_params=_params(("parallel", "parallel"),
                                _nbytes((t, tm), BF16) + _nbytes((t, tn), BF16) + 3 * _nbytes((tm, tn), F32)),
    )(c_idx, *ins)


def _chip_add(name, half, recv, geom, idx):
    r, c = geom.shard_half_shape
    tr, tc = _tile(r, 512, 16), _tile(c, 2048, 128)
    nr, ncol = r // tr, c // tc
    if geom.col:
        mine = lambda i, j, iref: (i, iref[0] * ncol + j)
        place = lambda i, j, iref: (iref[1] * nr + i, j)
    else:
        mine = lambda i, j, iref: (iref[0] * nr + i, j)
        place = lambda i, j, iref: (i, iref[1] * ncol + j)

    def body(i_ref, h_ref, r_ref, o_ref):
        acc = h_ref[...].astype(F32)
        for k in range(3):
            acc = acc + r_ref[k].astype(F32)
        o_ref[...] = acc

    return _pallas(
        body, name=name,
        grid_spec=pltpu.PrefetchScalarGridSpec(
            num_scalar_prefetch=1, grid=(nr, ncol),
            in_specs=[pl.BlockSpec((tr, tc), mine), pl.BlockSpec((3, tr, tc), lambda i, j, iref: (0, i, j))],
            out_specs=pl.BlockSpec((tr, tc), place)),
        out_shape=jax.ShapeDtypeStruct(geom.shard_shape, F32),
        compiler_params=_params(("parallel", "parallel"), 4 * _nbytes((tr, tc), F32)),
    )(idx, half, recv)


def _all_reduce_small(pack, after=None):
    r, d = pack.shape

    def body(p_ref, o_ref, slots, send_sems, recv_sems):
        x, y, c = _mesh_pos()
        me = 4 * x + 2 * y + c
        slots[me] = p_ref[...]
        copies = []
        for k in range(1, N_DEV):
            px, py, pc = x ^ ((k >> 2) & 1), y ^ ((k >> 1) & 1), c ^ (k & 1)
            copies.append(pltpu.make_async_remote_copy(
                src_ref=p_ref, dst_ref=slots.at[me], send_sem=send_sems.at[k - 1], recv_sem=recv_sems.at[k - 1],
                device_id=(px, py, pc), device_id_type=MESH))
        for cp in copies:
            cp.start()
        for k in range(1, N_DEV):
            peer = 4 * (x ^ ((k >> 2) & 1)) + 2 * (y ^ ((k >> 1) & 1)) + (c ^ (k & 1))
            pltpu.make_async_remote_copy(
                src_ref=p_ref, dst_ref=slots.at[peer], send_sem=send_sems.at[k - 1], recv_sem=recv_sems.at[k - 1],
                device_id=(x, y, c), device_id_type=MESH).wait_recv()
        for cp in copies:
            cp.wait_send()
        acc = slots[0]
        for k in range(1, N_DEV):
            acc = acc + slots[k]
        o_ref[...] = acc

    vm = pl.BlockSpec(memory_space=pltpu.VMEM)
    body, ins, in_specs = _ordered(body, [pack], [vm], after)
    return _pallas(
        body, name="all_reduce_small", in_specs=in_specs, out_specs=vm,
        out_shape=jax.ShapeDtypeStruct((r, d), F32),
        scratch_shapes=[pltpu.VMEM((N_DEV, r, d), F32), pltpu.SemaphoreType.DMA((N_DEV - 1,)),
                        pltpu.SemaphoreType.DMA((N_DEV - 1,))],
    )(*ins)


def _pack_rows(rows, d):
    out = []
    for a in rows:
        flat = a.reshape(-1)
        n = -(-flat.shape[0] // d) * d
        out.append(jnp.pad(flat, (0, n - flat.shape[0])).reshape(-1, d))
    packed = jnp.concatenate(out, axis=0)
    return jnp.pad(packed, ((0, 16 - packed.shape[0]), (0, 0)))


def _unpack_rows(packed, shapes, d):
    out, row = [], 0
    for shp in shapes:
        n = int(np.prod(shp))
        nrows = -(-n // d)
        out.append(packed[row:row + nrows].reshape(-1)[:n].reshape(shp))
        row += nrows
    return out


def kernel(x, pre_norm_ffn1, post_norm_ffn1, w_ffn1_gate_up, w_ffn1_down, pre_norm_mix, post_norm_mix, w_mix_in, hgrn_lower_bounds_fwd, hgrn_lower_bounds_bwd, hgrn_out_norm, attn_sink, w_mix_out, pre_norm_ffn2, post_norm_ffn2, w_ffn2_gate_up, w_ffn2_down, rel_bias_table, loss_target, m_pre_norm_ffn1, m_post_norm_ffn1, m_w_ffn1_gate_up, m_w_ffn1_down, m_pre_norm_mix, m_post_norm_mix, m_w_mix_in, m_hgrn_lower_bounds_fwd, m_hgrn_lower_bounds_bwd, m_hgrn_out_norm, m_attn_sink, m_w_mix_out, m_pre_norm_ffn2, m_post_norm_ffn2, m_w_ffn2_gate_up, m_w_ffn2_down, m_rel_bias_table, v_pre_norm_ffn1, v_post_norm_ffn1, v_w_ffn1_gate_up, v_w_ffn1_down, v_pre_norm_mix, v_post_norm_mix, v_w_mix_in, v_hgrn_lower_bounds_fwd, v_hgrn_lower_bounds_bwd, v_hgrn_out_norm, v_attn_sink, v_w_mix_out, v_pre_norm_ffn2, v_post_norm_ffn2, v_w_ffn2_gate_up, v_w_ffn2_down, v_rel_bias_table):
    t, d = x.shape[1], x.shape[2]
    hw = hgrn_out_norm.shape[1]
    aw = d - hw
    nah = aw // HEAD
    kvw = KV_HEADS * HEAD
    x0 = x[0]
    target = loss_target[0]

    big_names = ["w_ffn1_gate_up", "w_ffn1_down", "w_mix_in", "w_mix_out", "w_ffn2_gate_up", "w_ffn2_down"]
    big_w = [w_ffn1_gate_up[0], w_ffn1_down[0], w_mix_in[0], w_mix_out[0], w_ffn2_gate_up[0], w_ffn2_down[0]]
    big_m = [m_w_ffn1_gate_up[0], m_w_ffn1_down[0], m_w_mix_in[0], m_w_mix_out[0], m_w_ffn2_gate_up[0],
             m_w_ffn2_down[0]]
    big_v = [v_w_ffn1_gate_up[0], v_w_ffn1_down[0], v_w_mix_in[0], v_w_mix_out[0], v_w_ffn2_gate_up[0],
             v_w_ffn2_down[0]]
    col_sharded = [True, False, True, False, True, False]
    geoms = [_Big(w.shape, cs) for w, cs in zip(big_w, col_sharded)]

    cx, cy, cc = _mesh_pos()
    idx = jnp.stack([2 * cx + cy, cc]).astype(jnp.int32)
    c_idx = jnp.reshape(cc, (1,)).astype(jnp.int32)
    first = _cast_into_full(f"cast_{big_names[0]}", big_w[0], geoms[0], idx)
    (first,), direct_sems, tok = _gather_first_direct(first, geoms[0])
    rest = []
    for n, w, gm in zip(big_names[1:], big_w[1:], geoms[1:]):
        tok = _cast_into_full(f"cast_{n}", w, gm, idx, after=tok)
        rest.append(tok)
    (first,), relay_sems, tok = _gather_first_relay(first, geoms[0], direct_sems, after=tok)
    started_rest, sems_rest, rest_started = _gather_start("gather_start_rest", rest, geoms[1:], after=tok)
    started, gather_sems = [first] + started_rest, [relay_sems] + sems_rest

    def forward_weight(w, after):
        return _gather_forward(f"gather_forward_{big_names[w]}", started[w], geoms[w], gather_sems[w], after,
                               arrivals=1 if w == 0 else 3)

    def whole_weight(w, forwarded, after):
        return _gather_end(f"gather_end_{big_names[w]}", forwarded[0], geoms[w], forwarded[1], after)

    h1 = _norm_fwd("ffn1_pre_norm", x0, pre_norm_ffn1)
    w_gu1 = whole_weight(0, forward_weight(0, rest_started), h1)
    act1, dact_dgate1, dact_dup1 = _ffn_gate_up_act("ffn1_gate_up", h1, w_gu1)
    w_d1 = whole_weight(1, forward_weight(1, act1), act1)
    ff1 = _mm("ffn1_down", act1, w_d1, "nn", F32)
    fw = forward_weight(2, ff1)
    x1, hm = _resid_norm_fwd("ffn1_residual", x0, ff1, post_norm_ffn1, pre_norm_mix, 0.5)
    w_in = whole_weight(2, fw, hm)
    p = _mm("mix_in", hm, w_in, "nn", F32)
    fw = forward_weight(3, p)
    o_f, o_b, st_f, st_b = _hgrn_scan_fwd("hgrn_scan", p, hgrn_lower_bounds_fwd, hgrn_lower_bounds_bwd)
    y_h = _hgrn_out_fwd("hgrn_out", o_f, o_b, p, hgrn_out_norm, 4)
    kv_blk0 = (5 * hw + aw) // kvw
    k_pad = _pad_kv("attn_pad_k", p, kv_blk0, kvw)
    v_pad = _pad_kv("attn_pad_v", p, kv_blk0 + 1, kvw)
    bucket_ids = _t5_bucket_ids()
    bias = _bias_gather("attn_bias", rel_bias_table.T, bucket_ids).reshape(nah, WINDOW, SPAN)
    y_a = _attn_fwd("attn_fwd", p, k_pad, v_pad, bias, attn_sink, 5 * hw // aw)
    y_mix = _concat_cols("mix_concat", y_h, y_a)
    w_out = whole_weight(3, fw, y_mix)
    mixed = _mm("mix_out", y_mix, w_out, "nn", F32)
    fw = forward_weight(4, mixed)
    x2, h2 = _resid_norm_fwd("mix_residual", x1, mixed, post_norm_mix, pre_norm_ffn2, 1.0)
    w_gu2 = whole_weight(4, fw, h2)
    act2, dact_dgate2, dact_dup2 = _ffn_gate_up_act("ffn2_gate_up", h2, w_gu2)
    w_d2 = whole_weight(5, forward_weight(5, act2), act2)
    ff2 = _mm("ffn2_down", act2, w_d2, "nn", F32)
    loss_blk, dy, dff2, dg_post2 = _final_fwd_bwd("ffn2_residual_loss", x2, ff2, post_norm_ffn2, target, 0.5)

    reduce = [_GradReduce(n, gm, idx, c_idx) for n, gm in zip(big_names, geoms)]
    big_grads, big_delta, big_new_m, big_new_v = [None] * 6, [None] * 6, [None] * 6, [None] * 6

    def update(w, after):
        g, dl, nm, nv = _adamw(f"adamw_{big_names[w]}", big_w[w], reduce[w].finish(after), big_m[w], big_v[w])
        big_grads[w], big_delta[w], big_new_m[w], big_new_v[w] = g[None], dl[None], nm[None], nv[None]
        return dl

    def dw_start(w, x_act, dy_act, after=None):
        theirs = _dw_half(f"dw_theirs_{big_names[w]}", x_act, dy_act, geoms[w], c_idx, own=False, after=after)
        return reduce[w].pair_start(theirs)

    def dw_finish(w, x_act, dy_act, after):
        landed = reduce[w].pair_wait(after)
        half = _dw_half(f"dw_own_{big_names[w]}", x_act, dy_act, geoms[w], c_idx, own=True, addend=landed)
        return reduce[w].chip_start(half)

    tok = dw_start(5, act2, dff2)
    dgu2 = _ffn_dact("ffn2_dact", dff2, w_d2, dact_dgate2, dact_dup2, after=tok)
    tok = dw_finish(5, act2, dff2, after=dgu2)
    tok = dw_start(4, h2, dgu2, after=tok)
    dh2 = _ffn_dh("ffn2_dh", dgu2, w_gu2, after=tok)
    tok = dw_finish(4, h2, dgu2, after=dh2)
    dx2, dg_pre2, dmixed, dg_postm = _norms_bwd("mix_residual_bwd", dy, dh2, x2, pre_norm_ffn2,
                                                post=(mixed, post_norm_mix, 1.0), after=tok)
    tok = dw_start(3, y_mix, dmixed)
    dy_mix = _mm("mix_out_dx", dmixed, w_out, "nt", F32, after=tok)
    tok = dw_finish(3, y_mix, dmixed, after=dy_mix)
    dq_a, dk_pad, dv_pad, dbias, dsink = _attn_bwd("attn_bwd", p, k_pad, v_pad, bias, attn_sink, dy_mix,
                                                   5 * hw // aw, hw // aw, after=tok)
    tok = reduce[5].chip_finish(dq_a)
    drel_t = _bias_scatter("attn_dbias", dbias.reshape(nah, WINDOW * SPAN), bucket_ids)
    do, dg_h, dgain = _hgrn_out_bwd("hgrn_out_bwd", dy_mix, o_f, o_b, p, hgrn_out_norm, 4, after=tok)
    dq_f, dv_f, dz_f, dlb_f, dq_b, dv_b, dz_b, dlb_b = _hgrn_scan_bwd(
        "hgrn_scan_bwd", p, hgrn_lower_bounds_fwd, hgrn_lower_bounds_bwd, do, st_f, st_b)
    tok = reduce[4].chip_finish(dq_f)
    tok = reduce[3].chip_finish(tok)
    dp = _mix_dproj("mix_dproj", [(dq_f, dq_b), (dv_f, dv_b), (dz_f,), (dz_b,), (dg_h,), (dq_a,)],
                    [dk_pad, dv_pad], t, after=tok)
    tok = dw_start(2, hm, dp)
    dhm = _mm("mix_in_dx", dp, w_in, "nt", F32, after=tok)
    tok = dw_finish(2, hm, dp, after=dhm)
    dx1, dg_prem, dff1, dg_post1 = _norms_bwd("ffn1_residual_bwd", dx2, dhm, x1, pre_norm_mix,
                                              post=(ff1, post_norm_ffn1, 0.5), after=tok)
    tok = dw_start(1, act1, dff1)
    dgu1 = _ffn_dact("ffn1_dact", dff1, w_d1, dact_dgate1, dact_dup1, after=tok)
    tok = dw_finish(1, act1, dff1, after=dgu1)
    tok = reduce[2].chip_finish(tok)
    tok = dw_start(0, h1, dgu1, after=tok)
    done = update(2, tok)
    tok = dw_finish(0, h1, dgu1, after=done)
    dh1 = _ffn_dh("ffn1_dh", dgu1, w_gu1, after=tok)
    grad_x, dg_pre1 = _norms_bwd("ffn1_pre_norm_bwd", dx1, dh1, x0, pre_norm_ffn1)

    small_w = [pre_norm_ffn1, post_norm_ffn1, pre_norm_mix, post_norm_mix, hgrn_lower_bounds_fwd,
               hgrn_lower_bounds_bwd, hgrn_out_norm, attn_sink, pre_norm_ffn2, post_norm_ffn2, rel_bias_table]
    small_m = [m_pre_norm_ffn1, m_post_norm_ffn1, m_pre_norm_mix, m_post_norm_mix, m_hgrn_lower_bounds_fwd,
               m_hgrn_lower_bounds_bwd, m_hgrn_out_norm, m_attn_sink, m_pre_norm_ffn2, m_post_norm_ffn2,
               m_rel_bias_table]
    small_v = [v_pre_norm_ffn1, v_post_norm_ffn1, v_pre_norm_mix, v_post_norm_mix, v_hgrn_lower_bounds_fwd,
               v_hgrn_lower_bounds_bwd, v_hgrn_out_norm, v_attn_sink, v_pre_norm_ffn2, v_post_norm_ffn2,
               v_rel_bias_table]
    small_g = [dg_pre1, dg_post1, dg_prem, dg_postm, dlb_f, dlb_b, dgain, dsink[:, 0].reshape(1, nah), dg_pre2,
               dg_post2, drel_t.T]
    shapes = [a.shape for a in small_w]
    done = update(5, grad_x)
    done = update(4, done)
    done = update(3, done)
    summed = _all_reduce_small(_pack_rows(small_g + [loss_blk[0:1, 0:1]], d), after=done)
    loss = _unpack_rows(summed, shapes + [(1, 1)], d)[-1][0, 0]
    _, sd, sm, sv = _adamw("adamw_small", _pack_rows(small_w, d), summed, _pack_rows(small_m, d),
                           _pack_rows(small_v, d))
    small_grads = _unpack_rows(summed, shapes, d)
    small_delta, small_new_m, small_new_v = (_unpack_rows(a, shapes, d) for a in (sd, sm, sv))

    tok = reduce[1].chip_finish(sd)
    done = update(1, tok)
    tok = reduce[0].chip_finish(done)
    update(0, tok)

    def ordered(small, big):
        s = dict(zip(["pre1", "post1", "prem", "postm", "lbf", "lbb", "gain", "sink", "pre2", "post2", "rel"], small))
        b = dict(zip(["gu1", "d1", "win", "wout", "gu2", "d2"], big))
        return [s["pre1"], s["post1"], b["gu1"], b["d1"], s["prem"], s["postm"], b["win"], s["lbf"], s["lbb"],
                s["gain"], s["sink"], b["wout"], s["pre2"], s["post2"], b["gu2"], b["d2"], s["rel"]]

    return (loss, grad_x[None], *ordered(small_grads, big_grads), *ordered(small_delta, big_delta),
            *ordered(small_new_m, big_new_m), *ordered(small_new_v, big_new_v))
```

```python
import functools
import math

import jax
import jax.numpy as jnp
import numpy as np
from jax import lax
from jax.experimental import pallas as pl
from jax.experimental.pallas import tpu as pltpu

F32 = jnp.float32
BF16 = jnp.bfloat16

HEAD = 128
CHUNK = 64
WINDOW = 128
SPAN = 3 * WINDOW
KV_HEADS = 2
REL_BUCKETS = 32
REL_MAX_DIST = 128
EPS = 1e-6
NEG_INF = -1e30

ADAM_LR = 0.001
ADAM_B1 = 0.9
ADAM_B2 = 0.999
ADAM_EPS = 1e-08
ADAM_WD = 0.01
ADAM_STEP = 10

N_CHIPS = 4
N_DEV = 8
V7X_VMEM_BYTES = 64 * 1024 * 1024
MESH = pl.DeviceIdType.MESH
ANY = pl.BlockSpec(memory_space=pl.ANY)


def _tile(n, pref, mult):
    t = (min(pref, n) // mult) * mult
    while t >= mult:
        if n % t == 0:
            return t
        t -= mult
    return n


def _params(semantics, block_bytes):
    limit = min(V7X_VMEM_BYTES - (4 << 20), 2 * int(block_bytes) + (8 << 20))
    return pltpu.CompilerParams(dimension_semantics=semantics, vmem_limit_bytes=limit)


def _nbytes(shape, dtype):
    return int(np.prod(shape)) * jnp.dtype(dtype).itemsize


PIN_TO_HBM_BYTES = 4 << 20


def _pallas(body, **kw):
    def pin_shape(s):
        if isinstance(s, jax.ShapeDtypeStruct) and _nbytes(s.shape, s.dtype) >= PIN_TO_HBM_BYTES:
            return pltpu.HBM(s.shape, s.dtype)
        return s

    def pin(a):
        if getattr(a, "dtype", None) in (F32, BF16) and _nbytes(a.shape, a.dtype) >= PIN_TO_HBM_BYTES:
            return pltpu.with_memory_space_constraint(a, pltpu.HBM)
        return a

    out_shape = kw["out_shape"]
    kw["out_shape"] = [pin_shape(s) for s in out_shape] if isinstance(out_shape, (list, tuple)) else pin_shape(out_shape)
    call = pl.pallas_call(body, **kw)
    return lambda *args: call(*[pin(a) for a in args])


def _dot(a, b, ca=1, cb=0):
    return lax.dot_general(a, b, (((ca,), (cb,)), ((), ())), preferred_element_type=F32)


def _split3(x):
    hi = x.astype(BF16)
    r1 = x - hi.astype(F32)
    mid = r1.astype(BF16)
    lo = (r1 - mid.astype(F32)).astype(BF16)
    return hi, mid, lo


def _dot_exact(a, b, ca=1, cb=0, split="b"):
    if split == "b":
        return sum(_dot(a, p, ca, cb) for p in _split3(b))
    return sum(_dot(p, b, ca, cb) for p in _split3(a))


def _rms(x):
    return lax.rsqrt(jnp.mean(x * x, axis=-1, keepdims=True) + EPS)


def _norm_bwd(u, x, gain):
    r = _rms(x)
    xhat = x * r
    dgain = jnp.sum(u * xhat, axis=0, keepdims=True)
    v = u * gain
    dx = r * (v - xhat * jnp.mean(v * xhat, axis=-1, keepdims=True))
    return dx, dgain


def _sigmoid(x):
    return 1.0 / (1.0 + jnp.exp(-x))


def _accumulate(ref, val, first):
    @pl.when(first)
    def _():
        ref[...] = val

    @pl.when(jnp.logical_not(first))
    def _():
        ref[...] += val


def _ordered(body, ins, in_specs, after):
    if after is None:
        return body, list(ins), list(in_specs)
    n_in = len(ins)

    def wrapped(*refs):
        body(*refs[:n_in], *refs[n_in + 1:])

    return wrapped, list(ins) + [after], list(in_specs) + [pl.BlockSpec(memory_space=pl.ANY)]


def _matmul(name, a, b, *, form, out_dtype, tm, tn, tk, a_map=None, b_map=None,
            out_shape=None, out_block=None, out_map=None, sizes=None, after=None):
    if sizes is None:
        if form == "nn":
            (m, k), n = a.shape, b.shape[1]
        elif form == "nt":
            (m, k), n = a.shape, b.shape[0]
        else:
            (k, m), n = a.shape, b.shape[1]
    else:
        m, n, k = sizes
    gi, gj, gk = m // tm, n // tn, k // tk
    a_blk = (tm, tk) if form != "tn" else (tk, tm)
    b_blk = (tk, tn) if form != "nt" else (tn, tk)
    if a_map is None:
        a_map = (lambda i, j, kk: (i, kk)) if form != "tn" else (lambda i, j, kk: (kk, i))
    else:
        a_blk = (None,) + a_blk
    if b_map is None:
        b_map = (lambda i, j, kk: (kk, j)) if form != "nt" else (lambda i, j, kk: (j, kk))
    else:
        b_blk = (None,) + b_blk
    if out_shape is None:
        out_shape, out_block, out_map = (m, n), (tm, tn), (lambda i, j, kk: (i, j))
    ca, cb = {"nn": (1, 0), "nt": (1, 1), "tn": (0, 0)}[form]

    def body(a_ref, b_ref, o_ref, *acc):
        part = _dot(a_ref[...], b_ref[...], ca, cb)
        if gk == 1:
            o_ref[...] = part.astype(o_ref.dtype)
        else:
            kk = pl.program_id(2)
            _accumulate(acc[0], part, kk == 0)

            @pl.when(kk == gk - 1)
            def _():
                o_ref[...] = acc[0][...].astype(o_ref.dtype)

    scratch = [] if gk == 1 else [pltpu.VMEM((tm, tn), F32)]
    vmem = (_nbytes((tm, tk), a.dtype) + _nbytes((tk, tn), b.dtype) + _nbytes((tm, tn), out_dtype)
            + 2 * _nbytes((tm, tn), F32))
    body, ins, in_specs = _ordered(body, [a, b], [pl.BlockSpec(a_blk, a_map), pl.BlockSpec(b_blk, b_map)], after)
    return _pallas(
        body, name=name, grid=(gi, gj, gk), in_specs=in_specs,
        out_specs=pl.BlockSpec(out_block, out_map),
        out_shape=jax.ShapeDtypeStruct(out_shape, out_dtype),
        scratch_shapes=scratch,
        compiler_params=_params(("parallel", "parallel", "arbitrary"), vmem),
    )(*ins)


V7X_HBM_BYTES_PER_US = 3.0e6
V7X_MXU_FLOPS_PER_US = 0.9e9
V7X_VMEM_RMW_BYTES_PER_US = 10e6
GRID_STEP_US = 0.35
MATMUL_VMEM_BUDGET = 40 << 20
MATMUL_MAX_TILE_FLOPS = 1 << 33


def _divisors(n, mult, lo):
    return [t for t in range(mult, n + 1, mult) if n % t == 0 and t >= min(lo, n)]


def _mm_tiles(m, n, k, out_dtype=F32, n_unit=None, k_unit=None):
    out_bytes = jnp.dtype(out_dtype).itemsize
    best = None
    for tm in _divisors(m, 128, 256):
        for tn in _divisors(n_unit or n, 128, 256):
            for tk in _divisors(k_unit or k, 128, 512):
                gi, gj, gk = m // tm, n // tn, k // tk
                vmem = 4 * tm * tk + 4 * tk * tn + 2 * tm * tn * out_bytes + 4 * tm * tn * (2 if gk > 1 else 1)
                if vmem > MATMUL_VMEM_BUDGET or 2 * tm * tn * tk > MATMUL_MAX_TILE_FLOPS:
                    continue
                a_bytes = 2 * m * k * (gj if gk > 1 else 1)
                b_bytes = 2 * k * n * (1 if gj == 1 and gk == 1 else gi)
                hbm_us = (a_bytes + b_bytes + m * n * out_bytes) / V7X_HBM_BYTES_PER_US
                acc_us = (8 * m * n * gk / V7X_VMEM_RMW_BYTES_PER_US) if gk > 1 else 0.0
                cost = max(2 * m * n * k / V7X_MXU_FLOPS_PER_US, 1.3 * hbm_us) + GRID_STEP_US * gi * gj * gk + acc_us
                key = (round(cost, 1), vmem)
                if best is None or key < best[0]:
                    best = (key, (tm, tn, tk))
    return best[1]


def _mm(name, a, b, form, out_dtype, after=None):
    if form == "nn":
        m, k, n = a.shape[0], a.shape[1], b.shape[1]
    elif form == "nt":
        m, k, n = a.shape[0], a.shape[1], b.shape[0]
    else:
        m, k, n = a.shape[1], a.shape[0], b.shape[1]
    tm, tn, tk = _mm_tiles(m, n, k, out_dtype)
    return _matmul(name, a, b, form=form, out_dtype=out_dtype, tm=tm, tn=tn, tk=tk, after=after)


def _row_tile(t):
    return _tile(t, 256, 8)


def _norm_fwd(name, x, gain):
    t, d = x.shape
    tm = _row_tile(t)

    def body(x_ref, g_ref, h_ref):
        xv = x_ref[...]
        h_ref[...] = (xv * _rms(xv) * g_ref[...]).astype(BF16)

    row = pl.BlockSpec((tm, d), lambda i: (i, 0))
    vec = pl.BlockSpec((1, d), lambda i: (0, 0))
    return _pallas(
        body, name=name, grid=(t // tm,), in_specs=[row, vec], out_specs=row,
        out_shape=jax.ShapeDtypeStruct((t, d), BF16),
        compiler_params=_params(("parallel",), 2 * _nbytes((tm, d), F32)),
    )(x, gain)


def _resid_norm_fwd(name, xres, ff, gpost, gpre, scale):
    t, d = xres.shape
    tm = _row_tile(t)

    def body(x_ref, f_ref, gp_ref, gn_ref, xn_ref, h_ref):
        f = f_ref[...]
        xn = x_ref[...] + scale * (f * _rms(f) * gp_ref[...])
        xn_ref[...] = xn
        h_ref[...] = (xn * _rms(xn) * gn_ref[...]).astype(BF16)

    row = pl.BlockSpec((tm, d), lambda i: (i, 0))
    vec = pl.BlockSpec((1, d), lambda i: (0, 0))
    return _pallas(
        body, name=name, grid=(t // tm,), in_specs=[row, row, vec, vec], out_specs=[row, row],
        out_shape=[jax.ShapeDtypeStruct((t, d), F32), jax.ShapeDtypeStruct((t, d), BF16)],
        compiler_params=_params(("parallel",), 4 * _nbytes((tm, d), F32)),
    )(xres, ff, gpost, gpre)


def _final_fwd_bwd(name, xres, ff, gpost, target, scale):
    t, d = xres.shape
    tm = _row_tile(t)

    def body(x_ref, f_ref, gp_ref, t_ref, loss_ref, dy_ref, dff_ref, dg_ref):
        i = pl.program_id(0)
        f = f_ref[...]
        gp = gp_ref[...]
        y = x_ref[...] + scale * (f * _rms(f) * gp)
        err = y - t_ref[...]
        part = 0.5 * jnp.sum(jnp.mean(err * err, axis=-1, keepdims=True), axis=0, keepdims=True)
        _accumulate(loss_ref, jnp.broadcast_to(part, loss_ref.shape), i == 0)
        dy = err / d
        dy_ref[...] = dy
        dff, dg = _norm_bwd(scale * dy, f, gp)
        dff_ref[...] = dff.astype(BF16)
        _accumulate(dg_ref, dg, i == 0)

    row = pl.BlockSpec((tm, d), lambda i: (i, 0))
    vec = pl.BlockSpec((1, d), lambda i: (0, 0))
    return _pallas(
        body, name=name, grid=(t // tm,), in_specs=[row, row, vec, row],
        out_specs=[pl.BlockSpec((8, 128), lambda i: (0, 0)), row, row, vec],
        out_shape=[jax.ShapeDtypeStruct((8, 128), F32), jax.ShapeDtypeStruct((t, d), F32),
                   jax.ShapeDtypeStruct((t, d), BF16), jax.ShapeDtypeStruct((1, d), F32)],
        compiler_params=_params(("arbitrary",), 5 * _nbytes((tm, d), F32)),
    )(xres, ff, gpost, target)


def _norms_bwd(name, dres, dh, xin, gpre, post=None, after=None):
    t, d = dres.shape
    tm = _row_tile(t)
    with_post = post is not None

    def body(*refs):
        if with_post:
            dr_ref, dh_ref, x_ref, g_ref, f_ref, gp_ref, dx_ref, dg_ref, dff_ref, dgp_ref = refs
        else:
            dr_ref, dh_ref, x_ref, g_ref, dx_ref, dg_ref = refs
        i = pl.program_id(0)
        dx, dg = _norm_bwd(dh_ref[...], x_ref[...], g_ref[...])
        dx = dr_ref[...] + dx
        dx_ref[...] = dx
        _accumulate(dg_ref, dg, i == 0)
        if with_post:
            dff, dgp = _norm_bwd(post[2] * dx, f_ref[...], gp_ref[...])
            dff_ref[...] = dff.astype(BF16)
            _accumulate(dgp_ref, dgp, i == 0)

    row = pl.BlockSpec((tm, d), lambda i: (i, 0))
    vec = pl.BlockSpec((1, d), lambda i: (0, 0))
    ins, in_specs = [dres, dh, xin, gpre], [row, row, row, vec]
    out_specs = [row, vec]
    out_shape = [jax.ShapeDtypeStruct((t, d), F32), jax.ShapeDtypeStruct((1, d), F32)]
    if with_post:
        ins += [post[0], post[1]]
        in_specs += [row, vec]
        out_specs += [row, vec]
        out_shape += [jax.ShapeDtypeStruct((t, d), BF16), jax.ShapeDtypeStruct((1, d), F32)]
    body, ins, in_specs = _ordered(body, ins, in_specs, after)
    return _pallas(
        body, name=name, grid=(t // tm,), in_specs=in_specs, out_specs=out_specs, out_shape=out_shape,
        compiler_params=_params(("arbitrary",), 6 * _nbytes((tm, d), F32)),
    )(*ins)


SWIGLU_TILE = (1024, 512)


def _ffn_gate_up_act(name, h, w_gu):
    t, d = h.shape
    f = w_gu.shape[1] // 2
    tm, tn = _tile(t, SWIGLU_TILE[0], 128), _tile(f, SWIGLU_TILE[1], 128)
    nf = f // tn

    def body(h_ref, wg_ref, wu_ref, a_ref, dg_ref, du_ref):
        hv = h_ref[...]
        g = _dot(hv, wg_ref[...])
        u = _dot(hv, wu_ref[...])
        sig = _sigmoid(g)
        silu = g * sig
        a_ref[...] = (silu * u).astype(BF16)
        dg_ref[...] = (u * sig * (1.0 + g * (1.0 - sig))).astype(BF16)
        du_ref[...] = silu.astype(BF16)

    out = jax.ShapeDtypeStruct((t, f), BF16)
    blk = pl.BlockSpec((tm, tn), lambda i, j: (i, j))
    return _pallas(
        body, name=name, grid=(t // tm, nf),
        in_specs=[pl.BlockSpec((tm, d), lambda i, j: (i, 0)), pl.BlockSpec((d, tn), lambda i, j: (0, j)),
                  pl.BlockSpec((d, tn), lambda i, j: (0, j + nf))],
        out_specs=[blk, blk, blk], out_shape=[out, out, out],
        compiler_params=_params(("parallel", "parallel"),
                                _nbytes((tm, d), BF16) + 2 * _nbytes((d, tn), BF16) + 5 * _nbytes((tm, tn), F32)),
    )(h, w_gu, w_gu)


def _ffn_dact(name, dff, w_down, dact_dgate, dact_dup, after=None):
    t, d = dff.shape
    f = w_down.shape[0]
    tm, tn = _tile(t, SWIGLU_TILE[0], 128), _tile(f, SWIGLU_TILE[1], 128)

    def body(d_ref, w_ref, dg_ref, du_ref, o_ref):
        da = _dot(d_ref[...], w_ref[...], 1, 1)
        o_ref[0] = (da * dg_ref[...].astype(F32)).astype(BF16)
        o_ref[1] = (da * du_ref[...].astype(F32)).astype(BF16)

    blk = pl.BlockSpec((tm, tn), lambda i, j: (i, j))
    body, ins, in_specs = _ordered(
        body, [dff, w_down, dact_dgate, dact_dup],
        [pl.BlockSpec((tm, d), lambda i, j: (i, 0)), pl.BlockSpec((tn, d), lambda i, j: (j, 0)), blk, blk], after)
    return _pallas(
        body, name=name, grid=(t // tm, f // tn), in_specs=in_specs,
        out_specs=pl.BlockSpec((2, tm, tn), lambda i, j: (0, i, j)),
        out_shape=jax.ShapeDtypeStruct((2, t, f), BF16),
        compiler_params=_params(("parallel", "parallel"),
                                _nbytes((tm, d), BF16) + _nbytes((tn, d), BF16) + 5 * _nbytes((tm, tn), F32)),
    )(*ins)


def _ffn_dh(name, dgu, w_gu, after=None):
    _, t, f = dgu.shape
    d = w_gu.shape[0]
    tm, tn, tk = _mm_tiles(t, d, 2 * f, F32, k_unit=f)
    nkf = f // tk
    return _matmul(name, dgu, w_gu, form="nt", out_dtype=F32, tm=tm, tn=tn, tk=tk, sizes=(t, d, 2 * f),
                   a_map=lambda i, j, kk: (kk // nkf, i, kk % nkf), after=after)


def _lower_bound(lbp):
    m = jnp.max(lbp, axis=0, keepdims=True)
    e = jnp.exp(lbp - m)
    return e[0:1] / jnp.sum(e, axis=0, keepdims=True)


def _chunk_mask(reverse):
    row = lax.broadcasted_iota(jnp.int32, (CHUNK, CHUNK), 0)
    col = lax.broadcasted_iota(jnp.int32, (CHUNK, CHUNK), 1)
    return (col >= row) if reverse else (col <= row)


def _hgrn_gates(z, lb, mask_bf):
    sig = _sigmoid(z)
    f = lb + (1.0 - lb) * sig
    logf = jnp.log(f)
    k = 1.0 - f
    cum = _dot_exact(mask_bf, logf)
    last = jnp.sum(logf, axis=0, keepdims=True)
    return sig, f, k, cum, last


def _hgrn_scan_fwd(name, p, lbp_f, lbp_b):
    t = p.shape[0]
    hw = lbp_f.shape[1]
    nh, nc = hw // HEAD, t // CHUNK

    def body(qf, vf, zf, qb, vb, zb, lbf, lbb, of_ref, ob_ref, stf_ref, stb_ref, state):
        n = pl.program_id(0)

        @pl.when(n == 0)
        def _():
            state[...] = jnp.zeros_like(state)

        directions = [(qf, vf, zf, lbf, of_ref, stf_ref), (qb, vb, zb, lbb, ob_ref, stb_ref)]
        wide = []
        for d, (q_ref, v_ref, z_ref, lb_ref, o_ref, st_ref) in enumerate(directions):
            mask = _chunk_mask(d == 1)
            lb = _lower_bound(lb_ref[...])
            _, _, k, cum, last = _hgrn_gates(z_ref[...], lb, mask.astype(BF16))
            v = v_ref[...].astype(BF16)
            qd = (q_ref[...] * jnp.exp(cum)).astype(BF16)
            kd = (k * jnp.exp(-cum)).astype(BF16)
            kt = (k * jnp.exp(last - cum)).astype(BF16)
            s_all = state[d]
            st_ref[...] = s_all
            wide.append((mask, v, qd, kd, kt, jnp.exp(last), s_all, o_ref))
        pairs = [(d, slice(h * HEAD, (h + 1) * HEAD)) for d in range(2) for h in range(nh)]
        a = [jnp.where(wide[d][0], _dot(wide[d][2][:, sl], wide[d][3][:, sl], 1, 1), 0.0).astype(BF16)
             for d, sl in pairs]
        inter = [_dot(wide[d][2][:, sl], wide[d][6][:, sl].astype(BF16), 1, 1) for d, sl in pairs]
        intra = [_dot(a[i], wide[d][1][:, sl]) for i, (d, sl) in enumerate(pairs)]
        grow = [_dot(wide[d][1][:, sl], wide[d][4][:, sl], 0, 0) for d, sl in pairs]
        for i, (d, sl) in enumerate(pairs):
            wide[d][7][:, sl] = intra[i] + inter[i]
            state[d, :, sl] = wide[d][6][:, sl] * wide[d][5][:, sl] + grow[i]

    def col(group, reverse):
        return pl.BlockSpec((CHUNK, hw), lambda n: ((nc - 1 - n) if reverse else n, group))

    def st(reverse):
        return pl.BlockSpec((None, HEAD, hw), lambda n: ((nc - 1 - n) if reverse else n, 0, 0))

    lb_spec = pl.BlockSpec((2, hw), lambda n: (0, 0))
    out = jax.ShapeDtypeStruct((t, hw), F32)
    states = jax.ShapeDtypeStruct((nc, HEAD, hw), F32)
    return _pallas(
        body, name=name, grid=(nc,),
        in_specs=[col(0, False), col(1, False), col(2, False), col(0, True), col(1, True), col(3, True),
                  lb_spec, lb_spec],
        out_specs=[col(0, False), col(0, True), st(False), st(True)],
        out_shape=[out, out, states, states],
        scratch_shapes=[pltpu.VMEM((2, HEAD, hw), F32)],
        compiler_params=_params(("arbitrary",), 12 * _nbytes((HEAD, hw), F32)),
    )(p, p, p, p, p, p, lbp_f, lbp_b)


def _hgrn_scan_bwd(name, p, lbp_f, lbp_b, do, st_f, st_b):
    t = p.shape[0]
    hw = lbp_f.shape[1]
    nh, nc = hw // HEAD, t // CHUNK

    def body(qf, vf, zf, dof, sf, qb, vb, zb, dob, sb, lbf, lbb, dqf, dvf, dzf, dlbf, dqb, dvb, dzb, dlbb,
             dstate, dlb_acc, dqd_s, dkd_s, dkt_s, ddec_s):
        n = pl.program_id(0)

        @pl.when(n == 0)
        def _():
            dstate[...] = jnp.zeros_like(dstate)
            dlb_acc[...] = jnp.zeros_like(dlb_acc)

        directions = [(qf, vf, zf, dof, sf, lbf, dqf, dvf, dzf, dlbf), (qb, vb, zb, dob, sb, lbb, dqb, dvb, dzb, dlbb)]
        for d, (q_ref, v_ref, z_ref, do_ref, st_ref, lb_ref, dq_ref, dv_ref, dz_ref, dlb_ref) in enumerate(directions):
            mask = _chunk_mask(d == 1)
            mask_bf = mask.astype(BF16)
            lb = _lower_bound(lb_ref[...])
            sig, f, k, cum, last = _hgrn_gates(z_ref[...], lb, mask_bf)
            e_pos, e_neg, e_tail = jnp.exp(cum), jnp.exp(-cum), jnp.exp(last - cum)
            dec = jnp.exp(last)
            v = v_ref[...].astype(BF16)
            qd, kd, kt = q_ref[...] * e_pos, k * e_neg, k * e_tail
            qd_bf, kd_bf, kt_bf = qd.astype(BF16), kd.astype(BF16), kt.astype(BF16)
            s_all = st_ref[...]
            ds_all = dstate[d]
            dov = do_ref[...].astype(BF16)
            cols = [slice(h * HEAD, (h + 1) * HEAD) for h in range(nh)]
            s_bf = [s_all[:, sl].astype(BF16) for sl in cols]
            ds_bf = [ds_all[:, sl].astype(BF16) for sl in cols]
            a = [jnp.where(mask, _dot(qd_bf[:, sl], kd_bf[:, sl], 1, 1), 0.0).astype(BF16) for sl in cols]
            da = [jnp.where(mask, _dot(dov[:, sl], v[:, sl], 1, 1), 0.0).astype(BF16) for sl in cols]
            dv_h = [_dot(a[h], dov[:, sl], 0, 0) + _dot(kt_bf[:, sl], ds_bf[h], 1, 1) for h, sl in enumerate(cols)]
            dqd_h = [_dot(da[h], kd_bf[:, sl]) + _dot(dov[:, sl], s_bf[h]) for h, sl in enumerate(cols)]
            dkd_h = [_dot(da[h], qd_bf[:, sl], 0, 0) for h, sl in enumerate(cols)]
            dkt_h = [_dot(v[:, sl], ds_bf[h]) for h, sl in enumerate(cols)]
            dst_h = [_dot(dov[:, sl], qd_bf[:, sl], 0, 0) + ds_all[:, sl] * dec[:, sl] for sl in cols]
            for h, sl in enumerate(cols):
                dv_ref[:, sl] = dv_h[h]
                dqd_s[:, sl] = dqd_h[h]
                dkd_s[:, sl] = dkd_h[h]
                dkt_s[:, sl] = dkt_h[h]
                dstate[d, :, sl] = dst_h[h]
                ddec_s[:, sl] = jnp.sum(ds_all[:, sl] * s_all[:, sl], axis=0, keepdims=True)
            dqd, dkd, dkt = dqd_s[...], dkd_s[...], dkt_s[...]
            dlast = jnp.sum(dkt * kt, axis=0, keepdims=True) + dec * ddec_s[...]
            dq_ref[...] = dqd * e_pos
            dk = dkd * e_neg + dkt * e_tail
            dcum = dqd * qd - dkd * kd - dkt * kt
            dlogf = _dot_exact(mask_bf, dcum, 0, 0) + dlast
            df = dlogf / f - dk
            dz_ref[...] = df * (1.0 - lb) * sig * (1.0 - sig)
            dlb_acc[d] += jnp.sum(df * (1.0 - sig), axis=0, keepdims=True)

            @pl.when(n == nc - 1)
            def _():
                g = dlb_acc[d] * lb * (1.0 - lb)
                dlb_ref[0:1, :] = g
                dlb_ref[1:2, :] = -g

    def col(group, reverse):
        return pl.BlockSpec((CHUNK, hw), lambda n: (n if reverse else (nc - 1 - n), group))

    def st(reverse):
        return pl.BlockSpec((None, HEAD, hw), lambda n: (n if reverse else (nc - 1 - n), 0, 0))

    lb_spec = pl.BlockSpec((2, hw), lambda n: (0, 0))
    out = jax.ShapeDtypeStruct((t, hw), F32)
    dlb = jax.ShapeDtypeStruct((2, hw), F32)
    wide = pltpu.VMEM((CHUNK, hw), F32)
    return _pallas(
        body, name=name, grid=(nc,),
        in_specs=[col(0, False), col(1, False), col(2, False), col(0, False), st(False),
                  col(0, True), col(1, True), col(3, True), col(0, True), st(True), lb_spec, lb_spec],
        out_specs=[col(0, False), col(0, False), col(0, False), lb_spec,
                   col(0, True), col(0, True), col(0, True), lb_spec],
        out_shape=[out, out, out, dlb, out, out, out, dlb],
        scratch_shapes=[pltpu.VMEM((2, HEAD, hw), F32), pltpu.VMEM((2, 1, hw), F32), wide, wide, wide,
                        pltpu.VMEM((1, hw), F32)],
        compiler_params=_params(("arbitrary",), 16 * _nbytes((HEAD, hw), F32)),
    )(p, p, p, do, st_f, p, p, p, do, st_b, lbp_f, lbp_b)


def _hgrn_out_fwd(name, o_f, o_b, p, gain, g_group):
    t, hw = o_f.shape
    nh = hw // HEAD
    tm = _tile(t, 512, 8)

    def body(of_ref, ob_ref, g_ref, gain_ref, y_ref):
        o = of_ref[...] + ob_ref[...]
        g = g_ref[...]
        y_ref[...] = (o * _rms(o) * gain_ref[...] * (g * _sigmoid(g))).astype(BF16)

    blk = pl.BlockSpec((tm, HEAD), lambda i, h: (i, h))
    return _pallas(
        body, name=name, grid=(t // tm, nh),
        in_specs=[blk, blk, pl.BlockSpec((tm, HEAD), lambda i, h: (i, g_group * nh + h)),
                  pl.BlockSpec((1, HEAD), lambda i, h: (0, h))],
        out_specs=blk, out_shape=jax.ShapeDtypeStruct((t, hw), BF16),
        compiler_params=_params(("parallel", "parallel"), 1 << 20),
    )(o_f, o_b, p, gain)


def _hgrn_out_bwd(name, dy, o_f, o_b, p, gain, g_group, after=None):
    t, hw = o_f.shape
    nh = hw // HEAD
    tm = _tile(t, 512, 8)

    def body(dy_ref, of_ref, ob_ref, g_ref, gain_ref, do_ref, dg_ref, dgain_ref):
        i = pl.program_id(1)
        o = of_ref[...] + ob_ref[...]
        g = g_ref[...]
        gain_v = gain_ref[...]
        sig = _sigmoid(g)
        dyv = dy_ref[...]
        do, dgain = _norm_bwd(dyv * (g * sig), o, gain_v)
        do_ref[...] = do
        dg_ref[...] = dyv * (o * _rms(o) * gain_v) * sig * (1.0 + g * (1.0 - sig))
        _accumulate(dgain_ref, dgain, i == 0)

    blk = pl.BlockSpec((tm, HEAD), lambda h, i: (i, h))
    vec = pl.BlockSpec((1, HEAD), lambda h, i: (0, h))
    out = jax.ShapeDtypeStruct((t, hw), F32)
    body, ins, in_specs = _ordered(
        body, [dy, o_f, o_b, p, gain],
        [blk, blk, blk, pl.BlockSpec((tm, HEAD), lambda h, i: (i, g_group * nh + h)), vec], after)
    return _pallas(
        body, name=name, grid=(nh, t // tm), in_specs=in_specs,
        out_specs=[blk, blk, vec], out_shape=[out, out, jax.ShapeDtypeStruct((1, hw), F32)],
        compiler_params=_params(("parallel", "arbitrary"), 1 << 20),
    )(*ins)


def _t5_bucket_ids():
    c = np.arange(WINDOW)[:, None]
    s = np.arange(SPAN)[None, :]
    rel = s - WINDOW - c
    nb = REL_BUCKETS // 2
    max_exact = nb // 2
    bucket = (rel > 0).astype(np.int32) * nb
    n = np.abs(rel)
    large = max_exact + (np.log(np.maximum(n, 1) / max_exact) / np.log(REL_MAX_DIST / max_exact)
                         * (nb - max_exact)).astype(np.int32)
    large = np.minimum(large, nb - 1)
    ids = bucket + np.where(n < max_exact, n, large).astype(np.int32)
    return jnp.asarray(ids.reshape(1, WINDOW * SPAN), jnp.int32)


def _bias_onehot(ids_ref):
    n = ids_ref.shape[1]
    return (lax.broadcasted_iota(jnp.int32, (REL_BUCKETS, n), 0) == ids_ref[...]).astype(BF16)


def _bias_gather(name, table_t, ids):
    nh = table_t.shape[0]

    def body(t_ref, ids_ref, o_ref):
        o_ref[...] = _dot_exact(t_ref[...], _bias_onehot(ids_ref), split="a")

    return _pallas(
        body, name=name, out_shape=jax.ShapeDtypeStruct((nh, ids.shape[1]), F32),
        compiler_params=pltpu.CompilerParams(vmem_limit_bytes=32 << 20),
    )(table_t, ids)


def _bias_scatter(name, dbias, ids):
    nh = dbias.shape[0]

    def body(d_ref, ids_ref, o_ref):
        o_ref[...] = _dot_exact(d_ref[...], _bias_onehot(ids_ref), 1, 1, split="a")

    return _pallas(
        body, name=name, out_shape=jax.ShapeDtypeStruct((nh, REL_BUCKETS), F32),
        compiler_params=pltpu.CompilerParams(vmem_limit_bytes=32 << 20),
    )(dbias, ids)


def _attn_valid(i, t):
    c = lax.broadcasted_iota(jnp.int32, (WINDOW, SPAN), 0)
    s = lax.broadcasted_iota(jnp.int32, (WINDOW, SPAN), 1)
    rel = s - WINDOW - c
    pos = i * WINDOW - WINDOW + s
    return (jnp.abs(rel) <= WINDOW) & (pos >= 0) & (pos < t)


def _attn_probs(qs, khs, b_ref, s_ref, valid):
    heads = range(len(qs))
    sinks = [s_ref[0:1, h:h + 1] for h in heads]
    s = [_dot(qs[h], khs[h], 1, 1) / math.sqrt(HEAD) for h in heads]
    s = [jnp.where(valid, s[h] + b_ref[h], NEG_INF) for h in heads]
    m = [jnp.maximum(jnp.max(s[h], axis=-1, keepdims=True), sinks[h]) for h in heads]
    e = [jnp.exp(s[h] - m[h]) for h in heads]
    es = [jnp.exp(sinks[h] - m[h]) for h in heads]
    inv = [1.0 / (jnp.sum(e[h], axis=-1, keepdims=True) + es[h]) for h in heads]
    return [e[h] * inv[h] for h in heads], [es[h] * inv[h] for h in heads]


def _attn_fwd(name, p, k_pad, v_pad, bias, sink, q_group_blk):
    t = p.shape[0]
    nh = bias.shape[0]
    aw = nh * HEAD
    grp = nh // KV_HEADS
    nb = t // WINDOW

    def body(q_ref, k_ref, v_ref, b_ref, s_ref, y_ref):
        i = pl.program_id(0)
        valid = _attn_valid(i, t)
        start = pl.multiple_of(i * WINDOW, WINDOW)
        ks = k_ref[pl.ds(start, SPAN), :]
        vs = v_ref[pl.ds(start, SPAN), :]
        heads = range(nh)
        col = lambda h: slice(h * HEAD, (h + 1) * HEAD)
        qs = [q_ref[:, col(h)].astype(BF16) for h in heads]
        pr, _ = _attn_probs(qs, [ks[:, col(h // grp)] for h in heads], b_ref, s_ref, valid)
        out = [_dot(pr[h].astype(BF16), vs[:, col(h // grp)]) for h in heads]
        for h in heads:
            y_ref[:, col(h)] = out[h].astype(BF16)

    full = lambda a: pl.BlockSpec(a.shape, lambda i: (0,) * a.ndim)
    return _pallas(
        body, name=name, grid=(nb,),
        in_specs=[pl.BlockSpec((WINDOW, aw), lambda i: (i, q_group_blk)), full(k_pad), full(v_pad), full(bias),
                  full(sink)],
        out_specs=pl.BlockSpec((WINDOW, aw), lambda i: (i, 0)),
        out_shape=jax.ShapeDtypeStruct((t, aw), BF16),
        compiler_params=_params(("parallel",), _nbytes(k_pad.shape, BF16) * 2 + _nbytes(bias.shape, F32)),
    )(p, k_pad, v_pad, bias, sink)


def _attn_bwd(name, p, k_pad, v_pad, bias, sink, dy, q_group_blk, dy_blk, after=None):
    t = p.shape[0]
    nh = bias.shape[0]
    aw = nh * HEAD
    grp = nh // KV_HEADS
    nb = t // WINDOW
    kvw = k_pad.shape[1]

    def body(q_ref, k_ref, v_ref, b_ref, s_ref, dy_ref, dq_ref, dk_ref, dv_ref, db_ref, ds_ref):
        i = pl.program_id(0)

        @pl.when(i == 0)
        def _():
            dk_ref[...] = jnp.zeros_like(dk_ref)
            dv_ref[...] = jnp.zeros_like(dv_ref)
            db_ref[...] = jnp.zeros_like(db_ref)
            ds_ref[...] = jnp.zeros_like(ds_ref)

        valid = _attn_valid(i, t)
        start = pl.multiple_of(i * WINDOW, WINDOW)
        ks = k_ref[pl.ds(start, SPAN), :]
        vs = v_ref[pl.ds(start, SPAN), :]
        inv_sqrt = 1.0 / math.sqrt(HEAD)
        heads = range(nh)
        col = lambda h: slice(h * HEAD, (h + 1) * HEAD)
        qs = [q_ref[:, col(h)].astype(BF16) for h in heads]
        khs = [ks[:, col(h // grp)] for h in heads]
        pr, ps = _attn_probs(qs, khs, b_ref, s_ref, valid)
        dos = [dy_ref[:, col(h)].astype(BF16) for h in heads]
        dp = [_dot(dos[h], vs[:, col(h // grp)], 1, 1) for h in heads]
        delta = [jnp.sum(pr[h] * dp[h], axis=-1, keepdims=True) for h in heads]
        dsc = [pr[h] * (dp[h] - delta[h]) for h in heads]
        dsr = [(dsc[h] * inv_sqrt).astype(BF16) for h in heads]
        dq = [_dot(dsr[h], khs[h]) for h in heads]
        dk = [_dot(dsr[h], qs[h], 0, 0) for h in heads]
        dv = [_dot(pr[h].astype(BF16), dos[h], 0, 0) for h in heads]
        for h in heads:
            db_ref[h] += dsc[h]
            ds_ref[h:h + 1, :] += jnp.broadcast_to(jnp.sum(-ps[h] * delta[h], axis=0, keepdims=True), (1, 128))
            dq_ref[:, col(h)] = dq[h]
        for kv in range(KV_HEADS):
            group = range(kv * grp, (kv + 1) * grp)
            dk_ref[pl.ds(start, SPAN), col(kv)] += sum(dk[h] for h in group)
            dv_ref[pl.ds(start, SPAN), col(kv)] += sum(dv[h] for h in group)

    full = lambda a: pl.BlockSpec(a.shape, lambda i: (0,) * a.ndim)
    whole = lambda shape: pl.BlockSpec(shape, lambda i: (0,) * len(shape))
    pad_shape = (t + 2 * WINDOW, kvw)
    body, ins, in_specs = _ordered(
        body, [p, k_pad, v_pad, bias, sink, dy],
        [pl.BlockSpec((WINDOW, aw), lambda i: (i, q_group_blk)), full(k_pad), full(v_pad), full(bias), full(sink),
         pl.BlockSpec((WINDOW, aw), lambda i: (i, dy_blk))], after)
    return _pallas(
        body, name=name, grid=(nb,), in_specs=in_specs,
        out_specs=[pl.BlockSpec((WINDOW, aw), lambda i: (i, 0)), whole(pad_shape), whole(pad_shape),
                   whole(bias.shape), whole((nh, 128))],
        out_shape=[jax.ShapeDtypeStruct((t, aw), F32), jax.ShapeDtypeStruct(pad_shape, F32),
                   jax.ShapeDtypeStruct(pad_shape, F32), jax.ShapeDtypeStruct(bias.shape, F32),
                   jax.ShapeDtypeStruct((nh, 128), F32)],
        compiler_params=_params(("arbitrary",), 3 * _nbytes(pad_shape, F32) + 2 * _nbytes(bias.shape, F32)),
    )(*ins)


def _pad_kv(name, p, kv_blk, kvw):
    t = p.shape[0]
    nb = t // WINDOW

    def body(x_ref, o_ref):
        i = pl.program_id(0)
        inside = jnp.logical_and(i >= 1, i <= nb)
        o_ref[...] = jnp.where(inside, x_ref[...], 0.0).astype(BF16)

    return _pallas(
        body, name=name, grid=(nb + 2,),
        in_specs=[pl.BlockSpec((WINDOW, kvw), lambda i: (jnp.clip(i - 1, 0, nb - 1), kv_blk))],
        out_specs=pl.BlockSpec((WINDOW, kvw), lambda i: (i, 0)),
        out_shape=jax.ShapeDtypeStruct((t + 2 * WINDOW, kvw), BF16),
        compiler_params=_params(("parallel",), 1 << 20),
    )(p)


def _mix_dproj(name, pieces, kv_pads, t, after=None):
    hw = pieces[0][0].shape[1]
    kvw = kv_pads[0].shape[1]
    widths = [hw] * len(pieces) + [kvw] * len(kv_pads)
    total = sum(widths)
    tm = WINDOW
    flat = [a for pc in pieces for a in pc]

    def body(*refs):
        o_ref = refs[-1]
        pos, off = 0, 0
        for pc in pieces:
            val = refs[pos][...]
            for extra in range(1, len(pc)):
                val = val + refs[pos + extra][...]
            o_ref[:, off:off + hw] = val.astype(BF16)
            pos += len(pc)
            off += hw
        for _ in kv_pads:
            o_ref[:, off:off + kvw] = refs[pos][...].astype(BF16)
            pos += 1
            off += kvw

    in_specs = [pl.BlockSpec((tm, hw), lambda i: (i, 0)) for _ in flat]
    in_specs += [pl.BlockSpec((tm, kvw), lambda i: (i + 1, 0)) for _ in kv_pads]
    body, ins, in_specs = _ordered(body, [*flat, *kv_pads], in_specs, after)
    return _pallas(
        body, name=name, grid=(t // tm,), in_specs=in_specs,
        out_specs=pl.BlockSpec((tm, total), lambda i: (i, 0)),
        out_shape=jax.ShapeDtypeStruct((t, total), BF16),
        compiler_params=_params(("parallel",), 3 * _nbytes((tm, total), F32)),
    )(*ins)


def _concat_cols(name, a, b):
    t, wa = a.shape
    wb = b.shape[1]
    tm = _tile(t, 512, 16)

    def body(a_ref, b_ref, o_ref):
        o_ref[:, :wa] = a_ref[...]
        o_ref[:, wa:] = b_ref[...]

    return _pallas(
        body, name=name, grid=(t // tm,),
        in_specs=[pl.BlockSpec((tm, wa), lambda i: (i, 0)), pl.BlockSpec((tm, wb), lambda i: (i, 0))],
        out_specs=pl.BlockSpec((tm, wa + wb), lambda i: (i, 0)),
        out_shape=jax.ShapeDtypeStruct((t, wa + wb), a.dtype),
        compiler_params=_params(("parallel",), 2 * _nbytes((tm, wa + wb), a.dtype)),
    )(a, b)


def _cast_into_full(name, w, geom, idx, after=None):
    r, c = w.shape
    tr = _tile(r, 256, 16)
    nr = r // tr
    if geom.col:
        place = lambda i, iref: (i, iref[0])
    else:
        place = lambda i, iref: (iref[0] * nr + i, 0)

    def body(i_ref, w_ref, *rest):
        rest[-1][...] = w_ref[...].astype(BF16)

    in_specs = [pl.BlockSpec((tr, c), lambda i, iref: (i, 0))]
    ins = [w]
    if after is not None:
        in_specs.append(pl.BlockSpec(memory_space=pl.ANY))
        ins.append(after)
    return _pallas(
        body, name=name,
        grid_spec=pltpu.PrefetchScalarGridSpec(
            num_scalar_prefetch=1, grid=(nr,), in_specs=in_specs, out_specs=pl.BlockSpec((tr, c), place)),
        out_shape=pltpu.HBM(geom.full_shape, BF16),
        compiler_params=_params(("parallel",), 2 * _nbytes((tr, c), F32)),
    )(idx, *ins)


def _local_result(name, a):
    def body(a_ref, o_ref):
        pass

    return _pallas(body, name=name, in_specs=[ANY], out_specs=ANY, out_shape=pltpu.HBM(a.shape, a.dtype),
                   input_output_aliases={0: 0})(a)


def _adamw(name, w, g, m, v):
    r, c = w.shape
    tr = _tile(r, 128, 8)
    bc1 = 1.0 - ADAM_B1 ** ADAM_STEP
    bc2 = 1.0 - ADAM_B2 ** ADAM_STEP

    def body(w_ref, g_ref, m_ref, v_ref, d_ref, nm_ref, nv_ref):
        gv = g_ref[...]
        nm = ADAM_B1 * m_ref[...] + (1.0 - ADAM_B1) * gv
        nv = ADAM_B2 * v_ref[...] + (1.0 - ADAM_B2) * (gv * gv)
        nm_ref[...] = nm
        nv_ref[...] = nv
        d_ref[...] = -ADAM_LR * ((nm / bc1) / (jnp.sqrt(nv / bc2) + ADAM_EPS) + ADAM_WD * w_ref[...])

    blk = pl.BlockSpec((tr, c), lambda i: (i, 0))
    out = jax.ShapeDtypeStruct((r, c), F32)
    return _pallas(
        body, name=name, grid=(r // tr,), in_specs=[blk] * 4, out_specs=[blk] * 3, out_shape=[out] * 3,
        compiler_params=_params(("parallel",), 7 * _nbytes((tr, c), F32)),
    )(w, g, m, v)


def _mesh_pos():
    return lax.axis_index("x"), lax.axis_index("y"), lax.axis_index("c")


def _other_chips(x, y):
    return [(1 - x, y), (x, 1 - y), (1 - x, 1 - y)]


class _Big:
    def __init__(self, shard_shape, col_sharded):
        self.col = col_sharded
        r, c = shard_shape
        self.shard_shape = (r, c)
        self.full_shape = (r, N_CHIPS * c) if col_sharded else (N_CHIPS * r, c)
        self.half_shape = (r // 2, N_CHIPS * c) if col_sharded else (N_CHIPS * r, c // 2)
        self.shard_half_shape = (r // 2, c) if col_sharded else (r, c // 2)

    def region(self, ref, s, half=None):
        r, c = self.shard_shape
        if self.col:
            rows = slice(None) if half is None else pl.ds(half * (r // 2), r // 2)
            return ref.at[rows, pl.ds(s * c, c)]
        cols = slice(None) if half is None else pl.ds(half * (c // 2), c // 2)
        return ref.at[pl.ds(s * r, r), cols]

    def n_halves(self, ref, half, n):
        r, c = self.shard_shape
        if self.col:
            return ref.at[pl.ds(half * (r // 2), r // 2), pl.ds(0, n * c)]
        return ref.at[pl.ds(0, n * r), pl.ds(half * (c // 2), c // 2)]

    def three_halves(self, ref, half):
        return self.n_halves(ref, half, 3)

    def sub_half(self, ref, s, half, j):
        r, c = self.shard_shape
        if self.col:
            return ref.at[pl.ds(half * (r // 2) + j * (r // 4), r // 4), pl.ds(s * c, c)]
        return ref.at[pl.ds(s * r + j * (r // 2), r // 2), pl.ds(half * (c // 2), c // 2)]

    def half_of_full(self, ref, half):
        r, c = self.full_shape
        if self.col:
            return ref.at[pl.ds(half * (r // 2), r // 2), :]
        return ref.at[:, pl.ds(half * (c // 2), c // 2)]

    def half_of_shard(self, ref, half):
        r, c = self.shard_shape
        if self.col:
            return ref.at[pl.ds(half * (r // 2), r // 2), :]
        return ref.at[:, pl.ds(half * (c // 2), c // 2)]

    def shard_of_half(self, ref, s):
        r, c = self.shard_shape
        if self.col:
            return ref.at[:, pl.ds(s * c, c)]
        return ref.at[pl.ds(s * r, r), :]


HBM =pl.BlockSpec(memory_space=pltpu.HBM)
SEM = pl.BlockSpec(memory_space=pltpu.SEMAPHORE)
SPLIT_COPY = pltpu.CompilerParams(has_side_effects=pltpu.SideEffectType.DATAFLOW_SIDE_EFFECTING)


def _in_hbm(a):
    return pltpu.with_memory_space_constraint(a, pltpu.HBM)


def _gather_start(name, fulls, geoms, after):
    nw = len(fulls)

    def body(*refs):
        dst = refs[nw + 1:2 * nw + 1]
        sems = refs[2 * nw + 1:-1]
        x, y, c = _mesh_pos()
        mine = 2 * x + y
        for w in range(nw):
            own_half = geoms[w].region(dst[w], mine, c)
            for chip in _other_chips(x, y):
                pltpu.make_async_remote_copy(src_ref=own_half, dst_ref=own_half, send_sem=sems[2 * w],
                                             recv_sem=sems[2 * w + 1], device_id=(*chip, c),
                                             device_id_type=MESH).start()
        refs[-1][...] = jnp.zeros_like(refs[-1])

    out = _pallas(
        body, name=name, in_specs=[HBM] * nw + [pl.BlockSpec(memory_space=pl.ANY)],
        out_specs=[HBM] * nw + [SEM] * (2 * nw) + [pl.BlockSpec(memory_space=pltpu.VMEM)],
        out_shape=[pltpu.HBM(g.full_shape, BF16) for g in geoms] + [pltpu.SemaphoreType.DMA(())] * (2 * nw)
        + [jax.ShapeDtypeStruct((8, 128), F32)],
        input_output_aliases={w: w for w in range(nw)}, compiler_params=SPLIT_COPY,
    )(*[_in_hbm(a) for a in fulls], after)
    return list(out[:nw]), [(out[nw + 2 * w], out[nw + 2 * w + 1]) for w in range(nw)], out[-1]


def _gather_first_direct(full, geom):
    def start(refs, _, new):
        x, y, c = _mesh_pos()
        own = geom.region(refs[0], 2 * x + y, c)
        for chip in ((1 - x, y), (x, 1 - y)):
            _remote(own, own, new, (*chip, c)).start()

    return _split_copy_call("gather_first_direct", [full], start, new_sems=2)


def _gather_first_relay(full, geom, sems, after):
    def relay(refs, got, new):
        x, y, c = _mesh_pos()
        w = refs[0]
        two = geom.n_halves(w, c, 2)
        _remote(two, two, got, (x, y, 1 - c)).wait_recv()
        from_x = geom.sub_half(w, 2 * (1 - x) + y, c, 0)
        from_y = geom.sub_half(w, 2 * x + (1 - y), c, 1)
        _remote(from_x, from_x, new, (x, 1 - y, c)).start()
        _remote(from_y, from_y, new, (1 - x, y, c)).start()
        _remote(two, two, got, (x, y, 1 - c)).wait_send()

    return _split_copy_call("gather_first_relay", [full], relay, sems=sems, after=after, new_sems=2)


def _gather_forward(name, full, geom, sems, after, arrivals=3, only_diagonal=False):
    def body(w_in, send_sem, recv_sem, after_ref, w_ref, fwd_send, fwd_recv):
        x, y, c = _mesh_pos()
        sibling = (x, y, 1 - c)
        landed_all = geom.n_halves(w_ref, c, arrivals)
        _remote(landed_all, landed_all, (send_sem, recv_sem), sibling).wait_recv()
        for chip in _other_chips(x, y)[2 if only_diagonal else 0:]:
            landed = geom.region(w_ref, 2 * chip[0] + chip[1], c)
            pltpu.make_async_remote_copy(src_ref=landed, dst_ref=landed, send_sem=fwd_send, recv_sem=fwd_recv,
                                         device_id=sibling, device_id_type=MESH).start()
        _remote(landed_all, landed_all, (send_sem, recv_sem), sibling).wait_send()

    sem = pltpu.SemaphoreType.DMA(())
    out = _pallas(
        body, name=name, in_specs=[HBM, SEM, SEM, pl.BlockSpec(memory_space=pl.ANY)], out_specs=[HBM, SEM, SEM],
        out_shape=[pltpu.HBM(geom.full_shape, BF16), sem, sem],
        input_output_aliases={0: 0}, compiler_params=SPLIT_COPY,
    )(full, sems[0], sems[1], after)
    return out[0], (out[1], out[2])


def _gather_end(name, full, geom, sems, after, halves=3):
    def body(w_in, fwd_send, fwd_recv, after_ref, w_ref):
        x, y, c = _mesh_pos()
        sibling = (x, y, 1 - c)
        theirs, ours = geom.n_halves(w_ref, 1 - c, halves), geom.n_halves(w_ref, c, halves)
        _remote(theirs, theirs, (fwd_send, fwd_recv), sibling).wait_recv()
        _remote(ours, ours, (fwd_send, fwd_recv), sibling).wait_send()

    return _pallas(
        body, name=name, in_specs=[HBM, SEM, SEM, pl.BlockSpec(memory_space=pl.ANY)], out_specs=HBM,
        out_shape=pltpu.HBM(geom.full_shape, BF16),
        input_output_aliases={0: 0}, compiler_params=SPLIT_COPY,
    )(full, sems[0], sems[1], after)


def _split_copy_call(name, arrays, fn, sems=(), after=None, new_sems=0):
    n, ns = len(arrays), len(sems)
    n_in = n + ns + (after is not None)

    def body(*refs):
        fn(refs[n_in:n_in + n], refs[n:n + ns], refs[n_in + n:-1])
        refs[-1][...] = jnp.zeros_like(refs[-1])

    ins = list(arrays) if ns else [_in_hbm(a) for a in arrays]
    ins += list(sems) + ([after] if after is not None else [])
    in_specs = [HBM] * n + [SEM] * ns + ([pl.BlockSpec(memory_space=pl.ANY)] if after is not None else [])
    out = _pallas(
        body, name=name, in_specs=in_specs,
        out_specs=[HBM] * n + [SEM] * new_sems + [pl.BlockSpec(memory_space=pltpu.VMEM)],
        out_shape=[pltpu.HBM(a.shape, a.dtype) for a in arrays] + [pltpu.SemaphoreType.DMA(())] * new_sems
        + [jax.ShapeDtypeStruct((8, 128), F32)],
        input_output_aliases={i: i for i in range(n)}, compiler_params=SPLIT_COPY,
    )(*ins)
    return list(out[:n]), tuple(out[n:-1]), out[-1]


def _remote(src, dst, sems, to):
    return pltpu.make_async_remote_copy(src_ref=src, dst_ref=dst, send_sem=sems[0], recv_sem=sems[1],
                                        device_id=to, device_id_type=MESH)


class _GradReduce:
    def __init__(self, name, geom, idx, c_idx):
        self.name, self.geom, self.idx, self.c_idx = name, geom, idx, c_idx

    def pair_start(self, theirs):
        g = self.geom

        def start(refs, _, new):
            x, y, c = _mesh_pos()
            _remote(refs[0], refs[1], new, (x, y, 1 - c)).start()

        self.arrays, self.sems, token = _split_copy_call(
            f"pair_start_{self.name}", [theirs, lax.empty(g.half_shape, BF16)], start, new_sems=2)
        return token

    def pair_wait(self, after):
        def wait(refs, sems, _):
            x, y, c = _mesh_pos()
            copy = _remote(refs[0], refs[1], sems, (x, y, 1 - c))
            copy.wait_send()
            copy.wait_recv()

        (_, landed), _, _ = _split_copy_call(f"pair_wait_{self.name}", self.arrays, wait, self.sems, after)
        return landed

    def chip_start(self, half):
        g = self.geom

        def start(refs, _, new):
            x, y, c = _mesh_pos()
            for k, chip in enumerate(_other_chips(x, y)):
                _remote(g.shard_of_half(refs[0], 2 * chip[0] + chip[1]), refs[1].at[k], new, (*chip, c)).start()

        self.arrays, self.sems, token = _split_copy_call(
            f"chip_start_{self.name}", [half, lax.empty((3,) + g.shard_half_shape, BF16)], start, new_sems=2)
        return token

    def chip_finish(self, after):
        g = self.geom

        def wait(refs, sems, _):
            x, y, c = _mesh_pos()
            three = _remote(refs[1], refs[1], sems, (x, y, 1 - c))
            three.wait_send()
            three.wait_recv()

        (half, landed), _, _ = _split_copy_call(f"chip_wait_{self.name}", self.arrays, wait, self.sems, after)
        quarter = _chip_add(f"chip_add_{self.name}", half, landed, g, self.idx)

        def start(refs, _, new):
            x, y, c = _mesh_pos()
            own = g.half_of_shard(refs[0], c)
            _remote(own, own, new, (x, y, 1 - c)).start()

        self.arrays, self.sems, token = _split_copy_call(f"share_start_{self.name}", [quarter], start, new_sems=2)
        return token

    def finish(self, after):
        g = self.geom

        def wait(refs, sems, _):
            x, y, c = _mesh_pos()
            own, theirs = g.half_of_shard(refs[0], c), g.half_of_shard(refs[0], 1 - c)
            _remote(own, own, sems, (x, y, 1 - c)).wait_send()
            _remote(theirs, theirs, sems, (x, y, 1 - c)).wait_recv()

        (quarter,), _, _ = _split_copy_call(f"share_wait_{self.name}", self.arrays, wait, self.sems, after)
        return quarter


def _dw_half(name, x, dy, geom, c_idx, own, addend=None, after=None):
    stacked = dy.ndim == 3
    t, m = x.shape
    n = 2 * dy.shape[2] if stacked else dy.shape[1]
    hm, hn = (m // 2, n) if geom.col else (m, n // 2)
    tm, tn, tk = _mm_tiles(hm, hn, t, BF16, n_unit=(n // 2 if stacked else None))
    if tk != t:
        tm, tn = _tile(hm, 512, 128), _tile(hn // (2 if stacked else 1), 512, 128)
    gi, gj = hm // tm, hn // tn
    nf = (n // 2) // tn

    def sel(cref):
        return cref[0] if own else 1 - cref[0]

    a_map = (lambda i, j, cref: (0, sel(cref) * gi + i)) if geom.col else (lambda i, j, cref: (0, i))
    if stacked:
        b_blk, b_map = (None, t, tn), (lambda i, j, cref: (j // nf, 0, j % nf))
    elif geom.col:
        b_blk, b_map = (t, tn), (lambda i, j, cref: (0, j))
    else:
        b_blk, b_map = (t, tn), (lambda i, j, cref: (0, sel(cref) * gj + j))
    out_blk = pl.BlockSpec((tm, tn), lambda i, j, cref: (i, j))
    ins, in_specs = [x, dy], [pl.BlockSpec((t, tm), a_map), pl.BlockSpec(b_blk, b_map)]
    if addend is not None:
        ins.append(addend)
        in_specs.append(out_blk)
    if after is not None:
        ins.append(after)
        in_specs.append(pl.BlockSpec(memory_space=pl.ANY))

    def body(c_ref, *refs):
        acc = _dot(refs[0][...], refs[1][...], 0, 0)
        if addend is not None:
            acc = acc + refs[2][...].astype(F32)
        refs[len(ins)][...] = acc.astype(BF16)

    return _pallas(
        body, name=name,
        grid_spec=pltpu.PrefetchScalarGridSpec(num_scalar_prefetch=1, grid=(gi, gj), in_specs=in_specs,
                                               out_specs=out_blk),
        out_shape=jax.ShapeDtypeStruct((hm, hn), BF16),
        compiler_params=_params(("parallel", "parallel"),
                                _nbytes((t, tm), BF16) + _nbytes((t, tn), BF16) + 3 * _nbytes((tm, tn), F32)),
    )(c_idx, *ins)


def _chip_add(name, half, recv, geom, idx):
    r, c = geom.shard_half_shape
    tr, tc = _tile(r, 512, 16), _tile(c, 2048, 128)
    nr, ncol = r // tr, c // tc
    if geom.col:
        mine = lambda i, j, iref: (i, iref[0] * ncol + j)
        place = lambda i, j, iref: (iref[1] * nr + i, j)
    else:
        mine = lambda i, j, iref: (iref[0] * nr + i, j)
        place = lambda i, j, iref: (i, iref[1] * ncol + j)

    def body(i_ref, h_ref, r_ref, o_ref):
        acc = h_ref[...].astype(F32)
        for k in range(3):
            acc = acc + r_ref[k].astype(F32)
        o_ref[...] = acc

    return _pallas(
        body, name=name,
        grid_spec=pltpu.PrefetchScalarGridSpec(
            num_scalar_prefetch=1, grid=(nr, ncol),
            in_specs=[pl.BlockSpec((tr, tc), mine), pl.BlockSpec((3, tr, tc), lambda i, j, iref: (0, i, j))],
            out_specs=pl.BlockSpec((tr, tc), place)),
        out_shape=jax.ShapeDtypeStruct(geom.shard_shape, F32),
        compiler_params=_params(("parallel", "parallel"), 4 * _nbytes((tr, tc), F32)),
    )(idx, half, recv)


def _all_reduce_small(pack, after=None):
    r, d = pack.shape

    def body(p_ref, o_ref, slots, send_sems, recv_sems):
        x, y, c = _mesh_pos()
        me = 4 * x + 2 * y + c
        slots[me] = p_ref[...]
        copies = []
        for k in range(1, N_DEV):
            px, py, pc = x ^ ((k >> 2) & 1), y ^ ((k >> 1) & 1), c ^ (k & 1)
            copies.append(pltpu.make_async_remote_copy(
                src_ref=p_ref, dst_ref=slots.at[me], send_sem=send_sems.at[k - 1], recv_sem=recv_sems.at[k - 1],
                device_id=(px, py, pc), device_id_type=MESH))
        for cp in copies:
            cp.start()
        for k in range(1, N_DEV):
            peer = 4 * (x ^ ((k >> 2) & 1)) + 2 * (y ^ ((k >> 1) & 1)) + (c ^ (k & 1))
            pltpu.make_async_remote_copy(
                src_ref=p_ref, dst_ref=slots.at[peer], send_sem=send_sems.at[k - 1], recv_sem=recv_sems.at[k - 1],
                device_id=(x, y, c), device_id_type=MESH).wait_recv()
        for cp in copies:
            cp.wait_send()
        acc = slots[0]
        for k in range(1, N_DEV):
            acc = acc + slots[k]
        o_ref[...] = acc

    vm = pl.BlockSpec(memory_space=pltpu.VMEM)
    body, ins, in_specs = _ordered(body, [pack], [vm], after)
    return _pallas(
        body, name="all_reduce_small", in_specs=in_specs, out_specs=vm,
        out_shape=jax.ShapeDtypeStruct((r, d), F32),
        scratch_shapes=[pltpu.VMEM((N_DEV, r, d), F32), pltpu.SemaphoreType.DMA((N_DEV - 1,)),
                        pltpu.SemaphoreType.DMA((N_DEV - 1,))],
    )(*ins)


def _pack_rows(rows, d):
    out = []
    for a in rows:
        flat = a.reshape(-1)
        n = -(-flat.shape[0] // d) * d
        out.append(jnp.pad(flat, (0, n - flat.shape[0])).reshape(-1, d))
    packed = jnp.concatenate(out, axis=0)
    return jnp.pad(packed, ((0, 16 - packed.shape[0]), (0, 0)))


def _unpack_rows(packed, shapes, d):
    out, row = [], 0
    for shp in shapes:
        n = int(np.prod(shp))
        nrows = -(-n // d)
        out.append(packed[row:row + nrows].reshape(-1)[:n].reshape(shp))
        row += nrows
    return out


def kernel(x, pre_norm_ffn1, post_norm_ffn1, w_ffn1_gate_up, w_ffn1_down, pre_norm_mix, post_norm_mix, w_mix_in, hgrn_lower_bounds_fwd, hgrn_lower_bounds_bwd, hgrn_out_norm, attn_sink, w_mix_out, pre_norm_ffn2, post_norm_ffn2, w_ffn2_gate_up, w_ffn2_down, rel_bias_table, loss_target, m_pre_norm_ffn1, m_post_norm_ffn1, m_w_ffn1_gate_up, m_w_ffn1_down, m_pre_norm_mix, m_post_norm_mix, m_w_mix_in, m_hgrn_lower_bounds_fwd, m_hgrn_lower_bounds_bwd, m_hgrn_out_norm, m_attn_sink, m_w_mix_out, m_pre_norm_ffn2, m_post_norm_ffn2, m_w_ffn2_gate_up, m_w_ffn2_down, m_rel_bias_table, v_pre_norm_ffn1, v_post_norm_ffn1, v_w_ffn1_gate_up, v_w_ffn1_down, v_pre_norm_mix, v_post_norm_mix, v_w_mix_in, v_hgrn_lower_bounds_fwd, v_hgrn_lower_bounds_bwd, v_hgrn_out_norm, v_attn_sink, v_w_mix_out, v_pre_norm_ffn2, v_post_norm_ffn2, v_w_ffn2_gate_up, v_w_ffn2_down, v_rel_bias_table):
    t, d = x.shape[1], x.shape[2]
    hw = hgrn_out_norm.shape[1]
    aw = d - hw
    nah = aw // HEAD
    kvw = KV_HEADS * HEAD
    x0 = x[0]
    target = loss_target[0]

    big_names = ["w_ffn1_gate_up", "w_ffn1_down", "w_mix_in", "w_mix_out", "w_ffn2_gate_up", "w_ffn2_down"]
    big_w = [w_ffn1_gate_up[0], w_ffn1_down[0], w_mix_in[0], w_mix_out[0], w_ffn2_gate_up[0], w_ffn2_down[0]]
    big_m = [m_w_ffn1_gate_up[0], m_w_ffn1_down[0], m_w_mix_in[0], m_w_mix_out[0], m_w_ffn2_gate_up[0],
             m_w_ffn2_down[0]]
    big_v = [v_w_ffn1_gate_up[0], v_w_ffn1_down[0], v_w_mix_in[0], v_w_mix_out[0], v_w_ffn2_gate_up[0],
             v_w_ffn2_down[0]]
    col_sharded = [True, False, True, False, True, False]
    geoms = [_Big(w.shape, cs) for w, cs in zip(big_w, col_sharded)]

    cx, cy, cc = _mesh_pos()
    idx = jnp.stack([2 * cx + cy, cc]).astype(jnp.int32)
    c_idx = jnp.reshape(cc, (1,)).astype(jnp.int32)
    first = _cast_into_full(f"cast_{big_names[0]}", big_w[0], geoms[0], idx)
    (first,), direct_sems, tok = _gather_first_direct(first, geoms[0])
    rest = []
    for n, w, gm in zip(big_names[1:], big_w[1:], geoms[1:]):
        tok = _cast_into_full(f"cast_{n}", w, gm, idx, after=tok)
        rest.append(tok)
    (first,), relay_sems, tok = _gather_first_relay(first, geoms[0], direct_sems, after=tok)
    started_rest, sems_rest, rest_started = _gather_start("gather_start_rest", rest, geoms[1:], after=tok)
    started, gather_sems = [first] + started_rest, [relay_sems] + sems_rest

    def forward_weight(w, after):
        return _gather_forward(f"gather_forward_{big_names[w]}", started[w], geoms[w], gather_sems[w], after,
                               arrivals=1 if w == 0 else 3)

    def whole_weight(w, forwarded, after):
        return _gather_end(f"gather_end_{big_names[w]}", forwarded[0], geoms[w], forwarded[1], after)

    h1 = _norm_fwd("ffn1_pre_norm", x0, pre_norm_ffn1)
    w_gu1 = whole_weight(0, forward_weight(0, rest_started), h1)
    act1, dact_dgate1, dact_dup1 = _ffn_gate_up_act("ffn1_gate_up", h1, w_gu1)
    w_d1 = whole_weight(1, forward_weight(1, act1), act1)
    ff1 = _mm("ffn1_down", act1, w_d1, "nn", F32)
    fw = forward_weight(2, ff1)
    x1, hm = _resid_norm_fwd("ffn1_residual", x0, ff1, post_norm_ffn1, pre_norm_mix, 0.5)
    w_in = whole_weight(2, fw, hm)
    p = _mm("mix_in", hm, w_in, "nn", F32)
    fw = forward_weight(3, p)
    o_f, o_b, st_f, st_b = _hgrn_scan_fwd("hgrn_scan", p, hgrn_lower_bounds_fwd, hgrn_lower_bounds_bwd)
    y_h = _hgrn_out_fwd("hgrn_out", o_f, o_b, p, hgrn_out_norm, 4)
    kv_blk0 = (5 * hw + aw) // kvw
    k_pad = _pad_kv("attn_pad_k", p, kv_blk0, kvw)
    v_pad = _pad_kv("attn_pad_v", p, kv_blk0 + 1, kvw)
    bucket_ids = _t5_bucket_ids()
    bias = _bias_gather("attn_bias", rel_bias_table.T, bucket_ids).reshape(nah, WINDOW, SPAN)
    y_a = _attn_fwd("attn_fwd", p, k_pad, v_pad, bias, attn_sink, 5 * hw // aw)
    y_mix = _concat_cols("mix_concat", y_h, y_a)
    w_out = whole_weight(3, fw, y_mix)
    mixed = _mm("mix_out", y_mix, w_out, "nn", F32)
    fw = forward_weight(4, mixed)
    x2, h2 = _resid_norm_fwd("mix_residual", x1, mixed, post_norm_mix, pre_norm_ffn2, 1.0)
    w_gu2 = whole_weight(4, fw, h2)
    act2, dact_dgate2, dact_dup2 = _ffn_gate_up_act("ffn2_gate_up", h2, w_gu2)
    w_d2 = whole_weight(5, forward_weight(5, act2), act2)
    ff2 = _mm("ffn2_down", act2, w_d2, "nn", F32)
    loss_blk, dy, dff2, dg_post2 = _final_fwd_bwd("ffn2_residual_loss", x2, ff2, post_norm_ffn2, target, 0.5)

    reduce = [_GradReduce(n, gm, idx, c_idx) for n, gm in zip(big_names, geoms)]
    big_grads, big_delta, big_new_m, big_new_v = [None] * 6, [None] * 6, [None] * 6, [None] * 6

    def update(w, after):
        g = _local_result(f"grad_{big_names[w]}", reduce[w].finish(after))
        dl, nm, nv = _adamw(f"adamw_{big_names[w]}", big_w[w], g, big_m[w], big_v[w])
        big_grads[w], big_delta[w], big_new_m[w], big_new_v[w] = g[None], dl[None], nm[None], nv[None]
        return dl

    def dw_start(w, x_act, dy_act, after=None):
        theirs = _dw_half(f"dw_theirs_{big_names[w]}", x_act, dy_act, geoms[w], c_idx, own=False, after=after)
        return reduce[w].pair_start(theirs)

    def dw_finish(w, x_act, dy_act, after):
        landed = reduce[w].pair_wait(after)
        half = _dw_half(f"dw_own_{big_names[w]}", x_act, dy_act, geoms[w], c_idx, own=True, addend=landed)
        return reduce[w].chip_start(half)

    tok = dw_start(5, act2, dff2)
    dgu2 = _ffn_dact("ffn2_dact", dff2, w_d2, dact_dgate2, dact_dup2, after=tok)
    tok = dw_finish(5, act2, dff2, after=dgu2)
    tok = dw_start(4, h2, dgu2, after=tok)
    dh2 = _ffn_dh("ffn2_dh", dgu2, w_gu2, after=tok)
    tok = dw_finish(4, h2, dgu2, after=dh2)
    dx2, dg_pre2, dmixed, dg_postm = _norms_bwd("mix_residual_bwd", dy, dh2, x2, pre_norm_ffn2,
                                                post=(mixed, post_norm_mix, 1.0), after=tok)
    tok = dw_start(3, y_mix, dmixed)
    dy_mix = _mm("mix_out_dx", dmixed, w_out, "nt", F32, after=tok)
    tok = dw_finish(3, y_mix, dmixed, after=dy_mix)
    dq_a, dk_pad, dv_pad, dbias, dsink = _attn_bwd("attn_bwd", p, k_pad, v_pad, bias, attn_sink, dy_mix,
                                                   5 * hw // aw, hw // aw, after=tok)
    tok = reduce[5].chip_finish(dq_a)
    drel_t = _bias_scatter("attn_dbias", dbias.reshape(nah, WINDOW * SPAN), bucket_ids)
    do, dg_h, dgain = _hgrn_out_bwd("hgrn_out_bwd", dy_mix, o_f, o_b, p, hgrn_out_norm, 4, after=tok)
    dq_f, dv_f, dz_f, dlb_f, dq_b, dv_b, dz_b, dlb_b = _hgrn_scan_bwd(
        "hgrn_scan_bwd", p, hgrn_lower_bounds_fwd, hgrn_lower_bounds_bwd, do, st_f, st_b)
    tok = reduce[4].chip_finish(dq_f)
    tok = reduce[3].chip_finish(tok)
    dp = _mix_dproj("mix_dproj", [(dq_f, dq_b), (dv_f, dv_b), (dz_f,), (dz_b,), (dg_h,), (dq_a,)],
                    [dk_pad, dv_pad], t, after=tok)
    tok = dw_start(2, hm, dp)
    dhm = _mm("mix_in_dx", dp, w_in, "nt", F32, after=tok)
    tok = dw_finish(2, hm, dp, after=dhm)
    dx1, dg_prem, dff1, dg_post1 = _norms_bwd("ffn1_residual_bwd", dx2, dhm, x1, pre_norm_mix,
                                              post=(ff1, post_norm_ffn1, 0.5), after=tok)
    tok = dw_start(1, act1, dff1)
    dgu1 = _ffn_dact("ffn1_dact", dff1, w_d1, dact_dgate1, dact_dup1, after=tok)
    tok = dw_finish(1, act1, dff1, after=dgu1)
    tok = reduce[2].chip_finish(tok)
    tok = dw_start(0, h1, dgu1, after=tok)
    done = update(2, tok)
    tok = dw_finish(0, h1, dgu1, after=done)
    dh1 = _ffn_dh("ffn1_dh", dgu1, w_gu1, after=tok)
    grad_x, dg_pre1 = _norms_bwd("ffn1_pre_norm_bwd", dx1, dh1, x0, pre_norm_ffn1)

    small_w = [pre_norm_ffn1, post_norm_ffn1, pre_norm_mix, post_norm_mix, hgrn_lower_bounds_fwd,
               hgrn_lower_bounds_bwd, hgrn_out_norm, attn_sink, pre_norm_ffn2, post_norm_ffn2, rel_bias_table]
    small_m = [m_pre_norm_ffn1, m_post_norm_ffn1, m_pre_norm_mix, m_post_norm_mix, m_hgrn_lower_bounds_fwd,
               m_hgrn_lower_bounds_bwd, m_hgrn_out_norm, m_attn_sink, m_pre_norm_ffn2, m_post_norm_ffn2,
               m_rel_bias_table]
    small_v = [v_pre_norm_ffn1, v_post_norm_ffn1, v_pre_norm_mix, v_post_norm_mix, v_hgrn_lower_bounds_fwd,
               v_hgrn_lower_bounds_bwd, v_hgrn_out_norm, v_attn_sink, v_pre_norm_ffn2, v_post_norm_ffn2,
               v_rel_bias_table]
    small_g = [dg_pre1, dg_post1, dg_prem, dg_postm, dlb_f, dlb_b, dgain, dsink[:, 0].reshape(1, nah), dg_pre2,
               dg_post2, drel_t.T]
    shapes = [a.shape for a in small_w]
    done = update(5, grad_x)
    done = update(4, done)
    done = update(3, done)
    summed = _all_reduce_small(_pack_rows(small_g + [loss_blk[0:1, 0:1]], d), after=done)
    loss = _unpack_rows(summed, shapes + [(1, 1)], d)[-1][0, 0]
    sd, sm, sv = _adamw("adamw_small", _pack_rows(small_w, d), summed, _pack_rows(small_m, d),
                           _pack_rows(small_v, d))
    small_grads = _unpack_rows(summed, shapes, d)
    small_delta, small_new_m, small_new_v = (_unpack_rows(a, shapes, d) for a in (sd, sm, sv))

    tok = reduce[1].chip_finish(sd)
    done = update(1, tok)
    tok = reduce[0].chip_finish(done)
    update(0, tok)

    def ordered(small, big):
        s = dict(zip(["pre1", "post1", "prem", "postm", "lbf", "lbb", "gain", "sink", "pre2", "post2", "rel"], small))
        b = dict(zip(["gu1", "d1", "win", "wout", "gu2", "d2"], big))
        return [s["pre1"], s["post1"], b["gu1"], b["d1"], s["prem"], s["postm"], b["win"], s["lbf"], s["lbb"],
                s["gain"], s["sink"], b["wout"], s["pre2"], s["post2"], b["gu2"], b["d2"], s["rel"]]

    return (loss, grad_x[None], *ordered(small_grads, big_grads), *ordered(small_delta, big_delta),
            *ordered(small_new_m, big_new_m), *ordered(small_new_v, big_new_v))
```

```python
import functools
import math

import jax
import jax.numpy as jnp
import numpy as np
from jax import lax
from jax.experimental import pallas as pl
from jax.experimental.pallas import tpu as pltpu

F32 = jnp.float32
BF16 = jnp.bfloat16

HEAD = 128
CHUNK = 64
WINDOW = 128
SPAN = 3 * WINDOW
KV_HEADS = 2
REL_BUCKETS = 32
REL_MAX_DIST = 128
EPS = 1e-6
NEG_INF = -1e30

ADAM_LR = 0.001
ADAM_B1 = 0.9
ADAM_B2 = 0.999
ADAM_EPS = 1e-08
ADAM_WD = 0.01
ADAM_STEP = 10

N_CHIPS = 4
N_DEV = 8
V7X_VMEM_BYTES = 64 * 1024 * 1024
MESH = pl.DeviceIdType.MESH
ANY = pl.BlockSpec(memory_space=pl.ANY)


def _tile(n, pref, mult):
    t = (min(pref, n) // mult) * mult
    while t >= mult:
        if n % t == 0:
            return t
        t -= mult
    return n


def _params(semantics, block_bytes):
    limit = min(V7X_VMEM_BYTES - (4 << 20), 2 * int(block_bytes) + (8 << 20))
    return pltpu.CompilerParams(dimension_semantics=semantics, vmem_limit_bytes=limit)


def _nbytes(shape, dtype):
    return int(np.prod(shape)) * jnp.dtype(dtype).itemsize


PIN_TO_HBM_BYTES = 4 << 20


def _pallas(body, **kw):
    def pin_shape(s):
        if isinstance(s, jax.ShapeDtypeStruct) and _nbytes(s.shape, s.dtype) >= PIN_TO_HBM_BYTES:
            return pltpu.HBM(s.shape, s.dtype)
        return s

    def pin(a):
        if getattr(a, "dtype", None) in (F32, BF16) and _nbytes(a.shape, a.dtype) >= PIN_TO_HBM_BYTES:
            return pltpu.with_memory_space_constraint(a, pltpu.HBM)
        return a

    out_shape = kw["out_shape"]
    kw["out_shape"] = [pin_shape(s) for s in out_shape] if isinstance(out_shape, (list, tuple)) else pin_shape(out_shape)
    call = pl.pallas_call(body, **kw)
    return lambda *args: call(*[pin(a) for a in args])


def _dot(a, b, ca=1, cb=0):
    return lax.dot_general(a, b, (((ca,), (cb,)), ((), ())), preferred_element_type=F32)


def _split3(x):
    hi = x.astype(BF16)
    r1 = x - hi.astype(F32)
    mid = r1.astype(BF16)
    lo = (r1 - mid.astype(F32)).astype(BF16)
    return hi, mid, lo


def _dot_exact(a, b, ca=1, cb=0, split="b"):
    if split == "b":
        return sum(_dot(a, p, ca, cb) for p in _split3(b))
    return sum(_dot(p, b, ca, cb) for p in _split3(a))


def _rms(x):
    return lax.rsqrt(jnp.mean(x * x, axis=-1, keepdims=True) + EPS)


def _norm_bwd(u, x, gain):
    r = _rms(x)
    xhat = x * r
    dgain = jnp.sum(u * xhat, axis=0, keepdims=True)
    v = u * gain
    dx = r * (v - xhat * jnp.mean(v * xhat, axis=-1, keepdims=True))
    return dx, dgain


def _sigmoid(x):
    return 1.0 / (1.0 + jnp.exp(-x))


def _accumulate(ref, val, first):
    @pl.when(first)
    def _():
        ref[...] = val

    @pl.when(jnp.logical_not(first))
    def _():
        ref[...] += val


def _ordered(body, ins, in_specs, after):
    if after is None:
        return body, list(ins), list(in_specs)
    n_in = len(ins)

    def wrapped(*refs):
        body(*refs[:n_in], *refs[n_in + 1:])

    return wrapped, list(ins) + [after], list(in_specs) + [pl.BlockSpec(memory_space=pl.ANY)]


def _matmul(name, a, b, *, form, out_dtype, tm, tn, tk, a_map=None, b_map=None,
            out_shape=None, out_block=None, out_map=None, sizes=None, after=None):
    if sizes is None:
        if form == "nn":
            (m, k), n = a.shape, b.shape[1]
        elif form == "nt":
            (m, k), n = a.shape, b.shape[0]
        else:
            (k, m), n = a.shape, b.shape[1]
    else:
        m, n, k = sizes
    gi, gj, gk = m // tm, n // tn, k // tk
    a_blk = (tm, tk) if form != "tn" else (tk, tm)
    b_blk = (tk, tn) if form != "nt" else (tn, tk)
    if a_map is None:
        a_map = (lambda i, j, kk: (i, kk)) if form != "tn" else (lambda i, j, kk: (kk, i))
    else:
        a_blk = (None,) + a_blk
    if b_map is None:
        b_map = (lambda i, j, kk: (kk, j)) if form != "nt" else (lambda i, j, kk: (j, kk))
    else:
        b_blk = (None,) + b_blk
    if out_shape is None:
        out_shape, out_block, out_map = (m, n), (tm, tn), (lambda i, j, kk: (i, j))
    ca, cb = {"nn": (1, 0), "nt": (1, 1), "tn": (0, 0)}[form]

    def body(a_ref, b_ref, o_ref, *acc):
        part = _dot(a_ref[...], b_ref[...], ca, cb)
        if gk == 1:
            o_ref[...] = part.astype(o_ref.dtype)
        else:
            kk = pl.program_id(2)
            _accumulate(acc[0], part, kk == 0)

            @pl.when(kk == gk - 1)
            def _():
                o_ref[...] = acc[0][...].astype(o_ref.dtype)

    scratch = [] if gk == 1 else [pltpu.VMEM((tm, tn), F32)]
    vmem = (_nbytes((tm, tk), a.dtype) + _nbytes((tk, tn), b.dtype) + _nbytes((tm, tn), out_dtype)
            + 2 * _nbytes((tm, tn), F32))
    body, ins, in_specs = _ordered(body, [a, b], [pl.BlockSpec(a_blk, a_map), pl.BlockSpec(b_blk, b_map)], after)
    return _pallas(
        body, name=name, grid=(gi, gj, gk), in_specs=in_specs,
        out_specs=pl.BlockSpec(out_block, out_map),
        out_shape=jax.ShapeDtypeStruct(out_shape, out_dtype),
        scratch_shapes=scratch,
        compiler_params=_params(("parallel", "parallel", "arbitrary"), vmem),
    )(*ins)


V7X_HBM_BYTES_PER_US = 3.0e6
V7X_MXU_FLOPS_PER_US = 0.9e9
V7X_VMEM_RMW_BYTES_PER_US = 10e6
GRID_STEP_US = 0.35
MATMUL_VMEM_BUDGET = 40 << 20
MATMUL_MAX_TILE_FLOPS = 1 << 33


def _divisors(n, mult, lo):
    return [t for t in range(mult, n + 1, mult) if n % t == 0 and t >= min(lo, n)]


def _mm_tiles(m, n, k, out_dtype=F32, n_unit=None, k_unit=None):
    out_bytes = jnp.dtype(out_dtype).itemsize
    best = None
    for tm in _divisors(m, 128, 256):
        for tn in _divisors(n_unit or n, 128, 256):
            for tk in _divisors(k_unit or k, 128, 512):
                gi, gj, gk = m // tm, n // tn, k // tk
                vmem = 4 * tm * tk + 4 * tk * tn + 2 * tm * tn * out_bytes + 4 * tm * tn * (2 if gk > 1 else 1)
                if vmem > MATMUL_VMEM_BUDGET or 2 * tm * tn * tk > MATMUL_MAX_TILE_FLOPS:
                    continue
                a_bytes = 2 * m * k * (gj if gk > 1 else 1)
                b_bytes = 2 * k * n * (1 if gj == 1 and gk == 1 else gi)
                hbm_us = (a_bytes + b_bytes + m * n * out_bytes) / V7X_HBM_BYTES_PER_US
                acc_us = (8 * m * n * gk / V7X_VMEM_RMW_BYTES_PER_US) if gk > 1 else 0.0
                cost = max(2 * m * n * k / V7X_MXU_FLOPS_PER_US, 1.3 * hbm_us) + GRID_STEP_US * gi * gj * gk + acc_us
                key = (round(cost, 1), vmem)
                if best is None or key < best[0]:
                    best = (key, (tm, tn, tk))
    return best[1]


def _mm(name, a, b, form, out_dtype, after=None):
    if form == "nn":
        m, k, n = a.shape[0], a.shape[1], b.shape[1]
    elif form == "nt":
        m, k, n = a.shape[0], a.shape[1], b.shape[0]
    else:
        m, k, n = a.shape[1], a.shape[0], b.shape[1]
    tm, tn, tk = _mm_tiles(m, n, k, out_dtype)
    return _matmul(name, a, b, form=form, out_dtype=out_dtype, tm=tm, tn=tn, tk=tk, after=after)


def _row_tile(t):
    return _tile(t, 256, 8)


def _norm_fwd(name, x, gain):
    t, d = x.shape
    tm = _row_tile(t)

    def body(x_ref, g_ref, h_ref):
        xv = x_ref[...]
        h_ref[...] = (xv * _rms(xv) * g_ref[...]).astype(BF16)

    row = pl.BlockSpec((tm, d), lambda i: (i, 0))
    vec = pl.BlockSpec((1, d), lambda i: (0, 0))
    return _pallas(
        body, name=name, grid=(t // tm,), in_specs=[row, vec], out_specs=row,
        out_shape=jax.ShapeDtypeStruct((t, d), BF16),
        compiler_params=_params(("parallel",), 2 * _nbytes((tm, d), F32)),
    )(x, gain)


def _resid_norm_fwd(name, xres, ff, gpost, gpre, scale):
    t, d = xres.shape
    tm = _row_tile(t)

    def body(x_ref, f_ref, gp_ref, gn_ref, xn_ref, h_ref):
        f = f_ref[...]
        xn = x_ref[...] + scale * (f * _rms(f) * gp_ref[...])
        xn_ref[...] = xn
        h_ref[...] = (xn * _rms(xn) * gn_ref[...]).astype(BF16)

    row = pl.BlockSpec((tm, d), lambda i: (i, 0))
    vec = pl.BlockSpec((1, d), lambda i: (0, 0))
    return _pallas(
        body, name=name, grid=(t // tm,), in_specs=[row, row, vec, vec], out_specs=[row, row],
        out_shape=[jax.ShapeDtypeStruct((t, d), F32), jax.ShapeDtypeStruct((t, d), BF16)],
        compiler_params=_params(("parallel",), 4 * _nbytes((tm, d), F32)),
    )(xres, ff, gpost, gpre)


def _final_fwd_bwd(name, xres, ff, gpost, target, scale):
    t, d = xres.shape
    tm = _row_tile(t)

    def body(x_ref, f_ref, gp_ref, t_ref, loss_ref, dy_ref, dff_ref, dg_ref):
        i = pl.program_id(0)
        f = f_ref[...]
        gp = gp_ref[...]
        y = x_ref[...] + scale * (f * _rms(f) * gp)
        err = y - t_ref[...]
        part = 0.5 * jnp.sum(jnp.mean(err * err, axis=-1, keepdims=True), axis=0, keepdims=True)
        _accumulate(loss_ref, jnp.broadcast_to(part, loss_ref.shape), i == 0)
        dy = err / d
        dy_ref[...] = dy
        dff, dg = _norm_bwd(scale * dy, f, gp)
        dff_ref[...] = dff.astype(BF16)
        _accumulate(dg_ref, dg, i == 0)

    row = pl.BlockSpec((tm, d), lambda i: (i, 0))
    vec = pl.BlockSpec((1, d), lambda i: (0, 0))
    return _pallas(
        body, name=name, grid=(t // tm,), in_specs=[row, row, vec, row],
        out_specs=[pl.BlockSpec((8, 128), lambda i: (0, 0)), row, row, vec],
        out_shape=[jax.ShapeDtypeStruct((8, 128), F32), jax.ShapeDtypeStruct((t, d), F32),
                   jax.ShapeDtypeStruct((t, d), BF16), jax.ShapeDtypeStruct((1, d), F32)],
        compiler_params=_params(("arbitrary",), 5 * _nbytes((tm, d), F32)),
    )(xres, ff, gpost, target)


def _norms_bwd(name, dres, dh, xin, gpre, post=None, after=None):
    t, d = dres.shape
    tm = _row_tile(t)
    with_post = post is not None

    def body(*refs):
        if with_post:
            dr_ref, dh_ref, x_ref, g_ref, f_ref, gp_ref, dx_ref, dg_ref, dff_ref, dgp_ref = refs
        else:
            dr_ref, dh_ref, x_ref, g_ref, dx_ref, dg_ref = refs
        i = pl.program_id(0)
        dx, dg = _norm_bwd(dh_ref[...], x_ref[...], g_ref[...])
        dx = dr_ref[...] + dx
        dx_ref[...] = dx
        _accumulate(dg_ref, dg, i == 0)
        if with_post:
            dff, dgp = _norm_bwd(post[2] * dx, f_ref[...], gp_ref[...])
            dff_ref[...] = dff.astype(BF16)
            _accumulate(dgp_ref, dgp, i == 0)

    row = pl.BlockSpec((tm, d), lambda i: (i, 0))
    vec = pl.BlockSpec((1, d), lambda i: (0, 0))
    ins, in_specs = [dres, dh, xin, gpre], [row, row, row, vec]
    out_specs = [row, vec]
    out_shape = [jax.ShapeDtypeStruct((t, d), F32), jax.ShapeDtypeStruct((1, d), F32)]
    if with_post:
        ins += [post[0], post[1]]
        in_specs += [row, vec]
        out_specs += [row, vec]
        out_shape += [jax.ShapeDtypeStruct((t, d), BF16), jax.ShapeDtypeStruct((1, d), F32)]
    body, ins, in_specs = _ordered(body, ins, in_specs, after)
    return _pallas(
        body, name=name, grid=(t // tm,), in_specs=in_specs, out_specs=out_specs, out_shape=out_shape,
        compiler_params=_params(("arbitrary",), 6 * _nbytes((tm, d), F32)),
    )(*ins)


SWIGLU_TILE = (1024, 512)


def _ffn_gate_up_act(name, h, w_gu):
    t, d = h.shape
    f = w_gu.shape[1] // 2
    tm, tn = _tile(t, SWIGLU_TILE[0], 128), _tile(f, SWIGLU_TILE[1], 128)
    nf = f // tn

    def body(h_ref, wg_ref, wu_ref, a_ref, dg_ref, du_ref):
        hv = h_ref[...]
        g = _dot(hv, wg_ref[...])
        u = _dot(hv, wu_ref[...])
        sig = _sigmoid(g)
        silu = g * sig
        a_ref[...] = (silu * u).astype(BF16)
        dg_ref[...] = (u * sig * (1.0 + g * (1.0 - sig))).astype(BF16)
        du_ref[...] = silu.astype(BF16)

    out = jax.ShapeDtypeStruct((t, f), BF16)
    blk = pl.BlockSpec((tm, tn), lambda i, j: (i, j))
    return _pallas(
        body, name=name, grid=(t // tm, nf),
        in_specs=[pl.BlockSpec((tm, d), lambda i, j: (i, 0)), pl.BlockSpec((d, tn), lambda i, j: (0, j)),
                  pl.BlockSpec((d, tn), lambda i, j: (0, j + nf))],
        out_specs=[blk, blk, blk], out_shape=[out, out, out],
        compiler_params=_params(("parallel", "parallel"),
                                _nbytes((tm, d), BF16) + 2 * _nbytes((d, tn), BF16) + 5 * _nbytes((tm, tn), F32)),
    )(h, w_gu, w_gu)


def _ffn_dact(name, dff, w_down, dact_dgate, dact_dup, after=None):
    t, d = dff.shape
    f = w_down.shape[0]
    tm, tn = _tile(t, SWIGLU_TILE[0], 128), _tile(f, SWIGLU_TILE[1], 128)

    def body(d_ref, w_ref, dg_ref, du_ref, o_ref):
        da = _dot(d_ref[...], w_ref[...], 1, 1)
        o_ref[0] = (da * dg_ref[...].astype(F32)).astype(BF16)
        o_ref[1] = (da * du_ref[...].astype(F32)).astype(BF16)

    blk = pl.BlockSpec((tm, tn), lambda i, j: (i, j))
    body, ins, in_specs = _ordered(
        body, [dff, w_down, dact_dgate, dact_dup],
        [pl.BlockSpec((tm, d), lambda i, j: (i, 0)), pl.BlockSpec((tn, d), lambda i, j: (j, 0)), blk, blk], after)
    return _pallas(
        body, name=name, grid=(t // tm, f // tn), in_specs=in_specs,
        out_specs=pl.BlockSpec((2, tm, tn), lambda i, j: (0, i, j)),
        out_shape=jax.ShapeDtypeStruct((2, t, f), BF16),
        compiler_params=_params(("parallel", "parallel"),
                                _nbytes((tm, d), BF16) + _nbytes((tn, d), BF16) + 5 * _nbytes((tm, tn), F32)),
    )(*ins)


def _ffn_dh(name, dgu, w_gu, after=None):
    _, t, f = dgu.shape
    d = w_gu.shape[0]
    tm, tn, tk = _mm_tiles(t, d, 2 * f, F32, k_unit=f)
    nkf = f // tk
    return _matmul(name, dgu, w_gu, form="nt", out_dtype=F32, tm=tm, tn=tn, tk=tk, sizes=(t, d, 2 * f),
                   a_map=lambda i, j, kk: (kk // nkf, i, kk % nkf), after=after)


def _lower_bound(lbp):
    m = jnp.max(lbp, axis=0, keepdims=True)
    e = jnp.exp(lbp - m)
    return e[0:1] / jnp.sum(e, axis=0, keepdims=True)


def _chunk_mask(reverse):
    row = lax.broadcasted_iota(jnp.int32, (CHUNK, CHUNK), 0)
    col = lax.broadcasted_iota(jnp.int32, (CHUNK, CHUNK), 1)
    return (col >= row) if reverse else (col <= row)


def _hgrn_gates(z, lb, mask_bf):
    sig = _sigmoid(z)
    f = lb + (1.0 - lb) * sig
    logf = jnp.log(f)
    k = 1.0 - f
    cum = _dot_exact(mask_bf, logf)
    last = jnp.sum(logf, axis=0, keepdims=True)
    return sig, f, k, cum, last


def _hgrn_scan_fwd(name, p, lbp_f, lbp_b):
    t = p.shape[0]
    hw = lbp_f.shape[1]
    nh, nc = hw // HEAD, t // CHUNK

    def body(qf, vf, zf, qb, vb, zb, lbf, lbb, of_ref, ob_ref, stf_ref, stb_ref, state):
        n = pl.program_id(0)

        @pl.when(n == 0)
        def _():
            state[...] = jnp.zeros_like(state)

        directions = [(qf, vf, zf, lbf, of_ref, stf_ref), (qb, vb, zb, lbb, ob_ref, stb_ref)]
        wide = []
        for d, (q_ref, v_ref, z_ref, lb_ref, o_ref, st_ref) in enumerate(directions):
            mask = _chunk_mask(d == 1)
            lb = _lower_bound(lb_ref[...])
            _, _, k, cum, last = _hgrn_gates(z_ref[...], lb, mask.astype(BF16))
            v = v_ref[...].astype(BF16)
            qd = (q_ref[...] * jnp.exp(cum)).astype(BF16)
            kd = (k * jnp.exp(-cum)).astype(BF16)
            kt = (k * jnp.exp(last - cum)).astype(BF16)
            s_all = state[d]
            st_ref[...] = s_all
            wide.append((mask, v, qd, kd, kt, jnp.exp(last), s_all, o_ref))
        pairs = [(d, slice(h * HEAD, (h + 1) * HEAD)) for d in range(2) for h in range(nh)]
        a = [jnp.where(wide[d][0], _dot(wide[d][2][:, sl], wide[d][3][:, sl], 1, 1), 0.0).astype(BF16)
             for d, sl in pairs]
        inter = [_dot(wide[d][2][:, sl], wide[d][6][:, sl].astype(BF16), 1, 1) for d, sl in pairs]
        intra = [_dot(a[i], wide[d][1][:, sl]) for i, (d, sl) in enumerate(pairs)]
        grow = [_dot(wide[d][1][:, sl], wide[d][4][:, sl], 0, 0) for d, sl in pairs]
        for i, (d, sl) in enumerate(pairs):
            wide[d][7][:, sl] = intra[i] + inter[i]
            state[d, :, sl] = wide[d][6][:, sl] * wide[d][5][:, sl] + grow[i]

    def col(group, reverse):
        return pl.BlockSpec((CHUNK, hw), lambda n: ((nc - 1 - n) if reverse else n, group))

    def st(reverse):
        return pl.BlockSpec((None, HEAD, hw), lambda n: ((nc - 1 - n) if reverse else n, 0, 0))

    lb_spec = pl.BlockSpec((2, hw), lambda n: (0, 0))
    out = jax.ShapeDtypeStruct((t, hw), F32)
    states = jax.ShapeDtypeStruct((nc, HEAD, hw), F32)
    return _pallas(
        body, name=name, grid=(nc,),
        in_specs=[col(0, False), col(1, False), col(2, False), col(0, True), col(1, True), col(3, True),
                  lb_spec, lb_spec],
        out_specs=[col(0, False), col(0, True), st(False), st(True)],
        out_shape=[out, out, states, states],
        scratch_shapes=[pltpu.VMEM((2, HEAD, hw), F32)],
        compiler_params=_params(("arbitrary",), 12 * _nbytes((HEAD, hw), F32)),
    )(p, p, p, p, p, p, lbp_f, lbp_b)


def _hgrn_scan_bwd(name, p, lbp_f, lbp_b, do, st_f, st_b):
    t = p.shape[0]
    hw = lbp_f.shape[1]
    nh, nc = hw // HEAD, t // CHUNK

    def body(qf, vf, zf, dof, sf, qb, vb, zb, dob, sb, lbf, lbb, dqf, dvf, dzf, dlbf, dqb, dvb, dzb, dlbb,
             dstate, dlb_acc, dqd_s, dkd_s, dkt_s, ddec_s):
        n = pl.program_id(0)

        @pl.when(n == 0)
        def _():
            dstate[...] = jnp.zeros_like(dstate)
            dlb_acc[...] = jnp.zeros_like(dlb_acc)

        directions = [(qf, vf, zf, dof, sf, lbf, dqf, dvf, dzf, dlbf), (qb, vb, zb, dob, sb, lbb, dqb, dvb, dzb, dlbb)]
        for d, (q_ref, v_ref, z_ref, do_ref, st_ref, lb_ref, dq_ref, dv_ref, dz_ref, dlb_ref) in enumerate(directions):
            mask = _chunk_mask(d == 1)
            mask_bf = mask.astype(BF16)
            lb = _lower_bound(lb_ref[...])
            sig, f, k, cum, last = _hgrn_gates(z_ref[...], lb, mask_bf)
            e_pos, e_neg, e_tail = jnp.exp(cum), jnp.exp(-cum), jnp.exp(last - cum)
            dec = jnp.exp(last)
            v = v_ref[...].astype(BF16)
            qd, kd, kt = q_ref[...] * e_pos, k * e_neg, k * e_tail
            qd_bf, kd_bf, kt_bf = qd.astype(BF16), kd.astype(BF16), kt.astype(BF16)
            s_all = st_ref[...]
            ds_all = dstate[d]
            dov = do_ref[...].astype(BF16)
            cols = [slice(h * HEAD, (h + 1) * HEAD) for h in range(nh)]
            s_bf = [s_all[:, sl].astype(BF16) for sl in cols]
            ds_bf = [ds_all[:, sl].astype(BF16) for sl in cols]
            a = [jnp.where(mask, _dot(qd_bf[:, sl], kd_bf[:, sl], 1, 1), 0.0).astype(BF16) for sl in cols]
            da = [jnp.where(mask, _dot(dov[:, sl], v[:, sl], 1, 1), 0.0).astype(BF16) for sl in cols]
            dv_h = [_dot(a[h], dov[:, sl], 0, 0) + _dot(kt_bf[:, sl], ds_bf[h], 1, 1) for h, sl in enumerate(cols)]
            dqd_h = [_dot(da[h], kd_bf[:, sl]) + _dot(dov[:, sl], s_bf[h]) for h, sl in enumerate(cols)]
            dkd_h = [_dot(da[h], qd_bf[:, sl], 0, 0) for h, sl in enumerate(cols)]
            dkt_h = [_dot(v[:, sl], ds_bf[h]) for h, sl in enumerate(cols)]
            dst_h = [_dot(dov[:, sl], qd_bf[:, sl], 0, 0) + ds_all[:, sl] * dec[:, sl] for sl in cols]
            for h, sl in enumerate(cols):
                dv_ref[:, sl] = dv_h[h]
                dqd_s[:, sl] = dqd_h[h]
                dkd_s[:, sl] = dkd_h[h]
                dkt_s[:, sl] = dkt_h[h]
                dstate[d, :, sl] = dst_h[h]
                ddec_s[:, sl] = jnp.sum(ds_all[:, sl] * s_all[:, sl], axis=0, keepdims=True)
            dqd, dkd, dkt = dqd_s[...], dkd_s[...], dkt_s[...]
            dlast = jnp.sum(dkt * kt, axis=0, keepdims=True) + dec * ddec_s[...]
            dq_ref[...] = dqd * e_pos
            dk = dkd * e_neg + dkt * e_tail
            dcum = dqd * qd - dkd * kd - dkt * kt
            dlogf = _dot_exact(mask_bf, dcum, 0, 0) + dlast
            df = dlogf / f - dk
            dz_ref[...] = df * (1.0 - lb) * sig * (1.0 - sig)
            dlb_acc[d] += jnp.sum(df * (1.0 - sig), axis=0, keepdims=True)

            @pl.when(n == nc - 1)
            def _():
                g = dlb_acc[d] * lb * (1.0 - lb)
                dlb_ref[0:1, :] = g
                dlb_ref[1:2, :] = -g

    def col(group, reverse):
        return pl.BlockSpec((CHUNK, hw), lambda n: (n if reverse else (nc - 1 - n), group))

    def st(reverse):
        return pl.BlockSpec((None, HEAD, hw), lambda n: (n if reverse else (nc - 1 - n), 0, 0))

    lb_spec = pl.BlockSpec((2, hw), lambda n: (0, 0))
    out = jax.ShapeDtypeStruct((t, hw), F32)
    dlb = jax.ShapeDtypeStruct((2, hw), F32)
    wide = pltpu.VMEM((CHUNK, hw), F32)
    return _pallas(
        body, name=name, grid=(nc,),
        in_specs=[col(0, False), col(1, False), col(2, False), col(0, False), st(False),
                  col(0, True), col(1, True), col(3, True), col(0, True), st(True), lb_spec, lb_spec],
        out_specs=[col(0, False), col(0, False), col(0, False), lb_spec,
                   col(0, True), col(0, True), col(0, True), lb_spec],
        out_shape=[out, out, out, dlb, out, out, out, dlb],
        scratch_shapes=[pltpu.VMEM((2, HEAD, hw), F32), pltpu.VMEM((2, 1, hw), F32), wide, wide, wide,
                        pltpu.VMEM((1, hw), F32)],
        compiler_params=_params(("arbitrary",), 16 * _nbytes((HEAD, hw), F32)),
    )(p, p, p, do, st_f, p, p, p, do, st_b, lbp_f, lbp_b)


def _hgrn_out_fwd(name, o_f, o_b, p, gain, g_group):
    t, hw = o_f.shape
    nh = hw // HEAD
    tm = _tile(t, 512, 8)

    def body(of_ref, ob_ref, g_ref, gain_ref, y_ref):
        o = of_ref[...] + ob_ref[...]
        g = g_ref[...]
        y_ref[...] = (o * _rms(o) * gain_ref[...] * (g * _sigmoid(g))).astype(BF16)

    blk = pl.BlockSpec((tm, HEAD), lambda i, h: (i, h))
    return _pallas(
        body, name=name, grid=(t // tm, nh),
        in_specs=[blk, blk, pl.BlockSpec((tm, HEAD), lambda i, h: (i, g_group * nh + h)),
                  pl.BlockSpec((1, HEAD), lambda i, h: (0, h))],
        out_specs=blk, out_shape=jax.ShapeDtypeStruct((t, hw), BF16),
        compiler_params=_params(("parallel", "parallel"), 1 << 20),
    )(o_f, o_b, p, gain)


def _hgrn_out_bwd(name, dy, o_f, o_b, p, gain, g_group, after=None):
    t, hw = o_f.shape
    nh = hw // HEAD
    tm = _tile(t, 256, 8)

    def body(dy_ref, of_ref, ob_ref, g_ref, gain_ref, do_ref, dg_ref, dgain_ref):
        i = pl.program_id(0)
        o_all = of_ref[...] + ob_ref[...]
        g_all = g_ref[...]
        sig_all = _sigmoid(g_all)
        dy_all = dy_ref[...]
        up_all = dy_all * (g_all * sig_all)
        dsilu_all = dy_all * sig_all * (1.0 + g_all * (1.0 - sig_all))
        gain_all = gain_ref[...]
        for h in range(nh):
            sl = slice(h * HEAD, (h + 1) * HEAD)
            o, gain_v = o_all[:, sl], gain_all[:, sl]
            do, dgain = _norm_bwd(up_all[:, sl], o, gain_v)
            do_ref[:, sl] = do
            dg_ref[:, sl] = dsilu_all[:, sl] * (o * _rms(o) * gain_v)
            _accumulate(dgain_ref.at[:, sl], dgain, i == 0)

    blk = pl.BlockSpec((tm, hw), lambda i: (i, 0))
    vec = pl.BlockSpec((1, hw), lambda i: (0, 0))
    out = jax.ShapeDtypeStruct((t, hw), F32)
    body, ins, in_specs = _ordered(
        body, [dy, o_f, o_b, p, gain], [blk, blk, blk, pl.BlockSpec((tm, hw), lambda i: (i, g_group)), vec], after)
    return _pallas(
        body, name=name, grid=(t // tm,), in_specs=in_specs,
        out_specs=[blk, blk, vec], out_shape=[out, out, jax.ShapeDtypeStruct((1, hw), F32)],
        compiler_params=_params(("arbitrary",), 7 * _nbytes((tm, hw), F32)),
    )(*ins)


def _t5_bucket_ids():
    c = np.arange(WINDOW)[:, None]
    s = np.arange(SPAN)[None, :]
    rel = s - WINDOW - c
    nb = REL_BUCKETS // 2
    max_exact = nb // 2
    bucket = (rel > 0).astype(np.int32) * nb
    n = np.abs(rel)
    large = max_exact + (np.log(np.maximum(n, 1) / max_exact) / np.log(REL_MAX_DIST / max_exact)
                         * (nb - max_exact)).astype(np.int32)
    large = np.minimum(large, nb - 1)
    ids = bucket + np.where(n < max_exact, n, large).astype(np.int32)
    return jnp.asarray(ids.reshape(1, WINDOW * SPAN), jnp.int32)


def _bias_onehot(ids_ref):
    n = ids_ref.shape[1]
    return (lax.broadcasted_iota(jnp.int32, (REL_BUCKETS, n), 0) == ids_ref[...]).astype(BF16)


def _bias_gather(name, table_t, ids):
    nh = table_t.shape[0]

    def body(t_ref, ids_ref, o_ref):
        o_ref[...] = _dot_exact(t_ref[...], _bias_onehot(ids_ref), split="a")

    return _pallas(
        body, name=name, out_shape=jax.ShapeDtypeStruct((nh, ids.shape[1]), F32),
        compiler_params=pltpu.CompilerParams(vmem_limit_bytes=32 << 20),
    )(table_t, ids)


def _bias_scatter(name, dbias, ids):
    nh = dbias.shape[0]

    def body(d_ref, ids_ref, o_ref):
        o_ref[...] = _dot_exact(d_ref[...], _bias_onehot(ids_ref), 1, 1, split="a")

    return _pallas(
        body, name=name, out_shape=jax.ShapeDtypeStruct((nh, REL_BUCKETS), F32),
        compiler_params=pltpu.CompilerParams(vmem_limit_bytes=32 << 20),
    )(dbias, ids)


def _attn_valid(i, t):
    c = lax.broadcasted_iota(jnp.int32, (WINDOW, SPAN), 0)
    s = lax.broadcasted_iota(jnp.int32, (WINDOW, SPAN), 1)
    rel = s - WINDOW - c
    pos = i * WINDOW - WINDOW + s
    return (jnp.abs(rel) <= WINDOW) & (pos >= 0) & (pos < t)


def _attn_probs(qs, khs, b_ref, s_ref, valid):
    heads = range(len(qs))
    sinks = [s_ref[0:1, h:h + 1] for h in heads]
    s = [_dot(qs[h], khs[h], 1, 1) / math.sqrt(HEAD) for h in heads]
    s = [jnp.where(valid, s[h] + b_ref[h], NEG_INF) for h in heads]
    m = [jnp.maximum(jnp.max(s[h], axis=-1, keepdims=True), sinks[h]) for h in heads]
    e = [jnp.exp(s[h] - m[h]) for h in heads]
    es = [jnp.exp(sinks[h] - m[h]) for h in heads]
    inv = [1.0 / (jnp.sum(e[h], axis=-1, keepdims=True) + es[h]) for h in heads]
    return [e[h] * inv[h] for h in heads], [es[h] * inv[h] for h in heads]


def _attn_fwd(name, p, k_pad, v_pad, bias, sink, q_group_blk):
    t = p.shape[0]
    nh = bias.shape[0]
    aw = nh * HEAD
    grp = nh // KV_HEADS
    nb = t // WINDOW

    def body(q_ref, k_ref, v_ref, b_ref, s_ref, y_ref, pr_ref, ps_ref):
        i = pl.program_id(0)
        valid = _attn_valid(i, t)
        start = pl.multiple_of(i * WINDOW, WINDOW)
        ks = k_ref[pl.ds(start, SPAN), :]
        vs = v_ref[pl.ds(start, SPAN), :]
        heads = range(nh)
        col = lambda h: slice(h * HEAD, (h + 1) * HEAD)
        qs = [q_ref[:, col(h)].astype(BF16) for h in heads]
        pr, ps = _attn_probs(qs, [ks[:, col(h // grp)] for h in heads], b_ref, s_ref, valid)
        pr = [pr[h].astype(BF16) for h in heads]
        out = [_dot(pr[h], vs[:, col(h // grp)]) for h in heads]
        lane = lax.broadcasted_iota(jnp.int32, (WINDOW, 128), 1)
        sinks = jnp.zeros((WINDOW, 128), F32)
        for h in heads:
            y_ref[:, col(h)] = out[h].astype(BF16)
            pr_ref[h] = pr[h]
            sinks = jnp.where(lane == h, ps[h], sinks)
        ps_ref[...] = sinks

    full = lambda a: pl.BlockSpec(a.shape, lambda i: (0,) * a.ndim)
    return _pallas(
        body, name=name, grid=(nb,),
        in_specs=[pl.BlockSpec((WINDOW, aw), lambda i: (i, q_group_blk)), full(k_pad), full(v_pad), full(bias),
                  full(sink)],
        out_specs=[pl.BlockSpec((WINDOW, aw), lambda i: (i, 0)), pl.BlockSpec((nh, WINDOW, SPAN), lambda i: (0, i, 0)),
                   pl.BlockSpec((WINDOW, 128), lambda i: (i, 0))],
        out_shape=[jax.ShapeDtypeStruct((t, aw), BF16), jax.ShapeDtypeStruct((nh, t, SPAN), BF16),
                   jax.ShapeDtypeStruct((t, 128), F32)],
        compiler_params=_params(("parallel",), _nbytes(k_pad.shape, BF16) * 2 + 2 * _nbytes(bias.shape, F32)),
    )(p, k_pad, v_pad, bias, sink)


def _attn_bwd(name, p, k_pad, v_pad, probs, sink_probs, dy, q_group_blk, dy_blk, after=None):
    t = p.shape[0]
    nh = probs.shape[0]
    aw = nh * HEAD
    grp = nh // KV_HEADS
    nb = t // WINDOW
    kvw = k_pad.shape[1]

    def body(q_ref, k_ref, v_ref, pr_ref, ps_ref, dy_ref, dq_ref, dk_ref, dv_ref, db_ref, ds_ref):
        i = pl.program_id(0)

        @pl.when(i == 0)
        def _():
            dk_ref[...] = jnp.zeros_like(dk_ref)
            dv_ref[...] = jnp.zeros_like(dv_ref)
            db_ref[...] = jnp.zeros_like(db_ref)
            ds_ref[...] = jnp.zeros_like(ds_ref)

        start = pl.multiple_of(i * WINDOW, WINDOW)
        ks = k_ref[pl.ds(start, SPAN), :]
        vs = v_ref[pl.ds(start, SPAN), :]
        inv_sqrt = 1.0 / math.sqrt(HEAD)
        heads = range(nh)
        col = lambda h: slice(h * HEAD, (h + 1) * HEAD)
        qs = [q_ref[:, col(h)].astype(BF16) for h in heads]
        khs = [ks[:, col(h // grp)] for h in heads]
        pr_bf = [pr_ref[h] for h in heads]
        pr = [pr_bf[h].astype(F32) for h in heads]
        dos = [dy_ref[:, col(h)].astype(BF16) for h in heads]
        dp = [_dot(dos[h], vs[:, col(h // grp)], 1, 1) for h in heads]
        delta = [jnp.sum(pr[h] * dp[h], axis=-1, keepdims=True) for h in heads]
        dsc = [pr[h] * (dp[h] - delta[h]) for h in heads]
        dsr = [(dsc[h] * inv_sqrt).astype(BF16) for h in heads]
        dq = [_dot(dsr[h], khs[h]) for h in heads]
        dk = [_dot(dsr[h], qs[h], 0, 0) for h in heads]
        dv = [_dot(pr_bf[h], dos[h], 0, 0) for h in heads]
        for h in heads:
            db_ref[h] += dsc[h]
            dsink = jnp.sum(-ps_ref[:, h:h + 1] * delta[h], axis=0, keepdims=True)
            ds_ref[h:h + 1, :] += jnp.broadcast_to(dsink, (1, 128))
            dq_ref[:, col(h)] = dq[h]
        for kv in range(KV_HEADS):
            group = range(kv * grp, (kv + 1) * grp)
            dk_ref[pl.ds(start, SPAN), col(kv)] += sum(dk[h] for h in group)
            dv_ref[pl.ds(start, SPAN), col(kv)] += sum(dv[h] for h in group)

    full = lambda a: pl.BlockSpec(a.shape, lambda i: (0,) * a.ndim)
    whole = lambda shape: pl.BlockSpec(shape, lambda i: (0,) * len(shape))
    pad_shape = (t + 2 * WINDOW, kvw)
    bias_shape = (nh, WINDOW, SPAN)
    body, ins, in_specs = _ordered(
        body, [p, k_pad, v_pad, probs, sink_probs, dy],
        [pl.BlockSpec((WINDOW, aw), lambda i: (i, q_group_blk)), full(k_pad), full(v_pad),
         pl.BlockSpec((nh, WINDOW, SPAN), lambda i: (0, i, 0)), pl.BlockSpec((WINDOW, 128), lambda i: (i, 0)),
         pl.BlockSpec((WINDOW, aw), lambda i: (i, dy_blk))], after)
    return _pallas(
        body, name=name, grid=(nb,), in_specs=in_specs,
        out_specs=[pl.BlockSpec((WINDOW, aw), lambda i: (i, 0)), whole(pad_shape), whole(pad_shape),
                   whole(bias_shape), whole((nh, 128))],
        out_shape=[jax.ShapeDtypeStruct((t, aw), F32), jax.ShapeDtypeStruct(pad_shape, F32),
                   jax.ShapeDtypeStruct(pad_shape, F32), jax.ShapeDtypeStruct(bias_shape, F32),
                   jax.ShapeDtypeStruct((nh, 128), F32)],
        compiler_params=_params(("arbitrary",), 3 * _nbytes(pad_shape, F32) + 3 * _nbytes(bias_shape, F32)),
    )(*ins)


def _pad_kv(name, p, kv_blk, kvw):
    t = p.shape[0]
    nb = t // WINDOW

    def body(x_ref, o_ref):
        i = pl.program_id(0)
        inside = jnp.logical_and(i >= 1, i <= nb)
        o_ref[...] = jnp.where(inside, x_ref[...], 0.0).astype(BF16)

    return _pallas(
        body, name=name, grid=(nb + 2,),
        in_specs=[pl.BlockSpec((WINDOW, kvw), lambda i: (jnp.clip(i - 1, 0, nb - 1), kv_blk))],
        out_specs=pl.BlockSpec((WINDOW, kvw), lambda i: (i, 0)),
        out_shape=jax.ShapeDtypeStruct((t + 2 * WINDOW, kvw), BF16),
        compiler_params=_params(("parallel",), 1 << 20),
    )(p)


def _mix_dproj(name, pieces, kv_pads, t, after=None):
    hw = pieces[0][0].shape[1]
    kvw = kv_pads[0].shape[1]
    widths = [hw] * len(pieces) + [kvw] * len(kv_pads)
    total = sum(widths)
    tm = WINDOW
    flat = [a for pc in pieces for a in pc]

    def body(*refs):
        o_ref = refs[-1]
        pos, off = 0, 0
        for pc in pieces:
            val = refs[pos][...]
            for extra in range(1, len(pc)):
                val = val + refs[pos + extra][...]
            o_ref[:, off:off + hw] = val.astype(BF16)
            pos += len(pc)
            off += hw
        for _ in kv_pads:
            o_ref[:, off:off + kvw] = refs[pos][...].astype(BF16)
            pos += 1
            off += kvw

    in_specs = [pl.BlockSpec((tm, hw), lambda i: (i, 0)) for _ in flat]
    in_specs += [pl.BlockSpec((tm, kvw), lambda i: (i + 1, 0)) for _ in kv_pads]
    body, ins, in_specs = _ordered(body, [*flat, *kv_pads], in_specs, after)
    return _pallas(
        body, name=name, grid=(t // tm,), in_specs=in_specs,
        out_specs=pl.BlockSpec((tm, total), lambda i: (i, 0)),
        out_shape=jax.ShapeDtypeStruct((t, total), BF16),
        compiler_params=_params(("parallel",), 3 * _nbytes((tm, total), F32)),
    )(*ins)


def _concat_cols(name, a, b):
    t, wa = a.shape
    wb = b.shape[1]
    tm = _tile(t, 512, 16)

    def body(a_ref, b_ref, o_ref):
        o_ref[:, :wa] = a_ref[...]
        o_ref[:, wa:] = b_ref[...]

    return _pallas(
        body, name=name, grid=(t // tm,),
        in_specs=[pl.BlockSpec((tm, wa), lambda i: (i, 0)), pl.BlockSpec((tm, wb), lambda i: (i, 0))],
        out_specs=pl.BlockSpec((tm, wa + wb), lambda i: (i, 0)),
        out_shape=jax.ShapeDtypeStruct((t, wa + wb), a.dtype),
        compiler_params=_params(("parallel",), 2 * _nbytes((tm, wa + wb), a.dtype)),
    )(a, b)


def _cast_into_full(name, w, geom, idx, after=None):
    r, c = w.shape
    tr = _tile(r, 256, 16)
    nr = r // tr
    if geom.col:
        place = lambda i, iref: (i, iref[0])
    else:
        place = lambda i, iref: (iref[0] * nr + i, 0)

    def body(i_ref, w_ref, *rest):
        rest[-1][...] = w_ref[...].astype(BF16)

    in_specs = [pl.BlockSpec((tr, c), lambda i, iref: (i, 0))]
    ins = [w]
    if after is not None:
        in_specs.append(pl.BlockSpec(memory_space=pl.ANY))
        ins.append(after)
    return _pallas(
        body, name=name,
        grid_spec=pltpu.PrefetchScalarGridSpec(
            num_scalar_prefetch=1, grid=(nr,), in_specs=in_specs, out_specs=pl.BlockSpec((tr, c), place)),
        out_shape=pltpu.HBM(geom.full_shape, BF16),
        compiler_params=_params(("parallel",), 2 * _nbytes((tr, c), F32)),
    )(idx, *ins)


def _adamw(name, w, g, m, v):
    r, c = w.shape
    tr = _tile(r, 128, 8)
    bc1 = 1.0 - ADAM_B1 ** ADAM_STEP
    bc2 = 1.0 - ADAM_B2 ** ADAM_STEP

    def body(w_ref, g_ref, m_ref, v_ref, go_ref, d_ref, nm_ref, nv_ref):
        gv = g_ref[...]
        go_ref[...] = gv
        nm = ADAM_B1 * m_ref[...] + (1.0 - ADAM_B1) * gv
        nv = ADAM_B2 * v_ref[...] + (1.0 - ADAM_B2) * (gv * gv)
        nm_ref[...] = nm
        nv_ref[...] = nv
        d_ref[...] = -ADAM_LR * ((nm / bc1) / (jnp.sqrt(nv / bc2) + ADAM_EPS) + ADAM_WD * w_ref[...])

    blk = pl.BlockSpec((tr, c), lambda i: (i, 0))
    out = jax.ShapeDtypeStruct((r, c), F32)
    return _pallas(
        body, name=name, grid=(r // tr,), in_specs=[blk] * 4, out_specs=[blk] * 4, out_shape=[out] * 4,
        compiler_params=_params(("parallel",), 8 * _nbytes((tr, c), F32)),
    )(w, g, m, v)


def _mesh_pos():
    return lax.axis_index("x"), lax.axis_index("y"), lax.axis_index("c")


def _other_chips(x, y):
    return [(1 - x, y), (x, 1 - y), (1 - x, 1 - y)]


class _Big:
    def __init__(self, shard_shape, col_sharded):
        self.col = col_sharded
        r, c = shard_shape
        self.shard_shape = (r, c)
        self.full_shape = (r, N_CHIPS * c) if col_sharded else (N_CHIPS * r, c)
        self.half_shape = (r // 2, N_CHIPS * c) if col_sharded else (N_CHIPS * r, c // 2)
        self.shard_half_shape = (r // 2, c) if col_sharded else (r, c // 2)

    def region(self, ref, s, half=None):
        r, c = self.shard_shape
        if self.col:
            rows = slice(None) if half is None else pl.ds(half * (r // 2), r // 2)
            return ref.at[rows, pl.ds(s * c, c)]
        cols = slice(None) if half is None else pl.ds(half * (c // 2), c // 2)
        return ref.at[pl.ds(s * r, r), cols]

    def n_halves(self, ref, half, n):
        r, c = self.shard_shape
        if self.col:
            return ref.at[pl.ds(half * (r // 2), r // 2), pl.ds(0, n * c)]
        return ref.at[pl.ds(0, n * r), pl.ds(half * (c // 2), c // 2)]

    def three_halves(self, ref, half):
        return self.n_halves(ref, half, 3)

    def sub_half(self, ref, s, half, j):
        r, c = self.shard_shape
        if self.col:
            return ref.at[pl.ds(half * (r // 2) + j * (r // 4), r // 4), pl.ds(s * c, c)]
        return ref.at[pl.ds(s * r + j * (r // 2), r // 2), pl.ds(half * (c // 2), c // 2)]

    def half_of_full(self, ref, half):
        r, c = self.full_shape
        if self.col:
            return ref.at[pl.ds(half * (r // 2), r // 2), :]
        return ref.at[:, pl.ds(half * (c // 2), c // 2)]

    def half_of_shard(self, ref, half):
        r, c = self.shard_shape
        if self.col:
            return ref.at[pl.ds(half * (r // 2), r // 2), :]
        return ref.at[:, pl.ds(half * (c // 2), c // 2)]

    def shard_of_half(self, ref, s):
        r, c = self.shard_shape
        if self.col:
            return ref.at[:, pl.ds(s * c, c)]
        return ref.at[pl.ds(s * r, r), :]


HBM =pl.BlockSpec(memory_space=pltpu.HBM)
SEM = pl.BlockSpec(memory_space=pltpu.SEMAPHORE)
SPLIT_COPY = pltpu.CompilerParams(has_side_effects=pltpu.SideEffectType.DATAFLOW_SIDE_EFFECTING)


def _in_hbm(a):
    return pltpu.with_memory_space_constraint(a, pltpu.HBM)


def _gather_start(name, fulls, geoms, after):
    nw = len(fulls)

    def body(*refs):
        dst = refs[nw + 1:2 * nw + 1]
        sems = refs[2 * nw + 1:-1]
        x, y, c = _mesh_pos()
        mine = 2 * x + y
        for w in range(nw):
            own_half = geoms[w].region(dst[w], mine, c)
            for chip in _other_chips(x, y):
                pltpu.make_async_remote_copy(src_ref=own_half, dst_ref=own_half, send_sem=sems[2 * w],
                                             recv_sem=sems[2 * w + 1], device_id=(*chip, c),
                                             device_id_type=MESH).start()
        refs[-1][...] = jnp.zeros_like(refs[-1])

    out = _pallas(
        body, name=name, in_specs=[HBM] * nw + [pl.BlockSpec(memory_space=pl.ANY)],
        out_specs=[HBM] * nw + [SEM] * (2 * nw) + [pl.BlockSpec(memory_space=pltpu.VMEM)],
        out_shape=[pltpu.HBM(g.full_shape, BF16) for g in geoms] + [pltpu.SemaphoreType.DMA(())] * (2 * nw)
        + [jax.ShapeDtypeStruct((8, 128), F32)],
        input_output_aliases={w: w for w in range(nw)}, compiler_params=SPLIT_COPY,
    )(*[_in_hbm(a) for a in fulls], after)
    return list(out[:nw]), [(out[nw + 2 * w], out[nw + 2 * w + 1]) for w in range(nw)], out[-1]


def _gather_first_direct(full, geom):
    def start(refs, _, new):
        x, y, c = _mesh_pos()
        own = geom.region(refs[0], 2 * x + y, c)
        for chip in ((1 - x, y), (x, 1 - y)):
            _remote(own, own, new, (*chip, c)).start()

    return _split_copy_call("gather_first_direct", [full], start, new_sems=2)


def _gather_first_relay(full, geom, sems, after):
    def relay(refs, got, new):
        x, y, c = _mesh_pos()
        w = refs[0]
        two = geom.n_halves(w, c, 2)
        _remote(two, two, got, (x, y, 1 - c)).wait_recv()
        from_x = geom.sub_half(w, 2 * (1 - x) + y, c, 0)
        from_y = geom.sub_half(w, 2 * x + (1 - y), c, 1)
        _remote(from_x, from_x, new, (x, 1 - y, c)).start()
        _remote(from_y, from_y, new, (1 - x, y, c)).start()
        _remote(two, two, got, (x, y, 1 - c)).wait_send()

    return _split_copy_call("gather_first_relay", [full], relay, sems=sems, after=after, new_sems=2)


def _gather_forward(name, full, geom, sems, after, arrivals=3, only_diagonal=False):
    def body(w_in, send_sem, recv_sem, after_ref, w_ref, fwd_send, fwd_recv):
        x, y, c = _mesh_pos()
        sibling = (x, y, 1 - c)
        landed_all = geom.n_halves(w_ref, c, arrivals)
        _remote(landed_all, landed_all, (send_sem, recv_sem), sibling).wait_recv()
        for chip in _other_chips(x, y)[2 if only_diagonal else 0:]:
            landed = geom.region(w_ref, 2 * chip[0] + chip[1], c)
            pltpu.make_async_remote_copy(src_ref=landed, dst_ref=landed, send_sem=fwd_send, recv_sem=fwd_recv,
                                         device_id=sibling, device_id_type=MESH).start()
        _remote(landed_all, landed_all, (send_sem, recv_sem), sibling).wait_send()

    sem = pltpu.SemaphoreType.DMA(())
    out = _pallas(
        body, name=name, in_specs=[HBM, SEM, SEM, pl.BlockSpec(memory_space=pl.ANY)], out_specs=[HBM, SEM, SEM],
        out_shape=[pltpu.HBM(geom.full_shape, BF16), sem, sem],
        input_output_aliases={0: 0}, compiler_params=SPLIT_COPY,
    )(full, sems[0], sems[1], after)
    return out[0], (out[1], out[2])


def _gather_end(name, full, geom, sems, after, halves=3):
    def body(w_in, fwd_send, fwd_recv, after_ref, w_ref):
        x, y, c = _mesh_pos()
        sibling = (x, y, 1 - c)
        theirs, ours = geom.n_halves(w_ref, 1 - c, halves), geom.n_halves(w_ref, c, halves)
        _remote(theirs, theirs, (fwd_send, fwd_recv), sibling).wait_recv()
        _remote(ours, ours, (fwd_send, fwd_recv), sibling).wait_send()

    return _pallas(
        body, name=name, in_specs=[HBM, SEM, SEM, pl.BlockSpec(memory_space=pl.ANY)], out_specs=HBM,
        out_shape=pltpu.HBM(geom.full_shape, BF16),
        input_output_aliases={0: 0}, compiler_params=SPLIT_COPY,
    )(full, sems[0], sems[1], after)


def _split_copy_call(name, arrays, fn, sems=(), after=None, new_sems=0):
    n, ns = len(arrays), len(sems)
    n_in = n + ns + (after is not None)

    def body(*refs):
        fn(refs[n_in:n_in + n], refs[n:n + ns], refs[n_in + n:-1])
        refs[-1][...] = jnp.zeros_like(refs[-1])

    ins = list(arrays) if ns else [_in_hbm(a) for a in arrays]
    ins += list(sems) + ([after] if after is not None else [])
    in_specs = [HBM] * n + [SEM] * ns + ([pl.BlockSpec(memory_space=pl.ANY)] if after is not None else [])
    out = _pallas(
        body, name=name, in_specs=in_specs,
        out_specs=[HBM] * n + [SEM] * new_sems + [pl.BlockSpec(memory_space=pltpu.VMEM)],
        out_shape=[pltpu.HBM(a.shape, a.dtype) for a in arrays] + [pltpu.SemaphoreType.DMA(())] * new_sems
        + [jax.ShapeDtypeStruct((8, 128), F32)],
        input_output_aliases={i: i for i in range(n)}, compiler_params=SPLIT_COPY,
    )(*ins)
    return list(out[:n]), tuple(out[n:-1]), out[-1]


def _remote(src, dst, sems, to):
    return pltpu.make_async_remote_copy(src_ref=src, dst_ref=dst, send_sem=sems[0], recv_sem=sems[1],
                                        device_id=to, device_id_type=MESH)


class _GradReduce:
    def __init__(self, name, geom, idx, c_idx):
        self.name, self.geom, self.idx, self.c_idx = name, geom, idx, c_idx

    def pair_start(self, theirs):
        g = self.geom

        def start(refs, _, new):
            x, y, c = _mesh_pos()
            _remote(refs[0], refs[1], new, (x, y, 1 - c)).start()

        self.arrays, self.sems, token = _split_copy_call(
            f"pair_start_{self.name}", [theirs, lax.empty(g.half_shape, BF16)], start, new_sems=2)
        return token

    def pair_wait(self, after):
        def wait(refs, sems, _):
            x, y, c = _mesh_pos()
            copy = _remote(refs[0], refs[1], sems, (x, y, 1 - c))
            copy.wait_send()
            copy.wait_recv()

        (_, landed), _, _ = _split_copy_call(f"pair_wait_{self.name}", self.arrays, wait, self.sems, after)
        return landed

    def chip_start(self, half):
        g = self.geom

        def start(refs, _, new):
            x, y, c = _mesh_pos()
            for k, chip in enumerate(_other_chips(x, y)):
                _remote(g.shard_of_half(refs[0], 2 * chip[0] + chip[1]), refs[1].at[k], new, (*chip, c)).start()

        self.arrays, self.sems, token = _split_copy_call(
            f"chip_start_{self.name}", [half, lax.empty((3,) + g.shard_half_shape, BF16)], start, new_sems=2)
        return token

    def chip_finish(self, after):
        g = self.geom

        def wait(refs, sems, _):
            x, y, c = _mesh_pos()
            three = _remote(refs[1], refs[1], sems, (x, y, 1 - c))
            three.wait_send()
            three.wait_recv()

        (half, landed), _, _ = _split_copy_call(f"chip_wait_{self.name}", self.arrays, wait, self.sems, after)
        quarter = _chip_add(f"chip_add_{self.name}", half, landed, g, self.idx)

        def start(refs, _, new):
            x, y, c = _mesh_pos()
            own = g.half_of_shard(refs[0], c)
            _remote(own, own, new, (x, y, 1 - c)).start()

        self.arrays, self.sems, token = _split_copy_call(f"share_start_{self.name}", [quarter], start, new_sems=2)
        return token

    def finish(self, after):
        g = self.geom

        def wait(refs, sems, _):
            x, y, c = _mesh_pos()
            own, theirs = g.half_of_shard(refs[0], c), g.half_of_shard(refs[0], 1 - c)
            _remote(own, own, sems, (x, y, 1 - c)).wait_send()
            _remote(theirs, theirs, sems, (x, y, 1 - c)).wait_recv()

        (quarter,), _, _ = _split_copy_call(f"share_wait_{self.name}", self.arrays, wait, self.sems, after)
        return quarter


def _dw_half(name, x, dy, geom, c_idx, own, addend=None, after=None):
    stacked = dy.ndim == 3
    t, m = x.shape
    n = 2 * dy.shape[2] if stacked else dy.shape[1]
    hm, hn = (m // 2, n) if geom.col else (m, n // 2)
    tm, tn, tk = _mm_tiles(hm, hn, t, BF16, n_unit=(n // 2 if stacked else None))
    if tk != t:
        tm, tn = _tile(hm, 512, 128), _tile(hn // (2 if stacked else 1), 512, 128)
    gi, gj = hm // tm, hn // tn
    nf = (n // 2) // tn

    def sel(cref):
        return cref[0] if own else 1 - cref[0]

    a_map = (lambda i, j, cref: (0, sel(cref) * gi + i)) if geom.col else (lambda i, j, cref: (0, i))
    if stacked:
        b_blk, b_map = (None, t, tn), (lambda i, j, cref: (j // nf, 0, j % nf))
    elif geom.col:
        b_blk, b_map = (t, tn), (lambda i, j, cref: (0, j))
    else:
        b_blk, b_map = (t, tn), (lambda i, j, cref: (0, sel(cref) * gj + j))
    out_blk = pl.BlockSpec((tm, tn), lambda i, j, cref: (i, j))
    ins, in_specs = [x, dy], [pl.BlockSpec((t, tm), a_map), pl.BlockSpec(b_blk, b_map)]
    if addend is not None:
        ins.append(addend)
        in_specs.append(out_blk)
    if after is not None:
        ins.append(after)
        in_specs.append(pl.BlockSpec(memory_space=pl.ANY))

    def body(c_ref, *refs):
        acc = _dot(refs[0][...], refs[1][...], 0, 0)
        if addend is not None:
            acc = acc + refs[2][...].astype(F32)
        refs[len(ins)][...] = acc.astype(BF16)

    return _pallas(
        body, name=name,
        grid_spec=pltpu.PrefetchScalarGridSpec(num_scalar_prefetch=1, grid=(gi, gj), in_specs=in_specs,
                                               out_specs=out_blk),
        out_shape=jax.ShapeDtypeStruct((hm, hn), BF16),
        compiler_params=_params(("parallel", "parallel"),
                                _nbytes((t, tm), BF16) + _nbytes((t, tn), BF16) + 3 * _nbytes((tm, tn), F32)),
    )(c_idx, *ins)


def _chip_add(name, half, recv, geom, idx):
    r, c = geom.shard_half_shape
    tr, tc = _tile(r, 512, 16), _tile(c, 2048, 128)
    nr, ncol = r // tr, c // tc
    if geom.col:
        mine = lambda i, j, iref: (i, iref[0] * ncol + j)
        place = lambda i, j, iref: (iref[1] * nr + i, j)
    else:
        mine = lambda i, j, iref: (iref[0] * nr + i, j)
        place = lambda i, j, iref: (i, iref[1] * ncol + j)

    def body(i_ref, h_ref, r_ref, o_ref):
        acc = h_ref[...].astype(F32)
        for k in range(3):
            acc = acc + r_ref[k].astype(F32)
        o_ref[...] = acc

    return _pallas(
        body, name=name,
        grid_spec=pltpu.PrefetchScalarGridSpec(
            num_scalar_prefetch=1, grid=(nr, ncol),
            in_specs=[pl.BlockSpec((tr, tc), mine), pl.BlockSpec((3, tr, tc), lambda i, j, iref: (0, i, j))],
            out_specs=pl.BlockSpec((tr, tc), place)),
        out_shape=jax.ShapeDtypeStruct(geom.shard_shape, F32),
        compiler_params=_params(("parallel", "parallel"), 4 * _nbytes((tr, tc), F32)),
    )(idx, half, recv)


def _all_reduce_small(pack, after=None):
    r, d = pack.shape

    def body(p_ref, o_ref, slots, send_sems, recv_sems):
        x, y, c = _mesh_pos()
        me = 4 * x + 2 * y + c
        slots[me] = p_ref[...]
        copies = []
        for k in range(1, N_DEV):
            px, py, pc = x ^ ((k >> 2) & 1), y ^ ((k >> 1) & 1), c ^ (k & 1)
            copies.append(pltpu.make_async_remote_copy(
                src_ref=p_ref, dst_ref=slots.at[me], send_sem=send_sems.at[k - 1], recv_sem=recv_sems.at[k - 1],
                device_id=(px, py, pc), device_id_type=MESH))
        for cp in copies:
            cp.start()
        for k in range(1, N_DEV):
            peer = 4 * (x ^ ((k >> 2) & 1)) + 2 * (y ^ ((k >> 1) & 1)) + (c ^ (k & 1))
            pltpu.make_async_remote_copy(
                src_ref=p_ref, dst_ref=slots.at[peer], send_sem=send_sems.at[k - 1], recv_sem=recv_sems.at[k - 1],
                device_id=(x, y, c), device_id_type=MESH).wait_recv()
        for cp in copies:
            cp.wait_send()
        acc = slots[0]
        for k in range(1, N_DEV):
            acc = acc + slots[k]
        o_ref[...] = acc

    vm = pl.BlockSpec(memory_space=pltpu.VMEM)
    body, ins, in_specs = _ordered(body, [pack], [vm], after)
    return _pallas(
        body, name="all_reduce_small", in_specs=in_specs, out_specs=vm,
        out_shape=jax.ShapeDtypeStruct((r, d), F32),
        scratch_shapes=[pltpu.VMEM((N_DEV, r, d), F32), pltpu.SemaphoreType.DMA((N_DEV - 1,)),
                        pltpu.SemaphoreType.DMA((N_DEV - 1,))],
    )(*ins)


def _pack_rows(rows, d):
    out = []
    for a in rows:
        flat = a.reshape(-1)
        n = -(-flat.shape[0] // d) * d
        out.append(jnp.pad(flat, (0, n - flat.shape[0])).reshape(-1, d))
    packed = jnp.concatenate(out, axis=0)
    return jnp.pad(packed, ((0, 16 - packed.shape[0]), (0, 0)))


def _unpack_rows(packed, shapes, d):
    out, row = [], 0
    for shp in shapes:
        n = int(np.prod(shp))
        nrows = -(-n // d)
        out.append(packed[row:row + nrows].reshape(-1)[:n].reshape(shp))
        row += nrows
    return out


def kernel(x, pre_norm_ffn1, post_norm_ffn1, w_ffn1_gate_up, w_ffn1_down, pre_norm_mix, post_norm_mix, w_mix_in, hgrn_lower_bounds_fwd, hgrn_lower_bounds_bwd, hgrn_out_norm, attn_sink, w_mix_out, pre_norm_ffn2, post_norm_ffn2, w_ffn2_gate_up, w_ffn2_down, rel_bias_table, loss_target, m_pre_norm_ffn1, m_post_norm_ffn1, m_w_ffn1_gate_up, m_w_ffn1_down, m_pre_norm_mix, m_post_norm_mix, m_w_mix_in, m_hgrn_lower_bounds_fwd, m_hgrn_lower_bounds_bwd, m_hgrn_out_norm, m_attn_sink, m_w_mix_out, m_pre_norm_ffn2, m_post_norm_ffn2, m_w_ffn2_gate_up, m_w_ffn2_down, m_rel_bias_table, v_pre_norm_ffn1, v_post_norm_ffn1, v_w_ffn1_gate_up, v_w_ffn1_down, v_pre_norm_mix, v_post_norm_mix, v_w_mix_in, v_hgrn_lower_bounds_fwd, v_hgrn_lower_bounds_bwd, v_hgrn_out_norm, v_attn_sink, v_w_mix_out, v_pre_norm_ffn2, v_post_norm_ffn2, v_w_ffn2_gate_up, v_w_ffn2_down, v_rel_bias_table):
    t, d = x.shape[1], x.shape[2]
    hw = hgrn_out_norm.shape[1]
    aw = d - hw
    nah = aw // HEAD
    kvw = KV_HEADS * HEAD
    x0 = x[0]
    target = loss_target[0]

    big_names = ["w_ffn1_gate_up", "w_ffn1_down", "w_mix_in", "w_mix_out", "w_ffn2_gate_up", "w_ffn2_down"]
    big_w = [w_ffn1_gate_up[0], w_ffn1_down[0], w_mix_in[0], w_mix_out[0], w_ffn2_gate_up[0], w_ffn2_down[0]]
    big_m = [m_w_ffn1_gate_up[0], m_w_ffn1_down[0], m_w_mix_in[0], m_w_mix_out[0], m_w_ffn2_gate_up[0],
             m_w_ffn2_down[0]]
    big_v = [v_w_ffn1_gate_up[0], v_w_ffn1_down[0], v_w_mix_in[0], v_w_mix_out[0], v_w_ffn2_gate_up[0],
             v_w_ffn2_down[0]]
    col_sharded = [True, False, True, False, True, False]
    geoms = [_Big(w.shape, cs) for w, cs in zip(big_w, col_sharded)]

    cx, cy, cc = _mesh_pos()
    idx = jnp.stack([2 * cx + cy, cc]).astype(jnp.int32)
    c_idx = jnp.reshape(cc, (1,)).astype(jnp.int32)
    first = _cast_into_full(f"cast_{big_names[0]}", big_w[0], geoms[0], idx)
    (first,), direct_sems, tok = _gather_first_direct(first, geoms[0])
    rest = []
    for n, w, gm in zip(big_names[1:], big_w[1:], geoms[1:]):
        tok = _cast_into_full(f"cast_{n}", w, gm, idx, after=tok)
        rest.append(tok)
    (first,), relay_sems, tok = _gather_first_relay(first, geoms[0], direct_sems, after=tok)
    started_rest, sems_rest, rest_started = _gather_start("gather_start_rest", rest, geoms[1:], after=tok)
    started, gather_sems = [first] + started_rest, [relay_sems] + sems_rest

    def forward_weight(w, after):
        return _gather_forward(f"gather_forward_{big_names[w]}", started[w], geoms[w], gather_sems[w], after,
                               arrivals=1 if w == 0 else 3)

    def whole_weight(w, forwarded, after):
        return _gather_end(f"gather_end_{big_names[w]}", forwarded[0], geoms[w], forwarded[1], after)

    h1 = _norm_fwd("ffn1_pre_norm", x0, pre_norm_ffn1)
    w_gu1 = whole_weight(0, forward_weight(0, rest_started), h1)
    act1, dact_dgate1, dact_dup1 = _ffn_gate_up_act("ffn1_gate_up", h1, w_gu1)
    w_d1 = whole_weight(1, forward_weight(1, act1), act1)
    ff1 = _mm("ffn1_down", act1, w_d1, "nn", F32)
    fw = forward_weight(2, ff1)
    x1, hm = _resid_norm_fwd("ffn1_residual", x0, ff1, post_norm_ffn1, pre_norm_mix, 0.5)
    w_in = whole_weight(2, fw, hm)
    p = _mm("mix_in", hm, w_in, "nn", F32)
    fw = forward_weight(3, p)
    o_f, o_b, st_f, st_b = _hgrn_scan_fwd("hgrn_scan", p, hgrn_lower_bounds_fwd, hgrn_lower_bounds_bwd)
    y_h = _hgrn_out_fwd("hgrn_out", o_f, o_b, p, hgrn_out_norm, 4)
    kv_blk0 = (5 * hw + aw) // kvw
    k_pad = _pad_kv("attn_pad_k", p, kv_blk0, kvw)
    v_pad = _pad_kv("attn_pad_v", p, kv_blk0 + 1, kvw)
    bucket_ids = _t5_bucket_ids()
    bias = _bias_gather("attn_bias", rel_bias_table.T, bucket_ids).reshape(nah, WINDOW, SPAN)
    y_a, attn_probs, attn_sink_probs = _attn_fwd("attn_fwd", p, k_pad, v_pad, bias, attn_sink, 5 * hw // aw)
    y_mix = _concat_cols("mix_concat", y_h, y_a)
    w_out = whole_weight(3, fw, y_mix)
    mixed = _mm("mix_out", y_mix, w_out, "nn", F32)
    fw = forward_weight(4, mixed)
    x2, h2 = _resid_norm_fwd("mix_residual", x1, mixed, post_norm_mix, pre_norm_ffn2, 1.0)
    w_gu2 = whole_weight(4, fw, h2)
    act2, dact_dgate2, dact_dup2 = _ffn_gate_up_act("ffn2_gate_up", h2, w_gu2)
    w_d2 = whole_weight(5, forward_weight(5, act2), act2)
    ff2 = _mm("ffn2_down", act2, w_d2, "nn", F32)
    loss_blk, dy, dff2, dg_post2 = _final_fwd_bwd("ffn2_residual_loss", x2, ff2, post_norm_ffn2, target, 0.5)

    reduce = [_GradReduce(n, gm, idx, c_idx) for n, gm in zip(big_names, geoms)]
    big_grads, big_delta, big_new_m, big_new_v = [None] * 6, [None] * 6, [None] * 6, [None] * 6

    def update(w, after):
        g, dl, nm, nv = _adamw(f"adamw_{big_names[w]}", big_w[w], reduce[w].finish(after), big_m[w], big_v[w])
        big_grads[w], big_delta[w], big_new_m[w], big_new_v[w] = g[None], dl[None], nm[None], nv[None]
        return dl

    def dw_start(w, x_act, dy_act, after=None):
        theirs = _dw_half(f"dw_theirs_{big_names[w]}", x_act, dy_act, geoms[w], c_idx, own=False, after=after)
        return reduce[w].pair_start(theirs)

    def dw_finish(w, x_act, dy_act, after):
        landed = reduce[w].pair_wait(after)
        half = _dw_half(f"dw_own_{big_names[w]}", x_act, dy_act, geoms[w], c_idx, own=True, addend=landed)
        return reduce[w].chip_start(half)

    tok = dw_start(5, act2, dff2)
    dgu2 = _ffn_dact("ffn2_dact", dff2, w_d2, dact_dgate2, dact_dup2, after=tok)
    tok = dw_finish(5, act2, dff2, after=dgu2)
    tok = dw_start(4, h2, dgu2, after=tok)
    dh2 = _ffn_dh("ffn2_dh", dgu2, w_gu2, after=tok)
    tok = dw_finish(4, h2, dgu2, after=dh2)
    dx2, dg_pre2, dmixed, dg_postm = _norms_bwd("mix_residual_bwd", dy, dh2, x2, pre_norm_ffn2,
                                                post=(mixed, post_norm_mix, 1.0), after=tok)
    tok = dw_start(3, y_mix, dmixed)
    dy_mix = _mm("mix_out_dx", dmixed, w_out, "nt", F32, after=tok)
    tok = dw_finish(3, y_mix, dmixed, after=dy_mix)
    dq_a, dk_pad, dv_pad, dbias, dsink = _attn_bwd("attn_bwd", p, k_pad, v_pad, attn_probs, attn_sink_probs, dy_mix,
                                                   5 * hw // aw, hw // aw, after=tok)
    tok = reduce[5].chip_finish(dq_a)
    drel_t = _bias_scatter("attn_dbias", dbias.reshape(nah, WINDOW * SPAN), bucket_ids)
    do, dg_h, dgain = _hgrn_out_bwd("hgrn_out_bwd", dy_mix, o_f, o_b, p, hgrn_out_norm, 4, after=tok)
    dq_f, dv_f, dz_f, dlb_f, dq_b, dv_b, dz_b, dlb_b = _hgrn_scan_bwd(
        "hgrn_scan_bwd", p, hgrn_lower_bounds_fwd, hgrn_lower_bounds_bwd, do, st_f, st_b)
    tok = reduce[4].chip_finish(dq_f)
    tok = reduce[3].chip_finish(tok)
    dp = _mix_dproj("mix_dproj", [(dq_f, dq_b), (dv_f, dv_b), (dz_f,), (dz_b,), (dg_h,), (dq_a,)],
                    [dk_pad, dv_pad], t, after=tok)
    tok = dw_start(2, hm, dp)
    dhm = _mm("mix_in_dx", dp, w_in, "nt", F32, after=tok)
    tok = dw_finish(2, hm, dp, after=dhm)
    dx1, dg_prem, dff1, dg_post1 = _norms_bwd("ffn1_residual_bwd", dx2, dhm, x1, pre_norm_mix,
                                              post=(ff1, post_norm_ffn1, 0.5), after=tok)
    tok = dw_start(1, act1, dff1)
    dgu1 = _ffn_dact("ffn1_dact", dff1, w_d1, dact_dgate1, dact_dup1, after=tok)
    tok = dw_finish(1, act1, dff1, after=dgu1)
    tok = reduce[2].chip_finish(tok)
    tok = dw_start(0, h1, dgu1, after=tok)
    done = update(2, tok)
    tok = dw_finish(0, h1, dgu1, after=done)
    dh1 = _ffn_dh("ffn1_dh", dgu1, w_gu1, after=tok)
    grad_x, dg_pre1 = _norms_bwd("ffn1_pre_norm_bwd", dx1, dh1, x0, pre_norm_ffn1)

    small_w = [pre_norm_ffn1, post_norm_ffn1, pre_norm_mix, post_norm_mix, hgrn_lower_bounds_fwd,
               hgrn_lower_bounds_bwd, hgrn_out_norm, attn_sink, pre_norm_ffn2, post_norm_ffn2, rel_bias_table]
    small_m = [m_pre_norm_ffn1, m_post_norm_ffn1, m_pre_norm_mix, m_post_norm_mix, m_hgrn_lower_bounds_fwd,
               m_hgrn_lower_bounds_bwd, m_hgrn_out_norm, m_attn_sink, m_pre_norm_ffn2, m_post_norm_ffn2,
               m_rel_bias_table]
    small_v = [v_pre_norm_ffn1, v_post_norm_ffn1, v_pre_norm_mix, v_post_norm_mix, v_hgrn_lower_bounds_fwd,
               v_hgrn_lower_bounds_bwd, v_hgrn_out_norm, v_attn_sink, v_pre_norm_ffn2, v_post_norm_ffn2,
               v_rel_bias_table]
    small_g = [dg_pre1, dg_post1, dg_prem, dg_postm, dlb_f, dlb_b, dgain, dsink[:, 0].reshape(1, nah), dg_pre2,
               dg_post2, drel_t.T]
    shapes = [a.shape for a in small_w]
    done = update(5, grad_x)
    done = update(4, done)
    done = update(3, done)
    summed = _all_reduce_small(_pack_rows(small_g + [loss_blk[0:1, 0:1]], d), after=done)
    loss = _unpack_rows(summed, shapes + [(1, 1)], d)[-1][0, 0]
    _, sd, sm, sv = _adamw("adamw_small", _pack_rows(small_w, d), summed, _pack_rows(small_m, d),
                           _pack_rows(small_v, d))
    small_grads = _unpack_rows(summed, shapes, d)
    small_delta, small_new_m, small_new_v = (_unpack_rows(a, shapes, d) for a in (sd, sm, sv))

    tok = reduce[1].chip_finish(sd)
    done = update(1, tok)
    tok = reduce[0].chip_finish(done)
    update(0, tok)

    def ordered(small, big):
        s = dict(zip(["pre1", "post1", "prem", "postm", "lbf", "lbb", "gain", "sink", "pre2", "post2", "rel"], small))
        b = dict(zip(["gu1", "d1", "win", "wout", "gu2", "d2"], big))
        return [s["pre1"], s["post1"], b["gu1"], b["d1"], s["prem"], s["postm"], b["win"], s["lbf"], s["lbb"],
                s["gain"], s["sink"], b["wout"], s["pre2"], s["post2"], b["gu2"], b["d2"], s["rel"]]

    return (loss, grad_x[None], *ordered(small_grads, big_grads), *ordered(small_delta, big_delta),
            *ordered(small_new_m, big_new_m), *ordered(small_new_v, big_new_v))
```

```python
import functools
import math

import jax
import jax.numpy as jnp
import numpy as np
from jax import lax
from jax.experimental import pallas as pl
from jax.experimental.pallas import tpu as pltpu

F32 = jnp.float32
BF16 = jnp.bfloat16

HEAD = 128
CHUNK = 64
WINDOW = 128
SPAN = 3 * WINDOW
KV_HEADS = 2
REL_BUCKETS = 32
REL_MAX_DIST = 128
EPS = 1e-6
NEG_INF = -1e30

ADAM_LR = 0.001
ADAM_B1 = 0.9
ADAM_B2 = 0.999
ADAM_EPS = 1e-08
ADAM_WD = 0.01
ADAM_STEP = 10

N_CHIPS = 4
N_DEV = 8
V7X_VMEM_BYTES = 64 * 1024 * 1024
MESH = pl.DeviceIdType.MESH
ANY = pl.BlockSpec(memory_space=pl.ANY)


def _tile(n, pref, mult):
    t = (min(pref, n) // mult) * mult
    while t >= mult:
        if n % t == 0:
            return t
        t -= mult
    return n


def _params(semantics, block_bytes):
    limit = min(V7X_VMEM_BYTES - (4 << 20), 2 * int(block_bytes) + (8 << 20))
    return pltpu.CompilerParams(dimension_semantics=semantics, vmem_limit_bytes=limit)


def _nbytes(shape, dtype):
    return int(np.prod(shape)) * jnp.dtype(dtype).itemsize


PIN_TO_HBM_BYTES = 4 << 20


def _pallas(body, **kw):
    def pin_shape(s):
        if isinstance(s, jax.ShapeDtypeStruct) and _nbytes(s.shape, s.dtype) >= PIN_TO_HBM_BYTES:
            return pltpu.HBM(s.shape, s.dtype)
        return s

    def pin(a):
        if getattr(a, "dtype", None) in (F32, BF16) and _nbytes(a.shape, a.dtype) >= PIN_TO_HBM_BYTES:
            return pltpu.with_memory_space_constraint(a, pltpu.HBM)
        return a

    out_shape = kw["out_shape"]
    kw["out_shape"] = [pin_shape(s) for s in out_shape] if isinstance(out_shape, (list, tuple)) else pin_shape(out_shape)
    call = pl.pallas_call(body, **kw)
    return lambda *args: call(*[pin(a) for a in args])


def _dot(a, b, ca=1, cb=0):
    return lax.dot_general(a, b, (((ca,), (cb,)), ((), ())), preferred_element_type=F32)


def _split3(x):
    hi = x.astype(BF16)
    r1 = x - hi.astype(F32)
    mid = r1.astype(BF16)
    lo = (r1 - mid.astype(F32)).astype(BF16)
    return hi, mid, lo


def _dot_exact(a, b, ca=1, cb=0, split="b"):
    if split == "b":
        return sum(_dot(a, p, ca, cb) for p in _split3(b))
    return sum(_dot(p, b, ca, cb) for p in _split3(a))


def _rms(x):
    return lax.rsqrt(jnp.mean(x * x, axis=-1, keepdims=True) + EPS)


def _norm_bwd(u, x, gain):
    r = _rms(x)
    xhat = x * r
    dgain = jnp.sum(u * xhat, axis=0, keepdims=True)
    v = u * gain
    dx = r * (v - xhat * jnp.mean(v * xhat, axis=-1, keepdims=True))
    return dx, dgain


def _sigmoid(x):
    return 1.0 / (1.0 + jnp.exp(-x))


def _accumulate(ref, val, first):
    @pl.when(first)
    def _():
        ref[...] = val

    @pl.when(jnp.logical_not(first))
    def _():
        ref[...] += val


def _ordered(body, ins, in_specs, after):
    if after is None:
        return body, list(ins), list(in_specs)
    n_in = len(ins)

    def wrapped(*refs):
        body(*refs[:n_in], *refs[n_in + 1:])

    return wrapped, list(ins) + [after], list(in_specs) + [pl.BlockSpec(memory_space=pl.ANY)]


def _matmul(name, a, b, *, form, out_dtype, tm, tn, tk, a_map=None, b_map=None,
            out_shape=None, out_block=None, out_map=None, sizes=None, after=None):
    if sizes is None:
        if form == "nn":
            (m, k), n = a.shape, b.shape[1]
        elif form == "nt":
            (m, k), n = a.shape, b.shape[0]
        else:
            (k, m), n = a.shape, b.shape[1]
    else:
        m, n, k = sizes
    gi, gj, gk = m // tm, n // tn, k // tk
    a_blk = (tm, tk) if form != "tn" else (tk, tm)
    b_blk = (tk, tn) if form != "nt" else (tn, tk)
    if a_map is None:
        a_map = (lambda i, j, kk: (i, kk)) if form != "tn" else (lambda i, j, kk: (kk, i))
    else:
        a_blk = (None,) + a_blk
    if b_map is None:
        b_map = (lambda i, j, kk: (kk, j)) if form != "nt" else (lambda i, j, kk: (j, kk))
    else:
        b_blk = (None,) + b_blk
    if out_shape is None:
        out_shape, out_block, out_map = (m, n), (tm, tn), (lambda i, j, kk: (i, j))
    ca, cb = {"nn": (1, 0), "nt": (1, 1), "tn": (0, 0)}[form]

    def body(a_ref, b_ref, o_ref, *acc):
        part = _dot(a_ref[...], b_ref[...], ca, cb)
        if gk == 1:
            o_ref[...] = part.astype(o_ref.dtype)
        else:
            kk = pl.program_id(2)
            _accumulate(acc[0], part, kk == 0)

            @pl.when(kk == gk - 1)
            def _():
                o_ref[...] = acc[0][...].astype(o_ref.dtype)

    scratch = [] if gk == 1 else [pltpu.VMEM((tm, tn), F32)]
    vmem = (_nbytes((tm, tk), a.dtype) + _nbytes((tk, tn), b.dtype) + _nbytes((tm, tn), out_dtype)
            + 2 * _nbytes((tm, tn), F32))
    body, ins, in_specs = _ordered(body, [a, b], [pl.BlockSpec(a_blk, a_map), pl.BlockSpec(b_blk, b_map)], after)
    return _pallas(
        body, name=name, grid=(gi, gj, gk), in_specs=in_specs,
        out_specs=pl.BlockSpec(out_block, out_map),
        out_shape=jax.ShapeDtypeStruct(out_shape, out_dtype),
        scratch_shapes=scratch,
        compiler_params=_params(("parallel", "parallel", "arbitrary"), vmem),
    )(*ins)


V7X_HBM_BYTES_PER_US = 3.0e6
V7X_MXU_FLOPS_PER_US = 0.9e9
V7X_VMEM_RMW_BYTES_PER_US = 10e6
GRID_STEP_US = 0.35
MATMUL_VMEM_BUDGET = 40 << 20
MATMUL_MAX_TILE_FLOPS = 1 << 33


def _divisors(n, mult, lo):
    return [t for t in range(mult, n + 1, mult) if n % t == 0 and t >= min(lo, n)]


def _mm_tiles(m, n, k, out_dtype=F32, n_unit=None, k_unit=None):
    out_bytes = jnp.dtype(out_dtype).itemsize
    best = None
    for tm in _divisors(m, 128, 256):
        for tn in _divisors(n_unit or n, 128, 256):
            for tk in _divisors(k_unit or k, 128, 512):
                gi, gj, gk = m // tm, n // tn, k // tk
                vmem = 4 * tm * tk + 4 * tk * tn + 2 * tm * tn * out_bytes + 4 * tm * tn * (2 if gk > 1 else 1)
                if vmem > MATMUL_VMEM_BUDGET or 2 * tm * tn * tk > MATMUL_MAX_TILE_FLOPS:
                    continue
                a_bytes = 2 * m * k * (gj if gk > 1 else 1)
                b_bytes = 2 * k * n * (1 if gj == 1 and gk == 1 else gi)
                hbm_us = (a_bytes + b_bytes + m * n * out_bytes) / V7X_HBM_BYTES_PER_US
                acc_us = (8 * m * n * gk / V7X_VMEM_RMW_BYTES_PER_US) if gk > 1 else 0.0
                cost = max(2 * m * n * k / V7X_MXU_FLOPS_PER_US, 1.3 * hbm_us) + GRID_STEP_US * gi * gj * gk + acc_us
                key = (round(cost, 1), vmem)
                if best is None or key < best[0]:
                    best = (key, (tm, tn, tk))
    return best[1]


def _mm(name, a, b, form, out_dtype, after=None):
    if form == "nn":
        m, k, n = a.shape[0], a.shape[1], b.shape[1]
    elif form == "nt":
        m, k, n = a.shape[0], a.shape[1], b.shape[0]
    else:
        m, k, n = a.shape[1], a.shape[0], b.shape[1]
    tm, tn, tk = _mm_tiles(m, n, k, out_dtype)
    return _matmul(name, a, b, form=form, out_dtype=out_dtype, tm=tm, tn=tn, tk=tk, after=after)


def _row_tile(t):
    return _tile(t, 256, 8)


def _norm_fwd(name, x, gain):
    t, d = x.shape
    tm = _row_tile(t)

    def body(x_ref, g_ref, h_ref):
        xv = x_ref[...]
        h_ref[...] = (xv * _rms(xv) * g_ref[...]).astype(BF16)

    row = pl.BlockSpec((tm, d), lambda i: (i, 0))
    vec = pl.BlockSpec((1, d), lambda i: (0, 0))
    return _pallas(
        body, name=name, grid=(t // tm,), in_specs=[row, vec], out_specs=row,
        out_shape=jax.ShapeDtypeStruct((t, d), BF16),
        compiler_params=_params(("parallel",), 2 * _nbytes((tm, d), F32)),
    )(x, gain)


def _resid_norm_fwd(name, xres, ff, gpost, gpre, scale):
    t, d = xres.shape
    tm = _row_tile(t)

    def body(x_ref, f_ref, gp_ref, gn_ref, xn_ref, h_ref):
        f = f_ref[...]
        xn = x_ref[...] + scale * (f * _rms(f) * gp_ref[...])
        xn_ref[...] = xn
        h_ref[...] = (xn * _rms(xn) * gn_ref[...]).astype(BF16)

    row = pl.BlockSpec((tm, d), lambda i: (i, 0))
    vec = pl.BlockSpec((1, d), lambda i: (0, 0))
    return _pallas(
        body, name=name, grid=(t // tm,), in_specs=[row, row, vec, vec], out_specs=[row, row],
        out_shape=[jax.ShapeDtypeStruct((t, d), F32), jax.ShapeDtypeStruct((t, d), BF16)],
        compiler_params=_params(("parallel",), 4 * _nbytes((tm, d), F32)),
    )(xres, ff, gpost, gpre)


def _final_fwd_bwd(name, xres, ff, gpost, target, scale):
    t, d = xres.shape
    tm = _row_tile(t)

    def body(x_ref, f_ref, gp_ref, t_ref, loss_ref, dy_ref, dff_ref, dg_ref):
        i = pl.program_id(0)
        f = f_ref[...]
        gp = gp_ref[...]
        y = x_ref[...] + scale * (f * _rms(f) * gp)
        err = y - t_ref[...]
        part = 0.5 * jnp.sum(jnp.mean(err * err, axis=-1, keepdims=True), axis=0, keepdims=True)
        _accumulate(loss_ref, jnp.broadcast_to(part, loss_ref.shape), i == 0)
        dy = err / d
        dy_ref[...] = dy
        dff, dg = _norm_bwd(scale * dy, f, gp)
        dff_ref[...] = dff.astype(BF16)
        _accumulate(dg_ref, dg, i == 0)

    row = pl.BlockSpec((tm, d), lambda i: (i, 0))
    vec = pl.BlockSpec((1, d), lambda i: (0, 0))
    return _pallas(
        body, name=name, grid=(t // tm,), in_specs=[row, row, vec, row],
        out_specs=[pl.BlockSpec((8, 128), lambda i: (0, 0)), row, row, vec],
        out_shape=[jax.ShapeDtypeStruct((8, 128), F32), jax.ShapeDtypeStruct((t, d), F32),
                   jax.ShapeDtypeStruct((t, d), BF16), jax.ShapeDtypeStruct((1, d), F32)],
        compiler_params=_params(("arbitrary",), 5 * _nbytes((tm, d), F32)),
    )(xres, ff, gpost, target)


def _norms_bwd(name, dres, dh, xin, gpre, post=None, after=None):
    t, d = dres.shape
    tm = _row_tile(t)
    with_post = post is not None

    def body(*refs):
        if with_post:
            dr_ref, dh_ref, x_ref, g_ref, f_ref, gp_ref, dx_ref, dg_ref, dff_ref, dgp_ref = refs
        else:
            dr_ref, dh_ref, x_ref, g_ref, dx_ref, dg_ref = refs
        i = pl.program_id(0)
        dx, dg = _norm_bwd(dh_ref[...], x_ref[...], g_ref[...])
        dx = dr_ref[...] + dx
        dx_ref[...] = dx
        _accumulate(dg_ref, dg, i == 0)
        if with_post:
            dff, dgp = _norm_bwd(post[2] * dx, f_ref[...], gp_ref[...])
            dff_ref[...] = dff.astype(BF16)
            _accumulate(dgp_ref, dgp, i == 0)

    row = pl.BlockSpec((tm, d), lambda i: (i, 0))
    vec = pl.BlockSpec((1, d), lambda i: (0, 0))
    ins, in_specs = [dres, dh, xin, gpre], [row, row, row, vec]
    out_specs = [row, vec]
    out_shape = [jax.ShapeDtypeStruct((t, d), F32), jax.ShapeDtypeStruct((1, d), F32)]
    if with_post:
        ins += [post[0], post[1]]
        in_specs += [row, vec]
        out_specs += [row, vec]
        out_shape += [jax.ShapeDtypeStruct((t, d), BF16), jax.ShapeDtypeStruct((1, d), F32)]
    body, ins, in_specs = _ordered(body, ins, in_specs, after)
    return _pallas(
        body, name=name, grid=(t // tm,), in_specs=in_specs, out_specs=out_specs, out_shape=out_shape,
        compiler_params=_params(("arbitrary",), 6 * _nbytes((tm, d), F32)),
    )(*ins)


SWIGLU_TILE = (1024, 512)


def _ffn_gate_up_act(name, h, w_gu):
    t, d = h.shape
    f = w_gu.shape[1] // 2
    tm, tn = _tile(t, SWIGLU_TILE[0], 128), _tile(f, SWIGLU_TILE[1], 128)
    nf = f // tn

    def body(h_ref, wg_ref, wu_ref, a_ref, dg_ref, du_ref):
        hv = h_ref[...]
        g = _dot(hv, wg_ref[...])
        u = _dot(hv, wu_ref[...])
        sig = _sigmoid(g)
        silu = g * sig
        a_ref[...] = (silu * u).astype(BF16)
        dg_ref[...] = (u * sig * (1.0 + g * (1.0 - sig))).astype(BF16)
        du_ref[...] = silu.astype(BF16)

    out = jax.ShapeDtypeStruct((t, f), BF16)
    blk = pl.BlockSpec((tm, tn), lambda i, j: (i, j))
    return _pallas(
        body, name=name, grid=(t // tm, nf),
        in_specs=[pl.BlockSpec((tm, d), lambda i, j: (i, 0)), pl.BlockSpec((d, tn), lambda i, j: (0, j)),
                  pl.BlockSpec((d, tn), lambda i, j: (0, j + nf))],
        out_specs=[blk, blk, blk], out_shape=[out, out, out],
        compiler_params=_params(("parallel", "parallel"),
                                _nbytes((tm, d), BF16) + 2 * _nbytes((d, tn), BF16) + 5 * _nbytes((tm, tn), F32)),
    )(h, w_gu, w_gu)


def _ffn_dact(name, dff, w_down, dact_dgate, dact_dup, after=None):
    t, d = dff.shape
    f = w_down.shape[0]
    tm, tn = _tile(t, SWIGLU_TILE[0], 128), _tile(f, SWIGLU_TILE[1], 128)

    def body(d_ref, w_ref, dg_ref, du_ref, o_ref):
        da = _dot(d_ref[...], w_ref[...], 1, 1)
        o_ref[0] = (da * dg_ref[...].astype(F32)).astype(BF16)
        o_ref[1] = (da * du_ref[...].astype(F32)).astype(BF16)

    blk = pl.BlockSpec((tm, tn), lambda i, j: (i, j))
    body, ins, in_specs = _ordered(
        body, [dff, w_down, dact_dgate, dact_dup],
        [pl.BlockSpec((tm, d), lambda i, j: (i, 0)), pl.BlockSpec((tn, d), lambda i, j: (j, 0)), blk, blk], after)
    return _pallas(
        body, name=name, grid=(t // tm, f // tn), in_specs=in_specs,
        out_specs=pl.BlockSpec((2, tm, tn), lambda i, j: (0, i, j)),
        out_shape=jax.ShapeDtypeStruct((2, t, f), BF16),
        compiler_params=_params(("parallel", "parallel"),
                                _nbytes((tm, d), BF16) + _nbytes((tn, d), BF16) + 5 * _nbytes((tm, tn), F32)),
    )(*ins)


def _ffn_dh(name, dgu, w_gu, after=None):
    _, t, f = dgu.shape
    d = w_gu.shape[0]
    tm, tn, tk = _mm_tiles(t, d, 2 * f, F32, k_unit=f)
    nkf = f // tk
    return _matmul(name, dgu, w_gu, form="nt", out_dtype=F32, tm=tm, tn=tn, tk=tk, sizes=(t, d, 2 * f),
                   a_map=lambda i, j, kk: (kk // nkf, i, kk % nkf), after=after)


def _lower_bound(lbp):
    m = jnp.max(lbp, axis=0, keepdims=True)
    e = jnp.exp(lbp - m)
    return e[0:1] / jnp.sum(e, axis=0, keepdims=True)


def _chunk_mask(reverse):
    row = lax.broadcasted_iota(jnp.int32, (CHUNK, CHUNK), 0)
    col = lax.broadcasted_iota(jnp.int32, (CHUNK, CHUNK), 1)
    return (col >= row) if reverse else (col <= row)


def _hgrn_gates(z, lb, mask_bf):
    sig = _sigmoid(z)
    f = lb + (1.0 - lb) * sig
    logf = jnp.log(f)
    k = 1.0 - f
    cum = _dot_exact(mask_bf, logf)
    last = jnp.sum(logf, axis=0, keepdims=True)
    return sig, f, k, cum, last


def _hgrn_scan_fwd(name, p, lbp_f, lbp_b):
    t = p.shape[0]
    hw = lbp_f.shape[1]
    nh, nc = hw // HEAD, t // CHUNK

    def body(qf, vf, zf, qb, vb, zb, lbf, lbb, of_ref, ob_ref, stf_ref, stb_ref, gf_ref, gb_ref, decf_ref, decb_ref,
             state):
        n = pl.program_id(0)

        @pl.when(n == 0)
        def _():
            state[...] = jnp.zeros_like(state)

        directions = [(qf, vf, zf, lbf, of_ref, stf_ref, gf_ref, decf_ref),
                      (qb, vb, zb, lbb, ob_ref, stb_ref, gb_ref, decb_ref)]
        wide = []
        for d, (q_ref, v_ref, z_ref, lb_ref, o_ref, st_ref, g_ref, dec_ref) in enumerate(directions):
            mask = _chunk_mask(d == 1)
            lb = _lower_bound(lb_ref[...])
            sig, _, k, cum, last = _hgrn_gates(z_ref[...], lb, mask.astype(BF16))
            e_pos, e_neg, dec = jnp.exp(cum), jnp.exp(-cum), jnp.exp(last)
            g_ref[0], g_ref[1], g_ref[2], g_ref[3] = sig, k, e_pos, e_neg
            dec_ref[...] = dec
            v = v_ref[...].astype(BF16)
            qd = (q_ref[...] * e_pos).astype(BF16)
            kd = (k * e_neg).astype(BF16)
            kt = (k * jnp.exp(last - cum)).astype(BF16)
            s_all = state[d]
            st_ref[...] = s_all
            wide.append((mask, v, qd, kd, kt, dec, s_all, o_ref))
        pairs = [(d, slice(h * HEAD, (h + 1) * HEAD)) for d in range(2) for h in range(nh)]
        a = [jnp.where(wide[d][0], _dot(wide[d][2][:, sl], wide[d][3][:, sl], 1, 1), 0.0).astype(BF16)
             for d, sl in pairs]
        inter = [_dot(wide[d][2][:, sl], wide[d][6][:, sl].astype(BF16), 1, 1) for d, sl in pairs]
        intra = [_dot(a[i], wide[d][1][:, sl]) for i, (d, sl) in enumerate(pairs)]
        grow = [_dot(wide[d][1][:, sl], wide[d][4][:, sl], 0, 0) for d, sl in pairs]
        for i, (d, sl) in enumerate(pairs):
            wide[d][7][:, sl] = intra[i] + inter[i]
            state[d, :, sl] = wide[d][6][:, sl] * wide[d][5][:, sl] + grow[i]

    def col(group, reverse):
        return pl.BlockSpec((CHUNK, hw), lambda n: ((nc - 1 - n) if reverse else n, group))

    def st(reverse):
        return pl.BlockSpec((None, HEAD, hw), lambda n: ((nc - 1 - n) if reverse else n, 0, 0))

    def gates(reverse):
        return pl.BlockSpec((4, CHUNK, hw), lambda n: (0, (nc - 1 - n) if reverse else n, 0))

    def decay(reverse):
        return pl.BlockSpec((None, 1, hw), lambda n: ((nc - 1 - n) if reverse else n, 0, 0))

    lb_spec = pl.BlockSpec((2, hw), lambda n: (0, 0))
    out = jax.ShapeDtypeStruct((t, hw), F32)
    states = jax.ShapeDtypeStruct((nc, HEAD, hw), F32)
    gate_factors = jax.ShapeDtypeStruct((4, t, hw), F32)
    decays = jax.ShapeDtypeStruct((nc, 1, hw), F32)
    return _pallas(
        body, name=name, grid=(nc,),
        in_specs=[col(0, False), col(1, False), col(2, False), col(0, True), col(1, True), col(3, True),
                  lb_spec, lb_spec],
        out_specs=[col(0, False), col(0, True), st(False), st(True), gates(False), gates(True), decay(False),
                   decay(True)],
        out_shape=[out, out, states, states, gate_factors, gate_factors, decays, decays],
        scratch_shapes=[pltpu.VMEM((2, HEAD, hw), F32)],
        compiler_params=_params(("arbitrary",), 20 * _nbytes((HEAD, hw), F32)),
    )(p, p, p, p, p, p, lbp_f, lbp_b)


def _hgrn_scan_bwd(name, p, lbp_f, lbp_b, do, saved_f, saved_b):
    t = p.shape[0]
    hw = lbp_f.shape[1]
    nh, nc = hw // HEAD, t // CHUNK

    def body(qf, vf, dof, sf, gf, decf, qb, vb, dob, sb, gb, decb, lbf, lbb, dqf, dvf, dzf, dlbf, dqb, dvb, dzb, dlbb,
             dstate, dlb_acc, dqd_s, dkd_s, dkt_s, ddec_s):
        n = pl.program_id(0)

        @pl.when(n == 0)
        def _():
            dstate[...] = jnp.zeros_like(dstate)
            dlb_acc[...] = jnp.zeros_like(dlb_acc)

        directions = [(qf, vf, dof, sf, gf, decf, lbf, dqf, dvf, dzf, dlbf),
                      (qb, vb, dob, sb, gb, decb, lbb, dqb, dvb, dzb, dlbb)]
        for d, (q_ref, v_ref, do_ref, st_ref, g_ref, dec_ref, lb_ref, dq_ref, dv_ref, dz_ref, dlb_ref) in enumerate(
                directions):
            mask = _chunk_mask(d == 1)
            mask_bf = mask.astype(BF16)
            lb = _lower_bound(lb_ref[...])
            sig, k, e_pos, e_neg = g_ref[0], g_ref[1], g_ref[2], g_ref[3]
            f = 1.0 - k
            dec = dec_ref[...]
            e_tail = dec * e_neg
            v = v_ref[...].astype(BF16)
            qd, kd, kt = q_ref[...] * e_pos, k * e_neg, k * e_tail
            qd_bf, kd_bf, kt_bf = qd.astype(BF16), kd.astype(BF16), kt.astype(BF16)
            s_all = st_ref[...]
            ds_all = dstate[d]
            dov = do_ref[...].astype(BF16)
            cols = [slice(h * HEAD, (h + 1) * HEAD) for h in range(nh)]
            s_bf = [s_all[:, sl].astype(BF16) for sl in cols]
            ds_bf = [ds_all[:, sl].astype(BF16) for sl in cols]
            a = [jnp.where(mask, _dot(qd_bf[:, sl], kd_bf[:, sl], 1, 1), 0.0).astype(BF16) for sl in cols]
            da = [jnp.where(mask, _dot(dov[:, sl], v[:, sl], 1, 1), 0.0).astype(BF16) for sl in cols]
            dv_h = [_dot(a[h], dov[:, sl], 0, 0) + _dot(kt_bf[:, sl], ds_bf[h], 1, 1) for h, sl in enumerate(cols)]
            dqd_h = [_dot(da[h], kd_bf[:, sl]) + _dot(dov[:, sl], s_bf[h]) for h, sl in enumerate(cols)]
            dkd_h = [_dot(da[h], qd_bf[:, sl], 0, 0) for h, sl in enumerate(cols)]
            dkt_h = [_dot(v[:, sl], ds_bf[h]) for h, sl in enumerate(cols)]
            dst_h = [_dot(dov[:, sl], qd_bf[:, sl], 0, 0) + ds_all[:, sl] * dec[:, sl] for sl in cols]
            for h, sl in enumerate(cols):
                dv_ref[:, sl] = dv_h[h]
                dqd_s[:, sl] = dqd_h[h]
                dkd_s[:, sl] = dkd_h[h]
                dkt_s[:, sl] = dkt_h[h]
                dstate[d, :, sl] = dst_h[h]
                ddec_s[:, sl] = jnp.sum(ds_all[:, sl] * s_all[:, sl], axis=0, keepdims=True)
            dqd, dkd, dkt = dqd_s[...], dkd_s[...], dkt_s[...]
            dlast = jnp.sum(dkt * kt, axis=0, keepdims=True) + dec * ddec_s[...]
            dq_ref[...] = dqd * e_pos
            dk = dkd * e_neg + dkt * e_tail
            dcum = dqd * qd - dkd * kd - dkt * kt
            dlogf = _dot_exact(mask_bf, dcum, 0, 0) + dlast
            df = dlogf / f - dk
            dz_ref[...] = df * (1.0 - lb) * sig * (1.0 - sig)
            dlb_acc[d] += jnp.sum(df * (1.0 - sig), axis=0, keepdims=True)

            @pl.when(n == nc - 1)
            def _():
                g = dlb_acc[d] * lb * (1.0 - lb)
                dlb_ref[0:1, :] = g
                dlb_ref[1:2, :] = -g

    def col(group, reverse):
        return pl.BlockSpec((CHUNK, hw), lambda n: (n if reverse else (nc - 1 - n), group))

    def st(reverse):
        return pl.BlockSpec((None, HEAD, hw), lambda n: (n if reverse else (nc - 1 - n), 0, 0))

    def gates(reverse):
        return pl.BlockSpec((4, CHUNK, hw), lambda n: (0, n if reverse else (nc - 1 - n), 0))

    def decay(reverse):
        return pl.BlockSpec((None, 1, hw), lambda n: (n if reverse else (nc - 1 - n), 0, 0))

    lb_spec = pl.BlockSpec((2, hw), lambda n: (0, 0))
    out = jax.ShapeDtypeStruct((t, hw), F32)
    dlb = jax.ShapeDtypeStruct((2, hw), F32)
    wide = pltpu.VMEM((CHUNK, hw), F32)
    return _pallas(
        body, name=name, grid=(nc,),
        in_specs=[col(0, False), col(1, False), col(0, False), st(False), gates(False), decay(False),
                  col(0, True), col(1, True), col(0, True), st(True), gates(True), decay(True), lb_spec, lb_spec],
        out_specs=[col(0, False), col(0, False), col(0, False), lb_spec,
                   col(0, True), col(0, True), col(0, True), lb_spec],
        out_shape=[out, out, out, dlb, out, out, out, dlb],
        scratch_shapes=[pltpu.VMEM((2, HEAD, hw), F32), pltpu.VMEM((2, 1, hw), F32), wide, wide, wide,
                        pltpu.VMEM((1, hw), F32)],
        compiler_params=_params(("arbitrary",), 24 * _nbytes((HEAD, hw), F32)),
    )(p, p, do, *saved_f, p, p, do, *saved_b, lbp_f, lbp_b)


def _hgrn_out_fwd(name, o_f, o_b, p, gain, g_group):
    t, hw = o_f.shape
    nh = hw // HEAD
    tm = _tile(t, 512, 8)

    def body(of_ref, ob_ref, g_ref, gain_ref, y_ref):
        o = of_ref[...] + ob_ref[...]
        g = g_ref[...]
        y_ref[...] = (o * _rms(o) * gain_ref[...] * (g * _sigmoid(g))).astype(BF16)

    blk = pl.BlockSpec((tm, HEAD), lambda i, h: (i, h))
    return _pallas(
        body, name=name, grid=(t // tm, nh),
        in_specs=[blk, blk, pl.BlockSpec((tm, HEAD), lambda i, h: (i, g_group * nh + h)),
                  pl.BlockSpec((1, HEAD), lambda i, h: (0, h))],
        out_specs=blk, out_shape=jax.ShapeDtypeStruct((t, hw), BF16),
        compiler_params=_params(("parallel", "parallel"), 1 << 20),
    )(o_f, o_b, p, gain)


def _hgrn_out_bwd(name, dy, o_f, o_b, p, gain, g_group, after=None):
    t, hw = o_f.shape
    nh = hw // HEAD
    tm = _tile(t, 256, 8)

    def body(dy_ref, of_ref, ob_ref, g_ref, gain_ref, do_ref, dg_ref, dgain_ref):
        i = pl.program_id(0)
        o_all = of_ref[...] + ob_ref[...]
        g_all = g_ref[...]
        sig_all = _sigmoid(g_all)
        dy_all = dy_ref[...]
        up_all = dy_all * (g_all * sig_all)
        dsilu_all = dy_all * sig_all * (1.0 + g_all * (1.0 - sig_all))
        gain_all = gain_ref[...]
        for h in range(nh):
            sl = slice(h * HEAD, (h + 1) * HEAD)
            o, gain_v = o_all[:, sl], gain_all[:, sl]
            do, dgain = _norm_bwd(up_all[:, sl], o, gain_v)
            do_ref[:, sl] = do
            dg_ref[:, sl] = dsilu_all[:, sl] * (o * _rms(o) * gain_v)
            _accumulate(dgain_ref.at[:, sl], dgain, i == 0)

    blk = pl.BlockSpec((tm, hw), lambda i: (i, 0))
    vec = pl.BlockSpec((1, hw), lambda i: (0, 0))
    out = jax.ShapeDtypeStruct((t, hw), F32)
    body, ins, in_specs = _ordered(
        body, [dy, o_f, o_b, p, gain], [blk, blk, blk, pl.BlockSpec((tm, hw), lambda i: (i, g_group)), vec], after)
    return _pallas(
        body, name=name, grid=(t // tm,), in_specs=in_specs,
        out_specs=[blk, blk, vec], out_shape=[out, out, jax.ShapeDtypeStruct((1, hw), F32)],
        compiler_params=_params(("arbitrary",), 7 * _nbytes((tm, hw), F32)),
    )(*ins)


def _t5_bucket_ids():
    c = np.arange(WINDOW)[:, None]
    s = np.arange(SPAN)[None, :]
    rel = s - WINDOW - c
    nb = REL_BUCKETS // 2
    max_exact = nb // 2
    bucket = (rel > 0).astype(np.int32) * nb
    n = np.abs(rel)
    large = max_exact + (np.log(np.maximum(n, 1) / max_exact) / np.log(REL_MAX_DIST / max_exact)
                         * (nb - max_exact)).astype(np.int32)
    large = np.minimum(large, nb - 1)
    ids = bucket + np.where(n < max_exact, n, large).astype(np.int32)
    return jnp.asarray(ids.reshape(1, WINDOW * SPAN), jnp.int32)


def _bias_onehot(ids_ref):
    n = ids_ref.shape[1]
    return (lax.broadcasted_iota(jnp.int32, (REL_BUCKETS, n), 0) == ids_ref[...]).astype(BF16)


def _bias_gather(name, table_t, ids):
    nh = table_t.shape[0]

    def body(t_ref, ids_ref, o_ref):
        o_ref[...] = _dot_exact(t_ref[...], _bias_onehot(ids_ref), split="a")

    return _pallas(
        body, name=name, out_shape=jax.ShapeDtypeStruct((nh, ids.shape[1]), F32),
        compiler_params=pltpu.CompilerParams(vmem_limit_bytes=32 << 20),
    )(table_t, ids)


def _bias_scatter(name, dbias, ids):
    nh = dbias.shape[0]

    def body(d_ref, ids_ref, o_ref):
        o_ref[...] = _dot_exact(d_ref[...], _bias_onehot(ids_ref), 1, 1, split="a")

    return _pallas(
        body, name=name, out_shape=jax.ShapeDtypeStruct((nh, REL_BUCKETS), F32),
        compiler_params=pltpu.CompilerParams(vmem_limit_bytes=32 << 20),
    )(dbias, ids)


def _attn_valid(i, t):
    c = lax.broadcasted_iota(jnp.int32, (WINDOW, SPAN), 0)
    s = lax.broadcasted_iota(jnp.int32, (WINDOW, SPAN), 1)
    rel = s - WINDOW - c
    pos = i * WINDOW - WINDOW + s
    return (jnp.abs(rel) <= WINDOW) & (pos >= 0) & (pos < t)


def _attn_probs(qs, khs, b_ref, s_ref, valid):
    heads = range(len(qs))
    sinks = [s_ref[0:1, h:h + 1] for h in heads]
    s = [_dot(qs[h], khs[h], 1, 1) / math.sqrt(HEAD) for h in heads]
    s = [jnp.where(valid, s[h] + b_ref[h], NEG_INF) for h in heads]
    m = [jnp.maximum(jnp.max(s[h], axis=-1, keepdims=True), sinks[h]) for h in heads]
    e = [jnp.exp(s[h] - m[h]) for h in heads]
    es = [jnp.exp(sinks[h] - m[h]) for h in heads]
    inv = [1.0 / (jnp.sum(e[h], axis=-1, keepdims=True) + es[h]) for h in heads]
    return [e[h] * inv[h] for h in heads], [es[h] * inv[h] for h in heads]


def _attn_fwd(name, p, k_pad, v_pad, bias, sink, q_group_blk):
    t = p.shape[0]
    nh = bias.shape[0]
    aw = nh * HEAD
    grp = nh // KV_HEADS
    nb = t // WINDOW

    def body(q_ref, k_ref, v_ref, b_ref, s_ref, y_ref, pr_ref, ps_ref):
        i = pl.program_id(0)
        valid = _attn_valid(i, t)
        start = pl.multiple_of(i * WINDOW, WINDOW)
        ks = k_ref[pl.ds(start, SPAN), :]
        vs = v_ref[pl.ds(start, SPAN), :]
        heads = range(nh)
        col = lambda h: slice(h * HEAD, (h + 1) * HEAD)
        qs = [q_ref[:, col(h)].astype(BF16) for h in heads]
        pr, ps = _attn_probs(qs, [ks[:, col(h // grp)] for h in heads], b_ref, s_ref, valid)
        pr = [pr[h].astype(BF16) for h in heads]
        out = [_dot(pr[h], vs[:, col(h // grp)]) for h in heads]
        lane = lax.broadcasted_iota(jnp.int32, (WINDOW, 128), 1)
        sinks = jnp.zeros((WINDOW, 128), F32)
        for h in heads:
            y_ref[:, col(h)] = out[h].astype(BF16)
            pr_ref[h] = pr[h]
            sinks = jnp.where(lane == h, ps[h], sinks)
        ps_ref[...] = sinks

    full = lambda a: pl.BlockSpec(a.shape, lambda i: (0,) * a.ndim)
    return _pallas(
        body, name=name, grid=(nb,),
        in_specs=[pl.BlockSpec((WINDOW, aw), lambda i: (i, q_group_blk)), full(k_pad), full(v_pad), full(bias),
                  full(sink)],
        out_specs=[pl.BlockSpec((WINDOW, aw), lambda i: (i, 0)), pl.BlockSpec((nh, WINDOW, SPAN), lambda i: (0, i, 0)),
                   pl.BlockSpec((WINDOW, 128), lambda i: (i, 0))],
        out_shape=[jax.ShapeDtypeStruct((t, aw), BF16), jax.ShapeDtypeStruct((nh, t, SPAN), BF16),
                   jax.ShapeDtypeStruct((t, 128), F32)],
        compiler_params=_params(("parallel",), _nbytes(k_pad.shape, BF16) * 2 + 2 * _nbytes(bias.shape, F32)),
    )(p, k_pad, v_pad, bias, sink)


def _attn_bwd(name, p, k_pad, v_pad, probs, sink_probs, dy, q_group_blk, dy_blk, after=None):
    t = p.shape[0]
    nh = probs.shape[0]
    aw = nh * HEAD
    grp = nh // KV_HEADS
    nb = t // WINDOW
    kvw = k_pad.shape[1]

    def body(q_ref, k_ref, v_ref, pr_ref, ps_ref, dy_ref, dq_ref, dk_ref, dv_ref, db_ref, ds_ref):
        i = pl.program_id(0)

        @pl.when(i == 0)
        def _():
            dk_ref[...] = jnp.zeros_like(dk_ref)
            dv_ref[...] = jnp.zeros_like(dv_ref)
            db_ref[...] = jnp.zeros_like(db_ref)
            ds_ref[...] = jnp.zeros_like(ds_ref)

        start = pl.multiple_of(i * WINDOW, WINDOW)
        ks = k_ref[pl.ds(start, SPAN), :]
        vs = v_ref[pl.ds(start, SPAN), :]
        inv_sqrt = 1.0 / math.sqrt(HEAD)
        heads = range(nh)
        col = lambda h: slice(h * HEAD, (h + 1) * HEAD)
        qs = [q_ref[:, col(h)].astype(BF16) for h in heads]
        khs = [ks[:, col(h // grp)] for h in heads]
        pr_bf = [pr_ref[h] for h in heads]
        pr = [pr_bf[h].astype(F32) for h in heads]
        dos = [dy_ref[:, col(h)].astype(BF16) for h in heads]
        dp = [_dot(dos[h], vs[:, col(h // grp)], 1, 1) for h in heads]
        delta = [jnp.sum(pr[h] * dp[h], axis=-1, keepdims=True) for h in heads]
        dsc = [pr[h] * (dp[h] - delta[h]) for h in heads]
        dsr = [(dsc[h] * inv_sqrt).astype(BF16) for h in heads]
        dq = [_dot(dsr[h], khs[h]) for h in heads]
        dk = [_dot(dsr[h], qs[h], 0, 0) for h in heads]
        dv = [_dot(pr_bf[h], dos[h], 0, 0) for h in heads]
        for h in heads:
            db_ref[h] += dsc[h]
            dsink = jnp.sum(-ps_ref[:, h:h + 1] * delta[h], axis=0, keepdims=True)
            ds_ref[h:h + 1, :] += jnp.broadcast_to(dsink, (1, 128))
            dq_ref[:, col(h)] = dq[h]
        for kv in range(KV_HEADS):
            group = range(kv * grp, (kv + 1) * grp)
            dk_ref[pl.ds(start, SPAN), col(kv)] += sum(dk[h] for h in group)
            dv_ref[pl.ds(start, SPAN), col(kv)] += sum(dv[h] for h in group)

    full = lambda a: pl.BlockSpec(a.shape, lambda i: (0,) * a.ndim)
    whole = lambda shape: pl.BlockSpec(shape, lambda i: (0,) * len(shape))
    pad_shape = (t + 2 * WINDOW, kvw)
    bias_shape = (nh, WINDOW, SPAN)
    body, ins, in_specs = _ordered(
        body, [p, k_pad, v_pad, probs, sink_probs, dy],
        [pl.BlockSpec((WINDOW, aw), lambda i: (i, q_group_blk)), full(k_pad), full(v_pad),
         pl.BlockSpec((nh, WINDOW, SPAN), lambda i: (0, i, 0)), pl.BlockSpec((WINDOW, 128), lambda i: (i, 0)),
         pl.BlockSpec((WINDOW, aw), lambda i: (i, dy_blk))], after)
    return _pallas(
        body, name=name, grid=(nb,), in_specs=in_specs,
        out_specs=[pl.BlockSpec((WINDOW, aw), lambda i: (i, 0)), whole(pad_shape), whole(pad_shape),
                   whole(bias_shape), whole((nh, 128))],
        out_shape=[jax.ShapeDtypeStruct((t, aw), F32), jax.ShapeDtypeStruct(pad_shape, F32),
                   jax.ShapeDtypeStruct(pad_shape, F32), jax.ShapeDtypeStruct(bias_shape, F32),
                   jax.ShapeDtypeStruct((nh, 128), F32)],
        compiler_params=_params(("arbitrary",), 3 * _nbytes(pad_shape, F32) + 3 * _nbytes(bias_shape, F32)),
    )(*ins)


def _pad_kv(name, p, kv_blk, kvw):
    t = p.shape[0]
    nb = t // WINDOW

    def body(x_ref, o_ref):
        i = pl.program_id(0)
        inside = jnp.logical_and(i >= 1, i <= nb)
        o_ref[...] = jnp.where(inside, x_ref[...], 0.0).astype(BF16)

    return _pallas(
        body, name=name, grid=(nb + 2,),
        in_specs=[pl.BlockSpec((WINDOW, kvw), lambda i: (jnp.clip(i - 1, 0, nb - 1), kv_blk))],
        out_specs=pl.BlockSpec((WINDOW, kvw), lambda i: (i, 0)),
        out_shape=jax.ShapeDtypeStruct((t + 2 * WINDOW, kvw), BF16),
        compiler_params=_params(("parallel",), 1 << 20),
    )(p)


def _mix_dproj(name, pieces, kv_pads, t, after=None):
    hw = pieces[0][0].shape[1]
    kvw = kv_pads[0].shape[1]
    widths = [hw] * len(pieces) + [kvw] * len(kv_pads)
    total = sum(widths)
    tm = WINDOW
    flat = [a for pc in pieces for a in pc]

    def body(*refs):
        o_ref = refs[-1]
        pos, off = 0, 0
        for pc in pieces:
            val = refs[pos][...]
            for extra in range(1, len(pc)):
                val = val + refs[pos + extra][...]
            o_ref[:, off:off + hw] = val.astype(BF16)
            pos += len(pc)
            off += hw
        for _ in kv_pads:
            o_ref[:, off:off + kvw] = refs[pos][...].astype(BF16)
            pos += 1
            off += kvw

    in_specs = [pl.BlockSpec((tm, hw), lambda i: (i, 0)) for _ in flat]
    in_specs += [pl.BlockSpec((tm, kvw), lambda i: (i + 1, 0)) for _ in kv_pads]
    body, ins, in_specs = _ordered(body, [*flat, *kv_pads], in_specs, after)
    return _pallas(
        body, name=name, grid=(t // tm,), in_specs=in_specs,
        out_specs=pl.BlockSpec((tm, total), lambda i: (i, 0)),
        out_shape=jax.ShapeDtypeStruct((t, total), BF16),
        compiler_params=_params(("parallel",), 3 * _nbytes((tm, total), F32)),
    )(*ins)


def _concat_cols(name, a, b):
    t, wa = a.shape
    wb = b.shape[1]
    tm = _tile(t, 512, 16)

    def body(a_ref, b_ref, o_ref):
        o_ref[:, :wa] = a_ref[...]
        o_ref[:, wa:] = b_ref[...]

    return _pallas(
        body, name=name, grid=(t // tm,),
        in_specs=[pl.BlockSpec((tm, wa), lambda i: (i, 0)), pl.BlockSpec((tm, wb), lambda i: (i, 0))],
        out_specs=pl.BlockSpec((tm, wa + wb), lambda i: (i, 0)),
        out_shape=jax.ShapeDtypeStruct((t, wa + wb), a.dtype),
        compiler_params=_params(("parallel",), 2 * _nbytes((tm, wa + wb), a.dtype)),
    )(a, b)


def _cast_into_full(name, w, geom, idx, after=None):
    r, c = w.shape
    tr = _tile(r, 256, 16)
    nr = r // tr
    if geom.col:
        place = lambda i, iref: (i, iref[0])
    else:
        place = lambda i, iref: (iref[0] * nr + i, 0)

    def body(i_ref, w_ref, *rest):
        rest[-1][...] = w_ref[...].astype(BF16)

    in_specs = [pl.BlockSpec((tr, c), lambda i, iref: (i, 0))]
    ins = [w]
    if after is not None:
        in_specs.append(pl.BlockSpec(memory_space=pl.ANY))
        ins.append(after)
    return _pallas(
        body, name=name,
        grid_spec=pltpu.PrefetchScalarGridSpec(
            num_scalar_prefetch=1, grid=(nr,), in_specs=in_specs, out_specs=pl.BlockSpec((tr, c), place)),
        out_shape=pltpu.HBM(geom.full_shape, BF16),
        compiler_params=_params(("parallel",), 2 * _nbytes((tr, c), F32)),
    )(idx, *ins)


def _adamw(name, w, g, m, v):
    r, c = w.shape
    tr = _tile(r, 128, 8)
    bc1 = 1.0 - ADAM_B1 ** ADAM_STEP
    bc2 = 1.0 - ADAM_B2 ** ADAM_STEP

    def body(w_ref, g_ref, m_ref, v_ref, go_ref, d_ref, nm_ref, nv_ref):
        gv = g_ref[...]
        go_ref[...] = gv
        nm = ADAM_B1 * m_ref[...] + (1.0 - ADAM_B1) * gv
        nv = ADAM_B2 * v_ref[...] + (1.0 - ADAM_B2) * (gv * gv)
        nm_ref[...] = nm
        nv_ref[...] = nv
        d_ref[...] = -ADAM_LR * ((nm / bc1) / (jnp.sqrt(nv / bc2) + ADAM_EPS) + ADAM_WD * w_ref[...])

    blk = pl.BlockSpec((tr, c), lambda i: (i, 0))
    out = jax.ShapeDtypeStruct((r, c), F32)
    return _pallas(
        body, name=name, grid=(r // tr,), in_specs=[blk] * 4, out_specs=[blk] * 4, out_shape=[out] * 4,
        compiler_params=_params(("parallel",), 8 * _nbytes((tr, c), F32)),
    )(w, g, m, v)


def _mesh_pos():
    return lax.axis_index("x"), lax.axis_index("y"), lax.axis_index("c")


def _other_chips(x, y):
    return [(1 - x, y), (x, 1 - y), (1 - x, 1 - y)]


class _Big:
    def __init__(self, shard_shape, col_sharded):
        self.col = col_sharded
        r, c = shard_shape
        self.shard_shape = (r, c)
        self.full_shape = (r, N_CHIPS * c) if col_sharded else (N_CHIPS * r, c)
        self.half_shape = (r // 2, N_CHIPS * c) if col_sharded else (N_CHIPS * r, c // 2)
        self.shard_half_shape = (r // 2, c) if col_sharded else (r, c // 2)

    def region(self, ref, s, half=None):
        r, c = self.shard_shape
        if self.col:
            rows = slice(None) if half is None else pl.ds(half * (r // 2), r // 2)
            return ref.at[rows, pl.ds(s * c, c)]
        cols = slice(None) if half is None else pl.ds(half * (c // 2), c // 2)
        return ref.at[pl.ds(s * r, r), cols]

    def n_halves(self, ref, half, n):
        r, c = self.shard_shape
        if self.col:
            return ref.at[pl.ds(half * (r // 2), r // 2), pl.ds(0, n * c)]
        return ref.at[pl.ds(0, n * r), pl.ds(half * (c // 2), c // 2)]

    def three_halves(self, ref, half):
        return self.n_halves(ref, half, 3)

    def sub_half(self, ref, s, half, j):
        r, c = self.shard_shape
        if self.col:
            return ref.at[pl.ds(half * (r // 2) + j * (r // 4), r // 4), pl.ds(s * c, c)]
        return ref.at[pl.ds(s * r + j * (r // 2), r // 2), pl.ds(half * (c // 2), c // 2)]

    def half_of_full(self, ref, half):
        r, c = self.full_shape
        if self.col:
            return ref.at[pl.ds(half * (r // 2), r // 2), :]
        return ref.at[:, pl.ds(half * (c // 2), c // 2)]

    def half_of_shard(self, ref, half):
        r, c = self.shard_shape
        if self.col:
            return ref.at[pl.ds(half * (r // 2), r // 2), :]
        return ref.at[:, pl.ds(half * (c // 2), c // 2)]

    def shard_of_half(self, ref, s):
        r, c = self.shard_shape
        if self.col:
            return ref.at[:, pl.ds(s * c, c)]
        return ref.at[pl.ds(s * r, r), :]


HBM =pl.BlockSpec(memory_space=pltpu.HBM)
SEM = pl.BlockSpec(memory_space=pltpu.SEMAPHORE)
SPLIT_COPY = pltpu.CompilerParams(has_side_effects=pltpu.SideEffectType.DATAFLOW_SIDE_EFFECTING)


def _in_hbm(a):
    return pltpu.with_memory_space_constraint(a, pltpu.HBM)


def _gather_start(name, fulls, geoms, after):
    nw = len(fulls)

    def body(*refs):
        dst = refs[nw + 1:2 * nw + 1]
        sems = refs[2 * nw + 1:-1]
        x, y, c = _mesh_pos()
        mine = 2 * x + y
        for w in range(nw):
            own_half = geoms[w].region(dst[w], mine, c)
            for chip in _other_chips(x, y):
                pltpu.make_async_remote_copy(src_ref=own_half, dst_ref=own_half, send_sem=sems[2 * w],
                                             recv_sem=sems[2 * w + 1], device_id=(*chip, c),
                                             device_id_type=MESH).start()
        refs[-1][...] = jnp.zeros_like(refs[-1])

    out = _pallas(
        body, name=name, in_specs=[HBM] * nw + [pl.BlockSpec(memory_space=pl.ANY)],
        out_specs=[HBM] * nw + [SEM] * (2 * nw) + [pl.BlockSpec(memory_space=pltpu.VMEM)],
        out_shape=[pltpu.HBM(g.full_shape, BF16) for g in geoms] + [pltpu.SemaphoreType.DMA(())] * (2 * nw)
        + [jax.ShapeDtypeStruct((8, 128), F32)],
        input_output_aliases={w: w for w in range(nw)}, compiler_params=SPLIT_COPY,
    )(*[_in_hbm(a) for a in fulls], after)
    return list(out[:nw]), [(out[nw + 2 * w], out[nw + 2 * w + 1]) for w in range(nw)], out[-1]


def _gather_first_direct(full, geom):
    def start(refs, _, new):
        x, y, c = _mesh_pos()
        own = geom.region(refs[0], 2 * x + y, c)
        for chip in ((1 - x, y), (x, 1 - y)):
            _remote(own, own, new, (*chip, c)).start()

    return _split_copy_call("gather_first_direct", [full], start, new_sems=2)


def _gather_first_relay(full, geom, sems, after):
    def relay(refs, got, new):
        x, y, c = _mesh_pos()
        w = refs[0]
        two = geom.n_halves(w, c, 2)
        _remote(two, two, got, (x, y, 1 - c)).wait_recv()
        from_x = geom.sub_half(w, 2 * (1 - x) + y, c, 0)
        from_y = geom.sub_half(w, 2 * x + (1 - y), c, 1)
        _remote(from_x, from_x, new, (x, 1 - y, c)).start()
        _remote(from_y, from_y, new, (1 - x, y, c)).start()
        _remote(two, two, got, (x, y, 1 - c)).wait_send()

    return _split_copy_call("gather_first_relay", [full], relay, sems=sems, after=after, new_sems=2)


def _gather_forward(name, full, geom, sems, after, arrivals=3, only_diagonal=False):
    def body(w_in, send_sem, recv_sem, after_ref, w_ref, fwd_send, fwd_recv):
        x, y, c = _mesh_pos()
        sibling = (x, y, 1 - c)
        landed_all = geom.n_halves(w_ref, c, arrivals)
        _remote(landed_all, landed_all, (send_sem, recv_sem), sibling).wait_recv()
        for chip in _other_chips(x, y)[2 if only_diagonal else 0:]:
            landed = geom.region(w_ref, 2 * chip[0] + chip[1], c)
            pltpu.make_async_remote_copy(src_ref=landed, dst_ref=landed, send_sem=fwd_send, recv_sem=fwd_recv,
                                         device_id=sibling, device_id_type=MESH).start()
        _remote(landed_all, landed_all, (send_sem, recv_sem), sibling).wait_send()

    sem = pltpu.SemaphoreType.DMA(())
    out = _pallas(
        body, name=name, in_specs=[HBM, SEM, SEM, pl.BlockSpec(memory_space=pl.ANY)], out_specs=[HBM, SEM, SEM],
        out_shape=[pltpu.HBM(geom.full_shape, BF16), sem, sem],
        input_output_aliases={0: 0}, compiler_params=SPLIT_COPY,
    )(full, sems[0], sems[1], after)
    return out[0], (out[1], out[2])


def _gather_end(name, full, geom, sems, after, halves=3):
    def body(w_in, fwd_send, fwd_recv, after_ref, w_ref):
        x, y, c = _mesh_pos()
        sibling = (x, y, 1 - c)
        theirs, ours = geom.n_halves(w_ref, 1 - c, halves), geom.n_halves(w_ref, c, halves)
        _remote(theirs, theirs, (fwd_send, fwd_recv), sibling).wait_recv()
        _remote(ours, ours, (fwd_send, fwd_recv), sibling).wait_send()

    return _pallas(
        body, name=name, in_specs=[HBM, SEM, SEM, pl.BlockSpec(memory_space=pl.ANY)], out_specs=HBM,
        out_shape=pltpu.HBM(geom.full_shape, BF16),
        input_output_aliases={0: 0}, compiler_params=SPLIT_COPY,
    )(full, sems[0], sems[1], after)


def _split_copy_call(name, arrays, fn, sems=(), after=None, new_sems=0):
    n, ns = len(arrays), len(sems)
    n_in = n + ns + (after is not None)

    def body(*refs):
        fn(refs[n_in:n_in + n], refs[n:n + ns], refs[n_in + n:-1])
        refs[-1][...] = jnp.zeros_like(refs[-1])

    ins = list(arrays) if ns else [_in_hbm(a) for a in arrays]
    ins += list(sems) + ([after] if after is not None else [])
    in_specs = [HBM] * n + [SEM] * ns + ([pl.BlockSpec(memory_space=pl.ANY)] if after is not None else [])
    out = _pallas(
        body, name=name, in_specs=in_specs,
        out_specs=[HBM] * n + [SEM] * new_sems + [pl.BlockSpec(memory_space=pltpu.VMEM)],
        out_shape=[pltpu.HBM(a.shape, a.dtype) for a in arrays] + [pltpu.SemaphoreType.DMA(())] * new_sems
        + [jax.ShapeDtypeStruct((8, 128), F32)],
        input_output_aliases={i: i for i in range(n)}, compiler_params=SPLIT_COPY,
    )(*ins)
    return list(out[:n]), tuple(out[n:-1]), out[-1]


def _remote(src, dst, sems, to):
    return pltpu.make_async_remote_copy(src_ref=src, dst_ref=dst, send_sem=sems[0], recv_sem=sems[1],
                                        device_id=to, device_id_type=MESH)


class _GradReduce:
    def __init__(self, name, geom, idx, c_idx):
        self.name, self.geom, self.idx, self.c_idx = name, geom, idx, c_idx

    def pair_start(self, theirs):
        g = self.geom

        def start(refs, _, new):
            x, y, c = _mesh_pos()
            _remote(refs[0], refs[1], new, (x, y, 1 - c)).start()

        self.arrays, self.sems, token = _split_copy_call(
            f"pair_start_{self.name}", [theirs, lax.empty(g.half_shape, BF16)], start, new_sems=2)
        return token

    def pair_wait(self, after):
        def wait(refs, sems, _):
            x, y, c = _mesh_pos()
            copy = _remote(refs[0], refs[1], sems, (x, y, 1 - c))
            copy.wait_send()
            copy.wait_recv()

        (_, landed), _, _ = _split_copy_call(f"pair_wait_{self.name}", self.arrays, wait, self.sems, after)
        return landed

    def chip_start(self, half):
        g = self.geom

        def start(refs, _, new):
            x, y, c = _mesh_pos()
            for k, chip in enumerate(_other_chips(x, y)):
                _remote(g.shard_of_half(refs[0], 2 * chip[0] + chip[1]), refs[1].at[k], new, (*chip, c)).start()

        self.arrays, self.sems, token = _split_copy_call(
            f"chip_start_{self.name}", [half, lax.empty((3,) + g.shard_half_shape, BF16)], start, new_sems=2)
        return token

    def chip_finish(self, after):
        g = self.geom

        def wait(refs, sems, _):
            x, y, c = _mesh_pos()
            three = _remote(refs[1], refs[1], sems, (x, y, 1 - c))
            three.wait_send()
            three.wait_recv()

        (half, landed), _, _ = _split_copy_call(f"chip_wait_{self.name}", self.arrays, wait, self.sems, after)
        quarter = _chip_add(f"chip_add_{self.name}", half, landed, g, self.idx)

        def start(refs, _, new):
            x, y, c = _mesh_pos()
            own = g.half_of_shard(refs[0], c)
            _remote(own, own, new, (x, y, 1 - c)).start()

        self.arrays, self.sems, token = _split_copy_call(f"share_start_{self.name}", [quarter], start, new_sems=2)
        return token

    def finish(self, after):
        g = self.geom

        def wait(refs, sems, _):
            x, y, c = _mesh_pos()
            own, theirs = g.half_of_shard(refs[0], c), g.half_of_shard(refs[0], 1 - c)
            _remote(own, own, sems, (x, y, 1 - c)).wait_send()
            _remote(theirs, theirs, sems, (x, y, 1 - c)).wait_recv()

        (quarter,), _, _ = _split_copy_call(f"share_wait_{self.name}", self.arrays, wait, self.sems, after)
        return quarter


def _dw_half(name, x, dy, geom, c_idx, own, addend=None, after=None):
    stacked = dy.ndim == 3
    t, m = x.shape
    n = 2 * dy.shape[2] if stacked else dy.shape[1]
    hm, hn = (m // 2, n) if geom.col else (m, n // 2)
    tm, tn, tk = _mm_tiles(hm, hn, t, BF16, n_unit=(n // 2 if stacked else None))
    if tk != t:
        tm, tn = _tile(hm, 512, 128), _tile(hn // (2 if stacked else 1), 512, 128)
    gi, gj = hm // tm, hn // tn
    nf = (n // 2) // tn

    def sel(cref):
        return cref[0] if own else 1 - cref[0]

    a_map = (lambda i, j, cref: (0, sel(cref) * gi + i)) if geom.col else (lambda i, j, cref: (0, i))
    if stacked:
        b_blk, b_map = (None, t, tn), (lambda i, j, cref: (j // nf, 0, j % nf))
    elif geom.col:
        b_blk, b_map = (t, tn), (lambda i, j, cref: (0, j))
    else:
        b_blk, b_map = (t, tn), (lambda i, j, cref: (0, sel(cref) * gj + j))
    out_blk = pl.BlockSpec((tm, tn), lambda i, j, cref: (i, j))
    ins, in_specs = [x, dy], [pl.BlockSpec((t, tm), a_map), pl.BlockSpec(b_blk, b_map)]
    if addend is not None:
        ins.append(addend)
        in_specs.append(out_blk)
    if after is not None:
        ins.append(after)
        in_specs.append(pl.BlockSpec(memory_space=pl.ANY))

    def body(c_ref, *refs):
        acc = _dot(refs[0][...], refs[1][...], 0, 0)
        if addend is not None:
            acc = acc + refs[2][...].astype(F32)
        refs[len(ins)][...] = acc.astype(BF16)

    return _pallas(
        body, name=name,
        grid_spec=pltpu.PrefetchScalarGridSpec(num_scalar_prefetch=1, grid=(gi, gj), in_specs=in_specs,
                                               out_specs=out_blk),
        out_shape=jax.ShapeDtypeStruct((hm, hn), BF16),
        compiler_params=_params(("parallel", "parallel"),
                                _nbytes((t, tm), BF16) + _nbytes((t, tn), BF16) + 3 * _nbytes((tm, tn), F32)),
    )(c_idx, *ins)


def _chip_add(name, half, recv, geom, idx):
    r, c = geom.shard_half_shape
    tr, tc = _tile(r, 512, 16), _tile(c, 2048, 128)
    nr, ncol = r // tr, c // tc
    if geom.col:
        mine = lambda i, j, iref: (i, iref[0] * ncol + j)
        place = lambda i, j, iref: (iref[1] * nr + i, j)
    else:
        mine = lambda i, j, iref: (iref[0] * nr + i, j)
        place = lambda i, j, iref: (i, iref[1] * ncol + j)

    def body(i_ref, h_ref, r_ref, o_ref):
        acc = h_ref[...].astype(F32)
        for k in range(3):
            acc = acc + r_ref[k].astype(F32)
        o_ref[...] = acc

    return _pallas(
        body, name=name,
        grid_spec=pltpu.PrefetchScalarGridSpec(
            num_scalar_prefetch=1, grid=(nr, ncol),
            in_specs=[pl.BlockSpec((tr, tc), mine), pl.BlockSpec((3, tr, tc), lambda i, j, iref: (0, i, j))],
            out_specs=pl.BlockSpec((tr, tc), place)),
        out_shape=jax.ShapeDtypeStruct(geom.shard_shape, F32),
        compiler_params=_params(("parallel", "parallel"), 4 * _nbytes((tr, tc), F32)),
    )(idx, half, recv)


def _all_reduce_small(pack, after=None):
    r, d = pack.shape

    def body(p_ref, o_ref, slots, send_sems, recv_sems):
        x, y, c = _mesh_pos()
        me = 4 * x + 2 * y + c
        slots[me] = p_ref[...]
        copies = []
        for k in range(1, N_DEV):
            px, py, pc = x ^ ((k >> 2) & 1), y ^ ((k >> 1) & 1), c ^ (k & 1)
            copies.append(pltpu.make_async_remote_copy(
                src_ref=p_ref, dst_ref=slots.at[me], send_sem=send_sems.at[k - 1], recv_sem=recv_sems.at[k - 1],
                device_id=(px, py, pc), device_id_type=MESH))
        for cp in copies:
            cp.start()
        for k in range(1, N_DEV):
            peer = 4 * (x ^ ((k >> 2) & 1)) + 2 * (y ^ ((k >> 1) & 1)) + (c ^ (k & 1))
            pltpu.make_async_remote_copy(
                src_ref=p_ref, dst_ref=slots.at[peer], send_sem=send_sems.at[k - 1], recv_sem=recv_sems.at[k - 1],
                device_id=(x, y, c), device_id_type=MESH).wait_recv()
        for cp in copies:
            cp.wait_send()
        acc = slots[0]
        for k in range(1, N_DEV):
            acc = acc + slots[k]
        o_ref[...] = acc

    vm = pl.BlockSpec(memory_space=pltpu.VMEM)
    body, ins, in_specs = _ordered(body, [pack], [vm], after)
    return _pallas(
        body, name="all_reduce_small", in_specs=in_specs, out_specs=vm,
        out_shape=jax.ShapeDtypeStruct((r, d), F32),
        scratch_shapes=[pltpu.VMEM((N_DEV, r, d), F32), pltpu.SemaphoreType.DMA((N_DEV - 1,)),
                        pltpu.SemaphoreType.DMA((N_DEV - 1,))],
    )(*ins)


def _pack_rows(rows, d):
    out = []
    for a in rows:
        flat = a.reshape(-1)
        n = -(-flat.shape[0] // d) * d
        out.append(jnp.pad(flat, (0, n - flat.shape[0])).reshape(-1, d))
    packed = jnp.concatenate(out, axis=0)
    return jnp.pad(packed, ((0, 16 - packed.shape[0]), (0, 0)))


def _unpack_rows(packed, shapes, d):
    out, row = [], 0
    for shp in shapes:
        n = int(np.prod(shp))
        nrows = -(-n // d)
        out.append(packed[row:row + nrows].reshape(-1)[:n].reshape(shp))
        row += nrows
    return out


def kernel(x, pre_norm_ffn1, post_norm_ffn1, w_ffn1_gate_up, w_ffn1_down, pre_norm_mix, post_norm_mix, w_mix_in, hgrn_lower_bounds_fwd, hgrn_lower_bounds_bwd, hgrn_out_norm, attn_sink, w_mix_out, pre_norm_ffn2, post_norm_ffn2, w_ffn2_gate_up, w_ffn2_down, rel_bias_table, loss_target, m_pre_norm_ffn1, m_post_norm_ffn1, m_w_ffn1_gate_up, m_w_ffn1_down, m_pre_norm_mix, m_post_norm_mix, m_w_mix_in, m_hgrn_lower_bounds_fwd, m_hgrn_lower_bounds_bwd, m_hgrn_out_norm, m_attn_sink, m_w_mix_out, m_pre_norm_ffn2, m_post_norm_ffn2, m_w_ffn2_gate_up, m_w_ffn2_down, m_rel_bias_table, v_pre_norm_ffn1, v_post_norm_ffn1, v_w_ffn1_gate_up, v_w_ffn1_down, v_pre_norm_mix, v_post_norm_mix, v_w_mix_in, v_hgrn_lower_bounds_fwd, v_hgrn_lower_bounds_bwd, v_hgrn_out_norm, v_attn_sink, v_w_mix_out, v_pre_norm_ffn2, v_post_norm_ffn2, v_w_ffn2_gate_up, v_w_ffn2_down, v_rel_bias_table):
    t, d = x.shape[1], x.shape[2]
    hw = hgrn_out_norm.shape[1]
    aw = d - hw
    nah = aw // HEAD
    kvw = KV_HEADS * HEAD
    x0 = x[0]
    target = loss_target[0]

    big_names = ["w_ffn1_gate_up", "w_ffn1_down", "w_mix_in", "w_mix_out", "w_ffn2_gate_up", "w_ffn2_down"]
    big_w = [w_ffn1_gate_up[0], w_ffn1_down[0], w_mix_in[0], w_mix_out[0], w_ffn2_gate_up[0], w_ffn2_down[0]]
    big_m = [m_w_ffn1_gate_up[0], m_w_ffn1_down[0], m_w_mix_in[0], m_w_mix_out[0], m_w_ffn2_gate_up[0],
             m_w_ffn2_down[0]]
    big_v = [v_w_ffn1_gate_up[0], v_w_ffn1_down[0], v_w_mix_in[0], v_w_mix_out[0], v_w_ffn2_gate_up[0],
             v_w_ffn2_down[0]]
    col_sharded = [True, False, True, False, True, False]
    geoms = [_Big(w.shape, cs) for w, cs in zip(big_w, col_sharded)]

    cx, cy, cc = _mesh_pos()
    idx = jnp.stack([2 * cx + cy, cc]).astype(jnp.int32)
    c_idx = jnp.reshape(cc, (1,)).astype(jnp.int32)
    first = _cast_into_full(f"cast_{big_names[0]}", big_w[0], geoms[0], idx)
    (first,), direct_sems, tok = _gather_first_direct(first, geoms[0])
    rest = []
    for n, w, gm in zip(big_names[1:], big_w[1:], geoms[1:]):
        tok = _cast_into_full(f"cast_{n}", w, gm, idx, after=tok)
        rest.append(tok)
    (first,), relay_sems, tok = _gather_first_relay(first, geoms[0], direct_sems, after=tok)
    started_rest, sems_rest, rest_started = _gather_start("gather_start_rest", rest, geoms[1:], after=tok)
    started, gather_sems = [first] + started_rest, [relay_sems] + sems_rest

    def forward_weight(w, after):
        return _gather_forward(f"gather_forward_{big_names[w]}", started[w], geoms[w], gather_sems[w], after,
                               arrivals=1 if w == 0 else 3)

    def whole_weight(w, forwarded, after):
        return _gather_end(f"gather_end_{big_names[w]}", forwarded[0], geoms[w], forwarded[1], after)

    h1 = _norm_fwd("ffn1_pre_norm", x0, pre_norm_ffn1)
    w_gu1 = whole_weight(0, forward_weight(0, rest_started), h1)
    act1, dact_dgate1, dact_dup1 = _ffn_gate_up_act("ffn1_gate_up", h1, w_gu1)
    w_d1 = whole_weight(1, forward_weight(1, act1), act1)
    ff1 = _mm("ffn1_down", act1, w_d1, "nn", F32)
    fw = forward_weight(2, ff1)
    x1, hm = _resid_norm_fwd("ffn1_residual", x0, ff1, post_norm_ffn1, pre_norm_mix, 0.5)
    w_in = whole_weight(2, fw, hm)
    p = _mm("mix_in", hm, w_in, "nn", F32)
    fw = forward_weight(3, p)
    o_f, o_b, st_f, st_b, gates_f, gates_b, dec_f, dec_b = _hgrn_scan_fwd(
        "hgrn_scan", p, hgrn_lower_bounds_fwd, hgrn_lower_bounds_bwd)
    y_h = _hgrn_out_fwd("hgrn_out", o_f, o_b, p, hgrn_out_norm, 4)
    kv_blk0 = (5 * hw + aw) // kvw
    k_pad = _pad_kv("attn_pad_k", p, kv_blk0, kvw)
    v_pad = _pad_kv("attn_pad_v", p, kv_blk0 + 1, kvw)
    bucket_ids = _t5_bucket_ids()
    bias = _bias_gather("attn_bias", rel_bias_table.T, bucket_ids).reshape(nah, WINDOW, SPAN)
    y_a, attn_probs, attn_sink_probs = _attn_fwd("attn_fwd", p, k_pad, v_pad, bias, attn_sink, 5 * hw // aw)
    y_mix = _concat_cols("mix_concat", y_h, y_a)
    w_out = whole_weight(3, fw, y_mix)
    mixed = _mm("mix_out", y_mix, w_out, "nn", F32)
    fw = forward_weight(4, mixed)
    x2, h2 = _resid_norm_fwd("mix_residual", x1, mixed, post_norm_mix, pre_norm_ffn2, 1.0)
    w_gu2 = whole_weight(4, fw, h2)
    act2, dact_dgate2, dact_dup2 = _ffn_gate_up_act("ffn2_gate_up", h2, w_gu2)
    w_d2 = whole_weight(5, forward_weight(5, act2), act2)
    ff2 = _mm("ffn2_down", act2, w_d2, "nn", F32)
    loss_blk, dy, dff2, dg_post2 = _final_fwd_bwd("ffn2_residual_loss", x2, ff2, post_norm_ffn2, target, 0.5)

    reduce = [_GradReduce(n, gm, idx, c_idx) for n, gm in zip(big_names, geoms)]
    big_grads, big_delta, big_new_m, big_new_v = [None] * 6, [None] * 6, [None] * 6, [None] * 6

    def update(w, after):
        g, dl, nm, nv = _adamw(f"adamw_{big_names[w]}", big_w[w], reduce[w].finish(after), big_m[w], big_v[w])
        big_grads[w], big_delta[w], big_new_m[w], big_new_v[w] = g[None], dl[None], nm[None], nv[None]
        return dl

    def dw_start(w, x_act, dy_act, after=None):
        theirs = _dw_half(f"dw_theirs_{big_names[w]}", x_act, dy_act, geoms[w], c_idx, own=False, after=after)
        return reduce[w].pair_start(theirs)

    def dw_finish(w, x_act, dy_act, after):
        landed = reduce[w].pair_wait(after)
        half = _dw_half(f"dw_own_{big_names[w]}", x_act, dy_act, geoms[w], c_idx, own=True, addend=landed)
        return reduce[w].chip_start(half)

    tok = dw_start(5, act2, dff2)
    dgu2 = _ffn_dact("ffn2_dact", dff2, w_d2, dact_dgate2, dact_dup2, after=tok)
    tok = dw_finish(5, act2, dff2, after=dgu2)
    tok = dw_start(4, h2, dgu2, after=tok)
    dh2 = _ffn_dh("ffn2_dh", dgu2, w_gu2, after=tok)
    tok = dw_finish(4, h2, dgu2, after=dh2)
    dx2, dg_pre2, dmixed, dg_postm = _norms_bwd("mix_residual_bwd", dy, dh2, x2, pre_norm_ffn2,
                                                post=(mixed, post_norm_mix, 1.0), after=tok)
    tok = dw_start(3, y_mix, dmixed)
    dy_mix = _mm("mix_out_dx", dmixed, w_out, "nt", F32, after=tok)
    tok = dw_finish(3, y_mix, dmixed, after=dy_mix)
    dq_a, dk_pad, dv_pad, dbias, dsink = _attn_bwd("attn_bwd", p, k_pad, v_pad, attn_probs, attn_sink_probs, dy_mix,
                                                   5 * hw // aw, hw // aw, after=tok)
    tok = reduce[5].chip_finish(dq_a)
    drel_t = _bias_scatter("attn_dbias", dbias.reshape(nah, WINDOW * SPAN), bucket_ids)
    do, dg_h, dgain = _hgrn_out_bwd("hgrn_out_bwd", dy_mix, o_f, o_b, p, hgrn_out_norm, 4, after=tok)
    dq_f, dv_f, dz_f, dlb_f, dq_b, dv_b, dz_b, dlb_b = _hgrn_scan_bwd(
        "hgrn_scan_bwd", p, hgrn_lower_bounds_fwd, hgrn_lower_bounds_bwd, do, (st_f, gates_f, dec_f),
        (st_b, gates_b, dec_b))
    tok = reduce[4].chip_finish(dq_f)
    tok = reduce[3].chip_finish(tok)
    dp = _mix_dproj("mix_dproj", [(dq_f, dq_b), (dv_f, dv_b), (dz_f,), (dz_b,), (dg_h,), (dq_a,)],
                    [dk_pad, dv_pad], t, after=tok)
    tok = dw_start(2, hm, dp)
    dhm = _mm("mix_in_dx", dp, w_in, "nt", F32, after=tok)
    tok = dw_finish(2, hm, dp, after=dhm)
    dx1, dg_prem, dff1, dg_post1 = _norms_bwd("ffn1_residual_bwd", dx2, dhm, x1, pre_norm_mix,
                                              post=(ff1, post_norm_ffn1, 0.5), after=tok)
    tok = dw_start(1, act1, dff1)
    dgu1 = _ffn_dact("ffn1_dact", dff1, w_d1, dact_dgate1, dact_dup1, after=tok)
    tok = dw_finish(1, act1, dff1, after=dgu1)
    tok = reduce[2].chip_finish(tok)
    tok = dw_start(0, h1, dgu1, after=tok)
    done = update(2, tok)
    tok = dw_finish(0, h1, dgu1, after=done)
    dh1 = _ffn_dh("ffn1_dh", dgu1, w_gu1, after=tok)
    grad_x, dg_pre1 = _norms_bwd("ffn1_pre_norm_bwd", dx1, dh1, x0, pre_norm_ffn1)

    small_w = [pre_norm_ffn1, post_norm_ffn1, pre_norm_mix, post_norm_mix, hgrn_lower_bounds_fwd,
               hgrn_lower_bounds_bwd, hgrn_out_norm, attn_sink, pre_norm_ffn2, post_norm_ffn2, rel_bias_table]
    small_m = [m_pre_norm_ffn1, m_post_norm_ffn1, m_pre_norm_mix, m_post_norm_mix, m_hgrn_lower_bounds_fwd,
               m_hgrn_lower_bounds_bwd, m_hgrn_out_norm, m_attn_sink, m_pre_norm_ffn2, m_post_norm_ffn2,
               m_rel_bias_table]
    small_v = [v_pre_norm_ffn1, v_post_norm_ffn1, v_pre_norm_mix, v_post_norm_mix, v_hgrn_lower_bounds_fwd,
               v_hgrn_lower_bounds_bwd, v_hgrn_out_norm, v_attn_sink, v_pre_norm_ffn2, v_post_norm_ffn2,
               v_rel_bias_table]
    small_g = [dg_pre1, dg_post1, dg_prem, dg_postm, dlb_f, dlb_b, dgain, dsink[:, 0].reshape(1, nah), dg_pre2,
               dg_post2, drel_t.T]
    shapes = [a.shape for a in small_w]
    done = update(5, grad_x)
    done = update(4, done)
    done = update(3, done)
    summed = _all_reduce_small(_pack_rows(small_g + [loss_blk[0:1, 0:1]], d), after=done)
    loss = _unpack_rows(summed, shapes + [(1, 1)], d)[-1][0, 0]
    _, sd, sm, sv = _adamw("adamw_small", _pack_rows(small_w, d), summed, _pack_rows(small_m, d),
                           _pack_rows(small_v, d))
    small_grads = _unpack_rows(summed, shapes, d)
    small_delta, small_new_m, small_new_v = (_unpack_rows(a, shapes, d) for a in (sd, sm, sv))

    tok = reduce[1].chip_finish(sd)
    done = update(1, tok)
    tok = reduce[0].chip_finish(done)
    update(0, tok)

    def ordered(small, big):
        s = dict(zip(["pre1", "post1", "prem", "postm", "lbf", "lbb", "gain", "sink", "pre2", "post2", "rel"], small))
        b = dict(zip(["gu1", "d1", "win", "wout", "gu2", "d2"], big))
        return [s["pre1"], s["post1"], b["gu1"], b["d1"], s["prem"], s["postm"], b["win"], s["lbf"], s["lbb"],
                s["gain"], s["sink"], b["wout"], s["pre2"], s["post2"], b["gu2"], b["d2"], s["rel"]]

    return (loss, grad_x[None], *ordered(small_grads, big_grads), *ordered(small_delta, big_delta),
            *ordered(small_new_m, big_new_m), *ordered(small_new_v, big_new_v))
```

```python
import functools
import math

import jax
import jax.numpy as jnp
import numpy as np
from jax import lax
from jax.experimental import pallas as pl
from jax.experimental.pallas import tpu as pltpu

F32 = jnp.float32
BF16 = jnp.bfloat16

HEAD = 128
CHUNK = 64
WINDOW = 128
SPAN = 3 * WINDOW
KV_HEADS = 2
REL_BUCKETS = 32
REL_MAX_DIST = 128
EPS = 1e-6
NEG_INF = -1e30

ADAM_LR = 0.001
ADAM_B1 = 0.9
ADAM_B2 = 0.999
ADAM_EPS = 1e-08
ADAM_WD = 0.01
ADAM_STEP = 10

N_CHIPS = 4
N_DEV = 8
V7X_VMEM_BYTES = 64 * 1024 * 1024
MESH = pl.DeviceIdType.MESH
ANY = pl.BlockSpec(memory_space=pl.ANY)


def _tile(n, pref, mult):
    t = (min(pref, n) // mult) * mult
    while t >= mult:
        if n % t == 0:
            return t
        t -= mult
    return n


def _params(semantics, block_bytes):
    limit = min(V7X_VMEM_BYTES - (4 << 20), 2 * int(block_bytes) + (8 << 20))
    return pltpu.CompilerParams(dimension_semantics=semantics, vmem_limit_bytes=limit)


def _nbytes(shape, dtype):
    return int(np.prod(shape)) * jnp.dtype(dtype).itemsize


PIN_TO_HBM_BYTES = 4 << 20


def _pallas(body, **kw):
    def pin_shape(s):
        if isinstance(s, jax.ShapeDtypeStruct) and _nbytes(s.shape, s.dtype) >= PIN_TO_HBM_BYTES:
            return pltpu.HBM(s.shape, s.dtype)
        return s

    def pin(a):
        if getattr(a, "dtype", None) in (F32, BF16) and _nbytes(a.shape, a.dtype) >= PIN_TO_HBM_BYTES:
            return pltpu.with_memory_space_constraint(a, pltpu.HBM)
        return a

    out_shape = kw["out_shape"]
    kw["out_shape"] = [pin_shape(s) for s in out_shape] if isinstance(out_shape, (list, tuple)) else pin_shape(out_shape)
    call = pl.pallas_call(body, **kw)
    return lambda *args: call(*[pin(a) for a in args])


def _dot(a, b, ca=1, cb=0):
    return lax.dot_general(a, b, (((ca,), (cb,)), ((), ())), preferred_element_type=F32)


def _split3(x):
    hi = x.astype(BF16)
    r1 = x - hi.astype(F32)
    mid = r1.astype(BF16)
    lo = (r1 - mid.astype(F32)).astype(BF16)
    return hi, mid, lo


def _dot_exact(a, b, ca=1, cb=0, split="b"):
    if split == "b":
        return sum(_dot(a, p, ca, cb) for p in _split3(b))
    return sum(_dot(p, b, ca, cb) for p in _split3(a))


def _rms(x):
    return lax.rsqrt(jnp.mean(x * x, axis=-1, keepdims=True) + EPS)


def _norm_bwd(u, x, gain):
    r = _rms(x)
    xhat = x * r
    dgain = jnp.sum(u * xhat, axis=0, keepdims=True)
    v = u * gain
    dx = r * (v - xhat * jnp.mean(v * xhat, axis=-1, keepdims=True))
    return dx, dgain


def _sigmoid(x):
    return 1.0 / (1.0 + jnp.exp(-x))


def _accumulate(ref, val, first):
    @pl.when(first)
    def _():
        ref[...] = val

    @pl.when(jnp.logical_not(first))
    def _():
        ref[...] += val


def _ordered(body, ins, in_specs, after):
    if after is None:
        return body, list(ins), list(in_specs)
    n_in = len(ins)

    def wrapped(*refs):
        body(*refs[:n_in], *refs[n_in + 1:])

    return wrapped, list(ins) + [after], list(in_specs) + [pl.BlockSpec(memory_space=pl.ANY)]


def _matmul(name, a, b, *, form, out_dtype, tm, tn, tk, a_map=None, b_map=None,
            out_shape=None, out_block=None, out_map=None, sizes=None, after=None):
    if sizes is None:
        if form == "nn":
            (m, k), n = a.shape, b.shape[1]
        elif form == "nt":
            (m, k), n = a.shape, b.shape[0]
        else:
            (k, m), n = a.shape, b.shape[1]
    else:
        m, n, k = sizes
    gi, gj, gk = m // tm, n // tn, k // tk
    a_blk = (tm, tk) if form != "tn" else (tk, tm)
    b_blk = (tk, tn) if form != "nt" else (tn, tk)
    if a_map is None:
        a_map = (lambda i, j, kk: (i, kk)) if form != "tn" else (lambda i, j, kk: (kk, i))
    else:
        a_blk = (None,) + a_blk
    if b_map is None:
        b_map = (lambda i, j, kk: (kk, j)) if form != "nt" else (lambda i, j, kk: (j, kk))
    else:
        b_blk = (None,) + b_blk
    if out_shape is None:
        out_shape, out_block, out_map = (m, n), (tm, tn), (lambda i, j, kk: (i, j))
    ca, cb = {"nn": (1, 0), "nt": (1, 1), "tn": (0, 0)}[form]

    def body(a_ref, b_ref, o_ref, *acc):
        part = _dot(a_ref[...], b_ref[...], ca, cb)
        if gk == 1:
            o_ref[...] = part.astype(o_ref.dtype)
        else:
            kk = pl.program_id(2)
            _accumulate(acc[0], part, kk == 0)

            @pl.when(kk == gk - 1)
            def _():
                o_ref[...] = acc[0][...].astype(o_ref.dtype)

    scratch = [] if gk == 1 else [pltpu.VMEM((tm, tn), F32)]
    vmem = (_nbytes((tm, tk), a.dtype) + _nbytes((tk, tn), b.dtype) + _nbytes((tm, tn), out_dtype)
            + 2 * _nbytes((tm, tn), F32))
    body, ins, in_specs = _ordered(body, [a, b], [pl.BlockSpec(a_blk, a_map), pl.BlockSpec(b_blk, b_map)], after)
    return _pallas(
        body, name=name, grid=(gi, gj, gk), in_specs=in_specs,
        out_specs=pl.BlockSpec(out_block, out_map),
        out_shape=jax.ShapeDtypeStruct(out_shape, out_dtype),
        scratch_shapes=scratch,
        compiler_params=_params(("parallel", "parallel", "arbitrary"), vmem),
    )(*ins)


V7X_HBM_BYTES_PER_US = 3.0e6
V7X_MXU_FLOPS_PER_US = 0.9e9
V7X_VMEM_RMW_BYTES_PER_US = 10e6
GRID_STEP_US = 0.35
MATMUL_VMEM_BUDGET = 40 << 20
MATMUL_MAX_TILE_FLOPS = 1 << 33


def _divisors(n, mult, lo):
    return [t for t in range(mult, n + 1, mult) if n % t == 0 and t >= min(lo, n)]


def _mm_tiles(m, n, k, out_dtype=F32, n_unit=None, k_unit=None):
    out_bytes = jnp.dtype(out_dtype).itemsize
    best = None
    for tm in _divisors(m, 128, 256):
        for tn in _divisors(n_unit or n, 128, 256):
            for tk in _divisors(k_unit or k, 128, 512):
                gi, gj, gk = m // tm, n // tn, k // tk
                vmem = 4 * tm * tk + 4 * tk * tn + 2 * tm * tn * out_bytes + 4 * tm * tn * (2 if gk > 1 else 1)
                if vmem > MATMUL_VMEM_BUDGET or 2 * tm * tn * tk > MATMUL_MAX_TILE_FLOPS:
                    continue
                a_bytes = 2 * m * k * (gj if gk > 1 else 1)
                b_bytes = 2 * k * n * (1 if gj == 1 and gk == 1 else gi)
                hbm_us = (a_bytes + b_bytes + m * n * out_bytes) / V7X_HBM_BYTES_PER_US
                acc_us = (8 * m * n * gk / V7X_VMEM_RMW_BYTES_PER_US) if gk > 1 else 0.0
                cost = max(2 * m * n * k / V7X_MXU_FLOPS_PER_US, 1.3 * hbm_us) + GRID_STEP_US * gi * gj * gk + acc_us
                key = (round(cost, 1), vmem)
                if best is None or key < best[0]:
                    best = (key, (tm, tn, tk))
    return best[1]


def _mm(name, a, b, form, out_dtype, after=None):
    if form == "nn":
        m, k, n = a.shape[0], a.shape[1], b.shape[1]
    elif form == "nt":
        m, k, n = a.shape[0], a.shape[1], b.shape[0]
    else:
        m, k, n = a.shape[1], a.shape[0], b.shape[1]
    tm, tn, tk = _mm_tiles(m, n, k, out_dtype)
    return _matmul(name, a, b, form=form, out_dtype=out_dtype, tm=tm, tn=tn, tk=tk, after=after)


def _row_tile(t):
    return _tile(t, 256, 8)


def _norm_fwd(name, x, gain):
    t, d = x.shape
    tm = _row_tile(t)

    def body(x_ref, g_ref, h_ref):
        xv = x_ref[...]
        h_ref[...] = (xv * _rms(xv) * g_ref[...]).astype(BF16)

    row = pl.BlockSpec((tm, d), lambda i: (i, 0))
    vec = pl.BlockSpec((1, d), lambda i: (0, 0))
    return _pallas(
        body, name=name, grid=(t // tm,), in_specs=[row, vec], out_specs=row,
        out_shape=jax.ShapeDtypeStruct((t, d), BF16),
        compiler_params=_params(("parallel",), 2 * _nbytes((tm, d), F32)),
    )(x, gain)


def _resid_norm_fwd(name, xres, ff, gpost, gpre, scale):
    t, d = xres.shape
    tm = _row_tile(t)

    def body(x_ref, f_ref, gp_ref, gn_ref, xn_ref, h_ref):
        f = f_ref[...]
        xn = x_ref[...] + scale * (f * _rms(f) * gp_ref[...])
        xn_ref[...] = xn
        h_ref[...] = (xn * _rms(xn) * gn_ref[...]).astype(BF16)

    row = pl.BlockSpec((tm, d), lambda i: (i, 0))
    vec = pl.BlockSpec((1, d), lambda i: (0, 0))
    return _pallas(
        body, name=name, grid=(t // tm,), in_specs=[row, row, vec, vec], out_specs=[row, row],
        out_shape=[jax.ShapeDtypeStruct((t, d), F32), jax.ShapeDtypeStruct((t, d), BF16)],
        compiler_params=_params(("parallel",), 4 * _nbytes((tm, d), F32)),
    )(xres, ff, gpost, gpre)


def _final_fwd_bwd(name, xres, ff, gpost, target, scale):
    t, d = xres.shape
    tm = _row_tile(t)

    def body(x_ref, f_ref, gp_ref, t_ref, loss_ref, dy_ref, dff_ref, dg_ref):
        i = pl.program_id(0)
        f = f_ref[...]
        gp = gp_ref[...]
        y = x_ref[...] + scale * (f * _rms(f) * gp)
        err = y - t_ref[...]
        part = 0.5 * jnp.sum(jnp.mean(err * err, axis=-1, keepdims=True), axis=0, keepdims=True)
        _accumulate(loss_ref, jnp.broadcast_to(part, loss_ref.shape), i == 0)
        dy = err / d
        dy_ref[...] = dy
        dff, dg = _norm_bwd(scale * dy, f, gp)
        dff_ref[...] = dff.astype(BF16)
        _accumulate(dg_ref, dg, i == 0)

    row = pl.BlockSpec((tm, d), lambda i: (i, 0))
    vec = pl.BlockSpec((1, d), lambda i: (0, 0))
    return _pallas(
        body, name=name, grid=(t // tm,), in_specs=[row, row, vec, row],
        out_specs=[pl.BlockSpec((8, 128), lambda i: (0, 0)), row, row, vec],
        out_shape=[jax.ShapeDtypeStruct((8, 128), F32), jax.ShapeDtypeStruct((t, d), F32),
                   jax.ShapeDtypeStruct((t, d), BF16), jax.ShapeDtypeStruct((1, d), F32)],
        compiler_params=_params(("arbitrary",), 5 * _nbytes((tm, d), F32)),
    )(xres, ff, gpost, target)


def _norms_bwd(name, dres, dh, xin, gpre, post=None, after=None):
    t, d = dres.shape
    tm = _row_tile(t)
    with_post = post is not None

    def body(*refs):
        if with_post:
            dr_ref, dh_ref, x_ref, g_ref, f_ref, gp_ref, dx_ref, dg_ref, dff_ref, dgp_ref = refs
        else:
            dr_ref, dh_ref, x_ref, g_ref, dx_ref, dg_ref = refs
        i = pl.program_id(0)
        dx, dg = _norm_bwd(dh_ref[...], x_ref[...], g_ref[...])
        dx = dr_ref[...] + dx
        dx_ref[...] = dx
        _accumulate(dg_ref, dg, i == 0)
        if with_post:
            dff, dgp = _norm_bwd(post[2] * dx, f_ref[...], gp_ref[...])
            dff_ref[...] = dff.astype(BF16)
            _accumulate(dgp_ref, dgp, i == 0)

    row = pl.BlockSpec((tm, d), lambda i: (i, 0))
    vec = pl.BlockSpec((1, d), lambda i: (0, 0))
    ins, in_specs = [dres, dh, xin, gpre], [row, row, row, vec]
    out_specs = [row, vec]
    out_shape = [jax.ShapeDtypeStruct((t, d), F32), jax.ShapeDtypeStruct((1, d), F32)]
    if with_post:
        ins += [post[0], post[1]]
        in_specs += [row, vec]
        out_specs += [row, vec]
        out_shape += [jax.ShapeDtypeStruct((t, d), BF16), jax.ShapeDtypeStruct((1, d), F32)]
    body, ins, in_specs = _ordered(body, ins, in_specs, after)
    return _pallas(
        body, name=name, grid=(t // tm,), in_specs=in_specs, out_specs=out_specs, out_shape=out_shape,
        compiler_params=_params(("arbitrary",), 6 * _nbytes((tm, d), F32)),
    )(*ins)


SWIGLU_TILE = (1024, 512)


def _ffn_gate_up_act(name, h, w_gu):
    t, d = h.shape
    f = w_gu.shape[1] // 2
    tm, tn = _tile(t, SWIGLU_TILE[0], 128), _tile(f, SWIGLU_TILE[1], 128)
    nf = f // tn

    def body(h_ref, wg_ref, wu_ref, a_ref, dg_ref, du_ref):
        hv = h_ref[...]
        g = _dot(hv, wg_ref[...])
        u = _dot(hv, wu_ref[...])
        sig = _sigmoid(g)
        silu = g * sig
        a_ref[...] = (silu * u).astype(BF16)
        dg_ref[...] = (u * sig * (1.0 + g * (1.0 - sig))).astype(BF16)
        du_ref[...] = silu.astype(BF16)

    out = jax.ShapeDtypeStruct((t, f), BF16)
    blk = pl.BlockSpec((tm, tn), lambda i, j: (i, j))
    return _pallas(
        body, name=name, grid=(t // tm, nf),
        in_specs=[pl.BlockSpec((tm, d), lambda i, j: (i, 0)), pl.BlockSpec((d, tn), lambda i, j: (0, j)),
                  pl.BlockSpec((d, tn), lambda i, j: (0, j + nf))],
        out_specs=[blk, blk, blk], out_shape=[out, out, out],
        compiler_params=_params(("parallel", "parallel"),
                                _nbytes((tm, d), BF16) + 2 * _nbytes((d, tn), BF16) + 5 * _nbytes((tm, tn), F32)),
    )(h, w_gu, w_gu)


def _ffn_dact(name, dff, w_down, dact_dgate, dact_dup, after=None):
    t, d = dff.shape
    f = w_down.shape[0]
    tm, tn = _tile(t, SWIGLU_TILE[0], 128), _tile(f, SWIGLU_TILE[1], 128)

    def body(d_ref, w_ref, dg_ref, du_ref, o_ref):
        da = _dot(d_ref[...], w_ref[...], 1, 1)
        o_ref[0] = (da * dg_ref[...].astype(F32)).astype(BF16)
        o_ref[1] = (da * du_ref[...].astype(F32)).astype(BF16)

    blk = pl.BlockSpec((tm, tn), lambda i, j: (i, j))
    body, ins, in_specs = _ordered(
        body, [dff, w_down, dact_dgate, dact_dup],
        [pl.BlockSpec((tm, d), lambda i, j: (i, 0)), pl.BlockSpec((tn, d), lambda i, j: (j, 0)), blk, blk], after)
    return _pallas(
        body, name=name, grid=(t // tm, f // tn), in_specs=in_specs,
        out_specs=pl.BlockSpec((2, tm, tn), lambda i, j: (0, i, j)),
        out_shape=jax.ShapeDtypeStruct((2, t, f), BF16),
        compiler_params=_params(("parallel", "parallel"),
                                _nbytes((tm, d), BF16) + _nbytes((tn, d), BF16) + 5 * _nbytes((tm, tn), F32)),
    )(*ins)


def _ffn_dh(name, dgu, w_gu, after=None):
    _, t, f = dgu.shape
    d = w_gu.shape[0]
    tm, tn, tk = _mm_tiles(t, d, 2 * f, F32, k_unit=f)
    nkf = f // tk
    return _matmul(name, dgu, w_gu, form="nt", out_dtype=F32, tm=tm, tn=tn, tk=tk, sizes=(t, d, 2 * f),
                   a_map=lambda i, j, kk: (kk // nkf, i, kk % nkf), after=after)


def _lower_bound(lbp):
    m = jnp.max(lbp, axis=0, keepdims=True)
    e = jnp.exp(lbp - m)
    return e[0:1] / jnp.sum(e, axis=0, keepdims=True)


def _chunk_mask(reverse):
    row = lax.broadcasted_iota(jnp.int32, (CHUNK, CHUNK), 0)
    col = lax.broadcasted_iota(jnp.int32, (CHUNK, CHUNK), 1)
    return (col >= row) if reverse else (col <= row)


def _hgrn_gates(z, lb, mask_bf):
    sig = _sigmoid(z)
    f = lb + (1.0 - lb) * sig
    logf = jnp.log(f)
    k = 1.0 - f
    cum = _dot_exact(mask_bf, logf)
    last = jnp.sum(logf, axis=0, keepdims=True)
    return sig, f, k, cum, last


def _hgrn_scan_fwd(name, p, lbp_f, lbp_b):
    t = p.shape[0]
    hw = lbp_f.shape[1]
    nh, nc = hw // HEAD, t // CHUNK

    def body(qf, vf, zf, qb, vb, zb, lbf, lbb, of_ref, ob_ref, stf_ref, stb_ref, state):
        n = pl.program_id(0)

        @pl.when(n == 0)
        def _():
            state[...] = jnp.zeros_like(state)

        directions = [(qf, vf, zf, lbf, of_ref, stf_ref), (qb, vb, zb, lbb, ob_ref, stb_ref)]
        wide = []
        for d, (q_ref, v_ref, z_ref, lb_ref, o_ref, st_ref) in enumerate(directions):
            mask = _chunk_mask(d == 1)
            lb = _lower_bound(lb_ref[...])
            _, _, k, cum, last = _hgrn_gates(z_ref[...], lb, mask.astype(BF16))
            v = v_ref[...].astype(BF16)
            qd = (q_ref[...] * jnp.exp(cum)).astype(BF16)
            kd = (k * jnp.exp(-cum)).astype(BF16)
            kt = (k * jnp.exp(last - cum)).astype(BF16)
            s_all = state[d]
            st_ref[...] = s_all
            wide.append((mask, v, qd, kd, kt, jnp.exp(last), s_all, o_ref))
        pairs = [(d, slice(h * HEAD, (h + 1) * HEAD)) for d in range(2) for h in range(nh)]
        a = [jnp.where(wide[d][0], _dot(wide[d][2][:, sl], wide[d][3][:, sl], 1, 1), 0.0).astype(BF16)
             for d, sl in pairs]
        inter = [_dot(wide[d][2][:, sl], wide[d][6][:, sl].astype(BF16), 1, 1) for d, sl in pairs]
        intra = [_dot(a[i], wide[d][1][:, sl]) for i, (d, sl) in enumerate(pairs)]
        grow = [_dot(wide[d][1][:, sl], wide[d][4][:, sl], 0, 0) for d, sl in pairs]
        for i, (d, sl) in enumerate(pairs):
            wide[d][7][:, sl] = intra[i] + inter[i]
            state[d, :, sl] = wide[d][6][:, sl] * wide[d][5][:, sl] + grow[i]

    def col(group, reverse):
        return pl.BlockSpec((CHUNK, hw), lambda n: ((nc - 1 - n) if reverse else n, group))

    def st(reverse):
        return pl.BlockSpec((None, HEAD, hw), lambda n: ((nc - 1 - n) if reverse else n, 0, 0))

    lb_spec = pl.BlockSpec((2, hw), lambda n: (0, 0))
    out = jax.ShapeDtypeStruct((t, hw), F32)
    states = jax.ShapeDtypeStruct((nc, HEAD, hw), F32)
    return _pallas(
        body, name=name, grid=(nc,),
        in_specs=[col(0, False), col(1, False), col(2, False), col(0, True), col(1, True), col(3, True),
                  lb_spec, lb_spec],
        out_specs=[col(0, False), col(0, True), st(False), st(True)],
        out_shape=[out, out, states, states],
        scratch_shapes=[pltpu.VMEM((2, HEAD, hw), F32)],
        compiler_params=_params(("arbitrary",), 12 * _nbytes((HEAD, hw), F32)),
    )(p, p, p, p, p, p, lbp_f, lbp_b)


def _hgrn_scan_bwd(name, p, lbp_f, lbp_b, do, st_f, st_b):
    t = p.shape[0]
    hw = lbp_f.shape[1]
    nh, nc = hw // HEAD, t // CHUNK

    def body(qf, vf, zf, dof, sf, qb, vb, zb, dob, sb, lbf, lbb, dqf, dvf, dzf, dlbf, dqb, dvb, dzb, dlbb,
             dstate, dlb_acc, dqd_s, dkd_s, dkt_s, ddec_s):
        n = pl.program_id(0)

        @pl.when(n == 0)
        def _():
            dstate[...] = jnp.zeros_like(dstate)
            dlb_acc[...] = jnp.zeros_like(dlb_acc)

        directions = [(qf, vf, zf, dof, sf, lbf, dqf, dvf, dzf, dlbf), (qb, vb, zb, dob, sb, lbb, dqb, dvb, dzb, dlbb)]
        for d, (q_ref, v_ref, z_ref, do_ref, st_ref, lb_ref, dq_ref, dv_ref, dz_ref, dlb_ref) in enumerate(directions):
            mask = _chunk_mask(d == 1)
            mask_bf = mask.astype(BF16)
            lb = _lower_bound(lb_ref[...])
            sig, f, k, cum, last = _hgrn_gates(z_ref[...], lb, mask_bf)
            e_pos, e_neg, e_tail = jnp.exp(cum), jnp.exp(-cum), jnp.exp(last - cum)
            dec = jnp.exp(last)
            v = v_ref[...].astype(BF16)
            qd, kd, kt = q_ref[...] * e_pos, k * e_neg, k * e_tail
            qd_bf, kd_bf, kt_bf = qd.astype(BF16), kd.astype(BF16), kt.astype(BF16)
            s_all = st_ref[...]
            ds_all = dstate[d]
            dov = do_ref[...].astype(BF16)
            cols = [slice(h * HEAD, (h + 1) * HEAD) for h in range(nh)]
            s_bf = [s_all[:, sl].astype(BF16) for sl in cols]
            ds_bf = [ds_all[:, sl].astype(BF16) for sl in cols]
            a = [jnp.where(mask, _dot(qd_bf[:, sl], kd_bf[:, sl], 1, 1), 0.0).astype(BF16) for sl in cols]
            da = [jnp.where(mask, _dot(dov[:, sl], v[:, sl], 1, 1), 0.0).astype(BF16) for sl in cols]
            dv_h = [_dot(a[h], dov[:, sl], 0, 0) + _dot(kt_bf[:, sl], ds_bf[h], 1, 1) for h, sl in enumerate(cols)]
            dqd_h = [_dot(da[h], kd_bf[:, sl]) + _dot(dov[:, sl], s_bf[h]) for h, sl in enumerate(cols)]
            dkd_h = [_dot(da[h], qd_bf[:, sl], 0, 0) for h, sl in enumerate(cols)]
            dkt_h = [_dot(v[:, sl], ds_bf[h]) for h, sl in enumerate(cols)]
            dst_h = [_dot(dov[:, sl], qd_bf[:, sl], 0, 0) + ds_all[:, sl] * dec[:, sl] for sl in cols]
            for h, sl in enumerate(cols):
                dv_ref[:, sl] = dv_h[h]
                dqd_s[:, sl] = dqd_h[h]
                dkd_s[:, sl] = dkd_h[h]
                dkt_s[:, sl] = dkt_h[h]
                dstate[d, :, sl] = dst_h[h]
                ddec_s[:, sl] = jnp.sum(ds_all[:, sl] * s_all[:, sl], axis=0, keepdims=True)
            dqd, dkd, dkt = dqd_s[...], dkd_s[...], dkt_s[...]
            dlast = jnp.sum(dkt * kt, axis=0, keepdims=True) + dec * ddec_s[...]
            dq_ref[...] = dqd * e_pos
            dk = dkd * e_neg + dkt * e_tail
            dcum = dqd * qd - dkd * kd - dkt * kt
            dlogf = _dot_exact(mask_bf, dcum, 0, 0) + dlast
            df = dlogf / f - dk
            dz_ref[...] = df * (1.0 - lb) * sig * (1.0 - sig)
            dlb_acc[d] += jnp.sum(df * (1.0 - sig), axis=0, keepdims=True)

            @pl.when(n == nc - 1)
            def _():
                g = dlb_acc[d] * lb * (1.0 - lb)
                dlb_ref[0:1, :] = g
                dlb_ref[1:2, :] = -g

    def col(group, reverse):
        return pl.BlockSpec((CHUNK, hw), lambda n: (n if reverse else (nc - 1 - n), group))

    def st(reverse):
        return pl.BlockSpec((None, HEAD, hw), lambda n: (n if reverse else (nc - 1 - n), 0, 0))

    lb_spec = pl.BlockSpec((2, hw), lambda n: (0, 0))
    out = jax.ShapeDtypeStruct((t, hw), F32)
    dlb = jax.ShapeDtypeStruct((2, hw), F32)
    wide = pltpu.VMEM((CHUNK, hw), F32)
    return _pallas(
        body, name=name, grid=(nc,),
        in_specs=[col(0, False), col(1, False), col(2, False), col(0, False), st(False),
                  col(0, True), col(1, True), col(3, True), col(0, True), st(True), lb_spec, lb_spec],
        out_specs=[col(0, False), col(0, False), col(0, False), lb_spec,
                   col(0, True), col(0, True), col(0, True), lb_spec],
        out_shape=[out, out, out, dlb, out, out, out, dlb],
        scratch_shapes=[pltpu.VMEM((2, HEAD, hw), F32), pltpu.VMEM((2, 1, hw), F32), wide, wide, wide,
                        pltpu.VMEM((1, hw), F32)],
        compiler_params=_params(("arbitrary",), 16 * _nbytes((HEAD, hw), F32)),
    )(p, p, p, do, st_f, p, p, p, do, st_b, lbp_f, lbp_b)


def _hgrn_out_fwd(name, o_f, o_b, p, gain, g_group):
    t, hw = o_f.shape
    nh = hw // HEAD
    tm = _tile(t, 256, 16)

    def body(of_ref, ob_ref, g_ref, gain_ref, y_ref):
        o_all = of_ref[...] + ob_ref[...]
        g_all = g_ref[...]
        scale_all = gain_ref[...] * (g_all * _sigmoid(g_all))
        for h in range(nh):
            sl = slice(h * HEAD, (h + 1) * HEAD)
            o = o_all[:, sl]
            y_ref[:, sl] = (o * _rms(o) * scale_all[:, sl]).astype(BF16)

    blk = pl.BlockSpec((tm, hw), lambda i: (i, 0))
    return _pallas(
        body, name=name, grid=(t // tm,),
        in_specs=[blk, blk, pl.BlockSpec((tm, hw), lambda i: (i, g_group)), pl.BlockSpec((1, hw), lambda i: (0, 0))],
        out_specs=blk, out_shape=jax.ShapeDtypeStruct((t, hw), BF16),
        compiler_params=_params(("parallel",), 5 * _nbytes((tm, hw), F32)),
    )(o_f, o_b, p, gain)


def _hgrn_out_bwd(name, dy, o_f, o_b, p, gain, g_group, after=None):
    t, hw = o_f.shape
    nh = hw // HEAD
    tm = _tile(t, 256, 8)

    def body(dy_ref, of_ref, ob_ref, g_ref, gain_ref, do_ref, dg_ref, dgain_ref):
        i = pl.program_id(0)
        o_all = of_ref[...] + ob_ref[...]
        g_all = g_ref[...]
        sig_all = _sigmoid(g_all)
        dy_all = dy_ref[...]
        up_all = dy_all * (g_all * sig_all)
        dsilu_all = dy_all * sig_all * (1.0 + g_all * (1.0 - sig_all))
        gain_all = gain_ref[...]
        for h in range(nh):
            sl = slice(h * HEAD, (h + 1) * HEAD)
            o, gain_v = o_all[:, sl], gain_all[:, sl]
            do, dgain = _norm_bwd(up_all[:, sl], o, gain_v)
            do_ref[:, sl] = do
            dg_ref[:, sl] = dsilu_all[:, sl] * (o * _rms(o) * gain_v)
            _accumulate(dgain_ref.at[:, sl], dgain, i == 0)

    blk = pl.BlockSpec((tm, hw), lambda i: (i, 0))
    vec = pl.BlockSpec((1, hw), lambda i: (0, 0))
    out = jax.ShapeDtypeStruct((t, hw), F32)
    body, ins, in_specs = _ordered(
        body, [dy, o_f, o_b, p, gain], [blk, blk, blk, pl.BlockSpec((tm, hw), lambda i: (i, g_group)), vec], after)
    return _pallas(
        body, name=name, grid=(t // tm,), in_specs=in_specs,
        out_specs=[blk, blk, vec], out_shape=[out, out, jax.ShapeDtypeStruct((1, hw), F32)],
        compiler_params=_params(("arbitrary",), 7 * _nbytes((tm, hw), F32)),
    )(*ins)


def _t5_bucket_ids():
    c = np.arange(WINDOW)[:, None]
    s = np.arange(SPAN)[None, :]
    rel = s - WINDOW - c
    nb = REL_BUCKETS // 2
    max_exact = nb // 2
    bucket = (rel > 0).astype(np.int32) * nb
    n = np.abs(rel)
    large = max_exact + (np.log(np.maximum(n, 1) / max_exact) / np.log(REL_MAX_DIST / max_exact)
                         * (nb - max_exact)).astype(np.int32)
    large = np.minimum(large, nb - 1)
    ids = bucket + np.where(n < max_exact, n, large).astype(np.int32)
    return jnp.asarray(ids.reshape(1, WINDOW * SPAN), jnp.int32)


def _bias_onehot(ids_ref):
    n = ids_ref.shape[1]
    return (lax.broadcasted_iota(jnp.int32, (REL_BUCKETS, n), 0) == ids_ref[...]).astype(BF16)


def _bias_gather(name, table_t, ids):
    nh = table_t.shape[0]

    def body(t_ref, ids_ref, o_ref):
        o_ref[...] = _dot_exact(t_ref[...], _bias_onehot(ids_ref), split="a")

    return _pallas(
        body, name=name, out_shape=jax.ShapeDtypeStruct((nh, ids.shape[1]), F32),
        compiler_params=pltpu.CompilerParams(vmem_limit_bytes=32 << 20),
    )(table_t, ids)


def _bias_scatter(name, dbias, ids):
    nh = dbias.shape[0]

    def body(d_ref, ids_ref, o_ref):
        o_ref[...] = _dot_exact(d_ref[...], _bias_onehot(ids_ref), 1, 1, split="a")

    return _pallas(
        body, name=name, out_shape=jax.ShapeDtypeStruct((nh, REL_BUCKETS), F32),
        compiler_params=pltpu.CompilerParams(vmem_limit_bytes=32 << 20),
    )(dbias, ids)


def _attn_valid(i, t):
    c = lax.broadcasted_iota(jnp.int32, (WINDOW, SPAN), 0)
    s = lax.broadcasted_iota(jnp.int32, (WINDOW, SPAN), 1)
    rel = s - WINDOW - c
    pos = i * WINDOW - WINDOW + s
    return (jnp.abs(rel) <= WINDOW) & (pos >= 0) & (pos < t)


def _attn_probs(qs, khs, b_ref, s_ref, valid):
    heads = range(len(qs))
    sinks = [s_ref[0:1, h:h + 1] for h in heads]
    s = [_dot(qs[h], khs[h], 1, 1) / math.sqrt(HEAD) for h in heads]
    s = [jnp.where(valid, s[h] + b_ref[h], NEG_INF) for h in heads]
    m = [jnp.maximum(jnp.max(s[h], axis=-1, keepdims=True), sinks[h]) for h in heads]
    e = [jnp.exp(s[h] - m[h]) for h in heads]
    es = [jnp.exp(sinks[h] - m[h]) for h in heads]
    inv = [1.0 / (jnp.sum(e[h], axis=-1, keepdims=True) + es[h]) for h in heads]
    return [e[h] * inv[h] for h in heads], [es[h] * inv[h] for h in heads]


def _attn_fwd(name, p, k_pad, v_pad, bias, sink, q_group_blk):
    t = p.shape[0]
    nh = bias.shape[0]
    aw = nh * HEAD
    grp = nh // KV_HEADS
    nb = t // WINDOW

    def body(q_ref, k_ref, v_ref, b_ref, s_ref, y_ref, pr_ref, ps_ref):
        i = pl.program_id(0)
        valid = _attn_valid(i, t)
        start = pl.multiple_of(i * WINDOW, WINDOW)
        ks = k_ref[pl.ds(start, SPAN), :]
        vs = v_ref[pl.ds(start, SPAN), :]
        heads = range(nh)
        col = lambda h: slice(h * HEAD, (h + 1) * HEAD)
        qs = [q_ref[:, col(h)].astype(BF16) for h in heads]
        pr, ps = _attn_probs(qs, [ks[:, col(h // grp)] for h in heads], b_ref, s_ref, valid)
        pr = [pr[h].astype(BF16) for h in heads]
        out = [_dot(pr[h], vs[:, col(h // grp)]) for h in heads]
        lane = lax.broadcasted_iota(jnp.int32, (WINDOW, 128), 1)
        sinks = jnp.zeros((WINDOW, 128), F32)
        for h in heads:
            y_ref[:, col(h)] = out[h].astype(BF16)
            pr_ref[h] = pr[h]
            sinks = jnp.where(lane == h, ps[h], sinks)
        ps_ref[...] = sinks

    full = lambda a: pl.BlockSpec(a.shape, lambda i: (0,) * a.ndim)
    return _pallas(
        body, name=name, grid=(nb,),
        in_specs=[pl.BlockSpec((WINDOW, aw), lambda i: (i, q_group_blk)), full(k_pad), full(v_pad), full(bias),
                  full(sink)],
        out_specs=[pl.BlockSpec((WINDOW, aw), lambda i: (i, 0)), pl.BlockSpec((nh, WINDOW, SPAN), lambda i: (0, i, 0)),
                   pl.BlockSpec((WINDOW, 128), lambda i: (i, 0))],
        out_shape=[jax.ShapeDtypeStruct((t, aw), BF16), jax.ShapeDtypeStruct((nh, t, SPAN), BF16),
                   jax.ShapeDtypeStruct((t, 128), F32)],
        compiler_params=_params(("parallel",), _nbytes(k_pad.shape, BF16) * 2 + 2 * _nbytes(bias.shape, F32)),
    )(p, k_pad, v_pad, bias, sink)


def _attn_bwd(name, p, k_pad, v_pad, probs, sink_probs, dy, q_group_blk, dy_blk, after=None):
    t = p.shape[0]
    nh = probs.shape[0]
    aw = nh * HEAD
    grp = nh // KV_HEADS
    nb = t // WINDOW
    kvw = k_pad.shape[1]

    def body(q_ref, k_ref, v_ref, pr_ref, ps_ref, dy_ref, dq_ref, dk_ref, dv_ref, db_ref, ds_ref):
        i = pl.program_id(0)

        @pl.when(i == 0)
        def _():
            dk_ref[...] = jnp.zeros_like(dk_ref)
            dv_ref[...] = jnp.zeros_like(dv_ref)
            db_ref[...] = jnp.zeros_like(db_ref)
            ds_ref[...] = jnp.zeros_like(ds_ref)

        start = pl.multiple_of(i * WINDOW, WINDOW)
        ks = k_ref[pl.ds(start, SPAN), :]
        vs = v_ref[pl.ds(start, SPAN), :]
        inv_sqrt = 1.0 / math.sqrt(HEAD)
        heads = range(nh)
        col = lambda h: slice(h * HEAD, (h + 1) * HEAD)
        qs = [q_ref[:, col(h)].astype(BF16) for h in heads]
        khs = [ks[:, col(h // grp)] for h in heads]
        pr_bf = [pr_ref[h] for h in heads]
        pr = [pr_bf[h].astype(F32) for h in heads]
        dos = [dy_ref[:, col(h)].astype(BF16) for h in heads]
        dp = [_dot(dos[h], vs[:, col(h // grp)], 1, 1) for h in heads]
        delta = [jnp.sum(pr[h] * dp[h], axis=-1, keepdims=True) for h in heads]
        dsc = [pr[h] * (dp[h] - delta[h]) for h in heads]
        dsr = [(dsc[h] * inv_sqrt).astype(BF16) for h in heads]
        dq = [_dot(dsr[h], khs[h]) for h in heads]
        dk = [_dot(dsr[h], qs[h], 0, 0) for h in heads]
        dv = [_dot(pr_bf[h], dos[h], 0, 0) for h in heads]
        for h in heads:
            db_ref[h] += dsc[h]
            dsink = jnp.sum(-ps_ref[:, h:h + 1] * delta[h], axis=0, keepdims=True)
            ds_ref[h:h + 1, :] += jnp.broadcast_to(dsink, (1, 128))
            dq_ref[:, col(h)] = dq[h]
        for kv in range(KV_HEADS):
            group = range(kv * grp, (kv + 1) * grp)
            dk_ref[pl.ds(start, SPAN), col(kv)] += sum(dk[h] for h in group)
            dv_ref[pl.ds(start, SPAN), col(kv)] += sum(dv[h] for h in group)

    full = lambda a: pl.BlockSpec(a.shape, lambda i: (0,) * a.ndim)
    whole = lambda shape: pl.BlockSpec(shape, lambda i: (0,) * len(shape))
    pad_shape = (t + 2 * WINDOW, kvw)
    bias_shape = (nh, WINDOW, SPAN)
    body, ins, in_specs = _ordered(
        body, [p, k_pad, v_pad, probs, sink_probs, dy],
        [pl.BlockSpec((WINDOW, aw), lambda i: (i, q_group_blk)), full(k_pad), full(v_pad),
         pl.BlockSpec((nh, WINDOW, SPAN), lambda i: (0, i, 0)), pl.BlockSpec((WINDOW, 128), lambda i: (i, 0)),
         pl.BlockSpec((WINDOW, aw), lambda i: (i, dy_blk))], after)
    return _pallas(
        body, name=name, grid=(nb,), in_specs=in_specs,
        out_specs=[pl.BlockSpec((WINDOW, aw), lambda i: (i, 0)), whole(pad_shape), whole(pad_shape),
                   whole(bias_shape), whole((nh, 128))],
        out_shape=[jax.ShapeDtypeStruct((t, aw), F32), jax.ShapeDtypeStruct(pad_shape, F32),
                   jax.ShapeDtypeStruct(pad_shape, F32), jax.ShapeDtypeStruct(bias_shape, F32),
                   jax.ShapeDtypeStruct((nh, 128), F32)],
        compiler_params=_params(("arbitrary",), 3 * _nbytes(pad_shape, F32) + 3 * _nbytes(bias_shape, F32)),
    )(*ins)


def _pad_kv(name, p, kv_blk, kvw):
    t = p.shape[0]
    nb = t // WINDOW

    def body(x_ref, o_ref):
        i = pl.program_id(0)
        inside = jnp.logical_and(i >= 1, i <= nb)
        o_ref[...] = jnp.where(inside, x_ref[...], 0.0).astype(BF16)

    return _pallas(
        body, name=name, grid=(nb + 2,),
        in_specs=[pl.BlockSpec((WINDOW, kvw), lambda i: (jnp.clip(i - 1, 0, nb - 1), kv_blk))],
        out_specs=pl.BlockSpec((WINDOW, kvw), lambda i: (i, 0)),
        out_shape=jax.ShapeDtypeStruct((t + 2 * WINDOW, kvw), BF16),
        compiler_params=_params(("parallel",), 1 << 20),
    )(p)


def _mix_dproj(name, pieces, kv_pads, t, after=None):
    hw = pieces[0][0].shape[1]
    kvw = kv_pads[0].shape[1]
    widths = [hw] * len(pieces) + [kvw] * len(kv_pads)
    total = sum(widths)
    tm = WINDOW
    flat = [a for pc in pieces for a in pc]

    def body(*refs):
        o_ref = refs[-1]
        pos, off = 0, 0
        for pc in pieces:
            val = refs[pos][...]
            for extra in range(1, len(pc)):
                val = val + refs[pos + extra][...]
            o_ref[:, off:off + hw] = val.astype(BF16)
            pos += len(pc)
            off += hw
        for _ in kv_pads:
            o_ref[:, off:off + kvw] = refs[pos][...].astype(BF16)
            pos += 1
            off += kvw

    in_specs = [pl.BlockSpec((tm, hw), lambda i: (i, 0)) for _ in flat]
    in_specs += [pl.BlockSpec((tm, kvw), lambda i: (i + 1, 0)) for _ in kv_pads]
    body, ins, in_specs = _ordered(body, [*flat, *kv_pads], in_specs, after)
    return _pallas(
        body, name=name, grid=(t // tm,), in_specs=in_specs,
        out_specs=pl.BlockSpec((tm, total), lambda i: (i, 0)),
        out_shape=jax.ShapeDtypeStruct((t, total), BF16),
        compiler_params=_params(("parallel",), 3 * _nbytes((tm, total), F32)),
    )(*ins)


def _concat_cols(name, a, b):
    t, wa = a.shape
    wb = b.shape[1]
    tm = _tile(t, 512, 16)

    def body(a_ref, b_ref, o_ref):
        o_ref[:, :wa] = a_ref[...]
        o_ref[:, wa:] = b_ref[...]

    return _pallas(
        body, name=name, grid=(t // tm,),
        in_specs=[pl.BlockSpec((tm, wa), lambda i: (i, 0)), pl.BlockSpec((tm, wb), lambda i: (i, 0))],
        out_specs=pl.BlockSpec((tm, wa + wb), lambda i: (i, 0)),
        out_shape=jax.ShapeDtypeStruct((t, wa + wb), a.dtype),
        compiler_params=_params(("parallel",), 2 * _nbytes((tm, wa + wb), a.dtype)),
    )(a, b)


def _cast_into_full(name, w, geom, idx, after=None):
    r, c = w.shape
    tr = _tile(r, 256, 16)
    nr = r // tr
    if geom.col:
        place = lambda i, iref: (i, iref[0])
    else:
        place = lambda i, iref: (iref[0] * nr + i, 0)

    def body(i_ref, w_ref, *rest):
        rest[-1][...] = w_ref[...].astype(BF16)

    in_specs = [pl.BlockSpec((tr, c), lambda i, iref: (i, 0))]
    ins = [w]
    if after is not None:
        in_specs.append(pl.BlockSpec(memory_space=pl.ANY))
        ins.append(after)
    return _pallas(
        body, name=name,
        grid_spec=pltpu.PrefetchScalarGridSpec(
            num_scalar_prefetch=1, grid=(nr,), in_specs=in_specs, out_specs=pl.BlockSpec((tr, c), place)),
        out_shape=pltpu.HBM(geom.full_shape, BF16),
        compiler_params=_params(("parallel",), 2 * _nbytes((tr, c), F32)),
    )(idx, *ins)


def _adamw(name, w, g, m, v):
    r, c = w.shape
    tr = _tile(r, 128, 8)
    bc1 = 1.0 - ADAM_B1 ** ADAM_STEP
    bc2 = 1.0 - ADAM_B2 ** ADAM_STEP

    def body(w_ref, g_ref, m_ref, v_ref, go_ref, d_ref, nm_ref, nv_ref):
        gv = g_ref[...]
        go_ref[...] = gv
        nm = ADAM_B1 * m_ref[...] + (1.0 - ADAM_B1) * gv
        nv = ADAM_B2 * v_ref[...] + (1.0 - ADAM_B2) * (gv * gv)
        nm_ref[...] = nm
        nv_ref[...] = nv
        d_ref[...] = -ADAM_LR * ((nm / bc1) / (jnp.sqrt(nv / bc2) + ADAM_EPS) + ADAM_WD * w_ref[...])

    blk = pl.BlockSpec((tr, c), lambda i: (i, 0))
    out = jax.ShapeDtypeStruct((r, c), F32)
    return _pallas(
        body, name=name, grid=(r // tr,), in_specs=[blk] * 4, out_specs=[blk] * 4, out_shape=[out] * 4,
        compiler_params=_params(("parallel",), 8 * _nbytes((tr, c), F32)),
    )(w, g, m, v)


def _mesh_pos():
    return lax.axis_index("x"), lax.axis_index("y"), lax.axis_index("c")


def _other_chips(x, y):
    return [(1 - x, y), (x, 1 - y), (1 - x, 1 - y)]


class _Big:
    def __init__(self, shard_shape, col_sharded):
        self.col = col_sharded
        r, c = shard_shape
        self.shard_shape = (r, c)
        self.full_shape = (r, N_CHIPS * c) if col_sharded else (N_CHIPS * r, c)
        self.half_shape = (r // 2, N_CHIPS * c) if col_sharded else (N_CHIPS * r, c // 2)
        self.shard_half_shape = (r // 2, c) if col_sharded else (r, c // 2)

    def region(self, ref, s, half=None):
        r, c = self.shard_shape
        if self.col:
            rows = slice(None) if half is None else pl.ds(half * (r // 2), r // 2)
            return ref.at[rows, pl.ds(s * c, c)]
        cols = slice(None) if half is None else pl.ds(half * (c // 2), c // 2)
        return ref.at[pl.ds(s * r, r), cols]

    def n_halves(self, ref, half, n):
        r, c = self.shard_shape
        if self.col:
            return ref.at[pl.ds(half * (r // 2), r // 2), pl.ds(0, n * c)]
        return ref.at[pl.ds(0, n * r), pl.ds(half * (c // 2), c // 2)]

    def three_halves(self, ref, half):
        return self.n_halves(ref, half, 3)

    def sub_half(self, ref, s, half, j):
        r, c = self.shard_shape
        if self.col:
            return ref.at[pl.ds(half * (r // 2) + j * (r // 4), r // 4), pl.ds(s * c, c)]
        return ref.at[pl.ds(s * r + j * (r // 2), r // 2), pl.ds(half * (c // 2), c // 2)]

    def half_of_full(self, ref, half):
        r, c = self.full_shape
        if self.col:
            return ref.at[pl.ds(half * (r // 2), r // 2), :]
        return ref.at[:, pl.ds(half * (c // 2), c // 2)]

    def half_of_shard(self, ref, half):
        r, c = self.shard_shape
        if self.col:
            return ref.at[pl.ds(half * (r // 2), r // 2), :]
        return ref.at[:, pl.ds(half * (c // 2), c // 2)]

    def shard_of_half(self, ref, s):
        r, c = self.shard_shape
        if self.col:
            return ref.at[:, pl.ds(s * c, c)]
        return ref.at[pl.ds(s * r, r), :]


HBM =pl.BlockSpec(memory_space=pltpu.HBM)
SEM = pl.BlockSpec(memory_space=pltpu.SEMAPHORE)
SPLIT_COPY = pltpu.CompilerParams(has_side_effects=pltpu.SideEffectType.DATAFLOW_SIDE_EFFECTING)


def _in_hbm(a):
    return pltpu.with_memory_space_constraint(a, pltpu.HBM)


def _gather_start(name, fulls, geoms, after):
    nw = len(fulls)

    def body(*refs):
        dst = refs[nw + 1:2 * nw + 1]
        sems = refs[2 * nw + 1:-1]
        x, y, c = _mesh_pos()
        mine = 2 * x + y
        for w in range(nw):
            own_half = geoms[w].region(dst[w], mine, c)
            for chip in _other_chips(x, y):
                pltpu.make_async_remote_copy(src_ref=own_half, dst_ref=own_half, send_sem=sems[2 * w],
                                             recv_sem=sems[2 * w + 1], device_id=(*chip, c),
                                             device_id_type=MESH).start()
        refs[-1][...] = jnp.zeros_like(refs[-1])

    out = _pallas(
        body, name=name, in_specs=[HBM] * nw + [pl.BlockSpec(memory_space=pl.ANY)],
        out_specs=[HBM] * nw + [SEM] * (2 * nw) + [pl.BlockSpec(memory_space=pltpu.VMEM)],
        out_shape=[pltpu.HBM(g.full_shape, BF16) for g in geoms] + [pltpu.SemaphoreType.DMA(())] * (2 * nw)
        + [jax.ShapeDtypeStruct((8, 128), F32)],
        input_output_aliases={w: w for w in range(nw)}, compiler_params=SPLIT_COPY,
    )(*[_in_hbm(a) for a in fulls], after)
    return list(out[:nw]), [(out[nw + 2 * w], out[nw + 2 * w + 1]) for w in range(nw)], out[-1]


def _gather_first_direct(full, geom):
    def start(refs, _, new):
        x, y, c = _mesh_pos()
        own = geom.region(refs[0], 2 * x + y, c)
        for chip in ((1 - x, y), (x, 1 - y)):
            _remote(own, own, new, (*chip, c)).start()

    return _split_copy_call("gather_first_direct", [full], start, new_sems=2)


def _gather_first_relay(full, geom, sems, after):
    def relay(refs, got, new):
        x, y, c = _mesh_pos()
        w = refs[0]
        two = geom.n_halves(w, c, 2)
        _remote(two, two, got, (x, y, 1 - c)).wait_recv()
        from_x = geom.sub_half(w, 2 * (1 - x) + y, c, 0)
        from_y = geom.sub_half(w, 2 * x + (1 - y), c, 1)
        _remote(from_x, from_x, new, (x, 1 - y, c)).start()
        _remote(from_y, from_y, new, (1 - x, y, c)).start()
        _remote(two, two, got, (x, y, 1 - c)).wait_send()

    return _split_copy_call("gather_first_relay", [full], relay, sems=sems, after=after, new_sems=2)


def _gather_forward(name, full, geom, sems, after, arrivals=3, only_diagonal=False):
    def body(w_in, send_sem, recv_sem, after_ref, w_ref, fwd_send, fwd_recv):
        x, y, c = _mesh_pos()
        sibling = (x, y, 1 - c)
        landed_all = geom.n_halves(w_ref, c, arrivals)
        _remote(landed_all, landed_all, (send_sem, recv_sem), sibling).wait_recv()
        for chip in _other_chips(x, y)[2 if only_diagonal else 0:]:
            landed = geom.region(w_ref, 2 * chip[0] + chip[1], c)
            pltpu.make_async_remote_copy(src_ref=landed, dst_ref=landed, send_sem=fwd_send, recv_sem=fwd_recv,
                                         device_id=sibling, device_id_type=MESH).start()
        _remote(landed_all, landed_all, (send_sem, recv_sem), sibling).wait_send()

    sem = pltpu.SemaphoreType.DMA(())
    out = _pallas(
        body, name=name, in_specs=[HBM, SEM, SEM, pl.BlockSpec(memory_space=pl.ANY)], out_specs=[HBM, SEM, SEM],
        out_shape=[pltpu.HBM(geom.full_shape, BF16), sem, sem],
        input_output_aliases={0: 0}, compiler_params=SPLIT_COPY,
    )(full, sems[0], sems[1], after)
    return out[0], (out[1], out[2])


def _gather_end(name, full, geom, sems, after, halves=3):
    def body(w_in, fwd_send, fwd_recv, after_ref, w_ref):
        x, y, c = _mesh_pos()
        sibling = (x, y, 1 - c)
        theirs, ours = geom.n_halves(w_ref, 1 - c, halves), geom.n_halves(w_ref, c, halves)
        _remote(theirs, theirs, (fwd_send, fwd_recv), sibling).wait_recv()
        _remote(ours, ours, (fwd_send, fwd_recv), sibling).wait_send()

    return _pallas(
        body, name=name, in_specs=[HBM, SEM, SEM, pl.BlockSpec(memory_space=pl.ANY)], out_specs=HBM,
        out_shape=pltpu.HBM(geom.full_shape, BF16),
        input_output_aliases={0: 0}, compiler_params=SPLIT_COPY,
    )(full, sems[0], sems[1], after)


def _split_copy_call(name, arrays, fn, sems=(), after=None, new_sems=0):
    n, ns = len(arrays), len(sems)
    n_in = n + ns + (after is not None)

    def body(*refs):
        fn(refs[n_in:n_in + n], refs[n:n + ns], refs[n_in + n:-1])
        refs[-1][...] = jnp.zeros_like(refs[-1])

    ins = list(arrays) if ns else [_in_hbm(a) for a in arrays]
    ins += list(sems) + ([after] if after is not None else [])
    in_specs = [HBM] * n + [SEM] * ns + ([pl.BlockSpec(memory_space=pl.ANY)] if after is not None else [])
    out = _pallas(
        body, name=name, in_specs=in_specs,
        out_specs=[HBM] * n + [SEM] * new_sems + [pl.BlockSpec(memory_space=pltpu.VMEM)],
        out_shape=[pltpu.HBM(a.shape, a.dtype) for a in arrays] + [pltpu.SemaphoreType.DMA(())] * new_sems
        + [jax.ShapeDtypeStruct((8, 128), F32)],
        input_output_aliases={i: i for i in range(n)}, compiler_params=SPLIT_COPY,
    )(*ins)
    return list(out[:n]), tuple(out[n:-1]), out[-1]


def _remote(src, dst, sems, to):
    return pltpu.make_async_remote_copy(src_ref=src, dst_ref=dst, send_sem=sems[0], recv_sem=sems[1],
                                        device_id=to, device_id_type=MESH)


class _GradReduce:
    def __init__(self, name, geom, idx, c_idx):
        self.name, self.geom, self.idx, self.c_idx = name, geom, idx, c_idx

    def pair_start(self, theirs):
        g = self.geom

        def start(refs, _, new):
            x, y, c = _mesh_pos()
            _remote(refs[0], refs[1], new, (x, y, 1 - c)).start()

        self.arrays, self.sems, token = _split_copy_call(
            f"pair_start_{self.name}", [theirs, lax.empty(g.half_shape, BF16)], start, new_sems=2)
        return token

    def pair_wait(self, after):
        def wait(refs, sems, _):
            x, y, c = _mesh_pos()
            copy = _remote(refs[0], refs[1], sems, (x, y, 1 - c))
            copy.wait_send()
            copy.wait_recv()

        (_, landed), _, _ = _split_copy_call(f"pair_wait_{self.name}", self.arrays, wait, self.sems, after)
        return landed

    def chip_start(self, half):
        g = self.geom

        def start(refs, _, new):
            x, y, c = _mesh_pos()
            for k, chip in enumerate(_other_chips(x, y)):
                _remote(g.shard_of_half(refs[0], 2 * chip[0] + chip[1]), refs[1].at[k], new, (*chip, c)).start()

        self.arrays, self.sems, token = _split_copy_call(
            f"chip_start_{self.name}", [half, lax.empty((3,) + g.shard_half_shape, BF16)], start, new_sems=2)
        return token

    def chip_finish(self, after):
        g = self.geom

        def wait(refs, sems, _):
            x, y, c = _mesh_pos()
            three = _remote(refs[1], refs[1], sems, (x, y, 1 - c))
            three.wait_send()
            three.wait_recv()

        (half, landed), _, _ = _split_copy_call(f"chip_wait_{self.name}", self.arrays, wait, self.sems, after)
        quarter = _chip_add(f"chip_add_{self.name}", half, landed, g, self.idx)

        def start(refs, _, new):
            x, y, c = _mesh_pos()
            own = g.half_of_shard(refs[0], c)
            _remote(own, own, new, (x, y, 1 - c)).start()

        self.arrays, self.sems, token = _split_copy_call(f"share_start_{self.name}", [quarter], start, new_sems=2)
        return token

    def finish(self, after):
        g = self.geom

        def wait(refs, sems, _):
            x, y, c = _mesh_pos()
            own, theirs = g.half_of_shard(refs[0], c), g.half_of_shard(refs[0], 1 - c)
            _remote(own, own, sems, (x, y, 1 - c)).wait_send()
            _remote(theirs, theirs, sems, (x, y, 1 - c)).wait_recv()

        (quarter,), _, _ = _split_copy_call(f"share_wait_{self.name}", self.arrays, wait, self.sems, after)
        return quarter


def _dw_half(name, x, dy, geom, c_idx, own, addend=None, after=None):
    stacked = dy.ndim == 3
    t, m = x.shape
    n = 2 * dy.shape[2] if stacked else dy.shape[1]
    hm, hn = (m // 2, n) if geom.col else (m, n // 2)
    tm, tn, tk = _mm_tiles(hm, hn, t, BF16, n_unit=(n // 2 if stacked else None))
    if tk != t:
        tm, tn = _tile(hm, 512, 128), _tile(hn // (2 if stacked else 1), 512, 128)
    gi, gj = hm // tm, hn // tn
    nf = (n // 2) // tn

    def sel(cref):
        return cref[0] if own else 1 - cref[0]

    a_map = (lambda i, j, cref: (0, sel(cref) * gi + i)) if geom.col else (lambda i, j, cref: (0, i))
    if stacked:
        b_blk, b_map = (None, t, tn), (lambda i, j, cref: (j // nf, 0, j % nf))
    elif geom.col:
        b_blk, b_map = (t, tn), (lambda i, j, cref: (0, j))
    else:
        b_blk, b_map = (t, tn), (lambda i, j, cref: (0, sel(cref) * gj + j))
    out_blk = pl.BlockSpec((tm, tn), lambda i, j, cref: (i, j))
    ins, in_specs = [x, dy], [pl.BlockSpec((t, tm), a_map), pl.BlockSpec(b_blk, b_map)]
    if addend is not None:
        ins.append(addend)
        in_specs.append(out_blk)
    if after is not None:
        ins.append(after)
        in_specs.append(pl.BlockSpec(memory_space=pl.ANY))

    def body(c_ref, *refs):
        acc = _dot(refs[0][...], refs[1][...], 0, 0)
        if addend is not None:
            acc = acc + refs[2][...].astype(F32)
        refs[len(ins)][...] = acc.astype(BF16)

    return _pallas(
        body, name=name,
        grid_spec=pltpu.PrefetchScalarGridSpec(num_scalar_prefetch=1, grid=(gi, gj), in_specs=in_specs,
                                               out_specs=out_blk),
        out_shape=jax.ShapeDtypeStruct((hm, hn), BF16),
        compiler_params=_params(("parallel", "parallel"),
                                _nbytes((t, tm), BF16) + _nbytes((t, tn), BF16) + 3 * _nbytes((tm, tn), F32)),
    )(c_idx, *ins)


def _chip_add(name, half, recv, geom, idx):
    r, c = geom.shard_half_shape
    tr, tc = _tile(r, 512, 16), _tile(c, 2048, 128)
    nr, ncol = r // tr, c // tc
    if geom.col:
        mine = lambda i, j, iref: (i, iref[0] * ncol + j)
        place = lambda i, j, iref: (iref[1] * nr + i, j)
    else:
        mine = lambda i, j, iref: (iref[0] * nr + i, j)
        place = lambda i, j, iref: (i, iref[1] * ncol + j)

    def body(i_ref, h_ref, r_ref, o_ref):
        acc = h_ref[...].astype(F32)
        for k in range(3):
            acc = acc + r_ref[k].astype(F32)
        o_ref[...] = acc

    return _pallas(
        body, name=name,
        grid_spec=pltpu.PrefetchScalarGridSpec(
            num_scalar_prefetch=1, grid=(nr, ncol),
            in_specs=[pl.BlockSpec((tr, tc), mine), pl.BlockSpec((3, tr, tc), lambda i, j, iref: (0, i, j))],
            out_specs=pl.BlockSpec((tr, tc), place)),
        out_shape=jax.ShapeDtypeStruct(geom.shard_shape, F32),
        compiler_params=_params(("parallel", "parallel"), 4 * _nbytes((tr, tc), F32)),
    )(idx, half, recv)


def _all_reduce_small(pack, after=None):
    r, d = pack.shape

    def body(p_ref, o_ref, slots, send_sems, recv_sems):
        x, y, c = _mesh_pos()
        me = 4 * x + 2 * y + c
        slots[me] = p_ref[...]
        copies = []
        for k in range(1, N_DEV):
            px, py, pc = x ^ ((k >> 2) & 1), y ^ ((k >> 1) & 1), c ^ (k & 1)
            copies.append(pltpu.make_async_remote_copy(
                src_ref=p_ref, dst_ref=slots.at[me], send_sem=send_sems.at[k - 1], recv_sem=recv_sems.at[k - 1],
                device_id=(px, py, pc), device_id_type=MESH))
        for cp in copies:
            cp.start()
        for k in range(1, N_DEV):
            peer = 4 * (x ^ ((k >> 2) & 1)) + 2 * (y ^ ((k >> 1) & 1)) + (c ^ (k & 1))
            pltpu.make_async_remote_copy(
                src_ref=p_ref, dst_ref=slots.at[peer], send_sem=send_sems.at[k - 1], recv_sem=recv_sems.at[k - 1],
                device_id=(x, y, c), device_id_type=MESH).wait_recv()
        for cp in copies:
            cp.wait_send()
        acc = slots[0]
        for k in range(1, N_DEV):
            acc = acc + slots[k]
        o_ref[...] = acc

    vm = pl.BlockSpec(memory_space=pltpu.VMEM)
    body, ins, in_specs = _ordered(body, [pack], [vm], after)
    return _pallas(
        body, name="all_reduce_small", in_specs=in_specs, out_specs=vm,
        out_shape=jax.ShapeDtypeStruct((r, d), F32),
        scratch_shapes=[pltpu.VMEM((N_DEV, r, d), F32), pltpu.SemaphoreType.DMA((N_DEV - 1,)),
                        pltpu.SemaphoreType.DMA((N_DEV - 1,))],
    )(*ins)


def _pack_rows(rows, d):
    out = []
    for a in rows:
        flat = a.reshape(-1)
        n = -(-flat.shape[0] // d) * d
        out.append(jnp.pad(flat, (0, n - flat.shape[0])).reshape(-1, d))
    packed = jnp.concatenate(out, axis=0)
    return jnp.pad(packed, ((0, 16 - packed.shape[0]), (0, 0)))


def _unpack_rows(packed, shapes, d):
    out, row = [], 0
    for shp in shapes:
        n = int(np.prod(shp))
        nrows = -(-n // d)
        out.append(packed[row:row + nrows].reshape(-1)[:n].reshape(shp))
        row += nrows
    return out


def kernel(x, pre_norm_ffn1, post_norm_ffn1, w_ffn1_gate_up, w_ffn1_down, pre_norm_mix, post_norm_mix, w_mix_in, hgrn_lower_bounds_fwd, hgrn_lower_bounds_bwd, hgrn_out_norm, attn_sink, w_mix_out, pre_norm_ffn2, post_norm_ffn2, w_ffn2_gate_up, w_ffn2_down, rel_bias_table, loss_target, m_pre_norm_ffn1, m_post_norm_ffn1, m_w_ffn1_gate_up, m_w_ffn1_down, m_pre_norm_mix, m_post_norm_mix, m_w_mix_in, m_hgrn_lower_bounds_fwd, m_hgrn_lower_bounds_bwd, m_hgrn_out_norm, m_attn_sink, m_w_mix_out, m_pre_norm_ffn2, m_post_norm_ffn2, m_w_ffn2_gate_up, m_w_ffn2_down, m_rel_bias_table, v_pre_norm_ffn1, v_post_norm_ffn1, v_w_ffn1_gate_up, v_w_ffn1_down, v_pre_norm_mix, v_post_norm_mix, v_w_mix_in, v_hgrn_lower_bounds_fwd, v_hgrn_lower_bounds_bwd, v_hgrn_out_norm, v_attn_sink, v_w_mix_out, v_pre_norm_ffn2, v_post_norm_ffn2, v_w_ffn2_gate_up, v_w_ffn2_down, v_rel_bias_table):
    t, d = x.shape[1], x.shape[2]
    hw = hgrn_out_norm.shape[1]
    aw = d - hw
    nah = aw // HEAD
    kvw = KV_HEADS * HEAD
    x0 = x[0]
    target = loss_target[0]

    big_names = ["w_ffn1_gate_up", "w_ffn1_down", "w_mix_in", "w_mix_out", "w_ffn2_gate_up", "w_ffn2_down"]
    big_w = [w_ffn1_gate_up[0], w_ffn1_down[0], w_mix_in[0], w_mix_out[0], w_ffn2_gate_up[0], w_ffn2_down[0]]
    big_m = [m_w_ffn1_gate_up[0], m_w_ffn1_down[0], m_w_mix_in[0], m_w_mix_out[0], m_w_ffn2_gate_up[0],
             m_w_ffn2_down[0]]
    big_v = [v_w_ffn1_gate_up[0], v_w_ffn1_down[0], v_w_mix_in[0], v_w_mix_out[0], v_w_ffn2_gate_up[0],
             v_w_ffn2_down[0]]
    col_sharded = [True, False, True, False, True, False]
    geoms = [_Big(w.shape, cs) for w, cs in zip(big_w, col_sharded)]

    cx, cy, cc = _mesh_pos()
    idx = jnp.stack([2 * cx + cy, cc]).astype(jnp.int32)
    c_idx = jnp.reshape(cc, (1,)).astype(jnp.int32)
    first = _cast_into_full(f"cast_{big_names[0]}", big_w[0], geoms[0], idx)
    (first,), direct_sems, tok = _gather_first_direct(first, geoms[0])
    rest = []
    for n, w, gm in zip(big_names[1:], big_w[1:], geoms[1:]):
        tok = _cast_into_full(f"cast_{n}", w, gm, idx, after=tok)
        rest.append(tok)
    (first,), relay_sems, tok = _gather_first_relay(first, geoms[0], direct_sems, after=tok)
    started_rest, sems_rest, rest_started = _gather_start("gather_start_rest", rest, geoms[1:], after=tok)
    started, gather_sems = [first] + started_rest, [relay_sems] + sems_rest

    def forward_weight(w, after):
        return _gather_forward(f"gather_forward_{big_names[w]}", started[w], geoms[w], gather_sems[w], after,
                               arrivals=1 if w == 0 else 3)

    def whole_weight(w, forwarded, after):
        return _gather_end(f"gather_end_{big_names[w]}", forwarded[0], geoms[w], forwarded[1], after)

    h1 = _norm_fwd("ffn1_pre_norm", x0, pre_norm_ffn1)
    w_gu1 = whole_weight(0, forward_weight(0, rest_started), h1)
    act1, dact_dgate1, dact_dup1 = _ffn_gate_up_act("ffn1_gate_up", h1, w_gu1)
    w_d1 = whole_weight(1, forward_weight(1, act1), act1)
    ff1 = _mm("ffn1_down", act1, w_d1, "nn", F32)
    fw = forward_weight(2, ff1)
    x1, hm = _resid_norm_fwd("ffn1_residual", x0, ff1, post_norm_ffn1, pre_norm_mix, 0.5)
    w_in = whole_weight(2, fw, hm)
    p = _mm("mix_in", hm, w_in, "nn", F32)
    fw = forward_weight(3, p)
    o_f, o_b, st_f, st_b = _hgrn_scan_fwd("hgrn_scan", p, hgrn_lower_bounds_fwd, hgrn_lower_bounds_bwd)
    y_h = _hgrn_out_fwd("hgrn_out", o_f, o_b, p, hgrn_out_norm, 4)
    kv_blk0 = (5 * hw + aw) // kvw
    k_pad = _pad_kv("attn_pad_k", p, kv_blk0, kvw)
    v_pad = _pad_kv("attn_pad_v", p, kv_blk0 + 1, kvw)
    bucket_ids = _t5_bucket_ids()
    bias = _bias_gather("attn_bias", rel_bias_table.T, bucket_ids).reshape(nah, WINDOW, SPAN)
    y_a, attn_probs, attn_sink_probs = _attn_fwd("attn_fwd", p, k_pad, v_pad, bias, attn_sink, 5 * hw // aw)
    y_mix = _concat_cols("mix_concat", y_h, y_a)
    w_out = whole_weight(3, fw, y_mix)
    mixed = _mm("mix_out", y_mix, w_out, "nn", F32)
    fw = forward_weight(4, mixed)
    x2, h2 = _resid_norm_fwd("mix_residual", x1, mixed, post_norm_mix, pre_norm_ffn2, 1.0)
    w_gu2 = whole_weight(4, fw, h2)
    act2, dact_dgate2, dact_dup2 = _ffn_gate_up_act("ffn2_gate_up", h2, w_gu2)
    w_d2 = whole_weight(5, forward_weight(5, act2), act2)
    ff2 = _mm("ffn2_down", act2, w_d2, "nn", F32)
    loss_blk, dy, dff2, dg_post2 = _final_fwd_bwd("ffn2_residual_loss", x2, ff2, post_norm_ffn2, target, 0.5)

    reduce = [_GradReduce(n, gm, idx, c_idx) for n, gm in zip(big_names, geoms)]
    big_grads, big_delta, big_new_m, big_new_v = [None] * 6, [None] * 6, [None] * 6, [None] * 6

    def update(w, after):
        g, dl, nm, nv = _adamw(f"adamw_{big_names[w]}", big_w[w], reduce[w].finish(after), big_m[w], big_v[w])
        big_grads[w], big_delta[w], big_new_m[w], big_new_v[w] = g[None], dl[None], nm[None], nv[None]
        return dl

    def dw_start(w, x_act, dy_act, after=None):
        theirs = _dw_half(f"dw_theirs_{big_names[w]}", x_act, dy_act, geoms[w], c_idx, own=False, after=after)
        return reduce[w].pair_start(theirs)

    def dw_finish(w, x_act, dy_act, after):
        landed = reduce[w].pair_wait(after)
        half = _dw_half(f"dw_own_{big_names[w]}", x_act, dy_act, geoms[w], c_idx, own=True, addend=landed)
        return reduce[w].chip_start(half)

    tok = dw_start(5, act2, dff2)
    dgu2 = _ffn_dact("ffn2_dact", dff2, w_d2, dact_dgate2, dact_dup2, after=tok)
    tok = dw_finish(5, act2, dff2, after=dgu2)
    tok = dw_start(4, h2, dgu2, after=tok)
    dh2 = _ffn_dh("ffn2_dh", dgu2, w_gu2, after=tok)
    tok = dw_finish(4, h2, dgu2, after=dh2)
    dx2, dg_pre2, dmixed, dg_postm = _norms_bwd("mix_residual_bwd", dy, dh2, x2, pre_norm_ffn2,
                                                post=(mixed, post_norm_mix, 1.0), after=tok)
    tok = dw_start(3, y_mix, dmixed)
    dy_mix = _mm("mix_out_dx", dmixed, w_out, "nt", F32, after=tok)
    tok = dw_finish(3, y_mix, dmixed, after=dy_mix)
    dq_a, dk_pad, dv_pad, dbias, dsink = _attn_bwd("attn_bwd", p, k_pad, v_pad, attn_probs, attn_sink_probs, dy_mix,
                                                   5 * hw // aw, hw // aw, after=tok)
    tok = reduce[5].chip_finish(dq_a)
    drel_t = _bias_scatter("attn_dbias", dbias.reshape(nah, WINDOW * SPAN), bucket_ids)
    do, dg_h, dgain = _hgrn_out_bwd("hgrn_out_bwd", dy_mix, o_f, o_b, p, hgrn_out_norm, 4, after=tok)
    dq_f, dv_f, dz_f, dlb_f, dq_b, dv_b, dz_b, dlb_b = _hgrn_scan_bwd(
        "hgrn_scan_bwd", p, hgrn_lower_bounds_fwd, hgrn_lower_bounds_bwd, do, st_f, st_b)
    tok = reduce[4].chip_finish(dq_f)
    tok = reduce[3].chip_finish(tok)
    dp = _mix_dproj("mix_dproj", [(dq_f, dq_b), (dv_f, dv_b), (dz_f,), (dz_b,), (dg_h,), (dq_a,)],
                    [dk_pad, dv_pad], t, after=tok)
    tok = dw_start(2, hm, dp)
    dhm = _mm("mix_in_dx", dp, w_in, "nt", F32, after=tok)
    tok = dw_finish(2, hm, dp, after=dhm)
    dx1, dg_prem, dff1, dg_post1 = _norms_bwd("ffn1_residual_bwd", dx2, dhm, x1, pre_norm_mix,
                                              post=(ff1, post_norm_ffn1, 0.5), after=tok)
    tok = dw_start(1, act1, dff1)
    dgu1 = _ffn_dact("ffn1_dact", dff1, w_d1, dact_dgate1, dact_dup1, after=tok)
    tok = dw_finish(1, act1, dff1, after=dgu1)
    tok = reduce[2].chip_finish(tok)
    tok = dw_start(0, h1, dgu1, after=tok)
    done = update(2, tok)
    tok = dw_finish(0, h1, dgu1, after=done)
    dh1 = _ffn_dh("ffn1_dh", dgu1, w_gu1, after=tok)
    grad_x, dg_pre1 = _norms_bwd("ffn1_pre_norm_bwd", dx1, dh1, x0, pre_norm_ffn1)

    small_w = [pre_norm_ffn1, post_norm_ffn1, pre_norm_mix, post_norm_mix, hgrn_lower_bounds_fwd,
               hgrn_lower_bounds_bwd, hgrn_out_norm, attn_sink, pre_norm_ffn2, post_norm_ffn2, rel_bias_table]
    small_m = [m_pre_norm_ffn1, m_post_norm_ffn1, m_pre_norm_mix, m_post_norm_mix, m_hgrn_lower_bounds_fwd,
               m_hgrn_lower_bounds_bwd, m_hgrn_out_norm, m_attn_sink, m_pre_norm_ffn2, m_post_norm_ffn2,
               m_rel_bias_table]
    small_v = [v_pre_norm_ffn1, v_post_norm_ffn1, v_pre_norm_mix, v_post_norm_mix, v_hgrn_lower_bounds_fwd,
               v_hgrn_lower_bounds_bwd, v_hgrn_out_norm, v_attn_sink, v_pre_norm_ffn2, v_post_norm_ffn2,
               v_rel_bias_table]
    small_g = [dg_pre1, dg_post1, dg_prem, dg_postm, dlb_f, dlb_b, dgain, dsink[:, 0].reshape(1, nah), dg_pre2,
               dg_post2, drel_t.T]
    shapes = [a.shape for a in small_w]
    done = update(5, grad_x)
    done = update(4, done)
    done = update(3, done)
    summed = _all_reduce_small(_pack_rows(small_g + [loss_blk[0:1, 0:1]], d), after=done)
    loss = _unpack_rows(summed, shapes + [(1, 1)], d)[-1][0, 0]
    _, sd, sm, sv = _adamw("adamw_small", _pack_rows(small_w, d), summed, _pack_rows(small_m, d),
                           _pack_rows(small_v, d))
    small_grads = _unpack_rows(summed, shapes, d)
    small_delta, small_new_m, small_new_v = (_unpack_rows(a, shapes, d) for a in (sd, sm, sv))

    tok = reduce[1].chip_finish(sd)
    tok = reduce[0].chip_finish(tok)
    done = update(1, tok)
    update(0, done)

    def ordered(small, big):
        s = dict(zip(["pre1", "post1", "prem", "postm", "lbf", "lbb", "gain", "sink", "pre2", "post2", "rel"], small))
        b = dict(zip(["gu1", "d1", "win", "wout", "gu2", "d2"], big))
        return [s["pre1"], s["post1"], b["gu1"], b["d1"], s["prem"], s["postm"], b["win"], s["lbf"], s["lbb"],
                s["gain"], s["sink"], b["wout"], s["pre2"], s["post2"], b["gu2"], b["d2"], s["rel"]]

    return (loss, grad_x[None], *ordered(small_grads, big_grads), *ordered(small_delta, big_delta),
            *ordered(small_new_m, big_new_m), *ordered(small_new_v, big_new_v))
```

```python
import functools
import math

import jax
import jax.numpy as jnp
import numpy as np
from jax import lax
from jax.experimental import pallas as pl
from jax.experimental.pallas import tpu as pltpu

F32 = jnp.float32
BF16 = jnp.bfloat16

HEAD = 128
CHUNK = 64
WINDOW = 128
SPAN = 3 * WINDOW
KV_HEADS = 2
REL_BUCKETS = 32
REL_MAX_DIST = 128
EPS = 1e-6
NEG_INF = -1e30

ADAM_LR = 0.001
ADAM_B1 = 0.9
ADAM_B2 = 0.999
ADAM_EPS = 1e-08
ADAM_WD = 0.01
ADAM_STEP = 10

N_CHIPS = 4
N_DEV = 8
V7X_VMEM_BYTES = 64 * 1024 * 1024
MESH = pl.DeviceIdType.MESH
ANY = pl.BlockSpec(memory_space=pl.ANY)


def _tile(n, pref, mult):
    t = (min(pref, n) // mult) * mult
    while t >= mult:
        if n % t == 0:
            return t
        t -= mult
    return n


def _params(semantics, block_bytes):
    limit = min(V7X_VMEM_BYTES - (4 << 20), 2 * int(block_bytes) + (8 << 20))
    return pltpu.CompilerParams(dimension_semantics=semantics, vmem_limit_bytes=limit)


def _nbytes(shape, dtype):
    return int(np.prod(shape)) * jnp.dtype(dtype).itemsize


PIN_TO_HBM_BYTES = 4 << 20


def _pallas(body, **kw):
    def pin_shape(s):
        if isinstance(s, jax.ShapeDtypeStruct) and _nbytes(s.shape, s.dtype) >= PIN_TO_HBM_BYTES:
            return pltpu.HBM(s.shape, s.dtype)
        return s

    def pin(a):
        if getattr(a, "dtype", None) in (F32, BF16) and _nbytes(a.shape, a.dtype) >= PIN_TO_HBM_BYTES:
            return pltpu.with_memory_space_constraint(a, pltpu.HBM)
        return a

    out_shape = kw["out_shape"]
    kw["out_shape"] = [pin_shape(s) for s in out_shape] if isinstance(out_shape, (list, tuple)) else pin_shape(out_shape)
    call = pl.pallas_call(body, **kw)
    return lambda *args: call(*[pin(a) for a in args])


def _dot(a, b, ca=1, cb=0):
    return lax.dot_general(a, b, (((ca,), (cb,)), ((), ())), preferred_element_type=F32)


def _split3(x):
    hi = x.astype(BF16)
    r1 = x - hi.astype(F32)
    mid = r1.astype(BF16)
    lo = (r1 - mid.astype(F32)).astype(BF16)
    return hi, mid, lo


def _dot_exact(a, b, ca=1, cb=0, split="b"):
    if split == "b":
        return sum(_dot(a, p, ca, cb) for p in _split3(b))
    return sum(_dot(p, b, ca, cb) for p in _split3(a))


def _rms(x):
    return lax.rsqrt(jnp.mean(x * x, axis=-1, keepdims=True) + EPS)


def _norm_bwd(u, x, gain):
    r = _rms(x)
    xhat = x * r
    dgain = jnp.sum(u * xhat, axis=0, keepdims=True)
    v = u * gain
    dx = r * (v - xhat * jnp.mean(v * xhat, axis=-1, keepdims=True))
    return dx, dgain


def _sigmoid(x):
    return 1.0 / (1.0 + jnp.exp(-x))


def _accumulate(ref, val, first):
    @pl.when(first)
    def _():
        ref[...] = val

    @pl.when(jnp.logical_not(first))
    def _():
        ref[...] += val


def _ordered(body, ins, in_specs, after):
    if after is None:
        return body, list(ins), list(in_specs)
    n_in = len(ins)

    def wrapped(*refs):
        body(*refs[:n_in], *refs[n_in + 1:])

    return wrapped, list(ins) + [after], list(in_specs) + [pl.BlockSpec(memory_space=pl.ANY)]


def _matmul(name, a, b, *, form, out_dtype, tm, tn, tk, a_map=None, b_map=None,
            out_shape=None, out_block=None, out_map=None, sizes=None, after=None):
    if sizes is None:
        if form == "nn":
            (m, k), n = a.shape, b.shape[1]
        elif form == "nt":
            (m, k), n = a.shape, b.shape[0]
        else:
            (k, m), n = a.shape, b.shape[1]
    else:
        m, n, k = sizes
    gi, gj, gk = m // tm, n // tn, k // tk
    a_blk = (tm, tk) if form != "tn" else (tk, tm)
    b_blk = (tk, tn) if form != "nt" else (tn, tk)
    if a_map is None:
        a_map = (lambda i, j, kk: (i, kk)) if form != "tn" else (lambda i, j, kk: (kk, i))
    else:
        a_blk = (None,) + a_blk
    if b_map is None:
        b_map = (lambda i, j, kk: (kk, j)) if form != "nt" else (lambda i, j, kk: (j, kk))
    else:
        b_blk = (None,) + b_blk
    if out_shape is None:
        out_shape, out_block, out_map = (m, n), (tm, tn), (lambda i, j, kk: (i, j))
    ca, cb = {"nn": (1, 0), "nt": (1, 1), "tn": (0, 0)}[form]

    def body(a_ref, b_ref, o_ref, *acc):
        part = _dot(a_ref[...], b_ref[...], ca, cb)
        if gk == 1:
            o_ref[...] = part.astype(o_ref.dtype)
        else:
            kk = pl.program_id(2)
            _accumulate(acc[0], part, kk == 0)

            @pl.when(kk == gk - 1)
            def _():
                o_ref[...] = acc[0][...].astype(o_ref.dtype)

    scratch = [] if gk == 1 else [pltpu.VMEM((tm, tn), F32)]
    vmem = (_nbytes((tm, tk), a.dtype) + _nbytes((tk, tn), b.dtype) + _nbytes((tm, tn), out_dtype)
            + 2 * _nbytes((tm, tn), F32))
    body, ins, in_specs = _ordered(body, [a, b], [pl.BlockSpec(a_blk, a_map), pl.BlockSpec(b_blk, b_map)], after)
    return _pallas(
        body, name=name, grid=(gi, gj, gk), in_specs=in_specs,
        out_specs=pl.BlockSpec(out_block, out_map),
        out_shape=jax.ShapeDtypeStruct(out_shape, out_dtype),
        scratch_shapes=scratch,
        compiler_params=_params(("parallel", "parallel", "arbitrary"), vmem),
    )(*ins)


V7X_HBM_BYTES_PER_US = 3.0e6
V7X_MXU_FLOPS_PER_US = 0.9e9
V7X_VMEM_RMW_BYTES_PER_US = 10e6
GRID_STEP_US = 0.35
MATMUL_VMEM_BUDGET = 40 << 20
MATMUL_MAX_TILE_FLOPS = 1 << 33


def _divisors(n, mult, lo):
    return [t for t in range(mult, n + 1, mult) if n % t == 0 and t >= min(lo, n)]


def _mm_tiles(m, n, k, out_dtype=F32, n_unit=None, k_unit=None):
    out_bytes = jnp.dtype(out_dtype).itemsize
    best = None
    for tm in _divisors(m, 128, 256):
        for tn in _divisors(n_unit or n, 128, 256):
            for tk in _divisors(k_unit or k, 128, 512):
                gi, gj, gk = m // tm, n // tn, k // tk
                vmem = 4 * tm * tk + 4 * tk * tn + 2 * tm * tn * out_bytes + 4 * tm * tn * (2 if gk > 1 else 1)
                if vmem > MATMUL_VMEM_BUDGET or 2 * tm * tn * tk > MATMUL_MAX_TILE_FLOPS:
                    continue
                a_bytes = 2 * m * k * (gj if gk > 1 else 1)
                b_bytes = 2 * k * n * (1 if gj == 1 and gk == 1 else gi)
                hbm_us = (a_bytes + b_bytes + m * n * out_bytes) / V7X_HBM_BYTES_PER_US
                acc_us = (8 * m * n * gk / V7X_VMEM_RMW_BYTES_PER_US) if gk > 1 else 0.0
                cost = max(2 * m * n * k / V7X_MXU_FLOPS_PER_US, 1.3 * hbm_us) + GRID_STEP_US * gi * gj * gk + acc_us
                key = (round(cost, 1), vmem)
                if best is None or key < best[0]:
                    best = (key, (tm, tn, tk))
    return best[1]


def _mm(name, a, b, form, out_dtype, after=None):
    if form == "nn":
        m, k, n = a.shape[0], a.shape[1], b.shape[1]
    elif form == "nt":
        m, k, n = a.shape[0], a.shape[1], b.shape[0]
    else:
        m, k, n = a.shape[1], a.shape[0], b.shape[1]
    tm, tn, tk = _mm_tiles(m, n, k, out_dtype)
    return _matmul(name, a, b, form=form, out_dtype=out_dtype, tm=tm, tn=tn, tk=tk, after=after)


def _row_tile(t):
    return _tile(t, 256, 8)


def _norm_fwd(name, x, gain):
    t, d = x.shape
    tm = _row_tile(t)

    def body(x_ref, g_ref, h_ref):
        xv = x_ref[...]
        h_ref[...] = (xv * _rms(xv) * g_ref[...]).astype(BF16)

    row = pl.BlockSpec((tm, d), lambda i: (i, 0))
    vec = pl.BlockSpec((1, d), lambda i: (0, 0))
    return _pallas(
        body, name=name, grid=(t // tm,), in_specs=[row, vec], out_specs=row,
        out_shape=jax.ShapeDtypeStruct((t, d), BF16),
        compiler_params=_params(("parallel",), 2 * _nbytes((tm, d), F32)),
    )(x, gain)


def _resid_norm_fwd(name, xres, ff, gpost, gpre, scale):
    t, d = xres.shape
    tm = _row_tile(t)

    def body(x_ref, f_ref, gp_ref, gn_ref, xn_ref, h_ref):
        f = f_ref[...]
        xn = x_ref[...] + scale * (f * _rms(f) * gp_ref[...])
        xn_ref[...] = xn
        h_ref[...] = (xn * _rms(xn) * gn_ref[...]).astype(BF16)

    row = pl.BlockSpec((tm, d), lambda i: (i, 0))
    vec = pl.BlockSpec((1, d), lambda i: (0, 0))
    return _pallas(
        body, name=name, grid=(t // tm,), in_specs=[row, row, vec, vec], out_specs=[row, row],
        out_shape=[jax.ShapeDtypeStruct((t, d), F32), jax.ShapeDtypeStruct((t, d), BF16)],
        compiler_params=_params(("parallel",), 4 * _nbytes((tm, d), F32)),
    )(xres, ff, gpost, gpre)


def _final_fwd_bwd(name, xres, ff, gpost, target, scale):
    t, d = xres.shape
    tm = _row_tile(t)

    def body(x_ref, f_ref, gp_ref, t_ref, loss_ref, dy_ref, dff_ref, dg_ref):
        i = pl.program_id(0)
        f = f_ref[...]
        gp = gp_ref[...]
        y = x_ref[...] + scale * (f * _rms(f) * gp)
        err = y - t_ref[...]
        part = 0.5 * jnp.sum(jnp.mean(err * err, axis=-1, keepdims=True), axis=0, keepdims=True)
        _accumulate(loss_ref, jnp.broadcast_to(part, loss_ref.shape), i == 0)
        dy = err / d
        dy_ref[...] = dy
        dff, dg = _norm_bwd(scale * dy, f, gp)
        dff_ref[...] = dff.astype(BF16)
        _accumulate(dg_ref, dg, i == 0)

    row = pl.BlockSpec((tm, d), lambda i: (i, 0))
    vec = pl.BlockSpec((1, d), lambda i: (0, 0))
    return _pallas(
        body, name=name, grid=(t // tm,), in_specs=[row, row, vec, row],
        out_specs=[pl.BlockSpec((8, 128), lambda i: (0, 0)), row, row, vec],
        out_shape=[jax.ShapeDtypeStruct((8, 128), F32), jax.ShapeDtypeStruct((t, d), F32),
                   jax.ShapeDtypeStruct((t, d), BF16), jax.ShapeDtypeStruct((1, d), F32)],
        compiler_params=_params(("arbitrary",), 5 * _nbytes((tm, d), F32)),
    )(xres, ff, gpost, target)


def _norms_bwd(name, dres, dh, xin, gpre, post=None, after=None):
    t, d = dres.shape
    tm = _row_tile(t)
    with_post = post is not None

    def body(*refs):
        if with_post:
            dr_ref, dh_ref, x_ref, g_ref, f_ref, gp_ref, dx_ref, dg_ref, dff_ref, dgp_ref = refs
        else:
            dr_ref, dh_ref, x_ref, g_ref, dx_ref, dg_ref = refs
        i = pl.program_id(0)
        dx, dg = _norm_bwd(dh_ref[...], x_ref[...], g_ref[...])
        dx = dr_ref[...] + dx
        dx_ref[...] = dx
        _accumulate(dg_ref, dg, i == 0)
        if with_post:
            dff, dgp = _norm_bwd(post[2] * dx, f_ref[...], gp_ref[...])
            dff_ref[...] = dff.astype(BF16)
            _accumulate(dgp_ref, dgp, i == 0)

    row = pl.BlockSpec((tm, d), lambda i: (i, 0))
    vec = pl.BlockSpec((1, d), lambda i: (0, 0))
    ins, in_specs = [dres, dh, xin, gpre], [row, row, row, vec]
    out_specs = [row, vec]
    out_shape = [jax.ShapeDtypeStruct((t, d), F32), jax.ShapeDtypeStruct((1, d), F32)]
    if with_post:
        ins += [post[0], post[1]]
        in_specs += [row, vec]
        out_specs += [row, vec]
        out_shape += [jax.ShapeDtypeStruct((t, d), BF16), jax.ShapeDtypeStruct((1, d), F32)]
    body, ins, in_specs = _ordered(body, ins, in_specs, after)
    return _pallas(
        body, name=name, grid=(t // tm,), in_specs=in_specs, out_specs=out_specs, out_shape=out_shape,
        compiler_params=_params(("arbitrary",), 6 * _nbytes((tm, d), F32)),
    )(*ins)


SWIGLU_TILE = (1024, 512)


def _ffn_gate_up_act(name, h, w_gu):
    t, d = h.shape
    f = w_gu.shape[1] // 2
    tm, tn = _tile(t, SWIGLU_TILE[0], 128), _tile(f, SWIGLU_TILE[1], 128)
    nf = f // tn

    def body(h_ref, wg_ref, wu_ref, a_ref, dg_ref, du_ref):
        hv = h_ref[...]
        g = _dot(hv, wg_ref[...])
        u = _dot(hv, wu_ref[...])
        sig = _sigmoid(g)
        silu = g * sig
        a_ref[...] = (silu * u).astype(BF16)
        dg_ref[...] = (u * sig * (1.0 + g * (1.0 - sig))).astype(BF16)
        du_ref[...] = silu.astype(BF16)

    out = jax.ShapeDtypeStruct((t, f), BF16)
    blk = pl.BlockSpec((tm, tn), lambda i, j: (i, j))
    return _pallas(
        body, name=name, grid=(t // tm, nf),
        in_specs=[pl.BlockSpec((tm, d), lambda i, j: (i, 0)), pl.BlockSpec((d, tn), lambda i, j: (0, j)),
                  pl.BlockSpec((d, tn), lambda i, j: (0, j + nf))],
        out_specs=[blk, blk, blk], out_shape=[out, out, out],
        compiler_params=_params(("parallel", "parallel"),
                                _nbytes((tm, d), BF16) + 2 * _nbytes((d, tn), BF16) + 5 * _nbytes((tm, tn), F32)),
    )(h, w_gu, w_gu)


def _ffn_dact(name, dff, w_down, dact_dgate, dact_dup, after=None):
    t, d = dff.shape
    f = w_down.shape[0]
    tm, tn = _tile(t, SWIGLU_TILE[0], 128), _tile(f, SWIGLU_TILE[1], 128)

    def body(d_ref, w_ref, dg_ref, du_ref, o_ref):
        da = _dot(d_ref[...], w_ref[...], 1, 1)
        o_ref[0] = (da * dg_ref[...].astype(F32)).astype(BF16)
        o_ref[1] = (da * du_ref[...].astype(F32)).astype(BF16)

    blk = pl.BlockSpec((tm, tn), lambda i, j: (i, j))
    body, ins, in_specs = _ordered(
        body, [dff, w_down, dact_dgate, dact_dup],
        [pl.BlockSpec((tm, d), lambda i, j: (i, 0)), pl.BlockSpec((tn, d), lambda i, j: (j, 0)), blk, blk], after)
    return _pallas(
        body, name=name, grid=(t // tm, f // tn), in_specs=in_specs,
        out_specs=pl.BlockSpec((2, tm, tn), lambda i, j: (0, i, j)),
        out_shape=jax.ShapeDtypeStruct((2, t, f), BF16),
        compiler_params=_params(("parallel", "parallel"),
                                _nbytes((tm, d), BF16) + _nbytes((tn, d), BF16) + 5 * _nbytes((tm, tn), F32)),
    )(*ins)


def _ffn_dh(name, dgu, w_gu, after=None):
    _, t, f = dgu.shape
    d = w_gu.shape[0]
    tm, tn, tk = _mm_tiles(t, d, 2 * f, F32, k_unit=f)
    nkf = f // tk
    return _matmul(name, dgu, w_gu, form="nt", out_dtype=F32, tm=tm, tn=tn, tk=tk, sizes=(t, d, 2 * f),
                   a_map=lambda i, j, kk: (kk // nkf, i, kk % nkf), after=after)


def _lower_bound(lbp):
    m = jnp.max(lbp, axis=0, keepdims=True)
    e = jnp.exp(lbp - m)
    return e[0:1] / jnp.sum(e, axis=0, keepdims=True)


def _chunk_mask(reverse):
    row = lax.broadcasted_iota(jnp.int32, (CHUNK, CHUNK), 0)
    col = lax.broadcasted_iota(jnp.int32, (CHUNK, CHUNK), 1)
    return (col >= row) if reverse else (col <= row)


def _hgrn_gates(z, lb, mask_bf):
    sig = _sigmoid(z)
    f = lb + (1.0 - lb) * sig
    logf = jnp.log(f)
    k = 1.0 - f
    cum = _dot_exact(mask_bf, logf)
    last = jnp.sum(logf, axis=0, keepdims=True)
    return sig, f, k, cum, last


def _hgrn_scan_fwd(name, p, lbp_f, lbp_b):
    t = p.shape[0]
    hw = lbp_f.shape[1]
    nh, nc = hw // HEAD, t // CHUNK

    def body(qf, vf, zf, qb, vb, zb, lbf, lbb, of_ref, ob_ref, stf_ref, stb_ref, state):
        n = pl.program_id(0)

        @pl.when(n == 0)
        def _():
            state[...] = jnp.zeros_like(state)

        directions = [(qf, vf, zf, lbf, of_ref, stf_ref), (qb, vb, zb, lbb, ob_ref, stb_ref)]
        wide = []
        for d, (q_ref, v_ref, z_ref, lb_ref, o_ref, st_ref) in enumerate(directions):
            mask = _chunk_mask(d == 1)
            lb = _lower_bound(lb_ref[...])
            _, _, k, cum, last = _hgrn_gates(z_ref[...], lb, mask.astype(BF16))
            v = v_ref[...].astype(BF16)
            qd = (q_ref[...] * jnp.exp(cum)).astype(BF16)
            kd = (k * jnp.exp(-cum)).astype(BF16)
            kt = (k * jnp.exp(last - cum)).astype(BF16)
            s_all = state[d]
            st_ref[...] = s_all
            wide.append((mask, v, qd, kd, kt, jnp.exp(last), s_all, o_ref))
        pairs = [(d, slice(h * HEAD, (h + 1) * HEAD)) for d in range(2) for h in range(nh)]
        a = [jnp.where(wide[d][0], _dot(wide[d][2][:, sl], wide[d][3][:, sl], 1, 1), 0.0).astype(BF16)
             for d, sl in pairs]
        inter = [_dot(wide[d][2][:, sl], wide[d][6][:, sl].astype(BF16), 1, 1) for d, sl in pairs]
        intra = [_dot(a[i], wide[d][1][:, sl]) for i, (d, sl) in enumerate(pairs)]
        grow = [_dot(wide[d][1][:, sl], wide[d][4][:, sl], 0, 0) for d, sl in pairs]
        for i, (d, sl) in enumerate(pairs):
            wide[d][7][:, sl] = intra[i] + inter[i]
            state[d, :, sl] = wide[d][6][:, sl] * wide[d][5][:, sl] + grow[i]

    def col(group, reverse):
        return pl.BlockSpec((CHUNK, hw), lambda n: ((nc - 1 - n) if reverse else n, group))

    def st(reverse):
        return pl.BlockSpec((None, HEAD, hw), lambda n: ((nc - 1 - n) if reverse else n, 0, 0))

    lb_spec = pl.BlockSpec((2, hw), lambda n: (0, 0))
    out = jax.ShapeDtypeStruct((t, hw), F32)
    states = jax.ShapeDtypeStruct((nc, HEAD, hw), F32)
    return _pallas(
        body, name=name, grid=(nc,),
        in_specs=[col(0, False), col(1, False), col(2, False), col(0, True), col(1, True), col(3, True),
                  lb_spec, lb_spec],
        out_specs=[col(0, False), col(0, True), st(False), st(True)],
        out_shape=[out, out, states, states],
        scratch_shapes=[pltpu.VMEM((2, HEAD, hw), F32)],
        compiler_params=_params(("arbitrary",), 12 * _nbytes((HEAD, hw), F32)),
    )(p, p, p, p, p, p, lbp_f, lbp_b)


def _hgrn_scan_bwd(name, p, lbp_f, lbp_b, do, st_f, st_b):
    t = p.shape[0]
    hw = lbp_f.shape[1]
    nh, nc = hw // HEAD, t // CHUNK

    def body(qf, vf, zf, dof, sf, qb, vb, zb, dob, sb, lbf, lbb, dqf, dvf, dzf, dlbf, dqb, dvb, dzb, dlbb,
             dstate, dlb_acc, dqd_s, dkd_s, dkt_s, ddec_s):
        n = pl.program_id(0)

        @pl.when(n == 0)
        def _():
            dstate[...] = jnp.zeros_like(dstate)
            dlb_acc[...] = jnp.zeros_like(dlb_acc)

        directions = [(qf, vf, zf, dof, sf, lbf, dqf, dvf, dzf, dlbf), (qb, vb, zb, dob, sb, lbb, dqb, dvb, dzb, dlbb)]
        for d, (q_ref, v_ref, z_ref, do_ref, st_ref, lb_ref, dq_ref, dv_ref, dz_ref, dlb_ref) in enumerate(directions):
            mask = _chunk_mask(d == 1)
            mask_bf = mask.astype(BF16)
            lb = _lower_bound(lb_ref[...])
            sig, f, k, cum, last = _hgrn_gates(z_ref[...], lb, mask_bf)
            e_pos, e_neg, e_tail = jnp.exp(cum), jnp.exp(-cum), jnp.exp(last - cum)
            dec = jnp.exp(last)
            v = v_ref[...].astype(BF16)
            qd, kd, kt = q_ref[...] * e_pos, k * e_neg, k * e_tail
            qd_bf, kd_bf, kt_bf = qd.astype(BF16), kd.astype(BF16), kt.astype(BF16)
            s_all = st_ref[...]
            ds_all = dstate[d]
            dov = do_ref[...].astype(BF16)
            cols = [slice(h * HEAD, (h + 1) * HEAD) for h in range(nh)]
            s_bf = [s_all[:, sl].astype(BF16) for sl in cols]
            ds_bf = [ds_all[:, sl].astype(BF16) for sl in cols]
            a = [jnp.where(mask, _dot(qd_bf[:, sl], kd_bf[:, sl], 1, 1), 0.0).astype(BF16) for sl in cols]
            da = [jnp.where(mask, _dot(dov[:, sl], v[:, sl], 1, 1), 0.0).astype(BF16) for sl in cols]
            dv_h = [_dot(a[h], dov[:, sl], 0, 0) + _dot(kt_bf[:, sl], ds_bf[h], 1, 1) for h, sl in enumerate(cols)]
            dqd_h = [_dot(da[h], kd_bf[:, sl]) + _dot(dov[:, sl], s_bf[h]) for h, sl in enumerate(cols)]
            dkd_h = [_dot(da[h], qd_bf[:, sl], 0, 0) for h, sl in enumerate(cols)]
            dkt_h = [_dot(v[:, sl], ds_bf[h]) for h, sl in enumerate(cols)]
            dst_h = [_dot(dov[:, sl], qd_bf[:, sl], 0, 0) + ds_all[:, sl] * dec[:, sl] for sl in cols]
            for h, sl in enumerate(cols):
                dv_ref[:, sl] = dv_h[h]
                dqd_s[:, sl] = dqd_h[h]
                dkd_s[:, sl] = dkd_h[h]
                dkt_s[:, sl] = dkt_h[h]
                dstate[d, :, sl] = dst_h[h]
                ddec_s[:, sl] = jnp.sum(ds_all[:, sl] * s_all[:, sl], axis=0, keepdims=True)
            dqd, dkd, dkt = dqd_s[...], dkd_s[...], dkt_s[...]
            dlast = jnp.sum(dkt * kt, axis=0, keepdims=True) + dec * ddec_s[...]
            dq_ref[...] = dqd * e_pos
            dk = dkd * e_neg + dkt * e_tail
            dcum = dqd * qd - dkd * kd - dkt * kt
            dlogf = _dot_exact(mask_bf, dcum, 0, 0) + dlast
            df = dlogf / f - dk
            dz_ref[...] = df * (1.0 - lb) * sig * (1.0 - sig)
            dlb_acc[d] += jnp.sum(df * (1.0 - sig), axis=0, keepdims=True)

            @pl.when(n == nc - 1)
            def _():
                g = dlb_acc[d] * lb * (1.0 - lb)
                dlb_ref[0:1, :] = g
                dlb_ref[1:2, :] = -g

    def col(group, reverse):
        return pl.BlockSpec((CHUNK, hw), lambda n: (n if reverse else (nc - 1 - n), group))

    def st(reverse):
        return pl.BlockSpec((None, HEAD, hw), lambda n: (n if reverse else (nc - 1 - n), 0, 0))

    lb_spec = pl.BlockSpec((2, hw), lambda n: (0, 0))
    out = jax.ShapeDtypeStruct((t, hw), F32)
    dlb = jax.ShapeDtypeStruct((2, hw), F32)
    wide = pltpu.VMEM((CHUNK, hw), F32)
    return _pallas(
        body, name=name, grid=(nc,),
        in_specs=[col(0, False), col(1, False), col(2, False), col(0, False), st(False),
                  col(0, True), col(1, True), col(3, True), col(0, True), st(True), lb_spec, lb_spec],
        out_specs=[col(0, False), col(0, False), col(0, False), lb_spec,
                   col(0, True), col(0, True), col(0, True), lb_spec],
        out_shape=[out, out, out, dlb, out, out, out, dlb],
        scratch_shapes=[pltpu.VMEM((2, HEAD, hw), F32), pltpu.VMEM((2, 1, hw), F32), wide, wide, wide,
                        pltpu.VMEM((1, hw), F32)],
        compiler_params=_params(("arbitrary",), 16 * _nbytes((HEAD, hw), F32)),
    )(p, p, p, do, st_f, p, p, p, do, st_b, lbp_f, lbp_b)


def _hgrn_out_fwd(name, o_f, o_b, p, gain, g_group):
    t, hw = o_f.shape
    nh = hw // HEAD
    tm = _tile(t, 256, 16)

    def body(of_ref, ob_ref, g_ref, gain_ref, y_ref):
        o_all = of_ref[...] + ob_ref[...]
        g_all = g_ref[...]
        scale_all = gain_ref[...] * (g_all * _sigmoid(g_all))
        for h in range(nh):
            sl = slice(h * HEAD, (h + 1) * HEAD)
            o = o_all[:, sl]
            y_ref[:, sl] = (o * _rms(o) * scale_all[:, sl]).astype(BF16)

    blk = pl.BlockSpec((tm, hw), lambda i: (i, 0))
    return _pallas(
        body, name=name, grid=(t // tm,),
        in_specs=[blk, blk, pl.BlockSpec((tm, hw), lambda i: (i, g_group)), pl.BlockSpec((1, hw), lambda i: (0, 0))],
        out_specs=blk, out_shape=jax.ShapeDtypeStruct((t, hw), BF16),
        compiler_params=_params(("parallel",), 5 * _nbytes((tm, hw), F32)),
    )(o_f, o_b, p, gain)


def _hgrn_out_bwd(name, dy, o_f, o_b, p, gain, g_group, after=None):
    t, hw = o_f.shape
    nh = hw // HEAD
    tm = _tile(t, 256, 8)

    def body(dy_ref, of_ref, ob_ref, g_ref, gain_ref, do_ref, dg_ref, dgain_ref):
        i = pl.program_id(0)
        o_all = of_ref[...] + ob_ref[...]
        g_all = g_ref[...]
        sig_all = _sigmoid(g_all)
        dy_all = dy_ref[...]
        up_all = dy_all * (g_all * sig_all)
        dsilu_all = dy_all * sig_all * (1.0 + g_all * (1.0 - sig_all))
        gain_all = gain_ref[...]
        for h in range(nh):
            sl = slice(h * HEAD, (h + 1) * HEAD)
            o, gain_v = o_all[:, sl], gain_all[:, sl]
            do, dgain = _norm_bwd(up_all[:, sl], o, gain_v)
            do_ref[:, sl] = do
            dg_ref[:, sl] = dsilu_all[:, sl] * (o * _rms(o) * gain_v)
            _accumulate(dgain_ref.at[:, sl], dgain, i == 0)

    blk = pl.BlockSpec((tm, hw), lambda i: (i, 0))
    vec = pl.BlockSpec((1, hw), lambda i: (0, 0))
    out = jax.ShapeDtypeStruct((t, hw), F32)
    body, ins, in_specs = _ordered(
        body, [dy, o_f, o_b, p, gain], [blk, blk, blk, pl.BlockSpec((tm, hw), lambda i: (i, g_group)), vec], after)
    return _pallas(
        body, name=name, grid=(t // tm,), in_specs=in_specs,
        out_specs=[blk, blk, vec], out_shape=[out, out, jax.ShapeDtypeStruct((1, hw), F32)],
        compiler_params=_params(("arbitrary",), 7 * _nbytes((tm, hw), F32)),
    )(*ins)


def _t5_bucket_ids():
    c = np.arange(WINDOW)[:, None]
    s = np.arange(SPAN)[None, :]
    rel = s - WINDOW - c
    nb = REL_BUCKETS // 2
    max_exact = nb // 2
    bucket = (rel > 0).astype(np.int32) * nb
    n = np.abs(rel)
    large = max_exact + (np.log(np.maximum(n, 1) / max_exact) / np.log(REL_MAX_DIST / max_exact)
                         * (nb - max_exact)).astype(np.int32)
    large = np.minimum(large, nb - 1)
    ids = bucket + np.where(n < max_exact, n, large).astype(np.int32)
    return jnp.asarray(ids.reshape(1, WINDOW * SPAN), jnp.int32)


def _bias_onehot(ids_ref):
    n = ids_ref.shape[1]
    return (lax.broadcasted_iota(jnp.int32, (REL_BUCKETS, n), 0) == ids_ref[...]).astype(BF16)


def _bias_gather(name, table_t, ids):
    nh = table_t.shape[0]

    def body(t_ref, ids_ref, o_ref):
        o_ref[...] = _dot_exact(t_ref[...], _bias_onehot(ids_ref), split="a")

    return _pallas(
        body, name=name, out_shape=jax.ShapeDtypeStruct((nh, ids.shape[1]), F32),
        compiler_params=pltpu.CompilerParams(vmem_limit_bytes=32 << 20),
    )(table_t, ids)


def _bias_scatter(name, dbias, ids):
    nh = dbias.shape[0]

    def body(d_ref, ids_ref, o_ref):
        o_ref[...] = _dot_exact(d_ref[...], _bias_onehot(ids_ref), 1, 1, split="a")

    return _pallas(
        body, name=name, out_shape=jax.ShapeDtypeStruct((nh, REL_BUCKETS), F32),
        compiler_params=pltpu.CompilerParams(vmem_limit_bytes=32 << 20),
    )(dbias, ids)


def _attn_valid(i, t):
    c = lax.broadcasted_iota(jnp.int32, (WINDOW, SPAN), 0)
    s = lax.broadcasted_iota(jnp.int32, (WINDOW, SPAN), 1)
    rel = s - WINDOW - c
    pos = i * WINDOW - WINDOW + s
    return (jnp.abs(rel) <= WINDOW) & (pos >= 0) & (pos < t)


def _attn_probs(qs, khs, b_ref, s_ref, valid):
    heads = range(len(qs))
    sinks = [s_ref[0:1, h:h + 1] for h in heads]
    s = [_dot(qs[h], khs[h], 1, 1) / math.sqrt(HEAD) for h in heads]
    s = [jnp.where(valid, s[h] + b_ref[h], NEG_INF) for h in heads]
    m = [jnp.maximum(jnp.max(s[h], axis=-1, keepdims=True), sinks[h]) for h in heads]
    e = [jnp.exp(s[h] - m[h]) for h in heads]
    es = [jnp.exp(sinks[h] - m[h]) for h in heads]
    inv = [1.0 / (jnp.sum(e[h], axis=-1, keepdims=True) + es[h]) for h in heads]
    return [e[h] * inv[h] for h in heads], [es[h] * inv[h] for h in heads]


def _kv_window_specs(kv_blk, kvw, nb):
    return [pl.BlockSpec((WINDOW, kvw), lambda i, s=s: (jnp.clip(i + s, 0, nb - 1), kv_blk)) for s in (-1, 0, 1)]


def _kv_window(refs):
    return jnp.concatenate([r[...] for r in refs], axis=0).astype(BF16)


def _attn_fwd(name, p, kv_blk, kvw, bias, sink, q_group_blk):
    t = p.shape[0]
    nh = bias.shape[0]
    aw = nh * HEAD
    grp = nh // KV_HEADS
    nb = t // WINDOW

    def body(q_ref, kp, kc, kn, vp, vc, vn, b_ref, s_ref, y_ref, pr_ref, ps_ref):
        i = pl.program_id(0)
        valid = _attn_valid(i, t)
        ks = _kv_window((kp, kc, kn))
        vs = _kv_window((vp, vc, vn))
        heads = range(nh)
        col = lambda h: slice(h * HEAD, (h + 1) * HEAD)
        qs = [q_ref[:, col(h)].astype(BF16) for h in heads]
        pr, ps = _attn_probs(qs, [ks[:, col(h // grp)] for h in heads], b_ref, s_ref, valid)
        pr = [pr[h].astype(BF16) for h in heads]
        out = [_dot(pr[h], vs[:, col(h // grp)]) for h in heads]
        lane = lax.broadcasted_iota(jnp.int32, (WINDOW, 128), 1)
        sinks = jnp.zeros((WINDOW, 128), F32)
        for h in heads:
            y_ref[:, col(h)] = out[h].astype(BF16)
            pr_ref[h] = pr[h]
            sinks = jnp.where(lane == h, ps[h], sinks)
        ps_ref[...] = sinks

    full = lambda a: pl.BlockSpec(a.shape, lambda i: (0,) * a.ndim)
    return _pallas(
        body, name=name, grid=(nb,),
        in_specs=[pl.BlockSpec((WINDOW, aw), lambda i: (i, q_group_blk)), *_kv_window_specs(kv_blk, kvw, nb),
                  *_kv_window_specs(kv_blk + 1, kvw, nb), full(bias), full(sink)],
        out_specs=[pl.BlockSpec((WINDOW, aw), lambda i: (i, 0)), pl.BlockSpec((nh, WINDOW, SPAN), lambda i: (0, i, 0)),
                   pl.BlockSpec((WINDOW, 128), lambda i: (i, 0))],
        out_shape=[jax.ShapeDtypeStruct((t, aw), BF16), jax.ShapeDtypeStruct((nh, t, SPAN), BF16),
                   jax.ShapeDtypeStruct((t, 128), F32)],
        compiler_params=_params(("parallel",), 3 * _nbytes(bias.shape, F32)),
    )(p, p, p, p, p, p, p, bias, sink)


def _attn_bwd(name, p, kv_blk, kvw, probs, sink_probs, dy, q_group_blk, dy_blk, after=None):
    t = p.shape[0]
    nh = probs.shape[0]
    aw = nh * HEAD
    grp = nh // KV_HEADS
    nb = t // WINDOW

    def body(q_ref, kp, kc, kn, vp, vc, vn, pr_ref, ps_ref, dy_ref, dq_ref, dk_ref, dv_ref, db_ref, ds_ref):
        i = pl.program_id(0)

        @pl.when(i == 0)
        def _():
            dk_ref[...] = jnp.zeros_like(dk_ref)
            dv_ref[...] = jnp.zeros_like(dv_ref)
            db_ref[...] = jnp.zeros_like(db_ref)
            ds_ref[...] = jnp.zeros_like(ds_ref)

        start = pl.multiple_of(i * WINDOW, WINDOW)
        ks = _kv_window((kp, kc, kn))
        vs = _kv_window((vp, vc, vn))
        inv_sqrt = 1.0 / math.sqrt(HEAD)
        heads = range(nh)
        col = lambda h: slice(h * HEAD, (h + 1) * HEAD)
        qs = [q_ref[:, col(h)].astype(BF16) for h in heads]
        khs = [ks[:, col(h // grp)] for h in heads]
        pr_bf = [pr_ref[h] for h in heads]
        pr = [pr_bf[h].astype(F32) for h in heads]
        dos = [dy_ref[:, col(h)].astype(BF16) for h in heads]
        dp = [_dot(dos[h], vs[:, col(h // grp)], 1, 1) for h in heads]
        delta = [jnp.sum(pr[h] * dp[h], axis=-1, keepdims=True) for h in heads]
        dsc = [pr[h] * (dp[h] - delta[h]) for h in heads]
        dsr = [(dsc[h] * inv_sqrt).astype(BF16) for h in heads]
        dq = [_dot(dsr[h], khs[h]) for h in heads]
        dk = [_dot(dsr[h], qs[h], 0, 0) for h in heads]
        dv = [_dot(pr_bf[h], dos[h], 0, 0) for h in heads]
        for h in heads:
            db_ref[h] += dsc[h]
            dsink = jnp.sum(-ps_ref[:, h:h + 1] * delta[h], axis=0, keepdims=True)
            ds_ref[h:h + 1, :] += jnp.broadcast_to(dsink, (1, 128))
            dq_ref[:, col(h)] = dq[h]
        for kv in range(KV_HEADS):
            group = range(kv * grp, (kv + 1) * grp)
            dk_ref[pl.ds(start, SPAN), col(kv)] += sum(dk[h] for h in group)
            dv_ref[pl.ds(start, SPAN), col(kv)] += sum(dv[h] for h in group)

    whole = lambda shape: pl.BlockSpec(shape, lambda i: (0,) * len(shape))
    pad_shape = (t + 2 * WINDOW, kvw)
    bias_shape = (nh, WINDOW, SPAN)
    body, ins, in_specs = _ordered(
        body, [p, p, p, p, p, p, p, probs, sink_probs, dy],
        [pl.BlockSpec((WINDOW, aw), lambda i: (i, q_group_blk)), *_kv_window_specs(kv_blk, kvw, nb),
         *_kv_window_specs(kv_blk + 1, kvw, nb),
         pl.BlockSpec((nh, WINDOW, SPAN), lambda i: (0, i, 0)), pl.BlockSpec((WINDOW, 128), lambda i: (i, 0)),
         pl.BlockSpec((WINDOW, aw), lambda i: (i, dy_blk))], after)
    return _pallas(
        body, name=name, grid=(nb,), in_specs=in_specs,
        out_specs=[pl.BlockSpec((WINDOW, aw), lambda i: (i, 0)), whole(pad_shape), whole(pad_shape),
                   whole(bias_shape), whole((nh, 128))],
        out_shape=[jax.ShapeDtypeStruct((t, aw), F32), jax.ShapeDtypeStruct(pad_shape, F32),
                   jax.ShapeDtypeStruct(pad_shape, F32), jax.ShapeDtypeStruct(bias_shape, F32),
                   jax.ShapeDtypeStruct((nh, 128), F32)],
        compiler_params=_params(("arbitrary",), 3 * _nbytes(pad_shape, F32) + 3 * _nbytes(bias_shape, F32)),
    )(*ins)


def _mix_dproj(name, pieces, kv_pads, t, after=None):
    hw = pieces[0][0].shape[1]
    kvw = kv_pads[0].shape[1]
    widths = [hw] * len(pieces) + [kvw] * len(kv_pads)
    total = sum(widths)
    tm = WINDOW
    flat = [a for pc in pieces for a in pc]

    def body(*refs):
        o_ref = refs[-1]
        pos, off = 0, 0
        for pc in pieces:
            val = refs[pos][...]
            for extra in range(1, len(pc)):
                val = val + refs[pos + extra][...]
            o_ref[:, off:off + hw] = val.astype(BF16)
            pos += len(pc)
            off += hw
        for _ in kv_pads:
            o_ref[:, off:off + kvw] = refs[pos][...].astype(BF16)
            pos += 1
            off += kvw

    in_specs = [pl.BlockSpec((tm, hw), lambda i: (i, 0)) for _ in flat]
    in_specs += [pl.BlockSpec((tm, kvw), lambda i: (i + 1, 0)) for _ in kv_pads]
    body, ins, in_specs = _ordered(body, [*flat, *kv_pads], in_specs, after)
    return _pallas(
        body, name=name, grid=(t // tm,), in_specs=in_specs,
        out_specs=pl.BlockSpec((tm, total), lambda i: (i, 0)),
        out_shape=jax.ShapeDtypeStruct((t, total), BF16),
        compiler_params=_params(("parallel",), 3 * _nbytes((tm, total), F32)),
    )(*ins)


def _concat_cols(name, a, b):
    t, wa = a.shape
    wb = b.shape[1]
    tm = _tile(t, 512, 16)

    def body(a_ref, b_ref, o_ref):
        o_ref[:, :wa] = a_ref[...]
        o_ref[:, wa:] = b_ref[...]

    return _pallas(
        body, name=name, grid=(t // tm,),
        in_specs=[pl.BlockSpec((tm, wa), lambda i: (i, 0)), pl.BlockSpec((tm, wb), lambda i: (i, 0))],
        out_specs=pl.BlockSpec((tm, wa + wb), lambda i: (i, 0)),
        out_shape=jax.ShapeDtypeStruct((t, wa + wb), a.dtype),
        compiler_params=_params(("parallel",), 2 * _nbytes((tm, wa + wb), a.dtype)),
    )(a, b)


def _cast_into_full(name, w, geom, idx, after=None):
    r, c = w.shape
    tr = _tile(r, 256, 16)
    nr = r // tr
    if geom.col:
        place = lambda i, iref: (i, iref[0])
    else:
        place = lambda i, iref: (iref[0] * nr + i, 0)

    def body(i_ref, w_ref, *rest):
        rest[-1][...] = w_ref[...].astype(BF16)

    in_specs = [pl.BlockSpec((tr, c), lambda i, iref: (i, 0))]
    ins = [w]
    if after is not None:
        in_specs.append(pl.BlockSpec(memory_space=pl.ANY))
        ins.append(after)
    return _pallas(
        body, name=name,
        grid_spec=pltpu.PrefetchScalarGridSpec(
            num_scalar_prefetch=1, grid=(nr,), in_specs=in_specs, out_specs=pl.BlockSpec((tr, c), place)),
        out_shape=pltpu.HBM(geom.full_shape, BF16),
        compiler_params=_params(("parallel",), 2 * _nbytes((tr, c), F32)),
    )(idx, *ins)


def _adamw(name, w, g, m, v):
    r, c = w.shape
    tr = _tile(r, 128, 8)
    bc1 = 1.0 - ADAM_B1 ** ADAM_STEP
    bc2 = 1.0 - ADAM_B2 ** ADAM_STEP

    def body(w_ref, g_ref, m_ref, v_ref, go_ref, d_ref, nm_ref, nv_ref):
        gv = g_ref[...]
        go_ref[...] = gv
        nm = ADAM_B1 * m_ref[...] + (1.0 - ADAM_B1) * gv
        nv = ADAM_B2 * v_ref[...] + (1.0 - ADAM_B2) * (gv * gv)
        nm_ref[...] = nm
        nv_ref[...] = nv
        d_ref[...] = -ADAM_LR * ((nm / bc1) / (jnp.sqrt(nv / bc2) + ADAM_EPS) + ADAM_WD * w_ref[...])

    blk = pl.BlockSpec((tr, c), lambda i: (i, 0))
    out = jax.ShapeDtypeStruct((r, c), F32)
    return _pallas(
        body, name=name, grid=(r // tr,), in_specs=[blk] * 4, out_specs=[blk] * 4, out_shape=[out] * 4,
        compiler_params=_params(("parallel",), 8 * _nbytes((tr, c), F32)),
    )(w, g, m, v)


def _mesh_pos():
    return lax.axis_index("x"), lax.axis_index("y"), lax.axis_index("c")


def _other_chips(x, y):
    return [(1 - x, y), (x, 1 - y), (1 - x, 1 - y)]


class _Big:
    def __init__(self, shard_shape, col_sharded):
        self.col = col_sharded
        r, c = shard_shape
        self.shard_shape = (r, c)
        self.full_shape = (r, N_CHIPS * c) if col_sharded else (N_CHIPS * r, c)
        self.half_shape = (r // 2, N_CHIPS * c) if col_sharded else (N_CHIPS * r, c // 2)
        self.shard_half_shape = (r // 2, c) if col_sharded else (r, c // 2)

    def region(self, ref, s, half=None):
        r, c = self.shard_shape
        if self.col:
            rows = slice(None) if half is None else pl.ds(half * (r // 2), r // 2)
            return ref.at[rows, pl.ds(s * c, c)]
        cols = slice(None) if half is None else pl.ds(half * (c // 2), c // 2)
        return ref.at[pl.ds(s * r, r), cols]

    def n_halves(self, ref, half, n):
        r, c = self.shard_shape
        if self.col:
            return ref.at[pl.ds(half * (r // 2), r // 2), pl.ds(0, n * c)]
        return ref.at[pl.ds(0, n * r), pl.ds(half * (c // 2), c // 2)]

    def three_halves(self, ref, half):
        return self.n_halves(ref, half, 3)

    def sub_half(self, ref, s, half, j):
        r, c = self.shard_shape
        if self.col:
            return ref.at[pl.ds(half * (r // 2) + j * (r // 4), r // 4), pl.ds(s * c, c)]
        return ref.at[pl.ds(s * r + j * (r // 2), r // 2), pl.ds(half * (c // 2), c // 2)]

    def half_of_full(self, ref, half):
        r, c = self.full_shape
        if self.col:
            return ref.at[pl.ds(half * (r // 2), r // 2), :]
        return ref.at[:, pl.ds(half * (c // 2), c // 2)]

    def half_of_shard(self, ref, half):
        r, c = self.shard_shape
        if self.col:
            return ref.at[pl.ds(half * (r // 2), r // 2), :]
        return ref.at[:, pl.ds(half * (c // 2), c // 2)]

    def shard_of_half(self, ref, s):
        r, c = self.shard_shape
        if self.col:
            return ref.at[:, pl.ds(s * c, c)]
        return ref.at[pl.ds(s * r, r), :]


HBM =pl.BlockSpec(memory_space=pltpu.HBM)
SEM = pl.BlockSpec(memory_space=pltpu.SEMAPHORE)
SPLIT_COPY = pltpu.CompilerParams(has_side_effects=pltpu.SideEffectType.DATAFLOW_SIDE_EFFECTING)


def _in_hbm(a):
    return pltpu.with_memory_space_constraint(a, pltpu.HBM)


def _gather_start(name, fulls, geoms, after):
    nw = len(fulls)

    def body(*refs):
        dst = refs[nw + 1:2 * nw + 1]
        sems = refs[2 * nw + 1:-1]
        x, y, c = _mesh_pos()
        mine = 2 * x + y
        for w in range(nw):
            own_half = geoms[w].region(dst[w], mine, c)
            for chip in _other_chips(x, y):
                pltpu.make_async_remote_copy(src_ref=own_half, dst_ref=own_half, send_sem=sems[2 * w],
                                             recv_sem=sems[2 * w + 1], device_id=(*chip, c),
                                             device_id_type=MESH).start()
        refs[-1][...] = jnp.zeros_like(refs[-1])

    out = _pallas(
        body, name=name, in_specs=[HBM] * nw + [pl.BlockSpec(memory_space=pl.ANY)],
        out_specs=[HBM] * nw + [SEM] * (2 * nw) + [pl.BlockSpec(memory_space=pltpu.VMEM)],
        out_shape=[pltpu.HBM(g.full_shape, BF16) for g in geoms] + [pltpu.SemaphoreType.DMA(())] * (2 * nw)
        + [jax.ShapeDtypeStruct((8, 128), F32)],
        input_output_aliases={w: w for w in range(nw)}, compiler_params=SPLIT_COPY,
    )(*[_in_hbm(a) for a in fulls], after)
    return list(out[:nw]), [(out[nw + 2 * w], out[nw + 2 * w + 1]) for w in range(nw)], out[-1]


def _gather_first_direct(full, geom):
    def start(refs, _, new):
        x, y, c = _mesh_pos()
        own = geom.region(refs[0], 2 * x + y, c)
        for chip in ((1 - x, y), (x, 1 - y)):
            _remote(own, own, new, (*chip, c)).start()

    return _split_copy_call("gather_first_direct", [full], start, new_sems=2)


def _gather_first_relay(full, geom, sems, after):
    def relay(refs, got, new):
        x, y, c = _mesh_pos()
        w = refs[0]
        two = geom.n_halves(w, c, 2)
        _remote(two, two, got, (x, y, 1 - c)).wait_recv()
        from_x = geom.sub_half(w, 2 * (1 - x) + y, c, 0)
        from_y = geom.sub_half(w, 2 * x + (1 - y), c, 1)
        _remote(from_x, from_x, new, (x, 1 - y, c)).start()
        _remote(from_y, from_y, new, (1 - x, y, c)).start()
        _remote(two, two, got, (x, y, 1 - c)).wait_send()

    return _split_copy_call("gather_first_relay", [full], relay, sems=sems, after=after, new_sems=2)


def _gather_forward(name, full, geom, sems, after, arrivals=3, only_diagonal=False):
    def body(w_in, send_sem, recv_sem, after_ref, w_ref, fwd_send, fwd_recv):
        x, y, c = _mesh_pos()
        sibling = (x, y, 1 - c)
        landed_all = geom.n_halves(w_ref, c, arrivals)
        _remote(landed_all, landed_all, (send_sem, recv_sem), sibling).wait_recv()
        for chip in _other_chips(x, y)[2 if only_diagonal else 0:]:
            landed = geom.region(w_ref, 2 * chip[0] + chip[1], c)
            pltpu.make_async_remote_copy(src_ref=landed, dst_ref=landed, send_sem=fwd_send, recv_sem=fwd_recv,
                                         device_id=sibling, device_id_type=MESH).start()
        _remote(landed_all, landed_all, (send_sem, recv_sem), sibling).wait_send()

    sem = pltpu.SemaphoreType.DMA(())
    out = _pallas(
        body, name=name, in_specs=[HBM, SEM, SEM, pl.BlockSpec(memory_space=pl.ANY)], out_specs=[HBM, SEM, SEM],
        out_shape=[pltpu.HBM(geom.full_shape, BF16), sem, sem],
        input_output_aliases={0: 0}, compiler_params=SPLIT_COPY,
    )(full, sems[0], sems[1], after)
    return out[0], (out[1], out[2])


def _gather_end(name, full, geom, sems, after, halves=3):
    def body(w_in, fwd_send, fwd_recv, after_ref, w_ref):
        x, y, c = _mesh_pos()
        sibling = (x, y, 1 - c)
        theirs, ours = geom.n_halves(w_ref, 1 - c, halves), geom.n_halves(w_ref, c, halves)
        _remote(theirs, theirs, (fwd_send, fwd_recv), sibling).wait_recv()
        _remote(ours, ours, (fwd_send, fwd_recv), sibling).wait_send()

    return _pallas(
        body, name=name, in_specs=[HBM, SEM, SEM, pl.BlockSpec(memory_space=pl.ANY)], out_specs=HBM,
        out_shape=pltpu.HBM(geom.full_shape, BF16),
        input_output_aliases={0: 0}, compiler_params=SPLIT_COPY,
    )(full, sems[0], sems[1], after)


def _split_copy_call(name, arrays, fn, sems=(), after=None, new_sems=0):
    n, ns = len(arrays), len(sems)
    n_in = n + ns + (after is not None)

    def body(*refs):
        fn(refs[n_in:n_in + n], refs[n:n + ns], refs[n_in + n:-1])
        refs[-1][...] = jnp.zeros_like(refs[-1])

    ins = list(arrays) if ns else [_in_hbm(a) for a in arrays]
    ins += list(sems) + ([after] if after is not None else [])
    in_specs = [HBM] * n + [SEM] * ns + ([pl.BlockSpec(memory_space=pl.ANY)] if after is not None else [])
    out = _pallas(
        body, name=name, in_specs=in_specs,
        out_specs=[HBM] * n + [SEM] * new_sems + [pl.BlockSpec(memory_space=pltpu.VMEM)],
        out_shape=[pltpu.HBM(a.shape, a.dtype) for a in arrays] + [pltpu.SemaphoreType.DMA(())] * new_sems
        + [jax.ShapeDtypeStruct((8, 128), F32)],
        input_output_aliases={i: i for i in range(n)}, compiler_params=SPLIT_COPY,
    )(*ins)
    return list(out[:n]), tuple(out[n:-1]), out[-1]


def _remote(src, dst, sems, to):
    return pltpu.make_async_remote_copy(src_ref=src, dst_ref=dst, send_sem=sems[0], recv_sem=sems[1],
                                        device_id=to, device_id_type=MESH)


class _GradReduce:
    def __init__(self, name, geom, idx, c_idx):
        self.name, self.geom, self.idx, self.c_idx = name, geom, idx, c_idx

    def pair_start(self, theirs):
        g = self.geom

        def start(refs, _, new):
            x, y, c = _mesh_pos()
            _remote(refs[0], refs[1], new, (x, y, 1 - c)).start()

        self.arrays, self.sems, token = _split_copy_call(
            f"pair_start_{self.name}", [theirs, lax.empty(g.half_shape, BF16)], start, new_sems=2)
        return token

    def pair_wait(self, after):
        def wait(refs, sems, _):
            x, y, c = _mesh_pos()
            copy = _remote(refs[0], refs[1], sems, (x, y, 1 - c))
            copy.wait_send()
            copy.wait_recv()

        (_, landed), _, _ = _split_copy_call(f"pair_wait_{self.name}", self.arrays, wait, self.sems, after)
        return landed

    def chip_start(self, half):
        g = self.geom

        def start(refs, _, new):
            x, y, c = _mesh_pos()
            for k, chip in enumerate(_other_chips(x, y)):
                _remote(g.shard_of_half(refs[0], 2 * chip[0] + chip[1]), refs[1].at[k], new, (*chip, c)).start()

        self.arrays, self.sems, token = _split_copy_call(
            f"chip_start_{self.name}", [half, lax.empty((3,) + g.shard_half_shape, BF16)], start, new_sems=2)
        return token

    def chip_finish(self, after):
        g = self.geom

        def wait(refs, sems, _):
            x, y, c = _mesh_pos()
            three = _remote(refs[1], refs[1], sems, (x, y, 1 - c))
            three.wait_send()
            three.wait_recv()

        (half, landed), _, _ = _split_copy_call(f"chip_wait_{self.name}", self.arrays, wait, self.sems, after)
        quarter = _chip_add(f"chip_add_{self.name}", half, landed, g, self.idx)

        def start(refs, _, new):
            x, y, c = _mesh_pos()
            own = g.half_of_shard(refs[0], c)
            _remote(own, own, new, (x, y, 1 - c)).start()

        self.arrays, self.sems, token = _split_copy_call(f"share_start_{self.name}", [quarter], start, new_sems=2)
        return token

    def finish(self, after):
        g = self.geom

        def wait(refs, sems, _):
            x, y, c = _mesh_pos()
            own, theirs = g.half_of_shard(refs[0], c), g.half_of_shard(refs[0], 1 - c)
            _remote(own, own, sems, (x, y, 1 - c)).wait_send()
            _remote(theirs, theirs, sems, (x, y, 1 - c)).wait_recv()

        (quarter,), _, _ = _split_copy_call(f"share_wait_{self.name}", self.arrays, wait, self.sems, after)
        return quarter


def _dw_half(name, x, dy, geom, c_idx, own, addend=None, after=None):
    stacked = dy.ndim == 3
    t, m = x.shape
    n = 2 * dy.shape[2] if stacked else dy.shape[1]
    hm, hn = (m // 2, n) if geom.col else (m, n // 2)
    tm, tn, tk = _mm_tiles(hm, hn, t, BF16, n_unit=(n // 2 if stacked else None))
    if tk != t:
        tm, tn = _tile(hm, 512, 128), _tile(hn // (2 if stacked else 1), 512, 128)
    gi, gj = hm // tm, hn // tn
    nf = (n // 2) // tn

    def sel(cref):
        return cref[0] if own else 1 - cref[0]

    a_map = (lambda i, j, cref: (0, sel(cref) * gi + i)) if geom.col else (lambda i, j, cref: (0, i))
    if stacked:
        b_blk, b_map = (None, t, tn), (lambda i, j, cref: (j // nf, 0, j % nf))
    elif geom.col:
        b_blk, b_map = (t, tn), (lambda i, j, cref: (0, j))
    else:
        b_blk, b_map = (t, tn), (lambda i, j, cref: (0, sel(cref) * gj + j))
    out_blk = pl.BlockSpec((tm, tn), lambda i, j, cref: (i, j))
    ins, in_specs = [x, dy], [pl.BlockSpec((t, tm), a_map), pl.BlockSpec(b_blk, b_map)]
    if addend is not None:
        ins.append(addend)
        in_specs.append(out_blk)
    if after is not None:
        ins.append(after)
        in_specs.append(pl.BlockSpec(memory_space=pl.ANY))

    def body(c_ref, *refs):
        acc = _dot(refs[0][...], refs[1][...], 0, 0)
        if addend is not None:
            acc = acc + refs[2][...].astype(F32)
        refs[len(ins)][...] = acc.astype(BF16)

    return _pallas(
        body, name=name,
        grid_spec=pltpu.PrefetchScalarGridSpec(num_scalar_prefetch=1, grid=(gi, gj), in_specs=in_specs,
                                               out_specs=out_blk),
        out_shape=jax.ShapeDtypeStruct((hm, hn), BF16),
        compiler_params=_params(("parallel", "parallel"),
                                _nbytes((t, tm), BF16) + _nbytes((t, tn), BF16) + 3 * _nbytes((tm, tn), F32)),
    )(c_idx, *ins)


def _chip_add(name, half, recv, geom, idx):
    r, c = geom.shard_half_shape
    tr, tc = _tile(r, 512, 16), _tile(c, 2048, 128)
    nr, ncol = r // tr, c // tc
    if geom.col:
        mine = lambda i, j, iref: (i, iref[0] * ncol + j)
        place = lambda i, j, iref: (iref[1] * nr + i, j)
    else:
        mine = lambda i, j, iref: (iref[0] * nr + i, j)
        place = lambda i, j, iref: (i, iref[1] * ncol + j)

    def body(i_ref, h_ref, r_ref, o_ref):
        acc = h_ref[...].astype(F32)
        for k in range(3):
            acc = acc + r_ref[k].astype(F32)
        o_ref[...] = acc

    return _pallas(
        body, name=name,
        grid_spec=pltpu.PrefetchScalarGridSpec(
            num_scalar_prefetch=1, grid=(nr, ncol),
            in_specs=[pl.BlockSpec((tr, tc), mine), pl.BlockSpec((3, tr, tc), lambda i, j, iref: (0, i, j))],
            out_specs=pl.BlockSpec((tr, tc), place)),
        out_shape=jax.ShapeDtypeStruct(geom.shard_shape, F32),
        compiler_params=_params(("parallel", "parallel"), 4 * _nbytes((tr, tc), F32)),
    )(idx, half, recv)


def _all_reduce_small(pack, after=None):
    r, d = pack.shape

    def body(p_ref, o_ref, slots, send_sems, recv_sems):
        x, y, c = _mesh_pos()
        me = 4 * x + 2 * y + c
        slots[me] = p_ref[...]
        copies = []
        for k in range(1, N_DEV):
            px, py, pc = x ^ ((k >> 2) & 1), y ^ ((k >> 1) & 1), c ^ (k & 1)
            copies.append(pltpu.make_async_remote_copy(
                src_ref=p_ref, dst_ref=slots.at[me], send_sem=send_sems.at[k - 1], recv_sem=recv_sems.at[k - 1],
                device_id=(px, py, pc), device_id_type=MESH))
        for cp in copies:
            cp.start()
        for k in range(1, N_DEV):
            peer = 4 * (x ^ ((k >> 2) & 1)) + 2 * (y ^ ((k >> 1) & 1)) + (c ^ (k & 1))
            pltpu.make_async_remote_copy(
                src_ref=p_ref, dst_ref=slots.at[peer], send_sem=send_sems.at[k - 1], recv_sem=recv_sems.at[k - 1],
                device_id=(x, y, c), device_id_type=MESH).wait_recv()
        for cp in copies:
            cp.wait_send()
        acc = slots[0]
        for k in range(1, N_DEV):
            acc = acc + slots[k]
        o_ref[...] = acc

    vm = pl.BlockSpec(memory_space=pltpu.VMEM)
    body, ins, in_specs = _ordered(body, [pack], [vm], after)
    return _pallas(
        body, name="all_reduce_small", in_specs=in_specs, out_specs=vm,
        out_shape=jax.ShapeDtypeStruct((r, d), F32),
        scratch_shapes=[pltpu.VMEM((N_DEV, r, d), F32), pltpu.SemaphoreType.DMA((N_DEV - 1,)),
                        pltpu.SemaphoreType.DMA((N_DEV - 1,))],
    )(*ins)


def _pack_rows(rows, d):
    out = []
    for a in rows:
        flat = a.reshape(-1)
        n = -(-flat.shape[0] // d) * d
        out.append(jnp.pad(flat, (0, n - flat.shape[0])).reshape(-1, d))
    packed = jnp.concatenate(out, axis=0)
    return jnp.pad(packed, ((0, 16 - packed.shape[0]), (0, 0)))


def _unpack_rows(packed, shapes, d):
    out, row = [], 0
    for shp in shapes:
        n = int(np.prod(shp))
        nrows = -(-n // d)
        out.append(packed[row:row + nrows].reshape(-1)[:n].reshape(shp))
        row += nrows
    return out


def kernel(x, pre_norm_ffn1, post_norm_ffn1, w_ffn1_gate_up, w_ffn1_down, pre_norm_mix, post_norm_mix, w_mix_in, hgrn_lower_bounds_fwd, hgrn_lower_bounds_bwd, hgrn_out_norm, attn_sink, w_mix_out, pre_norm_ffn2, post_norm_ffn2, w_ffn2_gate_up, w_ffn2_down, rel_bias_table, loss_target, m_pre_norm_ffn1, m_post_norm_ffn1, m_w_ffn1_gate_up, m_w_ffn1_down, m_pre_norm_mix, m_post_norm_mix, m_w_mix_in, m_hgrn_lower_bounds_fwd, m_hgrn_lower_bounds_bwd, m_hgrn_out_norm, m_attn_sink, m_w_mix_out, m_pre_norm_ffn2, m_post_norm_ffn2, m_w_ffn2_gate_up, m_w_ffn2_down, m_rel_bias_table, v_pre_norm_ffn1, v_post_norm_ffn1, v_w_ffn1_gate_up, v_w_ffn1_down, v_pre_norm_mix, v_post_norm_mix, v_w_mix_in, v_hgrn_lower_bounds_fwd, v_hgrn_lower_bounds_bwd, v_hgrn_out_norm, v_attn_sink, v_w_mix_out, v_pre_norm_ffn2, v_post_norm_ffn2, v_w_ffn2_gate_up, v_w_ffn2_down, v_rel_bias_table):
    t, d = x.shape[1], x.shape[2]
    hw = hgrn_out_norm.shape[1]
    aw = d - hw
    nah = aw // HEAD
    kvw = KV_HEADS * HEAD
    x0 = x[0]
    target = loss_target[0]

    big_names = ["w_ffn1_gate_up", "w_ffn1_down", "w_mix_in", "w_mix_out", "w_ffn2_gate_up", "w_ffn2_down"]
    big_w = [w_ffn1_gate_up[0], w_ffn1_down[0], w_mix_in[0], w_mix_out[0], w_ffn2_gate_up[0], w_ffn2_down[0]]
    big_m = [m_w_ffn1_gate_up[0], m_w_ffn1_down[0], m_w_mix_in[0], m_w_mix_out[0], m_w_ffn2_gate_up[0],
             m_w_ffn2_down[0]]
    big_v = [v_w_ffn1_gate_up[0], v_w_ffn1_down[0], v_w_mix_in[0], v_w_mix_out[0], v_w_ffn2_gate_up[0],
             v_w_ffn2_down[0]]
    col_sharded = [True, False, True, False, True, False]
    geoms = [_Big(w.shape, cs) for w, cs in zip(big_w, col_sharded)]

    cx, cy, cc = _mesh_pos()
    idx = jnp.stack([2 * cx + cy, cc]).astype(jnp.int32)
    c_idx = jnp.reshape(cc, (1,)).astype(jnp.int32)
    first = _cast_into_full(f"cast_{big_names[0]}", big_w[0], geoms[0], idx)
    (first,), direct_sems, tok = _gather_first_direct(first, geoms[0])
    rest = []
    for n, w, gm in zip(big_names[1:], big_w[1:], geoms[1:]):
        tok = _cast_into_full(f"cast_{n}", w, gm, idx, after=tok)
        rest.append(tok)
    (first,), relay_sems, tok = _gather_first_relay(first, geoms[0], direct_sems, after=tok)
    started_rest, sems_rest, rest_started = _gather_start("gather_start_rest", rest, geoms[1:], after=tok)
    started, gather_sems = [first] + started_rest, [relay_sems] + sems_rest

    def forward_weight(w, after):
        return _gather_forward(f"gather_forward_{big_names[w]}", started[w], geoms[w], gather_sems[w], after,
                               arrivals=1 if w == 0 else 3)

    def whole_weight(w, forwarded, after):
        return _gather_end(f"gather_end_{big_names[w]}", forwarded[0], geoms[w], forwarded[1], after)

    h1 = _norm_fwd("ffn1_pre_norm", x0, pre_norm_ffn1)
    w_gu1 = whole_weight(0, forward_weight(0, rest_started), h1)
    act1, dact_dgate1, dact_dup1 = _ffn_gate_up_act("ffn1_gate_up", h1, w_gu1)
    w_d1 = whole_weight(1, forward_weight(1, act1), act1)
    ff1 = _mm("ffn1_down", act1, w_d1, "nn", F32)
    fw = forward_weight(2, ff1)
    x1, hm = _resid_norm_fwd("ffn1_residual", x0, ff1, post_norm_ffn1, pre_norm_mix, 0.5)
    w_in = whole_weight(2, fw, hm)
    p = _mm("mix_in", hm, w_in, "nn", F32)
    fw = forward_weight(3, p)
    o_f, o_b, st_f, st_b = _hgrn_scan_fwd("hgrn_scan", p, hgrn_lower_bounds_fwd, hgrn_lower_bounds_bwd)
    y_h = _hgrn_out_fwd("hgrn_out", o_f, o_b, p, hgrn_out_norm, 4)
    kv_blk0 = (5 * hw + aw) // kvw
    bucket_ids = _t5_bucket_ids()
    bias = _bias_gather("attn_bias", rel_bias_table.T, bucket_ids).reshape(nah, WINDOW, SPAN)
    y_a, attn_probs, attn_sink_probs = _attn_fwd("attn_fwd", p, kv_blk0, kvw, bias, attn_sink, 5 * hw // aw)
    y_mix = _concat_cols("mix_concat", y_h, y_a)
    w_out = whole_weight(3, fw, y_mix)
    mixed = _mm("mix_out", y_mix, w_out, "nn", F32)
    fw = forward_weight(4, mixed)
    x2, h2 = _resid_norm_fwd("mix_residual", x1, mixed, post_norm_mix, pre_norm_ffn2, 1.0)
    w_gu2 = whole_weight(4, fw, h2)
    act2, dact_dgate2, dact_dup2 = _ffn_gate_up_act("ffn2_gate_up", h2, w_gu2)
    w_d2 = whole_weight(5, forward_weight(5, act2), act2)
    ff2 = _mm("ffn2_down", act2, w_d2, "nn", F32)
    loss_blk, dy, dff2, dg_post2 = _final_fwd_bwd("ffn2_residual_loss", x2, ff2, post_norm_ffn2, target, 0.5)

    reduce = [_GradReduce(n, gm, idx, c_idx) for n, gm in zip(big_names, geoms)]
    big_grads, big_delta, big_new_m, big_new_v = [None] * 6, [None] * 6, [None] * 6, [None] * 6

    def update(w, after):
        g, dl, nm, nv = _adamw(f"adamw_{big_names[w]}", big_w[w], reduce[w].finish(after), big_m[w], big_v[w])
        big_grads[w], big_delta[w], big_new_m[w], big_new_v[w] = g[None], dl[None], nm[None], nv[None]
        return dl

    def dw_start(w, x_act, dy_act, after=None):
        theirs = _dw_half(f"dw_theirs_{big_names[w]}", x_act, dy_act, geoms[w], c_idx, own=False, after=after)
        return reduce[w].pair_start(theirs)

    def dw_finish(w, x_act, dy_act, after):
        landed = reduce[w].pair_wait(after)
        half = _dw_half(f"dw_own_{big_names[w]}", x_act, dy_act, geoms[w], c_idx, own=True, addend=landed)
        return reduce[w].chip_start(half)

    tok = dw_start(5, act2, dff2)
    dgu2 = _ffn_dact("ffn2_dact", dff2, w_d2, dact_dgate2, dact_dup2, after=tok)
    tok = dw_finish(5, act2, dff2, after=dgu2)
    tok = dw_start(4, h2, dgu2, after=tok)
    dh2 = _ffn_dh("ffn2_dh", dgu2, w_gu2, after=tok)
    tok = dw_finish(4, h2, dgu2, after=dh2)
    dx2, dg_pre2, dmixed, dg_postm = _norms_bwd("mix_residual_bwd", dy, dh2, x2, pre_norm_ffn2,
                                                post=(mixed, post_norm_mix, 1.0), after=tok)
    tok = dw_start(3, y_mix, dmixed)
    dy_mix = _mm("mix_out_dx", dmixed, w_out, "nt", F32, after=tok)
    tok = dw_finish(3, y_mix, dmixed, after=dy_mix)
    dq_a, dk_pad, dv_pad, dbias, dsink = _attn_bwd("attn_bwd", p, kv_blk0, kvw, attn_probs, attn_sink_probs, dy_mix,
                                                   5 * hw // aw, hw // aw, after=tok)
    tok = reduce[5].chip_finish(dq_a)
    drel_t = _bias_scatter("attn_dbias", dbias.reshape(nah, WINDOW * SPAN), bucket_ids)
    do, dg_h, dgain = _hgrn_out_bwd("hgrn_out_bwd", dy_mix, o_f, o_b, p, hgrn_out_norm, 4, after=tok)
    dq_f, dv_f, dz_f, dlb_f, dq_b, dv_b, dz_b, dlb_b = _hgrn_scan_bwd(
        "hgrn_scan_bwd", p, hgrn_lower_bounds_fwd, hgrn_lower_bounds_bwd, do, st_f, st_b)
    tok = reduce[4].chip_finish(dq_f)
    tok = reduce[3].chip_finish(tok)
    dp = _mix_dproj("mix_dproj", [(dq_f, dq_b), (dv_f, dv_b), (dz_f,), (dz_b,), (dg_h,), (dq_a,)],
                    [dk_pad, dv_pad], t, after=tok)
    tok = dw_start(2, hm, dp)
    dhm = _mm("mix_in_dx", dp, w_in, "nt", F32, after=tok)
    tok = dw_finish(2, hm, dp, after=dhm)
    dx1, dg_prem, dff1, dg_post1 = _norms_bwd("ffn1_residual_bwd", dx2, dhm, x1, pre_norm_mix,
                                              post=(ff1, post_norm_ffn1, 0.5), after=tok)
    tok = dw_start(1, act1, dff1)
    dgu1 = _ffn_dact("ffn1_dact", dff1, w_d1, dact_dgate1, dact_dup1, after=tok)
    tok = dw_finish(1, act1, dff1, after=dgu1)
    tok = reduce[2].chip_finish(tok)
    tok = dw_start(0, h1, dgu1, after=tok)
    done = update(2, tok)
    tok = dw_finish(0, h1, dgu1, after=done)
    dh1 = _ffn_dh("ffn1_dh", dgu1, w_gu1, after=tok)
    grad_x, dg_pre1 = _norms_bwd("ffn1_pre_norm_bwd", dx1, dh1, x0, pre_norm_ffn1)

    small_w = [pre_norm_ffn1, post_norm_ffn1, pre_norm_mix, post_norm_mix, hgrn_lower_bounds_fwd,
               hgrn_lower_bounds_bwd, hgrn_out_norm, attn_sink, pre_norm_ffn2, post_norm_ffn2, rel_bias_table]
    small_m = [m_pre_norm_ffn1, m_post_norm_ffn1, m_pre_norm_mix, m_post_norm_mix, m_hgrn_lower_bounds_fwd,
               m_hgrn_lower_bounds_bwd, m_hgrn_out_norm, m_attn_sink, m_pre_norm_ffn2, m_post_norm_ffn2,
               m_rel_bias_table]
    small_v = [v_pre_norm_ffn1, v_post_norm_ffn1, v_pre_norm_mix, v_post_norm_mix, v_hgrn_lower_bounds_fwd,
               v_hgrn_lower_bounds_bwd, v_hgrn_out_norm, v_attn_sink, v_pre_norm_ffn2, v_post_norm_ffn2,
               v_rel_bias_table]
    small_g = [dg_pre1, dg_post1, dg_prem, dg_postm, dlb_f, dlb_b, dgain, dsink[:, 0].reshape(1, nah), dg_pre2,
               dg_post2, drel_t.T]
    shapes = [a.shape for a in small_w]
    done = update(5, grad_x)
    done = update(4, done)
    done = update(3, done)
    summed = _all_reduce_small(_pack_rows(small_g + [loss_blk[0:1, 0:1]], d), after=done)
    loss = _unpack_rows(summed, shapes + [(1, 1)], d)[-1][0, 0]
    _, sd, sm, sv = _adamw("adamw_small", _pack_rows(small_w, d), summed, _pack_rows(small_m, d),
                           _pack_rows(small_v, d))
    small_grads = _unpack_rows(summed, shapes, d)
    small_delta, small_new_m, small_new_v = (_unpack_rows(a, shapes, d) for a in (sd, sm, sv))

    tok = reduce[1].chip_finish(sd)
    tok = reduce[0].chip_finish(tok)
    done = update(1, tok)
    update(0, done)

    def ordered(small, big):
        s = dict(zip(["pre1", "post1", "prem", "postm", "lbf", "lbb", "gain", "sink", "pre2", "post2", "rel"], small))
        b = dict(zip(["gu1", "d1", "win", "wout", "gu2", "d2"], big))
        return [s["pre1"], s["post1"], b["gu1"], b["d1"], s["prem"], s["postm"], b["win"], s["lbf"], s["lbb"],
                s["gain"], s["sink"], b["wout"], s["pre2"], s["post2"], b["gu2"], b["d2"], s["rel"]]

    return (loss, grad_x[None], *ordered(small_grads, big_grads), *ordered(small_delta, big_delta),
            *ordered(small_new_m, big_new_m), *ordered(small_new_v, big_new_v))
```

```python
import functools
import math

import jax
import jax.numpy as jnp
import numpy as np
from jax import lax
from jax.experimental import pallas as pl
from jax.experimental.pallas import tpu as pltpu

F32 = jnp.float32
BF16 = jnp.bfloat16

HEAD = 128
CHUNK = 64
WINDOW = 128
SPAN = 3 * WINDOW
KV_HEADS = 2
REL_BUCKETS = 32
REL_MAX_DIST = 128
EPS = 1e-6
NEG_INF = -1e30

ADAM_LR = 0.001
ADAM_B1 = 0.9
ADAM_B2 = 0.999
ADAM_EPS = 1e-08
ADAM_WD = 0.01
ADAM_STEP = 10

N_CHIPS = 4
N_DEV = 8
V7X_VMEM_BYTES = 64 * 1024 * 1024
MESH = pl.DeviceIdType.MESH
ANY = pl.BlockSpec(memory_space=pl.ANY)


def _tile(n, pref, mult):
    t = (min(pref, n) // mult) * mult
    while t >= mult:
        if n % t == 0:
            return t
        t -= mult
    return n


def _params(semantics, block_bytes):
    limit = min(V7X_VMEM_BYTES - (4 << 20), 2 * int(block_bytes) + (8 << 20))
    return pltpu.CompilerParams(dimension_semantics=semantics, vmem_limit_bytes=limit)


def _nbytes(shape, dtype):
    return int(np.prod(shape)) * jnp.dtype(dtype).itemsize


PIN_TO_HBM_BYTES = 4 << 20


def _pallas(body, **kw):
    def pin_shape(s):
        if isinstance(s, jax.ShapeDtypeStruct) and _nbytes(s.shape, s.dtype) >= PIN_TO_HBM_BYTES:
            return pltpu.HBM(s.shape, s.dtype)
        return s

    def pin(a):
        if getattr(a, "dtype", None) in (F32, BF16) and _nbytes(a.shape, a.dtype) >= PIN_TO_HBM_BYTES:
            return pltpu.with_memory_space_constraint(a, pltpu.HBM)
        return a

    out_shape = kw["out_shape"]
    kw["out_shape"] = [pin_shape(s) for s in out_shape] if isinstance(out_shape, (list, tuple)) else pin_shape(out_shape)
    call = pl.pallas_call(body, **kw)
    return lambda *args: call(*[pin(a) for a in args])


def _dot(a, b, ca=1, cb=0):
    return lax.dot_general(a, b, (((ca,), (cb,)), ((), ())), preferred_element_type=F32)


def _split3(x):
    hi = x.astype(BF16)
    r1 = x - hi.astype(F32)
    mid = r1.astype(BF16)
    lo = (r1 - mid.astype(F32)).astype(BF16)
    return hi, mid, lo


def _dot_exact(a, b, ca=1, cb=0, split="b"):
    if split == "b":
        return sum(_dot(a, p, ca, cb) for p in _split3(b))
    return sum(_dot(p, b, ca, cb) for p in _split3(a))


def _rms(x):
    return lax.rsqrt(jnp.mean(x * x, axis=-1, keepdims=True) + EPS)


def _norm_bwd(u, x, gain):
    r = _rms(x)
    xhat = x * r
    dgain = jnp.sum(u * xhat, axis=0, keepdims=True)
    v = u * gain
    dx = r * (v - xhat * jnp.mean(v * xhat, axis=-1, keepdims=True))
    return dx, dgain


def _sigmoid(x):
    return 1.0 / (1.0 + jnp.exp(-x))


def _accumulate(ref, val, first):
    @pl.when(first)
    def _():
        ref[...] = val

    @pl.when(jnp.logical_not(first))
    def _():
        ref[...] += val


def _ordered(body, ins, in_specs, after):
    if after is None:
        return body, list(ins), list(in_specs)
    n_in = len(ins)

    def wrapped(*refs):
        body(*refs[:n_in], *refs[n_in + 1:])

    return wrapped, list(ins) + [after], list(in_specs) + [pl.BlockSpec(memory_space=pl.ANY)]


def _matmul(name, a, b, *, form, out_dtype, tm, tn, tk, a_map=None, b_map=None,
            out_shape=None, out_block=None, out_map=None, sizes=None, after=None):
    if sizes is None:
        if form == "nn":
            (m, k), n = a.shape, b.shape[1]
        elif form == "nt":
            (m, k), n = a.shape, b.shape[0]
        else:
            (k, m), n = a.shape, b.shape[1]
    else:
        m, n, k = sizes
    gi, gj, gk = m // tm, n // tn, k // tk
    a_blk = (tm, tk) if form != "tn" else (tk, tm)
    b_blk = (tk, tn) if form != "nt" else (tn, tk)
    if a_map is None:
        a_map = (lambda i, j, kk: (i, kk)) if form != "tn" else (lambda i, j, kk: (kk, i))
    else:
        a_blk = (None,) + a_blk
    if b_map is None:
        b_map = (lambda i, j, kk: (kk, j)) if form != "nt" else (lambda i, j, kk: (j, kk))
    else:
        b_blk = (None,) + b_blk
    if out_shape is None:
        out_shape, out_block, out_map = (m, n), (tm, tn), (lambda i, j, kk: (i, j))
    ca, cb = {"nn": (1, 0), "nt": (1, 1), "tn": (0, 0)}[form]

    def body(a_ref, b_ref, o_ref, *acc):
        part = _dot(a_ref[...], b_ref[...], ca, cb)
        if gk == 1:
            o_ref[...] = part.astype(o_ref.dtype)
        else:
            kk = pl.program_id(2)
            _accumulate(acc[0], part, kk == 0)

            @pl.when(kk == gk - 1)
            def _():
                o_ref[...] = acc[0][...].astype(o_ref.dtype)

    scratch = [] if gk == 1 else [pltpu.VMEM((tm, tn), F32)]
    vmem = (_nbytes((tm, tk), a.dtype) + _nbytes((tk, tn), b.dtype) + _nbytes((tm, tn), out_dtype)
            + 2 * _nbytes((tm, tn), F32))
    body, ins, in_specs = _ordered(body, [a, b], [pl.BlockSpec(a_blk, a_map), pl.BlockSpec(b_blk, b_map)], after)
    return _pallas(
        body, name=name, grid=(gi, gj, gk), in_specs=in_specs,
        out_specs=pl.BlockSpec(out_block, out_map),
        out_shape=jax.ShapeDtypeStruct(out_shape, out_dtype),
        scratch_shapes=scratch,
        compiler_params=_params(("parallel", "parallel", "arbitrary"), vmem),
    )(*ins)


V7X_HBM_BYTES_PER_US = 3.0e6
V7X_MXU_FLOPS_PER_US = 0.9e9
V7X_VMEM_RMW_BYTES_PER_US = 10e6
GRID_STEP_US = 0.35
MATMUL_VMEM_BUDGET = 40 << 20
MATMUL_MAX_TILE_FLOPS = 1 << 33


def _divisors(n, mult, lo):
    return [t for t in range(mult, n + 1, mult) if n % t == 0 and t >= min(lo, n)]


def _mm_tiles(m, n, k, out_dtype=F32, n_unit=None, k_unit=None):
    out_bytes = jnp.dtype(out_dtype).itemsize
    best = None
    for tm in _divisors(m, 128, 256):
        for tn in _divisors(n_unit or n, 128, 256):
            for tk in _divisors(k_unit or k, 128, 512):
                gi, gj, gk = m // tm, n // tn, k // tk
                vmem = 4 * tm * tk + 4 * tk * tn + 2 * tm * tn * out_bytes + 4 * tm * tn * (2 if gk > 1 else 1)
                if vmem > MATMUL_VMEM_BUDGET or 2 * tm * tn * tk > MATMUL_MAX_TILE_FLOPS:
                    continue
                a_bytes = 2 * m * k * (gj if gk > 1 else 1)
                b_bytes = 2 * k * n * (1 if gj == 1 and gk == 1 else gi)
                hbm_us = (a_bytes + b_bytes + m * n * out_bytes) / V7X_HBM_BYTES_PER_US
                acc_us = (8 * m * n * gk / V7X_VMEM_RMW_BYTES_PER_US) if gk > 1 else 0.0
                cost = max(2 * m * n * k / V7X_MXU_FLOPS_PER_US, 1.3 * hbm_us) + GRID_STEP_US * gi * gj * gk + acc_us
                key = (round(cost, 1), vmem)
                if best is None or key < best[0]:
                    best = (key, (tm, tn, tk))
    return best[1]


def _mm(name, a, b, form, out_dtype, after=None):
    if form == "nn":
        m, k, n = a.shape[0], a.shape[1], b.shape[1]
    elif form == "nt":
        m, k, n = a.shape[0], a.shape[1], b.shape[0]
    else:
        m, k, n = a.shape[1], a.shape[0], b.shape[1]
    tm, tn, tk = _mm_tiles(m, n, k, out_dtype)
    return _matmul(name, a, b, form=form, out_dtype=out_dtype, tm=tm, tn=tn, tk=tk, after=after)


def _row_tile(t):
    return _tile(t, 256, 8)


def _norm_fwd(name, x, gain):
    t, d = x.shape
    tm = _row_tile(t)

    def body(x_ref, g_ref, h_ref):
        xv = x_ref[...]
        h_ref[...] = (xv * _rms(xv) * g_ref[...]).astype(BF16)

    row = pl.BlockSpec((tm, d), lambda i: (i, 0))
    vec = pl.BlockSpec((1, d), lambda i: (0, 0))
    return _pallas(
        body, name=name, grid=(t // tm,), in_specs=[row, vec], out_specs=row,
        out_shape=jax.ShapeDtypeStruct((t, d), BF16),
        compiler_params=_params(("parallel",), 2 * _nbytes((tm, d), F32)),
    )(x, gain)


def _resid_norm_fwd(name, xres, ff, gpost, gpre, scale):
    t, d = xres.shape
    tm = _row_tile(t)

    def body(x_ref, f_ref, gp_ref, gn_ref, xn_ref, h_ref):
        f = f_ref[...]
        xn = x_ref[...] + scale * (f * _rms(f) * gp_ref[...])
        xn_ref[...] = xn
        h_ref[...] = (xn * _rms(xn) * gn_ref[...]).astype(BF16)

    row = pl.BlockSpec((tm, d), lambda i: (i, 0))
    vec = pl.BlockSpec((1, d), lambda i: (0, 0))
    return _pallas(
        body, name=name, grid=(t // tm,), in_specs=[row, row, vec, vec], out_specs=[row, row],
        out_shape=[jax.ShapeDtypeStruct((t, d), F32), jax.ShapeDtypeStruct((t, d), BF16)],
        compiler_params=_params(("parallel",), 4 * _nbytes((tm, d), F32)),
    )(xres, ff, gpost, gpre)


def _final_fwd_bwd(name, xres, ff, gpost, target, scale):
    t, d = xres.shape
    tm = _row_tile(t)

    def body(x_ref, f_ref, gp_ref, t_ref, loss_ref, dy_ref, dff_ref, dg_ref):
        i = pl.program_id(0)
        f = f_ref[...]
        gp = gp_ref[...]
        y = x_ref[...] + scale * (f * _rms(f) * gp)
        err = y - t_ref[...]
        part = 0.5 * jnp.sum(jnp.mean(err * err, axis=-1, keepdims=True), axis=0, keepdims=True)
        _accumulate(loss_ref, jnp.broadcast_to(part, loss_ref.shape), i == 0)
        dy = err / d
        dy_ref[...] = dy
        dff, dg = _norm_bwd(scale * dy, f, gp)
        dff_ref[...] = dff.astype(BF16)
        _accumulate(dg_ref, dg, i == 0)

    row = pl.BlockSpec((tm, d), lambda i: (i, 0))
    vec = pl.BlockSpec((1, d), lambda i: (0, 0))
    return _pallas(
        body, name=name, grid=(t // tm,), in_specs=[row, row, vec, row],
        out_specs=[pl.BlockSpec((8, 128), lambda i: (0, 0)), row, row, vec],
        out_shape=[jax.ShapeDtypeStruct((8, 128), F32), jax.ShapeDtypeStruct((t, d), F32),
                   jax.ShapeDtypeStruct((t, d), BF16), jax.ShapeDtypeStruct((1, d), F32)],
        compiler_params=_params(("arbitrary",), 5 * _nbytes((tm, d), F32)),
    )(xres, ff, gpost, target)


def _norms_bwd(name, dres, dh, xin, gpre, post=None, after=None):
    t, d = dres.shape
    tm = _row_tile(t)
    with_post = post is not None

    def body(*refs):
        if with_post:
            dr_ref, dh_ref, x_ref, g_ref, f_ref, gp_ref, dx_ref, dg_ref, dff_ref, dgp_ref = refs
        else:
            dr_ref, dh_ref, x_ref, g_ref, dx_ref, dg_ref = refs
        i = pl.program_id(0)
        dx, dg = _norm_bwd(dh_ref[...], x_ref[...], g_ref[...])
        dx = dr_ref[...] + dx
        dx_ref[...] = dx
        _accumulate(dg_ref, dg, i == 0)
        if with_post:
            dff, dgp = _norm_bwd(post[2] * dx, f_ref[...], gp_ref[...])
            dff_ref[...] = dff.astype(BF16)
            _accumulate(dgp_ref, dgp, i == 0)

    row = pl.BlockSpec((tm, d), lambda i: (i, 0))
    vec = pl.BlockSpec((1, d), lambda i: (0, 0))
    ins, in_specs = [dres, dh, xin, gpre], [row, row, row, vec]
    out_specs = [row, vec]
    out_shape = [jax.ShapeDtypeStruct((t, d), F32), jax.ShapeDtypeStruct((1, d), F32)]
    if with_post:
        ins += [post[0], post[1]]
        in_specs += [row, vec]
        out_specs += [row, vec]
        out_shape += [jax.ShapeDtypeStruct((t, d), BF16), jax.ShapeDtypeStruct((1, d), F32)]
    body, ins, in_specs = _ordered(body, ins, in_specs, after)
    return _pallas(
        body, name=name, grid=(t // tm,), in_specs=in_specs, out_specs=out_specs, out_shape=out_shape,
        compiler_params=_params(("arbitrary",), 6 * _nbytes((tm, d), F32)),
    )(*ins)


SWIGLU_TILE = (1024, 512)


def _ffn_gate_up_act(name, h, w_gu):
    t, d = h.shape
    f = w_gu.shape[1] // 2
    tm, tn = _tile(t, SWIGLU_TILE[0], 128), _tile(f, SWIGLU_TILE[1], 128)
    nf = f // tn

    def body(h_ref, wg_ref, wu_ref, a_ref, dg_ref, du_ref):
        hv = h_ref[...]
        g = _dot(hv, wg_ref[...])
        u = _dot(hv, wu_ref[...])
        sig = _sigmoid(g)
        silu = g * sig
        a_ref[...] = (silu * u).astype(BF16)
        dg_ref[...] = (u * sig * (1.0 + g * (1.0 - sig))).astype(BF16)
        du_ref[...] = silu.astype(BF16)

    out = jax.ShapeDtypeStruct((t, f), BF16)
    blk = pl.BlockSpec((tm, tn), lambda i, j: (i, j))
    return _pallas(
        body, name=name, grid=(t // tm, nf),
        in_specs=[pl.BlockSpec((tm, d), lambda i, j: (i, 0)), pl.BlockSpec((d, tn), lambda i, j: (0, j)),
                  pl.BlockSpec((d, tn), lambda i, j: (0, j + nf))],
        out_specs=[blk, blk, blk], out_shape=[out, out, out],
        compiler_params=_params(("parallel", "parallel"),
                                _nbytes((tm, d), BF16) + 2 * _nbytes((d, tn), BF16) + 5 * _nbytes((tm, tn), F32)),
    )(h, w_gu, w_gu)


def _ffn_dact(name, dff, w_down, dact_dgate, dact_dup, after=None):
    t, d = dff.shape
    f = w_down.shape[0]
    tm, tn = _tile(t, SWIGLU_TILE[0], 128), _tile(f, SWIGLU_TILE[1], 128)

    def body(d_ref, w_ref, dg_ref, du_ref, o_ref):
        da = _dot(d_ref[...], w_ref[...], 1, 1)
        o_ref[0] = (da * dg_ref[...].astype(F32)).astype(BF16)
        o_ref[1] = (da * du_ref[...].astype(F32)).astype(BF16)

    blk = pl.BlockSpec((tm, tn), lambda i, j: (i, j))
    body, ins, in_specs = _ordered(
        body, [dff, w_down, dact_dgate, dact_dup],
        [pl.BlockSpec((tm, d), lambda i, j: (i, 0)), pl.BlockSpec((tn, d), lambda i, j: (j, 0)), blk, blk], after)
    return _pallas(
        body, name=name, grid=(t // tm, f // tn), in_specs=in_specs,
        out_specs=pl.BlockSpec((2, tm, tn), lambda i, j: (0, i, j)),
        out_shape=jax.ShapeDtypeStruct((2, t, f), BF16),
        compiler_params=_params(("parallel", "parallel"),
                                _nbytes((tm, d), BF16) + _nbytes((tn, d), BF16) + 5 * _nbytes((tm, tn), F32)),
    )(*ins)


def _ffn_dh(name, dgu, w_gu, after=None):
    _, t, f = dgu.shape
    d = w_gu.shape[0]
    tm, tn, tk = _mm_tiles(t, d, 2 * f, F32, k_unit=f)
    nkf = f // tk
    return _matmul(name, dgu, w_gu, form="nt", out_dtype=F32, tm=tm, tn=tn, tk=tk, sizes=(t, d, 2 * f),
                   a_map=lambda i, j, kk: (kk // nkf, i, kk % nkf), after=after)


def _lower_bound(lbp):
    m = jnp.max(lbp, axis=0, keepdims=True)
    e = jnp.exp(lbp - m)
    return e[0:1] / jnp.sum(e, axis=0, keepdims=True)


def _chunk_mask(reverse):
    row = lax.broadcasted_iota(jnp.int32, (CHUNK, CHUNK), 0)
    col = lax.broadcasted_iota(jnp.int32, (CHUNK, CHUNK), 1)
    return (col >= row) if reverse else (col <= row)


def _hgrn_gates(z, lb, mask_bf):
    sig = _sigmoid(z)
    f = lb + (1.0 - lb) * sig
    logf = jnp.log(f)
    k = 1.0 - f
    cum = _dot_exact(mask_bf, logf)
    last = jnp.sum(logf, axis=0, keepdims=True)
    return sig, f, k, cum, last


def _hgrn_scan_fwd(name, p, lbp_f, lbp_b):
    t = p.shape[0]
    hw = lbp_f.shape[1]
    nh, nc = hw // HEAD, t // CHUNK

    def body(qf, vf, zf, qb, vb, zb, lbf, lbb, of_ref, ob_ref, stf_ref, stb_ref, state):
        n = pl.program_id(0)

        @pl.when(n == 0)
        def _():
            state[...] = jnp.zeros_like(state)

        directions = [(qf, vf, zf, lbf, of_ref, stf_ref), (qb, vb, zb, lbb, ob_ref, stb_ref)]
        wide = []
        for d, (q_ref, v_ref, z_ref, lb_ref, o_ref, st_ref) in enumerate(directions):
            mask = _chunk_mask(d == 1)
            lb = _lower_bound(lb_ref[...])
            _, _, k, cum, last = _hgrn_gates(z_ref[...], lb, mask.astype(BF16))
            v = v_ref[...].astype(BF16)
            qd = (q_ref[...] * jnp.exp(cum)).astype(BF16)
            kd = (k * jnp.exp(-cum)).astype(BF16)
            kt = (k * jnp.exp(last - cum)).astype(BF16)
            s_all = state[d]
            st_ref[...] = s_all
            wide.append((mask, v, qd, kd, kt, jnp.exp(last), s_all, o_ref))
        pairs = [(d, slice(h * HEAD, (h + 1) * HEAD)) for d in range(2) for h in range(nh)]
        a = [jnp.where(wide[d][0], _dot(wide[d][2][:, sl], wide[d][3][:, sl], 1, 1), 0.0).astype(BF16)
             for d, sl in pairs]
        inter = [_dot(wide[d][2][:, sl], wide[d][6][:, sl].astype(BF16), 1, 1) for d, sl in pairs]
        intra = [_dot(a[i], wide[d][1][:, sl]) for i, (d, sl) in enumerate(pairs)]
        grow = [_dot(wide[d][1][:, sl], wide[d][4][:, sl], 0, 0) for d, sl in pairs]
        for i, (d, sl) in enumerate(pairs):
            wide[d][7][:, sl] = intra[i] + inter[i]
            state[d, :, sl] = wide[d][6][:, sl] * wide[d][5][:, sl] + grow[i]

    def col(group, reverse):
        return pl.BlockSpec((CHUNK, hw), lambda n: ((nc - 1 - n) if reverse else n, group))

    def st(reverse):
        return pl.BlockSpec((None, HEAD, hw), lambda n: ((nc - 1 - n) if reverse else n, 0, 0))

    lb_spec = pl.BlockSpec((2, hw), lambda n: (0, 0))
    out = jax.ShapeDtypeStruct((t, hw), F32)
    states = jax.ShapeDtypeStruct((nc, HEAD, hw), F32)
    return _pallas(
        body, name=name, grid=(nc,),
        in_specs=[col(0, False), col(1, False), col(2, False), col(0, True), col(1, True), col(3, True),
                  lb_spec, lb_spec],
        out_specs=[col(0, False), col(0, True), st(False), st(True)],
        out_shape=[out, out, states, states],
        scratch_shapes=[pltpu.VMEM((2, HEAD, hw), F32)],
        compiler_params=_params(("arbitrary",), 12 * _nbytes((HEAD, hw), F32)),
    )(p, p, p, p, p, p, lbp_f, lbp_b)


def _hgrn_scan_bwd(name, p, lbp_f, lbp_b, do, st_f, st_b):
    t = p.shape[0]
    hw = lbp_f.shape[1]
    nh, nc = hw // HEAD, t // CHUNK

    def body(qf, vf, zf, dof, sf, qb, vb, zb, dob, sb, lbf, lbb, dqf, dvf, dzf, dlbf, dqb, dvb, dzb, dlbb,
             dstate, dlb_acc, dqd_s, dkd_s, dkt_s, ddec_s):
        n = pl.program_id(0)

        @pl.when(n == 0)
        def _():
            dstate[...] = jnp.zeros_like(dstate)
            dlb_acc[...] = jnp.zeros_like(dlb_acc)

        directions = [(qf, vf, zf, dof, sf, lbf, dqf, dvf, dzf, dlbf), (qb, vb, zb, dob, sb, lbb, dqb, dvb, dzb, dlbb)]
        for d, (q_ref, v_ref, z_ref, do_ref, st_ref, lb_ref, dq_ref, dv_ref, dz_ref, dlb_ref) in enumerate(directions):
            mask = _chunk_mask(d == 1)
            mask_bf = mask.astype(BF16)
            lb = _lower_bound(lb_ref[...])
            sig, f, k, cum, last = _hgrn_gates(z_ref[...], lb, mask_bf)
            e_pos, e_neg, e_tail = jnp.exp(cum), jnp.exp(-cum), jnp.exp(last - cum)
            dec = jnp.exp(last)
            v = v_ref[...].astype(BF16)
            qd, kd, kt = q_ref[...] * e_pos, k * e_neg, k * e_tail
            qd_bf, kd_bf, kt_bf = qd.astype(BF16), kd.astype(BF16), kt.astype(BF16)
            s_all = st_ref[...]
            ds_all = dstate[d]
            dov = do_ref[...].astype(BF16)
            cols = [slice(h * HEAD, (h + 1) * HEAD) for h in range(nh)]
            s_bf = [s_all[:, sl].astype(BF16) for sl in cols]
            ds_bf = [ds_all[:, sl].astype(BF16) for sl in cols]
            a = [jnp.where(mask, _dot(qd_bf[:, sl], kd_bf[:, sl], 1, 1), 0.0).astype(BF16) for sl in cols]
            da = [jnp.where(mask, _dot(dov[:, sl], v[:, sl], 1, 1), 0.0).astype(BF16) for sl in cols]
            dv_h = [_dot(a[h], dov[:, sl], 0, 0) + _dot(kt_bf[:, sl], ds_bf[h], 1, 1) for h, sl in enumerate(cols)]
            dqd_h = [_dot(da[h], kd_bf[:, sl]) + _dot(dov[:, sl], s_bf[h]) for h, sl in enumerate(cols)]
            dkd_h = [_dot(da[h], qd_bf[:, sl], 0, 0) for h, sl in enumerate(cols)]
            dkt_h = [_dot(v[:, sl], ds_bf[h]) for h, sl in enumerate(cols)]
            dst_h = [_dot(dov[:, sl], qd_bf[:, sl], 0, 0) + ds_all[:, sl] * dec[:, sl] for sl in cols]
            for h, sl in enumerate(cols):
                dv_ref[:, sl] = dv_h[h]
                dqd_s[:, sl] = dqd_h[h]
                dkd_s[:, sl] = dkd_h[h]
                dkt_s[:, sl] = dkt_h[h]
                dstate[d, :, sl] = dst_h[h]
                ddec_s[:, sl] = jnp.sum(ds_all[:, sl] * s_all[:, sl], axis=0, keepdims=True)
            dqd, dkd, dkt = dqd_s[...], dkd_s[...], dkt_s[...]
            dlast = jnp.sum(dkt * kt, axis=0, keepdims=True) + dec * ddec_s[...]
            dq_ref[...] = dqd * e_pos
            dk = dkd * e_neg + dkt * e_tail
            dcum = dqd * qd - dkd * kd - dkt * kt
            dlogf = _dot_exact(mask_bf, dcum, 0, 0) + dlast
            df = dlogf / f - dk
            dz_ref[...] = df * (1.0 - lb) * sig * (1.0 - sig)
            dlb_acc[d] += jnp.sum(df * (1.0 - sig), axis=0, keepdims=True)

            @pl.when(n == nc - 1)
            def _():
                g = dlb_acc[d] * lb * (1.0 - lb)
                dlb_ref[0:1, :] = g
                dlb_ref[1:2, :] = -g

    def col(group, reverse):
        return pl.BlockSpec((CHUNK, hw), lambda n: (n if reverse else (nc - 1 - n), group))

    def st(reverse):
        return pl.BlockSpec((None, HEAD, hw), lambda n: (n if reverse else (nc - 1 - n), 0, 0))

    lb_spec = pl.BlockSpec((2, hw), lambda n: (0, 0))
    out = jax.ShapeDtypeStruct((t, hw), F32)
    dlb = jax.ShapeDtypeStruct((2, hw), F32)
    wide = pltpu.VMEM((CHUNK, hw), F32)
    return _pallas(
        body, name=name, grid=(nc,),
        in_specs=[col(0, False), col(1, False), col(2, False), col(0, False), st(False),
                  col(0, True), col(1, True), col(3, True), col(0, True), st(True), lb_spec, lb_spec],
        out_specs=[col(0, False), col(0, False), col(0, False), lb_spec,
                   col(0, True), col(0, True), col(0, True), lb_spec],
        out_shape=[out, out, out, dlb, out, out, out, dlb],
        scratch_shapes=[pltpu.VMEM((2, HEAD, hw), F32), pltpu.VMEM((2, 1, hw), F32), wide, wide, wide,
                        pltpu.VMEM((1, hw), F32)],
        compiler_params=_params(("arbitrary",), 16 * _nbytes((HEAD, hw), F32)),
    )(p, p, p, do, st_f, p, p, p, do, st_b, lbp_f, lbp_b)


def _hgrn_out_fwd(name, o_f, o_b, p, gain, g_group):
    t, hw = o_f.shape
    nh = hw // HEAD
    tm = _tile(t, 256, 16)

    def body(of_ref, ob_ref, g_ref, gain_ref, y_ref):
        o_all = of_ref[...] + ob_ref[...]
        g_all = g_ref[...]
        scale_all = gain_ref[...] * (g_all * _sigmoid(g_all))
        for h in range(nh):
            sl = slice(h * HEAD, (h + 1) * HEAD)
            o = o_all[:, sl]
            y_ref[:, sl] = (o * _rms(o) * scale_all[:, sl]).astype(BF16)

    blk = pl.BlockSpec((tm, hw), lambda i: (i, 0))
    return _pallas(
        body, name=name, grid=(t // tm,),
        in_specs=[blk, blk, pl.BlockSpec((tm, hw), lambda i: (i, g_group)), pl.BlockSpec((1, hw), lambda i: (0, 0))],
        out_specs=blk, out_shape=jax.ShapeDtypeStruct((t, hw), BF16),
        compiler_params=_params(("parallel",), 5 * _nbytes((tm, hw), F32)),
    )(o_f, o_b, p, gain)


def _hgrn_out_bwd(name, dy, o_f, o_b, p, gain, g_group, after=None):
    t, hw = o_f.shape
    nh = hw // HEAD
    tm = _tile(t, 256, 8)

    def body(dy_ref, of_ref, ob_ref, g_ref, gain_ref, do_ref, dg_ref, dgain_ref):
        i = pl.program_id(0)
        o_all = of_ref[...] + ob_ref[...]
        g_all = g_ref[...]
        sig_all = _sigmoid(g_all)
        dy_all = dy_ref[...]
        up_all = dy_all * (g_all * sig_all)
        dsilu_all = dy_all * sig_all * (1.0 + g_all * (1.0 - sig_all))
        gain_all = gain_ref[...]
        for h in range(nh):
            sl = slice(h * HEAD, (h + 1) * HEAD)
            o, gain_v = o_all[:, sl], gain_all[:, sl]
            do, dgain = _norm_bwd(up_all[:, sl], o, gain_v)
            do_ref[:, sl] = do
            dg_ref[:, sl] = dsilu_all[:, sl] * (o * _rms(o) * gain_v)
            _accumulate(dgain_ref.at[:, sl], dgain, i == 0)

    blk = pl.BlockSpec((tm, hw), lambda i: (i, 0))
    vec = pl.BlockSpec((1, hw), lambda i: (0, 0))
    out = jax.ShapeDtypeStruct((t, hw), F32)
    body, ins, in_specs = _ordered(
        body, [dy, o_f, o_b, p, gain], [blk, blk, blk, pl.BlockSpec((tm, hw), lambda i: (i, g_group)), vec], after)
    return _pallas(
        body, name=name, grid=(t // tm,), in_specs=in_specs,
        out_specs=[blk, blk, vec], out_shape=[out, out, jax.ShapeDtypeStruct((1, hw), F32)],
        compiler_params=_params(("arbitrary",), 7 * _nbytes((tm, hw), F32)),
    )(*ins)


def _t5_bucket_ids():
    c = np.arange(WINDOW)[:, None]
    s = np.arange(SPAN)[None, :]
    rel = s - WINDOW - c
    nb = REL_BUCKETS // 2
    max_exact = nb // 2
    bucket = (rel > 0).astype(np.int32) * nb
    n = np.abs(rel)
    large = max_exact + (np.log(np.maximum(n, 1) / max_exact) / np.log(REL_MAX_DIST / max_exact)
                         * (nb - max_exact)).astype(np.int32)
    large = np.minimum(large, nb - 1)
    ids = bucket + np.where(n < max_exact, n, large).astype(np.int32)
    return jnp.asarray(ids.reshape(1, WINDOW * SPAN), jnp.int32)


def _bias_onehot(ids_ref):
    n = ids_ref.shape[1]
    return (lax.broadcasted_iota(jnp.int32, (REL_BUCKETS, n), 0) == ids_ref[...]).astype(BF16)


def _bias_gather(name, table_t, ids):
    nh = table_t.shape[0]

    def body(t_ref, ids_ref, o_ref):
        o_ref[...] = _dot_exact(t_ref[...], _bias_onehot(ids_ref), split="a")

    return _pallas(
        body, name=name, out_shape=jax.ShapeDtypeStruct((nh, ids.shape[1]), F32),
        compiler_params=pltpu.CompilerParams(vmem_limit_bytes=32 << 20),
    )(table_t, ids)


def _bias_scatter(name, dbias, ids):
    nh = dbias.shape[0]

    def body(d_ref, ids_ref, o_ref):
        o_ref[...] = _dot_exact(d_ref[...], _bias_onehot(ids_ref), 1, 1, split="a")

    return _pallas(
        body, name=name, out_shape=jax.ShapeDtypeStruct((nh, REL_BUCKETS), F32),
        compiler_params=pltpu.CompilerParams(vmem_limit_bytes=32 << 20),
    )(dbias, ids)


def _attn_valid(i, t):
    c = lax.broadcasted_iota(jnp.int32, (WINDOW, SPAN), 0)
    s = lax.broadcasted_iota(jnp.int32, (WINDOW, SPAN), 1)
    rel = s - WINDOW - c
    pos = i * WINDOW - WINDOW + s
    return (jnp.abs(rel) <= WINDOW) & (pos >= 0) & (pos < t)


def _attn_probs(qs, khs, b_ref, s_ref, valid):
    heads = range(len(qs))
    sinks = [s_ref[0:1, h:h + 1] for h in heads]
    s = [_dot(qs[h], khs[h], 1, 1) / math.sqrt(HEAD) for h in heads]
    s = [jnp.where(valid, s[h] + b_ref[h], NEG_INF) for h in heads]
    m = [jnp.maximum(jnp.max(s[h], axis=-1, keepdims=True), sinks[h]) for h in heads]
    e = [jnp.exp(s[h] - m[h]) for h in heads]
    es = [jnp.exp(sinks[h] - m[h]) for h in heads]
    inv = [1.0 / (jnp.sum(e[h], axis=-1, keepdims=True) + es[h]) for h in heads]
    return [e[h] * inv[h] for h in heads], [es[h] * inv[h] for h in heads]


def _kv_window_specs(kv_blk, kvw, nb):
    return [pl.BlockSpec((WINDOW, kvw), lambda i, s=s: (jnp.clip(i + s, 0, nb - 1), kv_blk)) for s in (-1, 0, 1)]


def _kv_window(refs):
    return jnp.concatenate([r[...] for r in refs], axis=0).astype(BF16)


def _attn_fwd(name, p, kv_blk, kvw, bias, sink, q_group_blk):
    t = p.shape[0]
    nh = bias.shape[0]
    aw = nh * HEAD
    grp = nh // KV_HEADS
    nb = t // WINDOW

    def body(q_ref, kp, kc, kn, vp, vc, vn, b_ref, s_ref, y_ref, pr_ref, ps_ref):
        i = pl.program_id(0)
        valid = _attn_valid(i, t)
        ks = _kv_window((kp, kc, kn))
        vs = _kv_window((vp, vc, vn))
        heads = range(nh)
        col = lambda h: slice(h * HEAD, (h + 1) * HEAD)
        qs = [q_ref[:, col(h)].astype(BF16) for h in heads]
        pr, ps = _attn_probs(qs, [ks[:, col(h // grp)] for h in heads], b_ref, s_ref, valid)
        pr = [pr[h].astype(BF16) for h in heads]
        out = [_dot(pr[h], vs[:, col(h // grp)]) for h in heads]
        lane = lax.broadcasted_iota(jnp.int32, (WINDOW, 128), 1)
        sinks = jnp.zeros((WINDOW, 128), F32)
        for h in heads:
            y_ref[:, col(h)] = out[h].astype(BF16)
            pr_ref[h] = pr[h]
            sinks = jnp.where(lane == h, ps[h], sinks)
        ps_ref[...] = sinks

    full = lambda a: pl.BlockSpec(a.shape, lambda i: (0,) * a.ndim)
    return _pallas(
        body, name=name, grid=(nb,),
        in_specs=[pl.BlockSpec((WINDOW, aw), lambda i: (i, q_group_blk)), *_kv_window_specs(kv_blk, kvw, nb),
                  *_kv_window_specs(kv_blk + 1, kvw, nb), full(bias), full(sink)],
        out_specs=[pl.BlockSpec((WINDOW, aw), lambda i: (i, 0)), pl.BlockSpec((nh, WINDOW, SPAN), lambda i: (0, i, 0)),
                   pl.BlockSpec((WINDOW, 128), lambda i: (i, 0))],
        out_shape=[jax.ShapeDtypeStruct((t, aw), BF16), jax.ShapeDtypeStruct((nh, t, SPAN), BF16),
                   jax.ShapeDtypeStruct((t, 128), F32)],
        compiler_params=_params(("parallel",), 3 * _nbytes(bias.shape, F32)),
    )(p, p, p, p, p, p, p, bias, sink)


def _attn_bwd(name, p, kv_blk, kvw, probs, sink_probs, dy, q_group_blk, dy_blk, after=None):
    t = p.shape[0]
    nh = probs.shape[0]
    aw = nh * HEAD
    grp = nh // KV_HEADS
    nb = t // WINDOW

    def body(q_ref, kp, kc, kn, vp, vc, vn, pr_ref, ps_ref, dy_ref, dq_ref, dk_ref, dv_ref, db_ref, ds_ref):
        i = pl.program_id(0)

        @pl.when(i == 0)
        def _():
            dk_ref[...] = jnp.zeros_like(dk_ref)
            dv_ref[...] = jnp.zeros_like(dv_ref)
            db_ref[...] = jnp.zeros_like(db_ref)
            ds_ref[...] = jnp.zeros_like(ds_ref)

        start = pl.multiple_of(i * WINDOW, WINDOW)
        ks = _kv_window((kp, kc, kn))
        vs = _kv_window((vp, vc, vn))
        inv_sqrt = 1.0 / math.sqrt(HEAD)
        heads = range(nh)
        col = lambda h: slice(h * HEAD, (h + 1) * HEAD)
        qs = [q_ref[:, col(h)].astype(BF16) for h in heads]
        khs = [ks[:, col(h // grp)] for h in heads]
        pr_bf = [pr_ref[h] for h in heads]
        pr = [pr_bf[h].astype(F32) for h in heads]
        dos = [dy_ref[:, col(h)].astype(BF16) for h in heads]
        dp = [_dot(dos[h], vs[:, col(h // grp)], 1, 1) for h in heads]
        delta = [jnp.sum(pr[h] * dp[h], axis=-1, keepdims=True) for h in heads]
        dsc = [pr[h] * (dp[h] - delta[h]) for h in heads]
        dsr = [(dsc[h] * inv_sqrt).astype(BF16) for h in heads]
        dq = [_dot(dsr[h], khs[h]) for h in heads]
        dk = [_dot(dsr[h], qs[h], 0, 0) for h in heads]
        dv = [_dot(pr_bf[h], dos[h], 0, 0) for h in heads]
        for h in heads:
            db_ref[h] += dsc[h]
            dsink = jnp.sum(-ps_ref[:, h:h + 1] * delta[h], axis=0, keepdims=True)
            ds_ref[h:h + 1, :] += jnp.broadcast_to(dsink, (1, 128))
            dq_ref[:, col(h)] = dq[h]
        for kv in range(KV_HEADS):
            group = range(kv * grp, (kv + 1) * grp)
            dk_ref[pl.ds(start, SPAN), col(kv)] += sum(dk[h] for h in group)
            dv_ref[pl.ds(start, SPAN), col(kv)] += sum(dv[h] for h in group)

    whole = lambda shape: pl.BlockSpec(shape, lambda i: (0,) * len(shape))
    pad_shape = (t + 2 * WINDOW, kvw)
    bias_shape = (nh, WINDOW, SPAN)
    body, ins, in_specs = _ordered(
        body, [p, p, p, p, p, p, p, probs, sink_probs, dy],
        [pl.BlockSpec((WINDOW, aw), lambda i: (i, q_group_blk)), *_kv_window_specs(kv_blk, kvw, nb),
         *_kv_window_specs(kv_blk + 1, kvw, nb),
         pl.BlockSpec((nh, WINDOW, SPAN), lambda i: (0, i, 0)), pl.BlockSpec((WINDOW, 128), lambda i: (i, 0)),
         pl.BlockSpec((WINDOW, aw), lambda i: (i, dy_blk))], after)
    return _pallas(
        body, name=name, grid=(nb,), in_specs=in_specs,
        out_specs=[pl.BlockSpec((WINDOW, aw), lambda i: (i, 0)), whole(pad_shape), whole(pad_shape),
                   whole(bias_shape), whole((nh, 128))],
        out_shape=[jax.ShapeDtypeStruct((t, aw), F32), jax.ShapeDtypeStruct(pad_shape, F32),
                   jax.ShapeDtypeStruct(pad_shape, F32), jax.ShapeDtypeStruct(bias_shape, F32),
                   jax.ShapeDtypeStruct((nh, 128), F32)],
        compiler_params=_params(("arbitrary",), 3 * _nbytes(pad_shape, F32) + 3 * _nbytes(bias_shape, F32)),
    )(*ins)


def _mix_dproj(name, pieces, kv_pads, t, after=None):
    hw = pieces[0][0].shape[1]
    kvw = kv_pads[0].shape[1]
    widths = [hw] * len(pieces) + [kvw] * len(kv_pads)
    total = sum(widths)
    tm = WINDOW
    flat = [a for pc in pieces for a in pc]

    def body(*refs):
        o_ref = refs[-1]
        pos, off = 0, 0
        for pc in pieces:
            val = refs[pos][...]
            for extra in range(1, len(pc)):
                val = val + refs[pos + extra][...]
            o_ref[:, off:off + hw] = val.astype(BF16)
            pos += len(pc)
            off += hw
        for _ in kv_pads:
            o_ref[:, off:off + kvw] = refs[pos][...].astype(BF16)
            pos += 1
            off += kvw

    in_specs = [pl.BlockSpec((tm, hw), lambda i: (i, 0)) for _ in flat]
    in_specs += [pl.BlockSpec((tm, kvw), lambda i: (i + 1, 0)) for _ in kv_pads]
    body, ins, in_specs = _ordered(body, [*flat, *kv_pads], in_specs, after)
    return _pallas(
        body, name=name, grid=(t // tm,), in_specs=in_specs,
        out_specs=pl.BlockSpec((tm, total), lambda i: (i, 0)),
        out_shape=jax.ShapeDtypeStruct((t, total), BF16),
        compiler_params=_params(("parallel",), 3 * _nbytes((tm, total), F32)),
    )(*ins)


def _concat_cols(name, a, b):
    t, wa = a.shape
    wb = b.shape[1]
    tm = _tile(t, 512, 16)

    def body(a_ref, b_ref, o_ref):
        o_ref[:, :wa] = a_ref[...]
        o_ref[:, wa:] = b_ref[...]

    return _pallas(
        body, name=name, grid=(t // tm,),
        in_specs=[pl.BlockSpec((tm, wa), lambda i: (i, 0)), pl.BlockSpec((tm, wb), lambda i: (i, 0))],
        out_specs=pl.BlockSpec((tm, wa + wb), lambda i: (i, 0)),
        out_shape=jax.ShapeDtypeStruct((t, wa + wb), a.dtype),
        compiler_params=_params(("parallel",), 2 * _nbytes((tm, wa + wb), a.dtype)),
    )(a, b)


def _cast_into_full(name, w, geom, idx, after=None):
    r, c = w.shape
    tr = _tile(r, 256, 16)
    nr = r // tr
    if geom.col:
        place = lambda i, iref: (i, iref[0])
    else:
        place = lambda i, iref: (iref[0] * nr + i, 0)

    def body(i_ref, w_ref, *rest):
        rest[-1][...] = w_ref[...].astype(BF16)

    in_specs = [pl.BlockSpec((tr, c), lambda i, iref: (i, 0))]
    ins = [w]
    if after is not None:
        in_specs.append(pl.BlockSpec(memory_space=pl.ANY))
        ins.append(after)
    return _pallas(
        body, name=name,
        grid_spec=pltpu.PrefetchScalarGridSpec(
            num_scalar_prefetch=1, grid=(nr,), in_specs=in_specs, out_specs=pl.BlockSpec((tr, c), place)),
        out_shape=pltpu.HBM(geom.full_shape, BF16),
        compiler_params=_params(("parallel",), 2 * _nbytes((tr, c), F32)),
    )(idx, *ins)


def _adamw(name, w, g, m, v):
    r, c = w.shape
    tr = _tile(r, 256, 8)
    bc1 = 1.0 - ADAM_B1 ** ADAM_STEP
    bc2 = 1.0 - ADAM_B2 ** ADAM_STEP

    def body(w_ref, g_ref, m_ref, v_ref, go_ref, d_ref, nm_ref, nv_ref):
        gv = g_ref[...]
        go_ref[...] = gv
        nm = ADAM_B1 * m_ref[...] + (1.0 - ADAM_B1) * gv
        nv = ADAM_B2 * v_ref[...] + (1.0 - ADAM_B2) * (gv * gv)
        nm_ref[...] = nm
        nv_ref[...] = nv
        d_ref[...] = -ADAM_LR * ((nm / bc1) / (jnp.sqrt(nv / bc2) + ADAM_EPS) + ADAM_WD * w_ref[...])

    blk = pl.BlockSpec((tr, c), lambda i: (i, 0))
    out = jax.ShapeDtypeStruct((r, c), F32)
    return _pallas(
        body, name=name, grid=(r // tr,), in_specs=[blk] * 4, out_specs=[blk] * 4, out_shape=[out] * 4,
        compiler_params=_params(("parallel",), 8 * _nbytes((tr, c), F32)),
    )(w, g, m, v)


def _mesh_pos():
    return lax.axis_index("x"), lax.axis_index("y"), lax.axis_index("c")


def _other_chips(x, y):
    return [(1 - x, y), (x, 1 - y), (1 - x, 1 - y)]


class _Big:
    def __init__(self, shard_shape, col_sharded):
        self.col = col_sharded
        r, c = shard_shape
        self.shard_shape = (r, c)
        self.full_shape = (r, N_CHIPS * c) if col_sharded else (N_CHIPS * r, c)
        self.half_shape = (r // 2, N_CHIPS * c) if col_sharded else (N_CHIPS * r, c // 2)
        self.shard_half_shape = (r // 2, c) if col_sharded else (r, c // 2)

    def region(self, ref, s, half=None):
        r, c = self.shard_shape
        if self.col:
            rows = slice(None) if half is None else pl.ds(half * (r // 2), r // 2)
            return ref.at[rows, pl.ds(s * c, c)]
        cols = slice(None) if half is None else pl.ds(half * (c // 2), c // 2)
        return ref.at[pl.ds(s * r, r), cols]

    def n_halves(self, ref, half, n):
        r, c = self.shard_shape
        if self.col:
            return ref.at[pl.ds(half * (r // 2), r // 2), pl.ds(0, n * c)]
        return ref.at[pl.ds(0, n * r), pl.ds(half * (c // 2), c // 2)]

    def three_halves(self, ref, half):
        return self.n_halves(ref, half, 3)

    def sub_half(self, ref, s, half, j):
        r, c = self.shard_shape
        if self.col:
            return ref.at[pl.ds(half * (r // 2) + j * (r // 4), r // 4), pl.ds(s * c, c)]
        return ref.at[pl.ds(s * r + j * (r // 2), r // 2), pl.ds(half * (c // 2), c // 2)]

    def half_of_full(self, ref, half):
        r, c = self.full_shape
        if self.col:
            return ref.at[pl.ds(half * (r // 2), r // 2), :]
        return ref.at[:, pl.ds(half * (c // 2), c // 2)]

    def half_of_shard(self, ref, half):
        r, c = self.shard_shape
        if self.col:
            return ref.at[pl.ds(half * (r // 2), r // 2), :]
        return ref.at[:, pl.ds(half * (c // 2), c // 2)]

    def shard_of_half(self, ref, s):
        r, c = self.shard_shape
        if self.col:
            return ref.at[:, pl.ds(s * c, c)]
        return ref.at[pl.ds(s * r, r), :]


HBM =pl.BlockSpec(memory_space=pltpu.HBM)
SEM = pl.BlockSpec(memory_space=pltpu.SEMAPHORE)
SPLIT_COPY = pltpu.CompilerParams(has_side_effects=pltpu.SideEffectType.DATAFLOW_SIDE_EFFECTING)


def _in_hbm(a):
    return pltpu.with_memory_space_constraint(a, pltpu.HBM)


def _gather_start(name, fulls, geoms, after):
    nw = len(fulls)

    def body(*refs):
        dst = refs[nw + 1:2 * nw + 1]
        sems = refs[2 * nw + 1:-1]
        x, y, c = _mesh_pos()
        mine = 2 * x + y
        for w in range(nw):
            own_half = geoms[w].region(dst[w], mine, c)
            for chip in _other_chips(x, y):
                pltpu.make_async_remote_copy(src_ref=own_half, dst_ref=own_half, send_sem=sems[2 * w],
                                             recv_sem=sems[2 * w + 1], device_id=(*chip, c),
                                             device_id_type=MESH).start()
        refs[-1][...] = jnp.zeros_like(refs[-1])

    out = _pallas(
        body, name=name, in_specs=[HBM] * nw + [pl.BlockSpec(memory_space=pl.ANY)],
        out_specs=[HBM] * nw + [SEM] * (2 * nw) + [pl.BlockSpec(memory_space=pltpu.VMEM)],
        out_shape=[pltpu.HBM(g.full_shape, BF16) for g in geoms] + [pltpu.SemaphoreType.DMA(())] * (2 * nw)
        + [jax.ShapeDtypeStruct((8, 128), F32)],
        input_output_aliases={w: w for w in range(nw)}, compiler_params=SPLIT_COPY,
    )(*[_in_hbm(a) for a in fulls], after)
    return list(out[:nw]), [(out[nw + 2 * w], out[nw + 2 * w + 1]) for w in range(nw)], out[-1]


def _gather_first_direct(full, geom):
    def start(refs, _, new):
        x, y, c = _mesh_pos()
        own = geom.region(refs[0], 2 * x + y, c)
        for chip in ((1 - x, y), (x, 1 - y)):
            _remote(own, own, new, (*chip, c)).start()

    return _split_copy_call("gather_first_direct", [full], start, new_sems=2)


def _gather_first_relay(full, geom, sems, after):
    def relay(refs, got, new):
        x, y, c = _mesh_pos()
        w = refs[0]
        two = geom.n_halves(w, c, 2)
        _remote(two, two, got, (x, y, 1 - c)).wait_recv()
        from_x = geom.sub_half(w, 2 * (1 - x) + y, c, 0)
        from_y = geom.sub_half(w, 2 * x + (1 - y), c, 1)
        _remote(from_x, from_x, new, (x, 1 - y, c)).start()
        _remote(from_y, from_y, new, (1 - x, y, c)).start()
        _remote(two, two, got, (x, y, 1 - c)).wait_send()

    return _split_copy_call("gather_first_relay", [full], relay, sems=sems, after=after, new_sems=2)


def _gather_forward(name, full, geom, sems, after, arrivals=3, only_diagonal=False):
    def body(w_in, send_sem, recv_sem, after_ref, w_ref, fwd_send, fwd_recv):
        x, y, c = _mesh_pos()
        sibling = (x, y, 1 - c)
        landed_all = geom.n_halves(w_ref, c, arrivals)
        _remote(landed_all, landed_all, (send_sem, recv_sem), sibling).wait_recv()
        for chip in _other_chips(x, y)[2 if only_diagonal else 0:]:
            landed = geom.region(w_ref, 2 * chip[0] + chip[1], c)
            pltpu.make_async_remote_copy(src_ref=landed, dst_ref=landed, send_sem=fwd_send, recv_sem=fwd_recv,
                                         device_id=sibling, device_id_type=MESH).start()
        _remote(landed_all, landed_all, (send_sem, recv_sem), sibling).wait_send()

    sem = pltpu.SemaphoreType.DMA(())
    out = _pallas(
        body, name=name, in_specs=[HBM, SEM, SEM, pl.BlockSpec(memory_space=pl.ANY)], out_specs=[HBM, SEM, SEM],
        out_shape=[pltpu.HBM(geom.full_shape, BF16), sem, sem],
        input_output_aliases={0: 0}, compiler_params=SPLIT_COPY,
    )(full, sems[0], sems[1], after)
    return out[0], (out[1], out[2])


def _gather_end(name, full, geom, sems, after, halves=3):
    def body(w_in, fwd_send, fwd_recv, after_ref, w_ref):
        x, y, c = _mesh_pos()
        sibling = (x, y, 1 - c)
        theirs, ours = geom.n_halves(w_ref, 1 - c, halves), geom.n_halves(w_ref, c, halves)
        _remote(theirs, theirs, (fwd_send, fwd_recv), sibling).wait_recv()
        _remote(ours, ours, (fwd_send, fwd_recv), sibling).wait_send()

    return _pallas(
        body, name=name, in_specs=[HBM, SEM, SEM, pl.BlockSpec(memory_space=pl.ANY)], out_specs=HBM,
        out_shape=pltpu.HBM(geom.full_shape, BF16),
        input_output_aliases={0: 0}, compiler_params=SPLIT_COPY,
    )(full, sems[0], sems[1], after)


def _split_copy_call(name, arrays, fn, sems=(), after=None, new_sems=0):
    n, ns = len(arrays), len(sems)
    n_in = n + ns + (after is not None)

    def body(*refs):
        fn(refs[n_in:n_in + n], refs[n:n + ns], refs[n_in + n:-1])
        refs[-1][...] = jnp.zeros_like(refs[-1])

    ins = list(arrays) if ns else [_in_hbm(a) for a in arrays]
    ins += list(sems) + ([after] if after is not None else [])
    in_specs = [HBM] * n + [SEM] * ns + ([pl.BlockSpec(memory_space=pl.ANY)] if after is not None else [])
    out = _pallas(
        body, name=name, in_specs=in_specs,
        out_specs=[HBM] * n + [SEM] * new_sems + [pl.BlockSpec(memory_space=pltpu.VMEM)],
        out_shape=[pltpu.HBM(a.shape, a.dtype) for a in arrays] + [pltpu.SemaphoreType.DMA(())] * new_sems
        + [jax.ShapeDtypeStruct((8, 128), F32)],
        input_output_aliases={i: i for i in range(n)}, compiler_params=SPLIT_COPY,
    )(*ins)
    return list(out[:n]), tuple(out[n:-1]), out[-1]


def _remote(src, dst, sems, to):
    return pltpu.make_async_remote_copy(src_ref=src, dst_ref=dst, send_sem=sems[0], recv_sem=sems[1],
                                        device_id=to, device_id_type=MESH)


class _GradReduce:
    def __init__(self, name, geom, idx, c_idx):
        self.name, self.geom, self.idx, self.c_idx = name, geom, idx, c_idx

    def pair_start(self, theirs):
        g = self.geom

        def start(refs, _, new):
            x, y, c = _mesh_pos()
            _remote(refs[0], refs[1], new, (x, y, 1 - c)).start()

        self.arrays, self.sems, token = _split_copy_call(
            f"pair_start_{self.name}", [theirs, lax.empty(g.half_shape, BF16)], start, new_sems=2)
        return token

    def pair_wait(self, after):
        def wait(refs, sems, _):
            x, y, c = _mesh_pos()
            copy = _remote(refs[0], refs[1], sems, (x, y, 1 - c))
            copy.wait_send()
            copy.wait_recv()

        (_, landed), _, _ = _split_copy_call(f"pair_wait_{self.name}", self.arrays, wait, self.sems, after)
        return landed

    def chip_start(self, half):
        g = self.geom

        def start(refs, _, new):
            x, y, c = _mesh_pos()
            for k, chip in enumerate(_other_chips(x, y)):
                _remote(g.shard_of_half(refs[0], 2 * chip[0] + chip[1]), refs[1].at[k], new, (*chip, c)).start()

        self.arrays, self.sems, token = _split_copy_call(
            f"chip_start_{self.name}", [half, lax.empty((3,) + g.shard_half_shape, BF16)], start, new_sems=2)
        return token

    def chip_finish(self, after):
        g = self.geom

        def wait(refs, sems, _):
            x, y, c = _mesh_pos()
            three = _remote(refs[1], refs[1], sems, (x, y, 1 - c))
            three.wait_send()
            three.wait_recv()

        (half, landed), _, _ = _split_copy_call(f"chip_wait_{self.name}", self.arrays, wait, self.sems, after)
        quarter = _chip_add(f"chip_add_{self.name}", half, landed, g, self.idx)

        def start(refs, _, new):
            x, y, c = _mesh_pos()
            own = g.half_of_shard(refs[0], c)
            _remote(own, own, new, (x, y, 1 - c)).start()

        self.arrays, self.sems, token = _split_copy_call(f"share_start_{self.name}", [quarter], start, new_sems=2)
        return token

    def finish(self, after):
        g = self.geom

        def wait(refs, sems, _):
            x, y, c = _mesh_pos()
            own, theirs = g.half_of_shard(refs[0], c), g.half_of_shard(refs[0], 1 - c)
            _remote(own, own, sems, (x, y, 1 - c)).wait_send()
            _remote(theirs, theirs, sems, (x, y, 1 - c)).wait_recv()

        (quarter,), _, _ = _split_copy_call(f"share_wait_{self.name}", self.arrays, wait, self.sems, after)
        return quarter


def _dw_half(name, x, dy, geom, c_idx, own, addend=None, after=None):
    stacked = dy.ndim == 3
    t, m = x.shape
    n = 2 * dy.shape[2] if stacked else dy.shape[1]
    hm, hn = (m // 2, n) if geom.col else (m, n // 2)
    tm, tn, tk = _mm_tiles(hm, hn, t, BF16, n_unit=(n // 2 if stacked else None))
    if tk != t:
        tm, tn = _tile(hm, 512, 128), _tile(hn // (2 if stacked else 1), 512, 128)
    gi, gj = hm // tm, hn // tn
    nf = (n // 2) // tn

    def sel(cref):
        return cref[0] if own else 1 - cref[0]

    a_map = (lambda i, j, cref: (0, sel(cref) * gi + i)) if geom.col else (lambda i, j, cref: (0, i))
    if stacked:
        b_blk, b_map = (None, t, tn), (lambda i, j, cref: (j // nf, 0, j % nf))
    elif geom.col:
        b_blk, b_map = (t, tn), (lambda i, j, cref: (0, j))
    else:
        b_blk, b_map = (t, tn), (lambda i, j, cref: (0, sel(cref) * gj + j))
    out_blk = pl.BlockSpec((tm, tn), lambda i, j, cref: (i, j))
    ins, in_specs = [x, dy], [pl.BlockSpec((t, tm), a_map), pl.BlockSpec(b_blk, b_map)]
    if addend is not None:
        ins.append(addend)
        in_specs.append(out_blk)
    if after is not None:
        ins.append(after)
        in_specs.append(pl.BlockSpec(memory_space=pl.ANY))

    def body(c_ref, *refs):
        acc = _dot(refs[0][...], refs[1][...], 0, 0)
        if addend is not None:
            acc = acc + refs[2][...].astype(F32)
        refs[len(ins)][...] = acc.astype(BF16)

    return _pallas(
        body, name=name,
        grid_spec=pltpu.PrefetchScalarGridSpec(num_scalar_prefetch=1, grid=(gi, gj), in_specs=in_specs,
                                               out_specs=out_blk),
        out_shape=jax.ShapeDtypeStruct((hm, hn), BF16),
        compiler_params=_params(("parallel", "parallel"),
                                _nbytes((t, tm), BF16) + _nbytes((t, tn), BF16) + 3 * _nbytes((tm, tn), F32)),
    )(c_idx, *ins)


def _chip_add(name, half, recv, geom, idx):
    r, c = geom.shard_half_shape
    tr, tc = _tile(r, 512, 16), _tile(c, 2048, 128)
    nr, ncol = r // tr, c // tc
    if geom.col:
        mine = lambda i, j, iref: (i, iref[0] * ncol + j)
        place = lambda i, j, iref: (iref[1] * nr + i, j)
    else:
        mine = lambda i, j, iref: (iref[0] * nr + i, j)
        place = lambda i, j, iref: (i, iref[1] * ncol + j)

    def body(i_ref, h_ref, r_ref, o_ref):
        acc = h_ref[...].astype(F32)
        for k in range(3):
            acc = acc + r_ref[k].astype(F32)
        o_ref[...] = acc

    return _pallas(
        body, name=name,
        grid_spec=pltpu.PrefetchScalarGridSpec(
            num_scalar_prefetch=1, grid=(nr, ncol),
            in_specs=[pl.BlockSpec((tr, tc), mine), pl.BlockSpec((3, tr, tc), lambda i, j, iref: (0, i, j))],
            out_specs=pl.BlockSpec((tr, tc), place)),
        out_shape=jax.ShapeDtypeStruct(geom.shard_shape, F32),
        compiler_params=_params(("parallel", "parallel"), 4 * _nbytes((tr, tc), F32)),
    )(idx, half, recv)


def _all_reduce_small(pack, after=None):
    r, d = pack.shape

    def body(p_ref, o_ref, slots, send_sems, recv_sems):
        x, y, c = _mesh_pos()
        me = 4 * x + 2 * y + c
        slots[me] = p_ref[...]
        copies = []
        for k in range(1, N_DEV):
            px, py, pc = x ^ ((k >> 2) & 1), y ^ ((k >> 1) & 1), c ^ (k & 1)
            copies.append(pltpu.make_async_remote_copy(
                src_ref=p_ref, dst_ref=slots.at[me], send_sem=send_sems.at[k - 1], recv_sem=recv_sems.at[k - 1],
                device_id=(px, py, pc), device_id_type=MESH))
        for cp in copies:
            cp.start()
        for k in range(1, N_DEV):
            peer = 4 * (x ^ ((k >> 2) & 1)) + 2 * (y ^ ((k >> 1) & 1)) + (c ^ (k & 1))
            pltpu.make_async_remote_copy(
                src_ref=p_ref, dst_ref=slots.at[peer], send_sem=send_sems.at[k - 1], recv_sem=recv_sems.at[k - 1],
                device_id=(x, y, c), device_id_type=MESH).wait_recv()
        for cp in copies:
            cp.wait_send()
        acc = slots[0]
        for k in range(1, N_DEV):
            acc = acc + slots[k]
        o_ref[...] = acc

    vm = pl.BlockSpec(memory_space=pltpu.VMEM)
    body, ins, in_specs = _ordered(body, [pack], [vm], after)
    return _pallas(
        body, name="all_reduce_small", in_specs=in_specs, out_specs=vm,
        out_shape=jax.ShapeDtypeStruct((r, d), F32),
        scratch_shapes=[pltpu.VMEM((N_DEV, r, d), F32), pltpu.SemaphoreType.DMA((N_DEV - 1,)),
                        pltpu.SemaphoreType.DMA((N_DEV - 1,))],
    )(*ins)


def _pack_rows(rows, d):
    out = []
    for a in rows:
        flat = a.reshape(-1)
        n = -(-flat.shape[0] // d) * d
        out.append(jnp.pad(flat, (0, n - flat.shape[0])).reshape(-1, d))
    packed = jnp.concatenate(out, axis=0)
    return jnp.pad(packed, ((0, 16 - packed.shape[0]), (0, 0)))


def _unpack_rows(packed, shapes, d):
    out, row = [], 0
    for shp in shapes:
        n = int(np.prod(shp))
        nrows = -(-n // d)
        out.append(packed[row:row + nrows].reshape(-1)[:n].reshape(shp))
        row += nrows
    return out


def kernel(x, pre_norm_ffn1, post_norm_ffn1, w_ffn1_gate_up, w_ffn1_down, pre_norm_mix, post_norm_mix, w_mix_in, hgrn_lower_bounds_fwd, hgrn_lower_bounds_bwd, hgrn_out_norm, attn_sink, w_mix_out, pre_norm_ffn2, post_norm_ffn2, w_ffn2_gate_up, w_ffn2_down, rel_bias_table, loss_target, m_pre_norm_ffn1, m_post_norm_ffn1, m_w_ffn1_gate_up, m_w_ffn1_down, m_pre_norm_mix, m_post_norm_mix, m_w_mix_in, m_hgrn_lower_bounds_fwd, m_hgrn_lower_bounds_bwd, m_hgrn_out_norm, m_attn_sink, m_w_mix_out, m_pre_norm_ffn2, m_post_norm_ffn2, m_w_ffn2_gate_up, m_w_ffn2_down, m_rel_bias_table, v_pre_norm_ffn1, v_post_norm_ffn1, v_w_ffn1_gate_up, v_w_ffn1_down, v_pre_norm_mix, v_post_norm_mix, v_w_mix_in, v_hgrn_lower_bounds_fwd, v_hgrn_lower_bounds_bwd, v_hgrn_out_norm, v_attn_sink, v_w_mix_out, v_pre_norm_ffn2, v_post_norm_ffn2, v_w_ffn2_gate_up, v_w_ffn2_down, v_rel_bias_table):
    t, d = x.shape[1], x.shape[2]
    hw = hgrn_out_norm.shape[1]
    aw = d - hw
    nah = aw // HEAD
    kvw = KV_HEADS * HEAD
    x0 = x[0]
    target = loss_target[0]

    big_names = ["w_ffn1_gate_up", "w_ffn1_down", "w_mix_in", "w_mix_out", "w_ffn2_gate_up", "w_ffn2_down"]
    big_w = [w_ffn1_gate_up[0], w_ffn1_down[0], w_mix_in[0], w_mix_out[0], w_ffn2_gate_up[0], w_ffn2_down[0]]
    big_m = [m_w_ffn1_gate_up[0], m_w_ffn1_down[0], m_w_mix_in[0], m_w_mix_out[0], m_w_ffn2_gate_up[0],
             m_w_ffn2_down[0]]
    big_v = [v_w_ffn1_gate_up[0], v_w_ffn1_down[0], v_w_mix_in[0], v_w_mix_out[0], v_w_ffn2_gate_up[0],
             v_w_ffn2_down[0]]
    col_sharded = [True, False, True, False, True, False]
    geoms = [_Big(w.shape, cs) for w, cs in zip(big_w, col_sharded)]

    cx, cy, cc = _mesh_pos()
    idx = jnp.stack([2 * cx + cy, cc]).astype(jnp.int32)
    c_idx = jnp.reshape(cc, (1,)).astype(jnp.int32)
    first = _cast_into_full(f"cast_{big_names[0]}", big_w[0], geoms[0], idx)
    (first,), direct_sems, tok = _gather_first_direct(first, geoms[0])
    rest = []
    for n, w, gm in zip(big_names[1:], big_w[1:], geoms[1:]):
        tok = _cast_into_full(f"cast_{n}", w, gm, idx, after=tok)
        rest.append(tok)
    (first,), relay_sems, tok = _gather_first_relay(first, geoms[0], direct_sems, after=tok)
    started_rest, sems_rest, rest_started = _gather_start("gather_start_rest", rest, geoms[1:], after=tok)
    started, gather_sems = [first] + started_rest, [relay_sems] + sems_rest

    def forward_weight(w, after):
        return _gather_forward(f"gather_forward_{big_names[w]}", started[w], geoms[w], gather_sems[w], after,
                               arrivals=1 if w == 0 else 3)

    def whole_weight(w, forwarded, after):
        return _gather_end(f"gather_end_{big_names[w]}", forwarded[0], geoms[w], forwarded[1], after)

    h1 = _norm_fwd("ffn1_pre_norm", x0, pre_norm_ffn1)
    w_gu1 = whole_weight(0, forward_weight(0, rest_started), h1)
    act1, dact_dgate1, dact_dup1 = _ffn_gate_up_act("ffn1_gate_up", h1, w_gu1)
    w_d1 = whole_weight(1, forward_weight(1, act1), act1)
    ff1 = _mm("ffn1_down", act1, w_d1, "nn", F32)
    fw = forward_weight(2, ff1)
    x1, hm = _resid_norm_fwd("ffn1_residual", x0, ff1, post_norm_ffn1, pre_norm_mix, 0.5)
    w_in = whole_weight(2, fw, hm)
    p = _mm("mix_in", hm, w_in, "nn", F32)
    fw = forward_weight(3, p)
    o_f, o_b, st_f, st_b = _hgrn_scan_fwd("hgrn_scan", p, hgrn_lower_bounds_fwd, hgrn_lower_bounds_bwd)
    y_h = _hgrn_out_fwd("hgrn_out", o_f, o_b, p, hgrn_out_norm, 4)
    kv_blk0 = (5 * hw + aw) // kvw
    bucket_ids = _t5_bucket_ids()
    bias = _bias_gather("attn_bias", rel_bias_table.T, bucket_ids).reshape(nah, WINDOW, SPAN)
    y_a, attn_probs, attn_sink_probs = _attn_fwd("attn_fwd", p, kv_blk0, kvw, bias, attn_sink, 5 * hw // aw)
    y_mix = _concat_cols("mix_concat", y_h, y_a)
    w_out = whole_weight(3, fw, y_mix)
    mixed = _mm("mix_out", y_mix, w_out, "nn", F32)
    fw = forward_weight(4, mixed)
    x2, h2 = _resid_norm_fwd("mix_residual", x1, mixed, post_norm_mix, pre_norm_ffn2, 1.0)
    w_gu2 = whole_weight(4, fw, h2)
    act2, dact_dgate2, dact_dup2 = _ffn_gate_up_act("ffn2_gate_up", h2, w_gu2)
    w_d2 = whole_weight(5, forward_weight(5, act2), act2)
    ff2 = _mm("ffn2_down", act2, w_d2, "nn", F32)
    loss_blk, dy, dff2, dg_post2 = _final_fwd_bwd("ffn2_residual_loss", x2, ff2, post_norm_ffn2, target, 0.5)

    reduce = [_GradReduce(n, gm, idx, c_idx) for n, gm in zip(big_names, geoms)]
    big_grads, big_delta, big_new_m, big_new_v = [None] * 6, [None] * 6, [None] * 6, [None] * 6

    def update(w, after):
        g, dl, nm, nv = _adamw(f"adamw_{big_names[w]}", big_w[w], reduce[w].finish(after), big_m[w], big_v[w])
        big_grads[w], big_delta[w], big_new_m[w], big_new_v[w] = g[None], dl[None], nm[None], nv[None]
        return dl

    def dw_start(w, x_act, dy_act, after=None):
        theirs = _dw_half(f"dw_theirs_{big_names[w]}", x_act, dy_act, geoms[w], c_idx, own=False, after=after)
        return reduce[w].pair_start(theirs)

    def dw_finish(w, x_act, dy_act, after):
        landed = reduce[w].pair_wait(after)
        half = _dw_half(f"dw_own_{big_names[w]}", x_act, dy_act, geoms[w], c_idx, own=True, addend=landed)
        return reduce[w].chip_start(half)

    tok = dw_start(5, act2, dff2)
    dgu2 = _ffn_dact("ffn2_dact", dff2, w_d2, dact_dgate2, dact_dup2, after=tok)
    tok = dw_finish(5, act2, dff2, after=dgu2)
    tok = dw_start(4, h2, dgu2, after=tok)
    dh2 = _ffn_dh("ffn2_dh", dgu2, w_gu2, after=tok)
    tok = dw_finish(4, h2, dgu2, after=dh2)
    dx2, dg_pre2, dmixed, dg_postm = _norms_bwd("mix_residual_bwd", dy, dh2, x2, pre_norm_ffn2,
                                                post=(mixed, post_norm_mix, 1.0), after=tok)
    tok = dw_start(3, y_mix, dmixed)
    dy_mix = _mm("mix_out_dx", dmixed, w_out, "nt", F32, after=tok)
    tok = dw_finish(3, y_mix, dmixed, after=dy_mix)
    dq_a, dk_pad, dv_pad, dbias, dsink = _attn_bwd("attn_bwd", p, kv_blk0, kvw, attn_probs, attn_sink_probs, dy_mix,
                                                   5 * hw // aw, hw // aw, after=tok)
    tok = reduce[5].chip_finish(dq_a)
    drel_t = _bias_scatter("attn_dbias", dbias.reshape(nah, WINDOW * SPAN), bucket_ids)
    do, dg_h, dgain = _hgrn_out_bwd("hgrn_out_bwd", dy_mix, o_f, o_b, p, hgrn_out_norm, 4, after=tok)
    dq_f, dv_f, dz_f, dlb_f, dq_b, dv_b, dz_b, dlb_b = _hgrn_scan_bwd(
        "hgrn_scan_bwd", p, hgrn_lower_bounds_fwd, hgrn_lower_bounds_bwd, do, st_f, st_b)
    tok = reduce[4].chip_finish(dq_f)
    tok = reduce[3].chip_finish(tok)
    dp = _mix_dproj("mix_dproj", [(dq_f, dq_b), (dv_f, dv_b), (dz_f,), (dz_b,), (dg_h,), (dq_a,)],
                    [dk_pad, dv_pad], t, after=tok)
    tok = dw_start(2, hm, dp)
    dhm = _mm("mix_in_dx", dp, w_in, "nt", F32, after=tok)
    tok = dw_finish(2, hm, dp, after=dhm)
    dx1, dg_prem, dff1, dg_post1 = _norms_bwd("ffn1_residual_bwd", dx2, dhm, x1, pre_norm_mix,
                                              post=(ff1, post_norm_ffn1, 0.5), after=tok)
    tok = dw_start(1, act1, dff1)
    dgu1 = _ffn_dact("ffn1_dact", dff1, w_d1, dact_dgate1, dact_dup1, after=tok)
    tok = dw_finish(1, act1, dff1, after=dgu1)
    tok = reduce[2].chip_finish(tok)
    tok = dw_start(0, h1, dgu1, after=tok)
    done = update(2, tok)
    tok = dw_finish(0, h1, dgu1, after=done)
    dh1 = _ffn_dh("ffn1_dh", dgu1, w_gu1, after=tok)
    grad_x, dg_pre1 = _norms_bwd("ffn1_pre_norm_bwd", dx1, dh1, x0, pre_norm_ffn1)

    small_w = [pre_norm_ffn1, post_norm_ffn1, pre_norm_mix, post_norm_mix, hgrn_lower_bounds_fwd,
               hgrn_lower_bounds_bwd, hgrn_out_norm, attn_sink, pre_norm_ffn2, post_norm_ffn2, rel_bias_table]
    small_m = [m_pre_norm_ffn1, m_post_norm_ffn1, m_pre_norm_mix, m_post_norm_mix, m_hgrn_lower_bounds_fwd,
               m_hgrn_lower_bounds_bwd, m_hgrn_out_norm, m_attn_sink, m_pre_norm_ffn2, m_post_norm_ffn2,
               m_rel_bias_table]
    small_v = [v_pre_norm_ffn1, v_post_norm_ffn1, v_pre_norm_mix, v_post_norm_mix, v_hgrn_lower_bounds_fwd,
               v_hgrn_lower_bounds_bwd, v_hgrn_out_norm, v_attn_sink, v_pre_norm_ffn2, v_post_norm_ffn2,
               v_rel_bias_table]
    small_g = [dg_pre1, dg_post1, dg_prem, dg_postm, dlb_f, dlb_b, dgain, dsink[:, 0].reshape(1, nah), dg_pre2,
               dg_post2, drel_t.T]
    shapes = [a.shape for a in small_w]
    done = update(5, grad_x)
    done = update(4, done)
    done = update(3, done)
    summed = _all_reduce_small(_pack_rows(small_g + [loss_blk[0:1, 0:1]], d), after=done)
    loss = _unpack_rows(summed, shapes + [(1, 1)], d)[-1][0, 0]
    _, sd, sm, sv = _adamw("adamw_small", _pack_rows(small_w, d), summed, _pack_rows(small_m, d),
                           _pack_rows(small_v, d))
    small_grads = _unpack_rows(summed, shapes, d)
    small_delta, small_new_m, small_new_v = (_unpack_rows(a, shapes, d) for a in (sd, sm, sv))

    tok = reduce[1].chip_finish(sd)
    tok = reduce[0].chip_finish(tok)
    done = update(1, tok)
    update(0, done)

    def ordered(small, big):
        s = dict(zip(["pre1", "post1", "prem", "postm", "lbf", "lbb", "gain", "sink", "pre2", "post2", "rel"], small))
        b = dict(zip(["gu1", "d1", "win", "wout", "gu2", "d2"], big))
        return [s["pre1"], s["post1"], b["gu1"], b["d1"], s["prem"], s["postm"], b["win"], s["lbf"], s["lbb"],
                s["gain"], s["sink"], b["wout"], s["pre2"], s["post2"], b["gu2"], b["d2"], s["rel"]]

    return (loss, grad_x[None], *ordered(small_grads, big_grads), *ordered(small_delta, big_delta),
            *ordered(small_new_m, big_new_m), *ordered(small_new_v, big_new_v))
```

```python
import functools
import math

import jax
import jax.numpy as jnp
import numpy as np
from jax import lax
from jax.experimental import pallas as pl
from jax.experimental.pallas import tpu as pltpu

F32 = jnp.float32
BF16 = jnp.bfloat16

HEAD = 128
CHUNK = 64
WINDOW = 128
SPAN = 3 * WINDOW
KV_HEADS = 2
REL_BUCKETS = 32
REL_MAX_DIST = 128
EPS = 1e-6
NEG_INF = -1e30

ADAM_LR = 0.001
ADAM_B1 = 0.9
ADAM_B2 = 0.999
ADAM_EPS = 1e-08
ADAM_WD = 0.01
ADAM_STEP = 10

N_CHIPS = 4
N_DEV = 8
V7X_VMEM_BYTES = 64 * 1024 * 1024
MESH = pl.DeviceIdType.MESH
ANY = pl.BlockSpec(memory_space=pl.ANY)


def _tile(n, pref, mult):
    t = (min(pref, n) // mult) * mult
    while t >= mult:
        if n % t == 0:
            return t
        t -= mult
    return n


def _params(semantics, block_bytes):
    limit = min(V7X_VMEM_BYTES - (4 << 20), 2 * int(block_bytes) + (8 << 20))
    return pltpu.CompilerParams(dimension_semantics=semantics, vmem_limit_bytes=limit)


def _nbytes(shape, dtype):
    return int(np.prod(shape)) * jnp.dtype(dtype).itemsize


PIN_TO_HBM_BYTES = 4 << 20


def _pallas(body, **kw):
    def pin_shape(s):
        if isinstance(s, jax.ShapeDtypeStruct) and _nbytes(s.shape, s.dtype) >= PIN_TO_HBM_BYTES:
            return pltpu.HBM(s.shape, s.dtype)
        return s

    def pin(a):
        if getattr(a, "dtype", None) in (F32, BF16) and _nbytes(a.shape, a.dtype) >= PIN_TO_HBM_BYTES:
            return pltpu.with_memory_space_constraint(a, pltpu.HBM)
        return a

    out_shape = kw["out_shape"]
    kw["out_shape"] = [pin_shape(s) for s in out_shape] if isinstance(out_shape, (list, tuple)) else pin_shape(out_shape)
    call = pl.pallas_call(body, **kw)
    return lambda *args: call(*[pin(a) for a in args])


def _dot(a, b, ca=1, cb=0):
    return lax.dot_general(a, b, (((ca,), (cb,)), ((), ())), preferred_element_type=F32)


def _split3(x):
    hi = x.astype(BF16)
    r1 = x - hi.astype(F32)
    mid = r1.astype(BF16)
    lo = (r1 - mid.astype(F32)).astype(BF16)
    return hi, mid, lo


def _dot_exact(a, b, ca=1, cb=0, split="b"):
    if split == "b":
        return sum(_dot(a, p, ca, cb) for p in _split3(b))
    return sum(_dot(p, b, ca, cb) for p in _split3(a))


def _rms(x):
    return lax.rsqrt(jnp.mean(x * x, axis=-1, keepdims=True) + EPS)


def _norm_bwd(u, x, gain):
    r = _rms(x)
    xhat = x * r
    dgain = jnp.sum(u * xhat, axis=0, keepdims=True)
    v = u * gain
    dx = r * (v - xhat * jnp.mean(v * xhat, axis=-1, keepdims=True))
    return dx, dgain


def _sigmoid(x):
    return 1.0 / (1.0 + jnp.exp(-x))


def _accumulate(ref, val, first):
    @pl.when(first)
    def _():
        ref[...] = val

    @pl.when(jnp.logical_not(first))
    def _():
        ref[...] += val


def _ordered(body, ins, in_specs, after):
    if after is None:
        return body, list(ins), list(in_specs)
    n_in = len(ins)

    def wrapped(*refs):
        body(*refs[:n_in], *refs[n_in + 1:])

    return wrapped, list(ins) + [after], list(in_specs) + [pl.BlockSpec(memory_space=pl.ANY)]


def _matmul(name, a, b, *, form, out_dtype, tm, tn, tk, a_map=None, b_map=None,
            out_shape=None, out_block=None, out_map=None, sizes=None, after=None):
    if sizes is None:
        if form == "nn":
            (m, k), n = a.shape, b.shape[1]
        elif form == "nt":
            (m, k), n = a.shape, b.shape[0]
        else:
            (k, m), n = a.shape, b.shape[1]
    else:
        m, n, k = sizes
    gi, gj, gk = m // tm, n // tn, k // tk
    a_blk = (tm, tk) if form != "tn" else (tk, tm)
    b_blk = (tk, tn) if form != "nt" else (tn, tk)
    if a_map is None:
        a_map = (lambda i, j, kk: (i, kk)) if form != "tn" else (lambda i, j, kk: (kk, i))
    else:
        a_blk = (None,) + a_blk
    if b_map is None:
        b_map = (lambda i, j, kk: (kk, j)) if form != "nt" else (lambda i, j, kk: (j, kk))
    else:
        b_blk = (None,) + b_blk
    if out_shape is None:
        out_shape, out_block, out_map = (m, n), (tm, tn), (lambda i, j, kk: (i, j))
    ca, cb = {"nn": (1, 0), "nt": (1, 1), "tn": (0, 0)}[form]

    def body(a_ref, b_ref, o_ref, *acc):
        part = _dot(a_ref[...], b_ref[...], ca, cb)
        if gk == 1:
            o_ref[...] = part.astype(o_ref.dtype)
        else:
            kk = pl.program_id(2)
            _accumulate(acc[0], part, kk == 0)

            @pl.when(kk == gk - 1)
            def _():
                o_ref[...] = acc[0][...].astype(o_ref.dtype)

    scratch = [] if gk == 1 else [pltpu.VMEM((tm, tn), F32)]
    vmem = (_nbytes((tm, tk), a.dtype) + _nbytes((tk, tn), b.dtype) + _nbytes((tm, tn), out_dtype)
            + 2 * _nbytes((tm, tn), F32))
    body, ins, in_specs = _ordered(body, [a, b], [pl.BlockSpec(a_blk, a_map), pl.BlockSpec(b_blk, b_map)], after)
    return _pallas(
        body, name=name, grid=(gi, gj, gk), in_specs=in_specs,
        out_specs=pl.BlockSpec(out_block, out_map),
        out_shape=jax.ShapeDtypeStruct(out_shape, out_dtype),
        scratch_shapes=scratch,
        compiler_params=_params(("parallel", "parallel", "arbitrary"), vmem),
    )(*ins)


V7X_HBM_BYTES_PER_US = 3.0e6
V7X_MXU_FLOPS_PER_US = 0.9e9
V7X_VMEM_RMW_BYTES_PER_US = 10e6
GRID_STEP_US = 0.35
MATMUL_VMEM_BUDGET = 40 << 20
MATMUL_MAX_TILE_FLOPS = 1 << 33


def _divisors(n, mult, lo):
    return [t for t in range(mult, n + 1, mult) if n % t == 0 and t >= min(lo, n)]


def _mm_tiles(m, n, k, out_dtype=F32, n_unit=None, k_unit=None):
    out_bytes = jnp.dtype(out_dtype).itemsize
    best = None
    for tm in _divisors(m, 128, 256):
        for tn in _divisors(n_unit or n, 128, 256):
            for tk in _divisors(k_unit or k, 128, 512):
                gi, gj, gk = m // tm, n // tn, k // tk
                vmem = 4 * tm * tk + 4 * tk * tn + 2 * tm * tn * out_bytes + 4 * tm * tn * (2 if gk > 1 else 1)
                if vmem > MATMUL_VMEM_BUDGET or 2 * tm * tn * tk > MATMUL_MAX_TILE_FLOPS:
                    continue
                a_bytes = 2 * m * k * (gj if gk > 1 else 1)
                b_bytes = 2 * k * n * (1 if gj == 1 and gk == 1 else gi)
                hbm_us = (a_bytes + b_bytes + m * n * out_bytes) / V7X_HBM_BYTES_PER_US
                acc_us = (8 * m * n * gk / V7X_VMEM_RMW_BYTES_PER_US) if gk > 1 else 0.0
                cost = max(2 * m * n * k / V7X_MXU_FLOPS_PER_US, 1.3 * hbm_us) + GRID_STEP_US * gi * gj * gk + acc_us
                key = (round(cost, 1), vmem)
                if best is None or key < best[0]:
                    best = (key, (tm, tn, tk))
    return best[1]


def _mm(name, a, b, form, out_dtype, after=None):
    if form == "nn":
        m, k, n = a.shape[0], a.shape[1], b.shape[1]
    elif form == "nt":
        m, k, n = a.shape[0], a.shape[1], b.shape[0]
    else:
        m, k, n = a.shape[1], a.shape[0], b.shape[1]
    tm, tn, tk = _mm_tiles(m, n, k, out_dtype)
    return _matmul(name, a, b, form=form, out_dtype=out_dtype, tm=tm, tn=tn, tk=tk, after=after)


def _row_tile(t):
    return _tile(t, 256, 8)


def _norm_fwd(name, x, gain):
    t, d = x.shape
    tm = _row_tile(t)

    def body(x_ref, g_ref, h_ref):
        xv = x_ref[...]
        h_ref[...] = (xv * _rms(xv) * g_ref[...]).astype(BF16)

    row = pl.BlockSpec((tm, d), lambda i: (i, 0))
    vec = pl.BlockSpec((1, d), lambda i: (0, 0))
    return _pallas(
        body, name=name, grid=(t // tm,), in_specs=[row, vec], out_specs=row,
        out_shape=jax.ShapeDtypeStruct((t, d), BF16),
        compiler_params=_params(("parallel",), 2 * _nbytes((tm, d), F32)),
    )(x, gain)


def _resid_norm_fwd(name, xres, ff, gpost, gpre, scale):
    t, d = xres.shape
    tm = _row_tile(t)

    def body(x_ref, f_ref, gp_ref, gn_ref, xn_ref, h_ref):
        f = f_ref[...]
        xn = x_ref[...] + scale * (f * _rms(f) * gp_ref[...])
        xn_ref[...] = xn
        h_ref[...] = (xn * _rms(xn) * gn_ref[...]).astype(BF16)

    row = pl.BlockSpec((tm, d), lambda i: (i, 0))
    vec = pl.BlockSpec((1, d), lambda i: (0, 0))
    return _pallas(
        body, name=name, grid=(t // tm,), in_specs=[row, row, vec, vec], out_specs=[row, row],
        out_shape=[jax.ShapeDtypeStruct((t, d), F32), jax.ShapeDtypeStruct((t, d), BF16)],
        compiler_params=_params(("parallel",), 4 * _nbytes((tm, d), F32)),
    )(xres, ff, gpost, gpre)


def _final_fwd_bwd(name, xres, ff, gpost, target, scale):
    t, d = xres.shape
    tm = _row_tile(t)

    def body(x_ref, f_ref, gp_ref, t_ref, loss_ref, dy_ref, dff_ref, dg_ref):
        i = pl.program_id(0)
        f = f_ref[...]
        gp = gp_ref[...]
        y = x_ref[...] + scale * (f * _rms(f) * gp)
        err = y - t_ref[...]
        part = 0.5 * jnp.sum(jnp.mean(err * err, axis=-1, keepdims=True), axis=0, keepdims=True)
        _accumulate(loss_ref, jnp.broadcast_to(part, loss_ref.shape), i == 0)
        dy = err / d
        dy_ref[...] = dy
        dff, dg = _norm_bwd(scale * dy, f, gp)
        dff_ref[...] = dff.astype(BF16)
        _accumulate(dg_ref, dg, i == 0)

    row = pl.BlockSpec((tm, d), lambda i: (i, 0))
    vec = pl.BlockSpec((1, d), lambda i: (0, 0))
    return _pallas(
        body, name=name, grid=(t // tm,), in_specs=[row, row, vec, row],
        out_specs=[pl.BlockSpec((8, 128), lambda i: (0, 0)), row, row, vec],
        out_shape=[jax.ShapeDtypeStruct((8, 128), F32), jax.ShapeDtypeStruct((t, d), F32),
                   jax.ShapeDtypeStruct((t, d), BF16), jax.ShapeDtypeStruct((1, d), F32)],
        compiler_params=_params(("arbitrary",), 5 * _nbytes((tm, d), F32)),
    )(xres, ff, gpost, target)


def _norms_bwd(name, dres, dh, xin, gpre, post=None, after=None):
    t, d = dres.shape
    tm = _row_tile(t)
    with_post = post is not None

    def body(*refs):
        if with_post:
            dr_ref, dh_ref, x_ref, g_ref, f_ref, gp_ref, dx_ref, dg_ref, dff_ref, dgp_ref = refs
        else:
            dr_ref, dh_ref, x_ref, g_ref, dx_ref, dg_ref = refs
        i = pl.program_id(0)
        dx, dg = _norm_bwd(dh_ref[...], x_ref[...], g_ref[...])
        dx = dr_ref[...] + dx
        dx_ref[...] = dx
        _accumulate(dg_ref, dg, i == 0)
        if with_post:
            dff, dgp = _norm_bwd(post[2] * dx, f_ref[...], gp_ref[...])
            dff_ref[...] = dff.astype(BF16)
            _accumulate(dgp_ref, dgp, i == 0)

    row = pl.BlockSpec((tm, d), lambda i: (i, 0))
    vec = pl.BlockSpec((1, d), lambda i: (0, 0))
    ins, in_specs = [dres, dh, xin, gpre], [row, row, row, vec]
    out_specs = [row, vec]
    out_shape = [jax.ShapeDtypeStruct((t, d), F32), jax.ShapeDtypeStruct((1, d), F32)]
    if with_post:
        ins += [post[0], post[1]]
        in_specs += [row, vec]
        out_specs += [row, vec]
        out_shape += [jax.ShapeDtypeStruct((t, d), BF16), jax.ShapeDtypeStruct((1, d), F32)]
    body, ins, in_specs = _ordered(body, ins, in_specs, after)
    return _pallas(
        body, name=name, grid=(t // tm,), in_specs=in_specs, out_specs=out_specs, out_shape=out_shape,
        compiler_params=_params(("arbitrary",), 6 * _nbytes((tm, d), F32)),
    )(*ins)


SWIGLU_TILE = (1024, 512)
DACT_TILE = (2048, 512)


def _ffn_gate_up_act(name, h, w_gu):
    t, d = h.shape
    f = w_gu.shape[1] // 2
    tm, tn = _tile(t, SWIGLU_TILE[0], 128), _tile(f, SWIGLU_TILE[1], 128)
    nf = f // tn

    def body(h_ref, wg_ref, wu_ref, a_ref, dg_ref, du_ref):
        hv = h_ref[...]
        g = _dot(hv, wg_ref[...])
        u = _dot(hv, wu_ref[...])
        sig = _sigmoid(g)
        silu = g * sig
        a_ref[...] = (silu * u).astype(BF16)
        dg_ref[...] = (u * sig * (1.0 + g * (1.0 - sig))).astype(BF16)
        du_ref[...] = silu.astype(BF16)

    out = jax.ShapeDtypeStruct((t, f), BF16)
    blk = pl.BlockSpec((tm, tn), lambda i, j: (i, j))
    return _pallas(
        body, name=name, grid=(t // tm, nf),
        in_specs=[pl.BlockSpec((tm, d), lambda i, j: (i, 0)), pl.BlockSpec((d, tn), lambda i, j: (0, j)),
                  pl.BlockSpec((d, tn), lambda i, j: (0, j + nf))],
        out_specs=[blk, blk, blk], out_shape=[out, out, out],
        compiler_params=_params(("parallel", "parallel"),
                                _nbytes((tm, d), BF16) + 2 * _nbytes((d, tn), BF16) + 5 * _nbytes((tm, tn), F32)),
    )(h, w_gu, w_gu)


def _ffn_dact(name, dff, w_down, dact_dgate, dact_dup, after=None):
    t, d = dff.shape
    f = w_down.shape[0]
    tm, tn = _tile(t, DACT_TILE[0], 128), _tile(f, DACT_TILE[1], 128)

    def body(d_ref, w_ref, dg_ref, du_ref, o_ref):
        da = _dot(d_ref[...], w_ref[...], 1, 1)
        o_ref[0] = (da * dg_ref[...].astype(F32)).astype(BF16)
        o_ref[1] = (da * du_ref[...].astype(F32)).astype(BF16)

    blk = pl.BlockSpec((tm, tn), lambda i, j: (i, j))
    body, ins, in_specs = _ordered(
        body, [dff, w_down, dact_dgate, dact_dup],
        [pl.BlockSpec((tm, d), lambda i, j: (i, 0)), pl.BlockSpec((tn, d), lambda i, j: (j, 0)), blk, blk], after)
    return _pallas(
        body, name=name, grid=(t // tm, f // tn), in_specs=in_specs,
        out_specs=pl.BlockSpec((2, tm, tn), lambda i, j: (0, i, j)),
        out_shape=jax.ShapeDtypeStruct((2, t, f), BF16),
        compiler_params=_params(("parallel", "parallel"),
                                _nbytes((tm, d), BF16) + _nbytes((tn, d), BF16) + 5 * _nbytes((tm, tn), F32)),
    )(*ins)


def _ffn_dh(name, dgu, w_gu, after=None):
    _, t, f = dgu.shape
    d = w_gu.shape[0]
    tm, tn, tk = _mm_tiles(t, d, 2 * f, F32, k_unit=f)
    nkf = f // tk
    return _matmul(name, dgu, w_gu, form="nt", out_dtype=F32, tm=tm, tn=tn, tk=tk, sizes=(t, d, 2 * f),
                   a_map=lambda i, j, kk: (kk // nkf, i, kk % nkf), after=after)


def _lower_bound(lbp):
    m = jnp.max(lbp, axis=0, keepdims=True)
    e = jnp.exp(lbp - m)
    return e[0:1] / jnp.sum(e, axis=0, keepdims=True)


def _chunk_mask(reverse):
    row = lax.broadcasted_iota(jnp.int32, (CHUNK, CHUNK), 0)
    col = lax.broadcasted_iota(jnp.int32, (CHUNK, CHUNK), 1)
    return (col >= row) if reverse else (col <= row)


def _hgrn_gates(z, lb, mask_bf):
    sig = _sigmoid(z)
    f = lb + (1.0 - lb) * sig
    logf = jnp.log(f)
    k = 1.0 - f
    cum = _dot_exact(mask_bf, logf)
    last = jnp.sum(logf, axis=0, keepdims=True)
    return sig, f, k, cum, last


def _hgrn_scan_fwd(name, p, lbp_f, lbp_b):
    t = p.shape[0]
    hw = lbp_f.shape[1]
    nh, nc = hw // HEAD, t // CHUNK

    def body(qf, vf, zf, qb, vb, zb, lbf, lbb, of_ref, ob_ref, stf_ref, stb_ref, state):
        n = pl.program_id(0)

        @pl.when(n == 0)
        def _():
            state[...] = jnp.zeros_like(state)

        directions = [(qf, vf, zf, lbf, of_ref, stf_ref), (qb, vb, zb, lbb, ob_ref, stb_ref)]
        wide = []
        for d, (q_ref, v_ref, z_ref, lb_ref, o_ref, st_ref) in enumerate(directions):
            mask = _chunk_mask(d == 1)
            lb = _lower_bound(lb_ref[...])
            _, _, k, cum, last = _hgrn_gates(z_ref[...], lb, mask.astype(BF16))
            v = v_ref[...].astype(BF16)
            qd = (q_ref[...] * jnp.exp(cum)).astype(BF16)
            kd = (k * jnp.exp(-cum)).astype(BF16)
            kt = (k * jnp.exp(last - cum)).astype(BF16)
            s_all = state[d]
            st_ref[...] = s_all
            wide.append((mask, v, qd, kd, kt, jnp.exp(last), s_all, o_ref))
        pairs = [(d, slice(h * HEAD, (h + 1) * HEAD)) for d in range(2) for h in range(nh)]
        a = [jnp.where(wide[d][0], _dot(wide[d][2][:, sl], wide[d][3][:, sl], 1, 1), 0.0).astype(BF16)
             for d, sl in pairs]
        inter = [_dot(wide[d][2][:, sl], wide[d][6][:, sl].astype(BF16), 1, 1) for d, sl in pairs]
        intra = [_dot(a[i], wide[d][1][:, sl]) for i, (d, sl) in enumerate(pairs)]
        grow = [_dot(wide[d][1][:, sl], wide[d][4][:, sl], 0, 0) for d, sl in pairs]
        for i, (d, sl) in enumerate(pairs):
            wide[d][7][:, sl] = intra[i] + inter[i]
            state[d, :, sl] = wide[d][6][:, sl] * wide[d][5][:, sl] + grow[i]

    def col(group, reverse):
        return pl.BlockSpec((CHUNK, hw), lambda n: ((nc - 1 - n) if reverse else n, group))

    def st(reverse):
        return pl.BlockSpec((None, HEAD, hw), lambda n: ((nc - 1 - n) if reverse else n, 0, 0))

    lb_spec = pl.BlockSpec((2, hw), lambda n: (0, 0))
    out = jax.ShapeDtypeStruct((t, hw), F32)
    states = jax.ShapeDtypeStruct((nc, HEAD, hw), F32)
    return _pallas(
        body, name=name, grid=(nc,),
        in_specs=[col(0, False), col(1, False), col(2, False), col(0, True), col(1, True), col(3, True),
                  lb_spec, lb_spec],
        out_specs=[col(0, False), col(0, True), st(False), st(True)],
        out_shape=[out, out, states, states],
        scratch_shapes=[pltpu.VMEM((2, HEAD, hw), F32)],
        compiler_params=_params(("arbitrary",), 12 * _nbytes((HEAD, hw), F32)),
    )(p, p, p, p, p, p, lbp_f, lbp_b)


def _hgrn_scan_bwd(name, p, lbp_f, lbp_b, do, st_f, st_b):
    t = p.shape[0]
    hw = lbp_f.shape[1]
    nh, nc = hw // HEAD, t // CHUNK

    def body(qf, vf, zf, dof, sf, qb, vb, zb, dob, sb, lbf, lbb, dqf, dvf, dzf, dlbf, dqb, dvb, dzb, dlbb,
             dstate, dlb_acc, dqd_s, dkd_s, dkt_s, ddec_s):
        n = pl.program_id(0)

        @pl.when(n == 0)
        def _():
            dstate[...] = jnp.zeros_like(dstate)
            dlb_acc[...] = jnp.zeros_like(dlb_acc)

        directions = [(qf, vf, zf, dof, sf, lbf, dqf, dvf, dzf, dlbf), (qb, vb, zb, dob, sb, lbb, dqb, dvb, dzb, dlbb)]
        for d, (q_ref, v_ref, z_ref, do_ref, st_ref, lb_ref, dq_ref, dv_ref, dz_ref, dlb_ref) in enumerate(directions):
            mask = _chunk_mask(d == 1)
            mask_bf = mask.astype(BF16)
            lb = _lower_bound(lb_ref[...])
            sig, f, k, cum, last = _hgrn_gates(z_ref[...], lb, mask_bf)
            e_pos, e_neg, e_tail = jnp.exp(cum), jnp.exp(-cum), jnp.exp(last - cum)
            dec = jnp.exp(last)
            v = v_ref[...].astype(BF16)
            qd, kd, kt = q_ref[...] * e_pos, k * e_neg, k * e_tail
            qd_bf, kd_bf, kt_bf = qd.astype(BF16), kd.astype(BF16), kt.astype(BF16)
            s_all = st_ref[...]
            ds_all = dstate[d]
            dov = do_ref[...].astype(BF16)
            cols = [slice(h * HEAD, (h + 1) * HEAD) for h in range(nh)]
            s_bf = [s_all[:, sl].astype(BF16) for sl in cols]
            ds_bf = [ds_all[:, sl].astype(BF16) for sl in cols]
            a = [jnp.where(mask, _dot(qd_bf[:, sl], kd_bf[:, sl], 1, 1), 0.0).astype(BF16) for sl in cols]
            da = [jnp.where(mask, _dot(dov[:, sl], v[:, sl], 1, 1), 0.0).astype(BF16) for sl in cols]
            dv_h = [_dot(a[h], dov[:, sl], 0, 0) + _dot(kt_bf[:, sl], ds_bf[h], 1, 1) for h, sl in enumerate(cols)]
            dqd_h = [_dot(da[h], kd_bf[:, sl]) + _dot(dov[:, sl], s_bf[h]) for h, sl in enumerate(cols)]
            dkd_h = [_dot(da[h], qd_bf[:, sl], 0, 0) for h, sl in enumerate(cols)]
            dkt_h = [_dot(v[:, sl], ds_bf[h]) for h, sl in enumerate(cols)]
            dst_h = [_dot(dov[:, sl], qd_bf[:, sl], 0, 0) + ds_all[:, sl] * dec[:, sl] for sl in cols]
            for h, sl in enumerate(cols):
                dv_ref[:, sl] = dv_h[h]
                dqd_s[:, sl] = dqd_h[h]
                dkd_s[:, sl] = dkd_h[h]
                dkt_s[:, sl] = dkt_h[h]
                dstate[d, :, sl] = dst_h[h]
                ddec_s[:, sl] = jnp.sum(ds_all[:, sl] * s_all[:, sl], axis=0, keepdims=True)
            dqd, dkd, dkt = dqd_s[...], dkd_s[...], dkt_s[...]
            dlast = jnp.sum(dkt * kt, axis=0, keepdims=True) + dec * ddec_s[...]
            dq_ref[...] = dqd * e_pos
            dk = dkd * e_neg + dkt * e_tail
            dcum = dqd * qd - dkd * kd - dkt * kt
            dlogf = _dot_exact(mask_bf, dcum, 0, 0) + dlast
            df = dlogf / f - dk
            dz_ref[...] = df * (1.0 - lb) * sig * (1.0 - sig)
            dlb_acc[d] += jnp.sum(df * (1.0 - sig), axis=0, keepdims=True)

            @pl.when(n == nc - 1)
            def _():
                g = dlb_acc[d] * lb * (1.0 - lb)
                dlb_ref[0:1, :] = g
                dlb_ref[1:2, :] = -g

    def col(group, reverse):
        return pl.BlockSpec((CHUNK, hw), lambda n: (n if reverse else (nc - 1 - n), group))

    def st(reverse):
        return pl.BlockSpec((None, HEAD, hw), lambda n: (n if reverse else (nc - 1 - n), 0, 0))

    lb_spec = pl.BlockSpec((2, hw), lambda n: (0, 0))
    out = jax.ShapeDtypeStruct((t, hw), F32)
    dlb = jax.ShapeDtypeStruct((2, hw), F32)
    wide = pltpu.VMEM((CHUNK, hw), F32)
    return _pallas(
        body, name=name, grid=(nc,),
        in_specs=[col(0, False), col(1, False), col(2, False), col(0, False), st(False),
                  col(0, True), col(1, True), col(3, True), col(0, True), st(True), lb_spec, lb_spec],
        out_specs=[col(0, False), col(0, False), col(0, False), lb_spec,
                   col(0, True), col(0, True), col(0, True), lb_spec],
        out_shape=[out, out, out, dlb, out, out, out, dlb],
        scratch_shapes=[pltpu.VMEM((2, HEAD, hw), F32), pltpu.VMEM((2, 1, hw), F32), wide, wide, wide,
                        pltpu.VMEM((1, hw), F32)],
        compiler_params=_params(("arbitrary",), 16 * _nbytes((HEAD, hw), F32)),
    )(p, p, p, do, st_f, p, p, p, do, st_b, lbp_f, lbp_b)


def _hgrn_out_fwd(name, o_f, o_b, p, gain, g_group):
    t, hw = o_f.shape
    nh = hw // HEAD
    tm = _tile(t, 256, 16)

    def body(of_ref, ob_ref, g_ref, gain_ref, y_ref):
        o_all = of_ref[...] + ob_ref[...]
        g_all = g_ref[...]
        scale_all = gain_ref[...] * (g_all * _sigmoid(g_all))
        for h in range(nh):
            sl = slice(h * HEAD, (h + 1) * HEAD)
            o = o_all[:, sl]
            y_ref[:, sl] = (o * _rms(o) * scale_all[:, sl]).astype(BF16)

    blk = pl.BlockSpec((tm, hw), lambda i: (i, 0))
    return _pallas(
        body, name=name, grid=(t // tm,),
        in_specs=[blk, blk, pl.BlockSpec((tm, hw), lambda i: (i, g_group)), pl.BlockSpec((1, hw), lambda i: (0, 0))],
        out_specs=blk, out_shape=jax.ShapeDtypeStruct((t, hw), BF16),
        compiler_params=_params(("parallel",), 5 * _nbytes((tm, hw), F32)),
    )(o_f, o_b, p, gain)


def _hgrn_out_bwd(name, dy, o_f, o_b, p, gain, g_group, after=None):
    t, hw = o_f.shape
    nh = hw // HEAD
    tm = _tile(t, 256, 8)

    def body(dy_ref, of_ref, ob_ref, g_ref, gain_ref, do_ref, dg_ref, dgain_ref):
        i = pl.program_id(0)
        o_all = of_ref[...] + ob_ref[...]
        g_all = g_ref[...]
        sig_all = _sigmoid(g_all)
        dy_all = dy_ref[...]
        up_all = dy_all * (g_all * sig_all)
        dsilu_all = dy_all * sig_all * (1.0 + g_all * (1.0 - sig_all))
        gain_all = gain_ref[...]
        for h in range(nh):
            sl = slice(h * HEAD, (h + 1) * HEAD)
            o, gain_v = o_all[:, sl], gain_all[:, sl]
            do, dgain = _norm_bwd(up_all[:, sl], o, gain_v)
            do_ref[:, sl] = do
            dg_ref[:, sl] = dsilu_all[:, sl] * (o * _rms(o) * gain_v)
            _accumulate(dgain_ref.at[:, sl], dgain, i == 0)

    blk = pl.BlockSpec((tm, hw), lambda i: (i, 0))
    vec = pl.BlockSpec((1, hw), lambda i: (0, 0))
    out = jax.ShapeDtypeStruct((t, hw), F32)
    body, ins, in_specs = _ordered(
        body, [dy, o_f, o_b, p, gain], [blk, blk, blk, pl.BlockSpec((tm, hw), lambda i: (i, g_group)), vec], after)
    return _pallas(
        body, name=name, grid=(t // tm,), in_specs=in_specs,
        out_specs=[blk, blk, vec], out_shape=[out, out, jax.ShapeDtypeStruct((1, hw), F32)],
        compiler_params=_params(("arbitrary",), 7 * _nbytes((tm, hw), F32)),
    )(*ins)


def _t5_bucket_ids():
    c = np.arange(WINDOW)[:, None]
    s = np.arange(SPAN)[None, :]
    rel = s - WINDOW - c
    nb = REL_BUCKETS // 2
    max_exact = nb // 2
    bucket = (rel > 0).astype(np.int32) * nb
    n = np.abs(rel)
    large = max_exact + (np.log(np.maximum(n, 1) / max_exact) / np.log(REL_MAX_DIST / max_exact)
                         * (nb - max_exact)).astype(np.int32)
    large = np.minimum(large, nb - 1)
    ids = bucket + np.where(n < max_exact, n, large).astype(np.int32)
    return jnp.asarray(ids.reshape(1, WINDOW * SPAN), jnp.int32)


def _bias_onehot(ids_ref):
    n = ids_ref.shape[1]
    return (lax.broadcasted_iota(jnp.int32, (REL_BUCKETS, n), 0) == ids_ref[...]).astype(BF16)


def _bias_gather(name, table_t, ids):
    nh = table_t.shape[0]

    def body(t_ref, ids_ref, o_ref):
        o_ref[...] = _dot_exact(t_ref[...], _bias_onehot(ids_ref), split="a")

    return _pallas(
        body, name=name, out_shape=jax.ShapeDtypeStruct((nh, ids.shape[1]), F32),
        compiler_params=pltpu.CompilerParams(vmem_limit_bytes=32 << 20),
    )(table_t, ids)


def _bias_scatter(name, dbias, ids):
    nh = dbias.shape[0]

    def body(d_ref, ids_ref, o_ref):
        o_ref[...] = _dot_exact(d_ref[...], _bias_onehot(ids_ref), 1, 1, split="a")

    return _pallas(
        body, name=name, out_shape=jax.ShapeDtypeStruct((nh, REL_BUCKETS), F32),
        compiler_params=pltpu.CompilerParams(vmem_limit_bytes=32 << 20),
    )(dbias, ids)


def _attn_valid(i, t):
    c = lax.broadcasted_iota(jnp.int32, (WINDOW, SPAN), 0)
    s = lax.broadcasted_iota(jnp.int32, (WINDOW, SPAN), 1)
    rel = s - WINDOW - c
    pos = i * WINDOW - WINDOW + s
    return (jnp.abs(rel) <= WINDOW) & (pos >= 0) & (pos < t)


def _attn_probs(qs, khs, b_ref, s_ref, valid):
    heads = range(len(qs))
    sinks = [s_ref[0:1, h:h + 1] for h in heads]
    s = [_dot(qs[h], khs[h], 1, 1) / math.sqrt(HEAD) for h in heads]
    s = [jnp.where(valid, s[h] + b_ref[h], NEG_INF) for h in heads]
    m = [jnp.maximum(jnp.max(s[h], axis=-1, keepdims=True), sinks[h]) for h in heads]
    e = [jnp.exp(s[h] - m[h]) for h in heads]
    es = [jnp.exp(sinks[h] - m[h]) for h in heads]
    inv = [1.0 / (jnp.sum(e[h], axis=-1, keepdims=True) + es[h]) for h in heads]
    return [e[h] * inv[h] for h in heads], [es[h] * inv[h] for h in heads]


def _kv_window_specs(kv_blk, kvw, nb):
    return [pl.BlockSpec((WINDOW, kvw), lambda i, s=s: (jnp.clip(i + s, 0, nb - 1), kv_blk)) for s in (-1, 0, 1)]


def _kv_window(refs):
    return jnp.concatenate([r[...] for r in refs], axis=0).astype(BF16)


def _attn_fwd(name, p, kv_blk, kvw, bias, sink, q_group_blk):
    t = p.shape[0]
    nh = bias.shape[0]
    aw = nh * HEAD
    grp = nh // KV_HEADS
    nb = t // WINDOW

    def body(q_ref, kp, kc, kn, vp, vc, vn, b_ref, s_ref, y_ref, pr_ref, ps_ref):
        i = pl.program_id(0)
        valid = _attn_valid(i, t)
        ks = _kv_window((kp, kc, kn))
        vs = _kv_window((vp, vc, vn))
        heads = range(nh)
        col = lambda h: slice(h * HEAD, (h + 1) * HEAD)
        qs = [q_ref[:, col(h)].astype(BF16) for h in heads]
        pr, ps = _attn_probs(qs, [ks[:, col(h // grp)] for h in heads], b_ref, s_ref, valid)
        pr = [pr[h].astype(BF16) for h in heads]
        out = [_dot(pr[h], vs[:, col(h // grp)]) for h in heads]
        lane = lax.broadcasted_iota(jnp.int32, (WINDOW, 128), 1)
        sinks = jnp.zeros((WINDOW, 128), F32)
        for h in heads:
            y_ref[:, col(h)] = out[h].astype(BF16)
            pr_ref[h] = pr[h]
            sinks = jnp.where(lane == h, ps[h], sinks)
        ps_ref[...] = sinks

    full = lambda a: pl.BlockSpec(a.shape, lambda i: (0,) * a.ndim)
    return _pallas(
        body, name=name, grid=(nb,),
        in_specs=[pl.BlockSpec((WINDOW, aw), lambda i: (i, q_group_blk)), *_kv_window_specs(kv_blk, kvw, nb),
                  *_kv_window_specs(kv_blk + 1, kvw, nb), full(bias), full(sink)],
        out_specs=[pl.BlockSpec((WINDOW, aw), lambda i: (i, 0)), pl.BlockSpec((nh, WINDOW, SPAN), lambda i: (0, i, 0)),
                   pl.BlockSpec((WINDOW, 128), lambda i: (i, 0))],
        out_shape=[jax.ShapeDtypeStruct((t, aw), BF16), jax.ShapeDtypeStruct((nh, t, SPAN), BF16),
                   jax.ShapeDtypeStruct((t, 128), F32)],
        compiler_params=_params(("parallel",), 3 * _nbytes(bias.shape, F32)),
    )(p, p, p, p, p, p, p, bias, sink)


def _attn_bwd(name, p, kv_blk, kvw, probs, sink_probs, dy, q_group_blk, dy_blk, after=None):
    t = p.shape[0]
    nh = probs.shape[0]
    aw = nh * HEAD
    grp = nh // KV_HEADS
    nb = t // WINDOW

    def body(q_ref, kp, kc, kn, vp, vc, vn, pr_ref, ps_ref, dy_ref, dq_ref, dk_ref, dv_ref, db_ref, ds_ref):
        i = pl.program_id(0)

        @pl.when(i == 0)
        def _():
            dk_ref[...] = jnp.zeros_like(dk_ref)
            dv_ref[...] = jnp.zeros_like(dv_ref)
            db_ref[...] = jnp.zeros_like(db_ref)
            ds_ref[...] = jnp.zeros_like(ds_ref)

        start = pl.multiple_of(i * WINDOW, WINDOW)
        ks = _kv_window((kp, kc, kn))
        vs = _kv_window((vp, vc, vn))
        inv_sqrt = 1.0 / math.sqrt(HEAD)
        heads = range(nh)
        col = lambda h: slice(h * HEAD, (h + 1) * HEAD)
        qs = [q_ref[:, col(h)].astype(BF16) for h in heads]
        khs = [ks[:, col(h // grp)] for h in heads]
        pr_bf = [pr_ref[h] for h in heads]
        pr = [pr_bf[h].astype(F32) for h in heads]
        dos = [dy_ref[:, col(h)].astype(BF16) for h in heads]
        dp = [_dot(dos[h], vs[:, col(h // grp)], 1, 1) for h in heads]
        delta = [jnp.sum(pr[h] * dp[h], axis=-1, keepdims=True) for h in heads]
        dsc = [pr[h] * (dp[h] - delta[h]) for h in heads]
        dsr = [(dsc[h] * inv_sqrt).astype(BF16) for h in heads]
        dq = [_dot(dsr[h], khs[h]) for h in heads]
        dk = [_dot(dsr[h], qs[h], 0, 0) for h in heads]
        dv = [_dot(pr_bf[h], dos[h], 0, 0) for h in heads]
        for h in heads:
            db_ref[h] += dsc[h]
            dsink = jnp.sum(-ps_ref[:, h:h + 1] * delta[h], axis=0, keepdims=True)
            ds_ref[h:h + 1, :] += jnp.broadcast_to(dsink, (1, 128))
            dq_ref[:, col(h)] = dq[h]
        for kv in range(KV_HEADS):
            group = range(kv * grp, (kv + 1) * grp)
            dk_ref[pl.ds(start, SPAN), col(kv)] += sum(dk[h] for h in group)
            dv_ref[pl.ds(start, SPAN), col(kv)] += sum(dv[h] for h in group)

    whole = lambda shape: pl.BlockSpec(shape, lambda i: (0,) * len(shape))
    pad_shape = (t + 2 * WINDOW, kvw)
    bias_shape = (nh, WINDOW, SPAN)
    body, ins, in_specs = _ordered(
        body, [p, p, p, p, p, p, p, probs, sink_probs, dy],
        [pl.BlockSpec((WINDOW, aw), lambda i: (i, q_group_blk)), *_kv_window_specs(kv_blk, kvw, nb),
         *_kv_window_specs(kv_blk + 1, kvw, nb),
         pl.BlockSpec((nh, WINDOW, SPAN), lambda i: (0, i, 0)), pl.BlockSpec((WINDOW, 128), lambda i: (i, 0)),
         pl.BlockSpec((WINDOW, aw), lambda i: (i, dy_blk))], after)
    return _pallas(
        body, name=name, grid=(nb,), in_specs=in_specs,
        out_specs=[pl.BlockSpec((WINDOW, aw), lambda i: (i, 0)), whole(pad_shape), whole(pad_shape),
                   whole(bias_shape), whole((nh, 128))],
        out_shape=[jax.ShapeDtypeStruct((t, aw), F32), jax.ShapeDtypeStruct(pad_shape, F32),
                   jax.ShapeDtypeStruct(pad_shape, F32), jax.ShapeDtypeStruct(bias_shape, F32),
                   jax.ShapeDtypeStruct((nh, 128), F32)],
        compiler_params=_params(("arbitrary",), 3 * _nbytes(pad_shape, F32) + 3 * _nbytes(bias_shape, F32)),
    )(*ins)


def _mix_dproj(name, pieces, kv_pads, t, after=None):
    hw = pieces[0][0].shape[1]
    kvw = kv_pads[0].shape[1]
    widths = [hw] * len(pieces) + [kvw] * len(kv_pads)
    total = sum(widths)
    tm = WINDOW
    flat = [a for pc in pieces for a in pc]

    def body(*refs):
        o_ref = refs[-1]
        pos, off = 0, 0
        for pc in pieces:
            val = refs[pos][...]
            for extra in range(1, len(pc)):
                val = val + refs[pos + extra][...]
            o_ref[:, off:off + hw] = val.astype(BF16)
            pos += len(pc)
            off += hw
        for _ in kv_pads:
            o_ref[:, off:off + kvw] = refs[pos][...].astype(BF16)
            pos += 1
            off += kvw

    in_specs = [pl.BlockSpec((tm, hw), lambda i: (i, 0)) for _ in flat]
    in_specs += [pl.BlockSpec((tm, kvw), lambda i: (i + 1, 0)) for _ in kv_pads]
    body, ins, in_specs = _ordered(body, [*flat, *kv_pads], in_specs, after)
    return _pallas(
        body, name=name, grid=(t // tm,), in_specs=in_specs,
        out_specs=pl.BlockSpec((tm, total), lambda i: (i, 0)),
        out_shape=jax.ShapeDtypeStruct((t, total), BF16),
        compiler_params=_params(("parallel",), 3 * _nbytes((tm, total), F32)),
    )(*ins)


def _concat_cols(name, a, b):
    t, wa = a.shape
    wb = b.shape[1]
    tm = _tile(t, 512, 16)

    def body(a_ref, b_ref, o_ref):
        o_ref[:, :wa] = a_ref[...]
        o_ref[:, wa:] = b_ref[...]

    return _pallas(
        body, name=name, grid=(t // tm,),
        in_specs=[pl.BlockSpec((tm, wa), lambda i: (i, 0)), pl.BlockSpec((tm, wb), lambda i: (i, 0))],
        out_specs=pl.BlockSpec((tm, wa + wb), lambda i: (i, 0)),
        out_shape=jax.ShapeDtypeStruct((t, wa + wb), a.dtype),
        compiler_params=_params(("parallel",), 2 * _nbytes((tm, wa + wb), a.dtype)),
    )(a, b)


def _cast_into_full(name, w, geom, idx, after=None):
    r, c = w.shape
    tr = _tile(r, 256, 16)
    nr = r // tr
    if geom.col:
        place = lambda i, iref: (i, iref[0])
    else:
        place = lambda i, iref: (iref[0] * nr + i, 0)

    def body(i_ref, w_ref, *rest):
        rest[-1][...] = w_ref[...].astype(BF16)

    in_specs = [pl.BlockSpec((tr, c), lambda i, iref: (i, 0))]
    ins = [w]
    if after is not None:
        in_specs.append(pl.BlockSpec(memory_space=pl.ANY))
        ins.append(after)
    return _pallas(
        body, name=name,
        grid_spec=pltpu.PrefetchScalarGridSpec(
            num_scalar_prefetch=1, grid=(nr,), in_specs=in_specs, out_specs=pl.BlockSpec((tr, c), place)),
        out_shape=pltpu.HBM(geom.full_shape, BF16),
        compiler_params=_params(("parallel",), 2 * _nbytes((tr, c), F32)),
    )(idx, *ins)


def _adamw(name, w, g, m, v):
    r, c = w.shape
    tr = _tile(r, 256, 8)
    bc1 = 1.0 - ADAM_B1 ** ADAM_STEP
    bc2 = 1.0 - ADAM_B2 ** ADAM_STEP

    def body(w_ref, g_ref, m_ref, v_ref, go_ref, d_ref, nm_ref, nv_ref):
        gv = g_ref[...]
        go_ref[...] = gv
        nm = ADAM_B1 * m_ref[...] + (1.0 - ADAM_B1) * gv
        nv = ADAM_B2 * v_ref[...] + (1.0 - ADAM_B2) * (gv * gv)
        nm_ref[...] = nm
        nv_ref[...] = nv
        d_ref[...] = -ADAM_LR * ((nm / bc1) / (jnp.sqrt(nv / bc2) + ADAM_EPS) + ADAM_WD * w_ref[...])

    blk = pl.BlockSpec((tr, c), lambda i: (i, 0))
    out = jax.ShapeDtypeStruct((r, c), F32)
    return _pallas(
        body, name=name, grid=(r // tr,), in_specs=[blk] * 4, out_specs=[blk] * 4, out_shape=[out] * 4,
        compiler_params=_params(("parallel",), 8 * _nbytes((tr, c), F32)),
    )(w, g, m, v)


def _mesh_pos():
    return lax.axis_index("x"), lax.axis_index("y"), lax.axis_index("c")


def _other_chips(x, y):
    return [(1 - x, y), (x, 1 - y), (1 - x, 1 - y)]


class _Big:
    def __init__(self, shard_shape, col_sharded):
        self.col = col_sharded
        r, c = shard_shape
        self.shard_shape = (r, c)
        self.full_shape = (r, N_CHIPS * c) if col_sharded else (N_CHIPS * r, c)
        self.half_shape = (r // 2, N_CHIPS * c) if col_sharded else (N_CHIPS * r, c // 2)
        self.shard_half_shape = (r // 2, c) if col_sharded else (r, c // 2)

    def region(self, ref, s, half=None):
        r, c = self.shard_shape
        if self.col:
            rows = slice(None) if half is None else pl.ds(half * (r // 2), r // 2)
            return ref.at[rows, pl.ds(s * c, c)]
        cols = slice(None) if half is None else pl.ds(half * (c // 2), c // 2)
        return ref.at[pl.ds(s * r, r), cols]

    def n_halves(self, ref, half, n):
        r, c = self.shard_shape
        if self.col:
            return ref.at[pl.ds(half * (r // 2), r // 2), pl.ds(0, n * c)]
        return ref.at[pl.ds(0, n * r), pl.ds(half * (c // 2), c // 2)]

    def three_halves(self, ref, half):
        return self.n_halves(ref, half, 3)

    def sub_half(self, ref, s, half, j):
        r, c = self.shard_shape
        if self.col:
            return ref.at[pl.ds(half * (r // 2) + j * (r // 4), r // 4), pl.ds(s * c, c)]
        return ref.at[pl.ds(s * r + j * (r // 2), r // 2), pl.ds(half * (c // 2), c // 2)]

    def half_of_full(self, ref, half):
        r, c = self.full_shape
        if self.col:
            return ref.at[pl.ds(half * (r // 2), r // 2), :]
        return ref.at[:, pl.ds(half * (c // 2), c // 2)]

    def half_of_shard(self, ref, half):
        r, c = self.shard_shape
        if self.col:
            return ref.at[pl.ds(half * (r // 2), r // 2), :]
        return ref.at[:, pl.ds(half * (c // 2), c // 2)]

    def shard_of_half(self, ref, s):
        r, c = self.shard_shape
        if self.col:
            return ref.at[:, pl.ds(s * c, c)]
        return ref.at[pl.ds(s * r, r), :]


HBM =pl.BlockSpec(memory_space=pltpu.HBM)
SEM = pl.BlockSpec(memory_space=pltpu.SEMAPHORE)
SPLIT_COPY = pltpu.CompilerParams(has_side_effects=pltpu.SideEffectType.DATAFLOW_SIDE_EFFECTING)


def _in_hbm(a):
    return pltpu.with_memory_space_constraint(a, pltpu.HBM)


def _gather_start(name, fulls, geoms, after):
    nw = len(fulls)

    def body(*refs):
        dst = refs[nw + 1:2 * nw + 1]
        sems = refs[2 * nw + 1:-1]
        x, y, c = _mesh_pos()
        mine = 2 * x + y
        for w in range(nw):
            own_half = geoms[w].region(dst[w], mine, c)
            for chip in _other_chips(x, y):
                pltpu.make_async_remote_copy(src_ref=own_half, dst_ref=own_half, send_sem=sems[2 * w],
                                             recv_sem=sems[2 * w + 1], device_id=(*chip, c),
                                             device_id_type=MESH).start()
        refs[-1][...] = jnp.zeros_like(refs[-1])

    out = _pallas(
        body, name=name, in_specs=[HBM] * nw + [pl.BlockSpec(memory_space=pl.ANY)],
        out_specs=[HBM] * nw + [SEM] * (2 * nw) + [pl.BlockSpec(memory_space=pltpu.VMEM)],
        out_shape=[pltpu.HBM(g.full_shape, BF16) for g in geoms] + [pltpu.SemaphoreType.DMA(())] * (2 * nw)
        + [jax.ShapeDtypeStruct((8, 128), F32)],
        input_output_aliases={w: w for w in range(nw)}, compiler_params=SPLIT_COPY,
    )(*[_in_hbm(a) for a in fulls], after)
    return list(out[:nw]), [(out[nw + 2 * w], out[nw + 2 * w + 1]) for w in range(nw)], out[-1]


def _gather_first_direct(full, geom):
    def start(refs, _, new):
        x, y, c = _mesh_pos()
        own = geom.region(refs[0], 2 * x + y, c)
        for chip in ((1 - x, y), (x, 1 - y)):
            _remote(own, own, new, (*chip, c)).start()

    return _split_copy_call("gather_first_direct", [full], start, new_sems=2)


def _gather_first_relay(full, geom, sems, after):
    def relay(refs, got, new):
        x, y, c = _mesh_pos()
        w = refs[0]
        two = geom.n_halves(w, c, 2)
        _remote(two, two, got, (x, y, 1 - c)).wait_recv()
        from_x = geom.sub_half(w, 2 * (1 - x) + y, c, 0)
        from_y = geom.sub_half(w, 2 * x + (1 - y), c, 1)
        _remote(from_x, from_x, new, (x, 1 - y, c)).start()
        _remote(from_y, from_y, new, (1 - x, y, c)).start()
        _remote(two, two, got, (x, y, 1 - c)).wait_send()

    return _split_copy_call("gather_first_relay", [full], relay, sems=sems, after=after, new_sems=2)


def _gather_forward(name, full, geom, sems, after, arrivals=3, only_diagonal=False):
    def body(w_in, send_sem, recv_sem, after_ref, w_ref, fwd_send, fwd_recv):
        x, y, c = _mesh_pos()
        sibling = (x, y, 1 - c)
        landed_all = geom.n_halves(w_ref, c, arrivals)
        _remote(landed_all, landed_all, (send_sem, recv_sem), sibling).wait_recv()
        for chip in _other_chips(x, y)[2 if only_diagonal else 0:]:
            landed = geom.region(w_ref, 2 * chip[0] + chip[1], c)
            pltpu.make_async_remote_copy(src_ref=landed, dst_ref=landed, send_sem=fwd_send, recv_sem=fwd_recv,
                                         device_id=sibling, device_id_type=MESH).start()
        _remote(landed_all, landed_all, (send_sem, recv_sem), sibling).wait_send()

    sem = pltpu.SemaphoreType.DMA(())
    out = _pallas(
        body, name=name, in_specs=[HBM, SEM, SEM, pl.BlockSpec(memory_space=pl.ANY)], out_specs=[HBM, SEM, SEM],
        out_shape=[pltpu.HBM(geom.full_shape, BF16), sem, sem],
        input_output_aliases={0: 0}, compiler_params=SPLIT_COPY,
    )(full, sems[0], sems[1], after)
    return out[0], (out[1], out[2])


def _gather_end(name, full, geom, sems, after, halves=3):
    def body(w_in, fwd_send, fwd_recv, after_ref, w_ref):
        x, y, c = _mesh_pos()
        sibling = (x, y, 1 - c)
        theirs, ours = geom.n_halves(w_ref, 1 - c, halves), geom.n_halves(w_ref, c, halves)
        _remote(theirs, theirs, (fwd_send, fwd_recv), sibling).wait_recv()
        _remote(ours, ours, (fwd_send, fwd_recv), sibling).wait_send()

    return _pallas(
        body, name=name, in_specs=[HBM, SEM, SEM, pl.BlockSpec(memory_space=pl.ANY)], out_specs=HBM,
        out_shape=pltpu.HBM(geom.full_shape, BF16),
        input_output_aliases={0: 0}, compiler_params=SPLIT_COPY,
    )(full, sems[0], sems[1], after)


def _split_copy_call(name, arrays, fn, sems=(), after=None, new_sems=0):
    n, ns = len(arrays), len(sems)
    n_in = n + ns + (after is not None)

    def body(*refs):
        fn(refs[n_in:n_in + n], refs[n:n + ns], refs[n_in + n:-1])
        refs[-1][...] = jnp.zeros_like(refs[-1])

    ins = list(arrays) if ns else [_in_hbm(a) for a in arrays]
    ins += list(sems) + ([after] if after is not None else [])
    in_specs = [HBM] * n + [SEM] * ns + ([pl.BlockSpec(memory_space=pl.ANY)] if after is not None else [])
    out = _pallas(
        body, name=name, in_specs=in_specs,
        out_specs=[HBM] * n + [SEM] * new_sems + [pl.BlockSpec(memory_space=pltpu.VMEM)],
        out_shape=[pltpu.HBM(a.shape, a.dtype) for a in arrays] + [pltpu.SemaphoreType.DMA(())] * new_sems
        + [jax.ShapeDtypeStruct((8, 128), F32)],
        input_output_aliases={i: i for i in range(n)}, compiler_params=SPLIT_COPY,
    )(*ins)
    return list(out[:n]), tuple(out[n:-1]), out[-1]


def _remote(src, dst, sems, to):
    return pltpu.make_async_remote_copy(src_ref=src, dst_ref=dst, send_sem=sems[0], recv_sem=sems[1],
                                        device_id=to, device_id_type=MESH)


class _GradReduce:
    def __init__(self, name, geom, idx, c_idx):
        self.name, self.geom, self.idx, self.c_idx = name, geom, idx, c_idx

    def pair_start(self, theirs):
        g = self.geom

        def start(refs, _, new):
            x, y, c = _mesh_pos()
            _remote(refs[0], refs[1], new, (x, y, 1 - c)).start()

        self.arrays, self.sems, token = _split_copy_call(
            f"pair_start_{self.name}", [theirs, lax.empty(g.half_shape, BF16)], start, new_sems=2)
        return token

    def pair_wait(self, after):
        def wait(refs, sems, _):
            x, y, c = _mesh_pos()
            copy = _remote(refs[0], refs[1], sems, (x, y, 1 - c))
            copy.wait_send()
            copy.wait_recv()

        (_, landed), _, _ = _split_copy_call(f"pair_wait_{self.name}", self.arrays, wait, self.sems, after)
        return landed

    def chip_start(self, half):
        g = self.geom

        def start(refs, _, new):
            x, y, c = _mesh_pos()
            for k, chip in enumerate(_other_chips(x, y)):
                _remote(g.shard_of_half(refs[0], 2 * chip[0] + chip[1]), refs[1].at[k], new, (*chip, c)).start()

        self.arrays, self.sems, token = _split_copy_call(
            f"chip_start_{self.name}", [half, lax.empty((3,) + g.shard_half_shape, BF16)], start, new_sems=2)
        return token

    def chip_finish(self, after):
        g = self.geom

        def wait(refs, sems, _):
            x, y, c = _mesh_pos()
            three = _remote(refs[1], refs[1], sems, (x, y, 1 - c))
            three.wait_send()
            three.wait_recv()

        (half, landed), _, _ = _split_copy_call(f"chip_wait_{self.name}", self.arrays, wait, self.sems, after)
        quarter = _chip_add(f"chip_add_{self.name}", half, landed, g, self.idx)

        def start(refs, _, new):
            x, y, c = _mesh_pos()
            own = g.half_of_shard(refs[0], c)
            _remote(own, own, new, (x, y, 1 - c)).start()

        self.arrays, self.sems, token = _split_copy_call(f"share_start_{self.name}", [quarter], start, new_sems=2)
        return token

    def finish(self, after):
        g = self.geom

        def wait(refs, sems, _):
            x, y, c = _mesh_pos()
            own, theirs = g.half_of_shard(refs[0], c), g.half_of_shard(refs[0], 1 - c)
            _remote(own, own, sems, (x, y, 1 - c)).wait_send()
            _remote(theirs, theirs, sems, (x, y, 1 - c)).wait_recv()

        (quarter,), _, _ = _split_copy_call(f"share_wait_{self.name}", self.arrays, wait, self.sems, after)
        return quarter


def _dw_half(name, x, dy, geom, c_idx, own, addend=None, after=None):
    stacked = dy.ndim == 3
    t, m = x.shape
    n = 2 * dy.shape[2] if stacked else dy.shape[1]
    hm, hn = (m // 2, n) if geom.col else (m, n // 2)
    tm, tn, tk = _mm_tiles(hm, hn, t, BF16, n_unit=(n // 2 if stacked else None))
    if tk != t:
        tm, tn = _tile(hm, 512, 128), _tile(hn // (2 if stacked else 1), 512, 128)
    gi, gj = hm // tm, hn // tn
    nf = (n // 2) // tn

    def sel(cref):
        return cref[0] if own else 1 - cref[0]

    a_map = (lambda i, j, cref: (0, sel(cref) * gi + i)) if geom.col else (lambda i, j, cref: (0, i))
    if stacked:
        b_blk, b_map = (None, t, tn), (lambda i, j, cref: (j // nf, 0, j % nf))
    elif geom.col:
        b_blk, b_map = (t, tn), (lambda i, j, cref: (0, j))
    else:
        b_blk, b_map = (t, tn), (lambda i, j, cref: (0, sel(cref) * gj + j))
    out_blk = pl.BlockSpec((tm, tn), lambda i, j, cref: (i, j))
    ins, in_specs = [x, dy], [pl.BlockSpec((t, tm), a_map), pl.BlockSpec(b_blk, b_map)]
    if addend is not None:
        ins.append(addend)
        in_specs.append(out_blk)
    if after is not None:
        ins.append(after)
        in_specs.append(pl.BlockSpec(memory_space=pl.ANY))

    def body(c_ref, *refs):
        acc = _dot(refs[0][...], refs[1][...], 0, 0)
        if addend is not None:
            acc = acc + refs[2][...].astype(F32)
        refs[len(ins)][...] = acc.astype(BF16)

    return _pallas(
        body, name=name,
        grid_spec=pltpu.PrefetchScalarGridSpec(num_scalar_prefetch=1, grid=(gi, gj), in_specs=in_specs,
                                               out_specs=out_blk),
        out_shape=jax.ShapeDtypeStruct((hm, hn), BF16),
        compiler_params=_params(("parallel", "parallel"),
                                _nbytes((t, tm), BF16) + _nbytes((t, tn), BF16) + 3 * _nbytes((tm, tn), F32)),
    )(c_idx, *ins)


def _chip_add(name, half, recv, geom, idx):
    r, c = geom.shard_half_shape
    tr, tc = _tile(r, 512, 16), _tile(c, 2048, 128)
    nr, ncol = r // tr, c // tc
    if geom.col:
        mine = lambda i, j, iref: (i, iref[0] * ncol + j)
        place = lambda i, j, iref: (iref[1] * nr + i, j)
    else:
        mine = lambda i, j, iref: (iref[0] * nr + i, j)
        place = lambda i, j, iref: (i, iref[1] * ncol + j)

    def body(i_ref, h_ref, r_ref, o_ref):
        acc = h_ref[...].astype(F32)
        for k in range(3):
            acc = acc + r_ref[k].astype(F32)
        o_ref[...] = acc

    return _pallas(
        body, name=name,
        grid_spec=pltpu.PrefetchScalarGridSpec(
            num_scalar_prefetch=1, grid=(nr, ncol),
            in_specs=[pl.BlockSpec((tr, tc), mine), pl.BlockSpec((3, tr, tc), lambda i, j, iref: (0, i, j))],
            out_specs=pl.BlockSpec((tr, tc), place)),
        out_shape=jax.ShapeDtypeStruct(geom.shard_shape, F32),
        compiler_params=_params(("parallel", "parallel"), 4 * _nbytes((tr, tc), F32)),
    )(idx, half, recv)


def _all_reduce_small(pack, after=None):
    r, d = pack.shape

    def body(p_ref, o_ref, slots, send_sems, recv_sems):
        x, y, c = _mesh_pos()
        me = 4 * x + 2 * y + c
        slots[me] = p_ref[...]
        copies = []
        for k in range(1, N_DEV):
            px, py, pc = x ^ ((k >> 2) & 1), y ^ ((k >> 1) & 1), c ^ (k & 1)
            copies.append(pltpu.make_async_remote_copy(
                src_ref=p_ref, dst_ref=slots.at[me], send_sem=send_sems.at[k - 1], recv_sem=recv_sems.at[k - 1],
                device_id=(px, py, pc), device_id_type=MESH))
        for cp in copies:
            cp.start()
        for k in range(1, N_DEV):
            peer = 4 * (x ^ ((k >> 2) & 1)) + 2 * (y ^ ((k >> 1) & 1)) + (c ^ (k & 1))
            pltpu.make_async_remote_copy(
                src_ref=p_ref, dst_ref=slots.at[peer], send_sem=send_sems.at[k - 1], recv_sem=recv_sems.at[k - 1],
                device_id=(x, y, c), device_id_type=MESH).wait_recv()
        for cp in copies:
            cp.wait_send()
        acc = slots[0]
        for k in range(1, N_DEV):
            acc = acc + slots[k]
        o_ref[...] = acc

    vm = pl.BlockSpec(memory_space=pltpu.VMEM)
    body, ins, in_specs = _ordered(body, [pack], [vm], after)
    return _pallas(
        body, name="all_reduce_small", in_specs=in_specs, out_specs=vm,
        out_shape=jax.ShapeDtypeStruct((r, d), F32),
        scratch_shapes=[pltpu.VMEM((N_DEV, r, d), F32), pltpu.SemaphoreType.DMA((N_DEV - 1,)),
                        pltpu.SemaphoreType.DMA((N_DEV - 1,))],
    )(*ins)


def _pack_rows(rows, d):
    out = []
    for a in rows:
        flat = a.reshape(-1)
        n = -(-flat.shape[0] // d) * d
        out.append(jnp.pad(flat, (0, n - flat.shape[0])).reshape(-1, d))
    packed = jnp.concatenate(out, axis=0)
    return jnp.pad(packed, ((0, 16 - packed.shape[0]), (0, 0)))


def _unpack_rows(packed, shapes, d):
    out, row = [], 0
    for shp in shapes:
        n = int(np.prod(shp))
        nrows = -(-n // d)
        out.append(packed[row:row + nrows].reshape(-1)[:n].reshape(shp))
        row += nrows
    return out


def kernel(x, pre_norm_ffn1, post_norm_ffn1, w_ffn1_gate_up, w_ffn1_down, pre_norm_mix, post_norm_mix, w_mix_in, hgrn_lower_bounds_fwd, hgrn_lower_bounds_bwd, hgrn_out_norm, attn_sink, w_mix_out, pre_norm_ffn2, post_norm_ffn2, w_ffn2_gate_up, w_ffn2_down, rel_bias_table, loss_target, m_pre_norm_ffn1, m_post_norm_ffn1, m_w_ffn1_gate_up, m_w_ffn1_down, m_pre_norm_mix, m_post_norm_mix, m_w_mix_in, m_hgrn_lower_bounds_fwd, m_hgrn_lower_bounds_bwd, m_hgrn_out_norm, m_attn_sink, m_w_mix_out, m_pre_norm_ffn2, m_post_norm_ffn2, m_w_ffn2_gate_up, m_w_ffn2_down, m_rel_bias_table, v_pre_norm_ffn1, v_post_norm_ffn1, v_w_ffn1_gate_up, v_w_ffn1_down, v_pre_norm_mix, v_post_norm_mix, v_w_mix_in, v_hgrn_lower_bounds_fwd, v_hgrn_lower_bounds_bwd, v_hgrn_out_norm, v_attn_sink, v_w_mix_out, v_pre_norm_ffn2, v_post_norm_ffn2, v_w_ffn2_gate_up, v_w_ffn2_down, v_rel_bias_table):
    t, d = x.shape[1], x.shape[2]
    hw = hgrn_out_norm.shape[1]
    aw = d - hw
    nah = aw // HEAD
    kvw = KV_HEADS * HEAD
    x0 = x[0]
    target = loss_target[0]

    big_names = ["w_ffn1_gate_up", "w_ffn1_down", "w_mix_in", "w_mix_out", "w_ffn2_gate_up", "w_ffn2_down"]
    big_w = [w_ffn1_gate_up[0], w_ffn1_down[0], w_mix_in[0], w_mix_out[0], w_ffn2_gate_up[0], w_ffn2_down[0]]
    big_m = [m_w_ffn1_gate_up[0], m_w_ffn1_down[0], m_w_mix_in[0], m_w_mix_out[0], m_w_ffn2_gate_up[0],
             m_w_ffn2_down[0]]
    big_v = [v_w_ffn1_gate_up[0], v_w_ffn1_down[0], v_w_mix_in[0], v_w_mix_out[0], v_w_ffn2_gate_up[0],
             v_w_ffn2_down[0]]
    col_sharded = [True, False, True, False, True, False]
    geoms = [_Big(w.shape, cs) for w, cs in zip(big_w, col_sharded)]

    cx, cy, cc = _mesh_pos()
    idx = jnp.stack([2 * cx + cy, cc]).astype(jnp.int32)
    c_idx = jnp.reshape(cc, (1,)).astype(jnp.int32)
    first = _cast_into_full(f"cast_{big_names[0]}", big_w[0], geoms[0], idx)
    (first,), direct_sems, tok = _gather_first_direct(first, geoms[0])
    rest = []
    for n, w, gm in zip(big_names[1:], big_w[1:], geoms[1:]):
        tok = _cast_into_full(f"cast_{n}", w, gm, idx, after=tok)
        rest.append(tok)
    (first,), relay_sems, tok = _gather_first_relay(first, geoms[0], direct_sems, after=tok)
    started_rest, sems_rest, rest_started = _gather_start("gather_start_rest", rest, geoms[1:], after=tok)
    started, gather_sems = [first] + started_rest, [relay_sems] + sems_rest

    def forward_weight(w, after):
        return _gather_forward(f"gather_forward_{big_names[w]}", started[w], geoms[w], gather_sems[w], after,
                               arrivals=1 if w == 0 else 3)

    def whole_weight(w, forwarded, after):
        return _gather_end(f"gather_end_{big_names[w]}", forwarded[0], geoms[w], forwarded[1], after)

    h1 = _norm_fwd("ffn1_pre_norm", x0, pre_norm_ffn1)
    w_gu1 = whole_weight(0, forward_weight(0, rest_started), h1)
    act1, dact_dgate1, dact_dup1 = _ffn_gate_up_act("ffn1_gate_up", h1, w_gu1)
    w_d1 = whole_weight(1, forward_weight(1, act1), act1)
    ff1 = _mm("ffn1_down", act1, w_d1, "nn", F32)
    fw = forward_weight(2, ff1)
    x1, hm = _resid_norm_fwd("ffn1_residual", x0, ff1, post_norm_ffn1, pre_norm_mix, 0.5)
    w_in = whole_weight(2, fw, hm)
    p = _mm("mix_in", hm, w_in, "nn", F32)
    fw = forward_weight(3, p)
    o_f, o_b, st_f, st_b = _hgrn_scan_fwd("hgrn_scan", p, hgrn_lower_bounds_fwd, hgrn_lower_bounds_bwd)
    y_h = _hgrn_out_fwd("hgrn_out", o_f, o_b, p, hgrn_out_norm, 4)
    kv_blk0 = (5 * hw + aw) // kvw
    bucket_ids = _t5_bucket_ids()
    bias = _bias_gather("attn_bias", rel_bias_table.T, bucket_ids).reshape(nah, WINDOW, SPAN)
    y_a, attn_probs, attn_sink_probs = _attn_fwd("attn_fwd", p, kv_blk0, kvw, bias, attn_sink, 5 * hw // aw)
    y_mix = _concat_cols("mix_concat", y_h, y_a)
    w_out = whole_weight(3, fw, y_mix)
    mixed = _mm("mix_out", y_mix, w_out, "nn", F32)
    fw = forward_weight(4, mixed)
    x2, h2 = _resid_norm_fwd("mix_residual", x1, mixed, post_norm_mix, pre_norm_ffn2, 1.0)
    w_gu2 = whole_weight(4, fw, h2)
    act2, dact_dgate2, dact_dup2 = _ffn_gate_up_act("ffn2_gate_up", h2, w_gu2)
    w_d2 = whole_weight(5, forward_weight(5, act2), act2)
    ff2 = _mm("ffn2_down", act2, w_d2, "nn", F32)
    loss_blk, dy, dff2, dg_post2 = _final_fwd_bwd("ffn2_residual_loss", x2, ff2, post_norm_ffn2, target, 0.5)

    reduce = [_GradReduce(n, gm, idx, c_idx) for n, gm in zip(big_names, geoms)]
    big_grads, big_delta, big_new_m, big_new_v = [None] * 6, [None] * 6, [None] * 6, [None] * 6

    def update(w, after):
        g, dl, nm, nv = _adamw(f"adamw_{big_names[w]}", big_w[w], reduce[w].finish(after), big_m[w], big_v[w])
        big_grads[w], big_delta[w], big_new_m[w], big_new_v[w] = g[None], dl[None], nm[None], nv[None]
        return dl

    def dw_start(w, x_act, dy_act, after=None):
        theirs = _dw_half(f"dw_theirs_{big_names[w]}", x_act, dy_act, geoms[w], c_idx, own=False, after=after)
        return reduce[w].pair_start(theirs)

    def dw_finish(w, x_act, dy_act, after):
        landed = reduce[w].pair_wait(after)
        half = _dw_half(f"dw_own_{big_names[w]}", x_act, dy_act, geoms[w], c_idx, own=True, addend=landed)
        return reduce[w].chip_start(half)

    tok = dw_start(5, act2, dff2)
    dgu2 = _ffn_dact("ffn2_dact", dff2, w_d2, dact_dgate2, dact_dup2, after=tok)
    tok = dw_finish(5, act2, dff2, after=dgu2)
    tok = dw_start(4, h2, dgu2, after=tok)
    dh2 = _ffn_dh("ffn2_dh", dgu2, w_gu2, after=tok)
    tok = dw_finish(4, h2, dgu2, after=dh2)
    dx2, dg_pre2, dmixed, dg_postm = _norms_bwd("mix_residual_bwd", dy, dh2, x2, pre_norm_ffn2,
                                                post=(mixed, post_norm_mix, 1.0), after=tok)
    tok = dw_start(3, y_mix, dmixed)
    dy_mix = _mm("mix_out_dx", dmixed, w_out, "nt", F32, after=tok)
    tok = dw_finish(3, y_mix, dmixed, after=dy_mix)
    dq_a, dk_pad, dv_pad, dbias, dsink = _attn_bwd("attn_bwd", p, kv_blk0, kvw, attn_probs, attn_sink_probs, dy_mix,
                                                   5 * hw // aw, hw // aw, after=tok)
    tok = reduce[5].chip_finish(dq_a)
    drel_t = _bias_scatter("attn_dbias", dbias.reshape(nah, WINDOW * SPAN), bucket_ids)
    do, dg_h, dgain = _hgrn_out_bwd("hgrn_out_bwd", dy_mix, o_f, o_b, p, hgrn_out_norm, 4, after=tok)
    dq_f, dv_f, dz_f, dlb_f, dq_b, dv_b, dz_b, dlb_b = _hgrn_scan_bwd(
        "hgrn_scan_bwd", p, hgrn_lower_bounds_fwd, hgrn_lower_bounds_bwd, do, st_f, st_b)
    tok = reduce[4].chip_finish(dq_f)
    tok = reduce[3].chip_finish(tok)
    dp = _mix_dproj("mix_dproj", [(dq_f, dq_b), (dv_f, dv_b), (dz_f,), (dz_b,), (dg_h,), (dq_a,)],
                    [dk_pad, dv_pad], t, after=tok)
    tok = dw_start(2, hm, dp)
    dhm = _mm("mix_in_dx", dp, w_in, "nt", F32, after=tok)
    tok = dw_finish(2, hm, dp, after=dhm)
    dx1, dg_prem, dff1, dg_post1 = _norms_bwd("ffn1_residual_bwd", dx2, dhm, x1, pre_norm_mix,
                                              post=(ff1, post_norm_ffn1, 0.5), after=tok)
    tok = dw_start(1, act1, dff1)
    dgu1 = _ffn_dact("ffn1_dact", dff1, w_d1, dact_dgate1, dact_dup1, after=tok)
    tok = dw_finish(1, act1, dff1, after=dgu1)
    tok = reduce[2].chip_finish(tok)
    tok = dw_start(0, h1, dgu1, after=tok)
    done = update(2, tok)
    tok = dw_finish(0, h1, dgu1, after=done)
    dh1 = _ffn_dh("ffn1_dh", dgu1, w_gu1, after=tok)
    grad_x, dg_pre1 = _norms_bwd("ffn1_pre_norm_bwd", dx1, dh1, x0, pre_norm_ffn1)

    small_w = [pre_norm_ffn1, post_norm_ffn1, pre_norm_mix, post_norm_mix, hgrn_lower_bounds_fwd,
               hgrn_lower_bounds_bwd, hgrn_out_norm, attn_sink, pre_norm_ffn2, post_norm_ffn2, rel_bias_table]
    small_m = [m_pre_norm_ffn1, m_post_norm_ffn1, m_pre_norm_mix, m_post_norm_mix, m_hgrn_lower_bounds_fwd,
               m_hgrn_lower_bounds_bwd, m_hgrn_out_norm, m_attn_sink, m_pre_norm_ffn2, m_post_norm_ffn2,
               m_rel_bias_table]
    small_v = [v_pre_norm_ffn1, v_post_norm_ffn1, v_pre_norm_mix, v_post_norm_mix, v_hgrn_lower_bounds_fwd,
               v_hgrn_lower_bounds_bwd, v_hgrn_out_norm, v_attn_sink, v_pre_norm_ffn2, v_post_norm_ffn2,
               v_rel_bias_table]
    small_g = [dg_pre1, dg_post1, dg_prem, dg_postm, dlb_f, dlb_b, dgain, dsink[:, 0].reshape(1, nah), dg_pre2,
               dg_post2, drel_t.T]
    shapes = [a.shape for a in small_w]
    done = update(5, grad_x)
    done = update(4, done)
    done = update(3, done)
    summed = _all_reduce_small(_pack_rows(small_g + [loss_blk[0:1, 0:1]], d), after=done)
    loss = _unpack_rows(summed, shapes + [(1, 1)], d)[-1][0, 0]
    _, sd, sm, sv = _adamw("adamw_small", _pack_rows(small_w, d), summed, _pack_rows(small_m, d),
                           _pack_rows(small_v, d))
    small_grads = _unpack_rows(summed, shapes, d)
    small_delta, small_new_m, small_new_v = (_unpack_rows(a, shapes, d) for a in (sd, sm, sv))

    tok = reduce[1].chip_finish(sd)
    tok = reduce[0].chip_finish(tok)
    done = update(1, tok)
    update(0, done)

    def ordered(small, big):
        s = dict(zip(["pre1", "post1", "prem", "postm", "lbf", "lbb", "gain", "sink", "pre2", "post2", "rel"], small))
        b = dict(zip(["gu1", "d1", "win", "wout", "gu2", "d2"], big))
        return [s["pre1"], s["post1"], b["gu1"], b["d1"], s["prem"], s["postm"], b["win"], s["lbf"], s["lbb"],
                s["gain"], s["sink"], b["wout"], s["pre2"], s["post2"], b["gu2"], b["d2"], s["rel"]]

    return (loss, grad_x[None], *ordered(small_grads, big_grads), *ordered(small_delta, big_delta),
            *ordered(small_new_m, big_new_m), *ordered(small_new_v, big_new_v))
```

```python
import math

import jax
import jax.numpy as jnp
import numpy as np
from jax import lax
from jax.experimental import pallas as pl
from jax.experimental.pallas import tpu as pltpu

F32 = jnp.float32
BF16 = jnp.bfloat16

HEAD = 128
CHUNK = 64
WINDOW = 128
SPAN = 3 * WINDOW
KV_HEADS = 2
REL_BUCKETS = 32
REL_MAX_DIST = 128
EPS = 1e-6
NEG_INF = -1e30

ADAM_LR = 0.001
ADAM_B1 = 0.9
ADAM_B2 = 0.999
ADAM_EPS = 1e-08
ADAM_WD = 0.01
ADAM_STEP = 10

N_CHIPS = 4
N_DEV = 8
V7X_VMEM_BYTES = 64 * 1024 * 1024
MESH = pl.DeviceIdType.MESH
ANY = pl.BlockSpec(memory_space=pl.ANY)


def _tile(n, pref, mult):
    t = (min(pref, n) // mult) * mult
    while t >= mult:
        if n % t == 0:
            return t
        t -= mult
    return n


def _params(semantics, block_bytes):
    limit = min(V7X_VMEM_BYTES - (4 << 20), 2 * int(block_bytes) + (8 << 20))
    return pltpu.CompilerParams(dimension_semantics=semantics, vmem_limit_bytes=limit)


def _nbytes(shape, dtype):
    return int(np.prod(shape)) * jnp.dtype(dtype).itemsize


PIN_TO_HBM_BYTES = 4 << 20


def _pallas(body, **kw):
    def pin_shape(s):
        if isinstance(s, jax.ShapeDtypeStruct) and _nbytes(s.shape, s.dtype) >= PIN_TO_HBM_BYTES:
            return pltpu.HBM(s.shape, s.dtype)
        return s

    def pin(a):
        if getattr(a, "dtype", None) in (F32, BF16) and _nbytes(a.shape, a.dtype) >= PIN_TO_HBM_BYTES:
            return pltpu.with_memory_space_constraint(a, pltpu.HBM)
        return a

    out_shape = kw["out_shape"]
    kw["out_shape"] = [pin_shape(s) for s in out_shape] if isinstance(out_shape, (list, tuple)) else pin_shape(out_shape)
    call = pl.pallas_call(body, **kw)
    return lambda *args: call(*[pin(a) for a in args])


def _dot(a, b, ca=1, cb=0):
    return lax.dot_general(a, b, (((ca,), (cb,)), ((), ())), preferred_element_type=F32)


def _split3(x):
    hi = x.astype(BF16)
    r1 = x - hi.astype(F32)
    mid = r1.astype(BF16)
    lo = (r1 - mid.astype(F32)).astype(BF16)
    return hi, mid, lo


def _dot_exact(a, b, ca=1, cb=0, split="b"):
    if split == "b":
        return sum(_dot(a, p, ca, cb) for p in _split3(b))
    return sum(_dot(p, b, ca, cb) for p in _split3(a))


def _rms(x):
    return lax.rsqrt(jnp.mean(x * x, axis=-1, keepdims=True) + EPS)


def _norm_bwd(u, x, gain):
    r = _rms(x)
    xhat = x * r
    dgain = jnp.sum(u * xhat, axis=0, keepdims=True)
    v = u * gain
    dx = r * (v - xhat * jnp.mean(v * xhat, axis=-1, keepdims=True))
    return dx, dgain


def _sigmoid(x):
    return 1.0 / (1.0 + jnp.exp(-x))


def _accumulate(ref, val, first):
    @pl.when(first)
    def _():
        ref[...] = val

    @pl.when(jnp.logical_not(first))
    def _():
        ref[...] += val


def _ordered(body, ins, in_specs, after):
    if after is None:
        return body, list(ins), list(in_specs)
    n_in = len(ins)

    def wrapped(*refs):
        body(*refs[:n_in], *refs[n_in + 1:])

    return wrapped, list(ins) + [after], list(in_specs) + [pl.BlockSpec(memory_space=pl.ANY)]


def _matmul(name, a, b, *, form, out_dtype, tm, tn, tk, a_map=None, sizes=None, after=None):
    if sizes is None:
        (m, k), n = a.shape, (b.shape[1] if form == "nn" else b.shape[0])
    else:
        m, n, k = sizes
    gi, gj, gk = m // tm, n // tn, k // tk
    a_blk = (tm, tk)
    b_blk = (tk, tn) if form == "nn" else (tn, tk)
    if a_map is None:
        a_map = lambda i, j, kk: (i, kk)
    else:
        a_blk = (None,) + a_blk
    b_map = (lambda i, j, kk: (kk, j)) if form == "nn" else (lambda i, j, kk: (j, kk))
    out_shape, out_block, out_map = (m, n), (tm, tn), (lambda i, j, kk: (i, j))
    ca, cb = {"nn": (1, 0), "nt": (1, 1)}[form]

    def body(a_ref, b_ref, o_ref, *acc):
        part = _dot(a_ref[...], b_ref[...], ca, cb)
        if gk == 1:
            o_ref[...] = part.astype(o_ref.dtype)
        else:
            kk = pl.program_id(2)
            _accumulate(acc[0], part, kk == 0)

            @pl.when(kk == gk - 1)
            def _():
                o_ref[...] = acc[0][...].astype(o_ref.dtype)

    scratch = [] if gk == 1 else [pltpu.VMEM((tm, tn), F32)]
    vmem = (_nbytes((tm, tk), a.dtype) + _nbytes((tk, tn), b.dtype) + _nbytes((tm, tn), out_dtype)
            + 2 * _nbytes((tm, tn), F32))
    body, ins, in_specs = _ordered(body, [a, b], [pl.BlockSpec(a_blk, a_map), pl.BlockSpec(b_blk, b_map)], after)
    return _pallas(
        body, name=name, grid=(gi, gj, gk), in_specs=in_specs,
        out_specs=pl.BlockSpec(out_block, out_map),
        out_shape=jax.ShapeDtypeStruct(out_shape, out_dtype),
        scratch_shapes=scratch,
        compiler_params=_params(("parallel", "parallel", "arbitrary"), vmem),
    )(*ins)


V7X_HBM_BYTES_PER_US = 3.0e6
V7X_MXU_FLOPS_PER_US = 0.9e9
V7X_VMEM_RMW_BYTES_PER_US = 10e6
GRID_STEP_US = 0.35
MATMUL_VMEM_BUDGET = 40 << 20
MATMUL_MAX_TILE_FLOPS = 1 << 33


def _divisors(n, mult, lo):
    return [t for t in range(mult, n + 1, mult) if n % t == 0 and t >= min(lo, n)]


def _mm_tiles(m, n, k, out_dtype=F32, n_unit=None, k_unit=None):
    out_bytes = jnp.dtype(out_dtype).itemsize
    best = None
    for tm in _divisors(m, 128, 256):
        for tn in _divisors(n_unit or n, 128, 256):
            for tk in _divisors(k_unit or k, 128, 512):
                gi, gj, gk = m // tm, n // tn, k // tk
                vmem = 4 * tm * tk + 4 * tk * tn + 2 * tm * tn * out_bytes + 4 * tm * tn * (2 if gk > 1 else 1)
                if vmem > MATMUL_VMEM_BUDGET or 2 * tm * tn * tk > MATMUL_MAX_TILE_FLOPS:
                    continue
                a_bytes = 2 * m * k * (gj if gk > 1 else 1)
                b_bytes = 2 * k * n * (1 if gj == 1 and gk == 1 else gi)
                hbm_us = (a_bytes + b_bytes + m * n * out_bytes) / V7X_HBM_BYTES_PER_US
                acc_us = (8 * m * n * gk / V7X_VMEM_RMW_BYTES_PER_US) if gk > 1 else 0.0
                cost = max(2 * m * n * k / V7X_MXU_FLOPS_PER_US, 1.3 * hbm_us) + GRID_STEP_US * gi * gj * gk + acc_us
                key = (round(cost, 1), vmem)
                if best is None or key < best[0]:
                    best = (key, (tm, tn, tk))
    return best[1]


def _mm(name, a, b, form, out_dtype, after=None):
    (m, k), n = a.shape, (b.shape[1] if form == "nn" else b.shape[0])
    tm, tn, tk = _mm_tiles(m, n, k, out_dtype)
    return _matmul(name, a, b, form=form, out_dtype=out_dtype, tm=tm, tn=tn, tk=tk, after=after)


def _row_tile(t):
    return _tile(t, 256, 8)


def _norm_fwd(name, x, gain):
    t, d = x.shape
    tm = _row_tile(t)

    def body(x_ref, g_ref, h_ref):
        xv = x_ref[...]
        h_ref[...] = (xv * _rms(xv) * g_ref[...]).astype(BF16)

    row = pl.BlockSpec((tm, d), lambda i: (i, 0))
    vec = pl.BlockSpec((1, d), lambda i: (0, 0))
    return _pallas(
        body, name=name, grid=(t // tm,), in_specs=[row, vec], out_specs=row,
        out_shape=jax.ShapeDtypeStruct((t, d), BF16),
        compiler_params=_params(("parallel",), 2 * _nbytes((tm, d), F32)),
    )(x, gain)


def _resid_norm_fwd(name, xres, ff, gpost, gpre, scale):
    t, d = xres.shape
    tm = _row_tile(t)

    def body(x_ref, f_ref, gp_ref, gn_ref, xn_ref, h_ref):
        f = f_ref[...]
        xn = x_ref[...] + scale * (f * _rms(f) * gp_ref[...])
        xn_ref[...] = xn
        h_ref[...] = (xn * _rms(xn) * gn_ref[...]).astype(BF16)

    row = pl.BlockSpec((tm, d), lambda i: (i, 0))
    vec = pl.BlockSpec((1, d), lambda i: (0, 0))
    return _pallas(
        body, name=name, grid=(t // tm,), in_specs=[row, row, vec, vec], out_specs=[row, row],
        out_shape=[jax.ShapeDtypeStruct((t, d), F32), jax.ShapeDtypeStruct((t, d), BF16)],
        compiler_params=_params(("parallel",), 4 * _nbytes((tm, d), F32)),
    )(xres, ff, gpost, gpre)


def _final_fwd_bwd(name, xres, ff, gpost, target, scale):
    t, d = xres.shape
    tm = _row_tile(t)

    def body(x_ref, f_ref, gp_ref, t_ref, loss_ref, dy_ref, dff_ref, dg_ref):
        i = pl.program_id(0)
        f = f_ref[...]
        gp = gp_ref[...]
        y = x_ref[...] + scale * (f * _rms(f) * gp)
        err = y - t_ref[...]
        part = 0.5 * jnp.sum(jnp.mean(err * err, axis=-1, keepdims=True), axis=0, keepdims=True)
        _accumulate(loss_ref, jnp.broadcast_to(part, loss_ref.shape), i == 0)
        dy = err / d
        dy_ref[...] = dy
        dff, dg = _norm_bwd(scale * dy, f, gp)
        dff_ref[...] = dff.astype(BF16)
        _accumulate(dg_ref, dg, i == 0)

    row = pl.BlockSpec((tm, d), lambda i: (i, 0))
    vec = pl.BlockSpec((1, d), lambda i: (0, 0))
    return _pallas(
        body, name=name, grid=(t // tm,), in_specs=[row, row, vec, row],
        out_specs=[pl.BlockSpec((8, 128), lambda i: (0, 0)), row, row, vec],
        out_shape=[jax.ShapeDtypeStruct((8, 128), F32), jax.ShapeDtypeStruct((t, d), F32),
                   jax.ShapeDtypeStruct((t, d), BF16), jax.ShapeDtypeStruct((1, d), F32)],
        compiler_params=_params(("arbitrary",), 5 * _nbytes((tm, d), F32)),
    )(xres, ff, gpost, target)


def _norms_bwd(name, dres, dh, xin, gpre, post=None, after=None):
    t, d = dres.shape
    tm = _row_tile(t)
    with_post = post is not None

    def body(*refs):
        if with_post:
            dr_ref, dh_ref, x_ref, g_ref, f_ref, gp_ref, dx_ref, dg_ref, dff_ref, dgp_ref = refs
        else:
            dr_ref, dh_ref, x_ref, g_ref, dx_ref, dg_ref = refs
        i = pl.program_id(0)
        dx, dg = _norm_bwd(dh_ref[...], x_ref[...], g_ref[...])
        dx = dr_ref[...] + dx
        dx_ref[...] = dx
        _accumulate(dg_ref, dg, i == 0)
        if with_post:
            dff, dgp = _norm_bwd(post[2] * dx, f_ref[...], gp_ref[...])
            dff_ref[...] = dff.astype(BF16)
            _accumulate(dgp_ref, dgp, i == 0)

    row = pl.BlockSpec((tm, d), lambda i: (i, 0))
    vec = pl.BlockSpec((1, d), lambda i: (0, 0))
    ins, in_specs = [dres, dh, xin, gpre], [row, row, row, vec]
    out_specs = [row, vec]
    out_shape = [jax.ShapeDtypeStruct((t, d), F32), jax.ShapeDtypeStruct((1, d), F32)]
    if with_post:
        ins += [post[0], post[1]]
        in_specs += [row, vec]
        out_specs += [row, vec]
        out_shape += [jax.ShapeDtypeStruct((t, d), BF16), jax.ShapeDtypeStruct((1, d), F32)]
    body, ins, in_specs = _ordered(body, ins, in_specs, after)
    return _pallas(
        body, name=name, grid=(t // tm,), in_specs=in_specs, out_specs=out_specs, out_shape=out_shape,
        compiler_params=_params(("arbitrary",), 6 * _nbytes((tm, d), F32)),
    )(*ins)


SWIGLU_TILE = (1024, 512)
DACT_TILE = (2048, 512)


def _ffn_gate_up_act(name, h, w_gu):
    t, d = h.shape
    f = w_gu.shape[1] // 2
    tm, tn = _tile(t, SWIGLU_TILE[0], 128), _tile(f, SWIGLU_TILE[1], 128)
    nf = f // tn

    def body(h_ref, wg_ref, wu_ref, a_ref, dg_ref, du_ref):
        hv = h_ref[...]
        g = _dot(hv, wg_ref[...])
        u = _dot(hv, wu_ref[...])
        sig = _sigmoid(g)
        silu = g * sig
        a_ref[...] = (silu * u).astype(BF16)
        dg_ref[...] = (u * sig * (1.0 + g * (1.0 - sig))).astype(BF16)
        du_ref[...] = silu.astype(BF16)

    out = jax.ShapeDtypeStruct((t, f), BF16)
    blk = pl.BlockSpec((tm, tn), lambda i, j: (i, j))
    return _pallas(
        body, name=name, grid=(t // tm, nf),
        in_specs=[pl.BlockSpec((tm, d), lambda i, j: (i, 0)), pl.BlockSpec((d, tn), lambda i, j: (0, j)),
                  pl.BlockSpec((d, tn), lambda i, j: (0, j + nf))],
        out_specs=[blk, blk, blk], out_shape=[out, out, out],
        compiler_params=_params(("parallel", "parallel"),
                                _nbytes((tm, d), BF16) + 2 * _nbytes((d, tn), BF16) + 5 * _nbytes((tm, tn), F32)),
    )(h, w_gu, w_gu)


def _ffn_dact(name, dff, w_down, dact_dgate, dact_dup, after=None):
    t, d = dff.shape
    f = w_down.shape[0]
    tm, tn = _tile(t, DACT_TILE[0], 128), _tile(f, DACT_TILE[1], 128)

    def body(d_ref, w_ref, dg_ref, du_ref, o_ref):
        da = _dot(d_ref[...], w_ref[...], 1, 1)
        o_ref[0] = (da * dg_ref[...].astype(F32)).astype(BF16)
        o_ref[1] = (da * du_ref[...].astype(F32)).astype(BF16)

    blk = pl.BlockSpec((tm, tn), lambda i, j: (i, j))
    body, ins, in_specs = _ordered(
        body, [dff, w_down, dact_dgate, dact_dup],
        [pl.BlockSpec((tm, d), lambda i, j: (i, 0)), pl.BlockSpec((tn, d), lambda i, j: (j, 0)), blk, blk], after)
    return _pallas(
        body, name=name, grid=(t // tm, f // tn), in_specs=in_specs,
        out_specs=pl.BlockSpec((2, tm, tn), lambda i, j: (0, i, j)),
        out_shape=jax.ShapeDtypeStruct((2, t, f), BF16),
        compiler_params=_params(("parallel", "parallel"),
                                _nbytes((tm, d), BF16) + _nbytes((tn, d), BF16) + 5 * _nbytes((tm, tn), F32)),
    )(*ins)


def _ffn_dh(name, dgu, w_gu, after=None):
    _, t, f = dgu.shape
    d = w_gu.shape[0]
    tm, tn, tk = 1024, 1024, 5632
    nkf = f // tk
    return _matmul(name, dgu, w_gu, form="nt", out_dtype=F32, tm=tm, tn=tn, tk=tk, sizes=(t, d, 2 * f),
                   a_map=lambda i, j, kk: (kk // nkf, i, kk % nkf), after=after)


def _lower_bound(lbp):
    m = jnp.max(lbp, axis=0, keepdims=True)
    e = jnp.exp(lbp - m)
    return e[0:1] / jnp.sum(e, axis=0, keepdims=True)


def _chunk_mask(reverse):
    row = lax.broadcasted_iota(jnp.int32, (CHUNK, CHUNK), 0)
    col = lax.broadcasted_iota(jnp.int32, (CHUNK, CHUNK), 1)
    return (col >= row) if reverse else (col <= row)


def _hgrn_gates(z, lb, mask_bf):
    sig = _sigmoid(z)
    f = lb + (1.0 - lb) * sig
    logf = jnp.log(f)
    k = 1.0 - f
    cum = _dot_exact(mask_bf, logf)
    last = jnp.sum(logf, axis=0, keepdims=True)
    return sig, f, k, cum, last


def _hgrn_scan_fwd(name, p, lbp_f, lbp_b):
    t = p.shape[0]
    hw = lbp_f.shape[1]
    nh, nc = hw // HEAD, t // CHUNK

    def body(qf, vf, zf, qb, vb, zb, lbf, lbb, of_ref, ob_ref, stf_ref, stb_ref, state):
        n = pl.program_id(0)

        @pl.when(n == 0)
        def _():
            state[...] = jnp.zeros_like(state)

        directions = [(qf, vf, zf, lbf, of_ref, stf_ref), (qb, vb, zb, lbb, ob_ref, stb_ref)]
        wide = []
        for d, (q_ref, v_ref, z_ref, lb_ref, o_ref, st_ref) in enumerate(directions):
            mask = _chunk_mask(d == 1)
            lb = _lower_bound(lb_ref[...])
            _, _, k, cum, last = _hgrn_gates(z_ref[...], lb, mask.astype(BF16))
            v = v_ref[...].astype(BF16)
            qd = (q_ref[...] * jnp.exp(cum)).astype(BF16)
            kd = (k * jnp.exp(-cum)).astype(BF16)
            kt = (k * jnp.exp(last - cum)).astype(BF16)
            s_all = state[d]
            st_ref[...] = s_all
            wide.append((mask, v, qd, kd, kt, jnp.exp(last), s_all, o_ref))
        pairs = [(d, slice(h * HEAD, (h + 1) * HEAD)) for d in range(2) for h in range(nh)]
        a = [jnp.where(wide[d][0], _dot(wide[d][2][:, sl], wide[d][3][:, sl], 1, 1), 0.0).astype(BF16)
             for d, sl in pairs]
        inter = [_dot(wide[d][2][:, sl], wide[d][6][:, sl].astype(BF16), 1, 1) for d, sl in pairs]
        intra = [_dot(a[i], wide[d][1][:, sl]) for i, (d, sl) in enumerate(pairs)]
        grow = [_dot(wide[d][1][:, sl], wide[d][4][:, sl], 0, 0) for d, sl in pairs]
        for i, (d, sl) in enumerate(pairs):
            wide[d][7][:, sl] = intra[i] + inter[i]
            state[d, :, sl] = wide[d][6][:, sl] * wide[d][5][:, sl] + grow[i]

    def col(group, reverse):
        return pl.BlockSpec((CHUNK, hw), lambda n: ((nc - 1 - n) if reverse else n, group))

    def st(reverse):
        return pl.BlockSpec((None, HEAD, hw), lambda n: ((nc - 1 - n) if reverse else n, 0, 0))

    lb_spec = pl.BlockSpec((2, hw), lambda n: (0, 0))
    out = jax.ShapeDtypeStruct((t, hw), F32)
    states = jax.ShapeDtypeStruct((nc, HEAD, hw), F32)
    return _pallas(
        body, name=name, grid=(nc,),
        in_specs=[col(0, False), col(1, False), col(2, False), col(0, True), col(1, True), col(3, True),
                  lb_spec, lb_spec],
        out_specs=[col(0, False), col(0, True), st(False), st(True)],
        out_shape=[out, out, states, states],
        scratch_shapes=[pltpu.VMEM((2, HEAD, hw), F32)],
        compiler_params=_params(("arbitrary",), 12 * _nbytes((HEAD, hw), F32)),
    )(p, p, p, p, p, p, lbp_f, lbp_b)


def _hgrn_scan_bwd(name, p, lbp_f, lbp_b, do, st_f, st_b):
    t = p.shape[0]
    hw = lbp_f.shape[1]
    nh, nc = hw // HEAD, t // CHUNK

    def body(qf, vf, zf, dof, sf, qb, vb, zb, dob, sb, lbf, lbb, dqf, dvf, dzf, dlbf, dqb, dvb, dzb, dlbb,
             dstate, dlb_acc, dqd_s, dkd_s, dkt_s, ddec_s):
        n = pl.program_id(0)

        @pl.when(n == 0)
        def _():
            dstate[...] = jnp.zeros_like(dstate)
            dlb_acc[...] = jnp.zeros_like(dlb_acc)

        directions = [(qf, vf, zf, dof, sf, lbf, dqf, dvf, dzf, dlbf), (qb, vb, zb, dob, sb, lbb, dqb, dvb, dzb, dlbb)]
        for d, (q_ref, v_ref, z_ref, do_ref, st_ref, lb_ref, dq_ref, dv_ref, dz_ref, dlb_ref) in enumerate(directions):
            mask = _chunk_mask(d == 1)
            mask_bf = mask.astype(BF16)
            lb = _lower_bound(lb_ref[...])
            sig, f, k, cum, last = _hgrn_gates(z_ref[...], lb, mask_bf)
            e_pos, e_neg, e_tail = jnp.exp(cum), jnp.exp(-cum), jnp.exp(last - cum)
            dec = jnp.exp(last)
            v = v_ref[...].astype(BF16)
            qd, kd, kt = q_ref[...] * e_pos, k * e_neg, k * e_tail
            qd_bf, kd_bf, kt_bf = qd.astype(BF16), kd.astype(BF16), kt.astype(BF16)
            s_all = st_ref[...]
            ds_all = dstate[d]
            dov = do_ref[...].astype(BF16)
            cols = [slice(h * HEAD, (h + 1) * HEAD) for h in range(nh)]
            s_bf = [s_all[:, sl].astype(BF16) for sl in cols]
            ds_bf = [ds_all[:, sl].astype(BF16) for sl in cols]
            a = [jnp.where(mask, _dot(qd_bf[:, sl], kd_bf[:, sl], 1, 1), 0.0).astype(BF16) for sl in cols]
            da = [jnp.where(mask, _dot(dov[:, sl], v[:, sl], 1, 1), 0.0).astype(BF16) for sl in cols]
            dv_h = [_dot(a[h], dov[:, sl], 0, 0) + _dot(kt_bf[:, sl], ds_bf[h], 1, 1) for h, sl in enumerate(cols)]
            dqd_h = [_dot(da[h], kd_bf[:, sl]) + _dot(dov[:, sl], s_bf[h]) for h, sl in enumerate(cols)]
            dkd_h = [_dot(da[h], qd_bf[:, sl], 0, 0) for h, sl in enumerate(cols)]
            dkt_h = [_dot(v[:, sl], ds_bf[h]) for h, sl in enumerate(cols)]
            dst_h = [_dot(dov[:, sl], qd_bf[:, sl], 0, 0) + ds_all[:, sl] * dec[:, sl] for sl in cols]
            for h, sl in enumerate(cols):
                dv_ref[:, sl] = dv_h[h]
                dqd_s[:, sl] = dqd_h[h]
                dkd_s[:, sl] = dkd_h[h]
                dkt_s[:, sl] = dkt_h[h]
                dstate[d, :, sl] = dst_h[h]
                ddec_s[:, sl] = jnp.sum(ds_all[:, sl] * s_all[:, sl], axis=0, keepdims=True)
            dqd, dkd, dkt = dqd_s[...], dkd_s[...], dkt_s[...]
            dlast = jnp.sum(dkt * kt, axis=0, keepdims=True) + dec * ddec_s[...]
            dq_ref[...] = dqd * e_pos
            dk = dkd * e_neg + dkt * e_tail
            dcum = dqd * qd - dkd * kd - dkt * kt
            dlogf = _dot_exact(mask_bf, dcum, 0, 0) + dlast
            df = dlogf / f - dk
            dz_ref[...] = df * (1.0 - lb) * sig * (1.0 - sig)
            dlb_acc[d] += jnp.sum(df * (1.0 - sig), axis=0, keepdims=True)

            @pl.when(n == nc - 1)
            def _():
                g = dlb_acc[d] * lb * (1.0 - lb)
                dlb_ref[0:1, :] = g
                dlb_ref[1:2, :] = -g

    def col(group, reverse):
        return pl.BlockSpec((CHUNK, hw), lambda n: (n if reverse else (nc - 1 - n), group))

    def st(reverse):
        return pl.BlockSpec((None, HEAD, hw), lambda n: (n if reverse else (nc - 1 - n), 0, 0))

    lb_spec = pl.BlockSpec((2, hw), lambda n: (0, 0))
    out = jax.ShapeDtypeStruct((t, hw), F32)
    dlb = jax.ShapeDtypeStruct((2, hw), F32)
    wide = pltpu.VMEM((CHUNK, hw), F32)
    return _pallas(
        body, name=name, grid=(nc,),
        in_specs=[col(0, False), col(1, False), col(2, False), col(0, False), st(False),
                  col(0, True), col(1, True), col(3, True), col(0, True), st(True), lb_spec, lb_spec],
        out_specs=[col(0, False), col(0, False), col(0, False), lb_spec,
                   col(0, True), col(0, True), col(0, True), lb_spec],
        out_shape=[out, out, out, dlb, out, out, out, dlb],
        scratch_shapes=[pltpu.VMEM((2, HEAD, hw), F32), pltpu.VMEM((2, 1, hw), F32), wide, wide, wide,
                        pltpu.VMEM((1, hw), F32)],
        compiler_params=_params(("arbitrary",), 16 * _nbytes((HEAD, hw), F32)),
    )(p, p, p, do, st_f, p, p, p, do, st_b, lbp_f, lbp_b)


def _hgrn_out_fwd(name, o_f, o_b, p, gain, g_group):
    t, hw = o_f.shape
    nh = hw // HEAD
    tm = _tile(t, 256, 16)

    def body(of_ref, ob_ref, g_ref, gain_ref, y_ref):
        o_all = of_ref[...] + ob_ref[...]
        g_all = g_ref[...]
        scale_all = gain_ref[...] * (g_all * _sigmoid(g_all))
        for h in range(nh):
            sl = slice(h * HEAD, (h + 1) * HEAD)
            o = o_all[:, sl]
            y_ref[:, sl] = (o * _rms(o) * scale_all[:, sl]).astype(BF16)

    blk = pl.BlockSpec((tm, hw), lambda i: (i, 0))
    return _pallas(
        body, name=name, grid=(t // tm,),
        in_specs=[blk, blk, pl.BlockSpec((tm, hw), lambda i: (i, g_group)), pl.BlockSpec((1, hw), lambda i: (0, 0))],
        out_specs=blk, out_shape=jax.ShapeDtypeStruct((t, hw), BF16),
        compiler_params=_params(("parallel",), 5 * _nbytes((tm, hw), F32)),
    )(o_f, o_b, p, gain)


def _hgrn_out_bwd(name, dy, o_f, o_b, p, gain, g_group, after=None):
    t, hw = o_f.shape
    nh = hw // HEAD
    tm = _tile(t, 256, 8)

    def body(dy_ref, of_ref, ob_ref, g_ref, gain_ref, do_ref, dg_ref, dgain_ref):
        i = pl.program_id(0)
        o_all = of_ref[...] + ob_ref[...]
        g_all = g_ref[...]
        sig_all = _sigmoid(g_all)
        dy_all = dy_ref[...]
        up_all = dy_all * (g_all * sig_all)
        dsilu_all = dy_all * sig_all * (1.0 + g_all * (1.0 - sig_all))
        gain_all = gain_ref[...]
        for h in range(nh):
            sl = slice(h * HEAD, (h + 1) * HEAD)
            o, gain_v = o_all[:, sl], gain_all[:, sl]
            do, dgain = _norm_bwd(up_all[:, sl], o, gain_v)
            do_ref[:, sl] = do
            dg_ref[:, sl] = dsilu_all[:, sl] * (o * _rms(o) * gain_v)
            _accumulate(dgain_ref.at[:, sl], dgain, i == 0)

    blk = pl.BlockSpec((tm, hw), lambda i: (i, 0))
    vec = pl.BlockSpec((1, hw), lambda i: (0, 0))
    out = jax.ShapeDtypeStruct((t, hw), F32)
    body, ins, in_specs = _ordered(
        body, [dy, o_f, o_b, p, gain], [blk, blk, blk, pl.BlockSpec((tm, hw), lambda i: (i, g_group)), vec], after)
    return _pallas(
        body, name=name, grid=(t // tm,), in_specs=in_specs,
        out_specs=[blk, blk, vec], out_shape=[out, out, jax.ShapeDtypeStruct((1, hw), F32)],
        compiler_params=_params(("arbitrary",), 7 * _nbytes((tm, hw), F32)),
    )(*ins)


def _t5_bucket_ids():
    c = np.arange(WINDOW)[:, None]
    s = np.arange(SPAN)[None, :]
    rel = s - WINDOW - c
    nb = REL_BUCKETS // 2
    max_exact = nb // 2
    bucket = (rel > 0).astype(np.int32) * nb
    n = np.abs(rel)
    large = max_exact + (np.log(np.maximum(n, 1) / max_exact) / np.log(REL_MAX_DIST / max_exact)
                         * (nb - max_exact)).astype(np.int32)
    large = np.minimum(large, nb - 1)
    ids = bucket + np.where(n < max_exact, n, large).astype(np.int32)
    return jnp.asarray(ids.reshape(1, WINDOW * SPAN), jnp.int32)


def _bias_onehot(ids_ref):
    n = ids_ref.shape[1]
    return (lax.broadcasted_iota(jnp.int32, (REL_BUCKETS, n), 0) == ids_ref[...]).astype(BF16)


def _bias_gather(name, table_t, ids):
    nh = table_t.shape[0]

    def body(t_ref, ids_ref, o_ref):
        o_ref[...] = _dot_exact(t_ref[...], _bias_onehot(ids_ref), split="a")

    return _pallas(
        body, name=name, out_shape=jax.ShapeDtypeStruct((nh, ids.shape[1]), F32),
        compiler_params=pltpu.CompilerParams(vmem_limit_bytes=32 << 20),
    )(table_t, ids)


def _bias_scatter(name, dbias, ids):
    nh = dbias.shape[0]

    def body(d_ref, ids_ref, o_ref):
        o_ref[...] = _dot_exact(d_ref[...], _bias_onehot(ids_ref), 1, 1, split="a")

    return _pallas(
        body, name=name, out_shape=jax.ShapeDtypeStruct((nh, REL_BUCKETS), F32),
        compiler_params=pltpu.CompilerParams(vmem_limit_bytes=32 << 20),
    )(dbias, ids)


def _attn_valid(i, t):
    c = lax.broadcasted_iota(jnp.int32, (WINDOW, SPAN), 0)
    s = lax.broadcasted_iota(jnp.int32, (WINDOW, SPAN), 1)
    rel = s - WINDOW - c
    pos = i * WINDOW - WINDOW + s
    return (jnp.abs(rel) <= WINDOW) & (pos >= 0) & (pos < t)


def _attn_probs(qs, khs, b_ref, s_ref, valid):
    heads = range(len(qs))
    sinks = [s_ref[0:1, h:h + 1] for h in heads]
    s = [_dot(qs[h], khs[h], 1, 1) / math.sqrt(HEAD) for h in heads]
    s = [jnp.where(valid, s[h] + b_ref[h], NEG_INF) for h in heads]
    m = [jnp.maximum(jnp.max(s[h], axis=-1, keepdims=True), sinks[h]) for h in heads]
    e = [jnp.exp(s[h] - m[h]) for h in heads]
    es = [jnp.exp(sinks[h] - m[h]) for h in heads]
    inv = [1.0 / (jnp.sum(e[h], axis=-1, keepdims=True) + es[h]) for h in heads]
    return [e[h] * inv[h] for h in heads], [es[h] * inv[h] for h in heads]


def _kv_window_specs(kv_blk, kvw, nb):
    return [pl.BlockSpec((WINDOW, kvw), lambda i, s=s: (jnp.clip(i + s, 0, nb - 1), kv_blk)) for s in (-1, 0, 1)]


def _kv_window(refs):
    return jnp.concatenate([r[...] for r in refs], axis=0).astype(BF16)


def _attn_fwd(name, p, kv_blk, kvw, bias, sink, q_group_blk):
    t = p.shape[0]
    nh = bias.shape[0]
    aw = nh * HEAD
    grp = nh // KV_HEADS
    nb = t // WINDOW

    def body(q_ref, kp, kc, kn, vp, vc, vn, b_ref, s_ref, y_ref, pr_ref, ps_ref):
        i = pl.program_id(0)
        valid = _attn_valid(i, t)
        ks = _kv_window((kp, kc, kn))
        vs = _kv_window((vp, vc, vn))
        heads = range(nh)
        col = lambda h: slice(h * HEAD, (h + 1) * HEAD)
        qs = [q_ref[:, col(h)].astype(BF16) for h in heads]
        pr, ps = _attn_probs(qs, [ks[:, col(h // grp)] for h in heads], b_ref, s_ref, valid)
        pr = [pr[h].astype(BF16) for h in heads]
        out = [_dot(pr[h], vs[:, col(h // grp)]) for h in heads]
        lane = lax.broadcasted_iota(jnp.int32, (WINDOW, 128), 1)
        sinks = jnp.zeros((WINDOW, 128), F32)
        for h in heads:
            y_ref[:, col(h)] = out[h].astype(BF16)
            pr_ref[h] = pr[h]
            sinks = jnp.where(lane == h, ps[h], sinks)
        ps_ref[...] = sinks

    full = lambda a: pl.BlockSpec(a.shape, lambda i: (0,) * a.ndim)
    return _pallas(
        body, name=name, grid=(nb,),
        in_specs=[pl.BlockSpec((WINDOW, aw), lambda i: (i, q_group_blk)), *_kv_window_specs(kv_blk, kvw, nb),
                  *_kv_window_specs(kv_blk + 1, kvw, nb), full(bias), full(sink)],
        out_specs=[pl.BlockSpec((WINDOW, aw), lambda i: (i, 0)), pl.BlockSpec((nh, WINDOW, SPAN), lambda i: (0, i, 0)),
                   pl.BlockSpec((WINDOW, 128), lambda i: (i, 0))],
        out_shape=[jax.ShapeDtypeStruct((t, aw), BF16), jax.ShapeDtypeStruct((nh, t, SPAN), BF16),
                   jax.ShapeDtypeStruct((t, 128), F32)],
        compiler_params=_params(("parallel",), 3 * _nbytes(bias.shape, F32)),
    )(p, p, p, p, p, p, p, bias, sink)


def _attn_bwd(name, p, kv_blk, kvw, probs, sink_probs, dy, q_group_blk, dy_blk, after=None):
    t = p.shape[0]
    nh = probs.shape[0]
    aw = nh * HEAD
    grp = nh // KV_HEADS
    nb = t // WINDOW

    def body(q_ref, kp, kc, kn, vp, vc, vn, pr_ref, ps_ref, dy_ref, dq_ref, dk_ref, dv_ref, db_ref, ds_ref):
        i = pl.program_id(0)

        @pl.when(i == 0)
        def _():
            dk_ref[...] = jnp.zeros_like(dk_ref)
            dv_ref[...] = jnp.zeros_like(dv_ref)
            db_ref[...] = jnp.zeros_like(db_ref)
            ds_ref[...] = jnp.zeros_like(ds_ref)

        start = pl.multiple_of(i * WINDOW, WINDOW)
        ks = _kv_window((kp, kc, kn))
        vs = _kv_window((vp, vc, vn))
        inv_sqrt = 1.0 / math.sqrt(HEAD)
        heads = range(nh)
        col = lambda h: slice(h * HEAD, (h + 1) * HEAD)
        qs = [q_ref[:, col(h)].astype(BF16) for h in heads]
        khs = [ks[:, col(h // grp)] for h in heads]
        pr_bf = [pr_ref[h] for h in heads]
        pr = [pr_bf[h].astype(F32) for h in heads]
        dos = [dy_ref[:, col(h)].astype(BF16) for h in heads]
        dp = [_dot(dos[h], vs[:, col(h // grp)], 1, 1) for h in heads]
        delta = [jnp.sum(pr[h] * dp[h], axis=-1, keepdims=True) for h in heads]
        dsc = [pr[h] * (dp[h] - delta[h]) for h in heads]
        dsr = [(dsc[h] * inv_sqrt).astype(BF16) for h in heads]
        dq = [_dot(dsr[h], khs[h]) for h in heads]
        dk = [_dot(dsr[h], qs[h], 0, 0) for h in heads]
        dv = [_dot(pr_bf[h], dos[h], 0, 0) for h in heads]
        for h in heads:
            db_ref[h] += dsc[h]
            dsink = jnp.sum(-ps_ref[:, h:h + 1] * delta[h], axis=0, keepdims=True)
            ds_ref[h:h + 1, :] += jnp.broadcast_to(dsink, (1, 128))
            dq_ref[:, col(h)] = dq[h]
        for kv in range(KV_HEADS):
            group = range(kv * grp, (kv + 1) * grp)
            dk_ref[pl.ds(start, SPAN), col(kv)] += sum(dk[h] for h in group)
            dv_ref[pl.ds(start, SPAN), col(kv)] += sum(dv[h] for h in group)

    whole = lambda shape: pl.BlockSpec(shape, lambda i: (0,) * len(shape))
    pad_shape = (t + 2 * WINDOW, kvw)
    bias_shape = (nh, WINDOW, SPAN)
    body, ins, in_specs = _ordered(
        body, [p, p, p, p, p, p, p, probs, sink_probs, dy],
        [pl.BlockSpec((WINDOW, aw), lambda i: (i, q_group_blk)), *_kv_window_specs(kv_blk, kvw, nb),
         *_kv_window_specs(kv_blk + 1, kvw, nb),
         pl.BlockSpec((nh, WINDOW, SPAN), lambda i: (0, i, 0)), pl.BlockSpec((WINDOW, 128), lambda i: (i, 0)),
         pl.BlockSpec((WINDOW, aw), lambda i: (i, dy_blk))], after)
    return _pallas(
        body, name=name, grid=(nb,), in_specs=in_specs,
        out_specs=[pl.BlockSpec((WINDOW, aw), lambda i: (i, 0)), whole(pad_shape), whole(pad_shape),
                   whole(bias_shape), whole((nh, 128))],
        out_shape=[jax.ShapeDtypeStruct((t, aw), F32), jax.ShapeDtypeStruct(pad_shape, F32),
                   jax.ShapeDtypeStruct(pad_shape, F32), jax.ShapeDtypeStruct(bias_shape, F32),
                   jax.ShapeDtypeStruct((nh, 128), F32)],
        compiler_params=_params(("arbitrary",), 3 * _nbytes(pad_shape, F32) + 3 * _nbytes(bias_shape, F32)),
    )(*ins)


def _mix_dproj(name, pieces, kv_pads, t, after=None):
    hw = pieces[0][0].shape[1]
    kvw = kv_pads[0].shape[1]
    widths = [hw] * len(pieces) + [kvw] * len(kv_pads)
    total = sum(widths)
    tm = WINDOW
    flat = [a for pc in pieces for a in pc]

    def body(*refs):
        o_ref = refs[-1]
        pos, off = 0, 0
        for pc in pieces:
            val = refs[pos][...]
            for extra in range(1, len(pc)):
                val = val + refs[pos + extra][...]
            o_ref[:, off:off + hw] = val.astype(BF16)
            pos += len(pc)
            off += hw
        for _ in kv_pads:
            o_ref[:, off:off + kvw] = refs[pos][...].astype(BF16)
            pos += 1
            off += kvw

    in_specs = [pl.BlockSpec((tm, hw), lambda i: (i, 0)) for _ in flat]
    in_specs += [pl.BlockSpec((tm, kvw), lambda i: (i + 1, 0)) for _ in kv_pads]
    body, ins, in_specs = _ordered(body, [*flat, *kv_pads], in_specs, after)
    return _pallas(
        body, name=name, grid=(t // tm,), in_specs=in_specs,
        out_specs=pl.BlockSpec((tm, total), lambda i: (i, 0)),
        out_shape=jax.ShapeDtypeStruct((t, total), BF16),
        compiler_params=_params(("parallel",), 3 * _nbytes((tm, total), F32)),
    )(*ins)


def _concat_cols(name, a, b):
    t, wa = a.shape
    wb = b.shape[1]
    tm = _tile(t, 512, 16)

    def body(a_ref, b_ref, o_ref):
        o_ref[:, :wa] = a_ref[...]
        o_ref[:, wa:] = b_ref[...]

    return _pallas(
        body, name=name, grid=(t // tm,),
        in_specs=[pl.BlockSpec((tm, wa), lambda i: (i, 0)), pl.BlockSpec((tm, wb), lambda i: (i, 0))],
        out_specs=pl.BlockSpec((tm, wa + wb), lambda i: (i, 0)),
        out_shape=jax.ShapeDtypeStruct((t, wa + wb), a.dtype),
        compiler_params=_params(("parallel",), 2 * _nbytes((tm, wa + wb), a.dtype)),
    )(a, b)


def _cast_into_full(name, w, geom, idx, after=None):
    r, c = w.shape
    tr = _tile(r, 256, 16)
    nr = r // tr
    if geom.col:
        place = lambda i, iref: (i, iref[0])
    else:
        place = lambda i, iref: (iref[0] * nr + i, 0)

    def body(i_ref, w_ref, *rest):
        rest[-1][...] = w_ref[...].astype(BF16)

    in_specs = [pl.BlockSpec((tr, c), lambda i, iref: (i, 0))]
    ins = [w]
    if after is not None:
        in_specs.append(pl.BlockSpec(memory_space=pl.ANY))
        ins.append(after)
    return _pallas(
        body, name=name,
        grid_spec=pltpu.PrefetchScalarGridSpec(
            num_scalar_prefetch=1, grid=(nr,), in_specs=in_specs, out_specs=pl.BlockSpec((tr, c), place)),
        out_shape=pltpu.HBM(geom.full_shape, BF16),
        compiler_params=_params(("parallel",), 2 * _nbytes((tr, c), F32)),
    )(idx, *ins)


def _adamw(name, w, g, m, v):
    r, c = w.shape
    tr = _tile(r, 256, 8)
    bc1 = 1.0 - ADAM_B1 ** ADAM_STEP
    bc2 = 1.0 - ADAM_B2 ** ADAM_STEP

    def body(w_ref, g_ref, m_ref, v_ref, go_ref, d_ref, nm_ref, nv_ref):
        gv = g_ref[...]
        go_ref[...] = gv
        nm = ADAM_B1 * m_ref[...] + (1.0 - ADAM_B1) * gv
        nv = ADAM_B2 * v_ref[...] + (1.0 - ADAM_B2) * (gv * gv)
        nm_ref[...] = nm
        nv_ref[...] = nv
        d_ref[...] = -ADAM_LR * ((nm / bc1) / (jnp.sqrt(nv / bc2) + ADAM_EPS) + ADAM_WD * w_ref[...])

    blk = pl.BlockSpec((tr, c), lambda i: (i, 0))
    out = jax.ShapeDtypeStruct((r, c), F32)
    return _pallas(
        body, name=name, grid=(r // tr,), in_specs=[blk] * 4, out_specs=[blk] * 4, out_shape=[out] * 4,
        compiler_params=_params(("parallel",), 8 * _nbytes((tr, c), F32)),
    )(w, g, m, v)


def _mesh_pos():
    return lax.axis_index("x"), lax.axis_index("y"), lax.axis_index("c")


def _other_chips(x, y):
    return [(1 - x, y), (x, 1 - y), (1 - x, 1 - y)]


class _Big:
    def __init__(self, shard_shape, col_sharded):
        self.col = col_sharded
        r, c = shard_shape
        self.shard_shape = (r, c)
        self.full_shape = (r, N_CHIPS * c) if col_sharded else (N_CHIPS * r, c)
        self.half_shape = (r // 2, N_CHIPS * c) if col_sharded else (N_CHIPS * r, c // 2)
        self.shard_half_shape = (r // 2, c) if col_sharded else (r, c // 2)

    def region(self, ref, s, half=None):
        r, c = self.shard_shape
        if self.col:
            rows = slice(None) if half is None else pl.ds(half * (r // 2), r // 2)
            return ref.at[rows, pl.ds(s * c, c)]
        cols = slice(None) if half is None else pl.ds(half * (c // 2), c // 2)
        return ref.at[pl.ds(s * r, r), cols]

    def n_halves(self, ref, half, n):
        r, c = self.shard_shape
        if self.col:
            return ref.at[pl.ds(half * (r // 2), r // 2), pl.ds(0, n * c)]
        return ref.at[pl.ds(0, n * r), pl.ds(half * (c // 2), c // 2)]

    def sub_half(self, ref, s, half, j):
        r, c = self.shard_shape
        if self.col:
            return ref.at[pl.ds(half * (r // 2) + j * (r // 4), r // 4), pl.ds(s * c, c)]
        return ref.at[pl.ds(s * r + j * (r // 2), r // 2), pl.ds(half * (c // 2), c // 2)]

    def half_of_shard(self, ref, half):
        r, c = self.shard_shape
        if self.col:
            return ref.at[pl.ds(half * (r // 2), r // 2), :]
        return ref.at[:, pl.ds(half * (c // 2), c // 2)]

    def shard_of_half(self, ref, s):
        r, c = self.shard_shape
        if self.col:
            return ref.at[:, pl.ds(s * c, c)]
        return ref.at[pl.ds(s * r, r), :]


HBM =pl.BlockSpec(memory_space=pltpu.HBM)
SEM = pl.BlockSpec(memory_space=pltpu.SEMAPHORE)
SPLIT_COPY = pltpu.CompilerParams(has_side_effects=pltpu.SideEffectType.DATAFLOW_SIDE_EFFECTING)


def _in_hbm(a):
    return pltpu.with_memory_space_constraint(a, pltpu.HBM)


def _gather_start(name, fulls, geoms, after):
    nw = len(fulls)

    def body(*refs):
        dst = refs[nw + 1:2 * nw + 1]
        sems = refs[2 * nw + 1:-1]
        x, y, c = _mesh_pos()
        mine = 2 * x + y
        for w in range(nw):
            own_half = geoms[w].region(dst[w], mine, c)
            for chip in _other_chips(x, y):
                pltpu.make_async_remote_copy(src_ref=own_half, dst_ref=own_half, send_sem=sems[2 * w],
                                             recv_sem=sems[2 * w + 1], device_id=(*chip, c),
                                             device_id_type=MESH).start()
        refs[-1][...] = jnp.zeros_like(refs[-1])

    out = _pallas(
        body, name=name, in_specs=[HBM] * nw + [pl.BlockSpec(memory_space=pl.ANY)],
        out_specs=[HBM] * nw + [SEM] * (2 * nw) + [pl.BlockSpec(memory_space=pltpu.VMEM)],
        out_shape=[pltpu.HBM(g.full_shape, BF16) for g in geoms] + [pltpu.SemaphoreType.DMA(())] * (2 * nw)
        + [jax.ShapeDtypeStruct((8, 128), F32)],
        input_output_aliases={w: w for w in range(nw)}, compiler_params=SPLIT_COPY,
    )(*[_in_hbm(a) for a in fulls], after)
    return list(out[:nw]), [(out[nw + 2 * w], out[nw + 2 * w + 1]) for w in range(nw)], out[-1]


def _gather_first_direct(full, geom):
    def start(refs, _, new):
        x, y, c = _mesh_pos()
        own = geom.region(refs[0], 2 * x + y, c)
        for chip in ((1 - x, y), (x, 1 - y)):
            _remote(own, own, new, (*chip, c)).start()

    return _split_copy_call("gather_first_direct", [full], start, new_sems=2)


def _gather_first_relay(full, geom, sems, after):
    def relay(refs, got, new):
        x, y, c = _mesh_pos()
        w = refs[0]
        two = geom.n_halves(w, c, 2)
        _remote(two, two, got, (x, y, 1 - c)).wait_recv()
        from_x = geom.sub_half(w, 2 * (1 - x) + y, c, 0)
        from_y = geom.sub_half(w, 2 * x + (1 - y), c, 1)
        _remote(from_x, from_x, new, (x, 1 - y, c)).start()
        _remote(from_y, from_y, new, (1 - x, y, c)).start()
        _remote(two, two, got, (x, y, 1 - c)).wait_send()

    return _split_copy_call("gather_first_relay", [full], relay, sems=sems, after=after, new_sems=2)


def _gather_forward(name, full, geom, sems, after, arrivals):
    def body(w_in, send_sem, recv_sem, after_ref, w_ref, fwd_send, fwd_recv):
        x, y, c = _mesh_pos()
        sibling = (x, y, 1 - c)
        landed_all = geom.n_halves(w_ref, c, arrivals)
        _remote(landed_all, landed_all, (send_sem, recv_sem), sibling).wait_recv()
        for chip in _other_chips(x, y):
            landed = geom.region(w_ref, 2 * chip[0] + chip[1], c)
            pltpu.make_async_remote_copy(src_ref=landed, dst_ref=landed, send_sem=fwd_send, recv_sem=fwd_recv,
                                         device_id=sibling, device_id_type=MESH).start()
        _remote(landed_all, landed_all, (send_sem, recv_sem), sibling).wait_send()

    sem = pltpu.SemaphoreType.DMA(())
    out = _pallas(
        body, name=name, in_specs=[HBM, SEM, SEM, pl.BlockSpec(memory_space=pl.ANY)], out_specs=[HBM, SEM, SEM],
        out_shape=[pltpu.HBM(geom.full_shape, BF16), sem, sem],
        input_output_aliases={0: 0}, compiler_params=SPLIT_COPY,
    )(full, sems[0], sems[1], after)
    return out[0], (out[1], out[2])


def _gather_end(name, full, geom, sems, after):
    def body(w_in, fwd_send, fwd_recv, after_ref, w_ref):
        x, y, c = _mesh_pos()
        sibling = (x, y, 1 - c)
        theirs, ours = geom.n_halves(w_ref, 1 - c, 3), geom.n_halves(w_ref, c, 3)
        _remote(theirs, theirs, (fwd_send, fwd_recv), sibling).wait_recv()
        _remote(ours, ours, (fwd_send, fwd_recv), sibling).wait_send()

    return _pallas(
        body, name=name, in_specs=[HBM, SEM, SEM, pl.BlockSpec(memory_space=pl.ANY)], out_specs=HBM,
        out_shape=pltpu.HBM(geom.full_shape, BF16),
        input_output_aliases={0: 0}, compiler_params=SPLIT_COPY,
    )(full, sems[0], sems[1], after)


def _split_copy_call(name, arrays, fn, sems=(), after=None, new_sems=0):
    n, ns = len(arrays), len(sems)
    n_in = n + ns + (after is not None)

    def body(*refs):
        fn(refs[n_in:n_in + n], refs[n:n + ns], refs[n_in + n:-1])
        refs[-1][...] = jnp.zeros_like(refs[-1])

    ins = list(arrays) if ns else [_in_hbm(a) for a in arrays]
    ins += list(sems) + ([after] if after is not None else [])
    in_specs = [HBM] * n + [SEM] * ns + ([pl.BlockSpec(memory_space=pl.ANY)] if after is not None else [])
    out = _pallas(
        body, name=name, in_specs=in_specs,
        out_specs=[HBM] * n + [SEM] * new_sems + [pl.BlockSpec(memory_space=pltpu.VMEM)],
        out_shape=[pltpu.HBM(a.shape, a.dtype) for a in arrays] + [pltpu.SemaphoreType.DMA(())] * new_sems
        + [jax.ShapeDtypeStruct((8, 128), F32)],
        input_output_aliases={i: i for i in range(n)}, compiler_params=SPLIT_COPY,
    )(*ins)
    return list(out[:n]), tuple(out[n:-1]), out[-1]


def _remote(src, dst, sems, to):
    return pltpu.make_async_remote_copy(src_ref=src, dst_ref=dst, send_sem=sems[0], recv_sem=sems[1],
                                        device_id=to, device_id_type=MESH)


class _GradReduce:
    def __init__(self, name, geom, idx, c_idx):
        self.name, self.geom, self.idx, self.c_idx = name, geom, idx, c_idx

    def pair_start(self, theirs):
        g = self.geom

        def start(refs, _, new):
            x, y, c = _mesh_pos()
            _remote(refs[0], refs[1], new, (x, y, 1 - c)).start()

        self.arrays, self.sems, token = _split_copy_call(
            f"pair_start_{self.name}", [theirs, lax.empty(g.half_shape, BF16)], start, new_sems=2)
        return token

    def pair_wait(self, after):
        def wait(refs, sems, _):
            x, y, c = _mesh_pos()
            copy = _remote(refs[0], refs[1], sems, (x, y, 1 - c))
            copy.wait_send()
            copy.wait_recv()

        (_, landed), _, _ = _split_copy_call(f"pair_wait_{self.name}", self.arrays, wait, self.sems, after)
        return landed

    def chip_start(self, half):
        g = self.geom

        def start(refs, _, new):
            x, y, c = _mesh_pos()
            for k, chip in enumerate(_other_chips(x, y)):
                _remote(g.shard_of_half(refs[0], 2 * chip[0] + chip[1]), refs[1].at[k], new, (*chip, c)).start()

        self.arrays, self.sems, token = _split_copy_call(
            f"chip_start_{self.name}", [half, lax.empty((3,) + g.shard_half_shape, BF16)], start, new_sems=2)
        return token

    def chip_finish(self, after):
        g = self.geom

        def wait(refs, sems, _):
            x, y, c = _mesh_pos()
            three = _remote(refs[1], refs[1], sems, (x, y, 1 - c))
            three.wait_send()
            three.wait_recv()

        (half, landed), _, _ = _split_copy_call(f"chip_wait_{self.name}", self.arrays, wait, self.sems, after)
        quarter = _chip_add(f"chip_add_{self.name}", half, landed, g, self.idx)

        def start(refs, _, new):
            x, y, c = _mesh_pos()
            own = g.half_of_shard(refs[0], c)
            _remote(own, own, new, (x, y, 1 - c)).start()

        self.arrays, self.sems, token = _split_copy_call(f"share_start_{self.name}", [quarter], start, new_sems=2)
        return token

    def finish(self, after):
        g = self.geom

        def wait(refs, sems, _):
            x, y, c = _mesh_pos()
            own, theirs = g.half_of_shard(refs[0], c), g.half_of_shard(refs[0], 1 - c)
            _remote(own, own, sems, (x, y, 1 - c)).wait_send()
            _remote(theirs, theirs, sems, (x, y, 1 - c)).wait_recv()

        (quarter,), _, _ = _split_copy_call(f"share_wait_{self.name}", self.arrays, wait, self.sems, after)
        return quarter


def _dw_half(name, x, dy, geom, c_idx, own, addend=None, after=None):
    stacked = dy.ndim == 3
    t, m = x.shape
    n = 2 * dy.shape[2] if stacked else dy.shape[1]
    hm, hn = (m // 2, n) if geom.col else (m, n // 2)
    tm, tn, tk = _mm_tiles(hm, hn, t, BF16, n_unit=(n // 2 if stacked else None))
    if tk != t:
        tm, tn = _tile(hm, 512, 128), _tile(hn // (2 if stacked else 1), 512, 128)
    gi, gj = hm // tm, hn // tn
    nf = (n // 2) // tn

    def sel(cref):
        return cref[0] if own else 1 - cref[0]

    a_map = (lambda i, j, cref: (0, sel(cref) * gi + i)) if geom.col else (lambda i, j, cref: (0, i))
    if stacked:
        b_blk, b_map = (None, t, tn), (lambda i, j, cref: (j // nf, 0, j % nf))
    elif geom.col:
        b_blk, b_map = (t, tn), (lambda i, j, cref: (0, j))
    else:
        b_blk, b_map = (t, tn), (lambda i, j, cref: (0, sel(cref) * gj + j))
    out_blk = pl.BlockSpec((tm, tn), lambda i, j, cref: (i, j))
    ins, in_specs = [x, dy], [pl.BlockSpec((t, tm), a_map), pl.BlockSpec(b_blk, b_map)]
    if addend is not None:
        ins.append(addend)
        in_specs.append(out_blk)
    if after is not None:
        ins.append(after)
        in_specs.append(pl.BlockSpec(memory_space=pl.ANY))

    def body(c_ref, *refs):
        acc = _dot(refs[0][...], refs[1][...], 0, 0)
        if addend is not None:
            acc = acc + refs[2][...].astype(F32)
        refs[len(ins)][...] = acc.astype(BF16)

    return _pallas(
        body, name=name,
        grid_spec=pltpu.PrefetchScalarGridSpec(num_scalar_prefetch=1, grid=(gi, gj), in_specs=in_specs,
                                               out_specs=out_blk),
        out_shape=jax.ShapeDtypeStruct((hm, hn), BF16),
        compiler_params=_params(("parallel", "parallel"),
                                _nbytes((t, tm), BF16) + _nbytes((t, tn), BF16) + 3 * _nbytes((tm, tn), F32)),
    )(c_idx, *ins)


def _chip_add(name, half, recv, geom, idx):
    r, c = geom.shard_half_shape
    tr, tc = _tile(r, 512, 16), _tile(c, 2048, 128)
    nr, ncol = r // tr, c // tc
    if geom.col:
        mine = lambda i, j, iref: (i, iref[0] * ncol + j)
        place = lambda i, j, iref: (iref[1] * nr + i, j)
    else:
        mine = lambda i, j, iref: (iref[0] * nr + i, j)
        place = lambda i, j, iref: (i, iref[1] * ncol + j)

    def body(i_ref, h_ref, r_ref, o_ref):
        acc = h_ref[...].astype(F32)
        for k in range(3):
            acc = acc + r_ref[k].astype(F32)
        o_ref[...] = acc

    return _pallas(
        body, name=name,
        grid_spec=pltpu.PrefetchScalarGridSpec(
            num_scalar_prefetch=1, grid=(nr, ncol),
            in_specs=[pl.BlockSpec((tr, tc), mine), pl.BlockSpec((3, tr, tc), lambda i, j, iref: (0, i, j))],
            out_specs=pl.BlockSpec((tr, tc), place)),
        out_shape=jax.ShapeDtypeStruct(geom.shard_shape, F32),
        compiler_params=_params(("parallel", "parallel"), 4 * _nbytes((tr, tc), F32)),
    )(idx, half, recv)


def _all_reduce_small(pack, after=None):
    r, d = pack.shape

    def body(p_ref, o_ref, slots, send_sems, recv_sems):
        x, y, c = _mesh_pos()
        me = 4 * x + 2 * y + c
        slots[me] = p_ref[...]
        copies = []
        for k in range(1, N_DEV):
            px, py, pc = x ^ ((k >> 2) & 1), y ^ ((k >> 1) & 1), c ^ (k & 1)
            copies.append(pltpu.make_async_remote_copy(
                src_ref=p_ref, dst_ref=slots.at[me], send_sem=send_sems.at[k - 1], recv_sem=recv_sems.at[k - 1],
                device_id=(px, py, pc), device_id_type=MESH))
        for cp in copies:
            cp.start()
        for k in range(1, N_DEV):
            peer = 4 * (x ^ ((k >> 2) & 1)) + 2 * (y ^ ((k >> 1) & 1)) + (c ^ (k & 1))
            pltpu.make_async_remote_copy(
                src_ref=p_ref, dst_ref=slots.at[peer], send_sem=send_sems.at[k - 1], recv_sem=recv_sems.at[k - 1],
                device_id=(x, y, c), device_id_type=MESH).wait_recv()
        for cp in copies:
            cp.wait_send()
        acc = slots[0]
        for k in range(1, N_DEV):
            acc = acc + slots[k]
        o_ref[...] = acc

    vm = pl.BlockSpec(memory_space=pltpu.VMEM)
    body, ins, in_specs = _ordered(body, [pack], [vm], after)
    return _pallas(
        body, name="all_reduce_small", in_specs=in_specs, out_specs=vm,
        out_shape=jax.ShapeDtypeStruct((r, d), F32),
        scratch_shapes=[pltpu.VMEM((N_DEV, r, d), F32), pltpu.SemaphoreType.DMA((N_DEV - 1,)),
                        pltpu.SemaphoreType.DMA((N_DEV - 1,))],
    )(*ins)


def _pack_rows(rows, d):
    out = []
    for a in rows:
        flat = a.reshape(-1)
        n = -(-flat.shape[0] // d) * d
        out.append(jnp.pad(flat, (0, n - flat.shape[0])).reshape(-1, d))
    packed = jnp.concatenate(out, axis=0)
    return jnp.pad(packed, ((0, 16 - packed.shape[0]), (0, 0)))


def _unpack_rows(packed, shapes, d):
    out, row = [], 0
    for shp in shapes:
        n = int(np.prod(shp))
        nrows = -(-n // d)
        out.append(packed[row:row + nrows].reshape(-1)[:n].reshape(shp))
        row += nrows
    return out


def kernel(x, pre_norm_ffn1, post_norm_ffn1, w_ffn1_gate_up, w_ffn1_down, pre_norm_mix, post_norm_mix, w_mix_in, hgrn_lower_bounds_fwd, hgrn_lower_bounds_bwd, hgrn_out_norm, attn_sink, w_mix_out, pre_norm_ffn2, post_norm_ffn2, w_ffn2_gate_up, w_ffn2_down, rel_bias_table, loss_target, m_pre_norm_ffn1, m_post_norm_ffn1, m_w_ffn1_gate_up, m_w_ffn1_down, m_pre_norm_mix, m_post_norm_mix, m_w_mix_in, m_hgrn_lower_bounds_fwd, m_hgrn_lower_bounds_bwd, m_hgrn_out_norm, m_attn_sink, m_w_mix_out, m_pre_norm_ffn2, m_post_norm_ffn2, m_w_ffn2_gate_up, m_w_ffn2_down, m_rel_bias_table, v_pre_norm_ffn1, v_post_norm_ffn1, v_w_ffn1_gate_up, v_w_ffn1_down, v_pre_norm_mix, v_post_norm_mix, v_w_mix_in, v_hgrn_lower_bounds_fwd, v_hgrn_lower_bounds_bwd, v_hgrn_out_norm, v_attn_sink, v_w_mix_out, v_pre_norm_ffn2, v_post_norm_ffn2, v_w_ffn2_gate_up, v_w_ffn2_down, v_rel_bias_table):
    t, d = x.shape[1], x.shape[2]
    hw = hgrn_out_norm.shape[1]
    aw = d - hw
    nah = aw // HEAD
    kvw = KV_HEADS * HEAD
    x0 = x[0]
    target = loss_target[0]

    big_names = ["w_ffn1_gate_up", "w_ffn1_down", "w_mix_in", "w_mix_out", "w_ffn2_gate_up", "w_ffn2_down"]
    big_w = [w_ffn1_gate_up[0], w_ffn1_down[0], w_mix_in[0], w_mix_out[0], w_ffn2_gate_up[0], w_ffn2_down[0]]
    big_m = [m_w_ffn1_gate_up[0], m_w_ffn1_down[0], m_w_mix_in[0], m_w_mix_out[0], m_w_ffn2_gate_up[0],
             m_w_ffn2_down[0]]
    big_v = [v_w_ffn1_gate_up[0], v_w_ffn1_down[0], v_w_mix_in[0], v_w_mix_out[0], v_w_ffn2_gate_up[0],
             v_w_ffn2_down[0]]
    col_sharded = [True, False, True, False, True, False]
    geoms = [_Big(w.shape, cs) for w, cs in zip(big_w, col_sharded)]

    cx, cy, cc = _mesh_pos()
    idx = jnp.stack([2 * cx + cy, cc]).astype(jnp.int32)
    c_idx = jnp.reshape(cc, (1,)).astype(jnp.int32)
    first = _cast_into_full(f"cast_{big_names[0]}", big_w[0], geoms[0], idx)
    (first,), direct_sems, tok = _gather_first_direct(first, geoms[0])
    rest = []
    for n, w, gm in zip(big_names[1:], big_w[1:], geoms[1:]):
        tok = _cast_into_full(f"cast_{n}", w, gm, idx, after=tok)
        rest.append(tok)
    (first,), relay_sems, tok = _gather_first_relay(first, geoms[0], direct_sems, after=tok)
    started_rest, sems_rest, rest_started = _gather_start("gather_start_rest", rest, geoms[1:], after=tok)
    started, gather_sems = [first] + started_rest, [relay_sems] + sems_rest

    def forward_weight(w, after):
        return _gather_forward(f"gather_forward_{big_names[w]}", started[w], geoms[w], gather_sems[w], after,
                               arrivals=1 if w == 0 else 3)

    def whole_weight(w, forwarded, after):
        return _gather_end(f"gather_end_{big_names[w]}", forwarded[0], geoms[w], forwarded[1], after)

    h1 = _norm_fwd("ffn1_pre_norm", x0, pre_norm_ffn1)
    w_gu1 = whole_weight(0, forward_weight(0, rest_started), h1)
    act1, dact_dgate1, dact_dup1 = _ffn_gate_up_act("ffn1_gate_up", h1, w_gu1)
    w_d1 = whole_weight(1, forward_weight(1, act1), act1)
    ff1 = _mm("ffn1_down", act1, w_d1, "nn", F32)
    fw = forward_weight(2, ff1)
    x1, hm = _resid_norm_fwd("ffn1_residual", x0, ff1, post_norm_ffn1, pre_norm_mix, 0.5)
    w_in = whole_weight(2, fw, hm)
    p = _mm("mix_in", hm, w_in, "nn", F32)
    fw = forward_weight(3, p)
    o_f, o_b, st_f, st_b = _hgrn_scan_fwd("hgrn_scan", p, hgrn_lower_bounds_fwd, hgrn_lower_bounds_bwd)
    y_h = _hgrn_out_fwd("hgrn_out", o_f, o_b, p, hgrn_out_norm, 4)
    kv_blk0 = (5 * hw + aw) // kvw
    bucket_ids = _t5_bucket_ids()
    bias = _bias_gather("attn_bias", rel_bias_table.T, bucket_ids).reshape(nah, WINDOW, SPAN)
    y_a, attn_probs, attn_sink_probs = _attn_fwd("attn_fwd", p, kv_blk0, kvw, bias, attn_sink, 5 * hw // aw)
    y_mix = _concat_cols("mix_concat", y_h, y_a)
    w_out = whole_weight(3, fw, y_mix)
    mixed = _mm("mix_out", y_mix, w_out, "nn", F32)
    fw = forward_weight(4, mixed)
    x2, h2 = _resid_norm_fwd("mix_residual", x1, mixed, post_norm_mix, pre_norm_ffn2, 1.0)
    w_gu2 = whole_weight(4, fw, h2)
    act2, dact_dgate2, dact_dup2 = _ffn_gate_up_act("ffn2_gate_up", h2, w_gu2)
    w_d2 = whole_weight(5, forward_weight(5, act2), act2)
    ff2 = _mm("ffn2_down", act2, w_d2, "nn", F32)
    loss_blk, dy, dff2, dg_post2 = _final_fwd_bwd("ffn2_residual_loss", x2, ff2, post_norm_ffn2, target, 0.5)

    reduce = [_GradReduce(n, gm, idx, c_idx) for n, gm in zip(big_names, geoms)]
    big_grads, big_delta, big_new_m, big_new_v = [None] * 6, [None] * 6, [None] * 6, [None] * 6

    def update(w, after):
        g, dl, nm, nv = _adamw(f"adamw_{big_names[w]}", big_w[w], reduce[w].finish(after), big_m[w], big_v[w])
        big_grads[w], big_delta[w], big_new_m[w], big_new_v[w] = g[None], dl[None], nm[None], nv[None]
        return dl

    def dw_start(w, x_act, dy_act, after=None):
        theirs = _dw_half(f"dw_theirs_{big_names[w]}", x_act, dy_act, geoms[w], c_idx, own=False, after=after)
        return reduce[w].pair_start(theirs)

    def dw_finish(w, x_act, dy_act, after):
        landed = reduce[w].pair_wait(after)
        half = _dw_half(f"dw_own_{big_names[w]}", x_act, dy_act, geoms[w], c_idx, own=True, addend=landed)
        return reduce[w].chip_start(half)

    tok = dw_start(5, act2, dff2)
    dgu2 = _ffn_dact("ffn2_dact", dff2, w_d2, dact_dgate2, dact_dup2, after=tok)
    tok = dw_finish(5, act2, dff2, after=dgu2)
    tok = dw_start(4, h2, dgu2, after=tok)
    dh2 = _ffn_dh("ffn2_dh", dgu2, w_gu2, after=tok)
    tok = dw_finish(4, h2, dgu2, after=dh2)
    dx2, dg_pre2, dmixed, dg_postm = _norms_bwd("mix_residual_bwd", dy, dh2, x2, pre_norm_ffn2,
                                                post=(mixed, post_norm_mix, 1.0), after=tok)
    tok = dw_start(3, y_mix, dmixed)
    dy_mix = _mm("mix_out_dx", dmixed, w_out, "nt", F32, after=tok)
    tok = dw_finish(3, y_mix, dmixed, after=dy_mix)
    dq_a, dk_pad, dv_pad, dbias, dsink = _attn_bwd("attn_bwd", p, kv_blk0, kvw, attn_probs, attn_sink_probs, dy_mix,
                                                   5 * hw // aw, hw // aw, after=tok)
    tok = reduce[5].chip_finish(dq_a)
    drel_t = _bias_scatter("attn_dbias", dbias.reshape(nah, WINDOW * SPAN), bucket_ids)
    do, dg_h, dgain = _hgrn_out_bwd("hgrn_out_bwd", dy_mix, o_f, o_b, p, hgrn_out_norm, 4, after=tok)
    dq_f, dv_f, dz_f, dlb_f, dq_b, dv_b, dz_b, dlb_b = _hgrn_scan_bwd(
        "hgrn_scan_bwd", p, hgrn_lower_bounds_fwd, hgrn_lower_bounds_bwd, do, st_f, st_b)
    tok = reduce[4].chip_finish(dq_f)
    tok = reduce[3].chip_finish(tok)
    dp = _mix_dproj("mix_dproj", [(dq_f, dq_b), (dv_f, dv_b), (dz_f,), (dz_b,), (dg_h,), (dq_a,)],
                    [dk_pad, dv_pad], t, after=tok)
    tok = dw_start(2, hm, dp)
    dhm = _mm("mix_in_dx", dp, w_in, "nt", F32, after=tok)
    tok = dw_finish(2, hm, dp, after=dhm)
    dx1, dg_prem, dff1, dg_post1 = _norms_bwd("ffn1_residual_bwd", dx2, dhm, x1, pre_norm_mix,
                                              post=(ff1, post_norm_ffn1, 0.5), after=tok)
    tok = dw_start(1, act1, dff1)
    dgu1 = _ffn_dact("ffn1_dact", dff1, w_d1, dact_dgate1, dact_dup1, after=tok)
    tok = dw_finish(1, act1, dff1, after=dgu1)
    tok = reduce[2].chip_finish(tok)
    tok = dw_start(0, h1, dgu1, after=tok)
    done = update(2, tok)
    tok = dw_finish(0, h1, dgu1, after=done)
    dh1 = _ffn_dh("ffn1_dh", dgu1, w_gu1, after=tok)
    grad_x, dg_pre1 = _norms_bwd("ffn1_pre_norm_bwd", dx1, dh1, x0, pre_norm_ffn1)

    small_w = [pre_norm_ffn1, post_norm_ffn1, pre_norm_mix, post_norm_mix, hgrn_lower_bounds_fwd,
               hgrn_lower_bounds_bwd, hgrn_out_norm, attn_sink, pre_norm_ffn2, post_norm_ffn2, rel_bias_table]
    small_m = [m_pre_norm_ffn1, m_post_norm_ffn1, m_pre_norm_mix, m_post_norm_mix, m_hgrn_lower_bounds_fwd,
               m_hgrn_lower_bounds_bwd, m_hgrn_out_norm, m_attn_sink, m_pre_norm_ffn2, m_post_norm_ffn2,
               m_rel_bias_table]
    small_v = [v_pre_norm_ffn1, v_post_norm_ffn1, v_pre_norm_mix, v_post_norm_mix, v_hgrn_lower_bounds_fwd,
               v_hgrn_lower_bounds_bwd, v_hgrn_out_norm, v_attn_sink, v_pre_norm_ffn2, v_post_norm_ffn2,
               v_rel_bias_table]
    small_g = [dg_pre1, dg_post1, dg_prem, dg_postm, dlb_f, dlb_b, dgain, dsink[:, 0].reshape(1, nah), dg_pre2,
               dg_post2, drel_t.T]
    shapes = [a.shape for a in small_w]
    done = update(5, grad_x)
    done = update(4, done)
    done = update(3, done)
    summed = _all_reduce_small(_pack_rows(small_g + [loss_blk[0:1, 0:1]], d), after=done)
    loss = _unpack_rows(summed, shapes + [(1, 1)], d)[-1][0, 0]
    _, sd, sm, sv = _adamw("adamw_small", _pack_rows(small_w, d), summed, _pack_rows(small_m, d),
                           _pack_rows(small_v, d))
    small_grads = _unpack_rows(summed, shapes, d)
    small_delta, small_new_m, small_new_v = (_unpack_rows(a, shapes, d) for a in (sd, sm, sv))

    tok = reduce[1].chip_finish(sd)
    tok = reduce[0].chip_finish(tok)
    done = update(1, tok)
    update(0, done)

    def ordered(small, big):
        s = dict(zip(["pre1", "post1", "prem", "postm", "lbf", "lbb", "gain", "sink", "pre2", "post2", "rel"], small))
        b = dict(zip(["gu1", "d1", "win", "wout", "gu2", "d2"], big))
        return [s["pre1"], s["post1"], b["gu1"], b["d1"], s["prem"], s["postm"], b["win"], s["lbf"], s["lbb"],
                s["gain"], s["sink"], b["wout"], s["pre2"], s["post2"], b["gu2"], b["d2"], s["rel"]]

    return (loss, grad_x[None], *ordered(small_grads, big_grads), *ordered(small_delta, big_delta),
            *ordered(small_new_m, big_new_m), *ordered(small_new_v, big_new_v))
```

```python
import math

import jax
import jax.numpy as jnp
import numpy as np
from jax import lax
from jax.experimental import pallas as pl
from jax.experimental.pallas import tpu as pltpu

F32 = jnp.float32
BF16 = jnp.bfloat16

HEAD = 128
CHUNK = 64
WINDOW = 128
SPAN = 3 * WINDOW
KV_HEADS = 2
REL_BUCKETS = 32
REL_MAX_DIST = 128
EPS = 1e-6
NEG_INF = -1e30

ADAM_LR = 0.001
ADAM_B1 = 0.9
ADAM_B2 = 0.999
ADAM_EPS = 1e-08
ADAM_WD = 0.01
ADAM_STEP = 10

N_CHIPS = 4
N_DEV = 8
V7X_VMEM_BYTES = 64 * 1024 * 1024
MESH = pl.DeviceIdType.MESH
ANY = pl.BlockSpec(memory_space=pl.ANY)


def _tile(n, pref, mult):
    t = (min(pref, n) // mult) * mult
    while t >= mult:
        if n % t == 0:
            return t
        t -= mult
    return n


def _params(semantics, block_bytes):
    limit = min(V7X_VMEM_BYTES - (4 << 20), 2 * int(block_bytes) + (8 << 20))
    return pltpu.CompilerParams(dimension_semantics=semantics, vmem_limit_bytes=limit)


def _nbytes(shape, dtype):
    return int(np.prod(shape)) * jnp.dtype(dtype).itemsize


PIN_TO_HBM_BYTES = 4 << 20


def _pallas(body, **kw):
    def pin_shape(s):
        if isinstance(s, jax.ShapeDtypeStruct) and _nbytes(s.shape, s.dtype) >= PIN_TO_HBM_BYTES:
            return pltpu.HBM(s.shape, s.dtype)
        return s

    def pin(a):
        if getattr(a, "dtype", None) in (F32, BF16) and _nbytes(a.shape, a.dtype) >= PIN_TO_HBM_BYTES:
            return pltpu.with_memory_space_constraint(a, pltpu.HBM)
        return a

    out_shape = kw["out_shape"]
    kw["out_shape"] = [pin_shape(s) for s in out_shape] if isinstance(out_shape, (list, tuple)) else pin_shape(out_shape)
    call = pl.pallas_call(body, **kw)
    return lambda *args: call(*[pin(a) for a in args])


def _dot(a, b, ca=1, cb=0):
    return lax.dot_general(a, b, (((ca,), (cb,)), ((), ())), preferred_element_type=F32)


def _split3(x):
    hi = x.astype(BF16)
    r1 = x - hi.astype(F32)
    mid = r1.astype(BF16)
    lo = (r1 - mid.astype(F32)).astype(BF16)
    return hi, mid, lo


def _dot_exact(a, b, ca=1, cb=0, split="b"):
    if split == "b":
        return sum(_dot(a, p, ca, cb) for p in _split3(b))
    return sum(_dot(p, b, ca, cb) for p in _split3(a))


def _rms(x):
    return lax.rsqrt(jnp.mean(x * x, axis=-1, keepdims=True) + EPS)


def _norm_bwd(u, x, gain):
    r = _rms(x)
    xhat = x * r
    dgain = jnp.sum(u * xhat, axis=0, keepdims=True)
    v = u * gain
    dx = r * (v - xhat * jnp.mean(v * xhat, axis=-1, keepdims=True))
    return dx, dgain


def _sigmoid(x):
    return 1.0 / (1.0 + jnp.exp(-x))


def _accumulate(ref, val, first):
    @pl.when(first)
    def _():
        ref[...] = val

    @pl.when(jnp.logical_not(first))
    def _():
        ref[...] += val


def _ordered(body, ins, in_specs, after):
    if after is None:
        return body, list(ins), list(in_specs)
    n_in = len(ins)

    def wrapped(*refs):
        body(*refs[:n_in], *refs[n_in + 1:])

    return wrapped, list(ins) + [after], list(in_specs) + [pl.BlockSpec(memory_space=pl.ANY)]


def _matmul(name, a, b, *, form, out_dtype, tm, tn, tk, a_map=None, sizes=None, after=None):
    if sizes is None:
        (m, k), n = a.shape, (b.shape[1] if form == "nn" else b.shape[0])
    else:
        m, n, k = sizes
    gi, gj, gk = m // tm, n // tn, k // tk
    a_blk = (tm, tk)
    b_blk = (tk, tn) if form == "nn" else (tn, tk)
    if a_map is None:
        a_map = lambda i, j, kk: (i, kk)
    else:
        a_blk = (None,) + a_blk
    b_map = (lambda i, j, kk: (kk, j)) if form == "nn" else (lambda i, j, kk: (j, kk))
    out_shape, out_block, out_map = (m, n), (tm, tn), (lambda i, j, kk: (i, j))
    ca, cb = {"nn": (1, 0), "nt": (1, 1)}[form]

    def body(a_ref, b_ref, o_ref, *acc):
        part = _dot(a_ref[...], b_ref[...], ca, cb)
        if gk == 1:
            o_ref[...] = part.astype(o_ref.dtype)
        else:
            kk = pl.program_id(2)
            _accumulate(acc[0], part, kk == 0)

            @pl.when(kk == gk - 1)
            def _():
                o_ref[...] = acc[0][...].astype(o_ref.dtype)

    scratch = [] if gk == 1 else [pltpu.VMEM((tm, tn), F32)]
    vmem = (_nbytes((tm, tk), a.dtype) + _nbytes((tk, tn), b.dtype) + _nbytes((tm, tn), out_dtype)
            + 2 * _nbytes((tm, tn), F32))
    body, ins, in_specs = _ordered(body, [a, b], [pl.BlockSpec(a_blk, a_map), pl.BlockSpec(b_blk, b_map)], after)
    return _pallas(
        body, name=name, grid=(gi, gj, gk), in_specs=in_specs,
        out_specs=pl.BlockSpec(out_block, out_map),
        out_shape=jax.ShapeDtypeStruct(out_shape, out_dtype),
        scratch_shapes=scratch,
        compiler_params=_params(("parallel", "parallel", "arbitrary"), vmem),
    )(*ins)


V7X_HBM_BYTES_PER_US = 3.0e6
V7X_MXU_FLOPS_PER_US = 0.9e9
V7X_VMEM_RMW_BYTES_PER_US = 10e6
GRID_STEP_US = 0.35
MATMUL_VMEM_BUDGET = 40 << 20
MATMUL_MAX_TILE_FLOPS = 1 << 33


def _divisors(n, mult, lo):
    return [t for t in range(mult, n + 1, mult) if n % t == 0 and t >= min(lo, n)]


def _mm_tiles(m, n, k, out_dtype=F32, n_unit=None):
    out_bytes = jnp.dtype(out_dtype).itemsize
    best = None
    for tm in _divisors(m, 128, 256):
        for tn in _divisors(n_unit or n, 128, 256):
            for tk in _divisors(k, 128, 512):
                gi, gj, gk = m // tm, n // tn, k // tk
                vmem = 4 * tm * tk + 4 * tk * tn + 2 * tm * tn * out_bytes + 4 * tm * tn * (2 if gk > 1 else 1)
                if vmem > MATMUL_VMEM_BUDGET or 2 * tm * tn * tk > MATMUL_MAX_TILE_FLOPS:
                    continue
                a_bytes = 2 * m * k * (gj if gk > 1 else 1)
                b_bytes = 2 * k * n * (1 if gj == 1 and gk == 1 else gi)
                hbm_us = (a_bytes + b_bytes + m * n * out_bytes) / V7X_HBM_BYTES_PER_US
                acc_us = (8 * m * n * gk / V7X_VMEM_RMW_BYTES_PER_US) if gk > 1 else 0.0
                cost = max(2 * m * n * k / V7X_MXU_FLOPS_PER_US, 1.3 * hbm_us) + GRID_STEP_US * gi * gj * gk + acc_us
                key = (round(cost, 1), vmem)
                if best is None or key < best[0]:
                    best = (key, (tm, tn, tk))
    return best[1]


def _mm(name, a, b, form, out_dtype, after=None):
    (m, k), n = a.shape, (b.shape[1] if form == "nn" else b.shape[0])
    tm, tn, tk = _mm_tiles(m, n, k, out_dtype)
    return _matmul(name, a, b, form=form, out_dtype=out_dtype, tm=tm, tn=tn, tk=tk, after=after)


def _row_tile(t):
    return _tile(t, 256, 8)


def _norm_fwd(name, x, gain):
    t, d = x.shape
    tm = _row_tile(t)

    def body(x_ref, g_ref, h_ref):
        xv = x_ref[...]
        h_ref[...] = (xv * _rms(xv) * g_ref[...]).astype(BF16)

    row = pl.BlockSpec((tm, d), lambda i: (i, 0))
    vec = pl.BlockSpec((1, d), lambda i: (0, 0))
    return _pallas(
        body, name=name, grid=(t // tm,), in_specs=[row, vec], out_specs=row,
        out_shape=jax.ShapeDtypeStruct((t, d), BF16),
        compiler_params=_params(("parallel",), 2 * _nbytes((tm, d), F32)),
    )(x, gain)


def _resid_norm_fwd(name, xres, ff, gpost, gpre, scale):
    t, d = xres.shape
    tm = _row_tile(t)

    def body(x_ref, f_ref, gp_ref, gn_ref, xn_ref, h_ref):
        f = f_ref[...]
        xn = x_ref[...] + scale * (f * _rms(f) * gp_ref[...])
        xn_ref[...] = xn
        h_ref[...] = (xn * _rms(xn) * gn_ref[...]).astype(BF16)

    row = pl.BlockSpec((tm, d), lambda i: (i, 0))
    vec = pl.BlockSpec((1, d), lambda i: (0, 0))
    return _pallas(
        body, name=name, grid=(t // tm,), in_specs=[row, row, vec, vec], out_specs=[row, row],
        out_shape=[jax.ShapeDtypeStruct((t, d), F32), jax.ShapeDtypeStruct((t, d), BF16)],
        compiler_params=_params(("parallel",), 4 * _nbytes((tm, d), F32)),
    )(xres, ff, gpost, gpre)


def _final_fwd_bwd(name, xres, ff, gpost, target, scale):
    t, d = xres.shape
    tm = _row_tile(t)

    def body(x_ref, f_ref, gp_ref, t_ref, loss_ref, dy_ref, dff_ref, dg_ref):
        i = pl.program_id(0)
        f = f_ref[...]
        gp = gp_ref[...]
        y = x_ref[...] + scale * (f * _rms(f) * gp)
        err = y - t_ref[...]
        part = 0.5 * jnp.sum(jnp.mean(err * err, axis=-1, keepdims=True), axis=0, keepdims=True)
        _accumulate(loss_ref, jnp.broadcast_to(part, loss_ref.shape), i == 0)
        dy = err / d
        dy_ref[...] = dy
        dff, dg = _norm_bwd(scale * dy, f, gp)
        dff_ref[...] = dff.astype(BF16)
        _accumulate(dg_ref, dg, i == 0)

    row = pl.BlockSpec((tm, d), lambda i: (i, 0))
    vec = pl.BlockSpec((1, d), lambda i: (0, 0))
    return _pallas(
        body, name=name, grid=(t // tm,), in_specs=[row, row, vec, row],
        out_specs=[pl.BlockSpec((8, 128), lambda i: (0, 0)), row, row, vec],
        out_shape=[jax.ShapeDtypeStruct((8, 128), F32), jax.ShapeDtypeStruct((t, d), F32),
                   jax.ShapeDtypeStruct((t, d), BF16), jax.ShapeDtypeStruct((1, d), F32)],
        compiler_params=_params(("arbitrary",), 5 * _nbytes((tm, d), F32)),
    )(xres, ff, gpost, target)


def _norms_bwd(name, dres, dh, xin, gpre, post=None, after=None):
    t, d = dres.shape
    tm = _row_tile(t)
    with_post = post is not None

    def body(*refs):
        if with_post:
            dr_ref, dh_ref, x_ref, g_ref, f_ref, gp_ref, dx_ref, dg_ref, dff_ref, dgp_ref = refs
        else:
            dr_ref, dh_ref, x_ref, g_ref, dx_ref, dg_ref = refs
        i = pl.program_id(0)
        dx, dg = _norm_bwd(dh_ref[...], x_ref[...], g_ref[...])
        dx = dr_ref[...] + dx
        dx_ref[...] = dx
        _accumulate(dg_ref, dg, i == 0)
        if with_post:
            dff, dgp = _norm_bwd(post[2] * dx, f_ref[...], gp_ref[...])
            dff_ref[...] = dff.astype(BF16)
            _accumulate(dgp_ref, dgp, i == 0)

    row = pl.BlockSpec((tm, d), lambda i: (i, 0))
    vec = pl.BlockSpec((1, d), lambda i: (0, 0))
    ins, in_specs = [dres, dh, xin, gpre], [row, row, row, vec]
    out_specs = [row, vec]
    out_shape = [jax.ShapeDtypeStruct((t, d), F32), jax.ShapeDtypeStruct((1, d), F32)]
    if with_post:
        ins += [post[0], post[1]]
        in_specs += [row, vec]
        out_specs += [row, vec]
        out_shape += [jax.ShapeDtypeStruct((t, d), BF16), jax.ShapeDtypeStruct((1, d), F32)]
    body, ins, in_specs = _ordered(body, ins, in_specs, after)
    return _pallas(
        body, name=name, grid=(t // tm,), in_specs=in_specs, out_specs=out_specs, out_shape=out_shape,
        compiler_params=_params(("arbitrary",), 6 * _nbytes((tm, d), F32)),
    )(*ins)


SWIGLU_TILE = (1024, 512)
DACT_TILE = (2048, 512)
DH_TILE = (1024, 1024)


def _ffn_gate_up_act(name, h, w_gu):
    t, d = h.shape
    f = w_gu.shape[1] // 2
    tm, tn = _tile(t, SWIGLU_TILE[0], 128), _tile(f, SWIGLU_TILE[1], 128)
    nf = f // tn

    def body(h_ref, wg_ref, wu_ref, a_ref, dg_ref, du_ref):
        hv = h_ref[...]
        g = _dot(hv, wg_ref[...])
        u = _dot(hv, wu_ref[...])
        sig = _sigmoid(g)
        silu = g * sig
        a_ref[...] = (silu * u).astype(BF16)
        dg_ref[...] = (u * sig * (1.0 + g * (1.0 - sig))).astype(BF16)
        du_ref[...] = silu.astype(BF16)

    out = jax.ShapeDtypeStruct((t, f), BF16)
    blk = pl.BlockSpec((tm, tn), lambda i, j: (i, j))
    return _pallas(
        body, name=name, grid=(t // tm, nf),
        in_specs=[pl.BlockSpec((tm, d), lambda i, j: (i, 0)), pl.BlockSpec((d, tn), lambda i, j: (0, j)),
                  pl.BlockSpec((d, tn), lambda i, j: (0, j + nf))],
        out_specs=[blk, blk, blk], out_shape=[out, out, out],
        compiler_params=_params(("parallel", "parallel"),
                                _nbytes((tm, d), BF16) + 2 * _nbytes((d, tn), BF16) + 5 * _nbytes((tm, tn), F32)),
    )(h, w_gu, w_gu)


def _ffn_dact(name, dff, w_down, dact_dgate, dact_dup, after=None):
    t, d = dff.shape
    f = w_down.shape[0]
    tm, tn = _tile(t, DACT_TILE[0], 128), _tile(f, DACT_TILE[1], 128)

    def body(d_ref, w_ref, dg_ref, du_ref, o_ref):
        da = _dot(d_ref[...], w_ref[...], 1, 1)
        o_ref[0] = (da * dg_ref[...].astype(F32)).astype(BF16)
        o_ref[1] = (da * du_ref[...].astype(F32)).astype(BF16)

    blk = pl.BlockSpec((tm, tn), lambda i, j: (i, j))
    body, ins, in_specs = _ordered(
        body, [dff, w_down, dact_dgate, dact_dup],
        [pl.BlockSpec((tm, d), lambda i, j: (i, 0)), pl.BlockSpec((tn, d), lambda i, j: (j, 0)), blk, blk], after)
    return _pallas(
        body, name=name, grid=(t // tm, f // tn), in_specs=in_specs,
        out_specs=pl.BlockSpec((2, tm, tn), lambda i, j: (0, i, j)),
        out_shape=jax.ShapeDtypeStruct((2, t, f), BF16),
        compiler_params=_params(("parallel", "parallel"),
                                _nbytes((tm, d), BF16) + _nbytes((tn, d), BF16) + 5 * _nbytes((tm, tn), F32)),
    )(*ins)


def _ffn_dh(name, dgu, w_gu, after=None):
    _, t, f = dgu.shape
    d = w_gu.shape[0]
    tm, tn, tk = _tile(t, DH_TILE[0], 128), _tile(d, DH_TILE[1], 128), f
    nkf = f // tk
    return _matmul(name, dgu, w_gu, form="nt", out_dtype=F32, tm=tm, tn=tn, tk=tk, sizes=(t, d, 2 * f),
                   a_map=lambda i, j, kk: (kk // nkf, i, kk % nkf), after=after)


def _lower_bound(lbp):
    m = jnp.max(lbp, axis=0, keepdims=True)
    e = jnp.exp(lbp - m)
    return e[0:1] / jnp.sum(e, axis=0, keepdims=True)


def _chunk_mask(reverse):
    row = lax.broadcasted_iota(jnp.int32, (CHUNK, CHUNK), 0)
    col = lax.broadcasted_iota(jnp.int32, (CHUNK, CHUNK), 1)
    return (col >= row) if reverse else (col <= row)


def _hgrn_gates(z, lb, mask_bf):
    sig = _sigmoid(z)
    f = lb + (1.0 - lb) * sig
    logf = jnp.log(f)
    k = 1.0 - f
    cum = _dot_exact(mask_bf, logf)
    last = jnp.sum(logf, axis=0, keepdims=True)
    return sig, f, k, cum, last


def _hgrn_scan_fwd(name, p, lbp_f, lbp_b):
    t = p.shape[0]
    hw = lbp_f.shape[1]
    nh, nc = hw // HEAD, t // CHUNK

    def body(qf, vf, zf, qb, vb, zb, lbf, lbb, of_ref, ob_ref, stf_ref, stb_ref, state):
        n = pl.program_id(0)

        @pl.when(n == 0)
        def _():
            state[...] = jnp.zeros_like(state)

        directions = [(qf, vf, zf, lbf, of_ref, stf_ref), (qb, vb, zb, lbb, ob_ref, stb_ref)]
        wide = []
        for d, (q_ref, v_ref, z_ref, lb_ref, o_ref, st_ref) in enumerate(directions):
            mask = _chunk_mask(d == 1)
            lb = _lower_bound(lb_ref[...])
            _, _, k, cum, last = _hgrn_gates(z_ref[...], lb, mask.astype(BF16))
            v = v_ref[...].astype(BF16)
            qd = (q_ref[...] * jnp.exp(cum)).astype(BF16)
            kd = (k * jnp.exp(-cum)).astype(BF16)
            kt = (k * jnp.exp(last - cum)).astype(BF16)
            s_all = state[d]
            st_ref[...] = s_all
            wide.append((mask, v, qd, kd, kt, jnp.exp(last), s_all, o_ref))
        pairs = [(d, slice(h * HEAD, (h + 1) * HEAD)) for d in range(2) for h in range(nh)]
        a = [jnp.where(wide[d][0], _dot(wide[d][2][:, sl], wide[d][3][:, sl], 1, 1), 0.0).astype(BF16)
             for d, sl in pairs]
        inter = [_dot(wide[d][2][:, sl], wide[d][6][:, sl].astype(BF16), 1, 1) for d, sl in pairs]
        intra = [_dot(a[i], wide[d][1][:, sl]) for i, (d, sl) in enumerate(pairs)]
        grow = [_dot(wide[d][1][:, sl], wide[d][4][:, sl], 0, 0) for d, sl in pairs]
        for i, (d, sl) in enumerate(pairs):
            wide[d][7][:, sl] = intra[i] + inter[i]
            state[d, :, sl] = wide[d][6][:, sl] * wide[d][5][:, sl] + grow[i]

    def col(group, reverse):
        return pl.BlockSpec((CHUNK, hw), lambda n: ((nc - 1 - n) if reverse else n, group))

    def st(reverse):
        return pl.BlockSpec((None, HEAD, hw), lambda n: ((nc - 1 - n) if reverse else n, 0, 0))

    lb_spec = pl.BlockSpec((2, hw), lambda n: (0, 0))
    out = jax.ShapeDtypeStruct((t, hw), F32)
    states = jax.ShapeDtypeStruct((nc, HEAD, hw), F32)
    return _pallas(
        body, name=name, grid=(nc,),
        in_specs=[col(0, False), col(1, False), col(2, False), col(0, True), col(1, True), col(3, True),
                  lb_spec, lb_spec],
        out_specs=[col(0, False), col(0, True), st(False), st(True)],
        out_shape=[out, out, states, states],
        scratch_shapes=[pltpu.VMEM((2, HEAD, hw), F32)],
        compiler_params=_params(("arbitrary",), 12 * _nbytes((HEAD, hw), F32)),
    )(p, p, p, p, p, p, lbp_f, lbp_b)


def _hgrn_scan_bwd(name, p, lbp_f, lbp_b, do, st_f, st_b):
    t = p.shape[0]
    hw = lbp_f.shape[1]
    nh, nc = hw // HEAD, t // CHUNK

    def body(qf, vf, zf, dof, sf, qb, vb, zb, dob, sb, lbf, lbb, dqf, dvf, dzf, dlbf, dqb, dvb, dzb, dlbb,
             dstate, dlb_acc, dqd_s, dkd_s, dkt_s, ddec_s):
        n = pl.program_id(0)

        @pl.when(n == 0)
        def _():
            dstate[...] = jnp.zeros_like(dstate)
            dlb_acc[...] = jnp.zeros_like(dlb_acc)

        directions = [(qf, vf, zf, dof, sf, lbf, dqf, dvf, dzf, dlbf), (qb, vb, zb, dob, sb, lbb, dqb, dvb, dzb, dlbb)]
        for d, (q_ref, v_ref, z_ref, do_ref, st_ref, lb_ref, dq_ref, dv_ref, dz_ref, dlb_ref) in enumerate(directions):
            mask = _chunk_mask(d == 1)
            mask_bf = mask.astype(BF16)
            lb = _lower_bound(lb_ref[...])
            sig, f, k, cum, last = _hgrn_gates(z_ref[...], lb, mask_bf)
            e_pos, e_neg, e_tail = jnp.exp(cum), jnp.exp(-cum), jnp.exp(last - cum)
            dec = jnp.exp(last)
            v = v_ref[...].astype(BF16)
            qd, kd, kt = q_ref[...] * e_pos, k * e_neg, k * e_tail
            qd_bf, kd_bf, kt_bf = qd.astype(BF16), kd.astype(BF16), kt.astype(BF16)
            s_all = st_ref[...]
            ds_all = dstate[d]
            dov = do_ref[...].astype(BF16)
            cols = [slice(h * HEAD, (h + 1) * HEAD) for h in range(nh)]
            s_bf = [s_all[:, sl].astype(BF16) for sl in cols]
            ds_bf = [ds_all[:, sl].astype(BF16) for sl in cols]
            a = [jnp.where(mask, _dot(qd_bf[:, sl], kd_bf[:, sl], 1, 1), 0.0).astype(BF16) for sl in cols]
            da = [jnp.where(mask, _dot(dov[:, sl], v[:, sl], 1, 1), 0.0).astype(BF16) for sl in cols]
            dv_h = [_dot(a[h], dov[:, sl], 0, 0) + _dot(kt_bf[:, sl], ds_bf[h], 1, 1) for h, sl in enumerate(cols)]
            dqd_h = [_dot(da[h], kd_bf[:, sl]) + _dot(dov[:, sl], s_bf[h]) for h, sl in enumerate(cols)]
            dkd_h = [_dot(da[h], qd_bf[:, sl], 0, 0) for h, sl in enumerate(cols)]
            dkt_h = [_dot(v[:, sl], ds_bf[h]) for h, sl in enumerate(cols)]
            dst_h = [_dot(dov[:, sl], qd_bf[:, sl], 0, 0) + ds_all[:, sl] * dec[:, sl] for sl in cols]
            for h, sl in enumerate(cols):
                dv_ref[:, sl] = dv_h[h]
                dqd_s[:, sl] = dqd_h[h]
                dkd_s[:, sl] = dkd_h[h]
                dkt_s[:, sl] = dkt_h[h]
                dstate[d, :, sl] = dst_h[h]
                ddec_s[:, sl] = jnp.sum(ds_all[:, sl] * s_all[:, sl], axis=0, keepdims=True)
            dqd, dkd, dkt = dqd_s[...], dkd_s[...], dkt_s[...]
            dlast = jnp.sum(dkt * kt, axis=0, keepdims=True) + dec * ddec_s[...]
            dq_ref[...] = dqd * e_pos
            dk = dkd * e_neg + dkt * e_tail
            dcum = dqd * qd - dkd * kd - dkt * kt
            dlogf = _dot_exact(mask_bf, dcum, 0, 0) + dlast
            df = dlogf / f - dk
            dz_ref[...] = df * (1.0 - lb) * sig * (1.0 - sig)
            dlb_acc[d] += jnp.sum(df * (1.0 - sig), axis=0, keepdims=True)

            @pl.when(n == nc - 1)
            def _():
                g = dlb_acc[d] * lb * (1.0 - lb)
                dlb_ref[0:1, :] = g
                dlb_ref[1:2, :] = -g

    def col(group, reverse):
        return pl.BlockSpec((CHUNK, hw), lambda n: (n if reverse else (nc - 1 - n), group))

    def st(reverse):
        return pl.BlockSpec((None, HEAD, hw), lambda n: (n if reverse else (nc - 1 - n), 0, 0))

    lb_spec = pl.BlockSpec((2, hw), lambda n: (0, 0))
    out = jax.ShapeDtypeStruct((t, hw), F32)
    dlb = jax.ShapeDtypeStruct((2, hw), F32)
    wide = pltpu.VMEM((CHUNK, hw), F32)
    return _pallas(
        body, name=name, grid=(nc,),
        in_specs=[col(0, False), col(1, False), col(2, False), col(0, False), st(False),
                  col(0, True), col(1, True), col(3, True), col(0, True), st(True), lb_spec, lb_spec],
        out_specs=[col(0, False), col(0, False), col(0, False), lb_spec,
                   col(0, True), col(0, True), col(0, True), lb_spec],
        out_shape=[out, out, out, dlb, out, out, out, dlb],
        scratch_shapes=[pltpu.VMEM((2, HEAD, hw), F32), pltpu.VMEM((2, 1, hw), F32), wide, wide, wide,
                        pltpu.VMEM((1, hw), F32)],
        compiler_params=_params(("arbitrary",), 16 * _nbytes((HEAD, hw), F32)),
    )(p, p, p, do, st_f, p, p, p, do, st_b, lbp_f, lbp_b)


def _hgrn_out_fwd(name, o_f, o_b, p, gain, g_group):
    t, hw = o_f.shape
    nh = hw // HEAD
    tm = _tile(t, 256, 16)

    def body(of_ref, ob_ref, g_ref, gain_ref, y_ref):
        o_all = of_ref[...] + ob_ref[...]
        g_all = g_ref[...]
        scale_all = gain_ref[...] * (g_all * _sigmoid(g_all))
        for h in range(nh):
            sl = slice(h * HEAD, (h + 1) * HEAD)
            o = o_all[:, sl]
            y_ref[:, sl] = (o * _rms(o) * scale_all[:, sl]).astype(BF16)

    blk = pl.BlockSpec((tm, hw), lambda i: (i, 0))
    return _pallas(
        body, name=name, grid=(t // tm,),
        in_specs=[blk, blk, pl.BlockSpec((tm, hw), lambda i: (i, g_group)), pl.BlockSpec((1, hw), lambda i: (0, 0))],
        out_specs=blk, out_shape=jax.ShapeDtypeStruct((t, hw), BF16),
        compiler_params=_params(("parallel",), 5 * _nbytes((tm, hw), F32)),
    )(o_f, o_b, p, gain)


def _hgrn_out_bwd(name, dy, o_f, o_b, p, gain, g_group, after=None):
    t, hw = o_f.shape
    nh = hw // HEAD
    tm = _tile(t, 256, 8)

    def body(dy_ref, of_ref, ob_ref, g_ref, gain_ref, do_ref, dg_ref, dgain_ref):
        i = pl.program_id(0)
        o_all = of_ref[...] + ob_ref[...]
        g_all = g_ref[...]
        sig_all = _sigmoid(g_all)
        dy_all = dy_ref[...]
        up_all = dy_all * (g_all * sig_all)
        dsilu_all = dy_all * sig_all * (1.0 + g_all * (1.0 - sig_all))
        gain_all = gain_ref[...]
        for h in range(nh):
            sl = slice(h * HEAD, (h + 1) * HEAD)
            o, gain_v = o_all[:, sl], gain_all[:, sl]
            do, dgain = _norm_bwd(up_all[:, sl], o, gain_v)
            do_ref[:, sl] = do
            dg_ref[:, sl] = dsilu_all[:, sl] * (o * _rms(o) * gain_v)
            _accumulate(dgain_ref.at[:, sl], dgain, i == 0)

    blk = pl.BlockSpec((tm, hw), lambda i: (i, 0))
    vec = pl.BlockSpec((1, hw), lambda i: (0, 0))
    out = jax.ShapeDtypeStruct((t, hw), F32)
    body, ins, in_specs = _ordered(
        body, [dy, o_f, o_b, p, gain], [blk, blk, blk, pl.BlockSpec((tm, hw), lambda i: (i, g_group)), vec], after)
    return _pallas(
        body, name=name, grid=(t // tm,), in_specs=in_specs,
        out_specs=[blk, blk, vec], out_shape=[out, out, jax.ShapeDtypeStruct((1, hw), F32)],
        compiler_params=_params(("arbitrary",), 7 * _nbytes((tm, hw), F32)),
    )(*ins)


def _t5_bucket_ids():
    c = np.arange(WINDOW)[:, None]
    s = np.arange(SPAN)[None, :]
    rel = s - WINDOW - c
    nb = REL_BUCKETS // 2
    max_exact = nb // 2
    bucket = (rel > 0).astype(np.int32) * nb
    n = np.abs(rel)
    large = max_exact + (np.log(np.maximum(n, 1) / max_exact) / np.log(REL_MAX_DIST / max_exact)
                         * (nb - max_exact)).astype(np.int32)
    large = np.minimum(large, nb - 1)
    ids = bucket + np.where(n < max_exact, n, large).astype(np.int32)
    return jnp.asarray(ids.reshape(1, WINDOW * SPAN), jnp.int32)


def _bias_onehot(ids_ref):
    n = ids_ref.shape[1]
    return (lax.broadcasted_iota(jnp.int32, (REL_BUCKETS, n), 0) == ids_ref[...]).astype(BF16)


def _bias_gather(name, table_t, ids):
    nh = table_t.shape[0]

    def body(t_ref, ids_ref, o_ref):
        o_ref[...] = _dot_exact(t_ref[...], _bias_onehot(ids_ref), split="a")

    return _pallas(
        body, name=name, out_shape=jax.ShapeDtypeStruct((nh, ids.shape[1]), F32),
        compiler_params=pltpu.CompilerParams(vmem_limit_bytes=32 << 20),
    )(table_t, ids)


def _bias_scatter(name, dbias, ids):
    nh = dbias.shape[0]

    def body(d_ref, ids_ref, o_ref):
        o_ref[...] = _dot_exact(d_ref[...], _bias_onehot(ids_ref), 1, 1, split="a")

    return _pallas(
        body, name=name, out_shape=jax.ShapeDtypeStruct((nh, REL_BUCKETS), F32),
        compiler_params=pltpu.CompilerParams(vmem_limit_bytes=32 << 20),
    )(dbias, ids)


def _attn_valid(i, t):
    c = lax.broadcasted_iota(jnp.int32, (WINDOW, SPAN), 0)
    s = lax.broadcasted_iota(jnp.int32, (WINDOW, SPAN), 1)
    rel = s - WINDOW - c
    pos = i * WINDOW - WINDOW + s
    return (jnp.abs(rel) <= WINDOW) & (pos >= 0) & (pos < t)


def _attn_probs(qs, khs, b_ref, s_ref, valid):
    heads = range(len(qs))
    sinks = [s_ref[0:1, h:h + 1] for h in heads]
    s = [_dot(qs[h], khs[h], 1, 1) / math.sqrt(HEAD) for h in heads]
    s = [jnp.where(valid, s[h] + b_ref[h], NEG_INF) for h in heads]
    m = [jnp.maximum(jnp.max(s[h], axis=-1, keepdims=True), sinks[h]) for h in heads]
    e = [jnp.exp(s[h] - m[h]) for h in heads]
    es = [jnp.exp(sinks[h] - m[h]) for h in heads]
    inv = [1.0 / (jnp.sum(e[h], axis=-1, keepdims=True) + es[h]) for h in heads]
    return [e[h] * inv[h] for h in heads], [es[h] * inv[h] for h in heads]


def _kv_window_specs(kv_blk, kvw, nb):
    return [pl.BlockSpec((WINDOW, kvw), lambda i, s=s: (jnp.clip(i + s, 0, nb - 1), kv_blk)) for s in (-1, 0, 1)]


def _kv_window(refs):
    return jnp.concatenate([r[...] for r in refs], axis=0).astype(BF16)


def _attn_fwd(name, p, kv_blk, kvw, bias, sink, q_group_blk):
    t = p.shape[0]
    nh = bias.shape[0]
    aw = nh * HEAD
    grp = nh // KV_HEADS
    nb = t // WINDOW

    def body(q_ref, kp, kc, kn, vp, vc, vn, b_ref, s_ref, y_ref, pr_ref, ps_ref):
        i = pl.program_id(0)
        valid = _attn_valid(i, t)
        ks = _kv_window((kp, kc, kn))
        vs = _kv_window((vp, vc, vn))
        heads = range(nh)
        col = lambda h: slice(h * HEAD, (h + 1) * HEAD)
        qs = [q_ref[:, col(h)].astype(BF16) for h in heads]
        pr, ps = _attn_probs(qs, [ks[:, col(h // grp)] for h in heads], b_ref, s_ref, valid)
        pr = [pr[h].astype(BF16) for h in heads]
        out = [_dot(pr[h], vs[:, col(h // grp)]) for h in heads]
        lane = lax.broadcasted_iota(jnp.int32, (WINDOW, 128), 1)
        sinks = jnp.zeros((WINDOW, 128), F32)
        for h in heads:
            y_ref[:, col(h)] = out[h].astype(BF16)
            pr_ref[h] = pr[h]
            sinks = jnp.where(lane == h, ps[h], sinks)
        ps_ref[...] = sinks

    full = lambda a: pl.BlockSpec(a.shape, lambda i: (0,) * a.ndim)
    return _pallas(
        body, name=name, grid=(nb,),
        in_specs=[pl.BlockSpec((WINDOW, aw), lambda i: (i, q_group_blk)), *_kv_window_specs(kv_blk, kvw, nb),
                  *_kv_window_specs(kv_blk + 1, kvw, nb), full(bias), full(sink)],
        out_specs=[pl.BlockSpec((WINDOW, aw), lambda i: (i, 0)), pl.BlockSpec((nh, WINDOW, SPAN), lambda i: (0, i, 0)),
                   pl.BlockSpec((WINDOW, 128), lambda i: (i, 0))],
        out_shape=[jax.ShapeDtypeStruct((t, aw), BF16), jax.ShapeDtypeStruct((nh, t, SPAN), BF16),
                   jax.ShapeDtypeStruct((t, 128), F32)],
        compiler_params=_params(("parallel",), 3 * _nbytes(bias.shape, F32)),
    )(p, p, p, p, p, p, p, bias, sink)


def _attn_bwd(name, p, kv_blk, kvw, probs, sink_probs, dy, q_group_blk, dy_blk, after=None):
    t = p.shape[0]
    nh = probs.shape[0]
    aw = nh * HEAD
    grp = nh // KV_HEADS
    nb = t // WINDOW

    def body(q_ref, kp, kc, kn, vp, vc, vn, pr_ref, ps_ref, dy_ref, dq_ref, dk_ref, dv_ref, db_ref, ds_ref):
        i = pl.program_id(0)

        @pl.when(i == 0)
        def _():
            dk_ref[...] = jnp.zeros_like(dk_ref)
            dv_ref[...] = jnp.zeros_like(dv_ref)
            db_ref[...] = jnp.zeros_like(db_ref)
            ds_ref[...] = jnp.zeros_like(ds_ref)

        start = pl.multiple_of(i * WINDOW, WINDOW)
        ks = _kv_window((kp, kc, kn))
        vs = _kv_window((vp, vc, vn))
        inv_sqrt = 1.0 / math.sqrt(HEAD)
        heads = range(nh)
        col = lambda h: slice(h * HEAD, (h + 1) * HEAD)
        qs = [q_ref[:, col(h)].astype(BF16) for h in heads]
        khs = [ks[:, col(h // grp)] for h in heads]
        pr_bf = [pr_ref[h] for h in heads]
        pr = [pr_bf[h].astype(F32) for h in heads]
        dos = [dy_ref[:, col(h)].astype(BF16) for h in heads]
        dp = [_dot(dos[h], vs[:, col(h // grp)], 1, 1) for h in heads]
        delta = [jnp.sum(pr[h] * dp[h], axis=-1, keepdims=True) for h in heads]
        dsc = [pr[h] * (dp[h] - delta[h]) for h in heads]
        dsr = [(dsc[h] * inv_sqrt).astype(BF16) for h in heads]
        dq = [_dot(dsr[h], khs[h]) for h in heads]
        dk = [_dot(dsr[h], qs[h], 0, 0) for h in heads]
        dv = [_dot(pr_bf[h], dos[h], 0, 0) for h in heads]
        for h in heads:
            db_ref[h] += dsc[h]
            dsink = jnp.sum(-ps_ref[:, h:h + 1] * delta[h], axis=0, keepdims=True)
            ds_ref[h:h + 1, :] += jnp.broadcast_to(dsink, (1, 128))
            dq_ref[:, col(h)] = dq[h]
        for kv in range(KV_HEADS):
            group = range(kv * grp, (kv + 1) * grp)
            dk_ref[pl.ds(start, SPAN), col(kv)] += sum(dk[h] for h in group)
            dv_ref[pl.ds(start, SPAN), col(kv)] += sum(dv[h] for h in group)

    whole = lambda shape: pl.BlockSpec(shape, lambda i: (0,) * len(shape))
    pad_shape = (t + 2 * WINDOW, kvw)
    bias_shape = (nh, WINDOW, SPAN)
    body, ins, in_specs = _ordered(
        body, [p, p, p, p, p, p, p, probs, sink_probs, dy],
        [pl.BlockSpec((WINDOW, aw), lambda i: (i, q_group_blk)), *_kv_window_specs(kv_blk, kvw, nb),
         *_kv_window_specs(kv_blk + 1, kvw, nb),
         pl.BlockSpec((nh, WINDOW, SPAN), lambda i: (0, i, 0)), pl.BlockSpec((WINDOW, 128), lambda i: (i, 0)),
         pl.BlockSpec((WINDOW, aw), lambda i: (i, dy_blk))], after)
    return _pallas(
        body, name=name, grid=(nb,), in_specs=in_specs,
        out_specs=[pl.BlockSpec((WINDOW, aw), lambda i: (i, 0)), whole(pad_shape), whole(pad_shape),
                   whole(bias_shape), whole((nh, 128))],
        out_shape=[jax.ShapeDtypeStruct((t, aw), F32), jax.ShapeDtypeStruct(pad_shape, F32),
                   jax.ShapeDtypeStruct(pad_shape, F32), jax.ShapeDtypeStruct(bias_shape, F32),
                   jax.ShapeDtypeStruct((nh, 128), F32)],
        compiler_params=_params(("arbitrary",), 3 * _nbytes(pad_shape, F32) + 3 * _nbytes(bias_shape, F32)),
    )(*ins)


def _mix_dproj(name, pieces, kv_pads, t, after=None):
    hw = pieces[0][0].shape[1]
    kvw = kv_pads[0].shape[1]
    widths = [hw] * len(pieces) + [kvw] * len(kv_pads)
    total = sum(widths)
    tm = WINDOW
    flat = [a for pc in pieces for a in pc]

    def body(*refs):
        o_ref = refs[-1]
        pos, off = 0, 0
        for pc in pieces:
            val = refs[pos][...]
            for extra in range(1, len(pc)):
                val = val + refs[pos + extra][...]
            o_ref[:, off:off + hw] = val.astype(BF16)
            pos += len(pc)
            off += hw
        for _ in kv_pads:
            o_ref[:, off:off + kvw] = refs[pos][...].astype(BF16)
            pos += 1
            off += kvw

    in_specs = [pl.BlockSpec((tm, hw), lambda i: (i, 0)) for _ in flat]
    in_specs += [pl.BlockSpec((tm, kvw), lambda i: (i + 1, 0)) for _ in kv_pads]
    body, ins, in_specs = _ordered(body, [*flat, *kv_pads], in_specs, after)
    return _pallas(
        body, name=name, grid=(t // tm,), in_specs=in_specs,
        out_specs=pl.BlockSpec((tm, total), lambda i: (i, 0)),
        out_shape=jax.ShapeDtypeStruct((t, total), BF16),
        compiler_params=_params(("parallel",), 3 * _nbytes((tm, total), F32)),
    )(*ins)


def _concat_cols(name, a, b):
    t, wa = a.shape
    wb = b.shape[1]
    tm = _tile(t, 512, 16)

    def body(a_ref, b_ref, o_ref):
        o_ref[:, :wa] = a_ref[...]
        o_ref[:, wa:] = b_ref[...]

    return _pallas(
        body, name=name, grid=(t // tm,),
        in_specs=[pl.BlockSpec((tm, wa), lambda i: (i, 0)), pl.BlockSpec((tm, wb), lambda i: (i, 0))],
        out_specs=pl.BlockSpec((tm, wa + wb), lambda i: (i, 0)),
        out_shape=jax.ShapeDtypeStruct((t, wa + wb), a.dtype),
        compiler_params=_params(("parallel",), 2 * _nbytes((tm, wa + wb), a.dtype)),
    )(a, b)


def _cast_into_full(name, w, geom, idx, after=None):
    r, c = w.shape
    tr = _tile(r, 256, 16)
    nr = r // tr
    if geom.col:
        place = lambda i, iref: (i, iref[0])
    else:
        place = lambda i, iref: (iref[0] * nr + i, 0)

    def body(i_ref, w_ref, *rest):
        rest[-1][...] = w_ref[...].astype(BF16)

    in_specs = [pl.BlockSpec((tr, c), lambda i, iref: (i, 0))]
    ins = [w]
    if after is not None:
        in_specs.append(pl.BlockSpec(memory_space=pl.ANY))
        ins.append(after)
    return _pallas(
        body, name=name,
        grid_spec=pltpu.PrefetchScalarGridSpec(
            num_scalar_prefetch=1, grid=(nr,), in_specs=in_specs, out_specs=pl.BlockSpec((tr, c), place)),
        out_shape=pltpu.HBM(geom.full_shape, BF16),
        compiler_params=_params(("parallel",), 2 * _nbytes((tr, c), F32)),
    )(idx, *ins)


def _adamw(name, w, g, m, v, after=None):
    r, c = w.shape
    tr = _tile(r, 256, 8)
    bc1 = 1.0 - ADAM_B1 ** ADAM_STEP
    bc2 = 1.0 - ADAM_B2 ** ADAM_STEP

    def body(w_ref, g_ref, m_ref, v_ref, go_ref, d_ref, nm_ref, nv_ref):
        gv = g_ref[...]
        go_ref[...] = gv
        nm = ADAM_B1 * m_ref[...] + (1.0 - ADAM_B1) * gv
        nv = ADAM_B2 * v_ref[...] + (1.0 - ADAM_B2) * (gv * gv)
        nm_ref[...] = nm
        nv_ref[...] = nv
        d_ref[...] = -ADAM_LR * ((nm / bc1) / (jnp.sqrt(nv / bc2) + ADAM_EPS) + ADAM_WD * w_ref[...])

    blk = pl.BlockSpec((tr, c), lambda i: (i, 0))
    out = jax.ShapeDtypeStruct((r, c), F32)
    body, ins, in_specs = _ordered(body, [w, g, m, v], [blk] * 4, after)
    return _pallas(
        body, name=name, grid=(r // tr,), in_specs=in_specs, out_specs=[blk] * 4, out_shape=[out] * 4,
        compiler_params=_params(("parallel",), 8 * _nbytes((tr, c), F32)),
    )(*ins)


def _mesh_pos():
    return lax.axis_index("x"), lax.axis_index("y"), lax.axis_index("c")


def _other_chips(x, y):
    return [(1 - x, y), (x, 1 - y), (1 - x, 1 - y)]


class _Big:
    def __init__(self, shard_shape, col_sharded):
        self.col = col_sharded
        r, c = shard_shape
        self.shard_shape = (r, c)
        self.full_shape = (r, N_CHIPS * c) if col_sharded else (N_CHIPS * r, c)
        self.half_shape = (r // 2, N_CHIPS * c) if col_sharded else (N_CHIPS * r, c // 2)
        self.shard_half_shape = (r // 2, c) if col_sharded else (r, c // 2)

    def region(self, ref, s, half=None):
        r, c = self.shard_shape
        if self.col:
            rows = slice(None) if half is None else pl.ds(half * (r // 2), r // 2)
            return ref.at[rows, pl.ds(s * c, c)]
        cols = slice(None) if half is None else pl.ds(half * (c // 2), c // 2)
        return ref.at[pl.ds(s * r, r), cols]

    def n_halves(self, ref, half, n):
        r, c = self.shard_shape
        if self.col:
            return ref.at[pl.ds(half * (r // 2), r // 2), pl.ds(0, n * c)]
        return ref.at[pl.ds(0, n * r), pl.ds(half * (c // 2), c // 2)]

    def sub_half(self, ref, s, half, j):
        r, c = self.shard_shape
        if self.col:
            return ref.at[pl.ds(half * (r // 2) + j * (r // 4), r // 4), pl.ds(s * c, c)]
        return ref.at[pl.ds(s * r + j * (r // 2), r // 2), pl.ds(half * (c // 2), c // 2)]

    def half_of_shard(self, ref, half):
        r, c = self.shard_shape
        if self.col:
            return ref.at[pl.ds(half * (r // 2), r // 2), :]
        return ref.at[:, pl.ds(half * (c // 2), c // 2)]

    def shard_of_half(self, ref, s):
        r, c = self.shard_shape
        if self.col:
            return ref.at[:, pl.ds(s * c, c)]
        return ref.at[pl.ds(s * r, r), :]


HBM =pl.BlockSpec(memory_space=pltpu.HBM)
SEM = pl.BlockSpec(memory_space=pltpu.SEMAPHORE)
SPLIT_COPY = pltpu.CompilerParams(has_side_effects=pltpu.SideEffectType.DATAFLOW_SIDE_EFFECTING)


def _in_hbm(a):
    return pltpu.with_memory_space_constraint(a, pltpu.HBM)


def _gather_start(name, fulls, geoms, after):
    nw = len(fulls)

    def body(*refs):
        dst = refs[nw + 1:2 * nw + 1]
        sems = refs[2 * nw + 1:-1]
        x, y, c = _mesh_pos()
        mine = 2 * x + y
        for w in range(nw):
            own_half = geoms[w].region(dst[w], mine, c)
            for chip in _other_chips(x, y):
                pltpu.make_async_remote_copy(src_ref=own_half, dst_ref=own_half, send_sem=sems[2 * w],
                                             recv_sem=sems[2 * w + 1], device_id=(*chip, c),
                                             device_id_type=MESH).start()
        refs[-1][...] = jnp.zeros_like(refs[-1])

    out = _pallas(
        body, name=name, in_specs=[HBM] * nw + [pl.BlockSpec(memory_space=pl.ANY)],
        out_specs=[HBM] * nw + [SEM] * (2 * nw) + [pl.BlockSpec(memory_space=pltpu.VMEM)],
        out_shape=[pltpu.HBM(g.full_shape, BF16) for g in geoms] + [pltpu.SemaphoreType.DMA(())] * (2 * nw)
        + [jax.ShapeDtypeStruct((8, 128), F32)],
        input_output_aliases={w: w for w in range(nw)}, compiler_params=SPLIT_COPY,
    )(*[_in_hbm(a) for a in fulls], after)
    return list(out[:nw]), [(out[nw + 2 * w], out[nw + 2 * w + 1]) for w in range(nw)], out[-1]


def _gather_first_direct(full, geom):
    def start(refs, _, new):
        x, y, c = _mesh_pos()
        own = geom.region(refs[0], 2 * x + y, c)
        for chip in ((1 - x, y), (x, 1 - y)):
            _remote(own, own, new, (*chip, c)).start()

    return _split_copy_call("gather_first_direct", [full], start, new_sems=2)


def _gather_first_relay(full, geom, sems, after):
    def relay(refs, got, new):
        x, y, c = _mesh_pos()
        w = refs[0]
        two = geom.n_halves(w, c, 2)
        _remote(two, two, got, (x, y, 1 - c)).wait_recv()
        from_x = geom.sub_half(w, 2 * (1 - x) + y, c, 0)
        from_y = geom.sub_half(w, 2 * x + (1 - y), c, 1)
        _remote(from_x, from_x, new, (x, 1 - y, c)).start()
        _remote(from_y, from_y, new, (1 - x, y, c)).start()
        _remote(two, two, got, (x, y, 1 - c)).wait_send()

    return _split_copy_call("gather_first_relay", [full], relay, sems=sems, after=after, new_sems=2)


def _gather_forward(name, full, geom, sems, after, arrivals):
    def body(w_in, send_sem, recv_sem, after_ref, w_ref, fwd_send, fwd_recv):
        x, y, c = _mesh_pos()
        sibling = (x, y, 1 - c)
        landed_all = geom.n_halves(w_ref, c, arrivals)
        _remote(landed_all, landed_all, (send_sem, recv_sem), sibling).wait_recv()
        for chip in _other_chips(x, y):
            landed = geom.region(w_ref, 2 * chip[0] + chip[1], c)
            pltpu.make_async_remote_copy(src_ref=landed, dst_ref=landed, send_sem=fwd_send, recv_sem=fwd_recv,
                                         device_id=sibling, device_id_type=MESH).start()
        _remote(landed_all, landed_all, (send_sem, recv_sem), sibling).wait_send()

    sem = pltpu.SemaphoreType.DMA(())
    out = _pallas(
        body, name=name, in_specs=[HBM, SEM, SEM, pl.BlockSpec(memory_space=pl.ANY)], out_specs=[HBM, SEM, SEM],
        out_shape=[pltpu.HBM(geom.full_shape, BF16), sem, sem],
        input_output_aliases={0: 0}, compiler_params=SPLIT_COPY,
    )(full, sems[0], sems[1], after)
    return out[0], (out[1], out[2])


def _gather_end(name, full, geom, sems, after):
    def body(w_in, fwd_send, fwd_recv, after_ref, w_ref):
        x, y, c = _mesh_pos()
        sibling = (x, y, 1 - c)
        theirs, ours = geom.n_halves(w_ref, 1 - c, 3), geom.n_halves(w_ref, c, 3)
        _remote(theirs, theirs, (fwd_send, fwd_recv), sibling).wait_recv()
        _remote(ours, ours, (fwd_send, fwd_recv), sibling).wait_send()

    return _pallas(
        body, name=name, in_specs=[HBM, SEM, SEM, pl.BlockSpec(memory_space=pl.ANY)], out_specs=HBM,
        out_shape=pltpu.HBM(geom.full_shape, BF16),
        input_output_aliases={0: 0}, compiler_params=SPLIT_COPY,
    )(full, sems[0], sems[1], after)


def _split_copy_call(name, arrays, fn, sems=(), after=None, new_sems=0):
    n, ns = len(arrays), len(sems)
    n_in = n + ns + (after is not None)

    def body(*refs):
        fn(refs[n_in:n_in + n], refs[n:n + ns], refs[n_in + n:-1])
        refs[-1][...] = jnp.zeros_like(refs[-1])

    ins = list(arrays) if ns else [_in_hbm(a) for a in arrays]
    ins += list(sems) + ([after] if after is not None else [])
    in_specs = [HBM] * n + [SEM] * ns + ([pl.BlockSpec(memory_space=pl.ANY)] if after is not None else [])
    out = _pallas(
        body, name=name, in_specs=in_specs,
        out_specs=[HBM] * n + [SEM] * new_sems + [pl.BlockSpec(memory_space=pltpu.VMEM)],
        out_shape=[pltpu.HBM(a.shape, a.dtype) for a in arrays] + [pltpu.SemaphoreType.DMA(())] * new_sems
        + [jax.ShapeDtypeStruct((8, 128), F32)],
        input_output_aliases={i: i for i in range(n)}, compiler_params=SPLIT_COPY,
    )(*ins)
    return list(out[:n]), tuple(out[n:-1]), out[-1]


def _remote(src, dst, sems, to):
    return pltpu.make_async_remote_copy(src_ref=src, dst_ref=dst, send_sem=sems[0], recv_sem=sems[1],
                                        device_id=to, device_id_type=MESH)


class _GradReduce:
    def __init__(self, name, geom, idx, c_idx):
        self.name, self.geom, self.idx, self.c_idx = name, geom, idx, c_idx

    def pair_start(self, theirs):
        g = self.geom

        def start(refs, _, new):
            x, y, c = _mesh_pos()
            _remote(refs[0], refs[1], new, (x, y, 1 - c)).start()

        self.arrays, self.sems, token = _split_copy_call(
            f"pair_start_{self.name}", [theirs, lax.empty(g.half_shape, BF16)], start, new_sems=2)
        return token

    def pair_wait(self, after):
        def wait(refs, sems, _):
            x, y, c = _mesh_pos()
            copy = _remote(refs[0], refs[1], sems, (x, y, 1 - c))
            copy.wait_send()
            copy.wait_recv()

        (_, landed), _, _ = _split_copy_call(f"pair_wait_{self.name}", self.arrays, wait, self.sems, after)
        return landed

    def chip_start(self, half):
        g = self.geom

        def start(refs, _, new):
            x, y, c = _mesh_pos()
            for k, chip in enumerate(_other_chips(x, y)):
                _remote(g.shard_of_half(refs[0], 2 * chip[0] + chip[1]), refs[1].at[k], new, (*chip, c)).start()

        self.arrays, self.sems, token = _split_copy_call(
            f"chip_start_{self.name}", [half, lax.empty((3,) + g.shard_half_shape, BF16)], start, new_sems=2)
        return token

    def chip_finish(self, after):
        g = self.geom

        def wait(refs, sems, _):
            x, y, c = _mesh_pos()
            three = _remote(refs[1], refs[1], sems, (x, y, 1 - c))
            three.wait_send()
            three.wait_recv()

        (half, landed), _, _ = _split_copy_call(f"chip_wait_{self.name}", self.arrays, wait, self.sems, after)
        quarter = _chip_add(f"chip_add_{self.name}", half, landed, g, self.idx)

        def start(refs, _, new):
            x, y, c = _mesh_pos()
            own = g.half_of_shard(refs[0], c)
            _remote(own, own, new, (x, y, 1 - c)).start()

        self.arrays, self.sems, token = _split_copy_call(f"share_start_{self.name}", [quarter], start, new_sems=2)
        return token

    def finish(self, after):
        g = self.geom

        def wait(refs, sems, _):
            x, y, c = _mesh_pos()
            own, theirs = g.half_of_shard(refs[0], c), g.half_of_shard(refs[0], 1 - c)
            _remote(own, own, sems, (x, y, 1 - c)).wait_send()
            _remote(theirs, theirs, sems, (x, y, 1 - c)).wait_recv()

        (quarter,), _, _ = _split_copy_call(f"share_wait_{self.name}", self.arrays, wait, self.sems, after)
        return quarter


def _finish_together(name, reducers, after):
    def wait(refs, sems, _):
        x, y, c = _mesh_pos()
        for k, r in enumerate(reducers):
            own, theirs = r.geom.half_of_shard(refs[k], c), r.geom.half_of_shard(refs[k], 1 - c)
            _remote(own, own, sems[2 * k:2 * k + 2], (x, y, 1 - c)).wait_send()
            _remote(theirs, theirs, sems[2 * k:2 * k + 2], (x, y, 1 - c)).wait_recv()

    arrays = [r.arrays[0] for r in reducers]
    sems = [s for r in reducers for s in r.sems]
    quarters, _, _ = _split_copy_call(name, arrays, wait, sems, after)
    return quarters


def _dw_half(name, x, dy, geom, c_idx, own, addend=None, after=None):
    stacked = dy.ndim == 3
    t, m = x.shape
    n = 2 * dy.shape[2] if stacked else dy.shape[1]
    hm, hn = (m // 2, n) if geom.col else (m, n // 2)
    tm, tn, tk = _mm_tiles(hm, hn, t, BF16, n_unit=(n // 2 if stacked else None))
    if tk != t:
        tm, tn = _tile(hm, 512, 128), _tile(hn // (2 if stacked else 1), 512, 128)
    gi, gj = hm // tm, hn // tn
    nf = (n // 2) // tn

    def sel(cref):
        return cref[0] if own else 1 - cref[0]

    a_map = (lambda i, j, cref: (0, sel(cref) * gi + i)) if geom.col else (lambda i, j, cref: (0, i))
    if stacked:
        b_blk, b_map = (None, t, tn), (lambda i, j, cref: (j // nf, 0, j % nf))
    elif geom.col:
        b_blk, b_map = (t, tn), (lambda i, j, cref: (0, j))
    else:
        b_blk, b_map = (t, tn), (lambda i, j, cref: (0, sel(cref) * gj + j))
    out_blk = pl.BlockSpec((tm, tn), lambda i, j, cref: (i, j))
    ins, in_specs = [x, dy], [pl.BlockSpec((t, tm), a_map), pl.BlockSpec(b_blk, b_map)]
    if addend is not None:
        ins.append(addend)
        in_specs.append(out_blk)
    if after is not None:
        ins.append(after)
        in_specs.append(pl.BlockSpec(memory_space=pl.ANY))

    def body(c_ref, *refs):
        acc = _dot(refs[0][...], refs[1][...], 0, 0)
        if addend is not None:
            acc = acc + refs[2][...].astype(F32)
        refs[len(ins)][...] = acc.astype(BF16)

    return _pallas(
        body, name=name,
        grid_spec=pltpu.PrefetchScalarGridSpec(num_scalar_prefetch=1, grid=(gi, gj), in_specs=in_specs,
                                               out_specs=out_blk),
        out_shape=jax.ShapeDtypeStruct((hm, hn), BF16),
        compiler_params=_params(("parallel", "parallel"),
                                _nbytes((t, tm), BF16) + _nbytes((t, tn), BF16) + 3 * _nbytes((tm, tn), F32)),
    )(c_idx, *ins)


def _chip_add(name, half, recv, geom, idx):
    r, c = geom.shard_half_shape
    tr, tc = _tile(r, 512, 16), _tile(c, 2048, 128)
    nr, ncol = r // tr, c // tc
    if geom.col:
        mine = lambda i, j, iref: (i, iref[0] * ncol + j)
        place = lambda i, j, iref: (iref[1] * nr + i, j)
    else:
        mine = lambda i, j, iref: (iref[0] * nr + i, j)
        place = lambda i, j, iref: (i, iref[1] * ncol + j)

    def body(i_ref, h_ref, r_ref, o_ref):
        acc = h_ref[...].astype(F32)
        for k in range(3):
            acc = acc + r_ref[k].astype(F32)
        o_ref[...] = acc

    return _pallas(
        body, name=name,
        grid_spec=pltpu.PrefetchScalarGridSpec(
            num_scalar_prefetch=1, grid=(nr, ncol),
            in_specs=[pl.BlockSpec((tr, tc), mine), pl.BlockSpec((3, tr, tc), lambda i, j, iref: (0, i, j))],
            out_specs=pl.BlockSpec((tr, tc), place)),
        out_shape=jax.ShapeDtypeStruct(geom.shard_shape, F32),
        compiler_params=_params(("parallel", "parallel"), 4 * _nbytes((tr, tc), F32)),
    )(idx, half, recv)


def _all_reduce_small(pack, after=None):
    r, d = pack.shape

    def body(p_ref, o_ref, slots, send_sems, recv_sems):
        x, y, c = _mesh_pos()
        me = 4 * x + 2 * y + c
        slots[me] = p_ref[...]
        copies = []
        for k in range(1, N_DEV):
            px, py, pc = x ^ ((k >> 2) & 1), y ^ ((k >> 1) & 1), c ^ (k & 1)
            copies.append(pltpu.make_async_remote_copy(
                src_ref=p_ref, dst_ref=slots.at[me], send_sem=send_sems.at[k - 1], recv_sem=recv_sems.at[k - 1],
                device_id=(px, py, pc), device_id_type=MESH))
        for cp in copies:
            cp.start()
        for k in range(1, N_DEV):
            peer = 4 * (x ^ ((k >> 2) & 1)) + 2 * (y ^ ((k >> 1) & 1)) + (c ^ (k & 1))
            pltpu.make_async_remote_copy(
                src_ref=p_ref, dst_ref=slots.at[peer], send_sem=send_sems.at[k - 1], recv_sem=recv_sems.at[k - 1],
                device_id=(x, y, c), device_id_type=MESH).wait_recv()
        for cp in copies:
            cp.wait_send()
        acc = slots[0]
        for k in range(1, N_DEV):
            acc = acc + slots[k]
        o_ref[...] = acc

    vm = pl.BlockSpec(memory_space=pltpu.VMEM)
    body, ins, in_specs = _ordered(body, [pack], [vm], after)
    return _pallas(
        body, name="all_reduce_small", in_specs=in_specs, out_specs=vm,
        out_shape=jax.ShapeDtypeStruct((r, d), F32),
        scratch_shapes=[pltpu.VMEM((N_DEV, r, d), F32), pltpu.SemaphoreType.DMA((N_DEV - 1,)),
                        pltpu.SemaphoreType.DMA((N_DEV - 1,))],
    )(*ins)


def _pack_rows(rows, d):
    out = []
    for a in rows:
        flat = a.reshape(-1)
        n = -(-flat.shape[0] // d) * d
        out.append(jnp.pad(flat, (0, n - flat.shape[0])).reshape(-1, d))
    packed = jnp.concatenate(out, axis=0)
    return jnp.pad(packed, ((0, 16 - packed.shape[0]), (0, 0)))


def _unpack_rows(packed, shapes, d):
    out, row = [], 0
    for shp in shapes:
        n = int(np.prod(shp))
        nrows = -(-n // d)
        out.append(packed[row:row + nrows].reshape(-1)[:n].reshape(shp))
        row += nrows
    return out


def kernel(x, pre_norm_ffn1, post_norm_ffn1, w_ffn1_gate_up, w_ffn1_down, pre_norm_mix, post_norm_mix, w_mix_in, hgrn_lower_bounds_fwd, hgrn_lower_bounds_bwd, hgrn_out_norm, attn_sink, w_mix_out, pre_norm_ffn2, post_norm_ffn2, w_ffn2_gate_up, w_ffn2_down, rel_bias_table, loss_target, m_pre_norm_ffn1, m_post_norm_ffn1, m_w_ffn1_gate_up, m_w_ffn1_down, m_pre_norm_mix, m_post_norm_mix, m_w_mix_in, m_hgrn_lower_bounds_fwd, m_hgrn_lower_bounds_bwd, m_hgrn_out_norm, m_attn_sink, m_w_mix_out, m_pre_norm_ffn2, m_post_norm_ffn2, m_w_ffn2_gate_up, m_w_ffn2_down, m_rel_bias_table, v_pre_norm_ffn1, v_post_norm_ffn1, v_w_ffn1_gate_up, v_w_ffn1_down, v_pre_norm_mix, v_post_norm_mix, v_w_mix_in, v_hgrn_lower_bounds_fwd, v_hgrn_lower_bounds_bwd, v_hgrn_out_norm, v_attn_sink, v_w_mix_out, v_pre_norm_ffn2, v_post_norm_ffn2, v_w_ffn2_gate_up, v_w_ffn2_down, v_rel_bias_table):
    t, d = x.shape[1], x.shape[2]
    hw = hgrn_out_norm.shape[1]
    aw = d - hw
    nah = aw // HEAD
    kvw = KV_HEADS * HEAD
    x0 = x[0]
    target = loss_target[0]

    big_names = ["w_ffn1_gate_up", "w_ffn1_down", "w_mix_in", "w_mix_out", "w_ffn2_gate_up", "w_ffn2_down"]
    big_w = [w_ffn1_gate_up[0], w_ffn1_down[0], w_mix_in[0], w_mix_out[0], w_ffn2_gate_up[0], w_ffn2_down[0]]
    big_m = [m_w_ffn1_gate_up[0], m_w_ffn1_down[0], m_w_mix_in[0], m_w_mix_out[0], m_w_ffn2_gate_up[0],
             m_w_ffn2_down[0]]
    big_v = [v_w_ffn1_gate_up[0], v_w_ffn1_down[0], v_w_mix_in[0], v_w_mix_out[0], v_w_ffn2_gate_up[0],
             v_w_ffn2_down[0]]
    col_sharded = [True, False, True, False, True, False]
    geoms = [_Big(w.shape, cs) for w, cs in zip(big_w, col_sharded)]

    cx, cy, cc = _mesh_pos()
    idx = jnp.stack([2 * cx + cy, cc]).astype(jnp.int32)
    c_idx = jnp.reshape(cc, (1,)).astype(jnp.int32)
    first = _cast_into_full(f"cast_{big_names[0]}", big_w[0], geoms[0], idx)
    (first,), direct_sems, tok = _gather_first_direct(first, geoms[0])
    rest = []
    for n, w, gm in zip(big_names[1:], big_w[1:], geoms[1:]):
        tok = _cast_into_full(f"cast_{n}", w, gm, idx, after=tok)
        rest.append(tok)
    (first,), relay_sems, tok = _gather_first_relay(first, geoms[0], direct_sems, after=tok)
    started_rest, sems_rest, rest_started = _gather_start("gather_start_rest", rest, geoms[1:], after=tok)
    started, gather_sems = [first] + started_rest, [relay_sems] + sems_rest

    def forward_weight(w, after):
        return _gather_forward(f"gather_forward_{big_names[w]}", started[w], geoms[w], gather_sems[w], after,
                               arrivals=1 if w == 0 else 3)

    def whole_weight(w, forwarded, after):
        return _gather_end(f"gather_end_{big_names[w]}", forwarded[0], geoms[w], forwarded[1], after)

    h1 = _norm_fwd("ffn1_pre_norm", x0, pre_norm_ffn1)
    w_gu1 = whole_weight(0, forward_weight(0, rest_started), h1)
    act1, dact_dgate1, dact_dup1 = _ffn_gate_up_act("ffn1_gate_up", h1, w_gu1)
    w_d1 = whole_weight(1, forward_weight(1, act1), act1)
    ff1 = _mm("ffn1_down", act1, w_d1, "nn", F32)
    fw = forward_weight(2, ff1)
    x1, hm = _resid_norm_fwd("ffn1_residual", x0, ff1, post_norm_ffn1, pre_norm_mix, 0.5)
    w_in = whole_weight(2, fw, hm)
    p = _mm("mix_in", hm, w_in, "nn", F32)
    fw = forward_weight(3, p)
    o_f, o_b, st_f, st_b = _hgrn_scan_fwd("hgrn_scan", p, hgrn_lower_bounds_fwd, hgrn_lower_bounds_bwd)
    y_h = _hgrn_out_fwd("hgrn_out", o_f, o_b, p, hgrn_out_norm, 4)
    kv_blk0 = (5 * hw + aw) // kvw
    bucket_ids = _t5_bucket_ids()
    bias = _bias_gather("attn_bias", rel_bias_table.T, bucket_ids).reshape(nah, WINDOW, SPAN)
    y_a, attn_probs, attn_sink_probs = _attn_fwd("attn_fwd", p, kv_blk0, kvw, bias, attn_sink, 5 * hw // aw)
    y_mix = _concat_cols("mix_concat", y_h, y_a)
    w_out = whole_weight(3, fw, y_mix)
    mixed = _mm("mix_out", y_mix, w_out, "nn", F32)
    fw = forward_weight(4, mixed)
    x2, h2 = _resid_norm_fwd("mix_residual", x1, mixed, post_norm_mix, pre_norm_ffn2, 1.0)
    w_gu2 = whole_weight(4, fw, h2)
    act2, dact_dgate2, dact_dup2 = _ffn_gate_up_act("ffn2_gate_up", h2, w_gu2)
    w_d2 = whole_weight(5, forward_weight(5, act2), act2)
    ff2 = _mm("ffn2_down", act2, w_d2, "nn", F32)
    loss_blk, dy, dff2, dg_post2 = _final_fwd_bwd("ffn2_residual_loss", x2, ff2, post_norm_ffn2, target, 0.5)

    reduce = [_GradReduce(n, gm, idx, c_idx) for n, gm in zip(big_names, geoms)]
    big_grads, big_delta, big_new_m, big_new_v = [None] * 6, [None] * 6, [None] * 6, [None] * 6

    def update(w, after, quarter=None):
        quarter = reduce[w].finish(after) if quarter is None else quarter
        g, dl, nm, nv = _adamw(f"adamw_{big_names[w]}", big_w[w], quarter, big_m[w], big_v[w], after=after)
        big_grads[w], big_delta[w], big_new_m[w], big_new_v[w] = g[None], dl[None], nm[None], nv[None]
        return dl

    def dw_start(w, x_act, dy_act, after=None):
        theirs = _dw_half(f"dw_theirs_{big_names[w]}", x_act, dy_act, geoms[w], c_idx, own=False, after=after)
        return reduce[w].pair_start(theirs)

    def dw_finish(w, x_act, dy_act, after):
        landed = reduce[w].pair_wait(after)
        half = _dw_half(f"dw_own_{big_names[w]}", x_act, dy_act, geoms[w], c_idx, own=True, addend=landed)
        return reduce[w].chip_start(half)

    tok = dw_start(5, act2, dff2)
    dgu2 = _ffn_dact("ffn2_dact", dff2, w_d2, dact_dgate2, dact_dup2, after=tok)
    tok = dw_finish(5, act2, dff2, after=dgu2)
    tok = dw_start(4, h2, dgu2, after=tok)
    dh2 = _ffn_dh("ffn2_dh", dgu2, w_gu2, after=tok)
    tok = dw_finish(4, h2, dgu2, after=dh2)
    dx2, dg_pre2, dmixed, dg_postm = _norms_bwd("mix_residual_bwd", dy, dh2, x2, pre_norm_ffn2,
                                                post=(mixed, post_norm_mix, 1.0), after=tok)
    tok = dw_start(3, y_mix, dmixed)
    dy_mix = _mm("mix_out_dx", dmixed, w_out, "nt", F32, after=tok)
    tok = dw_finish(3, y_mix, dmixed, after=dy_mix)
    dq_a, dk_pad, dv_pad, dbias, dsink = _attn_bwd("attn_bwd", p, kv_blk0, kvw, attn_probs, attn_sink_probs, dy_mix,
                                                   5 * hw // aw, hw // aw, after=tok)
    tok = reduce[5].chip_finish(dq_a)
    drel_t = _bias_scatter("attn_dbias", dbias.reshape(nah, WINDOW * SPAN), bucket_ids)
    do, dg_h, dgain = _hgrn_out_bwd("hgrn_out_bwd", dy_mix, o_f, o_b, p, hgrn_out_norm, 4, after=tok)
    dq_f, dv_f, dz_f, dlb_f, dq_b, dv_b, dz_b, dlb_b = _hgrn_scan_bwd(
        "hgrn_scan_bwd", p, hgrn_lower_bounds_fwd, hgrn_lower_bounds_bwd, do, st_f, st_b)
    tok = reduce[4].chip_finish(dq_f)
    tok = reduce[3].chip_finish(tok)
    dp = _mix_dproj("mix_dproj", [(dq_f, dq_b), (dv_f, dv_b), (dz_f,), (dz_b,), (dg_h,), (dq_a,)],
                    [dk_pad, dv_pad], t, after=tok)
    tok = dw_start(2, hm, dp)
    dhm = _mm("mix_in_dx", dp, w_in, "nt", F32, after=tok)
    tok = dw_finish(2, hm, dp, after=dhm)
    dx1, dg_prem, dff1, dg_post1 = _norms_bwd("ffn1_residual_bwd", dx2, dhm, x1, pre_norm_mix,
                                              post=(ff1, post_norm_ffn1, 0.5), after=tok)
    tok = dw_start(1, act1, dff1)
    dgu1 = _ffn_dact("ffn1_dact", dff1, w_d1, dact_dgate1, dact_dup1, after=tok)
    tok = dw_finish(1, act1, dff1, after=dgu1)
    tok = reduce[2].chip_finish(tok)
    tok = dw_start(0, h1, dgu1, after=tok)
    done = update(2, tok)
    tok = dw_finish(0, h1, dgu1, after=done)
    dh1 = _ffn_dh("ffn1_dh", dgu1, w_gu1, after=tok)
    grad_x, dg_pre1 = _norms_bwd("ffn1_pre_norm_bwd", dx1, dh1, x0, pre_norm_ffn1)

    small_w = [pre_norm_ffn1, post_norm_ffn1, pre_norm_mix, post_norm_mix, hgrn_lower_bounds_fwd,
               hgrn_lower_bounds_bwd, hgrn_out_norm, attn_sink, pre_norm_ffn2, post_norm_ffn2, rel_bias_table]
    small_m = [m_pre_norm_ffn1, m_post_norm_ffn1, m_pre_norm_mix, m_post_norm_mix, m_hgrn_lower_bounds_fwd,
               m_hgrn_lower_bounds_bwd, m_hgrn_out_norm, m_attn_sink, m_pre_norm_ffn2, m_post_norm_ffn2,
               m_rel_bias_table]
    small_v = [v_pre_norm_ffn1, v_post_norm_ffn1, v_pre_norm_mix, v_post_norm_mix, v_hgrn_lower_bounds_fwd,
               v_hgrn_lower_bounds_bwd, v_hgrn_out_norm, v_attn_sink, v_pre_norm_ffn2, v_post_norm_ffn2,
               v_rel_bias_table]
    small_g = [dg_pre1, dg_post1, dg_prem, dg_postm, dlb_f, dlb_b, dgain, dsink[:, 0].reshape(1, nah), dg_pre2,
               dg_post2, drel_t.T]
    shapes = [a.shape for a in small_w]
    early = _finish_together("share_wait_early", [reduce[5], reduce[4], reduce[3]], grad_x)
    done = grad_x
    for w, quarter in zip((5, 4, 3), early):
        done = update(w, done, quarter)
    summed = _all_reduce_small(_pack_rows(small_g + [loss_blk[0:1, 0:1]], d), after=done)
    loss = _unpack_rows(summed, shapes + [(1, 1)], d)[-1][0, 0]
    _, sd, sm, sv = _adamw("adamw_small", _pack_rows(small_w, d), summed, _pack_rows(small_m, d),
                           _pack_rows(small_v, d))
    small_grads = _unpack_rows(summed, shapes, d)
    small_delta, small_new_m, small_new_v = (_unpack_rows(a, shapes, d) for a in (sd, sm, sv))

    tok = reduce[1].chip_finish(sd)
    tok = reduce[0].chip_finish(tok)
    done = update(1, tok)
    update(0, done)

    def ordered(small, big):
        s = dict(zip(["pre1", "post1", "prem", "postm", "lbf", "lbb", "gain", "sink", "pre2", "post2", "rel"], small))
        b = dict(zip(["gu1", "d1", "win", "wout", "gu2", "d2"], big))
        return [s["pre1"], s["post1"], b["gu1"], b["d1"], s["prem"], s["postm"], b["win"], s["lbf"], s["lbb"],
                s["gain"], s["sink"], b["wout"], s["pre2"], s["post2"], b["gu2"], b["d2"], s["rel"]]

    return (loss, grad_x[None], *ordered(small_grads, big_grads), *ordered(small_delta, big_delta),
            *ordered(small_new_m, big_new_m), *ordered(small_new_v, big_new_v))
```
